```python
import jax, jax.numpy as jnp
from jax import lax
import numpy as np

D_MODEL = 1024
BATCH = 8
SEQ = 4096
DEPTH = 1

CHUNK = 64
GLA_HEADS = 4
GLA_DK = D_MODEL // 2 // GLA_HEADS
GLA_DV = D_MODEL // GLA_HEADS
GLA_QK = GLA_HEADS * GLA_DK
GLA_V = GLA_HEADS * GLA_DV
GLA_GATE_RANK = 16
GLA_TAU = 16.0
SGU_GROUPS = 4
SGU_BLOCK = 128
SGU_WIDTH = D_MODEL
SGU_DG = SGU_WIDTH // SGU_GROUPS
D_FF = -(-8 * D_MODEL // (3 * 256)) * 256
EPS = 1e-6

IN_SPLITS = (GLA_QK, GLA_QK, GLA_V, GLA_V, GLA_GATE_RANK,
             SGU_WIDTH, SGU_WIDTH, D_MODEL, D_MODEL)
D_IN = int(sum(IN_SPLITS))
IN_OFFSETS = tuple(int(o) for o in np.cumsum(IN_SPLITS)[:-1])

kernel_name = "hybrid_gla_sgu_swiglu_sandwich"


def rmsnorm(x, g):
    xf = x.astype(jnp.float32)
    y = xf * lax.rsqrt(jnp.mean(xf * xf, axis=-1, keepdims=True) + EPS)
    return (y * g.astype(jnp.float32)).astype(x.dtype)


def gla_branch(q, k, v, r, a_low, w_gate_up, b_gate, gla_norm):
    B, S, _ = q.shape
    N = S // CHUNK
    f32 = jnp.float32

    def heads(t, d):
        return t.reshape(B, N, CHUNK, GLA_HEADS, d)

    qh = heads(q, GLA_DK).astype(f32) * (GLA_DK ** -0.5)
    kh = heads(k, GLA_DK).astype(f32)
    vh = heads(v, GLA_DV).astype(f32)
    logit = jnp.einsum('bsr,rk->bsk', a_low, w_gate_up) + b_gate
    log_a = heads(jax.nn.log_sigmoid(logit.astype(f32)) / GLA_TAU, GLA_DK)
    cum = jnp.cumsum(log_a, axis=2)
    tot = cum[:, :, -1]
    k_dec = kh * jnp.exp(tot[:, :, None] - cum)
    upd = jnp.einsum('bnchk,bnchv->bnhkv', k_dec, vh)
    decay = jnp.exp(tot)

    def step(state, inp):
        a_c, u_c = inp
        state = a_c[..., None] * state + u_c
        return state, state

    s0 = jnp.zeros((B, GLA_HEADS, GLA_DK, GLA_DV), f32)
    _, states = lax.scan(step, s0, (jnp.moveaxis(decay, 1, 0), jnp.moveaxis(upd, 1, 0)))
    states = jnp.moveaxis(states, 0, 1)
    o = jnp.einsum('bnchk,bnhkv->bnchv', qh, states)
    o = o * lax.rsqrt(jnp.mean(o * o, axis=-1, keepdims=True) + EPS) * gla_norm.astype(f32)
    o = o.reshape(B, S, GLA_V).astype(q.dtype)
    return o * jax.nn.silu(r)


def sgu_branch(u, v, ln_g, ln_b, w_spatial, b_spatial):
    B, S, _ = u.shape
    u = jax.nn.gelu(u)
    vf = jax.nn.gelu(v).reshape(B, S, SGU_GROUPS, SGU_DG).astype(jnp.float32)
    mu = jnp.mean(vf, axis=-1, keepdims=True)
    var = jnp.mean(jnp.square(vf - mu), axis=-1, keepdims=True)
    vn = (vf - mu) * lax.rsqrt(var + EPS) * ln_g.astype(jnp.float32) + ln_b.astype(jnp.float32)
    vn = vn.astype(u.dtype).reshape(B, S // SGU_BLOCK, SGU_BLOCK, SGU_GROUPS, SGU_DG)
    pos = np.arange(SGU_BLOCK)
    mask = (pos[None, :] // CHUNK) <= (pos[:, None] // CHUNK)
    w = jnp.where(mask[None], w_spatial, 0.0)
    mixed = jnp.einsum('gij,bnjgc->bnigc', w, vn) + b_spatial.T[:, :, None]
    return u * mixed.reshape(B, S, SGU_WIDTH)


def _fwd_setup_inputs(seed: int = 0) -> dict:
    key = jax.random.key(seed)
    ks = jax.random.split(key, 20)
    L = DEPTH
    nrm = jax.random.normal

    def gain(k, shape):
        return 1.0 + 0.05 * nrm(k, shape, jnp.float32)

    return {
        "x": nrm(ks[0], (BATCH, SEQ, D_MODEL), jnp.float32),
        "norm_pre_mix": gain(ks[1], (L, D_MODEL)),
        "w_in": nrm(ks[2], (L, D_MODEL, D_IN), jnp.float32) * D_MODEL ** -0.5,
        "w_gate_up": nrm(ks[3], (L, GLA_GATE_RANK, GLA_QK), jnp.float32) * GLA_GATE_RANK ** -0.5,
        "b_gate": 0.01 * nrm(ks[4], (L, GLA_QK), jnp.float32),
        "gla_norm": gain(ks[5], (L, GLA_HEADS, GLA_DV)),
        "sgu_ln_g": gain(ks[6], (L, SGU_GROUPS, SGU_DG)),
        "sgu_ln_b": 0.02 * nrm(ks[7], (L, SGU_GROUPS, SGU_DG), jnp.float32),
        "w_spatial": nrm(ks[8], (L, SGU_GROUPS, SGU_BLOCK, SGU_BLOCK), jnp.float32) * SGU_BLOCK ** -0.5,
        "b_spatial": 1.0 + 0.02 * nrm(ks[9], (L, SGU_GROUPS, SGU_BLOCK), jnp.float32),
        "w_branch_gla": nrm(ks[10], (L, GLA_V, D_MODEL), jnp.float32) * GLA_V ** -0.5,
        "w_branch_sgu": nrm(ks[11], (L, SGU_WIDTH, D_MODEL), jnp.float32) * SGU_WIDTH ** -0.5,
        "w_out": nrm(ks[12], (L, D_MODEL, D_MODEL), jnp.float32) * D_MODEL ** -0.5,
        "norm_post_mix": gain(ks[13], (L, D_MODEL)),
        "norm_pre_ffn": gain(ks[14], (L, D_MODEL)),
        "w_ffn_in": nrm(ks[15], (L, D_MODEL, 2 * D_FF), jnp.float32) * D_MODEL ** -0.5,
        "w_ffn_out": nrm(ks[16], (L, D_FF, D_MODEL), jnp.float32) * D_FF ** -0.5,
        "norm_post_ffn": gain(ks[17], (L, D_MODEL)),
    }


def _fwd_reference(x, norm_pre_mix, w_in, w_gate_up, b_gate, gla_norm, sgu_ln_g, sgu_ln_b,
              w_spatial, b_spatial, w_branch_gla, w_branch_sgu, w_out, norm_post_mix,
              norm_pre_ffn, w_ffn_in, w_ffn_out, norm_post_ffn):
    for l in range(DEPTH):
        a = rmsnorm(x, norm_pre_mix[l])
        proj = jnp.einsum('bsd,de->bse', a, w_in[l])
        q, k, v, r, a_low, su, sv, g_gla, g_sgu = jnp.split(proj, IN_OFFSETS, axis=-1)
        y_gla = gla_branch(q, k, v, r, a_low, w_gate_up[l], b_gate[l], gla_norm[l])
        y_sgu = sgu_branch(su, sv, sgu_ln_g[l], sgu_ln_b[l], w_spatial[l], b_spatial[l])
        merged = (jax.nn.sigmoid(g_gla) * jnp.einsum('bsv,vd->bsd', y_gla, w_branch_gla[l])
                  + jax.nn.sigmoid(g_sgu) * jnp.einsum('bsv,vd->bsd', y_sgu, w_branch_sgu[l]))
        mix = jnp.einsum('bsd,de->bse', merged, w_out[l])
        x = x + rmsnorm(mix, norm_post_mix[l])
        h = rmsnorm(x, norm_pre_ffn[l])
        gate, up = jnp.split(jnp.einsum('bsd,df->bsf', h, w_ffn_in[l]), 2, axis=-1)
        y = jnp.einsum('bsf,fd->bsd', jax.nn.silu(gate) * up, w_ffn_out[l])
        x = x + rmsnorm(y, norm_post_ffn[l])
    return x


import jax as _jax
import jax.numpy as _jnp

TWIN_FORMAT = 'train_step'
FWD_PARAMS = ['x', 'norm_pre_mix', 'w_in', 'w_gate_up', 'b_gate', 'gla_norm', 'sgu_ln_g', 'sgu_ln_b', 'w_spatial', 'b_spatial', 'w_branch_gla', 'w_branch_sgu', 'w_out', 'norm_post_mix', 'norm_pre_ffn', 'w_ffn_in', 'w_ffn_out', 'norm_post_ffn']
TWIN_WEIGHTS = ['norm_pre_mix', 'w_in', 'w_gate_up', 'b_gate', 'gla_norm', 'sgu_ln_g', 'sgu_ln_b', 'w_spatial', 'b_spatial', 'w_branch_gla', 'w_branch_sgu', 'w_out', 'norm_post_mix', 'norm_pre_ffn', 'w_ffn_in', 'w_ffn_out', 'norm_post_ffn']
TWIN_DIFF_INPUT = 'x'
TWIN_INPUTS = ['x', 'norm_pre_mix', 'w_in', 'w_gate_up', 'b_gate', 'gla_norm', 'sgu_ln_g', 'sgu_ln_b', 'w_spatial', 'b_spatial', 'w_branch_gla', 'w_branch_sgu', 'w_out', 'norm_post_mix', 'norm_pre_ffn', 'w_ffn_in', 'w_ffn_out', 'norm_post_ffn', 'loss_target', 'm_norm_pre_mix', 'm_w_in', 'm_w_gate_up', 'm_b_gate', 'm_gla_norm', 'm_sgu_ln_g', 'm_sgu_ln_b', 'm_w_spatial', 'm_b_spatial', 'm_w_branch_gla', 'm_w_branch_sgu', 'm_w_out', 'm_norm_post_mix', 'm_norm_pre_ffn', 'm_w_ffn_in', 'm_w_ffn_out', 'm_norm_post_ffn', 'v_norm_pre_mix', 'v_w_in', 'v_w_gate_up', 'v_b_gate', 'v_gla_norm', 'v_sgu_ln_g', 'v_sgu_ln_b', 'v_w_spatial', 'v_b_spatial', 'v_w_branch_gla', 'v_w_branch_sgu', 'v_w_out', 'v_norm_post_mix', 'v_norm_pre_ffn', 'v_w_ffn_in', 'v_w_ffn_out', 'v_norm_post_ffn']
TWIN_OUTPUTS = ['loss', 'grad_x', 'grad_norm_pre_mix', 'grad_w_in', 'grad_w_gate_up', 'grad_b_gate', 'grad_gla_norm', 'grad_sgu_ln_g', 'grad_sgu_ln_b', 'grad_w_spatial', 'grad_b_spatial', 'grad_w_branch_gla', 'grad_w_branch_sgu', 'grad_w_out', 'grad_norm_post_mix', 'grad_norm_pre_ffn', 'grad_w_ffn_in', 'grad_w_ffn_out', 'grad_norm_post_ffn', 'delta_norm_pre_mix', 'delta_w_in', 'delta_w_gate_up', 'delta_b_gate', 'delta_gla_norm', 'delta_sgu_ln_g', 'delta_sgu_ln_b', 'delta_w_spatial', 'delta_b_spatial', 'delta_w_branch_gla', 'delta_w_branch_sgu', 'delta_w_out', 'delta_norm_post_mix', 'delta_norm_pre_ffn', 'delta_w_ffn_in', 'delta_w_ffn_out', 'delta_norm_post_ffn', 'new_m_norm_pre_mix', 'new_m_w_in', 'new_m_w_gate_up', 'new_m_b_gate', 'new_m_gla_norm', 'new_m_sgu_ln_g', 'new_m_sgu_ln_b', 'new_m_w_spatial', 'new_m_b_spatial', 'new_m_w_branch_gla', 'new_m_w_branch_sgu', 'new_m_w_out', 'new_m_norm_post_mix', 'new_m_norm_pre_ffn', 'new_m_w_ffn_in', 'new_m_w_ffn_out', 'new_m_norm_post_ffn', 'new_v_norm_pre_mix', 'new_v_w_in', 'new_v_w_gate_up', 'new_v_b_gate', 'new_v_gla_norm', 'new_v_sgu_ln_g', 'new_v_sgu_ln_b', 'new_v_w_spatial', 'new_v_b_spatial', 'new_v_w_branch_gla', 'new_v_w_branch_sgu', 'new_v_w_out', 'new_v_norm_post_mix', 'new_v_norm_pre_ffn', 'new_v_w_ffn_in', 'new_v_w_ffn_out', 'new_v_norm_post_ffn']
TWIN_LEAF_KINDS = {'loss': 'loss', 'grad_x': 'grad_x', 'grad_norm_pre_mix': 'grad_w', 'grad_w_in': 'grad_w', 'grad_w_gate_up': 'grad_w', 'grad_b_gate': 'grad_w', 'grad_gla_norm': 'grad_w', 'grad_sgu_ln_g': 'grad_w', 'grad_sgu_ln_b': 'grad_w', 'grad_w_spatial': 'grad_w', 'grad_b_spatial': 'grad_w', 'grad_w_branch_gla': 'grad_w', 'grad_w_branch_sgu': 'grad_w', 'grad_w_out': 'grad_w', 'grad_norm_post_mix': 'grad_w', 'grad_norm_pre_ffn': 'grad_w', 'grad_w_ffn_in': 'grad_w', 'grad_w_ffn_out': 'grad_w', 'grad_norm_post_ffn': 'grad_w', 'delta_norm_pre_mix': 'delta_w', 'delta_w_in': 'delta_w', 'delta_w_gate_up': 'delta_w', 'delta_b_gate': 'delta_w', 'delta_gla_norm': 'delta_w', 'delta_sgu_ln_g': 'delta_w', 'delta_sgu_ln_b': 'delta_w', 'delta_w_spatial': 'delta_w', 'delta_b_spatial': 'delta_w', 'delta_w_branch_gla': 'delta_w', 'delta_w_branch_sgu': 'delta_w', 'delta_w_out': 'delta_w', 'delta_norm_post_mix': 'delta_w', 'delta_norm_pre_ffn': 'delta_w', 'delta_w_ffn_in': 'delta_w', 'delta_w_ffn_out': 'delta_w', 'delta_norm_post_ffn': 'delta_w', 'new_m_norm_pre_mix': 'new_m', 'new_m_w_in': 'new_m', 'new_m_w_gate_up': 'new_m', 'new_m_b_gate': 'new_m', 'new_m_gla_norm': 'new_m', 'new_m_sgu_ln_g': 'new_m', 'new_m_sgu_ln_b': 'new_m', 'new_m_w_spatial': 'new_m', 'new_m_b_spatial': 'new_m', 'new_m_w_branch_gla': 'new_m', 'new_m_w_branch_sgu': 'new_m', 'new_m_w_out': 'new_m', 'new_m_norm_post_mix': 'new_m', 'new_m_norm_pre_ffn': 'new_m', 'new_m_w_ffn_in': 'new_m', 'new_m_w_ffn_out': 'new_m', 'new_m_norm_post_ffn': 'new_m', 'new_v_norm_pre_mix': 'new_v', 'new_v_w_in': 'new_v', 'new_v_w_gate_up': 'new_v', 'new_v_b_gate': 'new_v', 'new_v_gla_norm': 'new_v', 'new_v_sgu_ln_g': 'new_v', 'new_v_sgu_ln_b': 'new_v', 'new_v_w_spatial': 'new_v', 'new_v_b_spatial': 'new_v', 'new_v_w_branch_gla': 'new_v', 'new_v_w_branch_sgu': 'new_v', 'new_v_w_out': 'new_v', 'new_v_norm_post_mix': 'new_v', 'new_v_norm_pre_ffn': 'new_v', 'new_v_w_ffn_in': 'new_v', 'new_v_w_ffn_out': 'new_v', 'new_v_norm_post_ffn': 'new_v'}


def _forward(args):
    return _fwd_reference(*[args[k] for k in FWD_PARAMS])


def _output_shape():
    out = _jax.eval_shape(lambda: _forward(_fwd_setup_inputs(0)))
    return out.shape, out.dtype

N_MICROBATCH = 1
ADAM_LR = 0.001
ADAM_B1 = 0.9
ADAM_B2 = 0.999
ADAM_EPS = 1e-08
ADAM_WD = 0.01
ADAM_STEP = 10
PER_EXAMPLE_BATCH_AXIS = {'x': 0, 'loss_target': 0}
SHARED_INPUTS = []
_WEIGHT_DTYPES = {'norm_pre_mix': _jnp.float32, 'w_in': _jnp.float32, 'w_gate_up': _jnp.float32, 'b_gate': _jnp.float32, 'gla_norm': _jnp.float32, 'sgu_ln_g': _jnp.float32, 'sgu_ln_b': _jnp.float32, 'w_spatial': _jnp.float32, 'b_spatial': _jnp.float32, 'w_branch_gla': _jnp.float32, 'w_branch_sgu': _jnp.float32, 'w_out': _jnp.float32, 'norm_post_mix': _jnp.float32, 'norm_pre_ffn': _jnp.float32, 'w_ffn_in': _jnp.float32, 'w_ffn_out': _jnp.float32, 'norm_post_ffn': _jnp.float32}
MOMENT_SCALE = {'norm_pre_mix': 7.388240e-01, 'w_in': 2.856043e-01, 'w_gate_up': 4.418158e-02, 'b_gate': 1.565405e-01, 'gla_norm': 2.605351e-01, 'sgu_ln_g': 2.431475e-01, 'sgu_ln_b': 2.394231e-01, 'w_spatial': 3.304164e-01, 'b_spatial': 3.670520e-01, 'w_branch_gla': 2.605432e-01, 'w_branch_sgu': 7.818281e-01, 'w_out': 8.957911e-01, 'norm_post_mix': 3.231572e+01, 'norm_pre_ffn': 8.130397e-01, 'w_ffn_in': 3.527537e-01, 'w_ffn_out': 7.124003e-01, 'norm_post_ffn': 3.199571e+01}


def _to_microbatches(a, axis):
    t = _jnp.moveaxis(a, axis, 0)
    t = t.reshape((N_MICROBATCH, t.shape[0] // N_MICROBATCH) + t.shape[1:])
    return _jnp.moveaxis(t, 1, axis + 1)


def setup_inputs(seed: int = 0) -> dict:
    inp = _fwd_setup_inputs(seed)
    key = _jax.random.fold_in(_jax.random.key(seed), 7919)
    shape, _ = _output_shape()
    out = dict(inp)
    out["loss_target"] = _jax.random.normal(_jax.random.fold_in(key, 0), shape, _jnp.float32)
    for i, name in enumerate(TWIN_WEIGHTS):
        w = inp[name].astype(_jnp.float32)
        if MOMENT_SCALE is None:
            s = _jnp.sqrt(_jnp.mean(_jnp.square(w)) + 1e-30)
        else:
            s = MOMENT_SCALE[name]
        km, kv = _jax.random.split(_jax.random.fold_in(key, i + 1))
        out[name] = w
        out["m_" + name] = s * _jax.random.normal(km, w.shape, _jnp.float32)
        out["v_" + name] = (s * s) * _jax.random.uniform(kv, w.shape, _jnp.float32, 0.5, 1.5)
    if N_MICROBATCH > 1:
        for name, axis in PER_EXAMPLE_BATCH_AXIS.items():
            out[name] = _to_microbatches(out[name], axis)
    return {'x': out['x'], 'norm_pre_mix': out['norm_pre_mix'], 'w_in': out['w_in'], 'w_gate_up': out['w_gate_up'], 'b_gate': out['b_gate'], 'gla_norm': out['gla_norm'], 'sgu_ln_g': out['sgu_ln_g'], 'sgu_ln_b': out['sgu_ln_b'], 'w_spatial': out['w_spatial'], 'b_spatial': out['b_spatial'], 'w_branch_gla': out['w_branch_gla'], 'w_branch_sgu': out['w_branch_sgu'], 'w_out': out['w_out'], 'norm_post_mix': out['norm_post_mix'], 'norm_pre_ffn': out['norm_pre_ffn'], 'w_ffn_in': out['w_ffn_in'], 'w_ffn_out': out['w_ffn_out'], 'norm_post_ffn': out['norm_post_ffn'], 'loss_target': out['loss_target'], 'm_norm_pre_mix': out['m_norm_pre_mix'], 'm_w_in': out['m_w_in'], 'm_w_gate_up': out['m_w_gate_up'], 'm_b_gate': out['m_b_gate'], 'm_gla_norm': out['m_gla_norm'], 'm_sgu_ln_g': out['m_sgu_ln_g'], 'm_sgu_ln_b': out['m_sgu_ln_b'], 'm_w_spatial': out['m_w_spatial'], 'm_b_spatial': out['m_b_spatial'], 'm_w_branch_gla': out['m_w_branch_gla'], 'm_w_branch_sgu': out['m_w_branch_sgu'], 'm_w_out': out['m_w_out'], 'm_norm_post_mix': out['m_norm_post_mix'], 'm_norm_pre_ffn': out['m_norm_pre_ffn'], 'm_w_ffn_in': out['m_w_ffn_in'], 'm_w_ffn_out': out['m_w_ffn_out'], 'm_norm_post_ffn': out['m_norm_post_ffn'], 'v_norm_pre_mix': out['v_norm_pre_mix'], 'v_w_in': out['v_w_in'], 'v_w_gate_up': out['v_w_gate_up'], 'v_b_gate': out['v_b_gate'], 'v_gla_norm': out['v_gla_norm'], 'v_sgu_ln_g': out['v_sgu_ln_g'], 'v_sgu_ln_b': out['v_sgu_ln_b'], 'v_w_spatial': out['v_w_spatial'], 'v_b_spatial': out['v_b_spatial'], 'v_w_branch_gla': out['v_w_branch_gla'], 'v_w_branch_sgu': out['v_w_branch_sgu'], 'v_w_out': out['v_w_out'], 'v_norm_post_mix': out['v_norm_post_mix'], 'v_norm_pre_ffn': out['v_norm_pre_ffn'], 'v_w_ffn_in': out['v_w_ffn_in'], 'v_w_ffn_out': out['v_w_ffn_out'], 'v_norm_post_ffn': out['v_norm_post_ffn']}


def _loss(weights, diff, rest, loss_target):
    with _jax.named_scope("forward"):
        args = {**rest, TWIN_DIFF_INPUT: diff, **{k: w.astype(_WEIGHT_DTYPES[k]) for k, w in weights.items()}}
        y = _forward(args)
    with _jax.named_scope("loss_head"):
        err = _jnp.square(y.astype(_jnp.float32) - loss_target)
        return 0.5 * _jnp.sum(_jnp.mean(err, axis=-1)) if err.ndim else 0.5 * err


def _adamw(w, g, m, v):
    m = ADAM_B1 * m + (1.0 - ADAM_B1) * g
    v = ADAM_B2 * v + (1.0 - ADAM_B2) * _jnp.square(g)
    m_hat = m / (1.0 - ADAM_B1 ** ADAM_STEP)
    v_hat = v / (1.0 - ADAM_B2 ** ADAM_STEP)
    delta = -ADAM_LR * (m_hat / (_jnp.sqrt(v_hat) + ADAM_EPS) + ADAM_WD * w)
    return delta, m, v


def reference(x, norm_pre_mix, w_in, w_gate_up, b_gate, gla_norm, sgu_ln_g, sgu_ln_b, w_spatial, b_spatial, w_branch_gla, w_branch_sgu, w_out, norm_post_mix, norm_pre_ffn, w_ffn_in, w_ffn_out, norm_post_ffn, loss_target, m_norm_pre_mix, m_w_in, m_w_gate_up, m_b_gate, m_gla_norm, m_sgu_ln_g, m_sgu_ln_b, m_w_spatial, m_b_spatial, m_w_branch_gla, m_w_branch_sgu, m_w_out, m_norm_post_mix, m_norm_pre_ffn, m_w_ffn_in, m_w_ffn_out, m_norm_post_ffn, v_norm_pre_mix, v_w_in, v_w_gate_up, v_b_gate, v_gla_norm, v_sgu_ln_g, v_sgu_ln_b, v_w_spatial, v_b_spatial, v_w_branch_gla, v_w_branch_sgu, v_w_out, v_norm_post_mix, v_norm_pre_ffn, v_w_ffn_in, v_w_ffn_out, v_norm_post_ffn):
    given = dict(x=x, norm_pre_mix=norm_pre_mix, w_in=w_in, w_gate_up=w_gate_up, b_gate=b_gate, gla_norm=gla_norm, sgu_ln_g=sgu_ln_g, sgu_ln_b=sgu_ln_b, w_spatial=w_spatial, b_spatial=b_spatial, w_branch_gla=w_branch_gla, w_branch_sgu=w_branch_sgu, w_out=w_out, norm_post_mix=norm_post_mix, norm_pre_ffn=norm_pre_ffn, w_ffn_in=w_ffn_in, w_ffn_out=w_ffn_out, norm_post_ffn=norm_post_ffn, loss_target=loss_target, m_norm_pre_mix=m_norm_pre_mix, m_w_in=m_w_in, m_w_gate_up=m_w_gate_up, m_b_gate=m_b_gate, m_gla_norm=m_gla_norm, m_sgu_ln_g=m_sgu_ln_g, m_sgu_ln_b=m_sgu_ln_b, m_w_spatial=m_w_spatial, m_b_spatial=m_b_spatial, m_w_branch_gla=m_w_branch_gla, m_w_branch_sgu=m_w_branch_sgu, m_w_out=m_w_out, m_norm_post_mix=m_norm_post_mix, m_norm_pre_ffn=m_norm_pre_ffn, m_w_ffn_in=m_w_ffn_in, m_w_ffn_out=m_w_ffn_out, m_norm_post_ffn=m_norm_post_ffn, v_norm_pre_mix=v_norm_pre_mix, v_w_in=v_w_in, v_w_gate_up=v_w_gate_up, v_b_gate=v_b_gate, v_gla_norm=v_gla_norm, v_sgu_ln_g=v_sgu_ln_g, v_sgu_ln_b=v_sgu_ln_b, v_w_spatial=v_w_spatial, v_b_spatial=v_b_spatial, v_w_branch_gla=v_w_branch_gla, v_w_branch_sgu=v_w_branch_sgu, v_w_out=v_w_out, v_norm_post_mix=v_norm_post_mix, v_norm_pre_ffn=v_norm_pre_ffn, v_w_ffn_in=v_w_ffn_in, v_w_ffn_out=v_w_ffn_out, v_norm_post_ffn=v_norm_post_ffn)
    weights = {n: given[n] for n in TWIN_WEIGHTS}
    shared = {n: given[n] for n in SHARED_INPUTS}
    per_example = {n: given[n] for n in ['x']}
    grad_fn = _jax.value_and_grad(_loss, argnums=(0, 1))

    def one_microbatch(ex, loss_target):
        ex = dict(ex)
        diff = ex.pop(TWIN_DIFF_INPUT)
        return grad_fn(weights, diff, {**shared, **ex}, loss_target)

    if N_MICROBATCH == 1:
        loss, (grad_w, grad_x) = one_microbatch(per_example, given["loss_target"])
    else:
        def body(carry, xs):
            loss_sum, grad_sum = carry
            l_k, (gw_k, gx_k) = one_microbatch(xs[0], xs[1])
            with _jax.named_scope("update"):
                return (loss_sum + l_k, _jax.tree.map(_jnp.add, grad_sum, gw_k)), gx_k

        init = (_jnp.zeros((), _jnp.float32), _jax.tree.map(_jnp.zeros_like, weights))
        (loss, grad_w), grad_x = _jax.lax.scan(body, init, (per_example, given["loss_target"]))
    with _jax.named_scope("update"):
        delta_w, new_m, new_v = {}, {}, {}
        for n in TWIN_WEIGHTS:
            delta_w[n], new_m[n], new_v[n] = _adamw(weights[n], grad_w[n], given["m_" + n], given["v_" + n])
    return (loss, grad_x, *[grad_w[n] for n in TWIN_WEIGHTS], *[delta_w[n] for n in TWIN_WEIGHTS],
            *[new_m[n] for n in TWIN_WEIGHTS], *[new_v[n] for n in TWIN_WEIGHTS])
```

```python
import jax
import jax.numpy as jnp
from jax import lax
from jax.experimental import pallas as pl
from jax.experimental.pallas import tpu as pltpu

F32 = jnp.float32
BF16 = jnp.bfloat16

D_MODEL = 1024
N_DEV = 8
CHUNK = 64
HEADS = 4
DK = 128
DV = 256
QK = HEADS * DK
GV = HEADS * DV
RANK = 16
GROUPS = 4
SBLOCK = 128
DG = 256
D_FF = 2816
FF_BLK = 704
EPS = 1e-6
Q_SCALE = DK ** -0.5
LANES = 128
VMEM_BIG = 48 * 1024 * 1024

D_IN = 7184
IN_BLK = 898
A_COLS = (0, 3072)
LOW_COLS = (3072, 3088)
S_COLS = (3088, 5136)
G_COLS = (5136, 7184)

ADAM_LR = 0.001
ADAM_B1 = 0.9
ADAM_B2 = 0.999
ADAM_EPS = 1e-08
ADAM_WD = 0.01
ADAM_STEP = 10

MESH = pl.DeviceIdType.MESH
ANY = pl.BlockSpec(memory_space=pl.ANY)
VMEM_SPEC = pl.BlockSpec(memory_space=pltpu.VMEM)


def _cp(sem=None, vmem=None):
    return pltpu.CompilerParams(dimension_semantics=sem, vmem_limit_bytes=vmem)


def _sig(x):
    return 1.0 / (1.0 + jnp.exp(-x))


def _gelu(x):
    c = 0.7978845608028654
    t = jnp.tanh(c * (x + 0.044715 * x * x * x))
    return 0.5 * x * (1.0 + t)


def _gelu_grad(x):
    c = 0.7978845608028654
    x2 = x * x
    t = jnp.tanh(c * (x + 0.044715 * x * x2))
    return 0.5 * (1.0 + t) + 0.5 * x * (1.0 - t * t) * c * (1.0 + 3.0 * 0.044715 * x2)


def _logsig(x):
    return jnp.minimum(x, 0.0) - jnp.log1p(jnp.exp(-jnp.abs(x)))


def _dot(a, b, dims):
    return lax.dot_general(a, b, (dims, ((), ())), preferred_element_type=F32)


NN = ((1,), (0,))
NT = ((1,), (1,))
TN = ((0,), (0,))


def _exact_mask_dot(mask_bf16, x):
    hi = x.astype(BF16)
    r1 = x - hi.astype(F32)
    mid = r1.astype(BF16)
    lo = (r1 - mid.astype(F32)).astype(BF16)
    return _dot(mask_bf16, hi, NN) + _dot(mask_bf16, mid, NN) + _dot(mask_bf16, lo, NN)


def _pick_tile(dim, target):
    if dim <= target:
        return dim
    best = None
    for t in range(LANES, int(1.4 * target) + 1, LANES):
        if dim % t == 0:
            best = t
    assert best is not None, (dim, target)
    return best


def _mm_call(a, b, *, name, grid, a_spec, b_spec, o_spec, out_shape, dims, acc_shape):
    nk = grid[2]

    def body(a_ref, b_ref, o_ref, *acc):
        part = _dot(a_ref[...], b_ref[...], dims)
        if nk == 1:
            o_ref[...] = part.astype(o_ref.dtype)
        else:
            acc_ref = acc[0]
            k = pl.program_id(2)

            @pl.when(k == 0)
            def _():
                acc_ref[...] = part

            @pl.when(k > 0)
            def _():
                acc_ref[...] += part

            @pl.when(k == nk - 1)
            def _():
                o_ref[...] = acc_ref[...].astype(o_ref.dtype)

    return pl.pallas_call(
        body, name=name, grid=grid, in_specs=[a_spec, b_spec], out_specs=o_spec, out_shape=out_shape,
        scratch_shapes=[] if nk == 1 else [pltpu.VMEM(acc_shape, F32)],
        compiler_params=_cp(("parallel", "parallel", "arbitrary"), VMEM_BIG),
    )(a, b)


def _mm(a, b, mode, out_dtype, *, name, tm=512, tn=1024, tk=1024):
    if mode == "nn":
        (M, K), (_, N) = a.shape, b.shape
    elif mode == "nt":
        (M, K), (N, _) = a.shape, b.shape
    else:
        (K, M), (_, N) = a.shape, b.shape
    tm, tn, tk = _pick_tile(M, tm), _pick_tile(N, tn), _pick_tile(K, tk)
    if mode == "nn":
        a_spec = pl.BlockSpec((tm, tk), lambda i, j, k: (i, k))
        b_spec = pl.BlockSpec((tk, tn), lambda i, j, k: (k, j))
        dims = NN
    elif mode == "nt":
        a_spec = pl.BlockSpec((tm, tk), lambda i, j, k: (i, k))
        b_spec = pl.BlockSpec((tn, tk), lambda i, j, k: (j, k))
        dims = NT
    else:
        a_spec = pl.BlockSpec((tk, tm), lambda i, j, k: (k, i))
        b_spec = pl.BlockSpec((tk, tn), lambda i, j, k: (k, j))
        dims = TN
    return _mm_call(a, b, name=name, grid=(M // tm, N // tn, K // tk), a_spec=a_spec, b_spec=b_spec,
                    o_spec=pl.BlockSpec((tm, tn), lambda i, j, k: (i, j)),
                    out_shape=jax.ShapeDtypeStruct((M, N), out_dtype), dims=dims, acc_shape=(tm, tn))


def _ffn_in_fwd(h, wfi, *, name, tm=512):
    S = h.shape[0]
    tm = _pick_tile(S, tm)
    return _mm_call(h, wfi, name=name, grid=(S // tm, N_DEV, 1),
                    a_spec=pl.BlockSpec((tm, D_MODEL), lambda i, j, k: (i, 0)),
                    b_spec=pl.BlockSpec((None, D_MODEL, FF_BLK), lambda i, j, k: (j, 0, 0)),
                    o_spec=pl.BlockSpec((None, tm, FF_BLK), lambda i, j, k: (j, i, 0)),
                    out_shape=jax.ShapeDtypeStruct((N_DEV, S, FF_BLK), BF16), dims=NN, acc_shape=None)


def _ffn_out_fwd(z, wfo, *, name, tm=512):
    S = z.shape[1]
    tm = _pick_tile(S, tm)
    return _mm_call(z, wfo, name=name, grid=(S // tm, 1, 4),
                    a_spec=pl.BlockSpec((None, tm, FF_BLK), lambda i, j, k: (k, i, 0)),
                    b_spec=pl.BlockSpec((None, FF_BLK, D_MODEL), lambda i, j, k: (k, 0, 0)),
                    o_spec=pl.BlockSpec((tm, D_MODEL), lambda i, j, k: (i, 0)),
                    out_shape=jax.ShapeDtypeStruct((S, D_MODEL), F32), dims=NN, acc_shape=(tm, D_MODEL))


def _ffn_out_bwd_x(dy, wfo, *, name, tm=512):
    S = dy.shape[0]
    tm = _pick_tile(S, tm)
    return _mm_call(dy, wfo, name=name, grid=(S // tm, 4, 1),
                    a_spec=pl.BlockSpec((tm, D_MODEL), lambda i, j, k: (i, 0)),
                    b_spec=pl.BlockSpec((None, FF_BLK, D_MODEL), lambda i, j, k: (j, 0, 0)),
                    o_spec=pl.BlockSpec((None, tm, FF_BLK), lambda i, j, k: (j, i, 0)),
                    out_shape=jax.ShapeDtypeStruct((4, S, FF_BLK), BF16), dims=NT, acc_shape=None)


def _ffn_out_bwd_w(z, dy, *, name, tk=1024):
    S = dy.shape[0]
    tk = _pick_tile(S, tk)
    return _mm_call(z, dy, name=name, grid=(4, 1, S // tk),
                    a_spec=pl.BlockSpec((None, tk, FF_BLK), lambda i, j, k: (i, k, 0)),
                    b_spec=pl.BlockSpec((tk, D_MODEL), lambda i, j, k: (k, 0)),
                    o_spec=pl.BlockSpec((None, FF_BLK, D_MODEL), lambda i, j, k: (i, 0, 0)),
                    out_shape=jax.ShapeDtypeStruct((4, FF_BLK, D_MODEL), BF16), dims=TN,
                    acc_shape=(FF_BLK, D_MODEL))


def _ffn_in_bwd_x(dgu, wfi, *, name, tm=512):
    S = dgu.shape[1]
    tm = _pick_tile(S, tm)
    return _mm_call(dgu, wfi, name=name, grid=(S // tm, 1, N_DEV),
                    a_spec=pl.BlockSpec((None, tm, FF_BLK), lambda i, j, k: (k, i, 0)),
                    b_spec=pl.BlockSpec((None, D_MODEL, FF_BLK), lambda i, j, k: (k, 0, 0)),
                    o_spec=pl.BlockSpec((tm, D_MODEL), lambda i, j, k: (i, 0)),
                    out_shape=jax.ShapeDtypeStruct((S, D_MODEL), F32), dims=NT, acc_shape=(tm, D_MODEL))


def _ffn_in_bwd_w(h, dgu, *, name, tm=512, tk=1024):
    S = h.shape[0]
    tk = _pick_tile(S, tk)
    return _mm_call(h, dgu, name=name, grid=(D_MODEL // tm, N_DEV, S // tk),
                    a_spec=pl.BlockSpec((tk, tm), lambda i, j, k: (k, i)),
                    b_spec=pl.BlockSpec((None, tk, FF_BLK), lambda i, j, k: (j, k, 0)),
                    o_spec=pl.BlockSpec((None, tm, FF_BLK), lambda i, j, k: (j, i, 0)),
                    out_shape=jax.ShapeDtypeStruct((N_DEV, D_MODEL, FF_BLK), BF16), dims=TN,
                    acc_shape=(tm, FF_BLK))


def _row_spec(T, W, col_block=0):
    return pl.BlockSpec((T, W), lambda i: (i, col_block))


def _vec_spec(W):
    return pl.BlockSpec((1, W), lambda i: (0, 0))


def _rms_stats(x):
    r = lax.rsqrt(jnp.mean(x * x, axis=-1, keepdims=True) + EPS)
    return r, x * r


def _rms_bwd(xh, r, g, dy):
    dxh = dy * g
    dx = r * (dxh - xh * jnp.mean(dxh * xh, axis=-1, keepdims=True))
    dg = jnp.sum(dy * xh, axis=0, keepdims=True)
    return dx, dg


def _accum(ref, val):
    @pl.when(pl.program_id(0) == 0)
    def _():
        ref[...] = val

    @pl.when(pl.program_id(0) > 0)
    def _():
        ref[...] += val


def _rms_fwd_call(x, g, *, name, T=256):
    S, W = x.shape

    def body(x_ref, g_ref, o_ref):
        _, xh = _rms_stats(x_ref[...])
        o_ref[...] = (xh * g_ref[...]).astype(o_ref.dtype)

    return pl.pallas_call(
        body, name=name, grid=(S // T,),
        in_specs=[_row_spec(T, W), _vec_spec(W)],
        out_specs=_row_spec(T, W),
        out_shape=jax.ShapeDtypeStruct((S, W), BF16),
        compiler_params=_cp(("parallel",)),
    )(x, g)


def _merge_fwd_call(t1, t2, pg, *, name, T=256):
    S = t1.shape[0]

    def body(t1_ref, t2_ref, gg_ref, gs_ref, o_ref):
        sg = _sig(gg_ref[...].astype(F32))
        ss = _sig(gs_ref[...].astype(F32))
        o_ref[...] = (sg * t1_ref[...].astype(F32) + ss * t2_ref[...].astype(F32)).astype(o_ref.dtype)

    return pl.pallas_call(
        body, name=name, grid=(S // T,),
        in_specs=[_row_spec(T, D_MODEL), _row_spec(T, D_MODEL), _row_spec(T, D_MODEL, 0), _row_spec(T, D_MODEL, 1)],
        out_specs=_row_spec(T, D_MODEL),
        out_shape=jax.ShapeDtypeStruct((S, D_MODEL), BF16),
        compiler_params=_cp(("parallel",)),
    )(t1, t2, pg, pg)


def _merge_bwd_call(dm, t1, t2, pg, *, name, T=256):
    S = t1.shape[0]

    def body(dm_ref, t1_ref, t2_ref, gg_ref, gs_ref, dt1_ref, dt2_ref, dg_ref):
        dmv = dm_ref[...].astype(F32)
        sg = _sig(gg_ref[...].astype(F32))
        ss = _sig(gs_ref[...].astype(F32))
        dt1_ref[...] = (dmv * sg).astype(BF16)
        dt2_ref[...] = (dmv * ss).astype(BF16)
        dg_ref[:, :D_MODEL] = (dmv * t1_ref[...].astype(F32) * sg * (1.0 - sg)).astype(BF16)
        dg_ref[:, D_MODEL:] = (dmv * t2_ref[...].astype(F32) * ss * (1.0 - ss)).astype(BF16)

    out = jax.ShapeDtypeStruct((S, D_MODEL), BF16)
    return pl.pallas_call(
        body, name=name, grid=(S // T,),
        in_specs=[_row_spec(T, D_MODEL)] * 3 + [_row_spec(T, D_MODEL, 0), _row_spec(T, D_MODEL, 1)],
        out_specs=[_row_spec(T, D_MODEL), _row_spec(T, D_MODEL), _row_spec(T, 2 * D_MODEL)],
        out_shape=[out, out, jax.ShapeDtypeStruct((S, 2 * D_MODEL), BF16)],
        compiler_params=_cp(("parallel",)),
    )(dm, t1, t2, pg, pg)


def _post_mix_call(x, mix, g2, g3, *, name, T=256):
    S = x.shape[0]

    def body(x_ref, mix_ref, g2_ref, g3_ref, x1_ref, h_ref):
        _, mh = _rms_stats(mix_ref[...])
        x1 = x_ref[...] + mh * g2_ref[...]
        x1_ref[...] = x1
        _, xh = _rms_stats(x1)
        h_ref[...] = (xh * g3_ref[...]).astype(BF16)

    return pl.pallas_call(
        body, name=name, grid=(S // T,),
        in_specs=[_row_spec(T, D_MODEL)] * 2 + [_vec_spec(D_MODEL)] * 2,
        out_specs=[_row_spec(T, D_MODEL)] * 2,
        out_shape=[jax.ShapeDtypeStruct((S, D_MODEL), F32), jax.ShapeDtypeStruct((S, D_MODEL), BF16)],
        compiler_params=_cp(("parallel",)),
    )(x, mix, g2, g3)


def _swiglu_fwd_call(gu, *, name, T=256):
    S = gu.shape[1]

    def body(gate_ref, up_ref, z_ref):
        gt = gate_ref[...].astype(F32)
        z_ref[...] = (gt * _sig(gt) * up_ref[...].astype(F32)).astype(BF16)

    return pl.pallas_call(
        body, name=name, grid=(4, S // T),
        in_specs=[pl.BlockSpec((None, T, FF_BLK), lambda d, i: (d, i, 0)),
                  pl.BlockSpec((None, T, FF_BLK), lambda d, i: (d + 4, i, 0))],
        out_specs=pl.BlockSpec((None, T, FF_BLK), lambda d, i: (d, i, 0)),
        out_shape=jax.ShapeDtypeStruct((4, S, FF_BLK), BF16),
        compiler_params=_cp(("parallel", "parallel")),
    )(gu, gu)


def _swiglu_bwd_call(dz, gu, *, name, T=256):
    S = gu.shape[1]

    def body(dz_ref, gu_ref, o_ref):
        dzv = dz_ref[...].astype(F32)
        gt = gu_ref[0].astype(F32)
        up = gu_ref[1].astype(F32)
        s = _sig(gt)
        o_ref[0] = (dzv * up * s * (1.0 + gt * (1.0 - s))).astype(BF16)
        o_ref[1] = (dzv * gt * s).astype(BF16)

    pair = pl.BlockSpec((2, None, T, FF_BLK), lambda d, i: (0, d, i, 0))
    out = pl.pallas_call(
        body, name=name, grid=(4, S // T),
        in_specs=[pl.BlockSpec((None, T, FF_BLK), lambda d, i: (d, i, 0)), pair],
        out_specs=pair,
        out_shape=jax.ShapeDtypeStruct((2, 4, S, FF_BLK), BF16),
        compiler_params=_cp(("parallel", "parallel")),
    )(dz, gu.reshape(2, 4, S, FF_BLK))
    return out.reshape(N_DEV, S, FF_BLK)


def _loss_call(x1, y, target, g4, *, name, T=256):
    S = x1.shape[0]

    def body(x1_ref, y_ref, t_ref, g4_ref, loss_ref, dx2_ref, dy_ref, dg4_ref):
        r, yh = _rms_stats(y_ref[...])
        diff = x1_ref[...] + yh * g4_ref[...] - t_ref[...]
        part = 0.5 * jnp.sum(jnp.mean(diff * diff, axis=-1, keepdims=True), axis=0, keepdims=True)
        _accum(loss_ref, jnp.broadcast_to(part, (1, LANES)))
        dx2 = diff * (1.0 / D_MODEL)
        dx2_ref[...] = dx2
        dy, dg = _rms_bwd(yh, r, g4_ref[...], dx2)
        dy_ref[...] = dy.astype(BF16)
        _accum(dg4_ref, dg)

    return pl.pallas_call(
        body, name=name, grid=(S // T,),
        in_specs=[_row_spec(T, D_MODEL)] * 3 + [_vec_spec(D_MODEL)],
        out_specs=[_vec_spec(LANES), _row_spec(T, D_MODEL), _row_spec(T, D_MODEL), _vec_spec(D_MODEL)],
        out_shape=[jax.ShapeDtypeStruct((1, LANES), F32), jax.ShapeDtypeStruct((S, D_MODEL), F32),
                   jax.ShapeDtypeStruct((S, D_MODEL), BF16), jax.ShapeDtypeStruct((1, D_MODEL), F32)],
        compiler_params=_cp(("arbitrary",)),
    )(x1, y, target, g4)


def _mid_bwd_call(dx2, dh, x1, mix, g3, g2, *, name, T=256):
    S = x1.shape[0]

    def body(dx2_ref, dh_ref, x1_ref, mix_ref, g3_ref, g2_ref, dx1_ref, dmix_ref, dg3_ref, dg2_ref):
        r3, xh = _rms_stats(x1_ref[...])
        d3, dg3 = _rms_bwd(xh, r3, g3_ref[...], dh_ref[...])
        dx1 = dx2_ref[...] + d3
        dx1_ref[...] = dx1
        r2, mh = _rms_stats(mix_ref[...])
        dmix, dg2 = _rms_bwd(mh, r2, g2_ref[...], dx1)
        dmix_ref[...] = dmix.astype(BF16)
        _accum(dg3_ref, dg3)
        _accum(dg2_ref, dg2)

    return pl.pallas_call(
        body, name=name, grid=(S // T,),
        in_specs=[_row_spec(T, D_MODEL)] * 4 + [_vec_spec(D_MODEL)] * 2,
        out_specs=[_row_spec(T, D_MODEL), _row_spec(T, D_MODEL), _vec_spec(D_MODEL), _vec_spec(D_MODEL)],
        out_shape=[jax.ShapeDtypeStruct((S, D_MODEL), F32), jax.ShapeDtypeStruct((S, D_MODEL), BF16),
                   jax.ShapeDtypeStruct((1, D_MODEL), F32), jax.ShapeDtypeStruct((1, D_MODEL), F32)],
        compiler_params=_cp(("arbitrary",)),
    )(dx2, dh, x1, mix, g3, g2)


def _rms1_bwd_call(dx1, da_parts, x, g1, *, name, T=256):
    S = x.shape[0]
    n = len(da_parts)

    def body(dx1_ref, *rest):
        da_refs, (x_ref, g1_ref, gx_ref, dg1_ref) = rest[:n], rest[n:]
        da = da_refs[0][...]
        for ref in da_refs[1:]:
            da = da + ref[...]
        r, xh = _rms_stats(x_ref[...])
        dxa, dg = _rms_bwd(xh, r, g1_ref[...], da)
        gx_ref[...] = dx1_ref[...] + dxa
        _accum(dg1_ref, dg)

    return pl.pallas_call(
        body, name=name, grid=(S // T,),
        in_specs=[_row_spec(T, D_MODEL)] * (n + 2) + [_vec_spec(D_MODEL)],
        out_specs=[_row_spec(T, D_MODEL), _vec_spec(D_MODEL)],
        out_shape=[jax.ShapeDtypeStruct((S, D_MODEL), F32), jax.ShapeDtypeStruct((1, D_MODEL), F32)],
        compiler_params=_cp(("arbitrary",)),
    )(dx1, *da_parts, x, g1)


GLA_TILE = 256
Q_OFF, K_OFF, V_OFF, R_OFF = 0, QK, 2 * QK, 2 * QK + GV


def _tri(lower):
    r = lax.broadcasted_iota(jnp.int32, (CHUNK, CHUNK), 0)
    c = lax.broadcasted_iota(jnp.int32, (CHUNK, CHUNK), 1)
    return jnp.where((r >= c) if lower else (c >= r), 1.0, 0.0).astype(BF16)


def _gla_fwd_call(pa, alow, wg, bgate, gnorm, *, name):
    S = pa.shape[0]
    Tg = min(GLA_TILE, S)
    cb = Tg // CHUNK

    def body(pa_ref, al_ref, wg_ref, bg_ref, gn_ref, y_ref, st_ref, state):
        @pl.when(pl.program_id(0) == 0)
        def _():
            state[...] = jnp.zeros_like(state)

        logit = _dot(al_ref[...], wg_ref[...], NN) + bg_ref[...]
        ls = _logsig(logit) * (1.0 / 16.0)
        tri = _tri(True)
        for h in range(HEADS):
            for c in range(cb):
                rows = pl.ds(c * CHUNK, CHUNK)
                cum = _exact_mask_dot(tri, ls[c * CHUNK:(c + 1) * CHUNK, h * DK:(h + 1) * DK])
                tot = cum[CHUNK - 1:CHUNK]
                kd = (pa_ref[rows, pl.ds(K_OFF + h * DK, DK)].astype(F32) * jnp.exp(tot - cum)).astype(BF16)
                new = state[h] * jnp.exp(tot) + _dot(pa_ref[rows, pl.ds(V_OFF + h * DV, DV)], kd, TN)
                state[h] = new
                st_ref[c, h] = new
                qs = (pa_ref[rows, pl.ds(Q_OFF + h * DK, DK)].astype(F32) * Q_SCALE).astype(BF16)
                o = _dot(qs, new.astype(BF16), NT)
                rs = lax.rsqrt(jnp.mean(o * o, axis=-1, keepdims=True) + EPS)
                rr = pa_ref[rows, pl.ds(R_OFF + h * DV, DV)].astype(F32)
                y_ref[rows, pl.ds(h * DV, DV)] = (o * rs * gn_ref[h] * (rr * _sig(rr))).astype(BF16)

    return pl.pallas_call(
        body, name=name, grid=(S // Tg,),
        in_specs=[
            pl.BlockSpec((Tg, 2 * QK + 2 * GV), lambda t: (t, 0)),
            pl.BlockSpec((Tg, LANES), lambda t: (t, 0)),
            pl.BlockSpec((LANES, QK), lambda t: (0, 0)),
            pl.BlockSpec((1, QK), lambda t: (0, 0)),
            pl.BlockSpec((HEADS, 1, DV), lambda t: (0, 0, 0)),
        ],
        out_specs=[
            pl.BlockSpec((Tg, GV), lambda t: (t, 0)),
            pl.BlockSpec((cb, HEADS, DV, DK), lambda t: (t, 0, 0, 0)),
        ],
        out_shape=[jax.ShapeDtypeStruct((S, GV), BF16),
                   jax.ShapeDtypeStruct((S // CHUNK, HEADS, DV, DK), F32)],
        scratch_shapes=[pltpu.VMEM((HEADS, DV, DK), F32)],
        compiler_params=_cp(("arbitrary",), VMEM_BIG),
    )(pa, alow, wg, bgate, gnorm)


def _gla_bwd_call(pa, alow, wg, bgate, gnorm, states, dy, *, name):
    S = pa.shape[0]
    Tg = min(GLA_TILE, S)
    cb = Tg // CHUNK
    nt = S // Tg

    def rev(t):
        return nt - 1 - t

    def body(pa_ref, al_ref, wg_ref, bg_ref, gn_ref, st_ref, prev_ref, dy_ref,
             dpa_ref, dl_ref, dgn_ref, dbg_ref, carry):
        t = pl.program_id(0)

        @pl.when(t == 0)
        def _():
            carry[...] = jnp.zeros_like(carry)
            dgn_ref[...] = jnp.zeros_like(dgn_ref)
            dbg_ref[...] = jnp.zeros_like(dbg_ref)

        logit = _dot(al_ref[...], wg_ref[...], NN) + bg_ref[...]
        ls = _logsig(logit) * (1.0 / 16.0)
        sneg = _sig(-logit)
        tri = _tri(True)
        upper = _tri(False)
        first_tile = rev(t) == 0
        for h in range(HEADS):
            gn = gn_ref[h]
            hk = pl.ds(h * DK, DK)
            dbg = jnp.zeros((1, DK), F32)
            dgn = jnp.zeros((1, DV), F32)
            for c in reversed(range(cb)):
                rows = pl.ds(c * CHUNK, CHUNK)
                cum = _exact_mask_dot(tri, ls[c * CHUNK:(c + 1) * CHUNK, h * DK:(h + 1) * DK])
                tot = cum[CHUNK - 1:CHUNK]
                w = jnp.exp(tot - cum)
                decay = jnp.exp(tot)
                kd = pa_ref[rows, pl.ds(K_OFF + h * DK, DK)].astype(F32) * w
                kd16 = kd.astype(BF16)
                st = st_ref[c, h]
                if c > 0:
                    st_prev = st_ref[c - 1, h]
                else:
                    st_prev = jnp.where(first_tile, 0.0, prev_ref[0, h])
                qs = (pa_ref[rows, pl.ds(Q_OFF + h * DK, DK)].astype(F32) * Q_SCALE).astype(BF16)
                st16 = st.astype(BF16)
                o = _dot(qs, st16, NT)
                rs = lax.rsqrt(jnp.mean(o * o, axis=-1, keepdims=True) + EPS)
                oh = o * rs
                rr = pa_ref[rows, pl.ds(R_OFF + h * DV, DV)].astype(F32)
                sr = _sig(rr)
                dyv = dy_ref[rows, pl.ds(h * DV, DV)].astype(F32)
                dpa_ref[rows, pl.ds(R_OFF + h * DV, DV)] = (
                    dyv * oh * gn * sr * (1.0 + rr * (1.0 - sr))).astype(BF16)
                don = dyv * (rr * sr)
                dgn = dgn + jnp.sum(don * oh, axis=0, keepdims=True)
                doh = don * gn
                do16 = (rs * (doh - oh * jnp.mean(doh * oh, axis=-1, keepdims=True))).astype(BF16)
                dpa_ref[rows, pl.ds(Q_OFF + h * DK, DK)] = (_dot(do16, st16, NN) * Q_SCALE).astype(BF16)
                gt = _dot(do16, qs, TN) + carry[h]
                gt16 = gt.astype(BF16)
                dkd = _dot(pa_ref[rows, pl.ds(V_OFF + h * DV, DV)], gt16, NN)
                dpa_ref[rows, pl.ds(V_OFF + h * DV, DV)] = _dot(kd16, gt16, NT).astype(BF16)
                ddecay = jnp.sum(gt * st_prev, axis=0, keepdims=True)
                dpa_ref[rows, pl.ds(K_OFF + h * DK, DK)] = (dkd * w).astype(BF16)
                e = dkd * kd
                dtot = jnp.sum(e, axis=0, keepdims=True) + ddecay * decay
                dls = dtot - _exact_mask_dot(upper, e)
                dlogit = dls * (1.0 / 16.0) * sneg[c * CHUNK:(c + 1) * CHUNK, h * DK:(h + 1) * DK]
                dl_ref[rows, hk] = dlogit.astype(BF16)
                dbg = dbg + jnp.sum(dlogit, axis=0, keepdims=True)
                carry[h] = gt * decay
            dgn_ref[h] += dgn
            dbg_ref[:, hk] += dbg

    return pl.pallas_call(
        body, name=name, grid=(nt,),
        in_specs=[
            pl.BlockSpec((Tg, 2 * QK + 2 * GV), lambda t: (rev(t), 0)),
            pl.BlockSpec((Tg, LANES), lambda t: (rev(t), 0)),
            pl.BlockSpec((LANES, QK), lambda t: (0, 0)),
            pl.BlockSpec((1, QK), lambda t: (0, 0)),
            pl.BlockSpec((HEADS, 1, DV), lambda t: (0, 0, 0)),
            pl.BlockSpec((cb, HEADS, DV, DK), lambda t: (rev(t), 0, 0, 0)),
            pl.BlockSpec((1, HEADS, DV, DK), lambda t: (jnp.maximum(rev(t) * cb - 1, 0), 0, 0, 0)),
            pl.BlockSpec((Tg, GV), lambda t: (rev(t), 0)),
        ],
        out_specs=[
            pl.BlockSpec((Tg, 2 * QK + 2 * GV), lambda t: (rev(t), 0)),
            pl.BlockSpec((Tg, QK), lambda t: (rev(t), 0)),
            pl.BlockSpec((HEADS, 1, DV), lambda t: (0, 0, 0)),
            pl.BlockSpec((1, QK), lambda t: (0, 0)),
        ],
        out_shape=[jax.ShapeDtypeStruct((S, 2 * QK + 2 * GV), BF16), jax.ShapeDtypeStruct((S, QK), BF16),
                   jax.ShapeDtypeStruct((HEADS, 1, DV), F32), jax.ShapeDtypeStruct((1, QK), F32)],
        scratch_shapes=[pltpu.VMEM((HEADS, DV, DK), F32)],
        compiler_params=_cp(("arbitrary",), VMEM_BIG),
    )(pa, alow, wg, bgate, gnorm, states, states, dy)


SGU_TILE = 256


def _sgu_mask():
    r = lax.broadcasted_iota(jnp.int32, (SBLOCK, SBLOCK), 0)
    c = lax.broadcasted_iota(jnp.int32, (SBLOCK, SBLOCK), 1)
    return (c < CHUNK) | (r >= CHUNK)


def _ln_stats(vf):
    mu = jnp.mean(vf, axis=-1, keepdims=True)
    xc = vf - mu
    rs = lax.rsqrt(jnp.mean(xc * xc, axis=-1, keepdims=True) + EPS)
    return rs, xc * rs


def _sgu_fwd_call(ps, ln_g, ln_b, w_sp, b_sp, *, name):
    S = ps.shape[0]
    Ts = min(SGU_TILE, S)

    def body(ps_ref, lg_ref, lb_ref, w_ref, b_ref, y_ref):
        mask = _sgu_mask()
        for g in range(GROUPS):
            wm = jnp.where(mask, w_ref[g], 0.0).astype(BF16)
            for p in range(Ts // SBLOCK):
                rows = pl.ds(p * SBLOCK, SBLOCK)
                u = _gelu(ps_ref[rows, pl.ds(g * DG, DG)].astype(F32))
                _, xh = _ln_stats(_gelu(ps_ref[rows, pl.ds(D_MODEL + g * DG, DG)].astype(F32)))
                vn = xh * lg_ref[g] + lb_ref[g]
                mixed = _dot(wm, vn.astype(BF16), NN) + b_ref[g]
                y_ref[rows, pl.ds(g * DG, DG)] = (u * mixed).astype(BF16)

    full3 = lambda a, b, c: pl.BlockSpec((a, b, c), lambda t: (0, 0, 0))
    return pl.pallas_call(
        body, name=name, grid=(S // Ts,),
        in_specs=[pl.BlockSpec((Ts, 2 * D_MODEL), lambda t: (t, 0)),
                  full3(GROUPS, 1, DG), full3(GROUPS, 1, DG), full3(GROUPS, SBLOCK, SBLOCK), full3(GROUPS, SBLOCK, 1)],
        out_specs=pl.BlockSpec((Ts, D_MODEL), lambda t: (t, 0)),
        out_shape=jax.ShapeDtypeStruct((S, D_MODEL), BF16),
        compiler_params=_cp(("parallel",)),
    )(ps, ln_g, ln_b, w_sp, b_sp)


def _sgu_bwd_call(ps, ln_g, ln_b, w_sp, b_sp, dy, *, name):
    S = ps.shape[0]
    Ts = min(SGU_TILE, S)

    def body(ps_ref, lg_ref, lb_ref, w_ref, b_ref, dy_ref, ds_ref, dlg_ref, dlb_ref, dw_ref, db_ref):
        @pl.when(pl.program_id(0) == 0)
        def _():
            dlg_ref[...] = jnp.zeros_like(dlg_ref)
            dlb_ref[...] = jnp.zeros_like(dlb_ref)
            dw_ref[...] = jnp.zeros_like(dw_ref)
            db_ref[...] = jnp.zeros_like(db_ref)

        mask = _sgu_mask()
        for g in range(GROUPS):
            wm = jnp.where(mask, w_ref[g], 0.0).astype(BF16)
            lg = lg_ref[g]
            for p in range(Ts // SBLOCK):
                rows = pl.ds(p * SBLOCK, SBLOCK)
                su = ps_ref[rows, pl.ds(g * DG, DG)].astype(F32)
                sv = ps_ref[rows, pl.ds(D_MODEL + g * DG, DG)].astype(F32)
                u = _gelu(su)
                rs, xh = _ln_stats(_gelu(sv))
                vn16 = (xh * lg + lb_ref[g]).astype(BF16)
                mixed = _dot(wm, vn16, NN) + b_ref[g]
                dyv = dy_ref[rows, pl.ds(g * DG, DG)].astype(F32)
                ds_ref[rows, pl.ds(g * DG, DG)] = (dyv * mixed * _gelu_grad(su)).astype(BF16)
                dmix = dyv * u
                dmix16 = dmix.astype(BF16)
                db_ref[g] += jnp.sum(dmix, axis=-1, keepdims=True)
                dw_ref[g] += jnp.where(mask, _dot(dmix16, vn16, NT), 0.0)
                dvn = _dot(wm, dmix16, TN)
                dlg_ref[g] += jnp.sum(dvn * xh, axis=0, keepdims=True)
                dlb_ref[g] += jnp.sum(dvn, axis=0, keepdims=True)
                dxh = dvn * lg
                dvf = rs * (dxh - jnp.mean(dxh, axis=-1, keepdims=True)
                            - xh * jnp.mean(dxh * xh, axis=-1, keepdims=True))
                ds_ref[rows, pl.ds(D_MODEL + g * DG, DG)] = (dvf * _gelu_grad(sv)).astype(BF16)

    full3 = lambda a, b, c: pl.BlockSpec((a, b, c), lambda t: (0, 0, 0))
    return pl.pallas_call(
        body, name=name, grid=(S // Ts,),
        in_specs=[pl.BlockSpec((Ts, 2 * D_MODEL), lambda t: (t, 0)),
                  full3(GROUPS, 1, DG), full3(GROUPS, 1, DG), full3(GROUPS, SBLOCK, SBLOCK), full3(GROUPS, SBLOCK, 1),
                  pl.BlockSpec((Ts, D_MODEL), lambda t: (t, 0))],
        out_specs=[pl.BlockSpec((Ts, 2 * D_MODEL), lambda t: (t, 0)),
                   full3(GROUPS, 1, DG), full3(GROUPS, 1, DG), full3(GROUPS, SBLOCK, SBLOCK), full3(GROUPS, SBLOCK, 1)],
        out_shape=[jax.ShapeDtypeStruct((S, 2 * D_MODEL), BF16),
                   jax.ShapeDtypeStruct((GROUPS, 1, DG), F32), jax.ShapeDtypeStruct((GROUPS, 1, DG), F32),
                   jax.ShapeDtypeStruct((GROUPS, SBLOCK, SBLOCK), F32),
                   jax.ShapeDtypeStruct((GROUPS, SBLOCK, 1), F32)],
        compiler_params=_cp(("arbitrary",)),
    )(ps, ln_g, ln_b, w_sp, b_sp, dy)


def _position():
    return lax.axis_index("x"), lax.axis_index("y"), lax.axis_index("c")


def _gather_copies(srcs, dsts, send_sems, recv_sems, local_sems):
    x, y, c = _position()
    me, sibling = (x, y, c), (x, y, 1 - c)
    chips = [(1 - x, y), (x, 1 - y), (1 - x, 1 - y)]
    n = len(srcs)

    def slab(a, block):
        px, py, pc = block
        return dsts[a].at[4 * px + 2 * py + pc]

    def copy(a, k, block, to, src=None):
        return pltpu.make_async_remote_copy(
            src_ref=slab(a, block) if src is None else src, dst_ref=slab(a, block),
            send_sem=send_sems.at[7 * a + k], recv_sem=recv_sems.at[7 * a + k], device_id=to, device_id_type=MESH)

    mine = [pltpu.make_async_copy(srcs[a], slab(a, me), local_sems.at[a]) for a in range(n)]
    for cp in mine:
        cp.start()
    first = []
    for a in range(n):
        first.append(copy(a, 0, me, sibling, src=srcs[a]))
        first += [copy(a, 1 + j, me, (*chip, c), src=srcs[a]) for j, chip in enumerate(chips)]
    for cp in first:
        cp.start()
    passed = []
    for j, chip in enumerate(chips):
        for a in range(n):
            copy(a, 1 + j, (*chip, c), me).wait_recv()
            cp = copy(a, 4 + j, (*chip, c), sibling)
            cp.start()
            passed.append(cp)
    for a in range(n):
        copy(a, 0, sibling, me).wait_recv()
        for j, chip in enumerate(chips):
            copy(a, 4 + j, (*chip, 1 - c), me).wait_recv()
    for cp in first + passed:
        cp.wait_send()
    for cp in mine:
        cp.wait()


def _all_gather_hbm(shards, *, name):
    n = len(shards)

    def body(*refs):
        srcs, dsts = refs[:n], refs[n:2 * n]
        send_sems, recv_sems, local_sems = refs[2 * n:]
        _gather_copies(srcs, dsts, send_sems, recv_sems, local_sems)

    return pl.pallas_call(
        body, name=name,
        in_specs=[ANY] * n, out_specs=[ANY] * n,
        out_shape=[jax.ShapeDtypeStruct((N_DEV, *s.shape), s.dtype) for s in shards],
        scratch_shapes=[pltpu.SemaphoreType.DMA((7 * n,)), pltpu.SemaphoreType.DMA((7 * n,)),
                        pltpu.SemaphoreType.DMA((n,))],
    )(*shards)


def _all_reduce_small(part, *, name):
    R, W = part.shape

    def body(x_ref, out_ref, gathered, send_sems, recv_sems, local_sems):
        _gather_copies([x_ref], [gathered], send_sems, recv_sems, local_sems)
        acc = gathered[0]
        for d in range(1, N_DEV):
            acc = acc + gathered[d]
        out_ref[...] = acc

    return pl.pallas_call(
        body, name=name,
        in_specs=[VMEM_SPEC], out_specs=VMEM_SPEC,
        out_shape=jax.ShapeDtypeStruct((R, W), F32),
        scratch_shapes=[pltpu.VMEM((N_DEV, R, W), F32),
                        pltpu.SemaphoreType.DMA((7,)), pltpu.SemaphoreType.DMA((7,)), pltpu.SemaphoreType.DMA((1,))],
    )(part)


def _scatter_blocks(parts, *, name):
    n = len(parts)
    flips = [(fx, fy, fc) for fx in (0, 1) for fy in (0, 1) for fc in (0, 1)][1:]

    def body(*refs):
        srcs, dsts = refs[:n], refs[n:2 * n]
        send_sems, recv_sems, local_sems = refs[2 * n:]
        x, y, c = _position()
        me = 4 * x + 2 * y + c
        mine = [pltpu.make_async_copy(srcs[a].at[me], dsts[a].at[me], local_sems.at[a]) for a in range(n)]
        for cp in mine:
            cp.start()
        copies = []
        for k, (fx, fy, fc) in enumerate(flips):
            tx = 1 - x if fx else x
            ty = 1 - y if fy else y
            tc = 1 - c if fc else c
            peer = 4 * tx + 2 * ty + tc
            for a in range(n):
                cp = pltpu.make_async_remote_copy(
                    src_ref=srcs[a].at[peer], dst_ref=dsts[a].at[me],
                    send_sem=send_sems.at[7 * a + k], recv_sem=recv_sems.at[7 * a + k],
                    device_id=(tx, ty, tc), device_id_type=MESH)
                cp.start()
                copies.append(cp)
        for cp in copies:
            cp.wait()
        for cp in mine:
            cp.wait()

    return pl.pallas_call(
        body, name=name,
        in_specs=[ANY] * n, out_specs=[ANY] * n,
        out_shape=[jax.ShapeDtypeStruct(p.shape, p.dtype) for p in parts],
        scratch_shapes=[pltpu.SemaphoreType.DMA((7 * n,)), pltpu.SemaphoreType.DMA((7 * n,)),
                        pltpu.SemaphoreType.DMA((n,))],
    )(*parts)


def _adamw_math(w, g, m, v):
    m = ADAM_B1 * m + (1.0 - ADAM_B1) * g
    v = ADAM_B2 * v + (1.0 - ADAM_B2) * (g * g)
    m_hat = m / (1.0 - ADAM_B1 ** ADAM_STEP)
    v_hat = v / (1.0 - ADAM_B2 ** ADAM_STEP)
    delta = -ADAM_LR * (m_hat / (jnp.sqrt(v_hat) + ADAM_EPS) + ADAM_WD * w)
    return delta, m, v


def _adamw_reduce_call(recv, w, m, v, *, name, T=128):
    R, W = w.shape
    T = T if R % T == 0 else R // 2
    assert R % T == 0 and T % 16 == 0, (R, T)

    def body(p_ref, w_ref, m_ref, v_ref, g_out, d_out, m_out, v_out):
        g = p_ref[0].astype(F32)
        for d in range(1, N_DEV):
            g = g + p_ref[d].astype(F32)
        g_out[...] = g
        d_out[...], m_out[...], v_out[...] = _adamw_math(w_ref[...], g, m_ref[...], v_ref[...])

    row = pl.BlockSpec((T, W), lambda i: (i, 0))
    out = jax.ShapeDtypeStruct((R, W), F32)
    return pl.pallas_call(
        body, name=name, grid=(R // T,),
        in_specs=[pl.BlockSpec((N_DEV, T, W), lambda i: (0, i, 0)), row, row, row],
        out_specs=[row] * 4, out_shape=[out] * 4,
        compiler_params=_cp(("parallel",), VMEM_BIG),
    )(recv, w, m, v)


def _adamw_small_call(w, g, m, v, *, name):
    def body(w_ref, g_ref, m_ref, v_ref, d_out, m_out, v_out):
        d_out[...], m_out[...], v_out[...] = _adamw_math(w_ref[...], g_ref[...], m_ref[...], v_ref[...])

    out = jax.ShapeDtypeStruct(w.shape, F32)
    return pl.pallas_call(body, name=name, in_specs=[VMEM_SPEC] * 4, out_specs=[VMEM_SPEC] * 3,
                          out_shape=[out] * 3)(w, g, m, v)


def _pack_rows(parts, rows):
    buf = jnp.concatenate([p.reshape(-1, LANES) for p in parts], axis=0)
    return jnp.pad(buf, ((0, rows - buf.shape[0]), (0, 0)))


def _unpack_rows(buf, shapes):
    out, off = [], 0
    for shp in shapes:
        n = 1
        for s in shp:
            n *= s
        out.append(buf[off:off + n // LANES].reshape(shp))
        off += n // LANES
    return out


def _in_columns(blocks, lo, hi):
    pieces = []
    for d in range(N_DEV):
        s, e = max(lo, IN_BLK * d), min(hi, IN_BLK * (d + 1))
        if s < e:
            pieces.append(blocks[d][:, s - IN_BLK * d:e - IN_BLK * d])
    return jnp.concatenate(pieces, axis=1)


def _in_blocks(segments):
    blocks = []
    for d in range(N_DEV):
        lo, hi = IN_BLK * d, IN_BLK * (d + 1)
        pieces = []
        for arr, c0 in segments:
            s, e = max(lo, c0), min(hi, c0 + arr.shape[1])
            if s < e:
                pieces.append(arr[:, s - c0:e - c0])
        blocks.append(jnp.concatenate(pieces, axis=1))
    return jnp.stack(blocks)


def _local_step(x, target, W):
    g1, g2, g3, g4 = [W[n].reshape(1, D_MODEL) for n in ("norm_pre_mix", "norm_post_mix", "norm_pre_ffn", "norm_post_ffn")]
    w_a, w_low, w_s, w_g = W["w_in_a"], W["w_in_low"], W["w_in_s"], W["w_in_g"]
    wfi, wfo = W["w_ffn_in"], W["w_ffn_out"].reshape(4, FF_BLK, D_MODEL)
    wg = jnp.pad(W["w_gate_up"], ((0, LANES - RANK), (0, 0))).astype(BF16)
    bgate = W["b_gate"].reshape(1, QK)
    gnorm = W["gla_norm"].reshape(HEADS, 1, DV)
    ln_g = W["sgu_ln_g"].reshape(GROUPS, 1, DG)
    ln_b = W["sgu_ln_b"].reshape(GROUPS, 1, DG)
    w_sp = W["w_spatial"]
    b_sp = W["b_spatial"].reshape(GROUPS, SBLOCK, 1)

    a = _rms_fwd_call(x, g1, name="rms_pre_mix")
    pa = _mm(a, w_a, "nn", BF16, name="in_proj_qkvr")
    alow = _mm(a, w_low, "nn", BF16, name="in_proj_low")
    ps = _mm(a, w_s, "nn", BF16, name="in_proj_sgu")
    pg = _mm(a, w_g, "nn", BF16, name="in_proj_gates")
    y_gla, states = _gla_fwd_call(pa, alow, wg, bgate, gnorm, name="gla_fwd")
    y_sgu = _sgu_fwd_call(ps, ln_g, ln_b, w_sp, b_sp, name="sgu_fwd")
    t1 = _mm(y_gla, W["w_branch_gla"], "nn", BF16, name="branch_gla")
    t2 = _mm(y_sgu, W["w_branch_sgu"], "nn", BF16, name="branch_sgu")
    merged = _merge_fwd_call(t1, t2, pg, name="merge_fwd")
    mix = _mm(merged, W["w_out"], "nn", F32, name="out_proj")
    x1, h = _post_mix_call(x, mix, g2, g3, name="post_mix")
    gu = _ffn_in_fwd(h, wfi, name="ffn_in")
    z = _swiglu_fwd_call(gu, name="swiglu_fwd")
    y = _ffn_out_fwd(z, wfo, name="ffn_out")
    loss, dx2, dy, dg4 = _loss_call(x1, y, target, g4, name="loss_head")

    grads = {"norm_post_ffn": dg4}
    dz = _ffn_out_bwd_x(dy, wfo, name="d_ffn_out_x")
    grads["w_ffn_out"] = _ffn_out_bwd_w(z, dy, name="d_ffn_out_w").reshape(N_DEV, D_FF // N_DEV, D_MODEL)
    dgu = _swiglu_bwd_call(dz, gu, name="swiglu_bwd")
    dh = _ffn_in_bwd_x(dgu, wfi, name="d_ffn_in_x")
    grads["w_ffn_in"] = _ffn_in_bwd_w(h, dgu, name="d_ffn_in_w")
    dx1, dmix, grads["norm_pre_ffn"], grads["norm_post_mix"] = _mid_bwd_call(dx2, dh, x1, mix, g3, g2, name="mid_bwd")
    dmerged = _mm(dmix, W["w_out"], "nt", BF16, name="d_out_x")
    dt1, dt2, dpg = _merge_bwd_call(dmerged, t1, t2, pg, name="merge_bwd")
    dy_gla = _mm(dt1, W["w_branch_gla"], "nt", BF16, name="d_branch_gla_x")
    dy_sgu = _mm(dt2, W["w_branch_sgu"], "nt", BF16, name="d_branch_sgu_x")
    rows = D_MODEL // N_DEV
    grads["w_out"] = _mm(merged, dmix, "tn", BF16, name="d_out_w").reshape(N_DEV, rows, D_MODEL)
    grads["w_branch_gla"] = _mm(y_gla, dt1, "tn", BF16, name="d_branch_gla_w").reshape(N_DEV, rows, D_MODEL)
    grads["w_branch_sgu"] = _mm(y_sgu, dt2, "tn", BF16, name="d_branch_sgu_w").reshape(N_DEV, rows, D_MODEL)
    dps, dlg, dlb, dwsp, dbsp = _sgu_bwd_call(ps, ln_g, ln_b, w_sp, b_sp, dy_sgu, name="sgu_bwd")
    dpa, dlogit, dgn, dbg = _gla_bwd_call(pa, alow, wg, bgate, gnorm, states, dy_gla, name="gla_bwd")
    dlow = _mm(dlogit, wg, "nt", BF16, name="d_gate_up_x")
    dwg = _mm(alow, dlogit, "tn", F32, name="d_gate_up_w")
    da_parts = [_mm(dpa, w_a, "nt", F32, name="d_in_qkvr_x"), _mm(dlow, w_low, "nt", F32, name="d_in_low_x"),
                _mm(dps, w_s, "nt", F32, name="d_in_sgu_x"), _mm(dpg, w_g, "nt", F32, name="d_in_gates_x")]
    dw_a = _mm(a, dpa, "tn", BF16, name="d_in_qkvr_w")
    dw_low = _mm(a, dlow, "tn", BF16, name="d_in_low_w")
    dw_s = _mm(a, dps, "tn", BF16, name="d_in_sgu_w")
    dw_g = _mm(a, dpg, "tn", BF16, name="d_in_gates_w")
    grad_x, grads["norm_pre_mix"] = _rms1_bwd_call(dx1, da_parts, x, g1, name="rms_pre_mix_bwd")

    grads["w_in"] = _in_blocks([(dw_a, A_COLS[0]), (dw_low[:, :RANK], LOW_COLS[0]), (dw_s, S_COLS[0]), (dw_g, G_COLS[0])])
    grads["w_gate_up"] = dwg[:RANK]
    grads["b_gate"] = dbg
    grads["gla_norm"] = dgn
    grads["sgu_ln_g"] = dlg
    grads["sgu_ln_b"] = dlb
    grads["w_spatial"] = dwsp
    grads["b_spatial"] = dbsp
    return loss, grad_x, grads


WEIGHTS = ("norm_pre_mix", "w_in", "w_gate_up", "b_gate", "gla_norm", "sgu_ln_g", "sgu_ln_b", "w_spatial",
           "b_spatial", "w_branch_gla", "w_branch_sgu", "w_out", "norm_post_mix", "norm_pre_ffn", "w_ffn_in",
           "w_ffn_out", "norm_post_ffn")
BIG = ("w_in", "w_branch_gla", "w_branch_sgu", "w_out", "w_ffn_in", "w_ffn_out")
SMALL = tuple(n for n in WEIGHTS if n not in BIG)
SMALL_SHARDED = ("w_gate_up", "gla_norm", "sgu_ln_g", "sgu_ln_b")
SMALL_FULL = {"norm_pre_mix": (1024,), "w_gate_up": (16, 512), "b_gate": (512,), "gla_norm": (4, 256),
              "sgu_ln_g": (4, 256), "sgu_ln_b": (4, 256), "w_spatial": (4, 128, 128), "b_spatial": (4, 128),
              "norm_post_mix": (1024,), "norm_pre_ffn": (1024,), "norm_post_ffn": (1024,)}
SMALL_GRAD_ROWS = 640
SMALL_STATE_ROWS = 568
SMALL_GATHER_ROWS = 16


def kernel(x, norm_pre_mix, w_in, w_gate_up, b_gate, gla_norm, sgu_ln_g, sgu_ln_b, w_spatial, b_spatial, w_branch_gla, w_branch_sgu, w_out, norm_post_mix, norm_pre_ffn, w_ffn_in, w_ffn_out, norm_post_ffn, loss_target, m_norm_pre_mix, m_w_in, m_w_gate_up, m_b_gate, m_gla_norm, m_sgu_ln_g, m_sgu_ln_b, m_w_spatial, m_b_spatial, m_w_branch_gla, m_w_branch_sgu, m_w_out, m_norm_post_mix, m_norm_pre_ffn, m_w_ffn_in, m_w_ffn_out, m_norm_post_ffn, v_norm_pre_mix, v_w_in, v_w_gate_up, v_b_gate, v_gla_norm, v_sgu_ln_g, v_sgu_ln_b, v_w_spatial, v_b_spatial, v_w_branch_gla, v_w_branch_sgu, v_w_out, v_norm_post_mix, v_norm_pre_ffn, v_w_ffn_in, v_w_ffn_out, v_norm_post_ffn):
    given = dict(locals())
    w = {n: given[n][0] for n in WEIGHTS}
    m = {n: given["m_" + n][0] for n in WEIGHTS}
    v = {n: given["v_" + n][0] for n in WEIGHTS}
    xs, target = x[0], loss_target[0]
    me = 4 * lax.axis_index("x") + 2 * lax.axis_index("y") + lax.axis_index("c")

    small_shard = _pack_rows([w[n] for n in SMALL_SHARDED], SMALL_GATHER_ROWS)
    gathered = _all_gather_hbm([w[n].astype(BF16) for n in BIG] + [small_shard], name="gather_weights")
    W = {n: w[n] for n in SMALL if n not in SMALL_SHARDED}
    blocks = dict(zip(BIG, gathered))
    for n in ("w_branch_gla", "w_branch_sgu", "w_out", "w_ffn_out"):
        W[n] = blocks[n].reshape(-1, D_MODEL)
    W["w_ffn_in"] = blocks["w_ffn_in"]
    W["w_in_a"] = _in_columns(blocks["w_in"], *A_COLS)
    W["w_in_low"] = jnp.pad(_in_columns(blocks["w_in"], *LOW_COLS), ((0, 0), (0, LANES - RANK)))
    W["w_in_s"] = _in_columns(blocks["w_in"], *S_COLS)
    W["w_in_g"] = _in_columns(blocks["w_in"], *G_COLS)
    small_blocks = gathered[-1]
    off = 0
    for n in SMALL_SHARDED:
        r, c = w[n].shape
        rows = r * c // LANES
        blk = small_blocks[:, off:off + rows].reshape(N_DEV, r, c)
        W[n] = blk.transpose(1, 0, 2).reshape(r, N_DEV * c)
        off += rows

    loss_part, grad_x, grads = _local_step(xs, target, W)

    received = dict(zip(BIG, _scatter_blocks([grads[n] for n in BIG], name="scatter_grads")))
    small_part = jnp.concatenate([_pack_rows([grads[n] for n in SMALL], SMALL_GRAD_ROWS),
                                  jnp.broadcast_to(loss_part, (8, LANES))], axis=0)
    small_sum = _all_reduce_small(small_part, name="reduce_small")
    loss = small_sum[SMALL_GRAD_ROWS, 0]
    g_small = dict(zip(SMALL, _unpack_rows(small_sum, [SMALL_FULL[n] for n in SMALL])))
    for n in SMALL_SHARDED:
        c = w[n].shape[1]
        g_small[n] = lax.dynamic_slice_in_dim(g_small[n], me * c, c, axis=1)

    g_big, d_big, m_big, v_big = {}, {}, {}, {}
    for n in BIG:
        g_big[n], d_big[n], m_big[n], v_big[n] = _adamw_reduce_call(received[n], w[n], m[n], v[n], name="adamw_" + n)
    small_shapes = [w[n].shape for n in SMALL]
    small_state = [_pack_rows([s[n] for n in SMALL], SMALL_STATE_ROWS) for s in (w, g_small, m, v)]
    small_out = _adamw_small_call(*small_state, name="adamw_small")
    d_small, m_small, v_small = [dict(zip(SMALL, _unpack_rows(o, small_shapes))) for o in small_out]

    def pick(big, sml):
        return [(big[n] if n in BIG else sml[n])[None] for n in WEIGHTS]

    return (loss, grad_x[None], *pick(g_big, g_small), *pick(d_big, d_small), *pick(m_big, m_small),
            *pick(v_big, v_small))
```

```python
import jax
import jax.numpy as jnp
from jax import lax
from jax.experimental import pallas as pl
from jax.experimental.pallas import tpu as pltpu
from jax.experimental.pallas import tpu_sc as plsc

F32 = jnp.float32
BF16 = jnp.bfloat16

D_MODEL = 1024
N_DEV = 8
CHUNK = 64
HEADS = 4
DK = 128
DV = 256
QK = HEADS * DK
GV = HEADS * DV
RANK = 16
GROUPS = 4
SBLOCK = 128
DG = 256
D_FF = 2816
FF_BLK = 704
EPS = 1e-6
Q_SCALE = DK ** -0.5
LANES = 128
VMEM_BIG = 48 * 1024 * 1024

D_IN = 7184
IN_BLK = 898
A_COLS = (0, 3072)
LOW_COLS = (3072, 3088)
S_COLS = (3088, 5136)
G_COLS = (5136, 7184)

ADAM_LR = 0.001
ADAM_B1 = 0.9
ADAM_B2 = 0.999
ADAM_EPS = 1e-08
ADAM_WD = 0.01
ADAM_STEP = 10

MESH = pl.DeviceIdType.MESH
ANY = pl.BlockSpec(memory_space=pl.ANY)
VMEM_SPEC = pl.BlockSpec(memory_space=pltpu.VMEM)


def _cp(sem=None, vmem=None):
    return pltpu.CompilerParams(dimension_semantics=sem, vmem_limit_bytes=vmem)


def _sig(x):
    return 1.0 / (1.0 + jnp.exp(-x))


def _gelu(x):
    c = 0.7978845608028654
    t = jnp.tanh(c * (x + 0.044715 * x * x * x))
    return 0.5 * x * (1.0 + t)


def _gelu_grad(x):
    c = 0.7978845608028654
    x2 = x * x
    t = jnp.tanh(c * (x + 0.044715 * x * x2))
    return 0.5 * (1.0 + t) + 0.5 * x * (1.0 - t * t) * c * (1.0 + 3.0 * 0.044715 * x2)


def _logsig(x):
    return jnp.minimum(x, 0.0) - jnp.log1p(jnp.exp(-jnp.abs(x)))


def _dot(a, b, dims):
    return lax.dot_general(a, b, (dims, ((), ())), preferred_element_type=F32)


NN = ((1,), (0,))
NT = ((1,), (1,))
TN = ((0,), (0,))


def _exact_mask_dot(mask_bf16, x):
    hi = x.astype(BF16)
    r1 = x - hi.astype(F32)
    mid = r1.astype(BF16)
    lo = (r1 - mid.astype(F32)).astype(BF16)
    return _dot(mask_bf16, hi, NN) + _dot(mask_bf16, mid, NN) + _dot(mask_bf16, lo, NN)


def _pick_tile(dim, target):
    if dim <= target:
        return dim
    best = None
    for t in range(LANES, int(1.4 * target) + 1, LANES):
        if dim % t == 0:
            best = t
    assert best is not None, (dim, target)
    return best


def _mm_call(a, b, *, name, grid, a_spec, b_spec, o_spec, out_shape, dims, acc_shape):
    nk = grid[2]

    def body(a_ref, b_ref, o_ref, *acc):
        part = _dot(a_ref[...], b_ref[...], dims)
        if nk == 1:
            o_ref[...] = part.astype(o_ref.dtype)
        else:
            acc_ref = acc[0]
            k = pl.program_id(2)

            @pl.when(k == 0)
            def _():
                acc_ref[...] = part

            @pl.when(k > 0)
            def _():
                acc_ref[...] += part

            @pl.when(k == nk - 1)
            def _():
                o_ref[...] = acc_ref[...].astype(o_ref.dtype)

    return pl.pallas_call(
        body, name=name, grid=grid, in_specs=[a_spec, b_spec], out_specs=o_spec, out_shape=out_shape,
        scratch_shapes=[] if nk == 1 else [pltpu.VMEM(acc_shape, F32)],
        compiler_params=_cp(("parallel", "parallel", "arbitrary"), VMEM_BIG),
    )(a, b)


def _mm(a, b, mode, out_dtype, *, name, tm=512, tn=1024, tk=1024):
    if mode == "nn":
        (M, K), (_, N) = a.shape, b.shape
    elif mode == "nt":
        (M, K), (N, _) = a.shape, b.shape
    else:
        (K, M), (_, N) = a.shape, b.shape
    tm, tn, tk = _pick_tile(M, tm), _pick_tile(N, tn), _pick_tile(K, tk)
    if mode == "nn":
        a_spec = pl.BlockSpec((tm, tk), lambda i, j, k: (i, k))
        b_spec = pl.BlockSpec((tk, tn), lambda i, j, k: (k, j))
        dims = NN
    elif mode == "nt":
        a_spec = pl.BlockSpec((tm, tk), lambda i, j, k: (i, k))
        b_spec = pl.BlockSpec((tn, tk), lambda i, j, k: (j, k))
        dims = NT
    else:
        a_spec = pl.BlockSpec((tk, tm), lambda i, j, k: (k, i))
        b_spec = pl.BlockSpec((tk, tn), lambda i, j, k: (k, j))
        dims = TN
    return _mm_call(a, b, name=name, grid=(M // tm, N // tn, K // tk), a_spec=a_spec, b_spec=b_spec,
                    o_spec=pl.BlockSpec((tm, tn), lambda i, j, k: (i, j)),
                    out_shape=jax.ShapeDtypeStruct((M, N), out_dtype), dims=dims, acc_shape=(tm, tn))


def _ffn_in_fwd(h, wfi, *, name, tm=512):
    S = h.shape[0]
    tm = _pick_tile(S, tm)
    return _mm_call(h, wfi, name=name, grid=(S // tm, N_DEV, 1),
                    a_spec=pl.BlockSpec((tm, D_MODEL), lambda i, j, k: (i, 0)),
                    b_spec=pl.BlockSpec((None, D_MODEL, FF_BLK), lambda i, j, k: (j, 0, 0)),
                    o_spec=pl.BlockSpec((None, tm, FF_BLK), lambda i, j, k: (j, i, 0)),
                    out_shape=jax.ShapeDtypeStruct((N_DEV, S, FF_BLK), BF16), dims=NN, acc_shape=None)


def _ffn_out_fwd(z, wfo, *, name, tm=512):
    S = z.shape[1]
    tm = _pick_tile(S, tm)
    return _mm_call(z, wfo, name=name, grid=(S // tm, 1, 4),
                    a_spec=pl.BlockSpec((None, tm, FF_BLK), lambda i, j, k: (k, i, 0)),
                    b_spec=pl.BlockSpec((None, FF_BLK, D_MODEL), lambda i, j, k: (k, 0, 0)),
                    o_spec=pl.BlockSpec((tm, D_MODEL), lambda i, j, k: (i, 0)),
                    out_shape=jax.ShapeDtypeStruct((S, D_MODEL), F32), dims=NN, acc_shape=(tm, D_MODEL))


def _ffn_out_bwd_x(dy, wfo, *, name, tm=512):
    S = dy.shape[0]
    tm = _pick_tile(S, tm)
    return _mm_call(dy, wfo, name=name, grid=(S // tm, 4, 1),
                    a_spec=pl.BlockSpec((tm, D_MODEL), lambda i, j, k: (i, 0)),
                    b_spec=pl.BlockSpec((None, FF_BLK, D_MODEL), lambda i, j, k: (j, 0, 0)),
                    o_spec=pl.BlockSpec((None, tm, FF_BLK), lambda i, j, k: (j, i, 0)),
                    out_shape=jax.ShapeDtypeStruct((4, S, FF_BLK), BF16), dims=NT, acc_shape=None)


def _ffn_out_bwd_w(z, dy, *, name, tk=1024):
    S = dy.shape[0]
    tk = _pick_tile(S, tk)
    return _mm_call(z, dy, name=name, grid=(4, 1, S // tk),
                    a_spec=pl.BlockSpec((None, tk, FF_BLK), lambda i, j, k: (i, k, 0)),
                    b_spec=pl.BlockSpec((tk, D_MODEL), lambda i, j, k: (k, 0)),
                    o_spec=pl.BlockSpec((None, FF_BLK, D_MODEL), lambda i, j, k: (i, 0, 0)),
                    out_shape=jax.ShapeDtypeStruct((4, FF_BLK, D_MODEL), BF16), dims=TN,
                    acc_shape=(FF_BLK, D_MODEL))


def _ffn_in_bwd_x(dgu, wfi, *, name, tm=512):
    S = dgu.shape[1]
    tm = _pick_tile(S, tm)
    return _mm_call(dgu, wfi, name=name, grid=(S // tm, 1, N_DEV),
                    a_spec=pl.BlockSpec((None, tm, FF_BLK), lambda i, j, k: (k, i, 0)),
                    b_spec=pl.BlockSpec((None, D_MODEL, FF_BLK), lambda i, j, k: (k, 0, 0)),
                    o_spec=pl.BlockSpec((tm, D_MODEL), lambda i, j, k: (i, 0)),
                    out_shape=jax.ShapeDtypeStruct((S, D_MODEL), F32), dims=NT, acc_shape=(tm, D_MODEL))


def _ffn_in_bwd_w(h, dgu, *, name, tm=512, tk=1024):
    S = h.shape[0]
    tk = _pick_tile(S, tk)
    return _mm_call(h, dgu, name=name, grid=(D_MODEL // tm, N_DEV, S // tk),
                    a_spec=pl.BlockSpec((tk, tm), lambda i, j, k: (k, i)),
                    b_spec=pl.BlockSpec((None, tk, FF_BLK), lambda i, j, k: (j, k, 0)),
                    o_spec=pl.BlockSpec((None, tm, FF_BLK), lambda i, j, k: (j, i, 0)),
                    out_shape=jax.ShapeDtypeStruct((N_DEV, D_MODEL, FF_BLK), BF16), dims=TN,
                    acc_shape=(tm, FF_BLK))


def _row_spec(T, W, col_block=0):
    return pl.BlockSpec((T, W), lambda i: (i, col_block))


def _vec_spec(W):
    return pl.BlockSpec((1, W), lambda i: (0, 0))


def _rms_stats(x):
    r = lax.rsqrt(jnp.mean(x * x, axis=-1, keepdims=True) + EPS)
    return r, x * r


def _rms_bwd(xh, r, g, dy):
    dxh = dy * g
    dx = r * (dxh - xh * jnp.mean(dxh * xh, axis=-1, keepdims=True))
    dg = jnp.sum(dy * xh, axis=0, keepdims=True)
    return dx, dg


def _accum(ref, val):
    @pl.when(pl.program_id(0) == 0)
    def _():
        ref[...] = val

    @pl.when(pl.program_id(0) > 0)
    def _():
        ref[...] += val


def _rms_fwd_call(x, g, *, name, T=256):
    S, W = x.shape

    def body(x_ref, g_ref, o_ref):
        _, xh = _rms_stats(x_ref[...])
        o_ref[...] = (xh * g_ref[...]).astype(o_ref.dtype)

    return pl.pallas_call(
        body, name=name, grid=(S // T,),
        in_specs=[_row_spec(T, W), _vec_spec(W)],
        out_specs=_row_spec(T, W),
        out_shape=jax.ShapeDtypeStruct((S, W), BF16),
        compiler_params=_cp(("parallel",)),
    )(x, g)


def _merge_fwd_call(t1, t2, pg, *, name, T=256):
    S = t1.shape[0]

    def body(t1_ref, t2_ref, gg_ref, gs_ref, o_ref):
        sg = _sig(gg_ref[...].astype(F32))
        ss = _sig(gs_ref[...].astype(F32))
        o_ref[...] = (sg * t1_ref[...].astype(F32) + ss * t2_ref[...].astype(F32)).astype(o_ref.dtype)

    return pl.pallas_call(
        body, name=name, grid=(S // T,),
        in_specs=[_row_spec(T, D_MODEL), _row_spec(T, D_MODEL), _row_spec(T, D_MODEL, 0), _row_spec(T, D_MODEL, 1)],
        out_specs=_row_spec(T, D_MODEL),
        out_shape=jax.ShapeDtypeStruct((S, D_MODEL), BF16),
        compiler_params=_cp(("parallel",)),
    )(t1, t2, pg, pg)


def _merge_bwd_call(dm, t1, t2, pg, *, name, T=256):
    S = t1.shape[0]

    def body(dm_ref, t1_ref, t2_ref, gg_ref, gs_ref, dt1_ref, dt2_ref, dg_ref):
        dmv = dm_ref[...].astype(F32)
        sg = _sig(gg_ref[...].astype(F32))
        ss = _sig(gs_ref[...].astype(F32))
        dt1_ref[...] = (dmv * sg).astype(BF16)
        dt2_ref[...] = (dmv * ss).astype(BF16)
        dg_ref[:, :D_MODEL] = (dmv * t1_ref[...].astype(F32) * sg * (1.0 - sg)).astype(BF16)
        dg_ref[:, D_MODEL:] = (dmv * t2_ref[...].astype(F32) * ss * (1.0 - ss)).astype(BF16)

    out = jax.ShapeDtypeStruct((S, D_MODEL), BF16)
    return pl.pallas_call(
        body, name=name, grid=(S // T,),
        in_specs=[_row_spec(T, D_MODEL)] * 3 + [_row_spec(T, D_MODEL, 0), _row_spec(T, D_MODEL, 1)],
        out_specs=[_row_spec(T, D_MODEL), _row_spec(T, D_MODEL), _row_spec(T, 2 * D_MODEL)],
        out_shape=[out, out, jax.ShapeDtypeStruct((S, 2 * D_MODEL), BF16)],
        compiler_params=_cp(("parallel",)),
    )(dm, t1, t2, pg, pg)


def _post_mix_call(x, mix, g2, g3, *, name, T=256):
    S = x.shape[0]

    def body(x_ref, mix_ref, g2_ref, g3_ref, x1_ref, h_ref):
        _, mh = _rms_stats(mix_ref[...])
        x1 = x_ref[...] + mh * g2_ref[...]
        x1_ref[...] = x1
        _, xh = _rms_stats(x1)
        h_ref[...] = (xh * g3_ref[...]).astype(BF16)

    return pl.pallas_call(
        body, name=name, grid=(S // T,),
        in_specs=[_row_spec(T, D_MODEL)] * 2 + [_vec_spec(D_MODEL)] * 2,
        out_specs=[_row_spec(T, D_MODEL)] * 2,
        out_shape=[jax.ShapeDtypeStruct((S, D_MODEL), F32), jax.ShapeDtypeStruct((S, D_MODEL), BF16)],
        compiler_params=_cp(("parallel",)),
    )(x, mix, g2, g3)


def _swiglu_fwd_call(gu, *, name, T=256):
    S = gu.shape[1]

    def body(gate_ref, up_ref, z_ref):
        gt = gate_ref[...].astype(F32)
        z_ref[...] = (gt * _sig(gt) * up_ref[...].astype(F32)).astype(BF16)

    return pl.pallas_call(
        body, name=name, grid=(4, S // T),
        in_specs=[pl.BlockSpec((None, T, FF_BLK), lambda d, i: (d, i, 0)),
                  pl.BlockSpec((None, T, FF_BLK), lambda d, i: (d + 4, i, 0))],
        out_specs=pl.BlockSpec((None, T, FF_BLK), lambda d, i: (d, i, 0)),
        out_shape=jax.ShapeDtypeStruct((4, S, FF_BLK), BF16),
        compiler_params=_cp(("parallel", "parallel")),
    )(gu, gu)


def _swiglu_bwd_call(dz, gu, *, name, T=256):
    S = gu.shape[1]

    def body(dz_ref, gu_ref, o_ref):
        dzv = dz_ref[...].astype(F32)
        gt = gu_ref[0].astype(F32)
        up = gu_ref[1].astype(F32)
        s = _sig(gt)
        o_ref[0] = (dzv * up * s * (1.0 + gt * (1.0 - s))).astype(BF16)
        o_ref[1] = (dzv * gt * s).astype(BF16)

    pair = pl.BlockSpec((2, None, T, FF_BLK), lambda d, i: (0, d, i, 0))
    out = pl.pallas_call(
        body, name=name, grid=(4, S // T),
        in_specs=[pl.BlockSpec((None, T, FF_BLK), lambda d, i: (d, i, 0)), pair],
        out_specs=pair,
        out_shape=jax.ShapeDtypeStruct((2, 4, S, FF_BLK), BF16),
        compiler_params=_cp(("parallel", "parallel")),
    )(dz, gu.reshape(2, 4, S, FF_BLK))
    return out.reshape(N_DEV, S, FF_BLK)


def _loss_call(x1, y, target, g4, *, name, T=256):
    S = x1.shape[0]

    def body(x1_ref, y_ref, t_ref, g4_ref, loss_ref, dx2_ref, dy_ref, dg4_ref):
        r, yh = _rms_stats(y_ref[...])
        diff = x1_ref[...] + yh * g4_ref[...] - t_ref[...]
        part = 0.5 * jnp.sum(jnp.mean(diff * diff, axis=-1, keepdims=True), axis=0, keepdims=True)
        _accum(loss_ref, jnp.broadcast_to(part, (1, LANES)))
        dx2 = diff * (1.0 / D_MODEL)
        dx2_ref[...] = dx2
        dy, dg = _rms_bwd(yh, r, g4_ref[...], dx2)
        dy_ref[...] = dy.astype(BF16)
        _accum(dg4_ref, dg)

    return pl.pallas_call(
        body, name=name, grid=(S // T,),
        in_specs=[_row_spec(T, D_MODEL)] * 3 + [_vec_spec(D_MODEL)],
        out_specs=[_vec_spec(LANES), _row_spec(T, D_MODEL), _row_spec(T, D_MODEL), _vec_spec(D_MODEL)],
        out_shape=[jax.ShapeDtypeStruct((1, LANES), F32), jax.ShapeDtypeStruct((S, D_MODEL), F32),
                   jax.ShapeDtypeStruct((S, D_MODEL), BF16), jax.ShapeDtypeStruct((1, D_MODEL), F32)],
        compiler_params=_cp(("arbitrary",)),
    )(x1, y, target, g4)


def _mid_bwd_call(dx2, dh, x1, mix, g3, g2, *, name, T=256):
    S = x1.shape[0]

    def body(dx2_ref, dh_ref, x1_ref, mix_ref, g3_ref, g2_ref, dx1_ref, dmix_ref, dg3_ref, dg2_ref):
        r3, xh = _rms_stats(x1_ref[...])
        d3, dg3 = _rms_bwd(xh, r3, g3_ref[...], dh_ref[...])
        dx1 = dx2_ref[...] + d3
        dx1_ref[...] = dx1
        r2, mh = _rms_stats(mix_ref[...])
        dmix, dg2 = _rms_bwd(mh, r2, g2_ref[...], dx1)
        dmix_ref[...] = dmix.astype(BF16)
        _accum(dg3_ref, dg3)
        _accum(dg2_ref, dg2)

    return pl.pallas_call(
        body, name=name, grid=(S // T,),
        in_specs=[_row_spec(T, D_MODEL)] * 4 + [_vec_spec(D_MODEL)] * 2,
        out_specs=[_row_spec(T, D_MODEL), _row_spec(T, D_MODEL), _vec_spec(D_MODEL), _vec_spec(D_MODEL)],
        out_shape=[jax.ShapeDtypeStruct((S, D_MODEL), F32), jax.ShapeDtypeStruct((S, D_MODEL), BF16),
                   jax.ShapeDtypeStruct((1, D_MODEL), F32), jax.ShapeDtypeStruct((1, D_MODEL), F32)],
        compiler_params=_cp(("arbitrary",)),
    )(dx2, dh, x1, mix, g3, g2)


def _rms1_bwd_call(dx1, da_parts, x, g1, *, name, T=256):
    S = x.shape[0]
    n = len(da_parts)

    def body(dx1_ref, *rest):
        da_refs, (x_ref, g1_ref, gx_ref, dg1_ref) = rest[:n], rest[n:]
        da = da_refs[0][...]
        for ref in da_refs[1:]:
            da = da + ref[...]
        r, xh = _rms_stats(x_ref[...])
        dxa, dg = _rms_bwd(xh, r, g1_ref[...], da)
        gx_ref[...] = dx1_ref[...] + dxa
        _accum(dg1_ref, dg)

    return pl.pallas_call(
        body, name=name, grid=(S // T,),
        in_specs=[_row_spec(T, D_MODEL)] * (n + 2) + [_vec_spec(D_MODEL)],
        out_specs=[_row_spec(T, D_MODEL), _vec_spec(D_MODEL)],
        out_shape=[jax.ShapeDtypeStruct((S, D_MODEL), F32), jax.ShapeDtypeStruct((1, D_MODEL), F32)],
        compiler_params=_cp(("arbitrary",)),
    )(dx1, *da_parts, x, g1)


GLA_TILE = 256
Q_OFF, K_OFF, V_OFF, R_OFF = 0, QK, 2 * QK, 2 * QK + GV


def _tri(lower):
    r = lax.broadcasted_iota(jnp.int32, (CHUNK, CHUNK), 0)
    c = lax.broadcasted_iota(jnp.int32, (CHUNK, CHUNK), 1)
    return jnp.where((r >= c) if lower else (c >= r), 1.0, 0.0).astype(BF16)


def _gla_fwd_call(pa, alow, wg, bgate, gnorm, *, name):
    S = pa.shape[0]
    Tg = min(GLA_TILE, S)
    cb = Tg // CHUNK

    def body(pa_ref, al_ref, wg_ref, bg_ref, gn_ref, y_ref, st_ref, state):
        @pl.when(pl.program_id(0) == 0)
        def _():
            state[...] = jnp.zeros_like(state)

        logit = _dot(al_ref[...], wg_ref[...], NN) + bg_ref[...]
        ls = _logsig(logit) * (1.0 / 16.0)
        tri = _tri(True)
        for h in range(HEADS):
            for c in range(cb):
                rows = pl.ds(c * CHUNK, CHUNK)
                cum = _exact_mask_dot(tri, ls[c * CHUNK:(c + 1) * CHUNK, h * DK:(h + 1) * DK])
                tot = cum[CHUNK - 1:CHUNK]
                kd = (pa_ref[rows, pl.ds(K_OFF + h * DK, DK)].astype(F32) * jnp.exp(tot - cum)).astype(BF16)
                new = state[h] * jnp.exp(tot) + _dot(pa_ref[rows, pl.ds(V_OFF + h * DV, DV)], kd, TN)
                state[h] = new
                st_ref[c, h] = new
                qs = (pa_ref[rows, pl.ds(Q_OFF + h * DK, DK)].astype(F32) * Q_SCALE).astype(BF16)
                o = _dot(qs, new.astype(BF16), NT)
                rs = lax.rsqrt(jnp.mean(o * o, axis=-1, keepdims=True) + EPS)
                rr = pa_ref[rows, pl.ds(R_OFF + h * DV, DV)].astype(F32)
                y_ref[rows, pl.ds(h * DV, DV)] = (o * rs * gn_ref[h] * (rr * _sig(rr))).astype(BF16)

    return pl.pallas_call(
        body, name=name, grid=(S // Tg,),
        in_specs=[
            pl.BlockSpec((Tg, 2 * QK + 2 * GV), lambda t: (t, 0)),
            pl.BlockSpec((Tg, LANES), lambda t: (t, 0)),
            pl.BlockSpec((LANES, QK), lambda t: (0, 0)),
            pl.BlockSpec((1, QK), lambda t: (0, 0)),
            pl.BlockSpec((HEADS, 1, DV), lambda t: (0, 0, 0)),
        ],
        out_specs=[
            pl.BlockSpec((Tg, GV), lambda t: (t, 0)),
            pl.BlockSpec((cb, HEADS, DV, DK), lambda t: (t, 0, 0, 0)),
        ],
        out_shape=[jax.ShapeDtypeStruct((S, GV), BF16),
                   jax.ShapeDtypeStruct((S // CHUNK, HEADS, DV, DK), F32)],
        scratch_shapes=[pltpu.VMEM((HEADS, DV, DK), F32)],
        compiler_params=_cp(("arbitrary",), VMEM_BIG),
    )(pa, alow, wg, bgate, gnorm)


def _gla_bwd_call(pa, alow, wg, bgate, gnorm, states, dy, *, name):
    S = pa.shape[0]
    Tg = min(GLA_TILE, S)
    cb = Tg // CHUNK
    nt = S // Tg

    def rev(t):
        return nt - 1 - t

    def body(pa_ref, al_ref, wg_ref, bg_ref, gn_ref, st_ref, prev_ref, dy_ref,
             dpa_ref, dl_ref, dgn_ref, dbg_ref, carry):
        t = pl.program_id(0)

        @pl.when(t == 0)
        def _():
            carry[...] = jnp.zeros_like(carry)
            dgn_ref[...] = jnp.zeros_like(dgn_ref)
            dbg_ref[...] = jnp.zeros_like(dbg_ref)

        logit = _dot(al_ref[...], wg_ref[...], NN) + bg_ref[...]
        ls = _logsig(logit) * (1.0 / 16.0)
        sneg = _sig(-logit)
        tri = _tri(True)
        upper = _tri(False)
        first_tile = rev(t) == 0
        for h in range(HEADS):
            gn = gn_ref[h]
            hk = pl.ds(h * DK, DK)
            dbg = jnp.zeros((1, DK), F32)
            dgn = jnp.zeros((1, DV), F32)
            for c in reversed(range(cb)):
                rows = pl.ds(c * CHUNK, CHUNK)
                cum = _exact_mask_dot(tri, ls[c * CHUNK:(c + 1) * CHUNK, h * DK:(h + 1) * DK])
                tot = cum[CHUNK - 1:CHUNK]
                w = jnp.exp(tot - cum)
                decay = jnp.exp(tot)
                kd = pa_ref[rows, pl.ds(K_OFF + h * DK, DK)].astype(F32) * w
                kd16 = kd.astype(BF16)
                st = st_ref[c, h]
                if c > 0:
                    st_prev = st_ref[c - 1, h]
                else:
                    st_prev = jnp.where(first_tile, 0.0, prev_ref[0, h])
                qs = (pa_ref[rows, pl.ds(Q_OFF + h * DK, DK)].astype(F32) * Q_SCALE).astype(BF16)
                st16 = st.astype(BF16)
                o = _dot(qs, st16, NT)
                rs = lax.rsqrt(jnp.mean(o * o, axis=-1, keepdims=True) + EPS)
                oh = o * rs
                rr = pa_ref[rows, pl.ds(R_OFF + h * DV, DV)].astype(F32)
                sr = _sig(rr)
                dyv = dy_ref[rows, pl.ds(h * DV, DV)].astype(F32)
                dpa_ref[rows, pl.ds(R_OFF + h * DV, DV)] = (
                    dyv * oh * gn * sr * (1.0 + rr * (1.0 - sr))).astype(BF16)
                don = dyv * (rr * sr)
                dgn = dgn + jnp.sum(don * oh, axis=0, keepdims=True)
                doh = don * gn
                do16 = (rs * (doh - oh * jnp.mean(doh * oh, axis=-1, keepdims=True))).astype(BF16)
                dpa_ref[rows, pl.ds(Q_OFF + h * DK, DK)] = (_dot(do16, st16, NN) * Q_SCALE).astype(BF16)
                gt = _dot(do16, qs, TN) + carry[h]
                gt16 = gt.astype(BF16)
                dkd = _dot(pa_ref[rows, pl.ds(V_OFF + h * DV, DV)], gt16, NN)
                dpa_ref[rows, pl.ds(V_OFF + h * DV, DV)] = _dot(kd16, gt16, NT).astype(BF16)
                ddecay = jnp.sum(gt * st_prev, axis=0, keepdims=True)
                dpa_ref[rows, pl.ds(K_OFF + h * DK, DK)] = (dkd * w).astype(BF16)
                e = dkd * kd
                dtot = jnp.sum(e, axis=0, keepdims=True) + ddecay * decay
                dls = dtot - _exact_mask_dot(upper, e)
                dlogit = dls * (1.0 / 16.0) * sneg[c * CHUNK:(c + 1) * CHUNK, h * DK:(h + 1) * DK]
                dl_ref[rows, hk] = dlogit.astype(BF16)
                dbg = dbg + jnp.sum(dlogit, axis=0, keepdims=True)
                carry[h] = gt * decay
            dgn_ref[h] += dgn
            dbg_ref[:, hk] += dbg

    return pl.pallas_call(
        body, name=name, grid=(nt,),
        in_specs=[
            pl.BlockSpec((Tg, 2 * QK + 2 * GV), lambda t: (rev(t), 0)),
            pl.BlockSpec((Tg, LANES), lambda t: (rev(t), 0)),
            pl.BlockSpec((LANES, QK), lambda t: (0, 0)),
            pl.BlockSpec((1, QK), lambda t: (0, 0)),
            pl.BlockSpec((HEADS, 1, DV), lambda t: (0, 0, 0)),
            pl.BlockSpec((cb, HEADS, DV, DK), lambda t: (rev(t), 0, 0, 0)),
            pl.BlockSpec((1, HEADS, DV, DK), lambda t: (jnp.maximum(rev(t) * cb - 1, 0), 0, 0, 0)),
            pl.BlockSpec((Tg, GV), lambda t: (rev(t), 0)),
        ],
        out_specs=[
            pl.BlockSpec((Tg, 2 * QK + 2 * GV), lambda t: (rev(t), 0)),
            pl.BlockSpec((Tg, QK), lambda t: (rev(t), 0)),
            pl.BlockSpec((HEADS, 1, DV), lambda t: (0, 0, 0)),
            pl.BlockSpec((1, QK), lambda t: (0, 0)),
        ],
        out_shape=[jax.ShapeDtypeStruct((S, 2 * QK + 2 * GV), BF16), jax.ShapeDtypeStruct((S, QK), BF16),
                   jax.ShapeDtypeStruct((HEADS, 1, DV), F32), jax.ShapeDtypeStruct((1, QK), F32)],
        scratch_shapes=[pltpu.VMEM((HEADS, DV, DK), F32)],
        compiler_params=_cp(("arbitrary",), VMEM_BIG),
    )(pa, alow, wg, bgate, gnorm, states, states, dy)


SGU_TILE = 256


def _sgu_mask():
    r = lax.broadcasted_iota(jnp.int32, (SBLOCK, SBLOCK), 0)
    c = lax.broadcasted_iota(jnp.int32, (SBLOCK, SBLOCK), 1)
    return (c < CHUNK) | (r >= CHUNK)


def _ln_stats(vf):
    mu = jnp.mean(vf, axis=-1, keepdims=True)
    xc = vf - mu
    rs = lax.rsqrt(jnp.mean(xc * xc, axis=-1, keepdims=True) + EPS)
    return rs, xc * rs


def _sgu_fwd_call(ps, ln_g, ln_b, w_sp, b_sp, *, name):
    S = ps.shape[0]
    Ts = min(SGU_TILE, S)

    def body(ps_ref, lg_ref, lb_ref, w_ref, b_ref, y_ref):
        mask = _sgu_mask()
        for g in range(GROUPS):
            wm = jnp.where(mask, w_ref[g], 0.0).astype(BF16)
            for p in range(Ts // SBLOCK):
                rows = pl.ds(p * SBLOCK, SBLOCK)
                u = _gelu(ps_ref[rows, pl.ds(g * DG, DG)].astype(F32))
                _, xh = _ln_stats(_gelu(ps_ref[rows, pl.ds(D_MODEL + g * DG, DG)].astype(F32)))
                vn = xh * lg_ref[g] + lb_ref[g]
                mixed = _dot(wm, vn.astype(BF16), NN) + b_ref[g]
                y_ref[rows, pl.ds(g * DG, DG)] = (u * mixed).astype(BF16)

    full3 = lambda a, b, c: pl.BlockSpec((a, b, c), lambda t: (0, 0, 0))
    return pl.pallas_call(
        body, name=name, grid=(S // Ts,),
        in_specs=[pl.BlockSpec((Ts, 2 * D_MODEL), lambda t: (t, 0)),
                  full3(GROUPS, 1, DG), full3(GROUPS, 1, DG), full3(GROUPS, SBLOCK, SBLOCK), full3(GROUPS, SBLOCK, 1)],
        out_specs=pl.BlockSpec((Ts, D_MODEL), lambda t: (t, 0)),
        out_shape=jax.ShapeDtypeStruct((S, D_MODEL), BF16),
        compiler_params=_cp(("parallel",)),
    )(ps, ln_g, ln_b, w_sp, b_sp)


def _sgu_bwd_call(ps, ln_g, ln_b, w_sp, b_sp, dy, *, name):
    S = ps.shape[0]
    Ts = min(SGU_TILE, S)

    def body(ps_ref, lg_ref, lb_ref, w_ref, b_ref, dy_ref, ds_ref, dlg_ref, dlb_ref, dw_ref, db_ref):
        @pl.when(pl.program_id(0) == 0)
        def _():
            dlg_ref[...] = jnp.zeros_like(dlg_ref)
            dlb_ref[...] = jnp.zeros_like(dlb_ref)
            dw_ref[...] = jnp.zeros_like(dw_ref)
            db_ref[...] = jnp.zeros_like(db_ref)

        mask = _sgu_mask()
        for g in range(GROUPS):
            wm = jnp.where(mask, w_ref[g], 0.0).astype(BF16)
            lg = lg_ref[g]
            for p in range(Ts // SBLOCK):
                rows = pl.ds(p * SBLOCK, SBLOCK)
                su = ps_ref[rows, pl.ds(g * DG, DG)].astype(F32)
                sv = ps_ref[rows, pl.ds(D_MODEL + g * DG, DG)].astype(F32)
                u = _gelu(su)
                rs, xh = _ln_stats(_gelu(sv))
                vn16 = (xh * lg + lb_ref[g]).astype(BF16)
                mixed = _dot(wm, vn16, NN) + b_ref[g]
                dyv = dy_ref[rows, pl.ds(g * DG, DG)].astype(F32)
                ds_ref[rows, pl.ds(g * DG, DG)] = (dyv * mixed * _gelu_grad(su)).astype(BF16)
                dmix = dyv * u
                dmix16 = dmix.astype(BF16)
                db_ref[g] += jnp.sum(dmix, axis=-1, keepdims=True)
                dw_ref[g] += jnp.where(mask, _dot(dmix16, vn16, NT), 0.0)
                dvn = _dot(wm, dmix16, TN)
                dlg_ref[g] += jnp.sum(dvn * xh, axis=0, keepdims=True)
                dlb_ref[g] += jnp.sum(dvn, axis=0, keepdims=True)
                dxh = dvn * lg
                dvf = rs * (dxh - jnp.mean(dxh, axis=-1, keepdims=True)
                            - xh * jnp.mean(dxh * xh, axis=-1, keepdims=True))
                ds_ref[rows, pl.ds(D_MODEL + g * DG, DG)] = (dvf * _gelu_grad(sv)).astype(BF16)

    full3 = lambda a, b, c: pl.BlockSpec((a, b, c), lambda t: (0, 0, 0))
    return pl.pallas_call(
        body, name=name, grid=(S // Ts,),
        in_specs=[pl.BlockSpec((Ts, 2 * D_MODEL), lambda t: (t, 0)),
                  full3(GROUPS, 1, DG), full3(GROUPS, 1, DG), full3(GROUPS, SBLOCK, SBLOCK), full3(GROUPS, SBLOCK, 1),
                  pl.BlockSpec((Ts, D_MODEL), lambda t: (t, 0))],
        out_specs=[pl.BlockSpec((Ts, 2 * D_MODEL), lambda t: (t, 0)),
                   full3(GROUPS, 1, DG), full3(GROUPS, 1, DG), full3(GROUPS, SBLOCK, SBLOCK), full3(GROUPS, SBLOCK, 1)],
        out_shape=[jax.ShapeDtypeStruct((S, 2 * D_MODEL), BF16),
                   jax.ShapeDtypeStruct((GROUPS, 1, DG), F32), jax.ShapeDtypeStruct((GROUPS, 1, DG), F32),
                   jax.ShapeDtypeStruct((GROUPS, SBLOCK, SBLOCK), F32),
                   jax.ShapeDtypeStruct((GROUPS, SBLOCK, 1), F32)],
        compiler_params=_cp(("arbitrary",)),
    )(ps, ln_g, ln_b, w_sp, b_sp, dy)


def _position():
    return lax.axis_index("x"), lax.axis_index("y"), lax.axis_index("c")


def _gather_copies(srcs, dsts, send_sems, recv_sems, local_sems):
    x, y, c = _position()
    me, sibling = (x, y, c), (x, y, 1 - c)
    chips = [(1 - x, y), (x, 1 - y), (1 - x, 1 - y)]
    n = len(srcs)

    def slab(a, block):
        px, py, pc = block
        return dsts[a].at[4 * px + 2 * py + pc]

    def copy(a, k, block, to, src=None):
        return pltpu.make_async_remote_copy(
            src_ref=slab(a, block) if src is None else src, dst_ref=slab(a, block),
            send_sem=send_sems.at[7 * a + k], recv_sem=recv_sems.at[7 * a + k], device_id=to, device_id_type=MESH)

    mine = [pltpu.make_async_copy(srcs[a], slab(a, me), local_sems.at[a]) for a in range(n)]
    for cp in mine:
        cp.start()
    first = []
    for a in range(n):
        first.append(copy(a, 0, me, sibling, src=srcs[a]))
        first += [copy(a, 1 + j, me, (*chip, c), src=srcs[a]) for j, chip in enumerate(chips)]
    for cp in first:
        cp.start()
    passed = []
    for j, chip in enumerate(chips):
        for a in range(n):
            copy(a, 1 + j, (*chip, c), me).wait_recv()
            cp = copy(a, 4 + j, (*chip, c), sibling)
            cp.start()
            passed.append(cp)
    for a in range(n):
        copy(a, 0, sibling, me).wait_recv()
        for j, chip in enumerate(chips):
            copy(a, 4 + j, (*chip, 1 - c), me).wait_recv()
    for cp in first + passed:
        cp.wait_send()
    for cp in mine:
        cp.wait()


def _all_gather_hbm(shards, *, name):
    n = len(shards)

    def body(*refs):
        srcs, dsts = refs[:n], refs[n:2 * n]
        send_sems, recv_sems, local_sems = refs[2 * n:]
        _gather_copies(srcs, dsts, send_sems, recv_sems, local_sems)

    return pl.pallas_call(
        body, name=name,
        in_specs=[ANY] * n, out_specs=[ANY] * n,
        out_shape=[jax.ShapeDtypeStruct((N_DEV, *s.shape), s.dtype) for s in shards],
        scratch_shapes=_comm_sems(n),
    )(*shards)


def _all_reduce_small(part, *, name):
    R, W = part.shape

    def body(x_ref, out_ref, gathered, send_sems, recv_sems, local_sems):
        _gather_copies([x_ref], [gathered], send_sems, recv_sems, local_sems)
        acc = gathered[0]
        for d in range(1, N_DEV):
            acc = acc + gathered[d]
        out_ref[...] = acc

    return pl.pallas_call(
        body, name=name,
        in_specs=[VMEM_SPEC], out_specs=VMEM_SPEC,
        out_shape=jax.ShapeDtypeStruct((R, W), F32),
        scratch_shapes=[pltpu.VMEM((N_DEV, R, W), F32),
                        pltpu.SemaphoreType.DMA((7,)), pltpu.SemaphoreType.DMA((7,)), pltpu.SemaphoreType.DMA((1,))],
    )(part)


FLIPS = [(fx, fy, fc) for fx in (0, 1) for fy in (0, 1) for fc in (0, 1)][1:]


def _scatter_copies(srcs, dsts, send_sems, recv_sems, local_sems):
    n = len(srcs)
    x, y, c = _position()
    me = 4 * x + 2 * y + c
    mine = [pltpu.make_async_copy(srcs[a].at[me], dsts[a].at[me], local_sems.at[a]) for a in range(n)]
    for cp in mine:
        cp.start()
    copies = []
    for k, (fx, fy, fc) in enumerate(FLIPS):
        tx = 1 - x if fx else x
        ty = 1 - y if fy else y
        tc = 1 - c if fc else c
        peer = 4 * tx + 2 * ty + tc
        for a in range(n):
            cp = pltpu.make_async_remote_copy(
                src_ref=srcs[a].at[peer], dst_ref=dsts[a].at[me],
                send_sem=send_sems.at[7 * a + k], recv_sem=recv_sems.at[7 * a + k],
                device_id=(tx, ty, tc), device_id_type=MESH)
            cp.start()
            copies.append(cp)
    for cp in copies:
        cp.wait()
    for cp in mine:
        cp.wait()


def _comm_sems(n):
    return [pltpu.SemaphoreType.DMA((7 * n,)), pltpu.SemaphoreType.DMA((7 * n,)), pltpu.SemaphoreType.DMA((n,))]


def _scatter_blocks(parts, *, name):
    n = len(parts)

    def body(*refs):
        _scatter_copies(refs[:n], refs[n:2 * n], *refs[2 * n:])

    return pl.pallas_call(
        body, name=name,
        in_specs=[ANY] * n, out_specs=[ANY] * n,
        out_shape=[jax.ShapeDtypeStruct(p.shape, p.dtype) for p in parts],
        scratch_shapes=_comm_sems(n),
    )(*parts)


def _handshake(peers):
    barrier = pltpu.get_barrier_semaphore()
    for peer in peers:
        pl.semaphore_signal(barrier, inc=1, device_id=peer, device_id_type=MESH)
    pl.semaphore_wait(barrier, len(peers))


def _sequencer_call(arrays, out_types, copies_fn, peers_fn, *, name, collective_id):
    n = len(arrays)
    srcs = [jax.new_ref(a, memory_space=pltpu.MemorySpace.HBM) for a in arrays]
    dsts = [jax.empty_ref(t, memory_space=pltpu.MemorySpace.HBM) for t in out_types]

    @pl.kernel(mesh=plsc.ScalarSubcoreMesh(axis_name="sequencer", num_cores=1), name=name,
               scratch_types=_comm_sems(n), compiler_params=pltpu.CompilerParams(collective_id=collective_id))
    def launch(send_sems, recv_sems, local_sems):
        _handshake(peers_fn())
        copies_fn(srcs, dsts, send_sems, recv_sems, local_sems)

    launch()
    return [d[...] for d in dsts]


def _all_other_devices():
    x, y, c = _position()
    return [(1 - x if fx else x, 1 - y if fy else y, 1 - c if fc else c) for fx, fy, fc in FLIPS]


def _gather_peers():
    x, y, c = _position()
    return [(x, y, 1 - c), (1 - x, y, c), (x, 1 - y, c), (1 - x, 1 - y, c)]


def _scatter_blocks_async(parts, *, name, collective_id):
    return _sequencer_call(parts, [jax.ShapeDtypeStruct(p.shape, p.dtype) for p in parts],
                           _scatter_copies, _all_other_devices, name=name, collective_id=collective_id)


def _all_gather_async(shards, *, name, collective_id):
    return _sequencer_call(shards, [jax.ShapeDtypeStruct((N_DEV, *s.shape), s.dtype) for s in shards],
                           _gather_copies, _gather_peers, name=name, collective_id=collective_id)


def _adamw_math(w, g, m, v):
    m = ADAM_B1 * m + (1.0 - ADAM_B1) * g
    v = ADAM_B2 * v + (1.0 - ADAM_B2) * (g * g)
    m_hat = m / (1.0 - ADAM_B1 ** ADAM_STEP)
    v_hat = v / (1.0 - ADAM_B2 ** ADAM_STEP)
    delta = -ADAM_LR * (m_hat / (jnp.sqrt(v_hat) + ADAM_EPS) + ADAM_WD * w)
    return delta, m, v


def _adamw_reduce_call(recv, w, m, v, *, name, T=128):
    R, W = w.shape
    T = T if R % T == 0 else R // 2
    assert R % T == 0 and T % 16 == 0, (R, T)

    def body(p_ref, w_ref, m_ref, v_ref, g_out, d_out, m_out, v_out):
        g = p_ref[0].astype(F32)
        for d in range(1, N_DEV):
            g = g + p_ref[d].astype(F32)
        g_out[...] = g
        d_out[...], m_out[...], v_out[...] = _adamw_math(w_ref[...], g, m_ref[...], v_ref[...])

    row = pl.BlockSpec((T, W), lambda i: (i, 0))
    out = jax.ShapeDtypeStruct((R, W), F32)
    return pl.pallas_call(
        body, name=name, grid=(R // T,),
        in_specs=[pl.BlockSpec((N_DEV, T, W), lambda i: (0, i, 0)), row, row, row],
        out_specs=[row] * 4, out_shape=[out] * 4,
        compiler_params=_cp(("parallel",), VMEM_BIG),
    )(recv, w, m, v)


def _adamw_small_call(w, g, m, v, *, name):
    def body(w_ref, g_ref, m_ref, v_ref, d_out, m_out, v_out):
        d_out[...], m_out[...], v_out[...] = _adamw_math(w_ref[...], g_ref[...], m_ref[...], v_ref[...])

    out = jax.ShapeDtypeStruct(w.shape, F32)
    return pl.pallas_call(body, name=name, in_specs=[VMEM_SPEC] * 4, out_specs=[VMEM_SPEC] * 3,
                          out_shape=[out] * 3)(w, g, m, v)


def _tile_rows(n_elems):
    return -(-n_elems // (8 * LANES)) * 8


def _pack_rows(parts, rows):
    pieces = []
    for p in parts:
        q = p.reshape(-1, LANES)
        pieces.append(jnp.pad(q, ((0, _tile_rows(p.size) - q.shape[0]), (0, 0))))
    buf = jnp.concatenate(pieces, axis=0)
    return jnp.pad(buf, ((0, rows - buf.shape[0]), (0, 0)))


def _unpack_rows(buf, shapes):
    out, off = [], 0
    for shp in shapes:
        n = 1
        for s in shp:
            n *= s
        out.append(buf[off:off + n // LANES].reshape(shp))
        off += _tile_rows(n)
    return out


def _in_columns(blocks, lo, hi):
    pieces = []
    for d in range(N_DEV):
        s, e = max(lo, IN_BLK * d), min(hi, IN_BLK * (d + 1))
        if s < e:
            pieces.append(blocks[d][:, s - IN_BLK * d:e - IN_BLK * d])
    return jnp.concatenate(pieces, axis=1)


def _in_blocks(segments):
    blocks = []
    for d in range(N_DEV):
        lo, hi = IN_BLK * d, IN_BLK * (d + 1)
        pieces = []
        for arr, c0 in segments:
            s, e = max(lo, c0), min(hi, c0 + arr.shape[1])
            if s < e:
                pieces.append(arr[:, s - c0:e - c0])
        blocks.append(jnp.concatenate(pieces, axis=1))
    return jnp.stack(blocks)


def _local_step(x, target, W, scatter):
    g1, g2, g3, g4 = [W[n].reshape(1, D_MODEL) for n in ("norm_pre_mix", "norm_post_mix", "norm_pre_ffn", "norm_post_ffn")]
    w_a, w_low, w_s, w_g = W["w_in_a"], W["w_in_low"], W["w_in_s"], W["w_in_g"]
    wfi, wfo = W["w_ffn_in"], W["w_ffn_out"].reshape(4, FF_BLK, D_MODEL)
    wg = jnp.pad(W["w_gate_up"], ((0, LANES - RANK), (0, 0))).astype(BF16)
    bgate = W["b_gate"].reshape(1, QK)
    gnorm = W["gla_norm"].reshape(HEADS, 1, DV)
    ln_g = W["sgu_ln_g"].reshape(GROUPS, 1, DG)
    ln_b = W["sgu_ln_b"].reshape(GROUPS, 1, DG)
    w_sp = W["w_spatial"]
    b_sp = W["b_spatial"].reshape(GROUPS, SBLOCK, 1)

    a = _rms_fwd_call(x, g1, name="rms_pre_mix")
    pa = _mm(a, w_a, "nn", BF16, name="in_proj_qkvr")
    alow = _mm(a, w_low, "nn", BF16, name="in_proj_low")
    ps = _mm(a, w_s, "nn", BF16, name="in_proj_sgu")
    pg = _mm(a, w_g, "nn", BF16, name="in_proj_gates")
    y_gla, states = _gla_fwd_call(pa, alow, wg, bgate, gnorm, name="gla_fwd")
    y_sgu = _sgu_fwd_call(ps, ln_g, ln_b, w_sp, b_sp, name="sgu_fwd")
    t1 = _mm(y_gla, W["w_branch_gla"], "nn", BF16, name="branch_gla")
    t2 = _mm(y_sgu, W["w_branch_sgu"], "nn", BF16, name="branch_sgu")
    merged = _merge_fwd_call(t1, t2, pg, name="merge_fwd")
    mix = _mm(merged, W["w_out"], "nn", F32, name="out_proj")
    x1, h = _post_mix_call(x, mix, g2, g3, name="post_mix")
    gu = _ffn_in_fwd(h, wfi, name="ffn_in")
    z = _swiglu_fwd_call(gu, name="swiglu_fwd")
    y = _ffn_out_fwd(z, wfo, name="ffn_out")
    loss, dx2, dy, dg4 = _loss_call(x1, y, target, g4, name="loss_head")

    grads = {"norm_post_ffn": dg4}
    received = {}
    dz = _ffn_out_bwd_x(dy, wfo, name="d_ffn_out_x")
    dw_ffn_out = _ffn_out_bwd_w(z, dy, name="d_ffn_out_w").reshape(N_DEV, D_FF // N_DEV, D_MODEL)
    dgu = _swiglu_bwd_call(dz, gu, name="swiglu_bwd")
    dw_ffn_in = _ffn_in_bwd_w(h, dgu, name="d_ffn_in_w")
    received.update(scatter(("w_ffn_out", "w_ffn_in"), [dw_ffn_out, dw_ffn_in]))
    dh = _ffn_in_bwd_x(dgu, wfi, name="d_ffn_in_x")
    dx1, dmix, grads["norm_pre_ffn"], grads["norm_post_mix"] = _mid_bwd_call(dx2, dh, x1, mix, g3, g2, name="mid_bwd")
    dmerged = _mm(dmix, W["w_out"], "nt", BF16, name="d_out_x")
    dt1, dt2, dpg = _merge_bwd_call(dmerged, t1, t2, pg, name="merge_bwd")
    rows = D_MODEL // N_DEV
    dw_out = _mm(merged, dmix, "tn", BF16, name="d_out_w").reshape(N_DEV, rows, D_MODEL)
    dw_bg = _mm(y_gla, dt1, "tn", BF16, name="d_branch_gla_w").reshape(N_DEV, rows, D_MODEL)
    dw_bs = _mm(y_sgu, dt2, "tn", BF16, name="d_branch_sgu_w").reshape(N_DEV, rows, D_MODEL)
    received.update(scatter(("w_out", "w_branch_gla", "w_branch_sgu"), [dw_out, dw_bg, dw_bs]))
    dy_gla = _mm(dt1, W["w_branch_gla"], "nt", BF16, name="d_branch_gla_x")
    dy_sgu = _mm(dt2, W["w_branch_sgu"], "nt", BF16, name="d_branch_sgu_x")
    dps, dlg, dlb, dwsp, dbsp = _sgu_bwd_call(ps, ln_g, ln_b, w_sp, b_sp, dy_sgu, name="sgu_bwd")
    dpa, dlogit, dgn, dbg = _gla_bwd_call(pa, alow, wg, bgate, gnorm, states, dy_gla, name="gla_bwd")
    dlow = _mm(dlogit, wg, "nt", BF16, name="d_gate_up_x")
    dw_a = _mm(a, dpa, "tn", BF16, name="d_in_qkvr_w")
    dw_low = _mm(a, dlow, "tn", BF16, name="d_in_low_w")
    dw_s = _mm(a, dps, "tn", BF16, name="d_in_sgu_w")
    dw_g = _mm(a, dpg, "tn", BF16, name="d_in_gates_w")
    dw_in = _in_blocks([(dw_a, A_COLS[0]), (dw_low[:, :RANK], LOW_COLS[0]), (dw_s, S_COLS[0]), (dw_g, G_COLS[0])])
    received.update(scatter(("w_in",), [dw_in]))
    dwg = _mm(alow, dlogit, "tn", F32, name="d_gate_up_w")
    da_parts = [_mm(dpa, w_a, "nt", F32, name="d_in_qkvr_x"), _mm(dlow, w_low, "nt", F32, name="d_in_low_x"),
                _mm(dps, w_s, "nt", F32, name="d_in_sgu_x"), _mm(dpg, w_g, "nt", F32, name="d_in_gates_x")]
    grad_x, grads["norm_pre_mix"] = _rms1_bwd_call(dx1, da_parts, x, g1, name="rms_pre_mix_bwd")

    grads["w_gate_up"] = dwg[:RANK]
    grads["b_gate"] = dbg
    grads["gla_norm"] = dgn
    grads["sgu_ln_g"] = dlg
    grads["sgu_ln_b"] = dlb
    grads["w_spatial"] = dwsp
    grads["b_spatial"] = dbsp
    return loss, grad_x, grads, received


WEIGHTS = ("norm_pre_mix", "w_in", "w_gate_up", "b_gate", "gla_norm", "sgu_ln_g", "sgu_ln_b", "w_spatial",
           "b_spatial", "w_branch_gla", "w_branch_sgu", "w_out", "norm_post_mix", "norm_pre_ffn", "w_ffn_in",
           "w_ffn_out", "norm_post_ffn")
BIG = ("w_in", "w_branch_gla", "w_branch_sgu", "w_out", "w_ffn_in", "w_ffn_out")
MIXER = ("w_branch_gla", "w_branch_sgu", "w_out")
FFN = ("w_ffn_in", "w_ffn_out")
SMALL = tuple(n for n in WEIGHTS if n not in BIG)
SMALL_SHARDED = ("w_gate_up", "gla_norm", "sgu_ln_g", "sgu_ln_b")
SMALL_FULL = {"norm_pre_mix": (1024,), "w_gate_up": (16, 512), "b_gate": (512,), "gla_norm": (4, 256),
              "sgu_ln_g": (4, 256), "sgu_ln_b": (4, 256), "w_spatial": (4, 128, 128), "b_spatial": (4, 128),
              "norm_post_mix": (1024,), "norm_pre_ffn": (1024,), "norm_post_ffn": (1024,)}
SMALL_GRAD_ROWS = 648
SMALL_STATE_ROWS = 600
SMALL_GATHER_ROWS = 32


def kernel(x, norm_pre_mix, w_in, w_gate_up, b_gate, gla_norm, sgu_ln_g, sgu_ln_b, w_spatial, b_spatial, w_branch_gla, w_branch_sgu, w_out, norm_post_mix, norm_pre_ffn, w_ffn_in, w_ffn_out, norm_post_ffn, loss_target, m_norm_pre_mix, m_w_in, m_w_gate_up, m_b_gate, m_gla_norm, m_sgu_ln_g, m_sgu_ln_b, m_w_spatial, m_b_spatial, m_w_branch_gla, m_w_branch_sgu, m_w_out, m_norm_post_mix, m_norm_pre_ffn, m_w_ffn_in, m_w_ffn_out, m_norm_post_ffn, v_norm_pre_mix, v_w_in, v_w_gate_up, v_b_gate, v_gla_norm, v_sgu_ln_g, v_sgu_ln_b, v_w_spatial, v_b_spatial, v_w_branch_gla, v_w_branch_sgu, v_w_out, v_norm_post_mix, v_norm_pre_ffn, v_w_ffn_in, v_w_ffn_out, v_norm_post_ffn):
    given = dict(locals())
    w = {n: given[n][0] for n in WEIGHTS}
    m = {n: given["m_" + n][0] for n in WEIGHTS}
    v = {n: given["v_" + n][0] for n in WEIGHTS}
    xs, target = x[0], loss_target[0]
    me = 4 * lax.axis_index("x") + 2 * lax.axis_index("y") + lax.axis_index("c")

    small_shard = _pack_rows([w[n] for n in SMALL_SHARDED], SMALL_GATHER_ROWS)
    first = _all_gather_hbm([w["w_in"].astype(BF16), small_shard], name="gather_w_in")
    mixer, first = lax.optimization_barrier(([w[n].astype(BF16) for n in MIXER], first))
    mixer_blocks = _all_gather_async(mixer, name="gather_mixer", collective_id=1)
    ffn, mixer_blocks = lax.optimization_barrier(([w[n].astype(BF16) for n in FFN], mixer_blocks))
    ffn_blocks = _all_gather_async(ffn, name="gather_ffn", collective_id=2)
    W = {n: w[n] for n in SMALL if n not in SMALL_SHARDED}
    blocks = {"w_in": first[0], **dict(zip(MIXER, mixer_blocks)), **dict(zip(FFN, ffn_blocks))}
    for n in ("w_branch_gla", "w_branch_sgu", "w_out", "w_ffn_out"):
        W[n] = blocks[n].reshape(-1, D_MODEL)
    W["w_ffn_in"] = blocks["w_ffn_in"]
    W["w_in_a"] = _in_columns(blocks["w_in"], *A_COLS)
    W["w_in_low"] = jnp.pad(_in_columns(blocks["w_in"], *LOW_COLS), ((0, 0), (0, LANES - RANK)))
    W["w_in_s"] = _in_columns(blocks["w_in"], *S_COLS)
    W["w_in_g"] = _in_columns(blocks["w_in"], *G_COLS)
    small_blocks = first[1]
    off = 0
    for n in SMALL_SHARDED:
        r, c = w[n].shape
        blk = small_blocks[:, off:off + r * c // LANES].reshape(N_DEV, r, c)
        W[n] = blk.transpose(1, 0, 2).reshape(r, N_DEV * c)
        off += _tile_rows(r * c)

    scatter_ids = iter((3, 4, 5))

    def scatter(names, parts):
        got = _scatter_blocks_async(parts, name="scatter_" + "_".join(names), collective_id=next(scatter_ids))
        return dict(zip(names, got))

    loss_part, grad_x, grads, received = _local_step(xs, target, W, scatter)

    small_part = jnp.concatenate([_pack_rows([grads[n] for n in SMALL], SMALL_GRAD_ROWS),
                                  jnp.broadcast_to(loss_part, (8, LANES))], axis=0)
    small_sum = _all_reduce_small(small_part, name="reduce_small")
    loss = small_sum[SMALL_GRAD_ROWS, 0]
    g_small = dict(zip(SMALL, _unpack_rows(small_sum, [SMALL_FULL[n] for n in SMALL])))
    for n in SMALL_SHARDED:
        c = w[n].shape[1]
        g_small[n] = lax.dynamic_slice_in_dim(g_small[n], me * c, c, axis=1)

    g_big, d_big, m_big, v_big = {}, {}, {}, {}
    for n in BIG:
        g_big[n], d_big[n], m_big[n], v_big[n] = _adamw_reduce_call(received[n], w[n], m[n], v[n], name="adamw_" + n)
    small_shapes = [w[n].shape for n in SMALL]
    small_state = [_pack_rows([s[n] for n in SMALL], SMALL_STATE_ROWS) for s in (w, g_small, m, v)]
    small_out = _adamw_small_call(*small_state, name="adamw_small")
    d_small, m_small, v_small = [dict(zip(SMALL, _unpack_rows(o, small_shapes))) for o in small_out]

    def pick(big, sml):
        return [(big[n] if n in BIG else sml[n])[None] for n in WEIGHTS]

    return (loss, grad_x[None], *pick(g_big, g_small), *pick(d_big, d_small), *pick(m_big, m_small),
            *pick(v_big, v_small))
```

```python
import jax
import jax.numpy as jnp
from jax import lax
from jax.experimental import pallas as pl
from jax.experimental.pallas import tpu as pltpu
from jax.experimental.pallas import tpu_sc as plsc

F32 = jnp.float32
BF16 = jnp.bfloat16

D_MODEL = 1024
N_DEV = 8
CHUNK = 64
HEADS = 4
DK = 128
DV = 256
QK = HEADS * DK
GV = HEADS * DV
RANK = 16
GROUPS = 4
SBLOCK = 128
DG = 256
D_FF = 2816
FF_BLK = 704
EPS = 1e-6
Q_SCALE = DK ** -0.5
LANES = 128
VMEM_BIG = 48 * 1024 * 1024

D_IN = 7184
IN_BLK = 898
A_COLS = (0, 3072)
LOW_COLS = (3072, 3088)
S_COLS = (3088, 5136)
G_COLS = (5136, 7184)

ADAM_LR = 0.001
ADAM_B1 = 0.9
ADAM_B2 = 0.999
ADAM_EPS = 1e-08
ADAM_WD = 0.01
ADAM_STEP = 10

MESH = pl.DeviceIdType.MESH
ANY = pl.BlockSpec(memory_space=pl.ANY)
VMEM_SPEC = pl.BlockSpec(memory_space=pltpu.VMEM)


def _cp(sem=None, vmem=None):
    return pltpu.CompilerParams(dimension_semantics=sem, vmem_limit_bytes=vmem)


def _sig(x):
    return 1.0 / (1.0 + jnp.exp(-x))


def _gelu(x):
    c = 0.7978845608028654
    t = jnp.tanh(c * (x + 0.044715 * x * x * x))
    return 0.5 * x * (1.0 + t)


def _gelu_grad(x):
    c = 0.7978845608028654
    x2 = x * x
    t = jnp.tanh(c * (x + 0.044715 * x * x2))
    return 0.5 * (1.0 + t) + 0.5 * x * (1.0 - t * t) * c * (1.0 + 3.0 * 0.044715 * x2)


def _logsig(x):
    return jnp.minimum(x, 0.0) - jnp.log1p(jnp.exp(-jnp.abs(x)))


def _dot(a, b, dims):
    return lax.dot_general(a, b, (dims, ((), ())), preferred_element_type=F32)


NN = ((1,), (0,))
NT = ((1,), (1,))
TN = ((0,), (0,))


def _exact_mask_dot(mask_bf16, x):
    hi = x.astype(BF16)
    r1 = x - hi.astype(F32)
    mid = r1.astype(BF16)
    lo = (r1 - mid.astype(F32)).astype(BF16)
    return _dot(mask_bf16, hi, NN) + _dot(mask_bf16, mid, NN) + _dot(mask_bf16, lo, NN)


def _pick_tile(dim, target):
    if dim <= target:
        return dim
    best = None
    for t in range(LANES, int(1.4 * target) + 1, LANES):
        if dim % t == 0:
            best = t
    assert best is not None, (dim, target)
    return best


def _mm_call(a, b, *, name, grid, a_spec, b_spec, o_spec, out_shape, dims, acc_shape):
    nk = grid[2]

    def body(a_ref, b_ref, o_ref, *acc):
        part = _dot(a_ref[...], b_ref[...], dims)
        if nk == 1:
            o_ref[...] = part.astype(o_ref.dtype)
        else:
            acc_ref = acc[0]
            k = pl.program_id(2)

            @pl.when(k == 0)
            def _():
                acc_ref[...] = part

            @pl.when(k > 0)
            def _():
                acc_ref[...] += part

            @pl.when(k == nk - 1)
            def _():
                o_ref[...] = acc_ref[...].astype(o_ref.dtype)

    return pl.pallas_call(
        body, name=name, grid=grid, in_specs=[a_spec, b_spec], out_specs=o_spec, out_shape=out_shape,
        scratch_shapes=[] if nk == 1 else [pltpu.VMEM(acc_shape, F32)],
        compiler_params=_cp(("parallel", "parallel", "arbitrary"), VMEM_BIG),
    )(a, b)


def _mm(a, b, mode, out_dtype, *, name, tm=512, tn=1024, tk=1024):
    if mode == "nn":
        (M, K), (_, N) = a.shape, b.shape
    elif mode == "nt":
        (M, K), (N, _) = a.shape, b.shape
    else:
        (K, M), (_, N) = a.shape, b.shape
    tm, tn, tk = _pick_tile(M, tm), _pick_tile(N, tn), _pick_tile(K, tk)
    if mode == "nn":
        a_spec = pl.BlockSpec((tm, tk), lambda i, j, k: (i, k))
        b_spec = pl.BlockSpec((tk, tn), lambda i, j, k: (k, j))
        dims = NN
    elif mode == "nt":
        a_spec = pl.BlockSpec((tm, tk), lambda i, j, k: (i, k))
        b_spec = pl.BlockSpec((tn, tk), lambda i, j, k: (j, k))
        dims = NT
    else:
        a_spec = pl.BlockSpec((tk, tm), lambda i, j, k: (k, i))
        b_spec = pl.BlockSpec((tk, tn), lambda i, j, k: (k, j))
        dims = TN
    return _mm_call(a, b, name=name, grid=(M // tm, N // tn, K // tk), a_spec=a_spec, b_spec=b_spec,
                    o_spec=pl.BlockSpec((tm, tn), lambda i, j, k: (i, j)),
                    out_shape=jax.ShapeDtypeStruct((M, N), out_dtype), dims=dims, acc_shape=(tm, tn))


ROW_TILE = 512


def _rt(T, W, c=0):
    return pl.BlockSpec((T, W), lambda i: (i, c))


def _rt3(nb, T, W):
    return pl.BlockSpec((nb, T, W), lambda i: (0, i, 0))


def _res(shape):
    nd = len(shape)
    return pl.BlockSpec(tuple(shape), lambda i: (0,) * nd, pipeline_mode=pl.Buffered(1))


def _acc(shape):
    nd = len(shape)
    return pl.BlockSpec(tuple(shape), lambda i: (0,) * nd, pipeline_mode=pl.Buffered(1))


def _nbytes(shape, dtype):
    n = jnp.dtype(dtype).itemsize
    for s in shape:
        n *= s
    return n


def _vmem_limit(tiles, resident, temps=16 * 1024 * 1024):
    need = 2 * sum(_nbytes(s, d) for s, d in tiles) + sum(_nbytes(s, d) for s, d in resident) + temps
    return min(need, 60 * 1024 * 1024)


def _tok_call(body, *, name, S, T, ins, outs, semantics="parallel"):
    tiles = [(spec.block_shape, a.dtype) for a, spec, kind in ins + outs if kind == "tile"]
    resident = [(a.shape, a.dtype) for a, spec, kind in ins + outs if kind == "res"]
    return pl.pallas_call(
        body, name=name, grid=(S // T,),
        in_specs=[spec for _, spec, _ in ins], out_specs=[spec for _, spec, _ in outs],
        out_shape=[jax.ShapeDtypeStruct(a.shape, a.dtype) for a, _, _ in outs],
        compiler_params=_cp((semantics,), _vmem_limit(tiles, resident)),
    )(*[a for a, _, _ in ins])


def _tile(a, spec):
    return (a, spec, "tile")


def _whole(a):
    return (a, _res(a.shape), "res")


def _out_tile(shape, dtype, spec):
    return (jax.ShapeDtypeStruct(shape, dtype), spec, "tile")


def _out_acc(shape, dtype=F32):
    return (jax.ShapeDtypeStruct(shape, dtype), _acc(shape), "res")


def _rms_stats(x):
    r = lax.rsqrt(jnp.mean(x * x, axis=-1, keepdims=True) + EPS)
    return r, x * r


def _rms_bwd(xh, r, g, dy):
    dxh = dy * g
    dx = r * (dxh - xh * jnp.mean(dxh * xh, axis=-1, keepdims=True))
    dg = jnp.sum(dy * xh, axis=0, keepdims=True)
    return dx, dg


def _accum(ref, val):
    @pl.when(pl.program_id(0) == 0)
    def _():
        ref[...] = val

    @pl.when(pl.program_id(0) > 0)
    def _():
        ref[...] += val


def _dot_cols(a, w_ref, o_ref, dims, chunk=1024):
    n = o_ref.shape[1]
    for n0 in range(0, n, chunk):
        n1 = min(n, n0 + chunk)
        w = w_ref[:, n0:n1] if dims == NN else w_ref[n0:n1, :]
        o_ref[:, n0:n1] = _dot(a, w, dims).astype(o_ref.dtype)


def _in_proj_call(x, g1, w_a, w_low, w_s, w_g, *, name, T=ROW_TILE):
    S = x.shape[0]

    def body(x_ref, g_ref, wa_ref, wl_ref, ws_ref, wg_ref, a_ref, pa_ref, al_ref, ps_ref, pg_ref):
        _, xh = _rms_stats(x_ref[...])
        a = (xh * g_ref[...]).astype(BF16)
        a_ref[...] = a
        _dot_cols(a, wa_ref, pa_ref, NN)
        _dot_cols(a, wl_ref, al_ref, NN)
        _dot_cols(a, ws_ref, ps_ref, NN)
        _dot_cols(a, wg_ref, pg_ref, NN)

    widths = (D_MODEL, w_a.shape[1], w_low.shape[1], w_s.shape[1], w_g.shape[1])
    return _tok_call(
        body, name=name, S=S, T=T,
        ins=[_tile(x, _rt(T, D_MODEL)), _whole(g1), _whole(w_a), _whole(w_low), _whole(w_s), _whole(w_g)],
        outs=[_out_tile((S, w), BF16, _rt(T, w)) for w in widths])


def _mixer_tail_call(y_gla, y_sgu, pg, x, w_bg, w_bs, w_out, g2, g3, *, name, T=ROW_TILE):
    S = x.shape[0]

    def body(yg_ref, ys_ref, pg_ref, x_ref, wbg_ref, wbs_ref, wo_ref, g2_ref, g3_ref,
             t1_ref, t2_ref, mg_ref, mix_ref, x1_ref, h_ref):
        t1 = _dot(yg_ref[...], wbg_ref[...], NN)
        t2 = _dot(ys_ref[...], wbs_ref[...], NN)
        t1_ref[...] = t1.astype(BF16)
        t2_ref[...] = t2.astype(BF16)
        sg = _sig(pg_ref[:, :D_MODEL].astype(F32))
        ss = _sig(pg_ref[:, D_MODEL:].astype(F32))
        merged = (sg * t1 + ss * t2).astype(BF16)
        mg_ref[...] = merged
        mix = _dot(merged, wo_ref[...], NN)
        mix_ref[...] = mix
        _, mh = _rms_stats(mix)
        x1 = x_ref[...] + mh * g2_ref[...]
        x1_ref[...] = x1
        _, xh = _rms_stats(x1)
        h_ref[...] = (xh * g3_ref[...]).astype(BF16)

    row = _rt(T, D_MODEL)
    b16 = lambda: _out_tile((S, D_MODEL), BF16, row)
    f32 = lambda: _out_tile((S, D_MODEL), F32, row)
    return _tok_call(
        body, name=name, S=S, T=T,
        ins=[_tile(y_gla, row), _tile(y_sgu, row), _tile(pg, _rt(T, 2 * D_MODEL)), _tile(x, row),
             _whole(w_bg), _whole(w_bs), _whole(w_out), _whole(g2), _whole(g3)],
        outs=[b16(), b16(), b16(), f32(), f32(), b16()])


def _ffn_in_call(h, wfi, *, name, T=ROW_TILE):
    S = h.shape[0]

    def body(h_ref, w_ref, gu_ref, z_ref):
        hv = h_ref[...]
        for d in range(4):
            gate = _dot(hv, w_ref[d], NN)
            up = _dot(hv, w_ref[d + 4], NN)
            gu_ref[d] = gate.astype(BF16)
            gu_ref[d + 4] = up.astype(BF16)
            z_ref[d] = (gate * _sig(gate) * up).astype(BF16)

    return _tok_call(
        body, name=name, S=S, T=T,
        ins=[_tile(h, _rt(T, D_MODEL)), _whole(wfi)],
        outs=[_out_tile((N_DEV, S, FF_BLK), BF16, _rt3(N_DEV, T, FF_BLK)),
              _out_tile((4, S, FF_BLK), BF16, _rt3(4, T, FF_BLK))])


def _ffn_out_loss_call(z, wfo, x1, target, g4, *, name, T=ROW_TILE):
    S = x1.shape[0]

    def body(z_ref, w_ref, x1_ref, t_ref, g4_ref, loss_ref, dx2_ref, dy_ref, dg4_ref):
        y = _dot(z_ref[0], w_ref[0], NN)
        for d in range(1, 4):
            y = y + _dot(z_ref[d], w_ref[d], NN)
        r, yh = _rms_stats(y)
        diff = x1_ref[...] + yh * g4_ref[...] - t_ref[...]
        part = 0.5 * jnp.sum(jnp.mean(diff * diff, axis=-1, keepdims=True), axis=0, keepdims=True)
        _accum(loss_ref, jnp.broadcast_to(part, (1, LANES)))
        dx2 = diff * (1.0 / D_MODEL)
        dx2_ref[...] = dx2
        dy, dg = _rms_bwd(yh, r, g4_ref[...], dx2)
        dy_ref[...] = dy.astype(BF16)
        _accum(dg4_ref, dg)

    row = _rt(T, D_MODEL)
    return _tok_call(
        body, name=name, S=S, T=T, semantics="arbitrary",
        ins=[_tile(z, _rt3(4, T, FF_BLK)), _whole(wfo), _tile(x1, row), _tile(target, row), _whole(g4)],
        outs=[_out_acc((1, LANES)), _out_tile((S, D_MODEL), F32, row), _out_tile((S, D_MODEL), BF16, row),
              _out_acc((1, D_MODEL))])


def _ffn_out_bwd_call(dy, wfo, gu, *, name, T=ROW_TILE):
    S = dy.shape[0]

    def body(dy_ref, w_ref, gu_ref, dgu_ref):
        dyv = dy_ref[...]
        for d in range(4):
            dz = _dot(dyv, w_ref[d], NT)
            gt = gu_ref[d].astype(F32)
            up = gu_ref[d + 4].astype(F32)
            s = _sig(gt)
            dgu_ref[d] = (dz * up * s * (1.0 + gt * (1.0 - s))).astype(BF16)
            dgu_ref[d + 4] = (dz * gt * s).astype(BF16)

    blocks = _rt3(N_DEV, T, FF_BLK)
    return _tok_call(
        body, name=name, S=S, T=T,
        ins=[_tile(dy, _rt(T, D_MODEL)), _whole(wfo), _tile(gu, blocks)],
        outs=[_out_tile((N_DEV, S, FF_BLK), BF16, blocks)])[0]


def _ffn_in_bwd_call(dgu, wfi, dx2, x1, mix, g3, g2, *, name, T=ROW_TILE):
    S = x1.shape[0]

    def body(dgu_ref, w_ref, dx2_ref, x1_ref, mix_ref, g3_ref, g2_ref, dx1_ref, dmix_ref, dg3_ref, dg2_ref):
        dh = _dot(dgu_ref[0], w_ref[0], NT)
        for d in range(1, N_DEV):
            dh = dh + _dot(dgu_ref[d], w_ref[d], NT)
        r3, xh = _rms_stats(x1_ref[...])
        d3, dg3 = _rms_bwd(xh, r3, g3_ref[...], dh)
        dx1 = dx2_ref[...] + d3
        dx1_ref[...] = dx1
        r2, mh = _rms_stats(mix_ref[...])
        dmix, dg2 = _rms_bwd(mh, r2, g2_ref[...], dx1)
        dmix_ref[...] = dmix.astype(BF16)
        _accum(dg3_ref, dg3)
        _accum(dg2_ref, dg2)

    row = _rt(T, D_MODEL)
    return _tok_call(
        body, name=name, S=S, T=T, semantics="arbitrary",
        ins=[_tile(dgu, _rt3(N_DEV, T, FF_BLK)), _whole(wfi), _tile(dx2, row), _tile(x1, row), _tile(mix, row),
             _whole(g3), _whole(g2)],
        outs=[_out_tile((S, D_MODEL), F32, row), _out_tile((S, D_MODEL), BF16, row),
              _out_acc((1, D_MODEL)), _out_acc((1, D_MODEL))])


def _mixer_bwd_call(dmix, t1, t2, pg, w_out, w_bg, w_bs, *, name, T=ROW_TILE):
    S = dmix.shape[0]

    def body(dmix_ref, t1_ref, t2_ref, pg_ref, wo_ref, wbg_ref, wbs_ref, dt1_ref, dt2_ref, dpg_ref, dyg_ref, dys_ref):
        dm = _dot(dmix_ref[...], wo_ref[...], NT)
        sg = _sig(pg_ref[:, :D_MODEL].astype(F32))
        ss = _sig(pg_ref[:, D_MODEL:].astype(F32))
        dt1 = (dm * sg).astype(BF16)
        dt2 = (dm * ss).astype(BF16)
        dt1_ref[...] = dt1
        dt2_ref[...] = dt2
        dpg_ref[:, :D_MODEL] = (dm * t1_ref[...].astype(F32) * sg * (1.0 - sg)).astype(BF16)
        dpg_ref[:, D_MODEL:] = (dm * t2_ref[...].astype(F32) * ss * (1.0 - ss)).astype(BF16)
        dyg_ref[...] = _dot(dt1, wbg_ref[...], NT).astype(BF16)
        dys_ref[...] = _dot(dt2, wbs_ref[...], NT).astype(BF16)

    row = _rt(T, D_MODEL)
    wide = _rt(T, 2 * D_MODEL)
    b16 = lambda: _out_tile((S, D_MODEL), BF16, row)
    return _tok_call(
        body, name=name, S=S, T=T,
        ins=[_tile(dmix, row), _tile(t1, row), _tile(t2, row), _tile(pg, wide), _whole(w_out), _whole(w_bg), _whole(w_bs)],
        outs=[b16(), b16(), _out_tile((S, 2 * D_MODEL), BF16, wide), b16(), b16()])


def _in_proj_bwd_call(dpa, dlow, dps, dpg, w_a, w_low, w_s, w_g, x, dx1, g1, *, name, T=ROW_TILE):
    S = x.shape[0]

    def body(dpa_ref, dl_ref, dps_ref, dpg_ref, wa_ref, wl_ref, ws_ref, wg_ref, x_ref, dx1_ref, g1_ref, gx_ref, dg1_ref):
        da = (_dot(dpa_ref[...], wa_ref[...], NT) + _dot(dl_ref[...], wl_ref[...], NT)
              + _dot(dps_ref[...], ws_ref[...], NT) + _dot(dpg_ref[...], wg_ref[...], NT))
        r, xh = _rms_stats(x_ref[...])
        dxa, dg = _rms_bwd(xh, r, g1_ref[...], da)
        gx_ref[...] = dx1_ref[...] + dxa
        _accum(dg1_ref, dg)

    row = _rt(T, D_MODEL)
    return _tok_call(
        body, name=name, S=S, T=T, semantics="arbitrary",
        ins=[_tile(dpa, _rt(T, dpa.shape[1])), _tile(dlow, _rt(T, dlow.shape[1])), _tile(dps, _rt(T, dps.shape[1])),
             _tile(dpg, _rt(T, dpg.shape[1])), _whole(w_a), _whole(w_low), _whole(w_s), _whole(w_g),
             _tile(x, row), _tile(dx1, row), _whole(g1)],
        outs=[_out_tile((S, D_MODEL), F32, row), _out_acc((1, D_MODEL))])


TOKEN_TILE = 512


def _weight_grads_call(arrays, pairs, *, name, tk=TOKEN_TILE):
    S = arrays[0].shape[-2]
    tk = min(tk, S)
    n_in = len(arrays)

    def out_shape(i, j):
        a, b = arrays[i], arrays[j]
        if a.ndim == 3:
            return (a.shape[0], a.shape[2], b.shape[1])
        if b.ndim == 3:
            return (b.shape[0], a.shape[1], b.shape[2])
        return (a.shape[1], b.shape[1])

    shapes = [out_shape(i, j) for i, j in pairs]

    def body(*refs):
        ins, outs, accs = refs[:n_in], refs[n_in:n_in + len(pairs)], refs[n_in + len(pairs):]
        k = pl.program_id(0)

        @pl.when(k == 0)
        def _():
            for acc in accs:
                acc[...] = jnp.zeros_like(acc)

        for (i, j), acc in zip(pairs, accs):
            a_ref, b_ref = ins[i], ins[j]
            if len(a_ref.shape) == 3:
                for n in range(a_ref.shape[0]):
                    acc[n] += _dot(a_ref[n], b_ref[...], TN)
            elif len(b_ref.shape) == 3:
                a = a_ref[...]
                for n in range(b_ref.shape[0]):
                    acc[n] += _dot(a, b_ref[n], TN)
            else:
                a = a_ref[...]
                for n0 in range(0, b_ref.shape[1], 1024):
                    n1 = min(b_ref.shape[1], n0 + 1024)
                    acc[:, n0:n1] += _dot(a, b_ref[:, n0:n1], TN)

        @pl.when(k == S // tk - 1)
        def _():
            for out, acc in zip(outs, accs):
                out[...] = acc[...].astype(out.dtype)

    def in_spec(a):
        if a.ndim == 3:
            return pl.BlockSpec((a.shape[0], tk, a.shape[2]), lambda k: (0, k, 0))
        return pl.BlockSpec((tk, a.shape[1]), lambda k: (k, 0))

    tiles = [(in_spec(a).block_shape, a.dtype) for a in arrays]
    resident = [(s, F32) for s in shapes] + [(s, BF16) for s in shapes]
    return pl.pallas_call(
        body, name=name, grid=(S // tk,),
        in_specs=[in_spec(a) for a in arrays],
        out_specs=[_acc(s) for s in shapes],
        out_shape=[jax.ShapeDtypeStruct(s, BF16) for s in shapes],
        scratch_shapes=[pltpu.VMEM(s, F32) for s in shapes],
        compiler_params=_cp(("arbitrary",), _vmem_limit(tiles, resident)),
    )(*arrays)


def _ffn_in_grad_call(h, dgu, *, name, tk=TOKEN_TILE):
    S = h.shape[0]
    tk = min(tk, S)
    nb = 4
    shape = (nb, D_MODEL, FF_BLK)

    def body(h_ref, dgu_ref, out_ref, acc):
        k = pl.program_id(1)

        @pl.when(k == 0)
        def _():
            acc[...] = jnp.zeros_like(acc)

        hv = h_ref[...]
        for n in range(nb):
            acc[n] += _dot(hv, dgu_ref[n], TN)

        @pl.when(k == S // tk - 1)
        def _():
            out_ref[...] = acc[...].astype(out_ref.dtype)

    tiles = [((tk, D_MODEL), BF16), ((nb, tk, FF_BLK), BF16), (shape, BF16)]
    return pl.pallas_call(
        body, name=name, grid=(N_DEV // nb, S // tk),
        in_specs=[pl.BlockSpec((tk, D_MODEL), lambda g, k: (k, 0)),
                  pl.BlockSpec((nb, tk, FF_BLK), lambda g, k: (g, k, 0))],
        out_specs=pl.BlockSpec(shape, lambda g, k: (g, 0, 0)),
        out_shape=jax.ShapeDtypeStruct((N_DEV, D_MODEL, FF_BLK), BF16),
        scratch_shapes=[pltpu.VMEM(shape, F32)],
        compiler_params=_cp(("parallel", "arbitrary"), _vmem_limit(tiles, [(shape, F32)])),
    )(h, dgu)


GLA_TILE = 256
Q_OFF, K_OFF, V_OFF, R_OFF = 0, QK, 2 * QK, 2 * QK + GV


def _tri(lower):
    r = lax.broadcasted_iota(jnp.int32, (CHUNK, CHUNK), 0)
    c = lax.broadcasted_iota(jnp.int32, (CHUNK, CHUNK), 1)
    return jnp.where((r >= c) if lower else (c >= r), 1.0, 0.0).astype(BF16)


def _gla_fwd_call(pa, alow, wg, bgate, gnorm, *, name):
    S = pa.shape[0]
    Tg = min(GLA_TILE, S)
    cb = Tg // CHUNK

    def body(pa_ref, al_ref, wg_ref, bg_ref, gn_ref, y_ref, st_ref, state):
        @pl.when(pl.program_id(0) == 0)
        def _():
            state[...] = jnp.zeros_like(state)

        logit = _dot(al_ref[...], wg_ref[...], NN) + bg_ref[...]
        ls = _logsig(logit) * (1.0 / 16.0)
        tri = _tri(True)
        for h in range(HEADS):
            for c in range(cb):
                rows = pl.ds(c * CHUNK, CHUNK)
                cum = _exact_mask_dot(tri, ls[c * CHUNK:(c + 1) * CHUNK, h * DK:(h + 1) * DK])
                tot = cum[CHUNK - 1:CHUNK]
                kd = (pa_ref[rows, pl.ds(K_OFF + h * DK, DK)].astype(F32) * jnp.exp(tot - cum)).astype(BF16)
                new = state[h] * jnp.exp(tot) + _dot(pa_ref[rows, pl.ds(V_OFF + h * DV, DV)], kd, TN)
                state[h] = new
                st_ref[c, h] = new
                qs = (pa_ref[rows, pl.ds(Q_OFF + h * DK, DK)].astype(F32) * Q_SCALE).astype(BF16)
                o = _dot(qs, new.astype(BF16), NT)
                rs = lax.rsqrt(jnp.mean(o * o, axis=-1, keepdims=True) + EPS)
                rr = pa_ref[rows, pl.ds(R_OFF + h * DV, DV)].astype(F32)
                y_ref[rows, pl.ds(h * DV, DV)] = (o * rs * gn_ref[h] * (rr * _sig(rr))).astype(BF16)

    return pl.pallas_call(
        body, name=name, grid=(S // Tg,),
        in_specs=[
            pl.BlockSpec((Tg, 2 * QK + 2 * GV), lambda t: (t, 0)),
            pl.BlockSpec((Tg, LANES), lambda t: (t, 0)),
            pl.BlockSpec((LANES, QK), lambda t: (0, 0)),
            pl.BlockSpec((1, QK), lambda t: (0, 0)),
            pl.BlockSpec((HEADS, 1, DV), lambda t: (0, 0, 0)),
        ],
        out_specs=[
            pl.BlockSpec((Tg, GV), lambda t: (t, 0)),
            pl.BlockSpec((cb, HEADS, DV, DK), lambda t: (t, 0, 0, 0)),
        ],
        out_shape=[jax.ShapeDtypeStruct((S, GV), BF16),
                   jax.ShapeDtypeStruct((S // CHUNK, HEADS, DV, DK), F32)],
        scratch_shapes=[pltpu.VMEM((HEADS, DV, DK), F32)],
        compiler_params=_cp(("arbitrary",), VMEM_BIG),
    )(pa, alow, wg, bgate, gnorm)


def _gla_bwd_call(pa, alow, wg, bgate, gnorm, states, dy, *, name):
    S = pa.shape[0]
    Tg = min(GLA_TILE, S)
    cb = Tg // CHUNK
    nt = S // Tg

    def rev(t):
        return nt - 1 - t

    def body(pa_ref, al_ref, wg_ref, bg_ref, gn_ref, st_ref, prev_ref, dy_ref,
             dpa_ref, dl_ref, dgn_ref, dbg_ref, carry):
        t = pl.program_id(0)

        @pl.when(t == 0)
        def _():
            carry[...] = jnp.zeros_like(carry)
            dgn_ref[...] = jnp.zeros_like(dgn_ref)
            dbg_ref[...] = jnp.zeros_like(dbg_ref)

        logit = _dot(al_ref[...], wg_ref[...], NN) + bg_ref[...]
        ls = _logsig(logit) * (1.0 / 16.0)
        sneg = _sig(-logit)
        tri = _tri(True)
        upper = _tri(False)
        first_tile = rev(t) == 0
        for h in range(HEADS):
            gn = gn_ref[h]
            hk = pl.ds(h * DK, DK)
            dbg = jnp.zeros((1, DK), F32)
            dgn = jnp.zeros((1, DV), F32)
            for c in reversed(range(cb)):
                rows = pl.ds(c * CHUNK, CHUNK)
                cum = _exact_mask_dot(tri, ls[c * CHUNK:(c + 1) * CHUNK, h * DK:(h + 1) * DK])
                tot = cum[CHUNK - 1:CHUNK]
                w = jnp.exp(tot - cum)
                decay = jnp.exp(tot)
                kd = pa_ref[rows, pl.ds(K_OFF + h * DK, DK)].astype(F32) * w
                kd16 = kd.astype(BF16)
                st = st_ref[c, h]
                if c > 0:
                    st_prev = st_ref[c - 1, h]
                else:
                    st_prev = jnp.where(first_tile, 0.0, prev_ref[0, h])
                qs = (pa_ref[rows, pl.ds(Q_OFF + h * DK, DK)].astype(F32) * Q_SCALE).astype(BF16)
                st16 = st.astype(BF16)
                o = _dot(qs, st16, NT)
                rs = lax.rsqrt(jnp.mean(o * o, axis=-1, keepdims=True) + EPS)
                oh = o * rs
                rr = pa_ref[rows, pl.ds(R_OFF + h * DV, DV)].astype(F32)
                sr = _sig(rr)
                dyv = dy_ref[rows, pl.ds(h * DV, DV)].astype(F32)
                dpa_ref[rows, pl.ds(R_OFF + h * DV, DV)] = (
                    dyv * oh * gn * sr * (1.0 + rr * (1.0 - sr))).astype(BF16)
                don = dyv * (rr * sr)
                dgn = dgn + jnp.sum(don * oh, axis=0, keepdims=True)
                doh = don * gn
                do16 = (rs * (doh - oh * jnp.mean(doh * oh, axis=-1, keepdims=True))).astype(BF16)
                dpa_ref[rows, pl.ds(Q_OFF + h * DK, DK)] = (_dot(do16, st16, NN) * Q_SCALE).astype(BF16)
                gt = _dot(do16, qs, TN) + carry[h]
                gt16 = gt.astype(BF16)
                dkd = _dot(pa_ref[rows, pl.ds(V_OFF + h * DV, DV)], gt16, NN)
                dpa_ref[rows, pl.ds(V_OFF + h * DV, DV)] = _dot(kd16, gt16, NT).astype(BF16)
                ddecay = jnp.sum(gt * st_prev, axis=0, keepdims=True)
                dpa_ref[rows, pl.ds(K_OFF + h * DK, DK)] = (dkd * w).astype(BF16)
                e = dkd * kd
                dtot = jnp.sum(e, axis=0, keepdims=True) + ddecay * decay
                dls = dtot - _exact_mask_dot(upper, e)
                dlogit = dls * (1.0 / 16.0) * sneg[c * CHUNK:(c + 1) * CHUNK, h * DK:(h + 1) * DK]
                dl_ref[rows, hk] = dlogit.astype(BF16)
                dbg = dbg + jnp.sum(dlogit, axis=0, keepdims=True)
                carry[h] = gt * decay
            dgn_ref[h] += dgn
            dbg_ref[:, hk] += dbg

    return pl.pallas_call(
        body, name=name, grid=(nt,),
        in_specs=[
            pl.BlockSpec((Tg, 2 * QK + 2 * GV), lambda t: (rev(t), 0)),
            pl.BlockSpec((Tg, LANES), lambda t: (rev(t), 0)),
            pl.BlockSpec((LANES, QK), lambda t: (0, 0)),
            pl.BlockSpec((1, QK), lambda t: (0, 0)),
            pl.BlockSpec((HEADS, 1, DV), lambda t: (0, 0, 0)),
            pl.BlockSpec((cb, HEADS, DV, DK), lambda t: (rev(t), 0, 0, 0)),
            pl.BlockSpec((1, HEADS, DV, DK), lambda t: (jnp.maximum(rev(t) * cb - 1, 0), 0, 0, 0)),
            pl.BlockSpec((Tg, GV), lambda t: (rev(t), 0)),
        ],
        out_specs=[
            pl.BlockSpec((Tg, 2 * QK + 2 * GV), lambda t: (rev(t), 0)),
            pl.BlockSpec((Tg, QK), lambda t: (rev(t), 0)),
            pl.BlockSpec((HEADS, 1, DV), lambda t: (0, 0, 0)),
            pl.BlockSpec((1, QK), lambda t: (0, 0)),
        ],
        out_shape=[jax.ShapeDtypeStruct((S, 2 * QK + 2 * GV), BF16), jax.ShapeDtypeStruct((S, QK), BF16),
                   jax.ShapeDtypeStruct((HEADS, 1, DV), F32), jax.ShapeDtypeStruct((1, QK), F32)],
        scratch_shapes=[pltpu.VMEM((HEADS, DV, DK), F32)],
        compiler_params=_cp(("arbitrary",), VMEM_BIG),
    )(pa, alow, wg, bgate, gnorm, states, states, dy)


SGU_TILE = 256


def _sgu_mask():
    r = lax.broadcasted_iota(jnp.int32, (SBLOCK, SBLOCK), 0)
    c = lax.broadcasted_iota(jnp.int32, (SBLOCK, SBLOCK), 1)
    return (c < CHUNK) | (r >= CHUNK)


def _ln_stats(vf):
    mu = jnp.mean(vf, axis=-1, keepdims=True)
    xc = vf - mu
    rs = lax.rsqrt(jnp.mean(xc * xc, axis=-1, keepdims=True) + EPS)
    return rs, xc * rs


def _sgu_fwd_call(ps, ln_g, ln_b, w_sp, b_sp, *, name):
    S = ps.shape[0]
    Ts = min(SGU_TILE, S)

    def body(ps_ref, lg_ref, lb_ref, w_ref, b_ref, y_ref):
        mask = _sgu_mask()
        for g in range(GROUPS):
            wm = jnp.where(mask, w_ref[g], 0.0).astype(BF16)
            for p in range(Ts // SBLOCK):
                rows = pl.ds(p * SBLOCK, SBLOCK)
                u = _gelu(ps_ref[rows, pl.ds(g * DG, DG)].astype(F32))
                _, xh = _ln_stats(_gelu(ps_ref[rows, pl.ds(D_MODEL + g * DG, DG)].astype(F32)))
                vn = xh * lg_ref[g] + lb_ref[g]
                mixed = _dot(wm, vn.astype(BF16), NN) + b_ref[g]
                y_ref[rows, pl.ds(g * DG, DG)] = (u * mixed).astype(BF16)

    full3 = lambda a, b, c: pl.BlockSpec((a, b, c), lambda t: (0, 0, 0))
    return pl.pallas_call(
        body, name=name, grid=(S // Ts,),
        in_specs=[pl.BlockSpec((Ts, 2 * D_MODEL), lambda t: (t, 0)),
                  full3(GROUPS, 1, DG), full3(GROUPS, 1, DG), full3(GROUPS, SBLOCK, SBLOCK), full3(GROUPS, SBLOCK, 1)],
        out_specs=pl.BlockSpec((Ts, D_MODEL), lambda t: (t, 0)),
        out_shape=jax.ShapeDtypeStruct((S, D_MODEL), BF16),
        compiler_params=_cp(("parallel",)),
    )(ps, ln_g, ln_b, w_sp, b_sp)


def _sgu_bwd_call(ps, ln_g, ln_b, w_sp, b_sp, dy, *, name):
    S = ps.shape[0]
    Ts = min(SGU_TILE, S)

    def body(ps_ref, lg_ref, lb_ref, w_ref, b_ref, dy_ref, ds_ref, dlg_ref, dlb_ref, dw_ref, db_ref):
        @pl.when(pl.program_id(0) == 0)
        def _():
            dlg_ref[...] = jnp.zeros_like(dlg_ref)
            dlb_ref[...] = jnp.zeros_like(dlb_ref)
            dw_ref[...] = jnp.zeros_like(dw_ref)
            db_ref[...] = jnp.zeros_like(db_ref)

        mask = _sgu_mask()
        for g in range(GROUPS):
            wm = jnp.where(mask, w_ref[g], 0.0).astype(BF16)
            lg = lg_ref[g]
            for p in range(Ts // SBLOCK):
                rows = pl.ds(p * SBLOCK, SBLOCK)
                su = ps_ref[rows, pl.ds(g * DG, DG)].astype(F32)
                sv = ps_ref[rows, pl.ds(D_MODEL + g * DG, DG)].astype(F32)
                u = _gelu(su)
                rs, xh = _ln_stats(_gelu(sv))
                vn16 = (xh * lg + lb_ref[g]).astype(BF16)
                mixed = _dot(wm, vn16, NN) + b_ref[g]
                dyv = dy_ref[rows, pl.ds(g * DG, DG)].astype(F32)
                ds_ref[rows, pl.ds(g * DG, DG)] = (dyv * mixed * _gelu_grad(su)).astype(BF16)
                dmix = dyv * u
                dmix16 = dmix.astype(BF16)
                db_ref[g] += jnp.sum(dmix, axis=-1, keepdims=True)
                dw_ref[g] += jnp.where(mask, _dot(dmix16, vn16, NT), 0.0)
                dvn = _dot(wm, dmix16, TN)
                dlg_ref[g] += jnp.sum(dvn * xh, axis=0, keepdims=True)
                dlb_ref[g] += jnp.sum(dvn, axis=0, keepdims=True)
                dxh = dvn * lg
                dvf = rs * (dxh - jnp.mean(dxh, axis=-1, keepdims=True)
                            - xh * jnp.mean(dxh * xh, axis=-1, keepdims=True))
                ds_ref[rows, pl.ds(D_MODEL + g * DG, DG)] = (dvf * _gelu_grad(sv)).astype(BF16)

    full3 = lambda a, b, c: pl.BlockSpec((a, b, c), lambda t: (0, 0, 0))
    return pl.pallas_call(
        body, name=name, grid=(S // Ts,),
        in_specs=[pl.BlockSpec((Ts, 2 * D_MODEL), lambda t: (t, 0)),
                  full3(GROUPS, 1, DG), full3(GROUPS, 1, DG), full3(GROUPS, SBLOCK, SBLOCK), full3(GROUPS, SBLOCK, 1),
                  pl.BlockSpec((Ts, D_MODEL), lambda t: (t, 0))],
        out_specs=[pl.BlockSpec((Ts, 2 * D_MODEL), lambda t: (t, 0)),
                   full3(GROUPS, 1, DG), full3(GROUPS, 1, DG), full3(GROUPS, SBLOCK, SBLOCK), full3(GROUPS, SBLOCK, 1)],
        out_shape=[jax.ShapeDtypeStruct((S, 2 * D_MODEL), BF16),
                   jax.ShapeDtypeStruct((GROUPS, 1, DG), F32), jax.ShapeDtypeStruct((GROUPS, 1, DG), F32),
                   jax.ShapeDtypeStruct((GROUPS, SBLOCK, SBLOCK), F32),
                   jax.ShapeDtypeStruct((GROUPS, SBLOCK, 1), F32)],
        compiler_params=_cp(("arbitrary",)),
    )(ps, ln_g, ln_b, w_sp, b_sp, dy)


def _position():
    return lax.axis_index("x"), lax.axis_index("y"), lax.axis_index("c")


def _gather_copies(srcs, dsts, send_sems, recv_sems, local_sems):
    x, y, c = _position()
    me, sibling = (x, y, c), (x, y, 1 - c)
    chips = [(1 - x, y), (x, 1 - y), (1 - x, 1 - y)]
    n = len(srcs)

    def slab(a, block):
        px, py, pc = block
        return dsts[a].at[4 * px + 2 * py + pc]

    def copy(a, k, block, to, src=None):
        return pltpu.make_async_remote_copy(
            src_ref=slab(a, block) if src is None else src, dst_ref=slab(a, block),
            send_sem=send_sems.at[7 * a + k], recv_sem=recv_sems.at[7 * a + k], device_id=to, device_id_type=MESH)

    mine = [pltpu.make_async_copy(srcs[a], slab(a, me), local_sems.at[a]) for a in range(n)]
    for cp in mine:
        cp.start()
    first = []
    for a in range(n):
        first.append(copy(a, 0, me, sibling, src=srcs[a]))
        first += [copy(a, 1 + j, me, (*chip, c), src=srcs[a]) for j, chip in enumerate(chips)]
    for cp in first:
        cp.start()
    passed = []
    for j, chip in enumerate(chips):
        for a in range(n):
            copy(a, 1 + j, (*chip, c), me).wait_recv()
            cp = copy(a, 4 + j, (*chip, c), sibling)
            cp.start()
            passed.append(cp)
    for a in range(n):
        copy(a, 0, sibling, me).wait_recv()
        for j, chip in enumerate(chips):
            copy(a, 4 + j, (*chip, 1 - c), me).wait_recv()
    for cp in first + passed:
        cp.wait_send()
    for cp in mine:
        cp.wait()


def _all_gather_hbm(shards, *, name):
    n = len(shards)

    def body(*refs):
        srcs, dsts = refs[:n], refs[n:2 * n]
        send_sems, recv_sems, local_sems = refs[2 * n:]
        _gather_copies(srcs, dsts, send_sems, recv_sems, local_sems)

    return pl.pallas_call(
        body, name=name,
        in_specs=[ANY] * n, out_specs=[ANY] * n,
        out_shape=[jax.ShapeDtypeStruct((N_DEV, *s.shape), s.dtype) for s in shards],
        scratch_shapes=_comm_sems(n),
    )(*shards)


def _all_reduce_small(part, *, name):
    R, W = part.shape

    def body(x_ref, out_ref, gathered, send_sems, recv_sems, local_sems):
        _gather_copies([x_ref], [gathered], send_sems, recv_sems, local_sems)
        acc = gathered[0]
        for d in range(1, N_DEV):
            acc = acc + gathered[d]
        out_ref[...] = acc

    return pl.pallas_call(
        body, name=name,
        in_specs=[VMEM_SPEC], out_specs=VMEM_SPEC,
        out_shape=jax.ShapeDtypeStruct((R, W), F32),
        scratch_shapes=[pltpu.VMEM((N_DEV, R, W), F32),
                        pltpu.SemaphoreType.DMA((7,)), pltpu.SemaphoreType.DMA((7,)), pltpu.SemaphoreType.DMA((1,))],
    )(part)


FLIPS = [(fx, fy, fc) for fx in (0, 1) for fy in (0, 1) for fc in (0, 1)][1:]


def _scatter_copies(srcs, dsts, send_sems, recv_sems, local_sems):
    n = len(srcs)
    x, y, c = _position()
    me = 4 * x + 2 * y + c
    mine = [pltpu.make_async_copy(srcs[a].at[me], dsts[a].at[me], local_sems.at[a]) for a in range(n)]
    for cp in mine:
        cp.start()
    copies = []
    for k, (fx, fy, fc) in enumerate(FLIPS):
        tx = 1 - x if fx else x
        ty = 1 - y if fy else y
        tc = 1 - c if fc else c
        peer = 4 * tx + 2 * ty + tc
        for a in range(n):
            cp = pltpu.make_async_remote_copy(
                src_ref=srcs[a].at[peer], dst_ref=dsts[a].at[me],
                send_sem=send_sems.at[7 * a + k], recv_sem=recv_sems.at[7 * a + k],
                device_id=(tx, ty, tc), device_id_type=MESH)
            cp.start()
            copies.append(cp)
    for cp in copies:
        cp.wait()
    for cp in mine:
        cp.wait()


def _comm_sems(n):
    return [pltpu.SemaphoreType.DMA((7 * n,)), pltpu.SemaphoreType.DMA((7 * n,)), pltpu.SemaphoreType.DMA((n,))]


def _scatter_blocks(parts, *, name):
    n = len(parts)

    def body(*refs):
        _scatter_copies(refs[:n], refs[n:2 * n], *refs[2 * n:])

    return pl.pallas_call(
        body, name=name,
        in_specs=[ANY] * n, out_specs=[ANY] * n,
        out_shape=[jax.ShapeDtypeStruct(p.shape, p.dtype) for p in parts],
        scratch_shapes=_comm_sems(n),
    )(*parts)


def _handshake(peers):
    barrier = pltpu.get_barrier_semaphore()
    for peer in peers:
        pl.semaphore_signal(barrier, inc=1, device_id=peer, device_id_type=MESH)
    pl.semaphore_wait(barrier, len(peers))


def _sequencer_call(arrays, out_types, copies_fn, peers_fn, *, name, collective_id):
    n = len(arrays)
    srcs = [jax.new_ref(a, memory_space=pltpu.MemorySpace.HBM) for a in arrays]
    dsts = [jax.empty_ref(t, memory_space=pltpu.MemorySpace.HBM) for t in out_types]

    @pl.kernel(mesh=plsc.ScalarSubcoreMesh(axis_name="sequencer", num_cores=1), name=name,
               scratch_types=_comm_sems(n), compiler_params=pltpu.CompilerParams(collective_id=collective_id))
    def launch(send_sems, recv_sems, local_sems):
        _handshake(peers_fn())
        copies_fn(srcs, dsts, send_sems, recv_sems, local_sems)

    launch()
    return [d[...] for d in dsts]


def _all_other_devices():
    x, y, c = _position()
    return [(1 - x if fx else x, 1 - y if fy else y, 1 - c if fc else c) for fx, fy, fc in FLIPS]


def _gather_peers():
    x, y, c = _position()
    return [(x, y, 1 - c), (1 - x, y, c), (x, 1 - y, c), (1 - x, 1 - y, c)]


def _scatter_blocks_async(parts, *, name, collective_id):
    return _sequencer_call(parts, [jax.ShapeDtypeStruct(p.shape, p.dtype) for p in parts],
                           _scatter_copies, _all_other_devices, name=name, collective_id=collective_id)


def _all_gather_async(shards, *, name, collective_id):
    return _sequencer_call(shards, [jax.ShapeDtypeStruct((N_DEV, *s.shape), s.dtype) for s in shards],
                           _gather_copies, _gather_peers, name=name, collective_id=collective_id)


def _adamw_math(w, g, m, v):
    m = ADAM_B1 * m + (1.0 - ADAM_B1) * g
    v = ADAM_B2 * v + (1.0 - ADAM_B2) * (g * g)
    m_hat = m / (1.0 - ADAM_B1 ** ADAM_STEP)
    v_hat = v / (1.0 - ADAM_B2 ** ADAM_STEP)
    delta = -ADAM_LR * (m_hat / (jnp.sqrt(v_hat) + ADAM_EPS) + ADAM_WD * w)
    return delta, m, v


def _adamw_reduce_call(recv, w, m, v, *, name, T=128):
    R, W = w.shape
    T = T if R % T == 0 else R // 2
    assert R % T == 0 and T % 16 == 0, (R, T)

    def body(p_ref, w_ref, m_ref, v_ref, g_out, d_out, m_out, v_out):
        g = p_ref[0].astype(F32)
        for d in range(1, N_DEV):
            g = g + p_ref[d].astype(F32)
        g_out[...] = g
        d_out[...], m_out[...], v_out[...] = _adamw_math(w_ref[...], g, m_ref[...], v_ref[...])

    row = pl.BlockSpec((T, W), lambda i: (i, 0))
    out = jax.ShapeDtypeStruct((R, W), F32)
    return pl.pallas_call(
        body, name=name, grid=(R // T,),
        in_specs=[pl.BlockSpec((N_DEV, T, W), lambda i: (0, i, 0)), row, row, row],
        out_specs=[row] * 4, out_shape=[out] * 4,
        compiler_params=_cp(("parallel",), VMEM_BIG),
    )(recv, w, m, v)


def _adamw_small_call(w, g, m, v, *, name):
    def body(w_ref, g_ref, m_ref, v_ref, d_out, m_out, v_out):
        d_out[...], m_out[...], v_out[...] = _adamw_math(w_ref[...], g_ref[...], m_ref[...], v_ref[...])

    out = jax.ShapeDtypeStruct(w.shape, F32)
    return pl.pallas_call(body, name=name, in_specs=[VMEM_SPEC] * 4, out_specs=[VMEM_SPEC] * 3,
                          out_shape=[out] * 3)(w, g, m, v)


def _tile_rows(n_elems):
    return -(-n_elems // (8 * LANES)) * 8


def _pack_rows(parts, rows):
    pieces = []
    for p in parts:
        q = p.reshape(-1, LANES)
        pieces.append(jnp.pad(q, ((0, _tile_rows(p.size) - q.shape[0]), (0, 0))))
    buf = jnp.concatenate(pieces, axis=0)
    return jnp.pad(buf, ((0, rows - buf.shape[0]), (0, 0)))


def _unpack_rows(buf, shapes):
    out, off = [], 0
    for shp in shapes:
        n = 1
        for s in shp:
            n *= s
        out.append(buf[off:off + n // LANES].reshape(shp))
        off += _tile_rows(n)
    return out


def _in_columns(blocks, lo, hi):
    pieces = []
    for d in range(N_DEV):
        s, e = max(lo, IN_BLK * d), min(hi, IN_BLK * (d + 1))
        if s < e:
            pieces.append(blocks[d][:, s - IN_BLK * d:e - IN_BLK * d])
    return jnp.concatenate(pieces, axis=1)


def _in_blocks(segments):
    blocks = []
    for d in range(N_DEV):
        lo, hi = IN_BLK * d, IN_BLK * (d + 1)
        pieces = []
        for arr, c0 in segments:
            s, e = max(lo, c0), min(hi, c0 + arr.shape[1])
            if s < e:
                pieces.append(arr[:, s - c0:e - c0])
        blocks.append(jnp.concatenate(pieces, axis=1))
    return jnp.stack(blocks)


def _local_step(x, target, W, scatter):
    g1, g2, g3, g4 = [W[n].reshape(1, D_MODEL) for n in ("norm_pre_mix", "norm_post_mix", "norm_pre_ffn", "norm_post_ffn")]
    w_a, w_low, w_s, w_g = W["w_in_a"], W["w_in_low"], W["w_in_s"], W["w_in_g"]
    wfi, wfo = W["w_ffn_in"], W["w_ffn_out"].reshape(4, FF_BLK, D_MODEL)
    wg = jnp.pad(W["w_gate_up"], ((0, LANES - RANK), (0, 0))).astype(BF16)
    bgate = W["b_gate"].reshape(1, QK)
    gnorm = W["gla_norm"].reshape(HEADS, 1, DV)
    ln_g = W["sgu_ln_g"].reshape(GROUPS, 1, DG)
    ln_b = W["sgu_ln_b"].reshape(GROUPS, 1, DG)
    w_sp = W["w_spatial"]
    b_sp = W["b_spatial"].reshape(GROUPS, SBLOCK, 1)

    a, pa, alow, ps, pg = _in_proj_call(x, g1, w_a, w_low, w_s, w_g, name="in_proj")
    y_gla, states = _gla_fwd_call(pa, alow, wg, bgate, gnorm, name="gla_fwd")
    y_sgu = _sgu_fwd_call(ps, ln_g, ln_b, w_sp, b_sp, name="sgu_fwd")
    t1, t2, merged, mix, x1, h = _mixer_tail_call(y_gla, y_sgu, pg, x, W["w_branch_gla"], W["w_branch_sgu"],
                                                  W["w_out"], g2, g3, name="mixer_tail")
    gu, z = _ffn_in_call(h, wfi, name="ffn_in")
    loss, dx2, dy, dg4 = _ffn_out_loss_call(z, wfo, x1, target, g4, name="ffn_out_loss")

    grads = {"norm_post_ffn": dg4}
    received = {}
    dgu = _ffn_out_bwd_call(dy, wfo, gu, name="ffn_out_bwd")
    dw_ffn_out = _weight_grads_call([z, dy], [(0, 1)], name="d_ffn_out_w")[0].reshape(N_DEV, D_FF // N_DEV, D_MODEL)
    dw_ffn_in = _ffn_in_grad_call(h, dgu, name="d_ffn_in_w")
    received.update(scatter(("w_ffn_out", "w_ffn_in"), [dw_ffn_out, dw_ffn_in]))
    dx1, dmix, grads["norm_pre_ffn"], grads["norm_post_mix"] = _ffn_in_bwd_call(dgu, wfi, dx2, x1, mix, g3, g2, name="ffn_in_bwd")
    dt1, dt2, dpg, dy_gla, dy_sgu = _mixer_bwd_call(dmix, t1, t2, pg, W["w_out"], W["w_branch_gla"], W["w_branch_sgu"],
                                                    name="mixer_bwd")
    rows = D_MODEL // N_DEV
    mixer_grads = _weight_grads_call([merged, dmix, y_gla, dt1, y_sgu, dt2], [(0, 1), (2, 3), (4, 5)], name="d_mixer_w")
    received.update(scatter(("w_out", "w_branch_gla", "w_branch_sgu"),
                            [g.reshape(N_DEV, rows, D_MODEL) for g in mixer_grads]))
    dps, dlg, dlb, dwsp, dbsp = _sgu_bwd_call(ps, ln_g, ln_b, w_sp, b_sp, dy_sgu, name="sgu_bwd")
    dpa, dlogit, dgn, dbg = _gla_bwd_call(pa, alow, wg, bgate, gnorm, states, dy_gla, name="gla_bwd")
    dlow = _mm(dlogit, wg, "nt", BF16, name="d_gate_up_x")
    dw_a, dw_low = _weight_grads_call([a, dpa, dlow], [(0, 1), (0, 2)], name="d_in_w_qkvr")
    dw_s, dw_g = _weight_grads_call([a, dps, dpg], [(0, 1), (0, 2)], name="d_in_w_sgu_gates")
    dw_in = _in_blocks([(dw_a, A_COLS[0]), (dw_low[:, :RANK], LOW_COLS[0]), (dw_s, S_COLS[0]), (dw_g, G_COLS[0])])
    received.update(scatter(("w_in",), [dw_in]))
    dwg = _mm(alow, dlogit, "tn", F32, name="d_gate_up_w")
    grad_x, grads["norm_pre_mix"] = _in_proj_bwd_call(dpa, dlow, dps, dpg, w_a, w_low, w_s, w_g, x, dx1, g1,
                                                      name="in_proj_bwd")

    grads["w_gate_up"] = dwg[:RANK]
    grads["b_gate"] = dbg
    grads["gla_norm"] = dgn
    grads["sgu_ln_g"] = dlg
    grads["sgu_ln_b"] = dlb
    grads["w_spatial"] = dwsp
    grads["b_spatial"] = dbsp
    return loss, grad_x, grads, received


WEIGHTS = ("norm_pre_mix", "w_in", "w_gate_up", "b_gate", "gla_norm", "sgu_ln_g", "sgu_ln_b", "w_spatial",
           "b_spatial", "w_branch_gla", "w_branch_sgu", "w_out", "norm_post_mix", "norm_pre_ffn", "w_ffn_in",
           "w_ffn_out", "norm_post_ffn")
BIG = ("w_in", "w_branch_gla", "w_branch_sgu", "w_out", "w_ffn_in", "w_ffn_out")
MIXER = ("w_branch_gla", "w_branch_sgu", "w_out")
FFN = ("w_ffn_in", "w_ffn_out")
SMALL = tuple(n for n in WEIGHTS if n not in BIG)
SMALL_SHARDED = ("w_gate_up", "gla_norm", "sgu_ln_g", "sgu_ln_b")
SMALL_FULL = {"norm_pre_mix": (1024,), "w_gate_up": (16, 512), "b_gate": (512,), "gla_norm": (4, 256),
              "sgu_ln_g": (4, 256), "sgu_ln_b": (4, 256), "w_spatial": (4, 128, 128), "b_spatial": (4, 128),
              "norm_post_mix": (1024,), "norm_pre_ffn": (1024,), "norm_post_ffn": (1024,)}
SMALL_GRAD_ROWS = 648
SMALL_STATE_ROWS = 600
SMALL_GATHER_ROWS = 32


def kernel(x, norm_pre_mix, w_in, w_gate_up, b_gate, gla_norm, sgu_ln_g, sgu_ln_b, w_spatial, b_spatial, w_branch_gla, w_branch_sgu, w_out, norm_post_mix, norm_pre_ffn, w_ffn_in, w_ffn_out, norm_post_ffn, loss_target, m_norm_pre_mix, m_w_in, m_w_gate_up, m_b_gate, m_gla_norm, m_sgu_ln_g, m_sgu_ln_b, m_w_spatial, m_b_spatial, m_w_branch_gla, m_w_branch_sgu, m_w_out, m_norm_post_mix, m_norm_pre_ffn, m_w_ffn_in, m_w_ffn_out, m_norm_post_ffn, v_norm_pre_mix, v_w_in, v_w_gate_up, v_b_gate, v_gla_norm, v_sgu_ln_g, v_sgu_ln_b, v_w_spatial, v_b_spatial, v_w_branch_gla, v_w_branch_sgu, v_w_out, v_norm_post_mix, v_norm_pre_ffn, v_w_ffn_in, v_w_ffn_out, v_norm_post_ffn):
    given = dict(locals())
    w = {n: given[n][0] for n in WEIGHTS}
    m = {n: given["m_" + n][0] for n in WEIGHTS}
    v = {n: given["v_" + n][0] for n in WEIGHTS}
    xs, target = x[0], loss_target[0]
    me = 4 * lax.axis_index("x") + 2 * lax.axis_index("y") + lax.axis_index("c")

    small_shard = _pack_rows([w[n] for n in SMALL_SHARDED], SMALL_GATHER_ROWS)
    first = _all_gather_hbm([w["w_in"].astype(BF16), small_shard], name="gather_w_in")
    rest, first = lax.optimization_barrier(([w[n].astype(BF16) for n in MIXER + FFN], first))
    rest_blocks = _all_gather_async(rest, name="gather_rest", collective_id=1)
    W = {n: w[n] for n in SMALL if n not in SMALL_SHARDED}
    blocks = {"w_in": first[0], **dict(zip(MIXER + FFN, rest_blocks))}
    for n in ("w_branch_gla", "w_branch_sgu", "w_out", "w_ffn_out"):
        W[n] = blocks[n].reshape(-1, D_MODEL)
    W["w_ffn_in"] = blocks["w_ffn_in"]
    W["w_in_a"] = _in_columns(blocks["w_in"], *A_COLS)
    W["w_in_low"] = jnp.pad(_in_columns(blocks["w_in"], *LOW_COLS), ((0, 0), (0, LANES - RANK)))
    W["w_in_s"] = _in_columns(blocks["w_in"], *S_COLS)
    W["w_in_g"] = _in_columns(blocks["w_in"], *G_COLS)
    small_blocks = first[1]
    off = 0
    for n in SMALL_SHARDED:
        r, c = w[n].shape
        blk = small_blocks[:, off:off + r * c // LANES].reshape(N_DEV, r, c)
        W[n] = blk.transpose(1, 0, 2).reshape(r, N_DEV * c)
        off += _tile_rows(r * c)

    scatter_ids = iter((3, 4, 5))

    def scatter(names, parts):
        got = _scatter_blocks_async(parts, name="scatter_" + "_".join(names), collective_id=next(scatter_ids))
        return dict(zip(names, got))

    loss_part, grad_x, grads, received = _local_step(xs, target, W, scatter)

    small_part = jnp.concatenate([_pack_rows([grads[n] for n in SMALL], SMALL_GRAD_ROWS),
                                  jnp.broadcast_to(loss_part, (8, LANES))], axis=0)
    small_sum = _all_reduce_small(small_part, name="reduce_small")
    loss = small_sum[SMALL_GRAD_ROWS, 0]
    g_small = dict(zip(SMALL, _unpack_rows(small_sum, [SMALL_FULL[n] for n in SMALL])))
    for n in SMALL_SHARDED:
        c = w[n].shape[1]
        g_small[n] = lax.dynamic_slice_in_dim(g_small[n], me * c, c, axis=1)

    g_big, d_big, m_big, v_big = {}, {}, {}, {}
    for n in BIG:
        g_big[n], d_big[n], m_big[n], v_big[n] = _adamw_reduce_call(received[n], w[n], m[n], v[n], name="adamw_" + n)
    small_shapes = [w[n].shape for n in SMALL]
    small_state = [_pack_rows([s[n] for n in SMALL], SMALL_STATE_ROWS) for s in (w, g_small, m, v)]
    small_out = _adamw_small_call(*small_state, name="adamw_small")
    d_small, m_small, v_small = [dict(zip(SMALL, _unpack_rows(o, small_shapes))) for o in small_out]

    def pick(big, sml):
        return [(big[n] if n in BIG else sml[n])[None] for n in WEIGHTS]

    return (loss, grad_x[None], *pick(g_big, g_small), *pick(d_big, d_small), *pick(m_big, m_small),
            *pick(v_big, v_small))
```

```python
import jax
import jax.numpy as jnp
from jax import lax
from jax.experimental import pallas as pl
from jax.experimental.pallas import tpu as pltpu
from jax.experimental.pallas import tpu_sc as plsc

F32 = jnp.float32
BF16 = jnp.bfloat16

D_MODEL = 1024
N_DEV = 8
CHUNK = 64
HEADS = 4
DK = 128
DV = 256
QK = HEADS * DK
GV = HEADS * DV
RANK = 16
GROUPS = 4
SBLOCK = 128
DG = 256
D_FF = 2816
FF_BLK = 704
EPS = 1e-6
Q_SCALE = DK ** -0.5
LANES = 128
VMEM_BIG = 48 * 1024 * 1024

D_IN = 7184
IN_BLK = 898
A_COLS = (0, 3072)
LOW_COLS = (3072, 3088)
S_COLS = (3088, 5136)
G_COLS = (5136, 7184)

ADAM_LR = 0.001
ADAM_B1 = 0.9
ADAM_B2 = 0.999
ADAM_EPS = 1e-08
ADAM_WD = 0.01
ADAM_STEP = 10

MESH = pl.DeviceIdType.MESH
ANY = pl.BlockSpec(memory_space=pl.ANY)
VMEM_SPEC = pl.BlockSpec(memory_space=pltpu.VMEM)


def _cp(sem=None, vmem=None):
    return pltpu.CompilerParams(dimension_semantics=sem, vmem_limit_bytes=vmem)


def _sig(x):
    return 1.0 / (1.0 + jnp.exp(-x))


def _gelu(x):
    c = 0.7978845608028654
    t = jnp.tanh(c * (x + 0.044715 * x * x * x))
    return 0.5 * x * (1.0 + t)


def _gelu_grad(x):
    c = 0.7978845608028654
    x2 = x * x
    t = jnp.tanh(c * (x + 0.044715 * x * x2))
    return 0.5 * (1.0 + t) + 0.5 * x * (1.0 - t * t) * c * (1.0 + 3.0 * 0.044715 * x2)


def _logsig(x):
    return jnp.minimum(x, 0.0) - jnp.log1p(jnp.exp(-jnp.abs(x)))


def _dot(a, b, dims):
    return lax.dot_general(a, b, (dims, ((), ())), preferred_element_type=F32)


NN = ((1,), (0,))
NT = ((1,), (1,))
TN = ((0,), (0,))


def _exact_mask_dot(mask_bf16, x):
    hi = x.astype(BF16)
    r1 = x - hi.astype(F32)
    mid = r1.astype(BF16)
    lo = (r1 - mid.astype(F32)).astype(BF16)
    return _dot(mask_bf16, hi, NN) + _dot(mask_bf16, mid, NN) + _dot(mask_bf16, lo, NN)


def _pick_tile(dim, target):
    if dim <= target:
        return dim
    best = None
    for t in range(LANES, int(1.4 * target) + 1, LANES):
        if dim % t == 0:
            best = t
    assert best is not None, (dim, target)
    return best


def _mm_call(a, b, *, name, grid, a_spec, b_spec, o_spec, out_shape, dims, acc_shape):
    nk = grid[2]

    def body(a_ref, b_ref, o_ref, *acc):
        part = _dot(a_ref[...], b_ref[...], dims)
        if nk == 1:
            o_ref[...] = part.astype(o_ref.dtype)
        else:
            acc_ref = acc[0]
            k = pl.program_id(2)

            @pl.when(k == 0)
            def _():
                acc_ref[...] = part

            @pl.when(k > 0)
            def _():
                acc_ref[...] += part

            @pl.when(k == nk - 1)
            def _():
                o_ref[...] = acc_ref[...].astype(o_ref.dtype)

    return pl.pallas_call(
        body, name=name, grid=grid, in_specs=[a_spec, b_spec], out_specs=o_spec, out_shape=out_shape,
        scratch_shapes=[] if nk == 1 else [pltpu.VMEM(acc_shape, F32)],
        compiler_params=_cp(("parallel", "parallel", "arbitrary"), VMEM_BIG),
    )(a, b)


def _mm(a, b, mode, out_dtype, *, name, tm=512, tn=1024, tk=1024):
    if mode == "nn":
        (M, K), (_, N) = a.shape, b.shape
    elif mode == "nt":
        (M, K), (N, _) = a.shape, b.shape
    else:
        (K, M), (_, N) = a.shape, b.shape
    tm, tn, tk = _pick_tile(M, tm), _pick_tile(N, tn), _pick_tile(K, tk)
    if mode == "nn":
        a_spec = pl.BlockSpec((tm, tk), lambda i, j, k: (i, k))
        b_spec = pl.BlockSpec((tk, tn), lambda i, j, k: (k, j))
        dims = NN
    elif mode == "nt":
        a_spec = pl.BlockSpec((tm, tk), lambda i, j, k: (i, k))
        b_spec = pl.BlockSpec((tn, tk), lambda i, j, k: (j, k))
        dims = NT
    else:
        a_spec = pl.BlockSpec((tk, tm), lambda i, j, k: (k, i))
        b_spec = pl.BlockSpec((tk, tn), lambda i, j, k: (k, j))
        dims = TN
    return _mm_call(a, b, name=name, grid=(M // tm, N // tn, K // tk), a_spec=a_spec, b_spec=b_spec,
                    o_spec=pl.BlockSpec((tm, tn), lambda i, j, k: (i, j)),
                    out_shape=jax.ShapeDtypeStruct((M, N), out_dtype), dims=dims, acc_shape=(tm, tn))


ROW_TILE = 512


def _rt(T, W, c=0):
    return pl.BlockSpec((T, W), lambda i: (i, c))


def _rt3(nb, T, W):
    return pl.BlockSpec((nb, T, W), lambda i: (0, i, 0))


def _res(shape):
    nd = len(shape)
    return pl.BlockSpec(tuple(shape), lambda i: (0,) * nd, pipeline_mode=pl.Buffered(1))


def _acc(shape):
    nd = len(shape)
    return pl.BlockSpec(tuple(shape), lambda i: (0,) * nd, pipeline_mode=pl.Buffered(1))


def _nbytes(shape, dtype):
    n = jnp.dtype(dtype).itemsize
    for s in shape:
        n *= s
    return n


def _vmem_limit(tiles, resident, temps=16 * 1024 * 1024):
    need = 2 * sum(_nbytes(s, d) for s, d in tiles) + sum(_nbytes(s, d) for s, d in resident) + temps
    return min(need, 60 * 1024 * 1024)


def _tok_call(body, *, name, S, T, ins, outs, semantics="parallel"):
    tiles = [(spec.block_shape, a.dtype) for a, spec, kind in ins + outs if kind == "tile"]
    resident = [(a.shape, a.dtype) for a, spec, kind in ins + outs if kind == "res"]
    return pl.pallas_call(
        body, name=name, grid=(S // T,),
        in_specs=[spec for _, spec, _ in ins], out_specs=[spec for _, spec, _ in outs],
        out_shape=[jax.ShapeDtypeStruct(a.shape, a.dtype) for a, _, _ in outs],
        compiler_params=_cp((semantics,), _vmem_limit(tiles, resident)),
    )(*[a for a, _, _ in ins])


def _tile(a, spec):
    return (a, spec, "tile")


def _whole(a):
    return (a, _res(a.shape), "res")


def _out_tile(shape, dtype, spec):
    return (jax.ShapeDtypeStruct(shape, dtype), spec, "tile")


def _out_acc(shape, dtype=F32):
    return (jax.ShapeDtypeStruct(shape, dtype), _acc(shape), "res")


def _rms_stats(x):
    r = lax.rsqrt(jnp.mean(x * x, axis=-1, keepdims=True) + EPS)
    return r, x * r


def _rms_bwd(xh, r, g, dy):
    dxh = dy * g
    dx = r * (dxh - xh * jnp.mean(dxh * xh, axis=-1, keepdims=True))
    dg = jnp.sum(dy * xh, axis=0, keepdims=True)
    return dx, dg


def _accum(ref, val):
    @pl.when(pl.program_id(0) == 0)
    def _():
        ref[...] = val

    @pl.when(pl.program_id(0) > 0)
    def _():
        ref[...] += val


def _dot_cols(a, w_ref, o_ref, dims, chunk=1024):
    n = o_ref.shape[1]
    for n0 in range(0, n, chunk):
        n1 = min(n, n0 + chunk)
        w = w_ref[:, n0:n1] if dims == NN else w_ref[n0:n1, :]
        o_ref[:, n0:n1] = _dot(a, w, dims).astype(o_ref.dtype)


def _in_proj_call(x, g1, w_a, w_low, w_s, w_g, *, name, T=ROW_TILE):
    S = x.shape[0]

    def body(x_ref, g_ref, wa_ref, wl_ref, ws_ref, wg_ref, a_ref, pa_ref, al_ref, ps_ref, pg_ref):
        _, xh = _rms_stats(x_ref[...])
        a = (xh * g_ref[...]).astype(BF16)
        a_ref[...] = a
        _dot_cols(a, wa_ref, pa_ref, NN)
        _dot_cols(a, wl_ref, al_ref, NN)
        _dot_cols(a, ws_ref, ps_ref, NN)
        _dot_cols(a, wg_ref, pg_ref, NN)

    widths = (D_MODEL, w_a.shape[1], w_low.shape[1], w_s.shape[1], w_g.shape[1])
    return _tok_call(
        body, name=name, S=S, T=T,
        ins=[_tile(x, _rt(T, D_MODEL)), _whole(g1), _whole(w_a), _whole(w_low), _whole(w_s), _whole(w_g)],
        outs=[_out_tile((S, w), BF16, _rt(T, w)) for w in widths])


def _mixer_tail_call(y_gla, y_sgu, pg, x, w_bg, w_bs, w_out, g2, g3, *, name, T=ROW_TILE):
    S = x.shape[0]

    def body(yg_ref, ys_ref, pg_ref, x_ref, wbg_ref, wbs_ref, wo_ref, g2_ref, g3_ref,
             t1_ref, t2_ref, mg_ref, mix_ref, x1_ref, h_ref):
        t1 = _dot(yg_ref[...], wbg_ref[...], NN)
        t2 = _dot(ys_ref[...], wbs_ref[...], NN)
        t1_ref[...] = t1.astype(BF16)
        t2_ref[...] = t2.astype(BF16)
        sg = _sig(pg_ref[:, :D_MODEL].astype(F32))
        ss = _sig(pg_ref[:, D_MODEL:].astype(F32))
        merged = (sg * t1 + ss * t2).astype(BF16)
        mg_ref[...] = merged
        mix = _dot(merged, wo_ref[...], NN)
        mix_ref[...] = mix
        _, mh = _rms_stats(mix)
        x1 = x_ref[...] + mh * g2_ref[...]
        x1_ref[...] = x1
        _, xh = _rms_stats(x1)
        h_ref[...] = (xh * g3_ref[...]).astype(BF16)

    row = _rt(T, D_MODEL)
    b16 = lambda: _out_tile((S, D_MODEL), BF16, row)
    f32 = lambda: _out_tile((S, D_MODEL), F32, row)
    return _tok_call(
        body, name=name, S=S, T=T,
        ins=[_tile(y_gla, row), _tile(y_sgu, row), _tile(pg, _rt(T, 2 * D_MODEL)), _tile(x, row),
             _whole(w_bg), _whole(w_bs), _whole(w_out), _whole(g2), _whole(g3)],
        outs=[b16(), b16(), b16(), f32(), f32(), b16()])


def _ffn_in_call(h, wfi, *, name, T=ROW_TILE):
    S = h.shape[0]

    def body(h_ref, w_ref, gu_ref, z_ref):
        hv = h_ref[...]
        for d in range(4):
            gate = _dot(hv, w_ref[d], NN)
            up = _dot(hv, w_ref[d + 4], NN)
            gu_ref[d] = gate.astype(BF16)
            gu_ref[d + 4] = up.astype(BF16)
            z_ref[d] = (gate * _sig(gate) * up).astype(BF16)

    return _tok_call(
        body, name=name, S=S, T=T,
        ins=[_tile(h, _rt(T, D_MODEL)), _whole(wfi)],
        outs=[_out_tile((N_DEV, S, FF_BLK), BF16, _rt3(N_DEV, T, FF_BLK)),
              _out_tile((4, S, FF_BLK), BF16, _rt3(4, T, FF_BLK))])


def _ffn_out_loss_call(z, wfo, x1, target, g4, *, name, T=ROW_TILE):
    S = x1.shape[0]

    def body(z_ref, w_ref, x1_ref, t_ref, g4_ref, loss_ref, dx2_ref, dy_ref, dg4_ref):
        y = _dot(z_ref[0], w_ref[0], NN)
        for d in range(1, 4):
            y = y + _dot(z_ref[d], w_ref[d], NN)
        r, yh = _rms_stats(y)
        diff = x1_ref[...] + yh * g4_ref[...] - t_ref[...]
        part = 0.5 * jnp.sum(jnp.mean(diff * diff, axis=-1, keepdims=True), axis=0, keepdims=True)
        _accum(loss_ref, jnp.broadcast_to(part, (1, LANES)))
        dx2 = diff * (1.0 / D_MODEL)
        dx2_ref[...] = dx2
        dy, dg = _rms_bwd(yh, r, g4_ref[...], dx2)
        dy_ref[...] = dy.astype(BF16)
        _accum(dg4_ref, dg)

    row = _rt(T, D_MODEL)
    return _tok_call(
        body, name=name, S=S, T=T, semantics="arbitrary",
        ins=[_tile(z, _rt3(4, T, FF_BLK)), _whole(wfo), _tile(x1, row), _tile(target, row), _whole(g4)],
        outs=[_out_acc((1, LANES)), _out_tile((S, D_MODEL), F32, row), _out_tile((S, D_MODEL), BF16, row),
              _out_acc((1, D_MODEL))])


def _ffn_out_bwd_call(dy, wfo, gu, *, name, T=ROW_TILE):
    S = dy.shape[0]

    def body(dy_ref, w_ref, gu_ref, dgu_ref):
        dyv = dy_ref[...]
        for d in range(4):
            dz = _dot(dyv, w_ref[d], NT)
            gt = gu_ref[d].astype(F32)
            up = gu_ref[d + 4].astype(F32)
            s = _sig(gt)
            dgu_ref[d] = (dz * up * s * (1.0 + gt * (1.0 - s))).astype(BF16)
            dgu_ref[d + 4] = (dz * gt * s).astype(BF16)

    blocks = _rt3(N_DEV, T, FF_BLK)
    return _tok_call(
        body, name=name, S=S, T=T,
        ins=[_tile(dy, _rt(T, D_MODEL)), _whole(wfo), _tile(gu, blocks)],
        outs=[_out_tile((N_DEV, S, FF_BLK), BF16, blocks)])[0]


def _ffn_in_bwd_call(dgu, wfi, dx2, x1, mix, g3, g2, *, name, T=ROW_TILE):
    S = x1.shape[0]

    def body(dgu_ref, w_ref, dx2_ref, x1_ref, mix_ref, g3_ref, g2_ref, dx1_ref, dmix_ref, dg3_ref, dg2_ref):
        dh = _dot(dgu_ref[0], w_ref[0], NT)
        for d in range(1, N_DEV):
            dh = dh + _dot(dgu_ref[d], w_ref[d], NT)
        r3, xh = _rms_stats(x1_ref[...])
        d3, dg3 = _rms_bwd(xh, r3, g3_ref[...], dh)
        dx1 = dx2_ref[...] + d3
        dx1_ref[...] = dx1
        r2, mh = _rms_stats(mix_ref[...])
        dmix, dg2 = _rms_bwd(mh, r2, g2_ref[...], dx1)
        dmix_ref[...] = dmix.astype(BF16)
        _accum(dg3_ref, dg3)
        _accum(dg2_ref, dg2)

    row = _rt(T, D_MODEL)
    return _tok_call(
        body, name=name, S=S, T=T, semantics="arbitrary",
        ins=[_tile(dgu, _rt3(N_DEV, T, FF_BLK)), _whole(wfi), _tile(dx2, row), _tile(x1, row), _tile(mix, row),
             _whole(g3), _whole(g2)],
        outs=[_out_tile((S, D_MODEL), F32, row), _out_tile((S, D_MODEL), BF16, row),
              _out_acc((1, D_MODEL)), _out_acc((1, D_MODEL))])


def _mixer_bwd_call(dmix, t1, t2, pg, w_out, w_bg, w_bs, *, name, T=ROW_TILE):
    S = dmix.shape[0]

    def body(dmix_ref, t1_ref, t2_ref, pg_ref, wo_ref, wbg_ref, wbs_ref, dt1_ref, dt2_ref, dpg_ref, dyg_ref, dys_ref):
        dm = _dot(dmix_ref[...], wo_ref[...], NT)
        sg = _sig(pg_ref[:, :D_MODEL].astype(F32))
        ss = _sig(pg_ref[:, D_MODEL:].astype(F32))
        dt1 = (dm * sg).astype(BF16)
        dt2 = (dm * ss).astype(BF16)
        dt1_ref[...] = dt1
        dt2_ref[...] = dt2
        dpg_ref[:, :D_MODEL] = (dm * t1_ref[...].astype(F32) * sg * (1.0 - sg)).astype(BF16)
        dpg_ref[:, D_MODEL:] = (dm * t2_ref[...].astype(F32) * ss * (1.0 - ss)).astype(BF16)
        dyg_ref[...] = _dot(dt1, wbg_ref[...], NT).astype(BF16)
        dys_ref[...] = _dot(dt2, wbs_ref[...], NT).astype(BF16)

    row = _rt(T, D_MODEL)
    wide = _rt(T, 2 * D_MODEL)
    b16 = lambda: _out_tile((S, D_MODEL), BF16, row)
    return _tok_call(
        body, name=name, S=S, T=T,
        ins=[_tile(dmix, row), _tile(t1, row), _tile(t2, row), _tile(pg, wide), _whole(w_out), _whole(w_bg), _whole(w_bs)],
        outs=[b16(), b16(), _out_tile((S, 2 * D_MODEL), BF16, wide), b16(), b16()])


def _in_proj_bwd_call(dpa, dlow, dps, dpg, w_a, w_low, w_s, w_g, x, dx1, g1, *, name, T=ROW_TILE):
    S = x.shape[0]

    def body(dpa_ref, dl_ref, dps_ref, dpg_ref, wa_ref, wl_ref, ws_ref, wg_ref, x_ref, dx1_ref, g1_ref, gx_ref, dg1_ref):
        da = (_dot(dpa_ref[...], wa_ref[...], NT) + _dot(dl_ref[...], wl_ref[...], NT)
              + _dot(dps_ref[...], ws_ref[...], NT) + _dot(dpg_ref[...], wg_ref[...], NT))
        r, xh = _rms_stats(x_ref[...])
        dxa, dg = _rms_bwd(xh, r, g1_ref[...], da)
        gx_ref[...] = dx1_ref[...] + dxa
        _accum(dg1_ref, dg)

    row = _rt(T, D_MODEL)
    return _tok_call(
        body, name=name, S=S, T=T, semantics="arbitrary",
        ins=[_tile(dpa, _rt(T, dpa.shape[1])), _tile(dlow, _rt(T, dlow.shape[1])), _tile(dps, _rt(T, dps.shape[1])),
             _tile(dpg, _rt(T, dpg.shape[1])), _whole(w_a), _whole(w_low), _whole(w_s), _whole(w_g),
             _tile(x, row), _tile(dx1, row), _whole(g1)],
        outs=[_out_tile((S, D_MODEL), F32, row), _out_acc((1, D_MODEL))])


TOKEN_TILE = 512


def _weight_grads_call(arrays, pairs, *, name, tk=TOKEN_TILE):
    S = arrays[0].shape[-2]
    tk = min(tk, S)
    n_in = len(arrays)

    def out_shape(i, j):
        a, b = arrays[i], arrays[j]
        if a.ndim == 3:
            return (a.shape[0], a.shape[2], b.shape[1])
        if b.ndim == 3:
            return (b.shape[0], a.shape[1], b.shape[2])
        return (a.shape[1], b.shape[1])

    shapes = [out_shape(i, j) for i, j in pairs]

    def body(*refs):
        ins, outs, accs = refs[:n_in], refs[n_in:n_in + len(pairs)], refs[n_in + len(pairs):]
        k = pl.program_id(0)

        @pl.when(k == 0)
        def _():
            for acc in accs:
                acc[...] = jnp.zeros_like(acc)

        for (i, j), acc in zip(pairs, accs):
            a_ref, b_ref = ins[i], ins[j]
            if len(a_ref.shape) == 3:
                for n in range(a_ref.shape[0]):
                    acc[n] += _dot(a_ref[n], b_ref[...], TN)
            elif len(b_ref.shape) == 3:
                a = a_ref[...]
                for n in range(b_ref.shape[0]):
                    acc[n] += _dot(a, b_ref[n], TN)
            else:
                a = a_ref[...]
                for n0 in range(0, b_ref.shape[1], 1024):
                    n1 = min(b_ref.shape[1], n0 + 1024)
                    acc[:, n0:n1] += _dot(a, b_ref[:, n0:n1], TN)

        @pl.when(k == S // tk - 1)
        def _():
            for out, acc in zip(outs, accs):
                out[...] = acc[...].astype(out.dtype)

    def in_spec(a):
        if a.ndim == 3:
            return pl.BlockSpec((a.shape[0], tk, a.shape[2]), lambda k: (0, k, 0))
        return pl.BlockSpec((tk, a.shape[1]), lambda k: (k, 0))

    tiles = [(in_spec(a).block_shape, a.dtype) for a in arrays]
    resident = [(s, F32) for s in shapes] + [(s, BF16) for s in shapes]
    return pl.pallas_call(
        body, name=name, grid=(S // tk,),
        in_specs=[in_spec(a) for a in arrays],
        out_specs=[_acc(s) for s in shapes],
        out_shape=[jax.ShapeDtypeStruct(s, BF16) for s in shapes],
        scratch_shapes=[pltpu.VMEM(s, F32) for s in shapes],
        compiler_params=_cp(("arbitrary",), _vmem_limit(tiles, resident)),
    )(*arrays)


def _ffn_in_grad_call(h, dgu, *, name, tk=TOKEN_TILE):
    S = h.shape[0]
    tk = min(tk, S)
    nb = 4
    shape = (nb, D_MODEL, FF_BLK)

    def body(h_ref, dgu_ref, out_ref, acc):
        k = pl.program_id(1)

        @pl.when(k == 0)
        def _():
            acc[...] = jnp.zeros_like(acc)

        hv = h_ref[...]
        for n in range(nb):
            acc[n] += _dot(hv, dgu_ref[n], TN)

        @pl.when(k == S // tk - 1)
        def _():
            out_ref[...] = acc[...].astype(out_ref.dtype)

    tiles = [((tk, D_MODEL), BF16), ((nb, tk, FF_BLK), BF16), (shape, BF16)]
    return pl.pallas_call(
        body, name=name, grid=(N_DEV // nb, S // tk),
        in_specs=[pl.BlockSpec((tk, D_MODEL), lambda g, k: (k, 0)),
                  pl.BlockSpec((nb, tk, FF_BLK), lambda g, k: (g, k, 0))],
        out_specs=pl.BlockSpec(shape, lambda g, k: (g, 0, 0)),
        out_shape=jax.ShapeDtypeStruct((N_DEV, D_MODEL, FF_BLK), BF16),
        scratch_shapes=[pltpu.VMEM(shape, F32)],
        compiler_params=_cp(("parallel", "arbitrary"), _vmem_limit(tiles, [(shape, F32)])),
    )(h, dgu)


GLA_TILE = 256
Q_OFF, K_OFF, V_OFF, R_OFF = 0, QK, 2 * QK, 2 * QK + GV


def _tri(lower):
    r = lax.broadcasted_iota(jnp.int32, (CHUNK, CHUNK), 0)
    c = lax.broadcasted_iota(jnp.int32, (CHUNK, CHUNK), 1)
    return jnp.where((r >= c) if lower else (c >= r), 1.0, 0.0).astype(BF16)


def _gla_fwd_call(pa, alow, wg, bgate, gnorm, *, name):
    S = pa.shape[0]
    Tg = min(GLA_TILE, S)
    cb = Tg // CHUNK

    def body(pa_ref, al_ref, wg_ref, bg_ref, gn_ref, y_ref, st_ref, state):
        @pl.when(pl.program_id(0) == 0)
        def _():
            state[...] = jnp.zeros_like(state)

        logit = _dot(al_ref[...], wg_ref[...], NN) + bg_ref[...]
        ls = _logsig(logit) * (1.0 / 16.0)
        tri = _tri(True)
        for h in range(HEADS):
            for c in range(cb):
                rows = pl.ds(c * CHUNK, CHUNK)
                cum = _exact_mask_dot(tri, ls[c * CHUNK:(c + 1) * CHUNK, h * DK:(h + 1) * DK])
                tot = cum[CHUNK - 1:CHUNK]
                kd = (pa_ref[rows, pl.ds(K_OFF + h * DK, DK)].astype(F32) * jnp.exp(tot - cum)).astype(BF16)
                new = state[h] * jnp.exp(tot) + _dot(pa_ref[rows, pl.ds(V_OFF + h * DV, DV)], kd, TN)
                state[h] = new
                st_ref[c, h] = new
                qs = (pa_ref[rows, pl.ds(Q_OFF + h * DK, DK)].astype(F32) * Q_SCALE).astype(BF16)
                o = _dot(qs, new.astype(BF16), NT)
                rs = lax.rsqrt(jnp.mean(o * o, axis=-1, keepdims=True) + EPS)
                rr = pa_ref[rows, pl.ds(R_OFF + h * DV, DV)].astype(F32)
                y_ref[rows, pl.ds(h * DV, DV)] = (o * rs * gn_ref[h] * (rr * _sig(rr))).astype(BF16)

    return pl.pallas_call(
        body, name=name, grid=(S // Tg,),
        in_specs=[
            pl.BlockSpec((Tg, 2 * QK + 2 * GV), lambda t: (t, 0)),
            pl.BlockSpec((Tg, LANES), lambda t: (t, 0)),
            pl.BlockSpec((LANES, QK), lambda t: (0, 0)),
            pl.BlockSpec((1, QK), lambda t: (0, 0)),
            pl.BlockSpec((HEADS, 1, DV), lambda t: (0, 0, 0)),
        ],
        out_specs=[
            pl.BlockSpec((Tg, GV), lambda t: (t, 0)),
            pl.BlockSpec((cb, HEADS, DV, DK), lambda t: (t, 0, 0, 0)),
        ],
        out_shape=[jax.ShapeDtypeStruct((S, GV), BF16),
                   jax.ShapeDtypeStruct((S // CHUNK, HEADS, DV, DK), F32)],
        scratch_shapes=[pltpu.VMEM((HEADS, DV, DK), F32)],
        compiler_params=_cp(("arbitrary",), VMEM_BIG),
    )(pa, alow, wg, bgate, gnorm)


def _gla_bwd_call(pa, alow, wg, bgate, gnorm, states, dy, *, name):
    S = pa.shape[0]
    Tg = min(GLA_TILE, S)
    cb = Tg // CHUNK
    nt = S // Tg

    def rev(t):
        return nt - 1 - t

    def body(pa_ref, al_ref, wg_ref, bg_ref, gn_ref, st_ref, prev_ref, dy_ref,
             dpa_ref, dl_ref, dgn_ref, dbg_ref, carry):
        t = pl.program_id(0)

        @pl.when(t == 0)
        def _():
            carry[...] = jnp.zeros_like(carry)
            dgn_ref[...] = jnp.zeros_like(dgn_ref)
            dbg_ref[...] = jnp.zeros_like(dbg_ref)

        logit = _dot(al_ref[...], wg_ref[...], NN) + bg_ref[...]
        ls = _logsig(logit) * (1.0 / 16.0)
        sneg = _sig(-logit)
        tri = _tri(True)
        upper = _tri(False)
        first_tile = rev(t) == 0
        for h in range(HEADS):
            gn = gn_ref[h]
            hk = pl.ds(h * DK, DK)
            dbg = jnp.zeros((1, DK), F32)
            dgn = jnp.zeros((1, DV), F32)
            for c in reversed(range(cb)):
                rows = pl.ds(c * CHUNK, CHUNK)
                cum = _exact_mask_dot(tri, ls[c * CHUNK:(c + 1) * CHUNK, h * DK:(h + 1) * DK])
                tot = cum[CHUNK - 1:CHUNK]
                w = jnp.exp(tot - cum)
                decay = jnp.exp(tot)
                kd = pa_ref[rows, pl.ds(K_OFF + h * DK, DK)].astype(F32) * w
                kd16 = kd.astype(BF16)
                st = st_ref[c, h]
                if c > 0:
                    st_prev = st_ref[c - 1, h]
                else:
                    st_prev = jnp.where(first_tile, 0.0, prev_ref[0, h])
                qs = (pa_ref[rows, pl.ds(Q_OFF + h * DK, DK)].astype(F32) * Q_SCALE).astype(BF16)
                st16 = st.astype(BF16)
                o = _dot(qs, st16, NT)
                rs = lax.rsqrt(jnp.mean(o * o, axis=-1, keepdims=True) + EPS)
                oh = o * rs
                rr = pa_ref[rows, pl.ds(R_OFF + h * DV, DV)].astype(F32)
                sr = _sig(rr)
                dyv = dy_ref[rows, pl.ds(h * DV, DV)].astype(F32)
                dpa_ref[rows, pl.ds(R_OFF + h * DV, DV)] = (
                    dyv * oh * gn * sr * (1.0 + rr * (1.0 - sr))).astype(BF16)
                don = dyv * (rr * sr)
                dgn = dgn + jnp.sum(don * oh, axis=0, keepdims=True)
                doh = don * gn
                do16 = (rs * (doh - oh * jnp.mean(doh * oh, axis=-1, keepdims=True))).astype(BF16)
                dpa_ref[rows, pl.ds(Q_OFF + h * DK, DK)] = (_dot(do16, st16, NN) * Q_SCALE).astype(BF16)
                gt = _dot(do16, qs, TN) + carry[h]
                gt16 = gt.astype(BF16)
                dkd = _dot(pa_ref[rows, pl.ds(V_OFF + h * DV, DV)], gt16, NN)
                dpa_ref[rows, pl.ds(V_OFF + h * DV, DV)] = _dot(kd16, gt16, NT).astype(BF16)
                ddecay = jnp.sum(gt * st_prev, axis=0, keepdims=True)
                dpa_ref[rows, pl.ds(K_OFF + h * DK, DK)] = (dkd * w).astype(BF16)
                e = dkd * kd
                dtot = jnp.sum(e, axis=0, keepdims=True) + ddecay * decay
                dls = dtot - _exact_mask_dot(upper, e)
                dlogit = dls * (1.0 / 16.0) * sneg[c * CHUNK:(c + 1) * CHUNK, h * DK:(h + 1) * DK]
                dl_ref[rows, hk] = dlogit.astype(BF16)
                dbg = dbg + jnp.sum(dlogit, axis=0, keepdims=True)
                carry[h] = gt * decay
            dgn_ref[h] += dgn
            dbg_ref[:, hk] += dbg

    return pl.pallas_call(
        body, name=name, grid=(nt,),
        in_specs=[
            pl.BlockSpec((Tg, 2 * QK + 2 * GV), lambda t: (rev(t), 0)),
            pl.BlockSpec((Tg, LANES), lambda t: (rev(t), 0)),
            pl.BlockSpec((LANES, QK), lambda t: (0, 0)),
            pl.BlockSpec((1, QK), lambda t: (0, 0)),
            pl.BlockSpec((HEADS, 1, DV), lambda t: (0, 0, 0)),
            pl.BlockSpec((cb, HEADS, DV, DK), lambda t: (rev(t), 0, 0, 0)),
            pl.BlockSpec((1, HEADS, DV, DK), lambda t: (jnp.maximum(rev(t) * cb - 1, 0), 0, 0, 0)),
            pl.BlockSpec((Tg, GV), lambda t: (rev(t), 0)),
        ],
        out_specs=[
            pl.BlockSpec((Tg, 2 * QK + 2 * GV), lambda t: (rev(t), 0)),
            pl.BlockSpec((Tg, QK), lambda t: (rev(t), 0)),
            pl.BlockSpec((HEADS, 1, DV), lambda t: (0, 0, 0)),
            pl.BlockSpec((1, QK), lambda t: (0, 0)),
        ],
        out_shape=[jax.ShapeDtypeStruct((S, 2 * QK + 2 * GV), BF16), jax.ShapeDtypeStruct((S, QK), BF16),
                   jax.ShapeDtypeStruct((HEADS, 1, DV), F32), jax.ShapeDtypeStruct((1, QK), F32)],
        scratch_shapes=[pltpu.VMEM((HEADS, DV, DK), F32)],
        compiler_params=_cp(("arbitrary",), VMEM_BIG),
    )(pa, alow, wg, bgate, gnorm, states, states, dy)


SGU_TILE = 256


def _sgu_mask():
    r = lax.broadcasted_iota(jnp.int32, (SBLOCK, SBLOCK), 0)
    c = lax.broadcasted_iota(jnp.int32, (SBLOCK, SBLOCK), 1)
    return (c < CHUNK) | (r >= CHUNK)


def _ln_stats(vf):
    mu = jnp.mean(vf, axis=-1, keepdims=True)
    xc = vf - mu
    rs = lax.rsqrt(jnp.mean(xc * xc, axis=-1, keepdims=True) + EPS)
    return rs, xc * rs


def _sgu_fwd_call(ps, ln_g, ln_b, w_sp, b_sp, *, name):
    S = ps.shape[0]
    Ts = min(SGU_TILE, S)

    def body(ps_ref, lg_ref, lb_ref, w_ref, b_ref, y_ref):
        mask = _sgu_mask()
        for g in range(GROUPS):
            wm = jnp.where(mask, w_ref[g], 0.0).astype(BF16)
            for p in range(Ts // SBLOCK):
                rows = pl.ds(p * SBLOCK, SBLOCK)
                u = _gelu(ps_ref[rows, pl.ds(g * DG, DG)].astype(F32))
                _, xh = _ln_stats(_gelu(ps_ref[rows, pl.ds(D_MODEL + g * DG, DG)].astype(F32)))
                vn = xh * lg_ref[g] + lb_ref[g]
                mixed = _dot(wm, vn.astype(BF16), NN) + b_ref[g]
                y_ref[rows, pl.ds(g * DG, DG)] = (u * mixed).astype(BF16)

    full3 = lambda a, b, c: pl.BlockSpec((a, b, c), lambda t: (0, 0, 0))
    return pl.pallas_call(
        body, name=name, grid=(S // Ts,),
        in_specs=[pl.BlockSpec((Ts, 2 * D_MODEL), lambda t: (t, 0)),
                  full3(GROUPS, 1, DG), full3(GROUPS, 1, DG), full3(GROUPS, SBLOCK, SBLOCK), full3(GROUPS, SBLOCK, 1)],
        out_specs=pl.BlockSpec((Ts, D_MODEL), lambda t: (t, 0)),
        out_shape=jax.ShapeDtypeStruct((S, D_MODEL), BF16),
        compiler_params=_cp(("parallel",)),
    )(ps, ln_g, ln_b, w_sp, b_sp)


def _sgu_bwd_call(ps, ln_g, ln_b, w_sp, b_sp, dy, *, name):
    S = ps.shape[0]
    Ts = min(SGU_TILE, S)

    def body(ps_ref, lg_ref, lb_ref, w_ref, b_ref, dy_ref, ds_ref, dlg_ref, dlb_ref, dw_ref, db_ref):
        @pl.when(pl.program_id(0) == 0)
        def _():
            dlg_ref[...] = jnp.zeros_like(dlg_ref)
            dlb_ref[...] = jnp.zeros_like(dlb_ref)
            dw_ref[...] = jnp.zeros_like(dw_ref)
            db_ref[...] = jnp.zeros_like(db_ref)

        mask = _sgu_mask()
        for g in range(GROUPS):
            wm = jnp.where(mask, w_ref[g], 0.0).astype(BF16)
            lg = lg_ref[g]
            for p in range(Ts // SBLOCK):
                rows = pl.ds(p * SBLOCK, SBLOCK)
                su = ps_ref[rows, pl.ds(g * DG, DG)].astype(F32)
                sv = ps_ref[rows, pl.ds(D_MODEL + g * DG, DG)].astype(F32)
                u = _gelu(su)
                rs, xh = _ln_stats(_gelu(sv))
                vn16 = (xh * lg + lb_ref[g]).astype(BF16)
                mixed = _dot(wm, vn16, NN) + b_ref[g]
                dyv = dy_ref[rows, pl.ds(g * DG, DG)].astype(F32)
                ds_ref[rows, pl.ds(g * DG, DG)] = (dyv * mixed * _gelu_grad(su)).astype(BF16)
                dmix = dyv * u
                dmix16 = dmix.astype(BF16)
                db_ref[g] += jnp.sum(dmix, axis=-1, keepdims=True)
                dw_ref[g] += jnp.where(mask, _dot(dmix16, vn16, NT), 0.0)
                dvn = _dot(wm, dmix16, TN)
                dlg_ref[g] += jnp.sum(dvn * xh, axis=0, keepdims=True)
                dlb_ref[g] += jnp.sum(dvn, axis=0, keepdims=True)
                dxh = dvn * lg
                dvf = rs * (dxh - jnp.mean(dxh, axis=-1, keepdims=True)
                            - xh * jnp.mean(dxh * xh, axis=-1, keepdims=True))
                ds_ref[rows, pl.ds(D_MODEL + g * DG, DG)] = (dvf * _gelu_grad(sv)).astype(BF16)

    full3 = lambda a, b, c: pl.BlockSpec((a, b, c), lambda t: (0, 0, 0))
    return pl.pallas_call(
        body, name=name, grid=(S // Ts,),
        in_specs=[pl.BlockSpec((Ts, 2 * D_MODEL), lambda t: (t, 0)),
                  full3(GROUPS, 1, DG), full3(GROUPS, 1, DG), full3(GROUPS, SBLOCK, SBLOCK), full3(GROUPS, SBLOCK, 1),
                  pl.BlockSpec((Ts, D_MODEL), lambda t: (t, 0))],
        out_specs=[pl.BlockSpec((Ts, 2 * D_MODEL), lambda t: (t, 0)),
                   full3(GROUPS, 1, DG), full3(GROUPS, 1, DG), full3(GROUPS, SBLOCK, SBLOCK), full3(GROUPS, SBLOCK, 1)],
        out_shape=[jax.ShapeDtypeStruct((S, 2 * D_MODEL), BF16),
                   jax.ShapeDtypeStruct((GROUPS, 1, DG), F32), jax.ShapeDtypeStruct((GROUPS, 1, DG), F32),
                   jax.ShapeDtypeStruct((GROUPS, SBLOCK, SBLOCK), F32),
                   jax.ShapeDtypeStruct((GROUPS, SBLOCK, 1), F32)],
        compiler_params=_cp(("arbitrary",)),
    )(ps, ln_g, ln_b, w_sp, b_sp, dy)


def _position():
    return lax.axis_index("x"), lax.axis_index("y"), lax.axis_index("c")


def _gather_copies(srcs, dsts, send_sems, recv_sems, local_sems):
    x, y, c = _position()
    me, sibling = (x, y, c), (x, y, 1 - c)
    chips = [(1 - x, y), (x, 1 - y), (1 - x, 1 - y)]
    n = len(srcs)

    def slab(a, block):
        px, py, pc = block
        return dsts[a].at[4 * px + 2 * py + pc]

    def copy(a, k, block, to, src=None):
        return pltpu.make_async_remote_copy(
            src_ref=slab(a, block) if src is None else src, dst_ref=slab(a, block),
            send_sem=send_sems.at[7 * a + k], recv_sem=recv_sems.at[7 * a + k], device_id=to, device_id_type=MESH)

    mine = [pltpu.make_async_copy(srcs[a], slab(a, me), local_sems.at[a]) for a in range(n)]
    for cp in mine:
        cp.start()
    first = []
    for a in range(n):
        first.append(copy(a, 0, me, sibling, src=srcs[a]))
        first += [copy(a, 1 + j, me, (*chip, c), src=srcs[a]) for j, chip in enumerate(chips)]
    for cp in first:
        cp.start()
    passed = []
    for j, chip in enumerate(chips):
        for a in range(n):
            copy(a, 1 + j, (*chip, c), me).wait_recv()
            cp = copy(a, 4 + j, (*chip, c), sibling)
            cp.start()
            passed.append(cp)
    for a in range(n):
        copy(a, 0, sibling, me).wait_recv()
        for j, chip in enumerate(chips):
            copy(a, 4 + j, (*chip, 1 - c), me).wait_recv()
    for cp in first + passed:
        cp.wait_send()
    for cp in mine:
        cp.wait()


def _all_gather_hbm(shards, *, name):
    n = len(shards)

    def body(*refs):
        srcs, dsts = refs[:n], refs[n:2 * n]
        send_sems, recv_sems, local_sems = refs[2 * n:]
        _gather_copies(srcs, dsts, send_sems, recv_sems, local_sems)

    return pl.pallas_call(
        body, name=name,
        in_specs=[ANY] * n, out_specs=[ANY] * n,
        out_shape=[jax.ShapeDtypeStruct((N_DEV, *s.shape), s.dtype) for s in shards],
        scratch_shapes=_comm_sems(n),
    )(*shards)


def _all_reduce_small(part, *, name):
    R, W = part.shape

    def body(x_ref, out_ref, gathered, send_sems, recv_sems, local_sems):
        _gather_copies([x_ref], [gathered], send_sems, recv_sems, local_sems)
        acc = gathered[0]
        for d in range(1, N_DEV):
            acc = acc + gathered[d]
        out_ref[...] = acc

    return pl.pallas_call(
        body, name=name,
        in_specs=[VMEM_SPEC], out_specs=VMEM_SPEC,
        out_shape=jax.ShapeDtypeStruct((R, W), F32),
        scratch_shapes=[pltpu.VMEM((N_DEV, R, W), F32),
                        pltpu.SemaphoreType.DMA((7,)), pltpu.SemaphoreType.DMA((7,)), pltpu.SemaphoreType.DMA((1,))],
    )(part)


FLIPS = [(fx, fy, fc) for fx in (0, 1) for fy in (0, 1) for fc in (0, 1)][1:]


def _scatter_copies(srcs, dsts, send_sems, recv_sems, local_sems):
    n = len(srcs)
    x, y, c = _position()
    me = 4 * x + 2 * y + c
    mine = [pltpu.make_async_copy(srcs[a].at[me], dsts[a].at[me], local_sems.at[a]) for a in range(n)]
    for cp in mine:
        cp.start()
    copies = []
    for k, (fx, fy, fc) in enumerate(FLIPS):
        tx = 1 - x if fx else x
        ty = 1 - y if fy else y
        tc = 1 - c if fc else c
        peer = 4 * tx + 2 * ty + tc
        for a in range(n):
            cp = pltpu.make_async_remote_copy(
                src_ref=srcs[a].at[peer], dst_ref=dsts[a].at[me],
                send_sem=send_sems.at[7 * a + k], recv_sem=recv_sems.at[7 * a + k],
                device_id=(tx, ty, tc), device_id_type=MESH)
            cp.start()
            copies.append(cp)
    for cp in copies:
        cp.wait()
    for cp in mine:
        cp.wait()


def _comm_sems(n):
    return [pltpu.SemaphoreType.DMA((7 * n,)), pltpu.SemaphoreType.DMA((7 * n,)), pltpu.SemaphoreType.DMA((n,))]


def _scatter_blocks(parts, *, name):
    n = len(parts)

    def body(*refs):
        _scatter_copies(refs[:n], refs[n:2 * n], *refs[2 * n:])

    return pl.pallas_call(
        body, name=name,
        in_specs=[ANY] * n, out_specs=[ANY] * n,
        out_shape=[jax.ShapeDtypeStruct(p.shape, p.dtype) for p in parts],
        scratch_shapes=_comm_sems(n),
    )(*parts)


def _handshake(peers):
    barrier = pltpu.get_barrier_semaphore()
    for peer in peers:
        pl.semaphore_signal(barrier, inc=1, device_id=peer, device_id_type=MESH)
    pl.semaphore_wait(barrier, len(peers))


def _sequencer_call(arrays, out_types, copies_fn, peers_fn, *, name, collective_id):
    n = len(arrays)
    srcs = [jax.new_ref(a, memory_space=pltpu.MemorySpace.HBM) for a in arrays]
    dsts = [jax.empty_ref(t, memory_space=pltpu.MemorySpace.HBM) for t in out_types]

    @pl.kernel(mesh=plsc.ScalarSubcoreMesh(axis_name="sequencer", num_cores=1), name=name,
               scratch_types=_comm_sems(n), compiler_params=pltpu.CompilerParams(collective_id=collective_id))
    def launch(send_sems, recv_sems, local_sems):
        _handshake(peers_fn())
        copies_fn(srcs, dsts, send_sems, recv_sems, local_sems)

    launch()
    return [d[...] for d in dsts]


def _all_other_devices():
    x, y, c = _position()
    return [(1 - x if fx else x, 1 - y if fy else y, 1 - c if fc else c) for fx, fy, fc in FLIPS]


def _gather_peers():
    x, y, c = _position()
    return [(x, y, 1 - c), (1 - x, y, c), (x, 1 - y, c), (1 - x, 1 - y, c)]


def _scatter_blocks_async(parts, *, name, collective_id):
    return _sequencer_call(parts, [jax.ShapeDtypeStruct(p.shape, p.dtype) for p in parts],
                           _scatter_copies, _all_other_devices, name=name, collective_id=collective_id)


def _all_gather_async(shards, *, name, collective_id):
    return _sequencer_call(shards, [jax.ShapeDtypeStruct((N_DEV, *s.shape), s.dtype) for s in shards],
                           _gather_copies, _gather_peers, name=name, collective_id=collective_id)


def _adamw_math(w, g, m, v):
    m = ADAM_B1 * m + (1.0 - ADAM_B1) * g
    v = ADAM_B2 * v + (1.0 - ADAM_B2) * (g * g)
    m_hat = m / (1.0 - ADAM_B1 ** ADAM_STEP)
    v_hat = v / (1.0 - ADAM_B2 ** ADAM_STEP)
    delta = -ADAM_LR * (m_hat / (jnp.sqrt(v_hat) + ADAM_EPS) + ADAM_WD * w)
    return delta, m, v


def _adamw_reduce_call(recv, w, m, v, *, name, T=128):
    R, W = w.shape
    T = T if R % T == 0 else R // 2
    assert R % T == 0 and T % 16 == 0, (R, T)

    def body(p_ref, w_ref, m_ref, v_ref, g_out, d_out, m_out, v_out):
        g = p_ref[0].astype(F32)
        for d in range(1, N_DEV):
            g = g + p_ref[d].astype(F32)
        g_out[...] = g
        d_out[...], m_out[...], v_out[...] = _adamw_math(w_ref[...], g, m_ref[...], v_ref[...])

    row = pl.BlockSpec((T, W), lambda i: (i, 0))
    out = jax.ShapeDtypeStruct((R, W), F32)
    return pl.pallas_call(
        body, name=name, grid=(R // T,),
        in_specs=[pl.BlockSpec((N_DEV, T, W), lambda i: (0, i, 0)), row, row, row],
        out_specs=[row] * 4, out_shape=[out] * 4,
        compiler_params=_cp(("parallel",), VMEM_BIG),
    )(recv, w, m, v)


def _adamw_small_call(w, g, m, v, *, name):
    def body(w_ref, g_ref, m_ref, v_ref, d_out, m_out, v_out):
        d_out[...], m_out[...], v_out[...] = _adamw_math(w_ref[...], g_ref[...], m_ref[...], v_ref[...])

    out = jax.ShapeDtypeStruct(w.shape, F32)
    return pl.pallas_call(body, name=name, in_specs=[VMEM_SPEC] * 4, out_specs=[VMEM_SPEC] * 3,
                          out_shape=[out] * 3)(w, g, m, v)


def _tile_rows(n_elems):
    return -(-n_elems // (8 * LANES)) * 8


def _pack_rows(parts, rows):
    pieces = []
    for p in parts:
        q = p.reshape(-1, LANES)
        pieces.append(jnp.pad(q, ((0, _tile_rows(p.size) - q.shape[0]), (0, 0))))
    buf = jnp.concatenate(pieces, axis=0)
    return jnp.pad(buf, ((0, rows - buf.shape[0]), (0, 0)))


def _unpack_rows(buf, shapes):
    out, off = [], 0
    for shp in shapes:
        n = 1
        for s in shp:
            n *= s
        out.append(buf[off:off + n // LANES].reshape(shp))
        off += _tile_rows(n)
    return out


def _in_columns(blocks, lo, hi):
    pieces = []
    for d in range(N_DEV):
        s, e = max(lo, IN_BLK * d), min(hi, IN_BLK * (d + 1))
        if s < e:
            pieces.append(blocks[d][:, s - IN_BLK * d:e - IN_BLK * d])
    return jnp.concatenate(pieces, axis=1)


def _in_blocks(segments):
    blocks = []
    for d in range(N_DEV):
        lo, hi = IN_BLK * d, IN_BLK * (d + 1)
        pieces = []
        for arr, c0 in segments:
            s, e = max(lo, c0), min(hi, c0 + arr.shape[1])
            if s < e:
                pieces.append(arr[:, s - c0:e - c0])
        blocks.append(jnp.concatenate(pieces, axis=1))
    return jnp.stack(blocks)


def _local_step(x, target, W, scatter):
    g1, g2, g3, g4 = [W[n].reshape(1, D_MODEL) for n in ("norm_pre_mix", "norm_post_mix", "norm_pre_ffn", "norm_post_ffn")]
    w_a, w_low, w_s, w_g = W["w_in_a"], W["w_in_low"], W["w_in_s"], W["w_in_g"]
    wfi, wfo = W["w_ffn_in"], W["w_ffn_out"].reshape(4, FF_BLK, D_MODEL)
    wg = jnp.pad(W["w_gate_up"], ((0, LANES - RANK), (0, 0))).astype(BF16)
    bgate = W["b_gate"].reshape(1, QK)
    gnorm = W["gla_norm"].reshape(HEADS, 1, DV)
    ln_g = W["sgu_ln_g"].reshape(GROUPS, 1, DG)
    ln_b = W["sgu_ln_b"].reshape(GROUPS, 1, DG)
    w_sp = W["w_spatial"]
    b_sp = W["b_spatial"].reshape(GROUPS, SBLOCK, 1)

    a, pa, alow, ps, pg = _in_proj_call(x, g1, w_a, w_low, w_s, w_g, name="in_proj")
    y_gla, states = _gla_fwd_call(pa, alow, wg, bgate, gnorm, name="gla_fwd")
    y_sgu = _sgu_fwd_call(ps, ln_g, ln_b, w_sp, b_sp, name="sgu_fwd")
    t1, t2, merged, mix, x1, h = _mixer_tail_call(y_gla, y_sgu, pg, x, W["w_branch_gla"], W["w_branch_sgu"],
                                                  W["w_out"], g2, g3, name="mixer_tail")
    gu, z = _ffn_in_call(h, wfi, name="ffn_in")
    loss, dx2, dy, dg4 = _ffn_out_loss_call(z, wfo, x1, target, g4, name="ffn_out_loss")

    grads = {"norm_post_ffn": dg4}
    received = {}
    dgu = _ffn_out_bwd_call(dy, wfo, gu, name="ffn_out_bwd")
    dw_ffn_out = _weight_grads_call([z, dy], [(0, 1)], name="d_ffn_out_w")[0].reshape(N_DEV, D_FF // N_DEV, D_MODEL)
    dw_ffn_in = _ffn_in_grad_call(h, dgu, name="d_ffn_in_w")
    dgu, dw_ffn_out, dw_ffn_in = lax.optimization_barrier((dgu, dw_ffn_out, dw_ffn_in))
    received.update(scatter(("w_ffn_out", "w_ffn_in"), [dw_ffn_out, dw_ffn_in]))
    dx1, dmix, grads["norm_pre_ffn"], grads["norm_post_mix"] = _ffn_in_bwd_call(dgu, wfi, dx2, x1, mix, g3, g2, name="ffn_in_bwd")
    dt1, dt2, dpg, dy_gla, dy_sgu = _mixer_bwd_call(dmix, t1, t2, pg, W["w_out"], W["w_branch_gla"], W["w_branch_sgu"],
                                                    name="mixer_bwd")
    rows = D_MODEL // N_DEV
    mixer_grads = _weight_grads_call([merged, dmix, y_gla, dt1, y_sgu, dt2], [(0, 1), (2, 3), (4, 5)], name="d_mixer_w")
    dy_gla, dy_sgu, mixer_grads = lax.optimization_barrier((dy_gla, dy_sgu, mixer_grads))
    received.update(scatter(("w_out", "w_branch_gla", "w_branch_sgu"),
                            [g.reshape(N_DEV, rows, D_MODEL) for g in mixer_grads]))
    dps, dlg, dlb, dwsp, dbsp = _sgu_bwd_call(ps, ln_g, ln_b, w_sp, b_sp, dy_sgu, name="sgu_bwd")
    dpa, dlogit, dgn, dbg = _gla_bwd_call(pa, alow, wg, bgate, gnorm, states, dy_gla, name="gla_bwd")
    dlow = _mm(dlogit, wg, "nt", BF16, name="d_gate_up_x")
    dw_a, dw_low = _weight_grads_call([a, dpa, dlow], [(0, 1), (0, 2)], name="d_in_w_qkvr")
    dw_s, dw_g = _weight_grads_call([a, dps, dpg], [(0, 1), (0, 2)], name="d_in_w_sgu_gates")
    dw_in = _in_blocks([(dw_a, A_COLS[0]), (dw_low[:, :RANK], LOW_COLS[0]), (dw_s, S_COLS[0]), (dw_g, G_COLS[0])])
    received.update(scatter(("w_in",), [dw_in]))
    dwg = _mm(alow, dlogit, "tn", F32, name="d_gate_up_w")
    grad_x, grads["norm_pre_mix"] = _in_proj_bwd_call(dpa, dlow, dps, dpg, w_a, w_low, w_s, w_g, x, dx1, g1,
                                                      name="in_proj_bwd")

    grads["w_gate_up"] = dwg[:RANK]
    grads["b_gate"] = dbg
    grads["gla_norm"] = dgn
    grads["sgu_ln_g"] = dlg
    grads["sgu_ln_b"] = dlb
    grads["w_spatial"] = dwsp
    grads["b_spatial"] = dbsp
    return loss, grad_x, grads, received


WEIGHTS = ("norm_pre_mix", "w_in", "w_gate_up", "b_gate", "gla_norm", "sgu_ln_g", "sgu_ln_b", "w_spatial",
           "b_spatial", "w_branch_gla", "w_branch_sgu", "w_out", "norm_post_mix", "norm_pre_ffn", "w_ffn_in",
           "w_ffn_out", "norm_post_ffn")
BIG = ("w_in", "w_branch_gla", "w_branch_sgu", "w_out", "w_ffn_in", "w_ffn_out")
MIXER = ("w_branch_gla", "w_branch_sgu", "w_out")
FFN = ("w_ffn_in", "w_ffn_out")
SMALL = tuple(n for n in WEIGHTS if n not in BIG)
SMALL_SHARDED = ("w_gate_up", "gla_norm", "sgu_ln_g", "sgu_ln_b")
SMALL_FULL = {"norm_pre_mix": (1024,), "w_gate_up": (16, 512), "b_gate": (512,), "gla_norm": (4, 256),
              "sgu_ln_g": (4, 256), "sgu_ln_b": (4, 256), "w_spatial": (4, 128, 128), "b_spatial": (4, 128),
              "norm_post_mix": (1024,), "norm_pre_ffn": (1024,), "norm_post_ffn": (1024,)}
SMALL_GRAD_ROWS = 648
SMALL_STATE_ROWS = 600
SMALL_GATHER_ROWS = 32


def kernel(x, norm_pre_mix, w_in, w_gate_up, b_gate, gla_norm, sgu_ln_g, sgu_ln_b, w_spatial, b_spatial, w_branch_gla, w_branch_sgu, w_out, norm_post_mix, norm_pre_ffn, w_ffn_in, w_ffn_out, norm_post_ffn, loss_target, m_norm_pre_mix, m_w_in, m_w_gate_up, m_b_gate, m_gla_norm, m_sgu_ln_g, m_sgu_ln_b, m_w_spatial, m_b_spatial, m_w_branch_gla, m_w_branch_sgu, m_w_out, m_norm_post_mix, m_norm_pre_ffn, m_w_ffn_in, m_w_ffn_out, m_norm_post_ffn, v_norm_pre_mix, v_w_in, v_w_gate_up, v_b_gate, v_gla_norm, v_sgu_ln_g, v_sgu_ln_b, v_w_spatial, v_b_spatial, v_w_branch_gla, v_w_branch_sgu, v_w_out, v_norm_post_mix, v_norm_pre_ffn, v_w_ffn_in, v_w_ffn_out, v_norm_post_ffn):
    given = dict(locals())
    w = {n: given[n][0] for n in WEIGHTS}
    m = {n: given["m_" + n][0] for n in WEIGHTS}
    v = {n: given["v_" + n][0] for n in WEIGHTS}
    xs, target = x[0], loss_target[0]
    me = 4 * lax.axis_index("x") + 2 * lax.axis_index("y") + lax.axis_index("c")

    small_shard = _pack_rows([w[n] for n in SMALL_SHARDED], SMALL_GATHER_ROWS)
    first = _all_gather_hbm([w["w_in"].astype(BF16), small_shard], name="gather_w_in")
    rest, first = lax.optimization_barrier(([w[n].astype(BF16) for n in MIXER + FFN], first))
    rest_blocks = _all_gather_async(rest, name="gather_rest", collective_id=1)
    W = {n: w[n] for n in SMALL if n not in SMALL_SHARDED}
    blocks = {"w_in": first[0], **dict(zip(MIXER + FFN, rest_blocks))}
    for n in ("w_branch_gla", "w_branch_sgu", "w_out", "w_ffn_out"):
        W[n] = blocks[n].reshape(-1, D_MODEL)
    W["w_ffn_in"] = blocks["w_ffn_in"]
    W["w_in_a"] = _in_columns(blocks["w_in"], *A_COLS)
    W["w_in_low"] = jnp.pad(_in_columns(blocks["w_in"], *LOW_COLS), ((0, 0), (0, LANES - RANK)))
    W["w_in_s"] = _in_columns(blocks["w_in"], *S_COLS)
    W["w_in_g"] = _in_columns(blocks["w_in"], *G_COLS)
    small_blocks = first[1]
    off = 0
    for n in SMALL_SHARDED:
        r, c = w[n].shape
        blk = small_blocks[:, off:off + r * c // LANES].reshape(N_DEV, r, c)
        W[n] = blk.transpose(1, 0, 2).reshape(r, N_DEV * c)
        off += _tile_rows(r * c)

    scatter_ids = iter((3, 4, 5))

    def scatter(names, parts):
        got = _scatter_blocks_async(parts, name="scatter_" + "_".join(names), collective_id=next(scatter_ids))
        return dict(zip(names, got))

    loss_part, grad_x, grads, received = _local_step(xs, target, W, scatter)

    small_part = jnp.concatenate([_pack_rows([grads[n] for n in SMALL], SMALL_GRAD_ROWS),
                                  jnp.broadcast_to(loss_part, (8, LANES))], axis=0)
    small_sum = _all_reduce_small(small_part, name="reduce_small")
    loss = small_sum[SMALL_GRAD_ROWS, 0]
    g_small = dict(zip(SMALL, _unpack_rows(small_sum, [SMALL_FULL[n] for n in SMALL])))
    for n in SMALL_SHARDED:
        c = w[n].shape[1]
        g_small[n] = lax.dynamic_slice_in_dim(g_small[n], me * c, c, axis=1)

    g_big, d_big, m_big, v_big = {}, {}, {}, {}
    for n in BIG:
        g_big[n], d_big[n], m_big[n], v_big[n] = _adamw_reduce_call(received[n], w[n], m[n], v[n], name="adamw_" + n)
    small_shapes = [w[n].shape for n in SMALL]
    small_state = [_pack_rows([s[n] for n in SMALL], SMALL_STATE_ROWS) for s in (w, g_small, m, v)]
    small_out = _adamw_small_call(*small_state, name="adamw_small")
    d_small, m_small, v_small = [dict(zip(SMALL, _unpack_rows(o, small_shapes))) for o in small_out]

    def pick(big, sml):
        return [(big[n] if n in BIG else sml[n])[None] for n in WEIGHTS]

    return (loss, grad_x[None], *pick(g_big, g_small), *pick(d_big, d_small), *pick(m_big, m_small),
            *pick(v_big, v_small))
```

```python
import jax
import jax.numpy as jnp
from jax import lax
from jax.experimental import pallas as pl
from jax.experimental.pallas import tpu as pltpu
from jax.experimental.pallas import tpu_sc as plsc

F32 = jnp.float32
BF16 = jnp.bfloat16

D_MODEL = 1024
N_DEV = 8
CHUNK = 64
HEADS = 4
DK = 128
DV = 256
QK = HEADS * DK
GV = HEADS * DV
RANK = 16
GROUPS = 4
SBLOCK = 128
DG = 256
D_FF = 2816
FF_BLK = 704
EPS = 1e-6
Q_SCALE = DK ** -0.5
LANES = 128
VMEM_BIG = 48 * 1024 * 1024

D_IN = 7184
IN_BLK = 898
A_COLS = (0, 3072)
LOW_COLS = (3072, 3088)
S_COLS = (3088, 5136)
G_COLS = (5136, 7184)

ADAM_LR = 0.001
ADAM_B1 = 0.9
ADAM_B2 = 0.999
ADAM_EPS = 1e-08
ADAM_WD = 0.01
ADAM_STEP = 10

MESH = pl.DeviceIdType.MESH
ANY = pl.BlockSpec(memory_space=pl.ANY)
VMEM_SPEC = pl.BlockSpec(memory_space=pltpu.VMEM)


def _cp(sem=None, vmem=None):
    return pltpu.CompilerParams(dimension_semantics=sem, vmem_limit_bytes=vmem)


def _sig(x):
    return 1.0 / (1.0 + jnp.exp(-x))


def _gelu(x):
    c = 0.7978845608028654
    t = jnp.tanh(c * (x + 0.044715 * x * x * x))
    return 0.5 * x * (1.0 + t)


def _gelu_grad(x):
    c = 0.7978845608028654
    x2 = x * x
    t = jnp.tanh(c * (x + 0.044715 * x * x2))
    return 0.5 * (1.0 + t) + 0.5 * x * (1.0 - t * t) * c * (1.0 + 3.0 * 0.044715 * x2)


def _logsig(x):
    return jnp.minimum(x, 0.0) - jnp.log1p(jnp.exp(-jnp.abs(x)))


def _dot(a, b, dims):
    return lax.dot_general(a, b, (dims, ((), ())), preferred_element_type=F32)


NN = ((1,), (0,))
NT = ((1,), (1,))
TN = ((0,), (0,))


def _exact_mask_dot(mask_bf16, x):
    hi = x.astype(BF16)
    r1 = x - hi.astype(F32)
    mid = r1.astype(BF16)
    lo = (r1 - mid.astype(F32)).astype(BF16)
    return _dot(mask_bf16, hi, NN) + _dot(mask_bf16, mid, NN) + _dot(mask_bf16, lo, NN)


def _pick_tile(dim, target):
    if dim <= target:
        return dim
    best = None
    for t in range(LANES, int(1.4 * target) + 1, LANES):
        if dim % t == 0:
            best = t
    assert best is not None, (dim, target)
    return best


def _mm_call(a, b, *, name, grid, a_spec, b_spec, o_spec, out_shape, dims, acc_shape):
    nk = grid[2]

    def body(a_ref, b_ref, o_ref, *acc):
        part = _dot(a_ref[...], b_ref[...], dims)
        if nk == 1:
            o_ref[...] = part.astype(o_ref.dtype)
        else:
            acc_ref = acc[0]
            k = pl.program_id(2)

            @pl.when(k == 0)
            def _():
                acc_ref[...] = part

            @pl.when(k > 0)
            def _():
                acc_ref[...] += part

            @pl.when(k == nk - 1)
            def _():
                o_ref[...] = acc_ref[...].astype(o_ref.dtype)

    return pl.pallas_call(
        body, name=name, grid=grid, in_specs=[a_spec, b_spec], out_specs=o_spec, out_shape=out_shape,
        scratch_shapes=[] if nk == 1 else [pltpu.VMEM(acc_shape, F32)],
        compiler_params=_cp(("parallel", "parallel", "arbitrary"), VMEM_BIG),
    )(a, b)


def _mm(a, b, mode, out_dtype, *, name, tm=512, tn=1024, tk=1024):
    if mode == "nn":
        (M, K), (_, N) = a.shape, b.shape
    elif mode == "nt":
        (M, K), (N, _) = a.shape, b.shape
    else:
        (K, M), (_, N) = a.shape, b.shape
    tm, tn, tk = _pick_tile(M, tm), _pick_tile(N, tn), _pick_tile(K, tk)
    if mode == "nn":
        a_spec = pl.BlockSpec((tm, tk), lambda i, j, k: (i, k))
        b_spec = pl.BlockSpec((tk, tn), lambda i, j, k: (k, j))
        dims = NN
    elif mode == "nt":
        a_spec = pl.BlockSpec((tm, tk), lambda i, j, k: (i, k))
        b_spec = pl.BlockSpec((tn, tk), lambda i, j, k: (j, k))
        dims = NT
    else:
        a_spec = pl.BlockSpec((tk, tm), lambda i, j, k: (k, i))
        b_spec = pl.BlockSpec((tk, tn), lambda i, j, k: (k, j))
        dims = TN
    return _mm_call(a, b, name=name, grid=(M // tm, N // tn, K // tk), a_spec=a_spec, b_spec=b_spec,
                    o_spec=pl.BlockSpec((tm, tn), lambda i, j, k: (i, j)),
                    out_shape=jax.ShapeDtypeStruct((M, N), out_dtype), dims=dims, acc_shape=(tm, tn))


ROW_TILE = 512


def _rt(T, W, c=0):
    return pl.BlockSpec((T, W), lambda i: (i, c))


def _rt3(nb, T, W):
    return pl.BlockSpec((nb, T, W), lambda i: (0, i, 0))


def _res(shape):
    nd = len(shape)
    return pl.BlockSpec(tuple(shape), lambda i: (0,) * nd, pipeline_mode=pl.Buffered(1))


def _acc(shape):
    nd = len(shape)
    return pl.BlockSpec(tuple(shape), lambda i: (0,) * nd, pipeline_mode=pl.Buffered(1))


def _nbytes(shape, dtype):
    n = jnp.dtype(dtype).itemsize
    for s in shape:
        n *= s
    return n


def _vmem_limit(tiles, resident, temps=16 * 1024 * 1024):
    need = 2 * sum(_nbytes(s, d) for s, d in tiles) + sum(_nbytes(s, d) for s, d in resident) + temps
    return min(need, 60 * 1024 * 1024)


def _tok_call(body, *, name, S, T, ins, outs, semantics="parallel"):
    tiles = [(spec.block_shape, a.dtype) for a, spec, kind in ins + outs if kind == "tile"]
    resident = [(a.shape, a.dtype) for a, spec, kind in ins + outs if kind == "res"]
    return pl.pallas_call(
        body, name=name, grid=(S // T,),
        in_specs=[spec for _, spec, _ in ins], out_specs=[spec for _, spec, _ in outs],
        out_shape=[jax.ShapeDtypeStruct(a.shape, a.dtype) for a, _, _ in outs],
        compiler_params=_cp((semantics,), _vmem_limit(tiles, resident)),
    )(*[a for a, _, _ in ins])


def _tile(a, spec):
    return (a, spec, "tile")


def _whole(a):
    return (a, _res(a.shape), "res")


def _out_tile(shape, dtype, spec):
    return (jax.ShapeDtypeStruct(shape, dtype), spec, "tile")


def _out_acc(shape, dtype=F32):
    return (jax.ShapeDtypeStruct(shape, dtype), _acc(shape), "res")


def _rms_stats(x):
    r = lax.rsqrt(jnp.mean(x * x, axis=-1, keepdims=True) + EPS)
    return r, x * r


def _rms_bwd(xh, r, g, dy):
    dxh = dy * g
    dx = r * (dxh - xh * jnp.mean(dxh * xh, axis=-1, keepdims=True))
    dg = jnp.sum(dy * xh, axis=0, keepdims=True)
    return dx, dg


def _accum(ref, val):
    @pl.when(pl.program_id(0) == 0)
    def _():
        ref[...] = val

    @pl.when(pl.program_id(0) > 0)
    def _():
        ref[...] += val


def _dot_cols(a, w_ref, o_ref, dims, chunk=1024):
    n = o_ref.shape[1]
    for n0 in range(0, n, chunk):
        n1 = min(n, n0 + chunk)
        w = w_ref[:, n0:n1] if dims == NN else w_ref[n0:n1, :]
        o_ref[:, n0:n1] = _dot(a, w, dims).astype(o_ref.dtype)


def _in_proj_call(x, g1, w_a, w_low, w_s, w_g, *, name, T=ROW_TILE):
    S = x.shape[0]

    def body(x_ref, g_ref, wa_ref, wl_ref, ws_ref, wg_ref, a_ref, pa_ref, al_ref, ps_ref, pg_ref):
        _, xh = _rms_stats(x_ref[...])
        a = (xh * g_ref[...]).astype(BF16)
        a_ref[...] = a
        _dot_cols(a, wa_ref, pa_ref, NN)
        _dot_cols(a, wl_ref, al_ref, NN)
        _dot_cols(a, ws_ref, ps_ref, NN)
        _dot_cols(a, wg_ref, pg_ref, NN)

    widths = (D_MODEL, w_a.shape[1], w_low.shape[1], w_s.shape[1], w_g.shape[1])
    return _tok_call(
        body, name=name, S=S, T=T,
        ins=[_tile(x, _rt(T, D_MODEL)), _whole(g1), _whole(w_a), _whole(w_low), _whole(w_s), _whole(w_g)],
        outs=[_out_tile((S, w), BF16, _rt(T, w)) for w in widths])


def _mixer_tail_call(y_gla, y_sgu, pg, x, w_bg, w_bs, w_out, g2, g3, *, name, T=ROW_TILE):
    S = x.shape[0]

    def body(yg_ref, ys_ref, pg_ref, x_ref, wbg_ref, wbs_ref, wo_ref, g2_ref, g3_ref,
             t1_ref, t2_ref, mg_ref, mix_ref, x1_ref, h_ref):
        t1 = _dot(yg_ref[...], wbg_ref[...], NN)
        t2 = _dot(ys_ref[...], wbs_ref[...], NN)
        t1_ref[...] = t1.astype(BF16)
        t2_ref[...] = t2.astype(BF16)
        sg = _sig(pg_ref[:, :D_MODEL].astype(F32))
        ss = _sig(pg_ref[:, D_MODEL:].astype(F32))
        merged = (sg * t1 + ss * t2).astype(BF16)
        mg_ref[...] = merged
        mix = _dot(merged, wo_ref[...], NN)
        mix_ref[...] = mix
        _, mh = _rms_stats(mix)
        x1 = x_ref[...] + mh * g2_ref[...]
        x1_ref[...] = x1
        _, xh = _rms_stats(x1)
        h_ref[...] = (xh * g3_ref[...]).astype(BF16)

    row = _rt(T, D_MODEL)
    b16 = lambda: _out_tile((S, D_MODEL), BF16, row)
    f32 = lambda: _out_tile((S, D_MODEL), F32, row)
    return _tok_call(
        body, name=name, S=S, T=T,
        ins=[_tile(y_gla, row), _tile(y_sgu, row), _tile(pg, _rt(T, 2 * D_MODEL)), _tile(x, row),
             _whole(w_bg), _whole(w_bs), _whole(w_out), _whole(g2), _whole(g3)],
        outs=[b16(), b16(), b16(), f32(), f32(), b16()])


def _ffn_in_call(h, wfi, *, name, T=ROW_TILE):
    S = h.shape[0]

    def body(h_ref, w_ref, gu_ref, z_ref):
        hv = h_ref[...]
        for d in range(4):
            gate = _dot(hv, w_ref[d], NN)
            up = _dot(hv, w_ref[d + 4], NN)
            gu_ref[d] = gate.astype(BF16)
            gu_ref[d + 4] = up.astype(BF16)
            z_ref[d] = (gate * _sig(gate) * up).astype(BF16)

    return _tok_call(
        body, name=name, S=S, T=T,
        ins=[_tile(h, _rt(T, D_MODEL)), _whole(wfi)],
        outs=[_out_tile((N_DEV, S, FF_BLK), BF16, _rt3(N_DEV, T, FF_BLK)),
              _out_tile((4, S, FF_BLK), BF16, _rt3(4, T, FF_BLK))])


def _ffn_out_loss_call(z, wfo, x1, target, g4, *, name, T=ROW_TILE):
    S = x1.shape[0]

    def body(z_ref, w_ref, x1_ref, t_ref, g4_ref, loss_ref, dx2_ref, dy_ref, dg4_ref):
        y = _dot(z_ref[0], w_ref[0], NN)
        for d in range(1, 4):
            y = y + _dot(z_ref[d], w_ref[d], NN)
        r, yh = _rms_stats(y)
        diff = x1_ref[...] + yh * g4_ref[...] - t_ref[...]
        part = 0.5 * jnp.sum(jnp.mean(diff * diff, axis=-1, keepdims=True), axis=0, keepdims=True)
        _accum(loss_ref, jnp.broadcast_to(part, (1, LANES)))
        dx2 = diff * (1.0 / D_MODEL)
        dx2_ref[...] = dx2
        dy, dg = _rms_bwd(yh, r, g4_ref[...], dx2)
        dy_ref[...] = dy.astype(BF16)
        _accum(dg4_ref, dg)

    row = _rt(T, D_MODEL)
    return _tok_call(
        body, name=name, S=S, T=T, semantics="arbitrary",
        ins=[_tile(z, _rt3(4, T, FF_BLK)), _whole(wfo), _tile(x1, row), _tile(target, row), _whole(g4)],
        outs=[_out_acc((1, LANES)), _out_tile((S, D_MODEL), F32, row), _out_tile((S, D_MODEL), BF16, row),
              _out_acc((1, D_MODEL))])


def _ffn_out_bwd_call(dy, wfo, gu, *, name, T=ROW_TILE):
    S = dy.shape[0]

    def body(dy_ref, w_ref, gu_ref, dgu_ref):
        dyv = dy_ref[...]
        for d in range(4):
            dz = _dot(dyv, w_ref[d], NT)
            gt = gu_ref[d].astype(F32)
            up = gu_ref[d + 4].astype(F32)
            s = _sig(gt)
            dgu_ref[d] = (dz * up * s * (1.0 + gt * (1.0 - s))).astype(BF16)
            dgu_ref[d + 4] = (dz * gt * s).astype(BF16)

    blocks = _rt3(N_DEV, T, FF_BLK)
    return _tok_call(
        body, name=name, S=S, T=T,
        ins=[_tile(dy, _rt(T, D_MODEL)), _whole(wfo), _tile(gu, blocks)],
        outs=[_out_tile((N_DEV, S, FF_BLK), BF16, blocks)])[0]


def _ffn_in_bwd_call(dgu, wfi, dx2, x1, mix, g3, g2, *, name, T=ROW_TILE):
    S = x1.shape[0]

    def body(dgu_ref, w_ref, dx2_ref, x1_ref, mix_ref, g3_ref, g2_ref, dx1_ref, dmix_ref, dg3_ref, dg2_ref):
        dh = _dot(dgu_ref[0], w_ref[0], NT)
        for d in range(1, N_DEV):
            dh = dh + _dot(dgu_ref[d], w_ref[d], NT)
        r3, xh = _rms_stats(x1_ref[...])
        d3, dg3 = _rms_bwd(xh, r3, g3_ref[...], dh)
        dx1 = dx2_ref[...] + d3
        dx1_ref[...] = dx1
        r2, mh = _rms_stats(mix_ref[...])
        dmix, dg2 = _rms_bwd(mh, r2, g2_ref[...], dx1)
        dmix_ref[...] = dmix.astype(BF16)
        _accum(dg3_ref, dg3)
        _accum(dg2_ref, dg2)

    row = _rt(T, D_MODEL)
    return _tok_call(
        body, name=name, S=S, T=T, semantics="arbitrary",
        ins=[_tile(dgu, _rt3(N_DEV, T, FF_BLK)), _whole(wfi), _tile(dx2, row), _tile(x1, row), _tile(mix, row),
             _whole(g3), _whole(g2)],
        outs=[_out_tile((S, D_MODEL), F32, row), _out_tile((S, D_MODEL), BF16, row),
              _out_acc((1, D_MODEL)), _out_acc((1, D_MODEL))])


def _mixer_bwd_call(dmix, t1, t2, pg, w_out, w_bg, w_bs, *, name, T=ROW_TILE):
    S = dmix.shape[0]

    def body(dmix_ref, t1_ref, t2_ref, pg_ref, wo_ref, wbg_ref, wbs_ref, dt1_ref, dt2_ref, dpg_ref, dyg_ref, dys_ref):
        dm = _dot(dmix_ref[...], wo_ref[...], NT)
        sg = _sig(pg_ref[:, :D_MODEL].astype(F32))
        ss = _sig(pg_ref[:, D_MODEL:].astype(F32))
        dt1 = (dm * sg).astype(BF16)
        dt2 = (dm * ss).astype(BF16)
        dt1_ref[...] = dt1
        dt2_ref[...] = dt2
        dpg_ref[:, :D_MODEL] = (dm * t1_ref[...].astype(F32) * sg * (1.0 - sg)).astype(BF16)
        dpg_ref[:, D_MODEL:] = (dm * t2_ref[...].astype(F32) * ss * (1.0 - ss)).astype(BF16)
        dyg_ref[...] = _dot(dt1, wbg_ref[...], NT).astype(BF16)
        dys_ref[...] = _dot(dt2, wbs_ref[...], NT).astype(BF16)

    row = _rt(T, D_MODEL)
    wide = _rt(T, 2 * D_MODEL)
    b16 = lambda: _out_tile((S, D_MODEL), BF16, row)
    return _tok_call(
        body, name=name, S=S, T=T,
        ins=[_tile(dmix, row), _tile(t1, row), _tile(t2, row), _tile(pg, wide), _whole(w_out), _whole(w_bg), _whole(w_bs)],
        outs=[b16(), b16(), _out_tile((S, 2 * D_MODEL), BF16, wide), b16(), b16()])


def _in_proj_bwd_call(dpa, dlow, dps, dpg, w_a, w_low, w_s, w_g, x, dx1, g1, *, name, T=ROW_TILE):
    S = x.shape[0]

    def body(dpa_ref, dl_ref, dps_ref, dpg_ref, wa_ref, wl_ref, ws_ref, wg_ref, x_ref, dx1_ref, g1_ref, gx_ref, dg1_ref):
        da = (_dot(dpa_ref[...], wa_ref[...], NT) + _dot(dl_ref[...], wl_ref[...], NT)
              + _dot(dps_ref[...], ws_ref[...], NT) + _dot(dpg_ref[...], wg_ref[...], NT))
        r, xh = _rms_stats(x_ref[...])
        dxa, dg = _rms_bwd(xh, r, g1_ref[...], da)
        gx_ref[...] = dx1_ref[...] + dxa
        _accum(dg1_ref, dg)

    row = _rt(T, D_MODEL)
    return _tok_call(
        body, name=name, S=S, T=T, semantics="arbitrary",
        ins=[_tile(dpa, _rt(T, dpa.shape[1])), _tile(dlow, _rt(T, dlow.shape[1])), _tile(dps, _rt(T, dps.shape[1])),
             _tile(dpg, _rt(T, dpg.shape[1])), _whole(w_a), _whole(w_low), _whole(w_s), _whole(w_g),
             _tile(x, row), _tile(dx1, row), _whole(g1)],
        outs=[_out_tile((S, D_MODEL), F32, row), _out_acc((1, D_MODEL))])


TOKEN_TILE = 512


def _weight_grads_call(arrays, pairs, *, name, tk=TOKEN_TILE):
    S = arrays[0].shape[-2]
    tk = min(tk, S)
    n_in = len(arrays)

    def out_shape(i, j):
        a, b = arrays[i], arrays[j]
        if a.ndim == 3:
            return (a.shape[0], a.shape[2], b.shape[1])
        if b.ndim == 3:
            return (b.shape[0], a.shape[1], b.shape[2])
        return (a.shape[1], b.shape[1])

    shapes = [out_shape(i, j) for i, j in pairs]

    def body(*refs):
        ins, outs, accs = refs[:n_in], refs[n_in:n_in + len(pairs)], refs[n_in + len(pairs):]
        k = pl.program_id(0)

        @pl.when(k == 0)
        def _():
            for acc in accs:
                acc[...] = jnp.zeros_like(acc)

        for (i, j), acc in zip(pairs, accs):
            a_ref, b_ref = ins[i], ins[j]
            if len(a_ref.shape) == 3:
                for n in range(a_ref.shape[0]):
                    acc[n] += _dot(a_ref[n], b_ref[...], TN)
            elif len(b_ref.shape) == 3:
                a = a_ref[...]
                for n in range(b_ref.shape[0]):
                    acc[n] += _dot(a, b_ref[n], TN)
            else:
                a = a_ref[...]
                for n0 in range(0, b_ref.shape[1], 1024):
                    n1 = min(b_ref.shape[1], n0 + 1024)
                    acc[:, n0:n1] += _dot(a, b_ref[:, n0:n1], TN)

        @pl.when(k == S // tk - 1)
        def _():
            for out, acc in zip(outs, accs):
                out[...] = acc[...].astype(out.dtype)

    def in_spec(a):
        if a.ndim == 3:
            return pl.BlockSpec((a.shape[0], tk, a.shape[2]), lambda k: (0, k, 0))
        return pl.BlockSpec((tk, a.shape[1]), lambda k: (k, 0))

    tiles = [(in_spec(a).block_shape, a.dtype) for a in arrays]
    resident = [(s, F32) for s in shapes] + [(s, BF16) for s in shapes]
    return pl.pallas_call(
        body, name=name, grid=(S // tk,),
        in_specs=[in_spec(a) for a in arrays],
        out_specs=[_acc(s) for s in shapes],
        out_shape=[jax.ShapeDtypeStruct(s, BF16) for s in shapes],
        scratch_shapes=[pltpu.VMEM(s, F32) for s in shapes],
        compiler_params=_cp(("arbitrary",), _vmem_limit(tiles, resident)),
    )(*arrays)


def _ffn_in_grad_call(h, dgu, *, name, tk=TOKEN_TILE):
    S = h.shape[0]
    tk = min(tk, S)
    nb = 4
    shape = (nb, D_MODEL, FF_BLK)

    def body(h_ref, dgu_ref, out_ref, acc):
        k = pl.program_id(1)

        @pl.when(k == 0)
        def _():
            acc[...] = jnp.zeros_like(acc)

        hv = h_ref[...]
        for n in range(nb):
            acc[n] += _dot(hv, dgu_ref[n], TN)

        @pl.when(k == S // tk - 1)
        def _():
            out_ref[...] = acc[...].astype(out_ref.dtype)

    tiles = [((tk, D_MODEL), BF16), ((nb, tk, FF_BLK), BF16), (shape, BF16)]
    return pl.pallas_call(
        body, name=name, grid=(N_DEV // nb, S // tk),
        in_specs=[pl.BlockSpec((tk, D_MODEL), lambda g, k: (k, 0)),
                  pl.BlockSpec((nb, tk, FF_BLK), lambda g, k: (g, k, 0))],
        out_specs=pl.BlockSpec(shape, lambda g, k: (g, 0, 0)),
        out_shape=jax.ShapeDtypeStruct((N_DEV, D_MODEL, FF_BLK), BF16),
        scratch_shapes=[pltpu.VMEM(shape, F32)],
        compiler_params=_cp(("parallel", "arbitrary"), _vmem_limit(tiles, [(shape, F32)])),
    )(h, dgu)


GLA_TILE = 256
Q_OFF, K_OFF, V_OFF, R_OFF = 0, QK, 2 * QK, 2 * QK + GV


def _tri(lower):
    r = lax.broadcasted_iota(jnp.int32, (CHUNK, CHUNK), 0)
    c = lax.broadcasted_iota(jnp.int32, (CHUNK, CHUNK), 1)
    return jnp.where((r >= c) if lower else (c >= r), 1.0, 0.0).astype(BF16)


def _gla_fwd_call(pa, alow, wg, bgate, gnorm, *, name):
    S = pa.shape[0]
    Tg = min(GLA_TILE, S)
    cb = Tg // CHUNK

    def body(pa_ref, al_ref, wg_ref, bg_ref, gn_ref, y_ref, st_ref, state):
        @pl.when(pl.program_id(0) == 0)
        def _():
            state[...] = jnp.zeros_like(state)

        logit = _dot(al_ref[...], wg_ref[...], NN) + bg_ref[...]
        ls = _logsig(logit) * (1.0 / 16.0)
        tri = _tri(True)
        for h in range(HEADS):
            for c in range(cb):
                rows = pl.ds(c * CHUNK, CHUNK)
                cum = _exact_mask_dot(tri, ls[c * CHUNK:(c + 1) * CHUNK, h * DK:(h + 1) * DK])
                tot = cum[CHUNK - 1:CHUNK]
                kd = (pa_ref[rows, pl.ds(K_OFF + h * DK, DK)].astype(F32) * jnp.exp(tot - cum)).astype(BF16)
                new = state[h] * jnp.exp(tot) + _dot(pa_ref[rows, pl.ds(V_OFF + h * DV, DV)], kd, TN)
                state[h] = new
                st_ref[c, h] = new
                qs = (pa_ref[rows, pl.ds(Q_OFF + h * DK, DK)].astype(F32) * Q_SCALE).astype(BF16)
                o = _dot(qs, new.astype(BF16), NT)
                rs = lax.rsqrt(jnp.mean(o * o, axis=-1, keepdims=True) + EPS)
                rr = pa_ref[rows, pl.ds(R_OFF + h * DV, DV)].astype(F32)
                y_ref[rows, pl.ds(h * DV, DV)] = (o * rs * gn_ref[h] * (rr * _sig(rr))).astype(BF16)

    return pl.pallas_call(
        body, name=name, grid=(S // Tg,),
        in_specs=[
            pl.BlockSpec((Tg, 2 * QK + 2 * GV), lambda t: (t, 0)),
            pl.BlockSpec((Tg, LANES), lambda t: (t, 0)),
            pl.BlockSpec((LANES, QK), lambda t: (0, 0)),
            pl.BlockSpec((1, QK), lambda t: (0, 0)),
            pl.BlockSpec((HEADS, 1, DV), lambda t: (0, 0, 0)),
        ],
        out_specs=[
            pl.BlockSpec((Tg, GV), lambda t: (t, 0)),
            pl.BlockSpec((cb, HEADS, DV, DK), lambda t: (t, 0, 0, 0)),
        ],
        out_shape=[jax.ShapeDtypeStruct((S, GV), BF16),
                   jax.ShapeDtypeStruct((S // CHUNK, HEADS, DV, DK), F32)],
        scratch_shapes=[pltpu.VMEM((HEADS, DV, DK), F32)],
        compiler_params=_cp(("arbitrary",), VMEM_BIG),
    )(pa, alow, wg, bgate, gnorm)


def _gla_bwd_call(pa, alow, wg, bgate, gnorm, states, dy, *, name):
    S = pa.shape[0]
    Tg = min(GLA_TILE, S)
    cb = Tg // CHUNK
    nt = S // Tg

    def rev(t):
        return nt - 1 - t

    def body(pa_ref, al_ref, wg_ref, bg_ref, gn_ref, st_ref, prev_ref, dy_ref,
             dpa_ref, dl_ref, dgn_ref, dbg_ref, carry):
        t = pl.program_id(0)

        @pl.when(t == 0)
        def _():
            carry[...] = jnp.zeros_like(carry)
            dgn_ref[...] = jnp.zeros_like(dgn_ref)
            dbg_ref[...] = jnp.zeros_like(dbg_ref)

        logit = _dot(al_ref[...], wg_ref[...], NN) + bg_ref[...]
        ls = _logsig(logit) * (1.0 / 16.0)
        sneg = _sig(-logit)
        tri = _tri(True)
        upper = _tri(False)
        first_tile = rev(t) == 0
        for h in range(HEADS):
            gn = gn_ref[h]
            hk = pl.ds(h * DK, DK)
            dbg = jnp.zeros((1, DK), F32)
            dgn = jnp.zeros((1, DV), F32)
            for c in reversed(range(cb)):
                rows = pl.ds(c * CHUNK, CHUNK)
                cum = _exact_mask_dot(tri, ls[c * CHUNK:(c + 1) * CHUNK, h * DK:(h + 1) * DK])
                tot = cum[CHUNK - 1:CHUNK]
                w = jnp.exp(tot - cum)
                decay = jnp.exp(tot)
                kd = pa_ref[rows, pl.ds(K_OFF + h * DK, DK)].astype(F32) * w
                kd16 = kd.astype(BF16)
                st = st_ref[c, h]
                if c > 0:
                    st_prev = st_ref[c - 1, h]
                else:
                    st_prev = jnp.where(first_tile, 0.0, prev_ref[0, h])
                qs = (pa_ref[rows, pl.ds(Q_OFF + h * DK, DK)].astype(F32) * Q_SCALE).astype(BF16)
                st16 = st.astype(BF16)
                o = _dot(qs, st16, NT)
                rs = lax.rsqrt(jnp.mean(o * o, axis=-1, keepdims=True) + EPS)
                oh = o * rs
                rr = pa_ref[rows, pl.ds(R_OFF + h * DV, DV)].astype(F32)
                sr = _sig(rr)
                dyv = dy_ref[rows, pl.ds(h * DV, DV)].astype(F32)
                dpa_ref[rows, pl.ds(R_OFF + h * DV, DV)] = (
                    dyv * oh * gn * sr * (1.0 + rr * (1.0 - sr))).astype(BF16)
                don = dyv * (rr * sr)
                dgn = dgn + jnp.sum(don * oh, axis=0, keepdims=True)
                doh = don * gn
                do16 = (rs * (doh - oh * jnp.mean(doh * oh, axis=-1, keepdims=True))).astype(BF16)
                dpa_ref[rows, pl.ds(Q_OFF + h * DK, DK)] = (_dot(do16, st16, NN) * Q_SCALE).astype(BF16)
                gt = _dot(do16, qs, TN) + carry[h]
                gt16 = gt.astype(BF16)
                dkd = _dot(pa_ref[rows, pl.ds(V_OFF + h * DV, DV)], gt16, NN)
                dpa_ref[rows, pl.ds(V_OFF + h * DV, DV)] = _dot(kd16, gt16, NT).astype(BF16)
                ddecay = jnp.sum(gt * st_prev, axis=0, keepdims=True)
                dpa_ref[rows, pl.ds(K_OFF + h * DK, DK)] = (dkd * w).astype(BF16)
                e = dkd * kd
                dtot = jnp.sum(e, axis=0, keepdims=True) + ddecay * decay
                dls = dtot - _exact_mask_dot(upper, e)
                dlogit = dls * (1.0 / 16.0) * sneg[c * CHUNK:(c + 1) * CHUNK, h * DK:(h + 1) * DK]
                dl_ref[rows, hk] = dlogit.astype(BF16)
                dbg = dbg + jnp.sum(dlogit, axis=0, keepdims=True)
                carry[h] = gt * decay
            dgn_ref[h] += dgn
            dbg_ref[:, hk] += dbg

    return pl.pallas_call(
        body, name=name, grid=(nt,),
        in_specs=[
            pl.BlockSpec((Tg, 2 * QK + 2 * GV), lambda t: (rev(t), 0)),
            pl.BlockSpec((Tg, LANES), lambda t: (rev(t), 0)),
            pl.BlockSpec((LANES, QK), lambda t: (0, 0)),
            pl.BlockSpec((1, QK), lambda t: (0, 0)),
            pl.BlockSpec((HEADS, 1, DV), lambda t: (0, 0, 0)),
            pl.BlockSpec((cb, HEADS, DV, DK), lambda t: (rev(t), 0, 0, 0)),
            pl.BlockSpec((1, HEADS, DV, DK), lambda t: (jnp.maximum(rev(t) * cb - 1, 0), 0, 0, 0)),
            pl.BlockSpec((Tg, GV), lambda t: (rev(t), 0)),
        ],
        out_specs=[
            pl.BlockSpec((Tg, 2 * QK + 2 * GV), lambda t: (rev(t), 0)),
            pl.BlockSpec((Tg, QK), lambda t: (rev(t), 0)),
            pl.BlockSpec((HEADS, 1, DV), lambda t: (0, 0, 0)),
            pl.BlockSpec((1, QK), lambda t: (0, 0)),
        ],
        out_shape=[jax.ShapeDtypeStruct((S, 2 * QK + 2 * GV), BF16), jax.ShapeDtypeStruct((S, QK), BF16),
                   jax.ShapeDtypeStruct((HEADS, 1, DV), F32), jax.ShapeDtypeStruct((1, QK), F32)],
        scratch_shapes=[pltpu.VMEM((HEADS, DV, DK), F32)],
        compiler_params=_cp(("arbitrary",), VMEM_BIG),
    )(pa, alow, wg, bgate, gnorm, states, states, dy)


SGU_TILE = 256


def _sgu_mask():
    r = lax.broadcasted_iota(jnp.int32, (SBLOCK, SBLOCK), 0)
    c = lax.broadcasted_iota(jnp.int32, (SBLOCK, SBLOCK), 1)
    return (c < CHUNK) | (r >= CHUNK)


def _ln_stats(vf):
    mu = jnp.mean(vf, axis=-1, keepdims=True)
    xc = vf - mu
    rs = lax.rsqrt(jnp.mean(xc * xc, axis=-1, keepdims=True) + EPS)
    return rs, xc * rs


def _sgu_fwd_call(ps, ln_g, ln_b, w_sp, b_sp, *, name):
    S = ps.shape[0]
    Ts = min(SGU_TILE, S)

    def body(ps_ref, lg_ref, lb_ref, w_ref, b_ref, y_ref):
        mask = _sgu_mask()
        for g in range(GROUPS):
            wm = jnp.where(mask, w_ref[g], 0.0).astype(BF16)
            for p in range(Ts // SBLOCK):
                rows = pl.ds(p * SBLOCK, SBLOCK)
                u = _gelu(ps_ref[rows, pl.ds(g * DG, DG)].astype(F32))
                _, xh = _ln_stats(_gelu(ps_ref[rows, pl.ds(D_MODEL + g * DG, DG)].astype(F32)))
                vn = xh * lg_ref[g] + lb_ref[g]
                mixed = _dot(wm, vn.astype(BF16), NN) + b_ref[g]
                y_ref[rows, pl.ds(g * DG, DG)] = (u * mixed).astype(BF16)

    full3 = lambda a, b, c: pl.BlockSpec((a, b, c), lambda t: (0, 0, 0))
    return pl.pallas_call(
        body, name=name, grid=(S // Ts,),
        in_specs=[pl.BlockSpec((Ts, 2 * D_MODEL), lambda t: (t, 0)),
                  full3(GROUPS, 1, DG), full3(GROUPS, 1, DG), full3(GROUPS, SBLOCK, SBLOCK), full3(GROUPS, SBLOCK, 1)],
        out_specs=pl.BlockSpec((Ts, D_MODEL), lambda t: (t, 0)),
        out_shape=jax.ShapeDtypeStruct((S, D_MODEL), BF16),
        compiler_params=_cp(("parallel",)),
    )(ps, ln_g, ln_b, w_sp, b_sp)


def _sgu_bwd_call(ps, ln_g, ln_b, w_sp, b_sp, dy, *, name):
    S = ps.shape[0]
    Ts = min(SGU_TILE, S)

    def body(ps_ref, lg_ref, lb_ref, w_ref, b_ref, dy_ref, ds_ref, dlg_ref, dlb_ref, dw_ref, db_ref):
        @pl.when(pl.program_id(0) == 0)
        def _():
            dlg_ref[...] = jnp.zeros_like(dlg_ref)
            dlb_ref[...] = jnp.zeros_like(dlb_ref)
            dw_ref[...] = jnp.zeros_like(dw_ref)
            db_ref[...] = jnp.zeros_like(db_ref)

        mask = _sgu_mask()
        for g in range(GROUPS):
            wm = jnp.where(mask, w_ref[g], 0.0).astype(BF16)
            lg = lg_ref[g]
            for p in range(Ts // SBLOCK):
                rows = pl.ds(p * SBLOCK, SBLOCK)
                su = ps_ref[rows, pl.ds(g * DG, DG)].astype(F32)
                sv = ps_ref[rows, pl.ds(D_MODEL + g * DG, DG)].astype(F32)
                u = _gelu(su)
                rs, xh = _ln_stats(_gelu(sv))
                vn16 = (xh * lg + lb_ref[g]).astype(BF16)
                mixed = _dot(wm, vn16, NN) + b_ref[g]
                dyv = dy_ref[rows, pl.ds(g * DG, DG)].astype(F32)
                ds_ref[rows, pl.ds(g * DG, DG)] = (dyv * mixed * _gelu_grad(su)).astype(BF16)
                dmix = dyv * u
                dmix16 = dmix.astype(BF16)
                db_ref[g] += jnp.sum(dmix, axis=-1, keepdims=True)
                dw_ref[g] += jnp.where(mask, _dot(dmix16, vn16, NT), 0.0)
                dvn = _dot(wm, dmix16, TN)
                dlg_ref[g] += jnp.sum(dvn * xh, axis=0, keepdims=True)
                dlb_ref[g] += jnp.sum(dvn, axis=0, keepdims=True)
                dxh = dvn * lg
                dvf = rs * (dxh - jnp.mean(dxh, axis=-1, keepdims=True)
                            - xh * jnp.mean(dxh * xh, axis=-1, keepdims=True))
                ds_ref[rows, pl.ds(D_MODEL + g * DG, DG)] = (dvf * _gelu_grad(sv)).astype(BF16)

    full3 = lambda a, b, c: pl.BlockSpec((a, b, c), lambda t: (0, 0, 0))
    return pl.pallas_call(
        body, name=name, grid=(S // Ts,),
        in_specs=[pl.BlockSpec((Ts, 2 * D_MODEL), lambda t: (t, 0)),
                  full3(GROUPS, 1, DG), full3(GROUPS, 1, DG), full3(GROUPS, SBLOCK, SBLOCK), full3(GROUPS, SBLOCK, 1),
                  pl.BlockSpec((Ts, D_MODEL), lambda t: (t, 0))],
        out_specs=[pl.BlockSpec((Ts, 2 * D_MODEL), lambda t: (t, 0)),
                   full3(GROUPS, 1, DG), full3(GROUPS, 1, DG), full3(GROUPS, SBLOCK, SBLOCK), full3(GROUPS, SBLOCK, 1)],
        out_shape=[jax.ShapeDtypeStruct((S, 2 * D_MODEL), BF16),
                   jax.ShapeDtypeStruct((GROUPS, 1, DG), F32), jax.ShapeDtypeStruct((GROUPS, 1, DG), F32),
                   jax.ShapeDtypeStruct((GROUPS, SBLOCK, SBLOCK), F32),
                   jax.ShapeDtypeStruct((GROUPS, SBLOCK, 1), F32)],
        compiler_params=_cp(("arbitrary",)),
    )(ps, ln_g, ln_b, w_sp, b_sp, dy)


def _position():
    return lax.axis_index("x"), lax.axis_index("y"), lax.axis_index("c")


def _gather_copies(srcs, dsts, send_sems, recv_sems, local_sems):
    x, y, c = _position()
    me, sibling = (x, y, c), (x, y, 1 - c)
    chips = [(1 - x, y), (x, 1 - y), (1 - x, 1 - y)]
    n = len(srcs)

    def slab(a, block):
        px, py, pc = block
        return dsts[a].at[4 * px + 2 * py + pc]

    def copy(a, k, block, to, src=None):
        return pltpu.make_async_remote_copy(
            src_ref=slab(a, block) if src is None else src, dst_ref=slab(a, block),
            send_sem=send_sems.at[7 * a + k], recv_sem=recv_sems.at[7 * a + k], device_id=to, device_id_type=MESH)

    mine = [pltpu.make_async_copy(srcs[a], slab(a, me), local_sems.at[a]) for a in range(n)]
    for cp in mine:
        cp.start()
    first = []
    for a in range(n):
        first.append(copy(a, 0, me, sibling, src=srcs[a]))
        first += [copy(a, 1 + j, me, (*chip, c), src=srcs[a]) for j, chip in enumerate(chips)]
    for cp in first:
        cp.start()
    passed = []
    for j, chip in enumerate(chips):
        for a in range(n):
            copy(a, 1 + j, (*chip, c), me).wait_recv()
            cp = copy(a, 4 + j, (*chip, c), sibling)
            cp.start()
            passed.append(cp)
    for a in range(n):
        copy(a, 0, sibling, me).wait_recv()
        for j, chip in enumerate(chips):
            copy(a, 4 + j, (*chip, 1 - c), me).wait_recv()
    for cp in first + passed:
        cp.wait_send()
    for cp in mine:
        cp.wait()


def _all_gather_hbm(shards, *, name):
    n = len(shards)

    def body(*refs):
        srcs, dsts = refs[:n], refs[n:2 * n]
        send_sems, recv_sems, local_sems = refs[2 * n:]
        _gather_copies(srcs, dsts, send_sems, recv_sems, local_sems)

    return pl.pallas_call(
        body, name=name,
        in_specs=[ANY] * n, out_specs=[ANY] * n,
        out_shape=[jax.ShapeDtypeStruct((N_DEV, *s.shape), s.dtype) for s in shards],
        scratch_shapes=_comm_sems(n),
    )(*shards)


def _all_reduce_small(part, *, name):
    R, W = part.shape

    def body(x_ref, out_ref, gathered, send_sems, recv_sems, local_sems):
        _gather_copies([x_ref], [gathered], send_sems, recv_sems, local_sems)
        acc = gathered[0]
        for d in range(1, N_DEV):
            acc = acc + gathered[d]
        out_ref[...] = acc

    return pl.pallas_call(
        body, name=name,
        in_specs=[VMEM_SPEC], out_specs=VMEM_SPEC,
        out_shape=jax.ShapeDtypeStruct((R, W), F32),
        scratch_shapes=[pltpu.VMEM((N_DEV, R, W), F32),
                        pltpu.SemaphoreType.DMA((7,)), pltpu.SemaphoreType.DMA((7,)), pltpu.SemaphoreType.DMA((1,))],
    )(part)


FLIPS = [(fx, fy, fc) for fx in (0, 1) for fy in (0, 1) for fc in (0, 1)][1:]


def _scatter_copies(srcs, dsts, send_sems, recv_sems, local_sems):
    n = len(srcs)
    x, y, c = _position()
    me = 4 * x + 2 * y + c
    mine = [pltpu.make_async_copy(srcs[a].at[me], dsts[a].at[me], local_sems.at[a]) for a in range(n)]
    for cp in mine:
        cp.start()
    copies = []
    for k, (fx, fy, fc) in enumerate(FLIPS):
        tx = 1 - x if fx else x
        ty = 1 - y if fy else y
        tc = 1 - c if fc else c
        peer = 4 * tx + 2 * ty + tc
        for a in range(n):
            cp = pltpu.make_async_remote_copy(
                src_ref=srcs[a].at[peer], dst_ref=dsts[a].at[me],
                send_sem=send_sems.at[7 * a + k], recv_sem=recv_sems.at[7 * a + k],
                device_id=(tx, ty, tc), device_id_type=MESH)
            cp.start()
            copies.append(cp)
    for cp in copies:
        cp.wait()
    for cp in mine:
        cp.wait()


def _comm_sems(n):
    return [pltpu.SemaphoreType.DMA((7 * n,)), pltpu.SemaphoreType.DMA((7 * n,)), pltpu.SemaphoreType.DMA((n,))]


def _scatter_blocks(parts, *, name):
    n = len(parts)

    def body(*refs):
        _scatter_copies(refs[:n], refs[n:2 * n], *refs[2 * n:])

    return pl.pallas_call(
        body, name=name,
        in_specs=[ANY] * n, out_specs=[ANY] * n,
        out_shape=[jax.ShapeDtypeStruct(p.shape, p.dtype) for p in parts],
        scratch_shapes=_comm_sems(n),
    )(*parts)


def _handshake(peers):
    barrier = pltpu.get_barrier_semaphore()
    for peer in peers:
        pl.semaphore_signal(barrier, inc=1, device_id=peer, device_id_type=MESH)
    pl.semaphore_wait(barrier, len(peers))


def _sequencer_call(arrays, out_types, copies_fn, peers_fn, *, name, collective_id):
    n = len(arrays)
    srcs = [jax.new_ref(a, memory_space=pltpu.MemorySpace.HBM) for a in arrays]
    dsts = [jax.empty_ref(t, memory_space=pltpu.MemorySpace.HBM) for t in out_types]

    @pl.kernel(mesh=plsc.ScalarSubcoreMesh(axis_name="sequencer", num_cores=1), name=name,
               scratch_types=_comm_sems(n), compiler_params=pltpu.CompilerParams(collective_id=collective_id))
    def launch(send_sems, recv_sems, local_sems):
        _handshake(peers_fn())
        copies_fn(srcs, dsts, send_sems, recv_sems, local_sems)

    launch()
    return [d[...] for d in dsts]


def _all_other_devices():
    x, y, c = _position()
    return [(1 - x if fx else x, 1 - y if fy else y, 1 - c if fc else c) for fx, fy, fc in FLIPS]


def _gather_peers():
    x, y, c = _position()
    return [(x, y, 1 - c), (1 - x, y, c), (x, 1 - y, c), (1 - x, 1 - y, c)]


def _scatter_blocks_async(parts, *, name, collective_id):
    return _sequencer_call(parts, [jax.ShapeDtypeStruct(p.shape, p.dtype) for p in parts],
                           _scatter_copies, _all_other_devices, name=name, collective_id=collective_id)


def _all_gather_async(shards, *, name, collective_id):
    return _sequencer_call(shards, [jax.ShapeDtypeStruct((N_DEV, *s.shape), s.dtype) for s in shards],
                           _gather_copies, _gather_peers, name=name, collective_id=collective_id)


def _adamw_math(w, g, m, v):
    m = ADAM_B1 * m + (1.0 - ADAM_B1) * g
    v = ADAM_B2 * v + (1.0 - ADAM_B2) * (g * g)
    m_hat = m / (1.0 - ADAM_B1 ** ADAM_STEP)
    v_hat = v / (1.0 - ADAM_B2 ** ADAM_STEP)
    delta = -ADAM_LR * (m_hat / (jnp.sqrt(v_hat) + ADAM_EPS) + ADAM_WD * w)
    return delta, m, v


def _adamw_reduce_call(recv, w, m, v, *, name, T=128):
    R, W = w.shape
    T = T if R % T == 0 else R // 2
    assert R % T == 0 and T % 16 == 0, (R, T)

    def body(p_ref, w_ref, m_ref, v_ref, g_out, d_out, m_out, v_out):
        g = p_ref[0].astype(F32)
        for d in range(1, N_DEV):
            g = g + p_ref[d].astype(F32)
        g_out[...] = g
        d_out[...], m_out[...], v_out[...] = _adamw_math(w_ref[...], g, m_ref[...], v_ref[...])

    row = pl.BlockSpec((T, W), lambda i: (i, 0))
    out = jax.ShapeDtypeStruct((R, W), F32)
    return pl.pallas_call(
        body, name=name, grid=(R // T,),
        in_specs=[pl.BlockSpec((N_DEV, T, W), lambda i: (0, i, 0)), row, row, row],
        out_specs=[row] * 4, out_shape=[out] * 4,
        compiler_params=_cp(("parallel",), VMEM_BIG),
    )(recv, w, m, v)


def _adamw_small_call(w, g, m, v, *, name):
    def body(w_ref, g_ref, m_ref, v_ref, d_out, m_out, v_out):
        d_out[...], m_out[...], v_out[...] = _adamw_math(w_ref[...], g_ref[...], m_ref[...], v_ref[...])

    out = jax.ShapeDtypeStruct(w.shape, F32)
    return pl.pallas_call(body, name=name, in_specs=[VMEM_SPEC] * 4, out_specs=[VMEM_SPEC] * 3,
                          out_shape=[out] * 3)(w, g, m, v)


def _tile_rows(n_elems):
    return -(-n_elems // (8 * LANES)) * 8


def _pack_rows(parts, rows):
    pieces = []
    for p in parts:
        q = p.reshape(-1, LANES)
        pieces.append(jnp.pad(q, ((0, _tile_rows(p.size) - q.shape[0]), (0, 0))))
    buf = jnp.concatenate(pieces, axis=0)
    return jnp.pad(buf, ((0, rows - buf.shape[0]), (0, 0)))


def _unpack_rows(buf, shapes):
    out, off = [], 0
    for shp in shapes:
        n = 1
        for s in shp:
            n *= s
        out.append(buf[off:off + n // LANES].reshape(shp))
        off += _tile_rows(n)
    return out


def _in_columns(blocks, lo, hi):
    pieces = []
    for d in range(N_DEV):
        s, e = max(lo, IN_BLK * d), min(hi, IN_BLK * (d + 1))
        if s < e:
            pieces.append(blocks[d][:, s - IN_BLK * d:e - IN_BLK * d])
    return jnp.concatenate(pieces, axis=1)


def _in_blocks(segments):
    blocks = []
    for d in range(N_DEV):
        lo, hi = IN_BLK * d, IN_BLK * (d + 1)
        pieces = []
        for arr, c0 in segments:
            s, e = max(lo, c0), min(hi, c0 + arr.shape[1])
            if s < e:
                pieces.append(arr[:, s - c0:e - c0])
        blocks.append(jnp.concatenate(pieces, axis=1))
    return jnp.stack(blocks)


def _local_step(x, target, W, scatter, finish):
    g1, g2, g3, g4 = [W[n].reshape(1, D_MODEL) for n in ("norm_pre_mix", "norm_post_mix", "norm_pre_ffn", "norm_post_ffn")]
    w_a, w_low, w_s, w_g = W["w_in_a"], W["w_in_low"], W["w_in_s"], W["w_in_g"]
    wfi, wfo = W["w_ffn_in"], W["w_ffn_out"].reshape(4, FF_BLK, D_MODEL)
    wg = jnp.pad(W["w_gate_up"], ((0, LANES - RANK), (0, 0))).astype(BF16)
    bgate = W["b_gate"].reshape(1, QK)
    gnorm = W["gla_norm"].reshape(HEADS, 1, DV)
    ln_g = W["sgu_ln_g"].reshape(GROUPS, 1, DG)
    ln_b = W["sgu_ln_b"].reshape(GROUPS, 1, DG)
    w_sp = W["w_spatial"]
    b_sp = W["b_spatial"].reshape(GROUPS, SBLOCK, 1)

    a, pa, alow, ps, pg = _in_proj_call(x, g1, w_a, w_low, w_s, w_g, name="in_proj")
    y_gla, states = _gla_fwd_call(pa, alow, wg, bgate, gnorm, name="gla_fwd")
    y_sgu = _sgu_fwd_call(ps, ln_g, ln_b, w_sp, b_sp, name="sgu_fwd")
    t1, t2, merged, mix, x1, h = _mixer_tail_call(y_gla, y_sgu, pg, x, W["w_branch_gla"], W["w_branch_sgu"],
                                                  W["w_out"], g2, g3, name="mixer_tail")
    gu, z = _ffn_in_call(h, wfi, name="ffn_in")
    loss, dx2, dy, dg4 = _ffn_out_loss_call(z, wfo, x1, target, g4, name="ffn_out_loss")

    grads = {"norm_post_ffn": dg4}
    done = {}
    dgu = _ffn_out_bwd_call(dy, wfo, gu, name="ffn_out_bwd")
    dw_ffn_out = _weight_grads_call([z, dy], [(0, 1)], name="d_ffn_out_w")[0].reshape(N_DEV, D_FF // N_DEV, D_MODEL)
    dw_ffn_in = _ffn_in_grad_call(h, dgu, name="d_ffn_in_w")
    dgu, dw_ffn_out, dw_ffn_in = lax.optimization_barrier((dgu, dw_ffn_out, dw_ffn_in))
    ffn_received = scatter(("w_ffn_out", "w_ffn_in"), [dw_ffn_out, dw_ffn_in])
    dx1, dmix, grads["norm_pre_ffn"], grads["norm_post_mix"] = _ffn_in_bwd_call(dgu, wfi, dx2, x1, mix, g3, g2, name="ffn_in_bwd")
    dt1, dt2, dpg, dy_gla, dy_sgu = _mixer_bwd_call(dmix, t1, t2, pg, W["w_out"], W["w_branch_gla"], W["w_branch_sgu"],
                                                    name="mixer_bwd")
    rows = D_MODEL // N_DEV
    mixer_grads = _weight_grads_call([merged, dmix, y_gla, dt1, y_sgu, dt2], [(0, 1), (2, 3), (4, 5)], name="d_mixer_w")
    dy_gla, dy_sgu, mixer_grads = lax.optimization_barrier((dy_gla, dy_sgu, mixer_grads))
    mixer_received = scatter(("w_out", "w_branch_gla", "w_branch_sgu"),
                             [g.reshape(N_DEV, rows, D_MODEL) for g in mixer_grads])
    dpa, dlogit, dgn, dbg = _gla_bwd_call(pa, alow, wg, bgate, gnorm, states, dy_gla, name="gla_bwd")
    ffn_done, dpa, dlogit = lax.optimization_barrier((finish(ffn_received), dpa, dlogit))
    done.update(ffn_done)
    dps, dlg, dlb, dwsp, dbsp = _sgu_bwd_call(ps, ln_g, ln_b, w_sp, b_sp, dy_sgu, name="sgu_bwd")
    mixer_done, dps = lax.optimization_barrier((finish(mixer_received), dps))
    done.update(mixer_done)
    dlow = _mm(dlogit, wg, "nt", BF16, name="d_gate_up_x")
    dw_a, dw_low = _weight_grads_call([a, dpa, dlow], [(0, 1), (0, 2)], name="d_in_w_qkvr")
    dw_s, dw_g = _weight_grads_call([a, dps, dpg], [(0, 1), (0, 2)], name="d_in_w_sgu_gates")
    dw_in = _in_blocks([(dw_a, A_COLS[0]), (dw_low[:, :RANK], LOW_COLS[0]), (dw_s, S_COLS[0]), (dw_g, G_COLS[0])])
    in_received = scatter(("w_in",), [dw_in])
    dwg = _mm(alow, dlogit, "tn", F32, name="d_gate_up_w")
    grad_x, grads["norm_pre_mix"] = _in_proj_bwd_call(dpa, dlow, dps, dpg, w_a, w_low, w_s, w_g, x, dx1, g1,
                                                      name="in_proj_bwd")

    grads["w_gate_up"] = dwg[:RANK]
    grads["b_gate"] = dbg
    grads["gla_norm"] = dgn
    grads["sgu_ln_g"] = dlg
    grads["sgu_ln_b"] = dlb
    grads["w_spatial"] = dwsp
    grads["b_spatial"] = dbsp
    done.update(finish(in_received))
    return loss, grad_x, grads, done


WEIGHTS = ("norm_pre_mix", "w_in", "w_gate_up", "b_gate", "gla_norm", "sgu_ln_g", "sgu_ln_b", "w_spatial",
           "b_spatial", "w_branch_gla", "w_branch_sgu", "w_out", "norm_post_mix", "norm_pre_ffn", "w_ffn_in",
           "w_ffn_out", "norm_post_ffn")
BIG = ("w_in", "w_branch_gla", "w_branch_sgu", "w_out", "w_ffn_in", "w_ffn_out")
MIXER = ("w_branch_gla", "w_branch_sgu", "w_out")
FFN = ("w_ffn_in", "w_ffn_out")
SMALL = tuple(n for n in WEIGHTS if n not in BIG)
SMALL_SHARDED = ("w_gate_up", "gla_norm", "sgu_ln_g", "sgu_ln_b")
SMALL_FULL = {"norm_pre_mix": (1024,), "w_gate_up": (16, 512), "b_gate": (512,), "gla_norm": (4, 256),
              "sgu_ln_g": (4, 256), "sgu_ln_b": (4, 256), "w_spatial": (4, 128, 128), "b_spatial": (4, 128),
              "norm_post_mix": (1024,), "norm_pre_ffn": (1024,), "norm_post_ffn": (1024,)}
SMALL_GRAD_ROWS = 648
SMALL_STATE_ROWS = 600
SMALL_GATHER_ROWS = 32


def kernel(x, norm_pre_mix, w_in, w_gate_up, b_gate, gla_norm, sgu_ln_g, sgu_ln_b, w_spatial, b_spatial, w_branch_gla, w_branch_sgu, w_out, norm_post_mix, norm_pre_ffn, w_ffn_in, w_ffn_out, norm_post_ffn, loss_target, m_norm_pre_mix, m_w_in, m_w_gate_up, m_b_gate, m_gla_norm, m_sgu_ln_g, m_sgu_ln_b, m_w_spatial, m_b_spatial, m_w_branch_gla, m_w_branch_sgu, m_w_out, m_norm_post_mix, m_norm_pre_ffn, m_w_ffn_in, m_w_ffn_out, m_norm_post_ffn, v_norm_pre_mix, v_w_in, v_w_gate_up, v_b_gate, v_gla_norm, v_sgu_ln_g, v_sgu_ln_b, v_w_spatial, v_b_spatial, v_w_branch_gla, v_w_branch_sgu, v_w_out, v_norm_post_mix, v_norm_pre_ffn, v_w_ffn_in, v_w_ffn_out, v_norm_post_ffn):
    given = dict(locals())
    w = {n: given[n][0] for n in WEIGHTS}
    m = {n: given["m_" + n][0] for n in WEIGHTS}
    v = {n: given["v_" + n][0] for n in WEIGHTS}
    xs, target = x[0], loss_target[0]
    me = 4 * lax.axis_index("x") + 2 * lax.axis_index("y") + lax.axis_index("c")

    small_shard = _pack_rows([w[n] for n in SMALL_SHARDED], SMALL_GATHER_ROWS)
    first = _all_gather_hbm([w["w_in"].astype(BF16), small_shard], name="gather_w_in")
    rest, first = lax.optimization_barrier(([w[n].astype(BF16) for n in MIXER + FFN], first))
    rest_blocks = _all_gather_async(rest, name="gather_rest", collective_id=1)
    W = {n: w[n] for n in SMALL if n not in SMALL_SHARDED}
    blocks = {"w_in": first[0], **dict(zip(MIXER + FFN, rest_blocks))}
    for n in ("w_branch_gla", "w_branch_sgu", "w_out", "w_ffn_out"):
        W[n] = blocks[n].reshape(-1, D_MODEL)
    W["w_ffn_in"] = blocks["w_ffn_in"]
    W["w_in_a"] = _in_columns(blocks["w_in"], *A_COLS)
    W["w_in_low"] = jnp.pad(_in_columns(blocks["w_in"], *LOW_COLS), ((0, 0), (0, LANES - RANK)))
    W["w_in_s"] = _in_columns(blocks["w_in"], *S_COLS)
    W["w_in_g"] = _in_columns(blocks["w_in"], *G_COLS)
    small_blocks = first[1]
    off = 0
    for n in SMALL_SHARDED:
        r, c = w[n].shape
        blk = small_blocks[:, off:off + r * c // LANES].reshape(N_DEV, r, c)
        W[n] = blk.transpose(1, 0, 2).reshape(r, N_DEV * c)
        off += _tile_rows(r * c)

    scatter_ids = iter((3, 4, 5))

    def scatter(names, parts):
        got = _scatter_blocks_async(parts, name="scatter_" + "_".join(names), collective_id=next(scatter_ids))
        return dict(zip(names, got))

    def finish(received):
        return {n: _adamw_reduce_call(r, w[n], m[n], v[n], name="adamw_" + n) for n, r in received.items()}

    loss_part, grad_x, grads, big_done = _local_step(xs, target, W, scatter, finish)

    small_part = jnp.concatenate([_pack_rows([grads[n] for n in SMALL], SMALL_GRAD_ROWS),
                                  jnp.broadcast_to(loss_part, (8, LANES))], axis=0)
    small_sum = _all_reduce_small(small_part, name="reduce_small")
    loss = small_sum[SMALL_GRAD_ROWS, 0]
    g_small = dict(zip(SMALL, _unpack_rows(small_sum, [SMALL_FULL[n] for n in SMALL])))
    for n in SMALL_SHARDED:
        c = w[n].shape[1]
        g_small[n] = lax.dynamic_slice_in_dim(g_small[n], me * c, c, axis=1)

    g_big, d_big, m_big, v_big = [{n: big_done[n][i] for n in BIG} for i in range(4)]
    small_shapes = [w[n].shape for n in SMALL]
    small_state = [_pack_rows([s[n] for n in SMALL], SMALL_STATE_ROWS) for s in (w, g_small, m, v)]
    small_out = _adamw_small_call(*small_state, name="adamw_small")
    d_small, m_small, v_small = [dict(zip(SMALL, _unpack_rows(o, small_shapes))) for o in small_out]

    def pick(big, sml):
        return [(big[n] if n in BIG else sml[n])[None] for n in WEIGHTS]

    return (loss, grad_x[None], *pick(g_big, g_small), *pick(d_big, d_small), *pick(m_big, m_small),
            *pick(v_big, v_small))
```

```python
import jax
import jax.numpy as jnp
from jax import lax
from jax.experimental import pallas as pl
from jax.experimental.pallas import tpu as pltpu
from jax.experimental.pallas import tpu_sc as plsc

F32 = jnp.float32
BF16 = jnp.bfloat16

D_MODEL = 1024
N_DEV = 8
CHUNK = 64
HEADS = 4
DK = 128
DV = 256
QK = HEADS * DK
GV = HEADS * DV
RANK = 16
GROUPS = 4
SBLOCK = 128
DG = 256
D_FF = 2816
FF_BLK = 704
EPS = 1e-6
Q_SCALE = DK ** -0.5
LANES = 128
VMEM_BIG = 48 * 1024 * 1024

D_IN = 7184
IN_BLK = 898
A_COLS = (0, 3072)
LOW_COLS = (3072, 3088)
S_COLS = (3088, 5136)
G_COLS = (5136, 7184)

ADAM_LR = 0.001
ADAM_B1 = 0.9
ADAM_B2 = 0.999
ADAM_EPS = 1e-08
ADAM_WD = 0.01
ADAM_STEP = 10

MESH = pl.DeviceIdType.MESH
ANY = pl.BlockSpec(memory_space=pl.ANY)
VMEM_SPEC = pl.BlockSpec(memory_space=pltpu.VMEM)


def _cp(sem=None, vmem=None):
    return pltpu.CompilerParams(dimension_semantics=sem, vmem_limit_bytes=vmem)


def _sig(x):
    return 1.0 / (1.0 + jnp.exp(-x))


def _gelu(x):
    c = 0.7978845608028654
    t = jnp.tanh(c * (x + 0.044715 * x * x * x))
    return 0.5 * x * (1.0 + t)


def _gelu_grad(x):
    c = 0.7978845608028654
    x2 = x * x
    t = jnp.tanh(c * (x + 0.044715 * x * x2))
    return 0.5 * (1.0 + t) + 0.5 * x * (1.0 - t * t) * c * (1.0 + 3.0 * 0.044715 * x2)


def _logsig(x):
    return jnp.minimum(x, 0.0) - jnp.log1p(jnp.exp(-jnp.abs(x)))


def _dot(a, b, dims):
    return lax.dot_general(a, b, (dims, ((), ())), preferred_element_type=F32)


NN = ((1,), (0,))
NT = ((1,), (1,))
TN = ((0,), (0,))


def _exact_mask_dot(mask_bf16, x):
    hi = x.astype(BF16)
    r1 = x - hi.astype(F32)
    mid = r1.astype(BF16)
    lo = (r1 - mid.astype(F32)).astype(BF16)
    return _dot(mask_bf16, hi, NN) + _dot(mask_bf16, mid, NN) + _dot(mask_bf16, lo, NN)


def _pick_tile(dim, target):
    if dim <= target:
        return dim
    best = None
    for t in range(LANES, int(1.4 * target) + 1, LANES):
        if dim % t == 0:
            best = t
    assert best is not None, (dim, target)
    return best


def _mm_call(a, b, *, name, grid, a_spec, b_spec, o_spec, out_shape, dims, acc_shape):
    nk = grid[2]

    def body(a_ref, b_ref, o_ref, *acc):
        part = _dot(a_ref[...], b_ref[...], dims)
        if nk == 1:
            o_ref[...] = part.astype(o_ref.dtype)
        else:
            acc_ref = acc[0]
            k = pl.program_id(2)

            @pl.when(k == 0)
            def _():
                acc_ref[...] = part

            @pl.when(k > 0)
            def _():
                acc_ref[...] += part

            @pl.when(k == nk - 1)
            def _():
                o_ref[...] = acc_ref[...].astype(o_ref.dtype)

    return pl.pallas_call(
        body, name=name, grid=grid, in_specs=[a_spec, b_spec], out_specs=o_spec, out_shape=out_shape,
        scratch_shapes=[] if nk == 1 else [pltpu.VMEM(acc_shape, F32)],
        compiler_params=_cp(("parallel", "parallel", "arbitrary"), VMEM_BIG),
    )(a, b)


def _mm(a, b, mode, out_dtype, *, name, tm=512, tn=1024, tk=1024):
    if mode == "nn":
        (M, K), (_, N) = a.shape, b.shape
    elif mode == "nt":
        (M, K), (N, _) = a.shape, b.shape
    else:
        (K, M), (_, N) = a.shape, b.shape
    tm, tn, tk = _pick_tile(M, tm), _pick_tile(N, tn), _pick_tile(K, tk)
    if mode == "nn":
        a_spec = pl.BlockSpec((tm, tk), lambda i, j, k: (i, k))
        b_spec = pl.BlockSpec((tk, tn), lambda i, j, k: (k, j))
        dims = NN
    elif mode == "nt":
        a_spec = pl.BlockSpec((tm, tk), lambda i, j, k: (i, k))
        b_spec = pl.BlockSpec((tn, tk), lambda i, j, k: (j, k))
        dims = NT
    else:
        a_spec = pl.BlockSpec((tk, tm), lambda i, j, k: (k, i))
        b_spec = pl.BlockSpec((tk, tn), lambda i, j, k: (k, j))
        dims = TN
    return _mm_call(a, b, name=name, grid=(M // tm, N // tn, K // tk), a_spec=a_spec, b_spec=b_spec,
                    o_spec=pl.BlockSpec((tm, tn), lambda i, j, k: (i, j)),
                    out_shape=jax.ShapeDtypeStruct((M, N), out_dtype), dims=dims, acc_shape=(tm, tn))


ROW_TILE = 512


def _rt(T, W, c=0):
    return pl.BlockSpec((T, W), lambda i: (i, c))


def _rt3(nb, T, W):
    return pl.BlockSpec((nb, T, W), lambda i: (0, i, 0))


def _res(shape):
    nd = len(shape)
    return pl.BlockSpec(tuple(shape), lambda i: (0,) * nd, pipeline_mode=pl.Buffered(1))


def _acc(shape):
    nd = len(shape)
    return pl.BlockSpec(tuple(shape), lambda i: (0,) * nd, pipeline_mode=pl.Buffered(1))


def _nbytes(shape, dtype):
    n = jnp.dtype(dtype).itemsize
    for s in shape:
        n *= s
    return n


def _vmem_limit(tiles, resident, temps=16 * 1024 * 1024):
    need = 2 * sum(_nbytes(s, d) for s, d in tiles) + sum(_nbytes(s, d) for s, d in resident) + temps
    return min(need, 60 * 1024 * 1024)


def _tok_call(body, *, name, S, T, ins, outs, semantics="parallel"):
    tiles = [(spec.block_shape, a.dtype) for a, spec, kind in ins + outs if kind == "tile"]
    resident = [(a.shape, a.dtype) for a, spec, kind in ins + outs if kind == "res"]
    return pl.pallas_call(
        body, name=name, grid=(S // T,),
        in_specs=[spec for _, spec, _ in ins], out_specs=[spec for _, spec, _ in outs],
        out_shape=[jax.ShapeDtypeStruct(a.shape, a.dtype) for a, _, _ in outs],
        compiler_params=_cp((semantics,), _vmem_limit(tiles, resident)),
    )(*[a for a, _, _ in ins])


def _tile(a, spec):
    return (a, spec, "tile")


def _whole(a):
    return (a, _res(a.shape), "res")


def _out_tile(shape, dtype, spec):
    return (jax.ShapeDtypeStruct(shape, dtype), spec, "tile")


def _out_acc(shape, dtype=F32):
    return (jax.ShapeDtypeStruct(shape, dtype), _acc(shape), "res")


def _rms_stats(x):
    r = lax.rsqrt(jnp.mean(x * x, axis=-1, keepdims=True) + EPS)
    return r, x * r


def _rms_bwd(xh, r, g, dy):
    dxh = dy * g
    dx = r * (dxh - xh * jnp.mean(dxh * xh, axis=-1, keepdims=True))
    dg = jnp.sum(dy * xh, axis=0, keepdims=True)
    return dx, dg


def _accum(ref, val):
    @pl.when(pl.program_id(0) == 0)
    def _():
        ref[...] = val

    @pl.when(pl.program_id(0) > 0)
    def _():
        ref[...] += val


def _dot_cols(a, w_ref, o_ref, dims, chunk=1024):
    n = o_ref.shape[1]
    for n0 in range(0, n, chunk):
        n1 = min(n, n0 + chunk)
        w = w_ref[:, n0:n1] if dims == NN else w_ref[n0:n1, :]
        o_ref[:, n0:n1] = _dot(a, w, dims).astype(o_ref.dtype)


def _in_proj_call(x, g1, w_a, w_low, w_s, w_g, *, name, T=ROW_TILE):
    S = x.shape[0]

    def body(x_ref, g_ref, wa_ref, wl_ref, ws_ref, wg_ref, a_ref, pa_ref, al_ref, ps_ref, pg_ref):
        _, xh = _rms_stats(x_ref[...])
        a = (xh * g_ref[...]).astype(BF16)
        a_ref[...] = a
        _dot_cols(a, wa_ref, pa_ref, NT)
        _dot_cols(a, wl_ref, al_ref, NT)
        _dot_cols(a, ws_ref, ps_ref, NT)
        _dot_cols(a, wg_ref, pg_ref, NT)

    widths = (D_MODEL, w_a.shape[0], w_low.shape[0], w_s.shape[0], w_g.shape[0])
    return _tok_call(
        body, name=name, S=S, T=T,
        ins=[_tile(x, _rt(T, D_MODEL)), _whole(g1), _whole(w_a), _whole(w_low), _whole(w_s), _whole(w_g)],
        outs=[_out_tile((S, w), BF16, _rt(T, w)) for w in widths])


def _mixer_tail_call(y_gla, y_sgu, pg, x, w_bg, w_bs, w_out, g2, g3, *, name, T=ROW_TILE):
    S = x.shape[0]

    def body(yg_ref, ys_ref, pg_ref, x_ref, wbg_ref, wbs_ref, wo_ref, g2_ref, g3_ref,
             t1_ref, t2_ref, mg_ref, mix_ref, x1_ref, h_ref):
        t1 = _dot(yg_ref[...], wbg_ref[...], NN)
        t2 = _dot(ys_ref[...], wbs_ref[...], NN)
        t1_ref[...] = t1.astype(BF16)
        t2_ref[...] = t2.astype(BF16)
        sg = _sig(pg_ref[:, :D_MODEL].astype(F32))
        ss = _sig(pg_ref[:, D_MODEL:].astype(F32))
        merged = (sg * t1 + ss * t2).astype(BF16)
        mg_ref[...] = merged
        mix = _dot(merged, wo_ref[...], NN)
        mix_ref[...] = mix
        _, mh = _rms_stats(mix)
        x1 = x_ref[...] + mh * g2_ref[...]
        x1_ref[...] = x1
        _, xh = _rms_stats(x1)
        h_ref[...] = (xh * g3_ref[...]).astype(BF16)

    row = _rt(T, D_MODEL)
    b16 = lambda: _out_tile((S, D_MODEL), BF16, row)
    f32 = lambda: _out_tile((S, D_MODEL), F32, row)
    return _tok_call(
        body, name=name, S=S, T=T,
        ins=[_tile(y_gla, row), _tile(y_sgu, row), _tile(pg, _rt(T, 2 * D_MODEL)), _tile(x, row),
             _whole(w_bg), _whole(w_bs), _whole(w_out), _whole(g2), _whole(g3)],
        outs=[b16(), b16(), b16(), f32(), f32(), b16()])


def _ffn_in_call(h, wfi, *, name, T=ROW_TILE):
    S = h.shape[0]

    def body(h_ref, w_ref, gu_ref, z_ref):
        hv = h_ref[...]
        for d in range(4):
            gate = _dot(hv, w_ref[d], NT)
            up = _dot(hv, w_ref[d + 4], NT)
            gu_ref[d] = gate.astype(BF16)
            gu_ref[d + 4] = up.astype(BF16)
            z_ref[d] = (gate * _sig(gate) * up).astype(BF16)

    return _tok_call(
        body, name=name, S=S, T=T,
        ins=[_tile(h, _rt(T, D_MODEL)), _whole(wfi)],
        outs=[_out_tile((N_DEV, S, FF_BLK), BF16, _rt3(N_DEV, T, FF_BLK)),
              _out_tile((4, S, FF_BLK), BF16, _rt3(4, T, FF_BLK))])


def _ffn_out_loss_call(z, wfo, x1, target, g4, *, name, T=ROW_TILE):
    S = x1.shape[0]

    def body(z_ref, w_ref, x1_ref, t_ref, g4_ref, loss_ref, dx2_ref, dy_ref, dg4_ref):
        y = _dot(z_ref[0], w_ref[0], NN)
        for d in range(1, 4):
            y = y + _dot(z_ref[d], w_ref[d], NN)
        r, yh = _rms_stats(y)
        diff = x1_ref[...] + yh * g4_ref[...] - t_ref[...]
        part = 0.5 * jnp.sum(jnp.mean(diff * diff, axis=-1, keepdims=True), axis=0, keepdims=True)
        _accum(loss_ref, jnp.broadcast_to(part, (1, LANES)))
        dx2 = diff * (1.0 / D_MODEL)
        dx2_ref[...] = dx2
        dy, dg = _rms_bwd(yh, r, g4_ref[...], dx2)
        dy_ref[...] = dy.astype(BF16)
        _accum(dg4_ref, dg)

    row = _rt(T, D_MODEL)
    return _tok_call(
        body, name=name, S=S, T=T, semantics="arbitrary",
        ins=[_tile(z, _rt3(4, T, FF_BLK)), _whole(wfo), _tile(x1, row), _tile(target, row), _whole(g4)],
        outs=[_out_acc((1, LANES)), _out_tile((S, D_MODEL), F32, row), _out_tile((S, D_MODEL), BF16, row),
              _out_acc((1, D_MODEL))])


def _ffn_out_bwd_call(dy, wfo, gu, *, name, T=ROW_TILE):
    S = dy.shape[0]

    def body(dy_ref, w_ref, gu_ref, dgu_ref):
        dyv = dy_ref[...]
        for d in range(4):
            dz = _dot(dyv, w_ref[d], NT)
            gt = gu_ref[d].astype(F32)
            up = gu_ref[d + 4].astype(F32)
            s = _sig(gt)
            dgu_ref[d] = (dz * up * s * (1.0 + gt * (1.0 - s))).astype(BF16)
            dgu_ref[d + 4] = (dz * gt * s).astype(BF16)

    blocks = _rt3(N_DEV, T, FF_BLK)
    return _tok_call(
        body, name=name, S=S, T=T,
        ins=[_tile(dy, _rt(T, D_MODEL)), _whole(wfo), _tile(gu, blocks)],
        outs=[_out_tile((N_DEV, S, FF_BLK), BF16, blocks)])[0]


def _ffn_in_bwd_call(dgu, wfi, dx2, x1, mix, g3, g2, *, name, T=ROW_TILE):
    S = x1.shape[0]

    def body(dgu_ref, w_ref, dx2_ref, x1_ref, mix_ref, g3_ref, g2_ref, dx1_ref, dmix_ref, dg3_ref, dg2_ref):
        dh = _dot(dgu_ref[0], w_ref[0], NN)
        for d in range(1, N_DEV):
            dh = dh + _dot(dgu_ref[d], w_ref[d], NN)
        r3, xh = _rms_stats(x1_ref[...])
        d3, dg3 = _rms_bwd(xh, r3, g3_ref[...], dh)
        dx1 = dx2_ref[...] + d3
        dx1_ref[...] = dx1
        r2, mh = _rms_stats(mix_ref[...])
        dmix, dg2 = _rms_bwd(mh, r2, g2_ref[...], dx1)
        dmix_ref[...] = dmix.astype(BF16)
        _accum(dg3_ref, dg3)
        _accum(dg2_ref, dg2)

    row = _rt(T, D_MODEL)
    return _tok_call(
        body, name=name, S=S, T=T, semantics="arbitrary",
        ins=[_tile(dgu, _rt3(N_DEV, T, FF_BLK)), _whole(wfi), _tile(dx2, row), _tile(x1, row), _tile(mix, row),
             _whole(g3), _whole(g2)],
        outs=[_out_tile((S, D_MODEL), F32, row), _out_tile((S, D_MODEL), BF16, row),
              _out_acc((1, D_MODEL)), _out_acc((1, D_MODEL))])


def _mixer_bwd_call(dmix, t1, t2, pg, w_out, w_bg, w_bs, *, name, T=ROW_TILE):
    S = dmix.shape[0]

    def body(dmix_ref, t1_ref, t2_ref, pg_ref, wo_ref, wbg_ref, wbs_ref, dt1_ref, dt2_ref, dpg_ref, dyg_ref, dys_ref):
        dm = _dot(dmix_ref[...], wo_ref[...], NT)
        sg = _sig(pg_ref[:, :D_MODEL].astype(F32))
        ss = _sig(pg_ref[:, D_MODEL:].astype(F32))
        dt1 = (dm * sg).astype(BF16)
        dt2 = (dm * ss).astype(BF16)
        dt1_ref[...] = dt1
        dt2_ref[...] = dt2
        dpg_ref[:, :D_MODEL] = (dm * t1_ref[...].astype(F32) * sg * (1.0 - sg)).astype(BF16)
        dpg_ref[:, D_MODEL:] = (dm * t2_ref[...].astype(F32) * ss * (1.0 - ss)).astype(BF16)
        dyg_ref[...] = _dot(dt1, wbg_ref[...], NT).astype(BF16)
        dys_ref[...] = _dot(dt2, wbs_ref[...], NT).astype(BF16)

    row = _rt(T, D_MODEL)
    wide = _rt(T, 2 * D_MODEL)
    b16 = lambda: _out_tile((S, D_MODEL), BF16, row)
    return _tok_call(
        body, name=name, S=S, T=T,
        ins=[_tile(dmix, row), _tile(t1, row), _tile(t2, row), _tile(pg, wide), _whole(w_out), _whole(w_bg), _whole(w_bs)],
        outs=[b16(), b16(), _out_tile((S, 2 * D_MODEL), BF16, wide), b16(), b16()])


def _in_proj_bwd_call(dpa, dlow, dps, dpg, w_a, w_low, w_s, w_g, x, dx1, g1, *, name, T=ROW_TILE):
    S = x.shape[0]

    def body(dpa_ref, dl_ref, dps_ref, dpg_ref, wa_ref, wl_ref, ws_ref, wg_ref, x_ref, dx1_ref, g1_ref, gx_ref, dg1_ref):
        da = (_dot(dpa_ref[...], wa_ref[...], NN) + _dot(dl_ref[...], wl_ref[...], NN)
              + _dot(dps_ref[...], ws_ref[...], NN) + _dot(dpg_ref[...], wg_ref[...], NN))
        r, xh = _rms_stats(x_ref[...])
        dxa, dg = _rms_bwd(xh, r, g1_ref[...], da)
        gx_ref[...] = dx1_ref[...] + dxa
        _accum(dg1_ref, dg)

    row = _rt(T, D_MODEL)
    return _tok_call(
        body, name=name, S=S, T=T, semantics="arbitrary",
        ins=[_tile(dpa, _rt(T, dpa.shape[1])), _tile(dlow, _rt(T, dlow.shape[1])), _tile(dps, _rt(T, dps.shape[1])),
             _tile(dpg, _rt(T, dpg.shape[1])), _whole(w_a), _whole(w_low), _whole(w_s), _whole(w_g),
             _tile(x, row), _tile(dx1, row), _whole(g1)],
        outs=[_out_tile((S, D_MODEL), F32, row), _out_acc((1, D_MODEL))])


TOKEN_TILE = 512


def _weight_grads_call(arrays, pairs, *, name, tk=TOKEN_TILE):
    S = arrays[0].shape[-2]
    tk = min(tk, S)
    n_in = len(arrays)

    def out_shape(i, j):
        a, b = arrays[i], arrays[j]
        if a.ndim == 3:
            return (a.shape[0], a.shape[2], b.shape[1])
        if b.ndim == 3:
            return (b.shape[0], a.shape[1], b.shape[2])
        return (a.shape[1], b.shape[1])

    shapes = [out_shape(i, j) for i, j in pairs]

    def body(*refs):
        ins, outs, accs = refs[:n_in], refs[n_in:n_in + len(pairs)], refs[n_in + len(pairs):]
        k = pl.program_id(0)

        @pl.when(k == 0)
        def _():
            for acc in accs:
                acc[...] = jnp.zeros_like(acc)

        for (i, j), acc in zip(pairs, accs):
            a_ref, b_ref = ins[i], ins[j]
            if len(a_ref.shape) == 3:
                for n in range(a_ref.shape[0]):
                    acc[n] += _dot(a_ref[n], b_ref[...], TN)
            elif len(b_ref.shape) == 3:
                a = a_ref[...]
                for n in range(b_ref.shape[0]):
                    acc[n] += _dot(a, b_ref[n], TN)
            else:
                b = b_ref[...]
                for m0 in range(0, a_ref.shape[1], 1024):
                    m1 = min(a_ref.shape[1], m0 + 1024)
                    acc[m0:m1, :] += _dot(a_ref[:, m0:m1], b, TN)

        @pl.when(k == S // tk - 1)
        def _():
            for out, acc in zip(outs, accs):
                out[...] = acc[...].astype(out.dtype)

    def in_spec(a):
        if a.ndim == 3:
            return pl.BlockSpec((a.shape[0], tk, a.shape[2]), lambda k: (0, k, 0))
        return pl.BlockSpec((tk, a.shape[1]), lambda k: (k, 0))

    tiles = [(in_spec(a).block_shape, a.dtype) for a in arrays]
    resident = [(s, F32) for s in shapes] + [(s, BF16) for s in shapes]
    return pl.pallas_call(
        body, name=name, grid=(S // tk,),
        in_specs=[in_spec(a) for a in arrays],
        out_specs=[_acc(s) for s in shapes],
        out_shape=[jax.ShapeDtypeStruct(s, BF16) for s in shapes],
        scratch_shapes=[pltpu.VMEM(s, F32) for s in shapes],
        compiler_params=_cp(("arbitrary",), _vmem_limit(tiles, resident)),
    )(*arrays)


def _ffn_in_grad_call(h, dgu, *, name, tk=TOKEN_TILE):
    S = h.shape[0]
    tk = min(tk, S)
    nb = 4
    shape = (nb, FF_BLK, D_MODEL)

    def body(h_ref, dgu_ref, out_ref, acc):
        k = pl.program_id(1)

        @pl.when(k == 0)
        def _():
            acc[...] = jnp.zeros_like(acc)

        hv = h_ref[...]
        for n in range(nb):
            acc[n] += _dot(dgu_ref[n], hv, TN)

        @pl.when(k == S // tk - 1)
        def _():
            out_ref[...] = acc[...].astype(out_ref.dtype)

    tiles = [((tk, D_MODEL), BF16), ((nb, tk, FF_BLK), BF16), (shape, BF16)]
    return pl.pallas_call(
        body, name=name, grid=(N_DEV // nb, S // tk),
        in_specs=[pl.BlockSpec((tk, D_MODEL), lambda g, k: (k, 0)),
                  pl.BlockSpec((nb, tk, FF_BLK), lambda g, k: (g, k, 0))],
        out_specs=pl.BlockSpec(shape, lambda g, k: (g, 0, 0)),
        out_shape=jax.ShapeDtypeStruct((N_DEV, FF_BLK, D_MODEL), BF16),
        scratch_shapes=[pltpu.VMEM(shape, F32)],
        compiler_params=_cp(("parallel", "arbitrary"), _vmem_limit(tiles, [(shape, F32)])),
    )(h, dgu)


GLA_TILE = 256
Q_OFF, K_OFF, V_OFF, R_OFF = 0, QK, 2 * QK, 2 * QK + GV


def _tri(lower):
    r = lax.broadcasted_iota(jnp.int32, (CHUNK, CHUNK), 0)
    c = lax.broadcasted_iota(jnp.int32, (CHUNK, CHUNK), 1)
    return jnp.where((r >= c) if lower else (c >= r), 1.0, 0.0).astype(BF16)


def _gla_fwd_call(pa, alow, wg, bgate, gnorm, *, name):
    S = pa.shape[0]
    Tg = min(GLA_TILE, S)
    cb = Tg // CHUNK

    def body(pa_ref, al_ref, wg_ref, bg_ref, gn_ref, y_ref, st_ref, state):
        @pl.when(pl.program_id(0) == 0)
        def _():
            state[...] = jnp.zeros_like(state)

        logit = _dot(al_ref[...], wg_ref[...], NN) + bg_ref[...]
        ls = _logsig(logit) * (1.0 / 16.0)
        tri = _tri(True)
        for h in range(HEADS):
            for c in range(cb):
                rows = pl.ds(c * CHUNK, CHUNK)
                cum = _exact_mask_dot(tri, ls[c * CHUNK:(c + 1) * CHUNK, h * DK:(h + 1) * DK])
                tot = cum[CHUNK - 1:CHUNK]
                kd = (pa_ref[rows, pl.ds(K_OFF + h * DK, DK)].astype(F32) * jnp.exp(tot - cum)).astype(BF16)
                new = state[h] * jnp.exp(tot) + _dot(pa_ref[rows, pl.ds(V_OFF + h * DV, DV)], kd, TN)
                state[h] = new
                st_ref[c, h] = new
                qs = (pa_ref[rows, pl.ds(Q_OFF + h * DK, DK)].astype(F32) * Q_SCALE).astype(BF16)
                o = _dot(qs, new.astype(BF16), NT)
                rs = lax.rsqrt(jnp.mean(o * o, axis=-1, keepdims=True) + EPS)
                rr = pa_ref[rows, pl.ds(R_OFF + h * DV, DV)].astype(F32)
                y_ref[rows, pl.ds(h * DV, DV)] = (o * rs * gn_ref[h] * (rr * _sig(rr))).astype(BF16)

    return pl.pallas_call(
        body, name=name, grid=(S // Tg,),
        in_specs=[
            pl.BlockSpec((Tg, 2 * QK + 2 * GV), lambda t: (t, 0)),
            pl.BlockSpec((Tg, LANES), lambda t: (t, 0)),
            pl.BlockSpec((LANES, QK), lambda t: (0, 0)),
            pl.BlockSpec((1, QK), lambda t: (0, 0)),
            pl.BlockSpec((HEADS, 1, DV), lambda t: (0, 0, 0)),
        ],
        out_specs=[
            pl.BlockSpec((Tg, GV), lambda t: (t, 0)),
            pl.BlockSpec((cb, HEADS, DV, DK), lambda t: (t, 0, 0, 0)),
        ],
        out_shape=[jax.ShapeDtypeStruct((S, GV), BF16),
                   jax.ShapeDtypeStruct((S // CHUNK, HEADS, DV, DK), F32)],
        scratch_shapes=[pltpu.VMEM((HEADS, DV, DK), F32)],
        compiler_params=_cp(("arbitrary",), VMEM_BIG),
    )(pa, alow, wg, bgate, gnorm)


def _gla_bwd_call(pa, alow, wg, bgate, gnorm, states, dy, *, name):
    S = pa.shape[0]
    Tg = min(GLA_TILE, S)
    cb = Tg // CHUNK
    nt = S // Tg

    def rev(t):
        return nt - 1 - t

    def body(pa_ref, al_ref, wg_ref, bg_ref, gn_ref, st_ref, prev_ref, dy_ref,
             dpa_ref, dl_ref, dgn_ref, dbg_ref, carry):
        t = pl.program_id(0)

        @pl.when(t == 0)
        def _():
            carry[...] = jnp.zeros_like(carry)
            dgn_ref[...] = jnp.zeros_like(dgn_ref)
            dbg_ref[...] = jnp.zeros_like(dbg_ref)

        logit = _dot(al_ref[...], wg_ref[...], NN) + bg_ref[...]
        ls = _logsig(logit) * (1.0 / 16.0)
        sneg = _sig(-logit)
        tri = _tri(True)
        upper = _tri(False)
        first_tile = rev(t) == 0
        for h in range(HEADS):
            gn = gn_ref[h]
            hk = pl.ds(h * DK, DK)
            dbg = jnp.zeros((1, DK), F32)
            dgn = jnp.zeros((1, DV), F32)
            for c in reversed(range(cb)):
                rows = pl.ds(c * CHUNK, CHUNK)
                cum = _exact_mask_dot(tri, ls[c * CHUNK:(c + 1) * CHUNK, h * DK:(h + 1) * DK])
                tot = cum[CHUNK - 1:CHUNK]
                w = jnp.exp(tot - cum)
                decay = jnp.exp(tot)
                kd = pa_ref[rows, pl.ds(K_OFF + h * DK, DK)].astype(F32) * w
                kd16 = kd.astype(BF16)
                st = st_ref[c, h]
                if c > 0:
                    st_prev = st_ref[c - 1, h]
                else:
                    st_prev = jnp.where(first_tile, 0.0, prev_ref[0, h])
                qs = (pa_ref[rows, pl.ds(Q_OFF + h * DK, DK)].astype(F32) * Q_SCALE).astype(BF16)
                st16 = st.astype(BF16)
                o = _dot(qs, st16, NT)
                rs = lax.rsqrt(jnp.mean(o * o, axis=-1, keepdims=True) + EPS)
                oh = o * rs
                rr = pa_ref[rows, pl.ds(R_OFF + h * DV, DV)].astype(F32)
                sr = _sig(rr)
                dyv = dy_ref[rows, pl.ds(h * DV, DV)].astype(F32)
                dpa_ref[rows, pl.ds(R_OFF + h * DV, DV)] = (
                    dyv * oh * gn * sr * (1.0 + rr * (1.0 - sr))).astype(BF16)
                don = dyv * (rr * sr)
                dgn = dgn + jnp.sum(don * oh, axis=0, keepdims=True)
                doh = don * gn
                do16 = (rs * (doh - oh * jnp.mean(doh * oh, axis=-1, keepdims=True))).astype(BF16)
                dpa_ref[rows, pl.ds(Q_OFF + h * DK, DK)] = (_dot(do16, st16, NN) * Q_SCALE).astype(BF16)
                gt = _dot(do16, qs, TN) + carry[h]
                gt16 = gt.astype(BF16)
                dkd = _dot(pa_ref[rows, pl.ds(V_OFF + h * DV, DV)], gt16, NN)
                dpa_ref[rows, pl.ds(V_OFF + h * DV, DV)] = _dot(kd16, gt16, NT).astype(BF16)
                ddecay = jnp.sum(gt * st_prev, axis=0, keepdims=True)
                dpa_ref[rows, pl.ds(K_OFF + h * DK, DK)] = (dkd * w).astype(BF16)
                e = dkd * kd
                dtot = jnp.sum(e, axis=0, keepdims=True) + ddecay * decay
                dls = dtot - _exact_mask_dot(upper, e)
                dlogit = dls * (1.0 / 16.0) * sneg[c * CHUNK:(c + 1) * CHUNK, h * DK:(h + 1) * DK]
                dl_ref[rows, hk] = dlogit.astype(BF16)
                dbg = dbg + jnp.sum(dlogit, axis=0, keepdims=True)
                carry[h] = gt * decay
            dgn_ref[h] += dgn
            dbg_ref[:, hk] += dbg

    return pl.pallas_call(
        body, name=name, grid=(nt,),
        in_specs=[
            pl.BlockSpec((Tg, 2 * QK + 2 * GV), lambda t: (rev(t), 0)),
            pl.BlockSpec((Tg, LANES), lambda t: (rev(t), 0)),
            pl.BlockSpec((LANES, QK), lambda t: (0, 0)),
            pl.BlockSpec((1, QK), lambda t: (0, 0)),
            pl.BlockSpec((HEADS, 1, DV), lambda t: (0, 0, 0)),
            pl.BlockSpec((cb, HEADS, DV, DK), lambda t: (rev(t), 0, 0, 0)),
            pl.BlockSpec((1, HEADS, DV, DK), lambda t: (jnp.maximum(rev(t) * cb - 1, 0), 0, 0, 0)),
            pl.BlockSpec((Tg, GV), lambda t: (rev(t), 0)),
        ],
        out_specs=[
            pl.BlockSpec((Tg, 2 * QK + 2 * GV), lambda t: (rev(t), 0)),
            pl.BlockSpec((Tg, QK), lambda t: (rev(t), 0)),
            pl.BlockSpec((HEADS, 1, DV), lambda t: (0, 0, 0)),
            pl.BlockSpec((1, QK), lambda t: (0, 0)),
        ],
        out_shape=[jax.ShapeDtypeStruct((S, 2 * QK + 2 * GV), BF16), jax.ShapeDtypeStruct((S, QK), BF16),
                   jax.ShapeDtypeStruct((HEADS, 1, DV), F32), jax.ShapeDtypeStruct((1, QK), F32)],
        scratch_shapes=[pltpu.VMEM((HEADS, DV, DK), F32)],
        compiler_params=_cp(("arbitrary",), VMEM_BIG),
    )(pa, alow, wg, bgate, gnorm, states, states, dy)


SGU_TILE = 256


def _sgu_mask():
    r = lax.broadcasted_iota(jnp.int32, (SBLOCK, SBLOCK), 0)
    c = lax.broadcasted_iota(jnp.int32, (SBLOCK, SBLOCK), 1)
    return (c < CHUNK) | (r >= CHUNK)


def _ln_stats(vf):
    mu = jnp.mean(vf, axis=-1, keepdims=True)
    xc = vf - mu
    rs = lax.rsqrt(jnp.mean(xc * xc, axis=-1, keepdims=True) + EPS)
    return rs, xc * rs


def _sgu_fwd_call(ps, ln_g, ln_b, w_sp, b_sp, *, name):
    S = ps.shape[0]
    Ts = min(SGU_TILE, S)

    def body(ps_ref, lg_ref, lb_ref, w_ref, b_ref, y_ref):
        mask = _sgu_mask()
        for g in range(GROUPS):
            wm = jnp.where(mask, w_ref[g], 0.0).astype(BF16)
            for p in range(Ts // SBLOCK):
                rows = pl.ds(p * SBLOCK, SBLOCK)
                u = _gelu(ps_ref[rows, pl.ds(g * DG, DG)].astype(F32))
                _, xh = _ln_stats(_gelu(ps_ref[rows, pl.ds(D_MODEL + g * DG, DG)].astype(F32)))
                vn = xh * lg_ref[g] + lb_ref[g]
                mixed = _dot(wm, vn.astype(BF16), NN) + b_ref[g]
                y_ref[rows, pl.ds(g * DG, DG)] = (u * mixed).astype(BF16)

    full3 = lambda a, b, c: pl.BlockSpec((a, b, c), lambda t: (0, 0, 0))
    return pl.pallas_call(
        body, name=name, grid=(S // Ts,),
        in_specs=[pl.BlockSpec((Ts, 2 * D_MODEL), lambda t: (t, 0)),
                  full3(GROUPS, 1, DG), full3(GROUPS, 1, DG), full3(GROUPS, SBLOCK, SBLOCK), full3(GROUPS, SBLOCK, 1)],
        out_specs=pl.BlockSpec((Ts, D_MODEL), lambda t: (t, 0)),
        out_shape=jax.ShapeDtypeStruct((S, D_MODEL), BF16),
        compiler_params=_cp(("parallel",)),
    )(ps, ln_g, ln_b, w_sp, b_sp)


def _sgu_bwd_call(ps, ln_g, ln_b, w_sp, b_sp, dy, *, name):
    S = ps.shape[0]
    Ts = min(SGU_TILE, S)

    def body(ps_ref, lg_ref, lb_ref, w_ref, b_ref, dy_ref, ds_ref, dlg_ref, dlb_ref, dw_ref, db_ref):
        @pl.when(pl.program_id(0) == 0)
        def _():
            dlg_ref[...] = jnp.zeros_like(dlg_ref)
            dlb_ref[...] = jnp.zeros_like(dlb_ref)
            dw_ref[...] = jnp.zeros_like(dw_ref)
            db_ref[...] = jnp.zeros_like(db_ref)

        mask = _sgu_mask()
        for g in range(GROUPS):
            wm = jnp.where(mask, w_ref[g], 0.0).astype(BF16)
            lg = lg_ref[g]
            for p in range(Ts // SBLOCK):
                rows = pl.ds(p * SBLOCK, SBLOCK)
                su = ps_ref[rows, pl.ds(g * DG, DG)].astype(F32)
                sv = ps_ref[rows, pl.ds(D_MODEL + g * DG, DG)].astype(F32)
                u = _gelu(su)
                rs, xh = _ln_stats(_gelu(sv))
                vn16 = (xh * lg + lb_ref[g]).astype(BF16)
                mixed = _dot(wm, vn16, NN) + b_ref[g]
                dyv = dy_ref[rows, pl.ds(g * DG, DG)].astype(F32)
                ds_ref[rows, pl.ds(g * DG, DG)] = (dyv * mixed * _gelu_grad(su)).astype(BF16)
                dmix = dyv * u
                dmix16 = dmix.astype(BF16)
                db_ref[g] += jnp.sum(dmix, axis=-1, keepdims=True)
                dw_ref[g] += jnp.where(mask, _dot(dmix16, vn16, NT), 0.0)
                dvn = _dot(wm, dmix16, TN)
                dlg_ref[g] += jnp.sum(dvn * xh, axis=0, keepdims=True)
                dlb_ref[g] += jnp.sum(dvn, axis=0, keepdims=True)
                dxh = dvn * lg
                dvf = rs * (dxh - jnp.mean(dxh, axis=-1, keepdims=True)
                            - xh * jnp.mean(dxh * xh, axis=-1, keepdims=True))
                ds_ref[rows, pl.ds(D_MODEL + g * DG, DG)] = (dvf * _gelu_grad(sv)).astype(BF16)

    full3 = lambda a, b, c: pl.BlockSpec((a, b, c), lambda t: (0, 0, 0))
    return pl.pallas_call(
        body, name=name, grid=(S // Ts,),
        in_specs=[pl.BlockSpec((Ts, 2 * D_MODEL), lambda t: (t, 0)),
                  full3(GROUPS, 1, DG), full3(GROUPS, 1, DG), full3(GROUPS, SBLOCK, SBLOCK), full3(GROUPS, SBLOCK, 1),
                  pl.BlockSpec((Ts, D_MODEL), lambda t: (t, 0))],
        out_specs=[pl.BlockSpec((Ts, 2 * D_MODEL), lambda t: (t, 0)),
                   full3(GROUPS, 1, DG), full3(GROUPS, 1, DG), full3(GROUPS, SBLOCK, SBLOCK), full3(GROUPS, SBLOCK, 1)],
        out_shape=[jax.ShapeDtypeStruct((S, 2 * D_MODEL), BF16),
                   jax.ShapeDtypeStruct((GROUPS, 1, DG), F32), jax.ShapeDtypeStruct((GROUPS, 1, DG), F32),
                   jax.ShapeDtypeStruct((GROUPS, SBLOCK, SBLOCK), F32),
                   jax.ShapeDtypeStruct((GROUPS, SBLOCK, 1), F32)],
        compiler_params=_cp(("arbitrary",)),
    )(ps, ln_g, ln_b, w_sp, b_sp, dy)


def _position():
    return lax.axis_index("x"), lax.axis_index("y"), lax.axis_index("c")


def _gather_copies(srcs, dsts, send_sems, recv_sems, local_sems):
    x, y, c = _position()
    me, sibling = (x, y, c), (x, y, 1 - c)
    chips = [(1 - x, y), (x, 1 - y), (1 - x, 1 - y)]
    n = len(srcs)

    def slab(a, block):
        px, py, pc = block
        return dsts[a].at[4 * px + 2 * py + pc]

    def copy(a, k, block, to, src=None):
        return pltpu.make_async_remote_copy(
            src_ref=slab(a, block) if src is None else src, dst_ref=slab(a, block),
            send_sem=send_sems.at[7 * a + k], recv_sem=recv_sems.at[7 * a + k], device_id=to, device_id_type=MESH)

    mine = [pltpu.make_async_copy(srcs[a], slab(a, me), local_sems.at[a]) for a in range(n)]
    for cp in mine:
        cp.start()
    first = []
    for a in range(n):
        first.append(copy(a, 0, me, sibling, src=srcs[a]))
        first += [copy(a, 1 + j, me, (*chip, c), src=srcs[a]) for j, chip in enumerate(chips)]
    for cp in first:
        cp.start()
    passed = []
    for j, chip in enumerate(chips):
        for a in range(n):
            copy(a, 1 + j, (*chip, c), me).wait_recv()
            cp = copy(a, 4 + j, (*chip, c), sibling)
            cp.start()
            passed.append(cp)
    for a in range(n):
        copy(a, 0, sibling, me).wait_recv()
        for j, chip in enumerate(chips):
            copy(a, 4 + j, (*chip, 1 - c), me).wait_recv()
    for cp in first + passed:
        cp.wait_send()
    for cp in mine:
        cp.wait()


def _all_gather_hbm(shards, *, name):
    n = len(shards)

    def body(*refs):
        srcs, dsts = refs[:n], refs[n:2 * n]
        send_sems, recv_sems, local_sems = refs[2 * n:]
        _gather_copies(srcs, dsts, send_sems, recv_sems, local_sems)

    return pl.pallas_call(
        body, name=name,
        in_specs=[ANY] * n, out_specs=[ANY] * n,
        out_shape=[jax.ShapeDtypeStruct((N_DEV, *s.shape), s.dtype) for s in shards],
        scratch_shapes=_comm_sems(n),
    )(*shards)


def _all_reduce_small(part, *, name):
    R, W = part.shape

    def body(x_ref, out_ref, gathered, send_sems, recv_sems, local_sems):
        _gather_copies([x_ref], [gathered], send_sems, recv_sems, local_sems)
        acc = gathered[0]
        for d in range(1, N_DEV):
            acc = acc + gathered[d]
        out_ref[...] = acc

    return pl.pallas_call(
        body, name=name,
        in_specs=[VMEM_SPEC], out_specs=VMEM_SPEC,
        out_shape=jax.ShapeDtypeStruct((R, W), F32),
        scratch_shapes=[pltpu.VMEM((N_DEV, R, W), F32),
                        pltpu.SemaphoreType.DMA((7,)), pltpu.SemaphoreType.DMA((7,)), pltpu.SemaphoreType.DMA((1,))],
    )(part)


FLIPS = [(fx, fy, fc) for fx in (0, 1) for fy in (0, 1) for fc in (0, 1)][1:]


def _scatter_copies(srcs, dsts, send_sems, recv_sems, local_sems):
    n = len(srcs)
    x, y, c = _position()
    me = 4 * x + 2 * y + c
    mine = [pltpu.make_async_copy(srcs[a].at[me], dsts[a].at[me], local_sems.at[a]) for a in range(n)]
    for cp in mine:
        cp.start()
    copies = []
    for k, (fx, fy, fc) in enumerate(FLIPS):
        tx = 1 - x if fx else x
        ty = 1 - y if fy else y
        tc = 1 - c if fc else c
        peer = 4 * tx + 2 * ty + tc
        for a in range(n):
            cp = pltpu.make_async_remote_copy(
                src_ref=srcs[a].at[peer], dst_ref=dsts[a].at[me],
                send_sem=send_sems.at[7 * a + k], recv_sem=recv_sems.at[7 * a + k],
                device_id=(tx, ty, tc), device_id_type=MESH)
            cp.start()
            copies.append(cp)
    for cp in copies:
        cp.wait()
    for cp in mine:
        cp.wait()


def _comm_sems(n):
    return [pltpu.SemaphoreType.DMA((7 * n,)), pltpu.SemaphoreType.DMA((7 * n,)), pltpu.SemaphoreType.DMA((n,))]


def _scatter_blocks(parts, *, name):
    n = len(parts)

    def body(*refs):
        _scatter_copies(refs[:n], refs[n:2 * n], *refs[2 * n:])

    return pl.pallas_call(
        body, name=name,
        in_specs=[ANY] * n, out_specs=[ANY] * n,
        out_shape=[jax.ShapeDtypeStruct(p.shape, p.dtype) for p in parts],
        scratch_shapes=_comm_sems(n),
    )(*parts)


def _handshake(peers):
    barrier = pltpu.get_barrier_semaphore()
    for peer in peers:
        pl.semaphore_signal(barrier, inc=1, device_id=peer, device_id_type=MESH)
    pl.semaphore_wait(barrier, len(peers))


def _sequencer_call(arrays, out_types, copies_fn, peers_fn, *, name, collective_id):
    n = len(arrays)
    srcs = [jax.new_ref(a, memory_space=pltpu.MemorySpace.HBM) for a in arrays]
    dsts = [jax.empty_ref(t, memory_space=pltpu.MemorySpace.HBM) for t in out_types]

    @pl.kernel(mesh=plsc.ScalarSubcoreMesh(axis_name="sequencer", num_cores=1), name=name,
               scratch_types=_comm_sems(n), compiler_params=pltpu.CompilerParams(collective_id=collective_id))
    def launch(send_sems, recv_sems, local_sems):
        _handshake(peers_fn())
        copies_fn(srcs, dsts, send_sems, recv_sems, local_sems)

    launch()
    return [d[...] for d in dsts]


def _all_other_devices():
    x, y, c = _position()
    return [(1 - x if fx else x, 1 - y if fy else y, 1 - c if fc else c) for fx, fy, fc in FLIPS]


def _gather_peers():
    x, y, c = _position()
    return [(x, y, 1 - c), (1 - x, y, c), (x, 1 - y, c), (1 - x, 1 - y, c)]


def _scatter_blocks_async(parts, *, name, collective_id):
    return _sequencer_call(parts, [jax.ShapeDtypeStruct(p.shape, p.dtype) for p in parts],
                           _scatter_copies, _all_other_devices, name=name, collective_id=collective_id)


def _all_gather_async(shards, *, name, collective_id):
    return _sequencer_call(shards, [jax.ShapeDtypeStruct((N_DEV, *s.shape), s.dtype) for s in shards],
                           _gather_copies, _gather_peers, name=name, collective_id=collective_id)


def _adamw_math(w, g, m, v):
    m = ADAM_B1 * m + (1.0 - ADAM_B1) * g
    v = ADAM_B2 * v + (1.0 - ADAM_B2) * (g * g)
    m_hat = m / (1.0 - ADAM_B1 ** ADAM_STEP)
    v_hat = v / (1.0 - ADAM_B2 ** ADAM_STEP)
    delta = -ADAM_LR * (m_hat / (jnp.sqrt(v_hat) + ADAM_EPS) + ADAM_WD * w)
    return delta, m, v


def _adamw_reduce_call(recv, w, m, v, *, name, T=128):
    R, W = w.shape
    if R % T == 0:
        tr, tw = T, W
    elif (R // 2) % 16 == 0:
        tr, tw = R // 2, W
    else:
        tr, tw = R, 2 * LANES

    def body(p_ref, w_ref, m_ref, v_ref, g_out, d_out, m_out, v_out):
        g = p_ref[0].astype(F32)
        for d in range(1, N_DEV):
            g = g + p_ref[d].astype(F32)
        g_out[...] = g
        d_out[...], m_out[...], v_out[...] = _adamw_math(w_ref[...], g, m_ref[...], v_ref[...])

    row = pl.BlockSpec((tr, tw), lambda i, j: (i, j))
    out = jax.ShapeDtypeStruct((R, W), F32)
    return pl.pallas_call(
        body, name=name, grid=(R // tr, W // tw),
        in_specs=[pl.BlockSpec((N_DEV, tr, tw), lambda i, j: (0, i, j)), row, row, row],
        out_specs=[row] * 4, out_shape=[out] * 4,
        compiler_params=_cp(("parallel", "parallel"), VMEM_BIG),
    )(recv, w, m, v)


def _adamw_small_call(w, g, m, v, *, name):
    def body(w_ref, g_ref, m_ref, v_ref, d_out, m_out, v_out):
        d_out[...], m_out[...], v_out[...] = _adamw_math(w_ref[...], g_ref[...], m_ref[...], v_ref[...])

    out = jax.ShapeDtypeStruct(w.shape, F32)
    return pl.pallas_call(body, name=name, in_specs=[VMEM_SPEC] * 4, out_specs=[VMEM_SPEC] * 3,
                          out_shape=[out] * 3)(w, g, m, v)


def _tile_rows(n_elems):
    return -(-n_elems // (8 * LANES)) * 8


def _pack_rows(parts, rows):
    pieces = []
    for p in parts:
        q = p.reshape(-1, LANES)
        pieces.append(jnp.pad(q, ((0, _tile_rows(p.size) - q.shape[0]), (0, 0))))
    buf = jnp.concatenate(pieces, axis=0)
    return jnp.pad(buf, ((0, rows - buf.shape[0]), (0, 0)))


def _unpack_rows(buf, shapes):
    out, off = [], 0
    for shp in shapes:
        n = 1
        for s in shp:
            n *= s
        out.append(buf[off:off + n // LANES].reshape(shp))
        off += _tile_rows(n)
    return out


def _join_blocks_call(blocks, *, name, tw=2 * LANES):
    nb, r, w = blocks.shape

    def body(b_ref, o_ref):
        for d in range(nb):
            o_ref[d * r:(d + 1) * r, :] = b_ref[d]

    return pl.pallas_call(
        body, name=name, grid=(w // tw,),
        in_specs=[pl.BlockSpec((nb, r, tw), lambda j: (0, 0, j))],
        out_specs=pl.BlockSpec((nb * r, tw), lambda j: (0, j)),
        out_shape=jax.ShapeDtypeStruct((nb * r, w), blocks.dtype),
        compiler_params=_cp(("parallel",)),
    )(blocks)


def _split_blocks_call(parts, *, name, tw=2 * LANES):
    w = parts[0].shape[1]
    starts = [0]
    for q in parts:
        starts.append(starts[-1] + q.shape[0])
    assert starts[-1] == N_DEV * IN_BLK

    def body(*refs):
        o_ref = refs[-1]
        for d in range(N_DEV):
            lo, hi = IN_BLK * d, IN_BLK * (d + 1)
            for ref, c0, c1 in zip(refs[:-1], starts[:-1], starts[1:]):
                s0, e0 = max(lo, c0), min(hi, c1)
                if s0 < e0:
                    o_ref[d, s0 - lo:e0 - lo, :] = ref[s0 - c0:e0 - c0, :]

    return pl.pallas_call(
        body, name=name, grid=(w // tw,),
        in_specs=[pl.BlockSpec((q.shape[0], tw), lambda j: (0, j)) for q in parts],
        out_specs=pl.BlockSpec((N_DEV, IN_BLK, tw), lambda j: (0, 0, j)),
        out_shape=jax.ShapeDtypeStruct((N_DEV, IN_BLK, w), parts[0].dtype),
        compiler_params=_cp(("parallel",)),
    )(*parts)


def _local_step(x, target, W, scatter, finish):
    g1, g2, g3, g4 = [W[n].reshape(1, D_MODEL) for n in ("norm_pre_mix", "norm_post_mix", "norm_pre_ffn", "norm_post_ffn")]
    w_a, w_low, w_s, w_g = W["w_in_a"], W["w_in_low"], W["w_in_s"], W["w_in_g"]
    wfi, wfo = W["w_ffn_in"], W["w_ffn_out"].reshape(4, FF_BLK, D_MODEL)
    wg = jnp.pad(W["w_gate_up"], ((0, LANES - RANK), (0, 0))).astype(BF16)
    bgate = W["b_gate"].reshape(1, QK)
    gnorm = W["gla_norm"].reshape(HEADS, 1, DV)
    ln_g = W["sgu_ln_g"].reshape(GROUPS, 1, DG)
    ln_b = W["sgu_ln_b"].reshape(GROUPS, 1, DG)
    w_sp = W["w_spatial"]
    b_sp = W["b_spatial"].reshape(GROUPS, SBLOCK, 1)

    a, pa, alow, ps, pg = _in_proj_call(x, g1, w_a, w_low, w_s, w_g, name="in_proj")
    y_gla, states = _gla_fwd_call(pa, alow, wg, bgate, gnorm, name="gla_fwd")
    y_sgu = _sgu_fwd_call(ps, ln_g, ln_b, w_sp, b_sp, name="sgu_fwd")
    t1, t2, merged, mix, x1, h = _mixer_tail_call(y_gla, y_sgu, pg, x, W["w_branch_gla"], W["w_branch_sgu"],
                                                  W["w_out"], g2, g3, name="mixer_tail")
    gu, z = _ffn_in_call(h, wfi, name="ffn_in")
    loss, dx2, dy, dg4 = _ffn_out_loss_call(z, wfo, x1, target, g4, name="ffn_out_loss")

    grads = {"norm_post_ffn": dg4}
    done = {}
    dgu = _ffn_out_bwd_call(dy, wfo, gu, name="ffn_out_bwd")
    dw_ffn_out = _weight_grads_call([z, dy], [(0, 1)], name="d_ffn_out_w")[0].reshape(N_DEV, D_FF // N_DEV, D_MODEL)
    dw_ffn_in = _ffn_in_grad_call(h, dgu, name="d_ffn_in_w")
    dgu, dw_ffn_out, dw_ffn_in = lax.optimization_barrier((dgu, dw_ffn_out, dw_ffn_in))
    ffn_received = scatter(("w_ffn_out", "w_ffn_in"), [dw_ffn_out, dw_ffn_in])
    dx1, dmix, grads["norm_pre_ffn"], grads["norm_post_mix"] = _ffn_in_bwd_call(dgu, wfi, dx2, x1, mix, g3, g2, name="ffn_in_bwd")
    dt1, dt2, dpg, dy_gla, dy_sgu = _mixer_bwd_call(dmix, t1, t2, pg, W["w_out"], W["w_branch_gla"], W["w_branch_sgu"],
                                                    name="mixer_bwd")
    rows = D_MODEL // N_DEV
    mixer_grads = _weight_grads_call([merged, dmix, y_gla, dt1, y_sgu, dt2], [(0, 1), (2, 3), (4, 5)], name="d_mixer_w")
    dy_gla, dy_sgu, mixer_grads = lax.optimization_barrier((dy_gla, dy_sgu, mixer_grads))
    mixer_received = scatter(("w_out", "w_branch_gla", "w_branch_sgu"),
                             [g.reshape(N_DEV, rows, D_MODEL) for g in mixer_grads])
    dpa, dlogit, dgn, dbg = _gla_bwd_call(pa, alow, wg, bgate, gnorm, states, dy_gla, name="gla_bwd")
    ffn_done, dpa, dlogit = lax.optimization_barrier((finish(ffn_received), dpa, dlogit))
    done.update(ffn_done)
    dps, dlg, dlb, dwsp, dbsp = _sgu_bwd_call(ps, ln_g, ln_b, w_sp, b_sp, dy_sgu, name="sgu_bwd")
    mixer_done, dps = lax.optimization_barrier((finish(mixer_received), dps))
    done.update(mixer_done)
    dlow = _mm(dlogit, wg, "nt", BF16, name="d_gate_up_x")
    dw_a, dw_low = _weight_grads_call([a, dpa, dlow], [(1, 0), (2, 0)], name="d_in_w_qkvr")
    dw_s, dw_g = _weight_grads_call([a, dps, dpg], [(1, 0), (2, 0)], name="d_in_w_sgu_gates")
    dw_in = _split_blocks_call([dw_a, dw_low[:RANK], dw_s, dw_g], name="split_w_in_grad")
    in_received = scatter(("w_in",), [dw_in])
    dwg = _mm(alow, dlogit, "tn", F32, name="d_gate_up_w")
    grad_x, grads["norm_pre_mix"] = _in_proj_bwd_call(dpa, dlow, dps, dpg, w_a, w_low, w_s, w_g, x, dx1, g1,
                                                      name="in_proj_bwd")

    grads["w_gate_up"] = dwg[:RANK]
    grads["b_gate"] = dbg
    grads["gla_norm"] = dgn
    grads["sgu_ln_g"] = dlg
    grads["sgu_ln_b"] = dlb
    grads["w_spatial"] = dwsp
    grads["b_spatial"] = dbsp
    done.update(finish(in_received))
    return loss, grad_x, grads, done


WEIGHTS = ("norm_pre_mix", "w_in", "w_gate_up", "b_gate", "gla_norm", "sgu_ln_g", "sgu_ln_b", "w_spatial",
           "b_spatial", "w_branch_gla", "w_branch_sgu", "w_out", "norm_post_mix", "norm_pre_ffn", "w_ffn_in",
           "w_ffn_out", "norm_post_ffn")
BIG = ("w_in", "w_branch_gla", "w_branch_sgu", "w_out", "w_ffn_in", "w_ffn_out")
MIXER = ("w_branch_gla", "w_branch_sgu", "w_out")
FFN = ("w_ffn_in", "w_ffn_out")
COLUMN_SHARDED = ("w_in", "w_ffn_in")
SMALL = tuple(n for n in WEIGHTS if n not in BIG)
SMALL_SHARDED = ("w_gate_up", "gla_norm", "sgu_ln_g", "sgu_ln_b")
SMALL_FULL = {"norm_pre_mix": (1024,), "w_gate_up": (16, 512), "b_gate": (512,), "gla_norm": (4, 256),
              "sgu_ln_g": (4, 256), "sgu_ln_b": (4, 256), "w_spatial": (4, 128, 128), "b_spatial": (4, 128),
              "norm_post_mix": (1024,), "norm_pre_ffn": (1024,), "norm_post_ffn": (1024,)}
SMALL_GRAD_ROWS = 648
SMALL_STATE_ROWS = 600
SMALL_GATHER_ROWS = 32


def kernel(x, norm_pre_mix, w_in, w_gate_up, b_gate, gla_norm, sgu_ln_g, sgu_ln_b, w_spatial, b_spatial, w_branch_gla, w_branch_sgu, w_out, norm_post_mix, norm_pre_ffn, w_ffn_in, w_ffn_out, norm_post_ffn, loss_target, m_norm_pre_mix, m_w_in, m_w_gate_up, m_b_gate, m_gla_norm, m_sgu_ln_g, m_sgu_ln_b, m_w_spatial, m_b_spatial, m_w_branch_gla, m_w_branch_sgu, m_w_out, m_norm_post_mix, m_norm_pre_ffn, m_w_ffn_in, m_w_ffn_out, m_norm_post_ffn, v_norm_pre_mix, v_w_in, v_w_gate_up, v_b_gate, v_gla_norm, v_sgu_ln_g, v_sgu_ln_b, v_w_spatial, v_b_spatial, v_w_branch_gla, v_w_branch_sgu, v_w_out, v_norm_post_mix, v_norm_pre_ffn, v_w_ffn_in, v_w_ffn_out, v_norm_post_ffn):
    given = dict(locals())
    def local(a, n):
        return a[0].T if n in COLUMN_SHARDED else a[0]

    w = {n: local(given[n], n) for n in WEIGHTS}
    m = {n: local(given["m_" + n], n) for n in WEIGHTS}
    v = {n: local(given["v_" + n], n) for n in WEIGHTS}
    xs, target = x[0], loss_target[0]
    me = 4 * lax.axis_index("x") + 2 * lax.axis_index("y") + lax.axis_index("c")

    small_shard = _pack_rows([w[n] for n in SMALL_SHARDED], SMALL_GATHER_ROWS)
    first = _all_gather_hbm([w["w_in"].astype(BF16), small_shard], name="gather_w_in")
    rest, first = lax.optimization_barrier(([w[n].astype(BF16) for n in MIXER + FFN], first))
    rest_blocks = _all_gather_async(rest, name="gather_rest", collective_id=1)
    W = {n: w[n] for n in SMALL if n not in SMALL_SHARDED}
    blocks = {"w_in": first[0], **dict(zip(MIXER + FFN, rest_blocks))}
    for n in ("w_branch_gla", "w_branch_sgu", "w_out", "w_ffn_out"):
        W[n] = blocks[n].reshape(-1, D_MODEL)
    W["w_ffn_in"] = blocks["w_ffn_in"]
    w_in_t = _join_blocks_call(blocks["w_in"], name="join_w_in")
    W["w_in_a"] = w_in_t[A_COLS[0]:A_COLS[1]]
    W["w_in_low"] = jnp.pad(w_in_t[LOW_COLS[0]:LOW_COLS[1]], ((0, LANES - RANK), (0, 0)))
    W["w_in_s"] = w_in_t[S_COLS[0]:S_COLS[1]]
    W["w_in_g"] = w_in_t[G_COLS[0]:G_COLS[1]]
    small_blocks = first[1]
    off = 0
    for n in SMALL_SHARDED:
        r, c = w[n].shape
        blk = small_blocks[:, off:off + r * c // LANES].reshape(N_DEV, r, c)
        W[n] = blk.transpose(1, 0, 2).reshape(r, N_DEV * c)
        off += _tile_rows(r * c)

    scatter_ids = iter((3, 4, 5))

    def scatter(names, parts):
        got = _scatter_blocks_async(parts, name="scatter_" + "_".join(names), collective_id=next(scatter_ids))
        return dict(zip(names, got))

    def finish(received):
        return {n: _adamw_reduce_call(r, w[n], m[n], v[n], name="adamw_" + n) for n, r in received.items()}

    loss_part, grad_x, grads, big_done = _local_step(xs, target, W, scatter, finish)

    small_part = jnp.concatenate([_pack_rows([grads[n] for n in SMALL], SMALL_GRAD_ROWS),
                                  jnp.broadcast_to(loss_part, (8, LANES))], axis=0)
    small_sum = _all_reduce_small(small_part, name="reduce_small")
    loss = small_sum[SMALL_GRAD_ROWS, 0]
    g_small = dict(zip(SMALL, _unpack_rows(small_sum, [SMALL_FULL[n] for n in SMALL])))
    for n in SMALL_SHARDED:
        c = w[n].shape[1]
        g_small[n] = lax.dynamic_slice_in_dim(g_small[n], me * c, c, axis=1)

    g_big, d_big, m_big, v_big = [{n: big_done[n][i] for n in BIG} for i in range(4)]
    small_shapes = [w[n].shape for n in SMALL]
    small_state = [_pack_rows([s[n] for n in SMALL], SMALL_STATE_ROWS) for s in (w, g_small, m, v)]
    small_out = _adamw_small_call(*small_state, name="adamw_small")
    d_small, m_small, v_small = [dict(zip(SMALL, _unpack_rows(o, small_shapes))) for o in small_out]

    def pick(big, sml):
        return [(big[n].T if n in COLUMN_SHARDED else big[n] if n in BIG else sml[n])[None] for n in WEIGHTS]

    return (loss, grad_x[None], *pick(g_big, g_small), *pick(d_big, d_small), *pick(m_big, m_small),
            *pick(v_big, v_small))
```

```python
import jax
import jax.numpy as jnp
from jax import lax
from jax.experimental import pallas as pl
from jax.experimental.pallas import tpu as pltpu
from jax.experimental.pallas import tpu_sc as plsc

F32 = jnp.float32
BF16 = jnp.bfloat16

D_MODEL = 1024
N_DEV = 8
CHUNK = 64
HEADS = 4
DK = 128
DV = 256
QK = HEADS * DK
GV = HEADS * DV
RANK = 16
GROUPS = 4
SBLOCK = 128
DG = 256
D_FF = 2816
FF_BLK = 704
EPS = 1e-6
Q_SCALE = DK ** -0.5
LANES = 128
VMEM_BIG = 48 * 1024 * 1024

D_IN = 7184
IN_BLK = 898
A_COLS = (0, 3072)
LOW_COLS = (3072, 3088)
S_COLS = (3088, 5136)
G_COLS = (5136, 7184)

ADAM_LR = 0.001
ADAM_B1 = 0.9
ADAM_B2 = 0.999
ADAM_EPS = 1e-08
ADAM_WD = 0.01
ADAM_STEP = 10

MESH = pl.DeviceIdType.MESH
ANY = pl.BlockSpec(memory_space=pl.ANY)
VMEM_SPEC = pl.BlockSpec(memory_space=pltpu.VMEM)


def _cp(sem=None, vmem=None):
    return pltpu.CompilerParams(dimension_semantics=sem, vmem_limit_bytes=vmem)


def _sig(x):
    return 1.0 / (1.0 + jnp.exp(-x))


def _gelu(x):
    c = 0.7978845608028654
    t = jnp.tanh(c * (x + 0.044715 * x * x * x))
    return 0.5 * x * (1.0 + t)


def _gelu_grad(x):
    c = 0.7978845608028654
    x2 = x * x
    t = jnp.tanh(c * (x + 0.044715 * x * x2))
    return 0.5 * (1.0 + t) + 0.5 * x * (1.0 - t * t) * c * (1.0 + 3.0 * 0.044715 * x2)


def _logsig(x):
    return jnp.minimum(x, 0.0) - jnp.log1p(jnp.exp(-jnp.abs(x)))


def _dot(a, b, dims):
    return lax.dot_general(a, b, (dims, ((), ())), preferred_element_type=F32)


NN = ((1,), (0,))
NT = ((1,), (1,))
TN = ((0,), (0,))


def _exact_mask_dot(mask_bf16, x):
    hi = x.astype(BF16)
    r1 = x - hi.astype(F32)
    mid = r1.astype(BF16)
    lo = (r1 - mid.astype(F32)).astype(BF16)
    return _dot(mask_bf16, hi, NN) + _dot(mask_bf16, mid, NN) + _dot(mask_bf16, lo, NN)


def _pick_tile(dim, target):
    if dim <= target:
        return dim
    best = None
    for t in range(LANES, int(1.4 * target) + 1, LANES):
        if dim % t == 0:
            best = t
    assert best is not None, (dim, target)
    return best


def _mm_call(a, b, *, name, grid, a_spec, b_spec, o_spec, out_shape, dims, acc_shape):
    nk = grid[2]

    def body(a_ref, b_ref, o_ref, *acc):
        part = _dot(a_ref[...], b_ref[...], dims)
        if nk == 1:
            o_ref[...] = part.astype(o_ref.dtype)
        else:
            acc_ref = acc[0]
            k = pl.program_id(2)

            @pl.when(k == 0)
            def _():
                acc_ref[...] = part

            @pl.when(k > 0)
            def _():
                acc_ref[...] += part

            @pl.when(k == nk - 1)
            def _():
                o_ref[...] = acc_ref[...].astype(o_ref.dtype)

    return pl.pallas_call(
        body, name=name, grid=grid, in_specs=[a_spec, b_spec], out_specs=o_spec, out_shape=out_shape,
        scratch_shapes=[] if nk == 1 else [pltpu.VMEM(acc_shape, F32)],
        compiler_params=_cp(("parallel", "parallel", "arbitrary"), VMEM_BIG),
    )(a, b)


def _mm(a, b, mode, out_dtype, *, name, tm=512, tn=1024, tk=1024):
    if mode == "nn":
        (M, K), (_, N) = a.shape, b.shape
    elif mode == "nt":
        (M, K), (N, _) = a.shape, b.shape
    else:
        (K, M), (_, N) = a.shape, b.shape
    tm, tn, tk = _pick_tile(M, tm), _pick_tile(N, tn), _pick_tile(K, tk)
    if mode == "nn":
        a_spec = pl.BlockSpec((tm, tk), lambda i, j, k: (i, k))
        b_spec = pl.BlockSpec((tk, tn), lambda i, j, k: (k, j))
        dims = NN
    elif mode == "nt":
        a_spec = pl.BlockSpec((tm, tk), lambda i, j, k: (i, k))
        b_spec = pl.BlockSpec((tn, tk), lambda i, j, k: (j, k))
        dims = NT
    else:
        a_spec = pl.BlockSpec((tk, tm), lambda i, j, k: (k, i))
        b_spec = pl.BlockSpec((tk, tn), lambda i, j, k: (k, j))
        dims = TN
    return _mm_call(a, b, name=name, grid=(M // tm, N // tn, K // tk), a_spec=a_spec, b_spec=b_spec,
                    o_spec=pl.BlockSpec((tm, tn), lambda i, j, k: (i, j)),
                    out_shape=jax.ShapeDtypeStruct((M, N), out_dtype), dims=dims, acc_shape=(tm, tn))


ROW_TILE = 512


def _rt(T, W, c=0):
    return pl.BlockSpec((T, W), lambda i: (i, c))


def _rt3(nb, T, W):
    return pl.BlockSpec((nb, T, W), lambda i: (0, i, 0))


def _res(shape):
    nd = len(shape)
    return pl.BlockSpec(tuple(shape), lambda i: (0,) * nd, pipeline_mode=pl.Buffered(1))


def _acc(shape):
    nd = len(shape)
    return pl.BlockSpec(tuple(shape), lambda i: (0,) * nd, pipeline_mode=pl.Buffered(1))


def _nbytes(shape, dtype):
    n = jnp.dtype(dtype).itemsize
    for s in shape:
        n *= s
    return n


def _vmem_limit(tiles, resident, temps=16 * 1024 * 1024):
    need = 2 * sum(_nbytes(s, d) for s, d in tiles) + sum(_nbytes(s, d) for s, d in resident) + temps
    return min(need, 60 * 1024 * 1024)


def _tok_call(body, *, name, S, T, ins, outs, semantics="parallel"):
    tiles = [(spec.block_shape, a.dtype) for a, spec, kind in ins + outs if kind == "tile"]
    resident = [(a.shape, a.dtype) for a, spec, kind in ins + outs if kind == "res"]
    return pl.pallas_call(
        body, name=name, grid=(S // T,),
        in_specs=[spec for _, spec, _ in ins], out_specs=[spec for _, spec, _ in outs],
        out_shape=[jax.ShapeDtypeStruct(a.shape, a.dtype) for a, _, _ in outs],
        compiler_params=_cp((semantics,), _vmem_limit(tiles, resident)),
    )(*[a for a, _, _ in ins])


def _tile(a, spec):
    return (a, spec, "tile")


def _whole(a):
    return (a, _res(a.shape), "res")


def _out_tile(shape, dtype, spec):
    return (jax.ShapeDtypeStruct(shape, dtype), spec, "tile")


def _out_acc(shape, dtype=F32):
    return (jax.ShapeDtypeStruct(shape, dtype), _acc(shape), "res")


def _rms_stats(x):
    r = lax.rsqrt(jnp.mean(x * x, axis=-1, keepdims=True) + EPS)
    return r, x * r


def _rms_bwd(xh, r, g, dy):
    dxh = dy * g
    dx = r * (dxh - xh * jnp.mean(dxh * xh, axis=-1, keepdims=True))
    dg = jnp.sum(dy * xh, axis=0, keepdims=True)
    return dx, dg


def _accum(ref, val):
    @pl.when(pl.program_id(0) == 0)
    def _():
        ref[...] = val

    @pl.when(pl.program_id(0) > 0)
    def _():
        ref[...] += val


def _dot_cols(a, w_ref, o_ref, dims, chunk=1024):
    n = o_ref.shape[1]
    for n0 in range(0, n, chunk):
        n1 = min(n, n0 + chunk)
        w = w_ref[:, n0:n1] if dims == NN else w_ref[n0:n1, :]
        o_ref[:, n0:n1] = _dot(a, w, dims).astype(o_ref.dtype)


def _in_proj_call(x, g1, w_a, w_low, w_s, w_g, *, name, T=ROW_TILE):
    S = x.shape[0]

    def body(x_ref, g_ref, wa_ref, wl_ref, ws_ref, wg_ref, a_ref, pa_ref, al_ref, ps_ref, pg_ref):
        _, xh = _rms_stats(x_ref[...])
        a = (xh * g_ref[...]).astype(BF16)
        a_ref[...] = a
        _dot_cols(a, wa_ref, pa_ref, NT)
        _dot_cols(a, wl_ref, al_ref, NT)
        _dot_cols(a, ws_ref, ps_ref, NT)
        _dot_cols(a, wg_ref, pg_ref, NT)

    widths = (D_MODEL, w_a.shape[0], w_low.shape[0], w_s.shape[0], w_g.shape[0])
    return _tok_call(
        body, name=name, S=S, T=T,
        ins=[_tile(x, _rt(T, D_MODEL)), _whole(g1), _whole(w_a), _whole(w_low), _whole(w_s), _whole(w_g)],
        outs=[_out_tile((S, w), BF16, _rt(T, w)) for w in widths])


def _mixer_tail_call(y_gla, y_sgu, pg, x, w_bg, w_bs, w_out, g2, g3, *, name, T=ROW_TILE):
    S = x.shape[0]

    def body(yg_ref, ys_ref, pg_ref, x_ref, wbg_ref, wbs_ref, wo_ref, g2_ref, g3_ref,
             t1_ref, t2_ref, mg_ref, mix_ref, x1_ref, h_ref):
        t1 = _dot(yg_ref[...], wbg_ref[...], NN)
        t2 = _dot(ys_ref[...], wbs_ref[...], NN)
        t1_ref[...] = t1.astype(BF16)
        t2_ref[...] = t2.astype(BF16)
        sg = _sig(pg_ref[:, :D_MODEL].astype(F32))
        ss = _sig(pg_ref[:, D_MODEL:].astype(F32))
        merged = (sg * t1 + ss * t2).astype(BF16)
        mg_ref[...] = merged
        mix = _dot(merged, wo_ref[...], NN)
        mix_ref[...] = mix
        _, mh = _rms_stats(mix)
        x1 = x_ref[...] + mh * g2_ref[...]
        x1_ref[...] = x1
        _, xh = _rms_stats(x1)
        h_ref[...] = (xh * g3_ref[...]).astype(BF16)

    row = _rt(T, D_MODEL)
    b16 = lambda: _out_tile((S, D_MODEL), BF16, row)
    f32 = lambda: _out_tile((S, D_MODEL), F32, row)
    return _tok_call(
        body, name=name, S=S, T=T,
        ins=[_tile(y_gla, row), _tile(y_sgu, row), _tile(pg, _rt(T, 2 * D_MODEL)), _tile(x, row),
             _whole(w_bg), _whole(w_bs), _whole(w_out), _whole(g2), _whole(g3)],
        outs=[b16(), b16(), b16(), f32(), f32(), b16()])


def _ffn_in_call(h, wfi, *, name, T=ROW_TILE):
    S = h.shape[0]

    def body(h_ref, w_ref, gu_ref, z_ref):
        hv = h_ref[...]
        for d in range(4):
            gate = _dot(hv, w_ref[d], NT)
            up = _dot(hv, w_ref[d + 4], NT)
            gu_ref[d] = gate.astype(BF16)
            gu_ref[d + 4] = up.astype(BF16)
            z_ref[d] = (gate * _sig(gate) * up).astype(BF16)

    return _tok_call(
        body, name=name, S=S, T=T,
        ins=[_tile(h, _rt(T, D_MODEL)), _whole(wfi)],
        outs=[_out_tile((N_DEV, S, FF_BLK), BF16, _rt3(N_DEV, T, FF_BLK)),
              _out_tile((4, S, FF_BLK), BF16, _rt3(4, T, FF_BLK))])


def _ffn_out_loss_call(z, wfo, x1, target, g4, *, name, T=ROW_TILE):
    S = x1.shape[0]

    def body(z_ref, w_ref, x1_ref, t_ref, g4_ref, loss_ref, dx2_ref, dy_ref, dg4_ref):
        y = _dot(z_ref[0], w_ref[0], NN)
        for d in range(1, 4):
            y = y + _dot(z_ref[d], w_ref[d], NN)
        r, yh = _rms_stats(y)
        diff = x1_ref[...] + yh * g4_ref[...] - t_ref[...]
        part = 0.5 * jnp.sum(jnp.mean(diff * diff, axis=-1, keepdims=True), axis=0, keepdims=True)
        _accum(loss_ref, jnp.broadcast_to(part, (1, LANES)))
        dx2 = diff * (1.0 / D_MODEL)
        dx2_ref[...] = dx2
        dy, dg = _rms_bwd(yh, r, g4_ref[...], dx2)
        dy_ref[...] = dy.astype(BF16)
        _accum(dg4_ref, dg)

    row = _rt(T, D_MODEL)
    return _tok_call(
        body, name=name, S=S, T=T, semantics="arbitrary",
        ins=[_tile(z, _rt3(4, T, FF_BLK)), _whole(wfo), _tile(x1, row), _tile(target, row), _whole(g4)],
        outs=[_out_acc((1, LANES)), _out_tile((S, D_MODEL), F32, row), _out_tile((S, D_MODEL), BF16, row),
              _out_acc((1, D_MODEL))])


def _ffn_out_bwd_call(dy, wfo, gu, *, name, T=ROW_TILE):
    S = dy.shape[0]

    def body(dy_ref, w_ref, gu_ref, dgu_ref):
        dyv = dy_ref[...]
        for d in range(4):
            dz = _dot(dyv, w_ref[d], NT)
            gt = gu_ref[d].astype(F32)
            up = gu_ref[d + 4].astype(F32)
            s = _sig(gt)
            dgu_ref[d] = (dz * up * s * (1.0 + gt * (1.0 - s))).astype(BF16)
            dgu_ref[d + 4] = (dz * gt * s).astype(BF16)

    blocks = _rt3(N_DEV, T, FF_BLK)
    return _tok_call(
        body, name=name, S=S, T=T,
        ins=[_tile(dy, _rt(T, D_MODEL)), _whole(wfo), _tile(gu, blocks)],
        outs=[_out_tile((N_DEV, S, FF_BLK), BF16, blocks)])[0]


def _ffn_in_bwd_call(dgu, wfi, dx2, x1, mix, g3, g2, *, name, T=ROW_TILE):
    S = x1.shape[0]

    def body(dgu_ref, w_ref, dx2_ref, x1_ref, mix_ref, g3_ref, g2_ref, dx1_ref, dmix_ref, dg3_ref, dg2_ref):
        dh = _dot(dgu_ref[0], w_ref[0], NN)
        for d in range(1, N_DEV):
            dh = dh + _dot(dgu_ref[d], w_ref[d], NN)
        r3, xh = _rms_stats(x1_ref[...])
        d3, dg3 = _rms_bwd(xh, r3, g3_ref[...], dh)
        dx1 = dx2_ref[...] + d3
        dx1_ref[...] = dx1
        r2, mh = _rms_stats(mix_ref[...])
        dmix, dg2 = _rms_bwd(mh, r2, g2_ref[...], dx1)
        dmix_ref[...] = dmix.astype(BF16)
        _accum(dg3_ref, dg3)
        _accum(dg2_ref, dg2)

    row = _rt(T, D_MODEL)
    return _tok_call(
        body, name=name, S=S, T=T, semantics="arbitrary",
        ins=[_tile(dgu, _rt3(N_DEV, T, FF_BLK)), _whole(wfi), _tile(dx2, row), _tile(x1, row), _tile(mix, row),
             _whole(g3), _whole(g2)],
        outs=[_out_tile((S, D_MODEL), F32, row), _out_tile((S, D_MODEL), BF16, row),
              _out_acc((1, D_MODEL)), _out_acc((1, D_MODEL))])


def _mixer_bwd_call(dmix, t1, t2, pg, w_out, w_bg, w_bs, *, name, T=ROW_TILE):
    S = dmix.shape[0]

    def body(dmix_ref, t1_ref, t2_ref, pg_ref, wo_ref, wbg_ref, wbs_ref, dt1_ref, dt2_ref, dpg_ref, dyg_ref, dys_ref):
        dm = _dot(dmix_ref[...], wo_ref[...], NT)
        sg = _sig(pg_ref[:, :D_MODEL].astype(F32))
        ss = _sig(pg_ref[:, D_MODEL:].astype(F32))
        dt1 = (dm * sg).astype(BF16)
        dt2 = (dm * ss).astype(BF16)
        dt1_ref[...] = dt1
        dt2_ref[...] = dt2
        dpg_ref[:, :D_MODEL] = (dm * t1_ref[...].astype(F32) * sg * (1.0 - sg)).astype(BF16)
        dpg_ref[:, D_MODEL:] = (dm * t2_ref[...].astype(F32) * ss * (1.0 - ss)).astype(BF16)
        dyg_ref[...] = _dot(dt1, wbg_ref[...], NT).astype(BF16)
        dys_ref[...] = _dot(dt2, wbs_ref[...], NT).astype(BF16)

    row = _rt(T, D_MODEL)
    wide = _rt(T, 2 * D_MODEL)
    b16 = lambda: _out_tile((S, D_MODEL), BF16, row)
    return _tok_call(
        body, name=name, S=S, T=T,
        ins=[_tile(dmix, row), _tile(t1, row), _tile(t2, row), _tile(pg, wide), _whole(w_out), _whole(w_bg), _whole(w_bs)],
        outs=[b16(), b16(), _out_tile((S, 2 * D_MODEL), BF16, wide), b16(), b16()])


def _in_proj_bwd_call(dpa, dlow, dps, dpg, w_a, w_low, w_s, w_g, x, dx1, g1, *, name, T=ROW_TILE):
    S = x.shape[0]

    def body(dpa_ref, dl_ref, dps_ref, dpg_ref, wa_ref, wl_ref, ws_ref, wg_ref, x_ref, dx1_ref, g1_ref, gx_ref, dg1_ref):
        da = (_dot(dpa_ref[...], wa_ref[...], NN) + _dot(dl_ref[...], wl_ref[...], NN)
              + _dot(dps_ref[...], ws_ref[...], NN) + _dot(dpg_ref[...], wg_ref[...], NN))
        r, xh = _rms_stats(x_ref[...])
        dxa, dg = _rms_bwd(xh, r, g1_ref[...], da)
        gx_ref[...] = dx1_ref[...] + dxa
        _accum(dg1_ref, dg)

    row = _rt(T, D_MODEL)
    return _tok_call(
        body, name=name, S=S, T=T, semantics="arbitrary",
        ins=[_tile(dpa, _rt(T, dpa.shape[1])), _tile(dlow, _rt(T, dlow.shape[1])), _tile(dps, _rt(T, dps.shape[1])),
             _tile(dpg, _rt(T, dpg.shape[1])), _whole(w_a), _whole(w_low), _whole(w_s), _whole(w_g),
             _tile(x, row), _tile(dx1, row), _whole(g1)],
        outs=[_out_tile((S, D_MODEL), F32, row), _out_acc((1, D_MODEL))])


TOKEN_TILE = 512


def _weight_grads_call(arrays, pairs, *, name, tk=TOKEN_TILE):
    S = arrays[0].shape[-2]
    tk = min(tk, S)
    n_in = len(arrays)

    def out_shape(i, j):
        a, b = arrays[i], arrays[j]
        if a.ndim == 3:
            return (a.shape[0], a.shape[2], b.shape[1])
        if b.ndim == 3:
            return (b.shape[0], a.shape[1], b.shape[2])
        return (a.shape[1], b.shape[1])

    shapes = [out_shape(i, j) for i, j in pairs]

    def body(*refs):
        ins, outs, accs = refs[:n_in], refs[n_in:n_in + len(pairs)], refs[n_in + len(pairs):]
        k = pl.program_id(0)

        @pl.when(k == 0)
        def _():
            for acc in accs:
                acc[...] = jnp.zeros_like(acc)

        for (i, j), acc in zip(pairs, accs):
            a_ref, b_ref = ins[i], ins[j]
            if len(a_ref.shape) == 3:
                for n in range(a_ref.shape[0]):
                    acc[n] += _dot(a_ref[n], b_ref[...], TN)
            elif len(b_ref.shape) == 3:
                a = a_ref[...]
                for n in range(b_ref.shape[0]):
                    acc[n] += _dot(a, b_ref[n], TN)
            else:
                b = b_ref[...]
                for m0 in range(0, a_ref.shape[1], 1024):
                    m1 = min(a_ref.shape[1], m0 + 1024)
                    acc[m0:m1, :] += _dot(a_ref[:, m0:m1], b, TN)

        @pl.when(k == S // tk - 1)
        def _():
            for out, acc in zip(outs, accs):
                out[...] = acc[...].astype(out.dtype)

    def in_spec(a):
        if a.ndim == 3:
            return pl.BlockSpec((a.shape[0], tk, a.shape[2]), lambda k: (0, k, 0))
        return pl.BlockSpec((tk, a.shape[1]), lambda k: (k, 0))

    tiles = [(in_spec(a).block_shape, a.dtype) for a in arrays]
    resident = [(s, F32) for s in shapes] + [(s, BF16) for s in shapes]
    return pl.pallas_call(
        body, name=name, grid=(S // tk,),
        in_specs=[in_spec(a) for a in arrays],
        out_specs=[_acc(s) for s in shapes],
        out_shape=[jax.ShapeDtypeStruct(s, BF16) for s in shapes],
        scratch_shapes=[pltpu.VMEM(s, F32) for s in shapes],
        compiler_params=_cp(("arbitrary",), _vmem_limit(tiles, resident)),
    )(*arrays)


def _ffn_in_grad_call(h, dgu, *, name, tk=TOKEN_TILE):
    S = h.shape[0]
    tk = min(tk, S)
    nb = 4
    shape = (nb, FF_BLK, D_MODEL)

    def body(h_ref, dgu_ref, out_ref, acc):
        k = pl.program_id(1)

        @pl.when(k == 0)
        def _():
            acc[...] = jnp.zeros_like(acc)

        hv = h_ref[...]
        for n in range(nb):
            acc[n] += _dot(dgu_ref[n], hv, TN)

        @pl.when(k == S // tk - 1)
        def _():
            out_ref[...] = acc[...].astype(out_ref.dtype)

    tiles = [((tk, D_MODEL), BF16), ((nb, tk, FF_BLK), BF16), (shape, BF16)]
    return pl.pallas_call(
        body, name=name, grid=(N_DEV // nb, S // tk),
        in_specs=[pl.BlockSpec((tk, D_MODEL), lambda g, k: (k, 0)),
                  pl.BlockSpec((nb, tk, FF_BLK), lambda g, k: (g, k, 0))],
        out_specs=pl.BlockSpec(shape, lambda g, k: (g, 0, 0)),
        out_shape=jax.ShapeDtypeStruct((N_DEV, FF_BLK, D_MODEL), BF16),
        scratch_shapes=[pltpu.VMEM(shape, F32)],
        compiler_params=_cp(("parallel", "arbitrary"), _vmem_limit(tiles, [(shape, F32)])),
    )(h, dgu)


GLA_TILE = 256
Q_OFF, K_OFF, V_OFF, R_OFF = 0, QK, 2 * QK, 2 * QK + GV


def _tri(lower):
    r = lax.broadcasted_iota(jnp.int32, (CHUNK, CHUNK), 0)
    c = lax.broadcasted_iota(jnp.int32, (CHUNK, CHUNK), 1)
    return jnp.where((r >= c) if lower else (c >= r), 1.0, 0.0).astype(BF16)


def _gla_fwd_call(pa, alow, wg, bgate, gnorm, *, name):
    S = pa.shape[0]
    Tg = min(GLA_TILE, S)
    cb = Tg // CHUNK

    def body(pa_ref, al_ref, wg_ref, bg_ref, gn_ref, y_ref, st_ref, state):
        @pl.when(pl.program_id(0) == 0)
        def _():
            state[...] = jnp.zeros_like(state)

        logit = _dot(al_ref[...], wg_ref[...], NN) + bg_ref[...]
        ls = _logsig(logit) * (1.0 / 16.0)
        tri = _tri(True)
        for c in range(cb):
            rows = pl.ds(c * CHUNK, CHUNK)
            cum = _exact_mask_dot(tri, ls[c * CHUNK:(c + 1) * CHUNK])
            tot = cum[CHUNK - 1:CHUNK]
            kd = (pa_ref[rows, pl.ds(K_OFF, QK)].astype(F32) * jnp.exp(tot - cum)).astype(BF16)
            decay = jnp.exp(tot)
            for h in range(HEADS):
                lanes = slice(h * DK, (h + 1) * DK)
                new = state[h] * decay[:, lanes] + _dot(pa_ref[rows, pl.ds(V_OFF + h * DV, DV)], kd[:, lanes], TN)
                state[h] = new
                st_ref[c, h] = new
        for c in range(cb):
            rows = pl.ds(c * CHUNK, CHUNK)
            for h in range(HEADS):
                qs = (pa_ref[rows, pl.ds(Q_OFF + h * DK, DK)].astype(F32) * Q_SCALE).astype(BF16)
                o = _dot(qs, st_ref[c, h].astype(BF16), NT)
                rs = lax.rsqrt(jnp.mean(o * o, axis=-1, keepdims=True) + EPS)
                rr = pa_ref[rows, pl.ds(R_OFF + h * DV, DV)].astype(F32)
                y_ref[rows, pl.ds(h * DV, DV)] = (o * rs * gn_ref[h] * (rr * _sig(rr))).astype(BF16)

    return pl.pallas_call(
        body, name=name, grid=(S // Tg,),
        in_specs=[
            pl.BlockSpec((Tg, 2 * QK + 2 * GV), lambda t: (t, 0)),
            pl.BlockSpec((Tg, LANES), lambda t: (t, 0)),
            pl.BlockSpec((LANES, QK), lambda t: (0, 0)),
            pl.BlockSpec((1, QK), lambda t: (0, 0)),
            pl.BlockSpec((HEADS, 1, DV), lambda t: (0, 0, 0)),
        ],
        out_specs=[
            pl.BlockSpec((Tg, GV), lambda t: (t, 0)),
            pl.BlockSpec((cb, HEADS, DV, DK), lambda t: (t, 0, 0, 0)),
        ],
        out_shape=[jax.ShapeDtypeStruct((S, GV), BF16),
                   jax.ShapeDtypeStruct((S // CHUNK, HEADS, DV, DK), F32)],
        scratch_shapes=[pltpu.VMEM((HEADS, DV, DK), F32)],
        compiler_params=_cp(("arbitrary",), VMEM_BIG),
    )(pa, alow, wg, bgate, gnorm)


def _gla_bwd_call(pa, alow, wg, bgate, gnorm, states, dy, *, name):
    S = pa.shape[0]
    Tg = min(GLA_TILE, S)
    cb = Tg // CHUNK
    nt = S // Tg

    def rev(t):
        return nt - 1 - t

    def body(pa_ref, al_ref, wg_ref, bg_ref, gn_ref, st_ref, prev_ref, dy_ref,
             dpa_ref, dl_ref, dgn_ref, dbg_ref, carry, pbuf):
        t = pl.program_id(0)

        @pl.when(t == 0)
        def _():
            carry[...] = jnp.zeros_like(carry)
            dgn_ref[...] = jnp.zeros_like(dgn_ref)
            dbg_ref[...] = jnp.zeros_like(dbg_ref)

        logit = _dot(al_ref[...], wg_ref[...], NN) + bg_ref[...]
        ls = _logsig(logit) * (1.0 / 16.0)
        sneg = _sig(-logit)
        tri = _tri(True)
        upper = _tri(False)
        first_tile = rev(t) == 0
        heads = range(HEADS)

        w, decay, kd = [], [], []
        for c in range(cb):
            rows = pl.ds(c * CHUNK, CHUNK)
            cum = _exact_mask_dot(tri, ls[c * CHUNK:(c + 1) * CHUNK])
            tot = cum[CHUNK - 1:CHUNK]
            w.append(jnp.exp(tot - cum))
            decay.append(jnp.exp(tot))
            kd.append(pa_ref[rows, pl.ds(K_OFF, QK)].astype(F32) * w[c])

        dgn = [jnp.zeros((1, DV), F32) for _ in heads]
        for c in range(cb):
            rows = pl.ds(c * CHUNK, CHUNK)
            for h in heads:
                gn = gn_ref[h]
                qs = (pa_ref[rows, pl.ds(Q_OFF + h * DK, DK)].astype(F32) * Q_SCALE).astype(BF16)
                st16 = st_ref[c, h].astype(BF16)
                o = _dot(qs, st16, NT)
                rs = lax.rsqrt(jnp.mean(o * o, axis=-1, keepdims=True) + EPS)
                oh = o * rs
                rr = pa_ref[rows, pl.ds(R_OFF + h * DV, DV)].astype(F32)
                sr = _sig(rr)
                dyv = dy_ref[rows, pl.ds(h * DV, DV)].astype(F32)
                dpa_ref[rows, pl.ds(R_OFF + h * DV, DV)] = (
                    dyv * oh * gn * sr * (1.0 + rr * (1.0 - sr))).astype(BF16)
                don = dyv * (rr * sr)
                dgn[h] = dgn[h] + jnp.sum(don * oh, axis=0, keepdims=True)
                doh = don * gn
                do16 = (rs * (doh - oh * jnp.mean(doh * oh, axis=-1, keepdims=True))).astype(BF16)
                dpa_ref[rows, pl.ds(Q_OFF + h * DK, DK)] = (_dot(do16, st16, NN) * Q_SCALE).astype(BF16)
                pbuf[c, h] = _dot(do16, qs, TN)
        for h in heads:
            dgn_ref[h] += dgn[h]

        dkd = [[None] * HEADS for _ in range(cb)]
        ddecay = [[None] * HEADS for _ in range(cb)]
        for c in reversed(range(cb)):
            rows = pl.ds(c * CHUNK, CHUNK)
            for h in heads:
                lanes = slice(h * DK, (h + 1) * DK)
                gt = pbuf[c, h] + carry[h]
                gt16 = gt.astype(BF16)
                dkd[c][h] = _dot(pa_ref[rows, pl.ds(V_OFF + h * DV, DV)], gt16, NN)
                dpa_ref[rows, pl.ds(V_OFF + h * DV, DV)] = _dot(kd[c][:, lanes].astype(BF16), gt16, NT).astype(BF16)
                if c > 0:
                    st_prev = st_ref[c - 1, h]
                else:
                    st_prev = jnp.where(first_tile, 0.0, prev_ref[0, h])
                ddecay[c][h] = jnp.sum(gt * st_prev, axis=0, keepdims=True)
                carry[h] = gt * decay[c][:, lanes]

        dbg = jnp.zeros((1, QK), F32)
        for c in range(cb):
            rows = pl.ds(c * CHUNK, CHUNK)
            dkd_c = jnp.concatenate(dkd[c], axis=1)
            dpa_ref[rows, pl.ds(K_OFF, QK)] = (dkd_c * w[c]).astype(BF16)
            e = dkd_c * kd[c]
            dtot = jnp.sum(e, axis=0, keepdims=True) + jnp.concatenate(ddecay[c], axis=1) * decay[c]
            dls = dtot - _exact_mask_dot(upper, e)
            dlogit = dls * (1.0 / 16.0) * sneg[c * CHUNK:(c + 1) * CHUNK]
            dl_ref[rows, :] = dlogit.astype(BF16)
            dbg = dbg + jnp.sum(dlogit, axis=0, keepdims=True)
        dbg_ref[...] += dbg

    return pl.pallas_call(
        body, name=name, grid=(nt,),
        in_specs=[
            pl.BlockSpec((Tg, 2 * QK + 2 * GV), lambda t: (rev(t), 0)),
            pl.BlockSpec((Tg, LANES), lambda t: (rev(t), 0)),
            pl.BlockSpec((LANES, QK), lambda t: (0, 0)),
            pl.BlockSpec((1, QK), lambda t: (0, 0)),
            pl.BlockSpec((HEADS, 1, DV), lambda t: (0, 0, 0)),
            pl.BlockSpec((cb, HEADS, DV, DK), lambda t: (rev(t), 0, 0, 0)),
            pl.BlockSpec((1, HEADS, DV, DK), lambda t: (jnp.maximum(rev(t) * cb - 1, 0), 0, 0, 0)),
            pl.BlockSpec((Tg, GV), lambda t: (rev(t), 0)),
        ],
        out_specs=[
            pl.BlockSpec((Tg, 2 * QK + 2 * GV), lambda t: (rev(t), 0)),
            pl.BlockSpec((Tg, QK), lambda t: (rev(t), 0)),
            pl.BlockSpec((HEADS, 1, DV), lambda t: (0, 0, 0)),
            pl.BlockSpec((1, QK), lambda t: (0, 0)),
        ],
        out_shape=[jax.ShapeDtypeStruct((S, 2 * QK + 2 * GV), BF16), jax.ShapeDtypeStruct((S, QK), BF16),
                   jax.ShapeDtypeStruct((HEADS, 1, DV), F32), jax.ShapeDtypeStruct((1, QK), F32)],
        scratch_shapes=[pltpu.VMEM((HEADS, DV, DK), F32), pltpu.VMEM((cb, HEADS, DV, DK), F32)],
        compiler_params=_cp(("arbitrary",), VMEM_BIG),
    )(pa, alow, wg, bgate, gnorm, states, states, dy)


SGU_TILE = 256


def _sgu_mask():
    r = lax.broadcasted_iota(jnp.int32, (SBLOCK, SBLOCK), 0)
    c = lax.broadcasted_iota(jnp.int32, (SBLOCK, SBLOCK), 1)
    return (c < CHUNK) | (r >= CHUNK)


def _ln_stats(vf):
    mu = jnp.mean(vf, axis=-1, keepdims=True)
    xc = vf - mu
    rs = lax.rsqrt(jnp.mean(xc * xc, axis=-1, keepdims=True) + EPS)
    return rs, xc * rs


def _sgu_fwd_call(ps, ln_g, ln_b, w_sp, b_sp, *, name):
    S = ps.shape[0]
    Ts = min(SGU_TILE, S)

    def body(ps_ref, lg_ref, lb_ref, w_ref, b_ref, y_ref):
        mask = _sgu_mask()
        for g in range(GROUPS):
            wm = jnp.where(mask, w_ref[g], 0.0).astype(BF16)
            for p in range(Ts // SBLOCK):
                rows = pl.ds(p * SBLOCK, SBLOCK)
                u = _gelu(ps_ref[rows, pl.ds(g * DG, DG)].astype(F32))
                _, xh = _ln_stats(_gelu(ps_ref[rows, pl.ds(D_MODEL + g * DG, DG)].astype(F32)))
                vn = xh * lg_ref[g] + lb_ref[g]
                mixed = _dot(wm, vn.astype(BF16), NN) + b_ref[g]
                y_ref[rows, pl.ds(g * DG, DG)] = (u * mixed).astype(BF16)

    full3 = lambda a, b, c: pl.BlockSpec((a, b, c), lambda t: (0, 0, 0))
    return pl.pallas_call(
        body, name=name, grid=(S // Ts,),
        in_specs=[pl.BlockSpec((Ts, 2 * D_MODEL), lambda t: (t, 0)),
                  full3(GROUPS, 1, DG), full3(GROUPS, 1, DG), full3(GROUPS, SBLOCK, SBLOCK), full3(GROUPS, SBLOCK, 1)],
        out_specs=pl.BlockSpec((Ts, D_MODEL), lambda t: (t, 0)),
        out_shape=jax.ShapeDtypeStruct((S, D_MODEL), BF16),
        compiler_params=_cp(("parallel",)),
    )(ps, ln_g, ln_b, w_sp, b_sp)


def _sgu_bwd_call(ps, ln_g, ln_b, w_sp, b_sp, dy, *, name):
    S = ps.shape[0]
    Ts = min(SGU_TILE, S)

    def body(ps_ref, lg_ref, lb_ref, w_ref, b_ref, dy_ref, ds_ref, dlg_ref, dlb_ref, dw_ref, db_ref):
        @pl.when(pl.program_id(0) == 0)
        def _():
            dlg_ref[...] = jnp.zeros_like(dlg_ref)
            dlb_ref[...] = jnp.zeros_like(dlb_ref)
            dw_ref[...] = jnp.zeros_like(dw_ref)
            db_ref[...] = jnp.zeros_like(db_ref)

        mask = _sgu_mask()
        for g in range(GROUPS):
            wm = jnp.where(mask, w_ref[g], 0.0).astype(BF16)
            lg = lg_ref[g]
            for p in range(Ts // SBLOCK):
                rows = pl.ds(p * SBLOCK, SBLOCK)
                su = ps_ref[rows, pl.ds(g * DG, DG)].astype(F32)
                sv = ps_ref[rows, pl.ds(D_MODEL + g * DG, DG)].astype(F32)
                u = _gelu(su)
                rs, xh = _ln_stats(_gelu(sv))
                vn16 = (xh * lg + lb_ref[g]).astype(BF16)
                mixed = _dot(wm, vn16, NN) + b_ref[g]
                dyv = dy_ref[rows, pl.ds(g * DG, DG)].astype(F32)
                ds_ref[rows, pl.ds(g * DG, DG)] = (dyv * mixed * _gelu_grad(su)).astype(BF16)
                dmix = dyv * u
                dmix16 = dmix.astype(BF16)
                db_ref[g] += jnp.sum(dmix, axis=-1, keepdims=True)
                dw_ref[g] += jnp.where(mask, _dot(dmix16, vn16, NT), 0.0)
                dvn = _dot(wm, dmix16, TN)
                dlg_ref[g] += jnp.sum(dvn * xh, axis=0, keepdims=True)
                dlb_ref[g] += jnp.sum(dvn, axis=0, keepdims=True)
                dxh = dvn * lg
                dvf = rs * (dxh - jnp.mean(dxh, axis=-1, keepdims=True)
                            - xh * jnp.mean(dxh * xh, axis=-1, keepdims=True))
                ds_ref[rows, pl.ds(D_MODEL + g * DG, DG)] = (dvf * _gelu_grad(sv)).astype(BF16)

    full3 = lambda a, b, c: pl.BlockSpec((a, b, c), lambda t: (0, 0, 0))
    return pl.pallas_call(
        body, name=name, grid=(S // Ts,),
        in_specs=[pl.BlockSpec((Ts, 2 * D_MODEL), lambda t: (t, 0)),
                  full3(GROUPS, 1, DG), full3(GROUPS, 1, DG), full3(GROUPS, SBLOCK, SBLOCK), full3(GROUPS, SBLOCK, 1),
                  pl.BlockSpec((Ts, D_MODEL), lambda t: (t, 0))],
        out_specs=[pl.BlockSpec((Ts, 2 * D_MODEL), lambda t: (t, 0)),
                   full3(GROUPS, 1, DG), full3(GROUPS, 1, DG), full3(GROUPS, SBLOCK, SBLOCK), full3(GROUPS, SBLOCK, 1)],
        out_shape=[jax.ShapeDtypeStruct((S, 2 * D_MODEL), BF16),
                   jax.ShapeDtypeStruct((GROUPS, 1, DG), F32), jax.ShapeDtypeStruct((GROUPS, 1, DG), F32),
                   jax.ShapeDtypeStruct((GROUPS, SBLOCK, SBLOCK), F32),
                   jax.ShapeDtypeStruct((GROUPS, SBLOCK, 1), F32)],
        compiler_params=_cp(("arbitrary",)),
    )(ps, ln_g, ln_b, w_sp, b_sp, dy)


def _position():
    return lax.axis_index("x"), lax.axis_index("y"), lax.axis_index("c")


def _gather_copies(srcs, dsts, send_sems, recv_sems, local_sems):
    x, y, c = _position()
    me, sibling = (x, y, c), (x, y, 1 - c)
    chips = [(1 - x, y), (x, 1 - y), (1 - x, 1 - y)]
    n = len(srcs)

    def slab(a, block):
        px, py, pc = block
        return dsts[a].at[4 * px + 2 * py + pc]

    def copy(a, k, block, to, src=None):
        return pltpu.make_async_remote_copy(
            src_ref=slab(a, block) if src is None else src, dst_ref=slab(a, block),
            send_sem=send_sems.at[7 * a + k], recv_sem=recv_sems.at[7 * a + k], device_id=to, device_id_type=MESH)

    mine = [pltpu.make_async_copy(srcs[a], slab(a, me), local_sems.at[a]) for a in range(n)]
    for cp in mine:
        cp.start()
    first = []
    for a in range(n):
        first.append(copy(a, 0, me, sibling, src=srcs[a]))
        first += [copy(a, 1 + j, me, (*chip, c), src=srcs[a]) for j, chip in enumerate(chips)]
    for cp in first:
        cp.start()
    passed = []
    for j, chip in enumerate(chips):
        for a in range(n):
            copy(a, 1 + j, (*chip, c), me).wait_recv()
            cp = copy(a, 4 + j, (*chip, c), sibling)
            cp.start()
            passed.append(cp)
    for a in range(n):
        copy(a, 0, sibling, me).wait_recv()
        for j, chip in enumerate(chips):
            copy(a, 4 + j, (*chip, 1 - c), me).wait_recv()
    for cp in first + passed:
        cp.wait_send()
    for cp in mine:
        cp.wait()


def _all_gather_hbm(shards, *, name):
    n = len(shards)

    def body(*refs):
        srcs, dsts = refs[:n], refs[n:2 * n]
        send_sems, recv_sems, local_sems = refs[2 * n:]
        _gather_copies(srcs, dsts, send_sems, recv_sems, local_sems)

    return pl.pallas_call(
        body, name=name,
        in_specs=[ANY] * n, out_specs=[ANY] * n,
        out_shape=[jax.ShapeDtypeStruct((N_DEV, *s.shape), s.dtype) for s in shards],
        scratch_shapes=_comm_sems(n),
    )(*shards)


def _all_reduce_small(part, *, name):
    R, W = part.shape

    def body(x_ref, out_ref, gathered, send_sems, recv_sems, local_sems):
        _gather_copies([x_ref], [gathered], send_sems, recv_sems, local_sems)
        acc = gathered[0]
        for d in range(1, N_DEV):
            acc = acc + gathered[d]
        out_ref[...] = acc

    return pl.pallas_call(
        body, name=name,
        in_specs=[VMEM_SPEC], out_specs=VMEM_SPEC,
        out_shape=jax.ShapeDtypeStruct((R, W), F32),
        scratch_shapes=[pltpu.VMEM((N_DEV, R, W), F32),
                        pltpu.SemaphoreType.DMA((7,)), pltpu.SemaphoreType.DMA((7,)), pltpu.SemaphoreType.DMA((1,))],
    )(part)


FLIPS = [(fx, fy, fc) for fx in (0, 1) for fy in (0, 1) for fc in (0, 1)][1:]


def _scatter_copies(srcs, dsts, send_sems, recv_sems, local_sems):
    n = len(srcs)
    x, y, c = _position()
    me = 4 * x + 2 * y + c
    mine = [pltpu.make_async_copy(srcs[a].at[me], dsts[a].at[me], local_sems.at[a]) for a in range(n)]
    for cp in mine:
        cp.start()
    copies = []
    for k, (fx, fy, fc) in enumerate(FLIPS):
        tx = 1 - x if fx else x
        ty = 1 - y if fy else y
        tc = 1 - c if fc else c
        peer = 4 * tx + 2 * ty + tc
        for a in range(n):
            cp = pltpu.make_async_remote_copy(
                src_ref=srcs[a].at[peer], dst_ref=dsts[a].at[me],
                send_sem=send_sems.at[7 * a + k], recv_sem=recv_sems.at[7 * a + k],
                device_id=(tx, ty, tc), device_id_type=MESH)
            cp.start()
            copies.append(cp)
    for cp in copies:
        cp.wait()
    for cp in mine:
        cp.wait()


def _comm_sems(n):
    return [pltpu.SemaphoreType.DMA((7 * n,)), pltpu.SemaphoreType.DMA((7 * n,)), pltpu.SemaphoreType.DMA((n,))]


def _scatter_blocks(parts, *, name):
    n = len(parts)

    def body(*refs):
        _scatter_copies(refs[:n], refs[n:2 * n], *refs[2 * n:])

    return pl.pallas_call(
        body, name=name,
        in_specs=[ANY] * n, out_specs=[ANY] * n,
        out_shape=[jax.ShapeDtypeStruct(p.shape, p.dtype) for p in parts],
        scratch_shapes=_comm_sems(n),
    )(*parts)


def _handshake(peers):
    barrier = pltpu.get_barrier_semaphore()
    for peer in peers:
        pl.semaphore_signal(barrier, inc=1, device_id=peer, device_id_type=MESH)
    pl.semaphore_wait(barrier, len(peers))


def _sequencer_call(arrays, out_types, copies_fn, peers_fn, *, name, collective_id):
    n = len(arrays)
    srcs = [jax.new_ref(a, memory_space=pltpu.MemorySpace.HBM) for a in arrays]
    dsts = [jax.empty_ref(t, memory_space=pltpu.MemorySpace.HBM) for t in out_types]

    @pl.kernel(mesh=plsc.ScalarSubcoreMesh(axis_name="sequencer", num_cores=1), name=name,
               scratch_types=_comm_sems(n), compiler_params=pltpu.CompilerParams(collective_id=collective_id))
    def launch(send_sems, recv_sems, local_sems):
        _handshake(peers_fn())
        copies_fn(srcs, dsts, send_sems, recv_sems, local_sems)

    launch()
    return [d[...] for d in dsts]


def _all_other_devices():
    x, y, c = _position()
    return [(1 - x if fx else x, 1 - y if fy else y, 1 - c if fc else c) for fx, fy, fc in FLIPS]


def _gather_peers():
    x, y, c = _position()
    return [(x, y, 1 - c), (1 - x, y, c), (x, 1 - y, c), (1 - x, 1 - y, c)]


def _scatter_blocks_async(parts, *, name, collective_id):
    return _sequencer_call(parts, [jax.ShapeDtypeStruct(p.shape, p.dtype) for p in parts],
                           _scatter_copies, _all_other_devices, name=name, collective_id=collective_id)


def _all_gather_async(shards, *, name, collective_id):
    return _sequencer_call(shards, [jax.ShapeDtypeStruct((N_DEV, *s.shape), s.dtype) for s in shards],
                           _gather_copies, _gather_peers, name=name, collective_id=collective_id)


def _adamw_math(w, g, m, v):
    m = ADAM_B1 * m + (1.0 - ADAM_B1) * g
    v = ADAM_B2 * v + (1.0 - ADAM_B2) * (g * g)
    m_hat = m / (1.0 - ADAM_B1 ** ADAM_STEP)
    v_hat = v / (1.0 - ADAM_B2 ** ADAM_STEP)
    delta = -ADAM_LR * (m_hat / (jnp.sqrt(v_hat) + ADAM_EPS) + ADAM_WD * w)
    return delta, m, v


def _adamw_reduce_call(recv, w, m, v, *, name, T=128):
    R, W = w.shape
    if R % T == 0:
        tr, tw = T, W
    elif (R // 2) % 16 == 0:
        tr, tw = R // 2, W
    else:
        tr, tw = R, 2 * LANES

    def body(p_ref, w_ref, m_ref, v_ref, g_out, d_out, m_out, v_out):
        g = p_ref[0].astype(F32)
        for d in range(1, N_DEV):
            g = g + p_ref[d].astype(F32)
        g_out[...] = g
        d_out[...], m_out[...], v_out[...] = _adamw_math(w_ref[...], g, m_ref[...], v_ref[...])

    row = pl.BlockSpec((tr, tw), lambda i, j: (i, j))
    out = jax.ShapeDtypeStruct((R, W), F32)
    return pl.pallas_call(
        body, name=name, grid=(R // tr, W // tw),
        in_specs=[pl.BlockSpec((N_DEV, tr, tw), lambda i, j: (0, i, j)), row, row, row],
        out_specs=[row] * 4, out_shape=[out] * 4,
        compiler_params=_cp(("parallel", "parallel"), VMEM_BIG),
    )(recv, w, m, v)


def _adamw_small_call(w, g, m, v, *, name):
    def body(w_ref, g_ref, m_ref, v_ref, d_out, m_out, v_out):
        d_out[...], m_out[...], v_out[...] = _adamw_math(w_ref[...], g_ref[...], m_ref[...], v_ref[...])

    out = jax.ShapeDtypeStruct(w.shape, F32)
    return pl.pallas_call(body, name=name, in_specs=[VMEM_SPEC] * 4, out_specs=[VMEM_SPEC] * 3,
                          out_shape=[out] * 3)(w, g, m, v)


def _tile_rows(n_elems):
    return -(-n_elems // (8 * LANES)) * 8


def _pack_rows(parts, rows):
    pieces = []
    for p in parts:
        q = p.reshape(-1, LANES)
        pieces.append(jnp.pad(q, ((0, _tile_rows(p.size) - q.shape[0]), (0, 0))))
    buf = jnp.concatenate(pieces, axis=0)
    return jnp.pad(buf, ((0, rows - buf.shape[0]), (0, 0)))


def _unpack_rows(buf, shapes):
    out, off = [], 0
    for shp in shapes:
        n = 1
        for s in shp:
            n *= s
        out.append(buf[off:off + n // LANES].reshape(shp))
        off += _tile_rows(n)
    return out


def _join_blocks_call(blocks, *, name, tw=2 * LANES):
    nb, r, w = blocks.shape

    def body(b_ref, o_ref):
        for d in range(nb):
            o_ref[d * r:(d + 1) * r, :] = b_ref[d]

    return pl.pallas_call(
        body, name=name, grid=(w // tw,),
        in_specs=[pl.BlockSpec((nb, r, tw), lambda j: (0, 0, j))],
        out_specs=pl.BlockSpec((nb * r, tw), lambda j: (0, j)),
        out_shape=jax.ShapeDtypeStruct((nb * r, w), blocks.dtype),
        compiler_params=_cp(("parallel",)),
    )(blocks)


def _split_blocks_call(parts, *, name, tw=2 * LANES):
    w = parts[0].shape[1]
    starts = [0]
    for q in parts:
        starts.append(starts[-1] + q.shape[0])
    assert starts[-1] == N_DEV * IN_BLK

    def body(*refs):
        o_ref = refs[-1]
        for d in range(N_DEV):
            lo, hi = IN_BLK * d, IN_BLK * (d + 1)
            for ref, c0, c1 in zip(refs[:-1], starts[:-1], starts[1:]):
                s0, e0 = max(lo, c0), min(hi, c1)
                if s0 < e0:
                    o_ref[d, s0 - lo:e0 - lo, :] = ref[s0 - c0:e0 - c0, :]

    return pl.pallas_call(
        body, name=name, grid=(w // tw,),
        in_specs=[pl.BlockSpec((q.shape[0], tw), lambda j: (0, j)) for q in parts],
        out_specs=pl.BlockSpec((N_DEV, IN_BLK, tw), lambda j: (0, 0, j)),
        out_shape=jax.ShapeDtypeStruct((N_DEV, IN_BLK, w), parts[0].dtype),
        compiler_params=_cp(("parallel",)),
    )(*parts)


def _local_step(x, target, W, scatter, finish):
    g1, g2, g3, g4 = [W[n].reshape(1, D_MODEL) for n in ("norm_pre_mix", "norm_post_mix", "norm_pre_ffn", "norm_post_ffn")]
    w_a, w_low, w_s, w_g = W["w_in_a"], W["w_in_low"], W["w_in_s"], W["w_in_g"]
    wfi, wfo = W["w_ffn_in"], W["w_ffn_out"].reshape(4, FF_BLK, D_MODEL)
    wg = jnp.pad(W["w_gate_up"], ((0, LANES - RANK), (0, 0))).astype(BF16)
    bgate = W["b_gate"].reshape(1, QK)
    gnorm = W["gla_norm"].reshape(HEADS, 1, DV)
    ln_g = W["sgu_ln_g"].reshape(GROUPS, 1, DG)
    ln_b = W["sgu_ln_b"].reshape(GROUPS, 1, DG)
    w_sp = W["w_spatial"]
    b_sp = W["b_spatial"].reshape(GROUPS, SBLOCK, 1)

    a, pa, alow, ps, pg = _in_proj_call(x, g1, w_a, w_low, w_s, w_g, name="in_proj")
    y_gla, states = _gla_fwd_call(pa, alow, wg, bgate, gnorm, name="gla_fwd")
    y_sgu = _sgu_fwd_call(ps, ln_g, ln_b, w_sp, b_sp, name="sgu_fwd")
    t1, t2, merged, mix, x1, h = _mixer_tail_call(y_gla, y_sgu, pg, x, W["w_branch_gla"], W["w_branch_sgu"],
                                                  W["w_out"], g2, g3, name="mixer_tail")
    gu, z = _ffn_in_call(h, wfi, name="ffn_in")
    loss, dx2, dy, dg4 = _ffn_out_loss_call(z, wfo, x1, target, g4, name="ffn_out_loss")

    grads = {"norm_post_ffn": dg4}
    done = {}
    dgu = _ffn_out_bwd_call(dy, wfo, gu, name="ffn_out_bwd")
    dw_ffn_out = _weight_grads_call([z, dy], [(0, 1)], name="d_ffn_out_w")[0].reshape(N_DEV, D_FF // N_DEV, D_MODEL)
    dw_ffn_in = _ffn_in_grad_call(h, dgu, name="d_ffn_in_w")
    dgu, dw_ffn_out, dw_ffn_in = lax.optimization_barrier((dgu, dw_ffn_out, dw_ffn_in))
    ffn_received = scatter(("w_ffn_out", "w_ffn_in"), [dw_ffn_out, dw_ffn_in])
    dx1, dmix, grads["norm_pre_ffn"], grads["norm_post_mix"] = _ffn_in_bwd_call(dgu, wfi, dx2, x1, mix, g3, g2, name="ffn_in_bwd")
    dt1, dt2, dpg, dy_gla, dy_sgu = _mixer_bwd_call(dmix, t1, t2, pg, W["w_out"], W["w_branch_gla"], W["w_branch_sgu"],
                                                    name="mixer_bwd")
    rows = D_MODEL // N_DEV
    mixer_grads = _weight_grads_call([merged, dmix, y_gla, dt1, y_sgu, dt2], [(0, 1), (2, 3), (4, 5)], name="d_mixer_w")
    dy_gla, dy_sgu, mixer_grads = lax.optimization_barrier((dy_gla, dy_sgu, mixer_grads))
    mixer_received = scatter(("w_out", "w_branch_gla", "w_branch_sgu"),
                             [g.reshape(N_DEV, rows, D_MODEL) for g in mixer_grads])
    dpa, dlogit, dgn, dbg = _gla_bwd_call(pa, alow, wg, bgate, gnorm, states, dy_gla, name="gla_bwd")
    ffn_done, dpa, dlogit = lax.optimization_barrier((finish(ffn_received), dpa, dlogit))
    done.update(ffn_done)
    dps, dlg, dlb, dwsp, dbsp = _sgu_bwd_call(ps, ln_g, ln_b, w_sp, b_sp, dy_sgu, name="sgu_bwd")
    mixer_done, dps = lax.optimization_barrier((finish(mixer_received), dps))
    done.update(mixer_done)
    dlow = _mm(dlogit, wg, "nt", BF16, name="d_gate_up_x")
    dw_a, dw_low = _weight_grads_call([a, dpa, dlow], [(1, 0), (2, 0)], name="d_in_w_qkvr")
    dw_s, dw_g = _weight_grads_call([a, dps, dpg], [(1, 0), (2, 0)], name="d_in_w_sgu_gates")
    dw_in = _split_blocks_call([dw_a, dw_low[:RANK], dw_s, dw_g], name="split_w_in_grad")
    in_received = scatter(("w_in",), [dw_in])
    dwg = _mm(alow, dlogit, "tn", F32, name="d_gate_up_w")
    grad_x, grads["norm_pre_mix"] = _in_proj_bwd_call(dpa, dlow, dps, dpg, w_a, w_low, w_s, w_g, x, dx1, g1,
                                                      name="in_proj_bwd")

    grads["w_gate_up"] = dwg[:RANK]
    grads["b_gate"] = dbg
    grads["gla_norm"] = dgn
    grads["sgu_ln_g"] = dlg
    grads["sgu_ln_b"] = dlb
    grads["w_spatial"] = dwsp
    grads["b_spatial"] = dbsp
    done.update(finish(in_received))
    return loss, grad_x, grads, done


WEIGHTS = ("norm_pre_mix", "w_in", "w_gate_up", "b_gate", "gla_norm", "sgu_ln_g", "sgu_ln_b", "w_spatial",
           "b_spatial", "w_branch_gla", "w_branch_sgu", "w_out", "norm_post_mix", "norm_pre_ffn", "w_ffn_in",
           "w_ffn_out", "norm_post_ffn")
BIG = ("w_in", "w_branch_gla", "w_branch_sgu", "w_out", "w_ffn_in", "w_ffn_out")
MIXER = ("w_branch_gla", "w_branch_sgu", "w_out")
FFN = ("w_ffn_in", "w_ffn_out")
COLUMN_SHARDED = ("w_in", "w_ffn_in")
SMALL = tuple(n for n in WEIGHTS if n not in BIG)
SMALL_SHARDED = ("w_gate_up", "gla_norm", "sgu_ln_g", "sgu_ln_b")
SMALL_FULL = {"norm_pre_mix": (1024,), "w_gate_up": (16, 512), "b_gate": (512,), "gla_norm": (4, 256),
              "sgu_ln_g": (4, 256), "sgu_ln_b": (4, 256), "w_spatial": (4, 128, 128), "b_spatial": (4, 128),
              "norm_post_mix": (1024,), "norm_pre_ffn": (1024,), "norm_post_ffn": (1024,)}
SMALL_GRAD_ROWS = 648
SMALL_STATE_ROWS = 600
SMALL_GATHER_ROWS = 32


def kernel(x, norm_pre_mix, w_in, w_gate_up, b_gate, gla_norm, sgu_ln_g, sgu_ln_b, w_spatial, b_spatial, w_branch_gla, w_branch_sgu, w_out, norm_post_mix, norm_pre_ffn, w_ffn_in, w_ffn_out, norm_post_ffn, loss_target, m_norm_pre_mix, m_w_in, m_w_gate_up, m_b_gate, m_gla_norm, m_sgu_ln_g, m_sgu_ln_b, m_w_spatial, m_b_spatial, m_w_branch_gla, m_w_branch_sgu, m_w_out, m_norm_post_mix, m_norm_pre_ffn, m_w_ffn_in, m_w_ffn_out, m_norm_post_ffn, v_norm_pre_mix, v_w_in, v_w_gate_up, v_b_gate, v_gla_norm, v_sgu_ln_g, v_sgu_ln_b, v_w_spatial, v_b_spatial, v_w_branch_gla, v_w_branch_sgu, v_w_out, v_norm_post_mix, v_norm_pre_ffn, v_w_ffn_in, v_w_ffn_out, v_norm_post_ffn):
    given = dict(locals())
    def local(a, n):
        return a[0].T if n in COLUMN_SHARDED else a[0]

    w = {n: local(given[n], n) for n in WEIGHTS}
    m = {n: local(given["m_" + n], n) for n in WEIGHTS}
    v = {n: local(given["v_" + n], n) for n in WEIGHTS}
    xs, target = x[0], loss_target[0]
    me = 4 * lax.axis_index("x") + 2 * lax.axis_index("y") + lax.axis_index("c")

    small_shard = _pack_rows([w[n] for n in SMALL_SHARDED], SMALL_GATHER_ROWS)
    first = _all_gather_hbm([w["w_in"].astype(BF16), small_shard], name="gather_w_in")
    rest, first = lax.optimization_barrier(([w[n].astype(BF16) for n in MIXER + FFN], first))
    rest_blocks = _all_gather_async(rest, name="gather_rest", collective_id=1)
    W = {n: w[n] for n in SMALL if n not in SMALL_SHARDED}
    blocks = {"w_in": first[0], **dict(zip(MIXER + FFN, rest_blocks))}
    for n in ("w_branch_gla", "w_branch_sgu", "w_out", "w_ffn_out"):
        W[n] = blocks[n].reshape(-1, D_MODEL)
    W["w_ffn_in"] = blocks["w_ffn_in"]
    w_in_t = _join_blocks_call(blocks["w_in"], name="join_w_in")
    W["w_in_a"] = w_in_t[A_COLS[0]:A_COLS[1]]
    W["w_in_low"] = jnp.pad(w_in_t[LOW_COLS[0]:LOW_COLS[1]], ((0, LANES - RANK), (0, 0)))
    W["w_in_s"] = w_in_t[S_COLS[0]:S_COLS[1]]
    W["w_in_g"] = w_in_t[G_COLS[0]:G_COLS[1]]
    small_blocks = first[1]
    off = 0
    for n in SMALL_SHARDED:
        r, c = w[n].shape
        blk = small_blocks[:, off:off + r * c // LANES].reshape(N_DEV, r, c)
        W[n] = blk.transpose(1, 0, 2).reshape(r, N_DEV * c)
        off += _tile_rows(r * c)

    scatter_ids = iter((3, 4, 5))

    def scatter(names, parts):
        got = _scatter_blocks_async(parts, name="scatter_" + "_".join(names), collective_id=next(scatter_ids))
        return dict(zip(names, got))

    def finish(received):
        return {n: _adamw_reduce_call(r, w[n], m[n], v[n], name="adamw_" + n) for n, r in received.items()}

    loss_part, grad_x, grads, big_done = _local_step(xs, target, W, scatter, finish)

    small_part = jnp.concatenate([_pack_rows([grads[n] for n in SMALL], SMALL_GRAD_ROWS),
                                  jnp.broadcast_to(loss_part, (8, LANES))], axis=0)
    small_sum = _all_reduce_small(small_part, name="reduce_small")
    loss = small_sum[SMALL_GRAD_ROWS, 0]
    g_small = dict(zip(SMALL, _unpack_rows(small_sum, [SMALL_FULL[n] for n in SMALL])))
    for n in SMALL_SHARDED:
        c = w[n].shape[1]
        g_small[n] = lax.dynamic_slice_in_dim(g_small[n], me * c, c, axis=1)

    g_big, d_big, m_big, v_big = [{n: big_done[n][i] for n in BIG} for i in range(4)]
    small_shapes = [w[n].shape for n in SMALL]
    small_state = [_pack_rows([s[n] for n in SMALL], SMALL_STATE_ROWS) for s in (w, g_small, m, v)]
    small_out = _adamw_small_call(*small_state, name="adamw_small")
    d_small, m_small, v_small = [dict(zip(SMALL, _unpack_rows(o, small_shapes))) for o in small_out]

    def pick(big, sml):
        return [(big[n].T if n in COLUMN_SHARDED else big[n] if n in BIG else sml[n])[None] for n in WEIGHTS]

    return (loss, grad_x[None], *pick(g_big, g_small), *pick(d_big, d_small), *pick(m_big, m_small),
            *pick(v_big, v_small))
```

```python
import jax
import jax.numpy as jnp
from jax import lax
from jax.experimental import pallas as pl
from jax.experimental.pallas import tpu as pltpu
from jax.experimental.pallas import tpu_sc as plsc

F32 = jnp.float32
BF16 = jnp.bfloat16

D_MODEL = 1024
N_DEV = 8
CHUNK = 64
HEADS = 4
DK = 128
DV = 256
QK = HEADS * DK
GV = HEADS * DV
RANK = 16
GROUPS = 4
SBLOCK = 128
DG = 256
D_FF = 2816
FF_BLK = 704
EPS = 1e-6
Q_SCALE = DK ** -0.5
LANES = 128
VMEM_BIG = 48 * 1024 * 1024

D_IN = 7184
IN_BLK = 898
A_COLS = (0, 3072)
LOW_COLS = (3072, 3088)
S_COLS = (3088, 5136)
G_COLS = (5136, 7184)

ADAM_LR = 0.001
ADAM_B1 = 0.9
ADAM_B2 = 0.999
ADAM_EPS = 1e-08
ADAM_WD = 0.01
ADAM_STEP = 10

MESH = pl.DeviceIdType.MESH
ANY = pl.BlockSpec(memory_space=pl.ANY)
VMEM_SPEC = pl.BlockSpec(memory_space=pltpu.VMEM)


def _cp(sem=None, vmem=None):
    return pltpu.CompilerParams(dimension_semantics=sem, vmem_limit_bytes=vmem)


def _sig(x):
    return 0.5 * jnp.tanh(0.5 * x) + 0.5


def _gelu(x):
    c = 0.7978845608028654
    t = jnp.tanh(c * (x + 0.044715 * x * x * x))
    return 0.5 * x * (1.0 + t)


def _gelu_grad(x):
    c = 0.7978845608028654
    x2 = x * x
    t = jnp.tanh(c * (x + 0.044715 * x * x2))
    return 0.5 * (1.0 + t) + 0.5 * x * (1.0 - t * t) * c * (1.0 + 3.0 * 0.044715 * x2)


def _logsig(x):
    return jnp.minimum(x, 0.0) - jnp.log1p(jnp.exp(-jnp.abs(x)))


def _dot(a, b, dims):
    return lax.dot_general(a, b, (dims, ((), ())), preferred_element_type=F32)


NN = ((1,), (0,))
NT = ((1,), (1,))
TN = ((0,), (0,))


def _exact_mask_dot(mask_bf16, x):
    hi = x.astype(BF16)
    r1 = x - hi.astype(F32)
    mid = r1.astype(BF16)
    lo = (r1 - mid.astype(F32)).astype(BF16)
    return _dot(mask_bf16, hi, NN) + _dot(mask_bf16, mid, NN) + _dot(mask_bf16, lo, NN)


def _pick_tile(dim, target):
    if dim <= target:
        return dim
    best = None
    for t in range(LANES, int(1.4 * target) + 1, LANES):
        if dim % t == 0:
            best = t
    assert best is not None, (dim, target)
    return best


def _mm_call(a, b, *, name, grid, a_spec, b_spec, o_spec, out_shape, dims, acc_shape):
    nk = grid[2]

    def body(a_ref, b_ref, o_ref, *acc):
        part = _dot(a_ref[...], b_ref[...], dims)
        if nk == 1:
            o_ref[...] = part.astype(o_ref.dtype)
        else:
            acc_ref = acc[0]
            k = pl.program_id(2)

            @pl.when(k == 0)
            def _():
                acc_ref[...] = part

            @pl.when(k > 0)
            def _():
                acc_ref[...] += part

            @pl.when(k == nk - 1)
            def _():
                o_ref[...] = acc_ref[...].astype(o_ref.dtype)

    return pl.pallas_call(
        body, name=name, grid=grid, in_specs=[a_spec, b_spec], out_specs=o_spec, out_shape=out_shape,
        scratch_shapes=[] if nk == 1 else [pltpu.VMEM(acc_shape, F32)],
        compiler_params=_cp(("parallel", "parallel", "arbitrary"), VMEM_BIG),
    )(a, b)


def _mm(a, b, mode, out_dtype, *, name, tm=512, tn=1024, tk=1024):
    if mode == "nn":
        (M, K), (_, N) = a.shape, b.shape
    elif mode == "nt":
        (M, K), (N, _) = a.shape, b.shape
    else:
        (K, M), (_, N) = a.shape, b.shape
    tm, tn, tk = _pick_tile(M, tm), _pick_tile(N, tn), _pick_tile(K, tk)
    if mode == "nn":
        a_spec = pl.BlockSpec((tm, tk), lambda i, j, k: (i, k))
        b_spec = pl.BlockSpec((tk, tn), lambda i, j, k: (k, j))
        dims = NN
    elif mode == "nt":
        a_spec = pl.BlockSpec((tm, tk), lambda i, j, k: (i, k))
        b_spec = pl.BlockSpec((tn, tk), lambda i, j, k: (j, k))
        dims = NT
    else:
        a_spec = pl.BlockSpec((tk, tm), lambda i, j, k: (k, i))
        b_spec = pl.BlockSpec((tk, tn), lambda i, j, k: (k, j))
        dims = TN
    return _mm_call(a, b, name=name, grid=(M // tm, N // tn, K // tk), a_spec=a_spec, b_spec=b_spec,
                    o_spec=pl.BlockSpec((tm, tn), lambda i, j, k: (i, j)),
                    out_shape=jax.ShapeDtypeStruct((M, N), out_dtype), dims=dims, acc_shape=(tm, tn))


ROW_TILE = 512
SUB_ROWS = 256


def _sub_tiles(T):
    return [pl.ds(r0, min(SUB_ROWS, T)) for r0 in range(0, T, SUB_ROWS)]


def _rt(T, W, c=0):
    return pl.BlockSpec((T, W), lambda i: (i, c))


def _rt3(nb, T, W):
    return pl.BlockSpec((nb, T, W), lambda i: (0, i, 0))


def _res(shape):
    nd = len(shape)
    return pl.BlockSpec(tuple(shape), lambda i: (0,) * nd, pipeline_mode=pl.Buffered(1))


def _acc(shape):
    nd = len(shape)
    return pl.BlockSpec(tuple(shape), lambda i: (0,) * nd, pipeline_mode=pl.Buffered(1))


def _nbytes(shape, dtype):
    n = jnp.dtype(dtype).itemsize
    for s in shape:
        n *= s
    return n


def _vmem_limit(tiles, resident, temps=16 * 1024 * 1024):
    need = 2 * sum(_nbytes(s, d) for s, d in tiles) + sum(_nbytes(s, d) for s, d in resident) + temps
    return min(need, 60 * 1024 * 1024)


def _tok_call(body, *, name, S, T, ins, outs, semantics="parallel"):
    tiles = [(spec.block_shape, a.dtype) for a, spec, kind in ins + outs if kind == "tile"]
    resident = [(a.shape, a.dtype) for a, spec, kind in ins + outs if kind == "res"]
    return pl.pallas_call(
        body, name=name, grid=(S // T,),
        in_specs=[spec for _, spec, _ in ins], out_specs=[spec for _, spec, _ in outs],
        out_shape=[jax.ShapeDtypeStruct(a.shape, a.dtype) for a, _, _ in outs],
        compiler_params=_cp((semantics,), _vmem_limit(tiles, resident)),
    )(*[a for a, _, _ in ins])


def _tile(a, spec):
    return (a, spec, "tile")


def _whole(a):
    return (a, _res(a.shape), "res")


def _out_tile(shape, dtype, spec):
    return (jax.ShapeDtypeStruct(shape, dtype), spec, "tile")


def _out_acc(shape, dtype=F32):
    return (jax.ShapeDtypeStruct(shape, dtype), _acc(shape), "res")


def _rms_stats(x):
    r = lax.rsqrt(jnp.mean(x * x, axis=-1, keepdims=True) + EPS)
    return r, x * r


def _rms_bwd(xh, r, g, dy):
    dxh = dy * g
    dx = r * (dxh - xh * jnp.mean(dxh * xh, axis=-1, keepdims=True))
    dg = jnp.sum(dy * xh, axis=0, keepdims=True)
    return dx, dg


def _accum(ref, val):
    @pl.when(pl.program_id(0) == 0)
    def _():
        ref[...] = val

    @pl.when(pl.program_id(0) > 0)
    def _():
        ref[...] += val


def _dot_rows_t(a, w_ref, row0, o_ref, chunk=1024):
    n = o_ref.shape[1]
    for n0 in range(0, n, chunk):
        n1 = min(n, n0 + chunk)
        o_ref[:, n0:n1] = _dot(a, w_ref[row0 + n0:row0 + n1, :], NT).astype(o_ref.dtype)


def _in_proj_call(x, g1, w_in_t, *, name, T=ROW_TILE):
    S = x.shape[0]

    def body(x_ref, g_ref, w_ref, a_ref, pa_ref, al_ref, ps_ref, pg_ref):
        _, xh = _rms_stats(x_ref[...])
        a = (xh * g_ref[...]).astype(BF16)
        a_ref[...] = a
        _dot_rows_t(a, w_ref, A_COLS[0], pa_ref)
        _dot_rows_t(a, w_ref, LOW_COLS[0], al_ref)
        _dot_rows_t(a, w_ref, S_COLS[0], ps_ref)
        _dot_rows_t(a, w_ref, G_COLS[0], pg_ref)

    widths = (D_MODEL, A_COLS[1] - A_COLS[0], LANES, S_COLS[1] - S_COLS[0], G_COLS[1] - G_COLS[0])
    return _tok_call(
        body, name=name, S=S, T=T,
        ins=[_tile(x, _rt(T, D_MODEL)), _whole(g1), _whole(w_in_t)],
        outs=[_out_tile((S, w), BF16, _rt(T, w)) for w in widths])


def _mixer_tail_call(y_gla, y_sgu, pg, x, w_bg, w_bs, w_out, g2, g3, *, name, T=ROW_TILE):
    S = x.shape[0]

    def body(yg_ref, ys_ref, pg_ref, x_ref, wbg_ref, wbs_ref, wo_ref, g2_ref, g3_ref,
             t1_ref, t2_ref, mg_ref, mix_ref, x1_ref, h_ref):
        for rows in _sub_tiles(T):
            t1 = _dot(yg_ref[rows, :], wbg_ref[...], NN)
            t2 = _dot(ys_ref[rows, :], wbs_ref[...], NN)
            t1_ref[rows, :] = t1.astype(BF16)
            t2_ref[rows, :] = t2.astype(BF16)
            sg = _sig(pg_ref[rows, pl.ds(0, D_MODEL)].astype(F32))
            ss = _sig(pg_ref[rows, pl.ds(D_MODEL, D_MODEL)].astype(F32))
            merged = (sg * t1 + ss * t2).astype(BF16)
            mg_ref[rows, :] = merged
            mix = _dot(merged, wo_ref[...], NN)
            mix_ref[rows, :] = mix
            _, mh = _rms_stats(mix)
            x1 = x_ref[rows, :] + mh * g2_ref[...]
            x1_ref[rows, :] = x1
            _, xh = _rms_stats(x1)
            h_ref[rows, :] = (xh * g3_ref[...]).astype(BF16)

    row = _rt(T, D_MODEL)
    b16 = lambda: _out_tile((S, D_MODEL), BF16, row)
    f32 = lambda: _out_tile((S, D_MODEL), F32, row)
    return _tok_call(
        body, name=name, S=S, T=T,
        ins=[_tile(y_gla, row), _tile(y_sgu, row), _tile(pg, _rt(T, 2 * D_MODEL)), _tile(x, row),
             _whole(w_bg), _whole(w_bs), _whole(w_out), _whole(g2), _whole(g3)],
        outs=[b16(), b16(), b16(), f32(), f32(), b16()])


def _ffn_in_call(h, wfi, *, name, T=ROW_TILE):
    S = h.shape[0]

    def body(h_ref, w_ref, gu_ref, z_ref):
        hv = h_ref[...]
        for d in range(4):
            gate = _dot(hv, w_ref[d], NT)
            up = _dot(hv, w_ref[d + 4], NT)
            gu_ref[d] = gate.astype(BF16)
            gu_ref[d + 4] = up.astype(BF16)
            z_ref[d] = (gate * _sig(gate) * up).astype(BF16)

    return _tok_call(
        body, name=name, S=S, T=T,
        ins=[_tile(h, _rt(T, D_MODEL)), _whole(wfi)],
        outs=[_out_tile((N_DEV, S, FF_BLK), BF16, _rt3(N_DEV, T, FF_BLK)),
              _out_tile((4, S, FF_BLK), BF16, _rt3(4, T, FF_BLK))])


def _ffn_out_loss_call(z, wfo, x1, target, g4, *, name, T=ROW_TILE):
    S = x1.shape[0]

    def body(z_ref, w_ref, x1_ref, t_ref, g4_ref, loss_ref, dx2_ref, dy_ref, dg4_ref):
        loss = jnp.zeros((1, 1), F32)
        dg4 = jnp.zeros((1, D_MODEL), F32)
        for rows in _sub_tiles(T):
            y = _dot(z_ref[0, rows, :], w_ref[0], NN)
            for d in range(1, 4):
                y = y + _dot(z_ref[d, rows, :], w_ref[d], NN)
            r, yh = _rms_stats(y)
            diff = x1_ref[rows, :] + yh * g4_ref[...] - t_ref[rows, :]
            loss = loss + 0.5 * jnp.sum(jnp.mean(diff * diff, axis=-1, keepdims=True), axis=0, keepdims=True)
            dx2 = diff * (1.0 / D_MODEL)
            dx2_ref[rows, :] = dx2
            dy, dg = _rms_bwd(yh, r, g4_ref[...], dx2)
            dy_ref[rows, :] = dy.astype(BF16)
            dg4 = dg4 + dg
        _accum(loss_ref, jnp.broadcast_to(loss, (1, LANES)))
        _accum(dg4_ref, dg4)

    row = _rt(T, D_MODEL)
    return _tok_call(
        body, name=name, S=S, T=T, semantics="arbitrary",
        ins=[_tile(z, _rt3(4, T, FF_BLK)), _whole(wfo), _tile(x1, row), _tile(target, row), _whole(g4)],
        outs=[_out_acc((1, LANES)), _out_tile((S, D_MODEL), F32, row), _out_tile((S, D_MODEL), BF16, row),
              _out_acc((1, D_MODEL))])


def _ffn_out_bwd_call(dy, wfo, gu, *, name, T=ROW_TILE):
    S = dy.shape[0]

    def body(dy_ref, w_ref, gu_ref, dgu_ref):
        dyv = dy_ref[...]
        for d in range(4):
            dz = _dot(dyv, w_ref[d], NT)
            gt = gu_ref[d].astype(F32)
            up = gu_ref[d + 4].astype(F32)
            s = _sig(gt)
            dgu_ref[d] = (dz * up * s * (1.0 + gt * (1.0 - s))).astype(BF16)
            dgu_ref[d + 4] = (dz * gt * s).astype(BF16)

    blocks = _rt3(N_DEV, T, FF_BLK)
    return _tok_call(
        body, name=name, S=S, T=T,
        ins=[_tile(dy, _rt(T, D_MODEL)), _whole(wfo), _tile(gu, blocks)],
        outs=[_out_tile((N_DEV, S, FF_BLK), BF16, blocks)])[0]


def _ffn_in_bwd_call(dgu, wfi, dx2, x1, mix, g3, g2, *, name, T=ROW_TILE):
    S = x1.shape[0]

    def body(dgu_ref, w_ref, dx2_ref, x1_ref, mix_ref, g3_ref, g2_ref, dx1_ref, dmix_ref, dg3_ref, dg2_ref):
        dg3 = jnp.zeros((1, D_MODEL), F32)
        dg2 = jnp.zeros((1, D_MODEL), F32)
        for rows in _sub_tiles(T):
            dh = _dot(dgu_ref[0, rows, :], w_ref[0], NN)
            for d in range(1, N_DEV):
                dh = dh + _dot(dgu_ref[d, rows, :], w_ref[d], NN)
            r3, xh = _rms_stats(x1_ref[rows, :])
            d3, g = _rms_bwd(xh, r3, g3_ref[...], dh)
            dg3 = dg3 + g
            dx1 = dx2_ref[rows, :] + d3
            dx1_ref[rows, :] = dx1
            r2, mh = _rms_stats(mix_ref[rows, :])
            dmix, g = _rms_bwd(mh, r2, g2_ref[...], dx1)
            dg2 = dg2 + g
            dmix_ref[rows, :] = dmix.astype(BF16)
        _accum(dg3_ref, dg3)
        _accum(dg2_ref, dg2)

    row = _rt(T, D_MODEL)
    return _tok_call(
        body, name=name, S=S, T=T, semantics="arbitrary",
        ins=[_tile(dgu, _rt3(N_DEV, T, FF_BLK)), _whole(wfi), _tile(dx2, row), _tile(x1, row), _tile(mix, row),
             _whole(g3), _whole(g2)],
        outs=[_out_tile((S, D_MODEL), F32, row), _out_tile((S, D_MODEL), BF16, row),
              _out_acc((1, D_MODEL)), _out_acc((1, D_MODEL))])


def _mixer_bwd_call(dmix, t1, t2, pg, w_out, w_bg, w_bs, *, name, T=ROW_TILE):
    S = dmix.shape[0]

    def body(dmix_ref, t1_ref, t2_ref, pg_ref, wo_ref, wbg_ref, wbs_ref, dt1_ref, dt2_ref, dpg_ref, dyg_ref, dys_ref):
        gg, gs = pl.ds(0, D_MODEL), pl.ds(D_MODEL, D_MODEL)
        for rows in _sub_tiles(T):
            dm = _dot(dmix_ref[rows, :], wo_ref[...], NT)
            sg = _sig(pg_ref[rows, gg].astype(F32))
            ss = _sig(pg_ref[rows, gs].astype(F32))
            dt1 = (dm * sg).astype(BF16)
            dt2 = (dm * ss).astype(BF16)
            dt1_ref[rows, :] = dt1
            dt2_ref[rows, :] = dt2
            dpg_ref[rows, gg] = (dm * t1_ref[rows, :].astype(F32) * sg * (1.0 - sg)).astype(BF16)
            dpg_ref[rows, gs] = (dm * t2_ref[rows, :].astype(F32) * ss * (1.0 - ss)).astype(BF16)
            dyg_ref[rows, :] = _dot(dt1, wbg_ref[...], NT).astype(BF16)
            dys_ref[rows, :] = _dot(dt2, wbs_ref[...], NT).astype(BF16)

    row = _rt(T, D_MODEL)
    wide = _rt(T, 2 * D_MODEL)
    b16 = lambda: _out_tile((S, D_MODEL), BF16, row)
    return _tok_call(
        body, name=name, S=S, T=T,
        ins=[_tile(dmix, row), _tile(t1, row), _tile(t2, row), _tile(pg, wide), _whole(w_out), _whole(w_bg), _whole(w_bs)],
        outs=[b16(), b16(), _out_tile((S, 2 * D_MODEL), BF16, wide), b16(), b16()])


def _in_proj_bwd_call(dpa, dlow, dps, dpg, w_in_t, x, dx1, g1, *, name, T=ROW_TILE):
    S = x.shape[0]

    def body(dpa_ref, dl_ref, dps_ref, dpg_ref, w_ref, x_ref, dx1_ref, g1_ref, gx_ref, dg1_ref):
        da = (_dot(dpa_ref[...], w_ref[A_COLS[0]:A_COLS[1], :], NN)
              + _dot(dl_ref[...], w_ref[LOW_COLS[0]:LOW_COLS[0] + LANES, :], NN)
              + _dot(dps_ref[...], w_ref[S_COLS[0]:S_COLS[1], :], NN)
              + _dot(dpg_ref[...], w_ref[G_COLS[0]:G_COLS[1], :], NN))
        r, xh = _rms_stats(x_ref[...])
        dxa, dg = _rms_bwd(xh, r, g1_ref[...], da)
        gx_ref[...] = dx1_ref[...] + dxa
        _accum(dg1_ref, dg)

    row = _rt(T, D_MODEL)
    return _tok_call(
        body, name=name, S=S, T=T, semantics="arbitrary",
        ins=[_tile(dpa, _rt(T, dpa.shape[1])), _tile(dlow, _rt(T, dlow.shape[1])), _tile(dps, _rt(T, dps.shape[1])),
             _tile(dpg, _rt(T, dpg.shape[1])), _whole(w_in_t), _tile(x, row), _tile(dx1, row), _whole(g1)],
        outs=[_out_tile((S, D_MODEL), F32, row), _out_acc((1, D_MODEL))])


TOKEN_TILE = 512


def _weight_grads_call(arrays, pairs, *, name, tk=TOKEN_TILE):
    S = arrays[0].shape[-2]
    tk = min(tk, S)
    n_in = len(arrays)

    def out_shape(i, j):
        a, b = arrays[i], arrays[j]
        if a.ndim == 3:
            return (a.shape[0], a.shape[2], b.shape[1])
        if b.ndim == 3:
            return (b.shape[0], a.shape[1], b.shape[2])
        return (a.shape[1], b.shape[1])

    shapes = [out_shape(i, j) for i, j in pairs]

    def body(*refs):
        ins, outs, accs = refs[:n_in], refs[n_in:n_in + len(pairs)], refs[n_in + len(pairs):]
        k = pl.program_id(0)

        @pl.when(k == 0)
        def _():
            for acc in accs:
                acc[...] = jnp.zeros_like(acc)

        for (i, j), acc in zip(pairs, accs):
            a_ref, b_ref = ins[i], ins[j]
            if len(a_ref.shape) == 3:
                for n in range(a_ref.shape[0]):
                    acc[n] += _dot(a_ref[n], b_ref[...], TN)
            elif len(b_ref.shape) == 3:
                a = a_ref[...]
                for n in range(b_ref.shape[0]):
                    acc[n] += _dot(a, b_ref[n], TN)
            else:
                b = b_ref[...]
                for m0 in range(0, a_ref.shape[1], 1024):
                    m1 = min(a_ref.shape[1], m0 + 1024)
                    acc[m0:m1, :] += _dot(a_ref[:, m0:m1], b, TN)

        @pl.when(k == S // tk - 1)
        def _():
            for out, acc in zip(outs, accs):
                out[...] = acc[...].astype(out.dtype)

    def in_spec(a):
        if a.ndim == 3:
            return pl.BlockSpec((a.shape[0], tk, a.shape[2]), lambda k: (0, k, 0))
        return pl.BlockSpec((tk, a.shape[1]), lambda k: (k, 0))

    tiles = [(in_spec(a).block_shape, a.dtype) for a in arrays]
    resident = [(s, F32) for s in shapes] + [(s, BF16) for s in shapes]
    return pl.pallas_call(
        body, name=name, grid=(S // tk,),
        in_specs=[in_spec(a) for a in arrays],
        out_specs=[_acc(s) for s in shapes],
        out_shape=[jax.ShapeDtypeStruct(s, BF16) for s in shapes],
        scratch_shapes=[pltpu.VMEM(s, F32) for s in shapes],
        compiler_params=_cp(("arbitrary",), _vmem_limit(tiles, resident)),
    )(*arrays)


def _ffn_in_grad_call(h, dgu, *, name, tk=TOKEN_TILE):
    S = h.shape[0]
    tk = min(tk, S)
    nb = 4
    shape = (nb, FF_BLK, D_MODEL)

    def body(h_ref, dgu_ref, out_ref, acc):
        k = pl.program_id(1)

        @pl.when(k == 0)
        def _():
            acc[...] = jnp.zeros_like(acc)

        hv = h_ref[...]
        for n in range(nb):
            acc[n] += _dot(dgu_ref[n], hv, TN)

        @pl.when(k == S // tk - 1)
        def _():
            out_ref[...] = acc[...].astype(out_ref.dtype)

    tiles = [((tk, D_MODEL), BF16), ((nb, tk, FF_BLK), BF16), (shape, BF16)]
    return pl.pallas_call(
        body, name=name, grid=(N_DEV // nb, S // tk),
        in_specs=[pl.BlockSpec((tk, D_MODEL), lambda g, k: (k, 0)),
                  pl.BlockSpec((nb, tk, FF_BLK), lambda g, k: (g, k, 0))],
        out_specs=pl.BlockSpec(shape, lambda g, k: (g, 0, 0)),
        out_shape=jax.ShapeDtypeStruct((N_DEV, FF_BLK, D_MODEL), BF16),
        scratch_shapes=[pltpu.VMEM(shape, F32)],
        compiler_params=_cp(("parallel", "arbitrary"), _vmem_limit(tiles, [(shape, F32)])),
    )(h, dgu)


GLA_TILE = 256
Q_OFF, K_OFF, V_OFF, R_OFF = 0, QK, 2 * QK, 2 * QK + GV


def _tri(lower):
    r = lax.broadcasted_iota(jnp.int32, (CHUNK, CHUNK), 0)
    c = lax.broadcasted_iota(jnp.int32, (CHUNK, CHUNK), 1)
    return jnp.where((r >= c) if lower else (c >= r), 1.0, 0.0).astype(BF16)


def _gla_fwd_call(pa, alow, wg, bgate, gnorm, *, name):
    S = pa.shape[0]
    Tg = min(GLA_TILE, S)
    cb = Tg // CHUNK

    def body(pa_ref, al_ref, wg_ref, bg_ref, gn_ref, y_ref, st_ref, state):
        @pl.when(pl.program_id(0) == 0)
        def _():
            state[...] = jnp.zeros_like(state)

        logit = _dot(al_ref[...], wg_ref[...], NN) + bg_ref[...]
        ls = _logsig(logit) * (1.0 / 16.0)
        tri = _tri(True)
        for c in range(cb):
            rows = pl.ds(c * CHUNK, CHUNK)
            cum = _exact_mask_dot(tri, ls[c * CHUNK:(c + 1) * CHUNK])
            tot = cum[CHUNK - 1:CHUNK]
            kd = (pa_ref[rows, pl.ds(K_OFF, QK)].astype(F32) * jnp.exp(tot - cum)).astype(BF16)
            decay = jnp.exp(tot)
            for h in range(HEADS):
                lanes = slice(h * DK, (h + 1) * DK)
                new = state[h] * decay[:, lanes] + _dot(pa_ref[rows, pl.ds(V_OFF + h * DV, DV)], kd[:, lanes], TN)
                state[h] = new
                st_ref[c, h] = new
        for c in range(cb):
            rows = pl.ds(c * CHUNK, CHUNK)
            for h in range(HEADS):
                qs = (pa_ref[rows, pl.ds(Q_OFF + h * DK, DK)].astype(F32) * Q_SCALE).astype(BF16)
                o = _dot(qs, st_ref[c, h].astype(BF16), NT)
                rs = lax.rsqrt(jnp.mean(o * o, axis=-1, keepdims=True) + EPS)
                rr = pa_ref[rows, pl.ds(R_OFF + h * DV, DV)].astype(F32)
                y_ref[rows, pl.ds(h * DV, DV)] = (o * rs * gn_ref[h] * (rr * _sig(rr))).astype(BF16)

    return pl.pallas_call(
        body, name=name, grid=(S // Tg,),
        in_specs=[
            pl.BlockSpec((Tg, 2 * QK + 2 * GV), lambda t: (t, 0)),
            pl.BlockSpec((Tg, LANES), lambda t: (t, 0)),
            pl.BlockSpec((LANES, QK), lambda t: (0, 0)),
            pl.BlockSpec((1, QK), lambda t: (0, 0)),
            pl.BlockSpec((HEADS, 1, DV), lambda t: (0, 0, 0)),
        ],
        out_specs=[
            pl.BlockSpec((Tg, GV), lambda t: (t, 0)),
            pl.BlockSpec((cb, HEADS, DV, DK), lambda t: (t, 0, 0, 0)),
        ],
        out_shape=[jax.ShapeDtypeStruct((S, GV), BF16),
                   jax.ShapeDtypeStruct((S // CHUNK, HEADS, DV, DK), F32)],
        scratch_shapes=[pltpu.VMEM((HEADS, DV, DK), F32)],
        compiler_params=_cp(("arbitrary",), VMEM_BIG),
    )(pa, alow, wg, bgate, gnorm)


def _gla_bwd_call(pa, alow, wg, bgate, gnorm, states, dy, *, name):
    S = pa.shape[0]
    Tg = min(GLA_TILE, S)
    cb = Tg // CHUNK
    nt = S // Tg

    def rev(t):
        return nt - 1 - t

    def body(pa_ref, al_ref, wg_ref, bg_ref, gn_ref, st_ref, prev_ref, dy_ref,
             dpa_ref, dl_ref, dgn_ref, dbg_ref, carry, pbuf):
        t = pl.program_id(0)

        @pl.when(t == 0)
        def _():
            carry[...] = jnp.zeros_like(carry)
            dgn_ref[...] = jnp.zeros_like(dgn_ref)
            dbg_ref[...] = jnp.zeros_like(dbg_ref)

        logit = _dot(al_ref[...], wg_ref[...], NN) + bg_ref[...]
        ls = _logsig(logit) * (1.0 / 16.0)
        sneg = 1.0 / (1.0 + jnp.exp(logit))
        tri = _tri(True)
        upper = _tri(False)
        first_tile = rev(t) == 0
        heads = range(HEADS)

        w, decay, kd = [], [], []
        for c in range(cb):
            rows = pl.ds(c * CHUNK, CHUNK)
            cum = _exact_mask_dot(tri, ls[c * CHUNK:(c + 1) * CHUNK])
            tot = cum[CHUNK - 1:CHUNK]
            w.append(jnp.exp(tot - cum))
            decay.append(jnp.exp(tot))
            kd.append(pa_ref[rows, pl.ds(K_OFF, QK)].astype(F32) * w[c])

        dgn = [jnp.zeros((1, DV), F32) for _ in heads]
        for c in range(cb):
            rows = pl.ds(c * CHUNK, CHUNK)
            for h in heads:
                gn = gn_ref[h]
                qs = (pa_ref[rows, pl.ds(Q_OFF + h * DK, DK)].astype(F32) * Q_SCALE).astype(BF16)
                st16 = st_ref[c, h].astype(BF16)
                o = _dot(qs, st16, NT)
                rs = lax.rsqrt(jnp.mean(o * o, axis=-1, keepdims=True) + EPS)
                oh = o * rs
                rr = pa_ref[rows, pl.ds(R_OFF + h * DV, DV)].astype(F32)
                sr = _sig(rr)
                dyv = dy_ref[rows, pl.ds(h * DV, DV)].astype(F32)
                dpa_ref[rows, pl.ds(R_OFF + h * DV, DV)] = (
                    dyv * oh * gn * sr * (1.0 + rr * (1.0 - sr))).astype(BF16)
                don = dyv * (rr * sr)
                dgn[h] = dgn[h] + jnp.sum(don * oh, axis=0, keepdims=True)
                doh = don * gn
                do16 = (rs * (doh - oh * jnp.mean(doh * oh, axis=-1, keepdims=True))).astype(BF16)
                dpa_ref[rows, pl.ds(Q_OFF + h * DK, DK)] = (_dot(do16, st16, NN) * Q_SCALE).astype(BF16)
                pbuf[c, h] = _dot(do16, qs, TN)
        for h in heads:
            dgn_ref[h] += dgn[h]

        dkd = [[None] * HEADS for _ in range(cb)]
        ddecay = [[None] * HEADS for _ in range(cb)]
        for c in reversed(range(cb)):
            rows = pl.ds(c * CHUNK, CHUNK)
            for h in heads:
                lanes = slice(h * DK, (h + 1) * DK)
                gt = pbuf[c, h] + carry[h]
                gt16 = gt.astype(BF16)
                dkd[c][h] = _dot(pa_ref[rows, pl.ds(V_OFF + h * DV, DV)], gt16, NN)
                dpa_ref[rows, pl.ds(V_OFF + h * DV, DV)] = _dot(kd[c][:, lanes].astype(BF16), gt16, NT).astype(BF16)
                if c > 0:
                    st_prev = st_ref[c - 1, h]
                else:
                    st_prev = jnp.where(first_tile, 0.0, prev_ref[0, h])
                ddecay[c][h] = jnp.sum(gt * st_prev, axis=0, keepdims=True)
                carry[h] = gt * decay[c][:, lanes]

        dbg = jnp.zeros((1, QK), F32)
        for c in range(cb):
            rows = pl.ds(c * CHUNK, CHUNK)
            dkd_c = jnp.concatenate(dkd[c], axis=1)
            dpa_ref[rows, pl.ds(K_OFF, QK)] = (dkd_c * w[c]).astype(BF16)
            e = dkd_c * kd[c]
            dtot = jnp.sum(e, axis=0, keepdims=True) + jnp.concatenate(ddecay[c], axis=1) * decay[c]
            dls = dtot - _exact_mask_dot(upper, e)
            dlogit = dls * (1.0 / 16.0) * sneg[c * CHUNK:(c + 1) * CHUNK]
            dl_ref[rows, :] = dlogit.astype(BF16)
            dbg = dbg + jnp.sum(dlogit, axis=0, keepdims=True)
        dbg_ref[...] += dbg

    return pl.pallas_call(
        body, name=name, grid=(nt,),
        in_specs=[
            pl.BlockSpec((Tg, 2 * QK + 2 * GV), lambda t: (rev(t), 0)),
            pl.BlockSpec((Tg, LANES), lambda t: (rev(t), 0)),
            pl.BlockSpec((LANES, QK), lambda t: (0, 0)),
            pl.BlockSpec((1, QK), lambda t: (0, 0)),
            pl.BlockSpec((HEADS, 1, DV), lambda t: (0, 0, 0)),
            pl.BlockSpec((cb, HEADS, DV, DK), lambda t: (rev(t), 0, 0, 0)),
            pl.BlockSpec((1, HEADS, DV, DK), lambda t: (jnp.maximum(rev(t) * cb - 1, 0), 0, 0, 0)),
            pl.BlockSpec((Tg, GV), lambda t: (rev(t), 0)),
        ],
        out_specs=[
            pl.BlockSpec((Tg, 2 * QK + 2 * GV), lambda t: (rev(t), 0)),
            pl.BlockSpec((Tg, QK), lambda t: (rev(t), 0)),
            pl.BlockSpec((HEADS, 1, DV), lambda t: (0, 0, 0)),
            pl.BlockSpec((1, QK), lambda t: (0, 0)),
        ],
        out_shape=[jax.ShapeDtypeStruct((S, 2 * QK + 2 * GV), BF16), jax.ShapeDtypeStruct((S, QK), BF16),
                   jax.ShapeDtypeStruct((HEADS, 1, DV), F32), jax.ShapeDtypeStruct((1, QK), F32)],
        scratch_shapes=[pltpu.VMEM((HEADS, DV, DK), F32), pltpu.VMEM((cb, HEADS, DV, DK), F32)],
        compiler_params=_cp(("arbitrary",), VMEM_BIG),
    )(pa, alow, wg, bgate, gnorm, states, states, dy)


SGU_TILE = 256


def _sgu_mask():
    r = lax.broadcasted_iota(jnp.int32, (SBLOCK, SBLOCK), 0)
    c = lax.broadcasted_iota(jnp.int32, (SBLOCK, SBLOCK), 1)
    return (c < CHUNK) | (r >= CHUNK)


def _ln_stats(vf):
    mu = jnp.mean(vf, axis=-1, keepdims=True)
    xc = vf - mu
    rs = lax.rsqrt(jnp.mean(xc * xc, axis=-1, keepdims=True) + EPS)
    return rs, xc * rs


def _sgu_fwd_call(ps, ln_g, ln_b, w_sp, b_sp, *, name):
    S = ps.shape[0]
    Ts = min(SGU_TILE, S)

    def body(ps_ref, lg_ref, lb_ref, w_ref, b_ref, y_ref):
        mask = _sgu_mask()
        for g in range(GROUPS):
            wm = jnp.where(mask, w_ref[g], 0.0).astype(BF16)
            for p in range(Ts // SBLOCK):
                rows = pl.ds(p * SBLOCK, SBLOCK)
                u = _gelu(ps_ref[rows, pl.ds(g * DG, DG)].astype(F32))
                _, xh = _ln_stats(_gelu(ps_ref[rows, pl.ds(D_MODEL + g * DG, DG)].astype(F32)))
                vn = xh * lg_ref[g] + lb_ref[g]
                mixed = _dot(wm, vn.astype(BF16), NN) + b_ref[g]
                y_ref[rows, pl.ds(g * DG, DG)] = (u * mixed).astype(BF16)

    full3 = lambda a, b, c: pl.BlockSpec((a, b, c), lambda t: (0, 0, 0))
    return pl.pallas_call(
        body, name=name, grid=(S // Ts,),
        in_specs=[pl.BlockSpec((Ts, 2 * D_MODEL), lambda t: (t, 0)),
                  full3(GROUPS, 1, DG), full3(GROUPS, 1, DG), full3(GROUPS, SBLOCK, SBLOCK), full3(GROUPS, SBLOCK, 1)],
        out_specs=pl.BlockSpec((Ts, D_MODEL), lambda t: (t, 0)),
        out_shape=jax.ShapeDtypeStruct((S, D_MODEL), BF16),
        compiler_params=_cp(("parallel",)),
    )(ps, ln_g, ln_b, w_sp, b_sp)


def _sgu_bwd_call(ps, ln_g, ln_b, w_sp, b_sp, dy, *, name):
    S = ps.shape[0]
    Ts = min(SGU_TILE, S)

    def body(ps_ref, lg_ref, lb_ref, w_ref, b_ref, dy_ref, ds_ref, dlg_ref, dlb_ref, dw_ref, db_ref):
        @pl.when(pl.program_id(0) == 0)
        def _():
            dlg_ref[...] = jnp.zeros_like(dlg_ref)
            dlb_ref[...] = jnp.zeros_like(dlb_ref)
            dw_ref[...] = jnp.zeros_like(dw_ref)
            db_ref[...] = jnp.zeros_like(db_ref)

        mask = _sgu_mask()
        for g in range(GROUPS):
            wm = jnp.where(mask, w_ref[g], 0.0).astype(BF16)
            lg = lg_ref[g]
            for p in range(Ts // SBLOCK):
                rows = pl.ds(p * SBLOCK, SBLOCK)
                su = ps_ref[rows, pl.ds(g * DG, DG)].astype(F32)
                sv = ps_ref[rows, pl.ds(D_MODEL + g * DG, DG)].astype(F32)
                u = _gelu(su)
                rs, xh = _ln_stats(_gelu(sv))
                vn16 = (xh * lg + lb_ref[g]).astype(BF16)
                mixed = _dot(wm, vn16, NN) + b_ref[g]
                dyv = dy_ref[rows, pl.ds(g * DG, DG)].astype(F32)
                ds_ref[rows, pl.ds(g * DG, DG)] = (dyv * mixed * _gelu_grad(su)).astype(BF16)
                dmix = dyv * u
                dmix16 = dmix.astype(BF16)
                db_ref[g] += jnp.sum(dmix, axis=-1, keepdims=True)
                dw_ref[g] += jnp.where(mask, _dot(dmix16, vn16, NT), 0.0)
                dvn = _dot(wm, dmix16, TN)
                dlg_ref[g] += jnp.sum(dvn * xh, axis=0, keepdims=True)
                dlb_ref[g] += jnp.sum(dvn, axis=0, keepdims=True)
                dxh = dvn * lg
                dvf = rs * (dxh - jnp.mean(dxh, axis=-1, keepdims=True)
                            - xh * jnp.mean(dxh * xh, axis=-1, keepdims=True))
                ds_ref[rows, pl.ds(D_MODEL + g * DG, DG)] = (dvf * _gelu_grad(sv)).astype(BF16)

    full3 = lambda a, b, c: pl.BlockSpec((a, b, c), lambda t: (0, 0, 0))
    return pl.pallas_call(
        body, name=name, grid=(S // Ts,),
        in_specs=[pl.BlockSpec((Ts, 2 * D_MODEL), lambda t: (t, 0)),
                  full3(GROUPS, 1, DG), full3(GROUPS, 1, DG), full3(GROUPS, SBLOCK, SBLOCK), full3(GROUPS, SBLOCK, 1),
                  pl.BlockSpec((Ts, D_MODEL), lambda t: (t, 0))],
        out_specs=[pl.BlockSpec((Ts, 2 * D_MODEL), lambda t: (t, 0)),
                   full3(GROUPS, 1, DG), full3(GROUPS, 1, DG), full3(GROUPS, SBLOCK, SBLOCK), full3(GROUPS, SBLOCK, 1)],
        out_shape=[jax.ShapeDtypeStruct((S, 2 * D_MODEL), BF16),
                   jax.ShapeDtypeStruct((GROUPS, 1, DG), F32), jax.ShapeDtypeStruct((GROUPS, 1, DG), F32),
                   jax.ShapeDtypeStruct((GROUPS, SBLOCK, SBLOCK), F32),
                   jax.ShapeDtypeStruct((GROUPS, SBLOCK, 1), F32)],
        compiler_params=_cp(("arbitrary",)),
    )(ps, ln_g, ln_b, w_sp, b_sp, dy)


def _position():
    return lax.axis_index("x"), lax.axis_index("y"), lax.axis_index("c")


def _gather_copies(srcs, dsts, send_sems, recv_sems, local_sems):
    x, y, c = _position()
    me, sibling = (x, y, c), (x, y, 1 - c)
    chips = [(1 - x, y), (x, 1 - y), (1 - x, 1 - y)]
    n = len(srcs)

    def slab(a, block):
        px, py, pc = block
        return dsts[a].at[4 * px + 2 * py + pc]

    def copy(a, k, block, to, src=None):
        return pltpu.make_async_remote_copy(
            src_ref=slab(a, block) if src is None else src, dst_ref=slab(a, block),
            send_sem=send_sems.at[7 * a + k], recv_sem=recv_sems.at[7 * a + k], device_id=to, device_id_type=MESH)

    mine = [pltpu.make_async_copy(srcs[a], slab(a, me), local_sems.at[a]) for a in range(n)]
    for cp in mine:
        cp.start()
    first = []
    for a in range(n):
        first.append(copy(a, 0, me, sibling, src=srcs[a]))
        first += [copy(a, 1 + j, me, (*chip, c), src=srcs[a]) for j, chip in enumerate(chips)]
    for cp in first:
        cp.start()
    passed = []
    for j, chip in enumerate(chips):
        for a in range(n):
            copy(a, 1 + j, (*chip, c), me).wait_recv()
            cp = copy(a, 4 + j, (*chip, c), sibling)
            cp.start()
            passed.append(cp)
    for a in range(n):
        copy(a, 0, sibling, me).wait_recv()
        for j, chip in enumerate(chips):
            copy(a, 4 + j, (*chip, 1 - c), me).wait_recv()
    for cp in first + passed:
        cp.wait_send()
    for cp in mine:
        cp.wait()


def _all_gather_hbm(shards, *, name):
    n = len(shards)

    def body(*refs):
        srcs, dsts = refs[:n], refs[n:2 * n]
        send_sems, recv_sems, local_sems = refs[2 * n:]
        _gather_copies(srcs, dsts, send_sems, recv_sems, local_sems)

    return pl.pallas_call(
        body, name=name,
        in_specs=[ANY] * n, out_specs=[ANY] * n,
        out_shape=[jax.ShapeDtypeStruct((N_DEV, *s.shape), s.dtype) for s in shards],
        scratch_shapes=_comm_sems(n),
    )(*shards)


def _all_reduce_small(part, *, name):
    R, W = part.shape

    def body(x_ref, out_ref, gathered, send_sems, recv_sems, local_sems):
        _gather_copies([x_ref], [gathered], send_sems, recv_sems, local_sems)
        acc = gathered[0]
        for d in range(1, N_DEV):
            acc = acc + gathered[d]
        out_ref[...] = acc

    return pl.pallas_call(
        body, name=name,
        in_specs=[VMEM_SPEC], out_specs=VMEM_SPEC,
        out_shape=jax.ShapeDtypeStruct((R, W), F32),
        scratch_shapes=[pltpu.VMEM((N_DEV, R, W), F32),
                        pltpu.SemaphoreType.DMA((7,)), pltpu.SemaphoreType.DMA((7,)), pltpu.SemaphoreType.DMA((1,))],
    )(part)


FLIPS = [(fx, fy, fc) for fx in (0, 1) for fy in (0, 1) for fc in (0, 1)][1:]


def _scatter_copies(srcs, dsts, send_sems, recv_sems, local_sems):
    n = len(srcs)
    x, y, c = _position()
    me = 4 * x + 2 * y + c
    mine = [pltpu.make_async_copy(srcs[a].at[me], dsts[a].at[me], local_sems.at[a]) for a in range(n)]
    for cp in mine:
        cp.start()
    copies = []
    for k, (fx, fy, fc) in enumerate(FLIPS):
        tx = 1 - x if fx else x
        ty = 1 - y if fy else y
        tc = 1 - c if fc else c
        peer = 4 * tx + 2 * ty + tc
        for a in range(n):
            cp = pltpu.make_async_remote_copy(
                src_ref=srcs[a].at[peer], dst_ref=dsts[a].at[me],
                send_sem=send_sems.at[7 * a + k], recv_sem=recv_sems.at[7 * a + k],
                device_id=(tx, ty, tc), device_id_type=MESH)
            cp.start()
            copies.append(cp)
    for cp in copies:
        cp.wait()
    for cp in mine:
        cp.wait()


def _comm_sems(n):
    return [pltpu.SemaphoreType.DMA((7 * n,)), pltpu.SemaphoreType.DMA((7 * n,)), pltpu.SemaphoreType.DMA((n,))]


def _scatter_blocks(parts, *, name):
    n = len(parts)

    def body(*refs):
        _scatter_copies(refs[:n], refs[n:2 * n], *refs[2 * n:])

    return pl.pallas_call(
        body, name=name,
        in_specs=[ANY] * n, out_specs=[ANY] * n,
        out_shape=[jax.ShapeDtypeStruct(p.shape, p.dtype) for p in parts],
        scratch_shapes=_comm_sems(n),
    )(*parts)


def _handshake(peers):
    barrier = pltpu.get_barrier_semaphore()
    for peer in peers:
        pl.semaphore_signal(barrier, inc=1, device_id=peer, device_id_type=MESH)
    pl.semaphore_wait(barrier, len(peers))


def _sequencer_call(arrays, out_types, copies_fn, peers_fn, *, name, collective_id):
    n = len(arrays)
    srcs = [jax.new_ref(a, memory_space=pltpu.MemorySpace.HBM) for a in arrays]
    dsts = [jax.empty_ref(t, memory_space=pltpu.MemorySpace.HBM) for t in out_types]

    @pl.kernel(mesh=plsc.ScalarSubcoreMesh(axis_name="sequencer", num_cores=1), name=name,
               scratch_types=_comm_sems(n), compiler_params=pltpu.CompilerParams(collective_id=collective_id))
    def launch(send_sems, recv_sems, local_sems):
        _handshake(peers_fn())
        copies_fn(srcs, dsts, send_sems, recv_sems, local_sems)

    launch()
    return [d[...] for d in dsts]


def _all_other_devices():
    x, y, c = _position()
    return [(1 - x if fx else x, 1 - y if fy else y, 1 - c if fc else c) for fx, fy, fc in FLIPS]


def _gather_peers():
    x, y, c = _position()
    return [(x, y, 1 - c), (1 - x, y, c), (x, 1 - y, c), (1 - x, 1 - y, c)]


def _scatter_blocks_async(parts, *, name, collective_id):
    return _sequencer_call(parts, [jax.ShapeDtypeStruct(p.shape, p.dtype) for p in parts],
                           _scatter_copies, _all_other_devices, name=name, collective_id=collective_id)


def _all_gather_async(shards, *, name, collective_id):
    return _sequencer_call(shards, [jax.ShapeDtypeStruct((N_DEV, *s.shape), s.dtype) for s in shards],
                           _gather_copies, _gather_peers, name=name, collective_id=collective_id)


def _adamw_math(w, g, m, v):
    m = ADAM_B1 * m + (1.0 - ADAM_B1) * g
    v = ADAM_B2 * v + (1.0 - ADAM_B2) * (g * g)
    m_hat = m / (1.0 - ADAM_B1 ** ADAM_STEP)
    v_hat = v / (1.0 - ADAM_B2 ** ADAM_STEP)
    delta = -ADAM_LR * (m_hat / (jnp.sqrt(v_hat) + ADAM_EPS) + ADAM_WD * w)
    return delta, m, v


def _adamw_reduce_call(recv, w, m, v, *, name, T=128):
    R, W = w.shape
    if R % T == 0:
        tr, tw = T, W
    elif (R // 2) % 16 == 0:
        tr, tw = R // 2, W
    else:
        tr, tw = R, 2 * LANES

    def body(p_ref, w_ref, m_ref, v_ref, g_out, d_out, m_out, v_out):
        g = p_ref[0].astype(F32)
        for d in range(1, N_DEV):
            g = g + p_ref[d].astype(F32)
        g_out[...] = g
        d_out[...], m_out[...], v_out[...] = _adamw_math(w_ref[...], g, m_ref[...], v_ref[...])

    row = pl.BlockSpec((tr, tw), lambda i, j: (i, j))
    out = jax.ShapeDtypeStruct((R, W), F32)
    return pl.pallas_call(
        body, name=name, grid=(R // tr, W // tw),
        in_specs=[pl.BlockSpec((N_DEV, tr, tw), lambda i, j: (0, i, j)), row, row, row],
        out_specs=[row] * 4, out_shape=[out] * 4,
        compiler_params=_cp(("parallel", "parallel"), VMEM_BIG),
    )(recv, w, m, v)


def _adamw_small_call(w, g, m, v, *, name):
    def body(w_ref, g_ref, m_ref, v_ref, d_out, m_out, v_out):
        d_out[...], m_out[...], v_out[...] = _adamw_math(w_ref[...], g_ref[...], m_ref[...], v_ref[...])

    out = jax.ShapeDtypeStruct(w.shape, F32)
    return pl.pallas_call(body, name=name, in_specs=[VMEM_SPEC] * 4, out_specs=[VMEM_SPEC] * 3,
                          out_shape=[out] * 3)(w, g, m, v)


def _tile_rows(n_elems):
    return -(-n_elems // (8 * LANES)) * 8


def _pack_rows(parts, rows):
    pieces = []
    for p in parts:
        q = p.reshape(-1, LANES)
        pieces.append(jnp.pad(q, ((0, _tile_rows(p.size) - q.shape[0]), (0, 0))))
    buf = jnp.concatenate(pieces, axis=0)
    return jnp.pad(buf, ((0, rows - buf.shape[0]), (0, 0)))


def _unpack_rows(buf, shapes):
    out, off = [], 0
    for shp in shapes:
        n = 1
        for s in shp:
            n *= s
        out.append(buf[off:off + n // LANES].reshape(shp))
        off += _tile_rows(n)
    return out


def _join_blocks_call(blocks, *, name, tw=2 * LANES):
    nb, r, w = blocks.shape

    def body(b_ref, o_ref):
        for d in range(nb):
            o_ref[d * r:(d + 1) * r, :] = b_ref[d]

    return pl.pallas_call(
        body, name=name, grid=(w // tw,),
        in_specs=[pl.BlockSpec((nb, r, tw), lambda j: (0, 0, j))],
        out_specs=pl.BlockSpec((nb * r, tw), lambda j: (0, j)),
        out_shape=jax.ShapeDtypeStruct((nb * r, w), blocks.dtype),
        compiler_params=_cp(("parallel",)),
    )(blocks)


def _split_blocks_call(parts, *, name, tw=2 * LANES):
    w = parts[0].shape[1]
    starts = [0]
    for q in parts:
        starts.append(starts[-1] + q.shape[0])
    assert starts[-1] == N_DEV * IN_BLK

    def body(*refs):
        o_ref = refs[-1]
        for d in range(N_DEV):
            lo, hi = IN_BLK * d, IN_BLK * (d + 1)
            for ref, c0, c1 in zip(refs[:-1], starts[:-1], starts[1:]):
                s0, e0 = max(lo, c0), min(hi, c1)
                if s0 < e0:
                    o_ref[d, s0 - lo:e0 - lo, :] = ref[s0 - c0:e0 - c0, :]

    return pl.pallas_call(
        body, name=name, grid=(w // tw,),
        in_specs=[pl.BlockSpec((q.shape[0], tw), lambda j: (0, j)) for q in parts],
        out_specs=pl.BlockSpec((N_DEV, IN_BLK, tw), lambda j: (0, 0, j)),
        out_shape=jax.ShapeDtypeStruct((N_DEV, IN_BLK, w), parts[0].dtype),
        compiler_params=_cp(("parallel",)),
    )(*parts)


def _local_step(x, target, W, scatter, finish):
    g1, g2, g3, g4 = [W[n].reshape(1, D_MODEL) for n in ("norm_pre_mix", "norm_post_mix", "norm_pre_ffn", "norm_post_ffn")]
    w_in_t = W["w_in_t"]
    wfi, wfo = W["w_ffn_in"], W["w_ffn_out"].reshape(4, FF_BLK, D_MODEL)
    wg = jnp.pad(W["w_gate_up"], ((0, LANES - RANK), (0, 0))).astype(BF16)
    bgate = W["b_gate"].reshape(1, QK)
    gnorm = W["gla_norm"].reshape(HEADS, 1, DV)
    ln_g = W["sgu_ln_g"].reshape(GROUPS, 1, DG)
    ln_b = W["sgu_ln_b"].reshape(GROUPS, 1, DG)
    w_sp = W["w_spatial"]
    b_sp = W["b_spatial"].reshape(GROUPS, SBLOCK, 1)

    a, pa, alow, ps, pg = _in_proj_call(x, g1, w_in_t, name="in_proj")
    y_gla, states = _gla_fwd_call(pa, alow, wg, bgate, gnorm, name="gla_fwd")
    y_sgu = _sgu_fwd_call(ps, ln_g, ln_b, w_sp, b_sp, name="sgu_fwd")
    t1, t2, merged, mix, x1, h = _mixer_tail_call(y_gla, y_sgu, pg, x, W["w_branch_gla"], W["w_branch_sgu"],
                                                  W["w_out"], g2, g3, name="mixer_tail")
    gu, z = _ffn_in_call(h, wfi, name="ffn_in")
    loss, dx2, dy, dg4 = _ffn_out_loss_call(z, wfo, x1, target, g4, name="ffn_out_loss")

    grads = {"norm_post_ffn": dg4}
    done = {}
    dgu = _ffn_out_bwd_call(dy, wfo, gu, name="ffn_out_bwd")
    dw_ffn_out = _weight_grads_call([z, dy], [(0, 1)], name="d_ffn_out_w")[0].reshape(N_DEV, D_FF // N_DEV, D_MODEL)
    dw_ffn_in = _ffn_in_grad_call(h, dgu, name="d_ffn_in_w")
    dgu, dw_ffn_out, dw_ffn_in = lax.optimization_barrier((dgu, dw_ffn_out, dw_ffn_in))
    ffn_received = scatter(("w_ffn_out", "w_ffn_in"), [dw_ffn_out, dw_ffn_in])
    dx1, dmix, grads["norm_pre_ffn"], grads["norm_post_mix"] = _ffn_in_bwd_call(dgu, wfi, dx2, x1, mix, g3, g2, name="ffn_in_bwd")
    dt1, dt2, dpg, dy_gla, dy_sgu = _mixer_bwd_call(dmix, t1, t2, pg, W["w_out"], W["w_branch_gla"], W["w_branch_sgu"],
                                                    name="mixer_bwd")
    rows = D_MODEL // N_DEV
    mixer_grads = _weight_grads_call([merged, dmix, y_gla, dt1, y_sgu, dt2], [(0, 1), (2, 3), (4, 5)], name="d_mixer_w")
    dy_gla, dy_sgu, mixer_grads = lax.optimization_barrier((dy_gla, dy_sgu, mixer_grads))
    mixer_received = scatter(("w_out", "w_branch_gla", "w_branch_sgu"),
                             [g.reshape(N_DEV, rows, D_MODEL) for g in mixer_grads])
    dpa, dlogit, dgn, dbg = _gla_bwd_call(pa, alow, wg, bgate, gnorm, states, dy_gla, name="gla_bwd")
    ffn_done, dpa, dlogit = lax.optimization_barrier((finish(ffn_received), dpa, dlogit))
    done.update(ffn_done)
    dps, dlg, dlb, dwsp, dbsp = _sgu_bwd_call(ps, ln_g, ln_b, w_sp, b_sp, dy_sgu, name="sgu_bwd")
    mixer_done, dps = lax.optimization_barrier((finish(mixer_received), dps))
    done.update(mixer_done)
    dlow = _mm(dlogit, wg, "nt", BF16, name="d_gate_up_x")
    dw_a, dw_low = _weight_grads_call([a, dpa, dlow], [(1, 0), (2, 0)], name="d_in_w_qkvr")
    dw_s, dw_g = _weight_grads_call([a, dps, dpg], [(1, 0), (2, 0)], name="d_in_w_sgu_gates")
    dw_in = _split_blocks_call([dw_a, dw_low[:RANK], dw_s, dw_g], name="split_w_in_grad")
    in_received = scatter(("w_in",), [dw_in])
    dwg = _mm(alow, dlogit, "tn", F32, name="d_gate_up_w")
    grad_x, grads["norm_pre_mix"] = _in_proj_bwd_call(dpa, dlow, dps, dpg, w_in_t, x, dx1, g1, name="in_proj_bwd")

    grads["w_gate_up"] = dwg[:RANK]
    grads["b_gate"] = dbg
    grads["gla_norm"] = dgn
    grads["sgu_ln_g"] = dlg
    grads["sgu_ln_b"] = dlb
    grads["w_spatial"] = dwsp
    grads["b_spatial"] = dbsp
    done.update(finish(in_received))
    return loss, grad_x, grads, done


WEIGHTS = ("norm_pre_mix", "w_in", "w_gate_up", "b_gate", "gla_norm", "sgu_ln_g", "sgu_ln_b", "w_spatial",
           "b_spatial", "w_branch_gla", "w_branch_sgu", "w_out", "norm_post_mix", "norm_pre_ffn", "w_ffn_in",
           "w_ffn_out", "norm_post_ffn")
BIG = ("w_in", "w_branch_gla", "w_branch_sgu", "w_out", "w_ffn_in", "w_ffn_out")
MIXER = ("w_branch_gla", "w_branch_sgu", "w_out")
FFN = ("w_ffn_in", "w_ffn_out")
COLUMN_SHARDED = ("w_in", "w_ffn_in")
SMALL = tuple(n for n in WEIGHTS if n not in BIG)
SMALL_SHARDED = ("w_gate_up", "gla_norm", "sgu_ln_g", "sgu_ln_b")
SMALL_FULL = {"norm_pre_mix": (1024,), "w_gate_up": (16, 512), "b_gate": (512,), "gla_norm": (4, 256),
              "sgu_ln_g": (4, 256), "sgu_ln_b": (4, 256), "w_spatial": (4, 128, 128), "b_spatial": (4, 128),
              "norm_post_mix": (1024,), "norm_pre_ffn": (1024,), "norm_post_ffn": (1024,)}
SMALL_GRAD_ROWS = 648
SMALL_STATE_ROWS = 600
SMALL_GATHER_ROWS = 32


def kernel(x, norm_pre_mix, w_in, w_gate_up, b_gate, gla_norm, sgu_ln_g, sgu_ln_b, w_spatial, b_spatial, w_branch_gla, w_branch_sgu, w_out, norm_post_mix, norm_pre_ffn, w_ffn_in, w_ffn_out, norm_post_ffn, loss_target, m_norm_pre_mix, m_w_in, m_w_gate_up, m_b_gate, m_gla_norm, m_sgu_ln_g, m_sgu_ln_b, m_w_spatial, m_b_spatial, m_w_branch_gla, m_w_branch_sgu, m_w_out, m_norm_post_mix, m_norm_pre_ffn, m_w_ffn_in, m_w_ffn_out, m_norm_post_ffn, v_norm_pre_mix, v_w_in, v_w_gate_up, v_b_gate, v_gla_norm, v_sgu_ln_g, v_sgu_ln_b, v_w_spatial, v_b_spatial, v_w_branch_gla, v_w_branch_sgu, v_w_out, v_norm_post_mix, v_norm_pre_ffn, v_w_ffn_in, v_w_ffn_out, v_norm_post_ffn):
    given = dict(locals())
    def local(a, n):
        return a[0].T if n in COLUMN_SHARDED else a[0]

    w = {n: local(given[n], n) for n in WEIGHTS}
    m = {n: local(given["m_" + n], n) for n in WEIGHTS}
    v = {n: local(given["v_" + n], n) for n in WEIGHTS}
    xs, target = x[0], loss_target[0]
    me = 4 * lax.axis_index("x") + 2 * lax.axis_index("y") + lax.axis_index("c")

    small_shard = _pack_rows([w[n] for n in SMALL_SHARDED], SMALL_GATHER_ROWS)
    first = _all_gather_hbm([w["w_in"].astype(BF16), small_shard], name="gather_w_in")
    rest, first = lax.optimization_barrier(([w[n].astype(BF16) for n in MIXER + FFN], first))
    rest_blocks = _all_gather_async(rest, name="gather_rest", collective_id=1)
    W = {n: w[n] for n in SMALL if n not in SMALL_SHARDED}
    blocks = dict(zip(MIXER + FFN, rest_blocks))
    for n in ("w_branch_gla", "w_branch_sgu", "w_out", "w_ffn_out"):
        W[n] = blocks[n].reshape(-1, D_MODEL)
    W["w_ffn_in"] = blocks["w_ffn_in"]
    W["w_in_t"] = _join_blocks_call(first[0], name="join_w_in")
    small_blocks = first[1]
    off = 0
    for n in SMALL_SHARDED:
        r, c = w[n].shape
        blk = small_blocks[:, off:off + r * c // LANES].reshape(N_DEV, r, c)
        W[n] = blk.transpose(1, 0, 2).reshape(r, N_DEV * c)
        off += _tile_rows(r * c)

    scatter_ids = iter((3, 4, 5))

    def scatter(names, parts):
        got = _scatter_blocks_async(parts, name="scatter_" + "_".join(names), collective_id=next(scatter_ids))
        return dict(zip(names, got))

    def finish(received):
        return {n: _adamw_reduce_call(r, w[n], m[n], v[n], name="adamw_" + n) for n, r in received.items()}

    loss_part, grad_x, grads, big_done = _local_step(xs, target, W, scatter, finish)

    small_part = jnp.concatenate([_pack_rows([grads[n] for n in SMALL], SMALL_GRAD_ROWS),
                                  jnp.broadcast_to(loss_part, (8, LANES))], axis=0)
    small_sum = _all_reduce_small(small_part, name="reduce_small")
    loss = small_sum[SMALL_GRAD_ROWS, 0]
    g_small = dict(zip(SMALL, _unpack_rows(small_sum, [SMALL_FULL[n] for n in SMALL])))
    for n in SMALL_SHARDED:
        c = w[n].shape[1]
        g_small[n] = lax.dynamic_slice_in_dim(g_small[n], me * c, c, axis=1)

    g_big, d_big, m_big, v_big = [{n: big_done[n][i] for n in BIG} for i in range(4)]
    small_shapes = [w[n].shape for n in SMALL]
    small_state = [_pack_rows([s[n] for n in SMALL], SMALL_STATE_ROWS) for s in (w, g_small, m, v)]
    small_out = _adamw_small_call(*small_state, name="adamw_small")
    d_small, m_small, v_small = [dict(zip(SMALL, _unpack_rows(o, small_shapes))) for o in small_out]

    def pick(big, sml):
        return [(big[n].T if n in COLUMN_SHARDED else big[n] if n in BIG else sml[n])[None] for n in WEIGHTS]

    return (loss, grad_x[None], *pick(g_big, g_small), *pick(d_big, d_small), *pick(m_big, m_small),
            *pick(v_big, v_small))
```

```python
import jax
import jax.numpy as jnp
from jax import lax
from jax.experimental import pallas as pl
from jax.experimental.pallas import tpu as pltpu
from jax.experimental.pallas import tpu_sc as plsc

F32 = jnp.float32
BF16 = jnp.bfloat16

D_MODEL = 1024
N_DEV = 8
CHUNK = 64
HEADS = 4
DK = 128
DV = 256
QK = HEADS * DK
GV = HEADS * DV
RANK = 16
GROUPS = 4
SBLOCK = 128
DG = 256
D_FF = 2816
FF_BLK = 704
EPS = 1e-6
Q_SCALE = DK ** -0.5
LANES = 128
VMEM_BIG = 48 * 1024 * 1024

D_IN = 7184
IN_BLK = 898
A_COLS = (0, 3072)
LOW_COLS = (3072, 3088)
S_COLS = (3088, 5136)
G_COLS = (5136, 7184)

ADAM_LR = 0.001
ADAM_B1 = 0.9
ADAM_B2 = 0.999
ADAM_EPS = 1e-08
ADAM_WD = 0.01
ADAM_STEP = 10

MESH = pl.DeviceIdType.MESH
ANY = pl.BlockSpec(memory_space=pl.ANY)
VMEM_SPEC = pl.BlockSpec(memory_space=pltpu.VMEM)


def _cp(sem=None, vmem=None):
    return pltpu.CompilerParams(dimension_semantics=sem, vmem_limit_bytes=vmem)


def _sig(x):
    return 0.5 * jnp.tanh(0.5 * x) + 0.5


def _gelu(x):
    c = 0.7978845608028654
    t = jnp.tanh(c * (x + 0.044715 * x * x * x))
    return 0.5 * x * (1.0 + t)


def _gelu_grad(x):
    c = 0.7978845608028654
    x2 = x * x
    t = jnp.tanh(c * (x + 0.044715 * x * x2))
    return 0.5 * (1.0 + t) + 0.5 * x * (1.0 - t * t) * c * (1.0 + 3.0 * 0.044715 * x2)


def _logsig(x):
    return jnp.minimum(x, 0.0) - jnp.log1p(jnp.exp(-jnp.abs(x)))


def _dot(a, b, dims):
    return lax.dot_general(a, b, (dims, ((), ())), preferred_element_type=F32)


NN = ((1,), (0,))
NT = ((1,), (1,))
TN = ((0,), (0,))


def _exact_mask_dot(mask_bf16, x):
    hi = x.astype(BF16)
    r1 = x - hi.astype(F32)
    mid = r1.astype(BF16)
    lo = (r1 - mid.astype(F32)).astype(BF16)
    return _dot(mask_bf16, hi, NN) + _dot(mask_bf16, mid, NN) + _dot(mask_bf16, lo, NN)


def _pick_tile(dim, target):
    if dim <= target:
        return dim
    best = None
    for t in range(LANES, int(1.4 * target) + 1, LANES):
        if dim % t == 0:
            best = t
    assert best is not None, (dim, target)
    return best


def _mm_call(a, b, *, name, grid, a_spec, b_spec, o_spec, out_shape, dims, acc_shape):
    nk = grid[2]

    def body(a_ref, b_ref, o_ref, *acc):
        part = _dot(a_ref[...], b_ref[...], dims)
        if nk == 1:
            o_ref[...] = part.astype(o_ref.dtype)
        else:
            acc_ref = acc[0]
            k = pl.program_id(2)

            @pl.when(k == 0)
            def _():
                acc_ref[...] = part

            @pl.when(k > 0)
            def _():
                acc_ref[...] += part

            @pl.when(k == nk - 1)
            def _():
                o_ref[...] = acc_ref[...].astype(o_ref.dtype)

    return pl.pallas_call(
        body, name=name, grid=grid, in_specs=[a_spec, b_spec], out_specs=o_spec, out_shape=out_shape,
        scratch_shapes=[] if nk == 1 else [pltpu.VMEM(acc_shape, F32)],
        compiler_params=_cp(("parallel", "parallel", "arbitrary"), VMEM_BIG),
    )(a, b)


def _mm(a, b, mode, out_dtype, *, name, tm=512, tn=1024, tk=1024):
    if mode == "nn":
        (M, K), (_, N) = a.shape, b.shape
    elif mode == "nt":
        (M, K), (N, _) = a.shape, b.shape
    else:
        (K, M), (_, N) = a.shape, b.shape
    tm, tn, tk = _pick_tile(M, tm), _pick_tile(N, tn), _pick_tile(K, tk)
    if mode == "nn":
        a_spec = pl.BlockSpec((tm, tk), lambda i, j, k: (i, k))
        b_spec = pl.BlockSpec((tk, tn), lambda i, j, k: (k, j))
        dims = NN
    elif mode == "nt":
        a_spec = pl.BlockSpec((tm, tk), lambda i, j, k: (i, k))
        b_spec = pl.BlockSpec((tn, tk), lambda i, j, k: (j, k))
        dims = NT
    else:
        a_spec = pl.BlockSpec((tk, tm), lambda i, j, k: (k, i))
        b_spec = pl.BlockSpec((tk, tn), lambda i, j, k: (k, j))
        dims = TN
    return _mm_call(a, b, name=name, grid=(M // tm, N // tn, K // tk), a_spec=a_spec, b_spec=b_spec,
                    o_spec=pl.BlockSpec((tm, tn), lambda i, j, k: (i, j)),
                    out_shape=jax.ShapeDtypeStruct((M, N), out_dtype), dims=dims, acc_shape=(tm, tn))


ROW_TILE = 512
SUB_ROWS = 256


def _sub_tiles(T):
    return [pl.ds(r0, min(SUB_ROWS, T)) for r0 in range(0, T, SUB_ROWS)]


def _rt(T, W, c=0):
    return pl.BlockSpec((T, W), lambda i: (i, c))


def _rt3(nb, T, W):
    return pl.BlockSpec((nb, T, W), lambda i: (0, i, 0))


def _res(shape):
    nd = len(shape)
    return pl.BlockSpec(tuple(shape), lambda i: (0,) * nd, pipeline_mode=pl.Buffered(1))


def _acc(shape):
    nd = len(shape)
    return pl.BlockSpec(tuple(shape), lambda i: (0,) * nd, pipeline_mode=pl.Buffered(1))


def _nbytes(shape, dtype):
    n = jnp.dtype(dtype).itemsize
    for s in shape:
        n *= s
    return n


def _vmem_limit(tiles, resident, temps=16 * 1024 * 1024):
    need = 2 * sum(_nbytes(s, d) for s, d in tiles) + sum(_nbytes(s, d) for s, d in resident) + temps
    return min(need, 60 * 1024 * 1024)


def _tok_call(body, *, name, S, T, ins, outs, semantics="parallel"):
    tiles = [(spec.block_shape, a.dtype) for a, spec, kind in ins + outs if kind == "tile"]
    resident = [(a.shape, a.dtype) for a, spec, kind in ins + outs if kind == "res"]
    return pl.pallas_call(
        body, name=name, grid=(S // T,),
        in_specs=[spec for _, spec, _ in ins], out_specs=[spec for _, spec, _ in outs],
        out_shape=[jax.ShapeDtypeStruct(a.shape, a.dtype) for a, _, _ in outs],
        compiler_params=_cp((semantics,), _vmem_limit(tiles, resident)),
    )(*[a for a, _, _ in ins])


def _tile(a, spec):
    return (a, spec, "tile")


def _whole(a):
    return (a, _res(a.shape), "res")


def _out_tile(shape, dtype, spec):
    return (jax.ShapeDtypeStruct(shape, dtype), spec, "tile")


def _out_acc(shape, dtype=F32):
    return (jax.ShapeDtypeStruct(shape, dtype), _acc(shape), "res")


def _rms_stats(x):
    r = lax.rsqrt(jnp.mean(x * x, axis=-1, keepdims=True) + EPS)
    return r, x * r


def _rms_bwd(xh, r, g, dy):
    dxh = dy * g
    dx = r * (dxh - xh * jnp.mean(dxh * xh, axis=-1, keepdims=True))
    dg = jnp.sum(dy * xh, axis=0, keepdims=True)
    return dx, dg


def _accum(ref, val):
    @pl.when(pl.program_id(0) == 0)
    def _():
        ref[...] = val

    @pl.when(pl.program_id(0) > 0)
    def _():
        ref[...] += val


def _dot_rows_t(a, w_ref, row0, o_ref, chunk=1024):
    n = o_ref.shape[1]
    for n0 in range(0, n, chunk):
        n1 = min(n, n0 + chunk)
        o_ref[:, n0:n1] = _dot(a, w_ref[row0 + n0:row0 + n1, :], NT).astype(o_ref.dtype)


def _in_proj_call(x, g1, w_in_t, *, name, T=ROW_TILE):
    S = x.shape[0]

    def body(x_ref, g_ref, w_ref, a_ref, pa_ref, al_ref, ps_ref, pg_ref):
        _, xh = _rms_stats(x_ref[...])
        a = (xh * g_ref[...]).astype(BF16)
        a_ref[...] = a
        _dot_rows_t(a, w_ref, A_COLS[0], pa_ref)
        _dot_rows_t(a, w_ref, LOW_COLS[0], al_ref)
        _dot_rows_t(a, w_ref, S_COLS[0], ps_ref)
        _dot_rows_t(a, w_ref, G_COLS[0], pg_ref)

    widths = (D_MODEL, A_COLS[1] - A_COLS[0], LANES, S_COLS[1] - S_COLS[0], G_COLS[1] - G_COLS[0])
    return _tok_call(
        body, name=name, S=S, T=T,
        ins=[_tile(x, _rt(T, D_MODEL)), _whole(g1), _whole(w_in_t)],
        outs=[_out_tile((S, w), BF16, _rt(T, w)) for w in widths])


def _mixer_tail_call(y_gla, y_sgu, pg, x, w_bg, w_bs, w_out, g2, g3, *, name, T=ROW_TILE):
    S = x.shape[0]

    def body(yg_ref, ys_ref, pg_ref, x_ref, wbg_ref, wbs_ref, wo_ref, g2_ref, g3_ref,
             t1_ref, t2_ref, mg_ref, mix_ref, x1_ref, h_ref):
        for rows in _sub_tiles(T):
            t1 = _dot(yg_ref[rows, :], wbg_ref[...], NN)
            t2 = _dot(ys_ref[rows, :], wbs_ref[...], NN)
            t1_ref[rows, :] = t1.astype(BF16)
            t2_ref[rows, :] = t2.astype(BF16)
            sg = _sig(pg_ref[rows, pl.ds(0, D_MODEL)].astype(F32))
            ss = _sig(pg_ref[rows, pl.ds(D_MODEL, D_MODEL)].astype(F32))
            merged = (sg * t1 + ss * t2).astype(BF16)
            mg_ref[rows, :] = merged
            mix = _dot(merged, wo_ref[...], NN)
            mix_ref[rows, :] = mix
            _, mh = _rms_stats(mix)
            x1 = x_ref[rows, :] + mh * g2_ref[...]
            x1_ref[rows, :] = x1
            _, xh = _rms_stats(x1)
            h_ref[rows, :] = (xh * g3_ref[...]).astype(BF16)

    row = _rt(T, D_MODEL)
    b16 = lambda: _out_tile((S, D_MODEL), BF16, row)
    f32 = lambda: _out_tile((S, D_MODEL), F32, row)
    return _tok_call(
        body, name=name, S=S, T=T,
        ins=[_tile(y_gla, row), _tile(y_sgu, row), _tile(pg, _rt(T, 2 * D_MODEL)), _tile(x, row),
             _whole(w_bg), _whole(w_bs), _whole(w_out), _whole(g2), _whole(g3)],
        outs=[b16(), b16(), b16(), f32(), f32(), b16()])


def _ffn_in_call(h, wfi, *, name, T=ROW_TILE):
    S = h.shape[0]

    def body(h_ref, w_ref, gu_ref, z_ref):
        hv = h_ref[...]
        for d in range(4):
            gate = _dot(hv, w_ref[d], NT)
            up = _dot(hv, w_ref[d + 4], NT)
            gu_ref[d] = gate.astype(BF16)
            gu_ref[d + 4] = up.astype(BF16)
            z_ref[d] = (gate * _sig(gate) * up).astype(BF16)

    return _tok_call(
        body, name=name, S=S, T=T,
        ins=[_tile(h, _rt(T, D_MODEL)), _whole(wfi)],
        outs=[_out_tile((N_DEV, S, FF_BLK), BF16, _rt3(N_DEV, T, FF_BLK)),
              _out_tile((4, S, FF_BLK), BF16, _rt3(4, T, FF_BLK))])


def _ffn_out_loss_call(z, wfo, x1, target, g4, *, name, T=ROW_TILE):
    S = x1.shape[0]

    def body(z_ref, w_ref, x1_ref, t_ref, g4_ref, loss_ref, dx2_ref, dy_ref, dg4_ref):
        loss = jnp.zeros((1, 1), F32)
        dg4 = jnp.zeros((1, D_MODEL), F32)
        for rows in _sub_tiles(T):
            y = _dot(z_ref[0, rows, :], w_ref[0], NN)
            for d in range(1, 4):
                y = y + _dot(z_ref[d, rows, :], w_ref[d], NN)
            r, yh = _rms_stats(y)
            diff = x1_ref[rows, :] + yh * g4_ref[...] - t_ref[rows, :]
            loss = loss + 0.5 * jnp.sum(jnp.mean(diff * diff, axis=-1, keepdims=True), axis=0, keepdims=True)
            dx2 = diff * (1.0 / D_MODEL)
            dx2_ref[rows, :] = dx2
            dy, dg = _rms_bwd(yh, r, g4_ref[...], dx2)
            dy_ref[rows, :] = dy.astype(BF16)
            dg4 = dg4 + dg
        _accum(loss_ref, jnp.broadcast_to(loss, (1, LANES)))
        _accum(dg4_ref, dg4)

    row = _rt(T, D_MODEL)
    return _tok_call(
        body, name=name, S=S, T=T, semantics="arbitrary",
        ins=[_tile(z, _rt3(4, T, FF_BLK)), _whole(wfo), _tile(x1, row), _tile(target, row), _whole(g4)],
        outs=[_out_acc((1, LANES)), _out_tile((S, D_MODEL), F32, row), _out_tile((S, D_MODEL), BF16, row),
              _out_acc((1, D_MODEL))])


def _ffn_out_bwd_call(dy, wfo, gu, *, name, T=ROW_TILE):
    S = dy.shape[0]

    def body(dy_ref, w_ref, gu_ref, dgu_ref):
        dyv = dy_ref[...]
        for d in range(4):
            dz = _dot(dyv, w_ref[d], NT)
            gt = gu_ref[d].astype(F32)
            up = gu_ref[d + 4].astype(F32)
            s = _sig(gt)
            dgu_ref[d] = (dz * up * s * (1.0 + gt * (1.0 - s))).astype(BF16)
            dgu_ref[d + 4] = (dz * gt * s).astype(BF16)

    blocks = _rt3(N_DEV, T, FF_BLK)
    return _tok_call(
        body, name=name, S=S, T=T,
        ins=[_tile(dy, _rt(T, D_MODEL)), _whole(wfo), _tile(gu, blocks)],
        outs=[_out_tile((N_DEV, S, FF_BLK), BF16, blocks)])[0]


def _ffn_in_bwd_call(dgu, wfi, dx2, x1, mix, g3, g2, *, name, T=ROW_TILE):
    S = x1.shape[0]

    def body(dgu_ref, w_ref, dx2_ref, x1_ref, mix_ref, g3_ref, g2_ref, dx1_ref, dmix_ref, dg3_ref, dg2_ref):
        dg3 = jnp.zeros((1, D_MODEL), F32)
        dg2 = jnp.zeros((1, D_MODEL), F32)
        for rows in _sub_tiles(T):
            dh = _dot(dgu_ref[0, rows, :], w_ref[0], NN)
            for d in range(1, N_DEV):
                dh = dh + _dot(dgu_ref[d, rows, :], w_ref[d], NN)
            r3, xh = _rms_stats(x1_ref[rows, :])
            d3, g = _rms_bwd(xh, r3, g3_ref[...], dh)
            dg3 = dg3 + g
            dx1 = dx2_ref[rows, :] + d3
            dx1_ref[rows, :] = dx1
            r2, mh = _rms_stats(mix_ref[rows, :])
            dmix, g = _rms_bwd(mh, r2, g2_ref[...], dx1)
            dg2 = dg2 + g
            dmix_ref[rows, :] = dmix.astype(BF16)
        _accum(dg3_ref, dg3)
        _accum(dg2_ref, dg2)

    row = _rt(T, D_MODEL)
    return _tok_call(
        body, name=name, S=S, T=T, semantics="arbitrary",
        ins=[_tile(dgu, _rt3(N_DEV, T, FF_BLK)), _whole(wfi), _tile(dx2, row), _tile(x1, row), _tile(mix, row),
             _whole(g3), _whole(g2)],
        outs=[_out_tile((S, D_MODEL), F32, row), _out_tile((S, D_MODEL), BF16, row),
              _out_acc((1, D_MODEL)), _out_acc((1, D_MODEL))])


def _mixer_bwd_call(dmix, t1, t2, pg, w_out, w_bg, w_bs, *, name, T=ROW_TILE):
    S = dmix.shape[0]

    def body(dmix_ref, t1_ref, t2_ref, pg_ref, wo_ref, wbg_ref, wbs_ref, dt1_ref, dt2_ref, dpg_ref, dyg_ref, dys_ref):
        gg, gs = pl.ds(0, D_MODEL), pl.ds(D_MODEL, D_MODEL)
        for rows in _sub_tiles(T):
            dm = _dot(dmix_ref[rows, :], wo_ref[...], NT)
            sg = _sig(pg_ref[rows, gg].astype(F32))
            ss = _sig(pg_ref[rows, gs].astype(F32))
            dt1 = (dm * sg).astype(BF16)
            dt2 = (dm * ss).astype(BF16)
            dt1_ref[rows, :] = dt1
            dt2_ref[rows, :] = dt2
            dpg_ref[rows, gg] = (dm * t1_ref[rows, :].astype(F32) * sg * (1.0 - sg)).astype(BF16)
            dpg_ref[rows, gs] = (dm * t2_ref[rows, :].astype(F32) * ss * (1.0 - ss)).astype(BF16)
            dyg_ref[rows, :] = _dot(dt1, wbg_ref[...], NT).astype(BF16)
            dys_ref[rows, :] = _dot(dt2, wbs_ref[...], NT).astype(BF16)

    row = _rt(T, D_MODEL)
    wide = _rt(T, 2 * D_MODEL)
    b16 = lambda: _out_tile((S, D_MODEL), BF16, row)
    return _tok_call(
        body, name=name, S=S, T=T,
        ins=[_tile(dmix, row), _tile(t1, row), _tile(t2, row), _tile(pg, wide), _whole(w_out), _whole(w_bg), _whole(w_bs)],
        outs=[b16(), b16(), _out_tile((S, 2 * D_MODEL), BF16, wide), b16(), b16()])


def _in_proj_bwd_call(dpa, dlow, dps, dpg, w_in_t, x, dx1, g1, *, name, T=ROW_TILE):
    S = x.shape[0]

    def body(dpa_ref, dl_ref, dps_ref, dpg_ref, w_ref, x_ref, dx1_ref, g1_ref, gx_ref, dg1_ref):
        da = (_dot(dpa_ref[...], w_ref[A_COLS[0]:A_COLS[1], :], NN)
              + _dot(dl_ref[...], w_ref[LOW_COLS[0]:LOW_COLS[0] + LANES, :], NN)
              + _dot(dps_ref[...], w_ref[S_COLS[0]:S_COLS[1], :], NN)
              + _dot(dpg_ref[...], w_ref[G_COLS[0]:G_COLS[1], :], NN))
        r, xh = _rms_stats(x_ref[...])
        dxa, dg = _rms_bwd(xh, r, g1_ref[...], da)
        gx_ref[...] = dx1_ref[...] + dxa
        _accum(dg1_ref, dg)

    row = _rt(T, D_MODEL)
    return _tok_call(
        body, name=name, S=S, T=T, semantics="arbitrary",
        ins=[_tile(dpa, _rt(T, dpa.shape[1])), _tile(dlow, _rt(T, dlow.shape[1])), _tile(dps, _rt(T, dps.shape[1])),
             _tile(dpg, _rt(T, dpg.shape[1])), _whole(w_in_t), _tile(x, row), _tile(dx1, row), _whole(g1)],
        outs=[_out_tile((S, D_MODEL), F32, row), _out_acc((1, D_MODEL))])


TOKEN_TILE = 512


def _weight_grads_call(arrays, pairs, *, name, tk=TOKEN_TILE):
    S = arrays[0].shape[-2]
    tk = min(tk, S)
    n_in = len(arrays)

    def out_shape(i, j):
        a, b = arrays[i], arrays[j]
        if a.ndim == 3:
            return (a.shape[0], a.shape[2], b.shape[1])
        if b.ndim == 3:
            return (b.shape[0], a.shape[1], b.shape[2])
        return (a.shape[1], b.shape[1])

    shapes = [out_shape(i, j) for i, j in pairs]

    def body(*refs):
        ins, outs, accs = refs[:n_in], refs[n_in:n_in + len(pairs)], refs[n_in + len(pairs):]
        k = pl.program_id(0)

        @pl.when(k == 0)
        def _():
            for acc in accs:
                acc[...] = jnp.zeros_like(acc)

        for (i, j), acc in zip(pairs, accs):
            a_ref, b_ref = ins[i], ins[j]
            if len(a_ref.shape) == 3:
                for n in range(a_ref.shape[0]):
                    acc[n] += _dot(a_ref[n], b_ref[...], TN)
            elif len(b_ref.shape) == 3:
                a = a_ref[...]
                for n in range(b_ref.shape[0]):
                    acc[n] += _dot(a, b_ref[n], TN)
            else:
                b = b_ref[...]
                for m0 in range(0, a_ref.shape[1], 1024):
                    m1 = min(a_ref.shape[1], m0 + 1024)
                    acc[m0:m1, :] += _dot(a_ref[:, m0:m1], b, TN)

        @pl.when(k == S // tk - 1)
        def _():
            for out, acc in zip(outs, accs):
                out[...] = acc[...].astype(out.dtype)

    def in_spec(a):
        if a.ndim == 3:
            return pl.BlockSpec((a.shape[0], tk, a.shape[2]), lambda k: (0, k, 0))
        return pl.BlockSpec((tk, a.shape[1]), lambda k: (k, 0))

    tiles = [(in_spec(a).block_shape, a.dtype) for a in arrays]
    resident = [(s, F32) for s in shapes] + [(s, BF16) for s in shapes]
    return pl.pallas_call(
        body, name=name, grid=(S // tk,),
        in_specs=[in_spec(a) for a in arrays],
        out_specs=[_acc(s) for s in shapes],
        out_shape=[jax.ShapeDtypeStruct(s, BF16) for s in shapes],
        scratch_shapes=[pltpu.VMEM(s, F32) for s in shapes],
        compiler_params=_cp(("arbitrary",), _vmem_limit(tiles, resident)),
    )(*arrays)


def _ffn_in_grad_call(h, dgu, *, name, tk=TOKEN_TILE):
    S = h.shape[0]
    tk = min(tk, S)
    nb = 4
    shape = (nb, FF_BLK, D_MODEL)

    def body(h_ref, dgu_ref, out_ref, acc):
        k = pl.program_id(1)

        @pl.when(k == 0)
        def _():
            acc[...] = jnp.zeros_like(acc)

        hv = h_ref[...]
        for n in range(nb):
            acc[n] += _dot(dgu_ref[n], hv, TN)

        @pl.when(k == S // tk - 1)
        def _():
            out_ref[...] = acc[...].astype(out_ref.dtype)

    tiles = [((tk, D_MODEL), BF16), ((nb, tk, FF_BLK), BF16), (shape, BF16)]
    return pl.pallas_call(
        body, name=name, grid=(N_DEV // nb, S // tk),
        in_specs=[pl.BlockSpec((tk, D_MODEL), lambda g, k: (k, 0)),
                  pl.BlockSpec((nb, tk, FF_BLK), lambda g, k: (g, k, 0))],
        out_specs=pl.BlockSpec(shape, lambda g, k: (g, 0, 0)),
        out_shape=jax.ShapeDtypeStruct((N_DEV, FF_BLK, D_MODEL), BF16),
        scratch_shapes=[pltpu.VMEM(shape, F32)],
        compiler_params=_cp(("parallel", "arbitrary"), _vmem_limit(tiles, [(shape, F32)])),
    )(h, dgu)


GLA_TILE = 256
Q_OFF, K_OFF, V_OFF, R_OFF = 0, QK, 2 * QK, 2 * QK + GV


def _tri(lower):
    r = lax.broadcasted_iota(jnp.int32, (CHUNK, CHUNK), 0)
    c = lax.broadcasted_iota(jnp.int32, (CHUNK, CHUNK), 1)
    return jnp.where((r >= c) if lower else (c >= r), 1.0, 0.0).astype(BF16)


def _gla_fwd_call(pa, alow, wg, bgate, gnorm, *, name):
    S = pa.shape[0]
    Tg = min(GLA_TILE, S)
    cb = Tg // CHUNK

    def body(pa_ref, al_ref, wg_ref, bg_ref, gn_ref, y_ref, st_ref, state):
        @pl.when(pl.program_id(0) == 0)
        def _():
            state[...] = jnp.zeros_like(state)

        logit = _dot(al_ref[...], wg_ref[...], NN) + bg_ref[...]
        ls = _logsig(logit) * (1.0 / 16.0)
        tri = _tri(True)
        for c in range(cb):
            rows = pl.ds(c * CHUNK, CHUNK)
            cum = _exact_mask_dot(tri, ls[c * CHUNK:(c + 1) * CHUNK])
            tot = cum[CHUNK - 1:CHUNK]
            kd = (pa_ref[rows, pl.ds(K_OFF, QK)].astype(F32) * jnp.exp(tot - cum)).astype(BF16)
            decay = jnp.exp(tot)
            for h in range(HEADS):
                lanes = slice(h * DK, (h + 1) * DK)
                new = state[h] * decay[:, lanes] + _dot(pa_ref[rows, pl.ds(V_OFF + h * DV, DV)], kd[:, lanes], TN)
                state[h] = new
                st_ref[c, h] = new
        for c in range(cb):
            rows = pl.ds(c * CHUNK, CHUNK)
            for h in range(HEADS):
                qs = (pa_ref[rows, pl.ds(Q_OFF + h * DK, DK)].astype(F32) * Q_SCALE).astype(BF16)
                o = _dot(qs, st_ref[c, h].astype(BF16), NT)
                rs = lax.rsqrt(jnp.mean(o * o, axis=-1, keepdims=True) + EPS)
                rr = pa_ref[rows, pl.ds(R_OFF + h * DV, DV)].astype(F32)
                y_ref[rows, pl.ds(h * DV, DV)] = (o * rs * gn_ref[h] * (rr * _sig(rr))).astype(BF16)

    return pl.pallas_call(
        body, name=name, grid=(S // Tg,),
        in_specs=[
            pl.BlockSpec((Tg, 2 * QK + 2 * GV), lambda t: (t, 0)),
            pl.BlockSpec((Tg, LANES), lambda t: (t, 0)),
            pl.BlockSpec((LANES, QK), lambda t: (0, 0)),
            pl.BlockSpec((1, QK), lambda t: (0, 0)),
            pl.BlockSpec((HEADS, 1, DV), lambda t: (0, 0, 0)),
        ],
        out_specs=[
            pl.BlockSpec((Tg, GV), lambda t: (t, 0)),
            pl.BlockSpec((cb, HEADS, DV, DK), lambda t: (t, 0, 0, 0)),
        ],
        out_shape=[jax.ShapeDtypeStruct((S, GV), BF16),
                   jax.ShapeDtypeStruct((S // CHUNK, HEADS, DV, DK), F32)],
        scratch_shapes=[pltpu.VMEM((HEADS, DV, DK), F32)],
        compiler_params=_cp(("arbitrary",), VMEM_BIG),
    )(pa, alow, wg, bgate, gnorm)


def _gla_bwd_call(pa, alow, wg, bgate, gnorm, states, dy, *, name):
    S = pa.shape[0]
    Tg = min(GLA_TILE, S)
    cb = Tg // CHUNK
    nt = S // Tg

    def rev(t):
        return nt - 1 - t

    def body(pa_ref, al_ref, wg_ref, bg_ref, gn_ref, st_ref, prev_ref, dy_ref,
             dpa_ref, dl_ref, dgn_ref, dbg_ref, carry, pbuf):
        t = pl.program_id(0)

        @pl.when(t == 0)
        def _():
            carry[...] = jnp.zeros_like(carry)
            dgn_ref[...] = jnp.zeros_like(dgn_ref)
            dbg_ref[...] = jnp.zeros_like(dbg_ref)

        logit = _dot(al_ref[...], wg_ref[...], NN) + bg_ref[...]
        ls = _logsig(logit) * (1.0 / 16.0)
        sneg = 1.0 / (1.0 + jnp.exp(logit))
        tri = _tri(True)
        upper = _tri(False)
        first_tile = rev(t) == 0
        heads = range(HEADS)

        w, decay, kd = [], [], []
        for c in range(cb):
            rows = pl.ds(c * CHUNK, CHUNK)
            cum = _exact_mask_dot(tri, ls[c * CHUNK:(c + 1) * CHUNK])
            tot = cum[CHUNK - 1:CHUNK]
            w.append(jnp.exp(tot - cum))
            decay.append(jnp.exp(tot))
            kd.append(pa_ref[rows, pl.ds(K_OFF, QK)].astype(F32) * w[c])

        dgn = [jnp.zeros((1, DV), F32) for _ in heads]
        for c in range(cb):
            rows = pl.ds(c * CHUNK, CHUNK)
            for h in heads:
                gn = gn_ref[h]
                qs = (pa_ref[rows, pl.ds(Q_OFF + h * DK, DK)].astype(F32) * Q_SCALE).astype(BF16)
                st16 = st_ref[c, h].astype(BF16)
                o = _dot(qs, st16, NT)
                rs = lax.rsqrt(jnp.mean(o * o, axis=-1, keepdims=True) + EPS)
                oh = o * rs
                rr = pa_ref[rows, pl.ds(R_OFF + h * DV, DV)].astype(F32)
                sr = _sig(rr)
                dyv = dy_ref[rows, pl.ds(h * DV, DV)].astype(F32)
                dpa_ref[rows, pl.ds(R_OFF + h * DV, DV)] = (
                    dyv * oh * gn * sr * (1.0 + rr * (1.0 - sr))).astype(BF16)
                don = dyv * (rr * sr)
                dgn[h] = dgn[h] + jnp.sum(don * oh, axis=0, keepdims=True)
                doh = don * gn
                do16 = (rs * (doh - oh * jnp.mean(doh * oh, axis=-1, keepdims=True))).astype(BF16)
                dpa_ref[rows, pl.ds(Q_OFF + h * DK, DK)] = (_dot(do16, st16, NN) * Q_SCALE).astype(BF16)
                pbuf[c, h] = _dot(do16, qs, TN)
        for h in heads:
            dgn_ref[h] += dgn[h]

        dkd = [[None] * HEADS for _ in range(cb)]
        ddecay = [[None] * HEADS for _ in range(cb)]
        for c in reversed(range(cb)):
            rows = pl.ds(c * CHUNK, CHUNK)
            for h in heads:
                lanes = slice(h * DK, (h + 1) * DK)
                gt = pbuf[c, h] + carry[h]
                gt16 = gt.astype(BF16)
                dkd[c][h] = _dot(pa_ref[rows, pl.ds(V_OFF + h * DV, DV)], gt16, NN)
                dpa_ref[rows, pl.ds(V_OFF + h * DV, DV)] = _dot(kd[c][:, lanes].astype(BF16), gt16, NT).astype(BF16)
                if c > 0:
                    st_prev = st_ref[c - 1, h]
                else:
                    st_prev = jnp.where(first_tile, 0.0, prev_ref[0, h])
                ddecay[c][h] = jnp.sum(gt * st_prev, axis=0, keepdims=True)
                carry[h] = gt * decay[c][:, lanes]

        dbg = jnp.zeros((1, QK), F32)
        for c in range(cb):
            rows = pl.ds(c * CHUNK, CHUNK)
            dkd_c = jnp.concatenate(dkd[c], axis=1)
            dpa_ref[rows, pl.ds(K_OFF, QK)] = (dkd_c * w[c]).astype(BF16)
            e = dkd_c * kd[c]
            dtot = jnp.sum(e, axis=0, keepdims=True) + jnp.concatenate(ddecay[c], axis=1) * decay[c]
            dls = dtot - _exact_mask_dot(upper, e)
            dlogit = dls * (1.0 / 16.0) * sneg[c * CHUNK:(c + 1) * CHUNK]
            dl_ref[rows, :] = dlogit.astype(BF16)
            dbg = dbg + jnp.sum(dlogit, axis=0, keepdims=True)
        dbg_ref[...] += dbg

    return pl.pallas_call(
        body, name=name, grid=(nt,),
        in_specs=[
            pl.BlockSpec((Tg, 2 * QK + 2 * GV), lambda t: (rev(t), 0)),
            pl.BlockSpec((Tg, LANES), lambda t: (rev(t), 0)),
            pl.BlockSpec((LANES, QK), lambda t: (0, 0)),
            pl.BlockSpec((1, QK), lambda t: (0, 0)),
            pl.BlockSpec((HEADS, 1, DV), lambda t: (0, 0, 0)),
            pl.BlockSpec((cb, HEADS, DV, DK), lambda t: (rev(t), 0, 0, 0)),
            pl.BlockSpec((1, HEADS, DV, DK), lambda t: (jnp.maximum(rev(t) * cb - 1, 0), 0, 0, 0)),
            pl.BlockSpec((Tg, GV), lambda t: (rev(t), 0)),
        ],
        out_specs=[
            pl.BlockSpec((Tg, 2 * QK + 2 * GV), lambda t: (rev(t), 0)),
            pl.BlockSpec((Tg, QK), lambda t: (rev(t), 0)),
            pl.BlockSpec((HEADS, 1, DV), lambda t: (0, 0, 0)),
            pl.BlockSpec((1, QK), lambda t: (0, 0)),
        ],
        out_shape=[jax.ShapeDtypeStruct((S, 2 * QK + 2 * GV), BF16), jax.ShapeDtypeStruct((S, QK), BF16),
                   jax.ShapeDtypeStruct((HEADS, 1, DV), F32), jax.ShapeDtypeStruct((1, QK), F32)],
        scratch_shapes=[pltpu.VMEM((HEADS, DV, DK), F32), pltpu.VMEM((cb, HEADS, DV, DK), F32)],
        compiler_params=_cp(("arbitrary",), VMEM_BIG),
    )(pa, alow, wg, bgate, gnorm, states, states, dy)


SGU_TILE = 256


def _sgu_mask():
    r = lax.broadcasted_iota(jnp.int32, (SBLOCK, SBLOCK), 0)
    c = lax.broadcasted_iota(jnp.int32, (SBLOCK, SBLOCK), 1)
    return (c < CHUNK) | (r >= CHUNK)


def _ln_stats(vf):
    mu = jnp.mean(vf, axis=-1, keepdims=True)
    xc = vf - mu
    rs = lax.rsqrt(jnp.mean(xc * xc, axis=-1, keepdims=True) + EPS)
    return rs, xc * rs


def _sgu_fwd_call(ps, ln_g, ln_b, w_sp, b_sp, *, name):
    S = ps.shape[0]
    Ts = min(SGU_TILE, S)

    def body(ps_ref, lg_ref, lb_ref, w_ref, b_ref, y_ref):
        mask = _sgu_mask()
        for g in range(GROUPS):
            wm = jnp.where(mask, w_ref[g], 0.0).astype(BF16)
            for p in range(Ts // SBLOCK):
                rows = pl.ds(p * SBLOCK, SBLOCK)
                u = _gelu(ps_ref[rows, pl.ds(g * DG, DG)].astype(F32))
                _, xh = _ln_stats(_gelu(ps_ref[rows, pl.ds(D_MODEL + g * DG, DG)].astype(F32)))
                vn = xh * lg_ref[g] + lb_ref[g]
                mixed = _dot(wm, vn.astype(BF16), NN) + b_ref[g]
                y_ref[rows, pl.ds(g * DG, DG)] = (u * mixed).astype(BF16)

    full3 = lambda a, b, c: pl.BlockSpec((a, b, c), lambda t: (0, 0, 0))
    return pl.pallas_call(
        body, name=name, grid=(S // Ts,),
        in_specs=[pl.BlockSpec((Ts, 2 * D_MODEL), lambda t: (t, 0)),
                  full3(GROUPS, 1, DG), full3(GROUPS, 1, DG), full3(GROUPS, SBLOCK, SBLOCK), full3(GROUPS, SBLOCK, 1)],
        out_specs=pl.BlockSpec((Ts, D_MODEL), lambda t: (t, 0)),
        out_shape=jax.ShapeDtypeStruct((S, D_MODEL), BF16),
        compiler_params=_cp(("parallel",)),
    )(ps, ln_g, ln_b, w_sp, b_sp)


def _sgu_bwd_call(ps, ln_g, ln_b, w_sp, b_sp, dy, *, name):
    S = ps.shape[0]
    Ts = min(SGU_TILE, S)

    def body(ps_ref, lg_ref, lb_ref, w_ref, b_ref, dy_ref, ds_ref, dlg_ref, dlb_ref, dw_ref, db_ref):
        @pl.when(pl.program_id(0) == 0)
        def _():
            dlg_ref[...] = jnp.zeros_like(dlg_ref)
            dlb_ref[...] = jnp.zeros_like(dlb_ref)
            dw_ref[...] = jnp.zeros_like(dw_ref)
            db_ref[...] = jnp.zeros_like(db_ref)

        mask = _sgu_mask()
        for g in range(GROUPS):
            wm = jnp.where(mask, w_ref[g], 0.0).astype(BF16)
            lg = lg_ref[g]
            for p in range(Ts // SBLOCK):
                rows = pl.ds(p * SBLOCK, SBLOCK)
                su = ps_ref[rows, pl.ds(g * DG, DG)].astype(F32)
                sv = ps_ref[rows, pl.ds(D_MODEL + g * DG, DG)].astype(F32)
                u = _gelu(su)
                rs, xh = _ln_stats(_gelu(sv))
                vn16 = (xh * lg + lb_ref[g]).astype(BF16)
                mixed = _dot(wm, vn16, NN) + b_ref[g]
                dyv = dy_ref[rows, pl.ds(g * DG, DG)].astype(F32)
                ds_ref[rows, pl.ds(g * DG, DG)] = (dyv * mixed * _gelu_grad(su)).astype(BF16)
                dmix = dyv * u
                dmix16 = dmix.astype(BF16)
                db_ref[g] += jnp.sum(dmix, axis=-1, keepdims=True)
                dw_ref[g] += jnp.where(mask, _dot(dmix16, vn16, NT), 0.0)
                dvn = _dot(wm, dmix16, TN)
                dlg_ref[g] += jnp.sum(dvn * xh, axis=0, keepdims=True)
                dlb_ref[g] += jnp.sum(dvn, axis=0, keepdims=True)
                dxh = dvn * lg
                dvf = rs * (dxh - jnp.mean(dxh, axis=-1, keepdims=True)
                            - xh * jnp.mean(dxh * xh, axis=-1, keepdims=True))
                ds_ref[rows, pl.ds(D_MODEL + g * DG, DG)] = (dvf * _gelu_grad(sv)).astype(BF16)

    full3 = lambda a, b, c: pl.BlockSpec((a, b, c), lambda t: (0, 0, 0))
    return pl.pallas_call(
        body, name=name, grid=(S // Ts,),
        in_specs=[pl.BlockSpec((Ts, 2 * D_MODEL), lambda t: (t, 0)),
                  full3(GROUPS, 1, DG), full3(GROUPS, 1, DG), full3(GROUPS, SBLOCK, SBLOCK), full3(GROUPS, SBLOCK, 1),
                  pl.BlockSpec((Ts, D_MODEL), lambda t: (t, 0))],
        out_specs=[pl.BlockSpec((Ts, 2 * D_MODEL), lambda t: (t, 0)),
                   full3(GROUPS, 1, DG), full3(GROUPS, 1, DG), full3(GROUPS, SBLOCK, SBLOCK), full3(GROUPS, SBLOCK, 1)],
        out_shape=[jax.ShapeDtypeStruct((S, 2 * D_MODEL), BF16),
                   jax.ShapeDtypeStruct((GROUPS, 1, DG), F32), jax.ShapeDtypeStruct((GROUPS, 1, DG), F32),
                   jax.ShapeDtypeStruct((GROUPS, SBLOCK, SBLOCK), F32),
                   jax.ShapeDtypeStruct((GROUPS, SBLOCK, 1), F32)],
        compiler_params=_cp(("arbitrary",)),
    )(ps, ln_g, ln_b, w_sp, b_sp, dy)


def _position():
    return lax.axis_index("x"), lax.axis_index("y"), lax.axis_index("c")


def _gather_copies(srcs, dsts, send_sems, recv_sems, local_sems):
    x, y, c = _position()
    me, sibling = (x, y, c), (x, y, 1 - c)
    chips = [(1 - x, y), (x, 1 - y), (1 - x, 1 - y)]
    n = len(srcs)

    def slab(a, block):
        px, py, pc = block
        return dsts[a].at[4 * px + 2 * py + pc]

    def copy(a, k, block, to, src=None):
        return pltpu.make_async_remote_copy(
            src_ref=slab(a, block) if src is None else src, dst_ref=slab(a, block),
            send_sem=send_sems.at[7 * a + k], recv_sem=recv_sems.at[7 * a + k], device_id=to, device_id_type=MESH)

    mine = [pltpu.make_async_copy(srcs[a], slab(a, me), local_sems.at[a]) for a in range(n)]
    for cp in mine:
        cp.start()
    first = []
    for a in range(n):
        first.append(copy(a, 0, me, sibling, src=srcs[a]))
        first += [copy(a, 1 + j, me, (*chip, c), src=srcs[a]) for j, chip in enumerate(chips)]
    for cp in first:
        cp.start()
    passed = []
    for j, chip in enumerate(chips):
        for a in range(n):
            copy(a, 1 + j, (*chip, c), me).wait_recv()
            cp = copy(a, 4 + j, (*chip, c), sibling)
            cp.start()
            passed.append(cp)
    for a in range(n):
        copy(a, 0, sibling, me).wait_recv()
        for j, chip in enumerate(chips):
            copy(a, 4 + j, (*chip, 1 - c), me).wait_recv()
    for cp in first + passed:
        cp.wait_send()
    for cp in mine:
        cp.wait()


def _all_gather_hbm(shards, *, name):
    n = len(shards)

    def body(*refs):
        srcs, dsts = refs[:n], refs[n:2 * n]
        send_sems, recv_sems, local_sems = refs[2 * n:]
        _gather_copies(srcs, dsts, send_sems, recv_sems, local_sems)

    return pl.pallas_call(
        body, name=name,
        in_specs=[ANY] * n, out_specs=[ANY] * n,
        out_shape=[jax.ShapeDtypeStruct((N_DEV, *s.shape), s.dtype) for s in shards],
        scratch_shapes=_comm_sems(n),
    )(*shards)


def _all_reduce_small(part, *, name):
    R, W = part.shape

    def body(x_ref, out_ref, gathered, send_sems, recv_sems, local_sems):
        _gather_copies([x_ref], [gathered], send_sems, recv_sems, local_sems)
        acc = gathered[0]
        for d in range(1, N_DEV):
            acc = acc + gathered[d]
        out_ref[...] = acc

    return pl.pallas_call(
        body, name=name,
        in_specs=[VMEM_SPEC], out_specs=VMEM_SPEC,
        out_shape=jax.ShapeDtypeStruct((R, W), F32),
        scratch_shapes=[pltpu.VMEM((N_DEV, R, W), F32),
                        pltpu.SemaphoreType.DMA((7,)), pltpu.SemaphoreType.DMA((7,)), pltpu.SemaphoreType.DMA((1,))],
    )(part)


FLIPS = [(fx, fy, fc) for fx in (0, 1) for fy in (0, 1) for fc in (0, 1)][1:]


def _scatter_copies(srcs, dsts, send_sems, recv_sems, local_sems):
    n = len(srcs)
    x, y, c = _position()
    me = 4 * x + 2 * y + c
    mine = [pltpu.make_async_copy(srcs[a].at[me], dsts[a].at[me], local_sems.at[a]) for a in range(n)]
    for cp in mine:
        cp.start()
    copies = []
    for k, (fx, fy, fc) in enumerate(FLIPS):
        tx = 1 - x if fx else x
        ty = 1 - y if fy else y
        tc = 1 - c if fc else c
        peer = 4 * tx + 2 * ty + tc
        for a in range(n):
            cp = pltpu.make_async_remote_copy(
                src_ref=srcs[a].at[peer], dst_ref=dsts[a].at[me],
                send_sem=send_sems.at[7 * a + k], recv_sem=recv_sems.at[7 * a + k],
                device_id=(tx, ty, tc), device_id_type=MESH)
            cp.start()
            copies.append(cp)
    for cp in copies:
        cp.wait()
    for cp in mine:
        cp.wait()


def _comm_sems(n):
    return [pltpu.SemaphoreType.DMA((7 * n,)), pltpu.SemaphoreType.DMA((7 * n,)), pltpu.SemaphoreType.DMA((n,))]


def _scatter_blocks(parts, *, name):
    n = len(parts)

    def body(*refs):
        _scatter_copies(refs[:n], refs[n:2 * n], *refs[2 * n:])

    return pl.pallas_call(
        body, name=name,
        in_specs=[ANY] * n, out_specs=[ANY] * n,
        out_shape=[jax.ShapeDtypeStruct(p.shape, p.dtype) for p in parts],
        scratch_shapes=_comm_sems(n),
    )(*parts)


def _handshake(peers):
    barrier = pltpu.get_barrier_semaphore()
    for peer in peers:
        pl.semaphore_signal(barrier, inc=1, device_id=peer, device_id_type=MESH)
    pl.semaphore_wait(barrier, len(peers))


def _sequencer_call(arrays, out_types, copies_fn, peers_fn, *, name, collective_id):
    n = len(arrays)
    srcs = [jax.new_ref(a, memory_space=pltpu.MemorySpace.HBM) for a in arrays]
    dsts = [jax.empty_ref(t, memory_space=pltpu.MemorySpace.HBM) for t in out_types]

    @pl.kernel(mesh=plsc.ScalarSubcoreMesh(axis_name="sequencer", num_cores=1), name=name,
               scratch_types=_comm_sems(n), compiler_params=pltpu.CompilerParams(collective_id=collective_id))
    def launch(send_sems, recv_sems, local_sems):
        _handshake(peers_fn())
        copies_fn(srcs, dsts, send_sems, recv_sems, local_sems)

    launch()
    return [d[...] for d in dsts]


def _all_other_devices():
    x, y, c = _position()
    return [(1 - x if fx else x, 1 - y if fy else y, 1 - c if fc else c) for fx, fy, fc in FLIPS]


def _gather_peers():
    x, y, c = _position()
    return [(x, y, 1 - c), (1 - x, y, c), (x, 1 - y, c), (1 - x, 1 - y, c)]


def _scatter_blocks_async(parts, *, name, collective_id):
    return _sequencer_call(parts, [jax.ShapeDtypeStruct(p.shape, p.dtype) for p in parts],
                           _scatter_copies, _all_other_devices, name=name, collective_id=collective_id)


def _all_gather_async(shards, *, name, collective_id):
    return _sequencer_call(shards, [jax.ShapeDtypeStruct((N_DEV, *s.shape), s.dtype) for s in shards],
                           _gather_copies, _gather_peers, name=name, collective_id=collective_id)


def _adamw_math(w, g, m, v):
    m = ADAM_B1 * m + (1.0 - ADAM_B1) * g
    v = ADAM_B2 * v + (1.0 - ADAM_B2) * (g * g)
    m_hat = m / (1.0 - ADAM_B1 ** ADAM_STEP)
    v_hat = v / (1.0 - ADAM_B2 ** ADAM_STEP)
    delta = -ADAM_LR * (m_hat / (jnp.sqrt(v_hat) + ADAM_EPS) + ADAM_WD * w)
    return delta, m, v


def _adamw_reduce_call(recv, w, m, v, *, name, T=128):
    R, W = w.shape
    if R % T == 0:
        tr, tw = T, W
    elif (R // 2) % 16 == 0:
        tr, tw = R // 2, W
    else:
        tr, tw = R, 2 * LANES

    def body(p_ref, w_ref, m_ref, v_ref, g_out, d_out, m_out, v_out):
        g = p_ref[0].astype(F32)
        for d in range(1, N_DEV):
            g = g + p_ref[d].astype(F32)
        g_out[...] = g
        d_out[...], m_out[...], v_out[...] = _adamw_math(w_ref[...], g, m_ref[...], v_ref[...])

    row = pl.BlockSpec((tr, tw), lambda i, j: (i, j))
    out = jax.ShapeDtypeStruct((R, W), F32)
    return pl.pallas_call(
        body, name=name, grid=(R // tr, W // tw),
        in_specs=[pl.BlockSpec((N_DEV, tr, tw), lambda i, j: (0, i, j)), row, row, row],
        out_specs=[row] * 4, out_shape=[out] * 4,
        compiler_params=_cp(("parallel", "parallel"), VMEM_BIG),
    )(recv, w, m, v)


def _adamw_reduce_rows_call(recv, w, m, v, *, name):
    G = recv.shape[2] // LANES
    R = w.shape[0] // G
    T = 128

    def body(p_ref, w_ref, m_ref, v_ref, g_out, d_out, m_out, v_out):
        for k in range(G):
            lanes = pl.ds(k * LANES, LANES)
            mine = pl.ds(k, T, stride=G)
            g = p_ref[0, :, lanes].astype(F32)
            for d in range(1, N_DEV):
                g = g + p_ref[d, :, lanes].astype(F32)
            g_out[mine, :] = g
            d_out[mine, :], m_out[mine, :], v_out[mine, :] = _adamw_math(w_ref[mine, :], g, m_ref[mine, :], v_ref[mine, :])

    rows = pl.BlockSpec((T * G, LANES), lambda i: (i, 0))
    out = jax.ShapeDtypeStruct(w.shape, F32)
    return pl.pallas_call(
        body, name=name, grid=(pl.cdiv(R, T),),
        in_specs=[pl.BlockSpec((N_DEV, T, G * LANES), lambda i: (0, i, 0)), rows, rows, rows],
        out_specs=[rows] * 4, out_shape=[out] * 4,
        compiler_params=_cp(("parallel",), VMEM_BIG),
    )(recv, w, m, v)


def _adamw_small_call(w, g, m, v, *, name):
    def body(w_ref, g_ref, m_ref, v_ref, d_out, m_out, v_out):
        d_out[...], m_out[...], v_out[...] = _adamw_math(w_ref[...], g_ref[...], m_ref[...], v_ref[...])

    out = jax.ShapeDtypeStruct(w.shape, F32)
    return pl.pallas_call(body, name=name, in_specs=[VMEM_SPEC] * 4, out_specs=[VMEM_SPEC] * 3,
                          out_shape=[out] * 3)(w, g, m, v)


def _tile_rows(n_elems):
    return -(-n_elems // (8 * LANES)) * 8


def _pack_rows(parts, rows):
    pieces = []
    for p in parts:
        q = p.reshape(-1, LANES)
        pieces.append(jnp.pad(q, ((0, _tile_rows(p.size) - q.shape[0]), (0, 0))))
    buf = jnp.concatenate(pieces, axis=0)
    return jnp.pad(buf, ((0, rows - buf.shape[0]), (0, 0)))


def _unpack_rows(buf, shapes):
    out, off = [], 0
    for shp in shapes:
        n = 1
        for s in shp:
            n *= s
        out.append(buf[off:off + n // LANES].reshape(shp))
        off += _tile_rows(n)
    return out


def _join_blocks_call(blocks, *, name, tw=2 * LANES):
    nb, r, w = blocks.shape

    def body(b_ref, o_ref):
        for d in range(nb):
            o_ref[d * r:(d + 1) * r, :] = b_ref[d]

    return pl.pallas_call(
        body, name=name, grid=(w // tw,),
        in_specs=[pl.BlockSpec((nb, r, tw), lambda j: (0, 0, j))],
        out_specs=pl.BlockSpec((nb * r, tw), lambda j: (0, j)),
        out_shape=jax.ShapeDtypeStruct((nb * r, w), blocks.dtype),
        compiler_params=_cp(("parallel",)),
    )(blocks)


def _split_blocks_call(parts, *, name, tw=2 * LANES):
    w = parts[0].shape[1]
    starts = [0]
    for q in parts:
        starts.append(starts[-1] + q.shape[0])
    assert starts[-1] == N_DEV * IN_BLK

    def body(*refs):
        o_ref = refs[-1]
        for d in range(N_DEV):
            lo, hi = IN_BLK * d, IN_BLK * (d + 1)
            for ref, c0, c1 in zip(refs[:-1], starts[:-1], starts[1:]):
                s0, e0 = max(lo, c0), min(hi, c1)
                if s0 < e0:
                    o_ref[d, s0 - lo:e0 - lo, :] = ref[s0 - c0:e0 - c0, :]

    return pl.pallas_call(
        body, name=name, grid=(w // tw,),
        in_specs=[pl.BlockSpec((q.shape[0], tw), lambda j: (0, j)) for q in parts],
        out_specs=pl.BlockSpec((N_DEV, IN_BLK, tw), lambda j: (0, 0, j)),
        out_shape=jax.ShapeDtypeStruct((N_DEV, IN_BLK, w), parts[0].dtype),
        compiler_params=_cp(("parallel",)),
    )(*parts)


def _local_step(x, target, W, scatter, finish):
    g1, g2, g3, g4 = [W[n].reshape(1, D_MODEL) for n in ("norm_pre_mix", "norm_post_mix", "norm_pre_ffn", "norm_post_ffn")]
    w_in_t = W["w_in_t"]
    wfi, wfo = W["w_ffn_in"], W["w_ffn_out"].reshape(4, FF_BLK, D_MODEL)
    wg = jnp.pad(W["w_gate_up"], ((0, LANES - RANK), (0, 0))).astype(BF16)
    bgate = W["b_gate"].reshape(1, QK)
    gnorm = W["gla_norm"].reshape(HEADS, 1, DV)
    ln_g = W["sgu_ln_g"].reshape(GROUPS, 1, DG)
    ln_b = W["sgu_ln_b"].reshape(GROUPS, 1, DG)
    w_sp = W["w_spatial"]
    b_sp = W["b_spatial"].reshape(GROUPS, SBLOCK, 1)

    a, pa, alow, ps, pg = _in_proj_call(x, g1, w_in_t, name="in_proj")
    y_gla, states = _gla_fwd_call(pa, alow, wg, bgate, gnorm, name="gla_fwd")
    y_sgu = _sgu_fwd_call(ps, ln_g, ln_b, w_sp, b_sp, name="sgu_fwd")
    t1, t2, merged, mix, x1, h = _mixer_tail_call(y_gla, y_sgu, pg, x, W["w_branch_gla"], W["w_branch_sgu"],
                                                  W["w_out"], g2, g3, name="mixer_tail")
    gu, z = _ffn_in_call(h, wfi, name="ffn_in")
    loss, dx2, dy, dg4 = _ffn_out_loss_call(z, wfo, x1, target, g4, name="ffn_out_loss")

    grads = {"norm_post_ffn": dg4}
    done = {}
    dgu = _ffn_out_bwd_call(dy, wfo, gu, name="ffn_out_bwd")
    dw_ffn_out = _weight_grads_call([z, dy], [(0, 1)], name="d_ffn_out_w")[0].reshape(N_DEV, D_FF // N_DEV, D_MODEL)
    dw_ffn_in = _ffn_in_grad_call(h, dgu, name="d_ffn_in_w")
    dgu, dw_ffn_out, dw_ffn_in = lax.optimization_barrier((dgu, dw_ffn_out, dw_ffn_in))
    ffn_received = scatter(("w_ffn_out", "w_ffn_in"), [dw_ffn_out, dw_ffn_in])
    dx1, dmix, grads["norm_pre_ffn"], grads["norm_post_mix"] = _ffn_in_bwd_call(dgu, wfi, dx2, x1, mix, g3, g2, name="ffn_in_bwd")
    dt1, dt2, dpg, dy_gla, dy_sgu = _mixer_bwd_call(dmix, t1, t2, pg, W["w_out"], W["w_branch_gla"], W["w_branch_sgu"],
                                                    name="mixer_bwd")
    rows = D_MODEL // N_DEV
    mixer_grads = _weight_grads_call([merged, dmix, y_gla, dt1, y_sgu, dt2], [(0, 1), (2, 3), (4, 5)], name="d_mixer_w")
    dy_gla, dy_sgu, mixer_grads = lax.optimization_barrier((dy_gla, dy_sgu, mixer_grads))
    mixer_received = scatter(("w_out", "w_branch_gla", "w_branch_sgu"),
                             [g.reshape(N_DEV, rows, D_MODEL) for g in mixer_grads])
    dpa, dlogit, dgn, dbg = _gla_bwd_call(pa, alow, wg, bgate, gnorm, states, dy_gla, name="gla_bwd")
    ffn_done, dpa, dlogit = lax.optimization_barrier((finish(ffn_received), dpa, dlogit))
    done.update(ffn_done)
    dps, dlg, dlb, dwsp, dbsp = _sgu_bwd_call(ps, ln_g, ln_b, w_sp, b_sp, dy_sgu, name="sgu_bwd")
    mixer_done, dps = lax.optimization_barrier((finish(mixer_received), dps))
    done.update(mixer_done)
    dlow = _mm(dlogit, wg, "nt", BF16, name="d_gate_up_x")
    dw_a, dw_low = _weight_grads_call([a, dpa, dlow], [(1, 0), (2, 0)], name="d_in_w_qkvr")
    dw_s, dw_g = _weight_grads_call([a, dps, dpg], [(1, 0), (2, 0)], name="d_in_w_sgu_gates")
    dw_in = _split_blocks_call([dw_a, dw_low[:RANK], dw_s, dw_g], name="split_w_in_grad")
    in_received = scatter(("w_in",), [dw_in])
    dwg = _mm(alow, dlogit, "tn", F32, name="d_gate_up_w")
    grad_x, grads["norm_pre_mix"] = _in_proj_bwd_call(dpa, dlow, dps, dpg, w_in_t, x, dx1, g1, name="in_proj_bwd")

    grads["w_gate_up"] = dwg[:RANK]
    grads["b_gate"] = dbg
    grads["gla_norm"] = dgn
    grads["sgu_ln_g"] = dlg
    grads["sgu_ln_b"] = dlb
    grads["w_spatial"] = dwsp
    grads["b_spatial"] = dbsp
    done.update(finish(in_received))
    return loss, grad_x, grads, done


WEIGHTS = ("norm_pre_mix", "w_in", "w_gate_up", "b_gate", "gla_norm", "sgu_ln_g", "sgu_ln_b", "w_spatial",
           "b_spatial", "w_branch_gla", "w_branch_sgu", "w_out", "norm_post_mix", "norm_pre_ffn", "w_ffn_in",
           "w_ffn_out", "norm_post_ffn")
BIG = ("w_in", "w_branch_gla", "w_branch_sgu", "w_out", "w_ffn_in", "w_ffn_out")
MIXER = ("w_branch_gla", "w_branch_sgu", "w_out")
FFN = ("w_ffn_in", "w_ffn_out")
COLUMN_SHARDED = ("w_in", "w_ffn_in")
SMALL = tuple(n for n in WEIGHTS if n not in BIG)
SMALL_SHARDED = ("w_gate_up", "gla_norm", "sgu_ln_g", "sgu_ln_b")
SMALL_FULL = {"norm_pre_mix": (1024,), "w_gate_up": (16, 512), "b_gate": (512,), "gla_norm": (4, 256),
              "sgu_ln_g": (4, 256), "sgu_ln_b": (4, 256), "w_spatial": (4, 128, 128), "b_spatial": (4, 128),
              "norm_post_mix": (1024,), "norm_pre_ffn": (1024,), "norm_post_ffn": (1024,)}
SMALL_GRAD_ROWS = 648
SMALL_STATE_ROWS = 600
SMALL_GATHER_ROWS = 32


def kernel(x, norm_pre_mix, w_in, w_gate_up, b_gate, gla_norm, sgu_ln_g, sgu_ln_b, w_spatial, b_spatial, w_branch_gla, w_branch_sgu, w_out, norm_post_mix, norm_pre_ffn, w_ffn_in, w_ffn_out, norm_post_ffn, loss_target, m_norm_pre_mix, m_w_in, m_w_gate_up, m_b_gate, m_gla_norm, m_sgu_ln_g, m_sgu_ln_b, m_w_spatial, m_b_spatial, m_w_branch_gla, m_w_branch_sgu, m_w_out, m_norm_post_mix, m_norm_pre_ffn, m_w_ffn_in, m_w_ffn_out, m_norm_post_ffn, v_norm_pre_mix, v_w_in, v_w_gate_up, v_b_gate, v_gla_norm, v_sgu_ln_g, v_sgu_ln_b, v_w_spatial, v_b_spatial, v_w_branch_gla, v_w_branch_sgu, v_w_out, v_norm_post_mix, v_norm_pre_ffn, v_w_ffn_in, v_w_ffn_out, v_norm_post_ffn):
    given = dict(locals())
    def local(a, n):
        return a[0].T if n in COLUMN_SHARDED else a[0]

    w = {n: local(given[n], n) for n in WEIGHTS}
    m = {n: local(given["m_" + n], n) for n in WEIGHTS}
    v = {n: local(given["v_" + n], n) for n in WEIGHTS}
    xs, target = x[0], loss_target[0]
    me = 4 * lax.axis_index("x") + 2 * lax.axis_index("y") + lax.axis_index("c")

    small_shard = _pack_rows([w[n] for n in SMALL_SHARDED], SMALL_GATHER_ROWS)
    first = _all_gather_hbm([w["w_in"].astype(BF16), small_shard], name="gather_w_in")
    rest, first = lax.optimization_barrier(([w[n].astype(BF16) for n in MIXER + FFN], first))
    rest_blocks = _all_gather_async(rest, name="gather_rest", collective_id=1)
    W = {n: w[n] for n in SMALL if n not in SMALL_SHARDED}
    blocks = dict(zip(MIXER + FFN, rest_blocks))
    for n in ("w_branch_gla", "w_branch_sgu", "w_out", "w_ffn_out"):
        W[n] = blocks[n].reshape(-1, D_MODEL)
    W["w_ffn_in"] = blocks["w_ffn_in"]
    W["w_in_t"] = _join_blocks_call(first[0], name="join_w_in")
    small_blocks = first[1]
    off = 0
    for n in SMALL_SHARDED:
        r, c = w[n].shape
        blk = small_blocks[:, off:off + r * c // LANES].reshape(N_DEV, r, c)
        W[n] = blk.transpose(1, 0, 2).reshape(r, N_DEV * c)
        off += _tile_rows(r * c)

    scatter_ids = iter((3, 4, 5))

    def scatter(names, parts):
        got = _scatter_blocks_async(parts, name="scatter_" + "_".join(names), collective_id=next(scatter_ids))
        return dict(zip(names, got))

    def finish(received):
        out = {}
        for n, r in received.items():
            if n == "w_in":
                groups = (IN_BLK * D_MODEL // LANES, LANES)
                res = _adamw_reduce_rows_call(r, *[s[n].reshape(groups) for s in (w, m, v)], name="adamw_" + n)
                out[n] = [o.reshape(IN_BLK, D_MODEL) for o in res]
            else:
                out[n] = _adamw_reduce_call(r, w[n], m[n], v[n], name="adamw_" + n)
        return out

    loss_part, grad_x, grads, big_done = _local_step(xs, target, W, scatter, finish)

    small_part = jnp.concatenate([_pack_rows([grads[n] for n in SMALL], SMALL_GRAD_ROWS),
                                  jnp.broadcast_to(loss_part, (8, LANES))], axis=0)
    small_sum = _all_reduce_small(small_part, name="reduce_small")
    loss = small_sum[SMALL_GRAD_ROWS, 0]
    g_small = dict(zip(SMALL, _unpack_rows(small_sum, [SMALL_FULL[n] for n in SMALL])))
    for n in SMALL_SHARDED:
        c = w[n].shape[1]
        g_small[n] = lax.dynamic_slice_in_dim(g_small[n], me * c, c, axis=1)

    g_big, d_big, m_big, v_big = [{n: big_done[n][i] for n in BIG} for i in range(4)]
    small_shapes = [w[n].shape for n in SMALL]
    small_state = [_pack_rows([s[n] for n in SMALL], SMALL_STATE_ROWS) for s in (w, g_small, m, v)]
    small_out = _adamw_small_call(*small_state, name="adamw_small")
    d_small, m_small, v_small = [dict(zip(SMALL, _unpack_rows(o, small_shapes))) for o in small_out]

    def pick(big, sml):
        return [(big[n].T if n in COLUMN_SHARDED else big[n] if n in BIG else sml[n])[None] for n in WEIGHTS]

    return (loss, grad_x[None], *pick(g_big, g_small), *pick(d_big, d_small), *pick(m_big, m_small),
            *pick(v_big, v_small))
```

```python
import jax
import jax.numpy as jnp
from jax import lax
from jax.experimental import pallas as pl
from jax.experimental.pallas import tpu as pltpu
from jax.experimental.pallas import tpu_sc as plsc

F32 = jnp.float32
BF16 = jnp.bfloat16

D_MODEL = 1024
N_DEV = 8
CHUNK = 64
HEADS = 4
DK = 128
DV = 256
QK = HEADS * DK
GV = HEADS * DV
RANK = 16
GROUPS = 4
SBLOCK = 128
DG = 256
D_FF = 2816
FF_BLK = 704
EPS = 1e-6
Q_SCALE = DK ** -0.5
LANES = 128
VMEM_BIG = 48 * 1024 * 1024

D_IN = 7184
IN_BLK = 898
A_COLS = (0, 3072)
LOW_COLS = (3072, 3088)
S_COLS = (3088, 5136)
G_COLS = (5136, 7184)

ADAM_LR = 0.001
ADAM_B1 = 0.9
ADAM_B2 = 0.999
ADAM_EPS = 1e-08
ADAM_WD = 0.01
ADAM_STEP = 10

MESH = pl.DeviceIdType.MESH
ANY = pl.BlockSpec(memory_space=pl.ANY)
VMEM_SPEC = pl.BlockSpec(memory_space=pltpu.VMEM)


def _cp(sem=None, vmem=None):
    return pltpu.CompilerParams(dimension_semantics=sem, vmem_limit_bytes=vmem)


def _sig(x):
    return 0.5 * jnp.tanh(0.5 * x) + 0.5


def _gelu(x):
    c = 0.7978845608028654
    t = jnp.tanh(c * (x + 0.044715 * x * x * x))
    return 0.5 * x * (1.0 + t)


def _gelu_grad(x):
    c = 0.7978845608028654
    x2 = x * x
    t = jnp.tanh(c * (x + 0.044715 * x * x2))
    return 0.5 * (1.0 + t) + 0.5 * x * (1.0 - t * t) * c * (1.0 + 3.0 * 0.044715 * x2)


def _logsig(x):
    return jnp.minimum(x, 0.0) - jnp.log1p(jnp.exp(-jnp.abs(x)))


def _dot(a, b, dims):
    return lax.dot_general(a, b, (dims, ((), ())), preferred_element_type=F32)


NN = ((1,), (0,))
NT = ((1,), (1,))
TN = ((0,), (0,))


def _exact_mask_dot(mask_bf16, x):
    hi = x.astype(BF16)
    r1 = x - hi.astype(F32)
    mid = r1.astype(BF16)
    lo = (r1 - mid.astype(F32)).astype(BF16)
    return _dot(mask_bf16, hi, NN) + _dot(mask_bf16, mid, NN) + _dot(mask_bf16, lo, NN)


def _pick_tile(dim, target):
    if dim <= target:
        return dim
    best = None
    for t in range(LANES, int(1.4 * target) + 1, LANES):
        if dim % t == 0:
            best = t
    assert best is not None, (dim, target)
    return best


def _mm_call(a, b, *, name, grid, a_spec, b_spec, o_spec, out_shape, dims, acc_shape):
    nk = grid[2]

    def body(a_ref, b_ref, o_ref, *acc):
        part = _dot(a_ref[...], b_ref[...], dims)
        if nk == 1:
            o_ref[...] = part.astype(o_ref.dtype)
        else:
            acc_ref = acc[0]
            k = pl.program_id(2)

            @pl.when(k == 0)
            def _():
                acc_ref[...] = part

            @pl.when(k > 0)
            def _():
                acc_ref[...] += part

            @pl.when(k == nk - 1)
            def _():
                o_ref[...] = acc_ref[...].astype(o_ref.dtype)

    return pl.pallas_call(
        body, name=name, grid=grid, in_specs=[a_spec, b_spec], out_specs=o_spec, out_shape=out_shape,
        scratch_shapes=[] if nk == 1 else [pltpu.VMEM(acc_shape, F32)],
        compiler_params=_cp(("parallel", "parallel", "arbitrary"), VMEM_BIG),
    )(a, b)


def _mm(a, b, mode, out_dtype, *, name, tm=512, tn=1024, tk=1024):
    if mode == "nn":
        (M, K), (_, N) = a.shape, b.shape
    elif mode == "nt":
        (M, K), (N, _) = a.shape, b.shape
    else:
        (K, M), (_, N) = a.shape, b.shape
    tm, tn, tk = _pick_tile(M, tm), _pick_tile(N, tn), _pick_tile(K, tk)
    if mode == "nn":
        a_spec = pl.BlockSpec((tm, tk), lambda i, j, k: (i, k))
        b_spec = pl.BlockSpec((tk, tn), lambda i, j, k: (k, j))
        dims = NN
    elif mode == "nt":
        a_spec = pl.BlockSpec((tm, tk), lambda i, j, k: (i, k))
        b_spec = pl.BlockSpec((tn, tk), lambda i, j, k: (j, k))
        dims = NT
    else:
        a_spec = pl.BlockSpec((tk, tm), lambda i, j, k: (k, i))
        b_spec = pl.BlockSpec((tk, tn), lambda i, j, k: (k, j))
        dims = TN
    return _mm_call(a, b, name=name, grid=(M // tm, N // tn, K // tk), a_spec=a_spec, b_spec=b_spec,
                    o_spec=pl.BlockSpec((tm, tn), lambda i, j, k: (i, j)),
                    out_shape=jax.ShapeDtypeStruct((M, N), out_dtype), dims=dims, acc_shape=(tm, tn))


ROW_TILE = 512
SUB_ROWS = 256


def _sub_tiles(T):
    return [pl.ds(r0, min(SUB_ROWS, T)) for r0 in range(0, T, SUB_ROWS)]


def _rt(T, W, c=0):
    return pl.BlockSpec((T, W), lambda i: (i, c))


def _rt3(nb, T, W):
    return pl.BlockSpec((nb, T, W), lambda i: (0, i, 0))


def _res(shape):
    nd = len(shape)
    return pl.BlockSpec(tuple(shape), lambda i: (0,) * nd, pipeline_mode=pl.Buffered(1))


def _acc(shape):
    nd = len(shape)
    return pl.BlockSpec(tuple(shape), lambda i: (0,) * nd, pipeline_mode=pl.Buffered(1))


def _nbytes(shape, dtype):
    n = jnp.dtype(dtype).itemsize
    for s in shape:
        n *= s
    return n


def _vmem_limit(tiles, resident, temps=16 * 1024 * 1024):
    need = 2 * sum(_nbytes(s, d) for s, d in tiles) + sum(_nbytes(s, d) for s, d in resident) + temps
    return min(need, 60 * 1024 * 1024)


def _tok_call(body, *, name, S, T, ins, outs, semantics="parallel"):
    tiles = [(spec.block_shape, a.dtype) for a, spec, kind in ins + outs if kind == "tile"]
    resident = [(a.shape, a.dtype) for a, spec, kind in ins + outs if kind == "res"]
    return pl.pallas_call(
        body, name=name, grid=(S // T,),
        in_specs=[spec for _, spec, _ in ins], out_specs=[spec for _, spec, _ in outs],
        out_shape=[jax.ShapeDtypeStruct(a.shape, a.dtype) for a, _, _ in outs],
        compiler_params=_cp((semantics,), _vmem_limit(tiles, resident)),
    )(*[a for a, _, _ in ins])


def _tile(a, spec):
    return (a, spec, "tile")


def _whole(a):
    return (a, _res(a.shape), "res")


def _out_tile(shape, dtype, spec):
    return (jax.ShapeDtypeStruct(shape, dtype), spec, "tile")


def _out_acc(shape, dtype=F32):
    return (jax.ShapeDtypeStruct(shape, dtype), _acc(shape), "res")


def _rms_stats(x):
    r = lax.rsqrt(jnp.mean(x * x, axis=-1, keepdims=True) + EPS)
    return r, x * r


def _rms_bwd(xh, r, g, dy):
    dxh = dy * g
    dx = r * (dxh - xh * jnp.mean(dxh * xh, axis=-1, keepdims=True))
    dg = jnp.sum(dy * xh, axis=0, keepdims=True)
    return dx, dg


def _accum(ref, val):
    @pl.when(pl.program_id(0) == 0)
    def _():
        ref[...] = val

    @pl.when(pl.program_id(0) > 0)
    def _():
        ref[...] += val


def _dot_rows_t(a, w_ref, row0, o_ref, chunk=1024):
    n = o_ref.shape[1]
    for n0 in range(0, n, chunk):
        n1 = min(n, n0 + chunk)
        o_ref[:, n0:n1] = _dot(a, w_ref[row0 + n0:row0 + n1, :], NT).astype(o_ref.dtype)


def _in_proj_call(x, g1, w_in_t, *, name, T=ROW_TILE):
    S = x.shape[0]

    def body(x_ref, g_ref, w_ref, a_ref, pa_ref, al_ref, ps_ref, pg_ref):
        _, xh = _rms_stats(x_ref[...])
        a = (xh * g_ref[...]).astype(BF16)
        a_ref[...] = a
        _dot_rows_t(a, w_ref, A_COLS[0], pa_ref)
        _dot_rows_t(a, w_ref, LOW_COLS[0], al_ref)
        _dot_rows_t(a, w_ref, S_COLS[0], ps_ref)
        _dot_rows_t(a, w_ref, G_COLS[0], pg_ref)

    widths = (D_MODEL, A_COLS[1] - A_COLS[0], LANES, S_COLS[1] - S_COLS[0], G_COLS[1] - G_COLS[0])
    return _tok_call(
        body, name=name, S=S, T=T,
        ins=[_tile(x, _rt(T, D_MODEL)), _whole(g1), _whole(w_in_t)],
        outs=[_out_tile((S, w), BF16, _rt(T, w)) for w in widths])


def _mixer_tail_call(y_gla, y_sgu, pg, x, w_bg, w_bs, w_out, g2, g3, *, name, T=ROW_TILE):
    S = x.shape[0]

    def body(yg_ref, ys_ref, pg_ref, x_ref, wbg_ref, wbs_ref, wo_ref, g2_ref, g3_ref,
             t1_ref, t2_ref, mg_ref, mix_ref, x1_ref, h_ref):
        for rows in _sub_tiles(T):
            t1 = _dot(yg_ref[rows, :], wbg_ref[...], NN)
            t2 = _dot(ys_ref[rows, :], wbs_ref[...], NN)
            t1_ref[rows, :] = t1.astype(BF16)
            t2_ref[rows, :] = t2.astype(BF16)
            sg = _sig(pg_ref[rows, pl.ds(0, D_MODEL)].astype(F32))
            ss = _sig(pg_ref[rows, pl.ds(D_MODEL, D_MODEL)].astype(F32))
            merged = (sg * t1 + ss * t2).astype(BF16)
            mg_ref[rows, :] = merged
            mix = _dot(merged, wo_ref[...], NN)
            mix_ref[rows, :] = mix
            _, mh = _rms_stats(mix)
            x1 = x_ref[rows, :] + mh * g2_ref[...]
            x1_ref[rows, :] = x1
            _, xh = _rms_stats(x1)
            h_ref[rows, :] = (xh * g3_ref[...]).astype(BF16)

    row = _rt(T, D_MODEL)
    b16 = lambda: _out_tile((S, D_MODEL), BF16, row)
    f32 = lambda: _out_tile((S, D_MODEL), F32, row)
    return _tok_call(
        body, name=name, S=S, T=T,
        ins=[_tile(y_gla, row), _tile(y_sgu, row), _tile(pg, _rt(T, 2 * D_MODEL)), _tile(x, row),
             _whole(w_bg), _whole(w_bs), _whole(w_out), _whole(g2), _whole(g3)],
        outs=[b16(), b16(), b16(), f32(), f32(), b16()])


def _ffn_in_call(h, wfi, *, name, T=ROW_TILE):
    S = h.shape[0]

    def body(h_ref, w_ref, gu_ref, z_ref):
        hv = h_ref[...]
        for d in range(4):
            gate = _dot(hv, w_ref[d], NT)
            up = _dot(hv, w_ref[d + 4], NT)
            gu_ref[d] = gate.astype(BF16)
            gu_ref[d + 4] = up.astype(BF16)
            z_ref[d] = (gate * _sig(gate) * up).astype(BF16)

    return _tok_call(
        body, name=name, S=S, T=T,
        ins=[_tile(h, _rt(T, D_MODEL)), _whole(wfi)],
        outs=[_out_tile((N_DEV, S, FF_BLK), BF16, _rt3(N_DEV, T, FF_BLK)),
              _out_tile((4, S, FF_BLK), BF16, _rt3(4, T, FF_BLK))])


def _ffn_out_loss_call(z, wfo, x1, target, g4, *, name, T=ROW_TILE):
    S = x1.shape[0]

    def body(z_ref, w_ref, x1_ref, t_ref, g4_ref, loss_ref, dx2_ref, dy_ref, dg4_ref):
        loss = jnp.zeros((1, 1), F32)
        dg4 = jnp.zeros((1, D_MODEL), F32)
        for rows in _sub_tiles(T):
            y = _dot(z_ref[0, rows, :], w_ref[0], NN)
            for d in range(1, 4):
                y = y + _dot(z_ref[d, rows, :], w_ref[d], NN)
            r, yh = _rms_stats(y)
            diff = x1_ref[rows, :] + yh * g4_ref[...] - t_ref[rows, :]
            loss = loss + 0.5 * jnp.sum(jnp.mean(diff * diff, axis=-1, keepdims=True), axis=0, keepdims=True)
            dx2 = diff * (1.0 / D_MODEL)
            dx2_ref[rows, :] = dx2
            dy, dg = _rms_bwd(yh, r, g4_ref[...], dx2)
            dy_ref[rows, :] = dy.astype(BF16)
            dg4 = dg4 + dg
        _accum(loss_ref, jnp.broadcast_to(loss, (1, LANES)))
        _accum(dg4_ref, dg4)

    row = _rt(T, D_MODEL)
    return _tok_call(
        body, name=name, S=S, T=T, semantics="arbitrary",
        ins=[_tile(z, _rt3(4, T, FF_BLK)), _whole(wfo), _tile(x1, row), _tile(target, row), _whole(g4)],
        outs=[_out_acc((1, LANES)), _out_tile((S, D_MODEL), F32, row), _out_tile((S, D_MODEL), BF16, row),
              _out_acc((1, D_MODEL))])


def _ffn_out_bwd_call(dy, wfo, gu, *, name, T=ROW_TILE):
    S = dy.shape[0]

    def body(dy_ref, w_ref, gu_ref, dgu_ref):
        dyv = dy_ref[...]
        for d in range(4):
            dz = _dot(dyv, w_ref[d], NT)
            gt = gu_ref[d].astype(F32)
            up = gu_ref[d + 4].astype(F32)
            s = _sig(gt)
            dgu_ref[d] = (dz * up * s * (1.0 + gt * (1.0 - s))).astype(BF16)
            dgu_ref[d + 4] = (dz * gt * s).astype(BF16)

    blocks = _rt3(N_DEV, T, FF_BLK)
    return _tok_call(
        body, name=name, S=S, T=T,
        ins=[_tile(dy, _rt(T, D_MODEL)), _whole(wfo), _tile(gu, blocks)],
        outs=[_out_tile((N_DEV, S, FF_BLK), BF16, blocks)])[0]


def _ffn_in_bwd_call(dgu, wfi, dx2, x1, mix, g3, g2, *, name, T=ROW_TILE):
    S = x1.shape[0]

    def body(dgu_ref, w_ref, dx2_ref, x1_ref, mix_ref, g3_ref, g2_ref, dx1_ref, dmix_ref, dg3_ref, dg2_ref):
        dg3 = jnp.zeros((1, D_MODEL), F32)
        dg2 = jnp.zeros((1, D_MODEL), F32)
        for rows in _sub_tiles(T):
            dh = _dot(dgu_ref[0, rows, :], w_ref[0], NN)
            for d in range(1, N_DEV):
                dh = dh + _dot(dgu_ref[d, rows, :], w_ref[d], NN)
            r3, xh = _rms_stats(x1_ref[rows, :])
            d3, g = _rms_bwd(xh, r3, g3_ref[...], dh)
            dg3 = dg3 + g
            dx1 = dx2_ref[rows, :] + d3
            dx1_ref[rows, :] = dx1
            r2, mh = _rms_stats(mix_ref[rows, :])
            dmix, g = _rms_bwd(mh, r2, g2_ref[...], dx1)
            dg2 = dg2 + g
            dmix_ref[rows, :] = dmix.astype(BF16)
        _accum(dg3_ref, dg3)
        _accum(dg2_ref, dg2)

    row = _rt(T, D_MODEL)
    return _tok_call(
        body, name=name, S=S, T=T, semantics="arbitrary",
        ins=[_tile(dgu, _rt3(N_DEV, T, FF_BLK)), _whole(wfi), _tile(dx2, row), _tile(x1, row), _tile(mix, row),
             _whole(g3), _whole(g2)],
        outs=[_out_tile((S, D_MODEL), F32, row), _out_tile((S, D_MODEL), BF16, row),
              _out_acc((1, D_MODEL)), _out_acc((1, D_MODEL))])


def _mixer_bwd_call(dmix, t1, t2, pg, w_out, w_bg, w_bs, *, name, T=ROW_TILE):
    S = dmix.shape[0]

    def body(dmix_ref, t1_ref, t2_ref, pg_ref, wo_ref, wbg_ref, wbs_ref, dt1_ref, dt2_ref, dpg_ref, dyg_ref, dys_ref):
        gg, gs = pl.ds(0, D_MODEL), pl.ds(D_MODEL, D_MODEL)
        for rows in _sub_tiles(T):
            dm = _dot(dmix_ref[rows, :], wo_ref[...], NT)
            sg = _sig(pg_ref[rows, gg].astype(F32))
            ss = _sig(pg_ref[rows, gs].astype(F32))
            dt1 = (dm * sg).astype(BF16)
            dt2 = (dm * ss).astype(BF16)
            dt1_ref[rows, :] = dt1
            dt2_ref[rows, :] = dt2
            dpg_ref[rows, gg] = (dm * t1_ref[rows, :].astype(F32) * sg * (1.0 - sg)).astype(BF16)
            dpg_ref[rows, gs] = (dm * t2_ref[rows, :].astype(F32) * ss * (1.0 - ss)).astype(BF16)
            dyg_ref[rows, :] = _dot(dt1, wbg_ref[...], NT).astype(BF16)
            dys_ref[rows, :] = _dot(dt2, wbs_ref[...], NT).astype(BF16)

    row = _rt(T, D_MODEL)
    wide = _rt(T, 2 * D_MODEL)
    b16 = lambda: _out_tile((S, D_MODEL), BF16, row)
    return _tok_call(
        body, name=name, S=S, T=T,
        ins=[_tile(dmix, row), _tile(t1, row), _tile(t2, row), _tile(pg, wide), _whole(w_out), _whole(w_bg), _whole(w_bs)],
        outs=[b16(), b16(), _out_tile((S, 2 * D_MODEL), BF16, wide), b16(), b16()])


def _in_proj_bwd_call(dpa, dlow, dps, dpg, w_in_t, x, dx1, g1, *, name, T=ROW_TILE):
    S = x.shape[0]

    def body(dpa_ref, dl_ref, dps_ref, dpg_ref, w_ref, x_ref, dx1_ref, g1_ref, gx_ref, dg1_ref):
        da = (_dot(dpa_ref[...], w_ref[A_COLS[0]:A_COLS[1], :], NN)
              + _dot(dl_ref[...], w_ref[LOW_COLS[0]:LOW_COLS[0] + LANES, :], NN)
              + _dot(dps_ref[...], w_ref[S_COLS[0]:S_COLS[1], :], NN)
              + _dot(dpg_ref[...], w_ref[G_COLS[0]:G_COLS[1], :], NN))
        r, xh = _rms_stats(x_ref[...])
        dxa, dg = _rms_bwd(xh, r, g1_ref[...], da)
        gx_ref[...] = dx1_ref[...] + dxa
        _accum(dg1_ref, dg)

    row = _rt(T, D_MODEL)
    return _tok_call(
        body, name=name, S=S, T=T, semantics="arbitrary",
        ins=[_tile(dpa, _rt(T, dpa.shape[1])), _tile(dlow, _rt(T, dlow.shape[1])), _tile(dps, _rt(T, dps.shape[1])),
             _tile(dpg, _rt(T, dpg.shape[1])), _whole(w_in_t), _tile(x, row), _tile(dx1, row), _whole(g1)],
        outs=[_out_tile((S, D_MODEL), F32, row), _out_acc((1, D_MODEL))])


TOKEN_TILE = 1024


def _weight_grads_part(arrays, pairs, *, tk=TOKEN_TILE, out_dtype=BF16):
    S = arrays[0].shape[-2]
    tk = min(tk, S)
    n_in = len(arrays)

    def out_shape(i, j):
        a, b = arrays[i], arrays[j]
        if a.ndim == 3:
            return (a.shape[0], a.shape[2], b.shape[1])
        if b.ndim == 3:
            return (b.shape[0], a.shape[1], b.shape[2])
        return (a.shape[1], b.shape[1])

    shapes = [out_shape(i, j) for i, j in pairs]

    in_place = out_dtype == F32

    def body(ins, outs, accs):
        k = pl.program_id(0)
        if in_place:
            accs = outs

        @pl.when(k == 0)
        def _():
            for acc in accs:
                acc[...] = jnp.zeros_like(acc)

        for (i, j), acc in zip(pairs, accs):
            a_ref, b_ref = ins[i], ins[j]
            if len(a_ref.shape) == 3:
                for n in range(a_ref.shape[0]):
                    acc[n] += _dot(a_ref[n], b_ref[...], TN)
            elif len(b_ref.shape) == 3:
                a = a_ref[...]
                for n in range(b_ref.shape[0]):
                    acc[n] += _dot(a, b_ref[n], TN)
            else:
                b = b_ref[...]
                for m0 in range(0, a_ref.shape[1], 1024):
                    m1 = min(a_ref.shape[1], m0 + 1024)
                    acc[m0:m1, :] += _dot(a_ref[:, m0:m1], b, TN)

        if not in_place:
            @pl.when(k == S // tk - 1)
            def _():
                for out, acc in zip(outs, accs):
                    out[...] = acc[...].astype(out.dtype)

    def in_spec(a):
        if a.ndim == 3:
            return pl.BlockSpec((a.shape[0], tk, a.shape[2]), lambda k: (0, k, 0))
        return pl.BlockSpec((tk, a.shape[1]), lambda k: (k, 0))

    return dict(
        body=body, steps=S // tk, arrays=list(arrays),
        in_specs=[in_spec(a) for a in arrays],
        out_specs=[_acc(s) for s in shapes],
        out_shapes=[jax.ShapeDtypeStruct(s, out_dtype) for s in shapes],
        scratch=[] if in_place else [pltpu.VMEM(s, F32) for s in shapes],
        tiles=[(in_spec(a).block_shape, a.dtype) for a in arrays],
        resident=[(s, F32) for s in shapes] + ([] if in_place else [(s, BF16) for s in shapes]))


def _weight_grads_call(arrays, pairs, *, name, tk=TOKEN_TILE):
    part = _weight_grads_part(arrays, pairs, tk=tk)
    n_in, n_out = len(part["arrays"]), len(part["out_shapes"])

    def body(*refs):
        part["body"](refs[:n_in], refs[n_in:n_in + n_out], refs[n_in + n_out:])

    return pl.pallas_call(
        body, name=name, grid=(part["steps"],),
        in_specs=part["in_specs"], out_specs=part["out_specs"], out_shape=part["out_shapes"],
        scratch_shapes=part["scratch"],
        compiler_params=_cp(("arbitrary",), _vmem_limit(part["tiles"], part["resident"])),
    )(*part["arrays"])


def _with_rider(body, n_in, n_out, n_scratch, rider):
    if rider is None:
        return body
    r_in, r_out = len(rider["arrays"]), len(rider["out_shapes"])

    def both(*refs):
        ins, rest = refs[:n_in + r_in], refs[n_in + r_in:]
        outs, scratch = rest[:n_out + r_out], rest[n_out + r_out:]
        body(*ins[:n_in], *outs[:n_out], *scratch[:n_scratch])
        rider["body"](ins[n_in:], outs[n_out:], scratch[n_scratch:])

    return both


def _rider_lists(rider):
    if rider is None:
        return [], [], [], [], [], [], []
    return (rider["arrays"], rider["in_specs"], rider["out_specs"], rider["out_shapes"], rider["scratch"],
            rider["tiles"], rider["resident"])


def _ffn_in_grad_call(h, dgu, *, name, tk=TOKEN_TILE):
    S = h.shape[0]
    tk = min(tk, S)
    nb = 4
    shape = (nb, FF_BLK, D_MODEL)

    def body(h_ref, dgu_ref, out_ref, acc):
        k = pl.program_id(1)

        @pl.when(k == 0)
        def _():
            acc[...] = jnp.zeros_like(acc)

        hv = h_ref[...]
        for n in range(nb):
            acc[n] += _dot(dgu_ref[n], hv, TN)

        @pl.when(k == S // tk - 1)
        def _():
            out_ref[...] = acc[...].astype(out_ref.dtype)

    tiles = [((tk, D_MODEL), BF16), ((nb, tk, FF_BLK), BF16), (shape, BF16)]
    return pl.pallas_call(
        body, name=name, grid=(N_DEV // nb, S // tk),
        in_specs=[pl.BlockSpec((tk, D_MODEL), lambda g, k: (k, 0)),
                  pl.BlockSpec((nb, tk, FF_BLK), lambda g, k: (g, k, 0))],
        out_specs=pl.BlockSpec(shape, lambda g, k: (g, 0, 0)),
        out_shape=jax.ShapeDtypeStruct((N_DEV, FF_BLK, D_MODEL), BF16),
        scratch_shapes=[pltpu.VMEM(shape, F32)],
        compiler_params=_cp(("parallel", "arbitrary"), _vmem_limit(tiles, [(shape, F32)])),
    )(h, dgu)


GLA_TILE = 256
Q_OFF, K_OFF, V_OFF, R_OFF = 0, QK, 2 * QK, 2 * QK + GV


def _tri(lower):
    r = lax.broadcasted_iota(jnp.int32, (CHUNK, CHUNK), 0)
    c = lax.broadcasted_iota(jnp.int32, (CHUNK, CHUNK), 1)
    return jnp.where((r >= c) if lower else (c >= r), 1.0, 0.0).astype(BF16)


def _gla_fwd_call(pa, alow, wg, bgate, gnorm, *, name):
    S = pa.shape[0]
    Tg = min(GLA_TILE, S)
    cb = Tg // CHUNK

    def body(pa_ref, al_ref, wg_ref, bg_ref, gn_ref, y_ref, st_ref, state):
        @pl.when(pl.program_id(0) == 0)
        def _():
            state[...] = jnp.zeros_like(state)

        logit = _dot(al_ref[...], wg_ref[...], NN) + bg_ref[...]
        ls = _logsig(logit) * (1.0 / 16.0)
        tri = _tri(True)
        for c in range(cb):
            rows = pl.ds(c * CHUNK, CHUNK)
            cum = _exact_mask_dot(tri, ls[c * CHUNK:(c + 1) * CHUNK])
            tot = cum[CHUNK - 1:CHUNK]
            kd = (pa_ref[rows, pl.ds(K_OFF, QK)].astype(F32) * jnp.exp(tot - cum)).astype(BF16)
            decay = jnp.exp(tot)
            for h in range(HEADS):
                lanes = slice(h * DK, (h + 1) * DK)
                new = state[h] * decay[:, lanes] + _dot(pa_ref[rows, pl.ds(V_OFF + h * DV, DV)], kd[:, lanes], TN)
                state[h] = new
                st_ref[c, h] = new
        for c in range(cb):
            rows = pl.ds(c * CHUNK, CHUNK)
            for h in range(HEADS):
                qs = (pa_ref[rows, pl.ds(Q_OFF + h * DK, DK)].astype(F32) * Q_SCALE).astype(BF16)
                o = _dot(qs, st_ref[c, h].astype(BF16), NT)
                rs = lax.rsqrt(jnp.mean(o * o, axis=-1, keepdims=True) + EPS)
                rr = pa_ref[rows, pl.ds(R_OFF + h * DV, DV)].astype(F32)
                y_ref[rows, pl.ds(h * DV, DV)] = (o * rs * gn_ref[h] * (rr * _sig(rr))).astype(BF16)

    return pl.pallas_call(
        body, name=name, grid=(S // Tg,),
        in_specs=[
            pl.BlockSpec((Tg, 2 * QK + 2 * GV), lambda t: (t, 0)),
            pl.BlockSpec((Tg, LANES), lambda t: (t, 0)),
            pl.BlockSpec((LANES, QK), lambda t: (0, 0)),
            pl.BlockSpec((1, QK), lambda t: (0, 0)),
            pl.BlockSpec((HEADS, 1, DV), lambda t: (0, 0, 0)),
        ],
        out_specs=[
            pl.BlockSpec((Tg, GV), lambda t: (t, 0)),
            pl.BlockSpec((cb, HEADS, DV, DK), lambda t: (t, 0, 0, 0)),
        ],
        out_shape=[jax.ShapeDtypeStruct((S, GV), BF16),
                   jax.ShapeDtypeStruct((S // CHUNK, HEADS, DV, DK), F32)],
        scratch_shapes=[pltpu.VMEM((HEADS, DV, DK), F32)],
        compiler_params=_cp(("arbitrary",), VMEM_BIG),
    )(pa, alow, wg, bgate, gnorm)


def _gla_bwd_call(pa, alow, wg, bgate, gnorm, states, dy, *, name, rider=None):
    S = pa.shape[0]
    Tg = min(GLA_TILE, S)
    cb = Tg // CHUNK
    nt = S // Tg
    r_arrays, r_in, r_out, r_shapes, r_scratch, r_tiles, r_resident = _rider_lists(rider)
    assert rider is None or rider["steps"] == nt

    def rev(t):
        return nt - 1 - t

    def body(pa_ref, al_ref, wg_ref, bg_ref, gn_ref, st_ref, prev_ref, dy_ref,
             dpa_ref, dl_ref, dgn_ref, dbg_ref, carry, pbuf):
        t = pl.program_id(0)

        @pl.when(t == 0)
        def _():
            carry[...] = jnp.zeros_like(carry)
            dgn_ref[...] = jnp.zeros_like(dgn_ref)
            dbg_ref[...] = jnp.zeros_like(dbg_ref)

        logit = _dot(al_ref[...], wg_ref[...], NN) + bg_ref[...]
        ls = _logsig(logit) * (1.0 / 16.0)
        sneg = 1.0 / (1.0 + jnp.exp(logit))
        tri = _tri(True)
        upper = _tri(False)
        first_tile = rev(t) == 0
        heads = range(HEADS)

        w, decay, kd = [], [], []
        for c in range(cb):
            rows = pl.ds(c * CHUNK, CHUNK)
            cum = _exact_mask_dot(tri, ls[c * CHUNK:(c + 1) * CHUNK])
            tot = cum[CHUNK - 1:CHUNK]
            w.append(jnp.exp(tot - cum))
            decay.append(jnp.exp(tot))
            kd.append(pa_ref[rows, pl.ds(K_OFF, QK)].astype(F32) * w[c])

        dgn = [jnp.zeros((1, DV), F32) for _ in heads]
        for c in range(cb):
            rows = pl.ds(c * CHUNK, CHUNK)
            for h in heads:
                gn = gn_ref[h]
                qs = (pa_ref[rows, pl.ds(Q_OFF + h * DK, DK)].astype(F32) * Q_SCALE).astype(BF16)
                st16 = st_ref[c, h].astype(BF16)
                o = _dot(qs, st16, NT)
                rs = lax.rsqrt(jnp.mean(o * o, axis=-1, keepdims=True) + EPS)
                oh = o * rs
                rr = pa_ref[rows, pl.ds(R_OFF + h * DV, DV)].astype(F32)
                sr = _sig(rr)
                dyv = dy_ref[rows, pl.ds(h * DV, DV)].astype(F32)
                dpa_ref[rows, pl.ds(R_OFF + h * DV, DV)] = (
                    dyv * oh * gn * sr * (1.0 + rr * (1.0 - sr))).astype(BF16)
                don = dyv * (rr * sr)
                dgn[h] = dgn[h] + jnp.sum(don * oh, axis=0, keepdims=True)
                doh = don * gn
                do16 = (rs * (doh - oh * jnp.mean(doh * oh, axis=-1, keepdims=True))).astype(BF16)
                dpa_ref[rows, pl.ds(Q_OFF + h * DK, DK)] = (_dot(do16, st16, NN) * Q_SCALE).astype(BF16)
                pbuf[c, h] = _dot(do16, qs, TN)
        for h in heads:
            dgn_ref[h] += dgn[h]

        dkd = [[None] * HEADS for _ in range(cb)]
        ddecay = [[None] * HEADS for _ in range(cb)]
        for c in reversed(range(cb)):
            rows = pl.ds(c * CHUNK, CHUNK)
            for h in heads:
                lanes = slice(h * DK, (h + 1) * DK)
                gt = pbuf[c, h] + carry[h]
                gt16 = gt.astype(BF16)
                dkd[c][h] = _dot(pa_ref[rows, pl.ds(V_OFF + h * DV, DV)], gt16, NN)
                dpa_ref[rows, pl.ds(V_OFF + h * DV, DV)] = _dot(kd[c][:, lanes].astype(BF16), gt16, NT).astype(BF16)
                if c > 0:
                    st_prev = st_ref[c - 1, h]
                else:
                    st_prev = jnp.where(first_tile, 0.0, prev_ref[0, h])
                ddecay[c][h] = jnp.sum(gt * st_prev, axis=0, keepdims=True)
                carry[h] = gt * decay[c][:, lanes]

        dbg = jnp.zeros((1, QK), F32)
        for c in range(cb):
            rows = pl.ds(c * CHUNK, CHUNK)
            dkd_c = jnp.concatenate(dkd[c], axis=1)
            dpa_ref[rows, pl.ds(K_OFF, QK)] = (dkd_c * w[c]).astype(BF16)
            e = dkd_c * kd[c]
            dtot = jnp.sum(e, axis=0, keepdims=True) + jnp.concatenate(ddecay[c], axis=1) * decay[c]
            dls = dtot - _exact_mask_dot(upper, e)
            dlogit = dls * (1.0 / 16.0) * sneg[c * CHUNK:(c + 1) * CHUNK]
            dl_ref[rows, :] = dlogit.astype(BF16)
            dbg = dbg + jnp.sum(dlogit, axis=0, keepdims=True)
        dbg_ref[...] += dbg

    wide = 2 * QK + 2 * GV
    tiles = [((Tg, wide), BF16), ((cb + 1, HEADS, DV, DK), F32), ((Tg, GV), BF16), ((Tg, wide), BF16),
             ((Tg, QK), BF16)] + r_tiles
    resident = [((cb + 1, HEADS, DV, DK), F32)] + r_resident
    return pl.pallas_call(
        _with_rider(body, 8, 4, 2, rider), name=name, grid=(nt,),
        in_specs=[
            pl.BlockSpec((Tg, wide), lambda t: (rev(t), 0)),
            pl.BlockSpec((Tg, LANES), lambda t: (rev(t), 0)),
            pl.BlockSpec((LANES, QK), lambda t: (0, 0)),
            pl.BlockSpec((1, QK), lambda t: (0, 0)),
            pl.BlockSpec((HEADS, 1, DV), lambda t: (0, 0, 0)),
            pl.BlockSpec((cb, HEADS, DV, DK), lambda t: (rev(t), 0, 0, 0)),
            pl.BlockSpec((1, HEADS, DV, DK), lambda t: (jnp.maximum(rev(t) * cb - 1, 0), 0, 0, 0)),
            pl.BlockSpec((Tg, GV), lambda t: (rev(t), 0)),
        ] + r_in,
        out_specs=[
            pl.BlockSpec((Tg, wide), lambda t: (rev(t), 0)),
            pl.BlockSpec((Tg, QK), lambda t: (rev(t), 0)),
            pl.BlockSpec((HEADS, 1, DV), lambda t: (0, 0, 0)),
            pl.BlockSpec((1, QK), lambda t: (0, 0)),
        ] + r_out,
        out_shape=[jax.ShapeDtypeStruct((S, wide), BF16), jax.ShapeDtypeStruct((S, QK), BF16),
                   jax.ShapeDtypeStruct((HEADS, 1, DV), F32), jax.ShapeDtypeStruct((1, QK), F32)] + r_shapes,
        scratch_shapes=[pltpu.VMEM((HEADS, DV, DK), F32), pltpu.VMEM((cb, HEADS, DV, DK), F32)] + r_scratch,
        compiler_params=_cp(("arbitrary",), _vmem_limit(tiles, resident)),
    )(pa, alow, wg, bgate, gnorm, states, states, dy, *r_arrays)


SGU_TILE = 256


def _sgu_mask():
    r = lax.broadcasted_iota(jnp.int32, (SBLOCK, SBLOCK), 0)
    c = lax.broadcasted_iota(jnp.int32, (SBLOCK, SBLOCK), 1)
    return (c < CHUNK) | (r >= CHUNK)


def _ln_stats(vf):
    mu = jnp.mean(vf, axis=-1, keepdims=True)
    xc = vf - mu
    rs = lax.rsqrt(jnp.mean(xc * xc, axis=-1, keepdims=True) + EPS)
    return rs, xc * rs


def _sgu_fwd_call(ps, ln_g, ln_b, w_sp, b_sp, *, name):
    S = ps.shape[0]
    Ts = min(SGU_TILE, S)

    def body(ps_ref, lg_ref, lb_ref, w_ref, b_ref, y_ref):
        mask = _sgu_mask()
        for g in range(GROUPS):
            wm = jnp.where(mask, w_ref[g], 0.0).astype(BF16)
            for p in range(Ts // SBLOCK):
                rows = pl.ds(p * SBLOCK, SBLOCK)
                u = _gelu(ps_ref[rows, pl.ds(g * DG, DG)].astype(F32))
                _, xh = _ln_stats(_gelu(ps_ref[rows, pl.ds(D_MODEL + g * DG, DG)].astype(F32)))
                vn = xh * lg_ref[g] + lb_ref[g]
                mixed = _dot(wm, vn.astype(BF16), NN) + b_ref[g]
                y_ref[rows, pl.ds(g * DG, DG)] = (u * mixed).astype(BF16)

    full3 = lambda a, b, c: pl.BlockSpec((a, b, c), lambda t: (0, 0, 0))
    return pl.pallas_call(
        body, name=name, grid=(S // Ts,),
        in_specs=[pl.BlockSpec((Ts, 2 * D_MODEL), lambda t: (t, 0)),
                  full3(GROUPS, 1, DG), full3(GROUPS, 1, DG), full3(GROUPS, SBLOCK, SBLOCK), full3(GROUPS, SBLOCK, 1)],
        out_specs=pl.BlockSpec((Ts, D_MODEL), lambda t: (t, 0)),
        out_shape=jax.ShapeDtypeStruct((S, D_MODEL), BF16),
        compiler_params=_cp(("parallel",)),
    )(ps, ln_g, ln_b, w_sp, b_sp)


def _sgu_bwd_call(ps, ln_g, ln_b, w_sp, b_sp, dy, *, name, rider=None):
    S = ps.shape[0]
    Ts = min(SGU_TILE, S)
    r_arrays, r_in, r_out, r_shapes, r_scratch, r_tiles, r_resident = _rider_lists(rider)
    assert rider is None or rider["steps"] == S // Ts

    def body(ps_ref, lg_ref, lb_ref, w_ref, b_ref, dy_ref, ds_ref, dlg_ref, dlb_ref, dw_ref, db_ref):
        @pl.when(pl.program_id(0) == 0)
        def _():
            dlg_ref[...] = jnp.zeros_like(dlg_ref)
            dlb_ref[...] = jnp.zeros_like(dlb_ref)
            dw_ref[...] = jnp.zeros_like(dw_ref)
            db_ref[...] = jnp.zeros_like(db_ref)

        mask = _sgu_mask()
        for g in range(GROUPS):
            wm = jnp.where(mask, w_ref[g], 0.0).astype(BF16)
            lg = lg_ref[g]
            for p in range(Ts // SBLOCK):
                rows = pl.ds(p * SBLOCK, SBLOCK)
                su = ps_ref[rows, pl.ds(g * DG, DG)].astype(F32)
                sv = ps_ref[rows, pl.ds(D_MODEL + g * DG, DG)].astype(F32)
                u = _gelu(su)
                rs, xh = _ln_stats(_gelu(sv))
                vn16 = (xh * lg + lb_ref[g]).astype(BF16)
                mixed = _dot(wm, vn16, NN) + b_ref[g]
                dyv = dy_ref[rows, pl.ds(g * DG, DG)].astype(F32)
                ds_ref[rows, pl.ds(g * DG, DG)] = (dyv * mixed * _gelu_grad(su)).astype(BF16)
                dmix = dyv * u
                dmix16 = dmix.astype(BF16)
                db_ref[g] += jnp.sum(dmix, axis=-1, keepdims=True)
                dw_ref[g] += jnp.where(mask, _dot(dmix16, vn16, NT), 0.0)
                dvn = _dot(wm, dmix16, TN)
                dlg_ref[g] += jnp.sum(dvn * xh, axis=0, keepdims=True)
                dlb_ref[g] += jnp.sum(dvn, axis=0, keepdims=True)
                dxh = dvn * lg
                dvf = rs * (dxh - jnp.mean(dxh, axis=-1, keepdims=True)
                            - xh * jnp.mean(dxh * xh, axis=-1, keepdims=True))
                ds_ref[rows, pl.ds(D_MODEL + g * DG, DG)] = (dvf * _gelu_grad(sv)).astype(BF16)

    full3 = lambda a, b, c: pl.BlockSpec((a, b, c), lambda t: (0, 0, 0))
    tiles = [((Ts, 2 * D_MODEL), BF16), ((Ts, D_MODEL), BF16), ((Ts, 2 * D_MODEL), BF16)] + r_tiles
    return pl.pallas_call(
        _with_rider(body, 6, 5, 0, rider), name=name, grid=(S // Ts,),
        in_specs=[pl.BlockSpec((Ts, 2 * D_MODEL), lambda t: (t, 0)),
                  full3(GROUPS, 1, DG), full3(GROUPS, 1, DG), full3(GROUPS, SBLOCK, SBLOCK), full3(GROUPS, SBLOCK, 1),
                  pl.BlockSpec((Ts, D_MODEL), lambda t: (t, 0))] + r_in,
        out_specs=[pl.BlockSpec((Ts, 2 * D_MODEL), lambda t: (t, 0)),
                   full3(GROUPS, 1, DG), full3(GROUPS, 1, DG), full3(GROUPS, SBLOCK, SBLOCK), full3(GROUPS, SBLOCK, 1)]
        + r_out,
        out_shape=[jax.ShapeDtypeStruct((S, 2 * D_MODEL), BF16),
                   jax.ShapeDtypeStruct((GROUPS, 1, DG), F32), jax.ShapeDtypeStruct((GROUPS, 1, DG), F32),
                   jax.ShapeDtypeStruct((GROUPS, SBLOCK, SBLOCK), F32),
                   jax.ShapeDtypeStruct((GROUPS, SBLOCK, 1), F32)] + r_shapes,
        scratch_shapes=r_scratch,
        compiler_params=_cp(("arbitrary",), _vmem_limit(tiles, r_resident)),
    )(ps, ln_g, ln_b, w_sp, b_sp, dy, *r_arrays)


def _position():
    return lax.axis_index("x"), lax.axis_index("y"), lax.axis_index("c")


def _gather_copies(srcs, dsts, send_sems, recv_sems, local_sems):
    x, y, c = _position()
    me, sibling = (x, y, c), (x, y, 1 - c)
    chips = [(1 - x, y), (x, 1 - y), (1 - x, 1 - y)]
    n = len(srcs)

    def slab(a, block):
        px, py, pc = block
        return dsts[a].at[4 * px + 2 * py + pc]

    def copy(a, k, block, to, src=None):
        return pltpu.make_async_remote_copy(
            src_ref=slab(a, block) if src is None else src, dst_ref=slab(a, block),
            send_sem=send_sems.at[7 * a + k], recv_sem=recv_sems.at[7 * a + k], device_id=to, device_id_type=MESH)

    mine = [pltpu.make_async_copy(srcs[a], slab(a, me), local_sems.at[a]) for a in range(n)]
    for cp in mine:
        cp.start()
    first = []
    for a in range(n):
        first.append(copy(a, 0, me, sibling, src=srcs[a]))
        first += [copy(a, 1 + j, me, (*chip, c), src=srcs[a]) for j, chip in enumerate(chips)]
    for cp in first:
        cp.start()
    passed = []
    for j, chip in enumerate(chips):
        for a in range(n):
            copy(a, 1 + j, (*chip, c), me).wait_recv()
            cp = copy(a, 4 + j, (*chip, c), sibling)
            cp.start()
            passed.append(cp)
    for a in range(n):
        copy(a, 0, sibling, me).wait_recv()
        for j, chip in enumerate(chips):
            copy(a, 4 + j, (*chip, 1 - c), me).wait_recv()
    for cp in first + passed:
        cp.wait_send()
    for cp in mine:
        cp.wait()


def _all_gather_hbm(shards, *, name):
    n = len(shards)

    def body(*refs):
        srcs, dsts = refs[:n], refs[n:2 * n]
        send_sems, recv_sems, local_sems = refs[2 * n:]
        _gather_copies(srcs, dsts, send_sems, recv_sems, local_sems)

    return pl.pallas_call(
        body, name=name,
        in_specs=[ANY] * n, out_specs=[ANY] * n,
        out_shape=[jax.ShapeDtypeStruct((N_DEV, *s.shape), s.dtype) for s in shards],
        scratch_shapes=_comm_sems(n),
    )(*shards)


def _all_reduce_small(part, *, name):
    R, W = part.shape

    def body(x_ref, out_ref, gathered, send_sems, recv_sems, local_sems):
        _gather_copies([x_ref], [gathered], send_sems, recv_sems, local_sems)
        acc = gathered[0]
        for d in range(1, N_DEV):
            acc = acc + gathered[d]
        out_ref[...] = acc

    return pl.pallas_call(
        body, name=name,
        in_specs=[VMEM_SPEC], out_specs=VMEM_SPEC,
        out_shape=jax.ShapeDtypeStruct((R, W), F32),
        scratch_shapes=[pltpu.VMEM((N_DEV, R, W), F32),
                        pltpu.SemaphoreType.DMA((7,)), pltpu.SemaphoreType.DMA((7,)), pltpu.SemaphoreType.DMA((1,))],
    )(part)


FLIPS = [(fx, fy, fc) for fx in (0, 1) for fy in (0, 1) for fc in (0, 1)][1:]


def _scatter_copies(srcs, dsts, send_sems, recv_sems, local_sems):
    n = len(srcs)
    x, y, c = _position()
    me = 4 * x + 2 * y + c
    mine = [pltpu.make_async_copy(srcs[a].at[me], dsts[a].at[me], local_sems.at[a]) for a in range(n)]
    for cp in mine:
        cp.start()
    copies = []
    for k, (fx, fy, fc) in enumerate(FLIPS):
        tx = 1 - x if fx else x
        ty = 1 - y if fy else y
        tc = 1 - c if fc else c
        peer = 4 * tx + 2 * ty + tc
        for a in range(n):
            cp = pltpu.make_async_remote_copy(
                src_ref=srcs[a].at[peer], dst_ref=dsts[a].at[me],
                send_sem=send_sems.at[7 * a + k], recv_sem=recv_sems.at[7 * a + k],
                device_id=(tx, ty, tc), device_id_type=MESH)
            cp.start()
            copies.append(cp)
    for cp in copies:
        cp.wait()
    for cp in mine:
        cp.wait()


def _comm_sems(n):
    return [pltpu.SemaphoreType.DMA((7 * n,)), pltpu.SemaphoreType.DMA((7 * n,)), pltpu.SemaphoreType.DMA((n,))]


def _scatter_blocks(parts, *, name):
    n = len(parts)

    def body(*refs):
        _scatter_copies(refs[:n], refs[n:2 * n], *refs[2 * n:])

    return pl.pallas_call(
        body, name=name,
        in_specs=[ANY] * n, out_specs=[ANY] * n,
        out_shape=[jax.ShapeDtypeStruct(p.shape, p.dtype) for p in parts],
        scratch_shapes=_comm_sems(n),
    )(*parts)


def _handshake(peers):
    barrier = pltpu.get_barrier_semaphore()
    for peer in peers:
        pl.semaphore_signal(barrier, inc=1, device_id=peer, device_id_type=MESH)
    pl.semaphore_wait(barrier, len(peers))


def _sequencer_call(arrays, out_types, copies_fn, peers_fn, *, name, collective_id):
    n = len(arrays)
    srcs = [jax.new_ref(a, memory_space=pltpu.MemorySpace.HBM) for a in arrays]
    dsts = [jax.empty_ref(t, memory_space=pltpu.MemorySpace.HBM) for t in out_types]

    @pl.kernel(mesh=plsc.ScalarSubcoreMesh(axis_name="sequencer", num_cores=1), name=name,
               scratch_types=_comm_sems(n), compiler_params=pltpu.CompilerParams(collective_id=collective_id))
    def launch(send_sems, recv_sems, local_sems):
        _handshake(peers_fn())
        copies_fn(srcs, dsts, send_sems, recv_sems, local_sems)

    launch()
    return [d[...] for d in dsts]


def _all_other_devices():
    x, y, c = _position()
    return [(1 - x if fx else x, 1 - y if fy else y, 1 - c if fc else c) for fx, fy, fc in FLIPS]


def _gather_peers():
    x, y, c = _position()
    return [(x, y, 1 - c), (1 - x, y, c), (x, 1 - y, c), (1 - x, 1 - y, c)]


def _scatter_blocks_async(parts, *, name, collective_id):
    return _sequencer_call(parts, [jax.ShapeDtypeStruct(p.shape, p.dtype) for p in parts],
                           _scatter_copies, _all_other_devices, name=name, collective_id=collective_id)


def _all_gather_async(shards, *, name, collective_id):
    return _sequencer_call(shards, [jax.ShapeDtypeStruct((N_DEV, *s.shape), s.dtype) for s in shards],
                           _gather_copies, _gather_peers, name=name, collective_id=collective_id)


def _adamw_math(w, g, m, v):
    m = ADAM_B1 * m + (1.0 - ADAM_B1) * g
    v = ADAM_B2 * v + (1.0 - ADAM_B2) * (g * g)
    m_hat = m / (1.0 - ADAM_B1 ** ADAM_STEP)
    v_hat = v / (1.0 - ADAM_B2 ** ADAM_STEP)
    delta = -ADAM_LR * (m_hat / (jnp.sqrt(v_hat) + ADAM_EPS) + ADAM_WD * w)
    return delta, m, v


def _adamw_reduce_call(recv, w, m, v, *, name, T=128):
    R, W = w.shape
    if R % T == 0:
        tr, tw = T, W
    elif (R // 2) % 16 == 0:
        tr, tw = R // 2, W
    else:
        tr, tw = R, 2 * LANES

    def body(p_ref, w_ref, m_ref, v_ref, g_out, d_out, m_out, v_out):
        g = p_ref[0].astype(F32)
        for d in range(1, N_DEV):
            g = g + p_ref[d].astype(F32)
        g_out[...] = g
        d_out[...], m_out[...], v_out[...] = _adamw_math(w_ref[...], g, m_ref[...], v_ref[...])

    row = pl.BlockSpec((tr, tw), lambda i, j: (i, j))
    out = jax.ShapeDtypeStruct((R, W), F32)
    return pl.pallas_call(
        body, name=name, grid=(R // tr, W // tw),
        in_specs=[pl.BlockSpec((N_DEV, tr, tw), lambda i, j: (0, i, j)), row, row, row],
        out_specs=[row] * 4, out_shape=[out] * 4,
        compiler_params=_cp(("parallel", "parallel"), VMEM_BIG),
    )(recv, w, m, v)


def _adamw_small_call(w, g, m, v, *, name):
    def body(w_ref, g_ref, m_ref, v_ref, d_out, m_out, v_out):
        d_out[...], m_out[...], v_out[...] = _adamw_math(w_ref[...], g_ref[...], m_ref[...], v_ref[...])

    out = jax.ShapeDtypeStruct(w.shape, F32)
    return pl.pallas_call(body, name=name, in_specs=[VMEM_SPEC] * 4, out_specs=[VMEM_SPEC] * 3,
                          out_shape=[out] * 3)(w, g, m, v)


def _tile_rows(n_elems):
    return -(-n_elems // (8 * LANES)) * 8


def _pack_rows(parts, rows):
    pieces = []
    for p in parts:
        q = p.reshape(-1, LANES)
        pieces.append(jnp.pad(q, ((0, _tile_rows(p.size) - q.shape[0]), (0, 0))))
    buf = jnp.concatenate(pieces, axis=0)
    return jnp.pad(buf, ((0, rows - buf.shape[0]), (0, 0)))


def _unpack_rows(buf, shapes):
    out, off = [], 0
    for shp in shapes:
        n = 1
        for s in shp:
            n *= s
        out.append(buf[off:off + n // LANES].reshape(shp))
        off += _tile_rows(n)
    return out


def _join_blocks_call(blocks, *, name, tw=2 * LANES):
    nb, r, w = blocks.shape

    def body(b_ref, o_ref):
        for d in range(nb):
            o_ref[d * r:(d + 1) * r, :] = b_ref[d]

    return pl.pallas_call(
        body, name=name, grid=(w // tw,),
        in_specs=[pl.BlockSpec((nb, r, tw), lambda j: (0, 0, j))],
        out_specs=pl.BlockSpec((nb * r, tw), lambda j: (0, j)),
        out_shape=jax.ShapeDtypeStruct((nb * r, w), blocks.dtype),
        compiler_params=_cp(("parallel",)),
    )(blocks)


def _split_blocks_call(parts, *, name, tw=2 * LANES):
    w = parts[0].shape[1]
    starts = [0]
    for q in parts:
        starts.append(starts[-1] + q.shape[0])
    assert starts[-1] == N_DEV * IN_BLK

    def body(*refs):
        o_ref = refs[-1]
        for d in range(N_DEV):
            lo, hi = IN_BLK * d, IN_BLK * (d + 1)
            for ref, c0, c1 in zip(refs[:-1], starts[:-1], starts[1:]):
                s0, e0 = max(lo, c0), min(hi, c1)
                if s0 < e0:
                    o_ref[d, s0 - lo:e0 - lo, :] = ref[s0 - c0:e0 - c0, :].astype(o_ref.dtype)

    return pl.pallas_call(
        body, name=name, grid=(w // tw,),
        in_specs=[pl.BlockSpec((q.shape[0], tw), lambda j: (0, j)) for q in parts],
        out_specs=pl.BlockSpec((N_DEV, IN_BLK, tw), lambda j: (0, 0, j)),
        out_shape=jax.ShapeDtypeStruct((N_DEV, IN_BLK, w), BF16),
        compiler_params=_cp(("parallel",)),
    )(*parts)


def _local_step(x, target, W, scatter, finish):
    g1, g2, g3, g4 = [W[n].reshape(1, D_MODEL) for n in ("norm_pre_mix", "norm_post_mix", "norm_pre_ffn", "norm_post_ffn")]
    w_in_t = W["w_in_t"]
    wfi, wfo = W["w_ffn_in"], W["w_ffn_out"].reshape(4, FF_BLK, D_MODEL)
    wg = jnp.pad(W["w_gate_up"], ((0, LANES - RANK), (0, 0))).astype(BF16)
    bgate = W["b_gate"].reshape(1, QK)
    gnorm = W["gla_norm"].reshape(HEADS, 1, DV)
    ln_g = W["sgu_ln_g"].reshape(GROUPS, 1, DG)
    ln_b = W["sgu_ln_b"].reshape(GROUPS, 1, DG)
    w_sp = W["w_spatial"]
    b_sp = W["b_spatial"].reshape(GROUPS, SBLOCK, 1)

    a, pa, alow, ps, pg = _in_proj_call(x, g1, w_in_t, name="in_proj")
    y_gla, states = _gla_fwd_call(pa, alow, wg, bgate, gnorm, name="gla_fwd")
    y_sgu = _sgu_fwd_call(ps, ln_g, ln_b, w_sp, b_sp, name="sgu_fwd")
    t1, t2, merged, mix, x1, h = _mixer_tail_call(y_gla, y_sgu, pg, x, W["w_branch_gla"], W["w_branch_sgu"],
                                                  W["w_out"], g2, g3, name="mixer_tail")
    gu, z = _ffn_in_call(h, wfi, name="ffn_in")
    loss, dx2, dy, dg4 = _ffn_out_loss_call(z, wfo, x1, target, g4, name="ffn_out_loss")

    grads = {"norm_post_ffn": dg4}
    done = {}
    dgu = _ffn_out_bwd_call(dy, wfo, gu, name="ffn_out_bwd")
    dw_ffn_out = _weight_grads_call([z, dy], [(0, 1)], name="d_ffn_out_w")[0].reshape(N_DEV, D_FF // N_DEV, D_MODEL)
    dw_ffn_in = _ffn_in_grad_call(h, dgu, name="d_ffn_in_w")
    dgu, dw_ffn_out, dw_ffn_in = lax.optimization_barrier((dgu, dw_ffn_out, dw_ffn_in))
    ffn_received = scatter(("w_ffn_out", "w_ffn_in"), [dw_ffn_out, dw_ffn_in])
    dx1, dmix, grads["norm_pre_ffn"], grads["norm_post_mix"] = _ffn_in_bwd_call(dgu, wfi, dx2, x1, mix, g3, g2, name="ffn_in_bwd")
    dt1, dt2, dpg, dy_gla, dy_sgu = _mixer_bwd_call(dmix, t1, t2, pg, W["w_out"], W["w_branch_gla"], W["w_branch_sgu"],
                                                    name="mixer_bwd")
    rows = D_MODEL // N_DEV
    mixer_grads = _weight_grads_call([merged, dmix, y_gla, dt1, y_sgu, dt2], [(0, 1), (2, 3), (4, 5)], name="d_mixer_w")
    dy_gla, dy_sgu, mixer_grads = lax.optimization_barrier((dy_gla, dy_sgu, mixer_grads))
    mixer_received = scatter(("w_out", "w_branch_gla", "w_branch_sgu"),
                             [g.reshape(N_DEV, rows, D_MODEL) for g in mixer_grads])
    dpa, dlogit, dgn, dbg = _gla_bwd_call(pa, alow, wg, bgate, gnorm, states, dy_gla, name="gla_bwd")
    ffn_done, dpa, dlogit = lax.optimization_barrier((finish(ffn_received), dpa, dlogit))
    done.update(ffn_done)
    dps, dlg, dlb, dwsp, dbsp = _sgu_bwd_call(ps, ln_g, ln_b, w_sp, b_sp, dy_sgu, name="sgu_bwd")
    mixer_done, dps = lax.optimization_barrier((finish(mixer_received), dps))
    done.update(mixer_done)
    dlow = _mm(dlogit, wg, "nt", BF16, name="d_gate_up_x")
    dw_a, dw_low = _weight_grads_call([a, dpa, dlow], [(1, 0), (2, 0)], name="d_in_w_qkvr")
    dw_s, dw_g = _weight_grads_call([a, dps, dpg], [(1, 0), (2, 0)], name="d_in_w_sgu_gates")
    dw_in = _split_blocks_call([dw_a, dw_low[:RANK], dw_s, dw_g], name="split_w_in_grad")
    in_received = scatter(("w_in",), [dw_in])
    dwg = _mm(alow, dlogit, "tn", F32, name="d_gate_up_w")
    grad_x, grads["norm_pre_mix"] = _in_proj_bwd_call(dpa, dlow, dps, dpg, w_in_t, x, dx1, g1, name="in_proj_bwd")

    grads["w_gate_up"] = dwg[:RANK]
    grads["b_gate"] = dbg
    grads["gla_norm"] = dgn
    grads["sgu_ln_g"] = dlg
    grads["sgu_ln_b"] = dlb
    grads["w_spatial"] = dwsp
    grads["b_spatial"] = dbsp
    done.update(finish(in_received))
    return loss, grad_x, grads, done


WEIGHTS = ("norm_pre_mix", "w_in", "w_gate_up", "b_gate", "gla_norm", "sgu_ln_g", "sgu_ln_b", "w_spatial",
           "b_spatial", "w_branch_gla", "w_branch_sgu", "w_out", "norm_post_mix", "norm_pre_ffn", "w_ffn_in",
           "w_ffn_out", "norm_post_ffn")
BIG = ("w_in", "w_branch_gla", "w_branch_sgu", "w_out", "w_ffn_in", "w_ffn_out")
MIXER = ("w_branch_gla", "w_branch_sgu", "w_out")
FFN = ("w_ffn_in", "w_ffn_out")
COLUMN_SHARDED = ("w_in", "w_ffn_in")
SMALL = tuple(n for n in WEIGHTS if n not in BIG)
SMALL_SHARDED = ("w_gate_up", "gla_norm", "sgu_ln_g", "sgu_ln_b")
SMALL_FULL = {"norm_pre_mix": (1024,), "w_gate_up": (16, 512), "b_gate": (512,), "gla_norm": (4, 256),
              "sgu_ln_g": (4, 256), "sgu_ln_b": (4, 256), "w_spatial": (4, 128, 128), "b_spatial": (4, 128),
              "norm_post_mix": (1024,), "norm_pre_ffn": (1024,), "norm_post_ffn": (1024,)}
SMALL_GRAD_ROWS = 648
SMALL_STATE_ROWS = 600
SMALL_GATHER_ROWS = 32


def kernel(x, norm_pre_mix, w_in, w_gate_up, b_gate, gla_norm, sgu_ln_g, sgu_ln_b, w_spatial, b_spatial, w_branch_gla, w_branch_sgu, w_out, norm_post_mix, norm_pre_ffn, w_ffn_in, w_ffn_out, norm_post_ffn, loss_target, m_norm_pre_mix, m_w_in, m_w_gate_up, m_b_gate, m_gla_norm, m_sgu_ln_g, m_sgu_ln_b, m_w_spatial, m_b_spatial, m_w_branch_gla, m_w_branch_sgu, m_w_out, m_norm_post_mix, m_norm_pre_ffn, m_w_ffn_in, m_w_ffn_out, m_norm_post_ffn, v_norm_pre_mix, v_w_in, v_w_gate_up, v_b_gate, v_gla_norm, v_sgu_ln_g, v_sgu_ln_b, v_w_spatial, v_b_spatial, v_w_branch_gla, v_w_branch_sgu, v_w_out, v_norm_post_mix, v_norm_pre_ffn, v_w_ffn_in, v_w_ffn_out, v_norm_post_ffn):
    given = dict(locals())
    def local(a, n):
        return a[0].T if n in COLUMN_SHARDED else a[0]

    w = {n: local(given[n], n) for n in WEIGHTS}
    m = {n: local(given["m_" + n], n) for n in WEIGHTS}
    v = {n: local(given["v_" + n], n) for n in WEIGHTS}
    xs, target = x[0], loss_target[0]
    me = 4 * lax.axis_index("x") + 2 * lax.axis_index("y") + lax.axis_index("c")

    small_shard = _pack_rows([w[n] for n in SMALL_SHARDED], SMALL_GATHER_ROWS)
    first = _all_gather_hbm([w["w_in"].astype(BF16), small_shard], name="gather_w_in")
    rest, first = lax.optimization_barrier(([w[n].astype(BF16) for n in MIXER + FFN], first))
    rest_blocks = _all_gather_async(rest, name="gather_rest", collective_id=1)
    W = {n: w[n] for n in SMALL if n not in SMALL_SHARDED}
    blocks = dict(zip(MIXER + FFN, rest_blocks))
    for n in ("w_branch_gla", "w_branch_sgu", "w_out", "w_ffn_out"):
        W[n] = blocks[n].reshape(-1, D_MODEL)
    W["w_ffn_in"] = blocks["w_ffn_in"]
    W["w_in_t"] = _join_blocks_call(first[0], name="join_w_in")
    small_blocks = first[1]
    off = 0
    for n in SMALL_SHARDED:
        r, c = w[n].shape
        blk = small_blocks[:, off:off + r * c // LANES].reshape(N_DEV, r, c)
        W[n] = blk.transpose(1, 0, 2).reshape(r, N_DEV * c)
        off += _tile_rows(r * c)

    scatter_ids = iter((3, 4, 5))

    def scatter(names, parts):
        got = _scatter_blocks_async(parts, name="scatter_" + "_".join(names), collective_id=next(scatter_ids))
        return dict(zip(names, got))

    def finish(received):
        return {n: _adamw_reduce_call(r, w[n], m[n], v[n], name="adamw_" + n) for n, r in received.items()}

    loss_part, grad_x, grads, big_done = _local_step(xs, target, W, scatter, finish)

    small_part = jnp.concatenate([_pack_rows([grads[n] for n in SMALL], SMALL_GRAD_ROWS),
                                  jnp.broadcast_to(loss_part, (8, LANES))], axis=0)
    small_sum = _all_reduce_small(small_part, name="reduce_small")
    loss = small_sum[SMALL_GRAD_ROWS, 0]
    g_small = dict(zip(SMALL, _unpack_rows(small_sum, [SMALL_FULL[n] for n in SMALL])))
    for n in SMALL_SHARDED:
        c = w[n].shape[1]
        g_small[n] = lax.dynamic_slice_in_dim(g_small[n], me * c, c, axis=1)

    g_big, d_big, m_big, v_big = [{n: big_done[n][i] for n in BIG} for i in range(4)]
    small_shapes = [w[n].shape for n in SMALL]
    small_state = [_pack_rows([s[n] for n in SMALL], SMALL_STATE_ROWS) for s in (w, g_small, m, v)]
    small_out = _adamw_small_call(*small_state, name="adamw_small")
    d_small, m_small, v_small = [dict(zip(SMALL, _unpack_rows(o, small_shapes))) for o in small_out]

    def pick(big, sml):
        return [(big[n].T if n in COLUMN_SHARDED else big[n] if n in BIG else sml[n])[None] for n in WEIGHTS]

    return (loss, grad_x[None], *pick(g_big, g_small), *pick(d_big, d_small), *pick(m_big, m_small),
            *pick(v_big, v_small))
```

```python
import jax
import jax.numpy as jnp
from jax import lax
from jax.experimental import pallas as pl
from jax.experimental.pallas import tpu as pltpu
from jax.experimental.pallas import tpu_sc as plsc

F32 = jnp.float32
BF16 = jnp.bfloat16

D_MODEL = 1024
N_DEV = 8
CHUNK = 64
HEADS = 4
DK = 128
DV = 256
QK = HEADS * DK
GV = HEADS * DV
RANK = 16
GROUPS = 4
SBLOCK = 128
DG = 256
D_FF = 2816
FF_BLK = 704
EPS = 1e-6
Q_SCALE = DK ** -0.5
LANES = 128
VMEM_BIG = 48 * 1024 * 1024

D_IN = 7184
IN_BLK = 898
A_COLS = (0, 3072)
LOW_COLS = (3072, 3088)
S_COLS = (3088, 5136)
G_COLS = (5136, 7184)

ADAM_LR = 0.001
ADAM_B1 = 0.9
ADAM_B2 = 0.999
ADAM_EPS = 1e-08
ADAM_WD = 0.01
ADAM_STEP = 10

MESH = pl.DeviceIdType.MESH
ANY = pl.BlockSpec(memory_space=pl.ANY)
VMEM_SPEC = pl.BlockSpec(memory_space=pltpu.VMEM)


def _cp(sem=None, vmem=None):
    return pltpu.CompilerParams(dimension_semantics=sem, vmem_limit_bytes=vmem)


def _sig(x):
    return 0.5 * jnp.tanh(0.5 * x) + 0.5


GELU_C = 0.7978845608028654
GELU_A = 0.044715


def _gelu(x):
    t = jnp.tanh((GELU_C * x) * (1.0 + GELU_A * (x * x)))
    return (0.5 * x) * (1.0 + t)


def _gelu_and_grad(x):
    x2 = x * x
    t = jnp.tanh((GELU_C * x) * (1.0 + GELU_A * x2))
    one_t = 1.0 + t
    hx = 0.5 * x
    grad = 0.5 * one_t + (hx * (1.0 - t * t)) * (GELU_C + (3.0 * GELU_A * GELU_C) * x2)
    return hx * one_t, grad


def _logsig(x):
    return jnp.minimum(x, 0.0) - jnp.log1p(jnp.exp(-jnp.abs(x)))


def _dot(a, b, dims):
    return lax.dot_general(a, b, (dims, ((), ())), preferred_element_type=F32)


NN = ((1,), (0,))
NT = ((1,), (1,))
TN = ((0,), (0,))


def _exact_mask_dot(mask_bf16, x):
    hi = x.astype(BF16)
    r1 = x - hi.astype(F32)
    mid = r1.astype(BF16)
    lo = (r1 - mid.astype(F32)).astype(BF16)
    return _dot(mask_bf16, hi, NN) + _dot(mask_bf16, mid, NN) + _dot(mask_bf16, lo, NN)


def _pick_tile(dim, target):
    if dim <= target:
        return dim
    best = None
    for t in range(LANES, int(1.4 * target) + 1, LANES):
        if dim % t == 0:
            best = t
    assert best is not None, (dim, target)
    return best


def _mm_call(a, b, *, name, grid, a_spec, b_spec, o_spec, out_shape, dims, acc_shape):
    nk = grid[2]

    def body(a_ref, b_ref, o_ref, *acc):
        part = _dot(a_ref[...], b_ref[...], dims)
        if nk == 1:
            o_ref[...] = part.astype(o_ref.dtype)
        else:
            acc_ref = acc[0]
            k = pl.program_id(2)

            @pl.when(k == 0)
            def _():
                acc_ref[...] = part

            @pl.when(k > 0)
            def _():
                acc_ref[...] += part

            @pl.when(k == nk - 1)
            def _():
                o_ref[...] = acc_ref[...].astype(o_ref.dtype)

    return pl.pallas_call(
        body, name=name, grid=grid, in_specs=[a_spec, b_spec], out_specs=o_spec, out_shape=out_shape,
        scratch_shapes=[] if nk == 1 else [pltpu.VMEM(acc_shape, F32)],
        compiler_params=_cp(("parallel", "parallel", "arbitrary"), VMEM_BIG),
    )(a, b)


def _mm(a, b, mode, out_dtype, *, name, tm=512, tn=1024, tk=1024):
    if mode == "nn":
        (M, K), (_, N) = a.shape, b.shape
    elif mode == "nt":
        (M, K), (N, _) = a.shape, b.shape
    else:
        (K, M), (_, N) = a.shape, b.shape
    tm, tn, tk = _pick_tile(M, tm), _pick_tile(N, tn), _pick_tile(K, tk)
    if mode == "nn":
        a_spec = pl.BlockSpec((tm, tk), lambda i, j, k: (i, k))
        b_spec = pl.BlockSpec((tk, tn), lambda i, j, k: (k, j))
        dims = NN
    elif mode == "nt":
        a_spec = pl.BlockSpec((tm, tk), lambda i, j, k: (i, k))
        b_spec = pl.BlockSpec((tn, tk), lambda i, j, k: (j, k))
        dims = NT
    else:
        a_spec = pl.BlockSpec((tk, tm), lambda i, j, k: (k, i))
        b_spec = pl.BlockSpec((tk, tn), lambda i, j, k: (k, j))
        dims = TN
    return _mm_call(a, b, name=name, grid=(M // tm, N // tn, K // tk), a_spec=a_spec, b_spec=b_spec,
                    o_spec=pl.BlockSpec((tm, tn), lambda i, j, k: (i, j)),
                    out_shape=jax.ShapeDtypeStruct((M, N), out_dtype), dims=dims, acc_shape=(tm, tn))


ROW_TILE = 512
SUB_ROWS = 256


def _sub_tiles(T):
    return [pl.ds(r0, min(SUB_ROWS, T)) for r0 in range(0, T, SUB_ROWS)]


def _rt(T, W, c=0):
    return pl.BlockSpec((T, W), lambda i: (i, c))


def _rt3(nb, T, W):
    return pl.BlockSpec((nb, T, W), lambda i: (0, i, 0))


def _res(shape):
    nd = len(shape)
    return pl.BlockSpec(tuple(shape), lambda i: (0,) * nd, pipeline_mode=pl.Buffered(1))


def _acc(shape):
    nd = len(shape)
    return pl.BlockSpec(tuple(shape), lambda i: (0,) * nd, pipeline_mode=pl.Buffered(1))


def _nbytes(shape, dtype):
    n = jnp.dtype(dtype).itemsize
    for s in shape:
        n *= s
    return n


def _vmem_limit(tiles, resident, temps=16 * 1024 * 1024):
    need = 2 * sum(_nbytes(s, d) for s, d in tiles) + sum(_nbytes(s, d) for s, d in resident) + temps
    return min(need, 60 * 1024 * 1024)


def _tok_call(body, *, name, S, T, ins, outs, semantics="parallel"):
    tiles = [(spec.block_shape, a.dtype) for a, spec, kind in ins + outs if kind == "tile"]
    resident = [(a.shape, a.dtype) for a, spec, kind in ins + outs if kind == "res"]
    return pl.pallas_call(
        body, name=name, grid=(S // T,),
        in_specs=[spec for _, spec, _ in ins], out_specs=[spec for _, spec, _ in outs],
        out_shape=[jax.ShapeDtypeStruct(a.shape, a.dtype) for a, _, _ in outs],
        compiler_params=_cp((semantics,), _vmem_limit(tiles, resident)),
    )(*[a for a, _, _ in ins])


def _tile(a, spec):
    return (a, spec, "tile")


def _whole(a):
    return (a, _res(a.shape), "res")


def _out_tile(shape, dtype, spec):
    return (jax.ShapeDtypeStruct(shape, dtype), spec, "tile")


def _out_acc(shape, dtype=F32):
    return (jax.ShapeDtypeStruct(shape, dtype), _acc(shape), "res")


def _rms_stats(x):
    r = lax.rsqrt(jnp.mean(x * x, axis=-1, keepdims=True) + EPS)
    return r, x * r


def _rms_bwd(xh, r, g, dy):
    dxh = dy * g
    dx = r * (dxh - xh * jnp.mean(dxh * xh, axis=-1, keepdims=True))
    dg = jnp.sum(dy * xh, axis=0, keepdims=True)
    return dx, dg


def _accum(ref, val):
    @pl.when(pl.program_id(0) == 0)
    def _():
        ref[...] = val

    @pl.when(pl.program_id(0) > 0)
    def _():
        ref[...] += val


def _dot_rows_t(a, w_ref, row0, o_ref, chunk=1024):
    n = o_ref.shape[1]
    for n0 in range(0, n, chunk):
        n1 = min(n, n0 + chunk)
        o_ref[:, n0:n1] = _dot(a, w_ref[row0 + n0:row0 + n1, :], NT).astype(o_ref.dtype)


def _in_proj_call(x, g1, w_blocks, *, name, T=ROW_TILE):
    S = x.shape[0]

    def body(x_ref, g_ref, blk_ref, a_ref, pa_ref, al_ref, ps_ref, pg_ref, w_ref):
        @pl.when(pl.program_id(0) == 0)
        def _():
            for d in range(N_DEV):
                w_ref[d * IN_BLK:(d + 1) * IN_BLK, :] = blk_ref[d]

        _, xh = _rms_stats(x_ref[...])
        a = (xh * g_ref[...]).astype(BF16)
        a_ref[...] = a
        _dot_rows_t(a, w_ref, A_COLS[0], pa_ref)
        _dot_rows_t(a, w_ref, LOW_COLS[0], al_ref)
        _dot_rows_t(a, w_ref, S_COLS[0], ps_ref)
        _dot_rows_t(a, w_ref, G_COLS[0], pg_ref)

    widths = (D_MODEL, A_COLS[1] - A_COLS[0], LANES, S_COLS[1] - S_COLS[0], G_COLS[1] - G_COLS[0])
    return _tok_call(
        body, name=name, S=S, T=T, semantics="arbitrary",
        ins=[_tile(x, _rt(T, D_MODEL)), _whole(g1), _whole(w_blocks)],
        outs=[_out_tile((S, w), BF16, _rt(T, w)) for w in widths] + [_out_acc((D_IN, D_MODEL), BF16)])


def _mixer_tail_call(y_gla, y_sgu, pg, x, w_bg, w_bs, w_out, g2, g3, *, name, T=ROW_TILE):
    S = x.shape[0]

    def body(yg_ref, ys_ref, pg_ref, x_ref, wbg_ref, wbs_ref, wo_ref, g2_ref, g3_ref,
             t1_ref, t2_ref, mg_ref, mix_ref, x1_ref, h_ref):
        for rows in _sub_tiles(T):
            t1 = _dot(yg_ref[rows, :], wbg_ref[...], NN)
            t2 = _dot(ys_ref[rows, :], wbs_ref[...], NN)
            t1_ref[rows, :] = t1.astype(BF16)
            t2_ref[rows, :] = t2.astype(BF16)
            sg = _sig(pg_ref[rows, pl.ds(0, D_MODEL)].astype(F32))
            ss = _sig(pg_ref[rows, pl.ds(D_MODEL, D_MODEL)].astype(F32))
            merged = (sg * t1 + ss * t2).astype(BF16)
            mg_ref[rows, :] = merged
            mix = _dot(merged, wo_ref[...], NN)
            mix_ref[rows, :] = mix
            _, mh = _rms_stats(mix)
            x1 = x_ref[rows, :] + mh * g2_ref[...]
            x1_ref[rows, :] = x1
            _, xh = _rms_stats(x1)
            h_ref[rows, :] = (xh * g3_ref[...]).astype(BF16)

    row = _rt(T, D_MODEL)
    b16 = lambda: _out_tile((S, D_MODEL), BF16, row)
    f32 = lambda: _out_tile((S, D_MODEL), F32, row)
    return _tok_call(
        body, name=name, S=S, T=T,
        ins=[_tile(y_gla, row), _tile(y_sgu, row), _tile(pg, _rt(T, 2 * D_MODEL)), _tile(x, row),
             _whole(w_bg), _whole(w_bs), _whole(w_out), _whole(g2), _whole(g3)],
        outs=[b16(), b16(), b16(), f32(), f32(), b16()])


def _ffn_in_call(h, wfi, *, name, T=ROW_TILE):
    S = h.shape[0]

    def body(h_ref, w_ref, gu_ref, z_ref):
        hv = h_ref[...]
        for d in range(4):
            gate = _dot(hv, w_ref[d], NT)
            up = _dot(hv, w_ref[d + 4], NT)
            gu_ref[d] = gate.astype(BF16)
            gu_ref[d + 4] = up.astype(BF16)
            z_ref[d] = (gate * _sig(gate) * up).astype(BF16)

    return _tok_call(
        body, name=name, S=S, T=T,
        ins=[_tile(h, _rt(T, D_MODEL)), _whole(wfi)],
        outs=[_out_tile((N_DEV, S, FF_BLK), BF16, _rt3(N_DEV, T, FF_BLK)),
              _out_tile((4, S, FF_BLK), BF16, _rt3(4, T, FF_BLK))])


def _ffn_out_loss_call(z, wfo, x1, target, g4, *, name, T=ROW_TILE):
    S = x1.shape[0]

    def body(z_ref, w_ref, x1_ref, t_ref, g4_ref, loss_ref, dx2_ref, dy_ref, dg4_ref):
        loss = jnp.zeros((1, 1), F32)
        dg4 = jnp.zeros((1, D_MODEL), F32)
        for rows in _sub_tiles(T):
            y = _dot(z_ref[0, rows, :], w_ref[0], NN)
            for d in range(1, 4):
                y = y + _dot(z_ref[d, rows, :], w_ref[d], NN)
            r, yh = _rms_stats(y)
            diff = x1_ref[rows, :] + yh * g4_ref[...] - t_ref[rows, :]
            loss = loss + 0.5 * jnp.sum(jnp.mean(diff * diff, axis=-1, keepdims=True), axis=0, keepdims=True)
            dx2 = diff * (1.0 / D_MODEL)
            dx2_ref[rows, :] = dx2
            dy, dg = _rms_bwd(yh, r, g4_ref[...], dx2)
            dy_ref[rows, :] = dy.astype(BF16)
            dg4 = dg4 + dg
        _accum(loss_ref, jnp.broadcast_to(loss, (1, LANES)))
        _accum(dg4_ref, dg4)

    row = _rt(T, D_MODEL)
    return _tok_call(
        body, name=name, S=S, T=T, semantics="arbitrary",
        ins=[_tile(z, _rt3(4, T, FF_BLK)), _whole(wfo), _tile(x1, row), _tile(target, row), _whole(g4)],
        outs=[_out_acc((1, LANES)), _out_tile((S, D_MODEL), F32, row), _out_tile((S, D_MODEL), BF16, row),
              _out_acc((1, D_MODEL))])


def _ffn_out_bwd_call(dy, wfo, gu, *, name, T=ROW_TILE):
    S = dy.shape[0]

    def body(dy_ref, w_ref, gu_ref, dgu_ref):
        dyv = dy_ref[...]
        for d in range(4):
            dz = _dot(dyv, w_ref[d], NT)
            gt = gu_ref[d].astype(F32)
            up = gu_ref[d + 4].astype(F32)
            s = _sig(gt)
            dgu_ref[d] = (dz * up * s * (1.0 + gt * (1.0 - s))).astype(BF16)
            dgu_ref[d + 4] = (dz * gt * s).astype(BF16)

    blocks = _rt3(N_DEV, T, FF_BLK)
    return _tok_call(
        body, name=name, S=S, T=T,
        ins=[_tile(dy, _rt(T, D_MODEL)), _whole(wfo), _tile(gu, blocks)],
        outs=[_out_tile((N_DEV, S, FF_BLK), BF16, blocks)])[0]


def _ffn_in_bwd_call(dgu, wfi, dx2, x1, mix, g3, g2, *, name, T=ROW_TILE):
    S = x1.shape[0]

    def body(dgu_ref, w_ref, dx2_ref, x1_ref, mix_ref, g3_ref, g2_ref, dx1_ref, dmix_ref, dg3_ref, dg2_ref):
        dg3 = jnp.zeros((1, D_MODEL), F32)
        dg2 = jnp.zeros((1, D_MODEL), F32)
        for rows in _sub_tiles(T):
            dh = _dot(dgu_ref[0, rows, :], w_ref[0], NN)
            for d in range(1, N_DEV):
                dh = dh + _dot(dgu_ref[d, rows, :], w_ref[d], NN)
            r3, xh = _rms_stats(x1_ref[rows, :])
            d3, g = _rms_bwd(xh, r3, g3_ref[...], dh)
            dg3 = dg3 + g
            dx1 = dx2_ref[rows, :] + d3
            dx1_ref[rows, :] = dx1
            r2, mh = _rms_stats(mix_ref[rows, :])
            dmix, g = _rms_bwd(mh, r2, g2_ref[...], dx1)
            dg2 = dg2 + g
            dmix_ref[rows, :] = dmix.astype(BF16)
        _accum(dg3_ref, dg3)
        _accum(dg2_ref, dg2)

    row = _rt(T, D_MODEL)
    return _tok_call(
        body, name=name, S=S, T=T, semantics="arbitrary",
        ins=[_tile(dgu, _rt3(N_DEV, T, FF_BLK)), _whole(wfi), _tile(dx2, row), _tile(x1, row), _tile(mix, row),
             _whole(g3), _whole(g2)],
        outs=[_out_tile((S, D_MODEL), F32, row), _out_tile((S, D_MODEL), BF16, row),
              _out_acc((1, D_MODEL)), _out_acc((1, D_MODEL))])


def _mixer_bwd_call(dmix, t1, t2, pg, w_out, w_bg, w_bs, *, name, T=ROW_TILE):
    S = dmix.shape[0]

    def body(dmix_ref, t1_ref, t2_ref, pg_ref, wo_ref, wbg_ref, wbs_ref, dt1_ref, dt2_ref, dpg_ref, dyg_ref, dys_ref):
        gg, gs = pl.ds(0, D_MODEL), pl.ds(D_MODEL, D_MODEL)
        for rows in _sub_tiles(T):
            dm = _dot(dmix_ref[rows, :], wo_ref[...], NT)
            sg = _sig(pg_ref[rows, gg].astype(F32))
            ss = _sig(pg_ref[rows, gs].astype(F32))
            dt1 = (dm * sg).astype(BF16)
            dt2 = (dm * ss).astype(BF16)
            dt1_ref[rows, :] = dt1
            dt2_ref[rows, :] = dt2
            dpg_ref[rows, gg] = (dm * t1_ref[rows, :].astype(F32) * sg * (1.0 - sg)).astype(BF16)
            dpg_ref[rows, gs] = (dm * t2_ref[rows, :].astype(F32) * ss * (1.0 - ss)).astype(BF16)
            dyg_ref[rows, :] = _dot(dt1, wbg_ref[...], NT).astype(BF16)
            dys_ref[rows, :] = _dot(dt2, wbs_ref[...], NT).astype(BF16)

    row = _rt(T, D_MODEL)
    wide = _rt(T, 2 * D_MODEL)
    b16 = lambda: _out_tile((S, D_MODEL), BF16, row)
    return _tok_call(
        body, name=name, S=S, T=T,
        ins=[_tile(dmix, row), _tile(t1, row), _tile(t2, row), _tile(pg, wide), _whole(w_out), _whole(w_bg), _whole(w_bs)],
        outs=[b16(), b16(), _out_tile((S, 2 * D_MODEL), BF16, wide), b16(), b16()])


def _in_proj_bwd_call(dpa, dlow, dps, dpg, w_in_t, x, dx1, g1, *, name, T=ROW_TILE):
    S = x.shape[0]

    def body(dpa_ref, dl_ref, dps_ref, dpg_ref, w_ref, x_ref, dx1_ref, g1_ref, gx_ref, dg1_ref):
        da = (_dot(dpa_ref[...], w_ref[A_COLS[0]:A_COLS[1], :], NN)
              + _dot(dl_ref[...], w_ref[LOW_COLS[0]:LOW_COLS[0] + LANES, :], NN)
              + _dot(dps_ref[...], w_ref[S_COLS[0]:S_COLS[1], :], NN)
              + _dot(dpg_ref[...], w_ref[G_COLS[0]:G_COLS[1], :], NN))
        r, xh = _rms_stats(x_ref[...])
        dxa, dg = _rms_bwd(xh, r, g1_ref[...], da)
        gx_ref[...] = dx1_ref[...] + dxa
        _accum(dg1_ref, dg)

    row = _rt(T, D_MODEL)
    return _tok_call(
        body, name=name, S=S, T=T, semantics="arbitrary",
        ins=[_tile(dpa, _rt(T, dpa.shape[1])), _tile(dlow, _rt(T, dlow.shape[1])), _tile(dps, _rt(T, dps.shape[1])),
             _tile(dpg, _rt(T, dpg.shape[1])), _whole(w_in_t), _tile(x, row), _tile(dx1, row), _whole(g1)],
        outs=[_out_tile((S, D_MODEL), F32, row), _out_acc((1, D_MODEL))])


TOKEN_TILE = 512


def _weight_grads_part(arrays, pairs, *, tk=TOKEN_TILE, out_dtype=BF16):
    S = arrays[0].shape[-2]
    tk = min(tk, S)
    n_in = len(arrays)

    def out_shape(i, j):
        a, b = arrays[i], arrays[j]
        if a.ndim == 3:
            return (a.shape[0], a.shape[2], b.shape[1])
        if b.ndim == 3:
            return (b.shape[0], a.shape[1], b.shape[2])
        return (a.shape[1], b.shape[1])

    shapes = [out_shape(i, j) for i, j in pairs]

    in_place = out_dtype == F32

    def body(ins, outs, accs):
        k = pl.program_id(0)
        if in_place:
            accs = outs

        @pl.when(k == 0)
        def _():
            for acc in accs:
                acc[...] = jnp.zeros_like(acc)

        for (i, j), acc in zip(pairs, accs):
            a_ref, b_ref = ins[i], ins[j]
            if len(a_ref.shape) == 3:
                for n in range(a_ref.shape[0]):
                    acc[n] += _dot(a_ref[n], b_ref[...], TN)
            elif len(b_ref.shape) == 3:
                a = a_ref[...]
                for n in range(b_ref.shape[0]):
                    acc[n] += _dot(a, b_ref[n], TN)
            else:
                b = b_ref[...]
                for m0 in range(0, a_ref.shape[1], 1024):
                    m1 = min(a_ref.shape[1], m0 + 1024)
                    acc[m0:m1, :] += _dot(a_ref[:, m0:m1], b, TN)

        if not in_place:
            @pl.when(k == S // tk - 1)
            def _():
                for out, acc in zip(outs, accs):
                    out[...] = acc[...].astype(out.dtype)

    def in_spec(a):
        if a.ndim == 3:
            return pl.BlockSpec((a.shape[0], tk, a.shape[2]), lambda k: (0, k, 0))
        return pl.BlockSpec((tk, a.shape[1]), lambda k: (k, 0))

    return dict(
        body=body, steps=S // tk, arrays=list(arrays),
        in_specs=[in_spec(a) for a in arrays],
        out_specs=[_acc(s) for s in shapes],
        out_shapes=[jax.ShapeDtypeStruct(s, out_dtype) for s in shapes],
        scratch=[] if in_place else [pltpu.VMEM(s, F32) for s in shapes],
        tiles=[(in_spec(a).block_shape, a.dtype) for a in arrays],
        resident=[(s, F32) for s in shapes] + ([] if in_place else [(s, BF16) for s in shapes]))


def _weight_grads_call(arrays, pairs, *, name, tk=TOKEN_TILE):
    part = _weight_grads_part(arrays, pairs, tk=tk)
    n_in, n_out = len(part["arrays"]), len(part["out_shapes"])

    def body(*refs):
        part["body"](refs[:n_in], refs[n_in:n_in + n_out], refs[n_in + n_out:])

    return pl.pallas_call(
        body, name=name, grid=(part["steps"],),
        in_specs=part["in_specs"], out_specs=part["out_specs"], out_shape=part["out_shapes"],
        scratch_shapes=part["scratch"],
        compiler_params=_cp(("arbitrary",), _vmem_limit(part["tiles"], part["resident"])),
    )(*part["arrays"])


def _with_rider(body, n_in, n_out, n_scratch, rider):
    if rider is None:
        return body
    r_in, r_out = len(rider["arrays"]), len(rider["out_shapes"])

    def both(*refs):
        ins, rest = refs[:n_in + r_in], refs[n_in + r_in:]
        outs, scratch = rest[:n_out + r_out], rest[n_out + r_out:]
        body(*ins[:n_in], *outs[:n_out], *scratch[:n_scratch])
        rider["body"](ins[n_in:], outs[n_out:], scratch[n_scratch:])

    return both


def _rider_lists(rider):
    if rider is None:
        return [], [], [], [], [], [], []
    return (rider["arrays"], rider["in_specs"], rider["out_specs"], rider["out_shapes"], rider["scratch"],
            rider["tiles"], rider["resident"])


def _ffn_in_grad_call(h, dgu, *, name, tk=TOKEN_TILE):
    S = h.shape[0]
    tk = min(tk, S)
    nb = 4
    shape = (nb, FF_BLK, D_MODEL)

    def body(h_ref, dgu_ref, out_ref, acc):
        k = pl.program_id(1)

        @pl.when(k == 0)
        def _():
            acc[...] = jnp.zeros_like(acc)

        hv = h_ref[...]
        for n in range(nb):
            acc[n] += _dot(dgu_ref[n], hv, TN)

        @pl.when(k == S // tk - 1)
        def _():
            out_ref[...] = acc[...].astype(out_ref.dtype)

    tiles = [((tk, D_MODEL), BF16), ((nb, tk, FF_BLK), BF16), (shape, BF16)]
    return pl.pallas_call(
        body, name=name, grid=(N_DEV // nb, S // tk),
        in_specs=[pl.BlockSpec((tk, D_MODEL), lambda g, k: (k, 0)),
                  pl.BlockSpec((nb, tk, FF_BLK), lambda g, k: (g, k, 0))],
        out_specs=pl.BlockSpec(shape, lambda g, k: (g, 0, 0)),
        out_shape=jax.ShapeDtypeStruct((N_DEV, FF_BLK, D_MODEL), BF16),
        scratch_shapes=[pltpu.VMEM(shape, F32)],
        compiler_params=_cp(("parallel", "arbitrary"), _vmem_limit(tiles, [(shape, F32)])),
    )(h, dgu)


GLA_TILE = 256
Q_OFF, K_OFF, V_OFF, R_OFF = 0, QK, 2 * QK, 2 * QK + GV


def _tri(lower):
    r = lax.broadcasted_iota(jnp.int32, (CHUNK, CHUNK), 0)
    c = lax.broadcasted_iota(jnp.int32, (CHUNK, CHUNK), 1)
    return jnp.where((r >= c) if lower else (c >= r), 1.0, 0.0).astype(BF16)


def _gla_fwd_call(pa, alow, wg, bgate, gnorm, *, name):
    S = pa.shape[0]
    Tg = min(GLA_TILE, S)
    cb = Tg // CHUNK

    def body(pa_ref, al_ref, wg_ref, bg_ref, gn_ref, y_ref, st_ref, state):
        @pl.when(pl.program_id(0) == 0)
        def _():
            state[...] = jnp.zeros_like(state)

        logit = _dot(al_ref[...], wg_ref[...], NN) + bg_ref[...]
        ls = _logsig(logit) * (1.0 / 16.0)
        tri = _tri(True)
        for c in range(cb):
            rows = pl.ds(c * CHUNK, CHUNK)
            cum = _exact_mask_dot(tri, ls[c * CHUNK:(c + 1) * CHUNK])
            tot = cum[CHUNK - 1:CHUNK]
            kd = (pa_ref[rows, pl.ds(K_OFF, QK)].astype(F32) * jnp.exp(tot - cum)).astype(BF16)
            decay = jnp.exp(tot)
            for h in range(HEADS):
                lanes = slice(h * DK, (h + 1) * DK)
                new = state[h] * decay[:, lanes] + _dot(pa_ref[rows, pl.ds(V_OFF + h * DV, DV)], kd[:, lanes], TN)
                state[h] = new
                st_ref[c, h] = new
        for c in range(cb):
            rows = pl.ds(c * CHUNK, CHUNK)
            for h in range(HEADS):
                qs = (pa_ref[rows, pl.ds(Q_OFF + h * DK, DK)].astype(F32) * Q_SCALE).astype(BF16)
                o = _dot(qs, st_ref[c, h].astype(BF16), NT)
                rs = lax.rsqrt(jnp.mean(o * o, axis=-1, keepdims=True) + EPS)
                rr = pa_ref[rows, pl.ds(R_OFF + h * DV, DV)].astype(F32)
                y_ref[rows, pl.ds(h * DV, DV)] = (o * rs * gn_ref[h] * (rr * _sig(rr))).astype(BF16)

    return pl.pallas_call(
        body, name=name, grid=(S // Tg,),
        in_specs=[
            pl.BlockSpec((Tg, 2 * QK + 2 * GV), lambda t: (t, 0)),
            pl.BlockSpec((Tg, LANES), lambda t: (t, 0)),
            pl.BlockSpec((LANES, QK), lambda t: (0, 0)),
            pl.BlockSpec((1, QK), lambda t: (0, 0)),
            pl.BlockSpec((HEADS, 1, DV), lambda t: (0, 0, 0)),
        ],
        out_specs=[
            pl.BlockSpec((Tg, GV), lambda t: (t, 0)),
            pl.BlockSpec((cb, HEADS, DV, DK), lambda t: (t, 0, 0, 0)),
        ],
        out_shape=[jax.ShapeDtypeStruct((S, GV), BF16),
                   jax.ShapeDtypeStruct((S // CHUNK, HEADS, DV, DK), F32)],
        scratch_shapes=[pltpu.VMEM((HEADS, DV, DK), F32)],
        compiler_params=_cp(("arbitrary",), VMEM_BIG),
    )(pa, alow, wg, bgate, gnorm)


def _gla_bwd_call(pa, alow, wg, bgate, gnorm, states, dy, *, name, rider=None):
    S = pa.shape[0]
    Tg = min(GLA_TILE, S)
    cb = Tg // CHUNK
    nt = S // Tg
    r_arrays, r_in, r_out, r_shapes, r_scratch, r_tiles, r_resident = _rider_lists(rider)
    assert rider is None or rider["steps"] == nt

    def rev(t):
        return nt - 1 - t

    def body(pa_ref, al_ref, wg_ref, bg_ref, gn_ref, st_ref, prev_ref, dy_ref,
             dpa_ref, dl_ref, dgn_ref, dbg_ref, carry, pbuf):
        t = pl.program_id(0)

        @pl.when(t == 0)
        def _():
            carry[...] = jnp.zeros_like(carry)
            dgn_ref[...] = jnp.zeros_like(dgn_ref)
            dbg_ref[...] = jnp.zeros_like(dbg_ref)

        logit = _dot(al_ref[...], wg_ref[...], NN) + bg_ref[...]
        ls = _logsig(logit) * (1.0 / 16.0)
        sneg = 1.0 / (1.0 + jnp.exp(logit))
        tri = _tri(True)
        upper = _tri(False)
        first_tile = rev(t) == 0
        heads = range(HEADS)

        w, decay, kd = [], [], []
        for c in range(cb):
            rows = pl.ds(c * CHUNK, CHUNK)
            cum = _exact_mask_dot(tri, ls[c * CHUNK:(c + 1) * CHUNK])
            tot = cum[CHUNK - 1:CHUNK]
            w.append(jnp.exp(tot - cum))
            decay.append(jnp.exp(tot))
            kd.append(pa_ref[rows, pl.ds(K_OFF, QK)].astype(F32) * w[c])

        dgn = [jnp.zeros((1, DV), F32) for _ in heads]
        for c in range(cb):
            rows = pl.ds(c * CHUNK, CHUNK)
            for h in heads:
                gn = gn_ref[h]
                qs = (pa_ref[rows, pl.ds(Q_OFF + h * DK, DK)].astype(F32) * Q_SCALE).astype(BF16)
                st16 = st_ref[c, h].astype(BF16)
                o = _dot(qs, st16, NT)
                rs = lax.rsqrt(jnp.mean(o * o, axis=-1, keepdims=True) + EPS)
                oh = o * rs
                rr = pa_ref[rows, pl.ds(R_OFF + h * DV, DV)].astype(F32)
                sr = _sig(rr)
                dyv = dy_ref[rows, pl.ds(h * DV, DV)].astype(F32)
                dpa_ref[rows, pl.ds(R_OFF + h * DV, DV)] = (
                    dyv * oh * gn * sr * (1.0 + rr * (1.0 - sr))).astype(BF16)
                don = dyv * (rr * sr)
                dgn[h] = dgn[h] + jnp.sum(don * oh, axis=0, keepdims=True)
                doh = don * gn
                do16 = (rs * (doh - oh * jnp.mean(doh * oh, axis=-1, keepdims=True))).astype(BF16)
                dpa_ref[rows, pl.ds(Q_OFF + h * DK, DK)] = (_dot(do16, st16, NN) * Q_SCALE).astype(BF16)
                pbuf[c, h] = _dot(do16, qs, TN)
        for h in heads:
            dgn_ref[h] += dgn[h]

        dkd = [[None] * HEADS for _ in range(cb)]
        ddecay = [[None] * HEADS for _ in range(cb)]
        for c in reversed(range(cb)):
            rows = pl.ds(c * CHUNK, CHUNK)
            for h in heads:
                lanes = slice(h * DK, (h + 1) * DK)
                gt = pbuf[c, h] + carry[h]
                gt16 = gt.astype(BF16)
                dkd[c][h] = _dot(pa_ref[rows, pl.ds(V_OFF + h * DV, DV)], gt16, NN)
                dpa_ref[rows, pl.ds(V_OFF + h * DV, DV)] = _dot(kd[c][:, lanes].astype(BF16), gt16, NT).astype(BF16)
                if c > 0:
                    st_prev = st_ref[c - 1, h]
                else:
                    st_prev = jnp.where(first_tile, 0.0, prev_ref[0, h])
                ddecay[c][h] = jnp.sum(gt * st_prev, axis=0, keepdims=True)
                carry[h] = gt * decay[c][:, lanes]

        dbg = jnp.zeros((1, QK), F32)
        for c in range(cb):
            rows = pl.ds(c * CHUNK, CHUNK)
            dkd_c = jnp.concatenate(dkd[c], axis=1)
            dpa_ref[rows, pl.ds(K_OFF, QK)] = (dkd_c * w[c]).astype(BF16)
            e = dkd_c * kd[c]
            dtot = jnp.sum(e, axis=0, keepdims=True) + jnp.concatenate(ddecay[c], axis=1) * decay[c]
            dls = dtot - _exact_mask_dot(upper, e)
            dlogit = dls * (1.0 / 16.0) * sneg[c * CHUNK:(c + 1) * CHUNK]
            dl_ref[rows, :] = dlogit.astype(BF16)
            dbg = dbg + jnp.sum(dlogit, axis=0, keepdims=True)
        dbg_ref[...] += dbg

    wide = 2 * QK + 2 * GV
    tiles = [((Tg, wide), BF16), ((cb + 1, HEADS, DV, DK), F32), ((Tg, GV), BF16), ((Tg, wide), BF16),
             ((Tg, QK), BF16)] + r_tiles
    resident = [((cb + 1, HEADS, DV, DK), F32)] + r_resident
    return pl.pallas_call(
        _with_rider(body, 8, 4, 2, rider), name=name, grid=(nt,),
        in_specs=[
            pl.BlockSpec((Tg, wide), lambda t: (rev(t), 0)),
            pl.BlockSpec((Tg, LANES), lambda t: (rev(t), 0)),
            pl.BlockSpec((LANES, QK), lambda t: (0, 0)),
            pl.BlockSpec((1, QK), lambda t: (0, 0)),
            pl.BlockSpec((HEADS, 1, DV), lambda t: (0, 0, 0)),
            pl.BlockSpec((cb, HEADS, DV, DK), lambda t: (rev(t), 0, 0, 0)),
            pl.BlockSpec((1, HEADS, DV, DK), lambda t: (jnp.maximum(rev(t) * cb - 1, 0), 0, 0, 0)),
            pl.BlockSpec((Tg, GV), lambda t: (rev(t), 0)),
        ] + r_in,
        out_specs=[
            pl.BlockSpec((Tg, wide), lambda t: (rev(t), 0)),
            pl.BlockSpec((Tg, QK), lambda t: (rev(t), 0)),
            pl.BlockSpec((HEADS, 1, DV), lambda t: (0, 0, 0)),
            pl.BlockSpec((1, QK), lambda t: (0, 0)),
        ] + r_out,
        out_shape=[jax.ShapeDtypeStruct((S, wide), BF16), jax.ShapeDtypeStruct((S, QK), BF16),
                   jax.ShapeDtypeStruct((HEADS, 1, DV), F32), jax.ShapeDtypeStruct((1, QK), F32)] + r_shapes,
        scratch_shapes=[pltpu.VMEM((HEADS, DV, DK), F32), pltpu.VMEM((cb, HEADS, DV, DK), F32)] + r_scratch,
        compiler_params=_cp(("arbitrary",), _vmem_limit(tiles, resident)),
    )(pa, alow, wg, bgate, gnorm, states, states, dy, *r_arrays)


SGU_TILE = 256


def _sgu_mask():
    r = lax.broadcasted_iota(jnp.int32, (SBLOCK, SBLOCK), 0)
    c = lax.broadcasted_iota(jnp.int32, (SBLOCK, SBLOCK), 1)
    return (c < CHUNK) | (r >= CHUNK)


def _ln_stats(vf):
    mu = jnp.mean(vf, axis=-1, keepdims=True)
    xc = vf - mu
    rs = lax.rsqrt(jnp.mean(xc * xc, axis=-1, keepdims=True) + EPS)
    return rs, xc * rs


def _sgu_fwd_call(ps, ln_g, ln_b, w_sp, b_sp, *, name):
    S = ps.shape[0]
    Ts = min(SGU_TILE, S)

    def body(ps_ref, lg_ref, lb_ref, w_ref, b_ref, y_ref):
        mask = _sgu_mask()
        for g in range(GROUPS):
            wm = jnp.where(mask, w_ref[g], 0.0).astype(BF16)
            for p in range(Ts // SBLOCK):
                rows = pl.ds(p * SBLOCK, SBLOCK)
                u = _gelu(ps_ref[rows, pl.ds(g * DG, DG)].astype(F32))
                _, xh = _ln_stats(_gelu(ps_ref[rows, pl.ds(D_MODEL + g * DG, DG)].astype(F32)))
                vn = xh * lg_ref[g] + lb_ref[g]
                mixed = _dot(wm, vn.astype(BF16), NN) + b_ref[g]
                y_ref[rows, pl.ds(g * DG, DG)] = (u * mixed).astype(BF16)

    full3 = lambda a, b, c: pl.BlockSpec((a, b, c), lambda t: (0, 0, 0))
    return pl.pallas_call(
        body, name=name, grid=(S // Ts,),
        in_specs=[pl.BlockSpec((Ts, 2 * D_MODEL), lambda t: (t, 0)),
                  full3(GROUPS, 1, DG), full3(GROUPS, 1, DG), full3(GROUPS, SBLOCK, SBLOCK), full3(GROUPS, SBLOCK, 1)],
        out_specs=pl.BlockSpec((Ts, D_MODEL), lambda t: (t, 0)),
        out_shape=jax.ShapeDtypeStruct((S, D_MODEL), BF16),
        compiler_params=_cp(("parallel",)),
    )(ps, ln_g, ln_b, w_sp, b_sp)


def _sgu_bwd_call(ps, ln_g, ln_b, w_sp, b_sp, dy, *, name, rider=None):
    S = ps.shape[0]
    Ts = min(SGU_TILE, S)
    r_arrays, r_in, r_out, r_shapes, r_scratch, r_tiles, r_resident = _rider_lists(rider)
    assert rider is None or rider["steps"] == S // Ts

    def body(ps_ref, lg_ref, lb_ref, w_ref, b_ref, dy_ref, ds_ref, dlg_ref, dlb_ref, dw_ref, db_ref):
        @pl.when(pl.program_id(0) == 0)
        def _():
            dlg_ref[...] = jnp.zeros_like(dlg_ref)
            dlb_ref[...] = jnp.zeros_like(dlb_ref)
            dw_ref[...] = jnp.zeros_like(dw_ref)
            db_ref[...] = jnp.zeros_like(db_ref)

        mask = _sgu_mask()
        for g in range(GROUPS):
            wm = jnp.where(mask, w_ref[g], 0.0).astype(BF16)
            lg = lg_ref[g]
            for p in range(Ts // SBLOCK):
                rows = pl.ds(p * SBLOCK, SBLOCK)
                su = ps_ref[rows, pl.ds(g * DG, DG)].astype(F32)
                sv = ps_ref[rows, pl.ds(D_MODEL + g * DG, DG)].astype(F32)
                u, du = _gelu_and_grad(su)
                gv, dgv = _gelu_and_grad(sv)
                rs, xh = _ln_stats(gv)
                vn16 = (xh * lg + lb_ref[g]).astype(BF16)
                mixed = _dot(wm, vn16, NN) + b_ref[g]
                dyv = dy_ref[rows, pl.ds(g * DG, DG)].astype(F32)
                ds_ref[rows, pl.ds(g * DG, DG)] = (dyv * mixed * du).astype(BF16)
                dmix = dyv * u
                dmix16 = dmix.astype(BF16)
                db_ref[g] += jnp.sum(dmix, axis=-1, keepdims=True)
                dw_ref[g] += jnp.where(mask, _dot(dmix16, vn16, NT), 0.0)
                dvn = _dot(wm, dmix16, TN)
                dlg_ref[g] += jnp.sum(dvn * xh, axis=0, keepdims=True)
                dlb_ref[g] += jnp.sum(dvn, axis=0, keepdims=True)
                dxh = dvn * lg
                dvf = rs * (dxh - jnp.mean(dxh, axis=-1, keepdims=True)
                            - xh * jnp.mean(dxh * xh, axis=-1, keepdims=True))
                ds_ref[rows, pl.ds(D_MODEL + g * DG, DG)] = (dvf * dgv).astype(BF16)

    full3 = lambda a, b, c: pl.BlockSpec((a, b, c), lambda t: (0, 0, 0))
    tiles = [((Ts, 2 * D_MODEL), BF16), ((Ts, D_MODEL), BF16), ((Ts, 2 * D_MODEL), BF16)] + r_tiles
    return pl.pallas_call(
        _with_rider(body, 6, 5, 0, rider), name=name, grid=(S // Ts,),
        in_specs=[pl.BlockSpec((Ts, 2 * D_MODEL), lambda t: (t, 0)),
                  full3(GROUPS, 1, DG), full3(GROUPS, 1, DG), full3(GROUPS, SBLOCK, SBLOCK), full3(GROUPS, SBLOCK, 1),
                  pl.BlockSpec((Ts, D_MODEL), lambda t: (t, 0))] + r_in,
        out_specs=[pl.BlockSpec((Ts, 2 * D_MODEL), lambda t: (t, 0)),
                   full3(GROUPS, 1, DG), full3(GROUPS, 1, DG), full3(GROUPS, SBLOCK, SBLOCK), full3(GROUPS, SBLOCK, 1)]
        + r_out,
        out_shape=[jax.ShapeDtypeStruct((S, 2 * D_MODEL), BF16),
                   jax.ShapeDtypeStruct((GROUPS, 1, DG), F32), jax.ShapeDtypeStruct((GROUPS, 1, DG), F32),
                   jax.ShapeDtypeStruct((GROUPS, SBLOCK, SBLOCK), F32),
                   jax.ShapeDtypeStruct((GROUPS, SBLOCK, 1), F32)] + r_shapes,
        scratch_shapes=r_scratch,
        compiler_params=_cp(("arbitrary",), _vmem_limit(tiles, r_resident)),
    )(ps, ln_g, ln_b, w_sp, b_sp, dy, *r_arrays)


def _position():
    return lax.axis_index("x"), lax.axis_index("y"), lax.axis_index("c")


def _gather_copies(srcs, dsts, send_sems, recv_sems, local_sems):
    x, y, c = _position()
    me, sibling = (x, y, c), (x, y, 1 - c)
    chips = [(1 - x, y), (x, 1 - y), (1 - x, 1 - y)]
    n = len(srcs)

    def slab(a, block):
        px, py, pc = block
        return dsts[a].at[4 * px + 2 * py + pc]

    def copy(a, k, block, to, src=None):
        return pltpu.make_async_remote_copy(
            src_ref=slab(a, block) if src is None else src, dst_ref=slab(a, block),
            send_sem=send_sems.at[7 * a + k], recv_sem=recv_sems.at[7 * a + k], device_id=to, device_id_type=MESH)

    mine = [pltpu.make_async_copy(srcs[a], slab(a, me), local_sems.at[a]) for a in range(n)]
    for cp in mine:
        cp.start()
    first = []
    for a in range(n):
        first.append(copy(a, 0, me, sibling, src=srcs[a]))
        first += [copy(a, 1 + j, me, (*chip, c), src=srcs[a]) for j, chip in enumerate(chips)]
    for cp in first:
        cp.start()
    passed = []
    for j, chip in enumerate(chips):
        for a in range(n):
            copy(a, 1 + j, (*chip, c), me).wait_recv()
            cp = copy(a, 4 + j, (*chip, c), sibling)
            cp.start()
            passed.append(cp)
    for a in range(n):
        copy(a, 0, sibling, me).wait_recv()
        for j, chip in enumerate(chips):
            copy(a, 4 + j, (*chip, 1 - c), me).wait_recv()
    for cp in first + passed:
        cp.wait_send()
    for cp in mine:
        cp.wait()


def _all_gather_hbm(shards, *, name):
    n = len(shards)

    def body(*refs):
        srcs, dsts = refs[:n], refs[n:2 * n]
        send_sems, recv_sems, local_sems = refs[2 * n:]
        _gather_copies(srcs, dsts, send_sems, recv_sems, local_sems)

    return pl.pallas_call(
        body, name=name,
        in_specs=[ANY] * n, out_specs=[ANY] * n,
        out_shape=[jax.ShapeDtypeStruct((N_DEV, *s.shape), s.dtype) for s in shards],
        scratch_shapes=_comm_sems(n),
    )(*shards)


def _all_reduce_small(part, *, name):
    R, W = part.shape

    def body(x_ref, out_ref, gathered, send_sems, recv_sems, local_sems):
        _gather_copies([x_ref], [gathered], send_sems, recv_sems, local_sems)
        acc = gathered[0]
        for d in range(1, N_DEV):
            acc = acc + gathered[d]
        out_ref[...] = acc

    return pl.pallas_call(
        body, name=name,
        in_specs=[VMEM_SPEC], out_specs=VMEM_SPEC,
        out_shape=jax.ShapeDtypeStruct((R, W), F32),
        scratch_shapes=[pltpu.VMEM((N_DEV, R, W), F32),
                        pltpu.SemaphoreType.DMA((7,)), pltpu.SemaphoreType.DMA((7,)), pltpu.SemaphoreType.DMA((1,))],
    )(part)


FLIPS = [(fx, fy, fc) for fx in (0, 1) for fy in (0, 1) for fc in (0, 1)][1:]


def _scatter_copies(srcs, dsts, send_sems, recv_sems, local_sems):
    n = len(srcs)
    x, y, c = _position()
    me = 4 * x + 2 * y + c
    mine = [pltpu.make_async_copy(srcs[a].at[me], dsts[a].at[me], local_sems.at[a]) for a in range(n)]
    for cp in mine:
        cp.start()
    copies = []
    for k, (fx, fy, fc) in enumerate(FLIPS):
        tx = 1 - x if fx else x
        ty = 1 - y if fy else y
        tc = 1 - c if fc else c
        peer = 4 * tx + 2 * ty + tc
        for a in range(n):
            cp = pltpu.make_async_remote_copy(
                src_ref=srcs[a].at[peer], dst_ref=dsts[a].at[me],
                send_sem=send_sems.at[7 * a + k], recv_sem=recv_sems.at[7 * a + k],
                device_id=(tx, ty, tc), device_id_type=MESH)
            cp.start()
            copies.append(cp)
    for cp in copies:
        cp.wait()
    for cp in mine:
        cp.wait()


def _comm_sems(n):
    return [pltpu.SemaphoreType.DMA((7 * n,)), pltpu.SemaphoreType.DMA((7 * n,)), pltpu.SemaphoreType.DMA((n,))]


def _scatter_blocks(parts, *, name):
    n = len(parts)

    def body(*refs):
        _scatter_copies(refs[:n], refs[n:2 * n], *refs[2 * n:])

    return pl.pallas_call(
        body, name=name,
        in_specs=[ANY] * n, out_specs=[ANY] * n,
        out_shape=[jax.ShapeDtypeStruct(p.shape, p.dtype) for p in parts],
        scratch_shapes=_comm_sems(n),
    )(*parts)


def _handshake(peers):
    barrier = pltpu.get_barrier_semaphore()
    for peer in peers:
        pl.semaphore_signal(barrier, inc=1, device_id=peer, device_id_type=MESH)
    pl.semaphore_wait(barrier, len(peers))


def _sequencer_call(arrays, out_types, copies_fn, peers_fn, *, name, collective_id):
    n = len(arrays)
    srcs = [jax.new_ref(a, memory_space=pltpu.MemorySpace.HBM) for a in arrays]
    dsts = [jax.empty_ref(t, memory_space=pltpu.MemorySpace.HBM) for t in out_types]

    @pl.kernel(mesh=plsc.ScalarSubcoreMesh(axis_name="sequencer", num_cores=1), name=name,
               scratch_types=_comm_sems(n), compiler_params=pltpu.CompilerParams(collective_id=collective_id))
    def launch(send_sems, recv_sems, local_sems):
        _handshake(peers_fn())
        copies_fn(srcs, dsts, send_sems, recv_sems, local_sems)

    launch()
    return [d[...] for d in dsts]


def _all_other_devices():
    x, y, c = _position()
    return [(1 - x if fx else x, 1 - y if fy else y, 1 - c if fc else c) for fx, fy, fc in FLIPS]


def _gather_peers():
    x, y, c = _position()
    return [(x, y, 1 - c), (1 - x, y, c), (x, 1 - y, c), (1 - x, 1 - y, c)]


def _scatter_blocks_async(parts, *, name, collective_id):
    return _sequencer_call(parts, [jax.ShapeDtypeStruct(p.shape, p.dtype) for p in parts],
                           _scatter_copies, _all_other_devices, name=name, collective_id=collective_id)


def _all_gather_async(shards, *, name, collective_id):
    return _sequencer_call(shards, [jax.ShapeDtypeStruct((N_DEV, *s.shape), s.dtype) for s in shards],
                           _gather_copies, _gather_peers, name=name, collective_id=collective_id)


def _adamw_math(w, g, m, v):
    m = ADAM_B1 * m + (1.0 - ADAM_B1) * g
    v = ADAM_B2 * v + (1.0 - ADAM_B2) * (g * g)
    m_hat = m / (1.0 - ADAM_B1 ** ADAM_STEP)
    v_hat = v / (1.0 - ADAM_B2 ** ADAM_STEP)
    delta = -ADAM_LR * (m_hat / (jnp.sqrt(v_hat) + ADAM_EPS) + ADAM_WD * w)
    return delta, m, v


def _adamw_reduce_call(recv, w, m, v, *, name, T=128):
    R, W = w.shape
    if R % T == 0:
        tr, tw = T, W
    elif (R // 2) % 16 == 0:
        tr, tw = R // 2, W
    else:
        tr, tw = R, 2 * LANES

    def body(p_ref, w_ref, m_ref, v_ref, g_out, d_out, m_out, v_out):
        g = p_ref[0].astype(F32)
        for d in range(1, N_DEV):
            g = g + p_ref[d].astype(F32)
        g_out[...] = g
        d_out[...], m_out[...], v_out[...] = _adamw_math(w_ref[...], g, m_ref[...], v_ref[...])

    row = pl.BlockSpec((tr, tw), lambda i, j: (i, j))
    out = jax.ShapeDtypeStruct((R, W), F32)
    return pl.pallas_call(
        body, name=name, grid=(R // tr, W // tw),
        in_specs=[pl.BlockSpec((N_DEV, tr, tw), lambda i, j: (0, i, j)), row, row, row],
        out_specs=[row] * 4, out_shape=[out] * 4,
        compiler_params=_cp(("parallel", "parallel"), VMEM_BIG),
    )(recv, w, m, v)


def _adamw_small_call(w, g, m, v, *, name):
    def body(w_ref, g_ref, m_ref, v_ref, d_out, m_out, v_out):
        d_out[...], m_out[...], v_out[...] = _adamw_math(w_ref[...], g_ref[...], m_ref[...], v_ref[...])

    out = jax.ShapeDtypeStruct(w.shape, F32)
    return pl.pallas_call(body, name=name, in_specs=[VMEM_SPEC] * 4, out_specs=[VMEM_SPEC] * 3,
                          out_shape=[out] * 3)(w, g, m, v)


def _tile_rows(n_elems):
    return -(-n_elems // (8 * LANES)) * 8


def _pack_rows(parts, rows):
    pieces = []
    for p in parts:
        q = p.reshape(-1, LANES)
        pieces.append(jnp.pad(q, ((0, _tile_rows(p.size) - q.shape[0]), (0, 0))))
    buf = jnp.concatenate(pieces, axis=0)
    return jnp.pad(buf, ((0, rows - buf.shape[0]), (0, 0)))


def _unpack_rows(buf, shapes):
    out, off = [], 0
    for shp in shapes:
        n = 1
        for s in shp:
            n *= s
        out.append(buf[off:off + n // LANES].reshape(shp))
        off += _tile_rows(n)
    return out


def _in_w_blocks_call(a, dps, dpg, dw_a, dw_low, *, name, tk=TOKEN_TILE):
    S = a.shape[0]
    tk = min(tk, S)
    steps = S // tk
    n_s, n_g = dps.shape[1], dpg.shape[1]
    out_shape = (N_DEV, IN_BLK, D_MODEL)

    def body(a_ref, dps_ref, dpg_ref, dwa_ref, dwl_ref, out_ref, acc_s, acc_g):
        k = pl.program_id(0)

        @pl.when(k == 0)
        def _():
            acc_s[...] = jnp.zeros_like(acc_s)
            acc_g[...] = jnp.zeros_like(acc_g)

        av = a_ref[...]
        for src, acc, n in ((dps_ref, acc_s, n_s), (dpg_ref, acc_g, n_g)):
            for m0 in range(0, n, 1024):
                acc[m0:m0 + 1024, :] += _dot(src[:, m0:m0 + 1024], av, TN)

        @pl.when(k == steps - 1)
        def _():
            groups = ((dwa_ref, A_COLS), (dwl_ref, LOW_COLS), (acc_s, S_COLS), (acc_g, G_COLS))
            for d in range(N_DEV):
                lo, hi = IN_BLK * d, IN_BLK * (d + 1)
                for ref, (c0, c1) in groups:
                    s0, e0 = max(lo, c0), min(hi, c1)
                    if s0 < e0:
                        out_ref[d, s0 - lo:e0 - lo, :] = ref[s0 - c0:e0 - c0, :].astype(BF16)

    tiles = [((tk, D_MODEL), BF16), ((tk, n_s), BF16), ((tk, n_g), BF16)]
    resident = [(dw_a.shape, BF16), (dw_low.shape, BF16), (out_shape, BF16), ((n_s, D_MODEL), F32), ((n_g, D_MODEL), F32)]
    return pl.pallas_call(
        body, name=name, grid=(steps,),
        in_specs=[pl.BlockSpec((tk, D_MODEL), lambda k: (k, 0)), pl.BlockSpec((tk, n_s), lambda k: (k, 0)),
                  pl.BlockSpec((tk, n_g), lambda k: (k, 0)), _res(dw_a.shape), _res(dw_low.shape)],
        out_specs=_acc(out_shape),
        out_shape=jax.ShapeDtypeStruct(out_shape, BF16),
        scratch_shapes=[pltpu.VMEM((n_s, D_MODEL), F32), pltpu.VMEM((n_g, D_MODEL), F32)],
        compiler_params=_cp(("arbitrary",), _vmem_limit(tiles, resident)),
    )(a, dps, dpg, dw_a, dw_low)


def _local_step(x, target, W, scatter, finish):
    g1, g2, g3, g4 = [W[n].reshape(1, D_MODEL) for n in ("norm_pre_mix", "norm_post_mix", "norm_pre_ffn", "norm_post_ffn")]
    wfi, wfo = W["w_ffn_in"], W["w_ffn_out"].reshape(4, FF_BLK, D_MODEL)
    wg = jnp.pad(W["w_gate_up"], ((0, LANES - RANK), (0, 0))).astype(BF16)
    bgate = W["b_gate"].reshape(1, QK)
    gnorm = W["gla_norm"].reshape(HEADS, 1, DV)
    ln_g = W["sgu_ln_g"].reshape(GROUPS, 1, DG)
    ln_b = W["sgu_ln_b"].reshape(GROUPS, 1, DG)
    w_sp = W["w_spatial"]
    b_sp = W["b_spatial"].reshape(GROUPS, SBLOCK, 1)

    a, pa, alow, ps, pg, w_in_t = _in_proj_call(x, g1, W["w_in_blocks"], name="in_proj")
    y_gla, states = _gla_fwd_call(pa, alow, wg, bgate, gnorm, name="gla_fwd")
    y_sgu = _sgu_fwd_call(ps, ln_g, ln_b, w_sp, b_sp, name="sgu_fwd")
    t1, t2, merged, mix, x1, h = _mixer_tail_call(y_gla, y_sgu, pg, x, W["w_branch_gla"], W["w_branch_sgu"],
                                                  W["w_out"], g2, g3, name="mixer_tail")
    gu, z = _ffn_in_call(h, wfi, name="ffn_in")
    loss, dx2, dy, dg4 = _ffn_out_loss_call(z, wfo, x1, target, g4, name="ffn_out_loss")

    grads = {"norm_post_ffn": dg4}
    done = {}
    dgu = _ffn_out_bwd_call(dy, wfo, gu, name="ffn_out_bwd")
    dw_ffn_out = _weight_grads_call([z, dy], [(0, 1)], name="d_ffn_out_w")[0].reshape(N_DEV, D_FF // N_DEV, D_MODEL)
    dw_ffn_in = _ffn_in_grad_call(h, dgu, name="d_ffn_in_w")
    dgu, dw_ffn_out, dw_ffn_in = lax.optimization_barrier((dgu, dw_ffn_out, dw_ffn_in))
    ffn_received = scatter(("w_ffn_out", "w_ffn_in"), [dw_ffn_out, dw_ffn_in])
    dx1, dmix, grads["norm_pre_ffn"], grads["norm_post_mix"] = _ffn_in_bwd_call(dgu, wfi, dx2, x1, mix, g3, g2, name="ffn_in_bwd")
    dt1, dt2, dpg, dy_gla, dy_sgu = _mixer_bwd_call(dmix, t1, t2, pg, W["w_out"], W["w_branch_gla"], W["w_branch_sgu"],
                                                    name="mixer_bwd")
    rows = D_MODEL // N_DEV
    mixer_grads = _weight_grads_call([merged, dmix, y_gla, dt1, y_sgu, dt2], [(0, 1), (2, 3), (4, 5)], name="d_mixer_w")
    dy_gla, dy_sgu, mixer_grads = lax.optimization_barrier((dy_gla, dy_sgu, mixer_grads))
    mixer_received = scatter(("w_out", "w_branch_gla", "w_branch_sgu"),
                             [g.reshape(N_DEV, rows, D_MODEL) for g in mixer_grads])
    dpa, dlogit, dgn, dbg = _gla_bwd_call(pa, alow, wg, bgate, gnorm, states, dy_gla, name="gla_bwd")
    ffn_done, dpa, dlogit = lax.optimization_barrier((finish(ffn_received), dpa, dlogit))
    done.update(ffn_done)
    dps, dlg, dlb, dwsp, dbsp = _sgu_bwd_call(ps, ln_g, ln_b, w_sp, b_sp, dy_sgu, name="sgu_bwd")
    mixer_done, dps = lax.optimization_barrier((finish(mixer_received), dps))
    done.update(mixer_done)
    dlow = _mm(dlogit, wg, "nt", BF16, name="d_gate_up_x")
    dw_a, dw_low = _weight_grads_call([a, dpa, dlow], [(1, 0), (2, 0)], name="d_in_w_qkvr")
    dw_in = _in_w_blocks_call(a, dps, dpg, dw_a, dw_low, name="d_in_w_blocks")
    in_received = scatter(("w_in",), [dw_in])
    dwg = _mm(alow, dlogit, "tn", F32, name="d_gate_up_w")
    grad_x, grads["norm_pre_mix"] = _in_proj_bwd_call(dpa, dlow, dps, dpg, w_in_t, x, dx1, g1, name="in_proj_bwd")

    grads["w_gate_up"] = dwg[:RANK]
    grads["b_gate"] = dbg
    grads["gla_norm"] = dgn
    grads["sgu_ln_g"] = dlg
    grads["sgu_ln_b"] = dlb
    grads["w_spatial"] = dwsp
    grads["b_spatial"] = dbsp
    done.update(finish(in_received))
    return loss, grad_x, grads, done


WEIGHTS = ("norm_pre_mix", "w_in", "w_gate_up", "b_gate", "gla_norm", "sgu_ln_g", "sgu_ln_b", "w_spatial",
           "b_spatial", "w_branch_gla", "w_branch_sgu", "w_out", "norm_post_mix", "norm_pre_ffn", "w_ffn_in",
           "w_ffn_out", "norm_post_ffn")
BIG = ("w_in", "w_branch_gla", "w_branch_sgu", "w_out", "w_ffn_in", "w_ffn_out")
MIXER = ("w_branch_gla", "w_branch_sgu", "w_out")
FFN = ("w_ffn_in", "w_ffn_out")
COLUMN_SHARDED = ("w_in", "w_ffn_in")
SMALL = tuple(n for n in WEIGHTS if n not in BIG)
SMALL_SHARDED = ("w_gate_up", "gla_norm", "sgu_ln_g", "sgu_ln_b")
SMALL_FULL = {"norm_pre_mix": (1024,), "w_gate_up": (16, 512), "b_gate": (512,), "gla_norm": (4, 256),
              "sgu_ln_g": (4, 256), "sgu_ln_b": (4, 256), "w_spatial": (4, 128, 128), "b_spatial": (4, 128),
              "norm_post_mix": (1024,), "norm_pre_ffn": (1024,), "norm_post_ffn": (1024,)}
SMALL_GRAD_ROWS = 648
SMALL_STATE_ROWS = 600
SMALL_GATHER_ROWS = 32


def kernel(x, norm_pre_mix, w_in, w_gate_up, b_gate, gla_norm, sgu_ln_g, sgu_ln_b, w_spatial, b_spatial, w_branch_gla, w_branch_sgu, w_out, norm_post_mix, norm_pre_ffn, w_ffn_in, w_ffn_out, norm_post_ffn, loss_target, m_norm_pre_mix, m_w_in, m_w_gate_up, m_b_gate, m_gla_norm, m_sgu_ln_g, m_sgu_ln_b, m_w_spatial, m_b_spatial, m_w_branch_gla, m_w_branch_sgu, m_w_out, m_norm_post_mix, m_norm_pre_ffn, m_w_ffn_in, m_w_ffn_out, m_norm_post_ffn, v_norm_pre_mix, v_w_in, v_w_gate_up, v_b_gate, v_gla_norm, v_sgu_ln_g, v_sgu_ln_b, v_w_spatial, v_b_spatial, v_w_branch_gla, v_w_branch_sgu, v_w_out, v_norm_post_mix, v_norm_pre_ffn, v_w_ffn_in, v_w_ffn_out, v_norm_post_ffn):
    given = dict(locals())
    def local(a, n):
        return a[0].T if n in COLUMN_SHARDED else a[0]

    w = {n: local(given[n], n) for n in WEIGHTS}
    m = {n: local(given["m_" + n], n) for n in WEIGHTS}
    v = {n: local(given["v_" + n], n) for n in WEIGHTS}
    xs, target = x[0], loss_target[0]
    me = 4 * lax.axis_index("x") + 2 * lax.axis_index("y") + lax.axis_index("c")

    small_shard = _pack_rows([w[n] for n in SMALL_SHARDED], SMALL_GATHER_ROWS)
    first = _all_gather_hbm([w["w_in"].astype(BF16), small_shard], name="gather_w_in")
    rest, first = lax.optimization_barrier(([w[n].astype(BF16) for n in MIXER + FFN], first))
    rest_blocks = _all_gather_async(rest, name="gather_rest", collective_id=1)
    W = {n: w[n] for n in SMALL if n not in SMALL_SHARDED}
    blocks = dict(zip(MIXER + FFN, rest_blocks))
    for n in ("w_branch_gla", "w_branch_sgu", "w_out", "w_ffn_out"):
        W[n] = blocks[n].reshape(-1, D_MODEL)
    W["w_ffn_in"] = blocks["w_ffn_in"]
    W["w_in_blocks"] = first[0]
    small_blocks = first[1]
    off = 0
    for n in SMALL_SHARDED:
        r, c = w[n].shape
        blk = small_blocks[:, off:off + r * c // LANES].reshape(N_DEV, r, c)
        W[n] = blk.transpose(1, 0, 2).reshape(r, N_DEV * c)
        off += _tile_rows(r * c)

    scatter_ids = iter((3, 4, 5))

    def scatter(names, parts):
        got = _scatter_blocks_async(parts, name="scatter_" + "_".join(names), collective_id=next(scatter_ids))
        return dict(zip(names, got))

    def finish(received):
        return {n: _adamw_reduce_call(r, w[n], m[n], v[n], name="adamw_" + n) for n, r in received.items()}

    loss_part, grad_x, grads, big_done = _local_step(xs, target, W, scatter, finish)

    small_part = jnp.concatenate([_pack_rows([grads[n] for n in SMALL], SMALL_GRAD_ROWS),
                                  jnp.broadcast_to(loss_part, (8, LANES))], axis=0)
    small_sum = _all_reduce_small(small_part, name="reduce_small")
    loss = small_sum[SMALL_GRAD_ROWS, 0]
    g_small = dict(zip(SMALL, _unpack_rows(small_sum, [SMALL_FULL[n] for n in SMALL])))
    for n in SMALL_SHARDED:
        c = w[n].shape[1]
        g_small[n] = lax.dynamic_slice_in_dim(g_small[n], me * c, c, axis=1)

    g_big, d_big, m_big, v_big = [{n: big_done[n][i] for n in BIG} for i in range(4)]
    small_shapes = [w[n].shape for n in SMALL]
    small_state = [_pack_rows([s[n] for n in SMALL], SMALL_STATE_ROWS) for s in (w, g_small, m, v)]
    small_out = _adamw_small_call(*small_state, name="adamw_small")
    d_small, m_small, v_small = [dict(zip(SMALL, _unpack_rows(o, small_shapes))) for o in small_out]

    def pick(big, sml):
        return [(big[n].T if n in COLUMN_SHARDED else big[n] if n in BIG else sml[n])[None] for n in WEIGHTS]

    return (loss, grad_x[None], *pick(g_big, g_small), *pick(d_big, d_small), *pick(m_big, m_small),
            *pick(v_big, v_small))
```

```python
import jax
import jax.numpy as jnp
from jax import lax
from jax.experimental import pallas as pl
from jax.experimental.pallas import tpu as pltpu
from jax.experimental.pallas import tpu_sc as plsc

F32 = jnp.float32
BF16 = jnp.bfloat16

D_MODEL = 1024
N_DEV = 8
CHUNK = 64
HEADS = 4
DK = 128
DV = 256
QK = HEADS * DK
GV = HEADS * DV
RANK = 16
GROUPS = 4
SBLOCK = 128
DG = 256
D_FF = 2816
FF_BLK = 704
EPS = 1e-6
Q_SCALE = DK ** -0.5
LANES = 128
VMEM_BIG = 48 * 1024 * 1024

D_IN = 7184
IN_BLK = 898
A_COLS = (0, 3072)
LOW_COLS = (3072, 3088)
S_COLS = (3088, 5136)
G_COLS = (5136, 7184)

ADAM_LR = 0.001
ADAM_B1 = 0.9
ADAM_B2 = 0.999
ADAM_EPS = 1e-08
ADAM_WD = 0.01
ADAM_STEP = 10

MESH = pl.DeviceIdType.MESH
ANY = pl.BlockSpec(memory_space=pl.ANY)
VMEM_SPEC = pl.BlockSpec(memory_space=pltpu.VMEM)


def _cp(sem=None, vmem=None):
    return pltpu.CompilerParams(dimension_semantics=sem, vmem_limit_bytes=vmem)


def _sig(x):
    return 0.5 * jnp.tanh(0.5 * x) + 0.5


GELU_C = 0.7978845608028654
GELU_A = 0.044715


def _gelu(x):
    t = jnp.tanh((GELU_C * x) * (1.0 + GELU_A * (x * x)))
    return (0.5 * x) * (1.0 + t)


def _gelu_and_grad(x):
    x2 = x * x
    t = jnp.tanh((GELU_C * x) * (1.0 + GELU_A * x2))
    one_t = 1.0 + t
    hx = 0.5 * x
    grad = 0.5 * one_t + (hx * (1.0 - t * t)) * (GELU_C + (3.0 * GELU_A * GELU_C) * x2)
    return hx * one_t, grad


def _logsig(x):
    return jnp.minimum(x, 0.0) - jnp.log1p(jnp.exp(-jnp.abs(x)))


def _dot(a, b, dims):
    return lax.dot_general(a, b, (dims, ((), ())), preferred_element_type=F32)


NN = ((1,), (0,))
NT = ((1,), (1,))
TN = ((0,), (0,))


def _exact_mask_dot(mask_bf16, x):
    hi = x.astype(BF16)
    r1 = x - hi.astype(F32)
    mid = r1.astype(BF16)
    lo = (r1 - mid.astype(F32)).astype(BF16)
    return _dot(mask_bf16, hi, NN) + _dot(mask_bf16, mid, NN) + _dot(mask_bf16, lo, NN)


def _pick_tile(dim, target):
    if dim <= target:
        return dim
    best = None
    for t in range(LANES, int(1.4 * target) + 1, LANES):
        if dim % t == 0:
            best = t
    assert best is not None, (dim, target)
    return best


def _mm_call(a, b, *, name, grid, a_spec, b_spec, o_spec, out_shape, dims, acc_shape):
    nk = grid[2]

    def body(a_ref, b_ref, o_ref, *acc):
        part = _dot(a_ref[...], b_ref[...], dims)
        if nk == 1:
            o_ref[...] = part.astype(o_ref.dtype)
        else:
            acc_ref = acc[0]
            k = pl.program_id(2)

            @pl.when(k == 0)
            def _():
                acc_ref[...] = part

            @pl.when(k > 0)
            def _():
                acc_ref[...] += part

            @pl.when(k == nk - 1)
            def _():
                o_ref[...] = acc_ref[...].astype(o_ref.dtype)

    return pl.pallas_call(
        body, name=name, grid=grid, in_specs=[a_spec, b_spec], out_specs=o_spec, out_shape=out_shape,
        scratch_shapes=[] if nk == 1 else [pltpu.VMEM(acc_shape, F32)],
        compiler_params=_cp(("parallel", "parallel", "arbitrary"), VMEM_BIG),
    )(a, b)


def _mm(a, b, mode, out_dtype, *, name, tm=512, tn=1024, tk=1024):
    if mode == "nn":
        (M, K), (_, N) = a.shape, b.shape
    elif mode == "nt":
        (M, K), (N, _) = a.shape, b.shape
    else:
        (K, M), (_, N) = a.shape, b.shape
    tm, tn, tk = _pick_tile(M, tm), _pick_tile(N, tn), _pick_tile(K, tk)
    if mode == "nn":
        a_spec = pl.BlockSpec((tm, tk), lambda i, j, k: (i, k))
        b_spec = pl.BlockSpec((tk, tn), lambda i, j, k: (k, j))
        dims = NN
    elif mode == "nt":
        a_spec = pl.BlockSpec((tm, tk), lambda i, j, k: (i, k))
        b_spec = pl.BlockSpec((tn, tk), lambda i, j, k: (j, k))
        dims = NT
    else:
        a_spec = pl.BlockSpec((tk, tm), lambda i, j, k: (k, i))
        b_spec = pl.BlockSpec((tk, tn), lambda i, j, k: (k, j))
        dims = TN
    return _mm_call(a, b, name=name, grid=(M // tm, N // tn, K // tk), a_spec=a_spec, b_spec=b_spec,
                    o_spec=pl.BlockSpec((tm, tn), lambda i, j, k: (i, j)),
                    out_shape=jax.ShapeDtypeStruct((M, N), out_dtype), dims=dims, acc_shape=(tm, tn))


ROW_TILE = 512
SUB_ROWS = 256


def _sub_tiles(T):
    return [pl.ds(r0, min(SUB_ROWS, T)) for r0 in range(0, T, SUB_ROWS)]


def _rt(T, W, c=0):
    return pl.BlockSpec((T, W), lambda i: (i, c))


def _rt3(nb, T, W):
    return pl.BlockSpec((nb, T, W), lambda i: (0, i, 0))


def _res(shape):
    nd = len(shape)
    return pl.BlockSpec(tuple(shape), lambda i: (0,) * nd, pipeline_mode=pl.Buffered(1))


def _acc(shape):
    nd = len(shape)
    return pl.BlockSpec(tuple(shape), lambda i: (0,) * nd, pipeline_mode=pl.Buffered(1))


def _nbytes(shape, dtype):
    n = jnp.dtype(dtype).itemsize
    for s in shape:
        n *= s
    return n


def _vmem_limit(tiles, resident, temps=16 * 1024 * 1024):
    need = 2 * sum(_nbytes(s, d) for s, d in tiles) + sum(_nbytes(s, d) for s, d in resident) + temps
    return min(need, 60 * 1024 * 1024)


def _tok_call(body, *, name, S, T, ins, outs, semantics="parallel"):
    tiles = [(spec.block_shape, a.dtype) for a, spec, kind in ins + outs if kind == "tile"]
    resident = [(a.shape, a.dtype) for a, spec, kind in ins + outs if kind == "res"]
    return pl.pallas_call(
        body, name=name, grid=(S // T,),
        in_specs=[spec for _, spec, _ in ins], out_specs=[spec for _, spec, _ in outs],
        out_shape=[jax.ShapeDtypeStruct(a.shape, a.dtype) for a, _, _ in outs],
        compiler_params=_cp((semantics,), _vmem_limit(tiles, resident)),
    )(*[a for a, _, _ in ins])


def _tile(a, spec):
    return (a, spec, "tile")


def _whole(a):
    return (a, _res(a.shape), "res")


def _out_tile(shape, dtype, spec):
    return (jax.ShapeDtypeStruct(shape, dtype), spec, "tile")


def _out_acc(shape, dtype=F32):
    return (jax.ShapeDtypeStruct(shape, dtype), _acc(shape), "res")


def _rms_stats(x):
    r = lax.rsqrt(jnp.mean(x * x, axis=-1, keepdims=True) + EPS)
    return r, x * r


def _rms_bwd(xh, r, g, dy):
    dxh = dy * g
    dx = r * (dxh - xh * jnp.mean(dxh * xh, axis=-1, keepdims=True))
    dg = jnp.sum(dy * xh, axis=0, keepdims=True)
    return dx, dg


def _accum(ref, val):
    @pl.when(pl.program_id(0) == 0)
    def _():
        ref[...] = val

    @pl.when(pl.program_id(0) > 0)
    def _():
        ref[...] += val


def _dot_rows_t(a, w_ref, row0, o_ref, chunk=1024):
    n = o_ref.shape[1]
    for n0 in range(0, n, chunk):
        n1 = min(n, n0 + chunk)
        o_ref[:, n0:n1] = _dot(a, w_ref[row0 + n0:row0 + n1, :], NT).astype(o_ref.dtype)


def _in_proj_call(x, g1, w_blocks, *, name, T=ROW_TILE):
    S = x.shape[0]

    def body(x_ref, g_ref, blk_ref, a_ref, pa_ref, al_ref, ps_ref, pg_ref, w_ref):
        @pl.when(pl.program_id(0) == 0)
        def _():
            for d in range(N_DEV):
                w_ref[d * IN_BLK:(d + 1) * IN_BLK, :] = blk_ref[d]

        _, xh = _rms_stats(x_ref[...])
        a = (xh * g_ref[...]).astype(BF16)
        a_ref[...] = a
        _dot_rows_t(a, w_ref, A_COLS[0], pa_ref)
        _dot_rows_t(a, w_ref, LOW_COLS[0], al_ref)
        _dot_rows_t(a, w_ref, S_COLS[0], ps_ref)
        _dot_rows_t(a, w_ref, G_COLS[0], pg_ref)

    widths = (D_MODEL, A_COLS[1] - A_COLS[0], LANES, S_COLS[1] - S_COLS[0], G_COLS[1] - G_COLS[0])
    return _tok_call(
        body, name=name, S=S, T=T, semantics="arbitrary",
        ins=[_tile(x, _rt(T, D_MODEL)), _whole(g1), _whole(w_blocks)],
        outs=[_out_tile((S, w), BF16, _rt(T, w)) for w in widths] + [_out_acc((D_IN, D_MODEL), BF16)])


def _mixer_tail_call(y_gla, y_sgu, pg, x, w_bg, w_bs, w_out, g2, g3, *, name, T=ROW_TILE):
    S = x.shape[0]

    def body(yg_ref, ys_ref, pg_ref, x_ref, wbg_ref, wbs_ref, wo_ref, g2_ref, g3_ref,
             t1_ref, t2_ref, mg_ref, mix_ref, x1_ref, h_ref):
        for rows in _sub_tiles(T):
            t1 = _dot(yg_ref[rows, :], wbg_ref[...], NN)
            t2 = _dot(ys_ref[rows, :], wbs_ref[...], NN)
            t1_ref[rows, :] = t1.astype(BF16)
            t2_ref[rows, :] = t2.astype(BF16)
            sg = _sig(pg_ref[rows, pl.ds(0, D_MODEL)].astype(F32))
            ss = _sig(pg_ref[rows, pl.ds(D_MODEL, D_MODEL)].astype(F32))
            merged = (sg * t1 + ss * t2).astype(BF16)
            mg_ref[rows, :] = merged
            mix = _dot(merged, wo_ref[...], NN)
            mix_ref[rows, :] = mix
            _, mh = _rms_stats(mix)
            x1 = x_ref[rows, :] + mh * g2_ref[...]
            x1_ref[rows, :] = x1
            _, xh = _rms_stats(x1)
            h_ref[rows, :] = (xh * g3_ref[...]).astype(BF16)

    row = _rt(T, D_MODEL)
    b16 = lambda: _out_tile((S, D_MODEL), BF16, row)
    f32 = lambda: _out_tile((S, D_MODEL), F32, row)
    return _tok_call(
        body, name=name, S=S, T=T,
        ins=[_tile(y_gla, row), _tile(y_sgu, row), _tile(pg, _rt(T, 2 * D_MODEL)), _tile(x, row),
             _whole(w_bg), _whole(w_bs), _whole(w_out), _whole(g2), _whole(g3)],
        outs=[b16(), b16(), b16(), f32(), f32(), b16()])


def _ffn_in_call(h, wfi, *, name, T=ROW_TILE):
    S = h.shape[0]

    def body(h_ref, w_ref, gu_ref, z_ref):
        hv = h_ref[...]
        for d in range(4):
            gate = _dot(hv, w_ref[d], NT)
            up = _dot(hv, w_ref[d + 4], NT)
            gu_ref[d] = gate.astype(BF16)
            gu_ref[d + 4] = up.astype(BF16)
            z_ref[d] = (gate * _sig(gate) * up).astype(BF16)

    return _tok_call(
        body, name=name, S=S, T=T,
        ins=[_tile(h, _rt(T, D_MODEL)), _whole(wfi)],
        outs=[_out_tile((N_DEV, S, FF_BLK), BF16, _rt3(N_DEV, T, FF_BLK)),
              _out_tile((4, S, FF_BLK), BF16, _rt3(4, T, FF_BLK))])


def _ffn_out_loss_call(z, wfo, x1, target, g4, *, name, T=ROW_TILE):
    S = x1.shape[0]

    def body(z_ref, w_ref, x1_ref, t_ref, g4_ref, loss_ref, dx2_ref, dy_ref, dg4_ref):
        loss = jnp.zeros((1, 1), F32)
        dg4 = jnp.zeros((1, D_MODEL), F32)
        for rows in _sub_tiles(T):
            y = _dot(z_ref[0, rows, :], w_ref[0], NN)
            for d in range(1, 4):
                y = y + _dot(z_ref[d, rows, :], w_ref[d], NN)
            r, yh = _rms_stats(y)
            diff = x1_ref[rows, :] + yh * g4_ref[...] - t_ref[rows, :]
            loss = loss + 0.5 * jnp.sum(jnp.mean(diff * diff, axis=-1, keepdims=True), axis=0, keepdims=True)
            dx2 = diff * (1.0 / D_MODEL)
            dx2_ref[rows, :] = dx2
            dy, dg = _rms_bwd(yh, r, g4_ref[...], dx2)
            dy_ref[rows, :] = dy.astype(BF16)
            dg4 = dg4 + dg
        _accum(loss_ref, jnp.broadcast_to(loss, (1, LANES)))
        _accum(dg4_ref, dg4)

    row = _rt(T, D_MODEL)
    return _tok_call(
        body, name=name, S=S, T=T, semantics="arbitrary",
        ins=[_tile(z, _rt3(4, T, FF_BLK)), _whole(wfo), _tile(x1, row), _tile(target, row), _whole(g4)],
        outs=[_out_acc((1, LANES)), _out_tile((S, D_MODEL), F32, row), _out_tile((S, D_MODEL), BF16, row),
              _out_acc((1, D_MODEL))])


def _ffn_out_bwd_call(dy, wfo, gu, *, name, T=ROW_TILE):
    S = dy.shape[0]

    def body(dy_ref, w_ref, gu_ref, dgu_ref):
        dyv = dy_ref[...]
        for d in range(4):
            dz = _dot(dyv, w_ref[d], NT)
            gt = gu_ref[d].astype(F32)
            up = gu_ref[d + 4].astype(F32)
            s = _sig(gt)
            dgu_ref[d] = (dz * up * s * (1.0 + gt * (1.0 - s))).astype(BF16)
            dgu_ref[d + 4] = (dz * gt * s).astype(BF16)

    blocks = _rt3(N_DEV, T, FF_BLK)
    return _tok_call(
        body, name=name, S=S, T=T,
        ins=[_tile(dy, _rt(T, D_MODEL)), _whole(wfo), _tile(gu, blocks)],
        outs=[_out_tile((N_DEV, S, FF_BLK), BF16, blocks)])[0]


def _ffn_in_bwd_call(dgu, wfi, dx2, x1, mix, g3, g2, *, name, T=ROW_TILE):
    S = x1.shape[0]

    def body(dgu_ref, w_ref, dx2_ref, x1_ref, mix_ref, g3_ref, g2_ref, dx1_ref, dmix_ref, dg3_ref, dg2_ref):
        dg3 = jnp.zeros((1, D_MODEL), F32)
        dg2 = jnp.zeros((1, D_MODEL), F32)
        for rows in _sub_tiles(T):
            dh = _dot(dgu_ref[0, rows, :], w_ref[0], NN)
            for d in range(1, N_DEV):
                dh = dh + _dot(dgu_ref[d, rows, :], w_ref[d], NN)
            r3, xh = _rms_stats(x1_ref[rows, :])
            d3, g = _rms_bwd(xh, r3, g3_ref[...], dh)
            dg3 = dg3 + g
            dx1 = dx2_ref[rows, :] + d3
            dx1_ref[rows, :] = dx1
            r2, mh = _rms_stats(mix_ref[rows, :])
            dmix, g = _rms_bwd(mh, r2, g2_ref[...], dx1)
            dg2 = dg2 + g
            dmix_ref[rows, :] = dmix.astype(BF16)
        _accum(dg3_ref, dg3)
        _accum(dg2_ref, dg2)

    row = _rt(T, D_MODEL)
    return _tok_call(
        body, name=name, S=S, T=T, semantics="arbitrary",
        ins=[_tile(dgu, _rt3(N_DEV, T, FF_BLK)), _whole(wfi), _tile(dx2, row), _tile(x1, row), _tile(mix, row),
             _whole(g3), _whole(g2)],
        outs=[_out_tile((S, D_MODEL), F32, row), _out_tile((S, D_MODEL), BF16, row),
              _out_acc((1, D_MODEL)), _out_acc((1, D_MODEL))])


def _mixer_bwd_call(dmix, t1, t2, pg, w_out, w_bg, w_bs, *, name, T=ROW_TILE):
    S = dmix.shape[0]

    def body(dmix_ref, t1_ref, t2_ref, pg_ref, wo_ref, wbg_ref, wbs_ref, dt1_ref, dt2_ref, dpg_ref, dyg_ref, dys_ref):
        gg, gs = pl.ds(0, D_MODEL), pl.ds(D_MODEL, D_MODEL)
        for rows in _sub_tiles(T):
            dm = _dot(dmix_ref[rows, :], wo_ref[...], NT)
            sg = _sig(pg_ref[rows, gg].astype(F32))
            ss = _sig(pg_ref[rows, gs].astype(F32))
            dt1 = (dm * sg).astype(BF16)
            dt2 = (dm * ss).astype(BF16)
            dt1_ref[rows, :] = dt1
            dt2_ref[rows, :] = dt2
            dpg_ref[rows, gg] = (dm * t1_ref[rows, :].astype(F32) * sg * (1.0 - sg)).astype(BF16)
            dpg_ref[rows, gs] = (dm * t2_ref[rows, :].astype(F32) * ss * (1.0 - ss)).astype(BF16)
            dyg_ref[rows, :] = _dot(dt1, wbg_ref[...], NT).astype(BF16)
            dys_ref[rows, :] = _dot(dt2, wbs_ref[...], NT).astype(BF16)

    row = _rt(T, D_MODEL)
    wide = _rt(T, 2 * D_MODEL)
    b16 = lambda: _out_tile((S, D_MODEL), BF16, row)
    return _tok_call(
        body, name=name, S=S, T=T,
        ins=[_tile(dmix, row), _tile(t1, row), _tile(t2, row), _tile(pg, wide), _whole(w_out), _whole(w_bg), _whole(w_bs)],
        outs=[b16(), b16(), _out_tile((S, 2 * D_MODEL), BF16, wide), b16(), b16()])


def _in_proj_bwd_call(dpa, dlow, dps, dpg, w_in_t, x, dx1, g1, *, name, T=ROW_TILE):
    S = x.shape[0]

    def body(dpa_ref, dl_ref, dps_ref, dpg_ref, w_ref, x_ref, dx1_ref, g1_ref, gx_ref, dg1_ref):
        da = (_dot(dpa_ref[...], w_ref[A_COLS[0]:A_COLS[1], :], NN)
              + _dot(dl_ref[...], w_ref[LOW_COLS[0]:LOW_COLS[0] + LANES, :], NN)
              + _dot(dps_ref[...], w_ref[S_COLS[0]:S_COLS[1], :], NN)
              + _dot(dpg_ref[...], w_ref[G_COLS[0]:G_COLS[1], :], NN))
        r, xh = _rms_stats(x_ref[...])
        dxa, dg = _rms_bwd(xh, r, g1_ref[...], da)
        gx_ref[...] = dx1_ref[...] + dxa
        _accum(dg1_ref, dg)

    row = _rt(T, D_MODEL)
    return _tok_call(
        body, name=name, S=S, T=T, semantics="arbitrary",
        ins=[_tile(dpa, _rt(T, dpa.shape[1])), _tile(dlow, _rt(T, dlow.shape[1])), _tile(dps, _rt(T, dps.shape[1])),
             _tile(dpg, _rt(T, dpg.shape[1])), _whole(w_in_t), _tile(x, row), _tile(dx1, row), _whole(g1)],
        outs=[_out_tile((S, D_MODEL), F32, row), _out_acc((1, D_MODEL))])


TOKEN_TILE = 512


def _weight_grads_part(arrays, pairs, *, tk=TOKEN_TILE, out_dtype=BF16):
    S = arrays[0].shape[-2]
    tk = min(tk, S)
    n_in = len(arrays)

    def out_shape(i, j):
        a, b = arrays[i], arrays[j]
        if a.ndim == 3:
            return (a.shape[0], a.shape[2], b.shape[1])
        if b.ndim == 3:
            return (b.shape[0], a.shape[1], b.shape[2])
        return (a.shape[1], b.shape[1])

    shapes = [out_shape(i, j) for i, j in pairs]

    in_place = out_dtype == F32

    def body(ins, outs, accs):
        k = pl.program_id(0)
        if in_place:
            accs = outs

        @pl.when(k == 0)
        def _():
            for acc in accs:
                acc[...] = jnp.zeros_like(acc)

        for (i, j), acc in zip(pairs, accs):
            a_ref, b_ref = ins[i], ins[j]
            if len(a_ref.shape) == 3:
                for n in range(a_ref.shape[0]):
                    acc[n] += _dot(a_ref[n], b_ref[...], TN)
            elif len(b_ref.shape) == 3:
                a = a_ref[...]
                for n in range(b_ref.shape[0]):
                    acc[n] += _dot(a, b_ref[n], TN)
            else:
                b = b_ref[...]
                for m0 in range(0, a_ref.shape[1], 1024):
                    m1 = min(a_ref.shape[1], m0 + 1024)
                    acc[m0:m1, :] += _dot(a_ref[:, m0:m1], b, TN)

        if not in_place:
            @pl.when(k == S // tk - 1)
            def _():
                for out, acc in zip(outs, accs):
                    out[...] = acc[...].astype(out.dtype)

    def in_spec(a):
        if a.ndim == 3:
            return pl.BlockSpec((a.shape[0], tk, a.shape[2]), lambda k: (0, k, 0))
        return pl.BlockSpec((tk, a.shape[1]), lambda k: (k, 0))

    return dict(
        body=body, steps=S // tk, arrays=list(arrays),
        in_specs=[in_spec(a) for a in arrays],
        out_specs=[_acc(s) for s in shapes],
        out_shapes=[jax.ShapeDtypeStruct(s, out_dtype) for s in shapes],
        scratch=[] if in_place else [pltpu.VMEM(s, F32) for s in shapes],
        tiles=[(in_spec(a).block_shape, a.dtype) for a in arrays],
        resident=[(s, F32) for s in shapes] + ([] if in_place else [(s, BF16) for s in shapes]))


def _weight_grads_call(arrays, pairs, *, name, tk=TOKEN_TILE):
    part = _weight_grads_part(arrays, pairs, tk=tk)
    n_in, n_out = len(part["arrays"]), len(part["out_shapes"])

    def body(*refs):
        part["body"](refs[:n_in], refs[n_in:n_in + n_out], refs[n_in + n_out:])

    return pl.pallas_call(
        body, name=name, grid=(part["steps"],),
        in_specs=part["in_specs"], out_specs=part["out_specs"], out_shape=part["out_shapes"],
        scratch_shapes=part["scratch"],
        compiler_params=_cp(("arbitrary",), _vmem_limit(part["tiles"], part["resident"])),
    )(*part["arrays"])


def _with_rider(body, n_in, n_out, n_scratch, rider):
    if rider is None:
        return body
    r_in, r_out = len(rider["arrays"]), len(rider["out_shapes"])

    def both(*refs):
        ins, rest = refs[:n_in + r_in], refs[n_in + r_in:]
        outs, scratch = rest[:n_out + r_out], rest[n_out + r_out:]
        body(*ins[:n_in], *outs[:n_out], *scratch[:n_scratch])
        rider["body"](ins[n_in:], outs[n_out:], scratch[n_scratch:])

    return both


def _rider_lists(rider):
    if rider is None:
        return [], [], [], [], [], [], []
    return (rider["arrays"], rider["in_specs"], rider["out_specs"], rider["out_shapes"], rider["scratch"],
            rider["tiles"], rider["resident"])


def _ffn_in_grad_call(h, dgu, *, name, tk=TOKEN_TILE):
    S = h.shape[0]
    tk = min(tk, S)
    nb = 4
    shape = (nb, FF_BLK, D_MODEL)

    def body(h_ref, dgu_ref, out_ref, acc):
        k = pl.program_id(1)

        @pl.when(k == 0)
        def _():
            acc[...] = jnp.zeros_like(acc)

        hv = h_ref[...]
        for n in range(nb):
            acc[n] += _dot(dgu_ref[n], hv, TN)

        @pl.when(k == S // tk - 1)
        def _():
            out_ref[...] = acc[...].astype(out_ref.dtype)

    tiles = [((tk, D_MODEL), BF16), ((nb, tk, FF_BLK), BF16), (shape, BF16)]
    return pl.pallas_call(
        body, name=name, grid=(N_DEV // nb, S // tk),
        in_specs=[pl.BlockSpec((tk, D_MODEL), lambda g, k: (k, 0)),
                  pl.BlockSpec((nb, tk, FF_BLK), lambda g, k: (g, k, 0))],
        out_specs=pl.BlockSpec(shape, lambda g, k: (g, 0, 0)),
        out_shape=jax.ShapeDtypeStruct((N_DEV, FF_BLK, D_MODEL), BF16),
        scratch_shapes=[pltpu.VMEM(shape, F32)],
        compiler_params=_cp(("parallel", "arbitrary"), _vmem_limit(tiles, [(shape, F32)])),
    )(h, dgu)


GLA_TILE = 256
Q_OFF, K_OFF, V_OFF, R_OFF = 0, QK, 2 * QK, 2 * QK + GV


def _tri(lower):
    r = lax.broadcasted_iota(jnp.int32, (CHUNK, CHUNK), 0)
    c = lax.broadcasted_iota(jnp.int32, (CHUNK, CHUNK), 1)
    return jnp.where((r >= c) if lower else (c >= r), 1.0, 0.0).astype(BF16)


def _gla_fwd_call(pa, alow, wg, bgate, gnorm, *, name):
    S = pa.shape[0]
    Tg = min(GLA_TILE, S)
    cb = Tg // CHUNK

    def body(pa_ref, al_ref, wg_ref, bg_ref, gn_ref, y_ref, st_ref, state):
        @pl.when(pl.program_id(0) == 0)
        def _():
            state[...] = jnp.zeros_like(state)

        logit = _dot(al_ref[...], wg_ref[...], NN) + bg_ref[...]
        ls = _logsig(logit) * (1.0 / 16.0)
        tri = _tri(True)
        for c in range(cb):
            rows = pl.ds(c * CHUNK, CHUNK)
            cum = _exact_mask_dot(tri, ls[c * CHUNK:(c + 1) * CHUNK])
            tot = cum[CHUNK - 1:CHUNK]
            kd = (pa_ref[rows, pl.ds(K_OFF, QK)].astype(F32) * jnp.exp(tot - cum)).astype(BF16)
            decay = jnp.exp(tot)
            for h in range(HEADS):
                lanes = slice(h * DK, (h + 1) * DK)
                new = state[h] * decay[:, lanes] + _dot(pa_ref[rows, pl.ds(V_OFF + h * DV, DV)], kd[:, lanes], TN)
                state[h] = new
                st_ref[c, h] = new
        for c in range(cb):
            rows = pl.ds(c * CHUNK, CHUNK)
            for h in range(HEADS):
                qs = (pa_ref[rows, pl.ds(Q_OFF + h * DK, DK)].astype(F32) * Q_SCALE).astype(BF16)
                o = _dot(qs, st_ref[c, h].astype(BF16), NT)
                rs = lax.rsqrt(jnp.mean(o * o, axis=-1, keepdims=True) + EPS)
                rr = pa_ref[rows, pl.ds(R_OFF + h * DV, DV)].astype(F32)
                y_ref[rows, pl.ds(h * DV, DV)] = (o * rs * gn_ref[h] * (rr * _sig(rr))).astype(BF16)

    return pl.pallas_call(
        body, name=name, grid=(S // Tg,),
        in_specs=[
            pl.BlockSpec((Tg, 2 * QK + 2 * GV), lambda t: (t, 0)),
            pl.BlockSpec((Tg, LANES), lambda t: (t, 0)),
            pl.BlockSpec((LANES, QK), lambda t: (0, 0)),
            pl.BlockSpec((1, QK), lambda t: (0, 0)),
            pl.BlockSpec((HEADS, 1, DV), lambda t: (0, 0, 0)),
        ],
        out_specs=[
            pl.BlockSpec((Tg, GV), lambda t: (t, 0)),
            pl.BlockSpec((cb, HEADS, DV, DK), lambda t: (t, 0, 0, 0)),
        ],
        out_shape=[jax.ShapeDtypeStruct((S, GV), BF16),
                   jax.ShapeDtypeStruct((S // CHUNK, HEADS, DV, DK), F32)],
        scratch_shapes=[pltpu.VMEM((HEADS, DV, DK), F32)],
        compiler_params=_cp(("arbitrary",), VMEM_BIG),
    )(pa, alow, wg, bgate, gnorm)


def _gla_bwd_call(pa, alow, wg, bgate, gnorm, states, dy, *, name, rider=None):
    S = pa.shape[0]
    Tg = min(GLA_TILE, S)
    cb = Tg // CHUNK
    nt = S // Tg
    r_arrays, r_in, r_out, r_shapes, r_scratch, r_tiles, r_resident = _rider_lists(rider)
    assert rider is None or rider["steps"] == nt

    def rev(t):
        return nt - 1 - t

    def body(pa_ref, al_ref, wg_ref, bg_ref, gn_ref, st_ref, prev_ref, dy_ref,
             dpa_ref, dl_ref, dgn_ref, dbg_ref, carry, pbuf):
        t = pl.program_id(0)

        @pl.when(t == 0)
        def _():
            carry[...] = jnp.zeros_like(carry)
            dgn_ref[...] = jnp.zeros_like(dgn_ref)
            dbg_ref[...] = jnp.zeros_like(dbg_ref)

        logit = _dot(al_ref[...], wg_ref[...], NN) + bg_ref[...]
        ls = _logsig(logit) * (1.0 / 16.0)
        sneg = 1.0 / (1.0 + jnp.exp(logit))
        tri = _tri(True)
        upper = _tri(False)
        first_tile = rev(t) == 0
        heads = range(HEADS)

        w, decay, kd = [], [], []
        for c in range(cb):
            rows = pl.ds(c * CHUNK, CHUNK)
            cum = _exact_mask_dot(tri, ls[c * CHUNK:(c + 1) * CHUNK])
            tot = cum[CHUNK - 1:CHUNK]
            w.append(jnp.exp(tot - cum))
            decay.append(jnp.exp(tot))
            kd.append(pa_ref[rows, pl.ds(K_OFF, QK)].astype(F32) * w[c])

        dgn = [jnp.zeros((1, DV), F32) for _ in heads]
        for c in range(cb):
            rows = pl.ds(c * CHUNK, CHUNK)
            for h in heads:
                gn = gn_ref[h]
                qs = (pa_ref[rows, pl.ds(Q_OFF + h * DK, DK)].astype(F32) * Q_SCALE).astype(BF16)
                st16 = st_ref[c, h].astype(BF16)
                o = _dot(qs, st16, NT)
                rs = lax.rsqrt(jnp.mean(o * o, axis=-1, keepdims=True) + EPS)
                oh = o * rs
                rr = pa_ref[rows, pl.ds(R_OFF + h * DV, DV)].astype(F32)
                sr = _sig(rr)
                dyv = dy_ref[rows, pl.ds(h * DV, DV)].astype(F32)
                dpa_ref[rows, pl.ds(R_OFF + h * DV, DV)] = (
                    dyv * oh * gn * sr * (1.0 + rr * (1.0 - sr))).astype(BF16)
                don = dyv * (rr * sr)
                dgn[h] = dgn[h] + jnp.sum(don * oh, axis=0, keepdims=True)
                doh = don * gn
                do16 = (rs * (doh - oh * jnp.mean(doh * oh, axis=-1, keepdims=True))).astype(BF16)
                dpa_ref[rows, pl.ds(Q_OFF + h * DK, DK)] = (_dot(do16, st16, NN) * Q_SCALE).astype(BF16)
                pbuf[c, h] = _dot(do16, qs, TN)
        for h in heads:
            dgn_ref[h] += dgn[h]

        dkd = [[None] * HEADS for _ in range(cb)]
        ddecay = [[None] * HEADS for _ in range(cb)]
        for c in reversed(range(cb)):
            rows = pl.ds(c * CHUNK, CHUNK)
            for h in heads:
                lanes = slice(h * DK, (h + 1) * DK)
                gt = pbuf[c, h] + carry[h]
                gt16 = gt.astype(BF16)
                dkd[c][h] = _dot(pa_ref[rows, pl.ds(V_OFF + h * DV, DV)], gt16, NN)
                dpa_ref[rows, pl.ds(V_OFF + h * DV, DV)] = _dot(kd[c][:, lanes].astype(BF16), gt16, NT).astype(BF16)
                if c > 0:
                    st_prev = st_ref[c - 1, h]
                else:
                    st_prev = jnp.where(first_tile, 0.0, prev_ref[0, h])
                ddecay[c][h] = jnp.sum(gt * st_prev, axis=0, keepdims=True)
                carry[h] = gt * decay[c][:, lanes]

        dbg = jnp.zeros((1, QK), F32)
        for c in range(cb):
            rows = pl.ds(c * CHUNK, CHUNK)
            dkd_c = jnp.concatenate(dkd[c], axis=1)
            dpa_ref[rows, pl.ds(K_OFF, QK)] = (dkd_c * w[c]).astype(BF16)
            e = dkd_c * kd[c]
            dtot = jnp.sum(e, axis=0, keepdims=True) + jnp.concatenate(ddecay[c], axis=1) * decay[c]
            dls = dtot - _exact_mask_dot(upper, e)
            dlogit = dls * (1.0 / 16.0) * sneg[c * CHUNK:(c + 1) * CHUNK]
            dl_ref[rows, :] = dlogit.astype(BF16)
            dbg = dbg + jnp.sum(dlogit, axis=0, keepdims=True)
        dbg_ref[...] += dbg

    wide = 2 * QK + 2 * GV
    tiles = [((Tg, wide), BF16), ((cb + 1, HEADS, DV, DK), F32), ((Tg, GV), BF16), ((Tg, wide), BF16),
             ((Tg, QK), BF16)] + r_tiles
    resident = [((cb + 1, HEADS, DV, DK), F32)] + r_resident
    return pl.pallas_call(
        _with_rider(body, 8, 4, 2, rider), name=name, grid=(nt,),
        in_specs=[
            pl.BlockSpec((Tg, wide), lambda t: (rev(t), 0)),
            pl.BlockSpec((Tg, LANES), lambda t: (rev(t), 0)),
            pl.BlockSpec((LANES, QK), lambda t: (0, 0)),
            pl.BlockSpec((1, QK), lambda t: (0, 0)),
            pl.BlockSpec((HEADS, 1, DV), lambda t: (0, 0, 0)),
            pl.BlockSpec((cb, HEADS, DV, DK), lambda t: (rev(t), 0, 0, 0)),
            pl.BlockSpec((1, HEADS, DV, DK), lambda t: (jnp.maximum(rev(t) * cb - 1, 0), 0, 0, 0)),
            pl.BlockSpec((Tg, GV), lambda t: (rev(t), 0)),
        ] + r_in,
        out_specs=[
            pl.BlockSpec((Tg, wide), lambda t: (rev(t), 0)),
            pl.BlockSpec((Tg, QK), lambda t: (rev(t), 0)),
            pl.BlockSpec((HEADS, 1, DV), lambda t: (0, 0, 0)),
            pl.BlockSpec((1, QK), lambda t: (0, 0)),
        ] + r_out,
        out_shape=[jax.ShapeDtypeStruct((S, wide), BF16), jax.ShapeDtypeStruct((S, QK), BF16),
                   jax.ShapeDtypeStruct((HEADS, 1, DV), F32), jax.ShapeDtypeStruct((1, QK), F32)] + r_shapes,
        scratch_shapes=[pltpu.VMEM((HEADS, DV, DK), F32), pltpu.VMEM((cb, HEADS, DV, DK), F32)] + r_scratch,
        compiler_params=_cp(("arbitrary",), _vmem_limit(tiles, resident)),
    )(pa, alow, wg, bgate, gnorm, states, states, dy, *r_arrays)


SGU_TILE = 256


def _sgu_mask():
    r = lax.broadcasted_iota(jnp.int32, (SBLOCK, SBLOCK), 0)
    c = lax.broadcasted_iota(jnp.int32, (SBLOCK, SBLOCK), 1)
    return (c < CHUNK) | (r >= CHUNK)


def _ln_stats(vf):
    mu = jnp.mean(vf, axis=-1, keepdims=True)
    xc = vf - mu
    rs = lax.rsqrt(jnp.mean(xc * xc, axis=-1, keepdims=True) + EPS)
    return rs, xc * rs


def _sgu_fwd_call(ps, ln_g, ln_b, w_sp, b_sp, *, name):
    S = ps.shape[0]
    Ts = min(SGU_TILE, S)

    def body(ps_ref, lg_ref, lb_ref, w_ref, b_ref, y_ref):
        mask = _sgu_mask()
        for g in range(GROUPS):
            wm = jnp.where(mask, w_ref[g], 0.0).astype(BF16)
            for p in range(Ts // SBLOCK):
                rows = pl.ds(p * SBLOCK, SBLOCK)
                u = _gelu(ps_ref[rows, pl.ds(g * DG, DG)].astype(F32))
                _, xh = _ln_stats(_gelu(ps_ref[rows, pl.ds(D_MODEL + g * DG, DG)].astype(F32)))
                vn = xh * lg_ref[g] + lb_ref[g]
                mixed = _dot(wm, vn.astype(BF16), NN) + b_ref[g]
                y_ref[rows, pl.ds(g * DG, DG)] = (u * mixed).astype(BF16)

    full3 = lambda a, b, c: pl.BlockSpec((a, b, c), lambda t: (0, 0, 0))
    return pl.pallas_call(
        body, name=name, grid=(S // Ts,),
        in_specs=[pl.BlockSpec((Ts, 2 * D_MODEL), lambda t: (t, 0)),
                  full3(GROUPS, 1, DG), full3(GROUPS, 1, DG), full3(GROUPS, SBLOCK, SBLOCK), full3(GROUPS, SBLOCK, 1)],
        out_specs=pl.BlockSpec((Ts, D_MODEL), lambda t: (t, 0)),
        out_shape=jax.ShapeDtypeStruct((S, D_MODEL), BF16),
        compiler_params=_cp(("parallel",)),
    )(ps, ln_g, ln_b, w_sp, b_sp)


def _sgu_bwd_call(ps, ln_g, ln_b, w_sp, b_sp, dy, *, name, rider=None):
    S = ps.shape[0]
    Ts = min(SGU_TILE, S)
    r_arrays, r_in, r_out, r_shapes, r_scratch, r_tiles, r_resident = _rider_lists(rider)
    assert rider is None or rider["steps"] == S // Ts

    def body(ps_ref, lg_ref, lb_ref, w_ref, b_ref, dy_ref, ds_ref, dlg_ref, dlb_ref, dw_ref, db_ref):
        @pl.when(pl.program_id(0) == 0)
        def _():
            dlg_ref[...] = jnp.zeros_like(dlg_ref)
            dlb_ref[...] = jnp.zeros_like(dlb_ref)
            dw_ref[...] = jnp.zeros_like(dw_ref)
            db_ref[...] = jnp.zeros_like(db_ref)

        mask = _sgu_mask()
        for g in range(GROUPS):
            wm = jnp.where(mask, w_ref[g], 0.0).astype(BF16)
            lg = lg_ref[g]
            for p in range(Ts // SBLOCK):
                rows = pl.ds(p * SBLOCK, SBLOCK)
                su = ps_ref[rows, pl.ds(g * DG, DG)].astype(F32)
                sv = ps_ref[rows, pl.ds(D_MODEL + g * DG, DG)].astype(F32)
                u, du = _gelu_and_grad(su)
                gv, dgv = _gelu_and_grad(sv)
                rs, xh = _ln_stats(gv)
                vn16 = (xh * lg + lb_ref[g]).astype(BF16)
                mixed = _dot(wm, vn16, NN) + b_ref[g]
                dyv = dy_ref[rows, pl.ds(g * DG, DG)].astype(F32)
                ds_ref[rows, pl.ds(g * DG, DG)] = (dyv * mixed * du).astype(BF16)
                dmix = dyv * u
                dmix16 = dmix.astype(BF16)
                db_ref[g] += jnp.sum(dmix, axis=-1, keepdims=True)
                dw_ref[g] += jnp.where(mask, _dot(dmix16, vn16, NT), 0.0)
                dvn = _dot(wm, dmix16, TN)
                dlg_ref[g] += jnp.sum(dvn * xh, axis=0, keepdims=True)
                dlb_ref[g] += jnp.sum(dvn, axis=0, keepdims=True)
                dxh = dvn * lg
                dvf = rs * (dxh - jnp.mean(dxh, axis=-1, keepdims=True)
                            - xh * jnp.mean(dxh * xh, axis=-1, keepdims=True))
                ds_ref[rows, pl.ds(D_MODEL + g * DG, DG)] = (dvf * dgv).astype(BF16)

    full3 = lambda a, b, c: pl.BlockSpec((a, b, c), lambda t: (0, 0, 0))
    tiles = [((Ts, 2 * D_MODEL), BF16), ((Ts, D_MODEL), BF16), ((Ts, 2 * D_MODEL), BF16)] + r_tiles
    return pl.pallas_call(
        _with_rider(body, 6, 5, 0, rider), name=name, grid=(S // Ts,),
        in_specs=[pl.BlockSpec((Ts, 2 * D_MODEL), lambda t: (t, 0)),
                  full3(GROUPS, 1, DG), full3(GROUPS, 1, DG), full3(GROUPS, SBLOCK, SBLOCK), full3(GROUPS, SBLOCK, 1),
                  pl.BlockSpec((Ts, D_MODEL), lambda t: (t, 0))] + r_in,
        out_specs=[pl.BlockSpec((Ts, 2 * D_MODEL), lambda t: (t, 0)),
                   full3(GROUPS, 1, DG), full3(GROUPS, 1, DG), full3(GROUPS, SBLOCK, SBLOCK), full3(GROUPS, SBLOCK, 1)]
        + r_out,
        out_shape=[jax.ShapeDtypeStruct((S, 2 * D_MODEL), BF16),
                   jax.ShapeDtypeStruct((GROUPS, 1, DG), F32), jax.ShapeDtypeStruct((GROUPS, 1, DG), F32),
                   jax.ShapeDtypeStruct((GROUPS, SBLOCK, SBLOCK), F32),
                   jax.ShapeDtypeStruct((GROUPS, SBLOCK, 1), F32)] + r_shapes,
        scratch_shapes=r_scratch,
        compiler_params=_cp(("arbitrary",), _vmem_limit(tiles, r_resident)),
    )(ps, ln_g, ln_b, w_sp, b_sp, dy, *r_arrays)


def _position():
    return lax.axis_index("x"), lax.axis_index("y"), lax.axis_index("c")


def _gather_copies(srcs, dsts, send_sems, recv_sems, local_sems):
    x, y, c = _position()
    me, sibling = (x, y, c), (x, y, 1 - c)
    chips = [(1 - x, y), (x, 1 - y), (1 - x, 1 - y)]
    n = len(srcs)

    def slab(a, block):
        px, py, pc = block
        return dsts[a].at[4 * px + 2 * py + pc]

    def copy(a, k, block, to, src=None):
        return pltpu.make_async_remote_copy(
            src_ref=slab(a, block) if src is None else src, dst_ref=slab(a, block),
            send_sem=send_sems.at[7 * a + k], recv_sem=recv_sems.at[7 * a + k], device_id=to, device_id_type=MESH)

    mine = [pltpu.make_async_copy(srcs[a], slab(a, me), local_sems.at[a]) for a in range(n)]
    for cp in mine:
        cp.start()
    first = []
    for a in range(n):
        first.append(copy(a, 0, me, sibling, src=srcs[a]))
        first += [copy(a, 1 + j, me, (*chip, c), src=srcs[a]) for j, chip in enumerate(chips)]
    for cp in first:
        cp.start()
    passed = []
    for j, chip in enumerate(chips):
        for a in range(n):
            copy(a, 1 + j, (*chip, c), me).wait_recv()
            cp = copy(a, 4 + j, (*chip, c), sibling)
            cp.start()
            passed.append(cp)
    for a in range(n):
        copy(a, 0, sibling, me).wait_recv()
        for j, chip in enumerate(chips):
            copy(a, 4 + j, (*chip, 1 - c), me).wait_recv()
    for cp in first + passed:
        cp.wait_send()
    for cp in mine:
        cp.wait()


def _all_gather_hbm(shards, *, name):
    n = len(shards)

    def body(*refs):
        srcs, dsts = refs[:n], refs[n:2 * n]
        send_sems, recv_sems, local_sems = refs[2 * n:]
        _gather_copies(srcs, dsts, send_sems, recv_sems, local_sems)

    return pl.pallas_call(
        body, name=name,
        in_specs=[ANY] * n, out_specs=[ANY] * n,
        out_shape=[jax.ShapeDtypeStruct((N_DEV, *s.shape), s.dtype) for s in shards],
        scratch_shapes=_comm_sems(n),
    )(*shards)


def _all_reduce_small(part, *, name):
    R, W = part.shape

    def body(x_ref, out_ref, gathered, send_sems, recv_sems, local_sems):
        _gather_copies([x_ref], [gathered], send_sems, recv_sems, local_sems)
        acc = gathered[0]
        for d in range(1, N_DEV):
            acc = acc + gathered[d]
        out_ref[...] = acc

    return pl.pallas_call(
        body, name=name,
        in_specs=[VMEM_SPEC], out_specs=VMEM_SPEC,
        out_shape=jax.ShapeDtypeStruct((R, W), F32),
        scratch_shapes=[pltpu.VMEM((N_DEV, R, W), F32),
                        pltpu.SemaphoreType.DMA((7,)), pltpu.SemaphoreType.DMA((7,)), pltpu.SemaphoreType.DMA((1,))],
    )(part)


FLIPS = [(fx, fy, fc) for fx in (0, 1) for fy in (0, 1) for fc in (0, 1)][1:]


def _scatter_copies(srcs, dsts, send_sems, recv_sems, local_sems, waves=1):
    n = len(srcs)
    x, y, c = _position()
    me = 4 * x + 2 * y + c
    mine = [pltpu.make_async_copy(srcs[a].at[me], dsts[a].at[me], local_sems.at[a]) for a in range(n)]
    for cp in mine:
        cp.start()
    peers = []
    for fx, fy, fc in FLIPS:
        tx = 1 - x if fx else x
        ty = 1 - y if fy else y
        tc = 1 - c if fc else c
        peers.append(((tx, ty, tc), 4 * tx + 2 * ty + tc))
    copies = []
    for w in range(waves):
        wave = []
        for k, (peer_id, peer) in enumerate(peers):
            for a in range(n):
                rows = srcs[a].shape[1]
                step = rows if waves == 1 else (rows // waves) // 16 * 16
                r0 = w * step
                cut = pl.ds(r0, step if w < waves - 1 else rows - r0)
                sem = (7 * a + k) * waves + w
                cp = pltpu.make_async_remote_copy(
                    src_ref=srcs[a].at[peer, cut], dst_ref=dsts[a].at[me, cut],
                    send_sem=send_sems.at[sem], recv_sem=recv_sems.at[sem],
                    device_id=peer_id, device_id_type=MESH)
                cp.start()
                wave.append(cp)
        for cp in wave:
            cp.wait_send()
        copies += wave
    for cp in copies:
        cp.wait_recv()
    for cp in mine:
        cp.wait()


def _comm_sems(n, waves=1):
    return [pltpu.SemaphoreType.DMA((7 * n * waves,)), pltpu.SemaphoreType.DMA((7 * n * waves,)),
            pltpu.SemaphoreType.DMA((n,))]


def _scatter_blocks(parts, *, name, waves=1):
    n = len(parts)

    def body(*refs):
        _scatter_copies(refs[:n], refs[n:2 * n], *refs[2 * n:], waves=waves)

    return pl.pallas_call(
        body, name=name,
        in_specs=[ANY] * n, out_specs=[ANY] * n,
        out_shape=[jax.ShapeDtypeStruct(p.shape, p.dtype) for p in parts],
        scratch_shapes=_comm_sems(n, waves),
    )(*parts)


def _handshake(peers):
    barrier = pltpu.get_barrier_semaphore()
    for peer in peers:
        pl.semaphore_signal(barrier, inc=1, device_id=peer, device_id_type=MESH)
    pl.semaphore_wait(barrier, len(peers))


def _sequencer_call(arrays, out_types, copies_fn, peers_fn, *, name, collective_id, waves=1):
    n = len(arrays)
    srcs = [jax.new_ref(a, memory_space=pltpu.MemorySpace.HBM) for a in arrays]
    dsts = [jax.empty_ref(t, memory_space=pltpu.MemorySpace.HBM) for t in out_types]
    extra = {} if waves == 1 else {"waves": waves}

    @pl.kernel(mesh=plsc.ScalarSubcoreMesh(axis_name="sequencer", num_cores=1), name=name,
               scratch_types=_comm_sems(n, waves), compiler_params=pltpu.CompilerParams(collective_id=collective_id))
    def launch(send_sems, recv_sems, local_sems):
        _handshake(peers_fn())
        copies_fn(srcs, dsts, send_sems, recv_sems, local_sems, **extra)

    launch()
    return [d[...] for d in dsts]


def _all_other_devices():
    x, y, c = _position()
    return [(1 - x if fx else x, 1 - y if fy else y, 1 - c if fc else c) for fx, fy, fc in FLIPS]


def _gather_peers():
    x, y, c = _position()
    return [(x, y, 1 - c), (1 - x, y, c), (x, 1 - y, c), (1 - x, 1 - y, c)]


def _scatter_blocks_async(parts, *, name, collective_id, waves=1):
    return _sequencer_call(parts, [jax.ShapeDtypeStruct(p.shape, p.dtype) for p in parts],
                           _scatter_copies, _all_other_devices, name=name, collective_id=collective_id, waves=waves)


def _all_gather_async(shards, *, name, collective_id):
    return _sequencer_call(shards, [jax.ShapeDtypeStruct((N_DEV, *s.shape), s.dtype) for s in shards],
                           _gather_copies, _gather_peers, name=name, collective_id=collective_id)


def _adamw_math(w, g, m, v):
    m = ADAM_B1 * m + (1.0 - ADAM_B1) * g
    v = ADAM_B2 * v + (1.0 - ADAM_B2) * (g * g)
    m_hat = m / (1.0 - ADAM_B1 ** ADAM_STEP)
    v_hat = v / (1.0 - ADAM_B2 ** ADAM_STEP)
    delta = -ADAM_LR * (m_hat / (jnp.sqrt(v_hat) + ADAM_EPS) + ADAM_WD * w)
    return delta, m, v


def _adamw_reduce_call(recv, w, m, v, *, name, T=128):
    R, W = w.shape
    if R % T == 0:
        tr, tw = T, W
    elif (R // 2) % 16 == 0:
        tr, tw = R // 2, W
    else:
        tr, tw = R, 2 * LANES

    def body(p_ref, w_ref, m_ref, v_ref, g_out, d_out, m_out, v_out):
        g = p_ref[0].astype(F32)
        for d in range(1, N_DEV):
            g = g + p_ref[d].astype(F32)
        g_out[...] = g
        d_out[...], m_out[...], v_out[...] = _adamw_math(w_ref[...], g, m_ref[...], v_ref[...])

    row = pl.BlockSpec((tr, tw), lambda i, j: (i, j))
    out = jax.ShapeDtypeStruct((R, W), F32)
    return pl.pallas_call(
        body, name=name, grid=(R // tr, W // tw),
        in_specs=[pl.BlockSpec((N_DEV, tr, tw), lambda i, j: (0, i, j)), row, row, row],
        out_specs=[row] * 4, out_shape=[out] * 4,
        compiler_params=_cp(("parallel", "parallel"), VMEM_BIG),
    )(recv, w, m, v)


def _adamw_small_call(w, g, m, v, *, name):
    def body(w_ref, g_ref, m_ref, v_ref, d_out, m_out, v_out):
        d_out[...], m_out[...], v_out[...] = _adamw_math(w_ref[...], g_ref[...], m_ref[...], v_ref[...])

    out = jax.ShapeDtypeStruct(w.shape, F32)
    return pl.pallas_call(body, name=name, in_specs=[VMEM_SPEC] * 4, out_specs=[VMEM_SPEC] * 3,
                          out_shape=[out] * 3)(w, g, m, v)


def _tile_rows(n_elems):
    return -(-n_elems // (8 * LANES)) * 8


def _pack_rows(parts, rows):
    pieces = []
    for p in parts:
        q = p.reshape(-1, LANES)
        pieces.append(jnp.pad(q, ((0, _tile_rows(p.size) - q.shape[0]), (0, 0))))
    buf = jnp.concatenate(pieces, axis=0)
    return jnp.pad(buf, ((0, rows - buf.shape[0]), (0, 0)))


def _unpack_rows(buf, shapes):
    out, off = [], 0
    for shp in shapes:
        n = 1
        for s in shp:
            n *= s
        out.append(buf[off:off + n // LANES].reshape(shp))
        off += _tile_rows(n)
    return out


def _in_w_blocks_call(a, dps, dpg, dw_a, dw_low, *, name, tk=TOKEN_TILE):
    S = a.shape[0]
    tk = min(tk, S)
    steps = S // tk
    n_s, n_g = dps.shape[1], dpg.shape[1]
    out_shape = (N_DEV, IN_BLK, D_MODEL)

    def body(a_ref, dps_ref, dpg_ref, dwa_ref, dwl_ref, out_ref, acc_s, acc_g):
        k = pl.program_id(0)

        @pl.when(k == 0)
        def _():
            acc_s[...] = jnp.zeros_like(acc_s)
            acc_g[...] = jnp.zeros_like(acc_g)

        av = a_ref[...]
        for src, acc, n in ((dps_ref, acc_s, n_s), (dpg_ref, acc_g, n_g)):
            for m0 in range(0, n, 1024):
                acc[m0:m0 + 1024, :] += _dot(src[:, m0:m0 + 1024], av, TN)

        @pl.when(k == steps - 1)
        def _():
            groups = ((dwa_ref, A_COLS), (dwl_ref, LOW_COLS), (acc_s, S_COLS), (acc_g, G_COLS))
            for d in range(N_DEV):
                lo, hi = IN_BLK * d, IN_BLK * (d + 1)
                for ref, (c0, c1) in groups:
                    s0, e0 = max(lo, c0), min(hi, c1)
                    if s0 < e0:
                        out_ref[d, s0 - lo:e0 - lo, :] = ref[s0 - c0:e0 - c0, :].astype(BF16)

    tiles = [((tk, D_MODEL), BF16), ((tk, n_s), BF16), ((tk, n_g), BF16)]
    resident = [(dw_a.shape, BF16), (dw_low.shape, BF16), (out_shape, BF16), ((n_s, D_MODEL), F32), ((n_g, D_MODEL), F32)]
    return pl.pallas_call(
        body, name=name, grid=(steps,),
        in_specs=[pl.BlockSpec((tk, D_MODEL), lambda k: (k, 0)), pl.BlockSpec((tk, n_s), lambda k: (k, 0)),
                  pl.BlockSpec((tk, n_g), lambda k: (k, 0)), _res(dw_a.shape), _res(dw_low.shape)],
        out_specs=_acc(out_shape),
        out_shape=jax.ShapeDtypeStruct(out_shape, BF16),
        scratch_shapes=[pltpu.VMEM((n_s, D_MODEL), F32), pltpu.VMEM((n_g, D_MODEL), F32)],
        compiler_params=_cp(("arbitrary",), _vmem_limit(tiles, resident)),
    )(a, dps, dpg, dw_a, dw_low)


def _local_step(x, target, W, scatter, finish):
    g1, g2, g3, g4 = [W[n].reshape(1, D_MODEL) for n in ("norm_pre_mix", "norm_post_mix", "norm_pre_ffn", "norm_post_ffn")]
    wfi, wfo = W["w_ffn_in"], W["w_ffn_out"].reshape(4, FF_BLK, D_MODEL)
    wg = jnp.pad(W["w_gate_up"], ((0, LANES - RANK), (0, 0))).astype(BF16)
    bgate = W["b_gate"].reshape(1, QK)
    gnorm = W["gla_norm"].reshape(HEADS, 1, DV)
    ln_g = W["sgu_ln_g"].reshape(GROUPS, 1, DG)
    ln_b = W["sgu_ln_b"].reshape(GROUPS, 1, DG)
    w_sp = W["w_spatial"]
    b_sp = W["b_spatial"].reshape(GROUPS, SBLOCK, 1)

    a, pa, alow, ps, pg, w_in_t = _in_proj_call(x, g1, W["w_in_blocks"], name="in_proj")
    y_gla, states = _gla_fwd_call(pa, alow, wg, bgate, gnorm, name="gla_fwd")
    y_sgu = _sgu_fwd_call(ps, ln_g, ln_b, w_sp, b_sp, name="sgu_fwd")
    t1, t2, merged, mix, x1, h = _mixer_tail_call(y_gla, y_sgu, pg, x, W["w_branch_gla"], W["w_branch_sgu"],
                                                  W["w_out"], g2, g3, name="mixer_tail")
    gu, z = _ffn_in_call(h, wfi, name="ffn_in")
    loss, dx2, dy, dg4 = _ffn_out_loss_call(z, wfo, x1, target, g4, name="ffn_out_loss")

    grads = {"norm_post_ffn": dg4}
    done = {}
    dgu = _ffn_out_bwd_call(dy, wfo, gu, name="ffn_out_bwd")
    dw_ffn_out = _weight_grads_call([z, dy], [(0, 1)], name="d_ffn_out_w")[0].reshape(N_DEV, D_FF // N_DEV, D_MODEL)
    dw_ffn_in = _ffn_in_grad_call(h, dgu, name="d_ffn_in_w")
    dgu, dw_ffn_out, dw_ffn_in = lax.optimization_barrier((dgu, dw_ffn_out, dw_ffn_in))
    ffn_received = scatter(("w_ffn_out", "w_ffn_in"), [dw_ffn_out, dw_ffn_in])
    dx1, dmix, grads["norm_pre_ffn"], grads["norm_post_mix"] = _ffn_in_bwd_call(dgu, wfi, dx2, x1, mix, g3, g2, name="ffn_in_bwd")
    dt1, dt2, dpg, dy_gla, dy_sgu = _mixer_bwd_call(dmix, t1, t2, pg, W["w_out"], W["w_branch_gla"], W["w_branch_sgu"],
                                                    name="mixer_bwd")
    rows = D_MODEL // N_DEV
    mixer_grads = _weight_grads_call([merged, dmix, y_gla, dt1, y_sgu, dt2], [(0, 1), (2, 3), (4, 5)], name="d_mixer_w")
    dy_gla, dy_sgu, mixer_grads = lax.optimization_barrier((dy_gla, dy_sgu, mixer_grads))
    mixer_received = scatter(("w_out", "w_branch_gla", "w_branch_sgu"),
                             [g.reshape(N_DEV, rows, D_MODEL) for g in mixer_grads])
    dpa, dlogit, dgn, dbg = _gla_bwd_call(pa, alow, wg, bgate, gnorm, states, dy_gla, name="gla_bwd")
    ffn_done, dpa, dlogit = lax.optimization_barrier((finish(ffn_received), dpa, dlogit))
    done.update(ffn_done)
    dps, dlg, dlb, dwsp, dbsp = _sgu_bwd_call(ps, ln_g, ln_b, w_sp, b_sp, dy_sgu, name="sgu_bwd")
    mixer_done, dps = lax.optimization_barrier((finish(mixer_received), dps))
    done.update(mixer_done)
    dlow = _mm(dlogit, wg, "nt", BF16, name="d_gate_up_x")
    dw_a, dw_low = _weight_grads_call([a, dpa, dlow], [(1, 0), (2, 0)], name="d_in_w_qkvr")
    dw_in = _in_w_blocks_call(a, dps, dpg, dw_a, dw_low, name="d_in_w_blocks")
    in_received = scatter(("w_in",), [dw_in])
    dwg = _mm(alow, dlogit, "tn", F32, name="d_gate_up_w")
    grad_x, grads["norm_pre_mix"] = _in_proj_bwd_call(dpa, dlow, dps, dpg, w_in_t, x, dx1, g1, name="in_proj_bwd")

    grads["w_gate_up"] = dwg[:RANK]
    grads["b_gate"] = dbg
    grads["gla_norm"] = dgn
    grads["sgu_ln_g"] = dlg
    grads["sgu_ln_b"] = dlb
    grads["w_spatial"] = dwsp
    grads["b_spatial"] = dbsp
    done.update(finish(in_received))
    return loss, grad_x, grads, done


WEIGHTS = ("norm_pre_mix", "w_in", "w_gate_up", "b_gate", "gla_norm", "sgu_ln_g", "sgu_ln_b", "w_spatial",
           "b_spatial", "w_branch_gla", "w_branch_sgu", "w_out", "norm_post_mix", "norm_pre_ffn", "w_ffn_in",
           "w_ffn_out", "norm_post_ffn")
BIG = ("w_in", "w_branch_gla", "w_branch_sgu", "w_out", "w_ffn_in", "w_ffn_out")
MIXER = ("w_branch_gla", "w_branch_sgu", "w_out")
FFN = ("w_ffn_in", "w_ffn_out")
COLUMN_SHARDED = ("w_in", "w_ffn_in")
LATE_SCATTER_WAVES = 4
SMALL = tuple(n for n in WEIGHTS if n not in BIG)
SMALL_SHARDED = ("w_gate_up", "gla_norm", "sgu_ln_g", "sgu_ln_b")
SMALL_FULL = {"norm_pre_mix": (1024,), "w_gate_up": (16, 512), "b_gate": (512,), "gla_norm": (4, 256),
              "sgu_ln_g": (4, 256), "sgu_ln_b": (4, 256), "w_spatial": (4, 128, 128), "b_spatial": (4, 128),
              "norm_post_mix": (1024,), "norm_pre_ffn": (1024,), "norm_post_ffn": (1024,)}
SMALL_GRAD_ROWS = 648
SMALL_STATE_ROWS = 600
SMALL_GATHER_ROWS = 32


def kernel(x, norm_pre_mix, w_in, w_gate_up, b_gate, gla_norm, sgu_ln_g, sgu_ln_b, w_spatial, b_spatial, w_branch_gla, w_branch_sgu, w_out, norm_post_mix, norm_pre_ffn, w_ffn_in, w_ffn_out, norm_post_ffn, loss_target, m_norm_pre_mix, m_w_in, m_w_gate_up, m_b_gate, m_gla_norm, m_sgu_ln_g, m_sgu_ln_b, m_w_spatial, m_b_spatial, m_w_branch_gla, m_w_branch_sgu, m_w_out, m_norm_post_mix, m_norm_pre_ffn, m_w_ffn_in, m_w_ffn_out, m_norm_post_ffn, v_norm_pre_mix, v_w_in, v_w_gate_up, v_b_gate, v_gla_norm, v_sgu_ln_g, v_sgu_ln_b, v_w_spatial, v_b_spatial, v_w_branch_gla, v_w_branch_sgu, v_w_out, v_norm_post_mix, v_norm_pre_ffn, v_w_ffn_in, v_w_ffn_out, v_norm_post_ffn):
    given = dict(locals())
    def local(a, n):
        return a[0].T if n in COLUMN_SHARDED else a[0]

    w = {n: local(given[n], n) for n in WEIGHTS}
    m = {n: local(given["m_" + n], n) for n in WEIGHTS}
    v = {n: local(given["v_" + n], n) for n in WEIGHTS}
    xs, target = x[0], loss_target[0]
    me = 4 * lax.axis_index("x") + 2 * lax.axis_index("y") + lax.axis_index("c")

    small_shard = _pack_rows([w[n] for n in SMALL_SHARDED], SMALL_GATHER_ROWS)
    first = _all_gather_hbm([w["w_in"].astype(BF16), small_shard], name="gather_w_in")
    rest, first = lax.optimization_barrier(([w[n].astype(BF16) for n in MIXER + FFN], first))
    rest_blocks = _all_gather_async(rest, name="gather_rest", collective_id=1)
    W = {n: w[n] for n in SMALL if n not in SMALL_SHARDED}
    blocks = dict(zip(MIXER + FFN, rest_blocks))
    for n in ("w_branch_gla", "w_branch_sgu", "w_out", "w_ffn_out"):
        W[n] = blocks[n].reshape(-1, D_MODEL)
    W["w_ffn_in"] = blocks["w_ffn_in"]
    W["w_in_blocks"] = first[0]
    small_blocks = first[1]
    off = 0
    for n in SMALL_SHARDED:
        r, c = w[n].shape
        blk = small_blocks[:, off:off + r * c // LANES].reshape(N_DEV, r, c)
        W[n] = blk.transpose(1, 0, 2).reshape(r, N_DEV * c)
        off += _tile_rows(r * c)

    scatter_ids = iter((3, 4, 5))

    def scatter(names, parts):
        waves = LATE_SCATTER_WAVES if "w_in" in names else 1
        got = _scatter_blocks_async(parts, name="scatter_" + "_".join(names), collective_id=next(scatter_ids),
                                    waves=waves)
        return dict(zip(names, got))

    def finish(received):
        return {n: _adamw_reduce_call(r, w[n], m[n], v[n], name="adamw_" + n) for n, r in received.items()}

    loss_part, grad_x, grads, big_done = _local_step(xs, target, W, scatter, finish)

    small_part = jnp.concatenate([_pack_rows([grads[n] for n in SMALL], SMALL_GRAD_ROWS),
                                  jnp.broadcast_to(loss_part, (8, LANES))], axis=0)
    small_sum = _all_reduce_small(small_part, name="reduce_small")
    loss = small_sum[SMALL_GRAD_ROWS, 0]
    g_small = dict(zip(SMALL, _unpack_rows(small_sum, [SMALL_FULL[n] for n in SMALL])))
    for n in SMALL_SHARDED:
        c = w[n].shape[1]
        g_small[n] = lax.dynamic_slice_in_dim(g_small[n], me * c, c, axis=1)

    g_big, d_big, m_big, v_big = [{n: big_done[n][i] for n in BIG} for i in range(4)]
    small_shapes = [w[n].shape for n in SMALL]
    small_state = [_pack_rows([s[n] for n in SMALL], SMALL_STATE_ROWS) for s in (w, g_small, m, v)]
    small_out = _adamw_small_call(*small_state, name="adamw_small")
    d_small, m_small, v_small = [dict(zip(SMALL, _unpack_rows(o, small_shapes))) for o in small_out]

    def pick(big, sml):
        return [(big[n].T if n in COLUMN_SHARDED else big[n] if n in BIG else sml[n])[None] for n in WEIGHTS]

    return (loss, grad_x[None], *pick(g_big, g_small), *pick(d_big, d_small), *pick(m_big, m_small),
            *pick(v_big, v_small))
```

```python
import jax
import jax.numpy as jnp
from jax import lax
from jax.experimental import pallas as pl
from jax.experimental.pallas import tpu as pltpu
from jax.experimental.pallas import tpu_sc as plsc

F32 = jnp.float32
BF16 = jnp.bfloat16

D_MODEL = 1024
N_DEV = 8
CHUNK = 64
HEADS = 4
DK = 128
DV = 256
QK = HEADS * DK
GV = HEADS * DV
RANK = 16
GROUPS = 4
SBLOCK = 128
DG = 256
D_FF = 2816
FF_BLK = 704
EPS = 1e-6
Q_SCALE = DK ** -0.5
LANES = 128
VMEM_BIG = 48 * 1024 * 1024

D_IN = 7184
IN_BLK = 898
A_COLS = (0, 3072)
LOW_COLS = (3072, 3088)
S_COLS = (3088, 5136)
G_COLS = (5136, 7184)

ADAM_LR = 0.001
ADAM_B1 = 0.9
ADAM_B2 = 0.999
ADAM_EPS = 1e-08
ADAM_WD = 0.01
ADAM_STEP = 10

MESH = pl.DeviceIdType.MESH
ANY = pl.BlockSpec(memory_space=pl.ANY)
VMEM_SPEC = pl.BlockSpec(memory_space=pltpu.VMEM)


def _cp(sem=None, vmem=None):
    return pltpu.CompilerParams(dimension_semantics=sem, vmem_limit_bytes=vmem)


def _sig(x):
    return 0.5 * jnp.tanh(0.5 * x) + 0.5


GELU_C = 0.7978845608028654
GELU_A = 0.044715


def _gelu(x):
    t = jnp.tanh((GELU_C * x) * (1.0 + GELU_A * (x * x)))
    return (0.5 * x) * (1.0 + t)


def _gelu_and_grad(x):
    x2 = x * x
    t = jnp.tanh((GELU_C * x) * (1.0 + GELU_A * x2))
    one_t = 1.0 + t
    hx = 0.5 * x
    grad = 0.5 * one_t + (hx * (1.0 - t * t)) * (GELU_C + (3.0 * GELU_A * GELU_C) * x2)
    return hx * one_t, grad


def _logsig(x):
    return jnp.minimum(x, 0.0) - jnp.log1p(jnp.exp(-jnp.abs(x)))


def _dot(a, b, dims):
    return lax.dot_general(a, b, (dims, ((), ())), preferred_element_type=F32)


NN = ((1,), (0,))
NT = ((1,), (1,))
TN = ((0,), (0,))


def _exact_mask_dot(mask_bf16, x):
    hi = x.astype(BF16)
    r1 = x - hi.astype(F32)
    mid = r1.astype(BF16)
    lo = (r1 - mid.astype(F32)).astype(BF16)
    return _dot(mask_bf16, hi, NN) + _dot(mask_bf16, mid, NN) + _dot(mask_bf16, lo, NN)


def _pick_tile(dim, target):
    if dim <= target:
        return dim
    best = None
    for t in range(LANES, int(1.4 * target) + 1, LANES):
        if dim % t == 0:
            best = t
    assert best is not None, (dim, target)
    return best


def _mm_call(a, b, *, name, grid, a_spec, b_spec, o_spec, out_shape, dims, acc_shape):
    nk = grid[2]

    def body(a_ref, b_ref, o_ref, *acc):
        part = _dot(a_ref[...], b_ref[...], dims)
        if nk == 1:
            o_ref[...] = part.astype(o_ref.dtype)
        else:
            acc_ref = acc[0]
            k = pl.program_id(2)

            @pl.when(k == 0)
            def _():
                acc_ref[...] = part

            @pl.when(k > 0)
            def _():
                acc_ref[...] += part

            @pl.when(k == nk - 1)
            def _():
                o_ref[...] = acc_ref[...].astype(o_ref.dtype)

    return pl.pallas_call(
        body, name=name, grid=grid, in_specs=[a_spec, b_spec], out_specs=o_spec, out_shape=out_shape,
        scratch_shapes=[] if nk == 1 else [pltpu.VMEM(acc_shape, F32)],
        compiler_params=_cp(("parallel", "parallel", "arbitrary"), VMEM_BIG),
    )(a, b)


def _mm(a, b, mode, out_dtype, *, name, tm=512, tn=1024, tk=1024):
    if mode == "nn":
        (M, K), (_, N) = a.shape, b.shape
    elif mode == "nt":
        (M, K), (N, _) = a.shape, b.shape
    else:
        (K, M), (_, N) = a.shape, b.shape
    tm, tn, tk = _pick_tile(M, tm), _pick_tile(N, tn), _pick_tile(K, tk)
    if mode == "nn":
        a_spec = pl.BlockSpec((tm, tk), lambda i, j, k: (i, k))
        b_spec = pl.BlockSpec((tk, tn), lambda i, j, k: (k, j))
        dims = NN
    elif mode == "nt":
        a_spec = pl.BlockSpec((tm, tk), lambda i, j, k: (i, k))
        b_spec = pl.BlockSpec((tn, tk), lambda i, j, k: (j, k))
        dims = NT
    else:
        a_spec = pl.BlockSpec((tk, tm), lambda i, j, k: (k, i))
        b_spec = pl.BlockSpec((tk, tn), lambda i, j, k: (k, j))
        dims = TN
    return _mm_call(a, b, name=name, grid=(M // tm, N // tn, K // tk), a_spec=a_spec, b_spec=b_spec,
                    o_spec=pl.BlockSpec((tm, tn), lambda i, j, k: (i, j)),
                    out_shape=jax.ShapeDtypeStruct((M, N), out_dtype), dims=dims, acc_shape=(tm, tn))


ROW_TILE = 512
SUB_ROWS = 256


def _sub_tiles(T):
    return [pl.ds(r0, min(SUB_ROWS, T)) for r0 in range(0, T, SUB_ROWS)]


def _rt(T, W, c=0):
    return pl.BlockSpec((T, W), lambda i: (i, c))


def _rt3(nb, T, W):
    return pl.BlockSpec((nb, T, W), lambda i: (0, i, 0))


def _res(shape):
    nd = len(shape)
    return pl.BlockSpec(tuple(shape), lambda i: (0,) * nd, pipeline_mode=pl.Buffered(1))


def _acc(shape):
    nd = len(shape)
    return pl.BlockSpec(tuple(shape), lambda i: (0,) * nd, pipeline_mode=pl.Buffered(1))


def _nbytes(shape, dtype):
    n = jnp.dtype(dtype).itemsize
    for s in shape:
        n *= s
    return n


def _vmem_limit(tiles, resident, temps=16 * 1024 * 1024):
    need = 2 * sum(_nbytes(s, d) for s, d in tiles) + sum(_nbytes(s, d) for s, d in resident) + temps
    return min(need, 60 * 1024 * 1024)


def _tok_call(body, *, name, S, T, ins, outs, semantics="parallel"):
    tiles = [(spec.block_shape, a.dtype) for a, spec, kind in ins + outs if kind == "tile"]
    resident = [(a.shape, a.dtype) for a, spec, kind in ins + outs if kind == "res"]
    return pl.pallas_call(
        body, name=name, grid=(S // T,),
        in_specs=[spec for _, spec, _ in ins], out_specs=[spec for _, spec, _ in outs],
        out_shape=[jax.ShapeDtypeStruct(a.shape, a.dtype) for a, _, _ in outs],
        compiler_params=_cp((semantics,), _vmem_limit(tiles, resident)),
    )(*[a for a, _, _ in ins])


def _tile(a, spec):
    return (a, spec, "tile")


def _whole(a):
    return (a, _res(a.shape), "res")


def _out_tile(shape, dtype, spec):
    return (jax.ShapeDtypeStruct(shape, dtype), spec, "tile")


def _out_acc(shape, dtype=F32):
    return (jax.ShapeDtypeStruct(shape, dtype), _acc(shape), "res")


def _rms_stats(x):
    r = lax.rsqrt(jnp.mean(x * x, axis=-1, keepdims=True) + EPS)
    return r, x * r


def _rms_bwd(xh, r, g, dy):
    dxh = dy * g
    dx = r * (dxh - xh * jnp.mean(dxh * xh, axis=-1, keepdims=True))
    dg = jnp.sum(dy * xh, axis=0, keepdims=True)
    return dx, dg


def _accum(ref, val):
    @pl.when(pl.program_id(0) == 0)
    def _():
        ref[...] = val

    @pl.when(pl.program_id(0) > 0)
    def _():
        ref[...] += val


def _dot_rows_t(a, w_ref, row0, o_ref, chunk=1024):
    n = o_ref.shape[1]
    for n0 in range(0, n, chunk):
        n1 = min(n, n0 + chunk)
        o_ref[:, n0:n1] = _dot(a, w_ref[row0 + n0:row0 + n1, :], NT).astype(o_ref.dtype)


def _in_proj_call(x, g1, w_blocks, *, name, T=ROW_TILE):
    S = x.shape[0]

    def body(x_ref, g_ref, blk_ref, a_ref, pa_ref, al_ref, ps_ref, pg_ref, w_ref):
        @pl.when(pl.program_id(0) == 0)
        def _():
            for d in range(N_DEV):
                w_ref[d * IN_BLK:(d + 1) * IN_BLK, :] = blk_ref[d]

        _, xh = _rms_stats(x_ref[...])
        a = (xh * g_ref[...]).astype(BF16)
        a_ref[...] = a
        _dot_rows_t(a, w_ref, A_COLS[0], pa_ref)
        _dot_rows_t(a, w_ref, LOW_COLS[0], al_ref)
        _dot_rows_t(a, w_ref, S_COLS[0], ps_ref)
        _dot_rows_t(a, w_ref, G_COLS[0], pg_ref)

    widths = (D_MODEL, A_COLS[1] - A_COLS[0], LANES, S_COLS[1] - S_COLS[0], G_COLS[1] - G_COLS[0])
    return _tok_call(
        body, name=name, S=S, T=T, semantics="arbitrary",
        ins=[_tile(x, _rt(T, D_MODEL)), _whole(g1), _whole(w_blocks)],
        outs=[_out_tile((S, w), BF16, _rt(T, w)) for w in widths] + [_out_acc((D_IN, D_MODEL), BF16)])


def _mixer_tail_call(y_gla, y_sgu, pg, x, w_bg, w_bs, w_out, g2, g3, *, name, T=ROW_TILE):
    S = x.shape[0]

    def body(yg_ref, ys_ref, pg_ref, x_ref, wbg_ref, wbs_ref, wo_ref, g2_ref, g3_ref,
             t1_ref, t2_ref, mg_ref, mix_ref, x1_ref, h_ref):
        for rows in _sub_tiles(T):
            t1 = _dot(yg_ref[rows, :], wbg_ref[...], NN)
            t2 = _dot(ys_ref[rows, :], wbs_ref[...], NN)
            t1_ref[rows, :] = t1.astype(BF16)
            t2_ref[rows, :] = t2.astype(BF16)
            sg = _sig(pg_ref[rows, pl.ds(0, D_MODEL)].astype(F32))
            ss = _sig(pg_ref[rows, pl.ds(D_MODEL, D_MODEL)].astype(F32))
            merged = (sg * t1 + ss * t2).astype(BF16)
            mg_ref[rows, :] = merged
            mix = _dot(merged, wo_ref[...], NN)
            mix_ref[rows, :] = mix
            _, mh = _rms_stats(mix)
            x1 = x_ref[rows, :] + mh * g2_ref[...]
            x1_ref[rows, :] = x1
            _, xh = _rms_stats(x1)
            h_ref[rows, :] = (xh * g3_ref[...]).astype(BF16)

    row = _rt(T, D_MODEL)
    b16 = lambda: _out_tile((S, D_MODEL), BF16, row)
    f32 = lambda: _out_tile((S, D_MODEL), F32, row)
    return _tok_call(
        body, name=name, S=S, T=T,
        ins=[_tile(y_gla, row), _tile(y_sgu, row), _tile(pg, _rt(T, 2 * D_MODEL)), _tile(x, row),
             _whole(w_bg), _whole(w_bs), _whole(w_out), _whole(g2), _whole(g3)],
        outs=[b16(), b16(), b16(), f32(), f32(), b16()])


def _ffn_in_call(h, wfi, *, name, T=ROW_TILE):
    S = h.shape[0]

    def body(h_ref, w_ref, gu_ref, z_ref):
        hv = h_ref[...]
        for d in range(4):
            gate = _dot(hv, w_ref[d], NT)
            up = _dot(hv, w_ref[d + 4], NT)
            gu_ref[d] = gate.astype(BF16)
            gu_ref[d + 4] = up.astype(BF16)
            z_ref[d] = (gate * _sig(gate) * up).astype(BF16)

    return _tok_call(
        body, name=name, S=S, T=T,
        ins=[_tile(h, _rt(T, D_MODEL)), _whole(wfi)],
        outs=[_out_tile((N_DEV, S, FF_BLK), BF16, _rt3(N_DEV, T, FF_BLK)),
              _out_tile((4, S, FF_BLK), BF16, _rt3(4, T, FF_BLK))])


def _ffn_out_loss_call(z, wfo, x1, target, g4, *, name, T=ROW_TILE):
    S = x1.shape[0]

    def body(z_ref, w_ref, x1_ref, t_ref, g4_ref, loss_ref, dx2_ref, dy_ref, dg4_ref):
        loss = jnp.zeros((1, 1), F32)
        dg4 = jnp.zeros((1, D_MODEL), F32)
        for rows in _sub_tiles(T):
            y = _dot(z_ref[0, rows, :], w_ref[0], NN)
            for d in range(1, 4):
                y = y + _dot(z_ref[d, rows, :], w_ref[d], NN)
            r, yh = _rms_stats(y)
            diff = x1_ref[rows, :] + yh * g4_ref[...] - t_ref[rows, :]
            loss = loss + 0.5 * jnp.sum(jnp.mean(diff * diff, axis=-1, keepdims=True), axis=0, keepdims=True)
            dx2 = diff * (1.0 / D_MODEL)
            dx2_ref[rows, :] = dx2
            dy, dg = _rms_bwd(yh, r, g4_ref[...], dx2)
            dy_ref[rows, :] = dy.astype(BF16)
            dg4 = dg4 + dg
        _accum(loss_ref, jnp.broadcast_to(loss, (1, LANES)))
        _accum(dg4_ref, dg4)

    row = _rt(T, D_MODEL)
    return _tok_call(
        body, name=name, S=S, T=T, semantics="arbitrary",
        ins=[_tile(z, _rt3(4, T, FF_BLK)), _whole(wfo), _tile(x1, row), _tile(target, row), _whole(g4)],
        outs=[_out_acc((1, LANES)), _out_tile((S, D_MODEL), F32, row), _out_tile((S, D_MODEL), BF16, row),
              _out_acc((1, D_MODEL))])


def _ffn_out_bwd_call(dy, wfo, gu, *, name, T=ROW_TILE):
    S = dy.shape[0]

    def body(dy_ref, w_ref, gu_ref, dgu_ref):
        dyv = dy_ref[...]
        for d in range(4):
            dz = _dot(dyv, w_ref[d], NT)
            gt = gu_ref[d].astype(F32)
            up = gu_ref[d + 4].astype(F32)
            s = _sig(gt)
            dgu_ref[d] = (dz * up * s * (1.0 + gt * (1.0 - s))).astype(BF16)
            dgu_ref[d + 4] = (dz * gt * s).astype(BF16)

    blocks = _rt3(N_DEV, T, FF_BLK)
    return _tok_call(
        body, name=name, S=S, T=T,
        ins=[_tile(dy, _rt(T, D_MODEL)), _whole(wfo), _tile(gu, blocks)],
        outs=[_out_tile((N_DEV, S, FF_BLK), BF16, blocks)])[0]


def _ffn_in_bwd_call(dgu, wfi, dx2, x1, mix, g3, g2, *, name, T=ROW_TILE):
    S = x1.shape[0]

    def body(dgu_ref, w_ref, dx2_ref, x1_ref, mix_ref, g3_ref, g2_ref, dx1_ref, dmix_ref, dg3_ref, dg2_ref):
        dg3 = jnp.zeros((1, D_MODEL), F32)
        dg2 = jnp.zeros((1, D_MODEL), F32)
        for rows in _sub_tiles(T):
            dh = _dot(dgu_ref[0, rows, :], w_ref[0], NN)
            for d in range(1, N_DEV):
                dh = dh + _dot(dgu_ref[d, rows, :], w_ref[d], NN)
            r3, xh = _rms_stats(x1_ref[rows, :])
            d3, g = _rms_bwd(xh, r3, g3_ref[...], dh)
            dg3 = dg3 + g
            dx1 = dx2_ref[rows, :] + d3
            dx1_ref[rows, :] = dx1
            r2, mh = _rms_stats(mix_ref[rows, :])
            dmix, g = _rms_bwd(mh, r2, g2_ref[...], dx1)
            dg2 = dg2 + g
            dmix_ref[rows, :] = dmix.astype(BF16)
        _accum(dg3_ref, dg3)
        _accum(dg2_ref, dg2)

    row = _rt(T, D_MODEL)
    return _tok_call(
        body, name=name, S=S, T=T, semantics="arbitrary",
        ins=[_tile(dgu, _rt3(N_DEV, T, FF_BLK)), _whole(wfi), _tile(dx2, row), _tile(x1, row), _tile(mix, row),
             _whole(g3), _whole(g2)],
        outs=[_out_tile((S, D_MODEL), F32, row), _out_tile((S, D_MODEL), BF16, row),
              _out_acc((1, D_MODEL)), _out_acc((1, D_MODEL))])


def _mixer_bwd_call(dmix, t1, t2, pg, w_out, w_bg, w_bs, *, name, T=ROW_TILE):
    S = dmix.shape[0]

    def body(dmix_ref, t1_ref, t2_ref, pg_ref, wo_ref, wbg_ref, wbs_ref, dt1_ref, dt2_ref, dpg_ref, dyg_ref, dys_ref):
        gg, gs = pl.ds(0, D_MODEL), pl.ds(D_MODEL, D_MODEL)
        for rows in _sub_tiles(T):
            dm = _dot(dmix_ref[rows, :], wo_ref[...], NT)
            sg = _sig(pg_ref[rows, gg].astype(F32))
            ss = _sig(pg_ref[rows, gs].astype(F32))
            dt1 = (dm * sg).astype(BF16)
            dt2 = (dm * ss).astype(BF16)
            dt1_ref[rows, :] = dt1
            dt2_ref[rows, :] = dt2
            dpg_ref[rows, gg] = (dm * t1_ref[rows, :].astype(F32) * sg * (1.0 - sg)).astype(BF16)
            dpg_ref[rows, gs] = (dm * t2_ref[rows, :].astype(F32) * ss * (1.0 - ss)).astype(BF16)
            dyg_ref[rows, :] = _dot(dt1, wbg_ref[...], NT).astype(BF16)
            dys_ref[rows, :] = _dot(dt2, wbs_ref[...], NT).astype(BF16)

    row = _rt(T, D_MODEL)
    wide = _rt(T, 2 * D_MODEL)
    b16 = lambda: _out_tile((S, D_MODEL), BF16, row)
    return _tok_call(
        body, name=name, S=S, T=T,
        ins=[_tile(dmix, row), _tile(t1, row), _tile(t2, row), _tile(pg, wide), _whole(w_out), _whole(w_bg), _whole(w_bs)],
        outs=[b16(), b16(), _out_tile((S, 2 * D_MODEL), BF16, wide), b16(), b16()])


def _in_proj_bwd_call(dpa, dlow, dps, dpg, w_in_t, x, dx1, g1, *, name, T=ROW_TILE):
    S = x.shape[0]

    def body(dpa_ref, dl_ref, dps_ref, dpg_ref, w_ref, x_ref, dx1_ref, g1_ref, gx_ref, dg1_ref):
        da = (_dot(dpa_ref[...], w_ref[A_COLS[0]:A_COLS[1], :], NN)
              + _dot(dl_ref[...], w_ref[LOW_COLS[0]:LOW_COLS[0] + LANES, :], NN)
              + _dot(dps_ref[...], w_ref[S_COLS[0]:S_COLS[1], :], NN)
              + _dot(dpg_ref[...], w_ref[G_COLS[0]:G_COLS[1], :], NN))
        r, xh = _rms_stats(x_ref[...])
        dxa, dg = _rms_bwd(xh, r, g1_ref[...], da)
        gx_ref[...] = dx1_ref[...] + dxa
        _accum(dg1_ref, dg)

    row = _rt(T, D_MODEL)
    return _tok_call(
        body, name=name, S=S, T=T, semantics="arbitrary",
        ins=[_tile(dpa, _rt(T, dpa.shape[1])), _tile(dlow, _rt(T, dlow.shape[1])), _tile(dps, _rt(T, dps.shape[1])),
             _tile(dpg, _rt(T, dpg.shape[1])), _whole(w_in_t), _tile(x, row), _tile(dx1, row), _whole(g1)],
        outs=[_out_tile((S, D_MODEL), F32, row), _out_acc((1, D_MODEL))])


TOKEN_TILE = 512


def _weight_grads_part(arrays, pairs, *, tk=TOKEN_TILE, out_dtype=BF16):
    S = arrays[0].shape[-2]
    tk = min(tk, S)
    n_in = len(arrays)

    def out_shape(i, j):
        a, b = arrays[i], arrays[j]
        if a.ndim == 3:
            return (a.shape[0], a.shape[2], b.shape[1])
        if b.ndim == 3:
            return (b.shape[0], a.shape[1], b.shape[2])
        return (a.shape[1], b.shape[1])

    shapes = [out_shape(i, j) for i, j in pairs]

    in_place = out_dtype == F32

    def body(ins, outs, accs):
        k = pl.program_id(0)
        if in_place:
            accs = outs

        @pl.when(k == 0)
        def _():
            for acc in accs:
                acc[...] = jnp.zeros_like(acc)

        for (i, j), acc in zip(pairs, accs):
            a_ref, b_ref = ins[i], ins[j]
            if len(a_ref.shape) == 3:
                for n in range(a_ref.shape[0]):
                    acc[n] += _dot(a_ref[n], b_ref[...], TN)
            elif len(b_ref.shape) == 3:
                a = a_ref[...]
                for n in range(b_ref.shape[0]):
                    acc[n] += _dot(a, b_ref[n], TN)
            else:
                b = b_ref[...]
                for m0 in range(0, a_ref.shape[1], 1024):
                    m1 = min(a_ref.shape[1], m0 + 1024)
                    acc[m0:m1, :] += _dot(a_ref[:, m0:m1], b, TN)

        if not in_place:
            @pl.when(k == S // tk - 1)
            def _():
                for out, acc in zip(outs, accs):
                    out[...] = acc[...].astype(out.dtype)

    def in_spec(a):
        if a.ndim == 3:
            return pl.BlockSpec((a.shape[0], tk, a.shape[2]), lambda k: (0, k, 0))
        return pl.BlockSpec((tk, a.shape[1]), lambda k: (k, 0))

    return dict(
        body=body, steps=S // tk, arrays=list(arrays),
        in_specs=[in_spec(a) for a in arrays],
        out_specs=[_acc(s) for s in shapes],
        out_shapes=[jax.ShapeDtypeStruct(s, out_dtype) for s in shapes],
        scratch=[] if in_place else [pltpu.VMEM(s, F32) for s in shapes],
        tiles=[(in_spec(a).block_shape, a.dtype) for a in arrays],
        resident=[(s, F32) for s in shapes] + ([] if in_place else [(s, BF16) for s in shapes]))


def _weight_grads_call(arrays, pairs, *, name, tk=TOKEN_TILE):
    part = _weight_grads_part(arrays, pairs, tk=tk)
    n_in, n_out = len(part["arrays"]), len(part["out_shapes"])

    def body(*refs):
        part["body"](refs[:n_in], refs[n_in:n_in + n_out], refs[n_in + n_out:])

    return pl.pallas_call(
        body, name=name, grid=(part["steps"],),
        in_specs=part["in_specs"], out_specs=part["out_specs"], out_shape=part["out_shapes"],
        scratch_shapes=part["scratch"],
        compiler_params=_cp(("arbitrary",), _vmem_limit(part["tiles"], part["resident"])),
    )(*part["arrays"])


def _with_rider(body, n_in, n_out, n_scratch, rider):
    if rider is None:
        return body
    r_in, r_out = len(rider["arrays"]), len(rider["out_shapes"])

    def both(*refs):
        ins, rest = refs[:n_in + r_in], refs[n_in + r_in:]
        outs, scratch = rest[:n_out + r_out], rest[n_out + r_out:]
        body(*ins[:n_in], *outs[:n_out], *scratch[:n_scratch])
        rider["body"](ins[n_in:], outs[n_out:], scratch[n_scratch:])

    return both


def _rider_lists(rider):
    if rider is None:
        return [], [], [], [], [], [], []
    return (rider["arrays"], rider["in_specs"], rider["out_specs"], rider["out_shapes"], rider["scratch"],
            rider["tiles"], rider["resident"])


def _ffn_in_grad_call(h, dgu, *, name, tk=TOKEN_TILE):
    S = h.shape[0]
    tk = min(tk, S)
    nb = 4
    shape = (nb, FF_BLK, D_MODEL)

    def body(h_ref, dgu_ref, out_ref, acc):
        k = pl.program_id(1)

        @pl.when(k == 0)
        def _():
            acc[...] = jnp.zeros_like(acc)

        hv = h_ref[...]
        for n in range(nb):
            acc[n] += _dot(dgu_ref[n], hv, TN)

        @pl.when(k == S // tk - 1)
        def _():
            out_ref[...] = acc[...].astype(out_ref.dtype)

    tiles = [((tk, D_MODEL), BF16), ((nb, tk, FF_BLK), BF16), (shape, BF16)]
    return pl.pallas_call(
        body, name=name, grid=(N_DEV // nb, S // tk),
        in_specs=[pl.BlockSpec((tk, D_MODEL), lambda g, k: (k, 0)),
                  pl.BlockSpec((nb, tk, FF_BLK), lambda g, k: (g, k, 0))],
        out_specs=pl.BlockSpec(shape, lambda g, k: (g, 0, 0)),
        out_shape=jax.ShapeDtypeStruct((N_DEV, FF_BLK, D_MODEL), BF16),
        scratch_shapes=[pltpu.VMEM(shape, F32)],
        compiler_params=_cp(("parallel", "arbitrary"), _vmem_limit(tiles, [(shape, F32)])),
    )(h, dgu)


GLA_TILE = 256
Q_OFF, K_OFF, V_OFF, R_OFF = 0, QK, 2 * QK, 2 * QK + GV


def _tri(lower):
    r = lax.broadcasted_iota(jnp.int32, (CHUNK, CHUNK), 0)
    c = lax.broadcasted_iota(jnp.int32, (CHUNK, CHUNK), 1)
    return jnp.where((r >= c) if lower else (c >= r), 1.0, 0.0).astype(BF16)


def _gla_fwd_call(pa, alow, wg, bgate, gnorm, *, name):
    S = pa.shape[0]
    Tg = min(GLA_TILE, S)
    cb = Tg // CHUNK

    def body(pa_ref, al_ref, wg_ref, bg_ref, gn_ref, y_ref, st_ref, state):
        @pl.when(pl.program_id(0) == 0)
        def _():
            state[...] = jnp.zeros_like(state)

        logit = _dot(al_ref[...], wg_ref[...], NN) + bg_ref[...]
        ls = _logsig(logit) * (1.0 / 16.0)
        tri = _tri(True)
        for c in range(cb):
            rows = pl.ds(c * CHUNK, CHUNK)
            cum = _exact_mask_dot(tri, ls[c * CHUNK:(c + 1) * CHUNK])
            tot = cum[CHUNK - 1:CHUNK]
            kd = (pa_ref[rows, pl.ds(K_OFF, QK)].astype(F32) * jnp.exp(tot - cum)).astype(BF16)
            decay = jnp.exp(tot)
            for h in range(HEADS):
                lanes = slice(h * DK, (h + 1) * DK)
                new = state[h] * decay[:, lanes] + _dot(pa_ref[rows, pl.ds(V_OFF + h * DV, DV)], kd[:, lanes], TN)
                state[h] = new
                st_ref[c, h] = new
        for c in range(cb):
            rows = pl.ds(c * CHUNK, CHUNK)
            for h in range(HEADS):
                qs = (pa_ref[rows, pl.ds(Q_OFF + h * DK, DK)].astype(F32) * Q_SCALE).astype(BF16)
                o = _dot(qs, st_ref[c, h].astype(BF16), NT)
                rs = lax.rsqrt(jnp.mean(o * o, axis=-1, keepdims=True) + EPS)
                rr = pa_ref[rows, pl.ds(R_OFF + h * DV, DV)].astype(F32)
                y_ref[rows, pl.ds(h * DV, DV)] = (o * rs * gn_ref[h] * (rr * _sig(rr))).astype(BF16)

    return pl.pallas_call(
        body, name=name, grid=(S // Tg,),
        in_specs=[
            pl.BlockSpec((Tg, 2 * QK + 2 * GV), lambda t: (t, 0)),
            pl.BlockSpec((Tg, LANES), lambda t: (t, 0)),
            pl.BlockSpec((LANES, QK), lambda t: (0, 0)),
            pl.BlockSpec((1, QK), lambda t: (0, 0)),
            pl.BlockSpec((HEADS, 1, DV), lambda t: (0, 0, 0)),
        ],
        out_specs=[
            pl.BlockSpec((Tg, GV), lambda t: (t, 0)),
            pl.BlockSpec((cb, HEADS, DV, DK), lambda t: (t, 0, 0, 0)),
        ],
        out_shape=[jax.ShapeDtypeStruct((S, GV), BF16),
                   jax.ShapeDtypeStruct((S // CHUNK, HEADS, DV, DK), F32)],
        scratch_shapes=[pltpu.VMEM((HEADS, DV, DK), F32)],
        compiler_params=_cp(("arbitrary",), VMEM_BIG),
    )(pa, alow, wg, bgate, gnorm)


def _gla_bwd_call(pa, alow, wg, bgate, gnorm, states, dy, *, name, rider=None):
    S = pa.shape[0]
    Tg = min(GLA_TILE, S)
    cb = Tg // CHUNK
    nt = S // Tg
    r_arrays, r_in, r_out, r_shapes, r_scratch, r_tiles, r_resident = _rider_lists(rider)
    assert rider is None or rider["steps"] == nt

    def rev(t):
        return nt - 1 - t

    def body(pa_ref, al_ref, wg_ref, bg_ref, gn_ref, st_ref, prev_ref, dy_ref,
             dpa_ref, dl_ref, dgn_ref, dbg_ref, carry, pbuf):
        t = pl.program_id(0)

        @pl.when(t == 0)
        def _():
            carry[...] = jnp.zeros_like(carry)
            dgn_ref[...] = jnp.zeros_like(dgn_ref)
            dbg_ref[...] = jnp.zeros_like(dbg_ref)

        logit = _dot(al_ref[...], wg_ref[...], NN) + bg_ref[...]
        ls = _logsig(logit) * (1.0 / 16.0)
        sneg = 1.0 / (1.0 + jnp.exp(logit))
        tri = _tri(True)
        upper = _tri(False)
        first_tile = rev(t) == 0
        heads = range(HEADS)

        w, decay, kd = [], [], []
        for c in range(cb):
            rows = pl.ds(c * CHUNK, CHUNK)
            cum = _exact_mask_dot(tri, ls[c * CHUNK:(c + 1) * CHUNK])
            tot = cum[CHUNK - 1:CHUNK]
            w.append(jnp.exp(tot - cum))
            decay.append(jnp.exp(tot))
            kd.append(pa_ref[rows, pl.ds(K_OFF, QK)].astype(F32) * w[c])

        dgn = [jnp.zeros((1, DV), F32) for _ in heads]
        for c in range(cb):
            rows = pl.ds(c * CHUNK, CHUNK)
            for h in heads:
                gn = gn_ref[h]
                qs = (pa_ref[rows, pl.ds(Q_OFF + h * DK, DK)].astype(F32) * Q_SCALE).astype(BF16)
                st16 = st_ref[c, h].astype(BF16)
                o = _dot(qs, st16, NT)
                rs = lax.rsqrt(jnp.mean(o * o, axis=-1, keepdims=True) + EPS)
                oh = o * rs
                rr = pa_ref[rows, pl.ds(R_OFF + h * DV, DV)].astype(F32)
                sr = _sig(rr)
                dyv = dy_ref[rows, pl.ds(h * DV, DV)].astype(F32)
                dpa_ref[rows, pl.ds(R_OFF + h * DV, DV)] = (
                    dyv * oh * gn * sr * (1.0 + rr * (1.0 - sr))).astype(BF16)
                don = dyv * (rr * sr)
                dgn[h] = dgn[h] + jnp.sum(don * oh, axis=0, keepdims=True)
                doh = don * gn
                do16 = (rs * (doh - oh * jnp.mean(doh * oh, axis=-1, keepdims=True))).astype(BF16)
                dpa_ref[rows, pl.ds(Q_OFF + h * DK, DK)] = (_dot(do16, st16, NN) * Q_SCALE).astype(BF16)
                pbuf[c, h] = _dot(do16, qs, TN)
        for h in heads:
            dgn_ref[h] += dgn[h]

        dkd = [[None] * HEADS for _ in range(cb)]
        ddecay = [[None] * HEADS for _ in range(cb)]
        for c in reversed(range(cb)):
            rows = pl.ds(c * CHUNK, CHUNK)
            for h in heads:
                lanes = slice(h * DK, (h + 1) * DK)
                gt = pbuf[c, h] + carry[h]
                gt16 = gt.astype(BF16)
                dkd[c][h] = _dot(pa_ref[rows, pl.ds(V_OFF + h * DV, DV)], gt16, NN)
                dpa_ref[rows, pl.ds(V_OFF + h * DV, DV)] = _dot(kd[c][:, lanes].astype(BF16), gt16, NT).astype(BF16)
                if c > 0:
                    st_prev = st_ref[c - 1, h]
                else:
                    st_prev = jnp.where(first_tile, 0.0, prev_ref[0, h])
                ddecay[c][h] = jnp.sum(gt * st_prev, axis=0, keepdims=True)
                carry[h] = gt * decay[c][:, lanes]

        dbg = jnp.zeros((1, QK), F32)
        for c in range(cb):
            rows = pl.ds(c * CHUNK, CHUNK)
            dkd_c = jnp.concatenate(dkd[c], axis=1)
            dpa_ref[rows, pl.ds(K_OFF, QK)] = (dkd_c * w[c]).astype(BF16)
            e = dkd_c * kd[c]
            dtot = jnp.sum(e, axis=0, keepdims=True) + jnp.concatenate(ddecay[c], axis=1) * decay[c]
            dls = dtot - _exact_mask_dot(upper, e)
            dlogit = dls * (1.0 / 16.0) * sneg[c * CHUNK:(c + 1) * CHUNK]
            dl_ref[rows, :] = dlogit.astype(BF16)
            dbg = dbg + jnp.sum(dlogit, axis=0, keepdims=True)
        dbg_ref[...] += dbg

    wide = 2 * QK + 2 * GV
    tiles = [((Tg, wide), BF16), ((cb + 1, HEADS, DV, DK), F32), ((Tg, GV), BF16), ((Tg, wide), BF16),
             ((Tg, QK), BF16)] + r_tiles
    resident = [((cb + 1, HEADS, DV, DK), F32)] + r_resident
    return pl.pallas_call(
        _with_rider(body, 8, 4, 2, rider), name=name, grid=(nt,),
        in_specs=[
            pl.BlockSpec((Tg, wide), lambda t: (rev(t), 0)),
            pl.BlockSpec((Tg, LANES), lambda t: (rev(t), 0)),
            pl.BlockSpec((LANES, QK), lambda t: (0, 0)),
            pl.BlockSpec((1, QK), lambda t: (0, 0)),
            pl.BlockSpec((HEADS, 1, DV), lambda t: (0, 0, 0)),
            pl.BlockSpec((cb, HEADS, DV, DK), lambda t: (rev(t), 0, 0, 0)),
            pl.BlockSpec((1, HEADS, DV, DK), lambda t: (jnp.maximum(rev(t) * cb - 1, 0), 0, 0, 0)),
            pl.BlockSpec((Tg, GV), lambda t: (rev(t), 0)),
        ] + r_in,
        out_specs=[
            pl.BlockSpec((Tg, wide), lambda t: (rev(t), 0)),
            pl.BlockSpec((Tg, QK), lambda t: (rev(t), 0)),
            pl.BlockSpec((HEADS, 1, DV), lambda t: (0, 0, 0)),
            pl.BlockSpec((1, QK), lambda t: (0, 0)),
        ] + r_out,
        out_shape=[jax.ShapeDtypeStruct((S, wide), BF16), jax.ShapeDtypeStruct((S, QK), BF16),
                   jax.ShapeDtypeStruct((HEADS, 1, DV), F32), jax.ShapeDtypeStruct((1, QK), F32)] + r_shapes,
        scratch_shapes=[pltpu.VMEM((HEADS, DV, DK), F32), pltpu.VMEM((cb, HEADS, DV, DK), F32)] + r_scratch,
        compiler_params=_cp(("arbitrary",), _vmem_limit(tiles, resident)),
    )(pa, alow, wg, bgate, gnorm, states, states, dy, *r_arrays)


SGU_TILE = 256


def _sgu_mask():
    r = lax.broadcasted_iota(jnp.int32, (SBLOCK, SBLOCK), 0)
    c = lax.broadcasted_iota(jnp.int32, (SBLOCK, SBLOCK), 1)
    return (c < CHUNK) | (r >= CHUNK)


def _ln_stats(vf):
    mu = jnp.mean(vf, axis=-1, keepdims=True)
    xc = vf - mu
    rs = lax.rsqrt(jnp.mean(xc * xc, axis=-1, keepdims=True) + EPS)
    return rs, xc * rs


def _sgu_fwd_call(ps, ln_g, ln_b, w_sp, b_sp, *, name):
    S = ps.shape[0]
    Ts = min(SGU_TILE, S)

    def body(ps_ref, lg_ref, lb_ref, w_ref, b_ref, y_ref):
        mask = _sgu_mask()
        for g in range(GROUPS):
            wm = jnp.where(mask, w_ref[g], 0.0).astype(BF16)
            for p in range(Ts // SBLOCK):
                rows = pl.ds(p * SBLOCK, SBLOCK)
                u = _gelu(ps_ref[rows, pl.ds(g * DG, DG)].astype(F32))
                _, xh = _ln_stats(_gelu(ps_ref[rows, pl.ds(D_MODEL + g * DG, DG)].astype(F32)))
                vn = xh * lg_ref[g] + lb_ref[g]
                mixed = _dot(wm, vn.astype(BF16), NN) + b_ref[g]
                y_ref[rows, pl.ds(g * DG, DG)] = (u * mixed).astype(BF16)

    full3 = lambda a, b, c: pl.BlockSpec((a, b, c), lambda t: (0, 0, 0))
    return pl.pallas_call(
        body, name=name, grid=(S // Ts,),
        in_specs=[pl.BlockSpec((Ts, 2 * D_MODEL), lambda t: (t, 0)),
                  full3(GROUPS, 1, DG), full3(GROUPS, 1, DG), full3(GROUPS, SBLOCK, SBLOCK), full3(GROUPS, SBLOCK, 1)],
        out_specs=pl.BlockSpec((Ts, D_MODEL), lambda t: (t, 0)),
        out_shape=jax.ShapeDtypeStruct((S, D_MODEL), BF16),
        compiler_params=_cp(("parallel",)),
    )(ps, ln_g, ln_b, w_sp, b_sp)


def _sgu_bwd_call(ps, ln_g, ln_b, w_sp, b_sp, dy, *, name, rider=None):
    S = ps.shape[0]
    Ts = min(SGU_TILE, S)
    r_arrays, r_in, r_out, r_shapes, r_scratch, r_tiles, r_resident = _rider_lists(rider)
    assert rider is None or rider["steps"] == S // Ts

    def body(ps_ref, lg_ref, lb_ref, w_ref, b_ref, dy_ref, ds_ref, dlg_ref, dlb_ref, dw_ref, db_ref):
        @pl.when(pl.program_id(0) == 0)
        def _():
            dlg_ref[...] = jnp.zeros_like(dlg_ref)
            dlb_ref[...] = jnp.zeros_like(dlb_ref)
            dw_ref[...] = jnp.zeros_like(dw_ref)
            db_ref[...] = jnp.zeros_like(db_ref)

        mask = _sgu_mask()
        for g in range(GROUPS):
            wm = jnp.where(mask, w_ref[g], 0.0).astype(BF16)
            lg = lg_ref[g]
            for p in range(Ts // SBLOCK):
                rows = pl.ds(p * SBLOCK, SBLOCK)
                su = ps_ref[rows, pl.ds(g * DG, DG)].astype(F32)
                sv = ps_ref[rows, pl.ds(D_MODEL + g * DG, DG)].astype(F32)
                u, du = _gelu_and_grad(su)
                gv, dgv = _gelu_and_grad(sv)
                rs, xh = _ln_stats(gv)
                vn16 = (xh * lg + lb_ref[g]).astype(BF16)
                mixed = _dot(wm, vn16, NN) + b_ref[g]
                dyv = dy_ref[rows, pl.ds(g * DG, DG)].astype(F32)
                ds_ref[rows, pl.ds(g * DG, DG)] = (dyv * mixed * du).astype(BF16)
                dmix = dyv * u
                dmix16 = dmix.astype(BF16)
                db_ref[g] += jnp.sum(dmix, axis=-1, keepdims=True)
                dw_ref[g] += jnp.where(mask, _dot(dmix16, vn16, NT), 0.0)
                dvn = _dot(wm, dmix16, TN)
                dlg_ref[g] += jnp.sum(dvn * xh, axis=0, keepdims=True)
                dlb_ref[g] += jnp.sum(dvn, axis=0, keepdims=True)
                dxh = dvn * lg
                dvf = rs * (dxh - jnp.mean(dxh, axis=-1, keepdims=True)
                            - xh * jnp.mean(dxh * xh, axis=-1, keepdims=True))
                ds_ref[rows, pl.ds(D_MODEL + g * DG, DG)] = (dvf * dgv).astype(BF16)

    full3 = lambda a, b, c: pl.BlockSpec((a, b, c), lambda t: (0, 0, 0))
    tiles = [((Ts, 2 * D_MODEL), BF16), ((Ts, D_MODEL), BF16), ((Ts, 2 * D_MODEL), BF16)] + r_tiles
    return pl.pallas_call(
        _with_rider(body, 6, 5, 0, rider), name=name, grid=(S // Ts,),
        in_specs=[pl.BlockSpec((Ts, 2 * D_MODEL), lambda t: (t, 0)),
                  full3(GROUPS, 1, DG), full3(GROUPS, 1, DG), full3(GROUPS, SBLOCK, SBLOCK), full3(GROUPS, SBLOCK, 1),
                  pl.BlockSpec((Ts, D_MODEL), lambda t: (t, 0))] + r_in,
        out_specs=[pl.BlockSpec((Ts, 2 * D_MODEL), lambda t: (t, 0)),
                   full3(GROUPS, 1, DG), full3(GROUPS, 1, DG), full3(GROUPS, SBLOCK, SBLOCK), full3(GROUPS, SBLOCK, 1)]
        + r_out,
        out_shape=[jax.ShapeDtypeStruct((S, 2 * D_MODEL), BF16),
                   jax.ShapeDtypeStruct((GROUPS, 1, DG), F32), jax.ShapeDtypeStruct((GROUPS, 1, DG), F32),
                   jax.ShapeDtypeStruct((GROUPS, SBLOCK, SBLOCK), F32),
                   jax.ShapeDtypeStruct((GROUPS, SBLOCK, 1), F32)] + r_shapes,
        scratch_shapes=r_scratch,
        compiler_params=_cp(("arbitrary",), _vmem_limit(tiles, r_resident)),
    )(ps, ln_g, ln_b, w_sp, b_sp, dy, *r_arrays)


def _position():
    return lax.axis_index("x"), lax.axis_index("y"), lax.axis_index("c")


def _gather_copies(srcs, dsts, send_sems, recv_sems, local_sems, relay=False):
    if relay:
        return _gather_copies_relayed(srcs, dsts, send_sems, recv_sems, local_sems)
    x, y, c = _position()
    me, sibling = (x, y, c), (x, y, 1 - c)
    chips = [(1 - x, y), (x, 1 - y), (1 - x, 1 - y)]
    n = len(srcs)

    def slab(a, block):
        px, py, pc = block
        return dsts[a].at[4 * px + 2 * py + pc]

    def copy(a, k, block, to, src=None):
        return pltpu.make_async_remote_copy(
            src_ref=slab(a, block) if src is None else src, dst_ref=slab(a, block),
            send_sem=send_sems.at[7 * a + k], recv_sem=recv_sems.at[7 * a + k], device_id=to, device_id_type=MESH)

    mine = [pltpu.make_async_copy(srcs[a], slab(a, me), local_sems.at[a]) for a in range(n)]
    for cp in mine:
        cp.start()
    first = []
    for a in range(n):
        first.append(copy(a, 0, me, sibling, src=srcs[a]))
        first += [copy(a, 1 + j, me, (*chip, c), src=srcs[a]) for j, chip in enumerate(chips)]
    for cp in first:
        cp.start()
    passed = []
    for j, chip in enumerate(chips):
        for a in range(n):
            copy(a, 1 + j, (*chip, c), me).wait_recv()
            cp = copy(a, 4 + j, (*chip, c), sibling)
            cp.start()
            passed.append(cp)
    for a in range(n):
        copy(a, 0, sibling, me).wait_recv()
        for j, chip in enumerate(chips):
            copy(a, 4 + j, (*chip, 1 - c), me).wait_recv()
    for cp in first + passed:
        cp.wait_send()
    for cp in mine:
        cp.wait()


def _gather_copies_relayed(srcs, dsts, send_sems, recv_sems, local_sems):
    x, y, c = _position()
    me, sibling = (x, y, c), (x, y, 1 - c)
    x_chip, y_chip, d_chip = (1 - x, y), (x, 1 - y), (1 - x, 1 - y)
    south = c == 0
    relay_from = (jnp.where(south, x, 1 - x), jnp.where(south, 1 - y, y), c)
    relay_to = (jnp.where(south, 1 - x, x), jnp.where(south, y, 1 - y), c)
    n = len(srcs)

    def slab(a, block):
        px, py, pc = block
        return dsts[a].at[4 * px + 2 * py + pc]

    def copy(a, k, block, to, src=None):
        return pltpu.make_async_remote_copy(
            src_ref=slab(a, block) if src is None else src, dst_ref=slab(a, block),
            send_sem=send_sems.at[7 * a + k], recv_sem=recv_sems.at[7 * a + k], device_id=to, device_id_type=MESH)

    mine = [pltpu.make_async_copy(srcs[a], slab(a, me), local_sems.at[a]) for a in range(n)]
    for cp in mine:
        cp.start()
    sent = []
    for a in range(n):
        sent += [copy(a, 0, me, sibling, src=srcs[a]), copy(a, 1, me, (*x_chip, c), src=srcs[a]),
                 copy(a, 2, me, (*y_chip, c), src=srcs[a])]
    for cp in sent:
        cp.start()
    for a in range(n):
        copy(a, 1, (*x_chip, c), me).wait_recv()
        copy(a, 2, (*y_chip, c), me).wait_recv()
        later = [copy(a, 3, relay_from, relay_to), copy(a, 4, (*x_chip, c), sibling), copy(a, 5, (*y_chip, c), sibling)]
        for cp in later:
            cp.start()
        sent += later
    for a in range(n):
        copy(a, 3, (*d_chip, c), me).wait_recv()
        cp = copy(a, 6, (*d_chip, c), sibling)
        cp.start()
        sent.append(cp)
    for a in range(n):
        copy(a, 0, sibling, me).wait_recv()
        for k, chip in ((4, x_chip), (5, y_chip), (6, d_chip)):
            copy(a, k, (*chip, 1 - c), me).wait_recv()
    for cp in sent:
        cp.wait_send()
    for cp in mine:
        cp.wait()


def _all_gather_hbm(shards, *, name):
    n = len(shards)

    def body(*refs):
        srcs, dsts = refs[:n], refs[n:2 * n]
        send_sems, recv_sems, local_sems = refs[2 * n:]
        _gather_copies(srcs, dsts, send_sems, recv_sems, local_sems, relay=True)

    return pl.pallas_call(
        body, name=name,
        in_specs=[ANY] * n, out_specs=[ANY] * n,
        out_shape=[jax.ShapeDtypeStruct((N_DEV, *s.shape), s.dtype) for s in shards],
        scratch_shapes=_comm_sems(n),
    )(*shards)


def _all_reduce_small(part, *, name):
    R, W = part.shape

    def body(x_ref, out_ref, gathered, send_sems, recv_sems, local_sems):
        _gather_copies([x_ref], [gathered], send_sems, recv_sems, local_sems)
        acc = gathered[0]
        for d in range(1, N_DEV):
            acc = acc + gathered[d]
        out_ref[...] = acc

    return pl.pallas_call(
        body, name=name,
        in_specs=[VMEM_SPEC], out_specs=VMEM_SPEC,
        out_shape=jax.ShapeDtypeStruct((R, W), F32),
        scratch_shapes=[pltpu.VMEM((N_DEV, R, W), F32),
                        pltpu.SemaphoreType.DMA((7,)), pltpu.SemaphoreType.DMA((7,)), pltpu.SemaphoreType.DMA((1,))],
    )(part)


FLIPS = [(fx, fy, fc) for fx in (0, 1) for fy in (0, 1) for fc in (0, 1)][1:]


def _scatter_copies(srcs, dsts, send_sems, recv_sems, local_sems, waves=1):
    n = len(srcs)
    x, y, c = _position()
    me = 4 * x + 2 * y + c
    mine = [pltpu.make_async_copy(srcs[a].at[me], dsts[a].at[me], local_sems.at[a]) for a in range(n)]
    for cp in mine:
        cp.start()
    peers = []
    for fx, fy, fc in FLIPS:
        tx = 1 - x if fx else x
        ty = 1 - y if fy else y
        tc = 1 - c if fc else c
        peers.append(((tx, ty, tc), 4 * tx + 2 * ty + tc))
    copies = []
    for w in range(waves):
        wave = []
        for k, (peer_id, peer) in enumerate(peers):
            for a in range(n):
                rows = srcs[a].shape[1]
                step = rows if waves == 1 else (rows // waves) // 16 * 16
                r0 = w * step
                cut = pl.ds(r0, step if w < waves - 1 else rows - r0)
                sem = (7 * a + k) * waves + w
                cp = pltpu.make_async_remote_copy(
                    src_ref=srcs[a].at[peer, cut], dst_ref=dsts[a].at[me, cut],
                    send_sem=send_sems.at[sem], recv_sem=recv_sems.at[sem],
                    device_id=peer_id, device_id_type=MESH)
                cp.start()
                wave.append(cp)
        for cp in wave:
            cp.wait_send()
        copies += wave
    for cp in copies:
        cp.wait_recv()
    for cp in mine:
        cp.wait()


def _comm_sems(n, waves=1):
    return [pltpu.SemaphoreType.DMA((7 * n * waves,)), pltpu.SemaphoreType.DMA((7 * n * waves,)),
            pltpu.SemaphoreType.DMA((n,))]


def _scatter_blocks(parts, *, name, waves=1):
    n = len(parts)

    def body(*refs):
        _scatter_copies(refs[:n], refs[n:2 * n], *refs[2 * n:], waves=waves)

    return pl.pallas_call(
        body, name=name,
        in_specs=[ANY] * n, out_specs=[ANY] * n,
        out_shape=[jax.ShapeDtypeStruct(p.shape, p.dtype) for p in parts],
        scratch_shapes=_comm_sems(n, waves),
    )(*parts)


def _handshake(peers):
    barrier = pltpu.get_barrier_semaphore()
    for peer in peers:
        pl.semaphore_signal(barrier, inc=1, device_id=peer, device_id_type=MESH)
    pl.semaphore_wait(barrier, len(peers))


def _sequencer_call(arrays, out_types, copies_fn, peers_fn, *, name, collective_id, waves=1):
    n = len(arrays)
    srcs = [jax.new_ref(a, memory_space=pltpu.MemorySpace.HBM) for a in arrays]
    dsts = [jax.empty_ref(t, memory_space=pltpu.MemorySpace.HBM) for t in out_types]
    extra = {} if waves == 1 else {"waves": waves}

    @pl.kernel(mesh=plsc.ScalarSubcoreMesh(axis_name="sequencer", num_cores=1), name=name,
               scratch_types=_comm_sems(n, waves), compiler_params=pltpu.CompilerParams(collective_id=collective_id))
    def launch(send_sems, recv_sems, local_sems):
        _handshake(peers_fn())
        copies_fn(srcs, dsts, send_sems, recv_sems, local_sems, **extra)

    launch()
    return [d[...] for d in dsts]


def _all_other_devices():
    x, y, c = _position()
    return [(1 - x if fx else x, 1 - y if fy else y, 1 - c if fc else c) for fx, fy, fc in FLIPS]


def _gather_relay_peers():
    x, y, c = _position()
    return [(x, y, 1 - c), (1 - x, y, c), (x, 1 - y, c)]


def _scatter_blocks_async(parts, *, name, collective_id, waves=1):
    return _sequencer_call(parts, [jax.ShapeDtypeStruct(p.shape, p.dtype) for p in parts],
                           _scatter_copies, _all_other_devices, name=name, collective_id=collective_id, waves=waves)


def _all_gather_async(shards, *, name, collective_id):
    return _sequencer_call(shards, [jax.ShapeDtypeStruct((N_DEV, *s.shape), s.dtype) for s in shards],
                           _gather_copies_relayed, _gather_relay_peers, name=name, collective_id=collective_id)


def _adamw_math(w, g, m, v):
    m = ADAM_B1 * m + (1.0 - ADAM_B1) * g
    v = ADAM_B2 * v + (1.0 - ADAM_B2) * (g * g)
    m_hat = m / (1.0 - ADAM_B1 ** ADAM_STEP)
    v_hat = v / (1.0 - ADAM_B2 ** ADAM_STEP)
    delta = -ADAM_LR * (m_hat / (jnp.sqrt(v_hat) + ADAM_EPS) + ADAM_WD * w)
    return delta, m, v


def _adamw_reduce_call(recv, w, m, v, *, name, T=128):
    R, W = w.shape
    if R % T == 0:
        tr, tw = T, W
    elif (R // 2) % 16 == 0:
        tr, tw = R // 2, W
    else:
        tr, tw = R, 2 * LANES

    def body(p_ref, w_ref, m_ref, v_ref, g_out, d_out, m_out, v_out):
        g = p_ref[0].astype(F32)
        for d in range(1, N_DEV):
            g = g + p_ref[d].astype(F32)
        g_out[...] = g
        d_out[...], m_out[...], v_out[...] = _adamw_math(w_ref[...], g, m_ref[...], v_ref[...])

    row = pl.BlockSpec((tr, tw), lambda i, j: (i, j))
    out = jax.ShapeDtypeStruct((R, W), F32)
    return pl.pallas_call(
        body, name=name, grid=(R // tr, W // tw),
        in_specs=[pl.BlockSpec((N_DEV, tr, tw), lambda i, j: (0, i, j)), row, row, row],
        out_specs=[row] * 4, out_shape=[out] * 4,
        compiler_params=_cp(("parallel", "parallel"), VMEM_BIG),
    )(recv, w, m, v)


def _adamw_small_call(w, g, m, v, *, name):
    def body(w_ref, g_ref, m_ref, v_ref, d_out, m_out, v_out):
        d_out[...], m_out[...], v_out[...] = _adamw_math(w_ref[...], g_ref[...], m_ref[...], v_ref[...])

    out = jax.ShapeDtypeStruct(w.shape, F32)
    return pl.pallas_call(body, name=name, in_specs=[VMEM_SPEC] * 4, out_specs=[VMEM_SPEC] * 3,
                          out_shape=[out] * 3)(w, g, m, v)


def _tile_rows(n_elems):
    return -(-n_elems // (8 * LANES)) * 8


def _pack_rows(parts, rows):
    pieces = []
    for p in parts:
        q = p.reshape(-1, LANES)
        pieces.append(jnp.pad(q, ((0, _tile_rows(p.size) - q.shape[0]), (0, 0))))
    buf = jnp.concatenate(pieces, axis=0)
    return jnp.pad(buf, ((0, rows - buf.shape[0]), (0, 0)))


def _unpack_rows(buf, shapes):
    out, off = [], 0
    for shp in shapes:
        n = 1
        for s in shp:
            n *= s
        out.append(buf[off:off + n // LANES].reshape(shp))
        off += _tile_rows(n)
    return out


def _in_w_blocks_call(a, dps, dpg, dw_a, dw_low, *, name, tk=TOKEN_TILE):
    S = a.shape[0]
    tk = min(tk, S)
    steps = S // tk
    n_s, n_g = dps.shape[1], dpg.shape[1]
    out_shape = (N_DEV, IN_BLK, D_MODEL)

    def body(a_ref, dps_ref, dpg_ref, dwa_ref, dwl_ref, out_ref, acc_s, acc_g):
        k = pl.program_id(0)

        @pl.when(k == 0)
        def _():
            acc_s[...] = jnp.zeros_like(acc_s)
            acc_g[...] = jnp.zeros_like(acc_g)

        av = a_ref[...]
        for src, acc, n in ((dps_ref, acc_s, n_s), (dpg_ref, acc_g, n_g)):
            for m0 in range(0, n, 1024):
                acc[m0:m0 + 1024, :] += _dot(src[:, m0:m0 + 1024], av, TN)

        @pl.when(k == steps - 1)
        def _():
            groups = ((dwa_ref, A_COLS), (dwl_ref, LOW_COLS), (acc_s, S_COLS), (acc_g, G_COLS))
            for d in range(N_DEV):
                lo, hi = IN_BLK * d, IN_BLK * (d + 1)
                for ref, (c0, c1) in groups:
                    s0, e0 = max(lo, c0), min(hi, c1)
                    if s0 < e0:
                        out_ref[d, s0 - lo:e0 - lo, :] = ref[s0 - c0:e0 - c0, :].astype(BF16)

    tiles = [((tk, D_MODEL), BF16), ((tk, n_s), BF16), ((tk, n_g), BF16)]
    resident = [(dw_a.shape, BF16), (dw_low.shape, BF16), (out_shape, BF16), ((n_s, D_MODEL), F32), ((n_g, D_MODEL), F32)]
    return pl.pallas_call(
        body, name=name, grid=(steps,),
        in_specs=[pl.BlockSpec((tk, D_MODEL), lambda k: (k, 0)), pl.BlockSpec((tk, n_s), lambda k: (k, 0)),
                  pl.BlockSpec((tk, n_g), lambda k: (k, 0)), _res(dw_a.shape), _res(dw_low.shape)],
        out_specs=_acc(out_shape),
        out_shape=jax.ShapeDtypeStruct(out_shape, BF16),
        scratch_shapes=[pltpu.VMEM((n_s, D_MODEL), F32), pltpu.VMEM((n_g, D_MODEL), F32)],
        compiler_params=_cp(("arbitrary",), _vmem_limit(tiles, resident)),
    )(a, dps, dpg, dw_a, dw_low)


def _local_step(x, target, W, scatter, finish):
    g1, g2, g3, g4 = [W[n].reshape(1, D_MODEL) for n in ("norm_pre_mix", "norm_post_mix", "norm_pre_ffn", "norm_post_ffn")]
    wfi, wfo = W["w_ffn_in"], W["w_ffn_out"].reshape(4, FF_BLK, D_MODEL)
    wg = jnp.pad(W["w_gate_up"], ((0, LANES - RANK), (0, 0))).astype(BF16)
    bgate = W["b_gate"].reshape(1, QK)
    gnorm = W["gla_norm"].reshape(HEADS, 1, DV)
    ln_g = W["sgu_ln_g"].reshape(GROUPS, 1, DG)
    ln_b = W["sgu_ln_b"].reshape(GROUPS, 1, DG)
    w_sp = W["w_spatial"]
    b_sp = W["b_spatial"].reshape(GROUPS, SBLOCK, 1)

    a, pa, alow, ps, pg, w_in_t = _in_proj_call(x, g1, W["w_in_blocks"], name="in_proj")
    y_gla, states = _gla_fwd_call(pa, alow, wg, bgate, gnorm, name="gla_fwd")
    y_sgu = _sgu_fwd_call(ps, ln_g, ln_b, w_sp, b_sp, name="sgu_fwd")
    t1, t2, merged, mix, x1, h = _mixer_tail_call(y_gla, y_sgu, pg, x, W["w_branch_gla"], W["w_branch_sgu"],
                                                  W["w_out"], g2, g3, name="mixer_tail")
    gu, z = _ffn_in_call(h, wfi, name="ffn_in")
    loss, dx2, dy, dg4 = _ffn_out_loss_call(z, wfo, x1, target, g4, name="ffn_out_loss")

    grads = {"norm_post_ffn": dg4}
    done = {}
    dgu = _ffn_out_bwd_call(dy, wfo, gu, name="ffn_out_bwd")
    dw_ffn_out = _weight_grads_call([z, dy], [(0, 1)], name="d_ffn_out_w")[0].reshape(N_DEV, D_FF // N_DEV, D_MODEL)
    dw_ffn_in = _ffn_in_grad_call(h, dgu, name="d_ffn_in_w")
    dgu, dw_ffn_out, dw_ffn_in = lax.optimization_barrier((dgu, dw_ffn_out, dw_ffn_in))
    ffn_received = scatter(("w_ffn_out", "w_ffn_in"), [dw_ffn_out, dw_ffn_in])
    dx1, dmix, grads["norm_pre_ffn"], grads["norm_post_mix"] = _ffn_in_bwd_call(dgu, wfi, dx2, x1, mix, g3, g2, name="ffn_in_bwd")
    dt1, dt2, dpg, dy_gla, dy_sgu = _mixer_bwd_call(dmix, t1, t2, pg, W["w_out"], W["w_branch_gla"], W["w_branch_sgu"],
                                                    name="mixer_bwd")
    rows = D_MODEL // N_DEV
    mixer_grads = _weight_grads_call([merged, dmix, y_gla, dt1, y_sgu, dt2], [(0, 1), (2, 3), (4, 5)], name="d_mixer_w")
    dy_gla, dy_sgu, mixer_grads = lax.optimization_barrier((dy_gla, dy_sgu, mixer_grads))
    mixer_received = scatter(("w_out", "w_branch_gla", "w_branch_sgu"),
                             [g.reshape(N_DEV, rows, D_MODEL) for g in mixer_grads])
    dpa, dlogit, dgn, dbg = _gla_bwd_call(pa, alow, wg, bgate, gnorm, states, dy_gla, name="gla_bwd")
    ffn_done, dpa, dlogit = lax.optimization_barrier((finish(ffn_received), dpa, dlogit))
    done.update(ffn_done)
    dps, dlg, dlb, dwsp, dbsp = _sgu_bwd_call(ps, ln_g, ln_b, w_sp, b_sp, dy_sgu, name="sgu_bwd")
    mixer_done, dps = lax.optimization_barrier((finish(mixer_received), dps))
    done.update(mixer_done)
    dlow = _mm(dlogit, wg, "nt", BF16, name="d_gate_up_x")
    dw_a, dw_low = _weight_grads_call([a, dpa, dlow], [(1, 0), (2, 0)], name="d_in_w_qkvr")
    dw_in = _in_w_blocks_call(a, dps, dpg, dw_a, dw_low, name="d_in_w_blocks")
    in_received = scatter(("w_in",), [dw_in])
    dwg = _mm(alow, dlogit, "tn", F32, name="d_gate_up_w")
    grad_x, grads["norm_pre_mix"] = _in_proj_bwd_call(dpa, dlow, dps, dpg, w_in_t, x, dx1, g1, name="in_proj_bwd")

    grads["w_gate_up"] = dwg[:RANK]
    grads["b_gate"] = dbg
    grads["gla_norm"] = dgn
    grads["sgu_ln_g"] = dlg
    grads["sgu_ln_b"] = dlb
    grads["w_spatial"] = dwsp
    grads["b_spatial"] = dbsp
    done.update(finish(in_received))
    return loss, grad_x, grads, done


WEIGHTS = ("norm_pre_mix", "w_in", "w_gate_up", "b_gate", "gla_norm", "sgu_ln_g", "sgu_ln_b", "w_spatial",
           "b_spatial", "w_branch_gla", "w_branch_sgu", "w_out", "norm_post_mix", "norm_pre_ffn", "w_ffn_in",
           "w_ffn_out", "norm_post_ffn")
BIG = ("w_in", "w_branch_gla", "w_branch_sgu", "w_out", "w_ffn_in", "w_ffn_out")
MIXER = ("w_branch_gla", "w_branch_sgu", "w_out")
FFN = ("w_ffn_in", "w_ffn_out")
COLUMN_SHARDED = ("w_in", "w_ffn_in")
LATE_SCATTER_WAVES = 4
SMALL = tuple(n for n in WEIGHTS if n not in BIG)
SMALL_SHARDED = ("w_gate_up", "gla_norm", "sgu_ln_g", "sgu_ln_b")
SMALL_FULL = {"norm_pre_mix": (1024,), "w_gate_up": (16, 512), "b_gate": (512,), "gla_norm": (4, 256),
              "sgu_ln_g": (4, 256), "sgu_ln_b": (4, 256), "w_spatial": (4, 128, 128), "b_spatial": (4, 128),
              "norm_post_mix": (1024,), "norm_pre_ffn": (1024,), "norm_post_ffn": (1024,)}
SMALL_GRAD_ROWS = 648
SMALL_STATE_ROWS = 600
SMALL_GATHER_ROWS = 32


def kernel(x, norm_pre_mix, w_in, w_gate_up, b_gate, gla_norm, sgu_ln_g, sgu_ln_b, w_spatial, b_spatial, w_branch_gla, w_branch_sgu, w_out, norm_post_mix, norm_pre_ffn, w_ffn_in, w_ffn_out, norm_post_ffn, loss_target, m_norm_pre_mix, m_w_in, m_w_gate_up, m_b_gate, m_gla_norm, m_sgu_ln_g, m_sgu_ln_b, m_w_spatial, m_b_spatial, m_w_branch_gla, m_w_branch_sgu, m_w_out, m_norm_post_mix, m_norm_pre_ffn, m_w_ffn_in, m_w_ffn_out, m_norm_post_ffn, v_norm_pre_mix, v_w_in, v_w_gate_up, v_b_gate, v_gla_norm, v_sgu_ln_g, v_sgu_ln_b, v_w_spatial, v_b_spatial, v_w_branch_gla, v_w_branch_sgu, v_w_out, v_norm_post_mix, v_norm_pre_ffn, v_w_ffn_in, v_w_ffn_out, v_norm_post_ffn):
    given = dict(locals())
    def local(a, n):
        return a[0].T if n in COLUMN_SHARDED else a[0]

    w = {n: local(given[n], n) for n in WEIGHTS}
    m = {n: local(given["m_" + n], n) for n in WEIGHTS}
    v = {n: local(given["v_" + n], n) for n in WEIGHTS}
    xs, target = x[0], loss_target[0]
    me = 4 * lax.axis_index("x") + 2 * lax.axis_index("y") + lax.axis_index("c")

    small_shard = _pack_rows([w[n] for n in SMALL_SHARDED], SMALL_GATHER_ROWS)
    first = _all_gather_hbm([w["w_in"].astype(BF16), small_shard], name="gather_w_in")
    rest, first = lax.optimization_barrier(([w[n].astype(BF16) for n in MIXER + FFN], first))
    rest_blocks = _all_gather_async(rest, name="gather_rest", collective_id=1)
    W = {n: w[n] for n in SMALL if n not in SMALL_SHARDED}
    blocks = dict(zip(MIXER + FFN, rest_blocks))
    for n in ("w_branch_gla", "w_branch_sgu", "w_out", "w_ffn_out"):
        W[n] = blocks[n].reshape(-1, D_MODEL)
    W["w_ffn_in"] = blocks["w_ffn_in"]
    W["w_in_blocks"] = first[0]
    small_blocks = first[1]
    off = 0
    for n in SMALL_SHARDED:
        r, c = w[n].shape
        blk = small_blocks[:, off:off + r * c // LANES].reshape(N_DEV, r, c)
        W[n] = blk.transpose(1, 0, 2).reshape(r, N_DEV * c)
        off += _tile_rows(r * c)

    scatter_ids = iter((3, 4, 5))

    def scatter(names, parts):
        waves = LATE_SCATTER_WAVES if "w_in" in names else 1
        got = _scatter_blocks_async(parts, name="scatter_" + "_".join(names), collective_id=next(scatter_ids),
                                    waves=waves)
        return dict(zip(names, got))

    def finish(received):
        return {n: _adamw_reduce_call(r, w[n], m[n], v[n], name="adamw_" + n) for n, r in received.items()}

    loss_part, grad_x, grads, big_done = _local_step(xs, target, W, scatter, finish)

    small_part = jnp.concatenate([_pack_rows([grads[n] for n in SMALL], SMALL_GRAD_ROWS),
                                  jnp.broadcast_to(loss_part, (8, LANES))], axis=0)
    small_sum = _all_reduce_small(small_part, name="reduce_small")
    loss = small_sum[SMALL_GRAD_ROWS, 0]
    g_small = dict(zip(SMALL, _unpack_rows(small_sum, [SMALL_FULL[n] for n in SMALL])))
    for n in SMALL_SHARDED:
        c = w[n].shape[1]
        g_small[n] = lax.dynamic_slice_in_dim(g_small[n], me * c, c, axis=1)

    g_big, d_big, m_big, v_big = [{n: big_done[n][i] for n in BIG} for i in range(4)]
    small_shapes = [w[n].shape for n in SMALL]
    small_state = [_pack_rows([s[n] for n in SMALL], SMALL_STATE_ROWS) for s in (w, g_small, m, v)]
    small_out = _adamw_small_call(*small_state, name="adamw_small")
    d_small, m_small, v_small = [dict(zip(SMALL, _unpack_rows(o, small_shapes))) for o in small_out]

    def pick(big, sml):
        return [(big[n].T if n in COLUMN_SHARDED else big[n] if n in BIG else sml[n])[None] for n in WEIGHTS]

    return (loss, grad_x[None], *pick(g_big, g_small), *pick(d_big, d_small), *pick(m_big, m_small),
            *pick(v_big, v_small))
```

```python
import jax
import jax.numpy as jnp
from jax import lax
from jax.experimental import pallas as pl
from jax.experimental.pallas import tpu as pltpu
from jax.experimental.pallas import tpu_sc as plsc

F32 = jnp.float32
BF16 = jnp.bfloat16

D_MODEL = 1024
N_DEV = 8
CHUNK = 64
HEADS = 4
DK = 128
DV = 256
QK = HEADS * DK
GV = HEADS * DV
RANK = 16
GROUPS = 4
SBLOCK = 128
DG = 256
D_FF = 2816
FF_BLK = 704
EPS = 1e-6
Q_SCALE = DK ** -0.5
LANES = 128
VMEM_BIG = 48 * 1024 * 1024

D_IN = 7184
IN_BLK = 898
A_COLS = (0, 3072)
LOW_COLS = (3072, 3088)
S_COLS = (3088, 5136)
G_COLS = (5136, 7184)

ADAM_LR = 0.001
ADAM_B1 = 0.9
ADAM_B2 = 0.999
ADAM_EPS = 1e-08
ADAM_WD = 0.01
ADAM_STEP = 10

MESH = pl.DeviceIdType.MESH
ANY = pl.BlockSpec(memory_space=pl.ANY)
VMEM_SPEC = pl.BlockSpec(memory_space=pltpu.VMEM)


def _cp(sem=None, vmem=None):
    return pltpu.CompilerParams(dimension_semantics=sem, vmem_limit_bytes=vmem)


def _sig(x):
    return 0.5 * jnp.tanh(0.5 * x) + 0.5


GELU_C = 0.7978845608028654
GELU_A = 0.044715


def _gelu(x):
    t = jnp.tanh((GELU_C * x) * (1.0 + GELU_A * (x * x)))
    return (0.5 * x) * (1.0 + t)


def _gelu_and_grad(x):
    x2 = x * x
    t = jnp.tanh((GELU_C * x) * (1.0 + GELU_A * x2))
    one_t = 1.0 + t
    hx = 0.5 * x
    grad = 0.5 * one_t + (hx * (1.0 - t * t)) * (GELU_C + (3.0 * GELU_A * GELU_C) * x2)
    return hx * one_t, grad


def _logsig(x):
    return jnp.minimum(x, 0.0) - jnp.log1p(jnp.exp(-jnp.abs(x)))


def _dot(a, b, dims):
    return lax.dot_general(a, b, (dims, ((), ())), preferred_element_type=F32)


NN = ((1,), (0,))
NT = ((1,), (1,))
TN = ((0,), (0,))


def _exact_mask_dot(mask_bf16, x):
    hi = x.astype(BF16)
    r1 = x - hi.astype(F32)
    mid = r1.astype(BF16)
    lo = (r1 - mid.astype(F32)).astype(BF16)
    return _dot(mask_bf16, hi, NN) + _dot(mask_bf16, mid, NN) + _dot(mask_bf16, lo, NN)


def _pick_tile(dim, target):
    if dim <= target:
        return dim
    best = None
    for t in range(LANES, int(1.4 * target) + 1, LANES):
        if dim % t == 0:
            best = t
    assert best is not None, (dim, target)
    return best


def _mm_call(a, b, *, name, grid, a_spec, b_spec, o_spec, out_shape, dims, acc_shape):
    nk = grid[2]

    def body(a_ref, b_ref, o_ref, *acc):
        part = _dot(a_ref[...], b_ref[...], dims)
        if nk == 1:
            o_ref[...] = part.astype(o_ref.dtype)
        else:
            acc_ref = acc[0]
            k = pl.program_id(2)

            @pl.when(k == 0)
            def _():
                acc_ref[...] = part

            @pl.when(k > 0)
            def _():
                acc_ref[...] += part

            @pl.when(k == nk - 1)
            def _():
                o_ref[...] = acc_ref[...].astype(o_ref.dtype)

    return pl.pallas_call(
        body, name=name, grid=grid, in_specs=[a_spec, b_spec], out_specs=o_spec, out_shape=out_shape,
        scratch_shapes=[] if nk == 1 else [pltpu.VMEM(acc_shape, F32)],
        compiler_params=_cp(("parallel", "parallel", "arbitrary"), VMEM_BIG),
    )(a, b)


def _mm(a, b, mode, out_dtype, *, name, tm=512, tn=1024, tk=1024):
    if mode == "nn":
        (M, K), (_, N) = a.shape, b.shape
    elif mode == "nt":
        (M, K), (N, _) = a.shape, b.shape
    else:
        (K, M), (_, N) = a.shape, b.shape
    tm, tn, tk = _pick_tile(M, tm), _pick_tile(N, tn), _pick_tile(K, tk)
    if mode == "nn":
        a_spec = pl.BlockSpec((tm, tk), lambda i, j, k: (i, k))
        b_spec = pl.BlockSpec((tk, tn), lambda i, j, k: (k, j))
        dims = NN
    elif mode == "nt":
        a_spec = pl.BlockSpec((tm, tk), lambda i, j, k: (i, k))
        b_spec = pl.BlockSpec((tn, tk), lambda i, j, k: (j, k))
        dims = NT
    else:
        a_spec = pl.BlockSpec((tk, tm), lambda i, j, k: (k, i))
        b_spec = pl.BlockSpec((tk, tn), lambda i, j, k: (k, j))
        dims = TN
    return _mm_call(a, b, name=name, grid=(M // tm, N // tn, K // tk), a_spec=a_spec, b_spec=b_spec,
                    o_spec=pl.BlockSpec((tm, tn), lambda i, j, k: (i, j)),
                    out_shape=jax.ShapeDtypeStruct((M, N), out_dtype), dims=dims, acc_shape=(tm, tn))


ROW_TILE = 512
SUB_ROWS = 256


def _sub_tiles(T):
    return [pl.ds(r0, min(SUB_ROWS, T)) for r0 in range(0, T, SUB_ROWS)]


def _rt(T, W, c=0):
    return pl.BlockSpec((T, W), lambda i: (i, c))


def _rt3(nb, T, W):
    return pl.BlockSpec((nb, T, W), lambda i: (0, i, 0))


def _res(shape):
    nd = len(shape)
    return pl.BlockSpec(tuple(shape), lambda i: (0,) * nd, pipeline_mode=pl.Buffered(1))


def _acc(shape):
    nd = len(shape)
    return pl.BlockSpec(tuple(shape), lambda i: (0,) * nd, pipeline_mode=pl.Buffered(1))


def _nbytes(shape, dtype):
    n = jnp.dtype(dtype).itemsize
    for s in shape:
        n *= s
    return n


def _vmem_limit(tiles, resident, temps=16 * 1024 * 1024):
    need = 2 * sum(_nbytes(s, d) for s, d in tiles) + sum(_nbytes(s, d) for s, d in resident) + temps
    return min(need, 60 * 1024 * 1024)


def _tok_call(body, *, name, S, T, ins, outs, semantics="parallel"):
    tiles = [(spec.block_shape, a.dtype) for a, spec, kind in ins + outs if kind == "tile"]
    resident = [(a.shape, a.dtype) for a, spec, kind in ins + outs if kind == "res"]
    return pl.pallas_call(
        body, name=name, grid=(S // T,),
        in_specs=[spec for _, spec, _ in ins], out_specs=[spec for _, spec, _ in outs],
        out_shape=[jax.ShapeDtypeStruct(a.shape, a.dtype) for a, _, _ in outs],
        compiler_params=_cp((semantics,), _vmem_limit(tiles, resident)),
    )(*[a for a, _, _ in ins])


def _tile(a, spec):
    return (a, spec, "tile")


def _whole(a):
    return (a, _res(a.shape), "res")


def _out_tile(shape, dtype, spec):
    return (jax.ShapeDtypeStruct(shape, dtype), spec, "tile")


def _out_acc(shape, dtype=F32):
    return (jax.ShapeDtypeStruct(shape, dtype), _acc(shape), "res")


def _rms_stats(x):
    r = lax.rsqrt(jnp.mean(x * x, axis=-1, keepdims=True) + EPS)
    return r, x * r


def _rms_bwd(xh, r, g, dy):
    dxh = dy * g
    dx = r * (dxh - xh * jnp.mean(dxh * xh, axis=-1, keepdims=True))
    dg = jnp.sum(dy * xh, axis=0, keepdims=True)
    return dx, dg


def _accum(ref, val):
    @pl.when(pl.program_id(0) == 0)
    def _():
        ref[...] = val

    @pl.when(pl.program_id(0) > 0)
    def _():
        ref[...] += val


def _dot_rows_t(a, w_ref, row0, o_ref, chunk=1024):
    n = o_ref.shape[1]
    for n0 in range(0, n, chunk):
        n1 = min(n, n0 + chunk)
        o_ref[:, n0:n1] = _dot(a, w_ref[row0 + n0:row0 + n1, :], NT).astype(o_ref.dtype)


def _in_proj_call(x, g1, w_blocks, *, name, T=ROW_TILE):
    S = x.shape[0]

    def body(x_ref, g_ref, blk_ref, a_ref, pa_ref, al_ref, ps_ref, pg_ref, w_ref):
        @pl.when(pl.program_id(0) == 0)
        def _():
            for d in range(N_DEV):
                w_ref[d * IN_BLK:(d + 1) * IN_BLK, :] = blk_ref[d]

        _, xh = _rms_stats(x_ref[...])
        a = (xh * g_ref[...]).astype(BF16)
        a_ref[...] = a
        _dot_rows_t(a, w_ref, A_COLS[0], pa_ref)
        _dot_rows_t(a, w_ref, LOW_COLS[0], al_ref)
        _dot_rows_t(a, w_ref, S_COLS[0], ps_ref)
        _dot_rows_t(a, w_ref, G_COLS[0], pg_ref)

    widths = (D_MODEL, A_COLS[1] - A_COLS[0], LANES, S_COLS[1] - S_COLS[0], G_COLS[1] - G_COLS[0])
    return _tok_call(
        body, name=name, S=S, T=T, semantics="arbitrary",
        ins=[_tile(x, _rt(T, D_MODEL)), _whole(g1), _whole(w_blocks)],
        outs=[_out_tile((S, w), BF16, _rt(T, w)) for w in widths] + [_out_acc((D_IN, D_MODEL), BF16)])


def _mixer_tail_call(y_gla, y_sgu, pg, x, w_bg, w_bs, w_out, g2, g3, *, name, T=ROW_TILE):
    S = x.shape[0]

    def body(yg_ref, ys_ref, pg_ref, x_ref, wbg_ref, wbs_ref, wo_ref, g2_ref, g3_ref,
             t1_ref, t2_ref, mg_ref, mix_ref, x1_ref, h_ref):
        for rows in _sub_tiles(T):
            t1 = _dot(yg_ref[rows, :], wbg_ref[...], NN)
            t2 = _dot(ys_ref[rows, :], wbs_ref[...], NN)
            t1_ref[rows, :] = t1.astype(BF16)
            t2_ref[rows, :] = t2.astype(BF16)
            sg = _sig(pg_ref[rows, pl.ds(0, D_MODEL)].astype(F32))
            ss = _sig(pg_ref[rows, pl.ds(D_MODEL, D_MODEL)].astype(F32))
            merged = (sg * t1 + ss * t2).astype(BF16)
            mg_ref[rows, :] = merged
            mix = _dot(merged, wo_ref[...], NN)
            mix_ref[rows, :] = mix
            _, mh = _rms_stats(mix)
            x1 = x_ref[rows, :] + mh * g2_ref[...]
            x1_ref[rows, :] = x1
            _, xh = _rms_stats(x1)
            h_ref[rows, :] = (xh * g3_ref[...]).astype(BF16)

    row = _rt(T, D_MODEL)
    b16 = lambda: _out_tile((S, D_MODEL), BF16, row)
    f32 = lambda: _out_tile((S, D_MODEL), F32, row)
    return _tok_call(
        body, name=name, S=S, T=T,
        ins=[_tile(y_gla, row), _tile(y_sgu, row), _tile(pg, _rt(T, 2 * D_MODEL)), _tile(x, row),
             _whole(w_bg), _whole(w_bs), _whole(w_out), _whole(g2), _whole(g3)],
        outs=[b16(), b16(), b16(), f32(), f32(), b16()])


def _ffn_in_call(h, wfi, *, name, T=ROW_TILE):
    S = h.shape[0]

    def body(h_ref, w_ref, gu_ref, z_ref):
        hv = h_ref[...]
        for d in range(4):
            gate = _dot(hv, w_ref[d], NT)
            up = _dot(hv, w_ref[d + 4], NT)
            gu_ref[d] = gate.astype(BF16)
            gu_ref[d + 4] = up.astype(BF16)
            z_ref[d] = (gate * _sig(gate) * up).astype(BF16)

    return _tok_call(
        body, name=name, S=S, T=T,
        ins=[_tile(h, _rt(T, D_MODEL)), _whole(wfi)],
        outs=[_out_tile((N_DEV, S, FF_BLK), BF16, _rt3(N_DEV, T, FF_BLK)),
              _out_tile((4, S, FF_BLK), BF16, _rt3(4, T, FF_BLK))])


def _ffn_out_loss_call(z, wfo, x1, target, g4, *, name, T=ROW_TILE):
    S = x1.shape[0]

    def body(z_ref, w_ref, x1_ref, t_ref, g4_ref, loss_ref, dx2_ref, dy_ref, dg4_ref):
        loss = jnp.zeros((1, 1), F32)
        dg4 = jnp.zeros((1, D_MODEL), F32)
        for rows in _sub_tiles(T):
            y = _dot(z_ref[0, rows, :], w_ref[0], NN)
            for d in range(1, 4):
                y = y + _dot(z_ref[d, rows, :], w_ref[d], NN)
            r, yh = _rms_stats(y)
            diff = x1_ref[rows, :] + yh * g4_ref[...] - t_ref[rows, :]
            loss = loss + 0.5 * jnp.sum(jnp.mean(diff * diff, axis=-1, keepdims=True), axis=0, keepdims=True)
            dx2 = diff * (1.0 / D_MODEL)
            dx2_ref[rows, :] = dx2
            dy, dg = _rms_bwd(yh, r, g4_ref[...], dx2)
            dy_ref[rows, :] = dy.astype(BF16)
            dg4 = dg4 + dg
        _accum(loss_ref, jnp.broadcast_to(loss, (1, LANES)))
        _accum(dg4_ref, dg4)

    row = _rt(T, D_MODEL)
    return _tok_call(
        body, name=name, S=S, T=T, semantics="arbitrary",
        ins=[_tile(z, _rt3(4, T, FF_BLK)), _whole(wfo), _tile(x1, row), _tile(target, row), _whole(g4)],
        outs=[_out_acc((1, LANES)), _out_tile((S, D_MODEL), F32, row), _out_tile((S, D_MODEL), BF16, row),
              _out_acc((1, D_MODEL))])


def _ffn_out_bwd_call(dy, wfo, gu, *, name, T=ROW_TILE):
    S = dy.shape[0]

    def body(dy_ref, w_ref, gu_ref, dgu_ref):
        dyv = dy_ref[...]
        for d in range(4):
            dz = _dot(dyv, w_ref[d], NT)
            gt = gu_ref[d].astype(F32)
            up = gu_ref[d + 4].astype(F32)
            s = _sig(gt)
            dgu_ref[d] = (dz * up * s * (1.0 + gt * (1.0 - s))).astype(BF16)
            dgu_ref[d + 4] = (dz * gt * s).astype(BF16)

    blocks = _rt3(N_DEV, T, FF_BLK)
    return _tok_call(
        body, name=name, S=S, T=T,
        ins=[_tile(dy, _rt(T, D_MODEL)), _whole(wfo), _tile(gu, blocks)],
        outs=[_out_tile((N_DEV, S, FF_BLK), BF16, blocks)])[0]


def _ffn_in_bwd_call(dgu, wfi, dx2, x1, mix, g3, g2, *, name, T=ROW_TILE):
    S = x1.shape[0]

    def body(dgu_ref, w_ref, dx2_ref, x1_ref, mix_ref, g3_ref, g2_ref, dx1_ref, dmix_ref, dg3_ref, dg2_ref):
        dg3 = jnp.zeros((1, D_MODEL), F32)
        dg2 = jnp.zeros((1, D_MODEL), F32)
        for rows in _sub_tiles(T):
            dh = _dot(dgu_ref[0, rows, :], w_ref[0], NN)
            for d in range(1, N_DEV):
                dh = dh + _dot(dgu_ref[d, rows, :], w_ref[d], NN)
            r3, xh = _rms_stats(x1_ref[rows, :])
            d3, g = _rms_bwd(xh, r3, g3_ref[...], dh)
            dg3 = dg3 + g
            dx1 = dx2_ref[rows, :] + d3
            dx1_ref[rows, :] = dx1
            r2, mh = _rms_stats(mix_ref[rows, :])
            dmix, g = _rms_bwd(mh, r2, g2_ref[...], dx1)
            dg2 = dg2 + g
            dmix_ref[rows, :] = dmix.astype(BF16)
        _accum(dg3_ref, dg3)
        _accum(dg2_ref, dg2)

    row = _rt(T, D_MODEL)
    return _tok_call(
        body, name=name, S=S, T=T, semantics="arbitrary",
        ins=[_tile(dgu, _rt3(N_DEV, T, FF_BLK)), _whole(wfi), _tile(dx2, row), _tile(x1, row), _tile(mix, row),
             _whole(g3), _whole(g2)],
        outs=[_out_tile((S, D_MODEL), F32, row), _out_tile((S, D_MODEL), BF16, row),
              _out_acc((1, D_MODEL)), _out_acc((1, D_MODEL))])


def _mixer_bwd_call(dmix, t1, t2, pg, w_out, w_bg, w_bs, *, name, T=ROW_TILE):
    S = dmix.shape[0]

    def body(dmix_ref, t1_ref, t2_ref, pg_ref, wo_ref, wbg_ref, wbs_ref, dt1_ref, dt2_ref, dpg_ref, dyg_ref, dys_ref):
        gg, gs = pl.ds(0, D_MODEL), pl.ds(D_MODEL, D_MODEL)
        for rows in _sub_tiles(T):
            dm = _dot(dmix_ref[rows, :], wo_ref[...], NT)
            sg = _sig(pg_ref[rows, gg].astype(F32))
            ss = _sig(pg_ref[rows, gs].astype(F32))
            dt1 = (dm * sg).astype(BF16)
            dt2 = (dm * ss).astype(BF16)
            dt1_ref[rows, :] = dt1
            dt2_ref[rows, :] = dt2
            dpg_ref[rows, gg] = (dm * t1_ref[rows, :].astype(F32) * sg * (1.0 - sg)).astype(BF16)
            dpg_ref[rows, gs] = (dm * t2_ref[rows, :].astype(F32) * ss * (1.0 - ss)).astype(BF16)
            dyg_ref[rows, :] = _dot(dt1, wbg_ref[...], NT).astype(BF16)
            dys_ref[rows, :] = _dot(dt2, wbs_ref[...], NT).astype(BF16)

    row = _rt(T, D_MODEL)
    wide = _rt(T, 2 * D_MODEL)
    b16 = lambda: _out_tile((S, D_MODEL), BF16, row)
    return _tok_call(
        body, name=name, S=S, T=T,
        ins=[_tile(dmix, row), _tile(t1, row), _tile(t2, row), _tile(pg, wide), _whole(w_out), _whole(w_bg), _whole(w_bs)],
        outs=[b16(), b16(), _out_tile((S, 2 * D_MODEL), BF16, wide), b16(), b16()])


def _in_proj_bwd_call(dpa, dlow, dps, dpg, w_in_t, x, dx1, g1, *, name, T=ROW_TILE):
    S = x.shape[0]

    def body(dpa_ref, dl_ref, dps_ref, dpg_ref, w_ref, x_ref, dx1_ref, g1_ref, gx_ref, dg1_ref):
        da = (_dot(dpa_ref[...], w_ref[A_COLS[0]:A_COLS[1], :], NN)
              + _dot(dl_ref[...], w_ref[LOW_COLS[0]:LOW_COLS[0] + LANES, :], NN)
              + _dot(dps_ref[...], w_ref[S_COLS[0]:S_COLS[1], :], NN)
              + _dot(dpg_ref[...], w_ref[G_COLS[0]:G_COLS[1], :], NN))
        r, xh = _rms_stats(x_ref[...])
        dxa, dg = _rms_bwd(xh, r, g1_ref[...], da)
        gx_ref[...] = dx1_ref[...] + dxa
        _accum(dg1_ref, dg)

    row = _rt(T, D_MODEL)
    return _tok_call(
        body, name=name, S=S, T=T, semantics="arbitrary",
        ins=[_tile(dpa, _rt(T, dpa.shape[1])), _tile(dlow, _rt(T, dlow.shape[1])), _tile(dps, _rt(T, dps.shape[1])),
             _tile(dpg, _rt(T, dpg.shape[1])), _whole(w_in_t), _tile(x, row), _tile(dx1, row), _whole(g1)],
        outs=[_out_tile((S, D_MODEL), F32, row), _out_acc((1, D_MODEL))])


TOKEN_TILE = 512


def _weight_grads_part(arrays, pairs, *, tk=TOKEN_TILE, out_dtype=BF16):
    S = arrays[0].shape[-2]
    tk = min(tk, S)
    n_in = len(arrays)

    def out_shape(i, j):
        a, b = arrays[i], arrays[j]
        if a.ndim == 3:
            return (a.shape[0], a.shape[2], b.shape[1])
        if b.ndim == 3:
            return (b.shape[0], a.shape[1], b.shape[2])
        return (a.shape[1], b.shape[1])

    shapes = [out_shape(i, j) for i, j in pairs]

    in_place = out_dtype == F32

    def body(ins, outs, accs):
        k = pl.program_id(0)
        if in_place:
            accs = outs

        @pl.when(k == 0)
        def _():
            for acc in accs:
                acc[...] = jnp.zeros_like(acc)

        for (i, j), acc in zip(pairs, accs):
            a_ref, b_ref = ins[i], ins[j]
            if len(a_ref.shape) == 3:
                for n in range(a_ref.shape[0]):
                    acc[n] += _dot(a_ref[n], b_ref[...], TN)
            elif len(b_ref.shape) == 3:
                a = a_ref[...]
                for n in range(b_ref.shape[0]):
                    acc[n] += _dot(a, b_ref[n], TN)
            else:
                b = b_ref[...]
                for m0 in range(0, a_ref.shape[1], 1024):
                    m1 = min(a_ref.shape[1], m0 + 1024)
                    acc[m0:m1, :] += _dot(a_ref[:, m0:m1], b, TN)

        if not in_place:
            @pl.when(k == S // tk - 1)
            def _():
                for out, acc in zip(outs, accs):
                    out[...] = acc[...].astype(out.dtype)

    def in_spec(a):
        if a.ndim == 3:
            return pl.BlockSpec((a.shape[0], tk, a.shape[2]), lambda k: (0, k, 0))
        return pl.BlockSpec((tk, a.shape[1]), lambda k: (k, 0))

    return dict(
        body=body, steps=S // tk, arrays=list(arrays),
        in_specs=[in_spec(a) for a in arrays],
        out_specs=[_acc(s) for s in shapes],
        out_shapes=[jax.ShapeDtypeStruct(s, out_dtype) for s in shapes],
        scratch=[] if in_place else [pltpu.VMEM(s, F32) for s in shapes],
        tiles=[(in_spec(a).block_shape, a.dtype) for a in arrays],
        resident=[(s, F32) for s in shapes] + ([] if in_place else [(s, BF16) for s in shapes]))


def _weight_grads_call(arrays, pairs, *, name, tk=TOKEN_TILE):
    part = _weight_grads_part(arrays, pairs, tk=tk)
    n_in, n_out = len(part["arrays"]), len(part["out_shapes"])

    def body(*refs):
        part["body"](refs[:n_in], refs[n_in:n_in + n_out], refs[n_in + n_out:])

    return pl.pallas_call(
        body, name=name, grid=(part["steps"],),
        in_specs=part["in_specs"], out_specs=part["out_specs"], out_shape=part["out_shapes"],
        scratch_shapes=part["scratch"],
        compiler_params=_cp(("arbitrary",), _vmem_limit(part["tiles"], part["resident"])),
    )(*part["arrays"])


def _with_rider(body, n_in, n_out, n_scratch, rider):
    if rider is None:
        return body
    r_in, r_out = len(rider["arrays"]), len(rider["out_shapes"])

    def both(*refs):
        ins, rest = refs[:n_in + r_in], refs[n_in + r_in:]
        outs, scratch = rest[:n_out + r_out], rest[n_out + r_out:]
        body(*ins[:n_in], *outs[:n_out], *scratch[:n_scratch])
        rider["body"](ins[n_in:], outs[n_out:], scratch[n_scratch:])

    return both


def _rider_lists(rider):
    if rider is None:
        return [], [], [], [], [], [], []
    return (rider["arrays"], rider["in_specs"], rider["out_specs"], rider["out_shapes"], rider["scratch"],
            rider["tiles"], rider["resident"])


def _ffn_in_grad_call(h, dgu, *, name, tk=TOKEN_TILE):
    S = h.shape[0]
    tk = min(tk, S)
    nb = 4
    shape = (nb, FF_BLK, D_MODEL)

    def body(h_ref, dgu_ref, out_ref, acc):
        k = pl.program_id(1)

        @pl.when(k == 0)
        def _():
            acc[...] = jnp.zeros_like(acc)

        hv = h_ref[...]
        for n in range(nb):
            acc[n] += _dot(dgu_ref[n], hv, TN)

        @pl.when(k == S // tk - 1)
        def _():
            out_ref[...] = acc[...].astype(out_ref.dtype)

    tiles = [((tk, D_MODEL), BF16), ((nb, tk, FF_BLK), BF16), (shape, BF16)]
    return pl.pallas_call(
        body, name=name, grid=(N_DEV // nb, S // tk),
        in_specs=[pl.BlockSpec((tk, D_MODEL), lambda g, k: (k, 0)),
                  pl.BlockSpec((nb, tk, FF_BLK), lambda g, k: (g, k, 0))],
        out_specs=pl.BlockSpec(shape, lambda g, k: (g, 0, 0)),
        out_shape=jax.ShapeDtypeStruct((N_DEV, FF_BLK, D_MODEL), BF16),
        scratch_shapes=[pltpu.VMEM(shape, F32)],
        compiler_params=_cp(("parallel", "arbitrary"), _vmem_limit(tiles, [(shape, F32)])),
    )(h, dgu)


GLA_TILE = 256
Q_OFF, K_OFF, V_OFF, R_OFF = 0, QK, 2 * QK, 2 * QK + GV


def _tri(lower):
    r = lax.broadcasted_iota(jnp.int32, (CHUNK, CHUNK), 0)
    c = lax.broadcasted_iota(jnp.int32, (CHUNK, CHUNK), 1)
    return jnp.where((r >= c) if lower else (c >= r), 1.0, 0.0).astype(BF16)


def _gla_fwd_call(pa, alow, wg, bgate, gnorm, *, name):
    S = pa.shape[0]
    Tg = min(GLA_TILE, S)
    cb = Tg // CHUNK

    def body(pa_ref, al_ref, wg_ref, bg_ref, gn_ref, y_ref, st_ref, state):
        @pl.when(pl.program_id(0) == 0)
        def _():
            state[...] = jnp.zeros_like(state)

        logit = _dot(al_ref[...], wg_ref[...], NN) + bg_ref[...]
        ls = _logsig(logit) * (1.0 / 16.0)
        tri = _tri(True)
        for c in range(cb):
            rows = pl.ds(c * CHUNK, CHUNK)
            cum = _exact_mask_dot(tri, ls[c * CHUNK:(c + 1) * CHUNK])
            tot = cum[CHUNK - 1:CHUNK]
            kd = (pa_ref[rows, pl.ds(K_OFF, QK)].astype(F32) * jnp.exp(tot - cum)).astype(BF16)
            decay = jnp.exp(tot)
            for h in range(HEADS):
                lanes = slice(h * DK, (h + 1) * DK)
                new = state[h] * decay[:, lanes] + _dot(pa_ref[rows, pl.ds(V_OFF + h * DV, DV)], kd[:, lanes], TN)
                state[h] = new
                st_ref[c, h] = new
        for c in range(cb):
            rows = pl.ds(c * CHUNK, CHUNK)
            for h in range(HEADS):
                qs = (pa_ref[rows, pl.ds(Q_OFF + h * DK, DK)].astype(F32) * Q_SCALE).astype(BF16)
                o = _dot(qs, st_ref[c, h].astype(BF16), NT)
                rs = lax.rsqrt(jnp.mean(o * o, axis=-1, keepdims=True) + EPS)
                rr = pa_ref[rows, pl.ds(R_OFF + h * DV, DV)].astype(F32)
                y_ref[rows, pl.ds(h * DV, DV)] = (o * rs * gn_ref[h] * (rr * _sig(rr))).astype(BF16)

    return pl.pallas_call(
        body, name=name, grid=(S // Tg,),
        in_specs=[
            pl.BlockSpec((Tg, 2 * QK + 2 * GV), lambda t: (t, 0)),
            pl.BlockSpec((Tg, LANES), lambda t: (t, 0)),
            pl.BlockSpec((LANES, QK), lambda t: (0, 0)),
            pl.BlockSpec((1, QK), lambda t: (0, 0)),
            pl.BlockSpec((HEADS, 1, DV), lambda t: (0, 0, 0)),
        ],
        out_specs=[
            pl.BlockSpec((Tg, GV), lambda t: (t, 0)),
            pl.BlockSpec((cb, HEADS, DV, DK), lambda t: (t, 0, 0, 0)),
        ],
        out_shape=[jax.ShapeDtypeStruct((S, GV), BF16),
                   jax.ShapeDtypeStruct((S // CHUNK, HEADS, DV, DK), F32)],
        scratch_shapes=[pltpu.VMEM((HEADS, DV, DK), F32)],
        compiler_params=_cp(("arbitrary",), VMEM_BIG),
    )(pa, alow, wg, bgate, gnorm)


def _gla_bwd_call(pa, alow, wg, bgate, gnorm, states, dy, *, name, rider=None):
    S = pa.shape[0]
    Tg = min(GLA_TILE, S)
    cb = Tg // CHUNK
    nt = S // Tg
    r_arrays, r_in, r_out, r_shapes, r_scratch, r_tiles, r_resident = _rider_lists(rider)
    assert rider is None or rider["steps"] == nt

    def rev(t):
        return nt - 1 - t

    def body(pa_ref, al_ref, wg_ref, bg_ref, gn_ref, st_ref, prev_ref, dy_ref,
             dpa_ref, dl_ref, dgn_ref, dbg_ref, carry, pbuf):
        t = pl.program_id(0)

        @pl.when(t == 0)
        def _():
            carry[...] = jnp.zeros_like(carry)
            dgn_ref[...] = jnp.zeros_like(dgn_ref)
            dbg_ref[...] = jnp.zeros_like(dbg_ref)

        logit = _dot(al_ref[...], wg_ref[...], NN) + bg_ref[...]
        ls = _logsig(logit) * (1.0 / 16.0)
        sneg = 1.0 / (1.0 + jnp.exp(logit))
        tri = _tri(True)
        upper = _tri(False)
        first_tile = rev(t) == 0
        heads = range(HEADS)

        w, decay, kd = [], [], []
        for c in range(cb):
            rows = pl.ds(c * CHUNK, CHUNK)
            cum = _exact_mask_dot(tri, ls[c * CHUNK:(c + 1) * CHUNK])
            tot = cum[CHUNK - 1:CHUNK]
            w.append(jnp.exp(tot - cum))
            decay.append(jnp.exp(tot))
            kd.append(pa_ref[rows, pl.ds(K_OFF, QK)].astype(F32) * w[c])

        dgn = [jnp.zeros((1, DV), F32) for _ in heads]
        for c in range(cb):
            rows = pl.ds(c * CHUNK, CHUNK)
            for h in heads:
                gn = gn_ref[h]
                qs = (pa_ref[rows, pl.ds(Q_OFF + h * DK, DK)].astype(F32) * Q_SCALE).astype(BF16)
                st16 = st_ref[c, h].astype(BF16)
                o = _dot(qs, st16, NT)
                rs = lax.rsqrt(jnp.mean(o * o, axis=-1, keepdims=True) + EPS)
                oh = o * rs
                rr = pa_ref[rows, pl.ds(R_OFF + h * DV, DV)].astype(F32)
                sr = _sig(rr)
                dyv = dy_ref[rows, pl.ds(h * DV, DV)].astype(F32)
                dpa_ref[rows, pl.ds(R_OFF + h * DV, DV)] = (
                    dyv * oh * gn * sr * (1.0 + rr * (1.0 - sr))).astype(BF16)
                don = dyv * (rr * sr)
                dgn[h] = dgn[h] + jnp.sum(don * oh, axis=0, keepdims=True)
                doh = don * gn
                do16 = (rs * (doh - oh * jnp.mean(doh * oh, axis=-1, keepdims=True))).astype(BF16)
                dpa_ref[rows, pl.ds(Q_OFF + h * DK, DK)] = (_dot(do16, st16, NN) * Q_SCALE).astype(BF16)
                pbuf[c, h] = _dot(do16, qs, TN)
        for h in heads:
            dgn_ref[h] += dgn[h]

        dkd = [[None] * HEADS for _ in range(cb)]
        ddecay = [[None] * HEADS for _ in range(cb)]
        for c in reversed(range(cb)):
            rows = pl.ds(c * CHUNK, CHUNK)
            for h in heads:
                lanes = slice(h * DK, (h + 1) * DK)
                gt = pbuf[c, h] + carry[h]
                gt16 = gt.astype(BF16)
                dkd[c][h] = _dot(pa_ref[rows, pl.ds(V_OFF + h * DV, DV)], gt16, NN)
                dpa_ref[rows, pl.ds(V_OFF + h * DV, DV)] = _dot(kd[c][:, lanes].astype(BF16), gt16, NT).astype(BF16)
                if c > 0:
                    st_prev = st_ref[c - 1, h]
                else:
                    st_prev = jnp.where(first_tile, 0.0, prev_ref[0, h])
                ddecay[c][h] = jnp.sum(gt * st_prev, axis=0, keepdims=True)
                carry[h] = gt * decay[c][:, lanes]

        dbg = jnp.zeros((1, QK), F32)
        for c in range(cb):
            rows = pl.ds(c * CHUNK, CHUNK)
            dkd_c = jnp.concatenate(dkd[c], axis=1)
            dpa_ref[rows, pl.ds(K_OFF, QK)] = (dkd_c * w[c]).astype(BF16)
            e = dkd_c * kd[c]
            dtot = jnp.sum(e, axis=0, keepdims=True) + jnp.concatenate(ddecay[c], axis=1) * decay[c]
            dls = dtot - _exact_mask_dot(upper, e)
            dlogit = dls * (1.0 / 16.0) * sneg[c * CHUNK:(c + 1) * CHUNK]
            dl_ref[rows, :] = dlogit.astype(BF16)
            dbg = dbg + jnp.sum(dlogit, axis=0, keepdims=True)
        dbg_ref[...] += dbg

    wide = 2 * QK + 2 * GV
    tiles = [((Tg, wide), BF16), ((cb + 1, HEADS, DV, DK), F32), ((Tg, GV), BF16), ((Tg, wide), BF16),
             ((Tg, QK), BF16)] + r_tiles
    resident = [((cb + 1, HEADS, DV, DK), F32)] + r_resident
    return pl.pallas_call(
        _with_rider(body, 8, 4, 2, rider), name=name, grid=(nt,),
        in_specs=[
            pl.BlockSpec((Tg, wide), lambda t: (rev(t), 0)),
            pl.BlockSpec((Tg, LANES), lambda t: (rev(t), 0)),
            pl.BlockSpec((LANES, QK), lambda t: (0, 0)),
            pl.BlockSpec((1, QK), lambda t: (0, 0)),
            pl.BlockSpec((HEADS, 1, DV), lambda t: (0, 0, 0)),
            pl.BlockSpec((cb, HEADS, DV, DK), lambda t: (rev(t), 0, 0, 0)),
            pl.BlockSpec((1, HEADS, DV, DK), lambda t: (jnp.maximum(rev(t) * cb - 1, 0), 0, 0, 0)),
            pl.BlockSpec((Tg, GV), lambda t: (rev(t), 0)),
        ] + r_in,
        out_specs=[
            pl.BlockSpec((Tg, wide), lambda t: (rev(t), 0)),
            pl.BlockSpec((Tg, QK), lambda t: (rev(t), 0)),
            pl.BlockSpec((HEADS, 1, DV), lambda t: (0, 0, 0)),
            pl.BlockSpec((1, QK), lambda t: (0, 0)),
        ] + r_out,
        out_shape=[jax.ShapeDtypeStruct((S, wide), BF16), jax.ShapeDtypeStruct((S, QK), BF16),
                   jax.ShapeDtypeStruct((HEADS, 1, DV), F32), jax.ShapeDtypeStruct((1, QK), F32)] + r_shapes,
        scratch_shapes=[pltpu.VMEM((HEADS, DV, DK), F32), pltpu.VMEM((cb, HEADS, DV, DK), F32)] + r_scratch,
        compiler_params=_cp(("arbitrary",), _vmem_limit(tiles, resident)),
    )(pa, alow, wg, bgate, gnorm, states, states, dy, *r_arrays)


SGU_TILE = 256


def _sgu_mask():
    r = lax.broadcasted_iota(jnp.int32, (SBLOCK, SBLOCK), 0)
    c = lax.broadcasted_iota(jnp.int32, (SBLOCK, SBLOCK), 1)
    return (c < CHUNK) | (r >= CHUNK)


def _ln_stats(vf):
    mu = jnp.mean(vf, axis=-1, keepdims=True)
    xc = vf - mu
    rs = lax.rsqrt(jnp.mean(xc * xc, axis=-1, keepdims=True) + EPS)
    return rs, xc * rs


def _sgu_fwd_call(ps, ln_g, ln_b, w_sp, b_sp, *, name):
    S = ps.shape[0]
    Ts = min(SGU_TILE, S)

    def body(ps_ref, lg_ref, lb_ref, w_ref, b_ref, y_ref):
        mask = _sgu_mask()
        for g in range(GROUPS):
            wm = jnp.where(mask, w_ref[g], 0.0).astype(BF16)
            for p in range(Ts // SBLOCK):
                rows = pl.ds(p * SBLOCK, SBLOCK)
                u = _gelu(ps_ref[rows, pl.ds(g * DG, DG)].astype(F32))
                _, xh = _ln_stats(_gelu(ps_ref[rows, pl.ds(D_MODEL + g * DG, DG)].astype(F32)))
                vn = xh * lg_ref[g] + lb_ref[g]
                mixed = _dot(wm, vn.astype(BF16), NN) + b_ref[g]
                y_ref[rows, pl.ds(g * DG, DG)] = (u * mixed).astype(BF16)

    full3 = lambda a, b, c: pl.BlockSpec((a, b, c), lambda t: (0, 0, 0))
    return pl.pallas_call(
        body, name=name, grid=(S // Ts,),
        in_specs=[pl.BlockSpec((Ts, 2 * D_MODEL), lambda t: (t, 0)),
                  full3(GROUPS, 1, DG), full3(GROUPS, 1, DG), full3(GROUPS, SBLOCK, SBLOCK), full3(GROUPS, SBLOCK, 1)],
        out_specs=pl.BlockSpec((Ts, D_MODEL), lambda t: (t, 0)),
        out_shape=jax.ShapeDtypeStruct((S, D_MODEL), BF16),
        compiler_params=_cp(("parallel",)),
    )(ps, ln_g, ln_b, w_sp, b_sp)


def _sgu_bwd_call(ps, ln_g, ln_b, w_sp, b_sp, dy, *, name, rider=None):
    S = ps.shape[0]
    Ts = min(SGU_TILE, S)
    r_arrays, r_in, r_out, r_shapes, r_scratch, r_tiles, r_resident = _rider_lists(rider)
    assert rider is None or rider["steps"] == S // Ts

    def body(ps_ref, lg_ref, lb_ref, w_ref, b_ref, dy_ref, ds_ref, dlg_ref, dlb_ref, dw_ref, db_ref):
        @pl.when(pl.program_id(0) == 0)
        def _():
            dlg_ref[...] = jnp.zeros_like(dlg_ref)
            dlb_ref[...] = jnp.zeros_like(dlb_ref)
            dw_ref[...] = jnp.zeros_like(dw_ref)
            db_ref[...] = jnp.zeros_like(db_ref)

        mask = _sgu_mask()
        for g in range(GROUPS):
            wm = jnp.where(mask, w_ref[g], 0.0).astype(BF16)
            lg = lg_ref[g]
            for p in range(Ts // SBLOCK):
                rows = pl.ds(p * SBLOCK, SBLOCK)
                su = ps_ref[rows, pl.ds(g * DG, DG)].astype(F32)
                sv = ps_ref[rows, pl.ds(D_MODEL + g * DG, DG)].astype(F32)
                u, du = _gelu_and_grad(su)
                gv, dgv = _gelu_and_grad(sv)
                rs, xh = _ln_stats(gv)
                vn16 = (xh * lg + lb_ref[g]).astype(BF16)
                mixed = _dot(wm, vn16, NN) + b_ref[g]
                dyv = dy_ref[rows, pl.ds(g * DG, DG)].astype(F32)
                ds_ref[rows, pl.ds(g * DG, DG)] = (dyv * mixed * du).astype(BF16)
                dmix = dyv * u
                dmix16 = dmix.astype(BF16)
                db_ref[g] += jnp.sum(dmix, axis=-1, keepdims=True)
                dw_ref[g] += jnp.where(mask, _dot(dmix16, vn16, NT), 0.0)
                dvn = _dot(wm, dmix16, TN)
                dlg_ref[g] += jnp.sum(dvn * xh, axis=0, keepdims=True)
                dlb_ref[g] += jnp.sum(dvn, axis=0, keepdims=True)
                dxh = dvn * lg
                dvf = rs * (dxh - jnp.mean(dxh, axis=-1, keepdims=True)
                            - xh * jnp.mean(dxh * xh, axis=-1, keepdims=True))
                ds_ref[rows, pl.ds(D_MODEL + g * DG, DG)] = (dvf * dgv).astype(BF16)

    full3 = lambda a, b, c: pl.BlockSpec((a, b, c), lambda t: (0, 0, 0))
    tiles = [((Ts, 2 * D_MODEL), BF16), ((Ts, D_MODEL), BF16), ((Ts, 2 * D_MODEL), BF16)] + r_tiles
    return pl.pallas_call(
        _with_rider(body, 6, 5, 0, rider), name=name, grid=(S // Ts,),
        in_specs=[pl.BlockSpec((Ts, 2 * D_MODEL), lambda t: (t, 0)),
                  full3(GROUPS, 1, DG), full3(GROUPS, 1, DG), full3(GROUPS, SBLOCK, SBLOCK), full3(GROUPS, SBLOCK, 1),
                  pl.BlockSpec((Ts, D_MODEL), lambda t: (t, 0))] + r_in,
        out_specs=[pl.BlockSpec((Ts, 2 * D_MODEL), lambda t: (t, 0)),
                   full3(GROUPS, 1, DG), full3(GROUPS, 1, DG), full3(GROUPS, SBLOCK, SBLOCK), full3(GROUPS, SBLOCK, 1)]
        + r_out,
        out_shape=[jax.ShapeDtypeStruct((S, 2 * D_MODEL), BF16),
                   jax.ShapeDtypeStruct((GROUPS, 1, DG), F32), jax.ShapeDtypeStruct((GROUPS, 1, DG), F32),
                   jax.ShapeDtypeStruct((GROUPS, SBLOCK, SBLOCK), F32),
                   jax.ShapeDtypeStruct((GROUPS, SBLOCK, 1), F32)] + r_shapes,
        scratch_shapes=r_scratch,
        compiler_params=_cp(("arbitrary",), _vmem_limit(tiles, r_resident)),
    )(ps, ln_g, ln_b, w_sp, b_sp, dy, *r_arrays)


def _position():
    return lax.axis_index("x"), lax.axis_index("y"), lax.axis_index("c")


def _gather_copies(srcs, dsts, send_sems, recv_sems, local_sems, relay=False):
    if relay:
        return _gather_copies_relayed(srcs, dsts, send_sems, recv_sems, local_sems)
    x, y, c = _position()
    me, sibling = (x, y, c), (x, y, 1 - c)
    chips = [(1 - x, y), (x, 1 - y), (1 - x, 1 - y)]
    n = len(srcs)

    def slab(a, block):
        px, py, pc = block
        return dsts[a].at[4 * px + 2 * py + pc]

    def copy(a, k, block, to, src=None):
        return pltpu.make_async_remote_copy(
            src_ref=slab(a, block) if src is None else src, dst_ref=slab(a, block),
            send_sem=send_sems.at[7 * a + k], recv_sem=recv_sems.at[7 * a + k], device_id=to, device_id_type=MESH)

    mine = [pltpu.make_async_copy(srcs[a], slab(a, me), local_sems.at[a]) for a in range(n)]
    for cp in mine:
        cp.start()
    first = []
    for a in range(n):
        first.append(copy(a, 0, me, sibling, src=srcs[a]))
        first += [copy(a, 1 + j, me, (*chip, c), src=srcs[a]) for j, chip in enumerate(chips)]
    for cp in first:
        cp.start()
    passed = []
    for j, chip in enumerate(chips):
        for a in range(n):
            copy(a, 1 + j, (*chip, c), me).wait_recv()
            cp = copy(a, 4 + j, (*chip, c), sibling)
            cp.start()
            passed.append(cp)
    for a in range(n):
        copy(a, 0, sibling, me).wait_recv()
        for j, chip in enumerate(chips):
            copy(a, 4 + j, (*chip, 1 - c), me).wait_recv()
    for cp in first + passed:
        cp.wait_send()
    for cp in mine:
        cp.wait()


def _gather_copies_relayed(srcs, dsts, send_sems, recv_sems, local_sems):
    x, y, c = _position()
    me, sibling = (x, y, c), (x, y, 1 - c)
    x_chip, y_chip, d_chip = (1 - x, y), (x, 1 - y), (1 - x, 1 - y)
    south = c == 0
    relay_from = (jnp.where(south, x, 1 - x), jnp.where(south, 1 - y, y), c)
    relay_to = (jnp.where(south, 1 - x, x), jnp.where(south, y, 1 - y), c)
    n = len(srcs)

    def slab(a, block):
        px, py, pc = block
        return dsts[a].at[4 * px + 2 * py + pc]

    def copy(a, k, block, to, src=None):
        return pltpu.make_async_remote_copy(
            src_ref=slab(a, block) if src is None else src, dst_ref=slab(a, block),
            send_sem=send_sems.at[7 * a + k], recv_sem=recv_sems.at[7 * a + k], device_id=to, device_id_type=MESH)

    mine = [pltpu.make_async_copy(srcs[a], slab(a, me), local_sems.at[a]) for a in range(n)]
    for cp in mine:
        cp.start()
    sent = []
    for a in range(n):
        sent += [copy(a, 0, me, sibling, src=srcs[a]), copy(a, 1, me, (*x_chip, c), src=srcs[a]),
                 copy(a, 2, me, (*y_chip, c), src=srcs[a])]
    for cp in sent:
        cp.start()
    for a in range(n):
        copy(a, 1, (*x_chip, c), me).wait_recv()
        copy(a, 2, (*y_chip, c), me).wait_recv()
        later = [copy(a, 3, relay_from, relay_to), copy(a, 4, (*x_chip, c), sibling), copy(a, 5, (*y_chip, c), sibling)]
        for cp in later:
            cp.start()
        sent += later
    for a in range(n):
        copy(a, 3, (*d_chip, c), me).wait_recv()
        cp = copy(a, 6, (*d_chip, c), sibling)
        cp.start()
        sent.append(cp)
    for a in range(n):
        copy(a, 0, sibling, me).wait_recv()
        for k, chip in ((4, x_chip), (5, y_chip), (6, d_chip)):
            copy(a, k, (*chip, 1 - c), me).wait_recv()
    for cp in sent:
        cp.wait_send()
    for cp in mine:
        cp.wait()


def _all_gather_hbm(shards, *, name):
    n = len(shards)

    def body(*refs):
        srcs, dsts = refs[:n], refs[n:2 * n]
        send_sems, recv_sems, local_sems = refs[2 * n:]
        _gather_copies(srcs, dsts, send_sems, recv_sems, local_sems, relay=True)

    return pl.pallas_call(
        body, name=name,
        in_specs=[ANY] * n, out_specs=[ANY] * n,
        out_shape=[jax.ShapeDtypeStruct((N_DEV, *s.shape), s.dtype) for s in shards],
        scratch_shapes=_comm_sems(n),
    )(*shards)


def _all_reduce_small(part, *, name):
    R, W = part.shape

    def body(x_ref, out_ref, gathered, send_sems, recv_sems, local_sems):
        _gather_copies([x_ref], [gathered], send_sems, recv_sems, local_sems)
        acc = gathered[0]
        for d in range(1, N_DEV):
            acc = acc + gathered[d]
        out_ref[...] = acc

    return pl.pallas_call(
        body, name=name,
        in_specs=[VMEM_SPEC], out_specs=VMEM_SPEC,
        out_shape=jax.ShapeDtypeStruct((R, W), F32),
        scratch_shapes=[pltpu.VMEM((N_DEV, R, W), F32),
                        pltpu.SemaphoreType.DMA((7,)), pltpu.SemaphoreType.DMA((7,)), pltpu.SemaphoreType.DMA((1,))],
    )(part)


FLIPS = [(fx, fy, fc) for fx in (0, 1) for fy in (0, 1) for fc in (0, 1)][1:]


def _scatter_copies(srcs, dsts, send_sems, recv_sems, local_sems, waves=1):
    n = len(srcs)
    x, y, c = _position()
    me = 4 * x + 2 * y + c
    mine = [pltpu.make_async_copy(srcs[a].at[me], dsts[a].at[me], local_sems.at[a]) for a in range(n)]
    for cp in mine:
        cp.start()
    peers = []
    for fx, fy, fc in FLIPS:
        tx = 1 - x if fx else x
        ty = 1 - y if fy else y
        tc = 1 - c if fc else c
        peers.append(((tx, ty, tc), 4 * tx + 2 * ty + tc))
    copies = []
    for w in range(waves):
        wave = []
        for k, (peer_id, peer) in enumerate(peers):
            for a in range(n):
                rows = srcs[a].shape[1]
                step = rows if waves == 1 else (rows // waves) // 16 * 16
                r0 = w * step
                cut = pl.ds(r0, step if w < waves - 1 else rows - r0)
                sem = (7 * a + k) * waves + w
                cp = pltpu.make_async_remote_copy(
                    src_ref=srcs[a].at[peer, cut], dst_ref=dsts[a].at[me, cut],
                    send_sem=send_sems.at[sem], recv_sem=recv_sems.at[sem],
                    device_id=peer_id, device_id_type=MESH)
                cp.start()
                wave.append(cp)
        for cp in wave:
            cp.wait_send()
        copies += wave
    for cp in copies:
        cp.wait_recv()
    for cp in mine:
        cp.wait()


def _comm_sems(n, waves=1):
    return [pltpu.SemaphoreType.DMA((7 * n * waves,)), pltpu.SemaphoreType.DMA((7 * n * waves,)),
            pltpu.SemaphoreType.DMA((n,))]


def _scatter_blocks(parts, *, name, waves=1):
    n = len(parts)

    def body(*refs):
        _scatter_copies(refs[:n], refs[n:2 * n], *refs[2 * n:], waves=waves)

    return pl.pallas_call(
        body, name=name,
        in_specs=[ANY] * n, out_specs=[ANY] * n,
        out_shape=[jax.ShapeDtypeStruct(p.shape, p.dtype) for p in parts],
        scratch_shapes=_comm_sems(n, waves),
    )(*parts)


def _handshake(peers):
    barrier = pltpu.get_barrier_semaphore()
    for peer in peers:
        pl.semaphore_signal(barrier, inc=1, device_id=peer, device_id_type=MESH)
    pl.semaphore_wait(barrier, len(peers))


def _sequencer_call(arrays, out_types, copies_fn, peers_fn, *, name, collective_id, waves=1):
    n = len(arrays)
    srcs = [jax.new_ref(a, memory_space=pltpu.MemorySpace.HBM) for a in arrays]
    dsts = [jax.empty_ref(t, memory_space=pltpu.MemorySpace.HBM) for t in out_types]
    extra = {} if waves == 1 else {"waves": waves}

    @pl.kernel(mesh=plsc.ScalarSubcoreMesh(axis_name="sequencer", num_cores=1), name=name,
               scratch_types=_comm_sems(n, waves), compiler_params=pltpu.CompilerParams(collective_id=collective_id))
    def launch(send_sems, recv_sems, local_sems):
        _handshake(peers_fn())
        copies_fn(srcs, dsts, send_sems, recv_sems, local_sems, **extra)

    launch()
    return [d[...] for d in dsts]


def _all_other_devices():
    x, y, c = _position()
    return [(1 - x if fx else x, 1 - y if fy else y, 1 - c if fc else c) for fx, fy, fc in FLIPS]


def _gather_relay_peers():
    x, y, c = _position()
    return [(x, y, 1 - c), (1 - x, y, c), (x, 1 - y, c)]


def _scatter_blocks_async(parts, *, name, collective_id, waves=1):
    return _sequencer_call(parts, [jax.ShapeDtypeStruct(p.shape, p.dtype) for p in parts],
                           _scatter_copies, _all_other_devices, name=name, collective_id=collective_id, waves=waves)


def _all_gather_async(shards, *, name, collective_id):
    return _sequencer_call(shards, [jax.ShapeDtypeStruct((N_DEV, *s.shape), s.dtype) for s in shards],
                           _gather_copies_relayed, _gather_relay_peers, name=name, collective_id=collective_id)


def _adamw_math(w, g, m, v):
    m = ADAM_B1 * m + (1.0 - ADAM_B1) * g
    v = ADAM_B2 * v + (1.0 - ADAM_B2) * (g * g)
    m_hat = m / (1.0 - ADAM_B1 ** ADAM_STEP)
    v_hat = v / (1.0 - ADAM_B2 ** ADAM_STEP)
    delta = -ADAM_LR * (m_hat / (jnp.sqrt(v_hat) + ADAM_EPS) + ADAM_WD * w)
    return delta, m, v


def _adamw_reduce_call(recv, w, m, v, *, name, T=128):
    R, W = w.shape
    if R % T == 0:
        tr, tw = T, W
    elif (R // 2) % 16 == 0:
        tr, tw = R // 2, W
    else:
        tr, tw = R, 2 * LANES

    def body(p_ref, w_ref, m_ref, v_ref, g_out, d_out, m_out, v_out):
        g = p_ref[0].astype(F32)
        for d in range(1, N_DEV):
            g = g + p_ref[d].astype(F32)
        g_out[...] = g
        d_out[...], m_out[...], v_out[...] = _adamw_math(w_ref[...], g, m_ref[...], v_ref[...])

    row = pl.BlockSpec((tr, tw), lambda i, j: (i, j))
    out = jax.ShapeDtypeStruct((R, W), F32)
    return pl.pallas_call(
        body, name=name, grid=(R // tr, W // tw),
        in_specs=[pl.BlockSpec((N_DEV, tr, tw), lambda i, j: (0, i, j)), row, row, row],
        out_specs=[row] * 4, out_shape=[out] * 4,
        compiler_params=_cp(("parallel", "parallel"), VMEM_BIG),
    )(recv, w, m, v)


def _adamw_small_call(w, g, m, v, *, name):
    def body(w_ref, g_ref, m_ref, v_ref, d_out, m_out, v_out):
        d_out[...], m_out[...], v_out[...] = _adamw_math(w_ref[...], g_ref[...], m_ref[...], v_ref[...])

    out = jax.ShapeDtypeStruct(w.shape, F32)
    return pl.pallas_call(body, name=name, in_specs=[VMEM_SPEC] * 4, out_specs=[VMEM_SPEC] * 3,
                          out_shape=[out] * 3)(w, g, m, v)


def _tile_rows(n_elems):
    return -(-n_elems // (8 * LANES)) * 8


def _pack_rows(parts, rows):
    pieces = []
    for p in parts:
        q = p.reshape(-1, LANES)
        pieces.append(jnp.pad(q, ((0, _tile_rows(p.size) - q.shape[0]), (0, 0))))
    buf = jnp.concatenate(pieces, axis=0)
    return jnp.pad(buf, ((0, rows - buf.shape[0]), (0, 0)))


def _unpack_rows(buf, shapes):
    out, off = [], 0
    for shp in shapes:
        n = 1
        for s in shp:
            n *= s
        out.append(buf[off:off + n // LANES].reshape(shp))
        off += _tile_rows(n)
    return out


def _in_w_blocks_call(a, dps, dpg, dw_a, dw_low, *, name, tk=TOKEN_TILE):
    S = a.shape[0]
    tk = min(tk, S)
    steps = S // tk
    n_s, n_g = dps.shape[1], dpg.shape[1]
    out_shape = (N_DEV, IN_BLK, D_MODEL)

    def body(a_ref, dps_ref, dpg_ref, dwa_ref, dwl_ref, out_ref, acc_s, acc_g):
        k = pl.program_id(0)

        @pl.when(k == 0)
        def _():
            acc_s[...] = jnp.zeros_like(acc_s)
            acc_g[...] = jnp.zeros_like(acc_g)

        av = a_ref[...]
        for src, acc, n in ((dps_ref, acc_s, n_s), (dpg_ref, acc_g, n_g)):
            for m0 in range(0, n, 1024):
                acc[m0:m0 + 1024, :] += _dot(src[:, m0:m0 + 1024], av, TN)

        @pl.when(k == steps - 1)
        def _():
            groups = ((dwa_ref, A_COLS), (dwl_ref, LOW_COLS), (acc_s, S_COLS), (acc_g, G_COLS))
            for d in range(N_DEV):
                lo, hi = IN_BLK * d, IN_BLK * (d + 1)
                for ref, (c0, c1) in groups:
                    s0, e0 = max(lo, c0), min(hi, c1)
                    if s0 < e0:
                        out_ref[d, s0 - lo:e0 - lo, :] = ref[s0 - c0:e0 - c0, :].astype(BF16)

    tiles = [((tk, D_MODEL), BF16), ((tk, n_s), BF16), ((tk, n_g), BF16)]
    resident = [(dw_a.shape, BF16), (dw_low.shape, BF16), (out_shape, BF16), ((n_s, D_MODEL), F32), ((n_g, D_MODEL), F32)]
    return pl.pallas_call(
        body, name=name, grid=(steps,),
        in_specs=[pl.BlockSpec((tk, D_MODEL), lambda k: (k, 0)), pl.BlockSpec((tk, n_s), lambda k: (k, 0)),
                  pl.BlockSpec((tk, n_g), lambda k: (k, 0)), _res(dw_a.shape), _res(dw_low.shape)],
        out_specs=_acc(out_shape),
        out_shape=jax.ShapeDtypeStruct(out_shape, BF16),
        scratch_shapes=[pltpu.VMEM((n_s, D_MODEL), F32), pltpu.VMEM((n_g, D_MODEL), F32)],
        compiler_params=_cp(("arbitrary",), _vmem_limit(tiles, resident)),
    )(a, dps, dpg, dw_a, dw_low)


def _local_step(x, target, W, scatter, finish):
    g1, g2, g3, g4 = [W[n].reshape(1, D_MODEL) for n in ("norm_pre_mix", "norm_post_mix", "norm_pre_ffn", "norm_post_ffn")]
    wfi, wfo = W["w_ffn_in"], W["w_ffn_out"].reshape(4, FF_BLK, D_MODEL)
    wg = jnp.pad(W["w_gate_up"], ((0, LANES - RANK), (0, 0))).astype(BF16)
    bgate = W["b_gate"].reshape(1, QK)
    gnorm = W["gla_norm"].reshape(HEADS, 1, DV)
    ln_g = W["sgu_ln_g"].reshape(GROUPS, 1, DG)
    ln_b = W["sgu_ln_b"].reshape(GROUPS, 1, DG)
    w_sp = W["w_spatial"]
    b_sp = W["b_spatial"].reshape(GROUPS, SBLOCK, 1)

    a, pa, alow, ps, pg, w_in_t = _in_proj_call(x, g1, W["w_in_blocks"], name="in_proj")
    y_gla, states = _gla_fwd_call(pa, alow, wg, bgate, gnorm, name="gla_fwd")
    y_sgu = _sgu_fwd_call(ps, ln_g, ln_b, w_sp, b_sp, name="sgu_fwd")
    t1, t2, merged, mix, x1, h = _mixer_tail_call(y_gla, y_sgu, pg, x, W["w_branch_gla"], W["w_branch_sgu"],
                                                  W["w_out"], g2, g3, name="mixer_tail")
    gu, z = _ffn_in_call(h, wfi, name="ffn_in")
    loss, dx2, dy, dg4 = _ffn_out_loss_call(z, wfo, x1, target, g4, name="ffn_out_loss")

    grads = {"norm_post_ffn": dg4}
    done = {}
    dgu = _ffn_out_bwd_call(dy, wfo, gu, name="ffn_out_bwd")
    dw_ffn_out = _weight_grads_call([z, dy], [(0, 1)], name="d_ffn_out_w")[0].reshape(N_DEV, D_FF // N_DEV, D_MODEL)
    dw_ffn_in = _ffn_in_grad_call(h, dgu, name="d_ffn_in_w")
    dgu, dw_ffn_out, dw_ffn_in = lax.optimization_barrier((dgu, dw_ffn_out, dw_ffn_in))
    ffn_received = scatter(("w_ffn_out", "w_ffn_in"), [dw_ffn_out, dw_ffn_in])
    dx1, dmix, grads["norm_pre_ffn"], grads["norm_post_mix"] = _ffn_in_bwd_call(dgu, wfi, dx2, x1, mix, g3, g2, name="ffn_in_bwd")
    dt1, dt2, dpg, dy_gla, dy_sgu = _mixer_bwd_call(dmix, t1, t2, pg, W["w_out"], W["w_branch_gla"], W["w_branch_sgu"],
                                                    name="mixer_bwd")
    rows = D_MODEL // N_DEV
    mixer_grads = _weight_grads_call([merged, dmix, y_gla, dt1, y_sgu, dt2], [(0, 1), (2, 3), (4, 5)], name="d_mixer_w")
    dy_gla, dy_sgu, mixer_grads = lax.optimization_barrier((dy_gla, dy_sgu, mixer_grads))
    mixer_received = scatter(("w_out", "w_branch_gla", "w_branch_sgu"),
                             [g.reshape(N_DEV, rows, D_MODEL) for g in mixer_grads])
    dpa, dlogit, dgn, dbg = _gla_bwd_call(pa, alow, wg, bgate, gnorm, states, dy_gla, name="gla_bwd")
    ffn_received, dpa, dlogit = lax.optimization_barrier((ffn_received, dpa, dlogit))
    dps, dlg, dlb, dwsp, dbsp = _sgu_bwd_call(ps, ln_g, ln_b, w_sp, b_sp, dy_sgu, name="sgu_bwd")
    mixer_received, dps = lax.optimization_barrier((mixer_received, dps))
    dlow = _mm(dlogit, wg, "nt", BF16, name="d_gate_up_x")
    dw_a, dw_low = _weight_grads_call([a, dpa, dlow], [(1, 0), (2, 0)], name="d_in_w_qkvr")
    dw_in = _in_w_blocks_call(a, dps, dpg, dw_a, dw_low, name="d_in_w_blocks")
    ffn_received, mixer_received, dw_in = lax.optimization_barrier((ffn_received, mixer_received, dw_in))
    in_received = scatter(("w_in",), [dw_in])
    done.update(finish(ffn_received))
    done.update(finish(mixer_received))
    dwg = _mm(alow, dlogit, "tn", F32, name="d_gate_up_w")
    grad_x, grads["norm_pre_mix"] = _in_proj_bwd_call(dpa, dlow, dps, dpg, w_in_t, x, dx1, g1, name="in_proj_bwd")

    grads["w_gate_up"] = dwg[:RANK]
    grads["b_gate"] = dbg
    grads["gla_norm"] = dgn
    grads["sgu_ln_g"] = dlg
    grads["sgu_ln_b"] = dlb
    grads["w_spatial"] = dwsp
    grads["b_spatial"] = dbsp
    done.update(finish(in_received))
    return loss, grad_x, grads, done


WEIGHTS = ("norm_pre_mix", "w_in", "w_gate_up", "b_gate", "gla_norm", "sgu_ln_g", "sgu_ln_b", "w_spatial",
           "b_spatial", "w_branch_gla", "w_branch_sgu", "w_out", "norm_post_mix", "norm_pre_ffn", "w_ffn_in",
           "w_ffn_out", "norm_post_ffn")
BIG = ("w_in", "w_branch_gla", "w_branch_sgu", "w_out", "w_ffn_in", "w_ffn_out")
MIXER = ("w_branch_gla", "w_branch_sgu", "w_out")
FFN = ("w_ffn_in", "w_ffn_out")
COLUMN_SHARDED = ("w_in", "w_ffn_in")
LATE_SCATTER_WAVES = 4
SMALL = tuple(n for n in WEIGHTS if n not in BIG)
SMALL_SHARDED = ("w_gate_up", "gla_norm", "sgu_ln_g", "sgu_ln_b")
SMALL_FULL = {"norm_pre_mix": (1024,), "w_gate_up": (16, 512), "b_gate": (512,), "gla_norm": (4, 256),
              "sgu_ln_g": (4, 256), "sgu_ln_b": (4, 256), "w_spatial": (4, 128, 128), "b_spatial": (4, 128),
              "norm_post_mix": (1024,), "norm_pre_ffn": (1024,), "norm_post_ffn": (1024,)}
SMALL_GRAD_ROWS = 648
SMALL_STATE_ROWS = 600
SMALL_GATHER_ROWS = 32


def kernel(x, norm_pre_mix, w_in, w_gate_up, b_gate, gla_norm, sgu_ln_g, sgu_ln_b, w_spatial, b_spatial, w_branch_gla, w_branch_sgu, w_out, norm_post_mix, norm_pre_ffn, w_ffn_in, w_ffn_out, norm_post_ffn, loss_target, m_norm_pre_mix, m_w_in, m_w_gate_up, m_b_gate, m_gla_norm, m_sgu_ln_g, m_sgu_ln_b, m_w_spatial, m_b_spatial, m_w_branch_gla, m_w_branch_sgu, m_w_out, m_norm_post_mix, m_norm_pre_ffn, m_w_ffn_in, m_w_ffn_out, m_norm_post_ffn, v_norm_pre_mix, v_w_in, v_w_gate_up, v_b_gate, v_gla_norm, v_sgu_ln_g, v_sgu_ln_b, v_w_spatial, v_b_spatial, v_w_branch_gla, v_w_branch_sgu, v_w_out, v_norm_post_mix, v_norm_pre_ffn, v_w_ffn_in, v_w_ffn_out, v_norm_post_ffn):
    given = dict(locals())
    def local(a, n):
        return a[0].T if n in COLUMN_SHARDED else a[0]

    w = {n: local(given[n], n) for n in WEIGHTS}
    m = {n: local(given["m_" + n], n) for n in WEIGHTS}
    v = {n: local(given["v_" + n], n) for n in WEIGHTS}
    xs, target = x[0], loss_target[0]
    me = 4 * lax.axis_index("x") + 2 * lax.axis_index("y") + lax.axis_index("c")

    small_shard = _pack_rows([w[n] for n in SMALL_SHARDED], SMALL_GATHER_ROWS)
    first = _all_gather_hbm([w["w_in"].astype(BF16), small_shard], name="gather_w_in")
    rest, first = lax.optimization_barrier(([w[n].astype(BF16) for n in MIXER + FFN], first))
    rest_blocks = _all_gather_async(rest, name="gather_rest", collective_id=1)
    W = {n: w[n] for n in SMALL if n not in SMALL_SHARDED}
    blocks = dict(zip(MIXER + FFN, rest_blocks))
    for n in ("w_branch_gla", "w_branch_sgu", "w_out", "w_ffn_out"):
        W[n] = blocks[n].reshape(-1, D_MODEL)
    W["w_ffn_in"] = blocks["w_ffn_in"]
    W["w_in_blocks"] = first[0]
    small_blocks = first[1]
    off = 0
    for n in SMALL_SHARDED:
        r, c = w[n].shape
        blk = small_blocks[:, off:off + r * c // LANES].reshape(N_DEV, r, c)
        W[n] = blk.transpose(1, 0, 2).reshape(r, N_DEV * c)
        off += _tile_rows(r * c)

    scatter_ids = iter((3, 4, 5))

    def scatter(names, parts):
        waves = LATE_SCATTER_WAVES if "w_in" in names else 1
        got = _scatter_blocks_async(parts, name="scatter_" + "_".join(names), collective_id=next(scatter_ids),
                                    waves=waves)
        return dict(zip(names, got))

    def finish(received):
        return {n: _adamw_reduce_call(r, w[n], m[n], v[n], name="adamw_" + n) for n, r in received.items()}

    loss_part, grad_x, grads, big_done = _local_step(xs, target, W, scatter, finish)

    small_part = jnp.concatenate([_pack_rows([grads[n] for n in SMALL], SMALL_GRAD_ROWS),
                                  jnp.broadcast_to(loss_part, (8, LANES))], axis=0)
    small_sum = _all_reduce_small(small_part, name="reduce_small")
    loss = small_sum[SMALL_GRAD_ROWS, 0]
    g_small = dict(zip(SMALL, _unpack_rows(small_sum, [SMALL_FULL[n] for n in SMALL])))
    for n in SMALL_SHARDED:
        c = w[n].shape[1]
        g_small[n] = lax.dynamic_slice_in_dim(g_small[n], me * c, c, axis=1)

    g_big, d_big, m_big, v_big = [{n: big_done[n][i] for n in BIG} for i in range(4)]
    small_shapes = [w[n].shape for n in SMALL]
    small_state = [_pack_rows([s[n] for n in SMALL], SMALL_STATE_ROWS) for s in (w, g_small, m, v)]
    small_out = _adamw_small_call(*small_state, name="adamw_small")
    d_small, m_small, v_small = [dict(zip(SMALL, _unpack_rows(o, small_shapes))) for o in small_out]

    def pick(big, sml):
        return [(big[n].T if n in COLUMN_SHARDED else big[n] if n in BIG else sml[n])[None] for n in WEIGHTS]

    return (loss, grad_x[None], *pick(g_big, g_small), *pick(d_big, d_small), *pick(m_big, m_small),
            *pick(v_big, v_small))
```

```python
import jax
import jax.numpy as jnp
from jax import lax
from jax.experimental import pallas as pl
from jax.experimental.pallas import tpu as pltpu
from jax.experimental.pallas import tpu_sc as plsc

F32 = jnp.float32
BF16 = jnp.bfloat16

D_MODEL = 1024
N_DEV = 8
CHUNK = 64
HEADS = 4
DK = 128
DV = 256
QK = HEADS * DK
GV = HEADS * DV
RANK = 16
GROUPS = 4
SBLOCK = 128
DG = 256
D_FF = 2816
FF_BLK = 704
EPS = 1e-6
Q_SCALE = DK ** -0.5
LANES = 128
VMEM_BIG = 48 * 1024 * 1024

D_IN = 7184
IN_BLK = 898
A_COLS = (0, 3072)
LOW_COLS = (3072, 3088)
S_COLS = (3088, 5136)
G_COLS = (5136, 7184)

ADAM_LR = 0.001
ADAM_B1 = 0.9
ADAM_B2 = 0.999
ADAM_EPS = 1e-08
ADAM_WD = 0.01
ADAM_STEP = 10

MESH = pl.DeviceIdType.MESH
ANY = pl.BlockSpec(memory_space=pl.ANY)
VMEM_SPEC = pl.BlockSpec(memory_space=pltpu.VMEM)


def _cp(sem=None, vmem=None):
    return pltpu.CompilerParams(dimension_semantics=sem, vmem_limit_bytes=vmem)


def _sig(x):
    return 0.5 * jnp.tanh(0.5 * x) + 0.5


GELU_C = 0.7978845608028654
GELU_A = 0.044715


def _gelu(x):
    t = jnp.tanh((GELU_C * x) * (1.0 + GELU_A * (x * x)))
    return (0.5 * x) * (1.0 + t)


def _gelu_and_grad(x):
    x2 = x * x
    t = jnp.tanh((GELU_C * x) * (1.0 + GELU_A * x2))
    one_t = 1.0 + t
    hx = 0.5 * x
    grad = 0.5 * one_t + (hx * (1.0 - t * t)) * (GELU_C + (3.0 * GELU_A * GELU_C) * x2)
    return hx * one_t, grad


def _logsig(x):
    return jnp.minimum(x, 0.0) - jnp.log1p(jnp.exp(-jnp.abs(x)))


def _dot(a, b, dims):
    return lax.dot_general(a, b, (dims, ((), ())), preferred_element_type=F32)


NN = ((1,), (0,))
NT = ((1,), (1,))
TN = ((0,), (0,))


def _exact_mask_dot(mask_bf16, x):
    hi = x.astype(BF16)
    r1 = x - hi.astype(F32)
    mid = r1.astype(BF16)
    lo = (r1 - mid.astype(F32)).astype(BF16)
    return _dot(mask_bf16, hi, NN) + _dot(mask_bf16, mid, NN) + _dot(mask_bf16, lo, NN)


def _pick_tile(dim, target):
    if dim <= target:
        return dim
    best = None
    for t in range(LANES, int(1.4 * target) + 1, LANES):
        if dim % t == 0:
            best = t
    assert best is not None, (dim, target)
    return best


def _mm_call(a, b, *, name, grid, a_spec, b_spec, o_spec, out_shape, dims, acc_shape):
    nk = grid[2]

    def body(a_ref, b_ref, o_ref, *acc):
        part = _dot(a_ref[...], b_ref[...], dims)
        if nk == 1:
            o_ref[...] = part.astype(o_ref.dtype)
        else:
            acc_ref = acc[0]
            k = pl.program_id(2)

            @pl.when(k == 0)
            def _():
                acc_ref[...] = part

            @pl.when(k > 0)
            def _():
                acc_ref[...] += part

            @pl.when(k == nk - 1)
            def _():
                o_ref[...] = acc_ref[...].astype(o_ref.dtype)

    return pl.pallas_call(
        body, name=name, grid=grid, in_specs=[a_spec, b_spec], out_specs=o_spec, out_shape=out_shape,
        scratch_shapes=[] if nk == 1 else [pltpu.VMEM(acc_shape, F32)],
        compiler_params=_cp(("parallel", "parallel", "arbitrary"), VMEM_BIG),
    )(a, b)


def _mm(a, b, mode, out_dtype, *, name, tm=512, tn=1024, tk=1024):
    if mode == "nn":
        (M, K), (_, N) = a.shape, b.shape
    elif mode == "nt":
        (M, K), (N, _) = a.shape, b.shape
    else:
        (K, M), (_, N) = a.shape, b.shape
    tm, tn, tk = _pick_tile(M, tm), _pick_tile(N, tn), _pick_tile(K, tk)
    if mode == "nn":
        a_spec = pl.BlockSpec((tm, tk), lambda i, j, k: (i, k))
        b_spec = pl.BlockSpec((tk, tn), lambda i, j, k: (k, j))
        dims = NN
    elif mode == "nt":
        a_spec = pl.BlockSpec((tm, tk), lambda i, j, k: (i, k))
        b_spec = pl.BlockSpec((tn, tk), lambda i, j, k: (j, k))
        dims = NT
    else:
        a_spec = pl.BlockSpec((tk, tm), lambda i, j, k: (k, i))
        b_spec = pl.BlockSpec((tk, tn), lambda i, j, k: (k, j))
        dims = TN
    return _mm_call(a, b, name=name, grid=(M // tm, N // tn, K // tk), a_spec=a_spec, b_spec=b_spec,
                    o_spec=pl.BlockSpec((tm, tn), lambda i, j, k: (i, j)),
                    out_shape=jax.ShapeDtypeStruct((M, N), out_dtype), dims=dims, acc_shape=(tm, tn))


ROW_TILE = 512
SUB_ROWS = 256


def _sub_tiles(T):
    return [pl.ds(r0, min(SUB_ROWS, T)) for r0 in range(0, T, SUB_ROWS)]


def _rt(T, W, c=0):
    return pl.BlockSpec((T, W), lambda i: (i, c))


def _rt3(nb, T, W):
    return pl.BlockSpec((nb, T, W), lambda i: (0, i, 0))


def _res(shape):
    nd = len(shape)
    return pl.BlockSpec(tuple(shape), lambda i: (0,) * nd, pipeline_mode=pl.Buffered(1))


def _acc(shape):
    nd = len(shape)
    return pl.BlockSpec(tuple(shape), lambda i: (0,) * nd, pipeline_mode=pl.Buffered(1))


def _nbytes(shape, dtype):
    n = jnp.dtype(dtype).itemsize
    for s in shape:
        n *= s
    return n


def _vmem_limit(tiles, resident, temps=16 * 1024 * 1024):
    need = 2 * sum(_nbytes(s, d) for s, d in tiles) + sum(_nbytes(s, d) for s, d in resident) + temps
    return min(need, 60 * 1024 * 1024)


def _tok_call(body, *, name, S, T, ins, outs, semantics="parallel"):
    tiles = [(spec.block_shape, a.dtype) for a, spec, kind in ins + outs if kind == "tile"]
    resident = [(a.shape, a.dtype) for a, spec, kind in ins + outs if kind == "res"]
    return pl.pallas_call(
        body, name=name, grid=(S // T,),
        in_specs=[spec for _, spec, _ in ins], out_specs=[spec for _, spec, _ in outs],
        out_shape=[jax.ShapeDtypeStruct(a.shape, a.dtype) for a, _, _ in outs],
        compiler_params=_cp((semantics,), _vmem_limit(tiles, resident)),
    )(*[a for a, _, _ in ins])


def _tile(a, spec):
    return (a, spec, "tile")


def _whole(a):
    return (a, _res(a.shape), "res")


def _out_tile(shape, dtype, spec):
    return (jax.ShapeDtypeStruct(shape, dtype), spec, "tile")


def _out_acc(shape, dtype=F32):
    return (jax.ShapeDtypeStruct(shape, dtype), _acc(shape), "res")


def _rms_stats(x):
    r = lax.rsqrt(jnp.mean(x * x, axis=-1, keepdims=True) + EPS)
    return r, x * r


def _rms_bwd(xh, r, g, dy):
    dxh = dy * g
    dx = r * (dxh - xh * jnp.mean(dxh * xh, axis=-1, keepdims=True))
    dg = jnp.sum(dy * xh, axis=0, keepdims=True)
    return dx, dg


def _accum(ref, val):
    @pl.when(pl.program_id(0) == 0)
    def _():
        ref[...] = val

    @pl.when(pl.program_id(0) > 0)
    def _():
        ref[...] += val


def _dot_rows_t(a, w_ref, row0, o_ref, chunk=1024):
    n = o_ref.shape[1]
    for n0 in range(0, n, chunk):
        n1 = min(n, n0 + chunk)
        o_ref[:, n0:n1] = _dot(a, w_ref[row0 + n0:row0 + n1, :], NT).astype(o_ref.dtype)


def _in_proj_call(x, g1, w_blocks, *, name, T=ROW_TILE):
    S = x.shape[0]

    def body(x_ref, g_ref, blk_ref, a_ref, pa_ref, al_ref, ps_ref, pg_ref, w_ref):
        @pl.when(pl.program_id(0) == 0)
        def _():
            for d in range(N_DEV):
                w_ref[d * IN_BLK:(d + 1) * IN_BLK, :] = blk_ref[d]

        _, xh = _rms_stats(x_ref[...])
        a = (xh * g_ref[...]).astype(BF16)
        a_ref[...] = a
        _dot_rows_t(a, w_ref, A_COLS[0], pa_ref)
        _dot_rows_t(a, w_ref, LOW_COLS[0], al_ref)
        _dot_rows_t(a, w_ref, S_COLS[0], ps_ref)
        _dot_rows_t(a, w_ref, G_COLS[0], pg_ref)

    widths = (D_MODEL, A_COLS[1] - A_COLS[0], LANES, S_COLS[1] - S_COLS[0], G_COLS[1] - G_COLS[0])
    return _tok_call(
        body, name=name, S=S, T=T, semantics="arbitrary",
        ins=[_tile(x, _rt(T, D_MODEL)), _whole(g1), _whole(w_blocks)],
        outs=[_out_tile((S, w), BF16, _rt(T, w)) for w in widths] + [_out_acc((D_IN, D_MODEL), BF16)])


def _mixer_tail_call(y_gla, y_sgu, pg, x, w_bg, w_bs, w_out, g2, g3, *, name, T=ROW_TILE):
    S = x.shape[0]

    def body(yg_ref, ys_ref, pg_ref, x_ref, wbg_ref, wbs_ref, wo_ref, g2_ref, g3_ref,
             t1_ref, t2_ref, mg_ref, mix_ref, x1_ref, h_ref):
        for rows in _sub_tiles(T):
            t1 = _dot(yg_ref[rows, :], wbg_ref[...], NN)
            t2 = _dot(ys_ref[rows, :], wbs_ref[...], NN)
            t1_ref[rows, :] = t1.astype(BF16)
            t2_ref[rows, :] = t2.astype(BF16)
            sg = _sig(pg_ref[rows, pl.ds(0, D_MODEL)].astype(F32))
            ss = _sig(pg_ref[rows, pl.ds(D_MODEL, D_MODEL)].astype(F32))
            merged = (sg * t1 + ss * t2).astype(BF16)
            mg_ref[rows, :] = merged
            mix = _dot(merged, wo_ref[...], NN)
            mix_ref[rows, :] = mix
            _, mh = _rms_stats(mix)
            x1 = x_ref[rows, :] + mh * g2_ref[...]
            x1_ref[rows, :] = x1
            _, xh = _rms_stats(x1)
            h_ref[rows, :] = (xh * g3_ref[...]).astype(BF16)

    row = _rt(T, D_MODEL)
    b16 = lambda: _out_tile((S, D_MODEL), BF16, row)
    f32 = lambda: _out_tile((S, D_MODEL), F32, row)
    return _tok_call(
        body, name=name, S=S, T=T,
        ins=[_tile(y_gla, row), _tile(y_sgu, row), _tile(pg, _rt(T, 2 * D_MODEL)), _tile(x, row),
             _whole(w_bg), _whole(w_bs), _whole(w_out), _whole(g2), _whole(g3)],
        outs=[b16(), b16(), b16(), f32(), f32(), b16()])


def _ffn_in_call(h, wfi, *, name, T=ROW_TILE):
    S = h.shape[0]

    def body(h_ref, w_ref, gu_ref, z_ref):
        hv = h_ref[...]
        for d in range(4):
            gate = _dot(hv, w_ref[d], NT)
            up = _dot(hv, w_ref[d + 4], NT)
            gu_ref[d] = gate.astype(BF16)
            gu_ref[d + 4] = up.astype(BF16)
            z_ref[d] = (gate * _sig(gate) * up).astype(BF16)

    return _tok_call(
        body, name=name, S=S, T=T,
        ins=[_tile(h, _rt(T, D_MODEL)), _whole(wfi)],
        outs=[_out_tile((N_DEV, S, FF_BLK), BF16, _rt3(N_DEV, T, FF_BLK)),
              _out_tile((4, S, FF_BLK), BF16, _rt3(4, T, FF_BLK))])


def _ffn_out_loss_call(z, wfo, x1, target, g4, *, name, T=ROW_TILE):
    S = x1.shape[0]

    def body(z_ref, w_ref, x1_ref, t_ref, g4_ref, loss_ref, dx2_ref, dy_ref, dg4_ref):
        loss = jnp.zeros((1, 1), F32)
        dg4 = jnp.zeros((1, D_MODEL), F32)
        for rows in _sub_tiles(T):
            y = _dot(z_ref[0, rows, :], w_ref[0], NN)
            for d in range(1, 4):
                y = y + _dot(z_ref[d, rows, :], w_ref[d], NN)
            r, yh = _rms_stats(y)
            diff = x1_ref[rows, :] + yh * g4_ref[...] - t_ref[rows, :]
            loss = loss + 0.5 * jnp.sum(jnp.mean(diff * diff, axis=-1, keepdims=True), axis=0, keepdims=True)
            dx2 = diff * (1.0 / D_MODEL)
            dx2_ref[rows, :] = dx2
            dy, dg = _rms_bwd(yh, r, g4_ref[...], dx2)
            dy_ref[rows, :] = dy.astype(BF16)
            dg4 = dg4 + dg
        _accum(loss_ref, jnp.broadcast_to(loss, (1, LANES)))
        _accum(dg4_ref, dg4)

    row = _rt(T, D_MODEL)
    return _tok_call(
        body, name=name, S=S, T=T, semantics="arbitrary",
        ins=[_tile(z, _rt3(4, T, FF_BLK)), _whole(wfo), _tile(x1, row), _tile(target, row), _whole(g4)],
        outs=[_out_acc((1, LANES)), _out_tile((S, D_MODEL), F32, row), _out_tile((S, D_MODEL), BF16, row),
              _out_acc((1, D_MODEL))])


def _ffn_out_bwd_call(dy, wfo, gu, *, name, T=ROW_TILE):
    S = dy.shape[0]

    def body(dy_ref, w_ref, gu_ref, dgu_ref):
        dyv = dy_ref[...]
        for d in range(4):
            dz = _dot(dyv, w_ref[d], NT)
            gt = gu_ref[d].astype(F32)
            up = gu_ref[d + 4].astype(F32)
            s = _sig(gt)
            dgu_ref[d] = (dz * up * s * (1.0 + gt * (1.0 - s))).astype(BF16)
            dgu_ref[d + 4] = (dz * gt * s).astype(BF16)

    blocks = _rt3(N_DEV, T, FF_BLK)
    return _tok_call(
        body, name=name, S=S, T=T,
        ins=[_tile(dy, _rt(T, D_MODEL)), _whole(wfo), _tile(gu, blocks)],
        outs=[_out_tile((N_DEV, S, FF_BLK), BF16, blocks)])[0]


def _ffn_in_bwd_call(dgu, wfi, dx2, x1, mix, g3, g2, *, name, T=ROW_TILE):
    S = x1.shape[0]

    def body(dgu_ref, w_ref, dx2_ref, x1_ref, mix_ref, g3_ref, g2_ref, dx1_ref, dmix_ref, dg3_ref, dg2_ref):
        dg3 = jnp.zeros((1, D_MODEL), F32)
        dg2 = jnp.zeros((1, D_MODEL), F32)
        for rows in _sub_tiles(T):
            dh = _dot(dgu_ref[0, rows, :], w_ref[0], NN)
            for d in range(1, N_DEV):
                dh = dh + _dot(dgu_ref[d, rows, :], w_ref[d], NN)
            r3, xh = _rms_stats(x1_ref[rows, :])
            d3, g = _rms_bwd(xh, r3, g3_ref[...], dh)
            dg3 = dg3 + g
            dx1 = dx2_ref[rows, :] + d3
            dx1_ref[rows, :] = dx1
            r2, mh = _rms_stats(mix_ref[rows, :])
            dmix, g = _rms_bwd(mh, r2, g2_ref[...], dx1)
            dg2 = dg2 + g
            dmix_ref[rows, :] = dmix.astype(BF16)
        _accum(dg3_ref, dg3)
        _accum(dg2_ref, dg2)

    row = _rt(T, D_MODEL)
    return _tok_call(
        body, name=name, S=S, T=T, semantics="arbitrary",
        ins=[_tile(dgu, _rt3(N_DEV, T, FF_BLK)), _whole(wfi), _tile(dx2, row), _tile(x1, row), _tile(mix, row),
             _whole(g3), _whole(g2)],
        outs=[_out_tile((S, D_MODEL), F32, row), _out_tile((S, D_MODEL), BF16, row),
              _out_acc((1, D_MODEL)), _out_acc((1, D_MODEL))])


def _mixer_bwd_call(dmix, t1, t2, pg, w_out, w_bg, w_bs, *, name, T=ROW_TILE):
    S = dmix.shape[0]

    def body(dmix_ref, t1_ref, t2_ref, pg_ref, wo_ref, wbg_ref, wbs_ref, dt1_ref, dt2_ref, dpg_ref, dyg_ref, dys_ref):
        gg, gs = pl.ds(0, D_MODEL), pl.ds(D_MODEL, D_MODEL)
        for rows in _sub_tiles(T):
            dm = _dot(dmix_ref[rows, :], wo_ref[...], NT)
            sg = _sig(pg_ref[rows, gg].astype(F32))
            ss = _sig(pg_ref[rows, gs].astype(F32))
            dt1 = (dm * sg).astype(BF16)
            dt2 = (dm * ss).astype(BF16)
            dt1_ref[rows, :] = dt1
            dt2_ref[rows, :] = dt2
            dpg_ref[rows, gg] = (dm * t1_ref[rows, :].astype(F32) * sg * (1.0 - sg)).astype(BF16)
            dpg_ref[rows, gs] = (dm * t2_ref[rows, :].astype(F32) * ss * (1.0 - ss)).astype(BF16)
            dyg_ref[rows, :] = _dot(dt1, wbg_ref[...], NT).astype(BF16)
            dys_ref[rows, :] = _dot(dt2, wbs_ref[...], NT).astype(BF16)

    row = _rt(T, D_MODEL)
    wide = _rt(T, 2 * D_MODEL)
    b16 = lambda: _out_tile((S, D_MODEL), BF16, row)
    return _tok_call(
        body, name=name, S=S, T=T,
        ins=[_tile(dmix, row), _tile(t1, row), _tile(t2, row), _tile(pg, wide), _whole(w_out), _whole(w_bg), _whole(w_bs)],
        outs=[b16(), b16(), _out_tile((S, 2 * D_MODEL), BF16, wide), b16(), b16()])


def _in_proj_bwd_call(dpa, dlow, dps, dpg, w_in_t, x, dx1, g1, *, name, T=ROW_TILE):
    S = x.shape[0]

    def body(dpa_ref, dl_ref, dps_ref, dpg_ref, w_ref, x_ref, dx1_ref, g1_ref, gx_ref, dg1_ref):
        da = (_dot(dpa_ref[...], w_ref[A_COLS[0]:A_COLS[1], :], NN)
              + _dot(dl_ref[...], w_ref[LOW_COLS[0]:LOW_COLS[0] + LANES, :], NN)
              + _dot(dps_ref[...], w_ref[S_COLS[0]:S_COLS[1], :], NN)
              + _dot(dpg_ref[...], w_ref[G_COLS[0]:G_COLS[1], :], NN))
        r, xh = _rms_stats(x_ref[...])
        dxa, dg = _rms_bwd(xh, r, g1_ref[...], da)
        gx_ref[...] = dx1_ref[...] + dxa
        _accum(dg1_ref, dg)

    row = _rt(T, D_MODEL)
    return _tok_call(
        body, name=name, S=S, T=T, semantics="arbitrary",
        ins=[_tile(dpa, _rt(T, dpa.shape[1])), _tile(dlow, _rt(T, dlow.shape[1])), _tile(dps, _rt(T, dps.shape[1])),
             _tile(dpg, _rt(T, dpg.shape[1])), _whole(w_in_t), _tile(x, row), _tile(dx1, row), _whole(g1)],
        outs=[_out_tile((S, D_MODEL), F32, row), _out_acc((1, D_MODEL))])


TOKEN_TILE = 512


def _weight_grads_part(arrays, pairs, *, tk=TOKEN_TILE, out_dtype=BF16):
    S = arrays[0].shape[-2]
    tk = min(tk, S)
    n_in = len(arrays)

    def out_shape(i, j):
        a, b = arrays[i], arrays[j]
        if a.ndim == 3:
            return (a.shape[0], a.shape[2], b.shape[1])
        if b.ndim == 3:
            return (b.shape[0], a.shape[1], b.shape[2])
        return (a.shape[1], b.shape[1])

    shapes = [out_shape(i, j) for i, j in pairs]

    in_place = out_dtype == F32

    def body(ins, outs, accs):
        k = pl.program_id(0)
        if in_place:
            accs = outs

        @pl.when(k == 0)
        def _():
            for acc in accs:
                acc[...] = jnp.zeros_like(acc)

        for (i, j), acc in zip(pairs, accs):
            a_ref, b_ref = ins[i], ins[j]
            if len(a_ref.shape) == 3:
                for n in range(a_ref.shape[0]):
                    acc[n] += _dot(a_ref[n], b_ref[...], TN)
            elif len(b_ref.shape) == 3:
                a = a_ref[...]
                for n in range(b_ref.shape[0]):
                    acc[n] += _dot(a, b_ref[n], TN)
            else:
                b = b_ref[...]
                for m0 in range(0, a_ref.shape[1], 1024):
                    m1 = min(a_ref.shape[1], m0 + 1024)
                    acc[m0:m1, :] += _dot(a_ref[:, m0:m1], b, TN)

        if not in_place:
            @pl.when(k == S // tk - 1)
            def _():
                for out, acc in zip(outs, accs):
                    out[...] = acc[...].astype(out.dtype)

    def in_spec(a):
        if a.ndim == 3:
            return pl.BlockSpec((a.shape[0], tk, a.shape[2]), lambda k: (0, k, 0))
        return pl.BlockSpec((tk, a.shape[1]), lambda k: (k, 0))

    return dict(
        body=body, steps=S // tk, arrays=list(arrays),
        in_specs=[in_spec(a) for a in arrays],
        out_specs=[_acc(s) for s in shapes],
        out_shapes=[jax.ShapeDtypeStruct(s, out_dtype) for s in shapes],
        scratch=[] if in_place else [pltpu.VMEM(s, F32) for s in shapes],
        tiles=[(in_spec(a).block_shape, a.dtype) for a in arrays],
        resident=[(s, F32) for s in shapes] + ([] if in_place else [(s, BF16) for s in shapes]))


def _weight_grads_call(arrays, pairs, *, name, tk=TOKEN_TILE):
    part = _weight_grads_part(arrays, pairs, tk=tk)
    n_in, n_out = len(part["arrays"]), len(part["out_shapes"])

    def body(*refs):
        part["body"](refs[:n_in], refs[n_in:n_in + n_out], refs[n_in + n_out:])

    return pl.pallas_call(
        body, name=name, grid=(part["steps"],),
        in_specs=part["in_specs"], out_specs=part["out_specs"], out_shape=part["out_shapes"],
        scratch_shapes=part["scratch"],
        compiler_params=_cp(("arbitrary",), _vmem_limit(part["tiles"], part["resident"])),
    )(*part["arrays"])


def _with_rider(body, n_in, n_out, n_scratch, rider):
    if rider is None:
        return body
    r_in, r_out = len(rider["arrays"]), len(rider["out_shapes"])

    def both(*refs):
        ins, rest = refs[:n_in + r_in], refs[n_in + r_in:]
        outs, scratch = rest[:n_out + r_out], rest[n_out + r_out:]
        body(*ins[:n_in], *outs[:n_out], *scratch[:n_scratch])
        rider["body"](ins[n_in:], outs[n_out:], scratch[n_scratch:])

    return both


def _rider_lists(rider):
    if rider is None:
        return [], [], [], [], [], [], []
    return (rider["arrays"], rider["in_specs"], rider["out_specs"], rider["out_shapes"], rider["scratch"],
            rider["tiles"], rider["resident"])


def _ffn_in_grad_call(h, dgu, *, name, tk=TOKEN_TILE):
    S = h.shape[0]
    tk = min(tk, S)
    nb = 4
    shape = (nb, FF_BLK, D_MODEL)

    def body(h_ref, dgu_ref, out_ref, acc):
        k = pl.program_id(1)

        @pl.when(k == 0)
        def _():
            acc[...] = jnp.zeros_like(acc)

        hv = h_ref[...]
        for n in range(nb):
            acc[n] += _dot(dgu_ref[n], hv, TN)

        @pl.when(k == S // tk - 1)
        def _():
            out_ref[...] = acc[...].astype(out_ref.dtype)

    tiles = [((tk, D_MODEL), BF16), ((nb, tk, FF_BLK), BF16), (shape, BF16)]
    return pl.pallas_call(
        body, name=name, grid=(N_DEV // nb, S // tk),
        in_specs=[pl.BlockSpec((tk, D_MODEL), lambda g, k: (k, 0)),
                  pl.BlockSpec((nb, tk, FF_BLK), lambda g, k: (g, k, 0))],
        out_specs=pl.BlockSpec(shape, lambda g, k: (g, 0, 0)),
        out_shape=jax.ShapeDtypeStruct((N_DEV, FF_BLK, D_MODEL), BF16),
        scratch_shapes=[pltpu.VMEM(shape, F32)],
        compiler_params=_cp(("parallel", "arbitrary"), _vmem_limit(tiles, [(shape, F32)])),
    )(h, dgu)


GLA_TILE = 256
Q_OFF, K_OFF, V_OFF, R_OFF = 0, QK, 2 * QK, 2 * QK + GV


def _tri(lower):
    r = lax.broadcasted_iota(jnp.int32, (CHUNK, CHUNK), 0)
    c = lax.broadcasted_iota(jnp.int32, (CHUNK, CHUNK), 1)
    return jnp.where((r >= c) if lower else (c >= r), 1.0, 0.0).astype(BF16)


def _gla_fwd_call(pa, alow, wg, bgate, gnorm, *, name):
    S = pa.shape[0]
    Tg = min(GLA_TILE, S)
    cb = Tg // CHUNK

    def body(pa_ref, al_ref, wg_ref, bg_ref, gn_ref, y_ref, st_ref, state):
        @pl.when(pl.program_id(0) == 0)
        def _():
            state[...] = jnp.zeros_like(state)

        logit = _dot(al_ref[...], wg_ref[...], NN) + bg_ref[...]
        ls = _logsig(logit) * (1.0 / 16.0)
        tri = _tri(True)
        for c in range(cb):
            rows = pl.ds(c * CHUNK, CHUNK)
            cum = _exact_mask_dot(tri, ls[c * CHUNK:(c + 1) * CHUNK])
            tot = cum[CHUNK - 1:CHUNK]
            kd = (pa_ref[rows, pl.ds(K_OFF, QK)].astype(F32) * jnp.exp(tot - cum)).astype(BF16)
            decay = jnp.exp(tot)
            for h in range(HEADS):
                lanes = slice(h * DK, (h + 1) * DK)
                new = state[h] * decay[:, lanes] + _dot(pa_ref[rows, pl.ds(V_OFF + h * DV, DV)], kd[:, lanes], TN)
                state[h] = new
                st_ref[c, h] = new
        for c in range(cb):
            rows = pl.ds(c * CHUNK, CHUNK)
            for h in range(HEADS):
                qs = (pa_ref[rows, pl.ds(Q_OFF + h * DK, DK)].astype(F32) * Q_SCALE).astype(BF16)
                o = _dot(qs, st_ref[c, h].astype(BF16), NT)
                rs = lax.rsqrt(jnp.mean(o * o, axis=-1, keepdims=True) + EPS)
                rr = pa_ref[rows, pl.ds(R_OFF + h * DV, DV)].astype(F32)
                y_ref[rows, pl.ds(h * DV, DV)] = (o * rs * gn_ref[h] * (rr * _sig(rr))).astype(BF16)

    return pl.pallas_call(
        body, name=name, grid=(S // Tg,),
        in_specs=[
            pl.BlockSpec((Tg, 2 * QK + 2 * GV), lambda t: (t, 0)),
            pl.BlockSpec((Tg, LANES), lambda t: (t, 0)),
            pl.BlockSpec((LANES, QK), lambda t: (0, 0)),
            pl.BlockSpec((1, QK), lambda t: (0, 0)),
            pl.BlockSpec((HEADS, 1, DV), lambda t: (0, 0, 0)),
        ],
        out_specs=[
            pl.BlockSpec((Tg, GV), lambda t: (t, 0)),
            pl.BlockSpec((cb, HEADS, DV, DK), lambda t: (t, 0, 0, 0)),
        ],
        out_shape=[jax.ShapeDtypeStruct((S, GV), BF16),
                   jax.ShapeDtypeStruct((S // CHUNK, HEADS, DV, DK), F32)],
        scratch_shapes=[pltpu.VMEM((HEADS, DV, DK), F32)],
        compiler_params=_cp(("arbitrary",), VMEM_BIG),
    )(pa, alow, wg, bgate, gnorm)


def _gla_bwd_call(pa, alow, wg, bgate, gnorm, states, dy, *, name, rider=None):
    S = pa.shape[0]
    Tg = min(GLA_TILE, S)
    cb = Tg // CHUNK
    nt = S // Tg
    r_arrays, r_in, r_out, r_shapes, r_scratch, r_tiles, r_resident = _rider_lists(rider)
    assert rider is None or rider["steps"] == nt

    def rev(t):
        return nt - 1 - t

    def body(pa_ref, al_ref, wg_ref, bg_ref, gn_ref, st_ref, prev_ref, dy_ref,
             dpa_ref, dl_ref, dgn_ref, dbg_ref, carry, pbuf):
        t = pl.program_id(0)

        @pl.when(t == 0)
        def _():
            carry[...] = jnp.zeros_like(carry)
            dgn_ref[...] = jnp.zeros_like(dgn_ref)
            dbg_ref[...] = jnp.zeros_like(dbg_ref)

        logit = _dot(al_ref[...], wg_ref[...], NN) + bg_ref[...]
        ls = _logsig(logit) * (1.0 / 16.0)
        sneg = 1.0 / (1.0 + jnp.exp(logit))
        tri = _tri(True)
        upper = _tri(False)
        first_tile = rev(t) == 0
        heads = range(HEADS)

        w, decay, kd = [], [], []
        for c in range(cb):
            rows = pl.ds(c * CHUNK, CHUNK)
            cum = _exact_mask_dot(tri, ls[c * CHUNK:(c + 1) * CHUNK])
            tot = cum[CHUNK - 1:CHUNK]
            w.append(jnp.exp(tot - cum))
            decay.append(jnp.exp(tot))
            kd.append(pa_ref[rows, pl.ds(K_OFF, QK)].astype(F32) * w[c])

        dgn = [jnp.zeros((1, DV), F32) for _ in heads]
        for c in range(cb):
            rows = pl.ds(c * CHUNK, CHUNK)
            for h in heads:
                gn = gn_ref[h]
                qs = (pa_ref[rows, pl.ds(Q_OFF + h * DK, DK)].astype(F32) * Q_SCALE).astype(BF16)
                st16 = st_ref[c, h].astype(BF16)
                o = _dot(qs, st16, NT)
                rs = lax.rsqrt(jnp.mean(o * o, axis=-1, keepdims=True) + EPS)
                oh = o * rs
                rr = pa_ref[rows, pl.ds(R_OFF + h * DV, DV)].astype(F32)
                sr = _sig(rr)
                dyv = dy_ref[rows, pl.ds(h * DV, DV)].astype(F32)
                dpa_ref[rows, pl.ds(R_OFF + h * DV, DV)] = (
                    dyv * oh * gn * sr * (1.0 + rr * (1.0 - sr))).astype(BF16)
                don = dyv * (rr * sr)
                dgn[h] = dgn[h] + jnp.sum(don * oh, axis=0, keepdims=True)
                doh = don * gn
                do16 = (rs * (doh - oh * jnp.mean(doh * oh, axis=-1, keepdims=True))).astype(BF16)
                dpa_ref[rows, pl.ds(Q_OFF + h * DK, DK)] = (_dot(do16, st16, NN) * Q_SCALE).astype(BF16)
                pbuf[c, h] = _dot(do16, qs, TN)
        for h in heads:
            dgn_ref[h] += dgn[h]

        dkd = [[None] * HEADS for _ in range(cb)]
        ddecay = [[None] * HEADS for _ in range(cb)]
        for c in reversed(range(cb)):
            rows = pl.ds(c * CHUNK, CHUNK)
            for h in heads:
                lanes = slice(h * DK, (h + 1) * DK)
                gt = pbuf[c, h] + carry[h]
                gt16 = gt.astype(BF16)
                dkd[c][h] = _dot(pa_ref[rows, pl.ds(V_OFF + h * DV, DV)], gt16, NN)
                dpa_ref[rows, pl.ds(V_OFF + h * DV, DV)] = _dot(kd[c][:, lanes].astype(BF16), gt16, NT).astype(BF16)
                if c > 0:
                    st_prev = st_ref[c - 1, h]
                else:
                    st_prev = jnp.where(first_tile, 0.0, prev_ref[0, h])
                ddecay[c][h] = jnp.sum(gt * st_prev, axis=0, keepdims=True)
                carry[h] = gt * decay[c][:, lanes]

        dbg = jnp.zeros((1, QK), F32)
        for c in range(cb):
            rows = pl.ds(c * CHUNK, CHUNK)
            dkd_c = jnp.concatenate(dkd[c], axis=1)
            dpa_ref[rows, pl.ds(K_OFF, QK)] = (dkd_c * w[c]).astype(BF16)
            e = dkd_c * kd[c]
            dtot = jnp.sum(e, axis=0, keepdims=True) + jnp.concatenate(ddecay[c], axis=1) * decay[c]
            dls = dtot - _exact_mask_dot(upper, e)
            dlogit = dls * (1.0 / 16.0) * sneg[c * CHUNK:(c + 1) * CHUNK]
            dl_ref[rows, :] = dlogit.astype(BF16)
            dbg = dbg + jnp.sum(dlogit, axis=0, keepdims=True)
        dbg_ref[...] += dbg

    wide = 2 * QK + 2 * GV
    tiles = [((Tg, wide), BF16), ((cb + 1, HEADS, DV, DK), F32), ((Tg, GV), BF16), ((Tg, wide), BF16),
             ((Tg, QK), BF16)] + r_tiles
    resident = [((cb + 1, HEADS, DV, DK), F32)] + r_resident
    return pl.pallas_call(
        _with_rider(body, 8, 4, 2, rider), name=name, grid=(nt,),
        in_specs=[
            pl.BlockSpec((Tg, wide), lambda t: (rev(t), 0)),
            pl.BlockSpec((Tg, LANES), lambda t: (rev(t), 0)),
            pl.BlockSpec((LANES, QK), lambda t: (0, 0)),
            pl.BlockSpec((1, QK), lambda t: (0, 0)),
            pl.BlockSpec((HEADS, 1, DV), lambda t: (0, 0, 0)),
            pl.BlockSpec((cb, HEADS, DV, DK), lambda t: (rev(t), 0, 0, 0)),
            pl.BlockSpec((1, HEADS, DV, DK), lambda t: (jnp.maximum(rev(t) * cb - 1, 0), 0, 0, 0)),
            pl.BlockSpec((Tg, GV), lambda t: (rev(t), 0)),
        ] + r_in,
        out_specs=[
            pl.BlockSpec((Tg, wide), lambda t: (rev(t), 0)),
            pl.BlockSpec((Tg, QK), lambda t: (rev(t), 0)),
            pl.BlockSpec((HEADS, 1, DV), lambda t: (0, 0, 0)),
            pl.BlockSpec((1, QK), lambda t: (0, 0)),
        ] + r_out,
        out_shape=[jax.ShapeDtypeStruct((S, wide), BF16), jax.ShapeDtypeStruct((S, QK), BF16),
                   jax.ShapeDtypeStruct((HEADS, 1, DV), F32), jax.ShapeDtypeStruct((1, QK), F32)] + r_shapes,
        scratch_shapes=[pltpu.VMEM((HEADS, DV, DK), F32), pltpu.VMEM((cb, HEADS, DV, DK), F32)] + r_scratch,
        compiler_params=_cp(("arbitrary",), _vmem_limit(tiles, resident)),
    )(pa, alow, wg, bgate, gnorm, states, states, dy, *r_arrays)


SGU_TILE = 256


def _sgu_mask():
    r = lax.broadcasted_iota(jnp.int32, (SBLOCK, SBLOCK), 0)
    c = lax.broadcasted_iota(jnp.int32, (SBLOCK, SBLOCK), 1)
    return (c < CHUNK) | (r >= CHUNK)


def _ln_stats(vf):
    mu = jnp.mean(vf, axis=-1, keepdims=True)
    xc = vf - mu
    rs = lax.rsqrt(jnp.mean(xc * xc, axis=-1, keepdims=True) + EPS)
    return rs, xc * rs


def _sgu_fwd_call(ps, ln_g, ln_b, w_sp, b_sp, *, name):
    S = ps.shape[0]
    Ts = min(SGU_TILE, S)

    def body(ps_ref, lg_ref, lb_ref, w_ref, b_ref, y_ref):
        mask = _sgu_mask()
        for g in range(GROUPS):
            wm = jnp.where(mask, w_ref[g], 0.0).astype(BF16)
            for p in range(Ts // SBLOCK):
                rows = pl.ds(p * SBLOCK, SBLOCK)
                u = _gelu(ps_ref[rows, pl.ds(g * DG, DG)].astype(F32))
                _, xh = _ln_stats(_gelu(ps_ref[rows, pl.ds(D_MODEL + g * DG, DG)].astype(F32)))
                vn = xh * lg_ref[g] + lb_ref[g]
                mixed = _dot(wm, vn.astype(BF16), NN) + b_ref[g]
                y_ref[rows, pl.ds(g * DG, DG)] = (u * mixed).astype(BF16)

    full3 = lambda a, b, c: pl.BlockSpec((a, b, c), lambda t: (0, 0, 0))
    return pl.pallas_call(
        body, name=name, grid=(S // Ts,),
        in_specs=[pl.BlockSpec((Ts, 2 * D_MODEL), lambda t: (t, 0)),
                  full3(GROUPS, 1, DG), full3(GROUPS, 1, DG), full3(GROUPS, SBLOCK, SBLOCK), full3(GROUPS, SBLOCK, 1)],
        out_specs=pl.BlockSpec((Ts, D_MODEL), lambda t: (t, 0)),
        out_shape=jax.ShapeDtypeStruct((S, D_MODEL), BF16),
        compiler_params=_cp(("parallel",)),
    )(ps, ln_g, ln_b, w_sp, b_sp)


def _sgu_bwd_call(ps, ln_g, ln_b, w_sp, b_sp, dy, *, name, rider=None):
    S = ps.shape[0]
    Ts = min(SGU_TILE, S)
    r_arrays, r_in, r_out, r_shapes, r_scratch, r_tiles, r_resident = _rider_lists(rider)
    assert rider is None or rider["steps"] == S // Ts

    def body(ps_ref, lg_ref, lb_ref, w_ref, b_ref, dy_ref, ds_ref, dlg_ref, dlb_ref, dw_ref, db_ref):
        @pl.when(pl.program_id(0) == 0)
        def _():
            dlg_ref[...] = jnp.zeros_like(dlg_ref)
            dlb_ref[...] = jnp.zeros_like(dlb_ref)
            dw_ref[...] = jnp.zeros_like(dw_ref)
            db_ref[...] = jnp.zeros_like(db_ref)

        mask = _sgu_mask()
        for g in range(GROUPS):
            wm = jnp.where(mask, w_ref[g], 0.0).astype(BF16)
            lg = lg_ref[g]
            for p in range(Ts // SBLOCK):
                rows = pl.ds(p * SBLOCK, SBLOCK)
                su = ps_ref[rows, pl.ds(g * DG, DG)].astype(F32)
                sv = ps_ref[rows, pl.ds(D_MODEL + g * DG, DG)].astype(F32)
                u, du = _gelu_and_grad(su)
                gv, dgv = _gelu_and_grad(sv)
                rs, xh = _ln_stats(gv)
                vn16 = (xh * lg + lb_ref[g]).astype(BF16)
                mixed = _dot(wm, vn16, NN) + b_ref[g]
                dyv = dy_ref[rows, pl.ds(g * DG, DG)].astype(F32)
                ds_ref[rows, pl.ds(g * DG, DG)] = (dyv * mixed * du).astype(BF16)
                dmix = dyv * u
                dmix16 = dmix.astype(BF16)
                db_ref[g] += jnp.sum(dmix, axis=-1, keepdims=True)
                dw_ref[g] += jnp.where(mask, _dot(dmix16, vn16, NT), 0.0)
                dvn = _dot(wm, dmix16, TN)
                dlg_ref[g] += jnp.sum(dvn * xh, axis=0, keepdims=True)
                dlb_ref[g] += jnp.sum(dvn, axis=0, keepdims=True)
                dxh = dvn * lg
                dvf = rs * (dxh - jnp.mean(dxh, axis=-1, keepdims=True)
                            - xh * jnp.mean(dxh * xh, axis=-1, keepdims=True))
                ds_ref[rows, pl.ds(D_MODEL + g * DG, DG)] = (dvf * dgv).astype(BF16)

    full3 = lambda a, b, c: pl.BlockSpec((a, b, c), lambda t: (0, 0, 0))
    tiles = [((Ts, 2 * D_MODEL), BF16), ((Ts, D_MODEL), BF16), ((Ts, 2 * D_MODEL), BF16)] + r_tiles
    return pl.pallas_call(
        _with_rider(body, 6, 5, 0, rider), name=name, grid=(S // Ts,),
        in_specs=[pl.BlockSpec((Ts, 2 * D_MODEL), lambda t: (t, 0)),
                  full3(GROUPS, 1, DG), full3(GROUPS, 1, DG), full3(GROUPS, SBLOCK, SBLOCK), full3(GROUPS, SBLOCK, 1),
                  pl.BlockSpec((Ts, D_MODEL), lambda t: (t, 0))] + r_in,
        out_specs=[pl.BlockSpec((Ts, 2 * D_MODEL), lambda t: (t, 0)),
                   full3(GROUPS, 1, DG), full3(GROUPS, 1, DG), full3(GROUPS, SBLOCK, SBLOCK), full3(GROUPS, SBLOCK, 1)]
        + r_out,
        out_shape=[jax.ShapeDtypeStruct((S, 2 * D_MODEL), BF16),
                   jax.ShapeDtypeStruct((GROUPS, 1, DG), F32), jax.ShapeDtypeStruct((GROUPS, 1, DG), F32),
                   jax.ShapeDtypeStruct((GROUPS, SBLOCK, SBLOCK), F32),
                   jax.ShapeDtypeStruct((GROUPS, SBLOCK, 1), F32)] + r_shapes,
        scratch_shapes=r_scratch,
        compiler_params=_cp(("arbitrary",), _vmem_limit(tiles, r_resident)),
    )(ps, ln_g, ln_b, w_sp, b_sp, dy, *r_arrays)


def _position():
    return lax.axis_index("x"), lax.axis_index("y"), lax.axis_index("c")


def _gather_copies(srcs, dsts, send_sems, recv_sems, local_sems, relay=False):
    if relay:
        return _gather_copies_relayed(srcs, dsts, send_sems, recv_sems, local_sems)
    x, y, c = _position()
    me, sibling = (x, y, c), (x, y, 1 - c)
    chips = [(1 - x, y), (x, 1 - y), (1 - x, 1 - y)]
    n = len(srcs)

    def slab(a, block):
        px, py, pc = block
        return dsts[a].at[4 * px + 2 * py + pc]

    def copy(a, k, block, to, src=None):
        return pltpu.make_async_remote_copy(
            src_ref=slab(a, block) if src is None else src, dst_ref=slab(a, block),
            send_sem=send_sems.at[7 * a + k], recv_sem=recv_sems.at[7 * a + k], device_id=to, device_id_type=MESH)

    mine = [pltpu.make_async_copy(srcs[a], slab(a, me), local_sems.at[a]) for a in range(n)]
    for cp in mine:
        cp.start()
    first = []
    for a in range(n):
        first.append(copy(a, 0, me, sibling, src=srcs[a]))
        first += [copy(a, 1 + j, me, (*chip, c), src=srcs[a]) for j, chip in enumerate(chips)]
    for cp in first:
        cp.start()
    passed = []
    for j, chip in enumerate(chips):
        for a in range(n):
            copy(a, 1 + j, (*chip, c), me).wait_recv()
            cp = copy(a, 4 + j, (*chip, c), sibling)
            cp.start()
            passed.append(cp)
    for a in range(n):
        copy(a, 0, sibling, me).wait_recv()
        for j, chip in enumerate(chips):
            copy(a, 4 + j, (*chip, 1 - c), me).wait_recv()
    for cp in first + passed:
        cp.wait_send()
    for cp in mine:
        cp.wait()


def _gather_copies_relayed(srcs, dsts, send_sems, recv_sems, local_sems):
    x, y, c = _position()
    me, sibling = (x, y, c), (x, y, 1 - c)
    x_chip, y_chip, d_chip = (1 - x, y), (x, 1 - y), (1 - x, 1 - y)
    south = c == 0
    relay_from = (jnp.where(south, x, 1 - x), jnp.where(south, 1 - y, y), c)
    relay_to = (jnp.where(south, 1 - x, x), jnp.where(south, y, 1 - y), c)
    n = len(srcs)

    def slab(a, block):
        px, py, pc = block
        return dsts[a].at[4 * px + 2 * py + pc]

    def copy(a, k, block, to, src=None):
        return pltpu.make_async_remote_copy(
            src_ref=slab(a, block) if src is None else src, dst_ref=slab(a, block),
            send_sem=send_sems.at[7 * a + k], recv_sem=recv_sems.at[7 * a + k], device_id=to, device_id_type=MESH)

    mine = [pltpu.make_async_copy(srcs[a], slab(a, me), local_sems.at[a]) for a in range(n)]
    for cp in mine:
        cp.start()
    sent = []
    for a in range(n):
        sent += [copy(a, 0, me, sibling, src=srcs[a]), copy(a, 1, me, (*x_chip, c), src=srcs[a]),
                 copy(a, 2, me, (*y_chip, c), src=srcs[a])]
    for cp in sent:
        cp.start()
    for a in range(n):
        copy(a, 1, (*x_chip, c), me).wait_recv()
        copy(a, 2, (*y_chip, c), me).wait_recv()
        later = [copy(a, 3, relay_from, relay_to), copy(a, 4, (*x_chip, c), sibling), copy(a, 5, (*y_chip, c), sibling)]
        for cp in later:
            cp.start()
        sent += later
    for a in range(n):
        copy(a, 3, (*d_chip, c), me).wait_recv()
        cp = copy(a, 6, (*d_chip, c), sibling)
        cp.start()
        sent.append(cp)
    for a in range(n):
        copy(a, 0, sibling, me).wait_recv()
        for k, chip in ((4, x_chip), (5, y_chip), (6, d_chip)):
            copy(a, k, (*chip, 1 - c), me).wait_recv()
    for cp in sent:
        cp.wait_send()
    for cp in mine:
        cp.wait()


def _all_gather_hbm(shards, *, name):
    n = len(shards)

    def body(*refs):
        srcs, dsts = refs[:n], refs[n:2 * n]
        send_sems, recv_sems, local_sems = refs[2 * n:]
        _gather_copies(srcs, dsts, send_sems, recv_sems, local_sems, relay=True)

    return pl.pallas_call(
        body, name=name,
        in_specs=[ANY] * n, out_specs=[ANY] * n,
        out_shape=[jax.ShapeDtypeStruct((N_DEV, *s.shape), s.dtype) for s in shards],
        scratch_shapes=_comm_sems(n),
    )(*shards)


def _all_reduce_small(part, *, name):
    R, W = part.shape

    def body(x_ref, out_ref, gathered, send_sems, recv_sems, local_sems):
        _gather_copies([x_ref], [gathered], send_sems, recv_sems, local_sems)
        acc = gathered[0]
        for d in range(1, N_DEV):
            acc = acc + gathered[d]
        out_ref[...] = acc

    return pl.pallas_call(
        body, name=name,
        in_specs=[VMEM_SPEC], out_specs=VMEM_SPEC,
        out_shape=jax.ShapeDtypeStruct((R, W), F32),
        scratch_shapes=[pltpu.VMEM((N_DEV, R, W), F32),
                        pltpu.SemaphoreType.DMA((7,)), pltpu.SemaphoreType.DMA((7,)), pltpu.SemaphoreType.DMA((1,))],
    )(part)


FLIPS = [(fx, fy, fc) for fx in (0, 1) for fy in (0, 1) for fc in (0, 1)][1:]


def _scatter_copies(srcs, dsts, send_sems, recv_sems, local_sems, waves=1):
    n = len(srcs)
    x, y, c = _position()
    me = 4 * x + 2 * y + c
    mine = [pltpu.make_async_copy(srcs[a].at[me], dsts[a].at[me], local_sems.at[a]) for a in range(n)]
    for cp in mine:
        cp.start()
    peers = []
    for fx, fy, fc in FLIPS:
        tx = 1 - x if fx else x
        ty = 1 - y if fy else y
        tc = 1 - c if fc else c
        peers.append(((tx, ty, tc), 4 * tx + 2 * ty + tc))
    copies = []
    for w in range(waves):
        wave = []
        for k, (peer_id, peer) in enumerate(peers):
            for a in range(n):
                rows = srcs[a].shape[1]
                step = rows if waves == 1 else (rows // waves) // 16 * 16
                r0 = w * step
                cut = pl.ds(r0, step if w < waves - 1 else rows - r0)
                sem = (7 * a + k) * waves + w
                cp = pltpu.make_async_remote_copy(
                    src_ref=srcs[a].at[peer, cut], dst_ref=dsts[a].at[me, cut],
                    send_sem=send_sems.at[sem], recv_sem=recv_sems.at[sem],
                    device_id=peer_id, device_id_type=MESH)
                cp.start()
                wave.append(cp)
        for cp in wave:
            cp.wait_send()
        copies += wave
    for cp in copies:
        cp.wait_recv()
    for cp in mine:
        cp.wait()


def _comm_sems(n, waves=1):
    return [pltpu.SemaphoreType.DMA((7 * n * waves,)), pltpu.SemaphoreType.DMA((7 * n * waves,)),
            pltpu.SemaphoreType.DMA((n,))]


def _scatter_blocks(parts, *, name, waves=1):
    n = len(parts)

    def body(*refs):
        _scatter_copies(refs[:n], refs[n:2 * n], *refs[2 * n:], waves=waves)

    return pl.pallas_call(
        body, name=name,
        in_specs=[ANY] * n, out_specs=[ANY] * n,
        out_shape=[jax.ShapeDtypeStruct(p.shape, p.dtype) for p in parts],
        scratch_shapes=_comm_sems(n, waves),
    )(*parts)


def _handshake(peers):
    barrier = pltpu.get_barrier_semaphore()
    for peer in peers:
        pl.semaphore_signal(barrier, inc=1, device_id=peer, device_id_type=MESH)
    pl.semaphore_wait(barrier, len(peers))


def _sequencer_call(arrays, out_types, copies_fn, peers_fn, *, name, collective_id, waves=1):
    n = len(arrays)
    srcs = [jax.new_ref(a, memory_space=pltpu.MemorySpace.HBM) for a in arrays]
    dsts = [jax.empty_ref(t, memory_space=pltpu.MemorySpace.HBM) for t in out_types]
    extra = {} if waves == 1 else {"waves": waves}

    @pl.kernel(mesh=plsc.ScalarSubcoreMesh(axis_name="sequencer", num_cores=1), name=name,
               scratch_types=_comm_sems(n, waves), compiler_params=pltpu.CompilerParams(collective_id=collective_id))
    def launch(send_sems, recv_sems, local_sems):
        _handshake(peers_fn())
        copies_fn(srcs, dsts, send_sems, recv_sems, local_sems, **extra)

    launch()
    return [d[...] for d in dsts]


def _all_other_devices():
    x, y, c = _position()
    return [(1 - x if fx else x, 1 - y if fy else y, 1 - c if fc else c) for fx, fy, fc in FLIPS]


def _gather_relay_peers():
    x, y, c = _position()
    return [(x, y, 1 - c), (1 - x, y, c), (x, 1 - y, c)]


def _scatter_blocks_async(parts, *, name, collective_id, waves=1):
    return _sequencer_call(parts, [jax.ShapeDtypeStruct(p.shape, p.dtype) for p in parts],
                           _scatter_copies, _all_other_devices, name=name, collective_id=collective_id, waves=waves)


def _all_gather_async(shards, *, name, collective_id):
    return _sequencer_call(shards, [jax.ShapeDtypeStruct((N_DEV, *s.shape), s.dtype) for s in shards],
                           _gather_copies_relayed, _gather_relay_peers, name=name, collective_id=collective_id)


def _adamw_math(w, g, m, v):
    m = ADAM_B1 * m + (1.0 - ADAM_B1) * g
    v = ADAM_B2 * v + (1.0 - ADAM_B2) * (g * g)
    m_hat = m / (1.0 - ADAM_B1 ** ADAM_STEP)
    v_hat = v / (1.0 - ADAM_B2 ** ADAM_STEP)
    delta = -ADAM_LR * (m_hat / (jnp.sqrt(v_hat) + ADAM_EPS) + ADAM_WD * w)
    return delta, m, v


def _adamw_reduce_call(recv, w, m, v, *, name, T=128):
    R, W = w.shape
    if R % T == 0:
        tr, tw = T, W
    elif (R // 2) % 16 == 0:
        tr, tw = R // 2, W
    else:
        tr, tw = R, 2 * LANES

    def body(p_ref, w_ref, m_ref, v_ref, g_out, d_out, m_out, v_out):
        g = p_ref[0].astype(F32)
        for d in range(1, N_DEV):
            g = g + p_ref[d].astype(F32)
        g_out[...] = g
        d_out[...], m_out[...], v_out[...] = _adamw_math(w_ref[...], g, m_ref[...], v_ref[...])

    row = pl.BlockSpec((tr, tw), lambda i, j: (i, j))
    out = jax.ShapeDtypeStruct((R, W), F32)
    return pl.pallas_call(
        body, name=name, grid=(R // tr, W // tw),
        in_specs=[pl.BlockSpec((N_DEV, tr, tw), lambda i, j: (0, i, j)), row, row, row],
        out_specs=[row] * 4, out_shape=[out] * 4,
        compiler_params=_cp(("parallel", "parallel"), VMEM_BIG),
    )(recv, w, m, v)


SMALL_EARLY = (("b_gate", 8), ("w_spatial", 512), ("b_spatial", 8), ("norm_post_mix", 8), ("norm_pre_ffn", 8),
               ("norm_post_ffn", 8), ("w_gate_up", 128), ("gla_norm", 64), ("sgu_ln_g", 64), ("sgu_ln_b", 64))
SMALL_EARLY_AT = {}
for _name, _rows in SMALL_EARLY:
    SMALL_EARLY_AT[_name] = sum(r for _, r in SMALL_EARLY[:len(SMALL_EARLY_AT)])
SMALL_EARLY_ROWS = sum(r for _, r in SMALL_EARLY)
SMALL_LATE_ROWS = 16


def _small_early_rows(grads):
    def rows(a, n_rows):
        a = a.reshape(-1, LANES)
        return jnp.pad(a, ((0, n_rows - a.shape[0]), (0, 0)))

    def device_major(a, n_rows):
        r, c = a.shape[0], a.shape[1] // N_DEV
        a = a.reshape(r, N_DEV, c).transpose(1, 0, 2)
        a = jnp.pad(a, ((0, 0), (0, n_rows // N_DEV - r), (0, LANES - c)))
        return a.reshape(n_rows, LANES)

    pieces = []
    for name, n_rows in SMALL_EARLY:
        g = grads[name]
        if name in SMALL_SHARDED:
            pieces.append(device_major(g.reshape(g.shape[0], -1) if g.ndim == 2 else g.reshape(g.shape[0], g.shape[-1]), n_rows))
        else:
            pieces.append(rows(g, n_rows))
    return jnp.concatenate(pieces, axis=0)


def _small_update_call(early, dg_pre_mix, loss_part, w, m, v, *, name):
    names = list(SMALL)
    n_p = len(names)
    E = SMALL_EARLY_ROWS

    def body(early_ref, dg1_ref, loss_ref, *rest):
        w_refs, m_refs, v_refs = [dict(zip(names, rest[i * n_p:(i + 1) * n_p])) for i in range(3)]
        outs = rest[3 * n_p:3 * n_p + 1 + 4 * n_p]
        loss_out = outs[0]
        g_out, d_out, m_out, v_out = [dict(zip(names, outs[1 + i * n_p:1 + (i + 1) * n_p])) for i in range(4)]
        got_early, got_late, late, tot, send_sems, recv_sems, local_sems = rest[3 * n_p + 1 + 4 * n_p:]

        for k in range(D_MODEL // LANES):
            late[k:k + 1, :] = dg1_ref[:, k * LANES:(k + 1) * LANES]
        late[8:16, :] = jnp.broadcast_to(loss_ref[...], (8, LANES))
        _gather_copies([early_ref, late], [got_early, got_late], send_sems, recv_sems, local_sems)
        acc, acc_late = got_early[0], got_late[0]
        for d in range(1, N_DEV):
            acc, acc_late = acc + got_early[d], acc_late + got_late[d]
        tot[0:E, :] = acc
        tot[E:E + SMALL_LATE_ROWS, :] = acc_late
        loss_out[...] = tot[E + 8:E + 9, :]

        x, y, c = _position()
        me = 4 * x + 2 * y + c

        def update(name, g, ix):
            g_out[name][ix] = g
            d_out[name][ix], m_out[name][ix], v_out[name][ix] = _adamw_math(
                w_refs[name][ix], g, m_refs[name][ix], v_refs[name][ix])

        for name in names:
            shape = w[name].shape
            if name in SMALL_SHARDED:
                per_dev = dict(SMALL_EARLY)[name] // N_DEV
                at = pl.multiple_of(SMALL_EARLY_AT[name] + me * per_dev, 8)
                g = tot[pl.ds(at, per_dev), :]
                update(name, g[:shape[1], :shape[2]], (0,))
            elif name == "w_spatial":
                for grp in range(GROUPS):
                    at = SMALL_EARLY_AT[name] + grp * SBLOCK
                    update(name, tot[at:at + SBLOCK, :], (0, grp))
            elif name == "b_spatial":
                at = SMALL_EARLY_AT[name]
                update(name, tot[at:at + GROUPS, :], (0,))
            else:
                at = E if name == "norm_pre_mix" else SMALL_EARLY_AT[name]
                for k in range(shape[1] // LANES):
                    update(name, tot[at + k:at + k + 1, :], (slice(None), pl.ds(k * LANES, LANES)))

    state = [s[n] for s in (w, m, v) for n in names]
    out_shapes = [jax.ShapeDtypeStruct((1, LANES), F32)] + [jax.ShapeDtypeStruct(w[n].shape, F32) for n in names] * 4
    outs = pl.pallas_call(
        body, name=name,
        in_specs=[VMEM_SPEC] * (3 + len(state)), out_specs=[VMEM_SPEC] * len(out_shapes), out_shape=out_shapes,
        scratch_shapes=[pltpu.VMEM((N_DEV, E, LANES), F32), pltpu.VMEM((N_DEV, SMALL_LATE_ROWS, LANES), F32),
                        pltpu.VMEM((SMALL_LATE_ROWS, LANES), F32), pltpu.VMEM((E + SMALL_LATE_ROWS, LANES), F32),
                        pltpu.SemaphoreType.DMA((14,)), pltpu.SemaphoreType.DMA((14,)), pltpu.SemaphoreType.DMA((2,))],
    )(early, dg_pre_mix, loss_part, *state)
    per_name = {n: tuple(outs[1 + i * n_p + j] for i in range(4)) for j, n in enumerate(names)}
    return outs[0], per_name


def _tile_rows(n_elems):
    return -(-n_elems // (8 * LANES)) * 8


def _pack_rows(parts, rows):
    pieces = []
    for p in parts:
        q = p.reshape(-1, LANES)
        pieces.append(jnp.pad(q, ((0, _tile_rows(p.size) - q.shape[0]), (0, 0))))
    buf = jnp.concatenate(pieces, axis=0)
    return jnp.pad(buf, ((0, rows - buf.shape[0]), (0, 0)))


def _unpack_rows(buf, shapes):
    out, off = [], 0
    for shp in shapes:
        n = 1
        for s in shp:
            n *= s
        out.append(buf[off:off + n // LANES].reshape(shp))
        off += _tile_rows(n)
    return out


def _in_w_blocks_call(a, dps, dpg, dw_a, dw_low, *, name, tk=TOKEN_TILE):
    S = a.shape[0]
    tk = min(tk, S)
    steps = S // tk
    n_s, n_g = dps.shape[1], dpg.shape[1]
    out_shape = (N_DEV, IN_BLK, D_MODEL)

    def body(a_ref, dps_ref, dpg_ref, dwa_ref, dwl_ref, out_ref, acc_s, acc_g):
        k = pl.program_id(0)

        @pl.when(k == 0)
        def _():
            acc_s[...] = jnp.zeros_like(acc_s)
            acc_g[...] = jnp.zeros_like(acc_g)

        av = a_ref[...]
        for src, acc, n in ((dps_ref, acc_s, n_s), (dpg_ref, acc_g, n_g)):
            for m0 in range(0, n, 1024):
                acc[m0:m0 + 1024, :] += _dot(src[:, m0:m0 + 1024], av, TN)

        @pl.when(k == steps - 1)
        def _():
            groups = ((dwa_ref, A_COLS), (dwl_ref, LOW_COLS), (acc_s, S_COLS), (acc_g, G_COLS))
            for d in range(N_DEV):
                lo, hi = IN_BLK * d, IN_BLK * (d + 1)
                for ref, (c0, c1) in groups:
                    s0, e0 = max(lo, c0), min(hi, c1)
                    if s0 < e0:
                        out_ref[d, s0 - lo:e0 - lo, :] = ref[s0 - c0:e0 - c0, :].astype(BF16)

    tiles = [((tk, D_MODEL), BF16), ((tk, n_s), BF16), ((tk, n_g), BF16)]
    resident = [(dw_a.shape, BF16), (dw_low.shape, BF16), (out_shape, BF16), ((n_s, D_MODEL), F32), ((n_g, D_MODEL), F32)]
    return pl.pallas_call(
        body, name=name, grid=(steps,),
        in_specs=[pl.BlockSpec((tk, D_MODEL), lambda k: (k, 0)), pl.BlockSpec((tk, n_s), lambda k: (k, 0)),
                  pl.BlockSpec((tk, n_g), lambda k: (k, 0)), _res(dw_a.shape), _res(dw_low.shape)],
        out_specs=_acc(out_shape),
        out_shape=jax.ShapeDtypeStruct(out_shape, BF16),
        scratch_shapes=[pltpu.VMEM((n_s, D_MODEL), F32), pltpu.VMEM((n_g, D_MODEL), F32)],
        compiler_params=_cp(("arbitrary",), _vmem_limit(tiles, resident)),
    )(a, dps, dpg, dw_a, dw_low)


def _local_step(x, target, W, scatter, finish):
    g1, g2, g3, g4 = [W[n].reshape(1, D_MODEL) for n in ("norm_pre_mix", "norm_post_mix", "norm_pre_ffn", "norm_post_ffn")]
    wfi, wfo = W["w_ffn_in"], W["w_ffn_out"].reshape(4, FF_BLK, D_MODEL)
    wg = jnp.pad(W["w_gate_up"], ((0, LANES - RANK), (0, 0))).astype(BF16)
    bgate = W["b_gate"].reshape(1, QK)
    gnorm = W["gla_norm"].reshape(HEADS, 1, DV)
    ln_g = W["sgu_ln_g"].reshape(GROUPS, 1, DG)
    ln_b = W["sgu_ln_b"].reshape(GROUPS, 1, DG)
    w_sp = W["w_spatial"]
    b_sp = W["b_spatial"].reshape(GROUPS, SBLOCK, 1)

    a, pa, alow, ps, pg, w_in_t = _in_proj_call(x, g1, W["w_in_blocks"], name="in_proj")
    y_gla, states = _gla_fwd_call(pa, alow, wg, bgate, gnorm, name="gla_fwd")
    y_sgu = _sgu_fwd_call(ps, ln_g, ln_b, w_sp, b_sp, name="sgu_fwd")
    t1, t2, merged, mix, x1, h = _mixer_tail_call(y_gla, y_sgu, pg, x, W["w_branch_gla"], W["w_branch_sgu"],
                                                  W["w_out"], g2, g3, name="mixer_tail")
    gu, z = _ffn_in_call(h, wfi, name="ffn_in")
    loss, dx2, dy, dg4 = _ffn_out_loss_call(z, wfo, x1, target, g4, name="ffn_out_loss")

    grads = {"norm_post_ffn": dg4}
    done = {}
    dgu = _ffn_out_bwd_call(dy, wfo, gu, name="ffn_out_bwd")
    dw_ffn_out = _weight_grads_call([z, dy], [(0, 1)], name="d_ffn_out_w")[0].reshape(N_DEV, D_FF // N_DEV, D_MODEL)
    dw_ffn_in = _ffn_in_grad_call(h, dgu, name="d_ffn_in_w")
    dgu, dw_ffn_out, dw_ffn_in = lax.optimization_barrier((dgu, dw_ffn_out, dw_ffn_in))
    ffn_received = scatter(("w_ffn_out", "w_ffn_in"), [dw_ffn_out, dw_ffn_in])
    dx1, dmix, grads["norm_pre_ffn"], grads["norm_post_mix"] = _ffn_in_bwd_call(dgu, wfi, dx2, x1, mix, g3, g2, name="ffn_in_bwd")
    dt1, dt2, dpg, dy_gla, dy_sgu = _mixer_bwd_call(dmix, t1, t2, pg, W["w_out"], W["w_branch_gla"], W["w_branch_sgu"],
                                                    name="mixer_bwd")
    rows = D_MODEL // N_DEV
    mixer_grads = _weight_grads_call([merged, dmix, y_gla, dt1, y_sgu, dt2], [(0, 1), (2, 3), (4, 5)], name="d_mixer_w")
    dy_gla, dy_sgu, mixer_grads = lax.optimization_barrier((dy_gla, dy_sgu, mixer_grads))
    mixer_received = scatter(("w_out", "w_branch_gla", "w_branch_sgu"),
                             [g.reshape(N_DEV, rows, D_MODEL) for g in mixer_grads])
    dpa, dlogit, dgn, dbg = _gla_bwd_call(pa, alow, wg, bgate, gnorm, states, dy_gla, name="gla_bwd")
    ffn_received, dpa, dlogit = lax.optimization_barrier((ffn_received, dpa, dlogit))
    dps, dlg, dlb, dwsp, dbsp = _sgu_bwd_call(ps, ln_g, ln_b, w_sp, b_sp, dy_sgu, name="sgu_bwd")
    mixer_received, dps = lax.optimization_barrier((mixer_received, dps))
    dlow = _mm(dlogit, wg, "nt", BF16, name="d_gate_up_x")
    dw_a, dw_low = _weight_grads_call([a, dpa, dlow], [(1, 0), (2, 0)], name="d_in_w_qkvr")
    dw_in = _in_w_blocks_call(a, dps, dpg, dw_a, dw_low, name="d_in_w_blocks")
    ffn_received, mixer_received, dw_in = lax.optimization_barrier((ffn_received, mixer_received, dw_in))
    in_received = scatter(("w_in",), [dw_in])
    done.update(finish(ffn_received))
    done.update(finish(mixer_received))
    dwg = _mm(alow, dlogit, "tn", F32, name="d_gate_up_w")
    grad_x, grads["norm_pre_mix"] = _in_proj_bwd_call(dpa, dlow, dps, dpg, w_in_t, x, dx1, g1, name="in_proj_bwd")

    grads["w_gate_up"] = dwg[:RANK]
    grads["b_gate"] = dbg
    grads["gla_norm"] = dgn
    grads["sgu_ln_g"] = dlg
    grads["sgu_ln_b"] = dlb
    grads["w_spatial"] = dwsp
    grads["b_spatial"] = dbsp
    done.update(finish(in_received))
    return loss, grad_x, grads, done


WEIGHTS = ("norm_pre_mix", "w_in", "w_gate_up", "b_gate", "gla_norm", "sgu_ln_g", "sgu_ln_b", "w_spatial",
           "b_spatial", "w_branch_gla", "w_branch_sgu", "w_out", "norm_post_mix", "norm_pre_ffn", "w_ffn_in",
           "w_ffn_out", "norm_post_ffn")
BIG = ("w_in", "w_branch_gla", "w_branch_sgu", "w_out", "w_ffn_in", "w_ffn_out")
MIXER = ("w_branch_gla", "w_branch_sgu", "w_out")
FFN = ("w_ffn_in", "w_ffn_out")
COLUMN_SHARDED = ("w_in", "w_ffn_in")
LATE_SCATTER_WAVES = 4
SMALL = tuple(n for n in WEIGHTS if n not in BIG)
SMALL_SHARDED = ("w_gate_up", "gla_norm", "sgu_ln_g", "sgu_ln_b")
SMALL_FULL = {"norm_pre_mix": (1024,), "w_gate_up": (16, 512), "b_gate": (512,), "gla_norm": (4, 256),
              "sgu_ln_g": (4, 256), "sgu_ln_b": (4, 256), "w_spatial": (4, 128, 128), "b_spatial": (4, 128),
              "norm_post_mix": (1024,), "norm_pre_ffn": (1024,), "norm_post_ffn": (1024,)}
SMALL_GRAD_ROWS = 648
SMALL_STATE_ROWS = 600
SMALL_GATHER_ROWS = 32


def kernel(x, norm_pre_mix, w_in, w_gate_up, b_gate, gla_norm, sgu_ln_g, sgu_ln_b, w_spatial, b_spatial, w_branch_gla, w_branch_sgu, w_out, norm_post_mix, norm_pre_ffn, w_ffn_in, w_ffn_out, norm_post_ffn, loss_target, m_norm_pre_mix, m_w_in, m_w_gate_up, m_b_gate, m_gla_norm, m_sgu_ln_g, m_sgu_ln_b, m_w_spatial, m_b_spatial, m_w_branch_gla, m_w_branch_sgu, m_w_out, m_norm_post_mix, m_norm_pre_ffn, m_w_ffn_in, m_w_ffn_out, m_norm_post_ffn, v_norm_pre_mix, v_w_in, v_w_gate_up, v_b_gate, v_gla_norm, v_sgu_ln_g, v_sgu_ln_b, v_w_spatial, v_b_spatial, v_w_branch_gla, v_w_branch_sgu, v_w_out, v_norm_post_mix, v_norm_pre_ffn, v_w_ffn_in, v_w_ffn_out, v_norm_post_ffn):
    given = dict(locals())
    def local(a, n):
        return a[0].T if n in COLUMN_SHARDED else a[0]

    w = {n: local(given[n], n) for n in WEIGHTS}
    m = {n: local(given["m_" + n], n) for n in WEIGHTS}
    v = {n: local(given["v_" + n], n) for n in WEIGHTS}
    xs, target = x[0], loss_target[0]
    me = 4 * lax.axis_index("x") + 2 * lax.axis_index("y") + lax.axis_index("c")

    small_shard = _pack_rows([w[n] for n in SMALL_SHARDED], SMALL_GATHER_ROWS)
    first = _all_gather_hbm([w["w_in"].astype(BF16), small_shard], name="gather_w_in")
    rest, first = lax.optimization_barrier(([w[n].astype(BF16) for n in MIXER + FFN], first))
    rest_blocks = _all_gather_async(rest, name="gather_rest", collective_id=1)
    W = {n: w[n] for n in SMALL if n not in SMALL_SHARDED}
    blocks = dict(zip(MIXER + FFN, rest_blocks))
    for n in ("w_branch_gla", "w_branch_sgu", "w_out", "w_ffn_out"):
        W[n] = blocks[n].reshape(-1, D_MODEL)
    W["w_ffn_in"] = blocks["w_ffn_in"]
    W["w_in_blocks"] = first[0]
    small_blocks = first[1]
    off = 0
    for n in SMALL_SHARDED:
        r, c = w[n].shape
        blk = small_blocks[:, off:off + r * c // LANES].reshape(N_DEV, r, c)
        W[n] = blk.transpose(1, 0, 2).reshape(r, N_DEV * c)
        off += _tile_rows(r * c)

    scatter_ids = iter((3, 4, 5))

    def scatter(names, parts):
        waves = LATE_SCATTER_WAVES if "w_in" in names else 1
        got = _scatter_blocks_async(parts, name="scatter_" + "_".join(names), collective_id=next(scatter_ids),
                                    waves=waves)
        return dict(zip(names, got))

    def finish(received):
        return {n: _adamw_reduce_call(r, w[n], m[n], v[n], name="adamw_" + n) for n, r in received.items()}

    loss_part, grad_x, grads, big_done = _local_step(xs, target, W, scatter, finish)

    loss_row, small_done = _small_update_call(
        _small_early_rows(grads), grads["norm_pre_mix"], loss_part,
        *[{n: given[prefix + n] for n in SMALL} for prefix in ("", "m_", "v_")], name="small_update")

    def pick(i):
        return [small_done[n][i] if n in SMALL else (big_done[n][i].T if n in COLUMN_SHARDED else big_done[n][i])[None]
                for n in WEIGHTS]

    return (loss_row[0, 0], grad_x[None], *pick(0), *pick(1), *pick(2), *pick(3))
```

```python
import jax
import jax.numpy as jnp
from jax import lax
from jax.experimental import pallas as pl
from jax.experimental.pallas import tpu as pltpu
from jax.experimental.pallas import tpu_sc as plsc

F32 = jnp.float32
BF16 = jnp.bfloat16

D_MODEL = 1024
N_DEV = 8
CHUNK = 64
HEADS = 4
DK = 128
DV = 256
QK = HEADS * DK
GV = HEADS * DV
RANK = 16
GROUPS = 4
SBLOCK = 128
DG = 256
D_FF = 2816
FF_BLK = 704
EPS = 1e-6
Q_SCALE = DK ** -0.5
LANES = 128
VMEM_BIG = 48 * 1024 * 1024

D_IN = 7184
IN_BLK = 898
A_COLS = (0, 3072)
LOW_COLS = (3072, 3088)
S_COLS = (3088, 5136)
G_COLS = (5136, 7184)

ADAM_LR = 0.001
ADAM_B1 = 0.9
ADAM_B2 = 0.999
ADAM_EPS = 1e-08
ADAM_WD = 0.01
ADAM_STEP = 10

MESH = pl.DeviceIdType.MESH
ANY = pl.BlockSpec(memory_space=pl.ANY)
VMEM_SPEC = pl.BlockSpec(memory_space=pltpu.VMEM)


def _cp(sem=None, vmem=None):
    return pltpu.CompilerParams(dimension_semantics=sem, vmem_limit_bytes=vmem)


def _sig(x):
    return 0.5 * jnp.tanh(0.5 * x) + 0.5


GELU_C = 0.7978845608028654
GELU_A = 0.044715


def _gelu(x):
    t = jnp.tanh((GELU_C * x) * (1.0 + GELU_A * (x * x)))
    return (0.5 * x) * (1.0 + t)


def _gelu_and_grad(x):
    x2 = x * x
    t = jnp.tanh((GELU_C * x) * (1.0 + GELU_A * x2))
    one_t = 1.0 + t
    hx = 0.5 * x
    grad = 0.5 * one_t + (hx * (1.0 - t * t)) * (GELU_C + (3.0 * GELU_A * GELU_C) * x2)
    return hx * one_t, grad


def _logsig(x):
    return jnp.minimum(x, 0.0) - jnp.log1p(jnp.exp(-jnp.abs(x)))


def _dot(a, b, dims):
    return lax.dot_general(a, b, (dims, ((), ())), preferred_element_type=F32)


NN = ((1,), (0,))
NT = ((1,), (1,))
TN = ((0,), (0,))


def _exact_mask_dot(mask_bf16, x):
    hi = x.astype(BF16)
    r1 = x - hi.astype(F32)
    mid = r1.astype(BF16)
    lo = (r1 - mid.astype(F32)).astype(BF16)
    return _dot(mask_bf16, hi, NN) + _dot(mask_bf16, mid, NN) + _dot(mask_bf16, lo, NN)


def _pick_tile(dim, target):
    if dim <= target:
        return dim
    best = None
    for t in range(LANES, int(1.4 * target) + 1, LANES):
        if dim % t == 0:
            best = t
    assert best is not None, (dim, target)
    return best


def _mm_call(a, b, *, name, grid, a_spec, b_spec, o_spec, out_shape, dims, acc_shape):
    nk = grid[2]

    def body(a_ref, b_ref, o_ref, *acc):
        part = _dot(a_ref[...], b_ref[...], dims)
        if nk == 1:
            o_ref[...] = part.astype(o_ref.dtype)
        else:
            acc_ref = acc[0]
            k = pl.program_id(2)

            @pl.when(k == 0)
            def _():
                acc_ref[...] = part

            @pl.when(k > 0)
            def _():
                acc_ref[...] += part

            @pl.when(k == nk - 1)
            def _():
                o_ref[...] = acc_ref[...].astype(o_ref.dtype)

    return pl.pallas_call(
        body, name=name, grid=grid, in_specs=[a_spec, b_spec], out_specs=o_spec, out_shape=out_shape,
        scratch_shapes=[] if nk == 1 else [pltpu.VMEM(acc_shape, F32)],
        compiler_params=_cp(("parallel", "parallel", "arbitrary"), VMEM_BIG),
    )(a, b)


def _mm(a, b, mode, out_dtype, *, name, tm=512, tn=1024, tk=1024):
    if mode == "nn":
        (M, K), (_, N) = a.shape, b.shape
    elif mode == "nt":
        (M, K), (N, _) = a.shape, b.shape
    else:
        (K, M), (_, N) = a.shape, b.shape
    tm, tn, tk = _pick_tile(M, tm), _pick_tile(N, tn), _pick_tile(K, tk)
    if mode == "nn":
        a_spec = pl.BlockSpec((tm, tk), lambda i, j, k: (i, k))
        b_spec = pl.BlockSpec((tk, tn), lambda i, j, k: (k, j))
        dims = NN
    elif mode == "nt":
        a_spec = pl.BlockSpec((tm, tk), lambda i, j, k: (i, k))
        b_spec = pl.BlockSpec((tn, tk), lambda i, j, k: (j, k))
        dims = NT
    else:
        a_spec = pl.BlockSpec((tk, tm), lambda i, j, k: (k, i))
        b_spec = pl.BlockSpec((tk, tn), lambda i, j, k: (k, j))
        dims = TN
    return _mm_call(a, b, name=name, grid=(M // tm, N // tn, K // tk), a_spec=a_spec, b_spec=b_spec,
                    o_spec=pl.BlockSpec((tm, tn), lambda i, j, k: (i, j)),
                    out_shape=jax.ShapeDtypeStruct((M, N), out_dtype), dims=dims, acc_shape=(tm, tn))


ROW_TILE = 512
SUB_ROWS = 256


def _sub_tiles(T):
    return [pl.ds(r0, min(SUB_ROWS, T)) for r0 in range(0, T, SUB_ROWS)]


def _rt(T, W, c=0):
    return pl.BlockSpec((T, W), lambda i: (i, c))


def _rt3(nb, T, W):
    return pl.BlockSpec((nb, T, W), lambda i: (0, i, 0))


def _res(shape):
    nd = len(shape)
    return pl.BlockSpec(tuple(shape), lambda i: (0,) * nd, pipeline_mode=pl.Buffered(1))


def _acc(shape):
    nd = len(shape)
    return pl.BlockSpec(tuple(shape), lambda i: (0,) * nd, pipeline_mode=pl.Buffered(1))


def _nbytes(shape, dtype):
    n = jnp.dtype(dtype).itemsize
    for s in shape:
        n *= s
    return n


def _vmem_limit(tiles, resident, temps=16 * 1024 * 1024):
    need = 2 * sum(_nbytes(s, d) for s, d in tiles) + sum(_nbytes(s, d) for s, d in resident) + temps
    return min(need, 60 * 1024 * 1024)


def _tok_call(body, *, name, S, T, ins, outs, semantics="parallel"):
    tiles = [(spec.block_shape, a.dtype) for a, spec, kind in ins + outs if kind == "tile"]
    resident = [(a.shape, a.dtype) for a, spec, kind in ins + outs if kind == "res"]
    return pl.pallas_call(
        body, name=name, grid=(S // T,),
        in_specs=[spec for _, spec, _ in ins], out_specs=[spec for _, spec, _ in outs],
        out_shape=[jax.ShapeDtypeStruct(a.shape, a.dtype) for a, _, _ in outs],
        compiler_params=_cp((semantics,), _vmem_limit(tiles, resident)),
    )(*[a for a, _, _ in ins])


def _tile(a, spec):
    return (a, spec, "tile")


def _whole(a):
    return (a, _res(a.shape), "res")


def _out_tile(shape, dtype, spec):
    return (jax.ShapeDtypeStruct(shape, dtype), spec, "tile")


def _out_acc(shape, dtype=F32):
    return (jax.ShapeDtypeStruct(shape, dtype), _acc(shape), "res")


def _rms_stats(x):
    r = lax.rsqrt(jnp.mean(x * x, axis=-1, keepdims=True) + EPS)
    return r, x * r


def _rms_bwd(xh, r, g, dy):
    dxh = dy * g
    dx = r * (dxh - xh * jnp.mean(dxh * xh, axis=-1, keepdims=True))
    dg = jnp.sum(dy * xh, axis=0, keepdims=True)
    return dx, dg


def _accum(ref, val):
    @pl.when(pl.program_id(0) == 0)
    def _():
        ref[...] = val

    @pl.when(pl.program_id(0) > 0)
    def _():
        ref[...] += val


def _dot_rows_t(a, w_ref, row0, o_ref, chunk=1024):
    n = o_ref.shape[1]
    for n0 in range(0, n, chunk):
        n1 = min(n, n0 + chunk)
        o_ref[:, n0:n1] = _dot(a, w_ref[row0 + n0:row0 + n1, :], NT).astype(o_ref.dtype)


def _in_proj_call(x, g1, w_blocks, *, name, T=ROW_TILE):
    S = x.shape[0]

    def body(x_ref, g_ref, blk_ref, a_ref, pa_ref, al_ref, ps_ref, pg_ref, w_ref):
        @pl.when(pl.program_id(0) == 0)
        def _():
            for d in range(N_DEV):
                w_ref[d * IN_BLK:(d + 1) * IN_BLK, :] = blk_ref[d]

        _, xh = _rms_stats(x_ref[...])
        a = (xh * g_ref[...]).astype(BF16)
        a_ref[...] = a
        _dot_rows_t(a, w_ref, A_COLS[0], pa_ref)
        _dot_rows_t(a, w_ref, LOW_COLS[0], al_ref)
        _dot_rows_t(a, w_ref, S_COLS[0], ps_ref)
        _dot_rows_t(a, w_ref, G_COLS[0], pg_ref)

    widths = (D_MODEL, A_COLS[1] - A_COLS[0], LANES, S_COLS[1] - S_COLS[0], G_COLS[1] - G_COLS[0])
    return _tok_call(
        body, name=name, S=S, T=T, semantics="arbitrary",
        ins=[_tile(x, _rt(T, D_MODEL)), _whole(g1), _whole(w_blocks)],
        outs=[_out_tile((S, w), BF16, _rt(T, w)) for w in widths] + [_out_acc((D_IN, D_MODEL), BF16)])


def _mixer_tail_call(y_gla, y_sgu, pg, x, w_bg, w_bs, w_out, g2, g3, *, name, T=ROW_TILE):
    S = x.shape[0]

    def body(yg_ref, ys_ref, pg_ref, x_ref, wbg_ref, wbs_ref, wo_ref, g2_ref, g3_ref,
             t1_ref, t2_ref, mg_ref, mix_ref, x1_ref, h_ref):
        for rows in _sub_tiles(T):
            t1 = _dot(yg_ref[rows, :], wbg_ref[...], NN)
            t2 = _dot(ys_ref[rows, :], wbs_ref[...], NN)
            t1_ref[rows, :] = t1.astype(BF16)
            t2_ref[rows, :] = t2.astype(BF16)
            sg = _sig(pg_ref[rows, pl.ds(0, D_MODEL)].astype(F32))
            ss = _sig(pg_ref[rows, pl.ds(D_MODEL, D_MODEL)].astype(F32))
            merged = (sg * t1 + ss * t2).astype(BF16)
            mg_ref[rows, :] = merged
            mix = _dot(merged, wo_ref[...], NN)
            mix_ref[rows, :] = mix
            _, mh = _rms_stats(mix)
            x1 = x_ref[rows, :] + mh * g2_ref[...]
            x1_ref[rows, :] = x1
            _, xh = _rms_stats(x1)
            h_ref[rows, :] = (xh * g3_ref[...]).astype(BF16)

    row = _rt(T, D_MODEL)
    b16 = lambda: _out_tile((S, D_MODEL), BF16, row)
    f32 = lambda: _out_tile((S, D_MODEL), F32, row)
    return _tok_call(
        body, name=name, S=S, T=T,
        ins=[_tile(y_gla, row), _tile(y_sgu, row), _tile(pg, _rt(T, 2 * D_MODEL)), _tile(x, row),
             _whole(w_bg), _whole(w_bs), _whole(w_out), _whole(g2), _whole(g3)],
        outs=[b16(), b16(), b16(), f32(), f32(), b16()])


def _ffn_in_call(h, wfi, *, name, T=ROW_TILE):
    S = h.shape[0]

    def body(h_ref, w_ref, gu_ref, z_ref):
        hv = h_ref[...]
        for d in range(4):
            gate = _dot(hv, w_ref[d], NT)
            up = _dot(hv, w_ref[d + 4], NT)
            gu_ref[d] = gate.astype(BF16)
            gu_ref[d + 4] = up.astype(BF16)
            z_ref[d] = (gate * _sig(gate) * up).astype(BF16)

    return _tok_call(
        body, name=name, S=S, T=T,
        ins=[_tile(h, _rt(T, D_MODEL)), _whole(wfi)],
        outs=[_out_tile((N_DEV, S, FF_BLK), BF16, _rt3(N_DEV, T, FF_BLK)),
              _out_tile((4, S, FF_BLK), BF16, _rt3(4, T, FF_BLK))])


def _ffn_out_loss_call(z, wfo, x1, target, g4, *, name, T=ROW_TILE):
    S = x1.shape[0]

    def body(z_ref, w_ref, x1_ref, t_ref, g4_ref, loss_ref, dx2_ref, dy_ref, dg4_ref):
        loss = jnp.zeros((1, 1), F32)
        dg4 = jnp.zeros((1, D_MODEL), F32)
        for rows in _sub_tiles(T):
            y = _dot(z_ref[0, rows, :], w_ref[0], NN)
            for d in range(1, 4):
                y = y + _dot(z_ref[d, rows, :], w_ref[d], NN)
            r, yh = _rms_stats(y)
            diff = x1_ref[rows, :] + yh * g4_ref[...] - t_ref[rows, :]
            loss = loss + 0.5 * jnp.sum(jnp.mean(diff * diff, axis=-1, keepdims=True), axis=0, keepdims=True)
            dx2 = diff * (1.0 / D_MODEL)
            dx2_ref[rows, :] = dx2
            dy, dg = _rms_bwd(yh, r, g4_ref[...], dx2)
            dy_ref[rows, :] = dy.astype(BF16)
            dg4 = dg4 + dg
        _accum(loss_ref, jnp.broadcast_to(loss, (1, LANES)))
        _accum(dg4_ref, dg4)

    row = _rt(T, D_MODEL)
    return _tok_call(
        body, name=name, S=S, T=T, semantics="arbitrary",
        ins=[_tile(z, _rt3(4, T, FF_BLK)), _whole(wfo), _tile(x1, row), _tile(target, row), _whole(g4)],
        outs=[_out_acc((1, LANES)), _out_tile((S, D_MODEL), F32, row), _out_tile((S, D_MODEL), BF16, row),
              _out_acc((1, D_MODEL))])


def _ffn_out_bwd_call(dy, wfo, gu, *, name, T=ROW_TILE):
    S = dy.shape[0]

    def body(dy_ref, w_ref, gu_ref, dgu_ref):
        dyv = dy_ref[...]
        for d in range(4):
            dz = _dot(dyv, w_ref[d], NT)
            gt = gu_ref[d].astype(F32)
            up = gu_ref[d + 4].astype(F32)
            s = _sig(gt)
            dgu_ref[d] = (dz * up * s * (1.0 + gt * (1.0 - s))).astype(BF16)
            dgu_ref[d + 4] = (dz * gt * s).astype(BF16)

    blocks = _rt3(N_DEV, T, FF_BLK)
    return _tok_call(
        body, name=name, S=S, T=T,
        ins=[_tile(dy, _rt(T, D_MODEL)), _whole(wfo), _tile(gu, blocks)],
        outs=[_out_tile((N_DEV, S, FF_BLK), BF16, blocks)])[0]


def _ffn_in_bwd_call(dgu, wfi, dx2, x1, mix, g3, g2, *, name, T=ROW_TILE):
    S = x1.shape[0]

    def body(dgu_ref, w_ref, dx2_ref, x1_ref, mix_ref, g3_ref, g2_ref, dx1_ref, dmix_ref, dg3_ref, dg2_ref):
        dg3 = jnp.zeros((1, D_MODEL), F32)
        dg2 = jnp.zeros((1, D_MODEL), F32)
        for rows in _sub_tiles(T):
            dh = _dot(dgu_ref[0, rows, :], w_ref[0], NN)
            for d in range(1, N_DEV):
                dh = dh + _dot(dgu_ref[d, rows, :], w_ref[d], NN)
            r3, xh = _rms_stats(x1_ref[rows, :])
            d3, g = _rms_bwd(xh, r3, g3_ref[...], dh)
            dg3 = dg3 + g
            dx1 = dx2_ref[rows, :] + d3
            dx1_ref[rows, :] = dx1
            r2, mh = _rms_stats(mix_ref[rows, :])
            dmix, g = _rms_bwd(mh, r2, g2_ref[...], dx1)
            dg2 = dg2 + g
            dmix_ref[rows, :] = dmix.astype(BF16)
        _accum(dg3_ref, dg3)
        _accum(dg2_ref, dg2)

    row = _rt(T, D_MODEL)
    return _tok_call(
        body, name=name, S=S, T=T, semantics="arbitrary",
        ins=[_tile(dgu, _rt3(N_DEV, T, FF_BLK)), _whole(wfi), _tile(dx2, row), _tile(x1, row), _tile(mix, row),
             _whole(g3), _whole(g2)],
        outs=[_out_tile((S, D_MODEL), F32, row), _out_tile((S, D_MODEL), BF16, row),
              _out_acc((1, D_MODEL)), _out_acc((1, D_MODEL))])


def _mixer_bwd_call(dmix, t1, t2, pg, w_out, w_bg, w_bs, *, name, T=ROW_TILE):
    S = dmix.shape[0]

    def body(dmix_ref, t1_ref, t2_ref, pg_ref, wo_ref, wbg_ref, wbs_ref, dt1_ref, dt2_ref, dpg_ref, dyg_ref, dys_ref):
        gg, gs = pl.ds(0, D_MODEL), pl.ds(D_MODEL, D_MODEL)
        for rows in _sub_tiles(T):
            dm = _dot(dmix_ref[rows, :], wo_ref[...], NT)
            sg = _sig(pg_ref[rows, gg].astype(F32))
            ss = _sig(pg_ref[rows, gs].astype(F32))
            dt1 = (dm * sg).astype(BF16)
            dt2 = (dm * ss).astype(BF16)
            dt1_ref[rows, :] = dt1
            dt2_ref[rows, :] = dt2
            dpg_ref[rows, gg] = (dm * t1_ref[rows, :].astype(F32) * sg * (1.0 - sg)).astype(BF16)
            dpg_ref[rows, gs] = (dm * t2_ref[rows, :].astype(F32) * ss * (1.0 - ss)).astype(BF16)
            dyg_ref[rows, :] = _dot(dt1, wbg_ref[...], NT).astype(BF16)
            dys_ref[rows, :] = _dot(dt2, wbs_ref[...], NT).astype(BF16)

    row = _rt(T, D_MODEL)
    wide = _rt(T, 2 * D_MODEL)
    b16 = lambda: _out_tile((S, D_MODEL), BF16, row)
    return _tok_call(
        body, name=name, S=S, T=T,
        ins=[_tile(dmix, row), _tile(t1, row), _tile(t2, row), _tile(pg, wide), _whole(w_out), _whole(w_bg), _whole(w_bs)],
        outs=[b16(), b16(), _out_tile((S, 2 * D_MODEL), BF16, wide), b16(), b16()])


def _in_proj_bwd_call(dpa, dlow, dps, dpg, w_in_t, x, dx1, g1, *, name, T=ROW_TILE):
    S = x.shape[0]

    def body(dpa_ref, dl_ref, dps_ref, dpg_ref, w_ref, x_ref, dx1_ref, g1_ref, gx_ref, dg1_ref):
        da = (_dot(dpa_ref[...], w_ref[A_COLS[0]:A_COLS[1], :], NN)
              + _dot(dl_ref[...], w_ref[LOW_COLS[0]:LOW_COLS[0] + LANES, :], NN)
              + _dot(dps_ref[...], w_ref[S_COLS[0]:S_COLS[1], :], NN)
              + _dot(dpg_ref[...], w_ref[G_COLS[0]:G_COLS[1], :], NN))
        r, xh = _rms_stats(x_ref[...])
        dxa, dg = _rms_bwd(xh, r, g1_ref[...], da)
        gx_ref[...] = dx1_ref[...] + dxa
        _accum(dg1_ref, dg)

    row = _rt(T, D_MODEL)
    return _tok_call(
        body, name=name, S=S, T=T, semantics="arbitrary",
        ins=[_tile(dpa, _rt(T, dpa.shape[1])), _tile(dlow, _rt(T, dlow.shape[1])), _tile(dps, _rt(T, dps.shape[1])),
             _tile(dpg, _rt(T, dpg.shape[1])), _whole(w_in_t), _tile(x, row), _tile(dx1, row), _whole(g1)],
        outs=[_out_tile((S, D_MODEL), F32, row), _out_acc((1, D_MODEL))])


TOKEN_TILE = 512


def _weight_grads_part(arrays, pairs, *, tk=TOKEN_TILE, out_dtype=BF16):
    S = arrays[0].shape[-2]
    tk = min(tk, S)
    n_in = len(arrays)

    def out_shape(i, j):
        a, b = arrays[i], arrays[j]
        if a.ndim == 3:
            return (a.shape[0], a.shape[2], b.shape[1])
        if b.ndim == 3:
            return (b.shape[0], a.shape[1], b.shape[2])
        return (a.shape[1], b.shape[1])

    shapes = [out_shape(i, j) for i, j in pairs]

    in_place = out_dtype == F32

    def body(ins, outs, accs):
        k = pl.program_id(0)
        if in_place:
            accs = outs

        @pl.when(k == 0)
        def _():
            for acc in accs:
                acc[...] = jnp.zeros_like(acc)

        for (i, j), acc in zip(pairs, accs):
            a_ref, b_ref = ins[i], ins[j]
            if len(a_ref.shape) == 3:
                for n in range(a_ref.shape[0]):
                    acc[n] += _dot(a_ref[n], b_ref[...], TN)
            elif len(b_ref.shape) == 3:
                a = a_ref[...]
                for n in range(b_ref.shape[0]):
                    acc[n] += _dot(a, b_ref[n], TN)
            else:
                b = b_ref[...]
                for m0 in range(0, a_ref.shape[1], 1024):
                    m1 = min(a_ref.shape[1], m0 + 1024)
                    acc[m0:m1, :] += _dot(a_ref[:, m0:m1], b, TN)

        if not in_place:
            @pl.when(k == S // tk - 1)
            def _():
                for out, acc in zip(outs, accs):
                    out[...] = acc[...].astype(out.dtype)

    def in_spec(a):
        if a.ndim == 3:
            return pl.BlockSpec((a.shape[0], tk, a.shape[2]), lambda k: (0, k, 0))
        return pl.BlockSpec((tk, a.shape[1]), lambda k: (k, 0))

    return dict(
        body=body, steps=S // tk, arrays=list(arrays),
        in_specs=[in_spec(a) for a in arrays],
        out_specs=[_acc(s) for s in shapes],
        out_shapes=[jax.ShapeDtypeStruct(s, out_dtype) for s in shapes],
        scratch=[] if in_place else [pltpu.VMEM(s, F32) for s in shapes],
        tiles=[(in_spec(a).block_shape, a.dtype) for a in arrays],
        resident=[(s, F32) for s in shapes] + ([] if in_place else [(s, BF16) for s in shapes]))


def _weight_grads_call(arrays, pairs, *, name, tk=TOKEN_TILE):
    part = _weight_grads_part(arrays, pairs, tk=tk)
    n_in, n_out = len(part["arrays"]), len(part["out_shapes"])

    def body(*refs):
        part["body"](refs[:n_in], refs[n_in:n_in + n_out], refs[n_in + n_out:])

    return pl.pallas_call(
        body, name=name, grid=(part["steps"],),
        in_specs=part["in_specs"], out_specs=part["out_specs"], out_shape=part["out_shapes"],
        scratch_shapes=part["scratch"],
        compiler_params=_cp(("arbitrary",), _vmem_limit(part["tiles"], part["resident"])),
    )(*part["arrays"])


def _with_rider(body, n_in, n_out, n_scratch, rider):
    if rider is None:
        return body
    r_in, r_out = len(rider["arrays"]), len(rider["out_shapes"])

    def both(*refs):
        ins, rest = refs[:n_in + r_in], refs[n_in + r_in:]
        outs, scratch = rest[:n_out + r_out], rest[n_out + r_out:]
        body(*ins[:n_in], *outs[:n_out], *scratch[:n_scratch])
        rider["body"](ins[n_in:], outs[n_out:], scratch[n_scratch:])

    return both


def _rider_lists(rider):
    if rider is None:
        return [], [], [], [], [], [], []
    return (rider["arrays"], rider["in_specs"], rider["out_specs"], rider["out_shapes"], rider["scratch"],
            rider["tiles"], rider["resident"])


def _ffn_in_grad_call(h, dgu, *, name, tk=TOKEN_TILE):
    S = h.shape[0]
    tk = min(tk, S)
    nb = 4
    shape = (nb, FF_BLK, D_MODEL)

    def body(h_ref, dgu_ref, out_ref, acc):
        k = pl.program_id(1)

        @pl.when(k == 0)
        def _():
            acc[...] = jnp.zeros_like(acc)

        hv = h_ref[...]
        for n in range(nb):
            acc[n] += _dot(dgu_ref[n], hv, TN)

        @pl.when(k == S // tk - 1)
        def _():
            out_ref[...] = acc[...].astype(out_ref.dtype)

    tiles = [((tk, D_MODEL), BF16), ((nb, tk, FF_BLK), BF16), (shape, BF16)]
    return pl.pallas_call(
        body, name=name, grid=(N_DEV // nb, S // tk),
        in_specs=[pl.BlockSpec((tk, D_MODEL), lambda g, k: (k, 0)),
                  pl.BlockSpec((nb, tk, FF_BLK), lambda g, k: (g, k, 0))],
        out_specs=pl.BlockSpec(shape, lambda g, k: (g, 0, 0)),
        out_shape=jax.ShapeDtypeStruct((N_DEV, FF_BLK, D_MODEL), BF16),
        scratch_shapes=[pltpu.VMEM(shape, F32)],
        compiler_params=_cp(("parallel", "arbitrary"), _vmem_limit(tiles, [(shape, F32)])),
    )(h, dgu)


GLA_TILE = 256
Q_OFF, K_OFF, V_OFF, R_OFF = 0, QK, 2 * QK, 2 * QK + GV


def _tri(lower):
    r = lax.broadcasted_iota(jnp.int32, (CHUNK, CHUNK), 0)
    c = lax.broadcasted_iota(jnp.int32, (CHUNK, CHUNK), 1)
    return jnp.where((r >= c) if lower else (c >= r), 1.0, 0.0).astype(BF16)


def _gla_fwd_call(pa, alow, wg, bgate, gnorm, *, name):
    S = pa.shape[0]
    Tg = min(GLA_TILE, S)
    cb = Tg // CHUNK

    def body(pa_ref, al_ref, wg_ref, bg_ref, gn_ref, y_ref, st_ref, state):
        @pl.when(pl.program_id(0) == 0)
        def _():
            state[...] = jnp.zeros_like(state)

        logit = _dot(al_ref[...], wg_ref[...], NN) + bg_ref[...]
        ls = _logsig(logit) * (1.0 / 16.0)
        tri = _tri(True)
        for c in range(cb):
            rows = pl.ds(c * CHUNK, CHUNK)
            cum = _exact_mask_dot(tri, ls[c * CHUNK:(c + 1) * CHUNK])
            tot = cum[CHUNK - 1:CHUNK]
            kd = (pa_ref[rows, pl.ds(K_OFF, QK)].astype(F32) * jnp.exp(tot - cum)).astype(BF16)
            decay = jnp.exp(tot)
            for h in range(HEADS):
                lanes = slice(h * DK, (h + 1) * DK)
                new = state[h] * decay[:, lanes] + _dot(pa_ref[rows, pl.ds(V_OFF + h * DV, DV)], kd[:, lanes], TN)
                state[h] = new
                st_ref[c, h] = new
        for c in range(cb):
            rows = pl.ds(c * CHUNK, CHUNK)
            for h in range(HEADS):
                qs = (pa_ref[rows, pl.ds(Q_OFF + h * DK, DK)].astype(F32) * Q_SCALE).astype(BF16)
                o = _dot(qs, st_ref[c, h].astype(BF16), NT)
                rs = lax.rsqrt(jnp.mean(o * o, axis=-1, keepdims=True) + EPS)
                rr = pa_ref[rows, pl.ds(R_OFF + h * DV, DV)].astype(F32)
                y_ref[rows, pl.ds(h * DV, DV)] = (o * rs * gn_ref[h] * (rr * _sig(rr))).astype(BF16)

    return pl.pallas_call(
        body, name=name, grid=(S // Tg,),
        in_specs=[
            pl.BlockSpec((Tg, 2 * QK + 2 * GV), lambda t: (t, 0)),
            pl.BlockSpec((Tg, LANES), lambda t: (t, 0)),
            pl.BlockSpec((LANES, QK), lambda t: (0, 0)),
            pl.BlockSpec((1, QK), lambda t: (0, 0)),
            pl.BlockSpec((HEADS, 1, DV), lambda t: (0, 0, 0)),
        ],
        out_specs=[
            pl.BlockSpec((Tg, GV), lambda t: (t, 0)),
            pl.BlockSpec((cb, HEADS, DV, DK), lambda t: (t, 0, 0, 0)),
        ],
        out_shape=[jax.ShapeDtypeStruct((S, GV), BF16),
                   jax.ShapeDtypeStruct((S // CHUNK, HEADS, DV, DK), F32)],
        scratch_shapes=[pltpu.VMEM((HEADS, DV, DK), F32)],
        compiler_params=_cp(("arbitrary",), VMEM_BIG),
    )(pa, alow, wg, bgate, gnorm)


def _gla_bwd_call(pa, alow, wg, bgate, gnorm, states, dy, *, name, rider=None):
    S = pa.shape[0]
    Tg = min(GLA_TILE, S)
    cb = Tg // CHUNK
    nt = S // Tg
    r_arrays, r_in, r_out, r_shapes, r_scratch, r_tiles, r_resident = _rider_lists(rider)
    assert rider is None or rider["steps"] == nt

    def rev(t):
        return nt - 1 - t

    def body(pa_ref, al_ref, wg_ref, bg_ref, gn_ref, st_ref, prev_ref, dy_ref,
             dpa_ref, dl_ref, dgn_ref, dbg_ref, carry, pbuf):
        t = pl.program_id(0)

        @pl.when(t == 0)
        def _():
            carry[...] = jnp.zeros_like(carry)
            dgn_ref[...] = jnp.zeros_like(dgn_ref)
            dbg_ref[...] = jnp.zeros_like(dbg_ref)

        logit = _dot(al_ref[...], wg_ref[...], NN) + bg_ref[...]
        ls = _logsig(logit) * (1.0 / 16.0)
        sneg = 1.0 / (1.0 + jnp.exp(logit))
        tri = _tri(True)
        upper = _tri(False)
        first_tile = rev(t) == 0
        heads = range(HEADS)

        w, decay, kd = [], [], []
        for c in range(cb):
            rows = pl.ds(c * CHUNK, CHUNK)
            cum = _exact_mask_dot(tri, ls[c * CHUNK:(c + 1) * CHUNK])
            tot = cum[CHUNK - 1:CHUNK]
            w.append(jnp.exp(tot - cum))
            decay.append(jnp.exp(tot))
            kd.append(pa_ref[rows, pl.ds(K_OFF, QK)].astype(F32) * w[c])

        dgn = [jnp.zeros((1, DV), F32) for _ in heads]
        for c in range(cb):
            rows = pl.ds(c * CHUNK, CHUNK)
            for h in heads:
                gn = gn_ref[h]
                qs = (pa_ref[rows, pl.ds(Q_OFF + h * DK, DK)].astype(F32) * Q_SCALE).astype(BF16)
                st16 = st_ref[c, h].astype(BF16)
                o = _dot(qs, st16, NT)
                rs = lax.rsqrt(jnp.mean(o * o, axis=-1, keepdims=True) + EPS)
                oh = o * rs
                rr = pa_ref[rows, pl.ds(R_OFF + h * DV, DV)].astype(F32)
                sr = _sig(rr)
                dyv = dy_ref[rows, pl.ds(h * DV, DV)].astype(F32)
                dpa_ref[rows, pl.ds(R_OFF + h * DV, DV)] = (
                    dyv * oh * gn * sr * (1.0 + rr * (1.0 - sr))).astype(BF16)
                don = dyv * (rr * sr)
                dgn[h] = dgn[h] + jnp.sum(don * oh, axis=0, keepdims=True)
                doh = don * gn
                do16 = (rs * (doh - oh * jnp.mean(doh * oh, axis=-1, keepdims=True))).astype(BF16)
                dpa_ref[rows, pl.ds(Q_OFF + h * DK, DK)] = (_dot(do16, st16, NN) * Q_SCALE).astype(BF16)
                pbuf[c, h] = _dot(do16, qs, TN)
        for h in heads:
            dgn_ref[h] += dgn[h]

        dkd = [[None] * HEADS for _ in range(cb)]
        ddecay = [[None] * HEADS for _ in range(cb)]
        for c in reversed(range(cb)):
            rows = pl.ds(c * CHUNK, CHUNK)
            for h in heads:
                lanes = slice(h * DK, (h + 1) * DK)
                gt = pbuf[c, h] + carry[h]
                gt16 = gt.astype(BF16)
                dkd[c][h] = _dot(pa_ref[rows, pl.ds(V_OFF + h * DV, DV)], gt16, NN)
                dpa_ref[rows, pl.ds(V_OFF + h * DV, DV)] = _dot(kd[c][:, lanes].astype(BF16), gt16, NT).astype(BF16)
                if c > 0:
                    st_prev = st_ref[c - 1, h]
                else:
                    st_prev = jnp.where(first_tile, 0.0, prev_ref[0, h])
                ddecay[c][h] = jnp.sum(gt * st_prev, axis=0, keepdims=True)
                carry[h] = gt * decay[c][:, lanes]

        dbg = jnp.zeros((1, QK), F32)
        for c in range(cb):
            rows = pl.ds(c * CHUNK, CHUNK)
            dkd_c = jnp.concatenate(dkd[c], axis=1)
            dpa_ref[rows, pl.ds(K_OFF, QK)] = (dkd_c * w[c]).astype(BF16)
            e = dkd_c * kd[c]
            dtot = jnp.sum(e, axis=0, keepdims=True) + jnp.concatenate(ddecay[c], axis=1) * decay[c]
            dls = dtot - _exact_mask_dot(upper, e)
            dlogit = dls * (1.0 / 16.0) * sneg[c * CHUNK:(c + 1) * CHUNK]
            dl_ref[rows, :] = dlogit.astype(BF16)
            dbg = dbg + jnp.sum(dlogit, axis=0, keepdims=True)
        dbg_ref[...] += dbg

    wide = 2 * QK + 2 * GV
    tiles = [((Tg, wide), BF16), ((cb + 1, HEADS, DV, DK), F32), ((Tg, GV), BF16), ((Tg, wide), BF16),
             ((Tg, QK), BF16)] + r_tiles
    resident = [((cb + 1, HEADS, DV, DK), F32)] + r_resident
    return pl.pallas_call(
        _with_rider(body, 8, 4, 2, rider), name=name, grid=(nt,),
        in_specs=[
            pl.BlockSpec((Tg, wide), lambda t: (rev(t), 0)),
            pl.BlockSpec((Tg, LANES), lambda t: (rev(t), 0)),
            pl.BlockSpec((LANES, QK), lambda t: (0, 0)),
            pl.BlockSpec((1, QK), lambda t: (0, 0)),
            pl.BlockSpec((HEADS, 1, DV), lambda t: (0, 0, 0)),
            pl.BlockSpec((cb, HEADS, DV, DK), lambda t: (rev(t), 0, 0, 0)),
            pl.BlockSpec((1, HEADS, DV, DK), lambda t: (jnp.maximum(rev(t) * cb - 1, 0), 0, 0, 0)),
            pl.BlockSpec((Tg, GV), lambda t: (rev(t), 0)),
        ] + r_in,
        out_specs=[
            pl.BlockSpec((Tg, wide), lambda t: (rev(t), 0)),
            pl.BlockSpec((Tg, QK), lambda t: (rev(t), 0)),
            pl.BlockSpec((HEADS, 1, DV), lambda t: (0, 0, 0)),
            pl.BlockSpec((1, QK), lambda t: (0, 0)),
        ] + r_out,
        out_shape=[jax.ShapeDtypeStruct((S, wide), BF16), jax.ShapeDtypeStruct((S, QK), BF16),
                   jax.ShapeDtypeStruct((HEADS, 1, DV), F32), jax.ShapeDtypeStruct((1, QK), F32)] + r_shapes,
        scratch_shapes=[pltpu.VMEM((HEADS, DV, DK), F32), pltpu.VMEM((cb, HEADS, DV, DK), F32)] + r_scratch,
        compiler_params=_cp(("arbitrary",), _vmem_limit(tiles, resident)),
    )(pa, alow, wg, bgate, gnorm, states, states, dy, *r_arrays)


SGU_TILE = 256


def _sgu_mask():
    r = lax.broadcasted_iota(jnp.int32, (SBLOCK, SBLOCK), 0)
    c = lax.broadcasted_iota(jnp.int32, (SBLOCK, SBLOCK), 1)
    return (c < CHUNK) | (r >= CHUNK)


def _ln_stats(vf):
    mu = jnp.mean(vf, axis=-1, keepdims=True)
    xc = vf - mu
    rs = lax.rsqrt(jnp.mean(xc * xc, axis=-1, keepdims=True) + EPS)
    return rs, xc * rs


def _sgu_fwd_call(ps, ln_g, ln_b, w_sp, b_sp, *, name):
    S = ps.shape[0]
    Ts = min(SGU_TILE, S)

    def body(ps_ref, lg_ref, lb_ref, w_ref, b_ref, y_ref):
        mask = _sgu_mask()
        for g in range(GROUPS):
            wm = jnp.where(mask, w_ref[g], 0.0).astype(BF16)
            for p in range(Ts // SBLOCK):
                rows = pl.ds(p * SBLOCK, SBLOCK)
                u = _gelu(ps_ref[rows, pl.ds(g * DG, DG)].astype(F32))
                _, xh = _ln_stats(_gelu(ps_ref[rows, pl.ds(D_MODEL + g * DG, DG)].astype(F32)))
                vn = xh * lg_ref[g] + lb_ref[g]
                mixed = _dot(wm, vn.astype(BF16), NN) + b_ref[g]
                y_ref[rows, pl.ds(g * DG, DG)] = (u * mixed).astype(BF16)

    full3 = lambda a, b, c: pl.BlockSpec((a, b, c), lambda t: (0, 0, 0))
    return pl.pallas_call(
        body, name=name, grid=(S // Ts,),
        in_specs=[pl.BlockSpec((Ts, 2 * D_MODEL), lambda t: (t, 0)),
                  full3(GROUPS, 1, DG), full3(GROUPS, 1, DG), full3(GROUPS, SBLOCK, SBLOCK), full3(GROUPS, SBLOCK, 1)],
        out_specs=pl.BlockSpec((Ts, D_MODEL), lambda t: (t, 0)),
        out_shape=jax.ShapeDtypeStruct((S, D_MODEL), BF16),
        compiler_params=_cp(("parallel",)),
    )(ps, ln_g, ln_b, w_sp, b_sp)


def _sgu_bwd_call(ps, ln_g, ln_b, w_sp, b_sp, dy, *, name, rider=None):
    S = ps.shape[0]
    Ts = min(SGU_TILE, S)
    r_arrays, r_in, r_out, r_shapes, r_scratch, r_tiles, r_resident = _rider_lists(rider)
    assert rider is None or rider["steps"] == S // Ts

    def body(ps_ref, lg_ref, lb_ref, w_ref, b_ref, dy_ref, ds_ref, dlg_ref, dlb_ref, dw_ref, db_ref):
        @pl.when(pl.program_id(0) == 0)
        def _():
            dlg_ref[...] = jnp.zeros_like(dlg_ref)
            dlb_ref[...] = jnp.zeros_like(dlb_ref)
            dw_ref[...] = jnp.zeros_like(dw_ref)
            db_ref[...] = jnp.zeros_like(db_ref)

        mask = _sgu_mask()
        for g in range(GROUPS):
            wm = jnp.where(mask, w_ref[g], 0.0).astype(BF16)
            lg = lg_ref[g]
            for p in range(Ts // SBLOCK):
                rows = pl.ds(p * SBLOCK, SBLOCK)
                su = ps_ref[rows, pl.ds(g * DG, DG)].astype(F32)
                sv = ps_ref[rows, pl.ds(D_MODEL + g * DG, DG)].astype(F32)
                u, du = _gelu_and_grad(su)
                gv, dgv = _gelu_and_grad(sv)
                rs, xh = _ln_stats(gv)
                vn16 = (xh * lg + lb_ref[g]).astype(BF16)
                mixed = _dot(wm, vn16, NN) + b_ref[g]
                dyv = dy_ref[rows, pl.ds(g * DG, DG)].astype(F32)
                ds_ref[rows, pl.ds(g * DG, DG)] = (dyv * mixed * du).astype(BF16)
                dmix = dyv * u
                dmix16 = dmix.astype(BF16)
                db_ref[g] += jnp.sum(dmix, axis=-1, keepdims=True)
                dw_ref[g] += jnp.where(mask, _dot(dmix16, vn16, NT), 0.0)
                dvn = _dot(wm, dmix16, TN)
                dlg_ref[g] += jnp.sum(dvn * xh, axis=0, keepdims=True)
                dlb_ref[g] += jnp.sum(dvn, axis=0, keepdims=True)
                dxh = dvn * lg
                dvf = rs * (dxh - jnp.mean(dxh, axis=-1, keepdims=True)
                            - xh * jnp.mean(dxh * xh, axis=-1, keepdims=True))
                ds_ref[rows, pl.ds(D_MODEL + g * DG, DG)] = (dvf * dgv).astype(BF16)

    full3 = lambda a, b, c: pl.BlockSpec((a, b, c), lambda t: (0, 0, 0))
    tiles = [((Ts, 2 * D_MODEL), BF16), ((Ts, D_MODEL), BF16), ((Ts, 2 * D_MODEL), BF16)] + r_tiles
    return pl.pallas_call(
        _with_rider(body, 6, 5, 0, rider), name=name, grid=(S // Ts,),
        in_specs=[pl.BlockSpec((Ts, 2 * D_MODEL), lambda t: (t, 0)),
                  full3(GROUPS, 1, DG), full3(GROUPS, 1, DG), full3(GROUPS, SBLOCK, SBLOCK), full3(GROUPS, SBLOCK, 1),
                  pl.BlockSpec((Ts, D_MODEL), lambda t: (t, 0))] + r_in,
        out_specs=[pl.BlockSpec((Ts, 2 * D_MODEL), lambda t: (t, 0)),
                   full3(GROUPS, 1, DG), full3(GROUPS, 1, DG), full3(GROUPS, SBLOCK, SBLOCK), full3(GROUPS, SBLOCK, 1)]
        + r_out,
        out_shape=[jax.ShapeDtypeStruct((S, 2 * D_MODEL), BF16),
                   jax.ShapeDtypeStruct((GROUPS, 1, DG), F32), jax.ShapeDtypeStruct((GROUPS, 1, DG), F32),
                   jax.ShapeDtypeStruct((GROUPS, SBLOCK, SBLOCK), F32),
                   jax.ShapeDtypeStruct((GROUPS, SBLOCK, 1), F32)] + r_shapes,
        scratch_shapes=r_scratch,
        compiler_params=_cp(("arbitrary",), _vmem_limit(tiles, r_resident)),
    )(ps, ln_g, ln_b, w_sp, b_sp, dy, *r_arrays)


def _position():
    return lax.axis_index("x"), lax.axis_index("y"), lax.axis_index("c")


def _gather_copies(srcs, dsts, send_sems, recv_sems, local_sems, relay=False):
    if relay:
        return _gather_copies_relayed(srcs, dsts, send_sems, recv_sems, local_sems)
    x, y, c = _position()
    me, sibling = (x, y, c), (x, y, 1 - c)
    chips = [(1 - x, y), (x, 1 - y), (1 - x, 1 - y)]
    n = len(srcs)

    def slab(a, block):
        px, py, pc = block
        return dsts[a].at[4 * px + 2 * py + pc]

    def copy(a, k, block, to, src=None):
        return pltpu.make_async_remote_copy(
            src_ref=slab(a, block) if src is None else src, dst_ref=slab(a, block),
            send_sem=send_sems.at[7 * a + k], recv_sem=recv_sems.at[7 * a + k], device_id=to, device_id_type=MESH)

    mine = [pltpu.make_async_copy(srcs[a], slab(a, me), local_sems.at[a]) for a in range(n)]
    for cp in mine:
        cp.start()
    first = []
    for a in range(n):
        first.append(copy(a, 0, me, sibling, src=srcs[a]))
        first += [copy(a, 1 + j, me, (*chip, c), src=srcs[a]) for j, chip in enumerate(chips)]
    for cp in first:
        cp.start()
    passed = []
    for j, chip in enumerate(chips):
        for a in range(n):
            copy(a, 1 + j, (*chip, c), me).wait_recv()
            cp = copy(a, 4 + j, (*chip, c), sibling)
            cp.start()
            passed.append(cp)
    for a in range(n):
        copy(a, 0, sibling, me).wait_recv()
        for j, chip in enumerate(chips):
            copy(a, 4 + j, (*chip, 1 - c), me).wait_recv()
    for cp in first + passed:
        cp.wait_send()
    for cp in mine:
        cp.wait()


def _gather_copies_relayed(srcs, dsts, send_sems, recv_sems, local_sems):
    x, y, c = _position()
    me, sibling = (x, y, c), (x, y, 1 - c)
    x_chip, y_chip, d_chip = (1 - x, y), (x, 1 - y), (1 - x, 1 - y)
    south = c == 0
    relay_from = (jnp.where(south, x, 1 - x), jnp.where(south, 1 - y, y), c)
    relay_to = (jnp.where(south, 1 - x, x), jnp.where(south, y, 1 - y), c)
    n = len(srcs)

    def slab(a, block):
        px, py, pc = block
        return dsts[a].at[4 * px + 2 * py + pc]

    def copy(a, k, block, to, src=None):
        return pltpu.make_async_remote_copy(
            src_ref=slab(a, block) if src is None else src, dst_ref=slab(a, block),
            send_sem=send_sems.at[7 * a + k], recv_sem=recv_sems.at[7 * a + k], device_id=to, device_id_type=MESH)

    mine = [pltpu.make_async_copy(srcs[a], slab(a, me), local_sems.at[a]) for a in range(n)]
    for cp in mine:
        cp.start()
    sent = []
    for a in range(n):
        sent += [copy(a, 0, me, sibling, src=srcs[a]), copy(a, 1, me, (*x_chip, c), src=srcs[a]),
                 copy(a, 2, me, (*y_chip, c), src=srcs[a])]
    for cp in sent:
        cp.start()
    for a in range(n):
        copy(a, 1, (*x_chip, c), me).wait_recv()
        copy(a, 2, (*y_chip, c), me).wait_recv()
        later = [copy(a, 3, relay_from, relay_to), copy(a, 4, (*x_chip, c), sibling), copy(a, 5, (*y_chip, c), sibling)]
        for cp in later:
            cp.start()
        sent += later
    for a in range(n):
        copy(a, 3, (*d_chip, c), me).wait_recv()
        cp = copy(a, 6, (*d_chip, c), sibling)
        cp.start()
        sent.append(cp)
    for a in range(n):
        copy(a, 0, sibling, me).wait_recv()
        for k, chip in ((4, x_chip), (5, y_chip), (6, d_chip)):
            copy(a, k, (*chip, 1 - c), me).wait_recv()
    for cp in sent:
        cp.wait_send()
    for cp in mine:
        cp.wait()


def _all_gather_hbm(shards, *, name):
    n = len(shards)

    def body(*refs):
        srcs, dsts = refs[:n], refs[n:2 * n]
        send_sems, recv_sems, local_sems = refs[2 * n:]
        _gather_copies(srcs, dsts, send_sems, recv_sems, local_sems, relay=True)

    return pl.pallas_call(
        body, name=name,
        in_specs=[ANY] * n, out_specs=[ANY] * n,
        out_shape=[jax.ShapeDtypeStruct((N_DEV, *s.shape), s.dtype) for s in shards],
        scratch_shapes=_comm_sems(n),
    )(*shards)


def _all_reduce_small(part, *, name):
    R, W = part.shape

    def body(x_ref, out_ref, gathered, send_sems, recv_sems, local_sems):
        _gather_copies([x_ref], [gathered], send_sems, recv_sems, local_sems)
        acc = gathered[0]
        for d in range(1, N_DEV):
            acc = acc + gathered[d]
        out_ref[...] = acc

    return pl.pallas_call(
        body, name=name,
        in_specs=[VMEM_SPEC], out_specs=VMEM_SPEC,
        out_shape=jax.ShapeDtypeStruct((R, W), F32),
        scratch_shapes=[pltpu.VMEM((N_DEV, R, W), F32),
                        pltpu.SemaphoreType.DMA((7,)), pltpu.SemaphoreType.DMA((7,)), pltpu.SemaphoreType.DMA((1,))],
    )(part)


FLIPS = [(fx, fy, fc) for fx in (0, 1) for fy in (0, 1) for fc in (0, 1)][1:]


def _scatter_copies(srcs, dsts, send_sems, recv_sems, local_sems, waves=1):
    n = len(srcs)
    x, y, c = _position()
    me = 4 * x + 2 * y + c
    mine = [pltpu.make_async_copy(srcs[a].at[me], dsts[a].at[me], local_sems.at[a]) for a in range(n)]
    for cp in mine:
        cp.start()
    peers = []
    for fx, fy, fc in FLIPS:
        tx = 1 - x if fx else x
        ty = 1 - y if fy else y
        tc = 1 - c if fc else c
        peers.append(((tx, ty, tc), 4 * tx + 2 * ty + tc))
    copies = []
    for w in range(waves):
        wave = []
        for k, (peer_id, peer) in enumerate(peers):
            for a in range(n):
                rows = srcs[a].shape[1]
                step = rows if waves == 1 else (rows // waves) // 16 * 16
                r0 = w * step
                cut = pl.ds(r0, step if w < waves - 1 else rows - r0)
                sem = (7 * a + k) * waves + w
                cp = pltpu.make_async_remote_copy(
                    src_ref=srcs[a].at[peer, cut], dst_ref=dsts[a].at[me, cut],
                    send_sem=send_sems.at[sem], recv_sem=recv_sems.at[sem],
                    device_id=peer_id, device_id_type=MESH)
                cp.start()
                wave.append(cp)
        for cp in wave:
            cp.wait_send()
        copies += wave
    for cp in copies:
        cp.wait_recv()
    for cp in mine:
        cp.wait()


def _comm_sems(n, waves=1):
    return [pltpu.SemaphoreType.DMA((7 * n * waves,)), pltpu.SemaphoreType.DMA((7 * n * waves,)),
            pltpu.SemaphoreType.DMA((n,))]


def _scatter_blocks(parts, *, name, waves=1):
    n = len(parts)

    def body(*refs):
        _scatter_copies(refs[:n], refs[n:2 * n], *refs[2 * n:], waves=waves)

    return pl.pallas_call(
        body, name=name,
        in_specs=[ANY] * n, out_specs=[ANY] * n,
        out_shape=[jax.ShapeDtypeStruct(p.shape, p.dtype) for p in parts],
        scratch_shapes=_comm_sems(n, waves),
    )(*parts)


def _handshake(peers):
    barrier = pltpu.get_barrier_semaphore()
    for peer in peers:
        pl.semaphore_signal(barrier, inc=1, device_id=peer, device_id_type=MESH)
    pl.semaphore_wait(barrier, len(peers))


def _sequencer_call(arrays, out_types, copies_fn, peers_fn, *, name, collective_id, waves=1):
    n = len(arrays)
    srcs = [jax.new_ref(a, memory_space=pltpu.MemorySpace.HBM) for a in arrays]
    dsts = [jax.empty_ref(t, memory_space=pltpu.MemorySpace.HBM) for t in out_types]
    extra = {} if waves == 1 else {"waves": waves}

    @pl.kernel(mesh=plsc.ScalarSubcoreMesh(axis_name="sequencer", num_cores=1), name=name,
               scratch_types=_comm_sems(n, waves), compiler_params=pltpu.CompilerParams(collective_id=collective_id))
    def launch(send_sems, recv_sems, local_sems):
        _handshake(peers_fn())
        copies_fn(srcs, dsts, send_sems, recv_sems, local_sems, **extra)

    launch()
    return [d[...] for d in dsts]


def _all_other_devices():
    x, y, c = _position()
    return [(1 - x if fx else x, 1 - y if fy else y, 1 - c if fc else c) for fx, fy, fc in FLIPS]


def _gather_relay_peers():
    x, y, c = _position()
    return [(x, y, 1 - c), (1 - x, y, c), (x, 1 - y, c)]


def _scatter_blocks_async(parts, *, name, collective_id, waves=1):
    return _sequencer_call(parts, [jax.ShapeDtypeStruct(p.shape, p.dtype) for p in parts],
                           _scatter_copies, _all_other_devices, name=name, collective_id=collective_id, waves=waves)


def _all_gather_async(shards, *, name, collective_id):
    return _sequencer_call(shards, [jax.ShapeDtypeStruct((N_DEV, *s.shape), s.dtype) for s in shards],
                           _gather_copies_relayed, _gather_relay_peers, name=name, collective_id=collective_id)


def _adamw_math(w, g, m, v):
    m = ADAM_B1 * m + (1.0 - ADAM_B1) * g
    v = ADAM_B2 * v + (1.0 - ADAM_B2) * (g * g)
    m_hat = m / (1.0 - ADAM_B1 ** ADAM_STEP)
    v_hat = v / (1.0 - ADAM_B2 ** ADAM_STEP)
    delta = -ADAM_LR * (m_hat / (jnp.sqrt(v_hat) + ADAM_EPS) + ADAM_WD * w)
    return delta, m, v


def _adamw_reduce_call(recv, w, m, v, *, name, T=128):
    R, W = w.shape
    if R % T == 0:
        tr, tw = T, W
    elif (R // 2) % 16 == 0:
        tr, tw = R // 2, W
    else:
        tr, tw = R, 2 * LANES

    def body(p_ref, w_ref, m_ref, v_ref, g_out, d_out, m_out, v_out):
        g = p_ref[0].astype(F32)
        for d in range(1, N_DEV):
            g = g + p_ref[d].astype(F32)
        g_out[...] = g
        d_out[...], m_out[...], v_out[...] = _adamw_math(w_ref[...], g, m_ref[...], v_ref[...])

    row = pl.BlockSpec((tr, tw), lambda i, j: (i, j))
    out = jax.ShapeDtypeStruct((R, W), F32)
    return pl.pallas_call(
        body, name=name, grid=(R // tr, W // tw),
        in_specs=[pl.BlockSpec((N_DEV, tr, tw), lambda i, j: (0, i, j)), row, row, row],
        out_specs=[row] * 4, out_shape=[out] * 4,
        compiler_params=_cp(("parallel", "parallel"), VMEM_BIG),
    )(recv, w, m, v)


SMALL_EARLY = (("b_gate", 8), ("w_spatial", 512), ("b_spatial", 8), ("norm_post_mix", 8), ("norm_pre_ffn", 8),
               ("norm_post_ffn", 8), ("w_gate_up", 128), ("gla_norm", 64), ("sgu_ln_g", 64), ("sgu_ln_b", 64))
SMALL_EARLY_AT = {}
for _name, _rows in SMALL_EARLY:
    SMALL_EARLY_AT[_name] = sum(r for _, r in SMALL_EARLY[:len(SMALL_EARLY_AT)])
SMALL_EARLY_ROWS = sum(r for _, r in SMALL_EARLY)
SMALL_LATE_ROWS = 16


def _small_early_rows(grads):
    def rows(a, n_rows):
        a = a.reshape(-1, LANES)
        return jnp.pad(a, ((0, n_rows - a.shape[0]), (0, 0)))

    def device_major(a, n_rows):
        r, c = a.shape[0], a.shape[1] // N_DEV
        a = a.reshape(r, N_DEV, c).transpose(1, 0, 2)
        a = jnp.pad(a, ((0, 0), (0, n_rows // N_DEV - r), (0, LANES - c)))
        return a.reshape(n_rows, LANES)

    pieces = []
    for name, n_rows in SMALL_EARLY:
        g = grads[name]
        if name in SMALL_SHARDED:
            pieces.append(device_major(g.reshape(g.shape[0], -1) if g.ndim == 2 else g.reshape(g.shape[0], g.shape[-1]), n_rows))
        else:
            pieces.append(rows(g, n_rows))
    return jnp.concatenate(pieces, axis=0)


def _small_update_call(early, dg_pre_mix, loss_part, w, m, v, *, name):
    names = list(SMALL)
    n_p = len(names)
    E = SMALL_EARLY_ROWS

    def reduce_body(early_ref, dg1_ref, loss_ref, tot, got_early, got_late, late, send_sems, recv_sems, local_sems):
        for k in range(D_MODEL // LANES):
            late[k:k + 1, :] = dg1_ref[:, k * LANES:(k + 1) * LANES]
        late[8:16, :] = jnp.broadcast_to(loss_ref[...], (8, LANES))
        _gather_copies([early_ref, late], [got_early, got_late], send_sems, recv_sems, local_sems)
        acc, acc_late = got_early[0], got_late[0]
        for d in range(1, N_DEV):
            acc, acc_late = acc + got_early[d], acc_late + got_late[d]
        tot[0:E, :] = acc
        tot[E:E + SMALL_LATE_ROWS, :] = acc_late

    total = pl.pallas_call(
        reduce_body, name=name + "_reduce",
        in_specs=[VMEM_SPEC] * 3, out_specs=VMEM_SPEC,
        out_shape=jax.ShapeDtypeStruct((E + SMALL_LATE_ROWS, LANES), F32),
        scratch_shapes=[pltpu.VMEM((N_DEV, E, LANES), F32), pltpu.VMEM((N_DEV, SMALL_LATE_ROWS, LANES), F32),
                        pltpu.VMEM((SMALL_LATE_ROWS, LANES), F32),
                        pltpu.SemaphoreType.DMA((14,)), pltpu.SemaphoreType.DMA((14,)), pltpu.SemaphoreType.DMA((2,))],
    )(early, dg_pre_mix, loss_part)

    def body(tot, *rest):
        w_refs, m_refs, v_refs = [dict(zip(names, rest[i * n_p:(i + 1) * n_p])) for i in range(3)]
        outs = rest[3 * n_p:]
        loss_out = outs[0]
        g_out, d_out, m_out, v_out = [dict(zip(names, outs[1 + i * n_p:1 + (i + 1) * n_p])) for i in range(4)]
        loss_out[...] = tot[E + 8:E + 9, :]

        x, y, c = _position()
        me = 4 * x + 2 * y + c

        def update(name, g, ix):
            g_out[name][ix] = g
            d_out[name][ix], m_out[name][ix], v_out[name][ix] = _adamw_math(
                w_refs[name][ix], g, m_refs[name][ix], v_refs[name][ix])

        for name in names:
            shape = w[name].shape
            if name in SMALL_SHARDED:
                per_dev = dict(SMALL_EARLY)[name] // N_DEV
                at = pl.multiple_of(SMALL_EARLY_AT[name] + me * per_dev, 8)
                g = tot[pl.ds(at, per_dev), :]
                update(name, g[:shape[1], :shape[2]], (0,))
            elif name == "w_spatial":
                for grp in range(GROUPS):
                    at = SMALL_EARLY_AT[name] + grp * SBLOCK
                    update(name, tot[at:at + SBLOCK, :], (0, grp))
            elif name == "b_spatial":
                at = SMALL_EARLY_AT[name]
                update(name, tot[at:at + GROUPS, :], (0,))
            else:
                at = E if name == "norm_pre_mix" else SMALL_EARLY_AT[name]
                for k in range(shape[1] // LANES):
                    update(name, tot[at + k:at + k + 1, :], (slice(None), pl.ds(k * LANES, LANES)))

    state = [s[n] for s in (w, m, v) for n in names]
    out_shapes = [jax.ShapeDtypeStruct((1, LANES), F32)] + [jax.ShapeDtypeStruct(w[n].shape, F32) for n in names] * 4
    outs = pl.pallas_call(
        body, name=name + "_adamw",
        in_specs=[VMEM_SPEC] * (1 + len(state)), out_specs=[VMEM_SPEC] * len(out_shapes), out_shape=out_shapes,
    )(total, *state)
    per_name = {n: tuple(outs[1 + i * n_p + j] for i in range(4)) for j, n in enumerate(names)}
    return outs[0], per_name


def _tile_rows(n_elems):
    return -(-n_elems // (8 * LANES)) * 8


def _pack_rows(parts, rows):
    pieces = []
    for p in parts:
        q = p.reshape(-1, LANES)
        pieces.append(jnp.pad(q, ((0, _tile_rows(p.size) - q.shape[0]), (0, 0))))
    buf = jnp.concatenate(pieces, axis=0)
    return jnp.pad(buf, ((0, rows - buf.shape[0]), (0, 0)))


def _unpack_rows(buf, shapes):
    out, off = [], 0
    for shp in shapes:
        n = 1
        for s in shp:
            n *= s
        out.append(buf[off:off + n // LANES].reshape(shp))
        off += _tile_rows(n)
    return out


def _in_w_blocks_call(a, dps, dpg, dw_a, dw_low, *, name, tk=TOKEN_TILE):
    S = a.shape[0]
    tk = min(tk, S)
    steps = S // tk
    n_s, n_g = dps.shape[1], dpg.shape[1]
    out_shape = (N_DEV, IN_BLK, D_MODEL)

    def body(a_ref, dps_ref, dpg_ref, dwa_ref, dwl_ref, out_ref, acc_s, acc_g):
        k = pl.program_id(0)

        @pl.when(k == 0)
        def _():
            acc_s[...] = jnp.zeros_like(acc_s)
            acc_g[...] = jnp.zeros_like(acc_g)

        av = a_ref[...]
        for src, acc, n in ((dps_ref, acc_s, n_s), (dpg_ref, acc_g, n_g)):
            for m0 in range(0, n, 1024):
                acc[m0:m0 + 1024, :] += _dot(src[:, m0:m0 + 1024], av, TN)

        @pl.when(k == steps - 1)
        def _():
            groups = ((dwa_ref, A_COLS), (dwl_ref, LOW_COLS), (acc_s, S_COLS), (acc_g, G_COLS))
            for d in range(N_DEV):
                lo, hi = IN_BLK * d, IN_BLK * (d + 1)
                for ref, (c0, c1) in groups:
                    s0, e0 = max(lo, c0), min(hi, c1)
                    if s0 < e0:
                        out_ref[d, s0 - lo:e0 - lo, :] = ref[s0 - c0:e0 - c0, :].astype(BF16)

    tiles = [((tk, D_MODEL), BF16), ((tk, n_s), BF16), ((tk, n_g), BF16)]
    resident = [(dw_a.shape, BF16), (dw_low.shape, BF16), (out_shape, BF16), ((n_s, D_MODEL), F32), ((n_g, D_MODEL), F32)]
    return pl.pallas_call(
        body, name=name, grid=(steps,),
        in_specs=[pl.BlockSpec((tk, D_MODEL), lambda k: (k, 0)), pl.BlockSpec((tk, n_s), lambda k: (k, 0)),
                  pl.BlockSpec((tk, n_g), lambda k: (k, 0)), _res(dw_a.shape), _res(dw_low.shape)],
        out_specs=_acc(out_shape),
        out_shape=jax.ShapeDtypeStruct(out_shape, BF16),
        scratch_shapes=[pltpu.VMEM((n_s, D_MODEL), F32), pltpu.VMEM((n_g, D_MODEL), F32)],
        compiler_params=_cp(("arbitrary",), _vmem_limit(tiles, resident)),
    )(a, dps, dpg, dw_a, dw_low)


def _local_step(x, target, W, scatter, finish):
    g1, g2, g3, g4 = [W[n].reshape(1, D_MODEL) for n in ("norm_pre_mix", "norm_post_mix", "norm_pre_ffn", "norm_post_ffn")]
    wfi, wfo = W["w_ffn_in"], W["w_ffn_out"].reshape(4, FF_BLK, D_MODEL)
    wg = jnp.pad(W["w_gate_up"], ((0, LANES - RANK), (0, 0))).astype(BF16)
    bgate = W["b_gate"].reshape(1, QK)
    gnorm = W["gla_norm"].reshape(HEADS, 1, DV)
    ln_g = W["sgu_ln_g"].reshape(GROUPS, 1, DG)
    ln_b = W["sgu_ln_b"].reshape(GROUPS, 1, DG)
    w_sp = W["w_spatial"]
    b_sp = W["b_spatial"].reshape(GROUPS, SBLOCK, 1)

    a, pa, alow, ps, pg, w_in_t = _in_proj_call(x, g1, W["w_in_blocks"], name="in_proj")
    y_gla, states = _gla_fwd_call(pa, alow, wg, bgate, gnorm, name="gla_fwd")
    y_sgu = _sgu_fwd_call(ps, ln_g, ln_b, w_sp, b_sp, name="sgu_fwd")
    t1, t2, merged, mix, x1, h = _mixer_tail_call(y_gla, y_sgu, pg, x, W["w_branch_gla"], W["w_branch_sgu"],
                                                  W["w_out"], g2, g3, name="mixer_tail")
    gu, z = _ffn_in_call(h, wfi, name="ffn_in")
    loss, dx2, dy, dg4 = _ffn_out_loss_call(z, wfo, x1, target, g4, name="ffn_out_loss")

    grads = {"norm_post_ffn": dg4}
    done = {}
    dgu = _ffn_out_bwd_call(dy, wfo, gu, name="ffn_out_bwd")
    dw_ffn_out = _weight_grads_call([z, dy], [(0, 1)], name="d_ffn_out_w")[0].reshape(N_DEV, D_FF // N_DEV, D_MODEL)
    dw_ffn_in = _ffn_in_grad_call(h, dgu, name="d_ffn_in_w")
    dgu, dw_ffn_out, dw_ffn_in = lax.optimization_barrier((dgu, dw_ffn_out, dw_ffn_in))
    ffn_received = scatter(("w_ffn_out", "w_ffn_in"), [dw_ffn_out, dw_ffn_in])
    dx1, dmix, grads["norm_pre_ffn"], grads["norm_post_mix"] = _ffn_in_bwd_call(dgu, wfi, dx2, x1, mix, g3, g2, name="ffn_in_bwd")
    dt1, dt2, dpg, dy_gla, dy_sgu = _mixer_bwd_call(dmix, t1, t2, pg, W["w_out"], W["w_branch_gla"], W["w_branch_sgu"],
                                                    name="mixer_bwd")
    rows = D_MODEL // N_DEV
    mixer_grads = _weight_grads_call([merged, dmix, y_gla, dt1, y_sgu, dt2], [(0, 1), (2, 3), (4, 5)], name="d_mixer_w")
    dy_gla, dy_sgu, mixer_grads = lax.optimization_barrier((dy_gla, dy_sgu, mixer_grads))
    mixer_received = scatter(("w_out", "w_branch_gla", "w_branch_sgu"),
                             [g.reshape(N_DEV, rows, D_MODEL) for g in mixer_grads])
    dpa, dlogit, dgn, dbg = _gla_bwd_call(pa, alow, wg, bgate, gnorm, states, dy_gla, name="gla_bwd")
    ffn_received, dpa, dlogit = lax.optimization_barrier((ffn_received, dpa, dlogit))
    dps, dlg, dlb, dwsp, dbsp = _sgu_bwd_call(ps, ln_g, ln_b, w_sp, b_sp, dy_sgu, name="sgu_bwd")
    mixer_received, dps = lax.optimization_barrier((mixer_received, dps))
    dlow = _mm(dlogit, wg, "nt", BF16, name="d_gate_up_x")
    dw_a, dw_low = _weight_grads_call([a, dpa, dlow], [(1, 0), (2, 0)], name="d_in_w_qkvr")
    dw_in = _in_w_blocks_call(a, dps, dpg, dw_a, dw_low, name="d_in_w_blocks")
    ffn_received, mixer_received, dw_in = lax.optimization_barrier((ffn_received, mixer_received, dw_in))
    in_received = scatter(("w_in",), [dw_in])
    done.update(finish(ffn_received))
    done.update(finish(mixer_received))
    dwg = _mm(alow, dlogit, "tn", F32, name="d_gate_up_w")
    grad_x, grads["norm_pre_mix"] = _in_proj_bwd_call(dpa, dlow, dps, dpg, w_in_t, x, dx1, g1, name="in_proj_bwd")

    grads["w_gate_up"] = dwg[:RANK]
    grads["b_gate"] = dbg
    grads["gla_norm"] = dgn
    grads["sgu_ln_g"] = dlg
    grads["sgu_ln_b"] = dlb
    grads["w_spatial"] = dwsp
    grads["b_spatial"] = dbsp
    done.update(finish(in_received))
    return loss, grad_x, grads, done


WEIGHTS = ("norm_pre_mix", "w_in", "w_gate_up", "b_gate", "gla_norm", "sgu_ln_g", "sgu_ln_b", "w_spatial",
           "b_spatial", "w_branch_gla", "w_branch_sgu", "w_out", "norm_post_mix", "norm_pre_ffn", "w_ffn_in",
           "w_ffn_out", "norm_post_ffn")
BIG = ("w_in", "w_branch_gla", "w_branch_sgu", "w_out", "w_ffn_in", "w_ffn_out")
MIXER = ("w_branch_gla", "w_branch_sgu", "w_out")
FFN = ("w_ffn_in", "w_ffn_out")
COLUMN_SHARDED = ("w_in", "w_ffn_in")
LATE_SCATTER_WAVES = 4
SMALL = tuple(n for n in WEIGHTS if n not in BIG)
SMALL_SHARDED = ("w_gate_up", "gla_norm", "sgu_ln_g", "sgu_ln_b")
SMALL_FULL = {"norm_pre_mix": (1024,), "w_gate_up": (16, 512), "b_gate": (512,), "gla_norm": (4, 256),
              "sgu_ln_g": (4, 256), "sgu_ln_b": (4, 256), "w_spatial": (4, 128, 128), "b_spatial": (4, 128),
              "norm_post_mix": (1024,), "norm_pre_ffn": (1024,), "norm_post_ffn": (1024,)}
SMALL_GRAD_ROWS = 648
SMALL_STATE_ROWS = 600
SMALL_GATHER_ROWS = 32


def kernel(x, norm_pre_mix, w_in, w_gate_up, b_gate, gla_norm, sgu_ln_g, sgu_ln_b, w_spatial, b_spatial, w_branch_gla, w_branch_sgu, w_out, norm_post_mix, norm_pre_ffn, w_ffn_in, w_ffn_out, norm_post_ffn, loss_target, m_norm_pre_mix, m_w_in, m_w_gate_up, m_b_gate, m_gla_norm, m_sgu_ln_g, m_sgu_ln_b, m_w_spatial, m_b_spatial, m_w_branch_gla, m_w_branch_sgu, m_w_out, m_norm_post_mix, m_norm_pre_ffn, m_w_ffn_in, m_w_ffn_out, m_norm_post_ffn, v_norm_pre_mix, v_w_in, v_w_gate_up, v_b_gate, v_gla_norm, v_sgu_ln_g, v_sgu_ln_b, v_w_spatial, v_b_spatial, v_w_branch_gla, v_w_branch_sgu, v_w_out, v_norm_post_mix, v_norm_pre_ffn, v_w_ffn_in, v_w_ffn_out, v_norm_post_ffn):
    given = dict(locals())
    def local(a, n):
        return a[0].T if n in COLUMN_SHARDED else a[0]

    w = {n: local(given[n], n) for n in WEIGHTS}
    m = {n: local(given["m_" + n], n) for n in WEIGHTS}
    v = {n: local(given["v_" + n], n) for n in WEIGHTS}
    xs, target = x[0], loss_target[0]
    me = 4 * lax.axis_index("x") + 2 * lax.axis_index("y") + lax.axis_index("c")

    small_shard = _pack_rows([w[n] for n in SMALL_SHARDED], SMALL_GATHER_ROWS)
    first = _all_gather_hbm([w["w_in"].astype(BF16), small_shard], name="gather_w_in")
    rest, first = lax.optimization_barrier(([w[n].astype(BF16) for n in MIXER + FFN], first))
    rest_blocks = _all_gather_async(rest, name="gather_rest", collective_id=1)
    W = {n: w[n] for n in SMALL if n not in SMALL_SHARDED}
    blocks = dict(zip(MIXER + FFN, rest_blocks))
    for n in ("w_branch_gla", "w_branch_sgu", "w_out", "w_ffn_out"):
        W[n] = blocks[n].reshape(-1, D_MODEL)
    W["w_ffn_in"] = blocks["w_ffn_in"]
    W["w_in_blocks"] = first[0]
    small_blocks = first[1]
    off = 0
    for n in SMALL_SHARDED:
        r, c = w[n].shape
        blk = small_blocks[:, off:off + r * c // LANES].reshape(N_DEV, r, c)
        W[n] = blk.transpose(1, 0, 2).reshape(r, N_DEV * c)
        off += _tile_rows(r * c)

    scatter_ids = iter((3, 4, 5))

    def scatter(names, parts):
        waves = LATE_SCATTER_WAVES if "w_in" in names else 1
        got = _scatter_blocks_async(parts, name="scatter_" + "_".join(names), collective_id=next(scatter_ids),
                                    waves=waves)
        return dict(zip(names, got))

    def finish(received):
        return {n: _adamw_reduce_call(r, w[n], m[n], v[n], name="adamw_" + n) for n, r in received.items()}

    loss_part, grad_x, grads, big_done = _local_step(xs, target, W, scatter, finish)

    loss_row, small_done = _small_update_call(
        _small_early_rows(grads), grads["norm_pre_mix"], loss_part,
        *[{n: given[prefix + n] for n in SMALL} for prefix in ("", "m_", "v_")], name="small_update")

    def pick(i):
        return [small_done[n][i] if n in SMALL else (big_done[n][i].T if n in COLUMN_SHARDED else big_done[n][i])[None]
                for n in WEIGHTS]

    return (loss_row[0, 0], grad_x[None], *pick(0), *pick(1), *pick(2), *pick(3))
```

```python
import jax
import jax.numpy as jnp
from jax import lax
from jax.experimental import pallas as pl
from jax.experimental.pallas import tpu as pltpu
from jax.experimental.pallas import tpu_sc as plsc

F32 = jnp.float32
BF16 = jnp.bfloat16

D_MODEL = 1024
N_DEV = 8
CHUNK = 64
HEADS = 4
DK = 128
DV = 256
QK = HEADS * DK
GV = HEADS * DV
RANK = 16
GROUPS = 4
SBLOCK = 128
DG = 256
D_FF = 2816
FF_BLK = 704
EPS = 1e-6
Q_SCALE = DK ** -0.5
LANES = 128
VMEM_BIG = 48 * 1024 * 1024

D_IN = 7184
IN_BLK = 898
A_COLS = (0, 3072)
LOW_COLS = (3072, 3088)
S_COLS = (3088, 5136)
G_COLS = (5136, 7184)

ADAM_LR = 0.001
ADAM_B1 = 0.9
ADAM_B2 = 0.999
ADAM_EPS = 1e-08
ADAM_WD = 0.01
ADAM_STEP = 10

MESH = pl.DeviceIdType.MESH
ANY = pl.BlockSpec(memory_space=pl.ANY)
VMEM_SPEC = pl.BlockSpec(memory_space=pltpu.VMEM)


def _cp(sem=None, vmem=None):
    return pltpu.CompilerParams(dimension_semantics=sem, vmem_limit_bytes=vmem)


def _sig(x):
    return 0.5 * jnp.tanh(0.5 * x) + 0.5


GELU_C = 0.7978845608028654
GELU_A = 0.044715


def _gelu(x):
    t = jnp.tanh((GELU_C * x) * (1.0 + GELU_A * (x * x)))
    return (0.5 * x) * (1.0 + t)


def _gelu_and_grad(x):
    x2 = x * x
    t = jnp.tanh((GELU_C * x) * (1.0 + GELU_A * x2))
    one_t = 1.0 + t
    hx = 0.5 * x
    grad = 0.5 * one_t + (hx * (1.0 - t * t)) * (GELU_C + (3.0 * GELU_A * GELU_C) * x2)
    return hx * one_t, grad


def _logsig(x):
    return jnp.minimum(x, 0.0) - jnp.log1p(jnp.exp(-jnp.abs(x)))


def _dot(a, b, dims):
    return lax.dot_general(a, b, (dims, ((), ())), preferred_element_type=F32)


NN = ((1,), (0,))
NT = ((1,), (1,))
TN = ((0,), (0,))


def _exact_mask_dot(mask_bf16, x):
    hi = x.astype(BF16)
    r1 = x - hi.astype(F32)
    mid = r1.astype(BF16)
    lo = (r1 - mid.astype(F32)).astype(BF16)
    return _dot(mask_bf16, hi, NN) + _dot(mask_bf16, mid, NN) + _dot(mask_bf16, lo, NN)


def _pick_tile(dim, target):
    if dim <= target:
        return dim
    best = None
    for t in range(LANES, int(1.4 * target) + 1, LANES):
        if dim % t == 0:
            best = t
    assert best is not None, (dim, target)
    return best


def _mm_call(a, b, *, name, grid, a_spec, b_spec, o_spec, out_shape, dims, acc_shape):
    nk = grid[2]

    def body(a_ref, b_ref, o_ref, *acc):
        part = _dot(a_ref[...], b_ref[...], dims)
        if nk == 1:
            o_ref[...] = part.astype(o_ref.dtype)
        else:
            acc_ref = acc[0]
            k = pl.program_id(2)

            @pl.when(k == 0)
            def _():
                acc_ref[...] = part

            @pl.when(k > 0)
            def _():
                acc_ref[...] += part

            @pl.when(k == nk - 1)
            def _():
                o_ref[...] = acc_ref[...].astype(o_ref.dtype)

    return pl.pallas_call(
        body, name=name, grid=grid, in_specs=[a_spec, b_spec], out_specs=o_spec, out_shape=out_shape,
        scratch_shapes=[] if nk == 1 else [pltpu.VMEM(acc_shape, F32)],
        compiler_params=_cp(("parallel", "parallel", "arbitrary"), VMEM_BIG),
    )(a, b)


def _mm(a, b, mode, out_dtype, *, name, tm=512, tn=1024, tk=1024):
    if mode == "nn":
        (M, K), (_, N) = a.shape, b.shape
    elif mode == "nt":
        (M, K), (N, _) = a.shape, b.shape
    else:
        (K, M), (_, N) = a.shape, b.shape
    tm, tn, tk = _pick_tile(M, tm), _pick_tile(N, tn), _pick_tile(K, tk)
    if mode == "nn":
        a_spec = pl.BlockSpec((tm, tk), lambda i, j, k: (i, k))
        b_spec = pl.BlockSpec((tk, tn), lambda i, j, k: (k, j))
        dims = NN
    elif mode == "nt":
        a_spec = pl.BlockSpec((tm, tk), lambda i, j, k: (i, k))
        b_spec = pl.BlockSpec((tn, tk), lambda i, j, k: (j, k))
        dims = NT
    else:
        a_spec = pl.BlockSpec((tk, tm), lambda i, j, k: (k, i))
        b_spec = pl.BlockSpec((tk, tn), lambda i, j, k: (k, j))
        dims = TN
    return _mm_call(a, b, name=name, grid=(M // tm, N // tn, K // tk), a_spec=a_spec, b_spec=b_spec,
                    o_spec=pl.BlockSpec((tm, tn), lambda i, j, k: (i, j)),
                    out_shape=jax.ShapeDtypeStruct((M, N), out_dtype), dims=dims, acc_shape=(tm, tn))


ROW_TILE = 512
SUB_ROWS = 256


def _sub_tiles(T):
    return [pl.ds(r0, min(SUB_ROWS, T)) for r0 in range(0, T, SUB_ROWS)]


def _rt(T, W, c=0):
    return pl.BlockSpec((T, W), lambda i: (i, c))


def _rt3(nb, T, W):
    return pl.BlockSpec((nb, T, W), lambda i: (0, i, 0))


def _res(shape):
    nd = len(shape)
    return pl.BlockSpec(tuple(shape), lambda i: (0,) * nd, pipeline_mode=pl.Buffered(1))


def _acc(shape):
    nd = len(shape)
    return pl.BlockSpec(tuple(shape), lambda i: (0,) * nd, pipeline_mode=pl.Buffered(1))


def _nbytes(shape, dtype):
    n = jnp.dtype(dtype).itemsize
    for s in shape:
        n *= s
    return n


def _vmem_limit(tiles, resident, temps=16 * 1024 * 1024):
    need = 2 * sum(_nbytes(s, d) for s, d in tiles) + sum(_nbytes(s, d) for s, d in resident) + temps
    return min(need, 60 * 1024 * 1024)


def _tok_call(body, *, name, S, T, ins, outs, semantics="parallel"):
    tiles = [(spec.block_shape, a.dtype) for a, spec, kind in ins + outs if kind == "tile"]
    resident = [(a.shape, a.dtype) for a, spec, kind in ins + outs if kind == "res"]
    return pl.pallas_call(
        body, name=name, grid=(S // T,),
        in_specs=[spec for _, spec, _ in ins], out_specs=[spec for _, spec, _ in outs],
        out_shape=[jax.ShapeDtypeStruct(a.shape, a.dtype) for a, _, _ in outs],
        compiler_params=_cp((semantics,), _vmem_limit(tiles, resident)),
    )(*[a for a, _, _ in ins])


def _tile(a, spec):
    return (a, spec, "tile")


def _whole(a):
    return (a, _res(a.shape), "res")


def _out_tile(shape, dtype, spec):
    return (jax.ShapeDtypeStruct(shape, dtype), spec, "tile")


def _out_acc(shape, dtype=F32):
    return (jax.ShapeDtypeStruct(shape, dtype), _acc(shape), "res")


def _rms_stats(x):
    r = lax.rsqrt(jnp.mean(x * x, axis=-1, keepdims=True) + EPS)
    return r, x * r


def _rms_bwd(xh, r, g, dy):
    dxh = dy * g
    dx = r * (dxh - xh * jnp.mean(dxh * xh, axis=-1, keepdims=True))
    dg = jnp.sum(dy * xh, axis=0, keepdims=True)
    return dx, dg


def _accum(ref, val):
    @pl.when(pl.program_id(0) == 0)
    def _():
        ref[...] = val

    @pl.when(pl.program_id(0) > 0)
    def _():
        ref[...] += val


def _dot_rows_t(a, w_ref, row0, o_ref, chunk=1024):
    n = o_ref.shape[1]
    for n0 in range(0, n, chunk):
        n1 = min(n, n0 + chunk)
        o_ref[:, n0:n1] = _dot(a, w_ref[row0 + n0:row0 + n1, :], NT).astype(o_ref.dtype)


def _in_proj_call(x, g1, w_blocks, *, name, T=ROW_TILE):
    S = x.shape[0]

    def body(x_ref, g_ref, blk_ref, a_ref, pa_ref, al_ref, ps_ref, pg_ref, w_ref):
        @pl.when(pl.program_id(0) == 0)
        def _():
            for d in range(N_DEV):
                w_ref[d * IN_BLK:(d + 1) * IN_BLK, :] = blk_ref[d]

        _, xh = _rms_stats(x_ref[...])
        a = (xh * g_ref[...]).astype(BF16)
        a_ref[...] = a
        _dot_rows_t(a, w_ref, A_COLS[0], pa_ref)
        _dot_rows_t(a, w_ref, LOW_COLS[0], al_ref)
        _dot_rows_t(a, w_ref, S_COLS[0], ps_ref)
        _dot_rows_t(a, w_ref, G_COLS[0], pg_ref)

    widths = (D_MODEL, A_COLS[1] - A_COLS[0], LANES, S_COLS[1] - S_COLS[0], G_COLS[1] - G_COLS[0])
    return _tok_call(
        body, name=name, S=S, T=T, semantics="arbitrary",
        ins=[_tile(x, _rt(T, D_MODEL)), _whole(g1), _whole(w_blocks)],
        outs=[_out_tile((S, w), BF16, _rt(T, w)) for w in widths] + [_out_acc((D_IN, D_MODEL), BF16)])


def _mixer_tail_call(y_gla, y_sgu, pg, x, w_bg, w_bs, w_out, g2, g3, *, name, T=ROW_TILE):
    S = x.shape[0]

    def body(yg_ref, ys_ref, pg_ref, x_ref, wbg_ref, wbs_ref, wo_ref, g2_ref, g3_ref,
             t1_ref, t2_ref, mg_ref, mix_ref, x1_ref, h_ref):
        for rows in _sub_tiles(T):
            t1 = _dot(yg_ref[rows, :], wbg_ref[...], NN)
            t2 = _dot(ys_ref[rows, :], wbs_ref[...], NN)
            t1_ref[rows, :] = t1.astype(BF16)
            t2_ref[rows, :] = t2.astype(BF16)
            sg = _sig(pg_ref[rows, pl.ds(0, D_MODEL)].astype(F32))
            ss = _sig(pg_ref[rows, pl.ds(D_MODEL, D_MODEL)].astype(F32))
            merged = (sg * t1 + ss * t2).astype(BF16)
            mg_ref[rows, :] = merged
            mix = _dot(merged, wo_ref[...], NN)
            mix_ref[rows, :] = mix
            _, mh = _rms_stats(mix)
            x1 = x_ref[rows, :] + mh * g2_ref[...]
            x1_ref[rows, :] = x1
            _, xh = _rms_stats(x1)
            h_ref[rows, :] = (xh * g3_ref[...]).astype(BF16)

    row = _rt(T, D_MODEL)
    b16 = lambda: _out_tile((S, D_MODEL), BF16, row)
    f32 = lambda: _out_tile((S, D_MODEL), F32, row)
    return _tok_call(
        body, name=name, S=S, T=T,
        ins=[_tile(y_gla, row), _tile(y_sgu, row), _tile(pg, _rt(T, 2 * D_MODEL)), _tile(x, row),
             _whole(w_bg), _whole(w_bs), _whole(w_out), _whole(g2), _whole(g3)],
        outs=[b16(), b16(), b16(), f32(), f32(), b16()])


def _ffn_in_call(h, wfi, *, name, T=ROW_TILE):
    S = h.shape[0]

    def body(h_ref, w_ref, gu_ref, z_ref):
        hv = h_ref[...]
        for d in range(4):
            gate = _dot(hv, w_ref[d], NT)
            up = _dot(hv, w_ref[d + 4], NT)
            gu_ref[d] = gate.astype(BF16)
            gu_ref[d + 4] = up.astype(BF16)
            z_ref[d] = (gate * _sig(gate) * up).astype(BF16)

    return _tok_call(
        body, name=name, S=S, T=T,
        ins=[_tile(h, _rt(T, D_MODEL)), _whole(wfi)],
        outs=[_out_tile((N_DEV, S, FF_BLK), BF16, _rt3(N_DEV, T, FF_BLK)),
              _out_tile((4, S, FF_BLK), BF16, _rt3(4, T, FF_BLK))])


def _ffn_out_loss_call(z, wfo, x1, target, g4, *, name, T=ROW_TILE):
    S = x1.shape[0]

    def body(z_ref, w_ref, x1_ref, t_ref, g4_ref, loss_ref, dx2_ref, dy_ref, dg4_ref):
        loss = jnp.zeros((1, 1), F32)
        dg4 = jnp.zeros((1, D_MODEL), F32)
        for rows in _sub_tiles(T):
            y = _dot(z_ref[0, rows, :], w_ref[0], NN)
            for d in range(1, 4):
                y = y + _dot(z_ref[d, rows, :], w_ref[d], NN)
            r, yh = _rms_stats(y)
            diff = x1_ref[rows, :] + yh * g4_ref[...] - t_ref[rows, :]
            loss = loss + 0.5 * jnp.sum(jnp.mean(diff * diff, axis=-1, keepdims=True), axis=0, keepdims=True)
            dx2 = diff * (1.0 / D_MODEL)
            dx2_ref[rows, :] = dx2
            dy, dg = _rms_bwd(yh, r, g4_ref[...], dx2)
            dy_ref[rows, :] = dy.astype(BF16)
            dg4 = dg4 + dg
        _accum(loss_ref, jnp.broadcast_to(loss, (1, LANES)))
        _accum(dg4_ref, dg4)

    row = _rt(T, D_MODEL)
    return _tok_call(
        body, name=name, S=S, T=T, semantics="arbitrary",
        ins=[_tile(z, _rt3(4, T, FF_BLK)), _whole(wfo), _tile(x1, row), _tile(target, row), _whole(g4)],
        outs=[_out_acc((1, LANES)), _out_tile((S, D_MODEL), F32, row), _out_tile((S, D_MODEL), BF16, row),
              _out_acc((1, D_MODEL))])


def _ffn_out_bwd_call(dy, wfo, gu, *, name, T=ROW_TILE):
    S = dy.shape[0]

    def body(dy_ref, w_ref, gu_ref, dgu_ref):
        dyv = dy_ref[...]
        for d in range(4):
            dz = _dot(dyv, w_ref[d], NT)
            gt = gu_ref[d].astype(F32)
            up = gu_ref[d + 4].astype(F32)
            s = _sig(gt)
            dgu_ref[d] = (dz * up * s * (1.0 + gt * (1.0 - s))).astype(BF16)
            dgu_ref[d + 4] = (dz * gt * s).astype(BF16)

    blocks = _rt3(N_DEV, T, FF_BLK)
    return _tok_call(
        body, name=name, S=S, T=T,
        ins=[_tile(dy, _rt(T, D_MODEL)), _whole(wfo), _tile(gu, blocks)],
        outs=[_out_tile((N_DEV, S, FF_BLK), BF16, blocks)])[0]


def _ffn_in_bwd_call(dgu, wfi, dx2, x1, mix, g3, g2, *, name, T=ROW_TILE):
    S = x1.shape[0]

    def body(dgu_ref, w_ref, dx2_ref, x1_ref, mix_ref, g3_ref, g2_ref, dx1_ref, dmix_ref, dg3_ref, dg2_ref):
        dg3 = jnp.zeros((1, D_MODEL), F32)
        dg2 = jnp.zeros((1, D_MODEL), F32)
        for rows in _sub_tiles(T):
            dh = _dot(dgu_ref[0, rows, :], w_ref[0], NN)
            for d in range(1, N_DEV):
                dh = dh + _dot(dgu_ref[d, rows, :], w_ref[d], NN)
            r3, xh = _rms_stats(x1_ref[rows, :])
            d3, g = _rms_bwd(xh, r3, g3_ref[...], dh)
            dg3 = dg3 + g
            dx1 = dx2_ref[rows, :] + d3
            dx1_ref[rows, :] = dx1
            r2, mh = _rms_stats(mix_ref[rows, :])
            dmix, g = _rms_bwd(mh, r2, g2_ref[...], dx1)
            dg2 = dg2 + g
            dmix_ref[rows, :] = dmix.astype(BF16)
        _accum(dg3_ref, dg3)
        _accum(dg2_ref, dg2)

    row = _rt(T, D_MODEL)
    return _tok_call(
        body, name=name, S=S, T=T, semantics="arbitrary",
        ins=[_tile(dgu, _rt3(N_DEV, T, FF_BLK)), _whole(wfi), _tile(dx2, row), _tile(x1, row), _tile(mix, row),
             _whole(g3), _whole(g2)],
        outs=[_out_tile((S, D_MODEL), F32, row), _out_tile((S, D_MODEL), BF16, row),
              _out_acc((1, D_MODEL)), _out_acc((1, D_MODEL))])


def _mixer_bwd_call(dmix, t1, t2, pg, w_out, w_bg, w_bs, *, name, T=ROW_TILE):
    S = dmix.shape[0]

    def body(dmix_ref, t1_ref, t2_ref, pg_ref, wo_ref, wbg_ref, wbs_ref, dt1_ref, dt2_ref, dpg_ref, dyg_ref, dys_ref):
        gg, gs = pl.ds(0, D_MODEL), pl.ds(D_MODEL, D_MODEL)
        for rows in _sub_tiles(T):
            dm = _dot(dmix_ref[rows, :], wo_ref[...], NT)
            sg = _sig(pg_ref[rows, gg].astype(F32))
            ss = _sig(pg_ref[rows, gs].astype(F32))
            dt1 = (dm * sg).astype(BF16)
            dt2 = (dm * ss).astype(BF16)
            dt1_ref[rows, :] = dt1
            dt2_ref[rows, :] = dt2
            dpg_ref[rows, gg] = (dm * t1_ref[rows, :].astype(F32) * sg * (1.0 - sg)).astype(BF16)
            dpg_ref[rows, gs] = (dm * t2_ref[rows, :].astype(F32) * ss * (1.0 - ss)).astype(BF16)
            dyg_ref[rows, :] = _dot(dt1, wbg_ref[...], NT).astype(BF16)
            dys_ref[rows, :] = _dot(dt2, wbs_ref[...], NT).astype(BF16)

    row = _rt(T, D_MODEL)
    wide = _rt(T, 2 * D_MODEL)
    b16 = lambda: _out_tile((S, D_MODEL), BF16, row)
    return _tok_call(
        body, name=name, S=S, T=T,
        ins=[_tile(dmix, row), _tile(t1, row), _tile(t2, row), _tile(pg, wide), _whole(w_out), _whole(w_bg), _whole(w_bs)],
        outs=[b16(), b16(), _out_tile((S, 2 * D_MODEL), BF16, wide), b16(), b16()])


def _in_proj_bwd_call(dpa, dlow, dps, dpg, w_in_t, x, dx1, g1, *, name, T=ROW_TILE):
    S = x.shape[0]

    def body(dpa_ref, dl_ref, dps_ref, dpg_ref, w_ref, x_ref, dx1_ref, g1_ref, gx_ref, dg1_ref):
        da = (_dot(dpa_ref[...], w_ref[A_COLS[0]:A_COLS[1], :], NN)
              + _dot(dl_ref[...], w_ref[LOW_COLS[0]:LOW_COLS[0] + LANES, :], NN)
              + _dot(dps_ref[...], w_ref[S_COLS[0]:S_COLS[1], :], NN)
              + _dot(dpg_ref[...], w_ref[G_COLS[0]:G_COLS[1], :], NN))
        r, xh = _rms_stats(x_ref[...])
        dxa, dg = _rms_bwd(xh, r, g1_ref[...], da)
        gx_ref[...] = dx1_ref[...] + dxa
        _accum(dg1_ref, dg)

    row = _rt(T, D_MODEL)
    return _tok_call(
        body, name=name, S=S, T=T, semantics="arbitrary",
        ins=[_tile(dpa, _rt(T, dpa.shape[1])), _tile(dlow, _rt(T, dlow.shape[1])), _tile(dps, _rt(T, dps.shape[1])),
             _tile(dpg, _rt(T, dpg.shape[1])), _whole(w_in_t), _tile(x, row), _tile(dx1, row), _whole(g1)],
        outs=[_out_tile((S, D_MODEL), F32, row), _out_acc((1, D_MODEL))])


TOKEN_TILE = 512


def _weight_grads_part(arrays, pairs, *, tk=TOKEN_TILE, out_dtype=BF16):
    S = arrays[0].shape[-2]
    tk = min(tk, S)
    n_in = len(arrays)

    def out_shape(i, j):
        a, b = arrays[i], arrays[j]
        if a.ndim == 3:
            return (a.shape[0], a.shape[2], b.shape[1])
        if b.ndim == 3:
            return (b.shape[0], a.shape[1], b.shape[2])
        return (a.shape[1], b.shape[1])

    shapes = [out_shape(i, j) for i, j in pairs]

    in_place = out_dtype == F32

    def body(ins, outs, accs):
        k = pl.program_id(0)
        if in_place:
            accs = outs

        @pl.when(k == 0)
        def _():
            for acc in accs:
                acc[...] = jnp.zeros_like(acc)

        for (i, j), acc in zip(pairs, accs):
            a_ref, b_ref = ins[i], ins[j]
            if len(a_ref.shape) == 3:
                for n in range(a_ref.shape[0]):
                    acc[n] += _dot(a_ref[n], b_ref[...], TN)
            elif len(b_ref.shape) == 3:
                a = a_ref[...]
                for n in range(b_ref.shape[0]):
                    acc[n] += _dot(a, b_ref[n], TN)
            else:
                b = b_ref[...]
                for m0 in range(0, a_ref.shape[1], 1024):
                    m1 = min(a_ref.shape[1], m0 + 1024)
                    acc[m0:m1, :] += _dot(a_ref[:, m0:m1], b, TN)

        if not in_place:
            @pl.when(k == S // tk - 1)
            def _():
                for out, acc in zip(outs, accs):
                    out[...] = acc[...].astype(out.dtype)

    def in_spec(a):
        if a.ndim == 3:
            return pl.BlockSpec((a.shape[0], tk, a.shape[2]), lambda k: (0, k, 0))
        return pl.BlockSpec((tk, a.shape[1]), lambda k: (k, 0))

    return dict(
        body=body, steps=S // tk, arrays=list(arrays),
        in_specs=[in_spec(a) for a in arrays],
        out_specs=[_acc(s) for s in shapes],
        out_shapes=[jax.ShapeDtypeStruct(s, out_dtype) for s in shapes],
        scratch=[] if in_place else [pltpu.VMEM(s, F32) for s in shapes],
        tiles=[(in_spec(a).block_shape, a.dtype) for a in arrays],
        resident=[(s, F32) for s in shapes] + ([] if in_place else [(s, BF16) for s in shapes]))


def _weight_grads_call(arrays, pairs, *, name, tk=TOKEN_TILE):
    part = _weight_grads_part(arrays, pairs, tk=tk)
    n_in, n_out = len(part["arrays"]), len(part["out_shapes"])

    def body(*refs):
        part["body"](refs[:n_in], refs[n_in:n_in + n_out], refs[n_in + n_out:])

    return pl.pallas_call(
        body, name=name, grid=(part["steps"],),
        in_specs=part["in_specs"], out_specs=part["out_specs"], out_shape=part["out_shapes"],
        scratch_shapes=part["scratch"],
        compiler_params=_cp(("arbitrary",), _vmem_limit(part["tiles"], part["resident"])),
    )(*part["arrays"])


def _with_rider(body, n_in, n_out, n_scratch, rider):
    if rider is None:
        return body
    r_in, r_out = len(rider["arrays"]), len(rider["out_shapes"])

    def both(*refs):
        ins, rest = refs[:n_in + r_in], refs[n_in + r_in:]
        outs, scratch = rest[:n_out + r_out], rest[n_out + r_out:]
        body(*ins[:n_in], *outs[:n_out], *scratch[:n_scratch])
        rider["body"](ins[n_in:], outs[n_out:], scratch[n_scratch:])

    return both


def _rider_lists(rider):
    if rider is None:
        return [], [], [], [], [], [], []
    return (rider["arrays"], rider["in_specs"], rider["out_specs"], rider["out_shapes"], rider["scratch"],
            rider["tiles"], rider["resident"])


def _ffn_in_grad_call(h, dgu, *, name, tk=TOKEN_TILE):
    S = h.shape[0]
    tk = min(tk, S)
    nb = 4
    shape = (nb, FF_BLK, D_MODEL)

    def body(h_ref, dgu_ref, out_ref, acc):
        k = pl.program_id(1)

        @pl.when(k == 0)
        def _():
            acc[...] = jnp.zeros_like(acc)

        hv = h_ref[...]
        for n in range(nb):
            acc[n] += _dot(dgu_ref[n], hv, TN)

        @pl.when(k == S // tk - 1)
        def _():
            out_ref[...] = acc[...].astype(out_ref.dtype)

    tiles = [((tk, D_MODEL), BF16), ((nb, tk, FF_BLK), BF16), (shape, BF16)]
    return pl.pallas_call(
        body, name=name, grid=(N_DEV // nb, S // tk),
        in_specs=[pl.BlockSpec((tk, D_MODEL), lambda g, k: (k, 0)),
                  pl.BlockSpec((nb, tk, FF_BLK), lambda g, k: (g, k, 0))],
        out_specs=pl.BlockSpec(shape, lambda g, k: (g, 0, 0)),
        out_shape=jax.ShapeDtypeStruct((N_DEV, FF_BLK, D_MODEL), BF16),
        scratch_shapes=[pltpu.VMEM(shape, F32)],
        compiler_params=_cp(("parallel", "arbitrary"), _vmem_limit(tiles, [(shape, F32)])),
    )(h, dgu)


GLA_TILE = 256
Q_OFF, K_OFF, V_OFF, R_OFF = 0, QK, 2 * QK, 2 * QK + GV


def _tri(lower):
    r = lax.broadcasted_iota(jnp.int32, (CHUNK, CHUNK), 0)
    c = lax.broadcasted_iota(jnp.int32, (CHUNK, CHUNK), 1)
    return jnp.where((r >= c) if lower else (c >= r), 1.0, 0.0).astype(BF16)


def _gla_fwd_call(pa, alow, wg, bgate, gnorm, *, name):
    S = pa.shape[0]
    Tg = min(GLA_TILE, S)
    cb = Tg // CHUNK

    def body(pa_ref, al_ref, wg_ref, bg_ref, gn_ref, y_ref, st_ref, state):
        @pl.when(pl.program_id(0) == 0)
        def _():
            state[...] = jnp.zeros_like(state)

        logit = _dot(al_ref[...], wg_ref[...], NN) + bg_ref[...]
        ls = _logsig(logit) * (1.0 / 16.0)
        tri = _tri(True)
        for c in range(cb):
            rows = pl.ds(c * CHUNK, CHUNK)
            cum = _exact_mask_dot(tri, ls[c * CHUNK:(c + 1) * CHUNK])
            tot = cum[CHUNK - 1:CHUNK]
            kd = (pa_ref[rows, pl.ds(K_OFF, QK)].astype(F32) * jnp.exp(tot - cum)).astype(BF16)
            decay = jnp.exp(tot)
            for h in range(HEADS):
                lanes = slice(h * DK, (h + 1) * DK)
                new = state[h] * decay[:, lanes] + _dot(pa_ref[rows, pl.ds(V_OFF + h * DV, DV)], kd[:, lanes], TN)
                state[h] = new
                st_ref[c, h] = new
        for c in range(cb):
            rows = pl.ds(c * CHUNK, CHUNK)
            for h in range(HEADS):
                qs = (pa_ref[rows, pl.ds(Q_OFF + h * DK, DK)].astype(F32) * Q_SCALE).astype(BF16)
                o = _dot(qs, st_ref[c, h].astype(BF16), NT)
                rs = lax.rsqrt(jnp.mean(o * o, axis=-1, keepdims=True) + EPS)
                rr = pa_ref[rows, pl.ds(R_OFF + h * DV, DV)].astype(F32)
                y_ref[rows, pl.ds(h * DV, DV)] = (o * rs * gn_ref[h] * (rr * _sig(rr))).astype(BF16)

    return pl.pallas_call(
        body, name=name, grid=(S // Tg,),
        in_specs=[
            pl.BlockSpec((Tg, 2 * QK + 2 * GV), lambda t: (t, 0)),
            pl.BlockSpec((Tg, LANES), lambda t: (t, 0)),
            pl.BlockSpec((LANES, QK), lambda t: (0, 0)),
            pl.BlockSpec((1, QK), lambda t: (0, 0)),
            pl.BlockSpec((HEADS, 1, DV), lambda t: (0, 0, 0)),
        ],
        out_specs=[
            pl.BlockSpec((Tg, GV), lambda t: (t, 0)),
            pl.BlockSpec((cb, HEADS, DV, DK), lambda t: (t, 0, 0, 0)),
        ],
        out_shape=[jax.ShapeDtypeStruct((S, GV), BF16),
                   jax.ShapeDtypeStruct((S // CHUNK, HEADS, DV, DK), F32)],
        scratch_shapes=[pltpu.VMEM((HEADS, DV, DK), F32)],
        compiler_params=_cp(("arbitrary",), VMEM_BIG),
    )(pa, alow, wg, bgate, gnorm)


def _gla_bwd_call(pa, alow, wg, bgate, gnorm, states, dy, *, name, rider=None):
    S = pa.shape[0]
    Tg = min(GLA_TILE, S)
    cb = Tg // CHUNK
    nt = S // Tg
    r_arrays, r_in, r_out, r_shapes, r_scratch, r_tiles, r_resident = _rider_lists(rider)
    assert rider is None or rider["steps"] == nt

    def rev(t):
        return nt - 1 - t

    def body(pa_ref, al_ref, wg_ref, bg_ref, gn_ref, st_ref, prev_ref, dy_ref,
             dpa_ref, dl_ref, dgn_ref, dbg_ref, carry, pbuf):
        t = pl.program_id(0)

        @pl.when(t == 0)
        def _():
            carry[...] = jnp.zeros_like(carry)
            dgn_ref[...] = jnp.zeros_like(dgn_ref)
            dbg_ref[...] = jnp.zeros_like(dbg_ref)

        logit = _dot(al_ref[...], wg_ref[...], NN) + bg_ref[...]
        ls = _logsig(logit) * (1.0 / 16.0)
        sneg = 1.0 / (1.0 + jnp.exp(logit))
        tri = _tri(True)
        upper = _tri(False)
        first_tile = rev(t) == 0
        heads = range(HEADS)

        w, decay, kd = [], [], []
        for c in range(cb):
            rows = pl.ds(c * CHUNK, CHUNK)
            cum = _exact_mask_dot(tri, ls[c * CHUNK:(c + 1) * CHUNK])
            tot = cum[CHUNK - 1:CHUNK]
            w.append(jnp.exp(tot - cum))
            decay.append(jnp.exp(tot))
            kd.append(pa_ref[rows, pl.ds(K_OFF, QK)].astype(F32) * w[c])

        dgn = [jnp.zeros((1, DV), F32) for _ in heads]
        for c in range(cb):
            rows = pl.ds(c * CHUNK, CHUNK)
            for h in heads:
                gn = gn_ref[h]
                qs = (pa_ref[rows, pl.ds(Q_OFF + h * DK, DK)].astype(F32) * Q_SCALE).astype(BF16)
                st16 = st_ref[c, h].astype(BF16)
                o = _dot(qs, st16, NT)
                rs = lax.rsqrt(jnp.mean(o * o, axis=-1, keepdims=True) + EPS)
                oh = o * rs
                rr = pa_ref[rows, pl.ds(R_OFF + h * DV, DV)].astype(F32)
                sr = _sig(rr)
                dyv = dy_ref[rows, pl.ds(h * DV, DV)].astype(F32)
                dpa_ref[rows, pl.ds(R_OFF + h * DV, DV)] = (
                    dyv * oh * gn * sr * (1.0 + rr * (1.0 - sr))).astype(BF16)
                don = dyv * (rr * sr)
                dgn[h] = dgn[h] + jnp.sum(don * oh, axis=0, keepdims=True)
                doh = don * gn
                do16 = (rs * (doh - oh * jnp.mean(doh * oh, axis=-1, keepdims=True))).astype(BF16)
                dpa_ref[rows, pl.ds(Q_OFF + h * DK, DK)] = (_dot(do16, st16, NN) * Q_SCALE).astype(BF16)
                pbuf[c, h] = _dot(do16, qs, TN)
        for h in heads:
            dgn_ref[h] += dgn[h]

        dkd = [[None] * HEADS for _ in range(cb)]
        ddecay = [[None] * HEADS for _ in range(cb)]
        for c in reversed(range(cb)):
            rows = pl.ds(c * CHUNK, CHUNK)
            for h in heads:
                lanes = slice(h * DK, (h + 1) * DK)
                gt = pbuf[c, h] + carry[h]
                gt16 = gt.astype(BF16)
                dkd[c][h] = _dot(pa_ref[rows, pl.ds(V_OFF + h * DV, DV)], gt16, NN)
                dpa_ref[rows, pl.ds(V_OFF + h * DV, DV)] = _dot(kd[c][:, lanes].astype(BF16), gt16, NT).astype(BF16)
                if c > 0:
                    st_prev = st_ref[c - 1, h]
                else:
                    st_prev = jnp.where(first_tile, 0.0, prev_ref[0, h])
                ddecay[c][h] = jnp.sum(gt * st_prev, axis=0, keepdims=True)
                carry[h] = gt * decay[c][:, lanes]

        dbg = jnp.zeros((1, QK), F32)
        for c in range(cb):
            rows = pl.ds(c * CHUNK, CHUNK)
            dkd_c = jnp.concatenate(dkd[c], axis=1)
            dpa_ref[rows, pl.ds(K_OFF, QK)] = (dkd_c * w[c]).astype(BF16)
            e = dkd_c * kd[c]
            dtot = jnp.sum(e, axis=0, keepdims=True) + jnp.concatenate(ddecay[c], axis=1) * decay[c]
            dls = dtot - _exact_mask_dot(upper, e)
            dlogit = dls * (1.0 / 16.0) * sneg[c * CHUNK:(c + 1) * CHUNK]
            dl_ref[rows, :] = dlogit.astype(BF16)
            dbg = dbg + jnp.sum(dlogit, axis=0, keepdims=True)
        dbg_ref[...] += dbg

    wide = 2 * QK + 2 * GV
    tiles = [((Tg, wide), BF16), ((cb + 1, HEADS, DV, DK), F32), ((Tg, GV), BF16), ((Tg, wide), BF16),
             ((Tg, QK), BF16)] + r_tiles
    resident = [((cb + 1, HEADS, DV, DK), F32)] + r_resident
    return pl.pallas_call(
        _with_rider(body, 8, 4, 2, rider), name=name, grid=(nt,),
        in_specs=[
            pl.BlockSpec((Tg, wide), lambda t: (rev(t), 0)),
            pl.BlockSpec((Tg, LANES), lambda t: (rev(t), 0)),
            pl.BlockSpec((LANES, QK), lambda t: (0, 0)),
            pl.BlockSpec((1, QK), lambda t: (0, 0)),
            pl.BlockSpec((HEADS, 1, DV), lambda t: (0, 0, 0)),
            pl.BlockSpec((cb, HEADS, DV, DK), lambda t: (rev(t), 0, 0, 0)),
            pl.BlockSpec((1, HEADS, DV, DK), lambda t: (jnp.maximum(rev(t) * cb - 1, 0), 0, 0, 0)),
            pl.BlockSpec((Tg, GV), lambda t: (rev(t), 0)),
        ] + r_in,
        out_specs=[
            pl.BlockSpec((Tg, wide), lambda t: (rev(t), 0)),
            pl.BlockSpec((Tg, QK), lambda t: (rev(t), 0)),
            pl.BlockSpec((HEADS, 1, DV), lambda t: (0, 0, 0)),
            pl.BlockSpec((1, QK), lambda t: (0, 0)),
        ] + r_out,
        out_shape=[jax.ShapeDtypeStruct((S, wide), BF16), jax.ShapeDtypeStruct((S, QK), BF16),
                   jax.ShapeDtypeStruct((HEADS, 1, DV), F32), jax.ShapeDtypeStruct((1, QK), F32)] + r_shapes,
        scratch_shapes=[pltpu.VMEM((HEADS, DV, DK), F32), pltpu.VMEM((cb, HEADS, DV, DK), F32)] + r_scratch,
        compiler_params=_cp(("arbitrary",), _vmem_limit(tiles, resident)),
    )(pa, alow, wg, bgate, gnorm, states, states, dy, *r_arrays)


SGU_TILE = 256


def _sgu_mask():
    r = lax.broadcasted_iota(jnp.int32, (SBLOCK, SBLOCK), 0)
    c = lax.broadcasted_iota(jnp.int32, (SBLOCK, SBLOCK), 1)
    return (c < CHUNK) | (r >= CHUNK)


def _ln_stats(vf):
    mu = jnp.mean(vf, axis=-1, keepdims=True)
    xc = vf - mu
    rs = lax.rsqrt(jnp.mean(xc * xc, axis=-1, keepdims=True) + EPS)
    return rs, xc * rs


def _sgu_fwd_call(ps, ln_g, ln_b, w_sp, b_sp, *, name):
    S = ps.shape[0]
    Ts = min(SGU_TILE, S)

    def body(ps_ref, lg_ref, lb_ref, w_ref, b_ref, y_ref):
        mask = _sgu_mask()
        for g in range(GROUPS):
            wm = jnp.where(mask, w_ref[g], 0.0).astype(BF16)
            for p in range(Ts // SBLOCK):
                rows = pl.ds(p * SBLOCK, SBLOCK)
                u = _gelu(ps_ref[rows, pl.ds(g * DG, DG)].astype(F32))
                _, xh = _ln_stats(_gelu(ps_ref[rows, pl.ds(D_MODEL + g * DG, DG)].astype(F32)))
                vn = xh * lg_ref[g] + lb_ref[g]
                mixed = _dot(wm, vn.astype(BF16), NN) + b_ref[g]
                y_ref[rows, pl.ds(g * DG, DG)] = (u * mixed).astype(BF16)

    full3 = lambda a, b, c: pl.BlockSpec((a, b, c), lambda t: (0, 0, 0))
    return pl.pallas_call(
        body, name=name, grid=(S // Ts,),
        in_specs=[pl.BlockSpec((Ts, 2 * D_MODEL), lambda t: (t, 0)),
                  full3(GROUPS, 1, DG), full3(GROUPS, 1, DG), full3(GROUPS, SBLOCK, SBLOCK), full3(GROUPS, SBLOCK, 1)],
        out_specs=pl.BlockSpec((Ts, D_MODEL), lambda t: (t, 0)),
        out_shape=jax.ShapeDtypeStruct((S, D_MODEL), BF16),
        compiler_params=_cp(("parallel",)),
    )(ps, ln_g, ln_b, w_sp, b_sp)


def _sgu_bwd_call(ps, ln_g, ln_b, w_sp, b_sp, dy, *, name, rider=None):
    S = ps.shape[0]
    Ts = min(SGU_TILE, S)
    r_arrays, r_in, r_out, r_shapes, r_scratch, r_tiles, r_resident = _rider_lists(rider)
    assert rider is None or rider["steps"] == S // Ts

    def body(ps_ref, lg_ref, lb_ref, w_ref, b_ref, dy_ref, ds_ref, dlg_ref, dlb_ref, dw_ref, db_ref):
        @pl.when(pl.program_id(0) == 0)
        def _():
            dlg_ref[...] = jnp.zeros_like(dlg_ref)
            dlb_ref[...] = jnp.zeros_like(dlb_ref)
            dw_ref[...] = jnp.zeros_like(dw_ref)
            db_ref[...] = jnp.zeros_like(db_ref)

        mask = _sgu_mask()
        for g in range(GROUPS):
            wm = jnp.where(mask, w_ref[g], 0.0).astype(BF16)
            lg = lg_ref[g]
            for p in range(Ts // SBLOCK):
                rows = pl.ds(p * SBLOCK, SBLOCK)
                su = ps_ref[rows, pl.ds(g * DG, DG)].astype(F32)
                sv = ps_ref[rows, pl.ds(D_MODEL + g * DG, DG)].astype(F32)
                u, du = _gelu_and_grad(su)
                gv, dgv = _gelu_and_grad(sv)
                rs, xh = _ln_stats(gv)
                vn16 = (xh * lg + lb_ref[g]).astype(BF16)
                mixed = _dot(wm, vn16, NN) + b_ref[g]
                dyv = dy_ref[rows, pl.ds(g * DG, DG)].astype(F32)
                ds_ref[rows, pl.ds(g * DG, DG)] = (dyv * mixed * du).astype(BF16)
                dmix = dyv * u
                dmix16 = dmix.astype(BF16)
                db_ref[g] += jnp.sum(dmix, axis=-1, keepdims=True)
                dw_ref[g] += jnp.where(mask, _dot(dmix16, vn16, NT), 0.0)
                dvn = _dot(wm, dmix16, TN)
                dlg_ref[g] += jnp.sum(dvn * xh, axis=0, keepdims=True)
                dlb_ref[g] += jnp.sum(dvn, axis=0, keepdims=True)
                dxh = dvn * lg
                dvf = rs * (dxh - jnp.mean(dxh, axis=-1, keepdims=True)
                            - xh * jnp.mean(dxh * xh, axis=-1, keepdims=True))
                ds_ref[rows, pl.ds(D_MODEL + g * DG, DG)] = (dvf * dgv).astype(BF16)

    full3 = lambda a, b, c: pl.BlockSpec((a, b, c), lambda t: (0, 0, 0))
    tiles = [((Ts, 2 * D_MODEL), BF16), ((Ts, D_MODEL), BF16), ((Ts, 2 * D_MODEL), BF16)] + r_tiles
    return pl.pallas_call(
        _with_rider(body, 6, 5, 0, rider), name=name, grid=(S // Ts,),
        in_specs=[pl.BlockSpec((Ts, 2 * D_MODEL), lambda t: (t, 0)),
                  full3(GROUPS, 1, DG), full3(GROUPS, 1, DG), full3(GROUPS, SBLOCK, SBLOCK), full3(GROUPS, SBLOCK, 1),
                  pl.BlockSpec((Ts, D_MODEL), lambda t: (t, 0))] + r_in,
        out_specs=[pl.BlockSpec((Ts, 2 * D_MODEL), lambda t: (t, 0)),
                   full3(GROUPS, 1, DG), full3(GROUPS, 1, DG), full3(GROUPS, SBLOCK, SBLOCK), full3(GROUPS, SBLOCK, 1)]
        + r_out,
        out_shape=[jax.ShapeDtypeStruct((S, 2 * D_MODEL), BF16),
                   jax.ShapeDtypeStruct((GROUPS, 1, DG), F32), jax.ShapeDtypeStruct((GROUPS, 1, DG), F32),
                   jax.ShapeDtypeStruct((GROUPS, SBLOCK, SBLOCK), F32),
                   jax.ShapeDtypeStruct((GROUPS, SBLOCK, 1), F32)] + r_shapes,
        scratch_shapes=r_scratch,
        compiler_params=_cp(("arbitrary",), _vmem_limit(tiles, r_resident)),
    )(ps, ln_g, ln_b, w_sp, b_sp, dy, *r_arrays)


def _position():
    return lax.axis_index("x"), lax.axis_index("y"), lax.axis_index("c")


def _gather_copies(srcs, dsts, send_sems, recv_sems, local_sems, relay=False):
    if relay:
        return _gather_copies_relayed(srcs, dsts, send_sems, recv_sems, local_sems)
    x, y, c = _position()
    me, sibling = (x, y, c), (x, y, 1 - c)
    chips = [(1 - x, y), (x, 1 - y), (1 - x, 1 - y)]
    n = len(srcs)

    def slab(a, block):
        px, py, pc = block
        return dsts[a].at[4 * px + 2 * py + pc]

    def copy(a, k, block, to, src=None):
        return pltpu.make_async_remote_copy(
            src_ref=slab(a, block) if src is None else src, dst_ref=slab(a, block),
            send_sem=send_sems.at[7 * a + k], recv_sem=recv_sems.at[7 * a + k], device_id=to, device_id_type=MESH)

    mine = [pltpu.make_async_copy(srcs[a], slab(a, me), local_sems.at[a]) for a in range(n)]
    for cp in mine:
        cp.start()
    first = []
    for a in range(n):
        first.append(copy(a, 0, me, sibling, src=srcs[a]))
        first += [copy(a, 1 + j, me, (*chip, c), src=srcs[a]) for j, chip in enumerate(chips)]
    for cp in first:
        cp.start()
    passed = []
    for j, chip in enumerate(chips):
        for a in range(n):
            copy(a, 1 + j, (*chip, c), me).wait_recv()
            cp = copy(a, 4 + j, (*chip, c), sibling)
            cp.start()
            passed.append(cp)
    for a in range(n):
        copy(a, 0, sibling, me).wait_recv()
        for j, chip in enumerate(chips):
            copy(a, 4 + j, (*chip, 1 - c), me).wait_recv()
    for cp in first + passed:
        cp.wait_send()
    for cp in mine:
        cp.wait()


def _gather_copies_relayed(srcs, dsts, send_sems, recv_sems, local_sems):
    x, y, c = _position()
    me, sibling = (x, y, c), (x, y, 1 - c)
    x_chip, y_chip, d_chip = (1 - x, y), (x, 1 - y), (1 - x, 1 - y)
    south = c == 0
    relay_from = (jnp.where(south, x, 1 - x), jnp.where(south, 1 - y, y), c)
    relay_to = (jnp.where(south, 1 - x, x), jnp.where(south, y, 1 - y), c)
    n = len(srcs)

    def slab(a, block):
        px, py, pc = block
        return dsts[a].at[4 * px + 2 * py + pc]

    def copy(a, k, block, to, src=None):
        return pltpu.make_async_remote_copy(
            src_ref=slab(a, block) if src is None else src, dst_ref=slab(a, block),
            send_sem=send_sems.at[7 * a + k], recv_sem=recv_sems.at[7 * a + k], device_id=to, device_id_type=MESH)

    mine = [pltpu.make_async_copy(srcs[a], slab(a, me), local_sems.at[a]) for a in range(n)]
    for cp in mine:
        cp.start()
    sent = []
    for a in range(n):
        sent += [copy(a, 0, me, sibling, src=srcs[a]), copy(a, 1, me, (*x_chip, c), src=srcs[a]),
                 copy(a, 2, me, (*y_chip, c), src=srcs[a])]
    for cp in sent:
        cp.start()
    for a in range(n):
        copy(a, 1, (*x_chip, c), me).wait_recv()
        copy(a, 2, (*y_chip, c), me).wait_recv()
        later = [copy(a, 3, relay_from, relay_to), copy(a, 4, (*x_chip, c), sibling), copy(a, 5, (*y_chip, c), sibling)]
        for cp in later:
            cp.start()
        sent += later
    for a in range(n):
        copy(a, 3, (*d_chip, c), me).wait_recv()
        cp = copy(a, 6, (*d_chip, c), sibling)
        cp.start()
        sent.append(cp)
    for a in range(n):
        copy(a, 0, sibling, me).wait_recv()
        for k, chip in ((4, x_chip), (5, y_chip), (6, d_chip)):
            copy(a, k, (*chip, 1 - c), me).wait_recv()
    for cp in sent:
        cp.wait_send()
    for cp in mine:
        cp.wait()


def _all_gather_hbm(shards, *, name):
    n = len(shards)

    def body(*refs):
        srcs, dsts = refs[:n], refs[n:2 * n]
        send_sems, recv_sems, local_sems = refs[2 * n:]
        _gather_copies(srcs, dsts, send_sems, recv_sems, local_sems, relay=True)

    return pl.pallas_call(
        body, name=name,
        in_specs=[ANY] * n, out_specs=[ANY] * n,
        out_shape=[jax.ShapeDtypeStruct((N_DEV, *s.shape), s.dtype) for s in shards],
        scratch_shapes=_comm_sems(n),
    )(*shards)


def _all_reduce_small(part, *, name):
    R, W = part.shape

    def body(x_ref, out_ref, gathered, send_sems, recv_sems, local_sems):
        _gather_copies([x_ref], [gathered], send_sems, recv_sems, local_sems)
        acc = gathered[0]
        for d in range(1, N_DEV):
            acc = acc + gathered[d]
        out_ref[...] = acc

    return pl.pallas_call(
        body, name=name,
        in_specs=[VMEM_SPEC], out_specs=VMEM_SPEC,
        out_shape=jax.ShapeDtypeStruct((R, W), F32),
        scratch_shapes=[pltpu.VMEM((N_DEV, R, W), F32),
                        pltpu.SemaphoreType.DMA((7,)), pltpu.SemaphoreType.DMA((7,)), pltpu.SemaphoreType.DMA((1,))],
    )(part)


FLIPS = [(fx, fy, fc) for fx in (0, 1) for fy in (0, 1) for fc in (0, 1)][1:]


def _scatter_copies(srcs, dsts, send_sems, recv_sems, local_sems, waves=1):
    n = len(srcs)
    x, y, c = _position()
    me = 4 * x + 2 * y + c
    mine = [pltpu.make_async_copy(srcs[a].at[me], dsts[a].at[me], local_sems.at[a]) for a in range(n)]
    for cp in mine:
        cp.start()
    peers = []
    for fx, fy, fc in FLIPS:
        tx = 1 - x if fx else x
        ty = 1 - y if fy else y
        tc = 1 - c if fc else c
        peers.append(((tx, ty, tc), 4 * tx + 2 * ty + tc))
    copies = []
    for w in range(waves):
        wave = []
        for k, (peer_id, peer) in enumerate(peers):
            for a in range(n):
                rows = srcs[a].shape[1]
                step = rows if waves == 1 else (rows // waves) // 16 * 16
                r0 = w * step
                cut = pl.ds(r0, step if w < waves - 1 else rows - r0)
                sem = (7 * a + k) * waves + w
                cp = pltpu.make_async_remote_copy(
                    src_ref=srcs[a].at[peer, cut], dst_ref=dsts[a].at[me, cut],
                    send_sem=send_sems.at[sem], recv_sem=recv_sems.at[sem],
                    device_id=peer_id, device_id_type=MESH)
                cp.start()
                wave.append(cp)
        for cp in wave:
            cp.wait_send()
        copies += wave
    for cp in copies:
        cp.wait_recv()
    for cp in mine:
        cp.wait()


def _comm_sems(n, waves=1):
    return [pltpu.SemaphoreType.DMA((7 * n * waves,)), pltpu.SemaphoreType.DMA((7 * n * waves,)),
            pltpu.SemaphoreType.DMA((n,))]


def _scatter_blocks(parts, *, name, waves=1):
    n = len(parts)

    def body(*refs):
        _scatter_copies(refs[:n], refs[n:2 * n], *refs[2 * n:], waves=waves)

    return pl.pallas_call(
        body, name=name,
        in_specs=[ANY] * n, out_specs=[ANY] * n,
        out_shape=[jax.ShapeDtypeStruct(p.shape, p.dtype) for p in parts],
        scratch_shapes=_comm_sems(n, waves),
    )(*parts)


def _handshake(peers):
    barrier = pltpu.get_barrier_semaphore()
    for peer in peers:
        pl.semaphore_signal(barrier, inc=1, device_id=peer, device_id_type=MESH)
    pl.semaphore_wait(barrier, len(peers))


def _sequencer_call(arrays, out_types, copies_fn, peers_fn, *, name, collective_id, waves=1):
    n = len(arrays)
    srcs = [jax.new_ref(a, memory_space=pltpu.MemorySpace.HBM) for a in arrays]
    dsts = [jax.empty_ref(t, memory_space=pltpu.MemorySpace.HBM) for t in out_types]
    extra = {} if waves == 1 else {"waves": waves}

    @pl.kernel(mesh=plsc.ScalarSubcoreMesh(axis_name="sequencer", num_cores=1), name=name,
               scratch_types=_comm_sems(n, waves), compiler_params=pltpu.CompilerParams(collective_id=collective_id))
    def launch(send_sems, recv_sems, local_sems):
        _handshake(peers_fn())
        copies_fn(srcs, dsts, send_sems, recv_sems, local_sems, **extra)

    launch()
    return [d[...] for d in dsts]


def _all_other_devices():
    x, y, c = _position()
    return [(1 - x if fx else x, 1 - y if fy else y, 1 - c if fc else c) for fx, fy, fc in FLIPS]


def _gather_relay_peers():
    x, y, c = _position()
    return [(x, y, 1 - c), (1 - x, y, c), (x, 1 - y, c)]


def _scatter_blocks_async(parts, *, name, collective_id, waves=1):
    return _sequencer_call(parts, [jax.ShapeDtypeStruct(p.shape, p.dtype) for p in parts],
                           _scatter_copies, _all_other_devices, name=name, collective_id=collective_id, waves=waves)


def _all_gather_async(shards, *, name, collective_id):
    return _sequencer_call(shards, [jax.ShapeDtypeStruct((N_DEV, *s.shape), s.dtype) for s in shards],
                           _gather_copies_relayed, _gather_relay_peers, name=name, collective_id=collective_id)


def _adamw_math(w, g, m, v):
    m = ADAM_B1 * m + (1.0 - ADAM_B1) * g
    v = ADAM_B2 * v + (1.0 - ADAM_B2) * (g * g)
    m_hat = m / (1.0 - ADAM_B1 ** ADAM_STEP)
    v_hat = v / (1.0 - ADAM_B2 ** ADAM_STEP)
    delta = -ADAM_LR * (m_hat / (jnp.sqrt(v_hat) + ADAM_EPS) + ADAM_WD * w)
    return delta, m, v


def _adamw_reduce_call(recv, w, m, v, *, name, T=128):
    R, W = w.shape
    if R % T == 0:
        tr, tw = T, W
    elif (R // 2) % 16 == 0:
        tr, tw = R // 2, W
    else:
        tr, tw = R, 2 * LANES

    def body(p_ref, w_ref, m_ref, v_ref, g_out, d_out, m_out, v_out):
        g = p_ref[0].astype(F32)
        for d in range(1, N_DEV):
            g = g + p_ref[d].astype(F32)
        g_out[...] = g
        d_out[...], m_out[...], v_out[...] = _adamw_math(w_ref[...], g, m_ref[...], v_ref[...])

    row = pl.BlockSpec((tr, tw), lambda i, j: (i, j))
    out = jax.ShapeDtypeStruct((R, W), F32)
    return pl.pallas_call(
        body, name=name, grid=(R // tr, W // tw),
        in_specs=[pl.BlockSpec((N_DEV, tr, tw), lambda i, j: (0, i, j)), row, row, row],
        out_specs=[row] * 4, out_shape=[out] * 4,
        compiler_params=_cp(("parallel", "parallel"), VMEM_BIG),
    )(recv, w, m, v)


SMALL_EARLY = (("b_gate", 8), ("w_spatial", 512), ("b_spatial", 8), ("norm_post_mix", 8), ("norm_pre_ffn", 8),
               ("norm_post_ffn", 8), ("w_gate_up", 128), ("gla_norm", 64), ("sgu_ln_g", 64), ("sgu_ln_b", 64))
SMALL_EARLY_AT = {}
for _name, _rows in SMALL_EARLY:
    SMALL_EARLY_AT[_name] = sum(r for _, r in SMALL_EARLY[:len(SMALL_EARLY_AT)])
SMALL_EARLY_ROWS = sum(r for _, r in SMALL_EARLY)
SMALL_LATE_ROWS = 16


def _small_early_rows(grads):
    def rows(a, n_rows):
        a = a.reshape(-1, LANES)
        return jnp.pad(a, ((0, n_rows - a.shape[0]), (0, 0)))

    def device_major(a, n_rows):
        r, c = a.shape[0], a.shape[1] // N_DEV
        a = a.reshape(r, N_DEV, c).transpose(1, 0, 2)
        a = jnp.pad(a, ((0, 0), (0, n_rows // N_DEV - r), (0, LANES - c)))
        return a.reshape(n_rows, LANES)

    pieces = []
    for name, n_rows in SMALL_EARLY:
        g = grads[name]
        if name in SMALL_SHARDED:
            pieces.append(device_major(g.reshape(g.shape[0], -1) if g.ndim == 2 else g.reshape(g.shape[0], g.shape[-1]), n_rows))
        else:
            pieces.append(rows(g, n_rows))
    return jnp.concatenate(pieces, axis=0)


def _small_update_call(early, dg_pre_mix, loss_part, w, m, v, *, name):
    names = list(SMALL)
    n_p = len(names)
    E = SMALL_EARLY_ROWS

    def reduce_body(early_ref, dg1_ref, loss_ref, tot, got_early, got_late, late, send_sems, recv_sems, local_sems):
        for k in range(D_MODEL // LANES):
            late[k:k + 1, :] = dg1_ref[:, k * LANES:(k + 1) * LANES]
        late[8:16, :] = jnp.broadcast_to(loss_ref[...], (8, LANES))
        _gather_copies([early_ref, late], [got_early, got_late], send_sems, recv_sems, local_sems)
        acc, acc_late = got_early[0], got_late[0]
        for d in range(1, N_DEV):
            acc, acc_late = acc + got_early[d], acc_late + got_late[d]
        tot[0:E, :] = acc
        tot[E:E + SMALL_LATE_ROWS, :] = acc_late

    total = pl.pallas_call(
        reduce_body, name=name + "_reduce",
        in_specs=[VMEM_SPEC] * 3, out_specs=VMEM_SPEC,
        out_shape=jax.ShapeDtypeStruct((E + SMALL_LATE_ROWS, LANES), F32),
        scratch_shapes=[pltpu.VMEM((N_DEV, E, LANES), F32), pltpu.VMEM((N_DEV, SMALL_LATE_ROWS, LANES), F32),
                        pltpu.VMEM((SMALL_LATE_ROWS, LANES), F32),
                        pltpu.SemaphoreType.DMA((14,)), pltpu.SemaphoreType.DMA((14,)), pltpu.SemaphoreType.DMA((2,))],
    )(early, dg_pre_mix, loss_part)

    def body(tot, *rest):
        w_refs, m_refs, v_refs = [dict(zip(names, rest[i * n_p:(i + 1) * n_p])) for i in range(3)]
        outs = rest[3 * n_p:]
        loss_out = outs[0]
        g_out, d_out, m_out, v_out = [dict(zip(names, outs[1 + i * n_p:1 + (i + 1) * n_p])) for i in range(4)]
        loss_out[...] = tot[E + 8:E + 9, :]

        x, y, c = _position()
        me = 4 * x + 2 * y + c

        def update(name, g, ix):
            g_out[name][ix] = g
            d_out[name][ix], m_out[name][ix], v_out[name][ix] = _adamw_math(
                w_refs[name][ix], g, m_refs[name][ix], v_refs[name][ix])

        for name in names:
            shape = w[name].shape
            if name in SMALL_SHARDED:
                per_dev = dict(SMALL_EARLY)[name] // N_DEV
                at = pl.multiple_of(SMALL_EARLY_AT[name] + me * per_dev, 8)
                g = tot[pl.ds(at, per_dev), :]
                update(name, g[:shape[1], :shape[2]], (0,))
            elif name == "w_spatial":
                for grp in range(GROUPS):
                    at = SMALL_EARLY_AT[name] + grp * SBLOCK
                    update(name, tot[at:at + SBLOCK, :], (0, grp))
            elif name == "b_spatial":
                at = SMALL_EARLY_AT[name]
                update(name, tot[at:at + GROUPS, :], (0,))
            else:
                at = E if name == "norm_pre_mix" else SMALL_EARLY_AT[name]
                for k in range(shape[1] // LANES):
                    update(name, tot[at + k:at + k + 1, :], (slice(None), pl.ds(k * LANES, LANES)))

    state = [s[n] for s in (w, m, v) for n in names]
    out_shapes = [jax.ShapeDtypeStruct((1, LANES), F32)] + [jax.ShapeDtypeStruct(w[n].shape, F32) for n in names] * 4
    outs = pl.pallas_call(
        body, name=name + "_adamw",
        in_specs=[VMEM_SPEC] * (1 + len(state)), out_specs=[VMEM_SPEC] * len(out_shapes), out_shape=out_shapes,
    )(total, *state)
    per_name = {n: tuple(outs[1 + i * n_p + j] for i in range(4)) for j, n in enumerate(names)}
    return outs[0], per_name


def _tile_rows(n_elems):
    return -(-n_elems // (8 * LANES)) * 8


def _pack_rows(parts, rows):
    pieces = []
    for p in parts:
        q = p.reshape(-1, LANES)
        pieces.append(jnp.pad(q, ((0, _tile_rows(p.size) - q.shape[0]), (0, 0))))
    buf = jnp.concatenate(pieces, axis=0)
    return jnp.pad(buf, ((0, rows - buf.shape[0]), (0, 0)))


def _unpack_rows(buf, shapes):
    out, off = [], 0
    for shp in shapes:
        n = 1
        for s in shp:
            n *= s
        out.append(buf[off:off + n // LANES].reshape(shp))
        off += _tile_rows(n)
    return out


def _in_w_blocks_call(a, dps, dpg, dw_a, dw_low, *, name, tk=TOKEN_TILE):
    S = a.shape[0]
    tk = min(tk, S)
    steps = S // tk
    n_s, n_g = dps.shape[1], dpg.shape[1]
    out_shape = (N_DEV, IN_BLK, D_MODEL)

    def body(a_ref, dps_ref, dpg_ref, dwa_ref, dwl_ref, out_ref, acc_s, acc_g):
        k = pl.program_id(0)

        @pl.when(k == 0)
        def _():
            acc_s[...] = jnp.zeros_like(acc_s)
            acc_g[...] = jnp.zeros_like(acc_g)

        av = a_ref[...]
        for src, acc, n in ((dps_ref, acc_s, n_s), (dpg_ref, acc_g, n_g)):
            for m0 in range(0, n, 1024):
                acc[m0:m0 + 1024, :] += _dot(src[:, m0:m0 + 1024], av, TN)

        @pl.when(k == steps - 1)
        def _():
            groups = ((dwa_ref, A_COLS), (dwl_ref, LOW_COLS), (acc_s, S_COLS), (acc_g, G_COLS))
            for d in range(N_DEV):
                lo, hi = IN_BLK * d, IN_BLK * (d + 1)
                for ref, (c0, c1) in groups:
                    s0, e0 = max(lo, c0), min(hi, c1)
                    if s0 < e0:
                        out_ref[d, s0 - lo:e0 - lo, :] = ref[s0 - c0:e0 - c0, :].astype(BF16)

    tiles = [((tk, D_MODEL), BF16), ((tk, n_s), BF16), ((tk, n_g), BF16)]
    resident = [(dw_a.shape, BF16), (dw_low.shape, BF16), (out_shape, BF16), ((n_s, D_MODEL), F32), ((n_g, D_MODEL), F32)]
    return pl.pallas_call(
        body, name=name, grid=(steps,),
        in_specs=[pl.BlockSpec((tk, D_MODEL), lambda k: (k, 0)), pl.BlockSpec((tk, n_s), lambda k: (k, 0)),
                  pl.BlockSpec((tk, n_g), lambda k: (k, 0)), _res(dw_a.shape), _res(dw_low.shape)],
        out_specs=_acc(out_shape),
        out_shape=jax.ShapeDtypeStruct(out_shape, BF16),
        scratch_shapes=[pltpu.VMEM((n_s, D_MODEL), F32), pltpu.VMEM((n_g, D_MODEL), F32)],
        compiler_params=_cp(("arbitrary",), _vmem_limit(tiles, resident)),
    )(a, dps, dpg, dw_a, dw_low)


def _local_step(x, target, W, scatter, finish):
    g1, g2, g3, g4 = [W[n].reshape(1, D_MODEL) for n in ("norm_pre_mix", "norm_post_mix", "norm_pre_ffn", "norm_post_ffn")]
    wfi, wfo = W["w_ffn_in"], W["w_ffn_out"].reshape(4, FF_BLK, D_MODEL)
    wg = jnp.pad(W["w_gate_up"], ((0, LANES - RANK), (0, 0))).astype(BF16)
    bgate = W["b_gate"].reshape(1, QK)
    gnorm = W["gla_norm"].reshape(HEADS, 1, DV)
    ln_g = W["sgu_ln_g"].reshape(GROUPS, 1, DG)
    ln_b = W["sgu_ln_b"].reshape(GROUPS, 1, DG)
    w_sp = W["w_spatial"]
    b_sp = W["b_spatial"].reshape(GROUPS, SBLOCK, 1)

    a, pa, alow, ps, pg, w_in_t = _in_proj_call(x, g1, W["w_in_blocks"], name="in_proj")
    y_gla, states = _gla_fwd_call(pa, alow, wg, bgate, gnorm, name="gla_fwd")
    y_sgu = _sgu_fwd_call(ps, ln_g, ln_b, w_sp, b_sp, name="sgu_fwd")
    t1, t2, merged, mix, x1, h = _mixer_tail_call(y_gla, y_sgu, pg, x, W["w_branch_gla"], W["w_branch_sgu"],
                                                  W["w_out"], g2, g3, name="mixer_tail")
    gu, z = _ffn_in_call(h, wfi, name="ffn_in")
    loss, dx2, dy, dg4 = _ffn_out_loss_call(z, wfo, x1, target, g4, name="ffn_out_loss")

    grads = {"norm_post_ffn": dg4}
    done = {}
    dgu = _ffn_out_bwd_call(dy, wfo, gu, name="ffn_out_bwd")
    dw_ffn_out = _weight_grads_call([z, dy], [(0, 1)], name="d_ffn_out_w")[0].reshape(N_DEV, D_FF // N_DEV, D_MODEL)
    dw_ffn_in = _ffn_in_grad_call(h, dgu, name="d_ffn_in_w")
    dgu, dw_ffn_out, dw_ffn_in = lax.optimization_barrier((dgu, dw_ffn_out, dw_ffn_in))
    ffn_received = scatter(("w_ffn_out", "w_ffn_in"), [dw_ffn_out, dw_ffn_in])
    dx1, dmix, grads["norm_pre_ffn"], grads["norm_post_mix"] = _ffn_in_bwd_call(dgu, wfi, dx2, x1, mix, g3, g2, name="ffn_in_bwd")
    dt1, dt2, dpg, dy_gla, dy_sgu = _mixer_bwd_call(dmix, t1, t2, pg, W["w_out"], W["w_branch_gla"], W["w_branch_sgu"],
                                                    name="mixer_bwd")
    rows = D_MODEL // N_DEV
    mixer_grads = _weight_grads_call([merged, dmix, y_gla, dt1, y_sgu, dt2], [(0, 1), (2, 3), (4, 5)], name="d_mixer_w")
    dy_gla, dy_sgu, mixer_grads = lax.optimization_barrier((dy_gla, dy_sgu, mixer_grads))
    mixer_received = scatter(("w_out", "w_branch_gla", "w_branch_sgu"),
                             [g.reshape(N_DEV, rows, D_MODEL) for g in mixer_grads])
    dpa, dlogit, dgn, dbg = _gla_bwd_call(pa, alow, wg, bgate, gnorm, states, dy_gla, name="gla_bwd")
    ffn_received, dpa, dlogit = lax.optimization_barrier((ffn_received, dpa, dlogit))
    dps, dlg, dlb, dwsp, dbsp = _sgu_bwd_call(ps, ln_g, ln_b, w_sp, b_sp, dy_sgu, name="sgu_bwd")
    mixer_received, dps = lax.optimization_barrier((mixer_received, dps))
    dlow = _mm(dlogit, wg, "nt", BF16, name="d_gate_up_x")
    dw_a, dw_low = _weight_grads_call([a, dpa, dlow], [(1, 0), (2, 0)], name="d_in_w_qkvr")
    dw_in = _in_w_blocks_call(a, dps, dpg, dw_a, dw_low, name="d_in_w_blocks")
    ffn_received, mixer_received, dw_in = lax.optimization_barrier((ffn_received, mixer_received, dw_in))
    in_received = scatter(("w_in",), [dw_in])
    early_done = {**finish(ffn_received), **finish(mixer_received)}
    dwg = _mm(alow, dlogit, "tn", F32, name="d_gate_up_w")
    grad_x, dg1 = _in_proj_bwd_call(dpa, dlow, dps, dpg, w_in_t, x, dx1, g1, name="in_proj_bwd")
    early_done, dg1 = lax.optimization_barrier((early_done, dg1))
    done.update(early_done)
    grads["norm_pre_mix"] = dg1

    grads["w_gate_up"] = dwg[:RANK]
    grads["b_gate"] = dbg
    grads["gla_norm"] = dgn
    grads["sgu_ln_g"] = dlg
    grads["sgu_ln_b"] = dlb
    grads["w_spatial"] = dwsp
    grads["b_spatial"] = dbsp
    done.update(finish(in_received))
    return loss, grad_x, grads, done


WEIGHTS = ("norm_pre_mix", "w_in", "w_gate_up", "b_gate", "gla_norm", "sgu_ln_g", "sgu_ln_b", "w_spatial",
           "b_spatial", "w_branch_gla", "w_branch_sgu", "w_out", "norm_post_mix", "norm_pre_ffn", "w_ffn_in",
           "w_ffn_out", "norm_post_ffn")
BIG = ("w_in", "w_branch_gla", "w_branch_sgu", "w_out", "w_ffn_in", "w_ffn_out")
MIXER = ("w_branch_gla", "w_branch_sgu", "w_out")
FFN = ("w_ffn_in", "w_ffn_out")
COLUMN_SHARDED = ("w_in", "w_ffn_in")
LATE_SCATTER_WAVES = 4
SMALL = tuple(n for n in WEIGHTS if n not in BIG)
SMALL_SHARDED = ("w_gate_up", "gla_norm", "sgu_ln_g", "sgu_ln_b")
SMALL_FULL = {"norm_pre_mix": (1024,), "w_gate_up": (16, 512), "b_gate": (512,), "gla_norm": (4, 256),
              "sgu_ln_g": (4, 256), "sgu_ln_b": (4, 256), "w_spatial": (4, 128, 128), "b_spatial": (4, 128),
              "norm_post_mix": (1024,), "norm_pre_ffn": (1024,), "norm_post_ffn": (1024,)}
SMALL_GRAD_ROWS = 648
SMALL_STATE_ROWS = 600
SMALL_GATHER_ROWS = 32


def kernel(x, norm_pre_mix, w_in, w_gate_up, b_gate, gla_norm, sgu_ln_g, sgu_ln_b, w_spatial, b_spatial, w_branch_gla, w_branch_sgu, w_out, norm_post_mix, norm_pre_ffn, w_ffn_in, w_ffn_out, norm_post_ffn, loss_target, m_norm_pre_mix, m_w_in, m_w_gate_up, m_b_gate, m_gla_norm, m_sgu_ln_g, m_sgu_ln_b, m_w_spatial, m_b_spatial, m_w_branch_gla, m_w_branch_sgu, m_w_out, m_norm_post_mix, m_norm_pre_ffn, m_w_ffn_in, m_w_ffn_out, m_norm_post_ffn, v_norm_pre_mix, v_w_in, v_w_gate_up, v_b_gate, v_gla_norm, v_sgu_ln_g, v_sgu_ln_b, v_w_spatial, v_b_spatial, v_w_branch_gla, v_w_branch_sgu, v_w_out, v_norm_post_mix, v_norm_pre_ffn, v_w_ffn_in, v_w_ffn_out, v_norm_post_ffn):
    given = dict(locals())
    def local(a, n):
        return a[0].T if n in COLUMN_SHARDED else a[0]

    w = {n: local(given[n], n) for n in WEIGHTS}
    m = {n: local(given["m_" + n], n) for n in WEIGHTS}
    v = {n: local(given["v_" + n], n) for n in WEIGHTS}
    xs, target = x[0], loss_target[0]
    me = 4 * lax.axis_index("x") + 2 * lax.axis_index("y") + lax.axis_index("c")

    small_shard = _pack_rows([w[n] for n in SMALL_SHARDED], SMALL_GATHER_ROWS)
    first = _all_gather_hbm([w["w_in"].astype(BF16), small_shard], name="gather_w_in")
    rest, first = lax.optimization_barrier(([w[n].astype(BF16) for n in MIXER + FFN], first))
    rest_blocks = _all_gather_async(rest, name="gather_rest", collective_id=1)
    W = {n: w[n] for n in SMALL if n not in SMALL_SHARDED}
    blocks = dict(zip(MIXER + FFN, rest_blocks))
    for n in ("w_branch_gla", "w_branch_sgu", "w_out", "w_ffn_out"):
        W[n] = blocks[n].reshape(-1, D_MODEL)
    W["w_ffn_in"] = blocks["w_ffn_in"]
    W["w_in_blocks"] = first[0]
    small_blocks = first[1]
    off = 0
    for n in SMALL_SHARDED:
        r, c = w[n].shape
        blk = small_blocks[:, off:off + r * c // LANES].reshape(N_DEV, r, c)
        W[n] = blk.transpose(1, 0, 2).reshape(r, N_DEV * c)
        off += _tile_rows(r * c)

    scatter_ids = iter((3, 4, 5))

    def scatter(names, parts):
        waves = LATE_SCATTER_WAVES if "w_in" in names else 1
        got = _scatter_blocks_async(parts, name="scatter_" + "_".join(names), collective_id=next(scatter_ids),
                                    waves=waves)
        return dict(zip(names, got))

    def finish(received):
        return {n: _adamw_reduce_call(r, w[n], m[n], v[n], name="adamw_" + n) for n, r in received.items()}

    loss_part, grad_x, grads, big_done = _local_step(xs, target, W, scatter, finish)

    loss_row, small_done = _small_update_call(
        _small_early_rows(grads), grads["norm_pre_mix"], loss_part,
        *[{n: given[prefix + n] for n in SMALL} for prefix in ("", "m_", "v_")], name="small_update")

    def pick(i):
        return [small_done[n][i] if n in SMALL else (big_done[n][i].T if n in COLUMN_SHARDED else big_done[n][i])[None]
                for n in WEIGHTS]

    return (loss_row[0, 0], grad_x[None], *pick(0), *pick(1), *pick(2), *pick(3))
```

```python
import jax
import jax.numpy as jnp
from jax import lax
from jax.experimental import pallas as pl
from jax.experimental.pallas import tpu as pltpu
from jax.experimental.pallas import tpu_sc as plsc

F32 = jnp.float32
BF16 = jnp.bfloat16

D_MODEL = 1024
N_DEV = 8
CHUNK = 64
HEADS = 4
DK = 128
DV = 256
QK = HEADS * DK
GV = HEADS * DV
RANK = 16
GROUPS = 4
SBLOCK = 128
DG = 256
D_FF = 2816
FF_BLK = 704
EPS = 1e-6
Q_SCALE = DK ** -0.5
LANES = 128
VMEM_BIG = 48 * 1024 * 1024

D_IN = 7184
IN_BLK = 898
A_COLS = (0, 3072)
LOW_COLS = (3072, 3088)
S_COLS = (3088, 5136)
G_COLS = (5136, 7184)

ADAM_LR = 0.001
ADAM_B1 = 0.9
ADAM_B2 = 0.999
ADAM_EPS = 1e-08
ADAM_WD = 0.01
ADAM_STEP = 10

MESH = pl.DeviceIdType.MESH
ANY = pl.BlockSpec(memory_space=pl.ANY)
VMEM_SPEC = pl.BlockSpec(memory_space=pltpu.VMEM)


def _cp(sem=None, vmem=None):
    return pltpu.CompilerParams(dimension_semantics=sem, vmem_limit_bytes=vmem)


def _sig(x):
    return 0.5 * jnp.tanh(0.5 * x) + 0.5


GELU_C = 0.7978845608028654
GELU_A = 0.044715


def _gelu(x):
    t = jnp.tanh((GELU_C * x) * (1.0 + GELU_A * (x * x)))
    return (0.5 * x) * (1.0 + t)


def _gelu_and_grad(x):
    x2 = x * x
    t = jnp.tanh((GELU_C * x) * (1.0 + GELU_A * x2))
    one_t = 1.0 + t
    hx = 0.5 * x
    grad = 0.5 * one_t + (hx * (1.0 - t * t)) * (GELU_C + (3.0 * GELU_A * GELU_C) * x2)
    return hx * one_t, grad


def _logsig(x):
    return jnp.minimum(x, 0.0) - jnp.log1p(jnp.exp(-jnp.abs(x)))


def _dot(a, b, dims):
    return lax.dot_general(a, b, (dims, ((), ())), preferred_element_type=F32)


NN = ((1,), (0,))
NT = ((1,), (1,))
TN = ((0,), (0,))


def _exact_mask_dot(mask_bf16, x):
    hi = x.astype(BF16)
    r1 = x - hi.astype(F32)
    mid = r1.astype(BF16)
    lo = (r1 - mid.astype(F32)).astype(BF16)
    return _dot(mask_bf16, hi, NN) + _dot(mask_bf16, mid, NN) + _dot(mask_bf16, lo, NN)


def _pick_tile(dim, target):
    if dim <= target:
        return dim
    best = None
    for t in range(LANES, int(1.4 * target) + 1, LANES):
        if dim % t == 0:
            best = t
    assert best is not None, (dim, target)
    return best


def _mm_call(a, b, *, name, grid, a_spec, b_spec, o_spec, out_shape, dims, acc_shape):
    nk = grid[2]

    def body(a_ref, b_ref, o_ref, *acc):
        part = _dot(a_ref[...], b_ref[...], dims)
        if nk == 1:
            o_ref[...] = part.astype(o_ref.dtype)
        else:
            acc_ref = acc[0]
            k = pl.program_id(2)

            @pl.when(k == 0)
            def _():
                acc_ref[...] = part

            @pl.when(k > 0)
            def _():
                acc_ref[...] += part

            @pl.when(k == nk - 1)
            def _():
                o_ref[...] = acc_ref[...].astype(o_ref.dtype)

    return pl.pallas_call(
        body, name=name, grid=grid, in_specs=[a_spec, b_spec], out_specs=o_spec, out_shape=out_shape,
        scratch_shapes=[] if nk == 1 else [pltpu.VMEM(acc_shape, F32)],
        compiler_params=_cp(("parallel", "parallel", "arbitrary"), VMEM_BIG),
    )(a, b)


def _mm(a, b, mode, out_dtype, *, name, tm=512, tn=1024, tk=1024):
    if mode == "nn":
        (M, K), (_, N) = a.shape, b.shape
    elif mode == "nt":
        (M, K), (N, _) = a.shape, b.shape
    else:
        (K, M), (_, N) = a.shape, b.shape
    tm, tn, tk = _pick_tile(M, tm), _pick_tile(N, tn), _pick_tile(K, tk)
    if mode == "nn":
        a_spec = pl.BlockSpec((tm, tk), lambda i, j, k: (i, k))
        b_spec = pl.BlockSpec((tk, tn), lambda i, j, k: (k, j))
        dims = NN
    elif mode == "nt":
        a_spec = pl.BlockSpec((tm, tk), lambda i, j, k: (i, k))
        b_spec = pl.BlockSpec((tn, tk), lambda i, j, k: (j, k))
        dims = NT
    else:
        a_spec = pl.BlockSpec((tk, tm), lambda i, j, k: (k, i))
        b_spec = pl.BlockSpec((tk, tn), lambda i, j, k: (k, j))
        dims = TN
    return _mm_call(a, b, name=name, grid=(M // tm, N // tn, K // tk), a_spec=a_spec, b_spec=b_spec,
                    o_spec=pl.BlockSpec((tm, tn), lambda i, j, k: (i, j)),
                    out_shape=jax.ShapeDtypeStruct((M, N), out_dtype), dims=dims, acc_shape=(tm, tn))


ROW_TILE = 512
SUB_ROWS = 256


def _sub_tiles(T):
    return [pl.ds(r0, min(SUB_ROWS, T)) for r0 in range(0, T, SUB_ROWS)]


def _rt(T, W, c=0):
    return pl.BlockSpec((T, W), lambda i: (i, c))


def _rt3(nb, T, W):
    return pl.BlockSpec((nb, T, W), lambda i: (0, i, 0))


def _res(shape):
    nd = len(shape)
    return pl.BlockSpec(tuple(shape), lambda i: (0,) * nd, pipeline_mode=pl.Buffered(1))


def _acc(shape):
    nd = len(shape)
    return pl.BlockSpec(tuple(shape), lambda i: (0,) * nd, pipeline_mode=pl.Buffered(1))


def _nbytes(shape, dtype):
    n = jnp.dtype(dtype).itemsize
    for s in shape:
        n *= s
    return n


def _vmem_limit(tiles, resident, temps=16 * 1024 * 1024):
    need = 2 * sum(_nbytes(s, d) for s, d in tiles) + sum(_nbytes(s, d) for s, d in resident) + temps
    return min(need, 60 * 1024 * 1024)


def _tok_call(body, *, name, S, T, ins, outs, semantics="parallel"):
    tiles = [(spec.block_shape, a.dtype) for a, spec, kind in ins + outs if kind == "tile"]
    resident = [(a.shape, a.dtype) for a, spec, kind in ins + outs if kind == "res"]
    return pl.pallas_call(
        body, name=name, grid=(S // T,),
        in_specs=[spec for _, spec, _ in ins], out_specs=[spec for _, spec, _ in outs],
        out_shape=[jax.ShapeDtypeStruct(a.shape, a.dtype) for a, _, _ in outs],
        compiler_params=_cp((semantics,), _vmem_limit(tiles, resident)),
    )(*[a for a, _, _ in ins])


def _tile(a, spec):
    return (a, spec, "tile")


def _whole(a):
    return (a, _res(a.shape), "res")


def _out_tile(shape, dtype, spec):
    return (jax.ShapeDtypeStruct(shape, dtype), spec, "tile")


def _out_acc(shape, dtype=F32):
    return (jax.ShapeDtypeStruct(shape, dtype), _acc(shape), "res")


def _rms_stats(x):
    r = lax.rsqrt(jnp.mean(x * x, axis=-1, keepdims=True) + EPS)
    return r, x * r


def _rms_bwd(xh, r, g, dy):
    dxh = dy * g
    dx = r * (dxh - xh * jnp.mean(dxh * xh, axis=-1, keepdims=True))
    dg = jnp.sum(dy * xh, axis=0, keepdims=True)
    return dx, dg


def _accum(ref, val):
    @pl.when(pl.program_id(0) == 0)
    def _():
        ref[...] = val

    @pl.when(pl.program_id(0) > 0)
    def _():
        ref[...] += val


def _dot_rows_t(a, w_ref, row0, o_ref, chunk=1024):
    n = o_ref.shape[1]
    for n0 in range(0, n, chunk):
        n1 = min(n, n0 + chunk)
        o_ref[:, n0:n1] = _dot(a, w_ref[row0 + n0:row0 + n1, :], NT).astype(o_ref.dtype)


def _in_proj_call(x, g1, w_blocks, *, name, T=ROW_TILE):
    S = x.shape[0]

    def body(x_ref, g_ref, blk_ref, a_ref, pa_ref, al_ref, ps_ref, pg_ref, w_ref):
        @pl.when(pl.program_id(0) == 0)
        def _():
            for d in range(N_DEV):
                w_ref[d * IN_BLK:(d + 1) * IN_BLK, :] = blk_ref[d]

        _, xh = _rms_stats(x_ref[...])
        a = (xh * g_ref[...]).astype(BF16)
        a_ref[...] = a
        _dot_rows_t(a, w_ref, A_COLS[0], pa_ref)
        _dot_rows_t(a, w_ref, LOW_COLS[0], al_ref)
        _dot_rows_t(a, w_ref, S_COLS[0], ps_ref)
        _dot_rows_t(a, w_ref, G_COLS[0], pg_ref)

    widths = (D_MODEL, A_COLS[1] - A_COLS[0], LANES, S_COLS[1] - S_COLS[0], G_COLS[1] - G_COLS[0])
    return _tok_call(
        body, name=name, S=S, T=T, semantics="arbitrary",
        ins=[_tile(x, _rt(T, D_MODEL)), _whole(g1), _whole(w_blocks)],
        outs=[_out_tile((S, w), BF16, _rt(T, w)) for w in widths] + [_out_acc((D_IN, D_MODEL), BF16)])


def _mixer_tail_call(y_gla, y_sgu, pg, x, w_bg, w_bs, w_out, g2, g3, *, name, T=ROW_TILE):
    S = x.shape[0]

    def body(yg_ref, ys_ref, pg_ref, x_ref, wbg_ref, wbs_ref, wo_ref, g2_ref, g3_ref,
             t1_ref, t2_ref, mg_ref, mix_ref, x1_ref, h_ref):
        for rows in _sub_tiles(T):
            t1 = _dot(yg_ref[rows, :], wbg_ref[...], NN)
            t2 = _dot(ys_ref[rows, :], wbs_ref[...], NN)
            t1_ref[rows, :] = t1.astype(BF16)
            t2_ref[rows, :] = t2.astype(BF16)
            sg = _sig(pg_ref[rows, pl.ds(0, D_MODEL)].astype(F32))
            ss = _sig(pg_ref[rows, pl.ds(D_MODEL, D_MODEL)].astype(F32))
            merged = (sg * t1 + ss * t2).astype(BF16)
            mg_ref[rows, :] = merged
            mix = _dot(merged, wo_ref[...], NN)
            mix_ref[rows, :] = mix
            _, mh = _rms_stats(mix)
            x1 = x_ref[rows, :] + mh * g2_ref[...]
            x1_ref[rows, :] = x1
            _, xh = _rms_stats(x1)
            h_ref[rows, :] = (xh * g3_ref[...]).astype(BF16)

    row = _rt(T, D_MODEL)
    b16 = lambda: _out_tile((S, D_MODEL), BF16, row)
    f32 = lambda: _out_tile((S, D_MODEL), F32, row)
    return _tok_call(
        body, name=name, S=S, T=T,
        ins=[_tile(y_gla, row), _tile(y_sgu, row), _tile(pg, _rt(T, 2 * D_MODEL)), _tile(x, row),
             _whole(w_bg), _whole(w_bs), _whole(w_out), _whole(g2), _whole(g3)],
        outs=[b16(), b16(), b16(), f32(), f32(), b16()])


def _ffn_in_call(h, wfi, *, name, T=ROW_TILE):
    S = h.shape[0]

    def body(h_ref, w_ref, gu_ref, z_ref):
        hv = h_ref[...]
        for d in range(4):
            gate = _dot(hv, w_ref[d], NT)
            up = _dot(hv, w_ref[d + 4], NT)
            gu_ref[d] = gate.astype(BF16)
            gu_ref[d + 4] = up.astype(BF16)
            z_ref[d] = (gate * _sig(gate) * up).astype(BF16)

    return _tok_call(
        body, name=name, S=S, T=T,
        ins=[_tile(h, _rt(T, D_MODEL)), _whole(wfi)],
        outs=[_out_tile((N_DEV, S, FF_BLK), BF16, _rt3(N_DEV, T, FF_BLK)),
              _out_tile((4, S, FF_BLK), BF16, _rt3(4, T, FF_BLK))])


def _ffn_out_loss_call(z, wfo, x1, target, g4, *, name, T=ROW_TILE):
    S = x1.shape[0]

    def body(z_ref, w_ref, x1_ref, t_ref, g4_ref, loss_ref, dx2_ref, dy_ref, dg4_ref):
        loss = jnp.zeros((1, 1), F32)
        dg4 = jnp.zeros((1, D_MODEL), F32)
        for rows in _sub_tiles(T):
            y = _dot(z_ref[0, rows, :], w_ref[0], NN)
            for d in range(1, 4):
                y = y + _dot(z_ref[d, rows, :], w_ref[d], NN)
            r, yh = _rms_stats(y)
            diff = x1_ref[rows, :] + yh * g4_ref[...] - t_ref[rows, :]
            loss = loss + 0.5 * jnp.sum(jnp.mean(diff * diff, axis=-1, keepdims=True), axis=0, keepdims=True)
            dx2 = diff * (1.0 / D_MODEL)
            dx2_ref[rows, :] = dx2
            dy, dg = _rms_bwd(yh, r, g4_ref[...], dx2)
            dy_ref[rows, :] = dy.astype(BF16)
            dg4 = dg4 + dg
        _accum(loss_ref, jnp.broadcast_to(loss, (1, LANES)))
        _accum(dg4_ref, dg4)

    row = _rt(T, D_MODEL)
    return _tok_call(
        body, name=name, S=S, T=T, semantics="arbitrary",
        ins=[_tile(z, _rt3(4, T, FF_BLK)), _whole(wfo), _tile(x1, row), _tile(target, row), _whole(g4)],
        outs=[_out_acc((1, LANES)), _out_tile((S, D_MODEL), F32, row), _out_tile((S, D_MODEL), BF16, row),
              _out_acc((1, D_MODEL))])


def _ffn_out_bwd_call(dy, wfo, gu, *, name, T=ROW_TILE):
    S = dy.shape[0]

    def body(dy_ref, w_ref, gu_ref, dgu_ref):
        dyv = dy_ref[...]
        for d in range(4):
            dz = _dot(dyv, w_ref[d], NT)
            gt = gu_ref[d].astype(F32)
            up = gu_ref[d + 4].astype(F32)
            s = _sig(gt)
            dgu_ref[d] = (dz * up * s * (1.0 + gt * (1.0 - s))).astype(BF16)
            dgu_ref[d + 4] = (dz * gt * s).astype(BF16)

    blocks = _rt3(N_DEV, T, FF_BLK)
    return _tok_call(
        body, name=name, S=S, T=T,
        ins=[_tile(dy, _rt(T, D_MODEL)), _whole(wfo), _tile(gu, blocks)],
        outs=[_out_tile((N_DEV, S, FF_BLK), BF16, blocks)])[0]


def _ffn_in_bwd_call(dgu, wfi, dx2, x1, mix, g3, g2, *, name, T=ROW_TILE):
    S = x1.shape[0]

    def body(dgu_ref, w_ref, dx2_ref, x1_ref, mix_ref, g3_ref, g2_ref, dx1_ref, dmix_ref, dg3_ref, dg2_ref):
        dg3 = jnp.zeros((1, D_MODEL), F32)
        dg2 = jnp.zeros((1, D_MODEL), F32)
        for rows in _sub_tiles(T):
            dh = _dot(dgu_ref[0, rows, :], w_ref[0], NN)
            for d in range(1, N_DEV):
                dh = dh + _dot(dgu_ref[d, rows, :], w_ref[d], NN)
            r3, xh = _rms_stats(x1_ref[rows, :])
            d3, g = _rms_bwd(xh, r3, g3_ref[...], dh)
            dg3 = dg3 + g
            dx1 = dx2_ref[rows, :] + d3
            dx1_ref[rows, :] = dx1
            r2, mh = _rms_stats(mix_ref[rows, :])
            dmix, g = _rms_bwd(mh, r2, g2_ref[...], dx1)
            dg2 = dg2 + g
            dmix_ref[rows, :] = dmix.astype(BF16)
        _accum(dg3_ref, dg3)
        _accum(dg2_ref, dg2)

    row = _rt(T, D_MODEL)
    return _tok_call(
        body, name=name, S=S, T=T, semantics="arbitrary",
        ins=[_tile(dgu, _rt3(N_DEV, T, FF_BLK)), _whole(wfi), _tile(dx2, row), _tile(x1, row), _tile(mix, row),
             _whole(g3), _whole(g2)],
        outs=[_out_tile((S, D_MODEL), F32, row), _out_tile((S, D_MODEL), BF16, row),
              _out_acc((1, D_MODEL)), _out_acc((1, D_MODEL))])


def _mixer_bwd_call(dmix, t1, t2, pg, w_out, w_bg, w_bs, *, name, T=ROW_TILE):
    S = dmix.shape[0]

    def body(dmix_ref, t1_ref, t2_ref, pg_ref, wo_ref, wbg_ref, wbs_ref, dt1_ref, dt2_ref, dpg_ref, dyg_ref, dys_ref):
        gg, gs = pl.ds(0, D_MODEL), pl.ds(D_MODEL, D_MODEL)
        for rows in _sub_tiles(T):
            dm = _dot(dmix_ref[rows, :], wo_ref[...], NT)
            sg = _sig(pg_ref[rows, gg].astype(F32))
            ss = _sig(pg_ref[rows, gs].astype(F32))
            dt1 = (dm * sg).astype(BF16)
            dt2 = (dm * ss).astype(BF16)
            dt1_ref[rows, :] = dt1
            dt2_ref[rows, :] = dt2
            dpg_ref[rows, gg] = (dm * t1_ref[rows, :].astype(F32) * sg * (1.0 - sg)).astype(BF16)
            dpg_ref[rows, gs] = (dm * t2_ref[rows, :].astype(F32) * ss * (1.0 - ss)).astype(BF16)
            dyg_ref[rows, :] = _dot(dt1, wbg_ref[...], NT).astype(BF16)
            dys_ref[rows, :] = _dot(dt2, wbs_ref[...], NT).astype(BF16)

    row = _rt(T, D_MODEL)
    wide = _rt(T, 2 * D_MODEL)
    b16 = lambda: _out_tile((S, D_MODEL), BF16, row)
    return _tok_call(
        body, name=name, S=S, T=T,
        ins=[_tile(dmix, row), _tile(t1, row), _tile(t2, row), _tile(pg, wide), _whole(w_out), _whole(w_bg), _whole(w_bs)],
        outs=[b16(), b16(), _out_tile((S, 2 * D_MODEL), BF16, wide), b16(), b16()])


def _in_proj_bwd_call(dpa, dlow, dps, dpg, w_in_t, x, dx1, g1, *, name, T=ROW_TILE):
    S = x.shape[0]

    def body(dpa_ref, dl_ref, dps_ref, dpg_ref, w_ref, x_ref, dx1_ref, g1_ref, gx_ref, dg1_ref):
        da = (_dot(dpa_ref[...], w_ref[A_COLS[0]:A_COLS[1], :], NN)
              + _dot(dl_ref[...], w_ref[LOW_COLS[0]:LOW_COLS[0] + LANES, :], NN)
              + _dot(dps_ref[...], w_ref[S_COLS[0]:S_COLS[1], :], NN)
              + _dot(dpg_ref[...], w_ref[G_COLS[0]:G_COLS[1], :], NN))
        r, xh = _rms_stats(x_ref[...])
        dxa, dg = _rms_bwd(xh, r, g1_ref[...], da)
        gx_ref[...] = dx1_ref[...] + dxa
        _accum(dg1_ref, dg)

    row = _rt(T, D_MODEL)
    return _tok_call(
        body, name=name, S=S, T=T, semantics="arbitrary",
        ins=[_tile(dpa, _rt(T, dpa.shape[1])), _tile(dlow, _rt(T, dlow.shape[1])), _tile(dps, _rt(T, dps.shape[1])),
             _tile(dpg, _rt(T, dpg.shape[1])), _whole(w_in_t), _tile(x, row), _tile(dx1, row), _whole(g1)],
        outs=[_out_tile((S, D_MODEL), F32, row), _out_acc((1, D_MODEL))])


TOKEN_TILE = 512


def _weight_grads_part(arrays, pairs, *, tk=TOKEN_TILE, out_dtype=BF16):
    S = arrays[0].shape[-2]
    tk = min(tk, S)
    n_in = len(arrays)

    def out_shape(i, j):
        a, b = arrays[i], arrays[j]
        if a.ndim == 3:
            return (a.shape[0], a.shape[2], b.shape[1])
        if b.ndim == 3:
            return (b.shape[0], a.shape[1], b.shape[2])
        return (a.shape[1], b.shape[1])

    shapes = [out_shape(i, j) for i, j in pairs]

    in_place = out_dtype == F32

    def body(ins, outs, accs):
        k = pl.program_id(0)
        if in_place:
            accs = outs

        @pl.when(k == 0)
        def _():
            for acc in accs:
                acc[...] = jnp.zeros_like(acc)

        for (i, j), acc in zip(pairs, accs):
            a_ref, b_ref = ins[i], ins[j]
            if len(a_ref.shape) == 3:
                for n in range(a_ref.shape[0]):
                    acc[n] += _dot(a_ref[n], b_ref[...], TN)
            elif len(b_ref.shape) == 3:
                a = a_ref[...]
                for n in range(b_ref.shape[0]):
                    acc[n] += _dot(a, b_ref[n], TN)
            else:
                b = b_ref[...]
                for m0 in range(0, a_ref.shape[1], 1024):
                    m1 = min(a_ref.shape[1], m0 + 1024)
                    acc[m0:m1, :] += _dot(a_ref[:, m0:m1], b, TN)

        if not in_place:
            @pl.when(k == S // tk - 1)
            def _():
                for out, acc in zip(outs, accs):
                    out[...] = acc[...].astype(out.dtype)

    def in_spec(a):
        if a.ndim == 3:
            return pl.BlockSpec((a.shape[0], tk, a.shape[2]), lambda k: (0, k, 0))
        return pl.BlockSpec((tk, a.shape[1]), lambda k: (k, 0))

    return dict(
        body=body, steps=S // tk, arrays=list(arrays),
        in_specs=[in_spec(a) for a in arrays],
        out_specs=[_acc(s) for s in shapes],
        out_shapes=[jax.ShapeDtypeStruct(s, out_dtype) for s in shapes],
        scratch=[] if in_place else [pltpu.VMEM(s, F32) for s in shapes],
        tiles=[(in_spec(a).block_shape, a.dtype) for a in arrays],
        resident=[(s, F32) for s in shapes] + ([] if in_place else [(s, BF16) for s in shapes]))


def _weight_grads_call(arrays, pairs, *, name, tk=TOKEN_TILE):
    part = _weight_grads_part(arrays, pairs, tk=tk)
    n_in, n_out = len(part["arrays"]), len(part["out_shapes"])

    def body(*refs):
        part["body"](refs[:n_in], refs[n_in:n_in + n_out], refs[n_in + n_out:])

    return pl.pallas_call(
        body, name=name, grid=(part["steps"],),
        in_specs=part["in_specs"], out_specs=part["out_specs"], out_shape=part["out_shapes"],
        scratch_shapes=part["scratch"],
        compiler_params=_cp(("arbitrary",), _vmem_limit(part["tiles"], part["resident"])),
    )(*part["arrays"])


def _with_rider(body, n_in, n_out, n_scratch, rider):
    if rider is None:
        return body
    r_in, r_out = len(rider["arrays"]), len(rider["out_shapes"])

    def both(*refs):
        ins, rest = refs[:n_in + r_in], refs[n_in + r_in:]
        outs, scratch = rest[:n_out + r_out], rest[n_out + r_out:]
        body(*ins[:n_in], *outs[:n_out], *scratch[:n_scratch])
        rider["body"](ins[n_in:], outs[n_out:], scratch[n_scratch:])

    return both


def _rider_lists(rider):
    if rider is None:
        return [], [], [], [], [], [], []
    return (rider["arrays"], rider["in_specs"], rider["out_specs"], rider["out_shapes"], rider["scratch"],
            rider["tiles"], rider["resident"])


def _ffn_in_grad_call(h, dgu, *, name, tk=TOKEN_TILE):
    S = h.shape[0]
    tk = min(tk, S)
    nb = 4
    shape = (nb, FF_BLK, D_MODEL)

    def body(h_ref, dgu_ref, out_ref, acc):
        k = pl.program_id(1)

        @pl.when(k == 0)
        def _():
            acc[...] = jnp.zeros_like(acc)

        hv = h_ref[...]
        for n in range(nb):
            acc[n] += _dot(dgu_ref[n], hv, TN)

        @pl.when(k == S // tk - 1)
        def _():
            out_ref[...] = acc[...].astype(out_ref.dtype)

    tiles = [((tk, D_MODEL), BF16), ((nb, tk, FF_BLK), BF16), (shape, BF16)]
    return pl.pallas_call(
        body, name=name, grid=(N_DEV // nb, S // tk),
        in_specs=[pl.BlockSpec((tk, D_MODEL), lambda g, k: (k, 0)),
                  pl.BlockSpec((nb, tk, FF_BLK), lambda g, k: (g, k, 0))],
        out_specs=pl.BlockSpec(shape, lambda g, k: (g, 0, 0)),
        out_shape=jax.ShapeDtypeStruct((N_DEV, FF_BLK, D_MODEL), BF16),
        scratch_shapes=[pltpu.VMEM(shape, F32)],
        compiler_params=_cp(("parallel", "arbitrary"), _vmem_limit(tiles, [(shape, F32)])),
    )(h, dgu)


GLA_TILE = 256
Q_OFF, K_OFF, V_OFF, R_OFF = 0, QK, 2 * QK, 2 * QK + GV


def _tri(lower):
    r = lax.broadcasted_iota(jnp.int32, (CHUNK, CHUNK), 0)
    c = lax.broadcasted_iota(jnp.int32, (CHUNK, CHUNK), 1)
    return jnp.where((r >= c) if lower else (c >= r), 1.0, 0.0).astype(BF16)


def _gla_fwd_call(pa, alow, wg, bgate, gnorm, *, name):
    S = pa.shape[0]
    Tg = min(GLA_TILE, S)
    cb = Tg // CHUNK

    def body(pa_ref, al_ref, wg_ref, bg_ref, gn_ref, y_ref, st_ref, state):
        @pl.when(pl.program_id(0) == 0)
        def _():
            state[...] = jnp.zeros_like(state)

        logit = _dot(al_ref[...], wg_ref[...], NN) + bg_ref[...]
        ls = _logsig(logit) * (1.0 / 16.0)
        tri = _tri(True)
        for c in range(cb):
            rows = pl.ds(c * CHUNK, CHUNK)
            cum = _exact_mask_dot(tri, ls[c * CHUNK:(c + 1) * CHUNK])
            tot = cum[CHUNK - 1:CHUNK]
            kd = (pa_ref[rows, pl.ds(K_OFF, QK)].astype(F32) * jnp.exp(tot - cum)).astype(BF16)
            decay = jnp.exp(tot)
            for h in range(HEADS):
                lanes = slice(h * DK, (h + 1) * DK)
                new = state[h] * decay[:, lanes] + _dot(pa_ref[rows, pl.ds(V_OFF + h * DV, DV)], kd[:, lanes], TN)
                state[h] = new
                st_ref[c, h] = new
        for c in range(cb):
            rows = pl.ds(c * CHUNK, CHUNK)
            for h in range(HEADS):
                qs = (pa_ref[rows, pl.ds(Q_OFF + h * DK, DK)].astype(F32) * Q_SCALE).astype(BF16)
                o = _dot(qs, st_ref[c, h].astype(BF16), NT)
                rs = lax.rsqrt(jnp.mean(o * o, axis=-1, keepdims=True) + EPS)
                rr = pa_ref[rows, pl.ds(R_OFF + h * DV, DV)].astype(F32)
                y_ref[rows, pl.ds(h * DV, DV)] = (o * rs * gn_ref[h] * (rr * _sig(rr))).astype(BF16)

    return pl.pallas_call(
        body, name=name, grid=(S // Tg,),
        in_specs=[
            pl.BlockSpec((Tg, 2 * QK + 2 * GV), lambda t: (t, 0)),
            pl.BlockSpec((Tg, LANES), lambda t: (t, 0)),
            pl.BlockSpec((LANES, QK), lambda t: (0, 0)),
            pl.BlockSpec((1, QK), lambda t: (0, 0)),
            pl.BlockSpec((HEADS, 1, DV), lambda t: (0, 0, 0)),
        ],
        out_specs=[
            pl.BlockSpec((Tg, GV), lambda t: (t, 0)),
            pl.BlockSpec((cb, HEADS, DV, DK), lambda t: (t, 0, 0, 0)),
        ],
        out_shape=[jax.ShapeDtypeStruct((S, GV), BF16),
                   jax.ShapeDtypeStruct((S // CHUNK, HEADS, DV, DK), F32)],
        scratch_shapes=[pltpu.VMEM((HEADS, DV, DK), F32)],
        compiler_params=_cp(("arbitrary",), VMEM_BIG),
    )(pa, alow, wg, bgate, gnorm)


def _gla_bwd_call(pa, alow, wg, bgate, gnorm, states, dy, *, name, rider=None):
    S = pa.shape[0]
    Tg = min(GLA_TILE, S)
    cb = Tg // CHUNK
    nt = S // Tg
    r_arrays, r_in, r_out, r_shapes, r_scratch, r_tiles, r_resident = _rider_lists(rider)
    assert rider is None or rider["steps"] == nt

    def rev(t):
        return nt - 1 - t

    def body(pa_ref, al_ref, wg_ref, bg_ref, gn_ref, st_ref, prev_ref, dy_ref,
             dpa_ref, dl_ref, dgn_ref, dbg_ref, carry, pbuf):
        t = pl.program_id(0)

        @pl.when(t == 0)
        def _():
            carry[...] = jnp.zeros_like(carry)
            dgn_ref[...] = jnp.zeros_like(dgn_ref)
            dbg_ref[...] = jnp.zeros_like(dbg_ref)

        logit = _dot(al_ref[...], wg_ref[...], NN) + bg_ref[...]
        ls = _logsig(logit) * (1.0 / 16.0)
        sneg = 1.0 / (1.0 + jnp.exp(logit))
        tri = _tri(True)
        upper = _tri(False)
        first_tile = rev(t) == 0
        heads = range(HEADS)

        w, decay, kd = [], [], []
        for c in range(cb):
            rows = pl.ds(c * CHUNK, CHUNK)
            cum = _exact_mask_dot(tri, ls[c * CHUNK:(c + 1) * CHUNK])
            tot = cum[CHUNK - 1:CHUNK]
            w.append(jnp.exp(tot - cum))
            decay.append(jnp.exp(tot))
            kd.append(pa_ref[rows, pl.ds(K_OFF, QK)].astype(F32) * w[c])

        dgn = [jnp.zeros((1, DV), F32) for _ in heads]
        for c in range(cb):
            rows = pl.ds(c * CHUNK, CHUNK)
            for h in heads:
                gn = gn_ref[h]
                qs = (pa_ref[rows, pl.ds(Q_OFF + h * DK, DK)].astype(F32) * Q_SCALE).astype(BF16)
                st16 = st_ref[c, h].astype(BF16)
                o = _dot(qs, st16, NT)
                rs = lax.rsqrt(jnp.mean(o * o, axis=-1, keepdims=True) + EPS)
                oh = o * rs
                rr = pa_ref[rows, pl.ds(R_OFF + h * DV, DV)].astype(F32)
                sr = _sig(rr)
                dyv = dy_ref[rows, pl.ds(h * DV, DV)].astype(F32)
                dpa_ref[rows, pl.ds(R_OFF + h * DV, DV)] = (
                    dyv * oh * gn * sr * (1.0 + rr * (1.0 - sr))).astype(BF16)
                don = dyv * (rr * sr)
                dgn[h] = dgn[h] + jnp.sum(don * oh, axis=0, keepdims=True)
                doh = don * gn
                do16 = (rs * (doh - oh * jnp.mean(doh * oh, axis=-1, keepdims=True))).astype(BF16)
                dpa_ref[rows, pl.ds(Q_OFF + h * DK, DK)] = (_dot(do16, st16, NN) * Q_SCALE).astype(BF16)
                pbuf[c, h] = _dot(do16, qs, TN)
        for h in heads:
            dgn_ref[h] += dgn[h]

        dkd = [[None] * HEADS for _ in range(cb)]
        ddecay = [[None] * HEADS for _ in range(cb)]
        for c in reversed(range(cb)):
            rows = pl.ds(c * CHUNK, CHUNK)
            for h in heads:
                lanes = slice(h * DK, (h + 1) * DK)
                gt = pbuf[c, h] + carry[h]
                gt16 = gt.astype(BF16)
                dkd[c][h] = _dot(pa_ref[rows, pl.ds(V_OFF + h * DV, DV)], gt16, NN)
                dpa_ref[rows, pl.ds(V_OFF + h * DV, DV)] = _dot(kd[c][:, lanes].astype(BF16), gt16, NT).astype(BF16)
                if c > 0:
                    st_prev = st_ref[c - 1, h]
                else:
                    st_prev = jnp.where(first_tile, 0.0, prev_ref[0, h])
                ddecay[c][h] = jnp.sum(gt * st_prev, axis=0, keepdims=True)
                carry[h] = gt * decay[c][:, lanes]

        dbg = jnp.zeros((1, QK), F32)
        for c in range(cb):
            rows = pl.ds(c * CHUNK, CHUNK)
            dkd_c = jnp.concatenate(dkd[c], axis=1)
            dpa_ref[rows, pl.ds(K_OFF, QK)] = (dkd_c * w[c]).astype(BF16)
            e = dkd_c * kd[c]
            dtot = jnp.sum(e, axis=0, keepdims=True) + jnp.concatenate(ddecay[c], axis=1) * decay[c]
            dls = dtot - _exact_mask_dot(upper, e)
            dlogit = dls * (1.0 / 16.0) * sneg[c * CHUNK:(c + 1) * CHUNK]
            dl_ref[rows, :] = dlogit.astype(BF16)
            dbg = dbg + jnp.sum(dlogit, axis=0, keepdims=True)
        dbg_ref[...] += dbg

    wide = 2 * QK + 2 * GV
    tiles = [((Tg, wide), BF16), ((cb + 1, HEADS, DV, DK), F32), ((Tg, GV), BF16), ((Tg, wide), BF16),
             ((Tg, QK), BF16)] + r_tiles
    resident = [((cb + 1, HEADS, DV, DK), F32)] + r_resident
    return pl.pallas_call(
        _with_rider(body, 8, 4, 2, rider), name=name, grid=(nt,),
        in_specs=[
            pl.BlockSpec((Tg, wide), lambda t: (rev(t), 0)),
            pl.BlockSpec((Tg, LANES), lambda t: (rev(t), 0)),
            pl.BlockSpec((LANES, QK), lambda t: (0, 0)),
            pl.BlockSpec((1, QK), lambda t: (0, 0)),
            pl.BlockSpec((HEADS, 1, DV), lambda t: (0, 0, 0)),
            pl.BlockSpec((cb, HEADS, DV, DK), lambda t: (rev(t), 0, 0, 0)),
            pl.BlockSpec((1, HEADS, DV, DK), lambda t: (jnp.maximum(rev(t) * cb - 1, 0), 0, 0, 0)),
            pl.BlockSpec((Tg, GV), lambda t: (rev(t), 0)),
        ] + r_in,
        out_specs=[
            pl.BlockSpec((Tg, wide), lambda t: (rev(t), 0)),
            pl.BlockSpec((Tg, QK), lambda t: (rev(t), 0)),
            pl.BlockSpec((HEADS, 1, DV), lambda t: (0, 0, 0)),
            pl.BlockSpec((1, QK), lambda t: (0, 0)),
        ] + r_out,
        out_shape=[jax.ShapeDtypeStruct((S, wide), BF16), jax.ShapeDtypeStruct((S, QK), BF16),
                   jax.ShapeDtypeStruct((HEADS, 1, DV), F32), jax.ShapeDtypeStruct((1, QK), F32)] + r_shapes,
        scratch_shapes=[pltpu.VMEM((HEADS, DV, DK), F32), pltpu.VMEM((cb, HEADS, DV, DK), F32)] + r_scratch,
        compiler_params=_cp(("arbitrary",), _vmem_limit(tiles, resident)),
    )(pa, alow, wg, bgate, gnorm, states, states, dy, *r_arrays)


SGU_TILE = 256


def _sgu_mask():
    r = lax.broadcasted_iota(jnp.int32, (SBLOCK, SBLOCK), 0)
    c = lax.broadcasted_iota(jnp.int32, (SBLOCK, SBLOCK), 1)
    return (c < CHUNK) | (r >= CHUNK)


def _ln_stats(vf):
    mu = jnp.mean(vf, axis=-1, keepdims=True)
    xc = vf - mu
    rs = lax.rsqrt(jnp.mean(xc * xc, axis=-1, keepdims=True) + EPS)
    return rs, xc * rs


def _sgu_fwd_call(ps, ln_g, ln_b, w_sp, b_sp, *, name):
    S = ps.shape[0]
    Ts = min(SGU_TILE, S)

    def body(ps_ref, lg_ref, lb_ref, w_ref, b_ref, y_ref):
        mask = _sgu_mask()
        for g in range(GROUPS):
            wm = jnp.where(mask, w_ref[g], 0.0).astype(BF16)
            for p in range(Ts // SBLOCK):
                rows = pl.ds(p * SBLOCK, SBLOCK)
                u = _gelu(ps_ref[rows, pl.ds(g * DG, DG)].astype(F32))
                _, xh = _ln_stats(_gelu(ps_ref[rows, pl.ds(D_MODEL + g * DG, DG)].astype(F32)))
                vn = xh * lg_ref[g] + lb_ref[g]
                mixed = _dot(wm, vn.astype(BF16), NN) + b_ref[g]
                y_ref[rows, pl.ds(g * DG, DG)] = (u * mixed).astype(BF16)

    full3 = lambda a, b, c: pl.BlockSpec((a, b, c), lambda t: (0, 0, 0))
    return pl.pallas_call(
        body, name=name, grid=(S // Ts,),
        in_specs=[pl.BlockSpec((Ts, 2 * D_MODEL), lambda t: (t, 0)),
                  full3(GROUPS, 1, DG), full3(GROUPS, 1, DG), full3(GROUPS, SBLOCK, SBLOCK), full3(GROUPS, SBLOCK, 1)],
        out_specs=pl.BlockSpec((Ts, D_MODEL), lambda t: (t, 0)),
        out_shape=jax.ShapeDtypeStruct((S, D_MODEL), BF16),
        compiler_params=_cp(("parallel",)),
    )(ps, ln_g, ln_b, w_sp, b_sp)


def _sgu_bwd_call(ps, ln_g, ln_b, w_sp, b_sp, dy, *, name, rider=None):
    S = ps.shape[0]
    Ts = min(SGU_TILE, S)
    r_arrays, r_in, r_out, r_shapes, r_scratch, r_tiles, r_resident = _rider_lists(rider)
    assert rider is None or rider["steps"] == S // Ts

    def body(ps_ref, lg_ref, lb_ref, w_ref, b_ref, dy_ref, ds_ref, dlg_ref, dlb_ref, dw_ref, db_ref):
        @pl.when(pl.program_id(0) == 0)
        def _():
            dlg_ref[...] = jnp.zeros_like(dlg_ref)
            dlb_ref[...] = jnp.zeros_like(dlb_ref)
            dw_ref[...] = jnp.zeros_like(dw_ref)
            db_ref[...] = jnp.zeros_like(db_ref)

        mask = _sgu_mask()
        for g in range(GROUPS):
            wm = jnp.where(mask, w_ref[g], 0.0).astype(BF16)
            lg = lg_ref[g]
            for p in range(Ts // SBLOCK):
                rows = pl.ds(p * SBLOCK, SBLOCK)
                su = ps_ref[rows, pl.ds(g * DG, DG)].astype(F32)
                sv = ps_ref[rows, pl.ds(D_MODEL + g * DG, DG)].astype(F32)
                u, du = _gelu_and_grad(su)
                gv, dgv = _gelu_and_grad(sv)
                rs, xh = _ln_stats(gv)
                vn16 = (xh * lg + lb_ref[g]).astype(BF16)
                mixed = _dot(wm, vn16, NN) + b_ref[g]
                dyv = dy_ref[rows, pl.ds(g * DG, DG)].astype(F32)
                ds_ref[rows, pl.ds(g * DG, DG)] = (dyv * mixed * du).astype(BF16)
                dmix = dyv * u
                dmix16 = dmix.astype(BF16)
                db_ref[g] += jnp.sum(dmix, axis=-1, keepdims=True)
                dw_ref[g] += jnp.where(mask, _dot(dmix16, vn16, NT), 0.0)
                dvn = _dot(wm, dmix16, TN)
                dlg_ref[g] += jnp.sum(dvn * xh, axis=0, keepdims=True)
                dlb_ref[g] += jnp.sum(dvn, axis=0, keepdims=True)
                dxh = dvn * lg
                dvf = rs * (dxh - jnp.mean(dxh, axis=-1, keepdims=True)
                            - xh * jnp.mean(dxh * xh, axis=-1, keepdims=True))
                ds_ref[rows, pl.ds(D_MODEL + g * DG, DG)] = (dvf * dgv).astype(BF16)

    full3 = lambda a, b, c: pl.BlockSpec((a, b, c), lambda t: (0, 0, 0))
    tiles = [((Ts, 2 * D_MODEL), BF16), ((Ts, D_MODEL), BF16), ((Ts, 2 * D_MODEL), BF16)] + r_tiles
    return pl.pallas_call(
        _with_rider(body, 6, 5, 0, rider), name=name, grid=(S // Ts,),
        in_specs=[pl.BlockSpec((Ts, 2 * D_MODEL), lambda t: (t, 0)),
                  full3(GROUPS, 1, DG), full3(GROUPS, 1, DG), full3(GROUPS, SBLOCK, SBLOCK), full3(GROUPS, SBLOCK, 1),
                  pl.BlockSpec((Ts, D_MODEL), lambda t: (t, 0))] + r_in,
        out_specs=[pl.BlockSpec((Ts, 2 * D_MODEL), lambda t: (t, 0)),
                   full3(GROUPS, 1, DG), full3(GROUPS, 1, DG), full3(GROUPS, SBLOCK, SBLOCK), full3(GROUPS, SBLOCK, 1)]
        + r_out,
        out_shape=[jax.ShapeDtypeStruct((S, 2 * D_MODEL), BF16),
                   jax.ShapeDtypeStruct((GROUPS, 1, DG), F32), jax.ShapeDtypeStruct((GROUPS, 1, DG), F32),
                   jax.ShapeDtypeStruct((GROUPS, SBLOCK, SBLOCK), F32),
                   jax.ShapeDtypeStruct((GROUPS, SBLOCK, 1), F32)] + r_shapes,
        scratch_shapes=r_scratch,
        compiler_params=_cp(("arbitrary",), _vmem_limit(tiles, r_resident)),
    )(ps, ln_g, ln_b, w_sp, b_sp, dy, *r_arrays)


def _position():
    return lax.axis_index("x"), lax.axis_index("y"), lax.axis_index("c")


def _gather_copies(srcs, dsts, send_sems, recv_sems, local_sems, relay=False):
    if relay:
        return _gather_copies_relayed(srcs, dsts, send_sems, recv_sems, local_sems)
    x, y, c = _position()
    me, sibling = (x, y, c), (x, y, 1 - c)
    chips = [(1 - x, y), (x, 1 - y), (1 - x, 1 - y)]
    n = len(srcs)

    def slab(a, block):
        px, py, pc = block
        return dsts[a].at[4 * px + 2 * py + pc]

    def copy(a, k, block, to, src=None):
        return pltpu.make_async_remote_copy(
            src_ref=slab(a, block) if src is None else src, dst_ref=slab(a, block),
            send_sem=send_sems.at[7 * a + k], recv_sem=recv_sems.at[7 * a + k], device_id=to, device_id_type=MESH)

    mine = [pltpu.make_async_copy(srcs[a], slab(a, me), local_sems.at[a]) for a in range(n)]
    for cp in mine:
        cp.start()
    first = []
    for a in range(n):
        first.append(copy(a, 0, me, sibling, src=srcs[a]))
        first += [copy(a, 1 + j, me, (*chip, c), src=srcs[a]) for j, chip in enumerate(chips)]
    for cp in first:
        cp.start()
    passed = []
    for j, chip in enumerate(chips):
        for a in range(n):
            copy(a, 1 + j, (*chip, c), me).wait_recv()
            cp = copy(a, 4 + j, (*chip, c), sibling)
            cp.start()
            passed.append(cp)
    for a in range(n):
        copy(a, 0, sibling, me).wait_recv()
        for j, chip in enumerate(chips):
            copy(a, 4 + j, (*chip, 1 - c), me).wait_recv()
    for cp in first + passed:
        cp.wait_send()
    for cp in mine:
        cp.wait()


def _gather_copies_relayed(srcs, dsts, send_sems, recv_sems, local_sems):
    x, y, c = _position()
    me, sibling = (x, y, c), (x, y, 1 - c)
    x_chip, y_chip, d_chip = (1 - x, y), (x, 1 - y), (1 - x, 1 - y)
    south = c == 0
    relay_from = (jnp.where(south, x, 1 - x), jnp.where(south, 1 - y, y), c)
    relay_to = (jnp.where(south, 1 - x, x), jnp.where(south, y, 1 - y), c)
    n = len(srcs)

    def slab(a, block):
        px, py, pc = block
        return dsts[a].at[4 * px + 2 * py + pc]

    def copy(a, k, block, to, src=None):
        return pltpu.make_async_remote_copy(
            src_ref=slab(a, block) if src is None else src, dst_ref=slab(a, block),
            send_sem=send_sems.at[7 * a + k], recv_sem=recv_sems.at[7 * a + k], device_id=to, device_id_type=MESH)

    mine = [pltpu.make_async_copy(srcs[a], slab(a, me), local_sems.at[a]) for a in range(n)]
    for cp in mine:
        cp.start()
    sent = []
    for a in range(n):
        sent += [copy(a, 0, me, sibling, src=srcs[a]), copy(a, 1, me, (*x_chip, c), src=srcs[a]),
                 copy(a, 2, me, (*y_chip, c), src=srcs[a])]
    for cp in sent:
        cp.start()
    for a in range(n):
        copy(a, 1, (*x_chip, c), me).wait_recv()
        copy(a, 2, (*y_chip, c), me).wait_recv()
        later = [copy(a, 3, relay_from, relay_to), copy(a, 4, (*x_chip, c), sibling), copy(a, 5, (*y_chip, c), sibling)]
        for cp in later:
            cp.start()
        sent += later
    for a in range(n):
        copy(a, 3, (*d_chip, c), me).wait_recv()
        cp = copy(a, 6, (*d_chip, c), sibling)
        cp.start()
        sent.append(cp)
    for a in range(n):
        copy(a, 0, sibling, me).wait_recv()
        for k, chip in ((4, x_chip), (5, y_chip), (6, d_chip)):
            copy(a, k, (*chip, 1 - c), me).wait_recv()
    for cp in sent:
        cp.wait_send()
    for cp in mine:
        cp.wait()


def _all_gather_hbm(shards, *, name):
    n = len(shards)

    def body(*refs):
        srcs, dsts = refs[:n], refs[n:2 * n]
        send_sems, recv_sems, local_sems = refs[2 * n:]
        _gather_copies(srcs, dsts, send_sems, recv_sems, local_sems, relay=True)

    return pl.pallas_call(
        body, name=name,
        in_specs=[ANY] * n, out_specs=[ANY] * n,
        out_shape=[jax.ShapeDtypeStruct((N_DEV, *s.shape), s.dtype) for s in shards],
        scratch_shapes=_comm_sems(n),
    )(*shards)


def _all_reduce_small(part, *, name):
    R, W = part.shape

    def body(x_ref, out_ref, gathered, send_sems, recv_sems, local_sems):
        _gather_copies([x_ref], [gathered], send_sems, recv_sems, local_sems)
        acc = gathered[0]
        for d in range(1, N_DEV):
            acc = acc + gathered[d]
        out_ref[...] = acc

    return pl.pallas_call(
        body, name=name,
        in_specs=[VMEM_SPEC], out_specs=VMEM_SPEC,
        out_shape=jax.ShapeDtypeStruct((R, W), F32),
        scratch_shapes=[pltpu.VMEM((N_DEV, R, W), F32),
                        pltpu.SemaphoreType.DMA((7,)), pltpu.SemaphoreType.DMA((7,)), pltpu.SemaphoreType.DMA((1,))],
    )(part)


FLIPS = [(fx, fy, fc) for fx in (0, 1) for fy in (0, 1) for fc in (0, 1)][1:]


def _scatter_copies(srcs, dsts, send_sems, recv_sems, local_sems, waves=1):
    n = len(srcs)
    x, y, c = _position()
    me = 4 * x + 2 * y + c
    mine = [pltpu.make_async_copy(srcs[a].at[me], dsts[a].at[me], local_sems.at[a]) for a in range(n)]
    for cp in mine:
        cp.start()
    peers = []
    for fx, fy, fc in FLIPS:
        tx = 1 - x if fx else x
        ty = 1 - y if fy else y
        tc = 1 - c if fc else c
        peers.append(((tx, ty, tc), 4 * tx + 2 * ty + tc))
    copies = []
    for w in range(waves):
        wave = []
        for k, (peer_id, peer) in enumerate(peers):
            for a in range(n):
                rows = srcs[a].shape[1]
                step = rows if waves == 1 else (rows // waves) // 16 * 16
                r0 = w * step
                cut = pl.ds(r0, step if w < waves - 1 else rows - r0)
                sem = (7 * a + k) * waves + w
                cp = pltpu.make_async_remote_copy(
                    src_ref=srcs[a].at[peer, cut], dst_ref=dsts[a].at[me, cut],
                    send_sem=send_sems.at[sem], recv_sem=recv_sems.at[sem],
                    device_id=peer_id, device_id_type=MESH)
                cp.start()
                wave.append(cp)
        for cp in wave:
            cp.wait_send()
        copies += wave
    for cp in copies:
        cp.wait_recv()
    for cp in mine:
        cp.wait()


def _comm_sems(n, waves=1):
    return [pltpu.SemaphoreType.DMA((7 * n * waves,)), pltpu.SemaphoreType.DMA((7 * n * waves,)),
            pltpu.SemaphoreType.DMA((n,))]


def _scatter_blocks(parts, *, name, waves=1):
    n = len(parts)

    def body(*refs):
        _scatter_copies(refs[:n], refs[n:2 * n], *refs[2 * n:], waves=waves)

    return pl.pallas_call(
        body, name=name,
        in_specs=[ANY] * n, out_specs=[ANY] * n,
        out_shape=[jax.ShapeDtypeStruct(p.shape, p.dtype) for p in parts],
        scratch_shapes=_comm_sems(n, waves),
    )(*parts)


def _handshake(peers):
    barrier = pltpu.get_barrier_semaphore()
    for peer in peers:
        pl.semaphore_signal(barrier, inc=1, device_id=peer, device_id_type=MESH)
    pl.semaphore_wait(barrier, len(peers))


def _sequencer_call(arrays, out_types, copies_fn, peers_fn, *, name, collective_id, waves=1):
    n = len(arrays)
    srcs = [jax.new_ref(a, memory_space=pltpu.MemorySpace.HBM) for a in arrays]
    dsts = [jax.empty_ref(t, memory_space=pltpu.MemorySpace.HBM) for t in out_types]
    extra = {} if waves == 1 else {"waves": waves}

    @pl.kernel(mesh=plsc.ScalarSubcoreMesh(axis_name="sequencer", num_cores=1), name=name,
               scratch_types=_comm_sems(n, waves), compiler_params=pltpu.CompilerParams(collective_id=collective_id))
    def launch(send_sems, recv_sems, local_sems):
        _handshake(peers_fn())
        copies_fn(srcs, dsts, send_sems, recv_sems, local_sems, **extra)

    launch()
    return [d[...] for d in dsts]


def _all_other_devices():
    x, y, c = _position()
    return [(1 - x if fx else x, 1 - y if fy else y, 1 - c if fc else c) for fx, fy, fc in FLIPS]


def _gather_relay_peers():
    x, y, c = _position()
    return [(x, y, 1 - c), (1 - x, y, c), (x, 1 - y, c)]


def _scatter_blocks_async(parts, *, name, collective_id, waves=1):
    return _sequencer_call(parts, [jax.ShapeDtypeStruct(p.shape, p.dtype) for p in parts],
                           _scatter_copies, _all_other_devices, name=name, collective_id=collective_id, waves=waves)


def _all_gather_async(shards, *, name, collective_id):
    return _sequencer_call(shards, [jax.ShapeDtypeStruct((N_DEV, *s.shape), s.dtype) for s in shards],
                           _gather_copies_relayed, _gather_relay_peers, name=name, collective_id=collective_id)


def _adamw_math(w, g, m, v):
    m = ADAM_B1 * m + (1.0 - ADAM_B1) * g
    v = ADAM_B2 * v + (1.0 - ADAM_B2) * (g * g)
    m_hat = m / (1.0 - ADAM_B1 ** ADAM_STEP)
    v_hat = v / (1.0 - ADAM_B2 ** ADAM_STEP)
    delta = -ADAM_LR * (m_hat / (jnp.sqrt(v_hat) + ADAM_EPS) + ADAM_WD * w)
    return delta, m, v


def _adamw_reduce_call(recv, w, m, v, *, name, T=128):
    R, W = w.shape
    if R % T == 0:
        tr, tw = T, W
    elif (R // 2) % 16 == 0:
        tr, tw = R // 2, W
    else:
        tr, tw = R, 2 * LANES

    def body(p_ref, w_ref, m_ref, v_ref, g_out, d_out, m_out, v_out):
        g = p_ref[0].astype(F32)
        for d in range(1, N_DEV):
            g = g + p_ref[d].astype(F32)
        g_out[...] = g
        d_out[...], m_out[...], v_out[...] = _adamw_math(w_ref[...], g, m_ref[...], v_ref[...])

    row = pl.BlockSpec((tr, tw), lambda i, j: (i, j))
    out = jax.ShapeDtypeStruct((R, W), F32)
    return pl.pallas_call(
        body, name=name, grid=(R // tr, W // tw),
        in_specs=[pl.BlockSpec((N_DEV, tr, tw), lambda i, j: (0, i, j)), row, row, row],
        out_specs=[row] * 4, out_shape=[out] * 4,
        compiler_params=_cp(("parallel", "parallel"), VMEM_BIG),
    )(recv, w, m, v)


SMALL_EARLY = (("b_gate", 8), ("w_spatial", 512), ("b_spatial", 8), ("norm_post_mix", 8), ("norm_pre_ffn", 8),
               ("norm_post_ffn", 8), ("w_gate_up", 128), ("gla_norm", 64), ("sgu_ln_g", 64), ("sgu_ln_b", 64))
SMALL_EARLY_AT = {}
for _name, _rows in SMALL_EARLY:
    SMALL_EARLY_AT[_name] = sum(r for _, r in SMALL_EARLY[:len(SMALL_EARLY_AT)])
SMALL_EARLY_ROWS = sum(r for _, r in SMALL_EARLY)
SMALL_LATE_ROWS = 16


def _small_early_rows(grads):
    def rows(a, n_rows):
        a = a.reshape(-1, LANES)
        return jnp.pad(a, ((0, n_rows - a.shape[0]), (0, 0)))

    def device_major(a, n_rows):
        r, c = a.shape[0], a.shape[1] // N_DEV
        a = a.reshape(r, N_DEV, c).transpose(1, 0, 2)
        a = jnp.pad(a, ((0, 0), (0, n_rows // N_DEV - r), (0, LANES - c)))
        return a.reshape(n_rows, LANES)

    pieces = []
    for name, n_rows in SMALL_EARLY:
        g = grads[name]
        if name in SMALL_SHARDED:
            pieces.append(device_major(g.reshape(g.shape[0], -1) if g.ndim == 2 else g.reshape(g.shape[0], g.shape[-1]), n_rows))
        else:
            pieces.append(rows(g, n_rows))
    return jnp.concatenate(pieces, axis=0)


def _small_update_call(early, dg_pre_mix, loss_part, w, m, v, *, name):
    names = list(SMALL)
    n_p = len(names)
    E = SMALL_EARLY_ROWS

    def reduce_body(early_ref, dg1_ref, loss_ref, tot, got_early, got_late, late, send_sems, recv_sems, local_sems):
        for k in range(D_MODEL // LANES):
            late[k:k + 1, :] = dg1_ref[:, k * LANES:(k + 1) * LANES]
        late[8:16, :] = jnp.broadcast_to(loss_ref[...], (8, LANES))
        _gather_copies([early_ref, late], [got_early, got_late], send_sems, recv_sems, local_sems)
        acc, acc_late = got_early[0], got_late[0]
        for d in range(1, N_DEV):
            acc, acc_late = acc + got_early[d], acc_late + got_late[d]
        tot[0:E, :] = acc
        tot[E:E + SMALL_LATE_ROWS, :] = acc_late

    total = pl.pallas_call(
        reduce_body, name=name + "_reduce",
        in_specs=[VMEM_SPEC] * 3, out_specs=VMEM_SPEC,
        out_shape=jax.ShapeDtypeStruct((E + SMALL_LATE_ROWS, LANES), F32),
        scratch_shapes=[pltpu.VMEM((N_DEV, E, LANES), F32), pltpu.VMEM((N_DEV, SMALL_LATE_ROWS, LANES), F32),
                        pltpu.VMEM((SMALL_LATE_ROWS, LANES), F32),
                        pltpu.SemaphoreType.DMA((14,)), pltpu.SemaphoreType.DMA((14,)), pltpu.SemaphoreType.DMA((2,))],
    )(early, dg_pre_mix, loss_part)

    def body(tot, *rest):
        w_refs, m_refs, v_refs = [dict(zip(names, rest[i * n_p:(i + 1) * n_p])) for i in range(3)]
        outs = rest[3 * n_p:]
        loss_out = outs[0]
        g_out, d_out, m_out, v_out = [dict(zip(names, outs[1 + i * n_p:1 + (i + 1) * n_p])) for i in range(4)]
        loss_out[...] = tot[E + 8:E + 9, :]

        x, y, c = _position()
        me = 4 * x + 2 * y + c

        def update(name, g, ix):
            g_out[name][ix] = g
            d_out[name][ix], m_out[name][ix], v_out[name][ix] = _adamw_math(
                w_refs[name][ix], g, m_refs[name][ix], v_refs[name][ix])

        for name in names:
            shape = w[name].shape
            if name in SMALL_SHARDED:
                per_dev = dict(SMALL_EARLY)[name] // N_DEV
                at = pl.multiple_of(SMALL_EARLY_AT[name] + me * per_dev, 8)
                g = tot[pl.ds(at, per_dev), :]
                update(name, g[:shape[1], :shape[2]], (0,))
            elif name == "w_spatial":
                for grp in range(GROUPS):
                    at = SMALL_EARLY_AT[name] + grp * SBLOCK
                    update(name, tot[at:at + SBLOCK, :], (0, grp))
            elif name == "b_spatial":
                at = SMALL_EARLY_AT[name]
                update(name, tot[at:at + GROUPS, :], (0,))
            else:
                at = E if name == "norm_pre_mix" else SMALL_EARLY_AT[name]
                for k in range(shape[1] // LANES):
                    update(name, tot[at + k:at + k + 1, :], (slice(None), pl.ds(k * LANES, LANES)))

    state = [s[n] for s in (w, m, v) for n in names]
    out_shapes = [jax.ShapeDtypeStruct((1, LANES), F32)] + [jax.ShapeDtypeStruct(w[n].shape, F32) for n in names] * 4
    outs = pl.pallas_call(
        body, name=name + "_adamw",
        in_specs=[VMEM_SPEC] * (1 + len(state)), out_specs=[VMEM_SPEC] * len(out_shapes), out_shape=out_shapes,
    )(total, *state)
    per_name = {n: tuple(outs[1 + i * n_p + j] for i in range(4)) for j, n in enumerate(names)}
    return outs[0], per_name


def _tile_rows(n_elems):
    return -(-n_elems // (8 * LANES)) * 8


def _pack_rows(parts, rows):
    pieces = []
    for p in parts:
        q = p.reshape(-1, LANES)
        pieces.append(jnp.pad(q, ((0, _tile_rows(p.size) - q.shape[0]), (0, 0))))
    buf = jnp.concatenate(pieces, axis=0)
    return jnp.pad(buf, ((0, rows - buf.shape[0]), (0, 0)))


def _unpack_rows(buf, shapes):
    out, off = [], 0
    for shp in shapes:
        n = 1
        for s in shp:
            n *= s
        out.append(buf[off:off + n // LANES].reshape(shp))
        off += _tile_rows(n)
    return out


def _in_w_blocks_call(a, live, finished, *, name, tk=TOKEN_TILE):
    S = a.shape[0]
    tk = min(tk, S)
    steps = S // tk
    n_live = len(live)
    out_shape = (N_DEV, IN_BLK, D_MODEL)

    def body(a_ref, *rest):
        live_refs, done_refs = rest[:n_live], rest[n_live:n_live + len(finished)]
        out_ref, accs = rest[n_live + len(finished)], rest[n_live + len(finished) + 1:]
        k = pl.program_id(0)

        @pl.when(k == 0)
        def _():
            for acc in accs:
                acc[...] = jnp.zeros_like(acc)

        av = a_ref[...]
        for src, acc in zip(live_refs, accs):
            for m0 in range(0, src.shape[1], 1024):
                m1 = min(src.shape[1], m0 + 1024)
                acc[m0:m1, :] += _dot(src[:, m0:m1], av, TN)

        @pl.when(k == steps - 1)
        def _():
            groups = [(acc, cols) for acc, (_, cols) in zip(accs, live)]
            groups += [(ref, cols) for ref, (_, cols) in zip(done_refs, finished)]
            for d in range(N_DEV):
                lo, hi = IN_BLK * d, IN_BLK * (d + 1)
                for ref, (c0, c1) in groups:
                    s0, e0 = max(lo, c0), min(hi, c1)
                    if s0 < e0:
                        out_ref[d, s0 - lo:e0 - lo, :] = ref[s0 - c0:e0 - c0, :].astype(BF16)

    acc_shapes = [(arr.shape[1], D_MODEL) for arr, _ in live]
    tiles = [((tk, D_MODEL), BF16)] + [((tk, arr.shape[1]), BF16) for arr, _ in live]
    resident = [(arr.shape, arr.dtype) for arr, _ in finished] + [(out_shape, BF16)] + [(s, F32) for s in acc_shapes]
    return pl.pallas_call(
        body, name=name, grid=(steps,),
        in_specs=[pl.BlockSpec((tk, D_MODEL), lambda k: (k, 0))]
        + [pl.BlockSpec((tk, arr.shape[1]), lambda k: (k, 0)) for arr, _ in live]
        + [_res(arr.shape) for arr, _ in finished],
        out_specs=_acc(out_shape),
        out_shape=jax.ShapeDtypeStruct(out_shape, BF16),
        scratch_shapes=[pltpu.VMEM(s, F32) for s in acc_shapes],
        compiler_params=_cp(("arbitrary",), _vmem_limit(tiles, resident)),
    )(a, *[arr for arr, _ in live], *[arr for arr, _ in finished])


def _local_step(x, target, W, scatter, finish):
    g1, g2, g3, g4 = [W[n].reshape(1, D_MODEL) for n in ("norm_pre_mix", "norm_post_mix", "norm_pre_ffn", "norm_post_ffn")]
    wfi, wfo = W["w_ffn_in"], W["w_ffn_out"].reshape(4, FF_BLK, D_MODEL)
    wg = jnp.pad(W["w_gate_up"], ((0, LANES - RANK), (0, 0))).astype(BF16)
    bgate = W["b_gate"].reshape(1, QK)
    gnorm = W["gla_norm"].reshape(HEADS, 1, DV)
    ln_g = W["sgu_ln_g"].reshape(GROUPS, 1, DG)
    ln_b = W["sgu_ln_b"].reshape(GROUPS, 1, DG)
    w_sp = W["w_spatial"]
    b_sp = W["b_spatial"].reshape(GROUPS, SBLOCK, 1)

    a, pa, alow, ps, pg, w_in_t = _in_proj_call(x, g1, W["w_in_blocks"], name="in_proj")
    y_gla, states = _gla_fwd_call(pa, alow, wg, bgate, gnorm, name="gla_fwd")
    y_sgu = _sgu_fwd_call(ps, ln_g, ln_b, w_sp, b_sp, name="sgu_fwd")
    t1, t2, merged, mix, x1, h = _mixer_tail_call(y_gla, y_sgu, pg, x, W["w_branch_gla"], W["w_branch_sgu"],
                                                  W["w_out"], g2, g3, name="mixer_tail")
    gu, z = _ffn_in_call(h, wfi, name="ffn_in")
    loss, dx2, dy, dg4 = _ffn_out_loss_call(z, wfo, x1, target, g4, name="ffn_out_loss")

    grads = {"norm_post_ffn": dg4}
    done = {}
    dgu = _ffn_out_bwd_call(dy, wfo, gu, name="ffn_out_bwd")
    dw_ffn_out = _weight_grads_call([z, dy], [(0, 1)], name="d_ffn_out_w")[0].reshape(N_DEV, D_FF // N_DEV, D_MODEL)
    dw_ffn_in = _ffn_in_grad_call(h, dgu, name="d_ffn_in_w")
    dgu, dw_ffn_out, dw_ffn_in = lax.optimization_barrier((dgu, dw_ffn_out, dw_ffn_in))
    ffn_received = scatter(("w_ffn_out", "w_ffn_in"), [dw_ffn_out, dw_ffn_in])
    dx1, dmix, grads["norm_pre_ffn"], grads["norm_post_mix"] = _ffn_in_bwd_call(dgu, wfi, dx2, x1, mix, g3, g2, name="ffn_in_bwd")
    dt1, dt2, dpg, dy_gla, dy_sgu = _mixer_bwd_call(dmix, t1, t2, pg, W["w_out"], W["w_branch_gla"], W["w_branch_sgu"],
                                                    name="mixer_bwd")
    rows = D_MODEL // N_DEV
    mixer_grads = _weight_grads_call([merged, dmix, y_gla, dt1, y_sgu, dt2], [(0, 1), (2, 3), (4, 5)], name="d_mixer_w")
    dy_gla, dy_sgu, mixer_grads = lax.optimization_barrier((dy_gla, dy_sgu, mixer_grads))
    mixer_received = scatter(("w_out", "w_branch_gla", "w_branch_sgu"),
                             [g.reshape(N_DEV, rows, D_MODEL) for g in mixer_grads])
    dps, dlg, dlb, dwsp, dbsp = _sgu_bwd_call(ps, ln_g, ln_b, w_sp, b_sp, dy_sgu, name="sgu_bwd")
    dw_s, dw_g = _weight_grads_call([a, dps, dpg], [(1, 0), (2, 0)], name="d_in_w_sgu_gates")
    dy_gla, dw_s, dw_g = lax.optimization_barrier((dy_gla, dw_s, dw_g))
    dpa, dlogit, dgn, dbg = _gla_bwd_call(pa, alow, wg, bgate, gnorm, states, dy_gla, name="gla_bwd")
    ffn_received, mixer_received, dpa, dlogit = lax.optimization_barrier((ffn_received, mixer_received, dpa, dlogit))
    dlow = _mm(dlogit, wg, "nt", BF16, name="d_gate_up_x")
    dw_in = _in_w_blocks_call(a, [(dpa, A_COLS), (dlow, LOW_COLS)], [(dw_s, S_COLS), (dw_g, G_COLS)],
                              name="d_in_w_blocks")
    ffn_received, mixer_received, dw_in = lax.optimization_barrier((ffn_received, mixer_received, dw_in))
    in_received = scatter(("w_in",), [dw_in])
    early_done = {**finish(ffn_received), **finish(mixer_received)}
    dwg = _mm(alow, dlogit, "tn", F32, name="d_gate_up_w")
    grad_x, dg1 = _in_proj_bwd_call(dpa, dlow, dps, dpg, w_in_t, x, dx1, g1, name="in_proj_bwd")
    early_done, dg1 = lax.optimization_barrier((early_done, dg1))
    done.update(early_done)
    grads["norm_pre_mix"] = dg1

    grads["w_gate_up"] = dwg[:RANK]
    grads["b_gate"] = dbg
    grads["gla_norm"] = dgn
    grads["sgu_ln_g"] = dlg
    grads["sgu_ln_b"] = dlb
    grads["w_spatial"] = dwsp
    grads["b_spatial"] = dbsp
    done.update(finish(in_received))
    return loss, grad_x, grads, done


WEIGHTS = ("norm_pre_mix", "w_in", "w_gate_up", "b_gate", "gla_norm", "sgu_ln_g", "sgu_ln_b", "w_spatial",
           "b_spatial", "w_branch_gla", "w_branch_sgu", "w_out", "norm_post_mix", "norm_pre_ffn", "w_ffn_in",
           "w_ffn_out", "norm_post_ffn")
BIG = ("w_in", "w_branch_gla", "w_branch_sgu", "w_out", "w_ffn_in", "w_ffn_out")
MIXER = ("w_branch_gla", "w_branch_sgu", "w_out")
FFN = ("w_ffn_in", "w_ffn_out")
COLUMN_SHARDED = ("w_in", "w_ffn_in")
LATE_SCATTER_WAVES = 4
SMALL = tuple(n for n in WEIGHTS if n not in BIG)
SMALL_SHARDED = ("w_gate_up", "gla_norm", "sgu_ln_g", "sgu_ln_b")
SMALL_FULL = {"norm_pre_mix": (1024,), "w_gate_up": (16, 512), "b_gate": (512,), "gla_norm": (4, 256),
              "sgu_ln_g": (4, 256), "sgu_ln_b": (4, 256), "w_spatial": (4, 128, 128), "b_spatial": (4, 128),
              "norm_post_mix": (1024,), "norm_pre_ffn": (1024,), "norm_post_ffn": (1024,)}
SMALL_GRAD_ROWS = 648
SMALL_STATE_ROWS = 600
SMALL_GATHER_ROWS = 32


def kernel(x, norm_pre_mix, w_in, w_gate_up, b_gate, gla_norm, sgu_ln_g, sgu_ln_b, w_spatial, b_spatial, w_branch_gla, w_branch_sgu, w_out, norm_post_mix, norm_pre_ffn, w_ffn_in, w_ffn_out, norm_post_ffn, loss_target, m_norm_pre_mix, m_w_in, m_w_gate_up, m_b_gate, m_gla_norm, m_sgu_ln_g, m_sgu_ln_b, m_w_spatial, m_b_spatial, m_w_branch_gla, m_w_branch_sgu, m_w_out, m_norm_post_mix, m_norm_pre_ffn, m_w_ffn_in, m_w_ffn_out, m_norm_post_ffn, v_norm_pre_mix, v_w_in, v_w_gate_up, v_b_gate, v_gla_norm, v_sgu_ln_g, v_sgu_ln_b, v_w_spatial, v_b_spatial, v_w_branch_gla, v_w_branch_sgu, v_w_out, v_norm_post_mix, v_norm_pre_ffn, v_w_ffn_in, v_w_ffn_out, v_norm_post_ffn):
    given = dict(locals())
    def local(a, n):
        return a[0].T if n in COLUMN_SHARDED else a[0]

    w = {n: local(given[n], n) for n in WEIGHTS}
    m = {n: local(given["m_" + n], n) for n in WEIGHTS}
    v = {n: local(given["v_" + n], n) for n in WEIGHTS}
    xs, target = x[0], loss_target[0]
    me = 4 * lax.axis_index("x") + 2 * lax.axis_index("y") + lax.axis_index("c")

    small_shard = _pack_rows([w[n] for n in SMALL_SHARDED], SMALL_GATHER_ROWS)
    first = _all_gather_hbm([w["w_in"].astype(BF16), small_shard], name="gather_w_in")
    rest, first = lax.optimization_barrier(([w[n].astype(BF16) for n in MIXER + FFN], first))
    rest_blocks = _all_gather_async(rest, name="gather_rest", collective_id=1)
    W = {n: w[n] for n in SMALL if n not in SMALL_SHARDED}
    blocks = dict(zip(MIXER + FFN, rest_blocks))
    for n in ("w_branch_gla", "w_branch_sgu", "w_out", "w_ffn_out"):
        W[n] = blocks[n].reshape(-1, D_MODEL)
    W["w_ffn_in"] = blocks["w_ffn_in"]
    W["w_in_blocks"] = first[0]
    small_blocks = first[1]
    off = 0
    for n in SMALL_SHARDED:
        r, c = w[n].shape
        blk = small_blocks[:, off:off + r * c // LANES].reshape(N_DEV, r, c)
        W[n] = blk.transpose(1, 0, 2).reshape(r, N_DEV * c)
        off += _tile_rows(r * c)

    scatter_ids = iter((3, 4, 5))

    def scatter(names, parts):
        waves = LATE_SCATTER_WAVES if "w_in" in names else 1
        got = _scatter_blocks_async(parts, name="scatter_" + "_".join(names), collective_id=next(scatter_ids),
                                    waves=waves)
        return dict(zip(names, got))

    def finish(received):
        return {n: _adamw_reduce_call(r, w[n], m[n], v[n], name="adamw_" + n) for n, r in received.items()}

    loss_part, grad_x, grads, big_done = _local_step(xs, target, W, scatter, finish)

    loss_row, small_done = _small_update_call(
        _small_early_rows(grads), grads["norm_pre_mix"], loss_part,
        *[{n: given[prefix + n] for n in SMALL} for prefix in ("", "m_", "v_")], name="small_update")

    def pick(i):
        return [small_done[n][i] if n in SMALL else (big_done[n][i].T if n in COLUMN_SHARDED else big_done[n][i])[None]
                for n in WEIGHTS]

    return (loss_row[0, 0], grad_x[None], *pick(0), *pick(1), *pick(2), *pick(3))
```

```python
import jax
import jax.numpy as jnp
from jax import lax
from jax.experimental import pallas as pl
from jax.experimental.pallas import tpu as pltpu
from jax.experimental.pallas import tpu_sc as plsc

F32 = jnp.float32
BF16 = jnp.bfloat16

D_MODEL = 1024
N_DEV = 8
CHUNK = 64
HEADS = 4
DK = 128
DV = 256
QK = HEADS * DK
GV = HEADS * DV
RANK = 16
GROUPS = 4
SBLOCK = 128
DG = 256
D_FF = 2816
FF_BLK = 704
EPS = 1e-6
Q_SCALE = DK ** -0.5
LANES = 128
VMEM_BIG = 48 * 1024 * 1024

D_IN = 7184
IN_BLK = 898
A_COLS = (0, 3072)
LOW_COLS = (3072, 3088)
S_COLS = (3088, 5136)
G_COLS = (5136, 7184)

ADAM_LR = 0.001
ADAM_B1 = 0.9
ADAM_B2 = 0.999
ADAM_EPS = 1e-08
ADAM_WD = 0.01
ADAM_STEP = 10

MESH = pl.DeviceIdType.MESH
ANY = pl.BlockSpec(memory_space=pl.ANY)
VMEM_SPEC = pl.BlockSpec(memory_space=pltpu.VMEM)


def _cp(sem=None, vmem=None):
    return pltpu.CompilerParams(dimension_semantics=sem, vmem_limit_bytes=vmem)


def _sig(x):
    return 0.5 * jnp.tanh(0.5 * x) + 0.5


GELU_C = 0.7978845608028654
GELU_A = 0.044715


def _gelu(x):
    t = jnp.tanh((GELU_C * x) * (1.0 + GELU_A * (x * x)))
    return (0.5 * x) * (1.0 + t)


def _gelu_and_grad(x):
    x2 = x * x
    t = jnp.tanh((GELU_C * x) * (1.0 + GELU_A * x2))
    one_t = 1.0 + t
    hx = 0.5 * x
    grad = 0.5 * one_t + (hx * (1.0 - t * t)) * (GELU_C + (3.0 * GELU_A * GELU_C) * x2)
    return hx * one_t, grad


def _logsig(x):
    return jnp.minimum(x, 0.0) - jnp.log1p(jnp.exp(-jnp.abs(x)))


def _dot(a, b, dims):
    return lax.dot_general(a, b, (dims, ((), ())), preferred_element_type=F32)


NN = ((1,), (0,))
NT = ((1,), (1,))
TN = ((0,), (0,))


def _exact_mask_dot(mask_bf16, x):
    hi = x.astype(BF16)
    r1 = x - hi.astype(F32)
    mid = r1.astype(BF16)
    lo = (r1 - mid.astype(F32)).astype(BF16)
    return _dot(mask_bf16, hi, NN) + _dot(mask_bf16, mid, NN) + _dot(mask_bf16, lo, NN)


def _pick_tile(dim, target):
    if dim <= target:
        return dim
    best = None
    for t in range(LANES, int(1.4 * target) + 1, LANES):
        if dim % t == 0:
            best = t
    assert best is not None, (dim, target)
    return best


def _mm_call(a, b, *, name, grid, a_spec, b_spec, o_spec, out_shape, dims, acc_shape):
    nk = grid[2]

    def body(a_ref, b_ref, o_ref, *acc):
        part = _dot(a_ref[...], b_ref[...], dims)
        if nk == 1:
            o_ref[...] = part.astype(o_ref.dtype)
        else:
            acc_ref = acc[0]
            k = pl.program_id(2)

            @pl.when(k == 0)
            def _():
                acc_ref[...] = part

            @pl.when(k > 0)
            def _():
                acc_ref[...] += part

            @pl.when(k == nk - 1)
            def _():
                o_ref[...] = acc_ref[...].astype(o_ref.dtype)

    return pl.pallas_call(
        body, name=name, grid=grid, in_specs=[a_spec, b_spec], out_specs=o_spec, out_shape=out_shape,
        scratch_shapes=[] if nk == 1 else [pltpu.VMEM(acc_shape, F32)],
        compiler_params=_cp(("parallel", "parallel", "arbitrary"), VMEM_BIG),
    )(a, b)


def _mm(a, b, mode, out_dtype, *, name, tm=512, tn=1024, tk=1024):
    if mode == "nn":
        (M, K), (_, N) = a.shape, b.shape
    elif mode == "nt":
        (M, K), (N, _) = a.shape, b.shape
    else:
        (K, M), (_, N) = a.shape, b.shape
    tm, tn, tk = _pick_tile(M, tm), _pick_tile(N, tn), _pick_tile(K, tk)
    if mode == "nn":
        a_spec = pl.BlockSpec((tm, tk), lambda i, j, k: (i, k))
        b_spec = pl.BlockSpec((tk, tn), lambda i, j, k: (k, j))
        dims = NN
    elif mode == "nt":
        a_spec = pl.BlockSpec((tm, tk), lambda i, j, k: (i, k))
        b_spec = pl.BlockSpec((tn, tk), lambda i, j, k: (j, k))
        dims = NT
    else:
        a_spec = pl.BlockSpec((tk, tm), lambda i, j, k: (k, i))
        b_spec = pl.BlockSpec((tk, tn), lambda i, j, k: (k, j))
        dims = TN
    return _mm_call(a, b, name=name, grid=(M // tm, N // tn, K // tk), a_spec=a_spec, b_spec=b_spec,
                    o_spec=pl.BlockSpec((tm, tn), lambda i, j, k: (i, j)),
                    out_shape=jax.ShapeDtypeStruct((M, N), out_dtype), dims=dims, acc_shape=(tm, tn))


ROW_TILE = 512
SUB_ROWS = 256


def _sub_tiles(T):
    return [pl.ds(r0, min(SUB_ROWS, T)) for r0 in range(0, T, SUB_ROWS)]


def _rt(T, W, c=0):
    return pl.BlockSpec((T, W), lambda i: (i, c))


def _rt3(nb, T, W):
    return pl.BlockSpec((nb, T, W), lambda i: (0, i, 0))


def _res(shape):
    nd = len(shape)
    return pl.BlockSpec(tuple(shape), lambda i: (0,) * nd, pipeline_mode=pl.Buffered(1))


def _acc(shape):
    nd = len(shape)
    return pl.BlockSpec(tuple(shape), lambda i: (0,) * nd, pipeline_mode=pl.Buffered(1))


def _nbytes(shape, dtype):
    n = jnp.dtype(dtype).itemsize
    for s in shape:
        n *= s
    return n


def _vmem_limit(tiles, resident, temps=16 * 1024 * 1024):
    need = 2 * sum(_nbytes(s, d) for s, d in tiles) + sum(_nbytes(s, d) for s, d in resident) + temps
    return min(need, 60 * 1024 * 1024)


def _tok_call(body, *, name, S, T, ins, outs, semantics="parallel"):
    tiles = [(spec.block_shape, a.dtype) for a, spec, kind in ins + outs if kind == "tile"]
    resident = [(a.shape, a.dtype) for a, spec, kind in ins + outs if kind == "res"]
    return pl.pallas_call(
        body, name=name, grid=(S // T,),
        in_specs=[spec for _, spec, _ in ins], out_specs=[spec for _, spec, _ in outs],
        out_shape=[jax.ShapeDtypeStruct(a.shape, a.dtype) for a, _, _ in outs],
        compiler_params=_cp((semantics,), _vmem_limit(tiles, resident)),
    )(*[a for a, _, _ in ins])


def _tile(a, spec):
    return (a, spec, "tile")


def _whole(a):
    return (a, _res(a.shape), "res")


def _out_tile(shape, dtype, spec):
    return (jax.ShapeDtypeStruct(shape, dtype), spec, "tile")


def _out_acc(shape, dtype=F32):
    return (jax.ShapeDtypeStruct(shape, dtype), _acc(shape), "res")


def _rms_stats(x):
    r = lax.rsqrt(jnp.mean(x * x, axis=-1, keepdims=True) + EPS)
    return r, x * r


def _rms_bwd(xh, r, g, dy):
    dxh = dy * g
    dx = r * (dxh - xh * jnp.mean(dxh * xh, axis=-1, keepdims=True))
    dg = jnp.sum(dy * xh, axis=0, keepdims=True)
    return dx, dg


def _accum(ref, val):
    @pl.when(pl.program_id(0) == 0)
    def _():
        ref[...] = val

    @pl.when(pl.program_id(0) > 0)
    def _():
        ref[...] += val


def _dot_rows_t(a, w_ref, row0, o_ref, chunk=1024):
    n = o_ref.shape[1]
    for n0 in range(0, n, chunk):
        n1 = min(n, n0 + chunk)
        o_ref[:, n0:n1] = _dot(a, w_ref[row0 + n0:row0 + n1, :], NT).astype(o_ref.dtype)


def _in_proj_call(x, g1, w_blocks, *, name, T=ROW_TILE):
    S = x.shape[0]

    def body(x_ref, g_ref, blk_ref, a_ref, pa_ref, al_ref, ps_ref, pg_ref, w_ref):
        @pl.when(pl.program_id(0) == 0)
        def _():
            for d in range(N_DEV):
                w_ref[d * IN_BLK:(d + 1) * IN_BLK, :] = blk_ref[d]

        _, xh = _rms_stats(x_ref[...])
        a = (xh * g_ref[...]).astype(BF16)
        a_ref[...] = a
        _dot_rows_t(a, w_ref, A_COLS[0], pa_ref)
        _dot_rows_t(a, w_ref, LOW_COLS[0], al_ref)
        _dot_rows_t(a, w_ref, S_COLS[0], ps_ref)
        _dot_rows_t(a, w_ref, G_COLS[0], pg_ref)

    widths = (D_MODEL, A_COLS[1] - A_COLS[0], LANES, S_COLS[1] - S_COLS[0], G_COLS[1] - G_COLS[0])
    return _tok_call(
        body, name=name, S=S, T=T, semantics="arbitrary",
        ins=[_tile(x, _rt(T, D_MODEL)), _whole(g1), _whole(w_blocks)],
        outs=[_out_tile((S, w), BF16, _rt(T, w)) for w in widths] + [_out_acc((D_IN, D_MODEL), BF16)])


def _mixer_tail_call(y_gla, y_sgu, pg, x, w_bg, w_bs, w_out, g2, g3, *, name, T=ROW_TILE):
    S = x.shape[0]

    def body(yg_ref, ys_ref, pg_ref, x_ref, wbg_ref, wbs_ref, wo_ref, g2_ref, g3_ref,
             t1_ref, t2_ref, mg_ref, mix_ref, x1_ref, h_ref):
        for rows in _sub_tiles(T):
            t1 = _dot(yg_ref[rows, :], wbg_ref[...], NN)
            t2 = _dot(ys_ref[rows, :], wbs_ref[...], NN)
            t1_ref[rows, :] = t1.astype(BF16)
            t2_ref[rows, :] = t2.astype(BF16)
            sg = _sig(pg_ref[rows, pl.ds(0, D_MODEL)].astype(F32))
            ss = _sig(pg_ref[rows, pl.ds(D_MODEL, D_MODEL)].astype(F32))
            merged = (sg * t1 + ss * t2).astype(BF16)
            mg_ref[rows, :] = merged
            mix = _dot(merged, wo_ref[...], NN)
            mix_ref[rows, :] = mix
            _, mh = _rms_stats(mix)
            x1 = x_ref[rows, :] + mh * g2_ref[...]
            x1_ref[rows, :] = x1
            _, xh = _rms_stats(x1)
            h_ref[rows, :] = (xh * g3_ref[...]).astype(BF16)

    row = _rt(T, D_MODEL)
    b16 = lambda: _out_tile((S, D_MODEL), BF16, row)
    f32 = lambda: _out_tile((S, D_MODEL), F32, row)
    return _tok_call(
        body, name=name, S=S, T=T,
        ins=[_tile(y_gla, row), _tile(y_sgu, row), _tile(pg, _rt(T, 2 * D_MODEL)), _tile(x, row),
             _whole(w_bg), _whole(w_bs), _whole(w_out), _whole(g2), _whole(g3)],
        outs=[b16(), b16(), b16(), f32(), f32(), b16()])


def _ffn_in_call(h, wfi, *, name, T=ROW_TILE):
    S = h.shape[0]

    def body(h_ref, w_ref, gu_ref, z_ref):
        hv = h_ref[...]
        for d in range(4):
            gate = _dot(hv, w_ref[d], NT)
            up = _dot(hv, w_ref[d + 4], NT)
            gu_ref[d] = gate.astype(BF16)
            gu_ref[d + 4] = up.astype(BF16)
            z_ref[d] = (gate * _sig(gate) * up).astype(BF16)

    return _tok_call(
        body, name=name, S=S, T=T,
        ins=[_tile(h, _rt(T, D_MODEL)), _whole(wfi)],
        outs=[_out_tile((N_DEV, S, FF_BLK), BF16, _rt3(N_DEV, T, FF_BLK)),
              _out_tile((4, S, FF_BLK), BF16, _rt3(4, T, FF_BLK))])


def _ffn_out_loss_call(z, wfo, x1, target, g4, *, name, T=ROW_TILE):
    S = x1.shape[0]

    def body(z_ref, w_ref, x1_ref, t_ref, g4_ref, loss_ref, dx2_ref, dy_ref, dg4_ref):
        loss = jnp.zeros((1, 1), F32)
        dg4 = jnp.zeros((1, D_MODEL), F32)
        for rows in _sub_tiles(T):
            y = _dot(z_ref[0, rows, :], w_ref[0], NN)
            for d in range(1, 4):
                y = y + _dot(z_ref[d, rows, :], w_ref[d], NN)
            r, yh = _rms_stats(y)
            diff = x1_ref[rows, :] + yh * g4_ref[...] - t_ref[rows, :]
            loss = loss + 0.5 * jnp.sum(jnp.mean(diff * diff, axis=-1, keepdims=True), axis=0, keepdims=True)
            dx2 = diff * (1.0 / D_MODEL)
            dx2_ref[rows, :] = dx2
            dy, dg = _rms_bwd(yh, r, g4_ref[...], dx2)
            dy_ref[rows, :] = dy.astype(BF16)
            dg4 = dg4 + dg
        _accum(loss_ref, jnp.broadcast_to(loss, (1, LANES)))
        _accum(dg4_ref, dg4)

    row = _rt(T, D_MODEL)
    return _tok_call(
        body, name=name, S=S, T=T, semantics="arbitrary",
        ins=[_tile(z, _rt3(4, T, FF_BLK)), _whole(wfo), _tile(x1, row), _tile(target, row), _whole(g4)],
        outs=[_out_acc((1, LANES)), _out_tile((S, D_MODEL), F32, row), _out_tile((S, D_MODEL), BF16, row),
              _out_acc((1, D_MODEL))])


def _ffn_out_bwd_call(dy, wfo, gu, *, name, T=ROW_TILE):
    S = dy.shape[0]

    def body(dy_ref, w_ref, gu_ref, dgu_ref):
        dyv = dy_ref[...]
        for d in range(4):
            dz = _dot(dyv, w_ref[d], NT).astype(BF16)
            gt = gu_ref[d]
            up = gu_ref[d + 4]
            s = _sig(gt)
            dgu_ref[d] = dz * up * s * (1.0 + gt * (1.0 - s))
            dgu_ref[d + 4] = dz * gt * s

    blocks = _rt3(N_DEV, T, FF_BLK)
    return _tok_call(
        body, name=name, S=S, T=T,
        ins=[_tile(dy, _rt(T, D_MODEL)), _whole(wfo), _tile(gu, blocks)],
        outs=[_out_tile((N_DEV, S, FF_BLK), BF16, blocks)])[0]


def _ffn_in_bwd_call(dgu, wfi, dx2, x1, mix, g3, g2, *, name, T=ROW_TILE):
    S = x1.shape[0]

    def body(dgu_ref, w_ref, dx2_ref, x1_ref, mix_ref, g3_ref, g2_ref, dx1_ref, dmix_ref, dg3_ref, dg2_ref):
        dg3 = jnp.zeros((1, D_MODEL), F32)
        dg2 = jnp.zeros((1, D_MODEL), F32)
        for rows in _sub_tiles(T):
            dh = _dot(dgu_ref[0, rows, :], w_ref[0], NN)
            for d in range(1, N_DEV):
                dh = dh + _dot(dgu_ref[d, rows, :], w_ref[d], NN)
            r3, xh = _rms_stats(x1_ref[rows, :])
            d3, g = _rms_bwd(xh, r3, g3_ref[...], dh)
            dg3 = dg3 + g
            dx1 = dx2_ref[rows, :] + d3
            dx1_ref[rows, :] = dx1
            r2, mh = _rms_stats(mix_ref[rows, :])
            dmix, g = _rms_bwd(mh, r2, g2_ref[...], dx1)
            dg2 = dg2 + g
            dmix_ref[rows, :] = dmix.astype(BF16)
        _accum(dg3_ref, dg3)
        _accum(dg2_ref, dg2)

    row = _rt(T, D_MODEL)
    return _tok_call(
        body, name=name, S=S, T=T, semantics="arbitrary",
        ins=[_tile(dgu, _rt3(N_DEV, T, FF_BLK)), _whole(wfi), _tile(dx2, row), _tile(x1, row), _tile(mix, row),
             _whole(g3), _whole(g2)],
        outs=[_out_tile((S, D_MODEL), F32, row), _out_tile((S, D_MODEL), BF16, row),
              _out_acc((1, D_MODEL)), _out_acc((1, D_MODEL))])


def _mixer_bwd_call(dmix, t1, t2, pg, w_out, w_bg, w_bs, *, name, T=ROW_TILE):
    S = dmix.shape[0]

    def body(dmix_ref, t1_ref, t2_ref, pg_ref, wo_ref, wbg_ref, wbs_ref, dt1_ref, dt2_ref, dpg_ref, dyg_ref, dys_ref):
        gg, gs = pl.ds(0, D_MODEL), pl.ds(D_MODEL, D_MODEL)
        for rows in _sub_tiles(T):
            dm = _dot(dmix_ref[rows, :], wo_ref[...], NT)
            sg = _sig(pg_ref[rows, gg].astype(F32))
            ss = _sig(pg_ref[rows, gs].astype(F32))
            dt1 = (dm * sg).astype(BF16)
            dt2 = (dm * ss).astype(BF16)
            dt1_ref[rows, :] = dt1
            dt2_ref[rows, :] = dt2
            dpg_ref[rows, gg] = (dm * t1_ref[rows, :].astype(F32) * sg * (1.0 - sg)).astype(BF16)
            dpg_ref[rows, gs] = (dm * t2_ref[rows, :].astype(F32) * ss * (1.0 - ss)).astype(BF16)
            dyg_ref[rows, :] = _dot(dt1, wbg_ref[...], NT).astype(BF16)
            dys_ref[rows, :] = _dot(dt2, wbs_ref[...], NT).astype(BF16)

    row = _rt(T, D_MODEL)
    wide = _rt(T, 2 * D_MODEL)
    b16 = lambda: _out_tile((S, D_MODEL), BF16, row)
    return _tok_call(
        body, name=name, S=S, T=T,
        ins=[_tile(dmix, row), _tile(t1, row), _tile(t2, row), _tile(pg, wide), _whole(w_out), _whole(w_bg), _whole(w_bs)],
        outs=[b16(), b16(), _out_tile((S, 2 * D_MODEL), BF16, wide), b16(), b16()])


def _in_proj_bwd_call(dpa, dlow, dps, dpg, w_in_t, x, dx1, g1, *, name, T=ROW_TILE):
    S = x.shape[0]

    def body(dpa_ref, dl_ref, dps_ref, dpg_ref, w_ref, x_ref, dx1_ref, g1_ref, gx_ref, dg1_ref):
        da = (_dot(dpa_ref[...], w_ref[A_COLS[0]:A_COLS[1], :], NN)
              + _dot(dl_ref[...], w_ref[LOW_COLS[0]:LOW_COLS[0] + LANES, :], NN)
              + _dot(dps_ref[...], w_ref[S_COLS[0]:S_COLS[1], :], NN)
              + _dot(dpg_ref[...], w_ref[G_COLS[0]:G_COLS[1], :], NN))
        r, xh = _rms_stats(x_ref[...])
        dxa, dg = _rms_bwd(xh, r, g1_ref[...], da)
        gx_ref[...] = dx1_ref[...] + dxa
        _accum(dg1_ref, dg)

    row = _rt(T, D_MODEL)
    return _tok_call(
        body, name=name, S=S, T=T, semantics="arbitrary",
        ins=[_tile(dpa, _rt(T, dpa.shape[1])), _tile(dlow, _rt(T, dlow.shape[1])), _tile(dps, _rt(T, dps.shape[1])),
             _tile(dpg, _rt(T, dpg.shape[1])), _whole(w_in_t), _tile(x, row), _tile(dx1, row), _whole(g1)],
        outs=[_out_tile((S, D_MODEL), F32, row), _out_acc((1, D_MODEL))])


TOKEN_TILE = 512


def _weight_grads_part(arrays, pairs, *, tk=TOKEN_TILE, out_dtype=BF16):
    S = arrays[0].shape[-2]
    tk = min(tk, S)
    n_in = len(arrays)

    def out_shape(i, j):
        a, b = arrays[i], arrays[j]
        if a.ndim == 3:
            return (a.shape[0], a.shape[2], b.shape[1])
        if b.ndim == 3:
            return (b.shape[0], a.shape[1], b.shape[2])
        return (a.shape[1], b.shape[1])

    shapes = [out_shape(i, j) for i, j in pairs]

    in_place = out_dtype == F32

    def body(ins, outs, accs):
        k = pl.program_id(0)
        if in_place:
            accs = outs

        @pl.when(k == 0)
        def _():
            for acc in accs:
                acc[...] = jnp.zeros_like(acc)

        for (i, j), acc in zip(pairs, accs):
            a_ref, b_ref = ins[i], ins[j]
            if len(a_ref.shape) == 3:
                for n in range(a_ref.shape[0]):
                    acc[n] += _dot(a_ref[n], b_ref[...], TN)
            elif len(b_ref.shape) == 3:
                a = a_ref[...]
                for n in range(b_ref.shape[0]):
                    acc[n] += _dot(a, b_ref[n], TN)
            else:
                b = b_ref[...]
                for m0 in range(0, a_ref.shape[1], 1024):
                    m1 = min(a_ref.shape[1], m0 + 1024)
                    acc[m0:m1, :] += _dot(a_ref[:, m0:m1], b, TN)

        if not in_place:
            @pl.when(k == S // tk - 1)
            def _():
                for out, acc in zip(outs, accs):
                    out[...] = acc[...].astype(out.dtype)

    def in_spec(a):
        if a.ndim == 3:
            return pl.BlockSpec((a.shape[0], tk, a.shape[2]), lambda k: (0, k, 0))
        return pl.BlockSpec((tk, a.shape[1]), lambda k: (k, 0))

    return dict(
        body=body, steps=S // tk, arrays=list(arrays),
        in_specs=[in_spec(a) for a in arrays],
        out_specs=[_acc(s) for s in shapes],
        out_shapes=[jax.ShapeDtypeStruct(s, out_dtype) for s in shapes],
        scratch=[] if in_place else [pltpu.VMEM(s, F32) for s in shapes],
        tiles=[(in_spec(a).block_shape, a.dtype) for a in arrays],
        resident=[(s, F32) for s in shapes] + ([] if in_place else [(s, BF16) for s in shapes]))


def _weight_grads_call(arrays, pairs, *, name, tk=TOKEN_TILE):
    part = _weight_grads_part(arrays, pairs, tk=tk)
    n_in, n_out = len(part["arrays"]), len(part["out_shapes"])

    def body(*refs):
        part["body"](refs[:n_in], refs[n_in:n_in + n_out], refs[n_in + n_out:])

    return pl.pallas_call(
        body, name=name, grid=(part["steps"],),
        in_specs=part["in_specs"], out_specs=part["out_specs"], out_shape=part["out_shapes"],
        scratch_shapes=part["scratch"],
        compiler_params=_cp(("arbitrary",), _vmem_limit(part["tiles"], part["resident"])),
    )(*part["arrays"])


def _with_rider(body, n_in, n_out, n_scratch, rider):
    if rider is None:
        return body
    r_in, r_out = len(rider["arrays"]), len(rider["out_shapes"])

    def both(*refs):
        ins, rest = refs[:n_in + r_in], refs[n_in + r_in:]
        outs, scratch = rest[:n_out + r_out], rest[n_out + r_out:]
        body(*ins[:n_in], *outs[:n_out], *scratch[:n_scratch])
        rider["body"](ins[n_in:], outs[n_out:], scratch[n_scratch:])

    return both


def _rider_lists(rider):
    if rider is None:
        return [], [], [], [], [], [], []
    return (rider["arrays"], rider["in_specs"], rider["out_specs"], rider["out_shapes"], rider["scratch"],
            rider["tiles"], rider["resident"])


def _ffn_in_grad_call(h, dgu, *, name, tk=TOKEN_TILE):
    S = h.shape[0]
    tk = min(tk, S)
    nb = 4
    shape = (nb, FF_BLK, D_MODEL)

    def body(h_ref, dgu_ref, out_ref, acc):
        k = pl.program_id(1)

        @pl.when(k == 0)
        def _():
            acc[...] = jnp.zeros_like(acc)

        hv = h_ref[...]
        for n in range(nb):
            acc[n] += _dot(dgu_ref[n], hv, TN)

        @pl.when(k == S // tk - 1)
        def _():
            out_ref[...] = acc[...].astype(out_ref.dtype)

    tiles = [((tk, D_MODEL), BF16), ((nb, tk, FF_BLK), BF16), (shape, BF16)]
    return pl.pallas_call(
        body, name=name, grid=(N_DEV // nb, S // tk),
        in_specs=[pl.BlockSpec((tk, D_MODEL), lambda g, k: (k, 0)),
                  pl.BlockSpec((nb, tk, FF_BLK), lambda g, k: (g, k, 0))],
        out_specs=pl.BlockSpec(shape, lambda g, k: (g, 0, 0)),
        out_shape=jax.ShapeDtypeStruct((N_DEV, FF_BLK, D_MODEL), BF16),
        scratch_shapes=[pltpu.VMEM(shape, F32)],
        compiler_params=_cp(("parallel", "arbitrary"), _vmem_limit(tiles, [(shape, F32)])),
    )(h, dgu)


GLA_TILE = 256
Q_OFF, K_OFF, V_OFF, R_OFF = 0, QK, 2 * QK, 2 * QK + GV


def _tri(lower):
    r = lax.broadcasted_iota(jnp.int32, (CHUNK, CHUNK), 0)
    c = lax.broadcasted_iota(jnp.int32, (CHUNK, CHUNK), 1)
    return jnp.where((r >= c) if lower else (c >= r), 1.0, 0.0).astype(BF16)


def _gla_fwd_call(pa, alow, wg, bgate, gnorm, *, name):
    S = pa.shape[0]
    Tg = min(GLA_TILE, S)
    cb = Tg // CHUNK

    def body(pa_ref, al_ref, wg_ref, bg_ref, gn_ref, y_ref, st_ref, state):
        @pl.when(pl.program_id(0) == 0)
        def _():
            state[...] = jnp.zeros_like(state)

        logit = _dot(al_ref[...], wg_ref[...], NN) + bg_ref[...]
        ls = _logsig(logit) * (1.0 / 16.0)
        tri = _tri(True)
        for c in range(cb):
            rows = pl.ds(c * CHUNK, CHUNK)
            cum = _exact_mask_dot(tri, ls[c * CHUNK:(c + 1) * CHUNK])
            tot = cum[CHUNK - 1:CHUNK]
            kd = (pa_ref[rows, pl.ds(K_OFF, QK)].astype(F32) * jnp.exp(tot - cum)).astype(BF16)
            decay = jnp.exp(tot)
            for h in range(HEADS):
                lanes = slice(h * DK, (h + 1) * DK)
                new = state[h] * decay[:, lanes] + _dot(pa_ref[rows, pl.ds(V_OFF + h * DV, DV)], kd[:, lanes], TN)
                state[h] = new
                st_ref[c, h] = new
        for c in range(cb):
            rows = pl.ds(c * CHUNK, CHUNK)
            for h in range(HEADS):
                qs = (pa_ref[rows, pl.ds(Q_OFF + h * DK, DK)].astype(F32) * Q_SCALE).astype(BF16)
                o = _dot(qs, st_ref[c, h].astype(BF16), NT)
                rs = lax.rsqrt(jnp.mean(o * o, axis=-1, keepdims=True) + EPS)
                rr = pa_ref[rows, pl.ds(R_OFF + h * DV, DV)].astype(F32)
                y_ref[rows, pl.ds(h * DV, DV)] = (o * rs * gn_ref[h] * (rr * _sig(rr))).astype(BF16)

    return pl.pallas_call(
        body, name=name, grid=(S // Tg,),
        in_specs=[
            pl.BlockSpec((Tg, 2 * QK + 2 * GV), lambda t: (t, 0)),
            pl.BlockSpec((Tg, LANES), lambda t: (t, 0)),
            pl.BlockSpec((LANES, QK), lambda t: (0, 0)),
            pl.BlockSpec((1, QK), lambda t: (0, 0)),
            pl.BlockSpec((HEADS, 1, DV), lambda t: (0, 0, 0)),
        ],
        out_specs=[
            pl.BlockSpec((Tg, GV), lambda t: (t, 0)),
            pl.BlockSpec((cb, HEADS, DV, DK), lambda t: (t, 0, 0, 0)),
        ],
        out_shape=[jax.ShapeDtypeStruct((S, GV), BF16),
                   jax.ShapeDtypeStruct((S // CHUNK, HEADS, DV, DK), F32)],
        scratch_shapes=[pltpu.VMEM((HEADS, DV, DK), F32)],
        compiler_params=_cp(("arbitrary",), VMEM_BIG),
    )(pa, alow, wg, bgate, gnorm)


def _gla_bwd_call(pa, alow, wg, bgate, gnorm, states, dy, *, name, rider=None):
    S = pa.shape[0]
    Tg = min(GLA_TILE, S)
    cb = Tg // CHUNK
    nt = S // Tg
    r_arrays, r_in, r_out, r_shapes, r_scratch, r_tiles, r_resident = _rider_lists(rider)
    assert rider is None or rider["steps"] == nt

    def rev(t):
        return nt - 1 - t

    def body(pa_ref, al_ref, wg_ref, bg_ref, gn_ref, st_ref, prev_ref, dy_ref,
             dpa_ref, dl_ref, dgn_ref, dbg_ref, carry, pbuf):
        t = pl.program_id(0)

        @pl.when(t == 0)
        def _():
            carry[...] = jnp.zeros_like(carry)
            dgn_ref[...] = jnp.zeros_like(dgn_ref)
            dbg_ref[...] = jnp.zeros_like(dbg_ref)

        logit = _dot(al_ref[...], wg_ref[...], NN) + bg_ref[...]
        ls = _logsig(logit) * (1.0 / 16.0)
        sneg = 1.0 / (1.0 + jnp.exp(logit))
        tri = _tri(True)
        upper = _tri(False)
        first_tile = rev(t) == 0
        heads = range(HEADS)

        w, decay, kd = [], [], []
        for c in range(cb):
            rows = pl.ds(c * CHUNK, CHUNK)
            cum = _exact_mask_dot(tri, ls[c * CHUNK:(c + 1) * CHUNK])
            tot = cum[CHUNK - 1:CHUNK]
            w.append(jnp.exp(tot - cum))
            decay.append(jnp.exp(tot))
            kd.append(pa_ref[rows, pl.ds(K_OFF, QK)].astype(F32) * w[c])

        dgn = [jnp.zeros((1, DV), F32) for _ in heads]
        for c in range(cb):
            rows = pl.ds(c * CHUNK, CHUNK)
            for h in heads:
                gn = gn_ref[h]
                qs = (pa_ref[rows, pl.ds(Q_OFF + h * DK, DK)].astype(F32) * Q_SCALE).astype(BF16)
                st16 = st_ref[c, h].astype(BF16)
                o = _dot(qs, st16, NT)
                rs = lax.rsqrt(jnp.mean(o * o, axis=-1, keepdims=True) + EPS)
                oh = o * rs
                rr = pa_ref[rows, pl.ds(R_OFF + h * DV, DV)].astype(F32)
                sr = _sig(rr)
                dyv = dy_ref[rows, pl.ds(h * DV, DV)].astype(F32)
                dpa_ref[rows, pl.ds(R_OFF + h * DV, DV)] = (
                    dyv * oh * gn * sr * (1.0 + rr * (1.0 - sr))).astype(BF16)
                don = dyv * (rr * sr)
                dgn[h] = dgn[h] + jnp.sum(don * oh, axis=0, keepdims=True)
                doh = don * gn
                do16 = (rs * (doh - oh * jnp.mean(doh * oh, axis=-1, keepdims=True))).astype(BF16)
                dpa_ref[rows, pl.ds(Q_OFF + h * DK, DK)] = (_dot(do16, st16, NN) * Q_SCALE).astype(BF16)
                pbuf[c, h] = _dot(do16, qs, TN)
        for h in heads:
            dgn_ref[h] += dgn[h]

        dkd = [[None] * HEADS for _ in range(cb)]
        ddecay = [[None] * HEADS for _ in range(cb)]
        for c in reversed(range(cb)):
            rows = pl.ds(c * CHUNK, CHUNK)
            for h in heads:
                lanes = slice(h * DK, (h + 1) * DK)
                gt = pbuf[c, h] + carry[h]
                gt16 = gt.astype(BF16)
                dkd[c][h] = _dot(pa_ref[rows, pl.ds(V_OFF + h * DV, DV)], gt16, NN)
                dpa_ref[rows, pl.ds(V_OFF + h * DV, DV)] = _dot(kd[c][:, lanes].astype(BF16), gt16, NT).astype(BF16)
                if c > 0:
                    st_prev = st_ref[c - 1, h]
                else:
                    st_prev = jnp.where(first_tile, 0.0, prev_ref[0, h])
                ddecay[c][h] = jnp.sum(gt * st_prev, axis=0, keepdims=True)
                carry[h] = gt * decay[c][:, lanes]

        dbg = jnp.zeros((1, QK), F32)
        for c in range(cb):
            rows = pl.ds(c * CHUNK, CHUNK)
            dkd_c = jnp.concatenate(dkd[c], axis=1)
            dpa_ref[rows, pl.ds(K_OFF, QK)] = (dkd_c * w[c]).astype(BF16)
            e = dkd_c * kd[c]
            dtot = jnp.sum(e, axis=0, keepdims=True) + jnp.concatenate(ddecay[c], axis=1) * decay[c]
            dls = dtot - _exact_mask_dot(upper, e)
            dlogit = dls * (1.0 / 16.0) * sneg[c * CHUNK:(c + 1) * CHUNK]
            dl_ref[rows, :] = dlogit.astype(BF16)
            dbg = dbg + jnp.sum(dlogit, axis=0, keepdims=True)
        dbg_ref[...] += dbg

    wide = 2 * QK + 2 * GV
    tiles = [((Tg, wide), BF16), ((cb + 1, HEADS, DV, DK), F32), ((Tg, GV), BF16), ((Tg, wide), BF16),
             ((Tg, QK), BF16)] + r_tiles
    resident = [((cb + 1, HEADS, DV, DK), F32)] + r_resident
    return pl.pallas_call(
        _with_rider(body, 8, 4, 2, rider), name=name, grid=(nt,),
        in_specs=[
            pl.BlockSpec((Tg, wide), lambda t: (rev(t), 0)),
            pl.BlockSpec((Tg, LANES), lambda t: (rev(t), 0)),
            pl.BlockSpec((LANES, QK), lambda t: (0, 0)),
            pl.BlockSpec((1, QK), lambda t: (0, 0)),
            pl.BlockSpec((HEADS, 1, DV), lambda t: (0, 0, 0)),
            pl.BlockSpec((cb, HEADS, DV, DK), lambda t: (rev(t), 0, 0, 0)),
            pl.BlockSpec((1, HEADS, DV, DK), lambda t: (jnp.maximum(rev(t) * cb - 1, 0), 0, 0, 0)),
            pl.BlockSpec((Tg, GV), lambda t: (rev(t), 0)),
        ] + r_in,
        out_specs=[
            pl.BlockSpec((Tg, wide), lambda t: (rev(t), 0)),
            pl.BlockSpec((Tg, QK), lambda t: (rev(t), 0)),
            pl.BlockSpec((HEADS, 1, DV), lambda t: (0, 0, 0)),
            pl.BlockSpec((1, QK), lambda t: (0, 0)),
        ] + r_out,
        out_shape=[jax.ShapeDtypeStruct((S, wide), BF16), jax.ShapeDtypeStruct((S, QK), BF16),
                   jax.ShapeDtypeStruct((HEADS, 1, DV), F32), jax.ShapeDtypeStruct((1, QK), F32)] + r_shapes,
        scratch_shapes=[pltpu.VMEM((HEADS, DV, DK), F32), pltpu.VMEM((cb, HEADS, DV, DK), F32)] + r_scratch,
        compiler_params=_cp(("arbitrary",), _vmem_limit(tiles, resident)),
    )(pa, alow, wg, bgate, gnorm, states, states, dy, *r_arrays)


SGU_TILE = 256


def _sgu_mask():
    r = lax.broadcasted_iota(jnp.int32, (SBLOCK, SBLOCK), 0)
    c = lax.broadcasted_iota(jnp.int32, (SBLOCK, SBLOCK), 1)
    return (c < CHUNK) | (r >= CHUNK)


def _ln_stats(vf):
    mu = jnp.mean(vf, axis=-1, keepdims=True)
    xc = vf - mu
    rs = lax.rsqrt(jnp.mean(xc * xc, axis=-1, keepdims=True) + EPS)
    return rs, xc * rs


def _sgu_fwd_call(ps, ln_g, ln_b, w_sp, b_sp, *, name):
    S = ps.shape[0]
    Ts = min(SGU_TILE, S)

    def body(ps_ref, lg_ref, lb_ref, w_ref, b_ref, y_ref):
        mask = _sgu_mask()
        for g in range(GROUPS):
            wm = jnp.where(mask, w_ref[g], 0.0).astype(BF16)
            for p in range(Ts // SBLOCK):
                rows = pl.ds(p * SBLOCK, SBLOCK)
                u = _gelu(ps_ref[rows, pl.ds(g * DG, DG)].astype(F32))
                _, xh = _ln_stats(_gelu(ps_ref[rows, pl.ds(D_MODEL + g * DG, DG)].astype(F32)))
                vn = xh * lg_ref[g] + lb_ref[g]
                mixed = _dot(wm, vn.astype(BF16), NN) + b_ref[g]
                y_ref[rows, pl.ds(g * DG, DG)] = (u * mixed).astype(BF16)

    full3 = lambda a, b, c: pl.BlockSpec((a, b, c), lambda t: (0, 0, 0))
    return pl.pallas_call(
        body, name=name, grid=(S // Ts,),
        in_specs=[pl.BlockSpec((Ts, 2 * D_MODEL), lambda t: (t, 0)),
                  full3(GROUPS, 1, DG), full3(GROUPS, 1, DG), full3(GROUPS, SBLOCK, SBLOCK), full3(GROUPS, SBLOCK, 1)],
        out_specs=pl.BlockSpec((Ts, D_MODEL), lambda t: (t, 0)),
        out_shape=jax.ShapeDtypeStruct((S, D_MODEL), BF16),
        compiler_params=_cp(("parallel",)),
    )(ps, ln_g, ln_b, w_sp, b_sp)


def _sgu_bwd_call(ps, ln_g, ln_b, w_sp, b_sp, dy, *, name, rider=None):
    S = ps.shape[0]
    Ts = min(SGU_TILE, S)
    r_arrays, r_in, r_out, r_shapes, r_scratch, r_tiles, r_resident = _rider_lists(rider)
    assert rider is None or rider["steps"] == S // Ts

    def body(ps_ref, lg_ref, lb_ref, w_ref, b_ref, dy_ref, ds_ref, dlg_ref, dlb_ref, dw_ref, db_ref):
        @pl.when(pl.program_id(0) == 0)
        def _():
            dlg_ref[...] = jnp.zeros_like(dlg_ref)
            dlb_ref[...] = jnp.zeros_like(dlb_ref)
            dw_ref[...] = jnp.zeros_like(dw_ref)
            db_ref[...] = jnp.zeros_like(db_ref)

        mask = _sgu_mask()
        for g in range(GROUPS):
            wm = jnp.where(mask, w_ref[g], 0.0).astype(BF16)
            lg = lg_ref[g]
            for p in range(Ts // SBLOCK):
                rows = pl.ds(p * SBLOCK, SBLOCK)
                su = ps_ref[rows, pl.ds(g * DG, DG)].astype(F32)
                sv = ps_ref[rows, pl.ds(D_MODEL + g * DG, DG)].astype(F32)
                u, du = _gelu_and_grad(su)
                gv, dgv = _gelu_and_grad(sv)
                rs, xh = _ln_stats(gv)
                vn16 = (xh * lg + lb_ref[g]).astype(BF16)
                mixed = _dot(wm, vn16, NN) + b_ref[g]
                dyv = dy_ref[rows, pl.ds(g * DG, DG)].astype(F32)
                ds_ref[rows, pl.ds(g * DG, DG)] = (dyv * mixed * du).astype(BF16)
                dmix = dyv * u
                dmix16 = dmix.astype(BF16)
                db_ref[g] += jnp.sum(dmix, axis=-1, keepdims=True)
                dw_ref[g] += jnp.where(mask, _dot(dmix16, vn16, NT), 0.0)
                dvn = _dot(wm, dmix16, TN)
                dlg_ref[g] += jnp.sum(dvn * xh, axis=0, keepdims=True)
                dlb_ref[g] += jnp.sum(dvn, axis=0, keepdims=True)
                dxh = dvn * lg
                dvf = rs * (dxh - jnp.mean(dxh, axis=-1, keepdims=True)
                            - xh * jnp.mean(dxh * xh, axis=-1, keepdims=True))
                ds_ref[rows, pl.ds(D_MODEL + g * DG, DG)] = (dvf * dgv).astype(BF16)

    full3 = lambda a, b, c: pl.BlockSpec((a, b, c), lambda t: (0, 0, 0))
    tiles = [((Ts, 2 * D_MODEL), BF16), ((Ts, D_MODEL), BF16), ((Ts, 2 * D_MODEL), BF16)] + r_tiles
    return pl.pallas_call(
        _with_rider(body, 6, 5, 0, rider), name=name, grid=(S // Ts,),
        in_specs=[pl.BlockSpec((Ts, 2 * D_MODEL), lambda t: (t, 0)),
                  full3(GROUPS, 1, DG), full3(GROUPS, 1, DG), full3(GROUPS, SBLOCK, SBLOCK), full3(GROUPS, SBLOCK, 1),
                  pl.BlockSpec((Ts, D_MODEL), lambda t: (t, 0))] + r_in,
        out_specs=[pl.BlockSpec((Ts, 2 * D_MODEL), lambda t: (t, 0)),
                   full3(GROUPS, 1, DG), full3(GROUPS, 1, DG), full3(GROUPS, SBLOCK, SBLOCK), full3(GROUPS, SBLOCK, 1)]
        + r_out,
        out_shape=[jax.ShapeDtypeStruct((S, 2 * D_MODEL), BF16),
                   jax.ShapeDtypeStruct((GROUPS, 1, DG), F32), jax.ShapeDtypeStruct((GROUPS, 1, DG), F32),
                   jax.ShapeDtypeStruct((GROUPS, SBLOCK, SBLOCK), F32),
                   jax.ShapeDtypeStruct((GROUPS, SBLOCK, 1), F32)] + r_shapes,
        scratch_shapes=r_scratch,
        compiler_params=_cp(("arbitrary",), _vmem_limit(tiles, r_resident)),
    )(ps, ln_g, ln_b, w_sp, b_sp, dy, *r_arrays)


def _position():
    return lax.axis_index("x"), lax.axis_index("y"), lax.axis_index("c")


def _gather_copies(srcs, dsts, send_sems, recv_sems, local_sems, relay=False):
    if relay:
        return _gather_copies_relayed(srcs, dsts, send_sems, recv_sems, local_sems)
    x, y, c = _position()
    me, sibling = (x, y, c), (x, y, 1 - c)
    chips = [(1 - x, y), (x, 1 - y), (1 - x, 1 - y)]
    n = len(srcs)

    def slab(a, block):
        px, py, pc = block
        return dsts[a].at[4 * px + 2 * py + pc]

    def copy(a, k, block, to, src=None):
        return pltpu.make_async_remote_copy(
            src_ref=slab(a, block) if src is None else src, dst_ref=slab(a, block),
            send_sem=send_sems.at[7 * a + k], recv_sem=recv_sems.at[7 * a + k], device_id=to, device_id_type=MESH)

    mine = [pltpu.make_async_copy(srcs[a], slab(a, me), local_sems.at[a]) for a in range(n)]
    for cp in mine:
        cp.start()
    first = []
    for a in range(n):
        first.append(copy(a, 0, me, sibling, src=srcs[a]))
        first += [copy(a, 1 + j, me, (*chip, c), src=srcs[a]) for j, chip in enumerate(chips)]
    for cp in first:
        cp.start()
    passed = []
    for j, chip in enumerate(chips):
        for a in range(n):
            copy(a, 1 + j, (*chip, c), me).wait_recv()
            cp = copy(a, 4 + j, (*chip, c), sibling)
            cp.start()
            passed.append(cp)
    for a in range(n):
        copy(a, 0, sibling, me).wait_recv()
        for j, chip in enumerate(chips):
            copy(a, 4 + j, (*chip, 1 - c), me).wait_recv()
    for cp in first + passed:
        cp.wait_send()
    for cp in mine:
        cp.wait()


def _gather_copies_relayed(srcs, dsts, send_sems, recv_sems, local_sems):
    x, y, c = _position()
    me, sibling = (x, y, c), (x, y, 1 - c)
    x_chip, y_chip, d_chip = (1 - x, y), (x, 1 - y), (1 - x, 1 - y)
    south = c == 0
    relay_from = (jnp.where(south, x, 1 - x), jnp.where(south, 1 - y, y), c)
    relay_to = (jnp.where(south, 1 - x, x), jnp.where(south, y, 1 - y), c)
    n = len(srcs)

    def slab(a, block):
        px, py, pc = block
        return dsts[a].at[4 * px + 2 * py + pc]

    def copy(a, k, block, to, src=None):
        return pltpu.make_async_remote_copy(
            src_ref=slab(a, block) if src is None else src, dst_ref=slab(a, block),
            send_sem=send_sems.at[7 * a + k], recv_sem=recv_sems.at[7 * a + k], device_id=to, device_id_type=MESH)

    mine = [pltpu.make_async_copy(srcs[a], slab(a, me), local_sems.at[a]) for a in range(n)]
    for cp in mine:
        cp.start()
    sent = []
    for a in range(n):
        sent += [copy(a, 0, me, sibling, src=srcs[a]), copy(a, 1, me, (*x_chip, c), src=srcs[a]),
                 copy(a, 2, me, (*y_chip, c), src=srcs[a])]
    for cp in sent:
        cp.start()
    for a in range(n):
        copy(a, 1, (*x_chip, c), me).wait_recv()
        copy(a, 2, (*y_chip, c), me).wait_recv()
        later = [copy(a, 3, relay_from, relay_to), copy(a, 4, (*x_chip, c), sibling), copy(a, 5, (*y_chip, c), sibling)]
        for cp in later:
            cp.start()
        sent += later
    for a in range(n):
        copy(a, 3, (*d_chip, c), me).wait_recv()
        cp = copy(a, 6, (*d_chip, c), sibling)
        cp.start()
        sent.append(cp)
    for a in range(n):
        copy(a, 0, sibling, me).wait_recv()
        for k, chip in ((4, x_chip), (5, y_chip), (6, d_chip)):
            copy(a, k, (*chip, 1 - c), me).wait_recv()
    for cp in sent:
        cp.wait_send()
    for cp in mine:
        cp.wait()


def _all_gather_hbm(shards, *, name):
    n = len(shards)

    def body(*refs):
        srcs, dsts = refs[:n], refs[n:2 * n]
        send_sems, recv_sems, local_sems = refs[2 * n:]
        _gather_copies(srcs, dsts, send_sems, recv_sems, local_sems, relay=True)

    return pl.pallas_call(
        body, name=name,
        in_specs=[ANY] * n, out_specs=[ANY] * n,
        out_shape=[jax.ShapeDtypeStruct((N_DEV, *s.shape), s.dtype) for s in shards],
        scratch_shapes=_comm_sems(n),
    )(*shards)


def _all_reduce_small(part, *, name):
    R, W = part.shape

    def body(x_ref, out_ref, gathered, send_sems, recv_sems, local_sems):
        _gather_copies([x_ref], [gathered], send_sems, recv_sems, local_sems)
        acc = gathered[0]
        for d in range(1, N_DEV):
            acc = acc + gathered[d]
        out_ref[...] = acc

    return pl.pallas_call(
        body, name=name,
        in_specs=[VMEM_SPEC], out_specs=VMEM_SPEC,
        out_shape=jax.ShapeDtypeStruct((R, W), F32),
        scratch_shapes=[pltpu.VMEM((N_DEV, R, W), F32),
                        pltpu.SemaphoreType.DMA((7,)), pltpu.SemaphoreType.DMA((7,)), pltpu.SemaphoreType.DMA((1,))],
    )(part)


FLIPS = [(fx, fy, fc) for fx in (0, 1) for fy in (0, 1) for fc in (0, 1)][1:]


def _scatter_copies(srcs, dsts, send_sems, recv_sems, local_sems, waves=1):
    n = len(srcs)
    x, y, c = _position()
    me = 4 * x + 2 * y + c
    mine = [pltpu.make_async_copy(srcs[a].at[me], dsts[a].at[me], local_sems.at[a]) for a in range(n)]
    for cp in mine:
        cp.start()
    peers = []
    for fx, fy, fc in FLIPS:
        tx = 1 - x if fx else x
        ty = 1 - y if fy else y
        tc = 1 - c if fc else c
        peers.append(((tx, ty, tc), 4 * tx + 2 * ty + tc))
    copies = []
    for w in range(waves):
        wave = []
        for k, (peer_id, peer) in enumerate(peers):
            for a in range(n):
                rows = srcs[a].shape[1]
                step = rows if waves == 1 else (rows // waves) // 16 * 16
                r0 = w * step
                cut = pl.ds(r0, step if w < waves - 1 else rows - r0)
                sem = (7 * a + k) * waves + w
                cp = pltpu.make_async_remote_copy(
                    src_ref=srcs[a].at[peer, cut], dst_ref=dsts[a].at[me, cut],
                    send_sem=send_sems.at[sem], recv_sem=recv_sems.at[sem],
                    device_id=peer_id, device_id_type=MESH)
                cp.start()
                wave.append(cp)
        for cp in wave:
            cp.wait_send()
        copies += wave
    for cp in copies:
        cp.wait_recv()
    for cp in mine:
        cp.wait()


def _comm_sems(n, waves=1):
    return [pltpu.SemaphoreType.DMA((7 * n * waves,)), pltpu.SemaphoreType.DMA((7 * n * waves,)),
            pltpu.SemaphoreType.DMA((n,))]


def _scatter_blocks(parts, *, name, waves=1):
    n = len(parts)

    def body(*refs):
        _scatter_copies(refs[:n], refs[n:2 * n], *refs[2 * n:], waves=waves)

    return pl.pallas_call(
        body, name=name,
        in_specs=[ANY] * n, out_specs=[ANY] * n,
        out_shape=[jax.ShapeDtypeStruct(p.shape, p.dtype) for p in parts],
        scratch_shapes=_comm_sems(n, waves),
    )(*parts)


def _handshake(peers):
    barrier = pltpu.get_barrier_semaphore()
    for peer in peers:
        pl.semaphore_signal(barrier, inc=1, device_id=peer, device_id_type=MESH)
    pl.semaphore_wait(barrier, len(peers))


def _sequencer_call(arrays, out_types, copies_fn, peers_fn, *, name, collective_id, waves=1):
    n = len(arrays)
    srcs = [jax.new_ref(a, memory_space=pltpu.MemorySpace.HBM) for a in arrays]
    dsts = [jax.empty_ref(t, memory_space=pltpu.MemorySpace.HBM) for t in out_types]
    extra = {} if waves == 1 else {"waves": waves}

    @pl.kernel(mesh=plsc.ScalarSubcoreMesh(axis_name="sequencer", num_cores=1), name=name,
               scratch_types=_comm_sems(n, waves), compiler_params=pltpu.CompilerParams(collective_id=collective_id))
    def launch(send_sems, recv_sems, local_sems):
        _handshake(peers_fn())
        copies_fn(srcs, dsts, send_sems, recv_sems, local_sems, **extra)

    launch()
    return [d[...] for d in dsts]


def _all_other_devices():
    x, y, c = _position()
    return [(1 - x if fx else x, 1 - y if fy else y, 1 - c if fc else c) for fx, fy, fc in FLIPS]


def _gather_relay_peers():
    x, y, c = _position()
    return [(x, y, 1 - c), (1 - x, y, c), (x, 1 - y, c)]


def _scatter_blocks_async(parts, *, name, collective_id, waves=1):
    return _sequencer_call(parts, [jax.ShapeDtypeStruct(p.shape, p.dtype) for p in parts],
                           _scatter_copies, _all_other_devices, name=name, collective_id=collective_id, waves=waves)


def _all_gather_async(shards, *, name, collective_id):
    return _sequencer_call(shards, [jax.ShapeDtypeStruct((N_DEV, *s.shape), s.dtype) for s in shards],
                           _gather_copies_relayed, _gather_relay_peers, name=name, collective_id=collective_id)


def _adamw_math(w, g, m, v):
    m = ADAM_B1 * m + (1.0 - ADAM_B1) * g
    v = ADAM_B2 * v + (1.0 - ADAM_B2) * (g * g)
    m_hat = m / (1.0 - ADAM_B1 ** ADAM_STEP)
    v_hat = v / (1.0 - ADAM_B2 ** ADAM_STEP)
    delta = -ADAM_LR * (m_hat / (jnp.sqrt(v_hat) + ADAM_EPS) + ADAM_WD * w)
    return delta, m, v


def _adamw_reduce_call(recv, w, m, v, *, name, T=128):
    R, W = w.shape
    if R % T == 0:
        tr, tw = T, W
    elif (R // 2) % 16 == 0:
        tr, tw = R // 2, W
    else:
        tr, tw = R, 2 * LANES

    def body(p_ref, w_ref, m_ref, v_ref, g_out, d_out, m_out, v_out):
        g = p_ref[0].astype(F32)
        for d in range(1, N_DEV):
            g = g + p_ref[d].astype(F32)
        g_out[...] = g
        d_out[...], m_out[...], v_out[...] = _adamw_math(w_ref[...], g, m_ref[...], v_ref[...])

    row = pl.BlockSpec((tr, tw), lambda i, j: (i, j))
    out = jax.ShapeDtypeStruct((R, W), F32)
    return pl.pallas_call(
        body, name=name, grid=(R // tr, W // tw),
        in_specs=[pl.BlockSpec((N_DEV, tr, tw), lambda i, j: (0, i, j)), row, row, row],
        out_specs=[row] * 4, out_shape=[out] * 4,
        compiler_params=_cp(("parallel", "parallel"), VMEM_BIG),
    )(recv, w, m, v)


SMALL_EARLY = (("b_gate", 8), ("w_spatial", 512), ("b_spatial", 8), ("norm_post_mix", 8), ("norm_pre_ffn", 8),
               ("norm_post_ffn", 8), ("w_gate_up", 128), ("gla_norm", 64), ("sgu_ln_g", 64), ("sgu_ln_b", 64))
SMALL_EARLY_AT = {}
for _name, _rows in SMALL_EARLY:
    SMALL_EARLY_AT[_name] = sum(r for _, r in SMALL_EARLY[:len(SMALL_EARLY_AT)])
SMALL_EARLY_ROWS = sum(r for _, r in SMALL_EARLY)
SMALL_LATE_ROWS = 16


def _small_early_rows(grads):
    def rows(a, n_rows):
        a = a.reshape(-1, LANES)
        return jnp.pad(a, ((0, n_rows - a.shape[0]), (0, 0)))

    def device_major(a, n_rows):
        r, c = a.shape[0], a.shape[1] // N_DEV
        a = a.reshape(r, N_DEV, c).transpose(1, 0, 2)
        a = jnp.pad(a, ((0, 0), (0, n_rows // N_DEV - r), (0, LANES - c)))
        return a.reshape(n_rows, LANES)

    pieces = []
    for name, n_rows in SMALL_EARLY:
        g = grads[name]
        if name in SMALL_SHARDED:
            pieces.append(device_major(g.reshape(g.shape[0], -1) if g.ndim == 2 else g.reshape(g.shape[0], g.shape[-1]), n_rows))
        else:
            pieces.append(rows(g, n_rows))
    return jnp.concatenate(pieces, axis=0)


def _small_update_call(early, dg_pre_mix, loss_part, w, m, v, *, name):
    names = list(SMALL)
    n_p = len(names)
    E = SMALL_EARLY_ROWS

    def reduce_body(early_ref, dg1_ref, loss_ref, tot, got_early, got_late, late, send_sems, recv_sems, local_sems):
        for k in range(D_MODEL // LANES):
            late[k:k + 1, :] = dg1_ref[:, k * LANES:(k + 1) * LANES]
        late[8:16, :] = jnp.broadcast_to(loss_ref[...], (8, LANES))
        _gather_copies([early_ref, late], [got_early, got_late], send_sems, recv_sems, local_sems)
        acc, acc_late = got_early[0], got_late[0]
        for d in range(1, N_DEV):
            acc, acc_late = acc + got_early[d], acc_late + got_late[d]
        tot[0:E, :] = acc
        tot[E:E + SMALL_LATE_ROWS, :] = acc_late

    total = pl.pallas_call(
        reduce_body, name=name + "_reduce",
        in_specs=[VMEM_SPEC] * 3, out_specs=VMEM_SPEC,
        out_shape=jax.ShapeDtypeStruct((E + SMALL_LATE_ROWS, LANES), F32),
        scratch_shapes=[pltpu.VMEM((N_DEV, E, LANES), F32), pltpu.VMEM((N_DEV, SMALL_LATE_ROWS, LANES), F32),
                        pltpu.VMEM((SMALL_LATE_ROWS, LANES), F32),
                        pltpu.SemaphoreType.DMA((14,)), pltpu.SemaphoreType.DMA((14,)), pltpu.SemaphoreType.DMA((2,))],
    )(early, dg_pre_mix, loss_part)

    def body(tot, *rest):
        w_refs, m_refs, v_refs = [dict(zip(names, rest[i * n_p:(i + 1) * n_p])) for i in range(3)]
        outs = rest[3 * n_p:]
        loss_out = outs[0]
        g_out, d_out, m_out, v_out = [dict(zip(names, outs[1 + i * n_p:1 + (i + 1) * n_p])) for i in range(4)]
        loss_out[...] = tot[E + 8:E + 9, :]

        x, y, c = _position()
        me = 4 * x + 2 * y + c

        def update(name, g, ix):
            g_out[name][ix] = g
            d_out[name][ix], m_out[name][ix], v_out[name][ix] = _adamw_math(
                w_refs[name][ix], g, m_refs[name][ix], v_refs[name][ix])

        for name in names:
            shape = w[name].shape
            if name in SMALL_SHARDED:
                per_dev = dict(SMALL_EARLY)[name] // N_DEV
                at = pl.multiple_of(SMALL_EARLY_AT[name] + me * per_dev, 8)
                g = tot[pl.ds(at, per_dev), :]
                update(name, g[:shape[1], :shape[2]], (0,))
            elif name == "w_spatial":
                for grp in range(GROUPS):
                    at = SMALL_EARLY_AT[name] + grp * SBLOCK
                    update(name, tot[at:at + SBLOCK, :], (0, grp))
            elif name == "b_spatial":
                at = SMALL_EARLY_AT[name]
                update(name, tot[at:at + GROUPS, :], (0,))
            else:
                at = E if name == "norm_pre_mix" else SMALL_EARLY_AT[name]
                for k in range(shape[1] // LANES):
                    update(name, tot[at + k:at + k + 1, :], (slice(None), pl.ds(k * LANES, LANES)))

    state = [s[n] for s in (w, m, v) for n in names]
    out_shapes = [jax.ShapeDtypeStruct((1, LANES), F32)] + [jax.ShapeDtypeStruct(w[n].shape, F32) for n in names] * 4
    outs = pl.pallas_call(
        body, name=name + "_adamw",
        in_specs=[VMEM_SPEC] * (1 + len(state)), out_specs=[VMEM_SPEC] * len(out_shapes), out_shape=out_shapes,
    )(total, *state)
    per_name = {n: tuple(outs[1 + i * n_p + j] for i in range(4)) for j, n in enumerate(names)}
    return outs[0], per_name


def _tile_rows(n_elems):
    return -(-n_elems // (8 * LANES)) * 8


def _pack_rows(parts, rows):
    pieces = []
    for p in parts:
        q = p.reshape(-1, LANES)
        pieces.append(jnp.pad(q, ((0, _tile_rows(p.size) - q.shape[0]), (0, 0))))
    buf = jnp.concatenate(pieces, axis=0)
    return jnp.pad(buf, ((0, rows - buf.shape[0]), (0, 0)))


def _unpack_rows(buf, shapes):
    out, off = [], 0
    for shp in shapes:
        n = 1
        for s in shp:
            n *= s
        out.append(buf[off:off + n // LANES].reshape(shp))
        off += _tile_rows(n)
    return out


def _in_w_blocks_call(a, live, finished, *, name, tk=TOKEN_TILE):
    S = a.shape[0]
    tk = min(tk, S)
    steps = S // tk
    n_live = len(live)
    out_shape = (N_DEV, IN_BLK, D_MODEL)

    def body(a_ref, *rest):
        live_refs, done_refs = rest[:n_live], rest[n_live:n_live + len(finished)]
        out_ref, accs = rest[n_live + len(finished)], rest[n_live + len(finished) + 1:]
        k = pl.program_id(0)

        @pl.when(k == 0)
        def _():
            for acc in accs:
                acc[...] = jnp.zeros_like(acc)

        av = a_ref[...]
        for src, acc in zip(live_refs, accs):
            for m0 in range(0, src.shape[1], 1024):
                m1 = min(src.shape[1], m0 + 1024)
                acc[m0:m1, :] += _dot(src[:, m0:m1], av, TN)

        @pl.when(k == steps - 1)
        def _():
            groups = [(acc, cols) for acc, (_, cols) in zip(accs, live)]
            groups += [(ref, cols) for ref, (_, cols) in zip(done_refs, finished)]
            for d in range(N_DEV):
                lo, hi = IN_BLK * d, IN_BLK * (d + 1)
                for ref, (c0, c1) in groups:
                    s0, e0 = max(lo, c0), min(hi, c1)
                    if s0 < e0:
                        out_ref[d, s0 - lo:e0 - lo, :] = ref[s0 - c0:e0 - c0, :].astype(BF16)

    acc_shapes = [(arr.shape[1], D_MODEL) for arr, _ in live]
    tiles = [((tk, D_MODEL), BF16)] + [((tk, arr.shape[1]), BF16) for arr, _ in live]
    resident = [(arr.shape, arr.dtype) for arr, _ in finished] + [(out_shape, BF16)] + [(s, F32) for s in acc_shapes]
    return pl.pallas_call(
        body, name=name, grid=(steps,),
        in_specs=[pl.BlockSpec((tk, D_MODEL), lambda k: (k, 0))]
        + [pl.BlockSpec((tk, arr.shape[1]), lambda k: (k, 0)) for arr, _ in live]
        + [_res(arr.shape) for arr, _ in finished],
        out_specs=_acc(out_shape),
        out_shape=jax.ShapeDtypeStruct(out_shape, BF16),
        scratch_shapes=[pltpu.VMEM(s, F32) for s in acc_shapes],
        compiler_params=_cp(("arbitrary",), _vmem_limit(tiles, resident)),
    )(a, *[arr for arr, _ in live], *[arr for arr, _ in finished])


def _local_step(x, target, W, scatter, finish):
    g1, g2, g3, g4 = [W[n].reshape(1, D_MODEL) for n in ("norm_pre_mix", "norm_post_mix", "norm_pre_ffn", "norm_post_ffn")]
    wfi, wfo = W["w_ffn_in"], W["w_ffn_out"].reshape(4, FF_BLK, D_MODEL)
    wg = jnp.pad(W["w_gate_up"], ((0, LANES - RANK), (0, 0))).astype(BF16)
    bgate = W["b_gate"].reshape(1, QK)
    gnorm = W["gla_norm"].reshape(HEADS, 1, DV)
    ln_g = W["sgu_ln_g"].reshape(GROUPS, 1, DG)
    ln_b = W["sgu_ln_b"].reshape(GROUPS, 1, DG)
    w_sp = W["w_spatial"]
    b_sp = W["b_spatial"].reshape(GROUPS, SBLOCK, 1)

    a, pa, alow, ps, pg, w_in_t = _in_proj_call(x, g1, W["w_in_blocks"], name="in_proj")
    y_gla, states = _gla_fwd_call(pa, alow, wg, bgate, gnorm, name="gla_fwd")
    y_sgu = _sgu_fwd_call(ps, ln_g, ln_b, w_sp, b_sp, name="sgu_fwd")
    t1, t2, merged, mix, x1, h = _mixer_tail_call(y_gla, y_sgu, pg, x, W["w_branch_gla"], W["w_branch_sgu"],
                                                  W["w_out"], g2, g3, name="mixer_tail")
    gu, z = _ffn_in_call(h, wfi, name="ffn_in")
    loss, dx2, dy, dg4 = _ffn_out_loss_call(z, wfo, x1, target, g4, name="ffn_out_loss")

    grads = {"norm_post_ffn": dg4}
    done = {}
    dgu = _ffn_out_bwd_call(dy, wfo, gu, name="ffn_out_bwd")
    dw_ffn_out = _weight_grads_call([z, dy], [(0, 1)], name="d_ffn_out_w")[0].reshape(N_DEV, D_FF // N_DEV, D_MODEL)
    dw_ffn_in = _ffn_in_grad_call(h, dgu, name="d_ffn_in_w")
    dgu, dw_ffn_out, dw_ffn_in = lax.optimization_barrier((dgu, dw_ffn_out, dw_ffn_in))
    ffn_received = scatter(("w_ffn_out", "w_ffn_in"), [dw_ffn_out, dw_ffn_in])
    dx1, dmix, grads["norm_pre_ffn"], grads["norm_post_mix"] = _ffn_in_bwd_call(dgu, wfi, dx2, x1, mix, g3, g2, name="ffn_in_bwd")
    dt1, dt2, dpg, dy_gla, dy_sgu = _mixer_bwd_call(dmix, t1, t2, pg, W["w_out"], W["w_branch_gla"], W["w_branch_sgu"],
                                                    name="mixer_bwd")
    rows = D_MODEL // N_DEV
    mixer_grads = _weight_grads_call([merged, dmix, y_gla, dt1, y_sgu, dt2], [(0, 1), (2, 3), (4, 5)], name="d_mixer_w")
    dy_gla, dy_sgu, mixer_grads = lax.optimization_barrier((dy_gla, dy_sgu, mixer_grads))
    mixer_received = scatter(("w_out", "w_branch_gla", "w_branch_sgu"),
                             [g.reshape(N_DEV, rows, D_MODEL) for g in mixer_grads])
    dps, dlg, dlb, dwsp, dbsp = _sgu_bwd_call(ps, ln_g, ln_b, w_sp, b_sp, dy_sgu, name="sgu_bwd")
    dw_s, dw_g = _weight_grads_call([a, dps, dpg], [(1, 0), (2, 0)], name="d_in_w_sgu_gates")
    dy_gla, dw_s, dw_g = lax.optimization_barrier((dy_gla, dw_s, dw_g))
    dpa, dlogit, dgn, dbg = _gla_bwd_call(pa, alow, wg, bgate, gnorm, states, dy_gla, name="gla_bwd")
    ffn_received, mixer_received, dpa, dlogit = lax.optimization_barrier((ffn_received, mixer_received, dpa, dlogit))
    dlow = _mm(dlogit, wg, "nt", BF16, name="d_gate_up_x")
    dw_in = _in_w_blocks_call(a, [(dpa, A_COLS), (dlow, LOW_COLS)], [(dw_s, S_COLS), (dw_g, G_COLS)],
                              name="d_in_w_blocks")
    ffn_received, mixer_received, dw_in = lax.optimization_barrier((ffn_received, mixer_received, dw_in))
    in_received = scatter(("w_in",), [dw_in])
    early_done = {**finish(ffn_received), **finish(mixer_received)}
    dwg = _mm(alow, dlogit, "tn", F32, name="d_gate_up_w")
    grad_x, dg1 = _in_proj_bwd_call(dpa, dlow, dps, dpg, w_in_t, x, dx1, g1, name="in_proj_bwd")
    early_done, dg1 = lax.optimization_barrier((early_done, dg1))
    done.update(early_done)
    grads["norm_pre_mix"] = dg1

    grads["w_gate_up"] = dwg[:RANK]
    grads["b_gate"] = dbg
    grads["gla_norm"] = dgn
    grads["sgu_ln_g"] = dlg
    grads["sgu_ln_b"] = dlb
    grads["w_spatial"] = dwsp
    grads["b_spatial"] = dbsp
    done.update(finish(in_received))
    return loss, grad_x, grads, done


WEIGHTS = ("norm_pre_mix", "w_in", "w_gate_up", "b_gate", "gla_norm", "sgu_ln_g", "sgu_ln_b", "w_spatial",
           "b_spatial", "w_branch_gla", "w_branch_sgu", "w_out", "norm_post_mix", "norm_pre_ffn", "w_ffn_in",
           "w_ffn_out", "norm_post_ffn")
BIG = ("w_in", "w_branch_gla", "w_branch_sgu", "w_out", "w_ffn_in", "w_ffn_out")
MIXER = ("w_branch_gla", "w_branch_sgu", "w_out")
FFN = ("w_ffn_in", "w_ffn_out")
COLUMN_SHARDED = ("w_in", "w_ffn_in")
LATE_SCATTER_WAVES = 4
SMALL = tuple(n for n in WEIGHTS if n not in BIG)
SMALL_SHARDED = ("w_gate_up", "gla_norm", "sgu_ln_g", "sgu_ln_b")
SMALL_FULL = {"norm_pre_mix": (1024,), "w_gate_up": (16, 512), "b_gate": (512,), "gla_norm": (4, 256),
              "sgu_ln_g": (4, 256), "sgu_ln_b": (4, 256), "w_spatial": (4, 128, 128), "b_spatial": (4, 128),
              "norm_post_mix": (1024,), "norm_pre_ffn": (1024,), "norm_post_ffn": (1024,)}
SMALL_GRAD_ROWS = 648
SMALL_STATE_ROWS = 600
SMALL_GATHER_ROWS = 32


def kernel(x, norm_pre_mix, w_in, w_gate_up, b_gate, gla_norm, sgu_ln_g, sgu_ln_b, w_spatial, b_spatial, w_branch_gla, w_branch_sgu, w_out, norm_post_mix, norm_pre_ffn, w_ffn_in, w_ffn_out, norm_post_ffn, loss_target, m_norm_pre_mix, m_w_in, m_w_gate_up, m_b_gate, m_gla_norm, m_sgu_ln_g, m_sgu_ln_b, m_w_spatial, m_b_spatial, m_w_branch_gla, m_w_branch_sgu, m_w_out, m_norm_post_mix, m_norm_pre_ffn, m_w_ffn_in, m_w_ffn_out, m_norm_post_ffn, v_norm_pre_mix, v_w_in, v_w_gate_up, v_b_gate, v_gla_norm, v_sgu_ln_g, v_sgu_ln_b, v_w_spatial, v_b_spatial, v_w_branch_gla, v_w_branch_sgu, v_w_out, v_norm_post_mix, v_norm_pre_ffn, v_w_ffn_in, v_w_ffn_out, v_norm_post_ffn):
    given = dict(locals())
    def local(a, n):
        return a[0].T if n in COLUMN_SHARDED else a[0]

    w = {n: local(given[n], n) for n in WEIGHTS}
    m = {n: local(given["m_" + n], n) for n in WEIGHTS}
    v = {n: local(given["v_" + n], n) for n in WEIGHTS}
    xs, target = x[0], loss_target[0]
    me = 4 * lax.axis_index("x") + 2 * lax.axis_index("y") + lax.axis_index("c")

    small_shard = _pack_rows([w[n] for n in SMALL_SHARDED], SMALL_GATHER_ROWS)
    first = _all_gather_hbm([w["w_in"].astype(BF16), small_shard], name="gather_w_in")
    rest, first = lax.optimization_barrier(([w[n].astype(BF16) for n in MIXER + FFN], first))
    rest_blocks = _all_gather_async(rest, name="gather_rest", collective_id=1)
    W = {n: w[n] for n in SMALL if n not in SMALL_SHARDED}
    blocks = dict(zip(MIXER + FFN, rest_blocks))
    for n in ("w_branch_gla", "w_branch_sgu", "w_out", "w_ffn_out"):
        W[n] = blocks[n].reshape(-1, D_MODEL)
    W["w_ffn_in"] = blocks["w_ffn_in"]
    W["w_in_blocks"] = first[0]
    small_blocks = first[1]
    off = 0
    for n in SMALL_SHARDED:
        r, c = w[n].shape
        blk = small_blocks[:, off:off + r * c // LANES].reshape(N_DEV, r, c)
        W[n] = blk.transpose(1, 0, 2).reshape(r, N_DEV * c)
        off += _tile_rows(r * c)

    scatter_ids = iter((3, 4, 5))

    def scatter(names, parts):
        waves = LATE_SCATTER_WAVES if "w_in" in names else 1
        got = _scatter_blocks_async(parts, name="scatter_" + "_".join(names), collective_id=next(scatter_ids),
                                    waves=waves)
        return dict(zip(names, got))

    def finish(received):
        return {n: _adamw_reduce_call(r, w[n], m[n], v[n], name="adamw_" + n) for n, r in received.items()}

    loss_part, grad_x, grads, big_done = _local_step(xs, target, W, scatter, finish)

    loss_row, small_done = _small_update_call(
        _small_early_rows(grads), grads["norm_pre_mix"], loss_part,
        *[{n: given[prefix + n] for n in SMALL} for prefix in ("", "m_", "v_")], name="small_update")

    def pick(i):
        return [small_done[n][i] if n in SMALL else (big_done[n][i].T if n in COLUMN_SHARDED else big_done[n][i])[None]
                for n in WEIGHTS]

    return (loss_row[0, 0], grad_x[None], *pick(0), *pick(1), *pick(2), *pick(3))
```

```python
import jax
import jax.numpy as jnp
from jax import lax
from jax.experimental import pallas as pl
from jax.experimental.pallas import tpu as pltpu
from jax.experimental.pallas import tpu_sc as plsc

F32 = jnp.float32
BF16 = jnp.bfloat16

D_MODEL = 1024
N_DEV = 8
CHUNK = 64
HEADS = 4
DK = 128
DV = 256
QK = HEADS * DK
GV = HEADS * DV
RANK = 16
GROUPS = 4
SBLOCK = 128
DG = 256
D_FF = 2816
FF_BLK = 704
EPS = 1e-6
Q_SCALE = DK ** -0.5
LANES = 128
VMEM_BIG = 48 * 1024 * 1024

D_IN = 7184
IN_BLK = 898
A_COLS = (0, 3072)
LOW_COLS = (3072, 3088)
S_COLS = (3088, 5136)
G_COLS = (5136, 7184)

ADAM_LR = 0.001
ADAM_B1 = 0.9
ADAM_B2 = 0.999
ADAM_EPS = 1e-08
ADAM_WD = 0.01
ADAM_STEP = 10

MESH = pl.DeviceIdType.MESH
ANY = pl.BlockSpec(memory_space=pl.ANY)
VMEM_SPEC = pl.BlockSpec(memory_space=pltpu.VMEM)


def _cp(sem=None, vmem=None):
    return pltpu.CompilerParams(dimension_semantics=sem, vmem_limit_bytes=vmem)


def _sig(x):
    return 0.5 * jnp.tanh(0.5 * x) + 0.5


GELU_C = 0.7978845608028654
GELU_A = 0.044715


def _gelu(x):
    t = jnp.tanh((GELU_C * x) * (1.0 + GELU_A * (x * x)))
    return (0.5 * x) * (1.0 + t)


def _gelu_and_grad(x):
    x2 = x * x
    t = jnp.tanh((GELU_C * x) * (1.0 + GELU_A * x2))
    one_t = 1.0 + t
    hx = 0.5 * x
    grad = 0.5 * one_t + (hx * (1.0 - t * t)) * (GELU_C + (3.0 * GELU_A * GELU_C) * x2)
    return hx * one_t, grad


def _logsig(x):
    return jnp.minimum(x, 0.0) - jnp.log1p(jnp.exp(-jnp.abs(x)))


def _dot(a, b, dims):
    return lax.dot_general(a, b, (dims, ((), ())), preferred_element_type=F32)


NN = ((1,), (0,))
NT = ((1,), (1,))
TN = ((0,), (0,))


def _exact_mask_dot(mask_bf16, x):
    hi = x.astype(BF16)
    r1 = x - hi.astype(F32)
    mid = r1.astype(BF16)
    lo = (r1 - mid.astype(F32)).astype(BF16)
    return _dot(mask_bf16, hi, NN) + _dot(mask_bf16, mid, NN) + _dot(mask_bf16, lo, NN)


def _pick_tile(dim, target):
    if dim <= target:
        return dim
    best = None
    for t in range(LANES, int(1.4 * target) + 1, LANES):
        if dim % t == 0:
            best = t
    assert best is not None, (dim, target)
    return best


def _mm_call(a, b, *, name, grid, a_spec, b_spec, o_spec, out_shape, dims, acc_shape):
    nk = grid[2]

    def body(a_ref, b_ref, o_ref, *acc):
        part = _dot(a_ref[...], b_ref[...], dims)
        if nk == 1:
            o_ref[...] = part.astype(o_ref.dtype)
        else:
            acc_ref = acc[0]
            k = pl.program_id(2)

            @pl.when(k == 0)
            def _():
                acc_ref[...] = part

            @pl.when(k > 0)
            def _():
                acc_ref[...] += part

            @pl.when(k == nk - 1)
            def _():
                o_ref[...] = acc_ref[...].astype(o_ref.dtype)

    return pl.pallas_call(
        body, name=name, grid=grid, in_specs=[a_spec, b_spec], out_specs=o_spec, out_shape=out_shape,
        scratch_shapes=[] if nk == 1 else [pltpu.VMEM(acc_shape, F32)],
        compiler_params=_cp(("parallel", "parallel", "arbitrary"), VMEM_BIG),
    )(a, b)


def _mm(a, b, mode, out_dtype, *, name, tm=512, tn=1024, tk=1024):
    if mode == "nn":
        (M, K), (_, N) = a.shape, b.shape
    elif mode == "nt":
        (M, K), (N, _) = a.shape, b.shape
    else:
        (K, M), (_, N) = a.shape, b.shape
    tm, tn, tk = _pick_tile(M, tm), _pick_tile(N, tn), _pick_tile(K, tk)
    if mode == "nn":
        a_spec = pl.BlockSpec((tm, tk), lambda i, j, k: (i, k))
        b_spec = pl.BlockSpec((tk, tn), lambda i, j, k: (k, j))
        dims = NN
    elif mode == "nt":
        a_spec = pl.BlockSpec((tm, tk), lambda i, j, k: (i, k))
        b_spec = pl.BlockSpec((tn, tk), lambda i, j, k: (j, k))
        dims = NT
    else:
        a_spec = pl.BlockSpec((tk, tm), lambda i, j, k: (k, i))
        b_spec = pl.BlockSpec((tk, tn), lambda i, j, k: (k, j))
        dims = TN
    return _mm_call(a, b, name=name, grid=(M // tm, N // tn, K // tk), a_spec=a_spec, b_spec=b_spec,
                    o_spec=pl.BlockSpec((tm, tn), lambda i, j, k: (i, j)),
                    out_shape=jax.ShapeDtypeStruct((M, N), out_dtype), dims=dims, acc_shape=(tm, tn))


ROW_TILE = 512
SUB_ROWS = 256


def _sub_tiles(T):
    return [pl.ds(r0, min(SUB_ROWS, T)) for r0 in range(0, T, SUB_ROWS)]


def _rt(T, W, c=0):
    return pl.BlockSpec((T, W), lambda i: (i, c))


def _rt3(nb, T, W):
    return pl.BlockSpec((nb, T, W), lambda i: (0, i, 0))


def _res(shape):
    nd = len(shape)
    return pl.BlockSpec(tuple(shape), lambda i: (0,) * nd, pipeline_mode=pl.Buffered(1))


def _acc(shape):
    nd = len(shape)
    return pl.BlockSpec(tuple(shape), lambda i: (0,) * nd, pipeline_mode=pl.Buffered(1))


def _nbytes(shape, dtype):
    n = jnp.dtype(dtype).itemsize
    for s in shape:
        n *= s
    return n


def _vmem_limit(tiles, resident, temps=16 * 1024 * 1024):
    need = 2 * sum(_nbytes(s, d) for s, d in tiles) + sum(_nbytes(s, d) for s, d in resident) + temps
    return min(need, 60 * 1024 * 1024)


def _tok_call(body, *, name, S, T, ins, outs, semantics="parallel"):
    tiles = [(spec.block_shape, a.dtype) for a, spec, kind in ins + outs if kind == "tile"]
    resident = [(a.shape, a.dtype) for a, spec, kind in ins + outs if kind == "res"]
    return pl.pallas_call(
        body, name=name, grid=(S // T,),
        in_specs=[spec for _, spec, _ in ins], out_specs=[spec for _, spec, _ in outs],
        out_shape=[jax.ShapeDtypeStruct(a.shape, a.dtype) for a, _, _ in outs],
        compiler_params=_cp((semantics,), _vmem_limit(tiles, resident)),
    )(*[a for a, _, _ in ins])


def _tile(a, spec):
    return (a, spec, "tile")


def _whole(a):
    return (a, _res(a.shape), "res")


def _out_tile(shape, dtype, spec):
    return (jax.ShapeDtypeStruct(shape, dtype), spec, "tile")


def _out_acc(shape, dtype=F32):
    return (jax.ShapeDtypeStruct(shape, dtype), _acc(shape), "res")


def _rms_stats(x):
    r = lax.rsqrt(jnp.mean(x * x, axis=-1, keepdims=True) + EPS)
    return r, x * r


def _rms_bwd(xh, r, g, dy):
    dxh = dy * g
    dx = r * (dxh - xh * jnp.mean(dxh * xh, axis=-1, keepdims=True))
    dg = jnp.sum(dy * xh, axis=0, keepdims=True)
    return dx, dg


def _accum(ref, val):
    @pl.when(pl.program_id(0) == 0)
    def _():
        ref[...] = val

    @pl.when(pl.program_id(0) > 0)
    def _():
        ref[...] += val


def _dot_rows_t(a, w_ref, row0, o_ref, chunk=1024):
    n = o_ref.shape[1]
    for n0 in range(0, n, chunk):
        n1 = min(n, n0 + chunk)
        o_ref[:, n0:n1] = _dot(a, w_ref[row0 + n0:row0 + n1, :], NT).astype(o_ref.dtype)


def _in_proj_call(x, g1, w_blocks, *, name, T=ROW_TILE):
    S = x.shape[0]

    def body(x_ref, g_ref, blk_ref, a_ref, pa_ref, al_ref, ps_ref, pg_ref, w_ref):
        @pl.when(pl.program_id(0) == 0)
        def _():
            for d in range(N_DEV):
                w_ref[d * IN_BLK:(d + 1) * IN_BLK, :] = blk_ref[d]

        _, xh = _rms_stats(x_ref[...])
        a = (xh * g_ref[...]).astype(BF16)
        a_ref[...] = a
        _dot_rows_t(a, w_ref, A_COLS[0], pa_ref)
        _dot_rows_t(a, w_ref, LOW_COLS[0], al_ref)
        _dot_rows_t(a, w_ref, S_COLS[0], ps_ref)
        _dot_rows_t(a, w_ref, G_COLS[0], pg_ref)

    widths = (D_MODEL, A_COLS[1] - A_COLS[0], LANES, S_COLS[1] - S_COLS[0], G_COLS[1] - G_COLS[0])
    return _tok_call(
        body, name=name, S=S, T=T, semantics="arbitrary",
        ins=[_tile(x, _rt(T, D_MODEL)), _whole(g1), _whole(w_blocks)],
        outs=[_out_tile((S, w), BF16, _rt(T, w)) for w in widths] + [_out_acc((D_IN, D_MODEL), BF16)])


def _mixer_tail_call(y_gla, y_sgu, pg, x, w_bg, w_bs, w_out, g2, g3, *, name, T=ROW_TILE):
    S = x.shape[0]

    def body(yg_ref, ys_ref, pg_ref, x_ref, wbg_ref, wbs_ref, wo_ref, g2_ref, g3_ref,
             t1_ref, t2_ref, mg_ref, mix_ref, x1_ref, h_ref):
        for rows in _sub_tiles(T):
            t1 = _dot(yg_ref[rows, :], wbg_ref[...], NN)
            t2 = _dot(ys_ref[rows, :], wbs_ref[...], NN)
            t1_ref[rows, :] = t1.astype(BF16)
            t2_ref[rows, :] = t2.astype(BF16)
            sg = _sig(pg_ref[rows, pl.ds(0, D_MODEL)].astype(F32))
            ss = _sig(pg_ref[rows, pl.ds(D_MODEL, D_MODEL)].astype(F32))
            merged = (sg * t1 + ss * t2).astype(BF16)
            mg_ref[rows, :] = merged
            mix = _dot(merged, wo_ref[...], NN)
            mix_ref[rows, :] = mix
            _, mh = _rms_stats(mix)
            x1 = x_ref[rows, :] + mh * g2_ref[...]
            x1_ref[rows, :] = x1
            _, xh = _rms_stats(x1)
            h_ref[rows, :] = (xh * g3_ref[...]).astype(BF16)

    row = _rt(T, D_MODEL)
    b16 = lambda: _out_tile((S, D_MODEL), BF16, row)
    f32 = lambda: _out_tile((S, D_MODEL), F32, row)
    return _tok_call(
        body, name=name, S=S, T=T,
        ins=[_tile(y_gla, row), _tile(y_sgu, row), _tile(pg, _rt(T, 2 * D_MODEL)), _tile(x, row),
             _whole(w_bg), _whole(w_bs), _whole(w_out), _whole(g2), _whole(g3)],
        outs=[b16(), b16(), b16(), f32(), f32(), b16()])


def _ffn_in_call(h, wfi, *, name, T=ROW_TILE):
    S = h.shape[0]

    def body(h_ref, w_ref, gu_ref, z_ref):
        hv = h_ref[...]
        for d in range(4):
            gate = _dot(hv, w_ref[d], NT)
            up = _dot(hv, w_ref[d + 4], NT)
            gu_ref[d] = gate.astype(BF16)
            gu_ref[d + 4] = up.astype(BF16)
            z_ref[d] = (gate * _sig(gate) * up).astype(BF16)

    return _tok_call(
        body, name=name, S=S, T=T,
        ins=[_tile(h, _rt(T, D_MODEL)), _whole(wfi)],
        outs=[_out_tile((N_DEV, S, FF_BLK), BF16, _rt3(N_DEV, T, FF_BLK)),
              _out_tile((4, S, FF_BLK), BF16, _rt3(4, T, FF_BLK))])


def _ffn_out_loss_call(z, wfo, x1, target, g4, *, name, T=ROW_TILE):
    S = x1.shape[0]

    def body(z_ref, w_ref, x1_ref, t_ref, g4_ref, loss_ref, dx2_ref, dy_ref, dg4_ref):
        loss = jnp.zeros((1, 1), F32)
        dg4 = jnp.zeros((1, D_MODEL), F32)
        for rows in _sub_tiles(T):
            y = _dot(z_ref[0, rows, :], w_ref[0], NN)
            for d in range(1, 4):
                y = y + _dot(z_ref[d, rows, :], w_ref[d], NN)
            r, yh = _rms_stats(y)
            diff = x1_ref[rows, :] + yh * g4_ref[...] - t_ref[rows, :]
            loss = loss + 0.5 * jnp.sum(jnp.mean(diff * diff, axis=-1, keepdims=True), axis=0, keepdims=True)
            dx2 = diff * (1.0 / D_MODEL)
            dx2_ref[rows, :] = dx2
            dy, dg = _rms_bwd(yh, r, g4_ref[...], dx2)
            dy_ref[rows, :] = dy.astype(BF16)
            dg4 = dg4 + dg
        _accum(loss_ref, jnp.broadcast_to(loss, (1, LANES)))
        _accum(dg4_ref, dg4)

    row = _rt(T, D_MODEL)
    return _tok_call(
        body, name=name, S=S, T=T, semantics="arbitrary",
        ins=[_tile(z, _rt3(4, T, FF_BLK)), _whole(wfo), _tile(x1, row), _tile(target, row), _whole(g4)],
        outs=[_out_acc((1, LANES)), _out_tile((S, D_MODEL), F32, row), _out_tile((S, D_MODEL), BF16, row),
              _out_acc((1, D_MODEL))])


def _ffn_out_bwd_call(dy, wfo, gu, *, name, T=ROW_TILE):
    S = dy.shape[0]

    def body(dy_ref, w_ref, gu_ref, dgu_ref):
        dyv = dy_ref[...]
        for d in range(4):
            dz = _dot(dyv, w_ref[d], NT).astype(BF16)
            gt = gu_ref[d]
            up = gu_ref[d + 4]
            s = _sig(gt)
            dgu_ref[d] = dz * up * s * (1.0 + gt * (1.0 - s))
            dgu_ref[d + 4] = dz * gt * s

    blocks = _rt3(N_DEV, T, FF_BLK)
    return _tok_call(
        body, name=name, S=S, T=T,
        ins=[_tile(dy, _rt(T, D_MODEL)), _whole(wfo), _tile(gu, blocks)],
        outs=[_out_tile((N_DEV, S, FF_BLK), BF16, blocks)])[0]


def _ffn_in_bwd_call(dgu, wfi, dx2, x1, mix, g3, g2, *, name, T=ROW_TILE):
    S = x1.shape[0]

    def body(dgu_ref, w_ref, dx2_ref, x1_ref, mix_ref, g3_ref, g2_ref, dx1_ref, dmix_ref, dg3_ref, dg2_ref):
        dg3 = jnp.zeros((1, D_MODEL), F32)
        dg2 = jnp.zeros((1, D_MODEL), F32)
        for rows in _sub_tiles(T):
            dh = _dot(dgu_ref[0, rows, :], w_ref[0], NN)
            for d in range(1, N_DEV):
                dh = dh + _dot(dgu_ref[d, rows, :], w_ref[d], NN)
            r3, xh = _rms_stats(x1_ref[rows, :])
            d3, g = _rms_bwd(xh, r3, g3_ref[...], dh)
            dg3 = dg3 + g
            dx1 = dx2_ref[rows, :] + d3
            dx1_ref[rows, :] = dx1
            r2, mh = _rms_stats(mix_ref[rows, :])
            dmix, g = _rms_bwd(mh, r2, g2_ref[...], dx1)
            dg2 = dg2 + g
            dmix_ref[rows, :] = dmix.astype(BF16)
        _accum(dg3_ref, dg3)
        _accum(dg2_ref, dg2)

    row = _rt(T, D_MODEL)
    return _tok_call(
        body, name=name, S=S, T=T, semantics="arbitrary",
        ins=[_tile(dgu, _rt3(N_DEV, T, FF_BLK)), _whole(wfi), _tile(dx2, row), _tile(x1, row), _tile(mix, row),
             _whole(g3), _whole(g2)],
        outs=[_out_tile((S, D_MODEL), F32, row), _out_tile((S, D_MODEL), BF16, row),
              _out_acc((1, D_MODEL)), _out_acc((1, D_MODEL))])


def _mixer_bwd_call(dmix, t1, t2, pg, w_out, w_bg, w_bs, *, name, T=ROW_TILE):
    S = dmix.shape[0]

    def body(dmix_ref, t1_ref, t2_ref, pg_ref, wo_ref, wbg_ref, wbs_ref, dt1_ref, dt2_ref, dpg_ref, dyg_ref, dys_ref):
        gg, gs = pl.ds(0, D_MODEL), pl.ds(D_MODEL, D_MODEL)
        for rows in _sub_tiles(T):
            dm = _dot(dmix_ref[rows, :], wo_ref[...], NT)
            sg = _sig(pg_ref[rows, gg].astype(F32))
            ss = _sig(pg_ref[rows, gs].astype(F32))
            dt1 = (dm * sg).astype(BF16)
            dt2 = (dm * ss).astype(BF16)
            dt1_ref[rows, :] = dt1
            dt2_ref[rows, :] = dt2
            dpg_ref[rows, gg] = (dm * t1_ref[rows, :].astype(F32) * sg * (1.0 - sg)).astype(BF16)
            dpg_ref[rows, gs] = (dm * t2_ref[rows, :].astype(F32) * ss * (1.0 - ss)).astype(BF16)
            dyg_ref[rows, :] = _dot(dt1, wbg_ref[...], NT).astype(BF16)
            dys_ref[rows, :] = _dot(dt2, wbs_ref[...], NT).astype(BF16)

    row = _rt(T, D_MODEL)
    wide = _rt(T, 2 * D_MODEL)
    b16 = lambda: _out_tile((S, D_MODEL), BF16, row)
    return _tok_call(
        body, name=name, S=S, T=T,
        ins=[_tile(dmix, row), _tile(t1, row), _tile(t2, row), _tile(pg, wide), _whole(w_out), _whole(w_bg), _whole(w_bs)],
        outs=[b16(), b16(), _out_tile((S, 2 * D_MODEL), BF16, wide), b16(), b16()])


def _in_proj_bwd_call(dpa, dlow, dps, dpg, w_in_t, x, dx1, g1, *, name, T=ROW_TILE):
    S = x.shape[0]

    def body(dpa_ref, dl_ref, dps_ref, dpg_ref, w_ref, x_ref, dx1_ref, g1_ref, gx_ref, dg1_ref):
        da = (_dot(dpa_ref[...], w_ref[A_COLS[0]:A_COLS[1], :], NN)
              + _dot(dl_ref[...], w_ref[LOW_COLS[0]:LOW_COLS[0] + LANES, :], NN)
              + _dot(dps_ref[...], w_ref[S_COLS[0]:S_COLS[1], :], NN)
              + _dot(dpg_ref[...], w_ref[G_COLS[0]:G_COLS[1], :], NN))
        r, xh = _rms_stats(x_ref[...])
        dxa, dg = _rms_bwd(xh, r, g1_ref[...], da)
        gx_ref[...] = dx1_ref[...] + dxa
        _accum(dg1_ref, dg)

    row = _rt(T, D_MODEL)
    return _tok_call(
        body, name=name, S=S, T=T, semantics="arbitrary",
        ins=[_tile(dpa, _rt(T, dpa.shape[1])), _tile(dlow, _rt(T, dlow.shape[1])), _tile(dps, _rt(T, dps.shape[1])),
             _tile(dpg, _rt(T, dpg.shape[1])), _whole(w_in_t), _tile(x, row), _tile(dx1, row), _whole(g1)],
        outs=[_out_tile((S, D_MODEL), F32, row), _out_acc((1, D_MODEL))])


TOKEN_TILE = 512


def _weight_grads_part(arrays, pairs, *, tk=TOKEN_TILE, out_dtype=BF16):
    S = arrays[0].shape[-2]
    tk = min(tk, S)
    n_in = len(arrays)

    def out_shape(i, j):
        a, b = arrays[i], arrays[j]
        if a.ndim == 3:
            return (a.shape[0], a.shape[2], b.shape[1])
        if b.ndim == 3:
            return (b.shape[0], a.shape[1], b.shape[2])
        return (a.shape[1], b.shape[1])

    shapes = [out_shape(i, j) for i, j in pairs]

    in_place = out_dtype == F32

    def body(ins, outs, accs):
        k = pl.program_id(0)
        if in_place:
            accs = outs

        @pl.when(k == 0)
        def _():
            for acc in accs:
                acc[...] = jnp.zeros_like(acc)

        for (i, j), acc in zip(pairs, accs):
            a_ref, b_ref = ins[i], ins[j]
            if len(a_ref.shape) == 3:
                for n in range(a_ref.shape[0]):
                    acc[n] += _dot(a_ref[n], b_ref[...], TN)
            elif len(b_ref.shape) == 3:
                a = a_ref[...]
                for n in range(b_ref.shape[0]):
                    acc[n] += _dot(a, b_ref[n], TN)
            else:
                b = b_ref[...]
                for m0 in range(0, a_ref.shape[1], 1024):
                    m1 = min(a_ref.shape[1], m0 + 1024)
                    acc[m0:m1, :] += _dot(a_ref[:, m0:m1], b, TN)

        if not in_place:
            @pl.when(k == S // tk - 1)
            def _():
                for out, acc in zip(outs, accs):
                    out[...] = acc[...].astype(out.dtype)

    def in_spec(a):
        if a.ndim == 3:
            return pl.BlockSpec((a.shape[0], tk, a.shape[2]), lambda k: (0, k, 0))
        return pl.BlockSpec((tk, a.shape[1]), lambda k: (k, 0))

    return dict(
        body=body, steps=S // tk, arrays=list(arrays),
        in_specs=[in_spec(a) for a in arrays],
        out_specs=[_acc(s) for s in shapes],
        out_shapes=[jax.ShapeDtypeStruct(s, out_dtype) for s in shapes],
        scratch=[] if in_place else [pltpu.VMEM(s, F32) for s in shapes],
        tiles=[(in_spec(a).block_shape, a.dtype) for a in arrays],
        resident=[(s, F32) for s in shapes] + ([] if in_place else [(s, BF16) for s in shapes]))


def _weight_grads_call(arrays, pairs, *, name, tk=TOKEN_TILE):
    part = _weight_grads_part(arrays, pairs, tk=tk)
    n_in, n_out = len(part["arrays"]), len(part["out_shapes"])

    def body(*refs):
        part["body"](refs[:n_in], refs[n_in:n_in + n_out], refs[n_in + n_out:])

    return pl.pallas_call(
        body, name=name, grid=(part["steps"],),
        in_specs=part["in_specs"], out_specs=part["out_specs"], out_shape=part["out_shapes"],
        scratch_shapes=part["scratch"],
        compiler_params=_cp(("arbitrary",), _vmem_limit(part["tiles"], part["resident"])),
    )(*part["arrays"])


def _with_rider(body, n_in, n_out, n_scratch, rider):
    if rider is None:
        return body
    r_in, r_out = len(rider["arrays"]), len(rider["out_shapes"])

    def both(*refs):
        ins, rest = refs[:n_in + r_in], refs[n_in + r_in:]
        outs, scratch = rest[:n_out + r_out], rest[n_out + r_out:]
        body(*ins[:n_in], *outs[:n_out], *scratch[:n_scratch])
        rider["body"](ins[n_in:], outs[n_out:], scratch[n_scratch:])

    return both


def _rider_lists(rider):
    if rider is None:
        return [], [], [], [], [], [], []
    return (rider["arrays"], rider["in_specs"], rider["out_specs"], rider["out_shapes"], rider["scratch"],
            rider["tiles"], rider["resident"])


def _ffn_in_grad_call(h, dgu, *, name, tk=TOKEN_TILE):
    S = h.shape[0]
    tk = min(tk, S)
    nb = 4
    shape = (nb, FF_BLK, D_MODEL)

    def body(h_ref, dgu_ref, out_ref, acc):
        k = pl.program_id(1)

        @pl.when(k == 0)
        def _():
            acc[...] = jnp.zeros_like(acc)

        hv = h_ref[...]
        for n in range(nb):
            acc[n] += _dot(dgu_ref[n], hv, TN)

        @pl.when(k == S // tk - 1)
        def _():
            out_ref[...] = acc[...].astype(out_ref.dtype)

    tiles = [((tk, D_MODEL), BF16), ((nb, tk, FF_BLK), BF16), (shape, BF16)]
    return pl.pallas_call(
        body, name=name, grid=(N_DEV // nb, S // tk),
        in_specs=[pl.BlockSpec((tk, D_MODEL), lambda g, k: (k, 0)),
                  pl.BlockSpec((nb, tk, FF_BLK), lambda g, k: (g, k, 0))],
        out_specs=pl.BlockSpec(shape, lambda g, k: (g, 0, 0)),
        out_shape=jax.ShapeDtypeStruct((N_DEV, FF_BLK, D_MODEL), BF16),
        scratch_shapes=[pltpu.VMEM(shape, F32)],
        compiler_params=_cp(("parallel", "arbitrary"), _vmem_limit(tiles, [(shape, F32)])),
    )(h, dgu)


GLA_TILE = 256
Q_OFF, K_OFF, V_OFF, R_OFF = 0, QK, 2 * QK, 2 * QK + GV


def _tri(lower):
    r = lax.broadcasted_iota(jnp.int32, (CHUNK, CHUNK), 0)
    c = lax.broadcasted_iota(jnp.int32, (CHUNK, CHUNK), 1)
    return jnp.where((r >= c) if lower else (c >= r), 1.0, 0.0).astype(BF16)


def _gla_fwd_call(pa, alow, wg, bgate, gnorm, *, name):
    S = pa.shape[0]
    Tg = min(GLA_TILE, S)
    cb = Tg // CHUNK

    def body(pa_ref, al_ref, wg_ref, bg_ref, gn_ref, y_ref, st_ref, state):
        @pl.when(pl.program_id(0) == 0)
        def _():
            state[...] = jnp.zeros_like(state)

        logit = _dot(al_ref[...], wg_ref[...], NN) + bg_ref[...]
        ls = _logsig(logit) * (1.0 / 16.0)
        tri = _tri(True)
        for c in range(cb):
            rows = pl.ds(c * CHUNK, CHUNK)
            cum = _exact_mask_dot(tri, ls[c * CHUNK:(c + 1) * CHUNK])
            tot = cum[CHUNK - 1:CHUNK]
            kd = (pa_ref[rows, pl.ds(K_OFF, QK)].astype(F32) * jnp.exp(tot - cum)).astype(BF16)
            decay = jnp.exp(tot)
            for h in range(HEADS):
                lanes = slice(h * DK, (h + 1) * DK)
                new = state[h] * decay[:, lanes] + _dot(pa_ref[rows, pl.ds(V_OFF + h * DV, DV)], kd[:, lanes], TN)
                state[h] = new
                st_ref[c, h] = new
        for c in range(cb):
            rows = pl.ds(c * CHUNK, CHUNK)
            for h in range(HEADS):
                qs = (pa_ref[rows, pl.ds(Q_OFF + h * DK, DK)].astype(F32) * Q_SCALE).astype(BF16)
                o = _dot(qs, st_ref[c, h].astype(BF16), NT)
                rs = lax.rsqrt(jnp.mean(o * o, axis=-1, keepdims=True) + EPS)
                rr = pa_ref[rows, pl.ds(R_OFF + h * DV, DV)].astype(F32)
                y_ref[rows, pl.ds(h * DV, DV)] = (o * rs * gn_ref[h] * (rr * _sig(rr))).astype(BF16)

    return pl.pallas_call(
        body, name=name, grid=(S // Tg,),
        in_specs=[
            pl.BlockSpec((Tg, 2 * QK + 2 * GV), lambda t: (t, 0)),
            pl.BlockSpec((Tg, LANES), lambda t: (t, 0)),
            pl.BlockSpec((LANES, QK), lambda t: (0, 0)),
            pl.BlockSpec((1, QK), lambda t: (0, 0)),
            pl.BlockSpec((HEADS, 1, DV), lambda t: (0, 0, 0)),
        ],
        out_specs=[
            pl.BlockSpec((Tg, GV), lambda t: (t, 0)),
            pl.BlockSpec((cb, HEADS, DV, DK), lambda t: (t, 0, 0, 0)),
        ],
        out_shape=[jax.ShapeDtypeStruct((S, GV), BF16),
                   jax.ShapeDtypeStruct((S // CHUNK, HEADS, DV, DK), F32)],
        scratch_shapes=[pltpu.VMEM((HEADS, DV, DK), F32)],
        compiler_params=_cp(("arbitrary",), VMEM_BIG),
    )(pa, alow, wg, bgate, gnorm)


def _gla_bwd_call(pa, alow, wg, bgate, gnorm, states, dy, *, name, rider=None):
    S = pa.shape[0]
    Tg = min(GLA_TILE, S)
    cb = Tg // CHUNK
    nt = S // Tg
    r_arrays, r_in, r_out, r_shapes, r_scratch, r_tiles, r_resident = _rider_lists(rider)
    assert rider is None or rider["steps"] == nt

    def rev(t):
        return nt - 1 - t

    def body(pa_ref, al_ref, wg_ref, bg_ref, gn_ref, st_ref, prev_ref, dy_ref,
             dpa_ref, dl_ref, dgn_ref, dbg_ref, carry, pbuf):
        t = pl.program_id(0)

        @pl.when(t == 0)
        def _():
            carry[...] = jnp.zeros_like(carry)
            dgn_ref[...] = jnp.zeros_like(dgn_ref)
            dbg_ref[...] = jnp.zeros_like(dbg_ref)

        logit = _dot(al_ref[...], wg_ref[...], NN) + bg_ref[...]
        ls = _logsig(logit) * (1.0 / 16.0)
        sneg = 1.0 / (1.0 + jnp.exp(logit))
        tri = _tri(True)
        upper = _tri(False)
        first_tile = rev(t) == 0
        heads = range(HEADS)

        w, decay, kd = [], [], []
        for c in range(cb):
            rows = pl.ds(c * CHUNK, CHUNK)
            cum = _exact_mask_dot(tri, ls[c * CHUNK:(c + 1) * CHUNK])
            tot = cum[CHUNK - 1:CHUNK]
            w.append(jnp.exp(tot - cum))
            decay.append(jnp.exp(tot))
            kd.append(pa_ref[rows, pl.ds(K_OFF, QK)].astype(F32) * w[c])

        dgn = [jnp.zeros((1, DV), F32) for _ in heads]
        for c in range(cb):
            rows = pl.ds(c * CHUNK, CHUNK)
            for h in heads:
                gn = gn_ref[h]
                qs = (pa_ref[rows, pl.ds(Q_OFF + h * DK, DK)].astype(F32) * Q_SCALE).astype(BF16)
                st16 = st_ref[c, h].astype(BF16)
                o = _dot(qs, st16, NT)
                rs = lax.rsqrt(jnp.mean(o * o, axis=-1, keepdims=True) + EPS)
                oh = o * rs
                rr = pa_ref[rows, pl.ds(R_OFF + h * DV, DV)].astype(F32)
                sr = _sig(rr)
                dyv = dy_ref[rows, pl.ds(h * DV, DV)].astype(F32)
                dpa_ref[rows, pl.ds(R_OFF + h * DV, DV)] = (
                    dyv * oh * gn * sr * (1.0 + rr * (1.0 - sr))).astype(BF16)
                don = dyv * (rr * sr)
                dgn[h] = dgn[h] + jnp.sum(don * oh, axis=0, keepdims=True)
                doh = don * gn
                do16 = (rs * (doh - oh * jnp.mean(doh * oh, axis=-1, keepdims=True))).astype(BF16)
                dpa_ref[rows, pl.ds(Q_OFF + h * DK, DK)] = (_dot(do16, st16, NN) * Q_SCALE).astype(BF16)
                pbuf[c, h] = _dot(do16, qs, TN)
        for h in heads:
            dgn_ref[h] += dgn[h]

        dkd = [[None] * HEADS for _ in range(cb)]
        ddecay = [[None] * HEADS for _ in range(cb)]
        for c in reversed(range(cb)):
            rows = pl.ds(c * CHUNK, CHUNK)
            for h in heads:
                lanes = slice(h * DK, (h + 1) * DK)
                gt = pbuf[c, h] + carry[h]
                gt16 = gt.astype(BF16)
                dkd[c][h] = _dot(pa_ref[rows, pl.ds(V_OFF + h * DV, DV)], gt16, NN)
                dpa_ref[rows, pl.ds(V_OFF + h * DV, DV)] = _dot(kd[c][:, lanes].astype(BF16), gt16, NT).astype(BF16)
                if c > 0:
                    st_prev = st_ref[c - 1, h]
                else:
                    st_prev = jnp.where(first_tile, 0.0, prev_ref[0, h])
                ddecay[c][h] = jnp.sum(gt * st_prev, axis=0, keepdims=True)
                carry[h] = gt * decay[c][:, lanes]

        dbg = jnp.zeros((1, QK), F32)
        for c in range(cb):
            rows = pl.ds(c * CHUNK, CHUNK)
            dkd_c = jnp.concatenate(dkd[c], axis=1)
            dpa_ref[rows, pl.ds(K_OFF, QK)] = (dkd_c * w[c]).astype(BF16)
            e = dkd_c * kd[c]
            dtot = jnp.sum(e, axis=0, keepdims=True) + jnp.concatenate(ddecay[c], axis=1) * decay[c]
            dls = dtot - _exact_mask_dot(upper, e)
            dlogit = dls * (1.0 / 16.0) * sneg[c * CHUNK:(c + 1) * CHUNK]
            dl_ref[rows, :] = dlogit.astype(BF16)
            dbg = dbg + jnp.sum(dlogit, axis=0, keepdims=True)
        dbg_ref[...] += dbg

    wide = 2 * QK + 2 * GV
    tiles = [((Tg, wide), BF16), ((cb + 1, HEADS, DV, DK), F32), ((Tg, GV), BF16), ((Tg, wide), BF16),
             ((Tg, QK), BF16)] + r_tiles
    resident = [((cb + 1, HEADS, DV, DK), F32)] + r_resident
    return pl.pallas_call(
        _with_rider(body, 8, 4, 2, rider), name=name, grid=(nt,),
        in_specs=[
            pl.BlockSpec((Tg, wide), lambda t: (rev(t), 0)),
            pl.BlockSpec((Tg, LANES), lambda t: (rev(t), 0)),
            pl.BlockSpec((LANES, QK), lambda t: (0, 0)),
            pl.BlockSpec((1, QK), lambda t: (0, 0)),
            pl.BlockSpec((HEADS, 1, DV), lambda t: (0, 0, 0)),
            pl.BlockSpec((cb, HEADS, DV, DK), lambda t: (rev(t), 0, 0, 0)),
            pl.BlockSpec((1, HEADS, DV, DK), lambda t: (jnp.maximum(rev(t) * cb - 1, 0), 0, 0, 0)),
            pl.BlockSpec((Tg, GV), lambda t: (rev(t), 0)),
        ] + r_in,
        out_specs=[
            pl.BlockSpec((Tg, wide), lambda t: (rev(t), 0)),
            pl.BlockSpec((Tg, QK), lambda t: (rev(t), 0)),
            pl.BlockSpec((HEADS, 1, DV), lambda t: (0, 0, 0)),
            pl.BlockSpec((1, QK), lambda t: (0, 0)),
        ] + r_out,
        out_shape=[jax.ShapeDtypeStruct((S, wide), BF16), jax.ShapeDtypeStruct((S, QK), BF16),
                   jax.ShapeDtypeStruct((HEADS, 1, DV), F32), jax.ShapeDtypeStruct((1, QK), F32)] + r_shapes,
        scratch_shapes=[pltpu.VMEM((HEADS, DV, DK), F32), pltpu.VMEM((cb, HEADS, DV, DK), F32)] + r_scratch,
        compiler_params=_cp(("arbitrary",), _vmem_limit(tiles, resident)),
    )(pa, alow, wg, bgate, gnorm, states, states, dy, *r_arrays)


SGU_TILE = 256


def _sgu_mask():
    r = lax.broadcasted_iota(jnp.int32, (SBLOCK, SBLOCK), 0)
    c = lax.broadcasted_iota(jnp.int32, (SBLOCK, SBLOCK), 1)
    return (c < CHUNK) | (r >= CHUNK)


def _ln_stats(vf):
    mu = jnp.mean(vf, axis=-1, keepdims=True)
    xc = vf - mu
    rs = lax.rsqrt(jnp.mean(xc * xc, axis=-1, keepdims=True) + EPS)
    return rs, xc * rs


def _sgu_fwd_call(ps, ln_g, ln_b, w_sp, b_sp, *, name):
    S = ps.shape[0]
    Ts = min(SGU_TILE, S)

    def body(ps_ref, lg_ref, lb_ref, w_ref, b_ref, y_ref):
        mask = _sgu_mask()
        for g in range(GROUPS):
            wm = jnp.where(mask, w_ref[g], 0.0).astype(BF16)
            for p in range(Ts // SBLOCK):
                rows = pl.ds(p * SBLOCK, SBLOCK)
                u = _gelu(ps_ref[rows, pl.ds(g * DG, DG)].astype(F32))
                _, xh = _ln_stats(_gelu(ps_ref[rows, pl.ds(D_MODEL + g * DG, DG)].astype(F32)))
                vn = xh * lg_ref[g] + lb_ref[g]
                mixed = _dot(wm, vn.astype(BF16), NN) + b_ref[g]
                y_ref[rows, pl.ds(g * DG, DG)] = (u * mixed).astype(BF16)

    full3 = lambda a, b, c: pl.BlockSpec((a, b, c), lambda t: (0, 0, 0))
    return pl.pallas_call(
        body, name=name, grid=(S // Ts,),
        in_specs=[pl.BlockSpec((Ts, 2 * D_MODEL), lambda t: (t, 0)),
                  full3(GROUPS, 1, DG), full3(GROUPS, 1, DG), full3(GROUPS, SBLOCK, SBLOCK), full3(GROUPS, SBLOCK, 1)],
        out_specs=pl.BlockSpec((Ts, D_MODEL), lambda t: (t, 0)),
        out_shape=jax.ShapeDtypeStruct((S, D_MODEL), BF16),
        compiler_params=_cp(("parallel",)),
    )(ps, ln_g, ln_b, w_sp, b_sp)


def _sgu_bwd_call(ps, ln_g, ln_b, w_sp, b_sp, dy, *, name, rider=None):
    S = ps.shape[0]
    Ts = min(SGU_TILE, S)
    r_arrays, r_in, r_out, r_shapes, r_scratch, r_tiles, r_resident = _rider_lists(rider)
    assert rider is None or rider["steps"] == S // Ts

    def body(ps_ref, lg_ref, lb_ref, w_ref, b_ref, dy_ref, ds_ref, dlg_ref, dlb_ref, dw_ref, db_ref):
        @pl.when(pl.program_id(0) == 0)
        def _():
            dlg_ref[...] = jnp.zeros_like(dlg_ref)
            dlb_ref[...] = jnp.zeros_like(dlb_ref)
            dw_ref[...] = jnp.zeros_like(dw_ref)
            db_ref[...] = jnp.zeros_like(db_ref)

        mask = _sgu_mask()
        for g in range(GROUPS):
            wm = jnp.where(mask, w_ref[g], 0.0).astype(BF16)
            lg = lg_ref[g]
            for p in range(Ts // SBLOCK):
                rows = pl.ds(p * SBLOCK, SBLOCK)
                su = ps_ref[rows, pl.ds(g * DG, DG)].astype(F32)
                sv = ps_ref[rows, pl.ds(D_MODEL + g * DG, DG)].astype(F32)
                u, du = _gelu_and_grad(su)
                gv, dgv = _gelu_and_grad(sv)
                rs, xh = _ln_stats(gv)
                vn16 = (xh * lg + lb_ref[g]).astype(BF16)
                mixed = _dot(wm, vn16, NN) + b_ref[g]
                dyv = dy_ref[rows, pl.ds(g * DG, DG)].astype(F32)
                ds_ref[rows, pl.ds(g * DG, DG)] = (dyv * mixed * du).astype(BF16)
                dmix = dyv * u
                dmix16 = dmix.astype(BF16)
                db_ref[g] += jnp.sum(dmix, axis=-1, keepdims=True)
                dw_ref[g] += jnp.where(mask, _dot(dmix16, vn16, NT), 0.0)
                dvn = _dot(wm, dmix16, TN)
                dlg_ref[g] += jnp.sum(dvn * xh, axis=0, keepdims=True)
                dlb_ref[g] += jnp.sum(dvn, axis=0, keepdims=True)
                dxh = dvn * lg
                dvf = rs * (dxh - jnp.mean(dxh, axis=-1, keepdims=True)
                            - xh * jnp.mean(dxh * xh, axis=-1, keepdims=True))
                ds_ref[rows, pl.ds(D_MODEL + g * DG, DG)] = (dvf * dgv).astype(BF16)

    full3 = lambda a, b, c: pl.BlockSpec((a, b, c), lambda t: (0, 0, 0))
    tiles = [((Ts, 2 * D_MODEL), BF16), ((Ts, D_MODEL), BF16), ((Ts, 2 * D_MODEL), BF16)] + r_tiles
    return pl.pallas_call(
        _with_rider(body, 6, 5, 0, rider), name=name, grid=(S // Ts,),
        in_specs=[pl.BlockSpec((Ts, 2 * D_MODEL), lambda t: (t, 0)),
                  full3(GROUPS, 1, DG), full3(GROUPS, 1, DG), full3(GROUPS, SBLOCK, SBLOCK), full3(GROUPS, SBLOCK, 1),
                  pl.BlockSpec((Ts, D_MODEL), lambda t: (t, 0))] + r_in,
        out_specs=[pl.BlockSpec((Ts, 2 * D_MODEL), lambda t: (t, 0)),
                   full3(GROUPS, 1, DG), full3(GROUPS, 1, DG), full3(GROUPS, SBLOCK, SBLOCK), full3(GROUPS, SBLOCK, 1)]
        + r_out,
        out_shape=[jax.ShapeDtypeStruct((S, 2 * D_MODEL), BF16),
                   jax.ShapeDtypeStruct((GROUPS, 1, DG), F32), jax.ShapeDtypeStruct((GROUPS, 1, DG), F32),
                   jax.ShapeDtypeStruct((GROUPS, SBLOCK, SBLOCK), F32),
                   jax.ShapeDtypeStruct((GROUPS, SBLOCK, 1), F32)] + r_shapes,
        scratch_shapes=r_scratch,
        compiler_params=_cp(("arbitrary",), _vmem_limit(tiles, r_resident)),
    )(ps, ln_g, ln_b, w_sp, b_sp, dy, *r_arrays)


def _position():
    return lax.axis_index("x"), lax.axis_index("y"), lax.axis_index("c")


def _gather_copies(srcs, dsts, send_sems, recv_sems, local_sems, relay=False):
    if relay:
        return _gather_copies_relayed(srcs, dsts, send_sems, recv_sems, local_sems)
    x, y, c = _position()
    me, sibling = (x, y, c), (x, y, 1 - c)
    chips = [(1 - x, y), (x, 1 - y), (1 - x, 1 - y)]
    n = len(srcs)

    def slab(a, block):
        px, py, pc = block
        return dsts[a].at[4 * px + 2 * py + pc]

    def copy(a, k, block, to, src=None):
        return pltpu.make_async_remote_copy(
            src_ref=slab(a, block) if src is None else src, dst_ref=slab(a, block),
            send_sem=send_sems.at[7 * a + k], recv_sem=recv_sems.at[7 * a + k], device_id=to, device_id_type=MESH)

    mine = [pltpu.make_async_copy(srcs[a], slab(a, me), local_sems.at[a]) for a in range(n)]
    for cp in mine:
        cp.start()
    first = []
    for a in range(n):
        first.append(copy(a, 0, me, sibling, src=srcs[a]))
        first += [copy(a, 1 + j, me, (*chip, c), src=srcs[a]) for j, chip in enumerate(chips)]
    for cp in first:
        cp.start()
    passed = []
    for j, chip in enumerate(chips):
        for a in range(n):
            copy(a, 1 + j, (*chip, c), me).wait_recv()
            cp = copy(a, 4 + j, (*chip, c), sibling)
            cp.start()
            passed.append(cp)
    for a in range(n):
        copy(a, 0, sibling, me).wait_recv()
        for j, chip in enumerate(chips):
            copy(a, 4 + j, (*chip, 1 - c), me).wait_recv()
    for cp in first + passed:
        cp.wait_send()
    for cp in mine:
        cp.wait()


def _gather_copies_relayed(srcs, dsts, send_sems, recv_sems, local_sems):
    x, y, c = _position()
    me, sibling = (x, y, c), (x, y, 1 - c)
    x_chip, y_chip, d_chip = (1 - x, y), (x, 1 - y), (1 - x, 1 - y)
    south = c == 0
    relay_from = (jnp.where(south, x, 1 - x), jnp.where(south, 1 - y, y), c)
    relay_to = (jnp.where(south, 1 - x, x), jnp.where(south, y, 1 - y), c)
    n = len(srcs)

    def slab(a, block):
        px, py, pc = block
        return dsts[a].at[4 * px + 2 * py + pc]

    def copy(a, k, block, to, src=None):
        return pltpu.make_async_remote_copy(
            src_ref=slab(a, block) if src is None else src, dst_ref=slab(a, block),
            send_sem=send_sems.at[7 * a + k], recv_sem=recv_sems.at[7 * a + k], device_id=to, device_id_type=MESH)

    mine = [pltpu.make_async_copy(srcs[a], slab(a, me), local_sems.at[a]) for a in range(n)]
    for cp in mine:
        cp.start()
    sent = []
    for a in range(n):
        sent += [copy(a, 0, me, sibling, src=srcs[a]), copy(a, 1, me, (*x_chip, c), src=srcs[a]),
                 copy(a, 2, me, (*y_chip, c), src=srcs[a])]
    for cp in sent:
        cp.start()
    for a in range(n):
        copy(a, 1, (*x_chip, c), me).wait_recv()
        copy(a, 2, (*y_chip, c), me).wait_recv()
        later = [copy(a, 3, relay_from, relay_to), copy(a, 4, (*x_chip, c), sibling), copy(a, 5, (*y_chip, c), sibling)]
        for cp in later:
            cp.start()
        sent += later
    for a in range(n):
        copy(a, 3, (*d_chip, c), me).wait_recv()
        cp = copy(a, 6, (*d_chip, c), sibling)
        cp.start()
        sent.append(cp)
    for a in range(n):
        copy(a, 0, sibling, me).wait_recv()
        for k, chip in ((4, x_chip), (5, y_chip), (6, d_chip)):
            copy(a, k, (*chip, 1 - c), me).wait_recv()
    for cp in sent:
        cp.wait_send()
    for cp in mine:
        cp.wait()


def _all_gather_hbm(shards, *, name):
    n = len(shards)

    def body(*refs):
        srcs, dsts = refs[:n], refs[n:2 * n]
        send_sems, recv_sems, local_sems = refs[2 * n:]
        _gather_copies(srcs, dsts, send_sems, recv_sems, local_sems, relay=True)

    return pl.pallas_call(
        body, name=name,
        in_specs=[ANY] * n, out_specs=[ANY] * n,
        out_shape=[jax.ShapeDtypeStruct((N_DEV, *s.shape), s.dtype) for s in shards],
        scratch_shapes=_comm_sems(n),
    )(*shards)


def _all_reduce_small(part, *, name):
    R, W = part.shape

    def body(x_ref, out_ref, gathered, send_sems, recv_sems, local_sems):
        _gather_copies([x_ref], [gathered], send_sems, recv_sems, local_sems)
        acc = gathered[0]
        for d in range(1, N_DEV):
            acc = acc + gathered[d]
        out_ref[...] = acc

    return pl.pallas_call(
        body, name=name,
        in_specs=[VMEM_SPEC], out_specs=VMEM_SPEC,
        out_shape=jax.ShapeDtypeStruct((R, W), F32),
        scratch_shapes=[pltpu.VMEM((N_DEV, R, W), F32),
                        pltpu.SemaphoreType.DMA((7,)), pltpu.SemaphoreType.DMA((7,)), pltpu.SemaphoreType.DMA((1,))],
    )(part)


FLIPS = [(fx, fy, fc) for fx in (0, 1) for fy in (0, 1) for fc in (0, 1)][1:]


def _scatter_copies(srcs, dsts, send_sems, recv_sems, local_sems, waves=1):
    n = len(srcs)
    x, y, c = _position()
    me = 4 * x + 2 * y + c
    mine = [pltpu.make_async_copy(srcs[a].at[me], dsts[a].at[me], local_sems.at[a]) for a in range(n)]
    for cp in mine:
        cp.start()
    peers = []
    for fx, fy, fc in FLIPS:
        tx = 1 - x if fx else x
        ty = 1 - y if fy else y
        tc = 1 - c if fc else c
        peers.append(((tx, ty, tc), 4 * tx + 2 * ty + tc))
    copies = []
    for w in range(waves):
        wave = []
        for k, (peer_id, peer) in enumerate(peers):
            for a in range(n):
                rows = srcs[a].shape[1]
                step = rows if waves == 1 else (rows // waves) // 16 * 16
                r0 = w * step
                cut = pl.ds(r0, step if w < waves - 1 else rows - r0)
                sem = (7 * a + k) * waves + w
                cp = pltpu.make_async_remote_copy(
                    src_ref=srcs[a].at[peer, cut], dst_ref=dsts[a].at[me, cut],
                    send_sem=send_sems.at[sem], recv_sem=recv_sems.at[sem],
                    device_id=peer_id, device_id_type=MESH)
                cp.start()
                wave.append(cp)
        for cp in wave:
            cp.wait_send()
        copies += wave
    for cp in copies:
        cp.wait_recv()
    for cp in mine:
        cp.wait()


def _pair_copies(srcs, dsts, send_sems, recv_sems, local_sems):
    x, y, c = _position()
    copies = [pltpu.make_async_remote_copy(
        src_ref=srcs[a], dst_ref=dsts[a], send_sem=send_sems.at[a], recv_sem=recv_sems.at[a],
        device_id=(x, y, 1 - c), device_id_type=MESH) for a in range(len(srcs))]
    for cp in copies:
        cp.start()
    for cp in copies:
        cp.wait()


def _chip_scatter_copies(srcs, dsts, send_sems, recv_sems, local_sems, waves=1):
    n = len(srcs)
    x, y, c = _position()
    here = 2 * x + y
    mine = [pltpu.make_async_copy(srcs[a].at[here], dsts[a].at[here], local_sems.at[a]) for a in range(n)]
    for cp in mine:
        cp.start()
    peers = []
    for fx, fy in ((1, 0), (0, 1), (1, 1)):
        tx = 1 - x if fx else x
        ty = 1 - y if fy else y
        peers.append(((tx, ty, c), 2 * tx + ty))
    copies = []
    for w in range(waves):
        wave = []
        for k, (peer_id, peer) in enumerate(peers):
            for a in range(n):
                rows = srcs[a].shape[1]
                step = rows if waves == 1 else (rows // waves) // 16 * 16
                r0 = w * step
                cut = pl.ds(r0, step if w < waves - 1 else rows - r0)
                sem = (3 * a + k) * waves + w
                cp = pltpu.make_async_remote_copy(
                    src_ref=srcs[a].at[peer, cut], dst_ref=dsts[a].at[here, cut],
                    send_sem=send_sems.at[sem], recv_sem=recv_sems.at[sem],
                    device_id=peer_id, device_id_type=MESH)
                cp.start()
                wave.append(cp)
        for cp in wave:
            cp.wait_send()
        copies += wave
    for cp in copies:
        cp.wait_recv()
    for cp in mine:
        cp.wait()


def _comm_sems(n, waves=1):
    return [pltpu.SemaphoreType.DMA((7 * n * waves,)), pltpu.SemaphoreType.DMA((7 * n * waves,)),
            pltpu.SemaphoreType.DMA((n,))]


def _scatter_blocks(parts, *, name, waves=1):
    n = len(parts)

    def body(*refs):
        _scatter_copies(refs[:n], refs[n:2 * n], *refs[2 * n:], waves=waves)

    return pl.pallas_call(
        body, name=name,
        in_specs=[ANY] * n, out_specs=[ANY] * n,
        out_shape=[jax.ShapeDtypeStruct(p.shape, p.dtype) for p in parts],
        scratch_shapes=_comm_sems(n, waves),
    )(*parts)


def _handshake(peers):
    barrier = pltpu.get_barrier_semaphore()
    for peer in peers:
        pl.semaphore_signal(barrier, inc=1, device_id=peer, device_id_type=MESH)
    pl.semaphore_wait(barrier, len(peers))


def _sequencer_call(arrays, out_types, copies_fn, peers_fn, *, name, collective_id, waves=1):
    n = len(arrays)
    srcs = [jax.new_ref(a, memory_space=pltpu.MemorySpace.HBM) for a in arrays]
    dsts = [jax.empty_ref(t, memory_space=pltpu.MemorySpace.HBM) for t in out_types]
    extra = {} if waves == 1 else {"waves": waves}

    @pl.kernel(mesh=plsc.ScalarSubcoreMesh(axis_name="sequencer", num_cores=1), name=name,
               scratch_types=_comm_sems(n, waves), compiler_params=pltpu.CompilerParams(collective_id=collective_id))
    def launch(send_sems, recv_sems, local_sems):
        _handshake(peers_fn())
        copies_fn(srcs, dsts, send_sems, recv_sems, local_sems, **extra)

    launch()
    return [d[...] for d in dsts]


def _all_other_devices():
    x, y, c = _position()
    return [(1 - x if fx else x, 1 - y if fy else y, 1 - c if fc else c) for fx, fy, fc in FLIPS]


def _gather_relay_peers():
    x, y, c = _position()
    return [(x, y, 1 - c), (1 - x, y, c), (x, 1 - y, c)]


def _sibling():
    x, y, c = _position()
    return [(x, y, 1 - c)]


def _same_core_of_other_chips():
    x, y, c = _position()
    return [(1 - x, y, c), (x, 1 - y, c), (1 - x, 1 - y, c)]


def _pair_exchange_async(parts, *, name, collective_id):
    return _sequencer_call(parts, [jax.ShapeDtypeStruct(p.shape, p.dtype) for p in parts],
                           _pair_copies, _sibling, name=name, collective_id=collective_id)


def _chip_scatter_async(parts, *, name, collective_id, waves=1):
    return _sequencer_call(parts, [jax.ShapeDtypeStruct(p.shape, p.dtype) for p in parts],
                           _chip_scatter_copies, _same_core_of_other_chips, name=name, collective_id=collective_id,
                           waves=waves)


def _pair_sum_call(mine, theirs, *, name, tw=2 * LANES):
    n, r, w = mine.shape

    def body(a_ref, b_ref, o_ref):
        o_ref[...] = (a_ref[...].astype(F32) + b_ref[...].astype(F32)).astype(o_ref.dtype)

    spec = pl.BlockSpec((None, r, tw), lambda i, j: (i, 0, j))
    return pl.pallas_call(
        body, name=name, grid=(n, w // tw), in_specs=[spec, spec], out_specs=spec,
        out_shape=jax.ShapeDtypeStruct(mine.shape, mine.dtype),
        compiler_params=_cp(("parallel", "parallel")),
    )(mine, theirs)


def _scatter_blocks_async(parts, *, name, collective_id, waves=1):
    return _sequencer_call(parts, [jax.ShapeDtypeStruct(p.shape, p.dtype) for p in parts],
                           _scatter_copies, _all_other_devices, name=name, collective_id=collective_id, waves=waves)


def _all_gather_async(shards, *, name, collective_id):
    return _sequencer_call(shards, [jax.ShapeDtypeStruct((N_DEV, *s.shape), s.dtype) for s in shards],
                           _gather_copies_relayed, _gather_relay_peers, name=name, collective_id=collective_id)


def _adamw_math(w, g, m, v):
    m = ADAM_B1 * m + (1.0 - ADAM_B1) * g
    v = ADAM_B2 * v + (1.0 - ADAM_B2) * (g * g)
    m_hat = m / (1.0 - ADAM_B1 ** ADAM_STEP)
    v_hat = v / (1.0 - ADAM_B2 ** ADAM_STEP)
    delta = -ADAM_LR * (m_hat / (jnp.sqrt(v_hat) + ADAM_EPS) + ADAM_WD * w)
    return delta, m, v


def _adamw_reduce_call(recv, w, m, v, *, name, T=128):
    R, W = w.shape
    n_parts = recv.shape[0]
    if R % T == 0:
        tr, tw = T, W
    elif (R // 2) % 16 == 0:
        tr, tw = R // 2, W
    else:
        tr, tw = R, 2 * LANES

    def body(p_ref, w_ref, m_ref, v_ref, g_out, d_out, m_out, v_out):
        g = p_ref[0].astype(F32)
        for d in range(1, n_parts):
            g = g + p_ref[d].astype(F32)
        g_out[...] = g
        d_out[...], m_out[...], v_out[...] = _adamw_math(w_ref[...], g, m_ref[...], v_ref[...])

    row = pl.BlockSpec((tr, tw), lambda i, j: (i, j))
    out = jax.ShapeDtypeStruct((R, W), F32)
    return pl.pallas_call(
        body, name=name, grid=(R // tr, W // tw),
        in_specs=[pl.BlockSpec((n_parts, tr, tw), lambda i, j: (0, i, j)), row, row, row],
        out_specs=[row] * 4, out_shape=[out] * 4,
        compiler_params=_cp(("parallel", "parallel"), VMEM_BIG),
    )(recv, w, m, v)


SMALL_EARLY = (("b_gate", 8), ("w_spatial", 512), ("b_spatial", 8), ("norm_post_mix", 8), ("norm_pre_ffn", 8),
               ("norm_post_ffn", 8), ("w_gate_up", 128), ("gla_norm", 64), ("sgu_ln_g", 64), ("sgu_ln_b", 64))
SMALL_EARLY_AT = {}
for _name, _rows in SMALL_EARLY:
    SMALL_EARLY_AT[_name] = sum(r for _, r in SMALL_EARLY[:len(SMALL_EARLY_AT)])
SMALL_EARLY_ROWS = sum(r for _, r in SMALL_EARLY)
SMALL_LATE_ROWS = 16


def _small_early_rows(grads):
    def rows(a, n_rows):
        a = a.reshape(-1, LANES)
        return jnp.pad(a, ((0, n_rows - a.shape[0]), (0, 0)))

    def device_major(a, n_rows):
        r, c = a.shape[0], a.shape[1] // N_DEV
        a = a.reshape(r, N_DEV, c).transpose(1, 0, 2)
        a = jnp.pad(a, ((0, 0), (0, n_rows // N_DEV - r), (0, LANES - c)))
        return a.reshape(n_rows, LANES)

    pieces = []
    for name, n_rows in SMALL_EARLY:
        g = grads[name]
        if name in SMALL_SHARDED:
            pieces.append(device_major(g.reshape(g.shape[0], -1) if g.ndim == 2 else g.reshape(g.shape[0], g.shape[-1]), n_rows))
        else:
            pieces.append(rows(g, n_rows))
    return jnp.concatenate(pieces, axis=0)


def _small_update_call(early, dg_pre_mix, loss_part, w, m, v, *, name):
    names = list(SMALL)
    n_p = len(names)
    E = SMALL_EARLY_ROWS

    def reduce_body(early_ref, dg1_ref, loss_ref, tot, got_early, got_late, late, send_sems, recv_sems, local_sems):
        for k in range(D_MODEL // LANES):
            late[k:k + 1, :] = dg1_ref[:, k * LANES:(k + 1) * LANES]
        late[8:16, :] = jnp.broadcast_to(loss_ref[...], (8, LANES))
        _gather_copies([early_ref, late], [got_early, got_late], send_sems, recv_sems, local_sems)
        acc, acc_late = got_early[0], got_late[0]
        for d in range(1, N_DEV):
            acc, acc_late = acc + got_early[d], acc_late + got_late[d]
        tot[0:E, :] = acc
        tot[E:E + SMALL_LATE_ROWS, :] = acc_late

    total = pl.pallas_call(
        reduce_body, name=name + "_reduce",
        in_specs=[VMEM_SPEC] * 3, out_specs=VMEM_SPEC,
        out_shape=jax.ShapeDtypeStruct((E + SMALL_LATE_ROWS, LANES), F32),
        scratch_shapes=[pltpu.VMEM((N_DEV, E, LANES), F32), pltpu.VMEM((N_DEV, SMALL_LATE_ROWS, LANES), F32),
                        pltpu.VMEM((SMALL_LATE_ROWS, LANES), F32),
                        pltpu.SemaphoreType.DMA((14,)), pltpu.SemaphoreType.DMA((14,)), pltpu.SemaphoreType.DMA((2,))],
    )(early, dg_pre_mix, loss_part)

    def body(tot, *rest):
        w_refs, m_refs, v_refs = [dict(zip(names, rest[i * n_p:(i + 1) * n_p])) for i in range(3)]
        outs = rest[3 * n_p:]
        loss_out = outs[0]
        g_out, d_out, m_out, v_out = [dict(zip(names, outs[1 + i * n_p:1 + (i + 1) * n_p])) for i in range(4)]
        loss_out[...] = tot[E + 8:E + 9, :]

        x, y, c = _position()
        me = 4 * x + 2 * y + c

        def update(name, g, ix):
            g_out[name][ix] = g
            d_out[name][ix], m_out[name][ix], v_out[name][ix] = _adamw_math(
                w_refs[name][ix], g, m_refs[name][ix], v_refs[name][ix])

        for name in names:
            shape = w[name].shape
            if name in SMALL_SHARDED:
                per_dev = dict(SMALL_EARLY)[name] // N_DEV
                at = pl.multiple_of(SMALL_EARLY_AT[name] + me * per_dev, 8)
                g = tot[pl.ds(at, per_dev), :]
                update(name, g[:shape[1], :shape[2]], (0,))
            elif name == "w_spatial":
                for grp in range(GROUPS):
                    at = SMALL_EARLY_AT[name] + grp * SBLOCK
                    update(name, tot[at:at + SBLOCK, :], (0, grp))
            elif name == "b_spatial":
                at = SMALL_EARLY_AT[name]
                update(name, tot[at:at + GROUPS, :], (0,))
            else:
                at = E if name == "norm_pre_mix" else SMALL_EARLY_AT[name]
                for k in range(shape[1] // LANES):
                    update(name, tot[at + k:at + k + 1, :], (slice(None), pl.ds(k * LANES, LANES)))

    state = [s[n] for s in (w, m, v) for n in names]
    out_shapes = [jax.ShapeDtypeStruct((1, LANES), F32)] + [jax.ShapeDtypeStruct(w[n].shape, F32) for n in names] * 4
    outs = pl.pallas_call(
        body, name=name + "_adamw",
        in_specs=[VMEM_SPEC] * (1 + len(state)), out_specs=[VMEM_SPEC] * len(out_shapes), out_shape=out_shapes,
    )(total, *state)
    per_name = {n: tuple(outs[1 + i * n_p + j] for i in range(4)) for j, n in enumerate(names)}
    return outs[0], per_name


def _tile_rows(n_elems):
    return -(-n_elems // (8 * LANES)) * 8


def _pack_rows(parts, rows):
    pieces = []
    for p in parts:
        q = p.reshape(-1, LANES)
        pieces.append(jnp.pad(q, ((0, _tile_rows(p.size) - q.shape[0]), (0, 0))))
    buf = jnp.concatenate(pieces, axis=0)
    return jnp.pad(buf, ((0, rows - buf.shape[0]), (0, 0)))


def _unpack_rows(buf, shapes):
    out, off = [], 0
    for shp in shapes:
        n = 1
        for s in shp:
            n *= s
        out.append(buf[off:off + n // LANES].reshape(shp))
        off += _tile_rows(n)
    return out


def _in_w_blocks_call(a, live, finished, *, name, tk=TOKEN_TILE):
    S = a.shape[0]
    tk = min(tk, S)
    steps = S // tk
    n_live = len(live)
    n_chips = N_DEV // 2
    out_shape = (n_chips, IN_BLK, D_MODEL)

    def body(a_ref, *rest):
        live_refs, done_refs = rest[:n_live], rest[n_live:n_live + len(finished)]
        keep_ref, send_ref = rest[n_live + len(finished):n_live + len(finished) + 2]
        accs = rest[n_live + len(finished) + 2:]
        k = pl.program_id(0)

        @pl.when(k == 0)
        def _():
            for acc in accs:
                acc[...] = jnp.zeros_like(acc)

        av = a_ref[...]
        for src, acc in zip(live_refs, accs):
            for m0 in range(0, src.shape[1], 1024):
                m1 = min(src.shape[1], m0 + 1024)
                acc[m0:m1, :] += _dot(src[:, m0:m1], av, TN)

        @pl.when(k == steps - 1)
        def _():
            groups = [(acc, cols) for acc, (_, cols) in zip(accs, live)]
            groups += [(ref, cols) for ref, (_, cols) in zip(done_refs, finished)]
            core = lax.axis_index("c")

            def cut(d, out_ref):
                lo, hi = IN_BLK * d, IN_BLK * (d + 1)
                for ref, (c0, c1) in groups:
                    s0, e0 = max(lo, c0), min(hi, c1)
                    if s0 < e0:
                        out_ref[d // 2, s0 - lo:e0 - lo, :] = ref[s0 - c0:e0 - c0, :].astype(BF16)

            for d in range(N_DEV):
                pl.when(core == d % 2)(lambda d=d: cut(d, keep_ref))
                pl.when(core != d % 2)(lambda d=d: cut(d, send_ref))

    acc_shapes = [(arr.shape[1], D_MODEL) for arr, _ in live]
    tiles = [((tk, D_MODEL), BF16)] + [((tk, arr.shape[1]), BF16) for arr, _ in live]
    resident = ([(arr.shape, arr.dtype) for arr, _ in finished] + [(out_shape, BF16)] * 2
                + [(s, F32) for s in acc_shapes])
    return pl.pallas_call(
        body, name=name, grid=(steps,),
        in_specs=[pl.BlockSpec((tk, D_MODEL), lambda k: (k, 0))]
        + [pl.BlockSpec((tk, arr.shape[1]), lambda k: (k, 0)) for arr, _ in live]
        + [_res(arr.shape) for arr, _ in finished],
        out_specs=[_acc(out_shape)] * 2,
        out_shape=[jax.ShapeDtypeStruct(out_shape, BF16)] * 2,
        scratch_shapes=[pltpu.VMEM(s, F32) for s in acc_shapes],
        compiler_params=_cp(("arbitrary",), _vmem_limit(tiles, resident)),
    )(a, *[arr for arr, _ in live], *[arr for arr, _ in finished])


def _local_step(x, target, W, scatter, finish, pair, chip_scatter):
    g1, g2, g3, g4 = [W[n].reshape(1, D_MODEL) for n in ("norm_pre_mix", "norm_post_mix", "norm_pre_ffn", "norm_post_ffn")]
    wfi, wfo = W["w_ffn_in"], W["w_ffn_out"].reshape(4, FF_BLK, D_MODEL)
    wg = jnp.pad(W["w_gate_up"], ((0, LANES - RANK), (0, 0))).astype(BF16)
    bgate = W["b_gate"].reshape(1, QK)
    gnorm = W["gla_norm"].reshape(HEADS, 1, DV)
    ln_g = W["sgu_ln_g"].reshape(GROUPS, 1, DG)
    ln_b = W["sgu_ln_b"].reshape(GROUPS, 1, DG)
    w_sp = W["w_spatial"]
    b_sp = W["b_spatial"].reshape(GROUPS, SBLOCK, 1)

    a, pa, alow, ps, pg, w_in_t = _in_proj_call(x, g1, W["w_in_blocks"], name="in_proj")
    y_gla, states = _gla_fwd_call(pa, alow, wg, bgate, gnorm, name="gla_fwd")
    y_sgu = _sgu_fwd_call(ps, ln_g, ln_b, w_sp, b_sp, name="sgu_fwd")
    t1, t2, merged, mix, x1, h = _mixer_tail_call(y_gla, y_sgu, pg, x, W["w_branch_gla"], W["w_branch_sgu"],
                                                  W["w_out"], g2, g3, name="mixer_tail")
    gu, z = _ffn_in_call(h, wfi, name="ffn_in")
    loss, dx2, dy, dg4 = _ffn_out_loss_call(z, wfo, x1, target, g4, name="ffn_out_loss")

    grads = {"norm_post_ffn": dg4}
    done = {}
    dgu = _ffn_out_bwd_call(dy, wfo, gu, name="ffn_out_bwd")
    dw_ffn_out = _weight_grads_call([z, dy], [(0, 1)], name="d_ffn_out_w")[0].reshape(N_DEV, D_FF // N_DEV, D_MODEL)
    dw_ffn_in = _ffn_in_grad_call(h, dgu, name="d_ffn_in_w")
    dgu, dw_ffn_out, dw_ffn_in = lax.optimization_barrier((dgu, dw_ffn_out, dw_ffn_in))
    ffn_received = scatter(("w_ffn_out", "w_ffn_in"), [dw_ffn_out, dw_ffn_in])
    dx1, dmix, grads["norm_pre_ffn"], grads["norm_post_mix"] = _ffn_in_bwd_call(dgu, wfi, dx2, x1, mix, g3, g2, name="ffn_in_bwd")
    dt1, dt2, dpg, dy_gla, dy_sgu = _mixer_bwd_call(dmix, t1, t2, pg, W["w_out"], W["w_branch_gla"], W["w_branch_sgu"],
                                                    name="mixer_bwd")
    rows = D_MODEL // N_DEV
    mixer_grads = _weight_grads_call([merged, dmix, y_gla, dt1, y_sgu, dt2], [(0, 1), (2, 3), (4, 5)], name="d_mixer_w")
    dy_gla, dy_sgu, mixer_grads = lax.optimization_barrier((dy_gla, dy_sgu, mixer_grads))
    mixer_received = scatter(("w_out", "w_branch_gla", "w_branch_sgu"),
                             [g.reshape(N_DEV, rows, D_MODEL) for g in mixer_grads])
    dps, dlg, dlb, dwsp, dbsp = _sgu_bwd_call(ps, ln_g, ln_b, w_sp, b_sp, dy_sgu, name="sgu_bwd")
    dw_s, dw_g = _weight_grads_call([a, dps, dpg], [(1, 0), (2, 0)], name="d_in_w_sgu_gates")
    dy_gla, dw_s, dw_g = lax.optimization_barrier((dy_gla, dw_s, dw_g))
    dpa, dlogit, dgn, dbg = _gla_bwd_call(pa, alow, wg, bgate, gnorm, states, dy_gla, name="gla_bwd")
    ffn_received, mixer_received, dpa, dlogit = lax.optimization_barrier((ffn_received, mixer_received, dpa, dlogit))
    dlow = _mm(dlogit, wg, "nt", BF16, name="d_gate_up_x")
    keep, send = _in_w_blocks_call(a, [(dpa, A_COLS), (dlow, LOW_COLS)], [(dw_s, S_COLS), (dw_g, G_COLS)],
                                   name="d_in_w_blocks")
    ffn_received, mixer_received, send = lax.optimization_barrier((ffn_received, mixer_received, send))
    from_sibling = pair(send)
    done.update({**finish(ffn_received), **finish(mixer_received)})
    dwg = _mm(alow, dlogit, "tn", F32, name="d_gate_up_w")
    done, dwg, keep = lax.optimization_barrier((done, dwg, keep))
    chip_sum = _pair_sum_call(keep, from_sibling, name="pair_sum_w_in")
    chip_sum, dpa = lax.optimization_barrier((chip_sum, dpa))
    in_received = chip_scatter(chip_sum)
    grad_x, grads["norm_pre_mix"] = _in_proj_bwd_call(dpa, dlow, dps, dpg, w_in_t, x, dx1, g1, name="in_proj_bwd")

    grads["w_gate_up"] = dwg[:RANK]
    grads["b_gate"] = dbg
    grads["gla_norm"] = dgn
    grads["sgu_ln_g"] = dlg
    grads["sgu_ln_b"] = dlb
    grads["w_spatial"] = dwsp
    grads["b_spatial"] = dbsp
    done.update(finish({"w_in": in_received}))
    return loss, grad_x, grads, done


WEIGHTS = ("norm_pre_mix", "w_in", "w_gate_up", "b_gate", "gla_norm", "sgu_ln_g", "sgu_ln_b", "w_spatial",
           "b_spatial", "w_branch_gla", "w_branch_sgu", "w_out", "norm_post_mix", "norm_pre_ffn", "w_ffn_in",
           "w_ffn_out", "norm_post_ffn")
BIG = ("w_in", "w_branch_gla", "w_branch_sgu", "w_out", "w_ffn_in", "w_ffn_out")
MIXER = ("w_branch_gla", "w_branch_sgu", "w_out")
FFN = ("w_ffn_in", "w_ffn_out")
COLUMN_SHARDED = ("w_in", "w_ffn_in")
LATE_SCATTER_WAVES = 4
SMALL = tuple(n for n in WEIGHTS if n not in BIG)
SMALL_SHARDED = ("w_gate_up", "gla_norm", "sgu_ln_g", "sgu_ln_b")
SMALL_FULL = {"norm_pre_mix": (1024,), "w_gate_up": (16, 512), "b_gate": (512,), "gla_norm": (4, 256),
              "sgu_ln_g": (4, 256), "sgu_ln_b": (4, 256), "w_spatial": (4, 128, 128), "b_spatial": (4, 128),
              "norm_post_mix": (1024,), "norm_pre_ffn": (1024,), "norm_post_ffn": (1024,)}
SMALL_GRAD_ROWS = 648
SMALL_STATE_ROWS = 600
SMALL_GATHER_ROWS = 32


def kernel(x, norm_pre_mix, w_in, w_gate_up, b_gate, gla_norm, sgu_ln_g, sgu_ln_b, w_spatial, b_spatial, w_branch_gla, w_branch_sgu, w_out, norm_post_mix, norm_pre_ffn, w_ffn_in, w_ffn_out, norm_post_ffn, loss_target, m_norm_pre_mix, m_w_in, m_w_gate_up, m_b_gate, m_gla_norm, m_sgu_ln_g, m_sgu_ln_b, m_w_spatial, m_b_spatial, m_w_branch_gla, m_w_branch_sgu, m_w_out, m_norm_post_mix, m_norm_pre_ffn, m_w_ffn_in, m_w_ffn_out, m_norm_post_ffn, v_norm_pre_mix, v_w_in, v_w_gate_up, v_b_gate, v_gla_norm, v_sgu_ln_g, v_sgu_ln_b, v_w_spatial, v_b_spatial, v_w_branch_gla, v_w_branch_sgu, v_w_out, v_norm_post_mix, v_norm_pre_ffn, v_w_ffn_in, v_w_ffn_out, v_norm_post_ffn):
    given = dict(locals())
    def local(a, n):
        return a[0].T if n in COLUMN_SHARDED else a[0]

    w = {n: local(given[n], n) for n in WEIGHTS}
    m = {n: local(given["m_" + n], n) for n in WEIGHTS}
    v = {n: local(given["v_" + n], n) for n in WEIGHTS}
    xs, target = x[0], loss_target[0]
    me = 4 * lax.axis_index("x") + 2 * lax.axis_index("y") + lax.axis_index("c")

    small_shard = _pack_rows([w[n] for n in SMALL_SHARDED], SMALL_GATHER_ROWS)
    first = _all_gather_hbm([w["w_in"].astype(BF16), small_shard], name="gather_w_in")
    rest, first = lax.optimization_barrier(([w[n].astype(BF16) for n in MIXER + FFN], first))
    rest_blocks = _all_gather_async(rest, name="gather_rest", collective_id=1)
    W = {n: w[n] for n in SMALL if n not in SMALL_SHARDED}
    blocks = dict(zip(MIXER + FFN, rest_blocks))
    for n in ("w_branch_gla", "w_branch_sgu", "w_out", "w_ffn_out"):
        W[n] = blocks[n].reshape(-1, D_MODEL)
    W["w_ffn_in"] = blocks["w_ffn_in"]
    W["w_in_blocks"] = first[0]
    small_blocks = first[1]
    off = 0
    for n in SMALL_SHARDED:
        r, c = w[n].shape
        blk = small_blocks[:, off:off + r * c // LANES].reshape(N_DEV, r, c)
        W[n] = blk.transpose(1, 0, 2).reshape(r, N_DEV * c)
        off += _tile_rows(r * c)

    scatter_ids = iter((3, 4))

    def scatter(names, parts):
        got = _scatter_blocks_async(parts, name="scatter_" + "_".join(names), collective_id=next(scatter_ids))
        return dict(zip(names, got))

    def finish(received):
        return {n: _adamw_reduce_call(r, w[n], m[n], v[n], name="adamw_" + n) for n, r in received.items()}

    def pair(part):
        return _pair_exchange_async([part], name="pair_w_in", collective_id=5)[0]

    def chip_scatter(part):
        return _chip_scatter_async([part], name="scatter_w_in", collective_id=6, waves=LATE_SCATTER_WAVES)[0]

    loss_part, grad_x, grads, big_done = _local_step(xs, target, W, scatter, finish, pair, chip_scatter)

    loss_row, small_done = _small_update_call(
        _small_early_rows(grads), grads["norm_pre_mix"], loss_part,
        *[{n: given[prefix + n] for n in SMALL} for prefix in ("", "m_", "v_")], name="small_update")

    def pick(i):
        return [small_done[n][i] if n in SMALL else (big_done[n][i].T if n in COLUMN_SHARDED else big_done[n][i])[None]
                for n in WEIGHTS]

    return (loss_row[0, 0], grad_x[None], *pick(0), *pick(1), *pick(2), *pick(3))
```

```python
import jax
import jax.numpy as jnp
from jax import lax
from jax.experimental import pallas as pl
from jax.experimental.pallas import tpu as pltpu
from jax.experimental.pallas import tpu_sc as plsc

F32 = jnp.float32
BF16 = jnp.bfloat16

D_MODEL = 1024
N_DEV = 8
CHUNK = 64
HEADS = 4
DK = 128
DV = 256
QK = HEADS * DK
GV = HEADS * DV
RANK = 16
GROUPS = 4
SBLOCK = 128
DG = 256
D_FF = 2816
FF_BLK = 704
EPS = 1e-6
Q_SCALE = DK ** -0.5
LANES = 128
VMEM_BIG = 48 * 1024 * 1024

D_IN = 7184
IN_BLK = 898
A_COLS = (0, 3072)
LOW_COLS = (3072, 3088)
S_COLS = (3088, 5136)
G_COLS = (5136, 7184)

ADAM_LR = 0.001
ADAM_B1 = 0.9
ADAM_B2 = 0.999
ADAM_EPS = 1e-08
ADAM_WD = 0.01
ADAM_STEP = 10

MESH = pl.DeviceIdType.MESH
ANY = pl.BlockSpec(memory_space=pl.ANY)
VMEM_SPEC = pl.BlockSpec(memory_space=pltpu.VMEM)


def _cp(sem=None, vmem=None):
    return pltpu.CompilerParams(dimension_semantics=sem, vmem_limit_bytes=vmem)


def _hbm(*arrays):
    return [pltpu.with_memory_space_constraint(a, pltpu.HBM) for a in arrays]


def _sig(x):
    return 0.5 * jnp.tanh(0.5 * x) + 0.5


GELU_C = 0.7978845608028654
GELU_A = 0.044715


def _gelu(x):
    t = jnp.tanh((GELU_C * x) * (1.0 + GELU_A * (x * x)))
    return (0.5 * x) * (1.0 + t)


def _gelu_and_grad(x):
    x2 = x * x
    t = jnp.tanh((GELU_C * x) * (1.0 + GELU_A * x2))
    one_t = 1.0 + t
    hx = 0.5 * x
    grad = 0.5 * one_t + (hx * (1.0 - t * t)) * (GELU_C + (3.0 * GELU_A * GELU_C) * x2)
    return hx * one_t, grad


def _logsig(x):
    return jnp.minimum(x, 0.0) - jnp.log1p(jnp.exp(-jnp.abs(x)))


def _dot(a, b, dims):
    return lax.dot_general(a, b, (dims, ((), ())), preferred_element_type=F32)


NN = ((1,), (0,))
NT = ((1,), (1,))
TN = ((0,), (0,))


def _exact_mask_dot(mask_bf16, x):
    hi = x.astype(BF16)
    r1 = x - hi.astype(F32)
    mid = r1.astype(BF16)
    lo = (r1 - mid.astype(F32)).astype(BF16)
    return _dot(mask_bf16, hi, NN) + _dot(mask_bf16, mid, NN) + _dot(mask_bf16, lo, NN)


def _pick_tile(dim, target):
    if dim <= target:
        return dim
    best = None
    for t in range(LANES, int(1.4 * target) + 1, LANES):
        if dim % t == 0:
            best = t
    assert best is not None, (dim, target)
    return best


def _mm_call(a, b, *, name, grid, a_spec, b_spec, o_spec, out_shape, dims, acc_shape):
    nk = grid[2]

    def body(a_ref, b_ref, o_ref, *acc):
        part = _dot(a_ref[...], b_ref[...], dims)
        if nk == 1:
            o_ref[...] = part.astype(o_ref.dtype)
        else:
            acc_ref = acc[0]
            k = pl.program_id(2)

            @pl.when(k == 0)
            def _():
                acc_ref[...] = part

            @pl.when(k > 0)
            def _():
                acc_ref[...] += part

            @pl.when(k == nk - 1)
            def _():
                o_ref[...] = acc_ref[...].astype(o_ref.dtype)

    return pl.pallas_call(
        body, name=name, grid=grid, in_specs=[a_spec, b_spec], out_specs=o_spec, out_shape=out_shape,
        scratch_shapes=[] if nk == 1 else [pltpu.VMEM(acc_shape, F32)],
        compiler_params=_cp(("parallel", "parallel", "arbitrary"), VMEM_BIG),
    )(a, b)


def _mm(a, b, mode, out_dtype, *, name, tm=512, tn=1024, tk=1024):
    if mode == "nn":
        (M, K), (_, N) = a.shape, b.shape
    elif mode == "nt":
        (M, K), (N, _) = a.shape, b.shape
    else:
        (K, M), (_, N) = a.shape, b.shape
    tm, tn, tk = _pick_tile(M, tm), _pick_tile(N, tn), _pick_tile(K, tk)
    if mode == "nn":
        a_spec = pl.BlockSpec((tm, tk), lambda i, j, k: (i, k))
        b_spec = pl.BlockSpec((tk, tn), lambda i, j, k: (k, j))
        dims = NN
    elif mode == "nt":
        a_spec = pl.BlockSpec((tm, tk), lambda i, j, k: (i, k))
        b_spec = pl.BlockSpec((tn, tk), lambda i, j, k: (j, k))
        dims = NT
    else:
        a_spec = pl.BlockSpec((tk, tm), lambda i, j, k: (k, i))
        b_spec = pl.BlockSpec((tk, tn), lambda i, j, k: (k, j))
        dims = TN
    return _mm_call(a, b, name=name, grid=(M // tm, N // tn, K // tk), a_spec=a_spec, b_spec=b_spec,
                    o_spec=pl.BlockSpec((tm, tn), lambda i, j, k: (i, j)),
                    out_shape=jax.ShapeDtypeStruct((M, N), out_dtype), dims=dims, acc_shape=(tm, tn))


ROW_TILE = 512
SUB_ROWS = 256


def _sub_tiles(T):
    return [pl.ds(r0, min(SUB_ROWS, T)) for r0 in range(0, T, SUB_ROWS)]


def _rt(T, W, c=0):
    return pl.BlockSpec((T, W), lambda i: (i, c))


def _rt3(nb, T, W):
    return pl.BlockSpec((nb, T, W), lambda i: (0, i, 0))


def _res(shape):
    nd = len(shape)
    return pl.BlockSpec(tuple(shape), lambda i: (0,) * nd, pipeline_mode=pl.Buffered(1))


def _acc(shape):
    nd = len(shape)
    return pl.BlockSpec(tuple(shape), lambda i: (0,) * nd, pipeline_mode=pl.Buffered(1))


def _nbytes(shape, dtype):
    n = jnp.dtype(dtype).itemsize
    for s in shape:
        n *= s
    return n


def _vmem_limit(tiles, resident, temps=16 * 1024 * 1024):
    need = 2 * sum(_nbytes(s, d) for s, d in tiles) + sum(_nbytes(s, d) for s, d in resident) + temps
    return min(need, 60 * 1024 * 1024)


def _tok_call(body, *, name, S, T, ins, outs, semantics="parallel"):
    tiles = [(spec.block_shape, a.dtype) for a, spec, kind in ins + outs if kind == "tile"]
    resident = [(a.shape, a.dtype) for a, spec, kind in ins + outs if kind == "res"]
    return pl.pallas_call(
        body, name=name, grid=(S // T,),
        in_specs=[spec for _, spec, _ in ins], out_specs=[spec for _, spec, _ in outs],
        out_shape=[pltpu.HBM(a.shape, a.dtype) for a, _, _ in outs],
        compiler_params=_cp((semantics,), _vmem_limit(tiles, resident)),
    )(*_hbm(*[a for a, _, _ in ins]))


def _tile(a, spec):
    return (a, spec, "tile")


def _whole(a):
    return (a, _res(a.shape), "res")


def _out_tile(shape, dtype, spec):
    return (jax.ShapeDtypeStruct(shape, dtype), spec, "tile")


def _out_acc(shape, dtype=F32):
    return (jax.ShapeDtypeStruct(shape, dtype), _acc(shape), "res")


def _rms_stats(x):
    r = lax.rsqrt(jnp.mean(x * x, axis=-1, keepdims=True) + EPS)
    return r, x * r


def _rms_bwd(xh, r, g, dy):
    dxh = dy * g
    dx = r * (dxh - xh * jnp.mean(dxh * xh, axis=-1, keepdims=True))
    dg = jnp.sum(dy * xh, axis=0, keepdims=True)
    return dx, dg


def _accum(ref, val):
    @pl.when(pl.program_id(0) == 0)
    def _():
        ref[...] = val

    @pl.when(pl.program_id(0) > 0)
    def _():
        ref[...] += val


def _dot_rows_t(a, w_ref, row0, o_ref, chunk=1024):
    n = o_ref.shape[1]
    for n0 in range(0, n, chunk):
        n1 = min(n, n0 + chunk)
        o_ref[:, n0:n1] = _dot(a, w_ref[row0 + n0:row0 + n1, :], NT).astype(o_ref.dtype)


def _in_proj_call(x, g1, w_blocks, *, name, T=ROW_TILE):
    S = x.shape[0]

    def body(x_ref, g_ref, blk_ref, a_ref, pa_ref, al_ref, ps_ref, pg_ref, w_ref):
        @pl.when(pl.program_id(0) == 0)
        def _():
            for d in range(N_DEV):
                w_ref[d * IN_BLK:(d + 1) * IN_BLK, :] = blk_ref[d]

        _, xh = _rms_stats(x_ref[...])
        a = (xh * g_ref[...]).astype(BF16)
        a_ref[...] = a
        _dot_rows_t(a, w_ref, A_COLS[0], pa_ref)
        _dot_rows_t(a, w_ref, LOW_COLS[0], al_ref)
        _dot_rows_t(a, w_ref, S_COLS[0], ps_ref)
        _dot_rows_t(a, w_ref, G_COLS[0], pg_ref)

    widths = (D_MODEL, A_COLS[1] - A_COLS[0], LANES, S_COLS[1] - S_COLS[0], G_COLS[1] - G_COLS[0])
    return _tok_call(
        body, name=name, S=S, T=T, semantics="arbitrary",
        ins=[_tile(x, _rt(T, D_MODEL)), _whole(g1), _whole(w_blocks)],
        outs=[_out_tile((S, w), BF16, _rt(T, w)) for w in widths] + [_out_acc((D_IN, D_MODEL), BF16)])


def _mixer_tail_call(y_gla, y_sgu, pg, x, w_bg, w_bs, w_out, g2, g3, *, name, T=ROW_TILE):
    S = x.shape[0]

    def body(yg_ref, ys_ref, pg_ref, x_ref, wbg_ref, wbs_ref, wo_ref, g2_ref, g3_ref,
             t1_ref, t2_ref, mg_ref, mix_ref, x1_ref, h_ref):
        for rows in _sub_tiles(T):
            t1 = _dot(yg_ref[rows, :], wbg_ref[...], NN)
            t2 = _dot(ys_ref[rows, :], wbs_ref[...], NN)
            t1_ref[rows, :] = t1.astype(BF16)
            t2_ref[rows, :] = t2.astype(BF16)
            sg = _sig(pg_ref[rows, pl.ds(0, D_MODEL)].astype(F32))
            ss = _sig(pg_ref[rows, pl.ds(D_MODEL, D_MODEL)].astype(F32))
            merged = (sg * t1 + ss * t2).astype(BF16)
            mg_ref[rows, :] = merged
            mix = _dot(merged, wo_ref[...], NN)
            mix_ref[rows, :] = mix
            _, mh = _rms_stats(mix)
            x1 = x_ref[rows, :] + mh * g2_ref[...]
            x1_ref[rows, :] = x1
            _, xh = _rms_stats(x1)
            h_ref[rows, :] = (xh * g3_ref[...]).astype(BF16)

    row = _rt(T, D_MODEL)
    b16 = lambda: _out_tile((S, D_MODEL), BF16, row)
    f32 = lambda: _out_tile((S, D_MODEL), F32, row)
    return _tok_call(
        body, name=name, S=S, T=T,
        ins=[_tile(y_gla, row), _tile(y_sgu, row), _tile(pg, _rt(T, 2 * D_MODEL)), _tile(x, row),
             _whole(w_bg), _whole(w_bs), _whole(w_out), _whole(g2), _whole(g3)],
        outs=[b16(), b16(), b16(), f32(), f32(), b16()])


def _ffn_in_call(h, wfi, *, name, T=ROW_TILE):
    S = h.shape[0]

    def body(h_ref, w_ref, gu_ref, z_ref):
        hv = h_ref[...]
        for d in range(4):
            gate = _dot(hv, w_ref[d], NT)
            up = _dot(hv, w_ref[d + 4], NT)
            gu_ref[d] = gate.astype(BF16)
            gu_ref[d + 4] = up.astype(BF16)
            z_ref[d] = (gate * _sig(gate) * up).astype(BF16)

    return _tok_call(
        body, name=name, S=S, T=T,
        ins=[_tile(h, _rt(T, D_MODEL)), _whole(wfi)],
        outs=[_out_tile((N_DEV, S, FF_BLK), BF16, _rt3(N_DEV, T, FF_BLK)),
              _out_tile((4, S, FF_BLK), BF16, _rt3(4, T, FF_BLK))])


def _ffn_out_loss_call(z, wfo, x1, target, g4, *, name, T=ROW_TILE):
    S = x1.shape[0]

    def body(z_ref, w_ref, x1_ref, t_ref, g4_ref, loss_ref, dx2_ref, dy_ref, dg4_ref):
        loss = jnp.zeros((1, 1), F32)
        dg4 = jnp.zeros((1, D_MODEL), F32)
        for rows in _sub_tiles(T):
            y = _dot(z_ref[0, rows, :], w_ref[0], NN)
            for d in range(1, 4):
                y = y + _dot(z_ref[d, rows, :], w_ref[d], NN)
            r, yh = _rms_stats(y)
            diff = x1_ref[rows, :] + yh * g4_ref[...] - t_ref[rows, :]
            loss = loss + 0.5 * jnp.sum(jnp.mean(diff * diff, axis=-1, keepdims=True), axis=0, keepdims=True)
            dx2 = diff * (1.0 / D_MODEL)
            dx2_ref[rows, :] = dx2
            dy, dg = _rms_bwd(yh, r, g4_ref[...], dx2)
            dy_ref[rows, :] = dy.astype(BF16)
            dg4 = dg4 + dg
        _accum(loss_ref, jnp.broadcast_to(loss, (1, LANES)))
        _accum(dg4_ref, dg4)

    row = _rt(T, D_MODEL)
    return _tok_call(
        body, name=name, S=S, T=T, semantics="arbitrary",
        ins=[_tile(z, _rt3(4, T, FF_BLK)), _whole(wfo), _tile(x1, row), _tile(target, row), _whole(g4)],
        outs=[_out_acc((1, LANES)), _out_tile((S, D_MODEL), F32, row), _out_tile((S, D_MODEL), BF16, row),
              _out_acc((1, D_MODEL))])


def _ffn_out_bwd_call(dy, wfo, gu, *, name, T=ROW_TILE):
    S = dy.shape[0]

    def body(dy_ref, w_ref, gu_ref, dgu_ref):
        dyv = dy_ref[...]
        for d in range(4):
            dz = _dot(dyv, w_ref[d], NT).astype(BF16)
            gt = gu_ref[d]
            up = gu_ref[d + 4]
            s = _sig(gt)
            dgu_ref[d] = dz * up * s * (1.0 + gt * (1.0 - s))
            dgu_ref[d + 4] = dz * gt * s

    blocks = _rt3(N_DEV, T, FF_BLK)
    return _tok_call(
        body, name=name, S=S, T=T,
        ins=[_tile(dy, _rt(T, D_MODEL)), _whole(wfo), _tile(gu, blocks)],
        outs=[_out_tile((N_DEV, S, FF_BLK), BF16, blocks)])[0]


def _ffn_in_bwd_call(dgu, wfi, dx2, x1, mix, g3, g2, *, name, T=ROW_TILE):
    S = x1.shape[0]

    def body(dgu_ref, w_ref, dx2_ref, x1_ref, mix_ref, g3_ref, g2_ref, dx1_ref, dmix_ref, dg3_ref, dg2_ref):
        dg3 = jnp.zeros((1, D_MODEL), F32)
        dg2 = jnp.zeros((1, D_MODEL), F32)
        for rows in _sub_tiles(T):
            dh = _dot(dgu_ref[0, rows, :], w_ref[0], NN)
            for d in range(1, N_DEV):
                dh = dh + _dot(dgu_ref[d, rows, :], w_ref[d], NN)
            r3, xh = _rms_stats(x1_ref[rows, :])
            d3, g = _rms_bwd(xh, r3, g3_ref[...], dh)
            dg3 = dg3 + g
            dx1 = dx2_ref[rows, :] + d3
            dx1_ref[rows, :] = dx1
            r2, mh = _rms_stats(mix_ref[rows, :])
            dmix, g = _rms_bwd(mh, r2, g2_ref[...], dx1)
            dg2 = dg2 + g
            dmix_ref[rows, :] = dmix.astype(BF16)
        _accum(dg3_ref, dg3)
        _accum(dg2_ref, dg2)

    row = _rt(T, D_MODEL)
    return _tok_call(
        body, name=name, S=S, T=T, semantics="arbitrary",
        ins=[_tile(dgu, _rt3(N_DEV, T, FF_BLK)), _whole(wfi), _tile(dx2, row), _tile(x1, row), _tile(mix, row),
             _whole(g3), _whole(g2)],
        outs=[_out_tile((S, D_MODEL), F32, row), _out_tile((S, D_MODEL), BF16, row),
              _out_acc((1, D_MODEL)), _out_acc((1, D_MODEL))])


def _mixer_bwd_call(dmix, t1, t2, pg, w_out, w_bg, w_bs, *, name, T=ROW_TILE):
    S = dmix.shape[0]

    def body(dmix_ref, t1_ref, t2_ref, pg_ref, wo_ref, wbg_ref, wbs_ref, dt1_ref, dt2_ref, dpg_ref, dyg_ref, dys_ref):
        gg, gs = pl.ds(0, D_MODEL), pl.ds(D_MODEL, D_MODEL)
        for rows in _sub_tiles(T):
            dm = _dot(dmix_ref[rows, :], wo_ref[...], NT)
            sg = _sig(pg_ref[rows, gg].astype(F32))
            ss = _sig(pg_ref[rows, gs].astype(F32))
            dt1 = (dm * sg).astype(BF16)
            dt2 = (dm * ss).astype(BF16)
            dt1_ref[rows, :] = dt1
            dt2_ref[rows, :] = dt2
            dpg_ref[rows, gg] = (dm * t1_ref[rows, :].astype(F32) * sg * (1.0 - sg)).astype(BF16)
            dpg_ref[rows, gs] = (dm * t2_ref[rows, :].astype(F32) * ss * (1.0 - ss)).astype(BF16)
            dyg_ref[rows, :] = _dot(dt1, wbg_ref[...], NT).astype(BF16)
            dys_ref[rows, :] = _dot(dt2, wbs_ref[...], NT).astype(BF16)

    row = _rt(T, D_MODEL)
    wide = _rt(T, 2 * D_MODEL)
    b16 = lambda: _out_tile((S, D_MODEL), BF16, row)
    return _tok_call(
        body, name=name, S=S, T=T,
        ins=[_tile(dmix, row), _tile(t1, row), _tile(t2, row), _tile(pg, wide), _whole(w_out), _whole(w_bg), _whole(w_bs)],
        outs=[b16(), b16(), _out_tile((S, 2 * D_MODEL), BF16, wide), b16(), b16()])


def _in_proj_bwd_call(dpa, dlow, dps, dpg, w_in_t, x, dx1, g1, *, name, T=ROW_TILE):
    S = x.shape[0]

    def body(dpa_ref, dl_ref, dps_ref, dpg_ref, w_ref, x_ref, dx1_ref, g1_ref, gx_ref, dg1_ref):
        da = (_dot(dpa_ref[...], w_ref[A_COLS[0]:A_COLS[1], :], NN)
              + _dot(dl_ref[...], w_ref[LOW_COLS[0]:LOW_COLS[0] + LANES, :], NN)
              + _dot(dps_ref[...], w_ref[S_COLS[0]:S_COLS[1], :], NN)
              + _dot(dpg_ref[...], w_ref[G_COLS[0]:G_COLS[1], :], NN))
        r, xh = _rms_stats(x_ref[...])
        dxa, dg = _rms_bwd(xh, r, g1_ref[...], da)
        gx_ref[...] = dx1_ref[...] + dxa
        _accum(dg1_ref, dg)

    row = _rt(T, D_MODEL)
    return _tok_call(
        body, name=name, S=S, T=T, semantics="arbitrary",
        ins=[_tile(dpa, _rt(T, dpa.shape[1])), _tile(dlow, _rt(T, dlow.shape[1])), _tile(dps, _rt(T, dps.shape[1])),
             _tile(dpg, _rt(T, dpg.shape[1])), _whole(w_in_t), _tile(x, row), _tile(dx1, row), _whole(g1)],
        outs=[_out_tile((S, D_MODEL), F32, row), _out_acc((1, D_MODEL))])


TOKEN_TILE = 512


def _weight_grads_part(arrays, pairs, *, tk=TOKEN_TILE, out_dtype=BF16):
    S = arrays[0].shape[-2]
    tk = min(tk, S)
    n_in = len(arrays)

    def out_shape(i, j):
        a, b = arrays[i], arrays[j]
        if a.ndim == 3:
            return (a.shape[0], a.shape[2], b.shape[1])
        if b.ndim == 3:
            return (b.shape[0], a.shape[1], b.shape[2])
        return (a.shape[1], b.shape[1])

    shapes = [out_shape(i, j) for i, j in pairs]

    in_place = out_dtype == F32

    def body(ins, outs, accs):
        k = pl.program_id(0)
        if in_place:
            accs = outs

        @pl.when(k == 0)
        def _():
            for acc in accs:
                acc[...] = jnp.zeros_like(acc)

        for (i, j), acc in zip(pairs, accs):
            a_ref, b_ref = ins[i], ins[j]
            if len(a_ref.shape) == 3:
                for n in range(a_ref.shape[0]):
                    acc[n] += _dot(a_ref[n], b_ref[...], TN)
            elif len(b_ref.shape) == 3:
                a = a_ref[...]
                for n in range(b_ref.shape[0]):
                    acc[n] += _dot(a, b_ref[n], TN)
            else:
                b = b_ref[...]
                for m0 in range(0, a_ref.shape[1], 1024):
                    m1 = min(a_ref.shape[1], m0 + 1024)
                    acc[m0:m1, :] += _dot(a_ref[:, m0:m1], b, TN)

        if not in_place:
            @pl.when(k == S // tk - 1)
            def _():
                for out, acc in zip(outs, accs):
                    out[...] = acc[...].astype(out.dtype)

    def in_spec(a):
        if a.ndim == 3:
            return pl.BlockSpec((a.shape[0], tk, a.shape[2]), lambda k: (0, k, 0))
        return pl.BlockSpec((tk, a.shape[1]), lambda k: (k, 0))

    return dict(
        body=body, steps=S // tk, arrays=list(arrays),
        in_specs=[in_spec(a) for a in arrays],
        out_specs=[_acc(s) for s in shapes],
        out_shapes=[pltpu.HBM(s, out_dtype) for s in shapes],
        scratch=[] if in_place else [pltpu.VMEM(s, F32) for s in shapes],
        tiles=[(in_spec(a).block_shape, a.dtype) for a in arrays],
        resident=[(s, F32) for s in shapes] + ([] if in_place else [(s, BF16) for s in shapes]))


def _weight_grads_call(arrays, pairs, *, name, tk=TOKEN_TILE):
    part = _weight_grads_part(arrays, pairs, tk=tk)
    n_in, n_out = len(part["arrays"]), len(part["out_shapes"])

    def body(*refs):
        part["body"](refs[:n_in], refs[n_in:n_in + n_out], refs[n_in + n_out:])

    return pl.pallas_call(
        body, name=name, grid=(part["steps"],),
        in_specs=part["in_specs"], out_specs=part["out_specs"], out_shape=part["out_shapes"],
        scratch_shapes=part["scratch"],
        compiler_params=_cp(("arbitrary",), _vmem_limit(part["tiles"], part["resident"])),
    )(*_hbm(*part["arrays"]))


def _with_rider(body, n_in, n_out, n_scratch, rider):
    if rider is None:
        return body
    r_in, r_out = len(rider["arrays"]), len(rider["out_shapes"])

    def both(*refs):
        ins, rest = refs[:n_in + r_in], refs[n_in + r_in:]
        outs, scratch = rest[:n_out + r_out], rest[n_out + r_out:]
        body(*ins[:n_in], *outs[:n_out], *scratch[:n_scratch])
        rider["body"](ins[n_in:], outs[n_out:], scratch[n_scratch:])

    return both


def _rider_lists(rider):
    if rider is None:
        return [], [], [], [], [], [], []
    return (rider["arrays"], rider["in_specs"], rider["out_specs"], rider["out_shapes"], rider["scratch"],
            rider["tiles"], rider["resident"])


def _ffn_in_grad_call(h, dgu, *, name, tk=TOKEN_TILE):
    S = h.shape[0]
    tk = min(tk, S)
    nb = 4
    shape = (nb, FF_BLK, D_MODEL)

    def body(h_ref, dgu_ref, out_ref, acc):
        k = pl.program_id(1)

        @pl.when(k == 0)
        def _():
            acc[...] = jnp.zeros_like(acc)

        hv = h_ref[...]
        for n in range(nb):
            acc[n] += _dot(dgu_ref[n], hv, TN)

        @pl.when(k == S // tk - 1)
        def _():
            out_ref[...] = acc[...].astype(out_ref.dtype)

    tiles = [((tk, D_MODEL), BF16), ((nb, tk, FF_BLK), BF16), (shape, BF16)]
    return pl.pallas_call(
        body, name=name, grid=(N_DEV // nb, S // tk),
        in_specs=[pl.BlockSpec((tk, D_MODEL), lambda g, k: (k, 0)),
                  pl.BlockSpec((nb, tk, FF_BLK), lambda g, k: (g, k, 0))],
        out_specs=pl.BlockSpec(shape, lambda g, k: (g, 0, 0)),
        out_shape=pltpu.HBM((N_DEV, FF_BLK, D_MODEL), BF16),
        scratch_shapes=[pltpu.VMEM(shape, F32)],
        compiler_params=_cp(("parallel", "arbitrary"), _vmem_limit(tiles, [(shape, F32)])),
    )(*_hbm(h, dgu))


GLA_TILE = 256
Q_OFF, K_OFF, V_OFF, R_OFF = 0, QK, 2 * QK, 2 * QK + GV


def _tri(lower):
    r = lax.broadcasted_iota(jnp.int32, (CHUNK, CHUNK), 0)
    c = lax.broadcasted_iota(jnp.int32, (CHUNK, CHUNK), 1)
    return jnp.where((r >= c) if lower else (c >= r), 1.0, 0.0).astype(BF16)


def _gla_fwd_call(pa, alow, wg, bgate, gnorm, *, name):
    S = pa.shape[0]
    Tg = min(GLA_TILE, S)
    cb = Tg // CHUNK

    def body(pa_ref, al_ref, wg_ref, bg_ref, gn_ref, y_ref, st_ref, state):
        @pl.when(pl.program_id(0) == 0)
        def _():
            state[...] = jnp.zeros_like(state)

        logit = _dot(al_ref[...], wg_ref[...], NN) + bg_ref[...]
        ls = _logsig(logit) * (1.0 / 16.0)
        tri = _tri(True)
        for c in range(cb):
            rows = pl.ds(c * CHUNK, CHUNK)
            cum = _exact_mask_dot(tri, ls[c * CHUNK:(c + 1) * CHUNK])
            tot = cum[CHUNK - 1:CHUNK]
            kd = (pa_ref[rows, pl.ds(K_OFF, QK)].astype(F32) * jnp.exp(tot - cum)).astype(BF16)
            decay = jnp.exp(tot)
            for h in range(HEADS):
                lanes = slice(h * DK, (h + 1) * DK)
                new = state[h] * decay[:, lanes] + _dot(pa_ref[rows, pl.ds(V_OFF + h * DV, DV)], kd[:, lanes], TN)
                state[h] = new
                st_ref[c, h] = new
        for c in range(cb):
            rows = pl.ds(c * CHUNK, CHUNK)
            for h in range(HEADS):
                qs = (pa_ref[rows, pl.ds(Q_OFF + h * DK, DK)].astype(F32) * Q_SCALE).astype(BF16)
                o = _dot(qs, st_ref[c, h].astype(BF16), NT)
                rs = lax.rsqrt(jnp.mean(o * o, axis=-1, keepdims=True) + EPS)
                rr = pa_ref[rows, pl.ds(R_OFF + h * DV, DV)].astype(F32)
                y_ref[rows, pl.ds(h * DV, DV)] = (o * rs * gn_ref[h] * (rr * _sig(rr))).astype(BF16)

    return pl.pallas_call(
        body, name=name, grid=(S // Tg,),
        in_specs=[
            pl.BlockSpec((Tg, 2 * QK + 2 * GV), lambda t: (t, 0)),
            pl.BlockSpec((Tg, LANES), lambda t: (t, 0)),
            pl.BlockSpec((LANES, QK), lambda t: (0, 0)),
            pl.BlockSpec((1, QK), lambda t: (0, 0)),
            pl.BlockSpec((HEADS, 1, DV), lambda t: (0, 0, 0)),
        ],
        out_specs=[
            pl.BlockSpec((Tg, GV), lambda t: (t, 0)),
            pl.BlockSpec((cb, HEADS, DV, DK), lambda t: (t, 0, 0, 0)),
        ],
        out_shape=[pltpu.HBM((S, GV), BF16), pltpu.HBM((S // CHUNK, HEADS, DV, DK), F32)],
        scratch_shapes=[pltpu.VMEM((HEADS, DV, DK), F32)],
        compiler_params=_cp(("arbitrary",), VMEM_BIG),
    )(*_hbm(pa, alow, wg, bgate, gnorm))


def _gla_bwd_call(pa, alow, wg, bgate, gnorm, states, dy, *, name, rider=None):
    S = pa.shape[0]
    Tg = min(GLA_TILE, S)
    cb = Tg // CHUNK
    nt = S // Tg
    r_arrays, r_in, r_out, r_shapes, r_scratch, r_tiles, r_resident = _rider_lists(rider)
    assert rider is None or rider["steps"] == nt

    def rev(t):
        return nt - 1 - t

    def body(pa_ref, al_ref, wg_ref, bg_ref, gn_ref, st_ref, prev_ref, dy_ref,
             dpa_ref, dl_ref, dgn_ref, dbg_ref, carry, pbuf):
        t = pl.program_id(0)

        @pl.when(t == 0)
        def _():
            carry[...] = jnp.zeros_like(carry)
            dgn_ref[...] = jnp.zeros_like(dgn_ref)
            dbg_ref[...] = jnp.zeros_like(dbg_ref)

        logit = _dot(al_ref[...], wg_ref[...], NN) + bg_ref[...]
        ls = _logsig(logit) * (1.0 / 16.0)
        sneg = 1.0 / (1.0 + jnp.exp(logit))
        tri = _tri(True)
        upper = _tri(False)
        first_tile = rev(t) == 0
        heads = range(HEADS)

        w, decay, kd = [], [], []
        for c in range(cb):
            rows = pl.ds(c * CHUNK, CHUNK)
            cum = _exact_mask_dot(tri, ls[c * CHUNK:(c + 1) * CHUNK])
            tot = cum[CHUNK - 1:CHUNK]
            w.append(jnp.exp(tot - cum))
            decay.append(jnp.exp(tot))
            kd.append(pa_ref[rows, pl.ds(K_OFF, QK)].astype(F32) * w[c])

        dgn = [jnp.zeros((1, DV), F32) for _ in heads]
        for c in range(cb):
            rows = pl.ds(c * CHUNK, CHUNK)
            for h in heads:
                gn = gn_ref[h]
                qs = (pa_ref[rows, pl.ds(Q_OFF + h * DK, DK)].astype(F32) * Q_SCALE).astype(BF16)
                st16 = st_ref[c, h].astype(BF16)
                o = _dot(qs, st16, NT)
                rs = lax.rsqrt(jnp.mean(o * o, axis=-1, keepdims=True) + EPS)
                oh = o * rs
                rr = pa_ref[rows, pl.ds(R_OFF + h * DV, DV)].astype(F32)
                sr = _sig(rr)
                dyv = dy_ref[rows, pl.ds(h * DV, DV)].astype(F32)
                dpa_ref[rows, pl.ds(R_OFF + h * DV, DV)] = (
                    dyv * oh * gn * sr * (1.0 + rr * (1.0 - sr))).astype(BF16)
                don = dyv * (rr * sr)
                dgn[h] = dgn[h] + jnp.sum(don * oh, axis=0, keepdims=True)
                doh = don * gn
                do16 = (rs * (doh - oh * jnp.mean(doh * oh, axis=-1, keepdims=True))).astype(BF16)
                dpa_ref[rows, pl.ds(Q_OFF + h * DK, DK)] = (_dot(do16, st16, NN) * Q_SCALE).astype(BF16)
                pbuf[c, h] = _dot(do16, qs, TN)
        for h in heads:
            dgn_ref[h] += dgn[h]

        dkd = [[None] * HEADS for _ in range(cb)]
        ddecay = [[None] * HEADS for _ in range(cb)]
        for c in reversed(range(cb)):
            rows = pl.ds(c * CHUNK, CHUNK)
            for h in heads:
                lanes = slice(h * DK, (h + 1) * DK)
                gt = pbuf[c, h] + carry[h]
                gt16 = gt.astype(BF16)
                dkd[c][h] = _dot(pa_ref[rows, pl.ds(V_OFF + h * DV, DV)], gt16, NN)
                dpa_ref[rows, pl.ds(V_OFF + h * DV, DV)] = _dot(kd[c][:, lanes].astype(BF16), gt16, NT).astype(BF16)
                if c > 0:
                    st_prev = st_ref[c - 1, h]
                else:
                    st_prev = jnp.where(first_tile, 0.0, prev_ref[0, h])
                ddecay[c][h] = jnp.sum(gt * st_prev, axis=0, keepdims=True)
                carry[h] = gt * decay[c][:, lanes]

        dbg = jnp.zeros((1, QK), F32)
        for c in range(cb):
            rows = pl.ds(c * CHUNK, CHUNK)
            dkd_c = jnp.concatenate(dkd[c], axis=1)
            dpa_ref[rows, pl.ds(K_OFF, QK)] = (dkd_c * w[c]).astype(BF16)
            e = dkd_c * kd[c]
            dtot = jnp.sum(e, axis=0, keepdims=True) + jnp.concatenate(ddecay[c], axis=1) * decay[c]
            dls = dtot - _exact_mask_dot(upper, e)
            dlogit = dls * (1.0 / 16.0) * sneg[c * CHUNK:(c + 1) * CHUNK]
            dl_ref[rows, :] = dlogit.astype(BF16)
            dbg = dbg + jnp.sum(dlogit, axis=0, keepdims=True)
        dbg_ref[...] += dbg

    wide = 2 * QK + 2 * GV
    tiles = [((Tg, wide), BF16), ((cb + 1, HEADS, DV, DK), F32), ((Tg, GV), BF16), ((Tg, wide), BF16),
             ((Tg, QK), BF16)] + r_tiles
    resident = [((cb + 1, HEADS, DV, DK), F32)] + r_resident
    return pl.pallas_call(
        _with_rider(body, 8, 4, 2, rider), name=name, grid=(nt,),
        in_specs=[
            pl.BlockSpec((Tg, wide), lambda t: (rev(t), 0)),
            pl.BlockSpec((Tg, LANES), lambda t: (rev(t), 0)),
            pl.BlockSpec((LANES, QK), lambda t: (0, 0)),
            pl.BlockSpec((1, QK), lambda t: (0, 0)),
            pl.BlockSpec((HEADS, 1, DV), lambda t: (0, 0, 0)),
            pl.BlockSpec((cb, HEADS, DV, DK), lambda t: (rev(t), 0, 0, 0)),
            pl.BlockSpec((1, HEADS, DV, DK), lambda t: (jnp.maximum(rev(t) * cb - 1, 0), 0, 0, 0)),
            pl.BlockSpec((Tg, GV), lambda t: (rev(t), 0)),
        ] + r_in,
        out_specs=[
            pl.BlockSpec((Tg, wide), lambda t: (rev(t), 0)),
            pl.BlockSpec((Tg, QK), lambda t: (rev(t), 0)),
            pl.BlockSpec((HEADS, 1, DV), lambda t: (0, 0, 0)),
            pl.BlockSpec((1, QK), lambda t: (0, 0)),
        ] + r_out,
        out_shape=[pltpu.HBM((S, wide), BF16), pltpu.HBM((S, QK), BF16),
                   jax.ShapeDtypeStruct((HEADS, 1, DV), F32), jax.ShapeDtypeStruct((1, QK), F32)] + r_shapes,
        scratch_shapes=[pltpu.VMEM((HEADS, DV, DK), F32), pltpu.VMEM((cb, HEADS, DV, DK), F32)] + r_scratch,
        compiler_params=_cp(("arbitrary",), _vmem_limit(tiles, resident)),
    )(*_hbm(pa, alow, wg, bgate, gnorm, states, states, dy, *r_arrays))


SGU_TILE = 256


def _sgu_mask():
    r = lax.broadcasted_iota(jnp.int32, (SBLOCK, SBLOCK), 0)
    c = lax.broadcasted_iota(jnp.int32, (SBLOCK, SBLOCK), 1)
    return (c < CHUNK) | (r >= CHUNK)


def _ln_stats(vf):
    mu = jnp.mean(vf, axis=-1, keepdims=True)
    xc = vf - mu
    rs = lax.rsqrt(jnp.mean(xc * xc, axis=-1, keepdims=True) + EPS)
    return rs, xc * rs


def _sgu_fwd_call(ps, ln_g, ln_b, w_sp, b_sp, *, name):
    S = ps.shape[0]
    Ts = min(SGU_TILE, S)

    def body(ps_ref, lg_ref, lb_ref, w_ref, b_ref, y_ref):
        mask = _sgu_mask()
        for g in range(GROUPS):
            wm = jnp.where(mask, w_ref[g], 0.0).astype(BF16)
            for p in range(Ts // SBLOCK):
                rows = pl.ds(p * SBLOCK, SBLOCK)
                u = _gelu(ps_ref[rows, pl.ds(g * DG, DG)].astype(F32))
                _, xh = _ln_stats(_gelu(ps_ref[rows, pl.ds(D_MODEL + g * DG, DG)].astype(F32)))
                vn = xh * lg_ref[g] + lb_ref[g]
                mixed = _dot(wm, vn.astype(BF16), NN) + b_ref[g]
                y_ref[rows, pl.ds(g * DG, DG)] = (u * mixed).astype(BF16)

    full3 = lambda a, b, c: pl.BlockSpec((a, b, c), lambda t: (0, 0, 0))
    return pl.pallas_call(
        body, name=name, grid=(S // Ts,),
        in_specs=[pl.BlockSpec((Ts, 2 * D_MODEL), lambda t: (t, 0)),
                  full3(GROUPS, 1, DG), full3(GROUPS, 1, DG), full3(GROUPS, SBLOCK, SBLOCK), full3(GROUPS, SBLOCK, 1)],
        out_specs=pl.BlockSpec((Ts, D_MODEL), lambda t: (t, 0)),
        out_shape=pltpu.HBM((S, D_MODEL), BF16),
        compiler_params=_cp(("parallel",)),
    )(*_hbm(ps, ln_g, ln_b, w_sp, b_sp))


def _sgu_bwd_call(ps, ln_g, ln_b, w_sp, b_sp, dy, *, name, rider=None):
    S = ps.shape[0]
    Ts = min(SGU_TILE, S)
    r_arrays, r_in, r_out, r_shapes, r_scratch, r_tiles, r_resident = _rider_lists(rider)
    assert rider is None or rider["steps"] == S // Ts

    def body(ps_ref, lg_ref, lb_ref, w_ref, b_ref, dy_ref, ds_ref, dlg_ref, dlb_ref, dw_ref, db_ref):
        @pl.when(pl.program_id(0) == 0)
        def _():
            dlg_ref[...] = jnp.zeros_like(dlg_ref)
            dlb_ref[...] = jnp.zeros_like(dlb_ref)
            dw_ref[...] = jnp.zeros_like(dw_ref)
            db_ref[...] = jnp.zeros_like(db_ref)

        mask = _sgu_mask()
        for g in range(GROUPS):
            wm = jnp.where(mask, w_ref[g], 0.0).astype(BF16)
            lg = lg_ref[g]
            for p in range(Ts // SBLOCK):
                rows = pl.ds(p * SBLOCK, SBLOCK)
                su = ps_ref[rows, pl.ds(g * DG, DG)].astype(F32)
                sv = ps_ref[rows, pl.ds(D_MODEL + g * DG, DG)].astype(F32)
                u, du = _gelu_and_grad(su)
                gv, dgv = _gelu_and_grad(sv)
                rs, xh = _ln_stats(gv)
                vn16 = (xh * lg + lb_ref[g]).astype(BF16)
                mixed = _dot(wm, vn16, NN) + b_ref[g]
                dyv = dy_ref[rows, pl.ds(g * DG, DG)].astype(F32)
                ds_ref[rows, pl.ds(g * DG, DG)] = (dyv * mixed * du).astype(BF16)
                dmix = dyv * u
                dmix16 = dmix.astype(BF16)
                db_ref[g] += jnp.sum(dmix, axis=-1, keepdims=True)
                dw_ref[g] += jnp.where(mask, _dot(dmix16, vn16, NT), 0.0)
                dvn = _dot(wm, dmix16, TN)
                dlg_ref[g] += jnp.sum(dvn * xh, axis=0, keepdims=True)
                dlb_ref[g] += jnp.sum(dvn, axis=0, keepdims=True)
                dxh = dvn * lg
                dvf = rs * (dxh - jnp.mean(dxh, axis=-1, keepdims=True)
                            - xh * jnp.mean(dxh * xh, axis=-1, keepdims=True))
                ds_ref[rows, pl.ds(D_MODEL + g * DG, DG)] = (dvf * dgv).astype(BF16)

    full3 = lambda a, b, c: pl.BlockSpec((a, b, c), lambda t: (0, 0, 0))
    tiles = [((Ts, 2 * D_MODEL), BF16), ((Ts, D_MODEL), BF16), ((Ts, 2 * D_MODEL), BF16)] + r_tiles
    return pl.pallas_call(
        _with_rider(body, 6, 5, 0, rider), name=name, grid=(S // Ts,),
        in_specs=[pl.BlockSpec((Ts, 2 * D_MODEL), lambda t: (t, 0)),
                  full3(GROUPS, 1, DG), full3(GROUPS, 1, DG), full3(GROUPS, SBLOCK, SBLOCK), full3(GROUPS, SBLOCK, 1),
                  pl.BlockSpec((Ts, D_MODEL), lambda t: (t, 0))] + r_in,
        out_specs=[pl.BlockSpec((Ts, 2 * D_MODEL), lambda t: (t, 0)),
                   full3(GROUPS, 1, DG), full3(GROUPS, 1, DG), full3(GROUPS, SBLOCK, SBLOCK), full3(GROUPS, SBLOCK, 1)]
        + r_out,
        out_shape=[pltpu.HBM((S, 2 * D_MODEL), BF16),
                   jax.ShapeDtypeStruct((GROUPS, 1, DG), F32), jax.ShapeDtypeStruct((GROUPS, 1, DG), F32),
                   jax.ShapeDtypeStruct((GROUPS, SBLOCK, SBLOCK), F32),
                   jax.ShapeDtypeStruct((GROUPS, SBLOCK, 1), F32)] + r_shapes,
        scratch_shapes=r_scratch,
        compiler_params=_cp(("arbitrary",), _vmem_limit(tiles, r_resident)),
    )(*_hbm(ps, ln_g, ln_b, w_sp, b_sp, dy, *r_arrays))


def _position():
    return lax.axis_index("x"), lax.axis_index("y"), lax.axis_index("c")


def _gather_copies(srcs, dsts, send_sems, recv_sems, local_sems, relay=False):
    if relay:
        return _gather_copies_relayed(srcs, dsts, send_sems, recv_sems, local_sems)
    x, y, c = _position()
    me, sibling = (x, y, c), (x, y, 1 - c)
    chips = [(1 - x, y), (x, 1 - y), (1 - x, 1 - y)]
    n = len(srcs)

    def slab(a, block):
        px, py, pc = block
        return dsts[a].at[4 * px + 2 * py + pc]

    def copy(a, k, block, to, src=None):
        return pltpu.make_async_remote_copy(
            src_ref=slab(a, block) if src is None else src, dst_ref=slab(a, block),
            send_sem=send_sems.at[7 * a + k], recv_sem=recv_sems.at[7 * a + k], device_id=to, device_id_type=MESH)

    mine = [pltpu.make_async_copy(srcs[a], slab(a, me), local_sems.at[a]) for a in range(n)]
    for cp in mine:
        cp.start()
    first = []
    for a in range(n):
        first.append(copy(a, 0, me, sibling, src=srcs[a]))
        first += [copy(a, 1 + j, me, (*chip, c), src=srcs[a]) for j, chip in enumerate(chips)]
    for cp in first:
        cp.start()
    passed = []
    for j, chip in enumerate(chips):
        for a in range(n):
            copy(a, 1 + j, (*chip, c), me).wait_recv()
            cp = copy(a, 4 + j, (*chip, c), sibling)
            cp.start()
            passed.append(cp)
    for a in range(n):
        copy(a, 0, sibling, me).wait_recv()
        for j, chip in enumerate(chips):
            copy(a, 4 + j, (*chip, 1 - c), me).wait_recv()
    for cp in first + passed:
        cp.wait_send()
    for cp in mine:
        cp.wait()


def _gather_copies_relayed(srcs, dsts, send_sems, recv_sems, local_sems):
    x, y, c = _position()
    me, sibling = (x, y, c), (x, y, 1 - c)
    x_chip, y_chip, d_chip = (1 - x, y), (x, 1 - y), (1 - x, 1 - y)
    south = c == 0
    relay_from = (jnp.where(south, x, 1 - x), jnp.where(south, 1 - y, y), c)
    relay_to = (jnp.where(south, 1 - x, x), jnp.where(south, y, 1 - y), c)
    n = len(srcs)

    def slab(a, block):
        px, py, pc = block
        return dsts[a].at[4 * px + 2 * py + pc]

    def copy(a, k, block, to, src=None):
        return pltpu.make_async_remote_copy(
            src_ref=slab(a, block) if src is None else src, dst_ref=slab(a, block),
            send_sem=send_sems.at[7 * a + k], recv_sem=recv_sems.at[7 * a + k], device_id=to, device_id_type=MESH)

    mine = [pltpu.make_async_copy(srcs[a], slab(a, me), local_sems.at[a]) for a in range(n)]
    for cp in mine:
        cp.start()
    sent = []
    for a in range(n):
        sent += [copy(a, 0, me, sibling, src=srcs[a]), copy(a, 1, me, (*x_chip, c), src=srcs[a]),
                 copy(a, 2, me, (*y_chip, c), src=srcs[a])]
    for cp in sent:
        cp.start()
    for a in range(n):
        copy(a, 1, (*x_chip, c), me).wait_recv()
        copy(a, 2, (*y_chip, c), me).wait_recv()
        later = [copy(a, 3, relay_from, relay_to), copy(a, 4, (*x_chip, c), sibling), copy(a, 5, (*y_chip, c), sibling)]
        for cp in later:
            cp.start()
        sent += later
    for a in range(n):
        copy(a, 3, (*d_chip, c), me).wait_recv()
        cp = copy(a, 6, (*d_chip, c), sibling)
        cp.start()
        sent.append(cp)
    for a in range(n):
        copy(a, 0, sibling, me).wait_recv()
        for k, chip in ((4, x_chip), (5, y_chip), (6, d_chip)):
            copy(a, k, (*chip, 1 - c), me).wait_recv()
    for cp in sent:
        cp.wait_send()
    for cp in mine:
        cp.wait()


def _all_gather_hbm(shards, *, name):
    n = len(shards)

    def body(*refs):
        srcs, dsts = refs[:n], refs[n:2 * n]
        send_sems, recv_sems, local_sems = refs[2 * n:]
        _gather_copies(srcs, dsts, send_sems, recv_sems, local_sems, relay=True)

    return pl.pallas_call(
        body, name=name,
        in_specs=[ANY] * n, out_specs=[ANY] * n,
        out_shape=[jax.ShapeDtypeStruct((N_DEV, *s.shape), s.dtype) for s in shards],
        scratch_shapes=_comm_sems(n),
    )(*shards)


def _all_reduce_small(part, *, name):
    R, W = part.shape

    def body(x_ref, out_ref, gathered, send_sems, recv_sems, local_sems):
        _gather_copies([x_ref], [gathered], send_sems, recv_sems, local_sems)
        acc = gathered[0]
        for d in range(1, N_DEV):
            acc = acc + gathered[d]
        out_ref[...] = acc

    return pl.pallas_call(
        body, name=name,
        in_specs=[VMEM_SPEC], out_specs=VMEM_SPEC,
        out_shape=jax.ShapeDtypeStruct((R, W), F32),
        scratch_shapes=[pltpu.VMEM((N_DEV, R, W), F32),
                        pltpu.SemaphoreType.DMA((7,)), pltpu.SemaphoreType.DMA((7,)), pltpu.SemaphoreType.DMA((1,))],
    )(part)


FLIPS = [(fx, fy, fc) for fx in (0, 1) for fy in (0, 1) for fc in (0, 1)][1:]


def _scatter_copies(srcs, dsts, send_sems, recv_sems, local_sems, waves=1):
    n = len(srcs)
    x, y, c = _position()
    me = 4 * x + 2 * y + c
    mine = [pltpu.make_async_copy(srcs[a].at[me], dsts[a].at[me], local_sems.at[a]) for a in range(n)]
    for cp in mine:
        cp.start()
    peers = []
    for fx, fy, fc in FLIPS:
        tx = 1 - x if fx else x
        ty = 1 - y if fy else y
        tc = 1 - c if fc else c
        peers.append(((tx, ty, tc), 4 * tx + 2 * ty + tc))
    copies = []
    for w in range(waves):
        wave = []
        for k, (peer_id, peer) in enumerate(peers):
            for a in range(n):
                rows = srcs[a].shape[1]
                step = rows if waves == 1 else (rows // waves) // 16 * 16
                r0 = w * step
                cut = pl.ds(r0, step if w < waves - 1 else rows - r0)
                sem = (7 * a + k) * waves + w
                cp = pltpu.make_async_remote_copy(
                    src_ref=srcs[a].at[peer, cut], dst_ref=dsts[a].at[me, cut],
                    send_sem=send_sems.at[sem], recv_sem=recv_sems.at[sem],
                    device_id=peer_id, device_id_type=MESH)
                cp.start()
                wave.append(cp)
        for cp in wave:
            cp.wait_send()
        copies += wave
    for cp in copies:
        cp.wait_recv()
    for cp in mine:
        cp.wait()


def _pair_copies(srcs, dsts, send_sems, recv_sems, local_sems):
    x, y, c = _position()
    copies = [pltpu.make_async_remote_copy(
        src_ref=srcs[a], dst_ref=dsts[a], send_sem=send_sems.at[a], recv_sem=recv_sems.at[a],
        device_id=(x, y, 1 - c), device_id_type=MESH) for a in range(len(srcs))]
    for cp in copies:
        cp.start()
    for cp in copies:
        cp.wait()


def _chip_scatter_copies(srcs, dsts, send_sems, recv_sems, local_sems, waves=1):
    n = len(srcs)
    x, y, c = _position()
    here = 2 * x + y
    mine = [pltpu.make_async_copy(srcs[a].at[here], dsts[a].at[here], local_sems.at[a]) for a in range(n)]
    for cp in mine:
        cp.start()
    peers = []
    for fx, fy in ((1, 0), (0, 1), (1, 1)):
        tx = 1 - x if fx else x
        ty = 1 - y if fy else y
        peers.append(((tx, ty, c), 2 * tx + ty))
    copies = []
    for w in range(waves):
        wave = []
        for k, (peer_id, peer) in enumerate(peers):
            for a in range(n):
                rows = srcs[a].shape[1]
                step = rows if waves == 1 else (rows // waves) // 16 * 16
                r0 = w * step
                cut = pl.ds(r0, step if w < waves - 1 else rows - r0)
                sem = (3 * a + k) * waves + w
                cp = pltpu.make_async_remote_copy(
                    src_ref=srcs[a].at[peer, cut], dst_ref=dsts[a].at[here, cut],
                    send_sem=send_sems.at[sem], recv_sem=recv_sems.at[sem],
                    device_id=peer_id, device_id_type=MESH)
                cp.start()
                wave.append(cp)
        for cp in wave:
            cp.wait_send()
        copies += wave
    for cp in copies:
        cp.wait_recv()
    for cp in mine:
        cp.wait()


def _comm_sems(n, waves=1):
    return [pltpu.SemaphoreType.DMA((7 * n * waves,)), pltpu.SemaphoreType.DMA((7 * n * waves,)),
            pltpu.SemaphoreType.DMA((n,))]


def _scatter_blocks(parts, *, name, waves=1):
    n = len(parts)

    def body(*refs):
        _scatter_copies(refs[:n], refs[n:2 * n], *refs[2 * n:], waves=waves)

    return pl.pallas_call(
        body, name=name,
        in_specs=[ANY] * n, out_specs=[ANY] * n,
        out_shape=[jax.ShapeDtypeStruct(p.shape, p.dtype) for p in parts],
        scratch_shapes=_comm_sems(n, waves),
    )(*parts)


def _handshake(peers):
    barrier = pltpu.get_barrier_semaphore()
    for peer in peers:
        pl.semaphore_signal(barrier, inc=1, device_id=peer, device_id_type=MESH)
    pl.semaphore_wait(barrier, len(peers))


def _sequencer_call(arrays, out_types, copies_fn, peers_fn, *, name, collective_id, waves=1):
    n = len(arrays)
    srcs = [jax.new_ref(a, memory_space=pltpu.MemorySpace.HBM) for a in arrays]
    dsts = [jax.empty_ref(t, memory_space=pltpu.MemorySpace.HBM) for t in out_types]
    extra = {} if waves == 1 else {"waves": waves}

    @pl.kernel(mesh=plsc.ScalarSubcoreMesh(axis_name="sequencer", num_cores=1), name=name,
               scratch_types=_comm_sems(n, waves), compiler_params=pltpu.CompilerParams(collective_id=collective_id))
    def launch(send_sems, recv_sems, local_sems):
        _handshake(peers_fn())
        copies_fn(srcs, dsts, send_sems, recv_sems, local_sems, **extra)

    launch()
    return [d[...] for d in dsts]


def _all_other_devices():
    x, y, c = _position()
    return [(1 - x if fx else x, 1 - y if fy else y, 1 - c if fc else c) for fx, fy, fc in FLIPS]


def _gather_relay_peers():
    x, y, c = _position()
    return [(x, y, 1 - c), (1 - x, y, c), (x, 1 - y, c)]


def _sibling():
    x, y, c = _position()
    return [(x, y, 1 - c)]


def _same_core_of_other_chips():
    x, y, c = _position()
    return [(1 - x, y, c), (x, 1 - y, c), (1 - x, 1 - y, c)]


def _pair_exchange_async(parts, *, name, collective_id):
    return _sequencer_call(parts, [jax.ShapeDtypeStruct(p.shape, p.dtype) for p in parts],
                           _pair_copies, _sibling, name=name, collective_id=collective_id)


def _chip_scatter_async(parts, *, name, collective_id, waves=1):
    return _sequencer_call(parts, [jax.ShapeDtypeStruct(p.shape, p.dtype) for p in parts],
                           _chip_scatter_copies, _same_core_of_other_chips, name=name, collective_id=collective_id,
                           waves=waves)


def _pair_sum_call(mine, theirs, *, name, tw=2 * LANES):
    n, r, w = mine.shape

    def body(a_ref, b_ref, o_ref):
        o_ref[...] = (a_ref[...].astype(F32) + b_ref[...].astype(F32)).astype(o_ref.dtype)

    spec = pl.BlockSpec((None, r, tw), lambda i, j: (i, 0, j))
    return pl.pallas_call(
        body, name=name, grid=(n, w // tw), in_specs=[spec, spec], out_specs=spec,
        out_shape=pltpu.HBM(mine.shape, mine.dtype),
        compiler_params=_cp(("parallel", "parallel")),
    )(*_hbm(mine, theirs))


def _scatter_blocks_async(parts, *, name, collective_id, waves=1):
    return _sequencer_call(parts, [jax.ShapeDtypeStruct(p.shape, p.dtype) for p in parts],
                           _scatter_copies, _all_other_devices, name=name, collective_id=collective_id, waves=waves)


def _all_gather_async(shards, *, name, collective_id):
    return _sequencer_call(shards, [jax.ShapeDtypeStruct((N_DEV, *s.shape), s.dtype) for s in shards],
                           _gather_copies_relayed, _gather_relay_peers, name=name, collective_id=collective_id)


def _adamw_math(w, g, m, v):
    m = ADAM_B1 * m + (1.0 - ADAM_B1) * g
    v = ADAM_B2 * v + (1.0 - ADAM_B2) * (g * g)
    m_hat = m / (1.0 - ADAM_B1 ** ADAM_STEP)
    v_hat = v / (1.0 - ADAM_B2 ** ADAM_STEP)
    delta = -ADAM_LR * (m_hat / (jnp.sqrt(v_hat) + ADAM_EPS) + ADAM_WD * w)
    return delta, m, v


def _adamw_reduce_call(recv, w, m, v, *, name, T=128):
    R, W = w.shape
    n_parts = recv.shape[0]
    if R % T == 0:
        tr, tw = T, W
    elif (R // 2) % 16 == 0:
        tr, tw = R // 2, W
    else:
        tr, tw = R, 2 * LANES

    def body(p_ref, w_ref, m_ref, v_ref, g_out, d_out, m_out, v_out):
        g = p_ref[0].astype(F32)
        for d in range(1, n_parts):
            g = g + p_ref[d].astype(F32)
        g_out[...] = g
        d_out[...], m_out[...], v_out[...] = _adamw_math(w_ref[...], g, m_ref[...], v_ref[...])

    row = pl.BlockSpec((tr, tw), lambda i, j: (i, j))
    out = pltpu.HBM((R, W), F32)
    return pl.pallas_call(
        body, name=name, grid=(R // tr, W // tw),
        in_specs=[pl.BlockSpec((n_parts, tr, tw), lambda i, j: (0, i, j)), row, row, row],
        out_specs=[row] * 4, out_shape=[out] * 4,
        compiler_params=_cp(("parallel", "parallel"), VMEM_BIG),
    )(*_hbm(recv, w, m, v))


SMALL_EARLY = (("b_gate", 8), ("w_spatial", 512), ("b_spatial", 8), ("norm_post_mix", 8), ("norm_pre_ffn", 8),
               ("norm_post_ffn", 8), ("w_gate_up", 128), ("gla_norm", 64), ("sgu_ln_g", 64), ("sgu_ln_b", 64))
SMALL_EARLY_AT = {}
for _name, _rows in SMALL_EARLY:
    SMALL_EARLY_AT[_name] = sum(r for _, r in SMALL_EARLY[:len(SMALL_EARLY_AT)])
SMALL_EARLY_ROWS = sum(r for _, r in SMALL_EARLY)
SMALL_LATE_ROWS = 16


def _small_early_rows(grads):
    def rows(a, n_rows):
        a = a.reshape(-1, LANES)
        return jnp.pad(a, ((0, n_rows - a.shape[0]), (0, 0)))

    def device_major(a, n_rows):
        r, c = a.shape[0], a.shape[1] // N_DEV
        a = a.reshape(r, N_DEV, c).transpose(1, 0, 2)
        a = jnp.pad(a, ((0, 0), (0, n_rows // N_DEV - r), (0, LANES - c)))
        return a.reshape(n_rows, LANES)

    pieces = []
    for name, n_rows in SMALL_EARLY:
        g = grads[name]
        if name in SMALL_SHARDED:
            pieces.append(device_major(g.reshape(g.shape[0], -1) if g.ndim == 2 else g.reshape(g.shape[0], g.shape[-1]), n_rows))
        else:
            pieces.append(rows(g, n_rows))
    return jnp.concatenate(pieces, axis=0)


def _small_update_call(early, dg_pre_mix, loss_part, w, m, v, *, name):
    names = list(SMALL)
    n_p = len(names)
    E = SMALL_EARLY_ROWS

    def reduce_body(early_ref, dg1_ref, loss_ref, tot, got_early, got_late, late, send_sems, recv_sems, local_sems):
        for k in range(D_MODEL // LANES):
            late[k:k + 1, :] = dg1_ref[:, k * LANES:(k + 1) * LANES]
        late[8:16, :] = jnp.broadcast_to(loss_ref[...], (8, LANES))
        _gather_copies([early_ref, late], [got_early, got_late], send_sems, recv_sems, local_sems)
        acc, acc_late = got_early[0], got_late[0]
        for d in range(1, N_DEV):
            acc, acc_late = acc + got_early[d], acc_late + got_late[d]
        tot[0:E, :] = acc
        tot[E:E + SMALL_LATE_ROWS, :] = acc_late

    total = pl.pallas_call(
        reduce_body, name=name + "_reduce",
        in_specs=[VMEM_SPEC] * 3, out_specs=VMEM_SPEC,
        out_shape=jax.ShapeDtypeStruct((E + SMALL_LATE_ROWS, LANES), F32),
        scratch_shapes=[pltpu.VMEM((N_DEV, E, LANES), F32), pltpu.VMEM((N_DEV, SMALL_LATE_ROWS, LANES), F32),
                        pltpu.VMEM((SMALL_LATE_ROWS, LANES), F32),
                        pltpu.SemaphoreType.DMA((14,)), pltpu.SemaphoreType.DMA((14,)), pltpu.SemaphoreType.DMA((2,))],
    )(early, dg_pre_mix, loss_part)

    def body(tot, *rest):
        w_refs, m_refs, v_refs = [dict(zip(names, rest[i * n_p:(i + 1) * n_p])) for i in range(3)]
        outs = rest[3 * n_p:]
        loss_out = outs[0]
        g_out, d_out, m_out, v_out = [dict(zip(names, outs[1 + i * n_p:1 + (i + 1) * n_p])) for i in range(4)]
        loss_out[...] = tot[E + 8:E + 9, :]

        x, y, c = _position()
        me = 4 * x + 2 * y + c

        def update(name, g, ix):
            g_out[name][ix] = g
            d_out[name][ix], m_out[name][ix], v_out[name][ix] = _adamw_math(
                w_refs[name][ix], g, m_refs[name][ix], v_refs[name][ix])

        for name in names:
            shape = w[name].shape
            if name in SMALL_SHARDED:
                per_dev = dict(SMALL_EARLY)[name] // N_DEV
                at = pl.multiple_of(SMALL_EARLY_AT[name] + me * per_dev, 8)
                g = tot[pl.ds(at, per_dev), :]
                update(name, g[:shape[1], :shape[2]], (0,))
            elif name == "w_spatial":
                for grp in range(GROUPS):
                    at = SMALL_EARLY_AT[name] + grp * SBLOCK
                    update(name, tot[at:at + SBLOCK, :], (0, grp))
            elif name == "b_spatial":
                at = SMALL_EARLY_AT[name]
                update(name, tot[at:at + GROUPS, :], (0,))
            else:
                at = E if name == "norm_pre_mix" else SMALL_EARLY_AT[name]
                for k in range(shape[1] // LANES):
                    update(name, tot[at + k:at + k + 1, :], (slice(None), pl.ds(k * LANES, LANES)))

    state = [s[n] for s in (w, m, v) for n in names]
    out_shapes = [jax.ShapeDtypeStruct((1, LANES), F32)] + [jax.ShapeDtypeStruct(w[n].shape, F32) for n in names] * 4
    outs = pl.pallas_call(
        body, name=name + "_adamw",
        in_specs=[VMEM_SPEC] * (1 + len(state)), out_specs=[VMEM_SPEC] * len(out_shapes), out_shape=out_shapes,
    )(total, *state)
    per_name = {n: tuple(outs[1 + i * n_p + j] for i in range(4)) for j, n in enumerate(names)}
    return outs[0], per_name


def _tile_rows(n_elems):
    return -(-n_elems // (8 * LANES)) * 8


def _pack_rows(parts, rows):
    pieces = []
    for p in parts:
        q = p.reshape(-1, LANES)
        pieces.append(jnp.pad(q, ((0, _tile_rows(p.size) - q.shape[0]), (0, 0))))
    buf = jnp.concatenate(pieces, axis=0)
    return jnp.pad(buf, ((0, rows - buf.shape[0]), (0, 0)))


def _unpack_rows(buf, shapes):
    out, off = [], 0
    for shp in shapes:
        n = 1
        for s in shp:
            n *= s
        out.append(buf[off:off + n // LANES].reshape(shp))
        off += _tile_rows(n)
    return out


def _in_w_blocks_call(a, live, finished, *, name, tk=TOKEN_TILE):
    S = a.shape[0]
    tk = min(tk, S)
    steps = S // tk
    n_live = len(live)
    n_chips = N_DEV // 2
    out_shape = (n_chips, IN_BLK, D_MODEL)

    def body(a_ref, *rest):
        live_refs, done_refs = rest[:n_live], rest[n_live:n_live + len(finished)]
        keep_ref, send_ref = rest[n_live + len(finished):n_live + len(finished) + 2]
        accs = rest[n_live + len(finished) + 2:]
        k = pl.program_id(0)

        @pl.when(k == 0)
        def _():
            for acc in accs:
                acc[...] = jnp.zeros_like(acc)

        av = a_ref[...]
        for src, acc in zip(live_refs, accs):
            for m0 in range(0, src.shape[1], 1024):
                m1 = min(src.shape[1], m0 + 1024)
                acc[m0:m1, :] += _dot(src[:, m0:m1], av, TN)

        @pl.when(k == steps - 1)
        def _():
            groups = [(acc, cols) for acc, (_, cols) in zip(accs, live)]
            groups += [(ref, cols) for ref, (_, cols) in zip(done_refs, finished)]
            core = lax.axis_index("c")

            def cut(d, out_ref):
                lo, hi = IN_BLK * d, IN_BLK * (d + 1)
                for ref, (c0, c1) in groups:
                    s0, e0 = max(lo, c0), min(hi, c1)
                    if s0 < e0:
                        out_ref[d // 2, s0 - lo:e0 - lo, :] = ref[s0 - c0:e0 - c0, :].astype(BF16)

            for d in range(N_DEV):
                pl.when(core == d % 2)(lambda d=d: cut(d, keep_ref))
                pl.when(core != d % 2)(lambda d=d: cut(d, send_ref))

    acc_shapes = [(arr.shape[1], D_MODEL) for arr, _ in live]
    tiles = [((tk, D_MODEL), BF16)] + [((tk, arr.shape[1]), BF16) for arr, _ in live]
    resident = ([(arr.shape, arr.dtype) for arr, _ in finished] + [(out_shape, BF16)] * 2
                + [(s, F32) for s in acc_shapes])
    return pl.pallas_call(
        body, name=name, grid=(steps,),
        in_specs=[pl.BlockSpec((tk, D_MODEL), lambda k: (k, 0))]
        + [pl.BlockSpec((tk, arr.shape[1]), lambda k: (k, 0)) for arr, _ in live]
        + [_res(arr.shape) for arr, _ in finished],
        out_specs=[_acc(out_shape)] * 2,
        out_shape=[pltpu.HBM(out_shape, BF16)] * 2,
        scratch_shapes=[pltpu.VMEM(s, F32) for s in acc_shapes],
        compiler_params=_cp(("arbitrary",), _vmem_limit(tiles, resident)),
    )(*_hbm(a, *[arr for arr, _ in live], *[arr for arr, _ in finished]))


def _local_step(x, target, W, scatter, finish, pair, chip_scatter):
    g1, g2, g3, g4 = [W[n].reshape(1, D_MODEL) for n in ("norm_pre_mix", "norm_post_mix", "norm_pre_ffn", "norm_post_ffn")]
    wfi, wfo = W["w_ffn_in"], W["w_ffn_out"].reshape(4, FF_BLK, D_MODEL)
    wg = jnp.pad(W["w_gate_up"], ((0, LANES - RANK), (0, 0))).astype(BF16)
    bgate = W["b_gate"].reshape(1, QK)
    gnorm = W["gla_norm"].reshape(HEADS, 1, DV)
    ln_g = W["sgu_ln_g"].reshape(GROUPS, 1, DG)
    ln_b = W["sgu_ln_b"].reshape(GROUPS, 1, DG)
    w_sp = W["w_spatial"]
    b_sp = W["b_spatial"].reshape(GROUPS, SBLOCK, 1)

    a, pa, alow, ps, pg, w_in_t = _in_proj_call(x, g1, W["w_in_blocks"], name="in_proj")
    y_gla, states = _gla_fwd_call(pa, alow, wg, bgate, gnorm, name="gla_fwd")
    y_sgu = _sgu_fwd_call(ps, ln_g, ln_b, w_sp, b_sp, name="sgu_fwd")
    t1, t2, merged, mix, x1, h = _mixer_tail_call(y_gla, y_sgu, pg, x, W["w_branch_gla"], W["w_branch_sgu"],
                                                  W["w_out"], g2, g3, name="mixer_tail")
    gu, z = _ffn_in_call(h, wfi, name="ffn_in")
    loss, dx2, dy, dg4 = _ffn_out_loss_call(z, wfo, x1, target, g4, name="ffn_out_loss")

    grads = {"norm_post_ffn": dg4}
    done = {}
    dgu = _ffn_out_bwd_call(dy, wfo, gu, name="ffn_out_bwd")
    dw_ffn_out = _weight_grads_call([z, dy], [(0, 1)], name="d_ffn_out_w")[0].reshape(N_DEV, D_FF // N_DEV, D_MODEL)
    dw_ffn_in = _ffn_in_grad_call(h, dgu, name="d_ffn_in_w")
    dgu, dw_ffn_out, dw_ffn_in = lax.optimization_barrier((dgu, dw_ffn_out, dw_ffn_in))
    ffn_received = scatter(("w_ffn_out", "w_ffn_in"), [dw_ffn_out, dw_ffn_in])
    dx1, dmix, grads["norm_pre_ffn"], grads["norm_post_mix"] = _ffn_in_bwd_call(dgu, wfi, dx2, x1, mix, g3, g2, name="ffn_in_bwd")
    dt1, dt2, dpg, dy_gla, dy_sgu = _mixer_bwd_call(dmix, t1, t2, pg, W["w_out"], W["w_branch_gla"], W["w_branch_sgu"],
                                                    name="mixer_bwd")
    rows = D_MODEL // N_DEV
    mixer_grads = _weight_grads_call([merged, dmix, y_gla, dt1, y_sgu, dt2], [(0, 1), (2, 3), (4, 5)], name="d_mixer_w")
    dy_gla, dy_sgu, mixer_grads = lax.optimization_barrier((dy_gla, dy_sgu, mixer_grads))
    mixer_received = scatter(("w_out", "w_branch_gla", "w_branch_sgu"),
                             [g.reshape(N_DEV, rows, D_MODEL) for g in mixer_grads])
    dps, dlg, dlb, dwsp, dbsp = _sgu_bwd_call(ps, ln_g, ln_b, w_sp, b_sp, dy_sgu, name="sgu_bwd")
    dw_s, dw_g = _weight_grads_call([a, dps, dpg], [(1, 0), (2, 0)], name="d_in_w_sgu_gates")
    dy_gla, dw_s, dw_g = lax.optimization_barrier((dy_gla, dw_s, dw_g))
    dpa, dlogit, dgn, dbg = _gla_bwd_call(pa, alow, wg, bgate, gnorm, states, dy_gla, name="gla_bwd")
    ffn_received, mixer_received, dpa, dlogit = lax.optimization_barrier((ffn_received, mixer_received, dpa, dlogit))
    dlow = _mm(dlogit, wg, "nt", BF16, name="d_gate_up_x")
    keep, send = _in_w_blocks_call(a, [(dpa, A_COLS), (dlow, LOW_COLS)], [(dw_s, S_COLS), (dw_g, G_COLS)],
                                   name="d_in_w_blocks")
    ffn_received, mixer_received, send = lax.optimization_barrier((ffn_received, mixer_received, send))
    from_sibling = pair(send)
    done.update({**finish(ffn_received), **finish(mixer_received)})
    dwg = _mm(alow, dlogit, "tn", F32, name="d_gate_up_w")
    done, dwg, keep = lax.optimization_barrier((done, dwg, keep))
    chip_sum = _pair_sum_call(keep, from_sibling, name="pair_sum_w_in")
    chip_sum, dpa = lax.optimization_barrier((chip_sum, dpa))
    in_received = chip_scatter(chip_sum)
    grad_x, grads["norm_pre_mix"] = _in_proj_bwd_call(dpa, dlow, dps, dpg, w_in_t, x, dx1, g1, name="in_proj_bwd")

    grads["w_gate_up"] = dwg[:RANK]
    grads["b_gate"] = dbg
    grads["gla_norm"] = dgn
    grads["sgu_ln_g"] = dlg
    grads["sgu_ln_b"] = dlb
    grads["w_spatial"] = dwsp
    grads["b_spatial"] = dbsp
    done.update(finish({"w_in": in_received}))
    return loss, grad_x, grads, done


WEIGHTS = ("norm_pre_mix", "w_in", "w_gate_up", "b_gate", "gla_norm", "sgu_ln_g", "sgu_ln_b", "w_spatial",
           "b_spatial", "w_branch_gla", "w_branch_sgu", "w_out", "norm_post_mix", "norm_pre_ffn", "w_ffn_in",
           "w_ffn_out", "norm_post_ffn")
BIG = ("w_in", "w_branch_gla", "w_branch_sgu", "w_out", "w_ffn_in", "w_ffn_out")
MIXER = ("w_branch_gla", "w_branch_sgu", "w_out")
FFN = ("w_ffn_in", "w_ffn_out")
COLUMN_SHARDED = ("w_in", "w_ffn_in")
LATE_SCATTER_WAVES = 4
SMALL = tuple(n for n in WEIGHTS if n not in BIG)
SMALL_SHARDED = ("w_gate_up", "gla_norm", "sgu_ln_g", "sgu_ln_b")
SMALL_FULL = {"norm_pre_mix": (1024,), "w_gate_up": (16, 512), "b_gate": (512,), "gla_norm": (4, 256),
              "sgu_ln_g": (4, 256), "sgu_ln_b": (4, 256), "w_spatial": (4, 128, 128), "b_spatial": (4, 128),
              "norm_post_mix": (1024,), "norm_pre_ffn": (1024,), "norm_post_ffn": (1024,)}
SMALL_GRAD_ROWS = 648
SMALL_STATE_ROWS = 600
SMALL_GATHER_ROWS = 32


def kernel(x, norm_pre_mix, w_in, w_gate_up, b_gate, gla_norm, sgu_ln_g, sgu_ln_b, w_spatial, b_spatial, w_branch_gla, w_branch_sgu, w_out, norm_post_mix, norm_pre_ffn, w_ffn_in, w_ffn_out, norm_post_ffn, loss_target, m_norm_pre_mix, m_w_in, m_w_gate_up, m_b_gate, m_gla_norm, m_sgu_ln_g, m_sgu_ln_b, m_w_spatial, m_b_spatial, m_w_branch_gla, m_w_branch_sgu, m_w_out, m_norm_post_mix, m_norm_pre_ffn, m_w_ffn_in, m_w_ffn_out, m_norm_post_ffn, v_norm_pre_mix, v_w_in, v_w_gate_up, v_b_gate, v_gla_norm, v_sgu_ln_g, v_sgu_ln_b, v_w_spatial, v_b_spatial, v_w_branch_gla, v_w_branch_sgu, v_w_out, v_norm_post_mix, v_norm_pre_ffn, v_w_ffn_in, v_w_ffn_out, v_norm_post_ffn):
    given = dict(locals())
    def local(a, n):
        return a[0].T if n in COLUMN_SHARDED else a[0]

    w = {n: local(given[n], n) for n in WEIGHTS}
    m = {n: local(given["m_" + n], n) for n in WEIGHTS}
    v = {n: local(given["v_" + n], n) for n in WEIGHTS}
    xs, target = x[0], loss_target[0]
    me = 4 * lax.axis_index("x") + 2 * lax.axis_index("y") + lax.axis_index("c")

    small_shard = _pack_rows([w[n] for n in SMALL_SHARDED], SMALL_GATHER_ROWS)
    first = _all_gather_hbm([w["w_in"].astype(BF16), small_shard], name="gather_w_in")
    rest, first = lax.optimization_barrier(([w[n].astype(BF16) for n in MIXER + FFN], first))
    rest_blocks = _all_gather_async(rest, name="gather_rest", collective_id=1)
    W = {n: w[n] for n in SMALL if n not in SMALL_SHARDED}
    blocks = dict(zip(MIXER + FFN, rest_blocks))
    for n in ("w_branch_gla", "w_branch_sgu", "w_out", "w_ffn_out"):
        W[n] = blocks[n].reshape(-1, D_MODEL)
    W["w_ffn_in"] = blocks["w_ffn_in"]
    W["w_in_blocks"] = first[0]
    small_blocks = first[1]
    off = 0
    for n in SMALL_SHARDED:
        r, c = w[n].shape
        blk = small_blocks[:, off:off + r * c // LANES].reshape(N_DEV, r, c)
        W[n] = blk.transpose(1, 0, 2).reshape(r, N_DEV * c)
        off += _tile_rows(r * c)

    scatter_ids = iter((3, 4))

    def scatter(names, parts):
        got = _scatter_blocks_async(parts, name="scatter_" + "_".join(names), collective_id=next(scatter_ids))
        return dict(zip(names, got))

    def finish(received):
        return {n: _adamw_reduce_call(r, w[n], m[n], v[n], name="adamw_" + n) for n, r in received.items()}

    def pair(part):
        return _pair_exchange_async([part], name="pair_w_in", collective_id=5)[0]

    def chip_scatter(part):
        return _chip_scatter_async([part], name="scatter_w_in", collective_id=6, waves=LATE_SCATTER_WAVES)[0]

    loss_part, grad_x, grads, big_done = _local_step(xs, target, W, scatter, finish, pair, chip_scatter)

    loss_row, small_done = _small_update_call(
        _small_early_rows(grads), grads["norm_pre_mix"], loss_part,
        *[{n: given[prefix + n] for n in SMALL} for prefix in ("", "m_", "v_")], name="small_update")

    def pick(i):
        return [small_done[n][i] if n in SMALL else (big_done[n][i].T if n in COLUMN_SHARDED else big_done[n][i])[None]
                for n in WEIGHTS]

    return (loss_row[0, 0], grad_x[None], *pick(0), *pick(1), *pick(2), *pick(3))
```

```python
import jax
import jax.numpy as jnp
from jax import lax
from jax.experimental import pallas as pl
from jax.experimental.pallas import tpu as pltpu
from jax.experimental.pallas import tpu_sc as plsc

F32 = jnp.float32
BF16 = jnp.bfloat16

D_MODEL = 1024
N_DEV = 8
CHUNK = 64
HEADS = 4
DK = 128
DV = 256
QK = HEADS * DK
GV = HEADS * DV
RANK = 16
GROUPS = 4
SBLOCK = 128
DG = 256
D_FF = 2816
FF_BLK = 704
EPS = 1e-6
Q_SCALE = DK ** -0.5
LANES = 128
VMEM_BIG = 48 * 1024 * 1024

D_IN = 7184
IN_BLK = 898
A_COLS = (0, 3072)
LOW_COLS = (3072, 3088)
S_COLS = (3088, 5136)
G_COLS = (5136, 7184)

ADAM_LR = 0.001
ADAM_B1 = 0.9
ADAM_B2 = 0.999
ADAM_EPS = 1e-08
ADAM_WD = 0.01
ADAM_STEP = 10

MESH = pl.DeviceIdType.MESH
ANY = pl.BlockSpec(memory_space=pl.ANY)
VMEM_SPEC = pl.BlockSpec(memory_space=pltpu.VMEM)


def _cp(sem=None, vmem=None):
    return pltpu.CompilerParams(dimension_semantics=sem, vmem_limit_bytes=vmem)


def _hbm(*arrays):
    return [pltpu.with_memory_space_constraint(a, pltpu.HBM) for a in arrays]


def _sig(x):
    return 0.5 * jnp.tanh(0.5 * x) + 0.5


GELU_C = 0.7978845608028654
GELU_A = 0.044715


def _gelu(x):
    t = jnp.tanh((GELU_C * x) * (1.0 + GELU_A * (x * x)))
    return (0.5 * x) * (1.0 + t)


def _gelu_and_grad(x):
    x2 = x * x
    t = jnp.tanh((GELU_C * x) * (1.0 + GELU_A * x2))
    one_t = 1.0 + t
    hx = 0.5 * x
    grad = 0.5 * one_t + (hx * (1.0 - t * t)) * (GELU_C + (3.0 * GELU_A * GELU_C) * x2)
    return hx * one_t, grad


def _logsig(x):
    return jnp.minimum(x, 0.0) - jnp.log1p(jnp.exp(-jnp.abs(x)))


def _dot(a, b, dims):
    return lax.dot_general(a, b, (dims, ((), ())), preferred_element_type=F32)


NN = ((1,), (0,))
NT = ((1,), (1,))
TN = ((0,), (0,))


def _exact_mask_dot(mask_bf16, x):
    hi = x.astype(BF16)
    r1 = x - hi.astype(F32)
    mid = r1.astype(BF16)
    lo = (r1 - mid.astype(F32)).astype(BF16)
    return _dot(mask_bf16, hi, NN) + _dot(mask_bf16, mid, NN) + _dot(mask_bf16, lo, NN)


def _pick_tile(dim, target):
    if dim <= target:
        return dim
    best = None
    for t in range(LANES, int(1.4 * target) + 1, LANES):
        if dim % t == 0:
            best = t
    assert best is not None, (dim, target)
    return best


def _mm_call(a, b, *, name, grid, a_spec, b_spec, o_spec, out_shape, dims, acc_shape):
    nk = grid[2]

    def body(a_ref, b_ref, o_ref, *acc):
        part = _dot(a_ref[...], b_ref[...], dims)
        if nk == 1:
            o_ref[...] = part.astype(o_ref.dtype)
        else:
            acc_ref = acc[0]
            k = pl.program_id(2)

            @pl.when(k == 0)
            def _():
                acc_ref[...] = part

            @pl.when(k > 0)
            def _():
                acc_ref[...] += part

            @pl.when(k == nk - 1)
            def _():
                o_ref[...] = acc_ref[...].astype(o_ref.dtype)

    return pl.pallas_call(
        body, name=name, grid=grid, in_specs=[a_spec, b_spec], out_specs=o_spec, out_shape=out_shape,
        scratch_shapes=[] if nk == 1 else [pltpu.VMEM(acc_shape, F32)],
        compiler_params=_cp(("parallel", "parallel", "arbitrary"), VMEM_BIG),
    )(a, b)


def _mm(a, b, mode, out_dtype, *, name, tm=512, tn=1024, tk=1024):
    if mode == "nn":
        (M, K), (_, N) = a.shape, b.shape
    elif mode == "nt":
        (M, K), (N, _) = a.shape, b.shape
    else:
        (K, M), (_, N) = a.shape, b.shape
    tm, tn, tk = _pick_tile(M, tm), _pick_tile(N, tn), _pick_tile(K, tk)
    if mode == "nn":
        a_spec = pl.BlockSpec((tm, tk), lambda i, j, k: (i, k))
        b_spec = pl.BlockSpec((tk, tn), lambda i, j, k: (k, j))
        dims = NN
    elif mode == "nt":
        a_spec = pl.BlockSpec((tm, tk), lambda i, j, k: (i, k))
        b_spec = pl.BlockSpec((tn, tk), lambda i, j, k: (j, k))
        dims = NT
    else:
        a_spec = pl.BlockSpec((tk, tm), lambda i, j, k: (k, i))
        b_spec = pl.BlockSpec((tk, tn), lambda i, j, k: (k, j))
        dims = TN
    return _mm_call(a, b, name=name, grid=(M // tm, N // tn, K // tk), a_spec=a_spec, b_spec=b_spec,
                    o_spec=pl.BlockSpec((tm, tn), lambda i, j, k: (i, j)),
                    out_shape=jax.ShapeDtypeStruct((M, N), out_dtype), dims=dims, acc_shape=(tm, tn))


ROW_TILE = 512
SUB_ROWS = 256


def _sub_tiles(T):
    return [pl.ds(r0, min(SUB_ROWS, T)) for r0 in range(0, T, SUB_ROWS)]


def _rt(T, W, c=0):
    return pl.BlockSpec((T, W), lambda i: (i, c))


def _rt3(nb, T, W):
    return pl.BlockSpec((nb, T, W), lambda i: (0, i, 0))


def _res(shape):
    nd = len(shape)
    return pl.BlockSpec(tuple(shape), lambda i: (0,) * nd, pipeline_mode=pl.Buffered(1))


def _acc(shape):
    nd = len(shape)
    return pl.BlockSpec(tuple(shape), lambda i: (0,) * nd, pipeline_mode=pl.Buffered(1))


def _nbytes(shape, dtype):
    n = jnp.dtype(dtype).itemsize
    for s in shape:
        n *= s
    return n


def _vmem_limit(tiles, resident, temps=16 * 1024 * 1024):
    need = 2 * sum(_nbytes(s, d) for s, d in tiles) + sum(_nbytes(s, d) for s, d in resident) + temps
    return min(need, 60 * 1024 * 1024)


def _tok_call(body, *, name, S, T, ins, outs, semantics="parallel"):
    tiles = [(spec.block_shape, a.dtype) for a, spec, kind in ins + outs if kind == "tile"]
    resident = [(a.shape, a.dtype) for a, spec, kind in ins + outs if kind == "res"]
    return pl.pallas_call(
        body, name=name, grid=(S // T,),
        in_specs=[spec for _, spec, _ in ins], out_specs=[spec for _, spec, _ in outs],
        out_shape=[pltpu.HBM(a.shape, a.dtype) for a, _, _ in outs],
        compiler_params=_cp((semantics,), _vmem_limit(tiles, resident)),
    )(*_hbm(*[a for a, _, _ in ins]))


def _tile(a, spec):
    return (a, spec, "tile")


def _whole(a):
    return (a, _res(a.shape), "res")


def _out_tile(shape, dtype, spec):
    return (jax.ShapeDtypeStruct(shape, dtype), spec, "tile")


def _out_acc(shape, dtype=F32):
    return (jax.ShapeDtypeStruct(shape, dtype), _acc(shape), "res")


def _rms_stats(x):
    r = lax.rsqrt(jnp.mean(x * x, axis=-1, keepdims=True) + EPS)
    return r, x * r


def _rms_bwd(xh, r, g, dy):
    dxh = dy * g
    dx = r * (dxh - xh * jnp.mean(dxh * xh, axis=-1, keepdims=True))
    dg = jnp.sum(dy * xh, axis=0, keepdims=True)
    return dx, dg


def _accum(ref, val):
    @pl.when(pl.program_id(0) == 0)
    def _():
        ref[...] = val

    @pl.when(pl.program_id(0) > 0)
    def _():
        ref[...] += val


def _dot_rows_t(a, w_ref, row0, o_ref, chunk=1024):
    n = o_ref.shape[1]
    for n0 in range(0, n, chunk):
        n1 = min(n, n0 + chunk)
        o_ref[:, n0:n1] = _dot(a, w_ref[row0 + n0:row0 + n1, :], NT).astype(o_ref.dtype)


def _in_proj_call(x, g1, w_blocks, *, name, T=ROW_TILE):
    S = x.shape[0]

    def body(x_ref, g_ref, blk_ref, a_ref, pa_ref, al_ref, ps_ref, pg_ref, w_ref):
        @pl.when(pl.program_id(0) == 0)
        def _():
            for d in range(N_DEV):
                w_ref[d * IN_BLK:(d + 1) * IN_BLK, :] = blk_ref[d]

        _, xh = _rms_stats(x_ref[...])
        a = (xh * g_ref[...]).astype(BF16)
        a_ref[...] = a
        _dot_rows_t(a, w_ref, A_COLS[0], pa_ref)
        _dot_rows_t(a, w_ref, LOW_COLS[0], al_ref)
        _dot_rows_t(a, w_ref, S_COLS[0], ps_ref)
        _dot_rows_t(a, w_ref, G_COLS[0], pg_ref)

    widths = (D_MODEL, A_COLS[1] - A_COLS[0], LANES, S_COLS[1] - S_COLS[0], G_COLS[1] - G_COLS[0])
    return _tok_call(
        body, name=name, S=S, T=T, semantics="arbitrary",
        ins=[_tile(x, _rt(T, D_MODEL)), _whole(g1), _whole(w_blocks)],
        outs=[_out_tile((S, w), BF16, _rt(T, w)) for w in widths] + [_out_acc((D_IN, D_MODEL), BF16)])


def _mixer_tail_call(y_gla, y_sgu, pg, x, w_bg, w_bs, w_out, g2, g3, *, name, T=ROW_TILE):
    S = x.shape[0]

    def body(yg_ref, ys_ref, pg_ref, x_ref, wbg_ref, wbs_ref, wo_ref, g2_ref, g3_ref,
             t1_ref, t2_ref, mg_ref, mix_ref, x1_ref, h_ref):
        for rows in _sub_tiles(T):
            t1 = _dot(yg_ref[rows, :], wbg_ref[...], NN)
            t2 = _dot(ys_ref[rows, :], wbs_ref[...], NN)
            t1_ref[rows, :] = t1.astype(BF16)
            t2_ref[rows, :] = t2.astype(BF16)
            sg = _sig(pg_ref[rows, pl.ds(0, D_MODEL)].astype(F32))
            ss = _sig(pg_ref[rows, pl.ds(D_MODEL, D_MODEL)].astype(F32))
            merged = (sg * t1 + ss * t2).astype(BF16)
            mg_ref[rows, :] = merged
            mix = _dot(merged, wo_ref[...], NN)
            mix_ref[rows, :] = mix
            _, mh = _rms_stats(mix)
            x1 = x_ref[rows, :] + mh * g2_ref[...]
            x1_ref[rows, :] = x1
            _, xh = _rms_stats(x1)
            h_ref[rows, :] = (xh * g3_ref[...]).astype(BF16)

    row = _rt(T, D_MODEL)
    b16 = lambda: _out_tile((S, D_MODEL), BF16, row)
    f32 = lambda: _out_tile((S, D_MODEL), F32, row)
    return _tok_call(
        body, name=name, S=S, T=T,
        ins=[_tile(y_gla, row), _tile(y_sgu, row), _tile(pg, _rt(T, 2 * D_MODEL)), _tile(x, row),
             _whole(w_bg), _whole(w_bs), _whole(w_out), _whole(g2), _whole(g3)],
        outs=[b16(), b16(), b16(), f32(), f32(), b16()])


def _ffn_in_call(h, wfi, *, name, T=ROW_TILE):
    S = h.shape[0]

    def body(h_ref, w_ref, gu_ref, z_ref):
        hv = h_ref[...]
        for d in range(4):
            gate = _dot(hv, w_ref[d], NT)
            up = _dot(hv, w_ref[d + 4], NT)
            gu_ref[d] = gate.astype(BF16)
            gu_ref[d + 4] = up.astype(BF16)
            z_ref[d] = (gate * _sig(gate) * up).astype(BF16)

    return _tok_call(
        body, name=name, S=S, T=T,
        ins=[_tile(h, _rt(T, D_MODEL)), _whole(wfi)],
        outs=[_out_tile((N_DEV, S, FF_BLK), BF16, _rt3(N_DEV, T, FF_BLK)),
              _out_tile((4, S, FF_BLK), BF16, _rt3(4, T, FF_BLK))])


def _ffn_out_loss_call(z, wfo, x1, target, g4, *, name, T=ROW_TILE):
    S = x1.shape[0]

    def body(z_ref, w_ref, x1_ref, t_ref, g4_ref, loss_ref, dx2_ref, dy_ref, dg4_ref):
        loss = jnp.zeros((1, 1), F32)
        dg4 = jnp.zeros((1, D_MODEL), F32)
        for rows in _sub_tiles(T):
            y = _dot(z_ref[0, rows, :], w_ref[0], NN)
            for d in range(1, 4):
                y = y + _dot(z_ref[d, rows, :], w_ref[d], NN)
            r, yh = _rms_stats(y)
            diff = x1_ref[rows, :] + yh * g4_ref[...] - t_ref[rows, :]
            loss = loss + 0.5 * jnp.sum(jnp.mean(diff * diff, axis=-1, keepdims=True), axis=0, keepdims=True)
            dx2 = diff * (1.0 / D_MODEL)
            dx2_ref[rows, :] = dx2
            dy, dg = _rms_bwd(yh, r, g4_ref[...], dx2)
            dy_ref[rows, :] = dy.astype(BF16)
            dg4 = dg4 + dg
        _accum(loss_ref, jnp.broadcast_to(loss, (1, LANES)))
        _accum(dg4_ref, dg4)

    row = _rt(T, D_MODEL)
    return _tok_call(
        body, name=name, S=S, T=T, semantics="arbitrary",
        ins=[_tile(z, _rt3(4, T, FF_BLK)), _whole(wfo), _tile(x1, row), _tile(target, row), _whole(g4)],
        outs=[_out_acc((1, LANES)), _out_tile((S, D_MODEL), F32, row), _out_tile((S, D_MODEL), BF16, row),
              _out_acc((1, D_MODEL))])


def _ffn_out_bwd_call(dy, wfo, gu, *, name, T=ROW_TILE):
    S = dy.shape[0]

    def body(dy_ref, w_ref, gu_ref, dgu_ref):
        dyv = dy_ref[...]
        for d in range(4):
            dz = _dot(dyv, w_ref[d], NT).astype(BF16)
            gt = gu_ref[d]
            up = gu_ref[d + 4]
            s = _sig(gt)
            dgu_ref[d] = dz * up * s * (1.0 + gt * (1.0 - s))
            dgu_ref[d + 4] = dz * gt * s

    blocks = _rt3(N_DEV, T, FF_BLK)
    return _tok_call(
        body, name=name, S=S, T=T,
        ins=[_tile(dy, _rt(T, D_MODEL)), _whole(wfo), _tile(gu, blocks)],
        outs=[_out_tile((N_DEV, S, FF_BLK), BF16, blocks)])[0]


def _ffn_in_bwd_call(dgu, wfi, dx2, x1, mix, g3, g2, *, name, T=ROW_TILE):
    S = x1.shape[0]

    def body(dgu_ref, w_ref, dx2_ref, x1_ref, mix_ref, g3_ref, g2_ref, dx1_ref, dmix_ref, dg3_ref, dg2_ref):
        dg3 = jnp.zeros((1, D_MODEL), F32)
        dg2 = jnp.zeros((1, D_MODEL), F32)
        for rows in _sub_tiles(T):
            dh = _dot(dgu_ref[0, rows, :], w_ref[0], NN)
            for d in range(1, N_DEV):
                dh = dh + _dot(dgu_ref[d, rows, :], w_ref[d], NN)
            r3, xh = _rms_stats(x1_ref[rows, :])
            d3, g = _rms_bwd(xh, r3, g3_ref[...], dh)
            dg3 = dg3 + g
            dx1 = dx2_ref[rows, :] + d3
            dx1_ref[rows, :] = dx1
            r2, mh = _rms_stats(mix_ref[rows, :])
            dmix, g = _rms_bwd(mh, r2, g2_ref[...], dx1)
            dg2 = dg2 + g
            dmix_ref[rows, :] = dmix.astype(BF16)
        _accum(dg3_ref, dg3)
        _accum(dg2_ref, dg2)

    row = _rt(T, D_MODEL)
    return _tok_call(
        body, name=name, S=S, T=T, semantics="arbitrary",
        ins=[_tile(dgu, _rt3(N_DEV, T, FF_BLK)), _whole(wfi), _tile(dx2, row), _tile(x1, row), _tile(mix, row),
             _whole(g3), _whole(g2)],
        outs=[_out_tile((S, D_MODEL), F32, row), _out_tile((S, D_MODEL), BF16, row),
              _out_acc((1, D_MODEL)), _out_acc((1, D_MODEL))])


def _mixer_bwd_call(dmix, t1, t2, pg, w_out, w_bg, w_bs, *, name, T=ROW_TILE):
    S = dmix.shape[0]

    def body(dmix_ref, t1_ref, t2_ref, pg_ref, wo_ref, wbg_ref, wbs_ref, dt1_ref, dt2_ref, dpg_ref, dyg_ref, dys_ref):
        gg, gs = pl.ds(0, D_MODEL), pl.ds(D_MODEL, D_MODEL)
        for rows in _sub_tiles(T):
            dm = _dot(dmix_ref[rows, :], wo_ref[...], NT)
            sg = _sig(pg_ref[rows, gg].astype(F32))
            ss = _sig(pg_ref[rows, gs].astype(F32))
            dt1 = (dm * sg).astype(BF16)
            dt2 = (dm * ss).astype(BF16)
            dt1_ref[rows, :] = dt1
            dt2_ref[rows, :] = dt2
            dpg_ref[rows, gg] = (dm * t1_ref[rows, :].astype(F32) * sg * (1.0 - sg)).astype(BF16)
            dpg_ref[rows, gs] = (dm * t2_ref[rows, :].astype(F32) * ss * (1.0 - ss)).astype(BF16)
            dyg_ref[rows, :] = _dot(dt1, wbg_ref[...], NT).astype(BF16)
            dys_ref[rows, :] = _dot(dt2, wbs_ref[...], NT).astype(BF16)

    row = _rt(T, D_MODEL)
    wide = _rt(T, 2 * D_MODEL)
    b16 = lambda: _out_tile((S, D_MODEL), BF16, row)
    return _tok_call(
        body, name=name, S=S, T=T,
        ins=[_tile(dmix, row), _tile(t1, row), _tile(t2, row), _tile(pg, wide), _whole(w_out), _whole(w_bg), _whole(w_bs)],
        outs=[b16(), b16(), _out_tile((S, 2 * D_MODEL), BF16, wide), b16(), b16()])


def _in_proj_bwd_call(dpa, dlow, dps, dpg, w_in_t, x, dx1, g1, *, name, T=ROW_TILE):
    S = x.shape[0]

    def body(dpa_ref, dl_ref, dps_ref, dpg_ref, w_ref, x_ref, dx1_ref, g1_ref, gx_ref, dg1_ref):
        da = (_dot(dpa_ref[...], w_ref[A_COLS[0]:A_COLS[1], :], NN)
              + _dot(dl_ref[...], w_ref[LOW_COLS[0]:LOW_COLS[0] + LANES, :], NN)
              + _dot(dps_ref[...], w_ref[S_COLS[0]:S_COLS[1], :], NN)
              + _dot(dpg_ref[...], w_ref[G_COLS[0]:G_COLS[1], :], NN))
        r, xh = _rms_stats(x_ref[...])
        dxa, dg = _rms_bwd(xh, r, g1_ref[...], da)
        gx_ref[...] = dx1_ref[...] + dxa
        _accum(dg1_ref, dg)

    row = _rt(T, D_MODEL)
    return _tok_call(
        body, name=name, S=S, T=T, semantics="arbitrary",
        ins=[_tile(dpa, _rt(T, dpa.shape[1])), _tile(dlow, _rt(T, dlow.shape[1])), _tile(dps, _rt(T, dps.shape[1])),
             _tile(dpg, _rt(T, dpg.shape[1])), _whole(w_in_t), _tile(x, row), _tile(dx1, row), _whole(g1)],
        outs=[_out_tile((S, D_MODEL), F32, row), _out_acc((1, D_MODEL))])


TOKEN_TILE = 512


def _weight_grads_part(arrays, pairs, *, tk=TOKEN_TILE, out_dtype=BF16):
    S = arrays[0].shape[-2]
    tk = min(tk, S)
    n_in = len(arrays)

    def out_shape(i, j):
        a, b = arrays[i], arrays[j]
        if a.ndim == 3:
            return (a.shape[0], a.shape[2], b.shape[1])
        if b.ndim == 3:
            return (b.shape[0], a.shape[1], b.shape[2])
        return (a.shape[1], b.shape[1])

    shapes = [out_shape(i, j) for i, j in pairs]

    in_place = out_dtype == F32

    def body(ins, outs, accs):
        k = pl.program_id(0)
        if in_place:
            accs = outs

        @pl.when(k == 0)
        def _():
            for acc in accs:
                acc[...] = jnp.zeros_like(acc)

        for (i, j), acc in zip(pairs, accs):
            a_ref, b_ref = ins[i], ins[j]
            if len(a_ref.shape) == 3:
                for n in range(a_ref.shape[0]):
                    acc[n] += _dot(a_ref[n], b_ref[...], TN)
            elif len(b_ref.shape) == 3:
                a = a_ref[...]
                for n in range(b_ref.shape[0]):
                    acc[n] += _dot(a, b_ref[n], TN)
            else:
                b = b_ref[...]
                for m0 in range(0, a_ref.shape[1], 1024):
                    m1 = min(a_ref.shape[1], m0 + 1024)
                    acc[m0:m1, :] += _dot(a_ref[:, m0:m1], b, TN)

        if not in_place:
            @pl.when(k == S // tk - 1)
            def _():
                for out, acc in zip(outs, accs):
                    out[...] = acc[...].astype(out.dtype)

    def in_spec(a):
        if a.ndim == 3:
            return pl.BlockSpec((a.shape[0], tk, a.shape[2]), lambda k: (0, k, 0))
        return pl.BlockSpec((tk, a.shape[1]), lambda k: (k, 0))

    return dict(
        body=body, steps=S // tk, arrays=list(arrays),
        in_specs=[in_spec(a) for a in arrays],
        out_specs=[_acc(s) for s in shapes],
        out_shapes=[pltpu.HBM(s, out_dtype) for s in shapes],
        scratch=[] if in_place else [pltpu.VMEM(s, F32) for s in shapes],
        tiles=[(in_spec(a).block_shape, a.dtype) for a in arrays],
        resident=[(s, F32) for s in shapes] + ([] if in_place else [(s, BF16) for s in shapes]))


def _weight_grads_call(arrays, pairs, *, name, tk=TOKEN_TILE):
    part = _weight_grads_part(arrays, pairs, tk=tk)
    n_in, n_out = len(part["arrays"]), len(part["out_shapes"])

    def body(*refs):
        part["body"](refs[:n_in], refs[n_in:n_in + n_out], refs[n_in + n_out:])

    return pl.pallas_call(
        body, name=name, grid=(part["steps"],),
        in_specs=part["in_specs"], out_specs=part["out_specs"], out_shape=part["out_shapes"],
        scratch_shapes=part["scratch"],
        compiler_params=_cp(("arbitrary",), _vmem_limit(part["tiles"], part["resident"])),
    )(*_hbm(*part["arrays"]))


def _ffn_in_grad_call(h, dgu, *, name, tk=TOKEN_TILE):
    S = h.shape[0]
    tk = min(tk, S)
    nb = 4
    shape = (nb, FF_BLK, D_MODEL)

    def body(h_ref, dgu_ref, out_ref, acc):
        k = pl.program_id(1)

        @pl.when(k == 0)
        def _():
            acc[...] = jnp.zeros_like(acc)

        hv = h_ref[...]
        for n in range(nb):
            acc[n] += _dot(dgu_ref[n], hv, TN)

        @pl.when(k == S // tk - 1)
        def _():
            out_ref[...] = acc[...].astype(out_ref.dtype)

    tiles = [((tk, D_MODEL), BF16), ((nb, tk, FF_BLK), BF16), (shape, BF16)]
    return pl.pallas_call(
        body, name=name, grid=(N_DEV // nb, S // tk),
        in_specs=[pl.BlockSpec((tk, D_MODEL), lambda g, k: (k, 0)),
                  pl.BlockSpec((nb, tk, FF_BLK), lambda g, k: (g, k, 0))],
        out_specs=pl.BlockSpec(shape, lambda g, k: (g, 0, 0)),
        out_shape=pltpu.HBM((N_DEV, FF_BLK, D_MODEL), BF16),
        scratch_shapes=[pltpu.VMEM(shape, F32)],
        compiler_params=_cp(("parallel", "arbitrary"), _vmem_limit(tiles, [(shape, F32)])),
    )(*_hbm(h, dgu))


GLA_TILE = 256
Q_OFF, K_OFF, V_OFF, R_OFF = 0, QK, 2 * QK, 2 * QK + GV


def _tri(lower):
    r = lax.broadcasted_iota(jnp.int32, (CHUNK, CHUNK), 0)
    c = lax.broadcasted_iota(jnp.int32, (CHUNK, CHUNK), 1)
    return jnp.where((r >= c) if lower else (c >= r), 1.0, 0.0).astype(BF16)


def _gla_fwd_call(pa, alow, wg, bgate, gnorm, *, name):
    S = pa.shape[0]
    Tg = min(GLA_TILE, S)
    cb = Tg // CHUNK

    def body(pa_ref, al_ref, wg_ref, bg_ref, gn_ref, y_ref, st_ref, state):
        @pl.when(pl.program_id(0) == 0)
        def _():
            state[...] = jnp.zeros_like(state)

        logit = _dot(al_ref[...], wg_ref[...], NN) + bg_ref[...]
        ls = _logsig(logit) * (1.0 / 16.0)
        tri = _tri(True)
        for c in range(cb):
            rows = pl.ds(c * CHUNK, CHUNK)
            cum = _exact_mask_dot(tri, ls[c * CHUNK:(c + 1) * CHUNK])
            tot = cum[CHUNK - 1:CHUNK]
            kd = (pa_ref[rows, pl.ds(K_OFF, QK)].astype(F32) * jnp.exp(tot - cum)).astype(BF16)
            decay = jnp.exp(tot)
            for h in range(HEADS):
                lanes = slice(h * DK, (h + 1) * DK)
                new = state[h] * decay[:, lanes] + _dot(pa_ref[rows, pl.ds(V_OFF + h * DV, DV)], kd[:, lanes], TN)
                state[h] = new
                st_ref[c, h] = new
        for c in range(cb):
            rows = pl.ds(c * CHUNK, CHUNK)
            for h in range(HEADS):
                qs = (pa_ref[rows, pl.ds(Q_OFF + h * DK, DK)].astype(F32) * Q_SCALE).astype(BF16)
                o = _dot(qs, st_ref[c, h].astype(BF16), NT)
                rs = lax.rsqrt(jnp.mean(o * o, axis=-1, keepdims=True) + EPS)
                rr = pa_ref[rows, pl.ds(R_OFF + h * DV, DV)].astype(F32)
                y_ref[rows, pl.ds(h * DV, DV)] = (o * rs * gn_ref[h] * (rr * _sig(rr))).astype(BF16)

    return pl.pallas_call(
        body, name=name, grid=(S // Tg,),
        in_specs=[
            pl.BlockSpec((Tg, 2 * QK + 2 * GV), lambda t: (t, 0)),
            pl.BlockSpec((Tg, LANES), lambda t: (t, 0)),
            pl.BlockSpec((LANES, QK), lambda t: (0, 0)),
            pl.BlockSpec((1, QK), lambda t: (0, 0)),
            pl.BlockSpec((HEADS, 1, DV), lambda t: (0, 0, 0)),
        ],
        out_specs=[
            pl.BlockSpec((Tg, GV), lambda t: (t, 0)),
            pl.BlockSpec((cb, HEADS, DV, DK), lambda t: (t, 0, 0, 0)),
        ],
        out_shape=[pltpu.HBM((S, GV), BF16), pltpu.HBM((S // CHUNK, HEADS, DV, DK), F32)],
        scratch_shapes=[pltpu.VMEM((HEADS, DV, DK), F32)],
        compiler_params=_cp(("arbitrary",), VMEM_BIG),
    )(*_hbm(pa, alow, wg, bgate, gnorm))


def _gla_bwd_call(pa, alow, wg, bgate, gnorm, states, dy, *, name):
    S = pa.shape[0]
    Tg = min(GLA_TILE, S)
    cb = Tg // CHUNK
    nt = S // Tg

    def rev(t):
        return nt - 1 - t

    def body(pa_ref, al_ref, wg_ref, bg_ref, gn_ref, st_ref, prev_ref, dy_ref,
             dpa_ref, dl_ref, dgn_ref, dbg_ref, carry, pbuf):
        t = pl.program_id(0)

        @pl.when(t == 0)
        def _():
            carry[...] = jnp.zeros_like(carry)
            dgn_ref[...] = jnp.zeros_like(dgn_ref)
            dbg_ref[...] = jnp.zeros_like(dbg_ref)

        logit = _dot(al_ref[...], wg_ref[...], NN) + bg_ref[...]
        ls = _logsig(logit) * (1.0 / 16.0)
        sneg = 1.0 / (1.0 + jnp.exp(logit))
        tri = _tri(True)
        upper = _tri(False)
        first_tile = rev(t) == 0
        heads = range(HEADS)

        w, decay, kd = [], [], []
        for c in range(cb):
            rows = pl.ds(c * CHUNK, CHUNK)
            cum = _exact_mask_dot(tri, ls[c * CHUNK:(c + 1) * CHUNK])
            tot = cum[CHUNK - 1:CHUNK]
            w.append(jnp.exp(tot - cum))
            decay.append(jnp.exp(tot))
            kd.append(pa_ref[rows, pl.ds(K_OFF, QK)].astype(F32) * w[c])

        dgn = [jnp.zeros((1, DV), F32) for _ in heads]
        for c in range(cb):
            rows = pl.ds(c * CHUNK, CHUNK)
            for h in heads:
                gn = gn_ref[h]
                qs = (pa_ref[rows, pl.ds(Q_OFF + h * DK, DK)].astype(F32) * Q_SCALE).astype(BF16)
                st16 = st_ref[c, h].astype(BF16)
                o = _dot(qs, st16, NT)
                rs = lax.rsqrt(jnp.mean(o * o, axis=-1, keepdims=True) + EPS)
                oh = o * rs
                rr = pa_ref[rows, pl.ds(R_OFF + h * DV, DV)].astype(F32)
                sr = _sig(rr)
                dyv = dy_ref[rows, pl.ds(h * DV, DV)].astype(F32)
                dpa_ref[rows, pl.ds(R_OFF + h * DV, DV)] = (
                    dyv * oh * gn * sr * (1.0 + rr * (1.0 - sr))).astype(BF16)
                don = dyv * (rr * sr)
                dgn[h] = dgn[h] + jnp.sum(don * oh, axis=0, keepdims=True)
                doh = don * gn
                do16 = (rs * (doh - oh * jnp.mean(doh * oh, axis=-1, keepdims=True))).astype(BF16)
                dpa_ref[rows, pl.ds(Q_OFF + h * DK, DK)] = (_dot(do16, st16, NN) * Q_SCALE).astype(BF16)
                pbuf[c, h] = _dot(do16, qs, TN)
        for h in heads:
            dgn_ref[h] += dgn[h]

        dkd = [[None] * HEADS for _ in range(cb)]
        ddecay = [[None] * HEADS for _ in range(cb)]
        for c in reversed(range(cb)):
            rows = pl.ds(c * CHUNK, CHUNK)
            for h in heads:
                lanes = slice(h * DK, (h + 1) * DK)
                gt = pbuf[c, h] + carry[h]
                gt16 = gt.astype(BF16)
                dkd[c][h] = _dot(pa_ref[rows, pl.ds(V_OFF + h * DV, DV)], gt16, NN)
                dpa_ref[rows, pl.ds(V_OFF + h * DV, DV)] = _dot(kd[c][:, lanes].astype(BF16), gt16, NT).astype(BF16)
                if c > 0:
                    st_prev = st_ref[c - 1, h]
                else:
                    st_prev = jnp.where(first_tile, 0.0, prev_ref[0, h])
                ddecay[c][h] = jnp.sum(gt * st_prev, axis=0, keepdims=True)
                carry[h] = gt * decay[c][:, lanes]

        dbg = jnp.zeros((1, QK), F32)
        for c in range(cb):
            rows = pl.ds(c * CHUNK, CHUNK)
            dkd_c = jnp.concatenate(dkd[c], axis=1)
            dpa_ref[rows, pl.ds(K_OFF, QK)] = (dkd_c * w[c]).astype(BF16)
            e = dkd_c * kd[c]
            dtot = jnp.sum(e, axis=0, keepdims=True) + jnp.concatenate(ddecay[c], axis=1) * decay[c]
            dls = dtot - _exact_mask_dot(upper, e)
            dlogit = dls * (1.0 / 16.0) * sneg[c * CHUNK:(c + 1) * CHUNK]
            dl_ref[rows, :] = dlogit.astype(BF16)
            dbg = dbg + jnp.sum(dlogit, axis=0, keepdims=True)
        dbg_ref[...] += dbg

    wide = 2 * QK + 2 * GV
    tiles = [((Tg, wide), BF16), ((cb + 1, HEADS, DV, DK), F32), ((Tg, GV), BF16), ((Tg, wide), BF16),
             ((Tg, QK), BF16)]
    resident = [((cb + 1, HEADS, DV, DK), F32)]
    return pl.pallas_call(
        body, name=name, grid=(nt,),
        in_specs=[
            pl.BlockSpec((Tg, wide), lambda t: (rev(t), 0)),
            pl.BlockSpec((Tg, LANES), lambda t: (rev(t), 0)),
            pl.BlockSpec((LANES, QK), lambda t: (0, 0)),
            pl.BlockSpec((1, QK), lambda t: (0, 0)),
            pl.BlockSpec((HEADS, 1, DV), lambda t: (0, 0, 0)),
            pl.BlockSpec((cb, HEADS, DV, DK), lambda t: (rev(t), 0, 0, 0)),
            pl.BlockSpec((1, HEADS, DV, DK), lambda t: (jnp.maximum(rev(t) * cb - 1, 0), 0, 0, 0)),
            pl.BlockSpec((Tg, GV), lambda t: (rev(t), 0)),
        ],
        out_specs=[
            pl.BlockSpec((Tg, wide), lambda t: (rev(t), 0)),
            pl.BlockSpec((Tg, QK), lambda t: (rev(t), 0)),
            pl.BlockSpec((HEADS, 1, DV), lambda t: (0, 0, 0)),
            pl.BlockSpec((1, QK), lambda t: (0, 0)),
        ],
        out_shape=[pltpu.HBM((S, wide), BF16), pltpu.HBM((S, QK), BF16),
                   jax.ShapeDtypeStruct((HEADS, 1, DV), F32), jax.ShapeDtypeStruct((1, QK), F32)],
        scratch_shapes=[pltpu.VMEM((HEADS, DV, DK), F32), pltpu.VMEM((cb, HEADS, DV, DK), F32)],
        compiler_params=_cp(("arbitrary",), _vmem_limit(tiles, resident)),
    )(*_hbm(pa, alow, wg, bgate, gnorm, states, states, dy))


SGU_TILE = 256


def _sgu_mask():
    r = lax.broadcasted_iota(jnp.int32, (SBLOCK, SBLOCK), 0)
    c = lax.broadcasted_iota(jnp.int32, (SBLOCK, SBLOCK), 1)
    return (c < CHUNK) | (r >= CHUNK)


def _ln_stats(vf):
    mu = jnp.mean(vf, axis=-1, keepdims=True)
    xc = vf - mu
    rs = lax.rsqrt(jnp.mean(xc * xc, axis=-1, keepdims=True) + EPS)
    return rs, xc * rs


def _sgu_fwd_call(ps, ln_g, ln_b, w_sp, b_sp, *, name):
    S = ps.shape[0]
    Ts = min(SGU_TILE, S)

    def body(ps_ref, lg_ref, lb_ref, w_ref, b_ref, y_ref):
        mask = _sgu_mask()
        for g in range(GROUPS):
            wm = jnp.where(mask, w_ref[g], 0.0).astype(BF16)
            for p in range(Ts // SBLOCK):
                rows = pl.ds(p * SBLOCK, SBLOCK)
                u = _gelu(ps_ref[rows, pl.ds(g * DG, DG)].astype(F32))
                _, xh = _ln_stats(_gelu(ps_ref[rows, pl.ds(D_MODEL + g * DG, DG)].astype(F32)))
                vn = xh * lg_ref[g] + lb_ref[g]
                mixed = _dot(wm, vn.astype(BF16), NN) + b_ref[g]
                y_ref[rows, pl.ds(g * DG, DG)] = (u * mixed).astype(BF16)

    full3 = lambda a, b, c: pl.BlockSpec((a, b, c), lambda t: (0, 0, 0))
    return pl.pallas_call(
        body, name=name, grid=(S // Ts,),
        in_specs=[pl.BlockSpec((Ts, 2 * D_MODEL), lambda t: (t, 0)),
                  full3(GROUPS, 1, DG), full3(GROUPS, 1, DG), full3(GROUPS, SBLOCK, SBLOCK), full3(GROUPS, SBLOCK, 1)],
        out_specs=pl.BlockSpec((Ts, D_MODEL), lambda t: (t, 0)),
        out_shape=pltpu.HBM((S, D_MODEL), BF16),
        compiler_params=_cp(("parallel",)),
    )(*_hbm(ps, ln_g, ln_b, w_sp, b_sp))


def _sgu_bwd_call(ps, ln_g, ln_b, w_sp, b_sp, dy, *, name):
    S = ps.shape[0]
    Ts = min(SGU_TILE, S)

    def body(ps_ref, lg_ref, lb_ref, w_ref, b_ref, dy_ref, ds_ref, dlg_ref, dlb_ref, dw_ref, db_ref):
        @pl.when(pl.program_id(0) == 0)
        def _():
            dlg_ref[...] = jnp.zeros_like(dlg_ref)
            dlb_ref[...] = jnp.zeros_like(dlb_ref)
            dw_ref[...] = jnp.zeros_like(dw_ref)
            db_ref[...] = jnp.zeros_like(db_ref)

        mask = _sgu_mask()
        for g in range(GROUPS):
            wm = jnp.where(mask, w_ref[g], 0.0).astype(BF16)
            lg = lg_ref[g]
            for p in range(Ts // SBLOCK):
                rows = pl.ds(p * SBLOCK, SBLOCK)
                su = ps_ref[rows, pl.ds(g * DG, DG)].astype(F32)
                sv = ps_ref[rows, pl.ds(D_MODEL + g * DG, DG)].astype(F32)
                u, du = _gelu_and_grad(su)
                gv, dgv = _gelu_and_grad(sv)
                rs, xh = _ln_stats(gv)
                vn16 = (xh * lg + lb_ref[g]).astype(BF16)
                mixed = _dot(wm, vn16, NN) + b_ref[g]
                dyv = dy_ref[rows, pl.ds(g * DG, DG)].astype(F32)
                ds_ref[rows, pl.ds(g * DG, DG)] = (dyv * mixed * du).astype(BF16)
                dmix = dyv * u
                dmix16 = dmix.astype(BF16)
                db_ref[g] += jnp.sum(dmix, axis=-1, keepdims=True)
                dw_ref[g] += jnp.where(mask, _dot(dmix16, vn16, NT), 0.0)
                dvn = _dot(wm, dmix16, TN)
                dlg_ref[g] += jnp.sum(dvn * xh, axis=0, keepdims=True)
                dlb_ref[g] += jnp.sum(dvn, axis=0, keepdims=True)
                dxh = dvn * lg
                dvf = rs * (dxh - jnp.mean(dxh, axis=-1, keepdims=True)
                            - xh * jnp.mean(dxh * xh, axis=-1, keepdims=True))
                ds_ref[rows, pl.ds(D_MODEL + g * DG, DG)] = (dvf * dgv).astype(BF16)

    full3 = lambda a, b, c: pl.BlockSpec((a, b, c), lambda t: (0, 0, 0))
    tiles = [((Ts, 2 * D_MODEL), BF16), ((Ts, D_MODEL), BF16), ((Ts, 2 * D_MODEL), BF16)]
    return pl.pallas_call(
        body, name=name, grid=(S // Ts,),
        in_specs=[pl.BlockSpec((Ts, 2 * D_MODEL), lambda t: (t, 0)),
                  full3(GROUPS, 1, DG), full3(GROUPS, 1, DG), full3(GROUPS, SBLOCK, SBLOCK), full3(GROUPS, SBLOCK, 1),
                  pl.BlockSpec((Ts, D_MODEL), lambda t: (t, 0))],
        out_specs=[pl.BlockSpec((Ts, 2 * D_MODEL), lambda t: (t, 0)),
                   full3(GROUPS, 1, DG), full3(GROUPS, 1, DG), full3(GROUPS, SBLOCK, SBLOCK), full3(GROUPS, SBLOCK, 1)],
        out_shape=[pltpu.HBM((S, 2 * D_MODEL), BF16),
                   jax.ShapeDtypeStruct((GROUPS, 1, DG), F32), jax.ShapeDtypeStruct((GROUPS, 1, DG), F32),
                   jax.ShapeDtypeStruct((GROUPS, SBLOCK, SBLOCK), F32),
                   jax.ShapeDtypeStruct((GROUPS, SBLOCK, 1), F32)],
        compiler_params=_cp(("arbitrary",), _vmem_limit(tiles, [])),
    )(*_hbm(ps, ln_g, ln_b, w_sp, b_sp, dy))


def _position():
    return lax.axis_index("x"), lax.axis_index("y"), lax.axis_index("c")


def _gather_copies(srcs, dsts, send_sems, recv_sems, local_sems, relay=False):
    if relay:
        return _gather_copies_relayed(srcs, dsts, send_sems, recv_sems, local_sems)
    x, y, c = _position()
    me, sibling = (x, y, c), (x, y, 1 - c)
    chips = [(1 - x, y), (x, 1 - y), (1 - x, 1 - y)]
    n = len(srcs)

    def slab(a, block):
        px, py, pc = block
        return dsts[a].at[4 * px + 2 * py + pc]

    def copy(a, k, block, to, src=None):
        return pltpu.make_async_remote_copy(
            src_ref=slab(a, block) if src is None else src, dst_ref=slab(a, block),
            send_sem=send_sems.at[7 * a + k], recv_sem=recv_sems.at[7 * a + k], device_id=to, device_id_type=MESH)

    mine = [pltpu.make_async_copy(srcs[a], slab(a, me), local_sems.at[a]) for a in range(n)]
    for cp in mine:
        cp.start()
    first = []
    for a in range(n):
        first.append(copy(a, 0, me, sibling, src=srcs[a]))
        first += [copy(a, 1 + j, me, (*chip, c), src=srcs[a]) for j, chip in enumerate(chips)]
    for cp in first:
        cp.start()
    passed = []
    for j, chip in enumerate(chips):
        for a in range(n):
            copy(a, 1 + j, (*chip, c), me).wait_recv()
            cp = copy(a, 4 + j, (*chip, c), sibling)
            cp.start()
            passed.append(cp)
    for a in range(n):
        copy(a, 0, sibling, me).wait_recv()
        for j, chip in enumerate(chips):
            copy(a, 4 + j, (*chip, 1 - c), me).wait_recv()
    for cp in first + passed:
        cp.wait_send()
    for cp in mine:
        cp.wait()


def _gather_copies_relayed(srcs, dsts, send_sems, recv_sems, local_sems):
    x, y, c = _position()
    me, sibling = (x, y, c), (x, y, 1 - c)
    x_chip, y_chip, d_chip = (1 - x, y), (x, 1 - y), (1 - x, 1 - y)
    south = c == 0
    relay_from = (jnp.where(south, x, 1 - x), jnp.where(south, 1 - y, y), c)
    relay_to = (jnp.where(south, 1 - x, x), jnp.where(south, y, 1 - y), c)
    n = len(srcs)

    def slab(a, block):
        px, py, pc = block
        return dsts[a].at[4 * px + 2 * py + pc]

    def copy(a, k, block, to, src=None):
        return pltpu.make_async_remote_copy(
            src_ref=slab(a, block) if src is None else src, dst_ref=slab(a, block),
            send_sem=send_sems.at[7 * a + k], recv_sem=recv_sems.at[7 * a + k], device_id=to, device_id_type=MESH)

    mine = [pltpu.make_async_copy(srcs[a], slab(a, me), local_sems.at[a]) for a in range(n)]
    for cp in mine:
        cp.start()
    sent = []
    for a in range(n):
        sent += [copy(a, 0, me, sibling, src=srcs[a]), copy(a, 1, me, (*x_chip, c), src=srcs[a]),
                 copy(a, 2, me, (*y_chip, c), src=srcs[a])]
    for cp in sent:
        cp.start()
    for a in range(n):
        copy(a, 1, (*x_chip, c), me).wait_recv()
        copy(a, 2, (*y_chip, c), me).wait_recv()
        later = [copy(a, 3, relay_from, relay_to), copy(a, 4, (*x_chip, c), sibling), copy(a, 5, (*y_chip, c), sibling)]
        for cp in later:
            cp.start()
        sent += later
    for a in range(n):
        copy(a, 3, (*d_chip, c), me).wait_recv()
        cp = copy(a, 6, (*d_chip, c), sibling)
        cp.start()
        sent.append(cp)
    for a in range(n):
        copy(a, 0, sibling, me).wait_recv()
        for k, chip in ((4, x_chip), (5, y_chip), (6, d_chip)):
            copy(a, k, (*chip, 1 - c), me).wait_recv()
    for cp in sent:
        cp.wait_send()
    for cp in mine:
        cp.wait()


def _all_gather_hbm(shards, *, name):
    n = len(shards)

    def body(*refs):
        srcs, dsts = refs[:n], refs[n:2 * n]
        send_sems, recv_sems, local_sems = refs[2 * n:]
        _gather_copies(srcs, dsts, send_sems, recv_sems, local_sems, relay=True)

    return pl.pallas_call(
        body, name=name,
        in_specs=[ANY] * n, out_specs=[ANY] * n,
        out_shape=[jax.ShapeDtypeStruct((N_DEV, *s.shape), s.dtype) for s in shards],
        scratch_shapes=_comm_sems(n),
    )(*shards)


FLIPS = [(fx, fy, fc) for fx in (0, 1) for fy in (0, 1) for fc in (0, 1)][1:]


def _scatter_copies(srcs, dsts, send_sems, recv_sems, local_sems, waves=1):
    n = len(srcs)
    x, y, c = _position()
    me = 4 * x + 2 * y + c
    mine = [pltpu.make_async_copy(srcs[a].at[me], dsts[a].at[me], local_sems.at[a]) for a in range(n)]
    for cp in mine:
        cp.start()
    peers = []
    for fx, fy, fc in FLIPS:
        tx = 1 - x if fx else x
        ty = 1 - y if fy else y
        tc = 1 - c if fc else c
        peers.append(((tx, ty, tc), 4 * tx + 2 * ty + tc))
    copies = []
    for w in range(waves):
        wave = []
        for k, (peer_id, peer) in enumerate(peers):
            for a in range(n):
                rows = srcs[a].shape[1]
                step = rows if waves == 1 else (rows // waves) // 16 * 16
                r0 = w * step
                cut = pl.ds(r0, step if w < waves - 1 else rows - r0)
                sem = (7 * a + k) * waves + w
                cp = pltpu.make_async_remote_copy(
                    src_ref=srcs[a].at[peer, cut], dst_ref=dsts[a].at[me, cut],
                    send_sem=send_sems.at[sem], recv_sem=recv_sems.at[sem],
                    device_id=peer_id, device_id_type=MESH)
                cp.start()
                wave.append(cp)
        for cp in wave:
            cp.wait_send()
        copies += wave
    for cp in copies:
        cp.wait_recv()
    for cp in mine:
        cp.wait()


def _pair_copies(srcs, dsts, send_sems, recv_sems, local_sems):
    x, y, c = _position()
    copies = [pltpu.make_async_remote_copy(
        src_ref=srcs[a], dst_ref=dsts[a], send_sem=send_sems.at[a], recv_sem=recv_sems.at[a],
        device_id=(x, y, 1 - c), device_id_type=MESH) for a in range(len(srcs))]
    for cp in copies:
        cp.start()
    for cp in copies:
        cp.wait()


def _chip_scatter_copies(srcs, dsts, send_sems, recv_sems, local_sems, waves=1):
    n = len(srcs)
    x, y, c = _position()
    here = 2 * x + y
    mine = [pltpu.make_async_copy(srcs[a].at[here], dsts[a].at[here], local_sems.at[a]) for a in range(n)]
    for cp in mine:
        cp.start()
    peers = []
    for fx, fy in ((1, 0), (0, 1), (1, 1)):
        tx = 1 - x if fx else x
        ty = 1 - y if fy else y
        peers.append(((tx, ty, c), 2 * tx + ty))
    copies = []
    for w in range(waves):
        wave = []
        for k, (peer_id, peer) in enumerate(peers):
            for a in range(n):
                rows = srcs[a].shape[1]
                step = rows if waves == 1 else (rows // waves) // 16 * 16
                r0 = w * step
                cut = pl.ds(r0, step if w < waves - 1 else rows - r0)
                sem = (3 * a + k) * waves + w
                cp = pltpu.make_async_remote_copy(
                    src_ref=srcs[a].at[peer, cut], dst_ref=dsts[a].at[here, cut],
                    send_sem=send_sems.at[sem], recv_sem=recv_sems.at[sem],
                    device_id=peer_id, device_id_type=MESH)
                cp.start()
                wave.append(cp)
        for cp in wave:
            cp.wait_send()
        copies += wave
    for cp in copies:
        cp.wait_recv()
    for cp in mine:
        cp.wait()


def _comm_sems(n, waves=1):
    return [pltpu.SemaphoreType.DMA((7 * n * waves,)), pltpu.SemaphoreType.DMA((7 * n * waves,)),
            pltpu.SemaphoreType.DMA((n,))]


def _handshake(peers):
    barrier = pltpu.get_barrier_semaphore()
    for peer in peers:
        pl.semaphore_signal(barrier, inc=1, device_id=peer, device_id_type=MESH)
    pl.semaphore_wait(barrier, len(peers))


def _sequencer_call(arrays, out_types, copies_fn, peers_fn, *, name, collective_id, waves=1):
    n = len(arrays)
    srcs = [jax.new_ref(a, memory_space=pltpu.MemorySpace.HBM) for a in arrays]
    dsts = [jax.empty_ref(t, memory_space=pltpu.MemorySpace.HBM) for t in out_types]
    extra = {} if waves == 1 else {"waves": waves}

    @pl.kernel(mesh=plsc.ScalarSubcoreMesh(axis_name="sequencer", num_cores=1), name=name,
               scratch_types=_comm_sems(n, waves), compiler_params=pltpu.CompilerParams(collective_id=collective_id))
    def launch(send_sems, recv_sems, local_sems):
        _handshake(peers_fn())
        copies_fn(srcs, dsts, send_sems, recv_sems, local_sems, **extra)

    launch()
    return [d[...] for d in dsts]


def _all_other_devices():
    x, y, c = _position()
    return [(1 - x if fx else x, 1 - y if fy else y, 1 - c if fc else c) for fx, fy, fc in FLIPS]


def _gather_relay_peers():
    x, y, c = _position()
    return [(x, y, 1 - c), (1 - x, y, c), (x, 1 - y, c)]


def _sibling():
    x, y, c = _position()
    return [(x, y, 1 - c)]


def _same_core_of_other_chips():
    x, y, c = _position()
    return [(1 - x, y, c), (x, 1 - y, c), (1 - x, 1 - y, c)]


def _pair_exchange_async(parts, *, name, collective_id):
    return _sequencer_call(parts, [jax.ShapeDtypeStruct(p.shape, p.dtype) for p in parts],
                           _pair_copies, _sibling, name=name, collective_id=collective_id)


def _chip_scatter_async(parts, *, name, collective_id, waves=1):
    return _sequencer_call(parts, [jax.ShapeDtypeStruct(p.shape, p.dtype) for p in parts],
                           _chip_scatter_copies, _same_core_of_other_chips, name=name, collective_id=collective_id,
                           waves=waves)


def _pair_sum_call(mine, theirs, *, name, tw=D_MODEL):
    n, r, w = mine.shape

    def body(a_ref, b_ref, o_ref):
        o_ref[...] = (a_ref[...].astype(F32) + b_ref[...].astype(F32)).astype(o_ref.dtype)

    spec = pl.BlockSpec((None, r, tw), lambda i, j: (i, 0, j))
    return pl.pallas_call(
        body, name=name, grid=(n, w // tw), in_specs=[spec, spec], out_specs=spec,
        out_shape=pltpu.HBM(mine.shape, mine.dtype),
        compiler_params=_cp(("parallel", "parallel")),
    )(*_hbm(mine, theirs))


def _scatter_blocks_async(parts, *, name, collective_id, waves=1):
    return _sequencer_call(parts, [jax.ShapeDtypeStruct(p.shape, p.dtype) for p in parts],
                           _scatter_copies, _all_other_devices, name=name, collective_id=collective_id, waves=waves)


def _all_gather_async(shards, *, name, collective_id):
    return _sequencer_call(shards, [jax.ShapeDtypeStruct((N_DEV, *s.shape), s.dtype) for s in shards],
                           _gather_copies_relayed, _gather_relay_peers, name=name, collective_id=collective_id)


def _adamw_math(w, g, m, v):
    m = ADAM_B1 * m + (1.0 - ADAM_B1) * g
    v = ADAM_B2 * v + (1.0 - ADAM_B2) * (g * g)
    m_hat = m / (1.0 - ADAM_B1 ** ADAM_STEP)
    v_hat = v / (1.0 - ADAM_B2 ** ADAM_STEP)
    delta = -ADAM_LR * (m_hat / (jnp.sqrt(v_hat) + ADAM_EPS) + ADAM_WD * w)
    return delta, m, v


def _adamw_reduce_call(recv, w, m, v, *, name, T=128):
    R, W = w.shape
    n_parts = recv.shape[0]
    if R % T == 0:
        tr, tw = T, W
    elif (R // 2) % 16 == 0:
        tr, tw = R // 2, W
    else:
        tr, tw = R, 2 * LANES

    def body(p_ref, w_ref, m_ref, v_ref, g_out, d_out, m_out, v_out):
        g = p_ref[0].astype(F32)
        for d in range(1, n_parts):
            g = g + p_ref[d].astype(F32)
        g_out[...] = g
        d_out[...], m_out[...], v_out[...] = _adamw_math(w_ref[...], g, m_ref[...], v_ref[...])

    row = pl.BlockSpec((tr, tw), lambda i, j: (i, j))
    out = pltpu.HBM((R, W), F32)
    return pl.pallas_call(
        body, name=name, grid=(R // tr, W // tw),
        in_specs=[pl.BlockSpec((n_parts, tr, tw), lambda i, j: (0, i, j)), row, row, row],
        out_specs=[row] * 4, out_shape=[out] * 4,
        compiler_params=_cp(("parallel", "parallel"), VMEM_BIG),
    )(*_hbm(recv, w, m, v))


SMALL_EARLY = (("b_gate", 8), ("w_spatial", 512), ("b_spatial", 8), ("norm_post_mix", 8), ("norm_pre_ffn", 8),
               ("norm_post_ffn", 8), ("w_gate_up", 128), ("gla_norm", 64), ("sgu_ln_g", 64), ("sgu_ln_b", 64))
SMALL_EARLY_AT = {}
for _name, _rows in SMALL_EARLY:
    SMALL_EARLY_AT[_name] = sum(r for _, r in SMALL_EARLY[:len(SMALL_EARLY_AT)])
SMALL_EARLY_ROWS = sum(r for _, r in SMALL_EARLY)
SMALL_LATE_ROWS = 16


def _small_early_rows(grads):
    def rows(a, n_rows):
        a = a.reshape(-1, LANES)
        return jnp.pad(a, ((0, n_rows - a.shape[0]), (0, 0)))

    def device_major(a, n_rows):
        r, c = a.shape[0], a.shape[1] // N_DEV
        a = a.reshape(r, N_DEV, c).transpose(1, 0, 2)
        a = jnp.pad(a, ((0, 0), (0, n_rows // N_DEV - r), (0, LANES - c)))
        return a.reshape(n_rows, LANES)

    pieces = []
    for name, n_rows in SMALL_EARLY:
        g = grads[name]
        if name in SMALL_SHARDED:
            pieces.append(device_major(g.reshape(g.shape[0], -1) if g.ndim == 2 else g.reshape(g.shape[0], g.shape[-1]), n_rows))
        else:
            pieces.append(rows(g, n_rows))
    return jnp.concatenate(pieces, axis=0)


def _small_update_call(early, dg_pre_mix, loss_part, w, m, v, *, name):
    names = list(SMALL)
    n_p = len(names)
    E = SMALL_EARLY_ROWS

    def reduce_body(early_ref, dg1_ref, loss_ref, tot, got_early, got_late, late, send_sems, recv_sems, local_sems):
        for k in range(D_MODEL // LANES):
            late[k:k + 1, :] = dg1_ref[:, k * LANES:(k + 1) * LANES]
        late[8:16, :] = jnp.broadcast_to(loss_ref[...], (8, LANES))
        _gather_copies([early_ref, late], [got_early, got_late], send_sems, recv_sems, local_sems)
        acc, acc_late = got_early[0], got_late[0]
        for d in range(1, N_DEV):
            acc, acc_late = acc + got_early[d], acc_late + got_late[d]
        tot[0:E, :] = acc
        tot[E:E + SMALL_LATE_ROWS, :] = acc_late

    total = pl.pallas_call(
        reduce_body, name=name + "_reduce",
        in_specs=[VMEM_SPEC] * 3, out_specs=VMEM_SPEC,
        out_shape=jax.ShapeDtypeStruct((E + SMALL_LATE_ROWS, LANES), F32),
        scratch_shapes=[pltpu.VMEM((N_DEV, E, LANES), F32), pltpu.VMEM((N_DEV, SMALL_LATE_ROWS, LANES), F32),
                        pltpu.VMEM((SMALL_LATE_ROWS, LANES), F32),
                        pltpu.SemaphoreType.DMA((14,)), pltpu.SemaphoreType.DMA((14,)), pltpu.SemaphoreType.DMA((2,))],
    )(early, dg_pre_mix, loss_part)

    def body(tot, *rest):
        w_refs, m_refs, v_refs = [dict(zip(names, rest[i * n_p:(i + 1) * n_p])) for i in range(3)]
        outs = rest[3 * n_p:]
        loss_out = outs[0]
        g_out, d_out, m_out, v_out = [dict(zip(names, outs[1 + i * n_p:1 + (i + 1) * n_p])) for i in range(4)]
        loss_out[...] = tot[E + 8:E + 9, :]

        x, y, c = _position()
        me = 4 * x + 2 * y + c

        def update(name, g, ix):
            g_out[name][ix] = g
            d_out[name][ix], m_out[name][ix], v_out[name][ix] = _adamw_math(
                w_refs[name][ix], g, m_refs[name][ix], v_refs[name][ix])

        for name in names:
            shape = w[name].shape
            if name in SMALL_SHARDED:
                per_dev = dict(SMALL_EARLY)[name] // N_DEV
                at = pl.multiple_of(SMALL_EARLY_AT[name] + me * per_dev, 8)
                g = tot[pl.ds(at, per_dev), :]
                update(name, g[:shape[1], :shape[2]], (0,))
            elif name == "w_spatial":
                for grp in range(GROUPS):
                    at = SMALL_EARLY_AT[name] + grp * SBLOCK
                    update(name, tot[at:at + SBLOCK, :], (0, grp))
            elif name == "b_spatial":
                at = SMALL_EARLY_AT[name]
                update(name, tot[at:at + GROUPS, :], (0,))
            else:
                at = E if name == "norm_pre_mix" else SMALL_EARLY_AT[name]
                for k in range(shape[1] // LANES):
                    update(name, tot[at + k:at + k + 1, :], (slice(None), pl.ds(k * LANES, LANES)))

    state = [s[n] for s in (w, m, v) for n in names]
    out_shapes = [jax.ShapeDtypeStruct((1, LANES), F32)] + [jax.ShapeDtypeStruct(w[n].shape, F32) for n in names] * 4
    outs = pl.pallas_call(
        body, name=name + "_adamw",
        in_specs=[VMEM_SPEC] * (1 + len(state)), out_specs=[VMEM_SPEC] * len(out_shapes), out_shape=out_shapes,
    )(total, *state)
    per_name = {n: tuple(outs[1 + i * n_p + j] for i in range(4)) for j, n in enumerate(names)}
    return outs[0], per_name


def _tile_rows(n_elems):
    return -(-n_elems // (8 * LANES)) * 8


def _pack_rows(parts, rows):
    pieces = []
    for p in parts:
        q = p.reshape(-1, LANES)
        pieces.append(jnp.pad(q, ((0, _tile_rows(p.size) - q.shape[0]), (0, 0))))
    buf = jnp.concatenate(pieces, axis=0)
    return jnp.pad(buf, ((0, rows - buf.shape[0]), (0, 0)))


def _in_w_blocks_call(a, live, finished, *, name, tk=TOKEN_TILE):
    S = a.shape[0]
    tk = min(tk, S)
    steps = S // tk
    n_live = len(live)
    n_chips = N_DEV // 2
    out_shape = (n_chips, IN_BLK, D_MODEL)

    def body(a_ref, *rest):
        live_refs, done_refs = rest[:n_live], rest[n_live:n_live + len(finished)]
        keep_ref, send_ref = rest[n_live + len(finished):n_live + len(finished) + 2]
        accs = rest[n_live + len(finished) + 2:]
        k = pl.program_id(0)

        @pl.when(k == 0)
        def _():
            for acc in accs:
                acc[...] = jnp.zeros_like(acc)

        av = a_ref[...]
        for src, acc in zip(live_refs, accs):
            for m0 in range(0, src.shape[1], 1024):
                m1 = min(src.shape[1], m0 + 1024)
                acc[m0:m1, :] += _dot(src[:, m0:m1], av, TN)

        @pl.when(k == steps - 1)
        def _():
            groups = [(acc, cols) for acc, (_, cols) in zip(accs, live)]
            groups += [(ref, cols) for ref, (_, cols) in zip(done_refs, finished)]
            core = lax.axis_index("c")

            def cut(d, out_ref):
                lo, hi = IN_BLK * d, IN_BLK * (d + 1)
                for ref, (c0, c1) in groups:
                    s0, e0 = max(lo, c0), min(hi, c1)
                    if s0 < e0:
                        out_ref[d // 2, s0 - lo:e0 - lo, :] = ref[s0 - c0:e0 - c0, :].astype(BF16)

            for d in range(N_DEV):
                pl.when(core == d % 2)(lambda d=d: cut(d, keep_ref))
                pl.when(core != d % 2)(lambda d=d: cut(d, send_ref))

    acc_shapes = [(arr.shape[1], D_MODEL) for arr, _ in live]
    tiles = [((tk, D_MODEL), BF16)] + [((tk, arr.shape[1]), BF16) for arr, _ in live]
    resident = ([(arr.shape, arr.dtype) for arr, _ in finished] + [(out_shape, BF16)] * 2
                + [(s, F32) for s in acc_shapes])
    return pl.pallas_call(
        body, name=name, grid=(steps,),
        in_specs=[pl.BlockSpec((tk, D_MODEL), lambda k: (k, 0))]
        + [pl.BlockSpec((tk, arr.shape[1]), lambda k: (k, 0)) for arr, _ in live]
        + [_res(arr.shape) for arr, _ in finished],
        out_specs=[_acc(out_shape)] * 2,
        out_shape=[pltpu.HBM(out_shape, BF16)] * 2,
        scratch_shapes=[pltpu.VMEM(s, F32) for s in acc_shapes],
        compiler_params=_cp(("arbitrary",), _vmem_limit(tiles, resident)),
    )(*_hbm(a, *[arr for arr, _ in live], *[arr for arr, _ in finished]))


def _local_step(x, target, W, scatter, finish, pair, chip_scatter):
    g1, g2, g3, g4 = [W[n].reshape(1, D_MODEL) for n in ("norm_pre_mix", "norm_post_mix", "norm_pre_ffn", "norm_post_ffn")]
    wfi, wfo = W["w_ffn_in"], W["w_ffn_out"].reshape(4, FF_BLK, D_MODEL)
    wg = jnp.pad(W["w_gate_up"], ((0, LANES - RANK), (0, 0))).astype(BF16)
    bgate = W["b_gate"].reshape(1, QK)
    gnorm = W["gla_norm"].reshape(HEADS, 1, DV)
    ln_g = W["sgu_ln_g"].reshape(GROUPS, 1, DG)
    ln_b = W["sgu_ln_b"].reshape(GROUPS, 1, DG)
    w_sp = W["w_spatial"]
    b_sp = W["b_spatial"].reshape(GROUPS, SBLOCK, 1)

    a, pa, alow, ps, pg, w_in_t = _in_proj_call(x, g1, W["w_in_blocks"], name="in_proj")
    y_gla, states = _gla_fwd_call(pa, alow, wg, bgate, gnorm, name="gla_fwd")
    y_sgu = _sgu_fwd_call(ps, ln_g, ln_b, w_sp, b_sp, name="sgu_fwd")
    t1, t2, merged, mix, x1, h = _mixer_tail_call(y_gla, y_sgu, pg, x, W["w_branch_gla"], W["w_branch_sgu"],
                                                  W["w_out"], g2, g3, name="mixer_tail")
    gu, z = _ffn_in_call(h, wfi, name="ffn_in")
    loss, dx2, dy, dg4 = _ffn_out_loss_call(z, wfo, x1, target, g4, name="ffn_out_loss")

    grads = {"norm_post_ffn": dg4}
    done = {}
    dgu = _ffn_out_bwd_call(dy, wfo, gu, name="ffn_out_bwd")
    dw_ffn_out = _weight_grads_call([z, dy], [(0, 1)], name="d_ffn_out_w")[0].reshape(N_DEV, D_FF // N_DEV, D_MODEL)
    dw_ffn_in = _ffn_in_grad_call(h, dgu, name="d_ffn_in_w")
    dgu, dw_ffn_out, dw_ffn_in = lax.optimization_barrier((dgu, dw_ffn_out, dw_ffn_in))
    ffn_received = scatter(("w_ffn_out", "w_ffn_in"), [dw_ffn_out, dw_ffn_in])
    dx1, dmix, grads["norm_pre_ffn"], grads["norm_post_mix"] = _ffn_in_bwd_call(dgu, wfi, dx2, x1, mix, g3, g2, name="ffn_in_bwd")
    dt1, dt2, dpg, dy_gla, dy_sgu = _mixer_bwd_call(dmix, t1, t2, pg, W["w_out"], W["w_branch_gla"], W["w_branch_sgu"],
                                                    name="mixer_bwd")
    rows = D_MODEL // N_DEV
    mixer_grads = _weight_grads_call([merged, dmix, y_gla, dt1, y_sgu, dt2], [(0, 1), (2, 3), (4, 5)], name="d_mixer_w")
    dy_gla, dy_sgu, mixer_grads = lax.optimization_barrier((dy_gla, dy_sgu, mixer_grads))
    mixer_received = scatter(("w_out", "w_branch_gla", "w_branch_sgu"),
                             [g.reshape(N_DEV, rows, D_MODEL) for g in mixer_grads])
    dps, dlg, dlb, dwsp, dbsp = _sgu_bwd_call(ps, ln_g, ln_b, w_sp, b_sp, dy_sgu, name="sgu_bwd")
    dw_s, dw_g = _weight_grads_call([a, dps, dpg], [(1, 0), (2, 0)], name="d_in_w_sgu_gates")
    dy_gla, dw_s, dw_g = lax.optimization_barrier((dy_gla, dw_s, dw_g))
    dpa, dlogit, dgn, dbg = _gla_bwd_call(pa, alow, wg, bgate, gnorm, states, dy_gla, name="gla_bwd")
    ffn_received, mixer_received, dpa, dlogit = lax.optimization_barrier((ffn_received, mixer_received, dpa, dlogit))
    dlow = _mm(dlogit, wg, "nt", BF16, name="d_gate_up_x")
    keep, send = _in_w_blocks_call(a, [(dpa, A_COLS), (dlow, LOW_COLS)], [(dw_s, S_COLS), (dw_g, G_COLS)],
                                   name="d_in_w_blocks")
    ffn_received, mixer_received, send = lax.optimization_barrier((ffn_received, mixer_received, send))
    from_sibling = pair(send)
    done.update({**finish(ffn_received), **finish(mixer_received)})
    dwg = _mm(alow, dlogit, "tn", F32, name="d_gate_up_w")
    done, dwg, keep = lax.optimization_barrier((done, dwg, keep))
    chip_sum = _pair_sum_call(keep, from_sibling, name="pair_sum_w_in")
    chip_sum, dpa = lax.optimization_barrier((chip_sum, dpa))
    in_received = chip_scatter(chip_sum)
    grad_x, grads["norm_pre_mix"] = _in_proj_bwd_call(dpa, dlow, dps, dpg, w_in_t, x, dx1, g1, name="in_proj_bwd")

    grads["w_gate_up"] = dwg[:RANK]
    grads["b_gate"] = dbg
    grads["gla_norm"] = dgn
    grads["sgu_ln_g"] = dlg
    grads["sgu_ln_b"] = dlb
    grads["w_spatial"] = dwsp
    grads["b_spatial"] = dbsp
    done.update(finish({"w_in": in_received}))
    return loss, grad_x, grads, done


WEIGHTS = ("norm_pre_mix", "w_in", "w_gate_up", "b_gate", "gla_norm", "sgu_ln_g", "sgu_ln_b", "w_spatial",
           "b_spatial", "w_branch_gla", "w_branch_sgu", "w_out", "norm_post_mix", "norm_pre_ffn", "w_ffn_in",
           "w_ffn_out", "norm_post_ffn")
BIG = ("w_in", "w_branch_gla", "w_branch_sgu", "w_out", "w_ffn_in", "w_ffn_out")
MIXER = ("w_branch_gla", "w_branch_sgu", "w_out")
FFN = ("w_ffn_in", "w_ffn_out")
COLUMN_SHARDED = ("w_in", "w_ffn_in")
LATE_SCATTER_WAVES = 4
SMALL = tuple(n for n in WEIGHTS if n not in BIG)
SMALL_SHARDED = ("w_gate_up", "gla_norm", "sgu_ln_g", "sgu_ln_b")
SMALL_GATHER_ROWS = 32


def kernel(x, norm_pre_mix, w_in, w_gate_up, b_gate, gla_norm, sgu_ln_g, sgu_ln_b, w_spatial, b_spatial, w_branch_gla, w_branch_sgu, w_out, norm_post_mix, norm_pre_ffn, w_ffn_in, w_ffn_out, norm_post_ffn, loss_target, m_norm_pre_mix, m_w_in, m_w_gate_up, m_b_gate, m_gla_norm, m_sgu_ln_g, m_sgu_ln_b, m_w_spatial, m_b_spatial, m_w_branch_gla, m_w_branch_sgu, m_w_out, m_norm_post_mix, m_norm_pre_ffn, m_w_ffn_in, m_w_ffn_out, m_norm_post_ffn, v_norm_pre_mix, v_w_in, v_w_gate_up, v_b_gate, v_gla_norm, v_sgu_ln_g, v_sgu_ln_b, v_w_spatial, v_b_spatial, v_w_branch_gla, v_w_branch_sgu, v_w_out, v_norm_post_mix, v_norm_pre_ffn, v_w_ffn_in, v_w_ffn_out, v_norm_post_ffn):
    given = dict(locals())
    def local(a, n):
        return a[0].T if n in COLUMN_SHARDED else a[0]

    w = {n: local(given[n], n) for n in WEIGHTS}
    m = {n: local(given["m_" + n], n) for n in WEIGHTS}
    v = {n: local(given["v_" + n], n) for n in WEIGHTS}
    xs, target = x[0], loss_target[0]

    small_shard = _pack_rows([w[n] for n in SMALL_SHARDED], SMALL_GATHER_ROWS)
    first = _all_gather_hbm([w["w_in"].astype(BF16), small_shard], name="gather_w_in")
    rest, first = lax.optimization_barrier(([w[n].astype(BF16) for n in MIXER + FFN], first))
    rest_blocks = _all_gather_async(rest, name="gather_rest", collective_id=1)
    W = {n: w[n] for n in SMALL if n not in SMALL_SHARDED}
    blocks = dict(zip(MIXER + FFN, rest_blocks))
    for n in ("w_branch_gla", "w_branch_sgu", "w_out", "w_ffn_out"):
        W[n] = blocks[n].reshape(-1, D_MODEL)
    W["w_ffn_in"] = blocks["w_ffn_in"]
    W["w_in_blocks"] = first[0]
    small_blocks = first[1]
    off = 0
    for n in SMALL_SHARDED:
        r, c = w[n].shape
        blk = small_blocks[:, off:off + r * c // LANES].reshape(N_DEV, r, c)
        W[n] = blk.transpose(1, 0, 2).reshape(r, N_DEV * c)
        off += _tile_rows(r * c)

    scatter_ids = iter((3, 4))

    def scatter(names, parts):
        got = _scatter_blocks_async(parts, name="scatter_" + "_".join(names), collective_id=next(scatter_ids))
        return dict(zip(names, got))

    def finish(received):
        return {n: _adamw_reduce_call(r, w[n], m[n], v[n], name="adamw_" + n) for n, r in received.items()}

    def pair(part):
        return _pair_exchange_async([part], name="pair_w_in", collective_id=5)[0]

    def chip_scatter(part):
        return _chip_scatter_async([part], name="scatter_w_in", collective_id=6, waves=LATE_SCATTER_WAVES)[0]

    loss_part, grad_x, grads, big_done = _local_step(xs, target, W, scatter, finish, pair, chip_scatter)

    loss_row, small_done = _small_update_call(
        _small_early_rows(grads), grads["norm_pre_mix"], loss_part,
        *[{n: given[prefix + n] for n in SMALL} for prefix in ("", "m_", "v_")], name="small_update")

    def pick(i):
        return [small_done[n][i] if n in SMALL else (big_done[n][i].T if n in COLUMN_SHARDED else big_done[n][i])[None]
                for n in WEIGHTS]

    return (loss_row[0, 0], grad_x[None], *pick(0), *pick(1), *pick(2), *pick(3))
```

```python
import jax
import jax.numpy as jnp
from jax import lax
from jax.experimental import pallas as pl
from jax.experimental.pallas import tpu as pltpu
from jax.experimental.pallas import tpu_sc as plsc

F32 = jnp.float32
BF16 = jnp.bfloat16

D_MODEL = 1024
N_DEV = 8
CHUNK = 64
HEADS = 4
DK = 128
DV = 256
QK = HEADS * DK
GV = HEADS * DV
RANK = 16
GROUPS = 4
SBLOCK = 128
DG = 256
D_FF = 2816
FF_BLK = 704
EPS = 1e-6
Q_SCALE = DK ** -0.5
LANES = 128
VMEM_BIG = 48 * 1024 * 1024

D_IN = 7184
IN_BLK = 898
A_COLS = (0, 3072)
LOW_COLS = (3072, 3088)
S_COLS = (3088, 5136)
G_COLS = (5136, 7184)

ADAM_LR = 0.001
ADAM_B1 = 0.9
ADAM_B2 = 0.999
ADAM_EPS = 1e-08
ADAM_WD = 0.01
ADAM_STEP = 10

MESH = pl.DeviceIdType.MESH
ANY = pl.BlockSpec(memory_space=pl.ANY)
VMEM_SPEC = pl.BlockSpec(memory_space=pltpu.VMEM)


def _cp(sem=None, vmem=None):
    return pltpu.CompilerParams(dimension_semantics=sem, vmem_limit_bytes=vmem)


def _hbm(*arrays):
    return [pltpu.with_memory_space_constraint(a, pltpu.HBM) for a in arrays]


def _sig(x):
    return 0.5 * jnp.tanh(0.5 * x) + 0.5


GELU_C = 0.7978845608028654
GELU_A = 0.044715


def _gelu(x):
    t = jnp.tanh((GELU_C * x) * (1.0 + GELU_A * (x * x)))
    return (0.5 * x) * (1.0 + t)


def _gelu_and_grad(x):
    x2 = x * x
    t = jnp.tanh((GELU_C * x) * (1.0 + GELU_A * x2))
    one_t = 1.0 + t
    hx = 0.5 * x
    grad = 0.5 * one_t + (hx * (1.0 - t * t)) * (GELU_C + (3.0 * GELU_A * GELU_C) * x2)
    return hx * one_t, grad


def _logsig(x):
    return jnp.minimum(x, 0.0) - jnp.log1p(jnp.exp(-jnp.abs(x)))


def _dot(a, b, dims):
    return lax.dot_general(a, b, (dims, ((), ())), preferred_element_type=F32)


NN = ((1,), (0,))
NT = ((1,), (1,))
TN = ((0,), (0,))


def _exact_mask_dot(mask_bf16, x):
    hi = x.astype(BF16)
    r1 = x - hi.astype(F32)
    mid = r1.astype(BF16)
    lo = (r1 - mid.astype(F32)).astype(BF16)
    return _dot(mask_bf16, hi, NN) + _dot(mask_bf16, mid, NN) + _dot(mask_bf16, lo, NN)


def _pick_tile(dim, target):
    if dim <= target:
        return dim
    best = None
    for t in range(LANES, int(1.4 * target) + 1, LANES):
        if dim % t == 0:
            best = t
    assert best is not None, (dim, target)
    return best


def _mm_call(a, b, *, name, grid, a_spec, b_spec, o_spec, out_shape, dims, acc_shape):
    nk = grid[2]

    def body(a_ref, b_ref, o_ref, *acc):
        part = _dot(a_ref[...], b_ref[...], dims)
        if nk == 1:
            o_ref[...] = part.astype(o_ref.dtype)
        else:
            acc_ref = acc[0]
            k = pl.program_id(2)

            @pl.when(k == 0)
            def _():
                acc_ref[...] = part

            @pl.when(k > 0)
            def _():
                acc_ref[...] += part

            @pl.when(k == nk - 1)
            def _():
                o_ref[...] = acc_ref[...].astype(o_ref.dtype)

    return pl.pallas_call(
        body, name=name, grid=grid, in_specs=[a_spec, b_spec], out_specs=o_spec, out_shape=out_shape,
        scratch_shapes=[] if nk == 1 else [pltpu.VMEM(acc_shape, F32)],
        compiler_params=_cp(("parallel", "parallel", "arbitrary"), VMEM_BIG),
    )(a, b)


def _mm(a, b, mode, out_dtype, *, name, tm=512, tn=1024, tk=1024):
    if mode == "nn":
        (M, K), (_, N) = a.shape, b.shape
    elif mode == "nt":
        (M, K), (N, _) = a.shape, b.shape
    else:
        (K, M), (_, N) = a.shape, b.shape
    tm, tn, tk = _pick_tile(M, tm), _pick_tile(N, tn), _pick_tile(K, tk)
    if mode == "nn":
        a_spec = pl.BlockSpec((tm, tk), lambda i, j, k: (i, k))
        b_spec = pl.BlockSpec((tk, tn), lambda i, j, k: (k, j))
        dims = NN
    elif mode == "nt":
        a_spec = pl.BlockSpec((tm, tk), lambda i, j, k: (i, k))
        b_spec = pl.BlockSpec((tn, tk), lambda i, j, k: (j, k))
        dims = NT
    else:
        a_spec = pl.BlockSpec((tk, tm), lambda i, j, k: (k, i))
        b_spec = pl.BlockSpec((tk, tn), lambda i, j, k: (k, j))
        dims = TN
    return _mm_call(a, b, name=name, grid=(M // tm, N // tn, K // tk), a_spec=a_spec, b_spec=b_spec,
                    o_spec=pl.BlockSpec((tm, tn), lambda i, j, k: (i, j)),
                    out_shape=jax.ShapeDtypeStruct((M, N), out_dtype), dims=dims, acc_shape=(tm, tn))


ROW_TILE = 512
SUB_ROWS = 256


def _sub_tiles(T):
    return [pl.ds(r0, min(SUB_ROWS, T)) for r0 in range(0, T, SUB_ROWS)]


def _rt(T, W, c=0):
    return pl.BlockSpec((T, W), lambda i: (i, c))


def _rt3(nb, T, W):
    return pl.BlockSpec((nb, T, W), lambda i: (0, i, 0))


def _res(shape):
    nd = len(shape)
    return pl.BlockSpec(tuple(shape), lambda i: (0,) * nd, pipeline_mode=pl.Buffered(1))


def _acc(shape):
    nd = len(shape)
    return pl.BlockSpec(tuple(shape), lambda i: (0,) * nd, pipeline_mode=pl.Buffered(1))


def _nbytes(shape, dtype):
    n = jnp.dtype(dtype).itemsize
    for s in shape:
        n *= s
    return n


def _vmem_limit(tiles, resident, temps=16 * 1024 * 1024):
    need = 2 * sum(_nbytes(s, d) for s, d in tiles) + sum(_nbytes(s, d) for s, d in resident) + temps
    return min(need, 60 * 1024 * 1024)


def _tok_call(body, *, name, S, T, ins, outs, semantics="parallel"):
    tiles = [(spec.block_shape, a.dtype) for a, spec, kind in ins + outs if kind == "tile"]
    resident = [(a.shape, a.dtype) for a, spec, kind in ins + outs if kind == "res"]
    return pl.pallas_call(
        body, name=name, grid=(S // T,),
        in_specs=[spec for _, spec, _ in ins], out_specs=[spec for _, spec, _ in outs],
        out_shape=[pltpu.HBM(a.shape, a.dtype) for a, _, _ in outs],
        compiler_params=_cp((semantics,), _vmem_limit(tiles, resident)),
    )(*_hbm(*[a for a, _, _ in ins]))


def _tile(a, spec):
    return (a, spec, "tile")


def _whole(a):
    return (a, _res(a.shape), "res")


def _out_tile(shape, dtype, spec):
    return (jax.ShapeDtypeStruct(shape, dtype), spec, "tile")


def _out_acc(shape, dtype=F32):
    return (jax.ShapeDtypeStruct(shape, dtype), _acc(shape), "res")


def _rms_stats(x):
    r = lax.rsqrt(jnp.mean(x * x, axis=-1, keepdims=True) + EPS)
    return r, x * r


def _rms_bwd(xh, r, g, dy):
    dxh = dy * g
    dx = r * (dxh - xh * jnp.mean(dxh * xh, axis=-1, keepdims=True))
    dg = jnp.sum(dy * xh, axis=0, keepdims=True)
    return dx, dg


def _accum(ref, val):
    @pl.when(pl.program_id(0) == 0)
    def _():
        ref[...] = val

    @pl.when(pl.program_id(0) > 0)
    def _():
        ref[...] += val


def _dot_rows_t(a, w_ref, row0, o_ref, chunk=1024):
    n = o_ref.shape[1]
    for n0 in range(0, n, chunk):
        n1 = min(n, n0 + chunk)
        o_ref[:, n0:n1] = _dot(a, w_ref[row0 + n0:row0 + n1, :], NT).astype(o_ref.dtype)


def _in_proj_call(x, g1, w_blocks, *, name, T=ROW_TILE):
    S = x.shape[0]

    def body(x_ref, g_ref, blk_ref, a_ref, pa_ref, al_ref, ps_ref, pg_ref, w_ref):
        @pl.when(pl.program_id(0) == 0)
        def _():
            for d in range(N_DEV):
                w_ref[d * IN_BLK:(d + 1) * IN_BLK, :] = blk_ref[d]

        _, xh = _rms_stats(x_ref[...])
        a = (xh * g_ref[...]).astype(BF16)
        a_ref[...] = a
        _dot_rows_t(a, w_ref, A_COLS[0], pa_ref)
        _dot_rows_t(a, w_ref, LOW_COLS[0], al_ref)
        _dot_rows_t(a, w_ref, S_COLS[0], ps_ref)
        _dot_rows_t(a, w_ref, G_COLS[0], pg_ref)

    widths = (D_MODEL, A_COLS[1] - A_COLS[0], LANES, S_COLS[1] - S_COLS[0], G_COLS[1] - G_COLS[0])
    return _tok_call(
        body, name=name, S=S, T=T, semantics="arbitrary",
        ins=[_tile(x, _rt(T, D_MODEL)), _whole(g1), _whole(w_blocks)],
        outs=[_out_tile((S, w), BF16, _rt(T, w)) for w in widths] + [_out_acc((D_IN, D_MODEL), BF16)])


def _mixer_tail_call(y_gla, y_sgu, pg, x, w_bg, w_bs, w_out, g2, g3, *, name, T=ROW_TILE):
    S = x.shape[0]

    def body(yg_ref, ys_ref, pg_ref, x_ref, wbg_ref, wbs_ref, wo_ref, g2_ref, g3_ref,
             t1_ref, t2_ref, mg_ref, mix_ref, x1_ref, h_ref):
        for rows in _sub_tiles(T):
            t1 = _dot(yg_ref[rows, :], wbg_ref[...], NN)
            t2 = _dot(ys_ref[rows, :], wbs_ref[...], NN)
            t1_ref[rows, :] = t1.astype(BF16)
            t2_ref[rows, :] = t2.astype(BF16)
            sg = _sig(pg_ref[rows, pl.ds(0, D_MODEL)].astype(F32))
            ss = _sig(pg_ref[rows, pl.ds(D_MODEL, D_MODEL)].astype(F32))
            merged = (sg * t1 + ss * t2).astype(BF16)
            mg_ref[rows, :] = merged
            mix = _dot(merged, wo_ref[...], NN)
            mix_ref[rows, :] = mix
            _, mh = _rms_stats(mix)
            x1 = x_ref[rows, :] + mh * g2_ref[...]
            x1_ref[rows, :] = x1
            _, xh = _rms_stats(x1)
            h_ref[rows, :] = (xh * g3_ref[...]).astype(BF16)

    row = _rt(T, D_MODEL)
    b16 = lambda: _out_tile((S, D_MODEL), BF16, row)
    f32 = lambda: _out_tile((S, D_MODEL), F32, row)
    return _tok_call(
        body, name=name, S=S, T=T,
        ins=[_tile(y_gla, row), _tile(y_sgu, row), _tile(pg, _rt(T, 2 * D_MODEL)), _tile(x, row),
             _whole(w_bg), _whole(w_bs), _whole(w_out), _whole(g2), _whole(g3)],
        outs=[b16(), b16(), b16(), f32(), f32(), b16()])


def _ffn_in_call(h, wfi, *, name, T=ROW_TILE):
    S = h.shape[0]

    def body(h_ref, w_ref, gu_ref, z_ref):
        hv = h_ref[...]
        for d in range(4):
            gate = _dot(hv, w_ref[d], NT)
            up = _dot(hv, w_ref[d + 4], NT)
            gu_ref[d] = gate.astype(BF16)
            gu_ref[d + 4] = up.astype(BF16)
            z_ref[d] = (gate * _sig(gate) * up).astype(BF16)

    return _tok_call(
        body, name=name, S=S, T=T,
        ins=[_tile(h, _rt(T, D_MODEL)), _whole(wfi)],
        outs=[_out_tile((N_DEV, S, FF_BLK), BF16, _rt3(N_DEV, T, FF_BLK)),
              _out_tile((4, S, FF_BLK), BF16, _rt3(4, T, FF_BLK))])


def _ffn_out_loss_call(z, wfo, x1, target, g4, *, name, T=ROW_TILE):
    S = x1.shape[0]

    def body(z_ref, w_ref, x1_ref, t_ref, g4_ref, loss_ref, dx2_ref, dy_ref, dg4_ref):
        loss = jnp.zeros((1, 1), F32)
        dg4 = jnp.zeros((1, D_MODEL), F32)
        for rows in _sub_tiles(T):
            y = _dot(z_ref[0, rows, :], w_ref[0], NN)
            for d in range(1, 4):
                y = y + _dot(z_ref[d, rows, :], w_ref[d], NN)
            r, yh = _rms_stats(y)
            diff = x1_ref[rows, :] + yh * g4_ref[...] - t_ref[rows, :]
            loss = loss + 0.5 * jnp.sum(jnp.mean(diff * diff, axis=-1, keepdims=True), axis=0, keepdims=True)
            dx2 = diff * (1.0 / D_MODEL)
            dx2_ref[rows, :] = dx2
            dy, dg = _rms_bwd(yh, r, g4_ref[...], dx2)
            dy_ref[rows, :] = dy.astype(BF16)
            dg4 = dg4 + dg
        _accum(loss_ref, jnp.broadcast_to(loss, (1, LANES)))
        _accum(dg4_ref, dg4)

    row = _rt(T, D_MODEL)
    return _tok_call(
        body, name=name, S=S, T=T, semantics="arbitrary",
        ins=[_tile(z, _rt3(4, T, FF_BLK)), _whole(wfo), _tile(x1, row), _tile(target, row), _whole(g4)],
        outs=[_out_acc((1, LANES)), _out_tile((S, D_MODEL), F32, row), _out_tile((S, D_MODEL), BF16, row),
              _out_acc((1, D_MODEL))])


def _ffn_out_bwd_call(dy, wfo, gu, *, name, T=ROW_TILE):
    S = dy.shape[0]

    def body(dy_ref, w_ref, gu_ref, dgu_ref):
        dyv = dy_ref[...]
        for d in range(4):
            dz = _dot(dyv, w_ref[d], NT).astype(BF16)
            gt = gu_ref[d]
            up = gu_ref[d + 4]
            s = _sig(gt)
            dgu_ref[d] = dz * up * s * (1.0 + gt * (1.0 - s))
            dgu_ref[d + 4] = dz * gt * s

    blocks = _rt3(N_DEV, T, FF_BLK)
    return _tok_call(
        body, name=name, S=S, T=T,
        ins=[_tile(dy, _rt(T, D_MODEL)), _whole(wfo), _tile(gu, blocks)],
        outs=[_out_tile((N_DEV, S, FF_BLK), BF16, blocks)])[0]


def _ffn_in_bwd_call(dgu, wfi, dx2, x1, mix, g3, g2, *, name, T=ROW_TILE):
    S = x1.shape[0]

    def body(dgu_ref, w_ref, dx2_ref, x1_ref, mix_ref, g3_ref, g2_ref, dx1_ref, dmix_ref, dg3_ref, dg2_ref):
        dg3 = jnp.zeros((1, D_MODEL), F32)
        dg2 = jnp.zeros((1, D_MODEL), F32)
        for rows in _sub_tiles(T):
            dh = _dot(dgu_ref[0, rows, :], w_ref[0], NN)
            for d in range(1, N_DEV):
                dh = dh + _dot(dgu_ref[d, rows, :], w_ref[d], NN)
            r3, xh = _rms_stats(x1_ref[rows, :])
            d3, g = _rms_bwd(xh, r3, g3_ref[...], dh)
            dg3 = dg3 + g
            dx1 = dx2_ref[rows, :] + d3
            dx1_ref[rows, :] = dx1
            r2, mh = _rms_stats(mix_ref[rows, :])
            dmix, g = _rms_bwd(mh, r2, g2_ref[...], dx1)
            dg2 = dg2 + g
            dmix_ref[rows, :] = dmix.astype(BF16)
        _accum(dg3_ref, dg3)
        _accum(dg2_ref, dg2)

    row = _rt(T, D_MODEL)
    return _tok_call(
        body, name=name, S=S, T=T, semantics="arbitrary",
        ins=[_tile(dgu, _rt3(N_DEV, T, FF_BLK)), _whole(wfi), _tile(dx2, row), _tile(x1, row), _tile(mix, row),
             _whole(g3), _whole(g2)],
        outs=[_out_tile((S, D_MODEL), F32, row), _out_tile((S, D_MODEL), BF16, row),
              _out_acc((1, D_MODEL)), _out_acc((1, D_MODEL))])


def _mixer_bwd_call(dmix, t1, t2, pg, w_out, w_bg, w_bs, *, name, T=ROW_TILE):
    S = dmix.shape[0]

    def body(dmix_ref, t1_ref, t2_ref, pg_ref, wo_ref, wbg_ref, wbs_ref, dt1_ref, dt2_ref, dpg_ref, dyg_ref, dys_ref):
        gg, gs = pl.ds(0, D_MODEL), pl.ds(D_MODEL, D_MODEL)
        for rows in _sub_tiles(T):
            dm = _dot(dmix_ref[rows, :], wo_ref[...], NT)
            sg = _sig(pg_ref[rows, gg].astype(F32))
            ss = _sig(pg_ref[rows, gs].astype(F32))
            dt1 = (dm * sg).astype(BF16)
            dt2 = (dm * ss).astype(BF16)
            dt1_ref[rows, :] = dt1
            dt2_ref[rows, :] = dt2
            dpg_ref[rows, gg] = (dm * t1_ref[rows, :].astype(F32) * sg * (1.0 - sg)).astype(BF16)
            dpg_ref[rows, gs] = (dm * t2_ref[rows, :].astype(F32) * ss * (1.0 - ss)).astype(BF16)
            dyg_ref[rows, :] = _dot(dt1, wbg_ref[...], NT).astype(BF16)
            dys_ref[rows, :] = _dot(dt2, wbs_ref[...], NT).astype(BF16)

    row = _rt(T, D_MODEL)
    wide = _rt(T, 2 * D_MODEL)
    b16 = lambda: _out_tile((S, D_MODEL), BF16, row)
    return _tok_call(
        body, name=name, S=S, T=T,
        ins=[_tile(dmix, row), _tile(t1, row), _tile(t2, row), _tile(pg, wide), _whole(w_out), _whole(w_bg), _whole(w_bs)],
        outs=[b16(), b16(), _out_tile((S, 2 * D_MODEL), BF16, wide), b16(), b16()])


def _in_proj_bwd_call(dpa, dlow, dps, dpg, w_in_t, x, dx1, g1, *, name, T=ROW_TILE):
    S = x.shape[0]

    def body(dpa_ref, dl_ref, dps_ref, dpg_ref, w_ref, x_ref, dx1_ref, g1_ref, gx_ref, dg1_ref):
        da = (_dot(dpa_ref[...], w_ref[A_COLS[0]:A_COLS[1], :], NN)
              + _dot(dl_ref[...], w_ref[LOW_COLS[0]:LOW_COLS[0] + LANES, :], NN)
              + _dot(dps_ref[...], w_ref[S_COLS[0]:S_COLS[1], :], NN)
              + _dot(dpg_ref[...], w_ref[G_COLS[0]:G_COLS[1], :], NN))
        r, xh = _rms_stats(x_ref[...])
        dxa, dg = _rms_bwd(xh, r, g1_ref[...], da)
        gx_ref[...] = dx1_ref[...] + dxa
        _accum(dg1_ref, dg)

    row = _rt(T, D_MODEL)
    return _tok_call(
        body, name=name, S=S, T=T, semantics="arbitrary",
        ins=[_tile(dpa, _rt(T, dpa.shape[1])), _tile(dlow, _rt(T, dlow.shape[1])), _tile(dps, _rt(T, dps.shape[1])),
             _tile(dpg, _rt(T, dpg.shape[1])), _whole(w_in_t), _tile(x, row), _tile(dx1, row), _whole(g1)],
        outs=[_out_tile((S, D_MODEL), F32, row), _out_acc((1, D_MODEL))])


TOKEN_TILE = 512


def _weight_grads_part(arrays, pairs, *, tk=TOKEN_TILE, out_dtype=BF16):
    S = arrays[0].shape[-2]
    tk = min(tk, S)
    n_in = len(arrays)

    def out_shape(i, j):
        a, b = arrays[i], arrays[j]
        if a.ndim == 3:
            return (a.shape[0], a.shape[2], b.shape[1])
        if b.ndim == 3:
            return (b.shape[0], a.shape[1], b.shape[2])
        return (a.shape[1], b.shape[1])

    shapes = [out_shape(i, j) for i, j in pairs]

    in_place = out_dtype == F32

    def body(ins, outs, accs):
        k = pl.program_id(0)
        if in_place:
            accs = outs

        @pl.when(k == 0)
        def _():
            for acc in accs:
                acc[...] = jnp.zeros_like(acc)

        for (i, j), acc in zip(pairs, accs):
            a_ref, b_ref = ins[i], ins[j]
            if len(a_ref.shape) == 3:
                for n in range(a_ref.shape[0]):
                    acc[n] += _dot(a_ref[n], b_ref[...], TN)
            elif len(b_ref.shape) == 3:
                a = a_ref[...]
                for n in range(b_ref.shape[0]):
                    acc[n] += _dot(a, b_ref[n], TN)
            else:
                b = b_ref[...]
                for m0 in range(0, a_ref.shape[1], 1024):
                    m1 = min(a_ref.shape[1], m0 + 1024)
                    acc[m0:m1, :] += _dot(a_ref[:, m0:m1], b, TN)

        if not in_place:
            @pl.when(k == S // tk - 1)
            def _():
                for out, acc in zip(outs, accs):
                    out[...] = acc[...].astype(out.dtype)

    def in_spec(a):
        if a.ndim == 3:
            return pl.BlockSpec((a.shape[0], tk, a.shape[2]), lambda k: (0, k, 0))
        return pl.BlockSpec((tk, a.shape[1]), lambda k: (k, 0))

    return dict(
        body=body, steps=S // tk, arrays=list(arrays),
        in_specs=[in_spec(a) for a in arrays],
        out_specs=[_acc(s) for s in shapes],
        out_shapes=[pltpu.HBM(s, out_dtype) for s in shapes],
        scratch=[] if in_place else [pltpu.VMEM(s, F32) for s in shapes],
        tiles=[(in_spec(a).block_shape, a.dtype) for a in arrays],
        resident=[(s, F32) for s in shapes] + ([] if in_place else [(s, BF16) for s in shapes]))


def _weight_grads_call(arrays, pairs, *, name, tk=TOKEN_TILE):
    part = _weight_grads_part(arrays, pairs, tk=tk)
    n_in, n_out = len(part["arrays"]), len(part["out_shapes"])

    def body(*refs):
        part["body"](refs[:n_in], refs[n_in:n_in + n_out], refs[n_in + n_out:])

    return pl.pallas_call(
        body, name=name, grid=(part["steps"],),
        in_specs=part["in_specs"], out_specs=part["out_specs"], out_shape=part["out_shapes"],
        scratch_shapes=part["scratch"],
        compiler_params=_cp(("arbitrary",), _vmem_limit(part["tiles"], part["resident"])),
    )(*_hbm(*part["arrays"]))


def _ffn_in_grad_call(h, dgu, *, name, tk=TOKEN_TILE):
    S = h.shape[0]
    tk = min(tk, S)
    nb = 4
    shape = (nb, FF_BLK, D_MODEL)

    def body(h_ref, dgu_ref, out_ref, acc):
        k = pl.program_id(1)

        @pl.when(k == 0)
        def _():
            acc[...] = jnp.zeros_like(acc)

        hv = h_ref[...]
        for n in range(nb):
            acc[n] += _dot(dgu_ref[n], hv, TN)

        @pl.when(k == S // tk - 1)
        def _():
            out_ref[...] = acc[...].astype(out_ref.dtype)

    tiles = [((tk, D_MODEL), BF16), ((nb, tk, FF_BLK), BF16), (shape, BF16)]
    return pl.pallas_call(
        body, name=name, grid=(N_DEV // nb, S // tk),
        in_specs=[pl.BlockSpec((tk, D_MODEL), lambda g, k: (k, 0)),
                  pl.BlockSpec((nb, tk, FF_BLK), lambda g, k: (g, k, 0))],
        out_specs=pl.BlockSpec(shape, lambda g, k: (g, 0, 0)),
        out_shape=pltpu.HBM((N_DEV, FF_BLK, D_MODEL), BF16),
        scratch_shapes=[pltpu.VMEM(shape, F32)],
        compiler_params=_cp(("parallel", "arbitrary"), _vmem_limit(tiles, [(shape, F32)])),
    )(*_hbm(h, dgu))


GLA_TILE = 256
Q_OFF, K_OFF, V_OFF, R_OFF = 0, QK, 2 * QK, 2 * QK + GV


def _tri(lower):
    r = lax.broadcasted_iota(jnp.int32, (CHUNK, CHUNK), 0)
    c = lax.broadcasted_iota(jnp.int32, (CHUNK, CHUNK), 1)
    return jnp.where((r >= c) if lower else (c >= r), 1.0, 0.0).astype(BF16)


def _gla_fwd_call(pa, alow, wg, bgate, gnorm, *, name):
    S = pa.shape[0]
    Tg = min(GLA_TILE, S)
    cb = Tg // CHUNK

    def body(pa_ref, al_ref, wg_ref, bg_ref, gn_ref, y_ref, st_ref, state):
        @pl.when(pl.program_id(0) == 0)
        def _():
            state[...] = jnp.zeros_like(state)

        logit = _dot(al_ref[...], wg_ref[...], NN) + bg_ref[...]
        ls = _logsig(logit) * (1.0 / 16.0)
        tri = _tri(True)
        for c in range(cb):
            rows = pl.ds(c * CHUNK, CHUNK)
            cum = _exact_mask_dot(tri, ls[c * CHUNK:(c + 1) * CHUNK])
            tot = cum[CHUNK - 1:CHUNK]
            kd = (pa_ref[rows, pl.ds(K_OFF, QK)].astype(F32) * jnp.exp(tot - cum)).astype(BF16)
            decay = jnp.exp(tot)
            for h in range(HEADS):
                lanes = slice(h * DK, (h + 1) * DK)
                new = state[h] * decay[:, lanes] + _dot(pa_ref[rows, pl.ds(V_OFF + h * DV, DV)], kd[:, lanes], TN)
                state[h] = new
                st_ref[c, h] = new
        for c in range(cb):
            rows = pl.ds(c * CHUNK, CHUNK)
            for h in range(HEADS):
                qs = (pa_ref[rows, pl.ds(Q_OFF + h * DK, DK)].astype(F32) * Q_SCALE).astype(BF16)
                o = _dot(qs, st_ref[c, h].astype(BF16), NT)
                rs = lax.rsqrt(jnp.mean(o * o, axis=-1, keepdims=True) + EPS)
                rr = pa_ref[rows, pl.ds(R_OFF + h * DV, DV)].astype(F32)
                y_ref[rows, pl.ds(h * DV, DV)] = (o * rs * gn_ref[h] * (rr * _sig(rr))).astype(BF16)

    return pl.pallas_call(
        body, name=name, grid=(S // Tg,),
        in_specs=[
            pl.BlockSpec((Tg, 2 * QK + 2 * GV), lambda t: (t, 0)),
            pl.BlockSpec((Tg, LANES), lambda t: (t, 0)),
            pl.BlockSpec((LANES, QK), lambda t: (0, 0)),
            pl.BlockSpec((1, QK), lambda t: (0, 0)),
            pl.BlockSpec((HEADS, 1, DV), lambda t: (0, 0, 0)),
        ],
        out_specs=[
            pl.BlockSpec((Tg, GV), lambda t: (t, 0)),
            pl.BlockSpec((cb, HEADS, DV, DK), lambda t: (t, 0, 0, 0)),
        ],
        out_shape=[pltpu.HBM((S, GV), BF16), pltpu.HBM((S // CHUNK, HEADS, DV, DK), F32)],
        scratch_shapes=[pltpu.VMEM((HEADS, DV, DK), F32)],
        compiler_params=_cp(("arbitrary",), VMEM_BIG),
    )(*_hbm(pa, alow, wg, bgate, gnorm))


def _gla_bwd_call(pa, alow, wg, bgate, gnorm, states, dy, *, name):
    S = pa.shape[0]
    Tg = min(GLA_TILE, S)
    cb = Tg // CHUNK
    nt = S // Tg

    def rev(t):
        return nt - 1 - t

    def body(pa_ref, al_ref, wg_ref, bg_ref, gn_ref, st_ref, prev_ref, dy_ref,
             dpa_ref, dl_ref, dgn_ref, dbg_ref, carry, pbuf):
        t = pl.program_id(0)

        @pl.when(t == 0)
        def _():
            carry[...] = jnp.zeros_like(carry)
            dgn_ref[...] = jnp.zeros_like(dgn_ref)
            dbg_ref[...] = jnp.zeros_like(dbg_ref)

        logit = _dot(al_ref[...], wg_ref[...], NN) + bg_ref[...]
        ls = _logsig(logit) * (1.0 / 16.0)
        sneg = 1.0 / (1.0 + jnp.exp(logit))
        tri = _tri(True)
        upper = _tri(False)
        first_tile = rev(t) == 0
        heads = range(HEADS)

        w, decay, kd = [], [], []
        for c in range(cb):
            rows = pl.ds(c * CHUNK, CHUNK)
            cum = _exact_mask_dot(tri, ls[c * CHUNK:(c + 1) * CHUNK])
            tot = cum[CHUNK - 1:CHUNK]
            w.append(jnp.exp(tot - cum))
            decay.append(jnp.exp(tot))
            kd.append(pa_ref[rows, pl.ds(K_OFF, QK)].astype(F32) * w[c])

        dgn = [jnp.zeros((1, DV), F32) for _ in heads]
        for c in range(cb):
            rows = pl.ds(c * CHUNK, CHUNK)
            for h in heads:
                gn = gn_ref[h]
                qs = (pa_ref[rows, pl.ds(Q_OFF + h * DK, DK)].astype(F32) * Q_SCALE).astype(BF16)
                st16 = st_ref[c, h].astype(BF16)
                o = _dot(qs, st16, NT)
                rs = lax.rsqrt(jnp.mean(o * o, axis=-1, keepdims=True) + EPS)
                oh = o * rs
                rr = pa_ref[rows, pl.ds(R_OFF + h * DV, DV)].astype(F32)
                sr = _sig(rr)
                dyv = dy_ref[rows, pl.ds(h * DV, DV)].astype(F32)
                dpa_ref[rows, pl.ds(R_OFF + h * DV, DV)] = (
                    dyv * oh * gn * sr * (1.0 + rr * (1.0 - sr))).astype(BF16)
                don = dyv * (rr * sr)
                dgn[h] = dgn[h] + jnp.sum(don * oh, axis=0, keepdims=True)
                doh = don * gn
                do16 = (rs * (doh - oh * jnp.mean(doh * oh, axis=-1, keepdims=True))).astype(BF16)
                dpa_ref[rows, pl.ds(Q_OFF + h * DK, DK)] = (_dot(do16, st16, NN) * Q_SCALE).astype(BF16)
                pbuf[c, h] = _dot(do16, qs, TN)
        for h in heads:
            dgn_ref[h] += dgn[h]

        dkd = [[None] * HEADS for _ in range(cb)]
        ddecay = [[None] * HEADS for _ in range(cb)]
        for c in reversed(range(cb)):
            rows = pl.ds(c * CHUNK, CHUNK)
            for h in heads:
                lanes = slice(h * DK, (h + 1) * DK)
                gt = pbuf[c, h] + carry[h]
                gt16 = gt.astype(BF16)
                dkd[c][h] = _dot(pa_ref[rows, pl.ds(V_OFF + h * DV, DV)], gt16, NN)
                dpa_ref[rows, pl.ds(V_OFF + h * DV, DV)] = _dot(kd[c][:, lanes].astype(BF16), gt16, NT).astype(BF16)
                if c > 0:
                    st_prev = st_ref[c - 1, h]
                else:
                    st_prev = jnp.where(first_tile, 0.0, prev_ref[0, h])
                ddecay[c][h] = jnp.sum(gt * st_prev, axis=0, keepdims=True)
                carry[h] = gt * decay[c][:, lanes]

        dbg = jnp.zeros((1, QK), F32)
        for c in range(cb):
            rows = pl.ds(c * CHUNK, CHUNK)
            dkd_c = jnp.concatenate(dkd[c], axis=1)
            dpa_ref[rows, pl.ds(K_OFF, QK)] = (dkd_c * w[c]).astype(BF16)
            e = dkd_c * kd[c]
            dtot = jnp.sum(e, axis=0, keepdims=True) + jnp.concatenate(ddecay[c], axis=1) * decay[c]
            dls = dtot - _exact_mask_dot(upper, e)
            dlogit = dls * (1.0 / 16.0) * sneg[c * CHUNK:(c + 1) * CHUNK]
            dl_ref[rows, :] = dlogit.astype(BF16)
            dbg = dbg + jnp.sum(dlogit, axis=0, keepdims=True)
        dbg_ref[...] += dbg

    wide = 2 * QK + 2 * GV
    tiles = [((Tg, wide), BF16), ((cb + 1, HEADS, DV, DK), F32), ((Tg, GV), BF16), ((Tg, wide), BF16),
             ((Tg, QK), BF16)]
    resident = [((cb + 1, HEADS, DV, DK), F32)]
    return pl.pallas_call(
        body, name=name, grid=(nt,),
        in_specs=[
            pl.BlockSpec((Tg, wide), lambda t: (rev(t), 0)),
            pl.BlockSpec((Tg, LANES), lambda t: (rev(t), 0)),
            pl.BlockSpec((LANES, QK), lambda t: (0, 0)),
            pl.BlockSpec((1, QK), lambda t: (0, 0)),
            pl.BlockSpec((HEADS, 1, DV), lambda t: (0, 0, 0)),
            pl.BlockSpec((cb, HEADS, DV, DK), lambda t: (rev(t), 0, 0, 0)),
            pl.BlockSpec((1, HEADS, DV, DK), lambda t: (jnp.maximum(rev(t) * cb - 1, 0), 0, 0, 0)),
            pl.BlockSpec((Tg, GV), lambda t: (rev(t), 0)),
        ],
        out_specs=[
            pl.BlockSpec((Tg, wide), lambda t: (rev(t), 0)),
            pl.BlockSpec((Tg, QK), lambda t: (rev(t), 0)),
            pl.BlockSpec((HEADS, 1, DV), lambda t: (0, 0, 0)),
            pl.BlockSpec((1, QK), lambda t: (0, 0)),
        ],
        out_shape=[pltpu.HBM((S, wide), BF16), pltpu.HBM((S, QK), BF16),
                   jax.ShapeDtypeStruct((HEADS, 1, DV), F32), jax.ShapeDtypeStruct((1, QK), F32)],
        scratch_shapes=[pltpu.VMEM((HEADS, DV, DK), F32), pltpu.VMEM((cb, HEADS, DV, DK), F32)],
        compiler_params=_cp(("arbitrary",), _vmem_limit(tiles, resident)),
    )(*_hbm(pa, alow, wg, bgate, gnorm, states, states, dy))


SGU_TILE = 256


def _sgu_mask():
    r = lax.broadcasted_iota(jnp.int32, (SBLOCK, SBLOCK), 0)
    c = lax.broadcasted_iota(jnp.int32, (SBLOCK, SBLOCK), 1)
    return (c < CHUNK) | (r >= CHUNK)


def _ln_stats(vf):
    mu = jnp.mean(vf, axis=-1, keepdims=True)
    xc = vf - mu
    rs = lax.rsqrt(jnp.mean(xc * xc, axis=-1, keepdims=True) + EPS)
    return rs, xc * rs


def _sgu_fwd_call(ps, ln_g, ln_b, w_sp, b_sp, *, name):
    S = ps.shape[0]
    Ts = min(SGU_TILE, S)

    def body(ps_ref, lg_ref, lb_ref, w_ref, b_ref, y_ref):
        mask = _sgu_mask()
        for g in range(GROUPS):
            wm = jnp.where(mask, w_ref[g], 0.0).astype(BF16)
            for p in range(Ts // SBLOCK):
                rows = pl.ds(p * SBLOCK, SBLOCK)
                u = _gelu(ps_ref[rows, pl.ds(g * DG, DG)].astype(F32))
                _, xh = _ln_stats(_gelu(ps_ref[rows, pl.ds(D_MODEL + g * DG, DG)].astype(F32)))
                vn = xh * lg_ref[g] + lb_ref[g]
                mixed = _dot(wm, vn.astype(BF16), NN) + b_ref[g]
                y_ref[rows, pl.ds(g * DG, DG)] = (u * mixed).astype(BF16)

    full3 = lambda a, b, c: pl.BlockSpec((a, b, c), lambda t: (0, 0, 0))
    return pl.pallas_call(
        body, name=name, grid=(S // Ts,),
        in_specs=[pl.BlockSpec((Ts, 2 * D_MODEL), lambda t: (t, 0)),
                  full3(GROUPS, 1, DG), full3(GROUPS, 1, DG), full3(GROUPS, SBLOCK, SBLOCK), full3(GROUPS, SBLOCK, 1)],
        out_specs=pl.BlockSpec((Ts, D_MODEL), lambda t: (t, 0)),
        out_shape=pltpu.HBM((S, D_MODEL), BF16),
        compiler_params=_cp(("parallel",)),
    )(*_hbm(ps, ln_g, ln_b, w_sp, b_sp))


def _sgu_bwd_call(ps, ln_g, ln_b, w_sp, b_sp, dy, *, name):
    S = ps.shape[0]
    Ts = min(SGU_TILE, S)

    def body(ps_ref, lg_ref, lb_ref, w_ref, b_ref, dy_ref, ds_ref, dlg_ref, dlb_ref, dw_ref, db_ref):
        @pl.when(pl.program_id(0) == 0)
        def _():
            dlg_ref[...] = jnp.zeros_like(dlg_ref)
            dlb_ref[...] = jnp.zeros_like(dlb_ref)
            dw_ref[...] = jnp.zeros_like(dw_ref)
            db_ref[...] = jnp.zeros_like(db_ref)

        mask = _sgu_mask()
        for g in range(GROUPS):
            wm = jnp.where(mask, w_ref[g], 0.0).astype(BF16)
            lg = lg_ref[g]
            for p in range(Ts // SBLOCK):
                rows = pl.ds(p * SBLOCK, SBLOCK)
                su = ps_ref[rows, pl.ds(g * DG, DG)].astype(F32)
                sv = ps_ref[rows, pl.ds(D_MODEL + g * DG, DG)].astype(F32)
                u, du = _gelu_and_grad(su)
                gv, dgv = _gelu_and_grad(sv)
                rs, xh = _ln_stats(gv)
                vn16 = (xh * lg + lb_ref[g]).astype(BF16)
                mixed = _dot(wm, vn16, NN) + b_ref[g]
                dyv = dy_ref[rows, pl.ds(g * DG, DG)].astype(F32)
                ds_ref[rows, pl.ds(g * DG, DG)] = (dyv * mixed * du).astype(BF16)
                dmix = dyv * u
                dmix16 = dmix.astype(BF16)
                db_ref[g] += jnp.sum(dmix, axis=-1, keepdims=True)
                dw_ref[g] += jnp.where(mask, _dot(dmix16, vn16, NT), 0.0)
                dvn = _dot(wm, dmix16, TN)
                dlg_ref[g] += jnp.sum(dvn * xh, axis=0, keepdims=True)
                dlb_ref[g] += jnp.sum(dvn, axis=0, keepdims=True)
                dxh = dvn * lg
                dvf = rs * (dxh - jnp.mean(dxh, axis=-1, keepdims=True)
                            - xh * jnp.mean(dxh * xh, axis=-1, keepdims=True))
                ds_ref[rows, pl.ds(D_MODEL + g * DG, DG)] = (dvf * dgv).astype(BF16)

    full3 = lambda a, b, c: pl.BlockSpec((a, b, c), lambda t: (0, 0, 0))
    tiles = [((Ts, 2 * D_MODEL), BF16), ((Ts, D_MODEL), BF16), ((Ts, 2 * D_MODEL), BF16)]
    return pl.pallas_call(
        body, name=name, grid=(S // Ts,),
        in_specs=[pl.BlockSpec((Ts, 2 * D_MODEL), lambda t: (t, 0)),
                  full3(GROUPS, 1, DG), full3(GROUPS, 1, DG), full3(GROUPS, SBLOCK, SBLOCK), full3(GROUPS, SBLOCK, 1),
                  pl.BlockSpec((Ts, D_MODEL), lambda t: (t, 0))],
        out_specs=[pl.BlockSpec((Ts, 2 * D_MODEL), lambda t: (t, 0)),
                   full3(GROUPS, 1, DG), full3(GROUPS, 1, DG), full3(GROUPS, SBLOCK, SBLOCK), full3(GROUPS, SBLOCK, 1)],
        out_shape=[pltpu.HBM((S, 2 * D_MODEL), BF16),
                   jax.ShapeDtypeStruct((GROUPS, 1, DG), F32), jax.ShapeDtypeStruct((GROUPS, 1, DG), F32),
                   jax.ShapeDtypeStruct((GROUPS, SBLOCK, SBLOCK), F32),
                   jax.ShapeDtypeStruct((GROUPS, SBLOCK, 1), F32)],
        compiler_params=_cp(("arbitrary",), _vmem_limit(tiles, [])),
    )(*_hbm(ps, ln_g, ln_b, w_sp, b_sp, dy))


def _position():
    return lax.axis_index("x"), lax.axis_index("y"), lax.axis_index("c")


def _gather_copies(srcs, dsts, send_sems, recv_sems, local_sems):
    x, y, c = _position()
    me, sibling = (x, y, c), (x, y, 1 - c)
    chips = [(1 - x, y), (x, 1 - y), (1 - x, 1 - y)]
    n = len(srcs)

    def slab(a, block):
        px, py, pc = block
        return dsts[a].at[4 * px + 2 * py + pc]

    def copy(a, k, block, to, src=None):
        return pltpu.make_async_remote_copy(
            src_ref=slab(a, block) if src is None else src, dst_ref=slab(a, block),
            send_sem=send_sems.at[7 * a + k], recv_sem=recv_sems.at[7 * a + k], device_id=to, device_id_type=MESH)

    mine = [pltpu.make_async_copy(srcs[a], slab(a, me), local_sems.at[a]) for a in range(n)]
    for cp in mine:
        cp.start()
    first = []
    for a in range(n):
        first.append(copy(a, 0, me, sibling, src=srcs[a]))
        first += [copy(a, 1 + j, me, (*chip, c), src=srcs[a]) for j, chip in enumerate(chips)]
    for cp in first:
        cp.start()
    passed = []
    for j, chip in enumerate(chips):
        for a in range(n):
            copy(a, 1 + j, (*chip, c), me).wait_recv()
            cp = copy(a, 4 + j, (*chip, c), sibling)
            cp.start()
            passed.append(cp)
    for a in range(n):
        copy(a, 0, sibling, me).wait_recv()
        for j, chip in enumerate(chips):
            copy(a, 4 + j, (*chip, 1 - c), me).wait_recv()
    for cp in first + passed:
        cp.wait_send()
    for cp in mine:
        cp.wait()


def _gather_copies_relayed(srcs, dsts, send_sems, recv_sems, local_sems, waves=1):
    x, y, c = _position()
    me, sibling = (x, y, c), (x, y, 1 - c)
    x_chip, y_chip, d_chip = (1 - x, y), (x, 1 - y), (1 - x, 1 - y)
    south = c == 0
    relay_from = (jnp.where(south, x, 1 - x), jnp.where(south, 1 - y, y), c)
    relay_to = (jnp.where(south, 1 - x, x), jnp.where(south, y, 1 - y), c)
    n = len(srcs)

    def cut(a, w):
        rows = srcs[a].shape[0]
        step = (rows // waves) // 16 * 16
        if waves == 1 or step == 0:
            return pl.ds(0, rows) if w == 0 else None
        return pl.ds(w * step, step if w < waves - 1 else rows - w * step)

    def slab(a, block, w):
        px, py, pc = block
        return dsts[a].at[4 * px + 2 * py + pc, cut(a, w)]

    def copy(a, k, w, block, to, own=False):
        sem = (7 * a + k) * waves + w
        return pltpu.make_async_remote_copy(
            src_ref=srcs[a].at[cut(a, w)] if own else slab(a, block, w), dst_ref=slab(a, block, w),
            send_sem=send_sems.at[sem], recv_sem=recv_sems.at[sem], device_id=to, device_id_type=MESH)

    pieces = [(a, w) for w in range(waves) for a in range(n) if cut(a, w) is not None]
    mine = [pltpu.make_async_copy(srcs[a], dsts[a].at[4 * x + 2 * y + c], local_sems.at[a]) for a in range(n)]
    for cp in mine:
        cp.start()
    sent = []
    for a, w in pieces:
        sent += [copy(a, 0, w, me, sibling, own=True), copy(a, 1, w, me, (*x_chip, c), own=True),
                 copy(a, 2, w, me, (*y_chip, c), own=True)]
    for cp in sent:
        cp.start()
    for a, w in pieces:
        copy(a, 1, w, (*x_chip, c), me).wait_recv()
        copy(a, 2, w, (*y_chip, c), me).wait_recv()
        later = [copy(a, 3, w, relay_from, relay_to), copy(a, 4, w, (*x_chip, c), sibling),
                 copy(a, 5, w, (*y_chip, c), sibling)]
        for cp in later:
            cp.start()
        sent += later
    for a, w in pieces:
        copy(a, 3, w, (*d_chip, c), me).wait_recv()
        cp = copy(a, 6, w, (*d_chip, c), sibling)
        cp.start()
        sent.append(cp)
    for a, w in pieces:
        copy(a, 0, w, sibling, me).wait_recv()
        for k, chip in ((4, x_chip), (5, y_chip), (6, d_chip)):
            copy(a, k, w, (*chip, 1 - c), me).wait_recv()
    for cp in sent:
        cp.wait_send()
    for cp in mine:
        cp.wait()


def _all_gather_hbm(shards, *, name, waves=1):
    n = len(shards)

    def body(*refs):
        srcs, dsts = refs[:n], refs[n:2 * n]
        send_sems, recv_sems, local_sems = refs[2 * n:]
        _gather_copies_relayed(srcs, dsts, send_sems, recv_sems, local_sems, waves=waves)

    return pl.pallas_call(
        body, name=name,
        in_specs=[ANY] * n, out_specs=[ANY] * n,
        out_shape=[jax.ShapeDtypeStruct((N_DEV, *s.shape), s.dtype) for s in shards],
        scratch_shapes=_comm_sems(n, waves),
    )(*shards)


FLIPS = [(fx, fy, fc) for fx in (0, 1) for fy in (0, 1) for fc in (0, 1)][1:]


def _scatter_copies(srcs, dsts, send_sems, recv_sems, local_sems, waves=1):
    n = len(srcs)
    x, y, c = _position()
    me = 4 * x + 2 * y + c
    mine = [pltpu.make_async_copy(srcs[a].at[me], dsts[a].at[me], local_sems.at[a]) for a in range(n)]
    for cp in mine:
        cp.start()
    peers = []
    for fx, fy, fc in FLIPS:
        tx = 1 - x if fx else x
        ty = 1 - y if fy else y
        tc = 1 - c if fc else c
        peers.append(((tx, ty, tc), 4 * tx + 2 * ty + tc))
    copies = []
    for w in range(waves):
        wave = []
        for k, (peer_id, peer) in enumerate(peers):
            for a in range(n):
                rows = srcs[a].shape[1]
                step = rows if waves == 1 else (rows // waves) // 16 * 16
                r0 = w * step
                cut = pl.ds(r0, step if w < waves - 1 else rows - r0)
                sem = (7 * a + k) * waves + w
                cp = pltpu.make_async_remote_copy(
                    src_ref=srcs[a].at[peer, cut], dst_ref=dsts[a].at[me, cut],
                    send_sem=send_sems.at[sem], recv_sem=recv_sems.at[sem],
                    device_id=peer_id, device_id_type=MESH)
                cp.start()
                wave.append(cp)
        for cp in wave:
            cp.wait_send()
        copies += wave
    for cp in copies:
        cp.wait_recv()
    for cp in mine:
        cp.wait()


def _pair_copies(srcs, dsts, send_sems, recv_sems, local_sems):
    x, y, c = _position()
    copies = [pltpu.make_async_remote_copy(
        src_ref=srcs[a], dst_ref=dsts[a], send_sem=send_sems.at[a], recv_sem=recv_sems.at[a],
        device_id=(x, y, 1 - c), device_id_type=MESH) for a in range(len(srcs))]
    for cp in copies:
        cp.start()
    for cp in copies:
        cp.wait()


def _chip_scatter_copies(srcs, dsts, send_sems, recv_sems, local_sems, waves=1):
    n = len(srcs)
    x, y, c = _position()
    here = 2 * x + y
    mine = [pltpu.make_async_copy(srcs[a].at[here], dsts[a].at[here], local_sems.at[a]) for a in range(n)]
    for cp in mine:
        cp.start()
    peers = []
    for fx, fy in ((1, 0), (0, 1), (1, 1)):
        tx = 1 - x if fx else x
        ty = 1 - y if fy else y
        peers.append(((tx, ty, c), 2 * tx + ty))
    copies = []
    for w in range(waves):
        wave = []
        for k, (peer_id, peer) in enumerate(peers):
            for a in range(n):
                rows = srcs[a].shape[1]
                step = rows if waves == 1 else (rows // waves) // 16 * 16
                r0 = w * step
                cut = pl.ds(r0, step if w < waves - 1 else rows - r0)
                sem = (3 * a + k) * waves + w
                cp = pltpu.make_async_remote_copy(
                    src_ref=srcs[a].at[peer, cut], dst_ref=dsts[a].at[here, cut],
                    send_sem=send_sems.at[sem], recv_sem=recv_sems.at[sem],
                    device_id=peer_id, device_id_type=MESH)
                cp.start()
                wave.append(cp)
        for cp in wave:
            cp.wait_send()
        copies += wave
    for cp in copies:
        cp.wait_recv()
    for cp in mine:
        cp.wait()


def _comm_sems(n, waves=1):
    return [pltpu.SemaphoreType.DMA((7 * n * waves,)), pltpu.SemaphoreType.DMA((7 * n * waves,)),
            pltpu.SemaphoreType.DMA((n,))]


def _handshake(peers):
    barrier = pltpu.get_barrier_semaphore()
    for peer in peers:
        pl.semaphore_signal(barrier, inc=1, device_id=peer, device_id_type=MESH)
    pl.semaphore_wait(barrier, len(peers))


def _sequencer_call(arrays, out_types, copies_fn, peers_fn, *, name, collective_id, waves=1):
    n = len(arrays)
    srcs = [jax.new_ref(a, memory_space=pltpu.MemorySpace.HBM) for a in arrays]
    dsts = [jax.empty_ref(t, memory_space=pltpu.MemorySpace.HBM) for t in out_types]
    extra = {} if waves == 1 else {"waves": waves}

    @pl.kernel(mesh=plsc.ScalarSubcoreMesh(axis_name="sequencer", num_cores=1), name=name,
               scratch_types=_comm_sems(n, waves), compiler_params=pltpu.CompilerParams(collective_id=collective_id))
    def launch(send_sems, recv_sems, local_sems):
        _handshake(peers_fn())
        copies_fn(srcs, dsts, send_sems, recv_sems, local_sems, **extra)

    launch()
    return [d[...] for d in dsts]


def _all_other_devices():
    x, y, c = _position()
    return [(1 - x if fx else x, 1 - y if fy else y, 1 - c if fc else c) for fx, fy, fc in FLIPS]


def _gather_relay_peers():
    x, y, c = _position()
    return [(x, y, 1 - c), (1 - x, y, c), (x, 1 - y, c)]


def _sibling():
    x, y, c = _position()
    return [(x, y, 1 - c)]


def _same_core_of_other_chips():
    x, y, c = _position()
    return [(1 - x, y, c), (x, 1 - y, c), (1 - x, 1 - y, c)]


def _pair_exchange_async(parts, *, name, collective_id):
    return _sequencer_call(parts, [jax.ShapeDtypeStruct(p.shape, p.dtype) for p in parts],
                           _pair_copies, _sibling, name=name, collective_id=collective_id)


def _chip_scatter_async(parts, *, name, collective_id, waves=1):
    return _sequencer_call(parts, [jax.ShapeDtypeStruct(p.shape, p.dtype) for p in parts],
                           _chip_scatter_copies, _same_core_of_other_chips, name=name, collective_id=collective_id,
                           waves=waves)


def _pair_sum_call(mine, theirs, *, name, tw=D_MODEL):
    n, r, w = mine.shape

    def body(a_ref, b_ref, o_ref):
        o_ref[...] = (a_ref[...].astype(F32) + b_ref[...].astype(F32)).astype(o_ref.dtype)

    spec = pl.BlockSpec((None, r, tw), lambda i, j: (i, 0, j))
    return pl.pallas_call(
        body, name=name, grid=(n, w // tw), in_specs=[spec, spec], out_specs=spec,
        out_shape=pltpu.HBM(mine.shape, mine.dtype),
        compiler_params=_cp(("parallel", "parallel")),
    )(*_hbm(mine, theirs))


def _scatter_blocks_async(parts, *, name, collective_id, waves=1):
    return _sequencer_call(parts, [jax.ShapeDtypeStruct(p.shape, p.dtype) for p in parts],
                           _scatter_copies, _all_other_devices, name=name, collective_id=collective_id, waves=waves)


def _all_gather_async(shards, *, name, collective_id):
    return _sequencer_call(shards, [jax.ShapeDtypeStruct((N_DEV, *s.shape), s.dtype) for s in shards],
                           _gather_copies_relayed, _gather_relay_peers, name=name, collective_id=collective_id)


def _adamw_math(w, g, m, v):
    m = ADAM_B1 * m + (1.0 - ADAM_B1) * g
    v = ADAM_B2 * v + (1.0 - ADAM_B2) * (g * g)
    m_hat = m / (1.0 - ADAM_B1 ** ADAM_STEP)
    v_hat = v / (1.0 - ADAM_B2 ** ADAM_STEP)
    delta = -ADAM_LR * (m_hat / (jnp.sqrt(v_hat) + ADAM_EPS) + ADAM_WD * w)
    return delta, m, v


def _adamw_reduce_call(recv, w, m, v, *, name, T=128):
    R, W = w.shape
    n_parts = recv.shape[0]
    if R % T == 0:
        tr, tw = T, W
    elif (R // 2) % 16 == 0:
        tr, tw = R // 2, W
    else:
        tr, tw = R, 2 * LANES

    def body(p_ref, w_ref, m_ref, v_ref, g_out, d_out, m_out, v_out):
        g = p_ref[0].astype(F32)
        for d in range(1, n_parts):
            g = g + p_ref[d].astype(F32)
        g_out[...] = g
        d_out[...], m_out[...], v_out[...] = _adamw_math(w_ref[...], g, m_ref[...], v_ref[...])

    row = pl.BlockSpec((tr, tw), lambda i, j: (i, j))
    out = pltpu.HBM((R, W), F32)
    return pl.pallas_call(
        body, name=name, grid=(R // tr, W // tw),
        in_specs=[pl.BlockSpec((n_parts, tr, tw), lambda i, j: (0, i, j)), row, row, row],
        out_specs=[row] * 4, out_shape=[out] * 4,
        compiler_params=_cp(("parallel", "parallel"), VMEM_BIG),
    )(*_hbm(recv, w, m, v))


SMALL_EARLY = (("b_gate", 8), ("w_spatial", 512), ("b_spatial", 8), ("norm_post_mix", 8), ("norm_pre_ffn", 8),
               ("norm_post_ffn", 8), ("w_gate_up", 128), ("gla_norm", 64), ("sgu_ln_g", 64), ("sgu_ln_b", 64))
SMALL_EARLY_AT = {}
for _name, _rows in SMALL_EARLY:
    SMALL_EARLY_AT[_name] = sum(r for _, r in SMALL_EARLY[:len(SMALL_EARLY_AT)])
SMALL_EARLY_ROWS = sum(r for _, r in SMALL_EARLY)
SMALL_LATE_ROWS = 16


def _small_early_rows(grads):
    def rows(a, n_rows):
        a = a.reshape(-1, LANES)
        return jnp.pad(a, ((0, n_rows - a.shape[0]), (0, 0)))

    def device_major(a, n_rows):
        r, c = a.shape[0], a.shape[1] // N_DEV
        a = a.reshape(r, N_DEV, c).transpose(1, 0, 2)
        a = jnp.pad(a, ((0, 0), (0, n_rows // N_DEV - r), (0, LANES - c)))
        return a.reshape(n_rows, LANES)

    pieces = []
    for name, n_rows in SMALL_EARLY:
        g = grads[name]
        if name in SMALL_SHARDED:
            pieces.append(device_major(g.reshape(g.shape[0], -1) if g.ndim == 2 else g.reshape(g.shape[0], g.shape[-1]), n_rows))
        else:
            pieces.append(rows(g, n_rows))
    return jnp.concatenate(pieces, axis=0)


def _small_update_call(early, dg_pre_mix, loss_part, w, m, v, *, name):
    names = list(SMALL)
    n_p = len(names)
    E = SMALL_EARLY_ROWS

    def reduce_body(early_ref, dg1_ref, loss_ref, tot, got_early, got_late, late, send_sems, recv_sems, local_sems):
        for k in range(D_MODEL // LANES):
            late[k:k + 1, :] = dg1_ref[:, k * LANES:(k + 1) * LANES]
        late[8:16, :] = jnp.broadcast_to(loss_ref[...], (8, LANES))
        _gather_copies([early_ref, late], [got_early, got_late], send_sems, recv_sems, local_sems)
        acc, acc_late = got_early[0], got_late[0]
        for d in range(1, N_DEV):
            acc, acc_late = acc + got_early[d], acc_late + got_late[d]
        tot[0:E, :] = acc
        tot[E:E + SMALL_LATE_ROWS, :] = acc_late

    total = pl.pallas_call(
        reduce_body, name=name + "_reduce",
        in_specs=[VMEM_SPEC] * 3, out_specs=VMEM_SPEC,
        out_shape=jax.ShapeDtypeStruct((E + SMALL_LATE_ROWS, LANES), F32),
        scratch_shapes=[pltpu.VMEM((N_DEV, E, LANES), F32), pltpu.VMEM((N_DEV, SMALL_LATE_ROWS, LANES), F32),
                        pltpu.VMEM((SMALL_LATE_ROWS, LANES), F32),
                        pltpu.SemaphoreType.DMA((14,)), pltpu.SemaphoreType.DMA((14,)), pltpu.SemaphoreType.DMA((2,))],
    )(early, dg_pre_mix, loss_part)

    def body(tot, *rest):
        w_refs, m_refs, v_refs = [dict(zip(names, rest[i * n_p:(i + 1) * n_p])) for i in range(3)]
        outs = rest[3 * n_p:]
        loss_out = outs[0]
        g_out, d_out, m_out, v_out = [dict(zip(names, outs[1 + i * n_p:1 + (i + 1) * n_p])) for i in range(4)]
        loss_out[...] = tot[E + 8:E + 9, :]

        x, y, c = _position()
        me = 4 * x + 2 * y + c

        def update(name, g, ix):
            g_out[name][ix] = g
            d_out[name][ix], m_out[name][ix], v_out[name][ix] = _adamw_math(
                w_refs[name][ix], g, m_refs[name][ix], v_refs[name][ix])

        for name in names:
            shape = w[name].shape
            if name in SMALL_SHARDED:
                per_dev = dict(SMALL_EARLY)[name] // N_DEV
                at = pl.multiple_of(SMALL_EARLY_AT[name] + me * per_dev, 8)
                g = tot[pl.ds(at, per_dev), :]
                update(name, g[:shape[1], :shape[2]], (0,))
            elif name == "w_spatial":
                for grp in range(GROUPS):
                    at = SMALL_EARLY_AT[name] + grp * SBLOCK
                    update(name, tot[at:at + SBLOCK, :], (0, grp))
            elif name == "b_spatial":
                at = SMALL_EARLY_AT[name]
                update(name, tot[at:at + GROUPS, :], (0,))
            else:
                at = E if name == "norm_pre_mix" else SMALL_EARLY_AT[name]
                for k in range(shape[1] // LANES):
                    update(name, tot[at + k:at + k + 1, :], (slice(None), pl.ds(k * LANES, LANES)))

    state = [s[n] for s in (w, m, v) for n in names]
    out_shapes = [jax.ShapeDtypeStruct((1, LANES), F32)] + [jax.ShapeDtypeStruct(w[n].shape, F32) for n in names] * 4
    outs = pl.pallas_call(
        body, name=name + "_adamw",
        in_specs=[VMEM_SPEC] * (1 + len(state)), out_specs=[VMEM_SPEC] * len(out_shapes), out_shape=out_shapes,
    )(total, *state)
    per_name = {n: tuple(outs[1 + i * n_p + j] for i in range(4)) for j, n in enumerate(names)}
    return outs[0], per_name


def _tile_rows(n_elems):
    return -(-n_elems // (8 * LANES)) * 8


def _pack_rows(parts, rows):
    pieces = []
    for p in parts:
        q = p.reshape(-1, LANES)
        pieces.append(jnp.pad(q, ((0, _tile_rows(p.size) - q.shape[0]), (0, 0))))
    buf = jnp.concatenate(pieces, axis=0)
    return jnp.pad(buf, ((0, rows - buf.shape[0]), (0, 0)))


def _in_w_blocks_call(a, live, finished, *, name, tk=TOKEN_TILE):
    S = a.shape[0]
    tk = min(tk, S)
    steps = S // tk
    n_live = len(live)
    n_chips = N_DEV // 2
    out_shape = (n_chips, IN_BLK, D_MODEL)

    def body(a_ref, *rest):
        live_refs, done_refs = rest[:n_live], rest[n_live:n_live + len(finished)]
        keep_ref, send_ref = rest[n_live + len(finished):n_live + len(finished) + 2]
        accs = rest[n_live + len(finished) + 2:]
        k = pl.program_id(0)

        @pl.when(k == 0)
        def _():
            for acc in accs:
                acc[...] = jnp.zeros_like(acc)

        av = a_ref[...]
        for src, acc in zip(live_refs, accs):
            for m0 in range(0, src.shape[1], 1024):
                m1 = min(src.shape[1], m0 + 1024)
                acc[m0:m1, :] += _dot(src[:, m0:m1], av, TN)

        @pl.when(k == steps - 1)
        def _():
            groups = [(acc, cols) for acc, (_, cols) in zip(accs, live)]
            groups += [(ref, cols) for ref, (_, cols) in zip(done_refs, finished)]
            core = lax.axis_index("c")

            def cut(d, out_ref):
                lo, hi = IN_BLK * d, IN_BLK * (d + 1)
                for ref, (c0, c1) in groups:
                    s0, e0 = max(lo, c0), min(hi, c1)
                    if s0 < e0:
                        out_ref[d // 2, s0 - lo:e0 - lo, :] = ref[s0 - c0:e0 - c0, :].astype(BF16)

            for d in range(N_DEV):
                pl.when(core == d % 2)(lambda d=d: cut(d, keep_ref))
                pl.when(core != d % 2)(lambda d=d: cut(d, send_ref))

    acc_shapes = [(arr.shape[1], D_MODEL) for arr, _ in live]
    tiles = [((tk, D_MODEL), BF16)] + [((tk, arr.shape[1]), BF16) for arr, _ in live]
    resident = ([(arr.shape, arr.dtype) for arr, _ in finished] + [(out_shape, BF16)] * 2
                + [(s, F32) for s in acc_shapes])
    return pl.pallas_call(
        body, name=name, grid=(steps,),
        in_specs=[pl.BlockSpec((tk, D_MODEL), lambda k: (k, 0))]
        + [pl.BlockSpec((tk, arr.shape[1]), lambda k: (k, 0)) for arr, _ in live]
        + [_res(arr.shape) for arr, _ in finished],
        out_specs=[_acc(out_shape)] * 2,
        out_shape=[pltpu.HBM(out_shape, BF16)] * 2,
        scratch_shapes=[pltpu.VMEM(s, F32) for s in acc_shapes],
        compiler_params=_cp(("arbitrary",), _vmem_limit(tiles, resident)),
    )(*_hbm(a, *[arr for arr, _ in live], *[arr for arr, _ in finished]))


def _local_step(x, target, W, scatter, finish, pair, chip_scatter):
    g1, g2, g3, g4 = [W[n].reshape(1, D_MODEL) for n in ("norm_pre_mix", "norm_post_mix", "norm_pre_ffn", "norm_post_ffn")]
    wfi, wfo = W["w_ffn_in"], W["w_ffn_out"].reshape(4, FF_BLK, D_MODEL)
    wg = jnp.pad(W["w_gate_up"], ((0, LANES - RANK), (0, 0))).astype(BF16)
    bgate = W["b_gate"].reshape(1, QK)
    gnorm = W["gla_norm"].reshape(HEADS, 1, DV)
    ln_g = W["sgu_ln_g"].reshape(GROUPS, 1, DG)
    ln_b = W["sgu_ln_b"].reshape(GROUPS, 1, DG)
    w_sp = W["w_spatial"]
    b_sp = W["b_spatial"].reshape(GROUPS, SBLOCK, 1)

    a, pa, alow, ps, pg, w_in_t = _in_proj_call(x, g1, W["w_in_blocks"], name="in_proj")
    y_gla, states = _gla_fwd_call(pa, alow, wg, bgate, gnorm, name="gla_fwd")
    y_sgu = _sgu_fwd_call(ps, ln_g, ln_b, w_sp, b_sp, name="sgu_fwd")
    t1, t2, merged, mix, x1, h = _mixer_tail_call(y_gla, y_sgu, pg, x, W["w_branch_gla"], W["w_branch_sgu"],
                                                  W["w_out"], g2, g3, name="mixer_tail")
    gu, z = _ffn_in_call(h, wfi, name="ffn_in")
    loss, dx2, dy, dg4 = _ffn_out_loss_call(z, wfo, x1, target, g4, name="ffn_out_loss")

    grads = {"norm_post_ffn": dg4}
    done = {}
    dgu = _ffn_out_bwd_call(dy, wfo, gu, name="ffn_out_bwd")
    dw_ffn_out = _weight_grads_call([z, dy], [(0, 1)], name="d_ffn_out_w")[0].reshape(N_DEV, D_FF // N_DEV, D_MODEL)
    dw_ffn_in = _ffn_in_grad_call(h, dgu, name="d_ffn_in_w")
    dgu, dw_ffn_out, dw_ffn_in = lax.optimization_barrier((dgu, dw_ffn_out, dw_ffn_in))
    ffn_received = scatter(("w_ffn_out", "w_ffn_in"), [dw_ffn_out, dw_ffn_in])
    dx1, dmix, grads["norm_pre_ffn"], grads["norm_post_mix"] = _ffn_in_bwd_call(dgu, wfi, dx2, x1, mix, g3, g2, name="ffn_in_bwd")
    dt1, dt2, dpg, dy_gla, dy_sgu = _mixer_bwd_call(dmix, t1, t2, pg, W["w_out"], W["w_branch_gla"], W["w_branch_sgu"],
                                                    name="mixer_bwd")
    rows = D_MODEL // N_DEV
    mixer_grads = _weight_grads_call([merged, dmix, y_gla, dt1, y_sgu, dt2], [(0, 1), (2, 3), (4, 5)], name="d_mixer_w")
    dy_gla, dy_sgu, mixer_grads = lax.optimization_barrier((dy_gla, dy_sgu, mixer_grads))
    mixer_received = scatter(("w_out", "w_branch_gla", "w_branch_sgu"),
                             [g.reshape(N_DEV, rows, D_MODEL) for g in mixer_grads])
    dps, dlg, dlb, dwsp, dbsp = _sgu_bwd_call(ps, ln_g, ln_b, w_sp, b_sp, dy_sgu, name="sgu_bwd")
    dw_s, dw_g = _weight_grads_call([a, dps, dpg], [(1, 0), (2, 0)], name="d_in_w_sgu_gates")
    dy_gla, dw_s, dw_g = lax.optimization_barrier((dy_gla, dw_s, dw_g))
    dpa, dlogit, dgn, dbg = _gla_bwd_call(pa, alow, wg, bgate, gnorm, states, dy_gla, name="gla_bwd")
    ffn_received, mixer_received, dpa, dlogit = lax.optimization_barrier((ffn_received, mixer_received, dpa, dlogit))
    dlow = _mm(dlogit, wg, "nt", BF16, name="d_gate_up_x")
    keep, send = _in_w_blocks_call(a, [(dpa, A_COLS), (dlow, LOW_COLS)], [(dw_s, S_COLS), (dw_g, G_COLS)],
                                   name="d_in_w_blocks")
    ffn_received, mixer_received, send = lax.optimization_barrier((ffn_received, mixer_received, send))
    from_sibling = pair(send)
    done.update({**finish(ffn_received), **finish(mixer_received)})
    dwg = _mm(alow, dlogit, "tn", F32, name="d_gate_up_w")
    done, dwg, keep = lax.optimization_barrier((done, dwg, keep))
    chip_sum = _pair_sum_call(keep, from_sibling, name="pair_sum_w_in")
    chip_sum, dpa = lax.optimization_barrier((chip_sum, dpa))
    in_received = chip_scatter(chip_sum)
    grad_x, grads["norm_pre_mix"] = _in_proj_bwd_call(dpa, dlow, dps, dpg, w_in_t, x, dx1, g1, name="in_proj_bwd")

    grads["w_gate_up"] = dwg[:RANK]
    grads["b_gate"] = dbg
    grads["gla_norm"] = dgn
    grads["sgu_ln_g"] = dlg
    grads["sgu_ln_b"] = dlb
    grads["w_spatial"] = dwsp
    grads["b_spatial"] = dbsp
    done.update(finish({"w_in": in_received}))
    return loss, grad_x, grads, done


WEIGHTS = ("norm_pre_mix", "w_in", "w_gate_up", "b_gate", "gla_norm", "sgu_ln_g", "sgu_ln_b", "w_spatial",
           "b_spatial", "w_branch_gla", "w_branch_sgu", "w_out", "norm_post_mix", "norm_pre_ffn", "w_ffn_in",
           "w_ffn_out", "norm_post_ffn")
BIG = ("w_in", "w_branch_gla", "w_branch_sgu", "w_out", "w_ffn_in", "w_ffn_out")
MIXER = ("w_branch_gla", "w_branch_sgu", "w_out")
FFN = ("w_ffn_in", "w_ffn_out")
COLUMN_SHARDED = ("w_in", "w_ffn_in")
LATE_SCATTER_WAVES = 4
GATHER_WAVES = 2
SMALL = tuple(n for n in WEIGHTS if n not in BIG)
SMALL_SHARDED = ("w_gate_up", "gla_norm", "sgu_ln_g", "sgu_ln_b")
SMALL_GATHER_ROWS = 32


def kernel(x, norm_pre_mix, w_in, w_gate_up, b_gate, gla_norm, sgu_ln_g, sgu_ln_b, w_spatial, b_spatial, w_branch_gla, w_branch_sgu, w_out, norm_post_mix, norm_pre_ffn, w_ffn_in, w_ffn_out, norm_post_ffn, loss_target, m_norm_pre_mix, m_w_in, m_w_gate_up, m_b_gate, m_gla_norm, m_sgu_ln_g, m_sgu_ln_b, m_w_spatial, m_b_spatial, m_w_branch_gla, m_w_branch_sgu, m_w_out, m_norm_post_mix, m_norm_pre_ffn, m_w_ffn_in, m_w_ffn_out, m_norm_post_ffn, v_norm_pre_mix, v_w_in, v_w_gate_up, v_b_gate, v_gla_norm, v_sgu_ln_g, v_sgu_ln_b, v_w_spatial, v_b_spatial, v_w_branch_gla, v_w_branch_sgu, v_w_out, v_norm_post_mix, v_norm_pre_ffn, v_w_ffn_in, v_w_ffn_out, v_norm_post_ffn):
    given = dict(locals())
    def local(a, n):
        return a[0].T if n in COLUMN_SHARDED else a[0]

    w = {n: local(given[n], n) for n in WEIGHTS}
    m = {n: local(given["m_" + n], n) for n in WEIGHTS}
    v = {n: local(given["v_" + n], n) for n in WEIGHTS}
    xs, target = x[0], loss_target[0]

    small_shard = _pack_rows([w[n] for n in SMALL_SHARDED], SMALL_GATHER_ROWS)
    first = _all_gather_hbm([w["w_in"].astype(BF16), small_shard], name="gather_w_in", waves=GATHER_WAVES)
    rest, first = lax.optimization_barrier(([w[n].astype(BF16) for n in MIXER + FFN], first))
    rest_blocks = _all_gather_async(rest, name="gather_rest", collective_id=1)
    W = {n: w[n] for n in SMALL if n not in SMALL_SHARDED}
    blocks = dict(zip(MIXER + FFN, rest_blocks))
    for n in ("w_branch_gla", "w_branch_sgu", "w_out", "w_ffn_out"):
        W[n] = blocks[n].reshape(-1, D_MODEL)
    W["w_ffn_in"] = blocks["w_ffn_in"]
    W["w_in_blocks"] = first[0]
    small_blocks = first[1]
    off = 0
    for n in SMALL_SHARDED:
        r, c = w[n].shape
        blk = small_blocks[:, off:off + r * c // LANES].reshape(N_DEV, r, c)
        W[n] = blk.transpose(1, 0, 2).reshape(r, N_DEV * c)
        off += _tile_rows(r * c)

    scatter_ids = iter((3, 4))

    def scatter(names, parts):
        got = _scatter_blocks_async(parts, name="scatter_" + "_".join(names), collective_id=next(scatter_ids))
        return dict(zip(names, got))

    def finish(received):
        return {n: _adamw_reduce_call(r, w[n], m[n], v[n], name="adamw_" + n) for n, r in received.items()}

    def pair(part):
        return _pair_exchange_async([part], name="pair_w_in", collective_id=5)[0]

    def chip_scatter(part):
        return _chip_scatter_async([part], name="scatter_w_in", collective_id=6, waves=LATE_SCATTER_WAVES)[0]

    loss_part, grad_x, grads, big_done = _local_step(xs, target, W, scatter, finish, pair, chip_scatter)

    loss_row, small_done = _small_update_call(
        _small_early_rows(grads), grads["norm_pre_mix"], loss_part,
        *[{n: given[prefix + n] for n in SMALL} for prefix in ("", "m_", "v_")], name="small_update")

    def pick(i):
        return [small_done[n][i] if n in SMALL else (big_done[n][i].T if n in COLUMN_SHARDED else big_done[n][i])[None]
                for n in WEIGHTS]

    return (loss_row[0, 0], grad_x[None], *pick(0), *pick(1), *pick(2), *pick(3))
```

```python
import jax
import jax.numpy as jnp
from jax import lax
from jax.experimental import pallas as pl
from jax.experimental.pallas import tpu as pltpu
from jax.experimental.pallas import tpu_sc as plsc

F32 = jnp.float32
BF16 = jnp.bfloat16

D_MODEL = 1024
N_DEV = 8
CHUNK = 64
HEADS = 4
DK = 128
DV = 256
QK = HEADS * DK
GV = HEADS * DV
RANK = 16
GROUPS = 4
SBLOCK = 128
DG = 256
D_FF = 2816
FF_BLK = 704
EPS = 1e-6
Q_SCALE = DK ** -0.5
LANES = 128
VMEM_BIG = 48 * 1024 * 1024

D_IN = 7184
IN_BLK = 898
A_COLS = (0, 3072)
LOW_COLS = (3072, 3088)
S_COLS = (3088, 5136)
G_COLS = (5136, 7184)

ADAM_LR = 0.001
ADAM_B1 = 0.9
ADAM_B2 = 0.999
ADAM_EPS = 1e-08
ADAM_WD = 0.01
ADAM_STEP = 10

MESH = pl.DeviceIdType.MESH
ANY = pl.BlockSpec(memory_space=pl.ANY)
VMEM_SPEC = pl.BlockSpec(memory_space=pltpu.VMEM)


def _cp(sem=None, vmem=None):
    return pltpu.CompilerParams(dimension_semantics=sem, vmem_limit_bytes=vmem)


def _hbm(*arrays):
    return [pltpu.with_memory_space_constraint(a, pltpu.HBM) for a in arrays]


def _sig(x):
    return 0.5 * jnp.tanh(0.5 * x) + 0.5


GELU_C = 0.7978845608028654
GELU_A = 0.044715


def _gelu(x):
    t = jnp.tanh((GELU_C * x) * (1.0 + GELU_A * (x * x)))
    return (0.5 * x) * (1.0 + t)


def _gelu_and_grad(x):
    x2 = x * x
    t = jnp.tanh((GELU_C * x) * (1.0 + GELU_A * x2))
    one_t = 1.0 + t
    hx = 0.5 * x
    grad = 0.5 * one_t + (hx * (1.0 - t * t)) * (GELU_C + (3.0 * GELU_A * GELU_C) * x2)
    return hx * one_t, grad


def _logsig(x):
    return jnp.minimum(x, 0.0) - jnp.log1p(jnp.exp(-jnp.abs(x)))


def _dot(a, b, dims):
    return lax.dot_general(a, b, (dims, ((), ())), preferred_element_type=F32)


NN = ((1,), (0,))
NT = ((1,), (1,))
TN = ((0,), (0,))


def _exact_mask_dot(mask_bf16, x):
    hi = x.astype(BF16)
    r1 = x - hi.astype(F32)
    mid = r1.astype(BF16)
    lo = (r1 - mid.astype(F32)).astype(BF16)
    return _dot(mask_bf16, hi, NN) + _dot(mask_bf16, mid, NN) + _dot(mask_bf16, lo, NN)


def _pick_tile(dim, target):
    if dim <= target:
        return dim
    best = None
    for t in range(LANES, int(1.4 * target) + 1, LANES):
        if dim % t == 0:
            best = t
    assert best is not None, (dim, target)
    return best


def _mm_call(a, b, *, name, grid, a_spec, b_spec, o_spec, out_shape, dims, acc_shape):
    nk = grid[2]

    def body(a_ref, b_ref, o_ref, *acc):
        part = _dot(a_ref[...], b_ref[...], dims)
        if nk == 1:
            o_ref[...] = part.astype(o_ref.dtype)
        else:
            acc_ref = acc[0]
            k = pl.program_id(2)

            @pl.when(k == 0)
            def _():
                acc_ref[...] = part

            @pl.when(k > 0)
            def _():
                acc_ref[...] += part

            @pl.when(k == nk - 1)
            def _():
                o_ref[...] = acc_ref[...].astype(o_ref.dtype)

    return pl.pallas_call(
        body, name=name, grid=grid, in_specs=[a_spec, b_spec], out_specs=o_spec, out_shape=out_shape,
        scratch_shapes=[] if nk == 1 else [pltpu.VMEM(acc_shape, F32)],
        compiler_params=_cp(("parallel", "parallel", "arbitrary"), VMEM_BIG),
    )(a, b)


def _mm(a, b, mode, out_dtype, *, name, tm=512, tn=1024, tk=1024):
    if mode == "nn":
        (M, K), (_, N) = a.shape, b.shape
    elif mode == "nt":
        (M, K), (N, _) = a.shape, b.shape
    else:
        (K, M), (_, N) = a.shape, b.shape
    tm, tn, tk = _pick_tile(M, tm), _pick_tile(N, tn), _pick_tile(K, tk)
    if mode == "nn":
        a_spec = pl.BlockSpec((tm, tk), lambda i, j, k: (i, k))
        b_spec = pl.BlockSpec((tk, tn), lambda i, j, k: (k, j))
        dims = NN
    elif mode == "nt":
        a_spec = pl.BlockSpec((tm, tk), lambda i, j, k: (i, k))
        b_spec = pl.BlockSpec((tn, tk), lambda i, j, k: (j, k))
        dims = NT
    else:
        a_spec = pl.BlockSpec((tk, tm), lambda i, j, k: (k, i))
        b_spec = pl.BlockSpec((tk, tn), lambda i, j, k: (k, j))
        dims = TN
    return _mm_call(a, b, name=name, grid=(M // tm, N // tn, K // tk), a_spec=a_spec, b_spec=b_spec,
                    o_spec=pl.BlockSpec((tm, tn), lambda i, j, k: (i, j)),
                    out_shape=jax.ShapeDtypeStruct((M, N), out_dtype), dims=dims, acc_shape=(tm, tn))


ROW_TILE = 512
SUB_ROWS = 256


def _sub_tiles(T):
    return [pl.ds(r0, min(SUB_ROWS, T)) for r0 in range(0, T, SUB_ROWS)]


def _rt(T, W, c=0):
    return pl.BlockSpec((T, W), lambda i: (i, c))


def _rt3(nb, T, W):
    return pl.BlockSpec((nb, T, W), lambda i: (0, i, 0))


def _res(shape):
    nd = len(shape)
    return pl.BlockSpec(tuple(shape), lambda i: (0,) * nd, pipeline_mode=pl.Buffered(1))


def _acc(shape):
    nd = len(shape)
    return pl.BlockSpec(tuple(shape), lambda i: (0,) * nd, pipeline_mode=pl.Buffered(1))


def _nbytes(shape, dtype):
    n = jnp.dtype(dtype).itemsize
    for s in shape:
        n *= s
    return n


def _vmem_limit(tiles, resident, temps=16 * 1024 * 1024):
    need = 2 * sum(_nbytes(s, d) for s, d in tiles) + sum(_nbytes(s, d) for s, d in resident) + temps
    return min(need, 60 * 1024 * 1024)


def _tok_call(body, *, name, S, T, ins, outs, semantics="parallel"):
    tiles = [(spec.block_shape, a.dtype) for a, spec, kind in ins + outs if kind == "tile"]
    resident = [(a.shape, a.dtype) for a, spec, kind in ins + outs if kind == "res"]
    return pl.pallas_call(
        body, name=name, grid=(S // T,),
        in_specs=[spec for _, spec, _ in ins], out_specs=[spec for _, spec, _ in outs],
        out_shape=[pltpu.HBM(a.shape, a.dtype) for a, _, _ in outs],
        compiler_params=_cp((semantics,), _vmem_limit(tiles, resident)),
    )(*_hbm(*[a for a, _, _ in ins]))


def _tile(a, spec):
    return (a, spec, "tile")


def _whole(a):
    return (a, _res(a.shape), "res")


def _out_tile(shape, dtype, spec):
    return (jax.ShapeDtypeStruct(shape, dtype), spec, "tile")


def _out_acc(shape, dtype=F32):
    return (jax.ShapeDtypeStruct(shape, dtype), _acc(shape), "res")


def _rms_stats(x):
    r = lax.rsqrt(jnp.mean(x * x, axis=-1, keepdims=True) + EPS)
    return r, x * r


def _rms_bwd(xh, r, g, dy):
    dxh = dy * g
    dx = r * (dxh - xh * jnp.mean(dxh * xh, axis=-1, keepdims=True))
    dg = jnp.sum(dy * xh, axis=0, keepdims=True)
    return dx, dg


def _accum(ref, val):
    @pl.when(pl.program_id(0) == 0)
    def _():
        ref[...] = val

    @pl.when(pl.program_id(0) > 0)
    def _():
        ref[...] += val


def _dot_rows_t(a, w_ref, row0, o_ref, chunk=1024):
    n = o_ref.shape[1]
    for n0 in range(0, n, chunk):
        n1 = min(n, n0 + chunk)
        o_ref[:, n0:n1] = _dot(a, w_ref[row0 + n0:row0 + n1, :], NT).astype(o_ref.dtype)


def _in_proj_call(x, g1, w_blocks, *, name, T=ROW_TILE):
    S = x.shape[0]

    def body(x_ref, g_ref, blk_ref, a_ref, pa_ref, al_ref, ps_ref, pg_ref, w_ref):
        @pl.when(pl.program_id(0) == 0)
        def _():
            for d in range(N_DEV):
                w_ref[d * IN_BLK:(d + 1) * IN_BLK, :] = blk_ref[d]

        _, xh = _rms_stats(x_ref[...])
        a = (xh * g_ref[...]).astype(BF16)
        a_ref[...] = a
        _dot_rows_t(a, w_ref, A_COLS[0], pa_ref)
        _dot_rows_t(a, w_ref, LOW_COLS[0], al_ref)
        _dot_rows_t(a, w_ref, S_COLS[0], ps_ref)
        _dot_rows_t(a, w_ref, G_COLS[0], pg_ref)

    widths = (D_MODEL, A_COLS[1] - A_COLS[0], LANES, S_COLS[1] - S_COLS[0], G_COLS[1] - G_COLS[0])
    return _tok_call(
        body, name=name, S=S, T=T, semantics="arbitrary",
        ins=[_tile(x, _rt(T, D_MODEL)), _whole(g1), _whole(w_blocks)],
        outs=[_out_tile((S, w), BF16, _rt(T, w)) for w in widths] + [_out_acc((D_IN, D_MODEL), BF16)])


def _mixer_tail_call(y_gla, y_sgu, pg, x, w_bg, w_bs, w_out, g2, g3, *, name, T=ROW_TILE):
    S = x.shape[0]

    def body(yg_ref, ys_ref, pg_ref, x_ref, wbg_ref, wbs_ref, wo_ref, g2_ref, g3_ref,
             t1_ref, t2_ref, mg_ref, mix_ref, x1_ref, h_ref):
        for rows in _sub_tiles(T):
            t1 = _dot(yg_ref[rows, :], wbg_ref[...], NN)
            t2 = _dot(ys_ref[rows, :], wbs_ref[...], NN)
            t1_ref[rows, :] = t1.astype(BF16)
            t2_ref[rows, :] = t2.astype(BF16)
            sg = _sig(pg_ref[rows, pl.ds(0, D_MODEL)].astype(F32))
            ss = _sig(pg_ref[rows, pl.ds(D_MODEL, D_MODEL)].astype(F32))
            merged = (sg * t1 + ss * t2).astype(BF16)
            mg_ref[rows, :] = merged
            mix = _dot(merged, wo_ref[...], NN)
            mix_ref[rows, :] = mix
            _, mh = _rms_stats(mix)
            x1 = x_ref[rows, :] + mh * g2_ref[...]
            x1_ref[rows, :] = x1
            _, xh = _rms_stats(x1)
            h_ref[rows, :] = (xh * g3_ref[...]).astype(BF16)

    row = _rt(T, D_MODEL)
    b16 = lambda: _out_tile((S, D_MODEL), BF16, row)
    f32 = lambda: _out_tile((S, D_MODEL), F32, row)
    return _tok_call(
        body, name=name, S=S, T=T,
        ins=[_tile(y_gla, row), _tile(y_sgu, row), _tile(pg, _rt(T, 2 * D_MODEL)), _tile(x, row),
             _whole(w_bg), _whole(w_bs), _whole(w_out), _whole(g2), _whole(g3)],
        outs=[b16(), b16(), b16(), f32(), f32(), b16()])


def _ffn_in_call(h, wfi, *, name, T=ROW_TILE):
    S = h.shape[0]

    def body(h_ref, w_ref, gu_ref, z_ref):
        hv = h_ref[...]
        for d in range(4):
            gate = _dot(hv, w_ref[d], NT)
            up = _dot(hv, w_ref[d + 4], NT)
            gu_ref[d] = gate.astype(BF16)
            gu_ref[d + 4] = up.astype(BF16)
            z_ref[d] = (gate * _sig(gate) * up).astype(BF16)

    return _tok_call(
        body, name=name, S=S, T=T,
        ins=[_tile(h, _rt(T, D_MODEL)), _whole(wfi)],
        outs=[_out_tile((N_DEV, S, FF_BLK), BF16, _rt3(N_DEV, T, FF_BLK)),
              _out_tile((4, S, FF_BLK), BF16, _rt3(4, T, FF_BLK))])


def _ffn_out_loss_call(z, wfo, x1, target, g4, *, name, T=ROW_TILE):
    S = x1.shape[0]

    def body(z_ref, w_ref, x1_ref, t_ref, g4_ref, loss_ref, dx2_ref, dy_ref, dg4_ref):
        loss = jnp.zeros((1, 1), F32)
        dg4 = jnp.zeros((1, D_MODEL), F32)
        for rows in _sub_tiles(T):
            y = _dot(z_ref[0, rows, :], w_ref[0], NN)
            for d in range(1, 4):
                y = y + _dot(z_ref[d, rows, :], w_ref[d], NN)
            r, yh = _rms_stats(y)
            diff = x1_ref[rows, :] + yh * g4_ref[...] - t_ref[rows, :]
            loss = loss + 0.5 * jnp.sum(jnp.mean(diff * diff, axis=-1, keepdims=True), axis=0, keepdims=True)
            dx2 = diff * (1.0 / D_MODEL)
            dx2_ref[rows, :] = dx2
            dy, dg = _rms_bwd(yh, r, g4_ref[...], dx2)
            dy_ref[rows, :] = dy.astype(BF16)
            dg4 = dg4 + dg
        _accum(loss_ref, jnp.broadcast_to(loss, (1, LANES)))
        _accum(dg4_ref, dg4)

    row = _rt(T, D_MODEL)
    return _tok_call(
        body, name=name, S=S, T=T, semantics="arbitrary",
        ins=[_tile(z, _rt3(4, T, FF_BLK)), _whole(wfo), _tile(x1, row), _tile(target, row), _whole(g4)],
        outs=[_out_acc((1, LANES)), _out_tile((S, D_MODEL), F32, row), _out_tile((S, D_MODEL), BF16, row),
              _out_acc((1, D_MODEL))])


def _ffn_out_bwd_call(dy, wfo, gu, *, name, T=ROW_TILE):
    S = dy.shape[0]

    def body(dy_ref, w_ref, gu_ref, dgu_ref):
        dyv = dy_ref[...]
        for d in range(4):
            dz = _dot(dyv, w_ref[d], NT).astype(BF16)
            gt = gu_ref[d]
            up = gu_ref[d + 4]
            s = _sig(gt)
            dgu_ref[d] = dz * up * s * (1.0 + gt * (1.0 - s))
            dgu_ref[d + 4] = dz * gt * s

    blocks = _rt3(N_DEV, T, FF_BLK)
    return _tok_call(
        body, name=name, S=S, T=T,
        ins=[_tile(dy, _rt(T, D_MODEL)), _whole(wfo), _tile(gu, blocks)],
        outs=[_out_tile((N_DEV, S, FF_BLK), BF16, blocks)])[0]


def _ffn_in_bwd_call(dgu, wfi, dx2, x1, mix, g3, g2, *, name, T=ROW_TILE):
    S = x1.shape[0]

    def body(dgu_ref, w_ref, dx2_ref, x1_ref, mix_ref, g3_ref, g2_ref, dx1_ref, dmix_ref, dg3_ref, dg2_ref):
        dg3 = jnp.zeros((1, D_MODEL), F32)
        dg2 = jnp.zeros((1, D_MODEL), F32)
        for rows in _sub_tiles(T):
            dh = _dot(dgu_ref[0, rows, :], w_ref[0], NN)
            for d in range(1, N_DEV):
                dh = dh + _dot(dgu_ref[d, rows, :], w_ref[d], NN)
            r3, xh = _rms_stats(x1_ref[rows, :])
            d3, g = _rms_bwd(xh, r3, g3_ref[...], dh)
            dg3 = dg3 + g
            dx1 = dx2_ref[rows, :] + d3
            dx1_ref[rows, :] = dx1
            r2, mh = _rms_stats(mix_ref[rows, :])
            dmix, g = _rms_bwd(mh, r2, g2_ref[...], dx1)
            dg2 = dg2 + g
            dmix_ref[rows, :] = dmix.astype(BF16)
        _accum(dg3_ref, dg3)
        _accum(dg2_ref, dg2)

    row = _rt(T, D_MODEL)
    return _tok_call(
        body, name=name, S=S, T=T, semantics="arbitrary",
        ins=[_tile(dgu, _rt3(N_DEV, T, FF_BLK)), _whole(wfi), _tile(dx2, row), _tile(x1, row), _tile(mix, row),
             _whole(g3), _whole(g2)],
        outs=[_out_tile((S, D_MODEL), F32, row), _out_tile((S, D_MODEL), BF16, row),
              _out_acc((1, D_MODEL)), _out_acc((1, D_MODEL))])


def _mixer_bwd_call(dmix, t1, t2, pg, w_out, w_bg, w_bs, *, name, T=ROW_TILE):
    S = dmix.shape[0]

    def body(dmix_ref, t1_ref, t2_ref, pg_ref, wo_ref, wbg_ref, wbs_ref, dt1_ref, dt2_ref, dpg_ref, dyg_ref, dys_ref):
        gg, gs = pl.ds(0, D_MODEL), pl.ds(D_MODEL, D_MODEL)
        for rows in _sub_tiles(T):
            dm = _dot(dmix_ref[rows, :], wo_ref[...], NT)
            sg = _sig(pg_ref[rows, gg].astype(F32))
            ss = _sig(pg_ref[rows, gs].astype(F32))
            dt1 = (dm * sg).astype(BF16)
            dt2 = (dm * ss).astype(BF16)
            dt1_ref[rows, :] = dt1
            dt2_ref[rows, :] = dt2
            dpg_ref[rows, gg] = (dm * t1_ref[rows, :].astype(F32) * sg * (1.0 - sg)).astype(BF16)
            dpg_ref[rows, gs] = (dm * t2_ref[rows, :].astype(F32) * ss * (1.0 - ss)).astype(BF16)
            dyg_ref[rows, :] = _dot(dt1, wbg_ref[...], NT).astype(BF16)
            dys_ref[rows, :] = _dot(dt2, wbs_ref[...], NT).astype(BF16)

    row = _rt(T, D_MODEL)
    wide = _rt(T, 2 * D_MODEL)
    b16 = lambda: _out_tile((S, D_MODEL), BF16, row)
    return _tok_call(
        body, name=name, S=S, T=T,
        ins=[_tile(dmix, row), _tile(t1, row), _tile(t2, row), _tile(pg, wide), _whole(w_out), _whole(w_bg), _whole(w_bs)],
        outs=[b16(), b16(), _out_tile((S, 2 * D_MODEL), BF16, wide), b16(), b16()])


def _in_proj_bwd_call(dpa, dlow, dps, dpg, w_in_t, x, dx1, g1, *, name, T=ROW_TILE):
    S = x.shape[0]

    def body(dpa_ref, dl_ref, dps_ref, dpg_ref, w_ref, x_ref, dx1_ref, g1_ref, gx_ref, dg1_ref):
        da = (_dot(dpa_ref[...], w_ref[A_COLS[0]:A_COLS[1], :], NN)
              + _dot(dl_ref[...], w_ref[LOW_COLS[0]:LOW_COLS[0] + LANES, :], NN)
              + _dot(dps_ref[...], w_ref[S_COLS[0]:S_COLS[1], :], NN)
              + _dot(dpg_ref[...], w_ref[G_COLS[0]:G_COLS[1], :], NN))
        r, xh = _rms_stats(x_ref[...])
        dxa, dg = _rms_bwd(xh, r, g1_ref[...], da)
        gx_ref[...] = dx1_ref[...] + dxa
        _accum(dg1_ref, dg)

    row = _rt(T, D_MODEL)
    return _tok_call(
        body, name=name, S=S, T=T, semantics="arbitrary",
        ins=[_tile(dpa, _rt(T, dpa.shape[1])), _tile(dlow, _rt(T, dlow.shape[1])), _tile(dps, _rt(T, dps.shape[1])),
             _tile(dpg, _rt(T, dpg.shape[1])), _whole(w_in_t), _tile(x, row), _tile(dx1, row), _whole(g1)],
        outs=[_out_tile((S, D_MODEL), F32, row), _out_acc((1, D_MODEL))])


TOKEN_TILE = 512


def _weight_grads_part(arrays, pairs, *, tk=TOKEN_TILE, out_dtype=BF16):
    S = arrays[0].shape[-2]
    tk = min(tk, S)
    n_in = len(arrays)

    def out_shape(i, j):
        a, b = arrays[i], arrays[j]
        if a.ndim == 3:
            return (a.shape[0], a.shape[2], b.shape[1])
        if b.ndim == 3:
            return (b.shape[0], a.shape[1], b.shape[2])
        return (a.shape[1], b.shape[1])

    shapes = [out_shape(i, j) for i, j in pairs]

    in_place = out_dtype == F32

    def body(ins, outs, accs):
        k = pl.program_id(0)
        if in_place:
            accs = outs

        @pl.when(k == 0)
        def _():
            for acc in accs:
                acc[...] = jnp.zeros_like(acc)

        for (i, j), acc in zip(pairs, accs):
            a_ref, b_ref = ins[i], ins[j]
            if len(a_ref.shape) == 3:
                for n in range(a_ref.shape[0]):
                    acc[n] += _dot(a_ref[n], b_ref[...], TN)
            elif len(b_ref.shape) == 3:
                a = a_ref[...]
                for n in range(b_ref.shape[0]):
                    acc[n] += _dot(a, b_ref[n], TN)
            else:
                b = b_ref[...]
                for m0 in range(0, a_ref.shape[1], 1024):
                    m1 = min(a_ref.shape[1], m0 + 1024)
                    acc[m0:m1, :] += _dot(a_ref[:, m0:m1], b, TN)

        if not in_place:
            @pl.when(k == S // tk - 1)
            def _():
                for out, acc in zip(outs, accs):
                    out[...] = acc[...].astype(out.dtype)

    def in_spec(a):
        if a.ndim == 3:
            return pl.BlockSpec((a.shape[0], tk, a.shape[2]), lambda k: (0, k, 0))
        return pl.BlockSpec((tk, a.shape[1]), lambda k: (k, 0))

    return dict(
        body=body, steps=S // tk, arrays=list(arrays),
        in_specs=[in_spec(a) for a in arrays],
        out_specs=[_acc(s) for s in shapes],
        out_shapes=[pltpu.HBM(s, out_dtype) for s in shapes],
        scratch=[] if in_place else [pltpu.VMEM(s, F32) for s in shapes],
        tiles=[(in_spec(a).block_shape, a.dtype) for a in arrays],
        resident=[(s, F32) for s in shapes] + ([] if in_place else [(s, BF16) for s in shapes]))


def _weight_grads_call(arrays, pairs, *, name, tk=TOKEN_TILE):
    part = _weight_grads_part(arrays, pairs, tk=tk)
    n_in, n_out = len(part["arrays"]), len(part["out_shapes"])

    def body(*refs):
        part["body"](refs[:n_in], refs[n_in:n_in + n_out], refs[n_in + n_out:])

    return pl.pallas_call(
        body, name=name, grid=(part["steps"],),
        in_specs=part["in_specs"], out_specs=part["out_specs"], out_shape=part["out_shapes"],
        scratch_shapes=part["scratch"],
        compiler_params=_cp(("arbitrary",), _vmem_limit(part["tiles"], part["resident"])),
    )(*_hbm(*part["arrays"]))


def _ffn_in_grad_call(h, dgu, *, name, tk=TOKEN_TILE):
    S = h.shape[0]
    tk = min(tk, S)
    nb = 4
    shape = (nb, FF_BLK, D_MODEL)

    def body(h_ref, dgu_ref, out_ref, acc):
        k = pl.program_id(1)

        @pl.when(k == 0)
        def _():
            acc[...] = jnp.zeros_like(acc)

        hv = h_ref[...]
        for n in range(nb):
            acc[n] += _dot(dgu_ref[n], hv, TN)

        @pl.when(k == S // tk - 1)
        def _():
            out_ref[...] = acc[...].astype(out_ref.dtype)

    tiles = [((tk, D_MODEL), BF16), ((nb, tk, FF_BLK), BF16), (shape, BF16)]
    return pl.pallas_call(
        body, name=name, grid=(N_DEV // nb, S // tk),
        in_specs=[pl.BlockSpec((tk, D_MODEL), lambda g, k: (k, 0)),
                  pl.BlockSpec((nb, tk, FF_BLK), lambda g, k: (g, k, 0))],
        out_specs=pl.BlockSpec(shape, lambda g, k: (g, 0, 0)),
        out_shape=pltpu.HBM((N_DEV, FF_BLK, D_MODEL), BF16),
        scratch_shapes=[pltpu.VMEM(shape, F32)],
        compiler_params=_cp(("parallel", "arbitrary"), _vmem_limit(tiles, [(shape, F32)])),
    )(*_hbm(h, dgu))


GLA_TILE = 256
Q_OFF, K_OFF, V_OFF, R_OFF = 0, QK, 2 * QK, 2 * QK + GV


def _tri(lower):
    r = lax.broadcasted_iota(jnp.int32, (CHUNK, CHUNK), 0)
    c = lax.broadcasted_iota(jnp.int32, (CHUNK, CHUNK), 1)
    return jnp.where((r >= c) if lower else (c >= r), 1.0, 0.0).astype(BF16)


def _gla_fwd_call(pa, alow, wg, bgate, gnorm, *, name):
    S = pa.shape[0]
    Tg = min(GLA_TILE, S)
    cb = Tg // CHUNK

    def body(pa_ref, al_ref, wg_ref, bg_ref, gn_ref, y_ref, st_ref, state):
        @pl.when(pl.program_id(0) == 0)
        def _():
            state[...] = jnp.zeros_like(state)

        logit = _dot(al_ref[...], wg_ref[...], NN) + bg_ref[...]
        ls = _logsig(logit) * (1.0 / 16.0)
        tri = _tri(True)
        for c in range(cb):
            rows = pl.ds(c * CHUNK, CHUNK)
            cum = _exact_mask_dot(tri, ls[c * CHUNK:(c + 1) * CHUNK])
            tot = cum[CHUNK - 1:CHUNK]
            kd = (pa_ref[rows, pl.ds(K_OFF, QK)].astype(F32) * jnp.exp(tot - cum)).astype(BF16)
            decay = jnp.exp(tot)
            for h in range(HEADS):
                lanes = slice(h * DK, (h + 1) * DK)
                new = state[h] * decay[:, lanes] + _dot(pa_ref[rows, pl.ds(V_OFF + h * DV, DV)], kd[:, lanes], TN)
                state[h] = new
                st_ref[c, h] = new
        for c in range(cb):
            rows = pl.ds(c * CHUNK, CHUNK)
            for h in range(HEADS):
                qs = (pa_ref[rows, pl.ds(Q_OFF + h * DK, DK)].astype(F32) * Q_SCALE).astype(BF16)
                o = _dot(qs, st_ref[c, h].astype(BF16), NT)
                rs = lax.rsqrt(jnp.mean(o * o, axis=-1, keepdims=True) + EPS)
                rr = pa_ref[rows, pl.ds(R_OFF + h * DV, DV)].astype(F32)
                y_ref[rows, pl.ds(h * DV, DV)] = (o * rs * gn_ref[h] * (rr * _sig(rr))).astype(BF16)

    return pl.pallas_call(
        body, name=name, grid=(S // Tg,),
        in_specs=[
            pl.BlockSpec((Tg, 2 * QK + 2 * GV), lambda t: (t, 0)),
            pl.BlockSpec((Tg, LANES), lambda t: (t, 0)),
            pl.BlockSpec((LANES, QK), lambda t: (0, 0)),
            pl.BlockSpec((1, QK), lambda t: (0, 0)),
            pl.BlockSpec((HEADS, 1, DV), lambda t: (0, 0, 0)),
        ],
        out_specs=[
            pl.BlockSpec((Tg, GV), lambda t: (t, 0)),
            pl.BlockSpec((cb, HEADS, DV, DK), lambda t: (t, 0, 0, 0)),
        ],
        out_shape=[pltpu.HBM((S, GV), BF16), pltpu.HBM((S // CHUNK, HEADS, DV, DK), F32)],
        scratch_shapes=[pltpu.VMEM((HEADS, DV, DK), F32)],
        compiler_params=_cp(("arbitrary",), VMEM_BIG),
    )(*_hbm(pa, alow, wg, bgate, gnorm))


def _gla_bwd_call(pa, alow, wg, bgate, gnorm, states, dy, *, name):
    S = pa.shape[0]
    Tg = min(GLA_TILE, S)
    cb = Tg // CHUNK
    nt = S // Tg

    def rev(t):
        return nt - 1 - t

    def body(pa_ref, al_ref, wg_ref, bg_ref, gn_ref, st_ref, prev_ref, dy_ref,
             dpa_ref, dl_ref, dgn_ref, dbg_ref, carry, pbuf):
        t = pl.program_id(0)

        @pl.when(t == 0)
        def _():
            carry[...] = jnp.zeros_like(carry)
            dgn_ref[...] = jnp.zeros_like(dgn_ref)
            dbg_ref[...] = jnp.zeros_like(dbg_ref)

        logit = _dot(al_ref[...], wg_ref[...], NN) + bg_ref[...]
        ls = _logsig(logit) * (1.0 / 16.0)
        sneg = 1.0 / (1.0 + jnp.exp(logit))
        tri = _tri(True)
        upper = _tri(False)
        first_tile = rev(t) == 0
        heads = range(HEADS)

        w, decay, kd = [], [], []
        for c in range(cb):
            rows = pl.ds(c * CHUNK, CHUNK)
            cum = _exact_mask_dot(tri, ls[c * CHUNK:(c + 1) * CHUNK])
            tot = cum[CHUNK - 1:CHUNK]
            w.append(jnp.exp(tot - cum))
            decay.append(jnp.exp(tot))
            kd.append(pa_ref[rows, pl.ds(K_OFF, QK)].astype(F32) * w[c])

        dgn = [jnp.zeros((1, DV), F32) for _ in heads]
        for c in range(cb):
            rows = pl.ds(c * CHUNK, CHUNK)
            for h in heads:
                gn = gn_ref[h]
                qs = (pa_ref[rows, pl.ds(Q_OFF + h * DK, DK)].astype(F32) * Q_SCALE).astype(BF16)
                st16 = st_ref[c, h].astype(BF16)
                o = _dot(qs, st16, NT)
                rs = lax.rsqrt(jnp.mean(o * o, axis=-1, keepdims=True) + EPS)
                oh = o * rs
                rr = pa_ref[rows, pl.ds(R_OFF + h * DV, DV)].astype(F32)
                sr = _sig(rr)
                dyv = dy_ref[rows, pl.ds(h * DV, DV)].astype(F32)
                dpa_ref[rows, pl.ds(R_OFF + h * DV, DV)] = (
                    dyv * oh * gn * sr * (1.0 + rr * (1.0 - sr))).astype(BF16)
                don = dyv * (rr * sr)
                dgn[h] = dgn[h] + jnp.sum(don * oh, axis=0, keepdims=True)
                doh = don * gn
                do16 = (rs * (doh - oh * jnp.mean(doh * oh, axis=-1, keepdims=True))).astype(BF16)
                dpa_ref[rows, pl.ds(Q_OFF + h * DK, DK)] = (_dot(do16, st16, NN) * Q_SCALE).astype(BF16)
                pbuf[c, h] = _dot(do16, qs, TN)
        for h in heads:
            dgn_ref[h] += dgn[h]

        dkd = [[None] * HEADS for _ in range(cb)]
        ddecay = [[None] * HEADS for _ in range(cb)]
        for c in reversed(range(cb)):
            rows = pl.ds(c * CHUNK, CHUNK)
            for h in heads:
                lanes = slice(h * DK, (h + 1) * DK)
                gt = pbuf[c, h] + carry[h]
                gt16 = gt.astype(BF16)
                dkd[c][h] = _dot(pa_ref[rows, pl.ds(V_OFF + h * DV, DV)], gt16, NN)
                dpa_ref[rows, pl.ds(V_OFF + h * DV, DV)] = _dot(kd[c][:, lanes].astype(BF16), gt16, NT).astype(BF16)
                if c > 0:
                    st_prev = st_ref[c - 1, h]
                else:
                    st_prev = jnp.where(first_tile, 0.0, prev_ref[0, h])
                ddecay[c][h] = jnp.sum(gt * st_prev, axis=0, keepdims=True)
                carry[h] = gt * decay[c][:, lanes]

        dbg = jnp.zeros((1, QK), F32)
        for c in range(cb):
            rows = pl.ds(c * CHUNK, CHUNK)
            dkd_c = jnp.concatenate(dkd[c], axis=1)
            dpa_ref[rows, pl.ds(K_OFF, QK)] = (dkd_c * w[c]).astype(BF16)
            e = dkd_c * kd[c]
            dtot = jnp.sum(e, axis=0, keepdims=True) + jnp.concatenate(ddecay[c], axis=1) * decay[c]
            dls = dtot - _exact_mask_dot(upper, e)
            dlogit = dls * (1.0 / 16.0) * sneg[c * CHUNK:(c + 1) * CHUNK]
            dl_ref[rows, :] = dlogit.astype(BF16)
            dbg = dbg + jnp.sum(dlogit, axis=0, keepdims=True)
        dbg_ref[...] += dbg

    wide = 2 * QK + 2 * GV
    tiles = [((Tg, wide), BF16), ((cb + 1, HEADS, DV, DK), F32), ((Tg, GV), BF16), ((Tg, wide), BF16),
             ((Tg, QK), BF16)]
    resident = [((cb + 1, HEADS, DV, DK), F32)]
    return pl.pallas_call(
        body, name=name, grid=(nt,),
        in_specs=[
            pl.BlockSpec((Tg, wide), lambda t: (rev(t), 0)),
            pl.BlockSpec((Tg, LANES), lambda t: (rev(t), 0)),
            pl.BlockSpec((LANES, QK), lambda t: (0, 0)),
            pl.BlockSpec((1, QK), lambda t: (0, 0)),
            pl.BlockSpec((HEADS, 1, DV), lambda t: (0, 0, 0)),
            pl.BlockSpec((cb, HEADS, DV, DK), lambda t: (rev(t), 0, 0, 0)),
            pl.BlockSpec((1, HEADS, DV, DK), lambda t: (jnp.maximum(rev(t) * cb - 1, 0), 0, 0, 0)),
            pl.BlockSpec((Tg, GV), lambda t: (rev(t), 0)),
        ],
        out_specs=[
            pl.BlockSpec((Tg, wide), lambda t: (rev(t), 0)),
            pl.BlockSpec((Tg, QK), lambda t: (rev(t), 0)),
            pl.BlockSpec((HEADS, 1, DV), lambda t: (0, 0, 0)),
            pl.BlockSpec((1, QK), lambda t: (0, 0)),
        ],
        out_shape=[pltpu.HBM((S, wide), BF16), pltpu.HBM((S, QK), BF16),
                   jax.ShapeDtypeStruct((HEADS, 1, DV), F32), jax.ShapeDtypeStruct((1, QK), F32)],
        scratch_shapes=[pltpu.VMEM((HEADS, DV, DK), F32), pltpu.VMEM((cb, HEADS, DV, DK), F32)],
        compiler_params=_cp(("arbitrary",), _vmem_limit(tiles, resident)),
    )(*_hbm(pa, alow, wg, bgate, gnorm, states, states, dy))


SGU_TILE = 256


def _sgu_mask():
    r = lax.broadcasted_iota(jnp.int32, (SBLOCK, SBLOCK), 0)
    c = lax.broadcasted_iota(jnp.int32, (SBLOCK, SBLOCK), 1)
    return (c < CHUNK) | (r >= CHUNK)


def _ln_stats(vf):
    mu = jnp.mean(vf, axis=-1, keepdims=True)
    xc = vf - mu
    rs = lax.rsqrt(jnp.mean(xc * xc, axis=-1, keepdims=True) + EPS)
    return rs, xc * rs


def _sgu_fwd_call(ps, ln_g, ln_b, w_sp, b_sp, *, name):
    S = ps.shape[0]
    Ts = min(SGU_TILE, S)

    def body(ps_ref, lg_ref, lb_ref, w_ref, b_ref, y_ref):
        mask = _sgu_mask()
        for g in range(GROUPS):
            wm = jnp.where(mask, w_ref[g], 0.0).astype(BF16)
            for p in range(Ts // SBLOCK):
                rows = pl.ds(p * SBLOCK, SBLOCK)
                u = _gelu(ps_ref[rows, pl.ds(g * DG, DG)].astype(F32))
                _, xh = _ln_stats(_gelu(ps_ref[rows, pl.ds(D_MODEL + g * DG, DG)].astype(F32)))
                vn = xh * lg_ref[g] + lb_ref[g]
                mixed = _dot(wm, vn.astype(BF16), NN) + b_ref[g]
                y_ref[rows, pl.ds(g * DG, DG)] = (u * mixed).astype(BF16)

    full3 = lambda a, b, c: pl.BlockSpec((a, b, c), lambda t: (0, 0, 0))
    return pl.pallas_call(
        body, name=name, grid=(S // Ts,),
        in_specs=[pl.BlockSpec((Ts, 2 * D_MODEL), lambda t: (t, 0)),
                  full3(GROUPS, 1, DG), full3(GROUPS, 1, DG), full3(GROUPS, SBLOCK, SBLOCK), full3(GROUPS, SBLOCK, 1)],
        out_specs=pl.BlockSpec((Ts, D_MODEL), lambda t: (t, 0)),
        out_shape=pltpu.HBM((S, D_MODEL), BF16),
        compiler_params=_cp(("parallel",)),
    )(*_hbm(ps, ln_g, ln_b, w_sp, b_sp))


def _sgu_bwd_call(ps, ln_g, ln_b, w_sp, b_sp, dy, *, name):
    S = ps.shape[0]
    Ts = min(SGU_TILE, S)

    def body(ps_ref, lg_ref, lb_ref, w_ref, b_ref, dy_ref, ds_ref, dlg_ref, dlb_ref, dw_ref, db_ref):
        @pl.when(pl.program_id(0) == 0)
        def _():
            dlg_ref[...] = jnp.zeros_like(dlg_ref)
            dlb_ref[...] = jnp.zeros_like(dlb_ref)
            dw_ref[...] = jnp.zeros_like(dw_ref)
            db_ref[...] = jnp.zeros_like(db_ref)

        mask = _sgu_mask()
        for g in range(GROUPS):
            wm = jnp.where(mask, w_ref[g], 0.0).astype(BF16)
            lg = lg_ref[g]
            for p in range(Ts // SBLOCK):
                rows = pl.ds(p * SBLOCK, SBLOCK)
                su = ps_ref[rows, pl.ds(g * DG, DG)].astype(F32)
                sv = ps_ref[rows, pl.ds(D_MODEL + g * DG, DG)].astype(F32)
                u, du = _gelu_and_grad(su)
                gv, dgv = _gelu_and_grad(sv)
                rs, xh = _ln_stats(gv)
                vn16 = (xh * lg + lb_ref[g]).astype(BF16)
                mixed = _dot(wm, vn16, NN) + b_ref[g]
                dyv = dy_ref[rows, pl.ds(g * DG, DG)].astype(F32)
                ds_ref[rows, pl.ds(g * DG, DG)] = (dyv * mixed * du).astype(BF16)
                dmix = dyv * u
                dmix16 = dmix.astype(BF16)
                db_ref[g] += jnp.sum(dmix, axis=-1, keepdims=True)
                dw_ref[g] += jnp.where(mask, _dot(dmix16, vn16, NT), 0.0)
                dvn = _dot(wm, dmix16, TN)
                dlg_ref[g] += jnp.sum(dvn * xh, axis=0, keepdims=True)
                dlb_ref[g] += jnp.sum(dvn, axis=0, keepdims=True)
                dxh = dvn * lg
                dvf = rs * (dxh - jnp.mean(dxh, axis=-1, keepdims=True)
                            - xh * jnp.mean(dxh * xh, axis=-1, keepdims=True))
                ds_ref[rows, pl.ds(D_MODEL + g * DG, DG)] = (dvf * dgv).astype(BF16)

    full3 = lambda a, b, c: pl.BlockSpec((a, b, c), lambda t: (0, 0, 0))
    tiles = [((Ts, 2 * D_MODEL), BF16), ((Ts, D_MODEL), BF16), ((Ts, 2 * D_MODEL), BF16)]
    return pl.pallas_call(
        body, name=name, grid=(S // Ts,),
        in_specs=[pl.BlockSpec((Ts, 2 * D_MODEL), lambda t: (t, 0)),
                  full3(GROUPS, 1, DG), full3(GROUPS, 1, DG), full3(GROUPS, SBLOCK, SBLOCK), full3(GROUPS, SBLOCK, 1),
                  pl.BlockSpec((Ts, D_MODEL), lambda t: (t, 0))],
        out_specs=[pl.BlockSpec((Ts, 2 * D_MODEL), lambda t: (t, 0)),
                   full3(GROUPS, 1, DG), full3(GROUPS, 1, DG), full3(GROUPS, SBLOCK, SBLOCK), full3(GROUPS, SBLOCK, 1)],
        out_shape=[pltpu.HBM((S, 2 * D_MODEL), BF16),
                   jax.ShapeDtypeStruct((GROUPS, 1, DG), F32), jax.ShapeDtypeStruct((GROUPS, 1, DG), F32),
                   jax.ShapeDtypeStruct((GROUPS, SBLOCK, SBLOCK), F32),
                   jax.ShapeDtypeStruct((GROUPS, SBLOCK, 1), F32)],
        compiler_params=_cp(("arbitrary",), _vmem_limit(tiles, [])),
    )(*_hbm(ps, ln_g, ln_b, w_sp, b_sp, dy))


def _position():
    return lax.axis_index("x"), lax.axis_index("y"), lax.axis_index("c")


def _gather_copies(srcs, dsts, send_sems, recv_sems, local_sems):
    x, y, c = _position()
    me, sibling = (x, y, c), (x, y, 1 - c)
    chips = [(1 - x, y), (x, 1 - y), (1 - x, 1 - y)]
    n = len(srcs)

    def slab(a, block):
        px, py, pc = block
        return dsts[a].at[4 * px + 2 * py + pc]

    def copy(a, k, block, to, src=None):
        return pltpu.make_async_remote_copy(
            src_ref=slab(a, block) if src is None else src, dst_ref=slab(a, block),
            send_sem=send_sems.at[7 * a + k], recv_sem=recv_sems.at[7 * a + k], device_id=to, device_id_type=MESH)

    mine = [pltpu.make_async_copy(srcs[a], slab(a, me), local_sems.at[a]) for a in range(n)]
    for cp in mine:
        cp.start()
    first = []
    for a in range(n):
        first.append(copy(a, 0, me, sibling, src=srcs[a]))
        first += [copy(a, 1 + j, me, (*chip, c), src=srcs[a]) for j, chip in enumerate(chips)]
    for cp in first:
        cp.start()
    passed = []
    for j, chip in enumerate(chips):
        for a in range(n):
            copy(a, 1 + j, (*chip, c), me).wait_recv()
            cp = copy(a, 4 + j, (*chip, c), sibling)
            cp.start()
            passed.append(cp)
    for a in range(n):
        copy(a, 0, sibling, me).wait_recv()
        for j, chip in enumerate(chips):
            copy(a, 4 + j, (*chip, 1 - c), me).wait_recv()
    for cp in first + passed:
        cp.wait_send()
    for cp in mine:
        cp.wait()


def _gather_copies_relayed(srcs, dsts, send_sems, recv_sems, local_sems, waves=1):
    x, y, c = _position()
    me, sibling = (x, y, c), (x, y, 1 - c)
    x_chip, y_chip, d_chip = (1 - x, y), (x, 1 - y), (1 - x, 1 - y)
    south = c == 0
    relay_from = (jnp.where(south, x, 1 - x), jnp.where(south, 1 - y, y), c)
    relay_to = (jnp.where(south, 1 - x, x), jnp.where(south, y, 1 - y), c)
    n = len(srcs)

    def cut(a, w):
        rows = srcs[a].shape[0]
        step = (rows // waves) // 16 * 16
        if waves == 1 or step == 0:
            return pl.ds(0, rows) if w == 0 else None
        return pl.ds(w * step, step if w < waves - 1 else rows - w * step)

    def slab(a, block, w):
        px, py, pc = block
        return dsts[a].at[4 * px + 2 * py + pc, cut(a, w)]

    def copy(a, k, w, block, to, own=False):
        sem = (7 * a + k) * waves + w
        return pltpu.make_async_remote_copy(
            src_ref=srcs[a].at[cut(a, w)] if own else slab(a, block, w), dst_ref=slab(a, block, w),
            send_sem=send_sems.at[sem], recv_sem=recv_sems.at[sem], device_id=to, device_id_type=MESH)

    pieces = [(a, w) for w in range(waves) for a in range(n) if cut(a, w) is not None]
    mine = [pltpu.make_async_copy(srcs[a], dsts[a].at[4 * x + 2 * y + c], local_sems.at[a]) for a in range(n)]
    for cp in mine:
        cp.start()
    sent = []
    for a, w in pieces:
        sent += [copy(a, 0, w, me, sibling, own=True), copy(a, 1, w, me, (*x_chip, c), own=True),
                 copy(a, 2, w, me, (*y_chip, c), own=True)]
    for cp in sent:
        cp.start()
    for a, w in pieces:
        copy(a, 1, w, (*x_chip, c), me).wait_recv()
        copy(a, 2, w, (*y_chip, c), me).wait_recv()
        later = [copy(a, 3, w, relay_from, relay_to), copy(a, 4, w, (*x_chip, c), sibling),
                 copy(a, 5, w, (*y_chip, c), sibling)]
        for cp in later:
            cp.start()
        sent += later
    for a, w in pieces:
        copy(a, 3, w, (*d_chip, c), me).wait_recv()
        cp = copy(a, 6, w, (*d_chip, c), sibling)
        cp.start()
        sent.append(cp)
    for a, w in pieces:
        copy(a, 0, w, sibling, me).wait_recv()
        for k, chip in ((4, x_chip), (5, y_chip), (6, d_chip)):
            copy(a, k, w, (*chip, 1 - c), me).wait_recv()
    for cp in sent:
        cp.wait_send()
    for cp in mine:
        cp.wait()


def _all_gather_hbm(shards, *, name, waves=1):
    n = len(shards)

    def body(*refs):
        srcs, dsts = refs[:n], refs[n:2 * n]
        send_sems, recv_sems, local_sems = refs[2 * n:]
        _gather_copies_relayed(srcs, dsts, send_sems, recv_sems, local_sems, waves=waves)

    return pl.pallas_call(
        body, name=name,
        in_specs=[ANY] * n, out_specs=[ANY] * n,
        out_shape=[jax.ShapeDtypeStruct((N_DEV, *s.shape), s.dtype) for s in shards],
        scratch_shapes=_comm_sems(n, waves),
    )(*shards)


FLIPS = [(fx, fy, fc) for fx in (0, 1) for fy in (0, 1) for fc in (0, 1)][1:]


def _scatter_copies(srcs, dsts, send_sems, recv_sems, local_sems, waves=1):
    n = len(srcs)
    x, y, c = _position()
    me = 4 * x + 2 * y + c
    mine = [pltpu.make_async_copy(srcs[a].at[me], dsts[a].at[me], local_sems.at[a]) for a in range(n)]
    for cp in mine:
        cp.start()
    peers = []
    for fx, fy, fc in FLIPS:
        tx = 1 - x if fx else x
        ty = 1 - y if fy else y
        tc = 1 - c if fc else c
        peers.append(((tx, ty, tc), 4 * tx + 2 * ty + tc))
    copies = []
    for w in range(waves):
        wave = []
        for k, (peer_id, peer) in enumerate(peers):
            for a in range(n):
                rows = srcs[a].shape[1]
                step = rows if waves == 1 else (rows // waves) // 16 * 16
                r0 = w * step
                cut = pl.ds(r0, step if w < waves - 1 else rows - r0)
                sem = (7 * a + k) * waves + w
                cp = pltpu.make_async_remote_copy(
                    src_ref=srcs[a].at[peer, cut], dst_ref=dsts[a].at[me, cut],
                    send_sem=send_sems.at[sem], recv_sem=recv_sems.at[sem],
                    device_id=peer_id, device_id_type=MESH)
                cp.start()
                wave.append(cp)
        for cp in wave:
            cp.wait_send()
        copies += wave
    for cp in copies:
        cp.wait_recv()
    for cp in mine:
        cp.wait()


def _pair_copies(srcs, dsts, send_sems, recv_sems, local_sems):
    x, y, c = _position()
    copies = [pltpu.make_async_remote_copy(
        src_ref=srcs[a], dst_ref=dsts[a], send_sem=send_sems.at[a], recv_sem=recv_sems.at[a],
        device_id=(x, y, 1 - c), device_id_type=MESH) for a in range(len(srcs))]
    for cp in copies:
        cp.start()
    for cp in copies:
        cp.wait()


def _chip_scatter_copies(srcs, dsts, send_sems, recv_sems, local_sems, waves=1):
    n = len(srcs)
    x, y, c = _position()
    here = 2 * x + y
    mine = [pltpu.make_async_copy(srcs[a].at[here], dsts[a].at[here], local_sems.at[a]) for a in range(n)]
    for cp in mine:
        cp.start()
    peers = []
    for fx, fy in ((1, 0), (0, 1), (1, 1)):
        tx = 1 - x if fx else x
        ty = 1 - y if fy else y
        peers.append(((tx, ty, c), 2 * tx + ty))
    copies = []
    for w in range(waves):
        wave = []
        for k, (peer_id, peer) in enumerate(peers):
            for a in range(n):
                rows = srcs[a].shape[1]
                step = rows if waves == 1 else (rows // waves) // 16 * 16
                r0 = w * step
                cut = pl.ds(r0, step if w < waves - 1 else rows - r0)
                sem = (3 * a + k) * waves + w
                cp = pltpu.make_async_remote_copy(
                    src_ref=srcs[a].at[peer, cut], dst_ref=dsts[a].at[here, cut],
                    send_sem=send_sems.at[sem], recv_sem=recv_sems.at[sem],
                    device_id=peer_id, device_id_type=MESH)
                cp.start()
                wave.append(cp)
        for cp in wave:
            cp.wait_send()
        copies += wave
    for cp in copies:
        cp.wait_recv()
    for cp in mine:
        cp.wait()


def _comm_sems(n, waves=1):
    return [pltpu.SemaphoreType.DMA((7 * n * waves,)), pltpu.SemaphoreType.DMA((7 * n * waves,)),
            pltpu.SemaphoreType.DMA((n,))]


def _handshake(peers):
    barrier = pltpu.get_barrier_semaphore()
    for peer in peers:
        pl.semaphore_signal(barrier, inc=1, device_id=peer, device_id_type=MESH)
    pl.semaphore_wait(barrier, len(peers))


def _sequencer_call(arrays, out_types, copies_fn, peers_fn, *, name, collective_id, waves=1):
    n = len(arrays)
    srcs = [jax.new_ref(a, memory_space=pltpu.MemorySpace.HBM) for a in arrays]
    dsts = [jax.empty_ref(t, memory_space=pltpu.MemorySpace.HBM) for t in out_types]
    extra = {} if waves == 1 else {"waves": waves}

    @pl.kernel(mesh=plsc.ScalarSubcoreMesh(axis_name="sequencer", num_cores=1), name=name,
               scratch_types=_comm_sems(n, waves), compiler_params=pltpu.CompilerParams(collective_id=collective_id))
    def launch(send_sems, recv_sems, local_sems):
        _handshake(peers_fn())
        copies_fn(srcs, dsts, send_sems, recv_sems, local_sems, **extra)

    launch()
    return [d[...] for d in dsts]


def _all_other_devices():
    x, y, c = _position()
    return [(1 - x if fx else x, 1 - y if fy else y, 1 - c if fc else c) for fx, fy, fc in FLIPS]


def _gather_relay_peers():
    x, y, c = _position()
    return [(x, y, 1 - c), (1 - x, y, c), (x, 1 - y, c)]


def _sibling():
    x, y, c = _position()
    return [(x, y, 1 - c)]


def _same_core_of_other_chips():
    x, y, c = _position()
    return [(1 - x, y, c), (x, 1 - y, c), (1 - x, 1 - y, c)]


def _pair_exchange_async(parts, *, name, collective_id):
    return _sequencer_call(parts, [jax.ShapeDtypeStruct(p.shape, p.dtype) for p in parts],
                           _pair_copies, _sibling, name=name, collective_id=collective_id)


def _chip_scatter_async(parts, *, name, collective_id, waves=1):
    return _sequencer_call(parts, [jax.ShapeDtypeStruct(p.shape, p.dtype) for p in parts],
                           _chip_scatter_copies, _same_core_of_other_chips, name=name, collective_id=collective_id,
                           waves=waves)


def _pair_sum_call(mine, theirs, *, name, tw=D_MODEL):
    n, r, w = mine.shape

    def body(a_ref, b_ref, o_ref):
        o_ref[...] = (a_ref[...].astype(F32) + b_ref[...].astype(F32)).astype(o_ref.dtype)

    spec = pl.BlockSpec((None, r, tw), lambda i, j: (i, 0, j))
    return pl.pallas_call(
        body, name=name, grid=(n, w // tw), in_specs=[spec, spec], out_specs=spec,
        out_shape=pltpu.HBM(mine.shape, mine.dtype),
        compiler_params=_cp(("parallel", "parallel")),
    )(*_hbm(mine, theirs))


def _scatter_blocks_async(parts, *, name, collective_id, waves=1):
    return _sequencer_call(parts, [jax.ShapeDtypeStruct(p.shape, p.dtype) for p in parts],
                           _scatter_copies, _all_other_devices, name=name, collective_id=collective_id, waves=waves)


def _all_gather_async(shards, *, name, collective_id):
    return _sequencer_call(shards, [jax.ShapeDtypeStruct((N_DEV, *s.shape), s.dtype) for s in shards],
                           _gather_copies_relayed, _gather_relay_peers, name=name, collective_id=collective_id)


def _adamw_math(w, g, m, v):
    m = ADAM_B1 * m + (1.0 - ADAM_B1) * g
    v = ADAM_B2 * v + (1.0 - ADAM_B2) * (g * g)
    m_hat = m / (1.0 - ADAM_B1 ** ADAM_STEP)
    v_hat = v / (1.0 - ADAM_B2 ** ADAM_STEP)
    delta = -ADAM_LR * (m_hat / (jnp.sqrt(v_hat) + ADAM_EPS) + ADAM_WD * w)
    return delta, m, v


def _adamw_reduce_call(recv, w, m, v, *, name, T=128):
    R, W = w.shape
    n_parts = recv.shape[0]
    if R % T == 0:
        tr, tw = T, W
    elif (R // 2) % 16 == 0:
        tr, tw = R // 2, W
    else:
        tr, tw = R, 2 * LANES

    def body(p_ref, w_ref, m_ref, v_ref, g_out, d_out, m_out, v_out):
        g = p_ref[0].astype(F32)
        for d in range(1, n_parts):
            g = g + p_ref[d].astype(F32)
        g_out[...] = g
        d_out[...], m_out[...], v_out[...] = _adamw_math(w_ref[...], g, m_ref[...], v_ref[...])

    row = pl.BlockSpec((tr, tw), lambda i, j: (i, j))
    out = pltpu.HBM((R, W), F32)
    return pl.pallas_call(
        body, name=name, grid=(R // tr, W // tw),
        in_specs=[pl.BlockSpec((n_parts, tr, tw), lambda i, j: (0, i, j)), row, row, row],
        out_specs=[row] * 4, out_shape=[out] * 4,
        compiler_params=_cp(("parallel", "parallel"), VMEM_BIG),
    )(*_hbm(recv, w, m, v))


SMALL_EARLY = (("b_gate", 8), ("w_spatial", 512), ("b_spatial", 8), ("norm_post_mix", 8), ("norm_pre_ffn", 8),
               ("norm_post_ffn", 8), ("w_gate_up", 128), ("gla_norm", 64), ("sgu_ln_g", 64), ("sgu_ln_b", 64))
SMALL_EARLY_AT = {}
for _name, _rows in SMALL_EARLY:
    SMALL_EARLY_AT[_name] = sum(r for _, r in SMALL_EARLY[:len(SMALL_EARLY_AT)])
SMALL_EARLY_ROWS = sum(r for _, r in SMALL_EARLY)
SMALL_LATE_ROWS = 16


def _small_early_rows(grads):
    def rows(a, n_rows):
        a = a.reshape(-1, LANES)
        return jnp.pad(a, ((0, n_rows - a.shape[0]), (0, 0)))

    def device_major(a, n_rows):
        r, c = a.shape[0], a.shape[1] // N_DEV
        a = a.reshape(r, N_DEV, c).transpose(1, 0, 2)
        a = jnp.pad(a, ((0, 0), (0, n_rows // N_DEV - r), (0, LANES - c)))
        return a.reshape(n_rows, LANES)

    pieces = []
    for name, n_rows in SMALL_EARLY:
        g = grads[name]
        if name in SMALL_SHARDED:
            pieces.append(device_major(g.reshape(g.shape[0], -1) if g.ndim == 2 else g.reshape(g.shape[0], g.shape[-1]), n_rows))
        else:
            pieces.append(rows(g, n_rows))
    return jnp.concatenate(pieces, axis=0)


def _small_update_call(early, dg_pre_mix, loss_part, w, m, v, *, name):
    names = list(SMALL)
    n_p = len(names)
    E = SMALL_EARLY_ROWS

    def reduce_body(early_ref, dg1_ref, loss_ref, tot, got_early, got_late, late, send_sems, recv_sems, local_sems):
        for k in range(D_MODEL // LANES):
            late[k:k + 1, :] = dg1_ref[:, k * LANES:(k + 1) * LANES]
        late[8:16, :] = jnp.broadcast_to(loss_ref[...], (8, LANES))
        _gather_copies([early_ref, late], [got_early, got_late], send_sems, recv_sems, local_sems)
        acc, acc_late = got_early[0], got_late[0]
        for d in range(1, N_DEV):
            acc, acc_late = acc + got_early[d], acc_late + got_late[d]
        tot[0:E, :] = acc
        tot[E:E + SMALL_LATE_ROWS, :] = acc_late

    total = pl.pallas_call(
        reduce_body, name=name + "_reduce",
        in_specs=[VMEM_SPEC] * 3, out_specs=VMEM_SPEC,
        out_shape=jax.ShapeDtypeStruct((E + SMALL_LATE_ROWS, LANES), F32),
        scratch_shapes=[pltpu.VMEM((N_DEV, E, LANES), F32), pltpu.VMEM((N_DEV, SMALL_LATE_ROWS, LANES), F32),
                        pltpu.VMEM((SMALL_LATE_ROWS, LANES), F32),
                        pltpu.SemaphoreType.DMA((14,)), pltpu.SemaphoreType.DMA((14,)), pltpu.SemaphoreType.DMA((2,))],
    )(early, dg_pre_mix, loss_part)

    def body(tot, *rest):
        w_refs, m_refs, v_refs = [dict(zip(names, rest[i * n_p:(i + 1) * n_p])) for i in range(3)]
        outs = rest[3 * n_p:]
        loss_out = outs[0]
        g_out, d_out, m_out, v_out = [dict(zip(names, outs[1 + i * n_p:1 + (i + 1) * n_p])) for i in range(4)]
        loss_out[...] = tot[E + 8:E + 9, :]

        x, y, c = _position()
        me = 4 * x + 2 * y + c

        def update(name, g, ix):
            g_out[name][ix] = g
            d_out[name][ix], m_out[name][ix], v_out[name][ix] = _adamw_math(
                w_refs[name][ix], g, m_refs[name][ix], v_refs[name][ix])

        for name in names:
            shape = w[name].shape
            if name in SMALL_SHARDED:
                per_dev = dict(SMALL_EARLY)[name] // N_DEV
                at = pl.multiple_of(SMALL_EARLY_AT[name] + me * per_dev, 8)
                g = tot[pl.ds(at, per_dev), :]
                update(name, g[:shape[1], :shape[2]], (0,))
            elif name == "w_spatial":
                for grp in range(GROUPS):
                    at = SMALL_EARLY_AT[name] + grp * SBLOCK
                    update(name, tot[at:at + SBLOCK, :], (0, grp))
            elif name == "b_spatial":
                at = SMALL_EARLY_AT[name]
                update(name, tot[at:at + GROUPS, :], (0,))
            else:
                at = E if name == "norm_pre_mix" else SMALL_EARLY_AT[name]
                for k in range(shape[1] // LANES):
                    update(name, tot[at + k:at + k + 1, :], (slice(None), pl.ds(k * LANES, LANES)))

    state = [s[n] for s in (w, m, v) for n in names]
    out_shapes = [jax.ShapeDtypeStruct((1, LANES), F32)] + [jax.ShapeDtypeStruct(w[n].shape, F32) for n in names] * 4
    outs = pl.pallas_call(
        body, name=name + "_adamw",
        in_specs=[VMEM_SPEC] * (1 + len(state)), out_specs=[VMEM_SPEC] * len(out_shapes), out_shape=out_shapes,
    )(total, *state)
    per_name = {n: tuple(outs[1 + i * n_p + j] for i in range(4)) for j, n in enumerate(names)}
    return outs[0], per_name


def _tile_rows(n_elems):
    return -(-n_elems // (8 * LANES)) * 8


def _pack_rows(parts, rows):
    pieces = []
    for p in parts:
        q = p.reshape(-1, LANES)
        pieces.append(jnp.pad(q, ((0, _tile_rows(p.size) - q.shape[0]), (0, 0))))
    buf = jnp.concatenate(pieces, axis=0)
    return jnp.pad(buf, ((0, rows - buf.shape[0]), (0, 0)))


def _in_w_blocks_call(a, live, finished, *, name, tk=TOKEN_TILE):
    S = a.shape[0]
    tk = min(tk, S)
    steps = S // tk
    n_live = len(live)
    n_chips = N_DEV // 2
    out_shape = (n_chips, IN_BLK, D_MODEL)

    def body(a_ref, *rest):
        live_refs, done_refs = rest[:n_live], rest[n_live:n_live + len(finished)]
        keep_ref, send_ref = rest[n_live + len(finished):n_live + len(finished) + 2]
        accs = rest[n_live + len(finished) + 2:]
        k = pl.program_id(0)

        @pl.when(k == 0)
        def _():
            for acc in accs:
                acc[...] = jnp.zeros_like(acc)

        av = a_ref[...]
        for src, acc in zip(live_refs, accs):
            for m0 in range(0, src.shape[1], 1024):
                m1 = min(src.shape[1], m0 + 1024)
                acc[m0:m1, :] += _dot(src[:, m0:m1], av, TN)

        @pl.when(k == steps - 1)
        def _():
            groups = [(acc, cols) for acc, (_, cols) in zip(accs, live)]
            groups += [(ref, cols) for ref, (_, cols) in zip(done_refs, finished)]
            core = lax.axis_index("c")

            def cut(d, out_ref):
                lo, hi = IN_BLK * d, IN_BLK * (d + 1)
                for ref, (c0, c1) in groups:
                    s0, e0 = max(lo, c0), min(hi, c1)
                    if s0 < e0:
                        out_ref[d // 2, s0 - lo:e0 - lo, :] = ref[s0 - c0:e0 - c0, :].astype(BF16)

            for d in range(N_DEV):
                pl.when(core == d % 2)(lambda d=d: cut(d, keep_ref))
                pl.when(core != d % 2)(lambda d=d: cut(d, send_ref))

    acc_shapes = [(arr.shape[1], D_MODEL) for arr, _ in live]
    tiles = [((tk, D_MODEL), BF16)] + [((tk, arr.shape[1]), BF16) for arr, _ in live]
    resident = ([(arr.shape, arr.dtype) for arr, _ in finished] + [(out_shape, BF16)] * 2
                + [(s, F32) for s in acc_shapes])
    return pl.pallas_call(
        body, name=name, grid=(steps,),
        in_specs=[pl.BlockSpec((tk, D_MODEL), lambda k: (k, 0))]
        + [pl.BlockSpec((tk, arr.shape[1]), lambda k: (k, 0)) for arr, _ in live]
        + [_res(arr.shape) for arr, _ in finished],
        out_specs=[_acc(out_shape)] * 2,
        out_shape=[pltpu.HBM(out_shape, BF16)] * 2,
        scratch_shapes=[pltpu.VMEM(s, F32) for s in acc_shapes],
        compiler_params=_cp(("arbitrary",), _vmem_limit(tiles, resident)),
    )(*_hbm(a, *[arr for arr, _ in live], *[arr for arr, _ in finished]))


def _local_step(x, target, W, scatter, finish, pair, chip_scatter):
    g1, g2, g3, g4 = [W[n].reshape(1, D_MODEL) for n in ("norm_pre_mix", "norm_post_mix", "norm_pre_ffn", "norm_post_ffn")]
    wfi, wfo = W["w_ffn_in"], W["w_ffn_out"].reshape(4, FF_BLK, D_MODEL)
    wg = jnp.pad(W["w_gate_up"], ((0, LANES - RANK), (0, 0))).astype(BF16)
    bgate = W["b_gate"].reshape(1, QK)
    gnorm = W["gla_norm"].reshape(HEADS, 1, DV)
    ln_g = W["sgu_ln_g"].reshape(GROUPS, 1, DG)
    ln_b = W["sgu_ln_b"].reshape(GROUPS, 1, DG)
    w_sp = W["w_spatial"]
    b_sp = W["b_spatial"].reshape(GROUPS, SBLOCK, 1)

    a, pa, alow, ps, pg, w_in_t = _in_proj_call(x, g1, W["w_in_blocks"], name="in_proj")
    y_gla, states = _gla_fwd_call(pa, alow, wg, bgate, gnorm, name="gla_fwd")
    y_sgu = _sgu_fwd_call(ps, ln_g, ln_b, w_sp, b_sp, name="sgu_fwd")
    t1, t2, merged, mix, x1, h = _mixer_tail_call(y_gla, y_sgu, pg, x, W["w_branch_gla"], W["w_branch_sgu"],
                                                  W["w_out"], g2, g3, name="mixer_tail")
    gu, z = _ffn_in_call(h, wfi, name="ffn_in")
    loss, dx2, dy, dg4 = _ffn_out_loss_call(z, wfo, x1, target, g4, name="ffn_out_loss")

    grads = {"norm_post_ffn": dg4}
    done = {}
    dgu = _ffn_out_bwd_call(dy, wfo, gu, name="ffn_out_bwd")
    dw_ffn_out = _weight_grads_call([z, dy], [(0, 1)], name="d_ffn_out_w")[0].reshape(N_DEV, D_FF // N_DEV, D_MODEL)
    dw_ffn_in = _ffn_in_grad_call(h, dgu, name="d_ffn_in_w")
    dgu, dw_ffn_out, dw_ffn_in = lax.optimization_barrier((dgu, dw_ffn_out, dw_ffn_in))
    ffn_received = scatter(("w_ffn_out", "w_ffn_in"), [dw_ffn_out, dw_ffn_in])
    dx1, dmix, grads["norm_pre_ffn"], grads["norm_post_mix"] = _ffn_in_bwd_call(dgu, wfi, dx2, x1, mix, g3, g2, name="ffn_in_bwd")
    dt1, dt2, dpg, dy_gla, dy_sgu = _mixer_bwd_call(dmix, t1, t2, pg, W["w_out"], W["w_branch_gla"], W["w_branch_sgu"],
                                                    name="mixer_bwd")
    rows = D_MODEL // N_DEV
    mixer_grads = _weight_grads_call([merged, dmix, y_gla, dt1, y_sgu, dt2], [(0, 1), (2, 3), (4, 5)], name="d_mixer_w")
    dy_gla, dy_sgu, mixer_grads = lax.optimization_barrier((dy_gla, dy_sgu, mixer_grads))
    mixer_received = scatter(("w_out", "w_branch_gla", "w_branch_sgu"),
                             [g.reshape(N_DEV, rows, D_MODEL) for g in mixer_grads])
    dps, dlg, dlb, dwsp, dbsp = _sgu_bwd_call(ps, ln_g, ln_b, w_sp, b_sp, dy_sgu, name="sgu_bwd")
    dw_s, dw_g = _weight_grads_call([a, dps, dpg], [(1, 0), (2, 0)], name="d_in_w_sgu_gates")
    dy_gla, dw_s, dw_g = lax.optimization_barrier((dy_gla, dw_s, dw_g))
    dpa, dlogit, dgn, dbg = _gla_bwd_call(pa, alow, wg, bgate, gnorm, states, dy_gla, name="gla_bwd")
    ffn_received, mixer_received, dpa, dlogit = lax.optimization_barrier((ffn_received, mixer_received, dpa, dlogit))
    dlow = _mm(dlogit, wg, "nt", BF16, name="d_gate_up_x")
    keep, send = _in_w_blocks_call(a, [(dpa, A_COLS), (dlow, LOW_COLS)], [(dw_s, S_COLS), (dw_g, G_COLS)],
                                   name="d_in_w_blocks")
    ffn_received, mixer_received, send = lax.optimization_barrier((ffn_received, mixer_received, send))
    from_sibling = pair(send)
    done.update({**finish(ffn_received), **finish(mixer_received)})
    dwg = _mm(alow, dlogit, "tn", F32, name="d_gate_up_w")
    done, dwg, keep = lax.optimization_barrier((done, dwg, keep))
    chip_sum = _pair_sum_call(keep, from_sibling, name="pair_sum_w_in")
    chip_sum, dpa = lax.optimization_barrier((chip_sum, dpa))
    in_received = chip_scatter(chip_sum)
    grad_x, grads["norm_pre_mix"] = _in_proj_bwd_call(dpa, dlow, dps, dpg, w_in_t, x, dx1, g1, name="in_proj_bwd")

    grads["w_gate_up"] = dwg[:RANK]
    grads["b_gate"] = dbg
    grads["gla_norm"] = dgn
    grads["sgu_ln_g"] = dlg
    grads["sgu_ln_b"] = dlb
    grads["w_spatial"] = dwsp
    grads["b_spatial"] = dbsp
    done.update(finish({"w_in": in_received}))
    return loss, grad_x, grads, done


WEIGHTS = ("norm_pre_mix", "w_in", "w_gate_up", "b_gate", "gla_norm", "sgu_ln_g", "sgu_ln_b", "w_spatial",
           "b_spatial", "w_branch_gla", "w_branch_sgu", "w_out", "norm_post_mix", "norm_pre_ffn", "w_ffn_in",
           "w_ffn_out", "norm_post_ffn")
BIG = ("w_in", "w_branch_gla", "w_branch_sgu", "w_out", "w_ffn_in", "w_ffn_out")
MIXER = ("w_branch_gla", "w_branch_sgu", "w_out")
FFN = ("w_ffn_in", "w_ffn_out")
COLUMN_SHARDED = ("w_in", "w_ffn_in")
LATE_SCATTER_WAVES = 4
GATHER_WAVES = 4
SMALL = tuple(n for n in WEIGHTS if n not in BIG)
SMALL_SHARDED = ("w_gate_up", "gla_norm", "sgu_ln_g", "sgu_ln_b")
SMALL_GATHER_ROWS = 32


def kernel(x, norm_pre_mix, w_in, w_gate_up, b_gate, gla_norm, sgu_ln_g, sgu_ln_b, w_spatial, b_spatial, w_branch_gla, w_branch_sgu, w_out, norm_post_mix, norm_pre_ffn, w_ffn_in, w_ffn_out, norm_post_ffn, loss_target, m_norm_pre_mix, m_w_in, m_w_gate_up, m_b_gate, m_gla_norm, m_sgu_ln_g, m_sgu_ln_b, m_w_spatial, m_b_spatial, m_w_branch_gla, m_w_branch_sgu, m_w_out, m_norm_post_mix, m_norm_pre_ffn, m_w_ffn_in, m_w_ffn_out, m_norm_post_ffn, v_norm_pre_mix, v_w_in, v_w_gate_up, v_b_gate, v_gla_norm, v_sgu_ln_g, v_sgu_ln_b, v_w_spatial, v_b_spatial, v_w_branch_gla, v_w_branch_sgu, v_w_out, v_norm_post_mix, v_norm_pre_ffn, v_w_ffn_in, v_w_ffn_out, v_norm_post_ffn):
    given = dict(locals())
    def local(a, n):
        return a[0].T if n in COLUMN_SHARDED else a[0]

    w = {n: local(given[n], n) for n in WEIGHTS}
    m = {n: local(given["m_" + n], n) for n in WEIGHTS}
    v = {n: local(given["v_" + n], n) for n in WEIGHTS}
    xs, target = x[0], loss_target[0]

    small_shard = _pack_rows([w[n] for n in SMALL_SHARDED], SMALL_GATHER_ROWS)
    first = _all_gather_hbm([w["w_in"].astype(BF16), small_shard], name="gather_w_in", waves=GATHER_WAVES)
    rest, first = lax.optimization_barrier(([w[n].astype(BF16) for n in MIXER + FFN], first))
    rest_blocks = _all_gather_async(rest, name="gather_rest", collective_id=1)
    W = {n: w[n] for n in SMALL if n not in SMALL_SHARDED}
    blocks = dict(zip(MIXER + FFN, rest_blocks))
    for n in ("w_branch_gla", "w_branch_sgu", "w_out", "w_ffn_out"):
        W[n] = blocks[n].reshape(-1, D_MODEL)
    W["w_ffn_in"] = blocks["w_ffn_in"]
    W["w_in_blocks"] = first[0]
    small_blocks = first[1]
    off = 0
    for n in SMALL_SHARDED:
        r, c = w[n].shape
        blk = small_blocks[:, off:off + r * c // LANES].reshape(N_DEV, r, c)
        W[n] = blk.transpose(1, 0, 2).reshape(r, N_DEV * c)
        off += _tile_rows(r * c)

    scatter_ids = iter((3, 4))

    def scatter(names, parts):
        got = _scatter_blocks_async(parts, name="scatter_" + "_".join(names), collective_id=next(scatter_ids))
        return dict(zip(names, got))

    def finish(received):
        return {n: _adamw_reduce_call(r, w[n], m[n], v[n], name="adamw_" + n) for n, r in received.items()}

    def pair(part):
        return _pair_exchange_async([part], name="pair_w_in", collective_id=5)[0]

    def chip_scatter(part):
        return _chip_scatter_async([part], name="scatter_w_in", collective_id=6, waves=LATE_SCATTER_WAVES)[0]

    loss_part, grad_x, grads, big_done = _local_step(xs, target, W, scatter, finish, pair, chip_scatter)

    loss_row, small_done = _small_update_call(
        _small_early_rows(grads), grads["norm_pre_mix"], loss_part,
        *[{n: given[prefix + n] for n in SMALL} for prefix in ("", "m_", "v_")], name="small_update")

    def pick(i):
        return [small_done[n][i] if n in SMALL else (big_done[n][i].T if n in COLUMN_SHARDED else big_done[n][i])[None]
                for n in WEIGHTS]

    return (loss_row[0, 0], grad_x[None], *pick(0), *pick(1), *pick(2), *pick(3))
```

```python
import jax
import jax.numpy as jnp
from jax import lax
from jax.experimental import pallas as pl
from jax.experimental.pallas import tpu as pltpu
from jax.experimental.pallas import tpu_sc as plsc

F32 = jnp.float32
BF16 = jnp.bfloat16

D_MODEL = 1024
N_DEV = 8
CHUNK = 64
HEADS = 4
DK = 128
DV = 256
QK = HEADS * DK
GV = HEADS * DV
RANK = 16
GROUPS = 4
SBLOCK = 128
DG = 256
D_FF = 2816
FF_BLK = 704
EPS = 1e-6
Q_SCALE = DK ** -0.5
LANES = 128
VMEM_BIG = 48 * 1024 * 1024

D_IN = 7184
IN_BLK = 898
A_COLS = (0, 3072)
LOW_COLS = (3072, 3088)
S_COLS = (3088, 5136)
G_COLS = (5136, 7184)

ADAM_LR = 0.001
ADAM_B1 = 0.9
ADAM_B2 = 0.999
ADAM_EPS = 1e-08
ADAM_WD = 0.01
ADAM_STEP = 10

MESH = pl.DeviceIdType.MESH
ANY = pl.BlockSpec(memory_space=pl.ANY)
VMEM_SPEC = pl.BlockSpec(memory_space=pltpu.VMEM)


def _cp(sem=None, vmem=None):
    return pltpu.CompilerParams(dimension_semantics=sem, vmem_limit_bytes=vmem)


def _hbm(*arrays):
    return [pltpu.with_memory_space_constraint(a, pltpu.HBM) for a in arrays]


def _sig(x):
    return 0.5 * jnp.tanh(0.5 * x) + 0.5


GELU_C = 0.7978845608028654
GELU_A = 0.044715


def _gelu(x):
    t = jnp.tanh((GELU_C * x) * (1.0 + GELU_A * (x * x)))
    return (0.5 * x) * (1.0 + t)


def _gelu_and_grad(x):
    x2 = x * x
    t = jnp.tanh((GELU_C * x) * (1.0 + GELU_A * x2))
    one_t = 1.0 + t
    hx = 0.5 * x
    grad = 0.5 * one_t + (hx * (1.0 - t * t)) * (GELU_C + (3.0 * GELU_A * GELU_C) * x2)
    return hx * one_t, grad


def _logsig(x):
    return jnp.minimum(x, 0.0) - jnp.log1p(jnp.exp(-jnp.abs(x)))


def _dot(a, b, dims):
    return lax.dot_general(a, b, (dims, ((), ())), preferred_element_type=F32)


NN = ((1,), (0,))
NT = ((1,), (1,))
TN = ((0,), (0,))


def _exact_mask_dot(mask_bf16, x):
    hi = x.astype(BF16)
    r1 = x - hi.astype(F32)
    mid = r1.astype(BF16)
    lo = (r1 - mid.astype(F32)).astype(BF16)
    return _dot(mask_bf16, hi, NN) + _dot(mask_bf16, mid, NN) + _dot(mask_bf16, lo, NN)


def _pick_tile(dim, target):
    if dim <= target:
        return dim
    best = None
    for t in range(LANES, int(1.4 * target) + 1, LANES):
        if dim % t == 0:
            best = t
    assert best is not None, (dim, target)
    return best


def _mm_call(a, b, *, name, grid, a_spec, b_spec, o_spec, out_shape, dims, acc_shape):
    nk = grid[2]

    def body(a_ref, b_ref, o_ref, *acc):
        part = _dot(a_ref[...], b_ref[...], dims)
        if nk == 1:
            o_ref[...] = part.astype(o_ref.dtype)
        else:
            acc_ref = acc[0]
            k = pl.program_id(2)

            @pl.when(k == 0)
            def _():
                acc_ref[...] = part

            @pl.when(k > 0)
            def _():
                acc_ref[...] += part

            @pl.when(k == nk - 1)
            def _():
                o_ref[...] = acc_ref[...].astype(o_ref.dtype)

    return pl.pallas_call(
        body, name=name, grid=grid, in_specs=[a_spec, b_spec], out_specs=o_spec, out_shape=out_shape,
        scratch_shapes=[] if nk == 1 else [pltpu.VMEM(acc_shape, F32)],
        compiler_params=_cp(("parallel", "parallel", "arbitrary"), VMEM_BIG),
    )(a, b)


def _mm(a, b, mode, out_dtype, *, name, tm=512, tn=1024, tk=1024):
    if mode == "nn":
        (M, K), (_, N) = a.shape, b.shape
    elif mode == "nt":
        (M, K), (N, _) = a.shape, b.shape
    else:
        (K, M), (_, N) = a.shape, b.shape
    tm, tn, tk = _pick_tile(M, tm), _pick_tile(N, tn), _pick_tile(K, tk)
    if mode == "nn":
        a_spec = pl.BlockSpec((tm, tk), lambda i, j, k: (i, k))
        b_spec = pl.BlockSpec((tk, tn), lambda i, j, k: (k, j))
        dims = NN
    elif mode == "nt":
        a_spec = pl.BlockSpec((tm, tk), lambda i, j, k: (i, k))
        b_spec = pl.BlockSpec((tn, tk), lambda i, j, k: (j, k))
        dims = NT
    else:
        a_spec = pl.BlockSpec((tk, tm), lambda i, j, k: (k, i))
        b_spec = pl.BlockSpec((tk, tn), lambda i, j, k: (k, j))
        dims = TN
    return _mm_call(a, b, name=name, grid=(M // tm, N // tn, K // tk), a_spec=a_spec, b_spec=b_spec,
                    o_spec=pl.BlockSpec((tm, tn), lambda i, j, k: (i, j)),
                    out_shape=jax.ShapeDtypeStruct((M, N), out_dtype), dims=dims, acc_shape=(tm, tn))


ROW_TILE = 512
SUB_ROWS = 256


def _sub_tiles(T):
    return [pl.ds(r0, min(SUB_ROWS, T)) for r0 in range(0, T, SUB_ROWS)]


def _rt(T, W, c=0):
    return pl.BlockSpec((T, W), lambda i: (i, c))


def _rt3(nb, T, W):
    return pl.BlockSpec((nb, T, W), lambda i: (0, i, 0))


def _res(shape):
    nd = len(shape)
    return pl.BlockSpec(tuple(shape), lambda i: (0,) * nd, pipeline_mode=pl.Buffered(1))


def _acc(shape):
    nd = len(shape)
    return pl.BlockSpec(tuple(shape), lambda i: (0,) * nd, pipeline_mode=pl.Buffered(1))


def _nbytes(shape, dtype):
    n = jnp.dtype(dtype).itemsize
    for s in shape:
        n *= s
    return n


def _vmem_limit(tiles, resident, temps=16 * 1024 * 1024):
    need = 2 * sum(_nbytes(s, d) for s, d in tiles) + sum(_nbytes(s, d) for s, d in resident) + temps
    return min(need, 60 * 1024 * 1024)


def _tok_call(body, *, name, S, T, ins, outs, semantics="parallel"):
    tiles = [(spec.block_shape, a.dtype) for a, spec, kind in ins + outs if kind == "tile"]
    resident = [(a.shape, a.dtype) for a, spec, kind in ins + outs if kind == "res"]
    return pl.pallas_call(
        body, name=name, grid=(S // T,),
        in_specs=[spec for _, spec, _ in ins], out_specs=[spec for _, spec, _ in outs],
        out_shape=[pltpu.HBM(a.shape, a.dtype) for a, _, _ in outs],
        compiler_params=_cp((semantics,), _vmem_limit(tiles, resident)),
    )(*_hbm(*[a for a, _, _ in ins]))


def _tile(a, spec):
    return (a, spec, "tile")


def _whole(a):
    return (a, _res(a.shape), "res")


def _out_tile(shape, dtype, spec):
    return (jax.ShapeDtypeStruct(shape, dtype), spec, "tile")


def _out_acc(shape, dtype=F32):
    return (jax.ShapeDtypeStruct(shape, dtype), _acc(shape), "res")


def _rms_stats(x):
    r = lax.rsqrt(jnp.mean(x * x, axis=-1, keepdims=True) + EPS)
    return r, x * r


def _rms_bwd(xh, r, g, dy):
    dxh = dy * g
    dx = r * (dxh - xh * jnp.mean(dxh * xh, axis=-1, keepdims=True))
    dg = jnp.sum(dy * xh, axis=0, keepdims=True)
    return dx, dg


def _accum(ref, val):
    @pl.when(pl.program_id(0) == 0)
    def _():
        ref[...] = val

    @pl.when(pl.program_id(0) > 0)
    def _():
        ref[...] += val


def _dot_rows_t(a, w_ref, row0, o_ref, chunk=1024):
    n = o_ref.shape[1]
    for n0 in range(0, n, chunk):
        n1 = min(n, n0 + chunk)
        o_ref[:, n0:n1] = _dot(a, w_ref[row0 + n0:row0 + n1, :], NT).astype(o_ref.dtype)


def _in_proj_call(x, g1, w_blocks, *, name, T=ROW_TILE):
    S = x.shape[0]

    def body(x_ref, g_ref, blk_ref, a_ref, pa_ref, al_ref, ps_ref, pg_ref, w_ref):
        @pl.when(pl.program_id(0) == 0)
        def _():
            for d in range(N_DEV):
                w_ref[d * IN_BLK:(d + 1) * IN_BLK, :] = blk_ref[d]

        _, xh = _rms_stats(x_ref[...])
        a = (xh * g_ref[...]).astype(BF16)
        a_ref[...] = a
        _dot_rows_t(a, w_ref, A_COLS[0], pa_ref)
        _dot_rows_t(a, w_ref, LOW_COLS[0], al_ref)
        _dot_rows_t(a, w_ref, S_COLS[0], ps_ref)
        _dot_rows_t(a, w_ref, G_COLS[0], pg_ref)

    widths = (D_MODEL, A_COLS[1] - A_COLS[0], LANES, S_COLS[1] - S_COLS[0], G_COLS[1] - G_COLS[0])
    return _tok_call(
        body, name=name, S=S, T=T, semantics="arbitrary",
        ins=[_tile(x, _rt(T, D_MODEL)), _whole(g1), _whole(w_blocks)],
        outs=[_out_tile((S, w), BF16, _rt(T, w)) for w in widths] + [_out_acc((D_IN, D_MODEL), BF16)])


def _mixer_tail_call(y_gla, y_sgu, pg, x, w_bg, w_bs, w_out, g2, g3, *, name, T=ROW_TILE):
    S = x.shape[0]

    def body(yg_ref, ys_ref, pg_ref, x_ref, wbg_ref, wbs_ref, wo_ref, g2_ref, g3_ref,
             t1_ref, t2_ref, mg_ref, mix_ref, x1_ref, h_ref):
        for rows in _sub_tiles(T):
            t1 = _dot(yg_ref[rows, :], wbg_ref[...], NN)
            t2 = _dot(ys_ref[rows, :], wbs_ref[...], NN)
            t1_ref[rows, :] = t1.astype(BF16)
            t2_ref[rows, :] = t2.astype(BF16)
            sg = _sig(pg_ref[rows, pl.ds(0, D_MODEL)].astype(F32))
            ss = _sig(pg_ref[rows, pl.ds(D_MODEL, D_MODEL)].astype(F32))
            merged = (sg * t1 + ss * t2).astype(BF16)
            mg_ref[rows, :] = merged
            mix = _dot(merged, wo_ref[...], NN)
            mix_ref[rows, :] = mix
            _, mh = _rms_stats(mix)
            x1 = x_ref[rows, :] + mh * g2_ref[...]
            x1_ref[rows, :] = x1
            _, xh = _rms_stats(x1)
            h_ref[rows, :] = (xh * g3_ref[...]).astype(BF16)

    row = _rt(T, D_MODEL)
    b16 = lambda: _out_tile((S, D_MODEL), BF16, row)
    f32 = lambda: _out_tile((S, D_MODEL), F32, row)
    return _tok_call(
        body, name=name, S=S, T=T,
        ins=[_tile(y_gla, row), _tile(y_sgu, row), _tile(pg, _rt(T, 2 * D_MODEL)), _tile(x, row),
             _whole(w_bg), _whole(w_bs), _whole(w_out), _whole(g2), _whole(g3)],
        outs=[b16(), b16(), b16(), f32(), f32(), b16()])


def _ffn_in_call(h, wfi, *, name, T=ROW_TILE):
    S = h.shape[0]

    def body(h_ref, w_ref, gu_ref, z_ref):
        hv = h_ref[...]
        for d in range(4):
            gate = _dot(hv, w_ref[d], NT)
            up = _dot(hv, w_ref[d + 4], NT)
            gu_ref[d] = gate.astype(BF16)
            gu_ref[d + 4] = up.astype(BF16)
            z_ref[d] = (gate * _sig(gate) * up).astype(BF16)

    return _tok_call(
        body, name=name, S=S, T=T,
        ins=[_tile(h, _rt(T, D_MODEL)), _whole(wfi)],
        outs=[_out_tile((N_DEV, S, FF_BLK), BF16, _rt3(N_DEV, T, FF_BLK)),
              _out_tile((4, S, FF_BLK), BF16, _rt3(4, T, FF_BLK))])


def _ffn_out_loss_call(z, wfo, x1, target, g4, *, name, T=ROW_TILE):
    S = x1.shape[0]

    def body(z_ref, w_ref, x1_ref, t_ref, g4_ref, loss_ref, dx2_ref, dy_ref, dg4_ref):
        loss = jnp.zeros((1, 1), F32)
        dg4 = jnp.zeros((1, D_MODEL), F32)
        for rows in _sub_tiles(T):
            y = _dot(z_ref[0, rows, :], w_ref[0], NN)
            for d in range(1, 4):
                y = y + _dot(z_ref[d, rows, :], w_ref[d], NN)
            r, yh = _rms_stats(y)
            diff = x1_ref[rows, :] + yh * g4_ref[...] - t_ref[rows, :]
            loss = loss + 0.5 * jnp.sum(jnp.mean(diff * diff, axis=-1, keepdims=True), axis=0, keepdims=True)
            dx2 = diff * (1.0 / D_MODEL)
            dx2_ref[rows, :] = dx2
            dy, dg = _rms_bwd(yh, r, g4_ref[...], dx2)
            dy_ref[rows, :] = dy.astype(BF16)
            dg4 = dg4 + dg
        _accum(loss_ref, jnp.broadcast_to(loss, (1, LANES)))
        _accum(dg4_ref, dg4)

    row = _rt(T, D_MODEL)
    return _tok_call(
        body, name=name, S=S, T=T, semantics="arbitrary",
        ins=[_tile(z, _rt3(4, T, FF_BLK)), _whole(wfo), _tile(x1, row), _tile(target, row), _whole(g4)],
        outs=[_out_acc((1, LANES)), _out_tile((S, D_MODEL), F32, row), _out_tile((S, D_MODEL), BF16, row),
              _out_acc((1, D_MODEL))])


def _ffn_out_bwd_call(dy, wfo, gu, *, name, T=ROW_TILE):
    S = dy.shape[0]

    def body(dy_ref, w_ref, gu_ref, dgu_ref):
        dyv = dy_ref[...]
        for d in range(4):
            dz = _dot(dyv, w_ref[d], NT).astype(BF16)
            gt = gu_ref[d]
            up = gu_ref[d + 4]
            s = _sig(gt)
            dgu_ref[d] = dz * up * s * (1.0 + gt * (1.0 - s))
            dgu_ref[d + 4] = dz * gt * s

    blocks = _rt3(N_DEV, T, FF_BLK)
    return _tok_call(
        body, name=name, S=S, T=T,
        ins=[_tile(dy, _rt(T, D_MODEL)), _whole(wfo), _tile(gu, blocks)],
        outs=[_out_tile((N_DEV, S, FF_BLK), BF16, blocks)])[0]


def _ffn_in_bwd_call(dgu, wfi, dx2, x1, mix, g3, g2, *, name, T=ROW_TILE):
    S = x1.shape[0]

    def body(dgu_ref, w_ref, dx2_ref, x1_ref, mix_ref, g3_ref, g2_ref, dx1_ref, dmix_ref, dg3_ref, dg2_ref):
        dg3 = jnp.zeros((1, D_MODEL), F32)
        dg2 = jnp.zeros((1, D_MODEL), F32)
        for rows in _sub_tiles(T):
            dh = _dot(dgu_ref[0, rows, :], w_ref[0], NN)
            for d in range(1, N_DEV):
                dh = dh + _dot(dgu_ref[d, rows, :], w_ref[d], NN)
            r3, xh = _rms_stats(x1_ref[rows, :])
            d3, g = _rms_bwd(xh, r3, g3_ref[...], dh)
            dg3 = dg3 + g
            dx1 = dx2_ref[rows, :] + d3
            dx1_ref[rows, :] = dx1
            r2, mh = _rms_stats(mix_ref[rows, :])
            dmix, g = _rms_bwd(mh, r2, g2_ref[...], dx1)
            dg2 = dg2 + g
            dmix_ref[rows, :] = dmix.astype(BF16)
        _accum(dg3_ref, dg3)
        _accum(dg2_ref, dg2)

    row = _rt(T, D_MODEL)
    return _tok_call(
        body, name=name, S=S, T=T, semantics="arbitrary",
        ins=[_tile(dgu, _rt3(N_DEV, T, FF_BLK)), _whole(wfi), _tile(dx2, row), _tile(x1, row), _tile(mix, row),
             _whole(g3), _whole(g2)],
        outs=[_out_tile((S, D_MODEL), F32, row), _out_tile((S, D_MODEL), BF16, row),
              _out_acc((1, D_MODEL)), _out_acc((1, D_MODEL))])


def _mixer_bwd_call(dmix, t1, t2, pg, w_out, w_bg, w_bs, *, name, T=ROW_TILE):
    S = dmix.shape[0]

    def body(dmix_ref, t1_ref, t2_ref, pg_ref, wo_ref, wbg_ref, wbs_ref, dt1_ref, dt2_ref, dpg_ref, dyg_ref, dys_ref):
        gg, gs = pl.ds(0, D_MODEL), pl.ds(D_MODEL, D_MODEL)
        for rows in _sub_tiles(T):
            dm = _dot(dmix_ref[rows, :], wo_ref[...], NT)
            sg = _sig(pg_ref[rows, gg].astype(F32))
            ss = _sig(pg_ref[rows, gs].astype(F32))
            dt1 = (dm * sg).astype(BF16)
            dt2 = (dm * ss).astype(BF16)
            dt1_ref[rows, :] = dt1
            dt2_ref[rows, :] = dt2
            dpg_ref[rows, gg] = (dm * t1_ref[rows, :].astype(F32) * sg * (1.0 - sg)).astype(BF16)
            dpg_ref[rows, gs] = (dm * t2_ref[rows, :].astype(F32) * ss * (1.0 - ss)).astype(BF16)
            dyg_ref[rows, :] = _dot(dt1, wbg_ref[...], NT).astype(BF16)
            dys_ref[rows, :] = _dot(dt2, wbs_ref[...], NT).astype(BF16)

    row = _rt(T, D_MODEL)
    wide = _rt(T, 2 * D_MODEL)
    b16 = lambda: _out_tile((S, D_MODEL), BF16, row)
    return _tok_call(
        body, name=name, S=S, T=T,
        ins=[_tile(dmix, row), _tile(t1, row), _tile(t2, row), _tile(pg, wide), _whole(w_out), _whole(w_bg), _whole(w_bs)],
        outs=[b16(), b16(), _out_tile((S, 2 * D_MODEL), BF16, wide), b16(), b16()])


def _in_proj_bwd_call(dpa, dlow, dps, dpg, w_in_t, x, dx1, g1, *, name, T=ROW_TILE):
    S = x.shape[0]

    def body(dpa_ref, dl_ref, dps_ref, dpg_ref, w_ref, x_ref, dx1_ref, g1_ref, gx_ref, dg1_ref):
        da = (_dot(dpa_ref[...], w_ref[A_COLS[0]:A_COLS[1], :], NN)
              + _dot(dl_ref[...], w_ref[LOW_COLS[0]:LOW_COLS[0] + LANES, :], NN)
              + _dot(dps_ref[...], w_ref[S_COLS[0]:S_COLS[1], :], NN)
              + _dot(dpg_ref[...], w_ref[G_COLS[0]:G_COLS[1], :], NN))
        r, xh = _rms_stats(x_ref[...])
        dxa, dg = _rms_bwd(xh, r, g1_ref[...], da)
        gx_ref[...] = dx1_ref[...] + dxa
        _accum(dg1_ref, dg)

    row = _rt(T, D_MODEL)
    return _tok_call(
        body, name=name, S=S, T=T, semantics="arbitrary",
        ins=[_tile(dpa, _rt(T, dpa.shape[1])), _tile(dlow, _rt(T, dlow.shape[1])), _tile(dps, _rt(T, dps.shape[1])),
             _tile(dpg, _rt(T, dpg.shape[1])), _whole(w_in_t), _tile(x, row), _tile(dx1, row), _whole(g1)],
        outs=[_out_tile((S, D_MODEL), F32, row), _out_acc((1, D_MODEL))])


TOKEN_TILE = 512


def _weight_grads_part(arrays, pairs, *, tk=TOKEN_TILE, out_dtype=BF16):
    S = arrays[0].shape[-2]
    tk = min(tk, S)
    n_in = len(arrays)

    def out_shape(i, j):
        a, b = arrays[i], arrays[j]
        if a.ndim == 3:
            return (a.shape[0], a.shape[2], b.shape[1])
        if b.ndim == 3:
            return (b.shape[0], a.shape[1], b.shape[2])
        return (a.shape[1], b.shape[1])

    shapes = [out_shape(i, j) for i, j in pairs]

    in_place = out_dtype == F32

    def body(ins, outs, accs):
        k = pl.program_id(0)
        if in_place:
            accs = outs

        @pl.when(k == 0)
        def _():
            for acc in accs:
                acc[...] = jnp.zeros_like(acc)

        for (i, j), acc in zip(pairs, accs):
            a_ref, b_ref = ins[i], ins[j]
            if len(a_ref.shape) == 3:
                for n in range(a_ref.shape[0]):
                    acc[n] += _dot(a_ref[n], b_ref[...], TN)
            elif len(b_ref.shape) == 3:
                a = a_ref[...]
                for n in range(b_ref.shape[0]):
                    acc[n] += _dot(a, b_ref[n], TN)
            else:
                b = b_ref[...]
                for m0 in range(0, a_ref.shape[1], 1024):
                    m1 = min(a_ref.shape[1], m0 + 1024)
                    acc[m0:m1, :] += _dot(a_ref[:, m0:m1], b, TN)

        if not in_place:
            @pl.when(k == S // tk - 1)
            def _():
                for out, acc in zip(outs, accs):
                    out[...] = acc[...].astype(out.dtype)

    def in_spec(a):
        if a.ndim == 3:
            return pl.BlockSpec((a.shape[0], tk, a.shape[2]), lambda k: (0, k, 0))
        return pl.BlockSpec((tk, a.shape[1]), lambda k: (k, 0))

    return dict(
        body=body, steps=S // tk, arrays=list(arrays),
        in_specs=[in_spec(a) for a in arrays],
        out_specs=[_acc(s) for s in shapes],
        out_shapes=[pltpu.HBM(s, out_dtype) for s in shapes],
        scratch=[] if in_place else [pltpu.VMEM(s, F32) for s in shapes],
        tiles=[(in_spec(a).block_shape, a.dtype) for a in arrays],
        resident=[(s, F32) for s in shapes] + ([] if in_place else [(s, BF16) for s in shapes]))


def _weight_grads_call(arrays, pairs, *, name, tk=TOKEN_TILE):
    part = _weight_grads_part(arrays, pairs, tk=tk)
    n_in, n_out = len(part["arrays"]), len(part["out_shapes"])

    def body(*refs):
        part["body"](refs[:n_in], refs[n_in:n_in + n_out], refs[n_in + n_out:])

    return pl.pallas_call(
        body, name=name, grid=(part["steps"],),
        in_specs=part["in_specs"], out_specs=part["out_specs"], out_shape=part["out_shapes"],
        scratch_shapes=part["scratch"],
        compiler_params=_cp(("arbitrary",), _vmem_limit(part["tiles"], part["resident"])),
    )(*_hbm(*part["arrays"]))


def _ffn_in_grad_call(h, dgu, *, name, tk=TOKEN_TILE):
    S = h.shape[0]
    tk = min(tk, S)
    nb = 4
    shape = (nb, FF_BLK, D_MODEL)

    def body(h_ref, dgu_ref, out_ref, acc):
        k = pl.program_id(1)

        @pl.when(k == 0)
        def _():
            acc[...] = jnp.zeros_like(acc)

        hv = h_ref[...]
        for n in range(nb):
            acc[n] += _dot(dgu_ref[n], hv, TN)

        @pl.when(k == S // tk - 1)
        def _():
            out_ref[...] = acc[...].astype(out_ref.dtype)

    tiles = [((tk, D_MODEL), BF16), ((nb, tk, FF_BLK), BF16), (shape, BF16)]
    return pl.pallas_call(
        body, name=name, grid=(N_DEV // nb, S // tk),
        in_specs=[pl.BlockSpec((tk, D_MODEL), lambda g, k: (k, 0)),
                  pl.BlockSpec((nb, tk, FF_BLK), lambda g, k: (g, k, 0))],
        out_specs=pl.BlockSpec(shape, lambda g, k: (g, 0, 0)),
        out_shape=pltpu.HBM((N_DEV, FF_BLK, D_MODEL), BF16),
        scratch_shapes=[pltpu.VMEM(shape, F32)],
        compiler_params=_cp(("parallel", "arbitrary"), _vmem_limit(tiles, [(shape, F32)])),
    )(*_hbm(h, dgu))


GLA_TILE = 256
Q_OFF, K_OFF, V_OFF, R_OFF = 0, QK, 2 * QK, 2 * QK + GV


def _tri(lower):
    r = lax.broadcasted_iota(jnp.int32, (CHUNK, CHUNK), 0)
    c = lax.broadcasted_iota(jnp.int32, (CHUNK, CHUNK), 1)
    return jnp.where((r >= c) if lower else (c >= r), 1.0, 0.0).astype(BF16)


def _gla_fwd_call(pa, alow, wg, bgate, gnorm, *, name):
    S = pa.shape[0]
    Tg = min(GLA_TILE, S)
    cb = Tg // CHUNK

    def body(pa_ref, al_ref, wg_ref, bg_ref, gn_ref, y_ref, st_ref, state):
        @pl.when(pl.program_id(0) == 0)
        def _():
            state[...] = jnp.zeros_like(state)

        logit = _dot(al_ref[...], wg_ref[...], NN) + bg_ref[...]
        ls = _logsig(logit) * (1.0 / 16.0)
        tri = _tri(True)
        for c in range(cb):
            rows = pl.ds(c * CHUNK, CHUNK)
            cum = _exact_mask_dot(tri, ls[c * CHUNK:(c + 1) * CHUNK])
            tot = cum[CHUNK - 1:CHUNK]
            kd = (pa_ref[rows, pl.ds(K_OFF, QK)].astype(F32) * jnp.exp(tot - cum)).astype(BF16)
            decay = jnp.exp(tot)
            for h in range(HEADS):
                lanes = slice(h * DK, (h + 1) * DK)
                new = state[h] * decay[:, lanes] + _dot(pa_ref[rows, pl.ds(V_OFF + h * DV, DV)], kd[:, lanes], TN)
                state[h] = new
                st_ref[c, h] = new
        for c in range(cb):
            rows = pl.ds(c * CHUNK, CHUNK)
            for h in range(HEADS):
                qs = (pa_ref[rows, pl.ds(Q_OFF + h * DK, DK)].astype(F32) * Q_SCALE).astype(BF16)
                o = _dot(qs, st_ref[c, h].astype(BF16), NT)
                rs = lax.rsqrt(jnp.mean(o * o, axis=-1, keepdims=True) + EPS)
                rr = pa_ref[rows, pl.ds(R_OFF + h * DV, DV)].astype(F32)
                y_ref[rows, pl.ds(h * DV, DV)] = (o * rs * gn_ref[h] * (rr * _sig(rr))).astype(BF16)

    return pl.pallas_call(
        body, name=name, grid=(S // Tg,),
        in_specs=[
            pl.BlockSpec((Tg, 2 * QK + 2 * GV), lambda t: (t, 0)),
            pl.BlockSpec((Tg, LANES), lambda t: (t, 0)),
            pl.BlockSpec((LANES, QK), lambda t: (0, 0)),
            pl.BlockSpec((1, QK), lambda t: (0, 0)),
            pl.BlockSpec((HEADS, 1, DV), lambda t: (0, 0, 0)),
        ],
        out_specs=[
            pl.BlockSpec((Tg, GV), lambda t: (t, 0)),
            pl.BlockSpec((cb, HEADS, DV, DK), lambda t: (t, 0, 0, 0)),
        ],
        out_shape=[pltpu.HBM((S, GV), BF16), pltpu.HBM((S // CHUNK, HEADS, DV, DK), F32)],
        scratch_shapes=[pltpu.VMEM((HEADS, DV, DK), F32)],
        compiler_params=_cp(("arbitrary",), VMEM_BIG),
    )(*_hbm(pa, alow, wg, bgate, gnorm))


def _gla_bwd_call(pa, alow, wg, bgate, gnorm, states, dy, *, name):
    S = pa.shape[0]
    Tg = min(GLA_TILE, S)
    cb = Tg // CHUNK
    nt = S // Tg

    def rev(t):
        return nt - 1 - t

    def body(pa_ref, al_ref, wg_ref, bg_ref, gn_ref, st_ref, prev_ref, dy_ref,
             dpa_ref, dl_ref, dgn_ref, dbg_ref, carry, pbuf):
        t = pl.program_id(0)

        @pl.when(t == 0)
        def _():
            carry[...] = jnp.zeros_like(carry)
            dgn_ref[...] = jnp.zeros_like(dgn_ref)
            dbg_ref[...] = jnp.zeros_like(dbg_ref)

        logit = _dot(al_ref[...], wg_ref[...], NN) + bg_ref[...]
        ls = _logsig(logit) * (1.0 / 16.0)
        sneg = 1.0 / (1.0 + jnp.exp(logit))
        tri = _tri(True)
        upper = _tri(False)
        first_tile = rev(t) == 0
        heads = range(HEADS)

        w, decay, kd = [], [], []
        for c in range(cb):
            rows = pl.ds(c * CHUNK, CHUNK)
            cum = _exact_mask_dot(tri, ls[c * CHUNK:(c + 1) * CHUNK])
            tot = cum[CHUNK - 1:CHUNK]
            w.append(jnp.exp(tot - cum))
            decay.append(jnp.exp(tot))
            kd.append(pa_ref[rows, pl.ds(K_OFF, QK)].astype(F32) * w[c])

        dgn = [jnp.zeros((1, DV), F32) for _ in heads]
        for c in range(cb):
            rows = pl.ds(c * CHUNK, CHUNK)
            for h in heads:
                gn = gn_ref[h]
                qs = (pa_ref[rows, pl.ds(Q_OFF + h * DK, DK)].astype(F32) * Q_SCALE).astype(BF16)
                st16 = st_ref[c, h].astype(BF16)
                o = _dot(qs, st16, NT)
                rs = lax.rsqrt(jnp.mean(o * o, axis=-1, keepdims=True) + EPS)
                oh = o * rs
                rr = pa_ref[rows, pl.ds(R_OFF + h * DV, DV)].astype(F32)
                sr = _sig(rr)
                dyv = dy_ref[rows, pl.ds(h * DV, DV)].astype(F32)
                dpa_ref[rows, pl.ds(R_OFF + h * DV, DV)] = (
                    dyv * oh * gn * sr * (1.0 + rr * (1.0 - sr))).astype(BF16)
                don = dyv * (rr * sr)
                dgn[h] = dgn[h] + jnp.sum(don * oh, axis=0, keepdims=True)
                doh = don * gn
                do16 = (rs * (doh - oh * jnp.mean(doh * oh, axis=-1, keepdims=True))).astype(BF16)
                dpa_ref[rows, pl.ds(Q_OFF + h * DK, DK)] = (_dot(do16, st16, NN) * Q_SCALE).astype(BF16)
                pbuf[c, h] = _dot(do16, qs, TN)
        for h in heads:
            dgn_ref[h] += dgn[h]

        dkd = [[None] * HEADS for _ in range(cb)]
        ddecay = [[None] * HEADS for _ in range(cb)]
        for c in reversed(range(cb)):
            rows = pl.ds(c * CHUNK, CHUNK)
            for h in heads:
                lanes = slice(h * DK, (h + 1) * DK)
                gt = pbuf[c, h] + carry[h]
                gt16 = gt.astype(BF16)
                dkd[c][h] = _dot(pa_ref[rows, pl.ds(V_OFF + h * DV, DV)], gt16, NN)
                dpa_ref[rows, pl.ds(V_OFF + h * DV, DV)] = _dot(kd[c][:, lanes].astype(BF16), gt16, NT).astype(BF16)
                if c > 0:
                    st_prev = st_ref[c - 1, h]
                else:
                    st_prev = jnp.where(first_tile, 0.0, prev_ref[0, h])
                ddecay[c][h] = jnp.sum(gt * st_prev, axis=0, keepdims=True)
                carry[h] = gt * decay[c][:, lanes]

        dbg = jnp.zeros((1, QK), F32)
        for c in range(cb):
            rows = pl.ds(c * CHUNK, CHUNK)
            dkd_c = jnp.concatenate(dkd[c], axis=1)
            dpa_ref[rows, pl.ds(K_OFF, QK)] = (dkd_c * w[c]).astype(BF16)
            e = dkd_c * kd[c]
            dtot = jnp.sum(e, axis=0, keepdims=True) + jnp.concatenate(ddecay[c], axis=1) * decay[c]
            dls = dtot - _exact_mask_dot(upper, e)
            dlogit = dls * (1.0 / 16.0) * sneg[c * CHUNK:(c + 1) * CHUNK]
            dl_ref[rows, :] = dlogit.astype(BF16)
            dbg = dbg + jnp.sum(dlogit, axis=0, keepdims=True)
        dbg_ref[...] += dbg

    wide = 2 * QK + 2 * GV
    tiles = [((Tg, wide), BF16), ((cb + 1, HEADS, DV, DK), F32), ((Tg, GV), BF16), ((Tg, wide), BF16),
             ((Tg, QK), BF16)]
    resident = [((cb + 1, HEADS, DV, DK), F32)]
    return pl.pallas_call(
        body, name=name, grid=(nt,),
        in_specs=[
            pl.BlockSpec((Tg, wide), lambda t: (rev(t), 0)),
            pl.BlockSpec((Tg, LANES), lambda t: (rev(t), 0)),
            pl.BlockSpec((LANES, QK), lambda t: (0, 0)),
            pl.BlockSpec((1, QK), lambda t: (0, 0)),
            pl.BlockSpec((HEADS, 1, DV), lambda t: (0, 0, 0)),
            pl.BlockSpec((cb, HEADS, DV, DK), lambda t: (rev(t), 0, 0, 0)),
            pl.BlockSpec((1, HEADS, DV, DK), lambda t: (jnp.maximum(rev(t) * cb - 1, 0), 0, 0, 0)),
            pl.BlockSpec((Tg, GV), lambda t: (rev(t), 0)),
        ],
        out_specs=[
            pl.BlockSpec((Tg, wide), lambda t: (rev(t), 0)),
            pl.BlockSpec((Tg, QK), lambda t: (rev(t), 0)),
            pl.BlockSpec((HEADS, 1, DV), lambda t: (0, 0, 0)),
            pl.BlockSpec((1, QK), lambda t: (0, 0)),
        ],
        out_shape=[pltpu.HBM((S, wide), BF16), pltpu.HBM((S, QK), BF16),
                   jax.ShapeDtypeStruct((HEADS, 1, DV), F32), jax.ShapeDtypeStruct((1, QK), F32)],
        scratch_shapes=[pltpu.VMEM((HEADS, DV, DK), F32), pltpu.VMEM((cb, HEADS, DV, DK), F32)],
        compiler_params=_cp(("arbitrary",), _vmem_limit(tiles, resident)),
    )(*_hbm(pa, alow, wg, bgate, gnorm, states, states, dy))


SGU_TILE = 256


def _sgu_mask():
    r = lax.broadcasted_iota(jnp.int32, (SBLOCK, SBLOCK), 0)
    c = lax.broadcasted_iota(jnp.int32, (SBLOCK, SBLOCK), 1)
    return (c < CHUNK) | (r >= CHUNK)


def _ln_stats(vf):
    mu = jnp.mean(vf, axis=-1, keepdims=True)
    xc = vf - mu
    rs = lax.rsqrt(jnp.mean(xc * xc, axis=-1, keepdims=True) + EPS)
    return rs, xc * rs


def _sgu_fwd_call(ps, ln_g, ln_b, w_sp, b_sp, *, name):
    S = ps.shape[0]
    Ts = min(SGU_TILE, S)

    def body(ps_ref, lg_ref, lb_ref, w_ref, b_ref, y_ref):
        mask = _sgu_mask()
        for g in range(GROUPS):
            wm = jnp.where(mask, w_ref[g], 0.0).astype(BF16)
            for p in range(Ts // SBLOCK):
                rows = pl.ds(p * SBLOCK, SBLOCK)
                u = _gelu(ps_ref[rows, pl.ds(g * DG, DG)].astype(F32))
                _, xh = _ln_stats(_gelu(ps_ref[rows, pl.ds(D_MODEL + g * DG, DG)].astype(F32)))
                vn = xh * lg_ref[g] + lb_ref[g]
                mixed = _dot(wm, vn.astype(BF16), NN) + b_ref[g]
                y_ref[rows, pl.ds(g * DG, DG)] = (u * mixed).astype(BF16)

    full3 = lambda a, b, c: pl.BlockSpec((a, b, c), lambda t: (0, 0, 0))
    return pl.pallas_call(
        body, name=name, grid=(S // Ts,),
        in_specs=[pl.BlockSpec((Ts, 2 * D_MODEL), lambda t: (t, 0)),
                  full3(GROUPS, 1, DG), full3(GROUPS, 1, DG), full3(GROUPS, SBLOCK, SBLOCK), full3(GROUPS, SBLOCK, 1)],
        out_specs=pl.BlockSpec((Ts, D_MODEL), lambda t: (t, 0)),
        out_shape=pltpu.HBM((S, D_MODEL), BF16),
        compiler_params=_cp(("parallel",)),
    )(*_hbm(ps, ln_g, ln_b, w_sp, b_sp))


def _sgu_bwd_call(ps, ln_g, ln_b, w_sp, b_sp, dy, *, name):
    S = ps.shape[0]
    Ts = min(SGU_TILE, S)

    def body(ps_ref, lg_ref, lb_ref, w_ref, b_ref, dy_ref, ds_ref, dlg_ref, dlb_ref, dw_ref, db_ref):
        @pl.when(pl.program_id(0) == 0)
        def _():
            dlg_ref[...] = jnp.zeros_like(dlg_ref)
            dlb_ref[...] = jnp.zeros_like(dlb_ref)
            dw_ref[...] = jnp.zeros_like(dw_ref)
            db_ref[...] = jnp.zeros_like(db_ref)

        mask = _sgu_mask()
        for g in range(GROUPS):
            wm = jnp.where(mask, w_ref[g], 0.0).astype(BF16)
            lg = lg_ref[g]
            for p in range(Ts // SBLOCK):
                rows = pl.ds(p * SBLOCK, SBLOCK)
                su = ps_ref[rows, pl.ds(g * DG, DG)].astype(F32)
                sv = ps_ref[rows, pl.ds(D_MODEL + g * DG, DG)].astype(F32)
                u, du = _gelu_and_grad(su)
                gv, dgv = _gelu_and_grad(sv)
                rs, xh = _ln_stats(gv)
                vn16 = (xh * lg + lb_ref[g]).astype(BF16)
                mixed = _dot(wm, vn16, NN) + b_ref[g]
                dyv = dy_ref[rows, pl.ds(g * DG, DG)].astype(F32)
                ds_ref[rows, pl.ds(g * DG, DG)] = (dyv * mixed * du).astype(BF16)
                dmix = dyv * u
                dmix16 = dmix.astype(BF16)
                db_ref[g] += jnp.sum(dmix, axis=-1, keepdims=True)
                dw_ref[g] += jnp.where(mask, _dot(dmix16, vn16, NT), 0.0)
                dvn = _dot(wm, dmix16, TN)
                dlg_ref[g] += jnp.sum(dvn * xh, axis=0, keepdims=True)
                dlb_ref[g] += jnp.sum(dvn, axis=0, keepdims=True)
                dxh = dvn * lg
                dvf = rs * (dxh - jnp.mean(dxh, axis=-1, keepdims=True)
                            - xh * jnp.mean(dxh * xh, axis=-1, keepdims=True))
                ds_ref[rows, pl.ds(D_MODEL + g * DG, DG)] = (dvf * dgv).astype(BF16)

    full3 = lambda a, b, c: pl.BlockSpec((a, b, c), lambda t: (0, 0, 0))
    tiles = [((Ts, 2 * D_MODEL), BF16), ((Ts, D_MODEL), BF16), ((Ts, 2 * D_MODEL), BF16)]
    return pl.pallas_call(
        body, name=name, grid=(S // Ts,),
        in_specs=[pl.BlockSpec((Ts, 2 * D_MODEL), lambda t: (t, 0)),
                  full3(GROUPS, 1, DG), full3(GROUPS, 1, DG), full3(GROUPS, SBLOCK, SBLOCK), full3(GROUPS, SBLOCK, 1),
                  pl.BlockSpec((Ts, D_MODEL), lambda t: (t, 0))],
        out_specs=[pl.BlockSpec((Ts, 2 * D_MODEL), lambda t: (t, 0)),
                   full3(GROUPS, 1, DG), full3(GROUPS, 1, DG), full3(GROUPS, SBLOCK, SBLOCK), full3(GROUPS, SBLOCK, 1)],
        out_shape=[pltpu.HBM((S, 2 * D_MODEL), BF16),
                   jax.ShapeDtypeStruct((GROUPS, 1, DG), F32), jax.ShapeDtypeStruct((GROUPS, 1, DG), F32),
                   jax.ShapeDtypeStruct((GROUPS, SBLOCK, SBLOCK), F32),
                   jax.ShapeDtypeStruct((GROUPS, SBLOCK, 1), F32)],
        compiler_params=_cp(("arbitrary",), _vmem_limit(tiles, [])),
    )(*_hbm(ps, ln_g, ln_b, w_sp, b_sp, dy))


def _position():
    return lax.axis_index("x"), lax.axis_index("y"), lax.axis_index("c")


def _gather_copies(srcs, dsts, send_sems, recv_sems, local_sems):
    x, y, c = _position()
    me, sibling = (x, y, c), (x, y, 1 - c)
    chips = [(1 - x, y), (x, 1 - y), (1 - x, 1 - y)]
    n = len(srcs)

    def slab(a, block):
        px, py, pc = block
        return dsts[a].at[4 * px + 2 * py + pc]

    def copy(a, k, block, to, src=None):
        return pltpu.make_async_remote_copy(
            src_ref=slab(a, block) if src is None else src, dst_ref=slab(a, block),
            send_sem=send_sems.at[7 * a + k], recv_sem=recv_sems.at[7 * a + k], device_id=to, device_id_type=MESH)

    mine = [pltpu.make_async_copy(srcs[a], slab(a, me), local_sems.at[a]) for a in range(n)]
    for cp in mine:
        cp.start()
    first = []
    for a in range(n):
        first.append(copy(a, 0, me, sibling, src=srcs[a]))
        first += [copy(a, 1 + j, me, (*chip, c), src=srcs[a]) for j, chip in enumerate(chips)]
    for cp in first:
        cp.start()
    passed = []
    for j, chip in enumerate(chips):
        for a in range(n):
            copy(a, 1 + j, (*chip, c), me).wait_recv()
            cp = copy(a, 4 + j, (*chip, c), sibling)
            cp.start()
            passed.append(cp)
    for a in range(n):
        copy(a, 0, sibling, me).wait_recv()
        for j, chip in enumerate(chips):
            copy(a, 4 + j, (*chip, 1 - c), me).wait_recv()
    for cp in first + passed:
        cp.wait_send()
    for cp in mine:
        cp.wait()


def _gather_copies_relayed(srcs, dsts, send_sems, recv_sems, local_sems, waves=1):
    x, y, c = _position()
    me, sibling = (x, y, c), (x, y, 1 - c)
    x_chip, y_chip, d_chip = (1 - x, y), (x, 1 - y), (1 - x, 1 - y)
    south = c == 0
    relay_from = (jnp.where(south, x, 1 - x), jnp.where(south, 1 - y, y), c)
    relay_to = (jnp.where(south, 1 - x, x), jnp.where(south, y, 1 - y), c)
    n = len(srcs)

    def cut(a, w):
        rows = srcs[a].shape[0]
        step = (rows // waves) // 16 * 16
        if waves == 1 or step == 0:
            return pl.ds(0, rows) if w == 0 else None
        return pl.ds(w * step, step if w < waves - 1 else rows - w * step)

    def slab(a, block, w):
        px, py, pc = block
        return dsts[a].at[4 * px + 2 * py + pc, cut(a, w)]

    def copy(a, k, w, block, to, own=False):
        sem = (7 * a + k) * waves + w
        return pltpu.make_async_remote_copy(
            src_ref=srcs[a].at[cut(a, w)] if own else slab(a, block, w), dst_ref=slab(a, block, w),
            send_sem=send_sems.at[sem], recv_sem=recv_sems.at[sem], device_id=to, device_id_type=MESH)

    pieces = [(a, w) for w in range(waves) for a in range(n) if cut(a, w) is not None]
    mine = [pltpu.make_async_copy(srcs[a], dsts[a].at[4 * x + 2 * y + c], local_sems.at[a]) for a in range(n)]
    for cp in mine:
        cp.start()
    sent = []
    for a, w in pieces:
        sent += [copy(a, 0, w, me, sibling, own=True), copy(a, 1, w, me, (*x_chip, c), own=True),
                 copy(a, 2, w, me, (*y_chip, c), own=True)]
    for cp in sent:
        cp.start()
    for a, w in pieces:
        copy(a, 1, w, (*x_chip, c), me).wait_recv()
        copy(a, 2, w, (*y_chip, c), me).wait_recv()
        later = [copy(a, 3, w, relay_from, relay_to), copy(a, 4, w, (*x_chip, c), sibling),
                 copy(a, 5, w, (*y_chip, c), sibling)]
        for cp in later:
            cp.start()
        sent += later
    for a, w in pieces:
        copy(a, 3, w, (*d_chip, c), me).wait_recv()
        cp = copy(a, 6, w, (*d_chip, c), sibling)
        cp.start()
        sent.append(cp)
    for a, w in pieces:
        copy(a, 0, w, sibling, me).wait_recv()
        for k, chip in ((4, x_chip), (5, y_chip), (6, d_chip)):
            copy(a, k, w, (*chip, 1 - c), me).wait_recv()
    for cp in sent:
        cp.wait_send()
    for cp in mine:
        cp.wait()


def _all_gather_hbm(shards, *, name, waves=1):
    n = len(shards)

    def body(*refs):
        srcs, dsts = refs[:n], refs[n:2 * n]
        send_sems, recv_sems, local_sems = refs[2 * n:]
        _gather_copies_relayed(srcs, dsts, send_sems, recv_sems, local_sems, waves=waves)

    return pl.pallas_call(
        body, name=name,
        in_specs=[ANY] * n, out_specs=[ANY] * n,
        out_shape=[jax.ShapeDtypeStruct((N_DEV, *s.shape), s.dtype) for s in shards],
        scratch_shapes=_comm_sems(n, waves),
    )(*shards)


FLIPS = [(fx, fy, fc) for fx in (0, 1) for fy in (0, 1) for fc in (0, 1)][1:]


def _scatter_copies(srcs, dsts, send_sems, recv_sems, local_sems, waves=1):
    n = len(srcs)
    x, y, c = _position()
    me = 4 * x + 2 * y + c
    mine = [pltpu.make_async_copy(srcs[a].at[me], dsts[a].at[me], local_sems.at[a]) for a in range(n)]
    for cp in mine:
        cp.start()
    peers = []
    for fx, fy, fc in FLIPS:
        tx = 1 - x if fx else x
        ty = 1 - y if fy else y
        tc = 1 - c if fc else c
        peers.append(((tx, ty, tc), 4 * tx + 2 * ty + tc))
    copies = []
    for w in range(waves):
        wave = []
        for k, (peer_id, peer) in enumerate(peers):
            for a in range(n):
                rows = srcs[a].shape[1]
                step = rows if waves == 1 else (rows // waves) // 16 * 16
                r0 = w * step
                cut = pl.ds(r0, step if w < waves - 1 else rows - r0)
                sem = (7 * a + k) * waves + w
                cp = pltpu.make_async_remote_copy(
                    src_ref=srcs[a].at[peer, cut], dst_ref=dsts[a].at[me, cut],
                    send_sem=send_sems.at[sem], recv_sem=recv_sems.at[sem],
                    device_id=peer_id, device_id_type=MESH)
                cp.start()
                wave.append(cp)
        for cp in wave:
            cp.wait_send()
        copies += wave
    for cp in copies:
        cp.wait_recv()
    for cp in mine:
        cp.wait()


def _pair_copies(srcs, dsts, send_sems, recv_sems, local_sems):
    x, y, c = _position()
    copies = [pltpu.make_async_remote_copy(
        src_ref=srcs[a], dst_ref=dsts[a], send_sem=send_sems.at[a], recv_sem=recv_sems.at[a],
        device_id=(x, y, 1 - c), device_id_type=MESH) for a in range(len(srcs))]
    for cp in copies:
        cp.start()
    for cp in copies:
        cp.wait()


def _chip_scatter_copies(srcs, dsts, send_sems, recv_sems, local_sems, waves=1):
    n = len(srcs)
    x, y, c = _position()
    here = 2 * x + y
    mine = [pltpu.make_async_copy(srcs[a].at[here], dsts[a].at[here], local_sems.at[a]) for a in range(n)]
    for cp in mine:
        cp.start()
    peers = []
    for fx, fy in ((1, 0), (0, 1), (1, 1)):
        tx = 1 - x if fx else x
        ty = 1 - y if fy else y
        peers.append(((tx, ty, c), 2 * tx + ty))
    copies = []
    for w in range(waves):
        wave = []
        for k, (peer_id, peer) in enumerate(peers):
            for a in range(n):
                rows = srcs[a].shape[1]
                step = rows if waves == 1 else (rows // waves) // 16 * 16
                r0 = w * step
                cut = pl.ds(r0, step if w < waves - 1 else rows - r0)
                sem = (3 * a + k) * waves + w
                cp = pltpu.make_async_remote_copy(
                    src_ref=srcs[a].at[peer, cut], dst_ref=dsts[a].at[here, cut],
                    send_sem=send_sems.at[sem], recv_sem=recv_sems.at[sem],
                    device_id=peer_id, device_id_type=MESH)
                cp.start()
                wave.append(cp)
        for cp in wave:
            cp.wait_send()
        copies += wave
    for cp in copies:
        cp.wait_recv()
    for cp in mine:
        cp.wait()


def _comm_sems(n, waves=1):
    return [pltpu.SemaphoreType.DMA((7 * n * waves,)), pltpu.SemaphoreType.DMA((7 * n * waves,)),
            pltpu.SemaphoreType.DMA((n,))]


def _handshake(peers):
    barrier = pltpu.get_barrier_semaphore()
    for peer in peers:
        pl.semaphore_signal(barrier, inc=1, device_id=peer, device_id_type=MESH)
    pl.semaphore_wait(barrier, len(peers))


def _sequencer_call(arrays, out_types, copies_fn, peers_fn, *, name, collective_id, waves=1):
    n = len(arrays)
    srcs = [jax.new_ref(a, memory_space=pltpu.MemorySpace.HBM) for a in arrays]
    dsts = [jax.empty_ref(t, memory_space=pltpu.MemorySpace.HBM) for t in out_types]
    extra = {} if waves == 1 else {"waves": waves}

    @pl.kernel(mesh=plsc.ScalarSubcoreMesh(axis_name="sequencer", num_cores=1), name=name,
               scratch_types=_comm_sems(n, waves), compiler_params=pltpu.CompilerParams(collective_id=collective_id))
    def launch(send_sems, recv_sems, local_sems):
        _handshake(peers_fn())
        copies_fn(srcs, dsts, send_sems, recv_sems, local_sems, **extra)

    launch()
    return [d[...] for d in dsts]


def _all_other_devices():
    x, y, c = _position()
    return [(1 - x if fx else x, 1 - y if fy else y, 1 - c if fc else c) for fx, fy, fc in FLIPS]


def _gather_relay_peers():
    x, y, c = _position()
    return [(x, y, 1 - c), (1 - x, y, c), (x, 1 - y, c)]


def _gather_peers():
    x, y, c = _position()
    return [(x, y, 1 - c), (1 - x, y, c), (x, 1 - y, c), (1 - x, 1 - y, c)]


def _small_gather_async(shards, *, name, collective_id):
    return _sequencer_call(shards, [jax.ShapeDtypeStruct((N_DEV, *s.shape), s.dtype) for s in shards],
                           _gather_copies, _gather_peers, name=name, collective_id=collective_id)


def _sibling():
    x, y, c = _position()
    return [(x, y, 1 - c)]


def _same_core_of_other_chips():
    x, y, c = _position()
    return [(1 - x, y, c), (x, 1 - y, c), (1 - x, 1 - y, c)]


def _pair_exchange_async(parts, *, name, collective_id):
    return _sequencer_call(parts, [jax.ShapeDtypeStruct(p.shape, p.dtype) for p in parts],
                           _pair_copies, _sibling, name=name, collective_id=collective_id)


def _chip_scatter_async(parts, *, name, collective_id, waves=1):
    return _sequencer_call(parts, [jax.ShapeDtypeStruct(p.shape, p.dtype) for p in parts],
                           _chip_scatter_copies, _same_core_of_other_chips, name=name, collective_id=collective_id,
                           waves=waves)


def _pair_sum_call(mine, theirs, *, name, tw=D_MODEL):
    n, r, w = mine.shape

    def body(a_ref, b_ref, o_ref):
        o_ref[...] = (a_ref[...].astype(F32) + b_ref[...].astype(F32)).astype(o_ref.dtype)

    spec = pl.BlockSpec((None, r, tw), lambda i, j: (i, 0, j))
    return pl.pallas_call(
        body, name=name, grid=(n, w // tw), in_specs=[spec, spec], out_specs=spec,
        out_shape=pltpu.HBM(mine.shape, mine.dtype),
        compiler_params=_cp(("parallel", "parallel")),
    )(*_hbm(mine, theirs))


def _scatter_blocks_async(parts, *, name, collective_id, waves=1):
    return _sequencer_call(parts, [jax.ShapeDtypeStruct(p.shape, p.dtype) for p in parts],
                           _scatter_copies, _all_other_devices, name=name, collective_id=collective_id, waves=waves)


def _all_gather_async(shards, *, name, collective_id):
    return _sequencer_call(shards, [jax.ShapeDtypeStruct((N_DEV, *s.shape), s.dtype) for s in shards],
                           _gather_copies_relayed, _gather_relay_peers, name=name, collective_id=collective_id)


def _adamw_math(w, g, m, v):
    m = ADAM_B1 * m + (1.0 - ADAM_B1) * g
    v = ADAM_B2 * v + (1.0 - ADAM_B2) * (g * g)
    m_hat = m / (1.0 - ADAM_B1 ** ADAM_STEP)
    v_hat = v / (1.0 - ADAM_B2 ** ADAM_STEP)
    delta = -ADAM_LR * (m_hat / (jnp.sqrt(v_hat) + ADAM_EPS) + ADAM_WD * w)
    return delta, m, v


def _adamw_reduce_call(recv, w, m, v, *, name, T=128):
    R, W = w.shape
    n_parts = recv.shape[0]
    if R % T == 0:
        tr, tw = T, W
    elif (R // 2) % 16 == 0:
        tr, tw = R // 2, W
    else:
        tr, tw = R, 2 * LANES

    def body(p_ref, w_ref, m_ref, v_ref, g_out, d_out, m_out, v_out):
        g = p_ref[0].astype(F32)
        for d in range(1, n_parts):
            g = g + p_ref[d].astype(F32)
        g_out[...] = g
        d_out[...], m_out[...], v_out[...] = _adamw_math(w_ref[...], g, m_ref[...], v_ref[...])

    row = pl.BlockSpec((tr, tw), lambda i, j: (i, j))
    out = pltpu.HBM((R, W), F32)
    return pl.pallas_call(
        body, name=name, grid=(R // tr, W // tw),
        in_specs=[pl.BlockSpec((n_parts, tr, tw), lambda i, j: (0, i, j)), row, row, row],
        out_specs=[row] * 4, out_shape=[out] * 4,
        compiler_params=_cp(("parallel", "parallel"), VMEM_BIG),
    )(*_hbm(recv, w, m, v))


SMALL_EARLY = (("b_gate", 8), ("w_spatial", 512), ("b_spatial", 8), ("norm_post_mix", 8), ("norm_pre_ffn", 8),
               ("norm_post_ffn", 8), ("w_gate_up", 128), ("gla_norm", 64), ("sgu_ln_g", 64), ("sgu_ln_b", 64))
SMALL_EARLY_AT = {}
for _name, _rows in SMALL_EARLY:
    SMALL_EARLY_AT[_name] = sum(r for _, r in SMALL_EARLY[:len(SMALL_EARLY_AT)])
SMALL_EARLY_ROWS = sum(r for _, r in SMALL_EARLY)
SMALL_LATE_ROWS = 16


def _small_early_rows(grads):
    def rows(a, n_rows):
        a = a.reshape(-1, LANES)
        return jnp.pad(a, ((0, n_rows - a.shape[0]), (0, 0)))

    def device_major(a, n_rows):
        r, c = a.shape[0], a.shape[1] // N_DEV
        a = a.reshape(r, N_DEV, c).transpose(1, 0, 2)
        a = jnp.pad(a, ((0, 0), (0, n_rows // N_DEV - r), (0, LANES - c)))
        return a.reshape(n_rows, LANES)

    pieces = []
    for name, n_rows in SMALL_EARLY:
        g = grads[name]
        if name in SMALL_SHARDED:
            pieces.append(device_major(g.reshape(g.shape[0], -1) if g.ndim == 2 else g.reshape(g.shape[0], g.shape[-1]), n_rows))
        else:
            pieces.append(rows(g, n_rows))
    return jnp.concatenate(pieces, axis=0)


def _small_update_call(got_early, got_late, w, m, v, *, name):
    names = list(SMALL)
    n_p = len(names)
    E = SMALL_EARLY_ROWS

    def body(early_ref, late_ref, *rest):
        w_refs, m_refs, v_refs = [dict(zip(names, rest[i * n_p:(i + 1) * n_p])) for i in range(3)]
        outs, tot = rest[3 * n_p:-1], rest[-1]
        loss_out = outs[0]
        g_out, d_out, m_out, v_out = [dict(zip(names, outs[1 + i * n_p:1 + (i + 1) * n_p])) for i in range(4)]
        acc, acc_late = early_ref[0], late_ref[0]
        for d in range(1, N_DEV):
            acc, acc_late = acc + early_ref[d], acc_late + late_ref[d]
        tot[0:E, :] = acc
        tot[E:E + SMALL_LATE_ROWS, :] = acc_late
        loss_out[...] = tot[E + 8:E + 9, :]

        x, y, c = _position()
        me = 4 * x + 2 * y + c

        def update(name, g, ix):
            g_out[name][ix] = g
            d_out[name][ix], m_out[name][ix], v_out[name][ix] = _adamw_math(
                w_refs[name][ix], g, m_refs[name][ix], v_refs[name][ix])

        for name in names:
            shape = w[name].shape
            if name in SMALL_SHARDED:
                per_dev = dict(SMALL_EARLY)[name] // N_DEV
                at = pl.multiple_of(SMALL_EARLY_AT[name] + me * per_dev, 8)
                g = tot[pl.ds(at, per_dev), :]
                update(name, g[:shape[1], :shape[2]], (0,))
            elif name == "w_spatial":
                for grp in range(GROUPS):
                    at = SMALL_EARLY_AT[name] + grp * SBLOCK
                    update(name, tot[at:at + SBLOCK, :], (0, grp))
            elif name == "b_spatial":
                at = SMALL_EARLY_AT[name]
                update(name, tot[at:at + GROUPS, :], (0,))
            else:
                at = E if name == "norm_pre_mix" else SMALL_EARLY_AT[name]
                for k in range(shape[1] // LANES):
                    update(name, tot[at + k:at + k + 1, :], (slice(None), pl.ds(k * LANES, LANES)))

    state = [s[n] for s in (w, m, v) for n in names]
    out_shapes = [jax.ShapeDtypeStruct((1, LANES), F32)] + [jax.ShapeDtypeStruct(w[n].shape, F32) for n in names] * 4
    outs = pl.pallas_call(
        body, name=name,
        in_specs=[VMEM_SPEC] * (2 + len(state)), out_specs=[VMEM_SPEC] * len(out_shapes), out_shape=out_shapes,
        scratch_shapes=[pltpu.VMEM((E + SMALL_LATE_ROWS, LANES), F32)],
    )(got_early, got_late, *state)
    per_name = {n: tuple(outs[1 + i * n_p + j] for i in range(4)) for j, n in enumerate(names)}
    return outs[0], per_name


def _tile_rows(n_elems):
    return -(-n_elems // (8 * LANES)) * 8


def _pack_rows(parts, rows):
    pieces = []
    for p in parts:
        q = p.reshape(-1, LANES)
        pieces.append(jnp.pad(q, ((0, _tile_rows(p.size) - q.shape[0]), (0, 0))))
    buf = jnp.concatenate(pieces, axis=0)
    return jnp.pad(buf, ((0, rows - buf.shape[0]), (0, 0)))


def _in_w_blocks_call(a, live, finished, *, name, tk=TOKEN_TILE):
    S = a.shape[0]
    tk = min(tk, S)
    steps = S // tk
    n_live = len(live)
    n_chips = N_DEV // 2
    out_shape = (n_chips, IN_BLK, D_MODEL)

    def body(a_ref, *rest):
        live_refs, done_refs = rest[:n_live], rest[n_live:n_live + len(finished)]
        keep_ref, send_ref = rest[n_live + len(finished):n_live + len(finished) + 2]
        accs = rest[n_live + len(finished) + 2:]
        k = pl.program_id(0)

        @pl.when(k == 0)
        def _():
            for acc in accs:
                acc[...] = jnp.zeros_like(acc)

        av = a_ref[...]
        for src, acc in zip(live_refs, accs):
            for m0 in range(0, src.shape[1], 1024):
                m1 = min(src.shape[1], m0 + 1024)
                acc[m0:m1, :] += _dot(src[:, m0:m1], av, TN)

        @pl.when(k == steps - 1)
        def _():
            groups = [(acc, cols) for acc, (_, cols) in zip(accs, live)]
            groups += [(ref, cols) for ref, (_, cols) in zip(done_refs, finished)]
            core = lax.axis_index("c")

            def cut(d, out_ref):
                lo, hi = IN_BLK * d, IN_BLK * (d + 1)
                for ref, (c0, c1) in groups:
                    s0, e0 = max(lo, c0), min(hi, c1)
                    if s0 < e0:
                        out_ref[d // 2, s0 - lo:e0 - lo, :] = ref[s0 - c0:e0 - c0, :].astype(BF16)

            for d in range(N_DEV):
                pl.when(core == d % 2)(lambda d=d: cut(d, keep_ref))
                pl.when(core != d % 2)(lambda d=d: cut(d, send_ref))

    acc_shapes = [(arr.shape[1], D_MODEL) for arr, _ in live]
    tiles = [((tk, D_MODEL), BF16)] + [((tk, arr.shape[1]), BF16) for arr, _ in live]
    resident = ([(arr.shape, arr.dtype) for arr, _ in finished] + [(out_shape, BF16)] * 2
                + [(s, F32) for s in acc_shapes])
    return pl.pallas_call(
        body, name=name, grid=(steps,),
        in_specs=[pl.BlockSpec((tk, D_MODEL), lambda k: (k, 0))]
        + [pl.BlockSpec((tk, arr.shape[1]), lambda k: (k, 0)) for arr, _ in live]
        + [_res(arr.shape) for arr, _ in finished],
        out_specs=[_acc(out_shape)] * 2,
        out_shape=[pltpu.HBM(out_shape, BF16)] * 2,
        scratch_shapes=[pltpu.VMEM(s, F32) for s in acc_shapes],
        compiler_params=_cp(("arbitrary",), _vmem_limit(tiles, resident)),
    )(*_hbm(a, *[arr for arr, _ in live], *[arr for arr, _ in finished]))


def _local_step(x, target, W, scatter, finish, pair, chip_scatter):
    g1, g2, g3, g4 = [W[n].reshape(1, D_MODEL) for n in ("norm_pre_mix", "norm_post_mix", "norm_pre_ffn", "norm_post_ffn")]
    wfi, wfo = W["w_ffn_in"], W["w_ffn_out"].reshape(4, FF_BLK, D_MODEL)
    wg = jnp.pad(W["w_gate_up"], ((0, LANES - RANK), (0, 0))).astype(BF16)
    bgate = W["b_gate"].reshape(1, QK)
    gnorm = W["gla_norm"].reshape(HEADS, 1, DV)
    ln_g = W["sgu_ln_g"].reshape(GROUPS, 1, DG)
    ln_b = W["sgu_ln_b"].reshape(GROUPS, 1, DG)
    w_sp = W["w_spatial"]
    b_sp = W["b_spatial"].reshape(GROUPS, SBLOCK, 1)

    a, pa, alow, ps, pg, w_in_t = _in_proj_call(x, g1, W["w_in_blocks"], name="in_proj")
    y_gla, states = _gla_fwd_call(pa, alow, wg, bgate, gnorm, name="gla_fwd")
    y_sgu = _sgu_fwd_call(ps, ln_g, ln_b, w_sp, b_sp, name="sgu_fwd")
    t1, t2, merged, mix, x1, h = _mixer_tail_call(y_gla, y_sgu, pg, x, W["w_branch_gla"], W["w_branch_sgu"],
                                                  W["w_out"], g2, g3, name="mixer_tail")
    gu, z = _ffn_in_call(h, wfi, name="ffn_in")
    loss, dx2, dy, dg4 = _ffn_out_loss_call(z, wfo, x1, target, g4, name="ffn_out_loss")

    grads = {"norm_post_ffn": dg4}
    done = {}
    dgu = _ffn_out_bwd_call(dy, wfo, gu, name="ffn_out_bwd")
    dw_ffn_out = _weight_grads_call([z, dy], [(0, 1)], name="d_ffn_out_w")[0].reshape(N_DEV, D_FF // N_DEV, D_MODEL)
    dw_ffn_in = _ffn_in_grad_call(h, dgu, name="d_ffn_in_w")
    dgu, dw_ffn_out, dw_ffn_in = lax.optimization_barrier((dgu, dw_ffn_out, dw_ffn_in))
    ffn_received = scatter(("w_ffn_out", "w_ffn_in"), [dw_ffn_out, dw_ffn_in])
    dx1, dmix, grads["norm_pre_ffn"], grads["norm_post_mix"] = _ffn_in_bwd_call(dgu, wfi, dx2, x1, mix, g3, g2, name="ffn_in_bwd")
    dt1, dt2, dpg, dy_gla, dy_sgu = _mixer_bwd_call(dmix, t1, t2, pg, W["w_out"], W["w_branch_gla"], W["w_branch_sgu"],
                                                    name="mixer_bwd")
    rows = D_MODEL // N_DEV
    mixer_grads = _weight_grads_call([merged, dmix, y_gla, dt1, y_sgu, dt2], [(0, 1), (2, 3), (4, 5)], name="d_mixer_w")
    dy_gla, dy_sgu, mixer_grads = lax.optimization_barrier((dy_gla, dy_sgu, mixer_grads))
    mixer_received = scatter(("w_out", "w_branch_gla", "w_branch_sgu"),
                             [g.reshape(N_DEV, rows, D_MODEL) for g in mixer_grads])
    dps, dlg, dlb, dwsp, dbsp = _sgu_bwd_call(ps, ln_g, ln_b, w_sp, b_sp, dy_sgu, name="sgu_bwd")
    dw_s, dw_g = _weight_grads_call([a, dps, dpg], [(1, 0), (2, 0)], name="d_in_w_sgu_gates")
    dy_gla, dw_s, dw_g = lax.optimization_barrier((dy_gla, dw_s, dw_g))
    dpa, dlogit, dgn, dbg = _gla_bwd_call(pa, alow, wg, bgate, gnorm, states, dy_gla, name="gla_bwd")
    ffn_received, mixer_received, dpa, dlogit = lax.optimization_barrier((ffn_received, mixer_received, dpa, dlogit))
    dlow = _mm(dlogit, wg, "nt", BF16, name="d_gate_up_x")
    keep, send = _in_w_blocks_call(a, [(dpa, A_COLS), (dlow, LOW_COLS)], [(dw_s, S_COLS), (dw_g, G_COLS)],
                                   name="d_in_w_blocks")
    ffn_received, mixer_received, send = lax.optimization_barrier((ffn_received, mixer_received, send))
    from_sibling = pair(send)
    done.update({**finish(ffn_received), **finish(mixer_received)})
    dwg = _mm(alow, dlogit, "tn", F32, name="d_gate_up_w")
    done, dwg, keep = lax.optimization_barrier((done, dwg, keep))
    chip_sum = _pair_sum_call(keep, from_sibling, name="pair_sum_w_in")
    chip_sum, dpa = lax.optimization_barrier((chip_sum, dpa))
    in_received = chip_scatter(chip_sum)
    grad_x, grads["norm_pre_mix"] = _in_proj_bwd_call(dpa, dlow, dps, dpg, w_in_t, x, dx1, g1, name="in_proj_bwd")

    grads["w_gate_up"] = dwg[:RANK]
    grads["b_gate"] = dbg
    grads["gla_norm"] = dgn
    grads["sgu_ln_g"] = dlg
    grads["sgu_ln_b"] = dlb
    grads["w_spatial"] = dwsp
    grads["b_spatial"] = dbsp
    done.update(finish({"w_in": in_received}))
    return loss, grad_x, grads, done


WEIGHTS = ("norm_pre_mix", "w_in", "w_gate_up", "b_gate", "gla_norm", "sgu_ln_g", "sgu_ln_b", "w_spatial",
           "b_spatial", "w_branch_gla", "w_branch_sgu", "w_out", "norm_post_mix", "norm_pre_ffn", "w_ffn_in",
           "w_ffn_out", "norm_post_ffn")
BIG = ("w_in", "w_branch_gla", "w_branch_sgu", "w_out", "w_ffn_in", "w_ffn_out")
MIXER = ("w_branch_gla", "w_branch_sgu", "w_out")
FFN = ("w_ffn_in", "w_ffn_out")
COLUMN_SHARDED = ("w_in", "w_ffn_in")
LATE_SCATTER_WAVES = 4
GATHER_WAVES = 4
SMALL = tuple(n for n in WEIGHTS if n not in BIG)
SMALL_SHARDED = ("w_gate_up", "gla_norm", "sgu_ln_g", "sgu_ln_b")
SMALL_GATHER_ROWS = 32


def kernel(x, norm_pre_mix, w_in, w_gate_up, b_gate, gla_norm, sgu_ln_g, sgu_ln_b, w_spatial, b_spatial, w_branch_gla, w_branch_sgu, w_out, norm_post_mix, norm_pre_ffn, w_ffn_in, w_ffn_out, norm_post_ffn, loss_target, m_norm_pre_mix, m_w_in, m_w_gate_up, m_b_gate, m_gla_norm, m_sgu_ln_g, m_sgu_ln_b, m_w_spatial, m_b_spatial, m_w_branch_gla, m_w_branch_sgu, m_w_out, m_norm_post_mix, m_norm_pre_ffn, m_w_ffn_in, m_w_ffn_out, m_norm_post_ffn, v_norm_pre_mix, v_w_in, v_w_gate_up, v_b_gate, v_gla_norm, v_sgu_ln_g, v_sgu_ln_b, v_w_spatial, v_b_spatial, v_w_branch_gla, v_w_branch_sgu, v_w_out, v_norm_post_mix, v_norm_pre_ffn, v_w_ffn_in, v_w_ffn_out, v_norm_post_ffn):
    given = dict(locals())
    def local(a, n):
        return a[0].T if n in COLUMN_SHARDED else a[0]

    w = {n: local(given[n], n) for n in WEIGHTS}
    m = {n: local(given["m_" + n], n) for n in WEIGHTS}
    v = {n: local(given["v_" + n], n) for n in WEIGHTS}
    xs, target = x[0], loss_target[0]

    small_shard = _pack_rows([w[n] for n in SMALL_SHARDED], SMALL_GATHER_ROWS)
    first = _all_gather_hbm([w["w_in"].astype(BF16), small_shard], name="gather_w_in", waves=GATHER_WAVES)
    rest, first = lax.optimization_barrier(([w[n].astype(BF16) for n in MIXER + FFN], first))
    rest_blocks = _all_gather_async(rest, name="gather_rest", collective_id=1)
    W = {n: w[n] for n in SMALL if n not in SMALL_SHARDED}
    blocks = dict(zip(MIXER + FFN, rest_blocks))
    for n in ("w_branch_gla", "w_branch_sgu", "w_out", "w_ffn_out"):
        W[n] = blocks[n].reshape(-1, D_MODEL)
    W["w_ffn_in"] = blocks["w_ffn_in"]
    W["w_in_blocks"] = first[0]
    small_blocks = first[1]
    off = 0
    for n in SMALL_SHARDED:
        r, c = w[n].shape
        blk = small_blocks[:, off:off + r * c // LANES].reshape(N_DEV, r, c)
        W[n] = blk.transpose(1, 0, 2).reshape(r, N_DEV * c)
        off += _tile_rows(r * c)

    scatter_ids = iter((3, 4))

    def scatter(names, parts):
        got = _scatter_blocks_async(parts, name="scatter_" + "_".join(names), collective_id=next(scatter_ids))
        return dict(zip(names, got))

    def finish(received):
        return {n: _adamw_reduce_call(r, w[n], m[n], v[n], name="adamw_" + n) for n, r in received.items()}

    def pair(part):
        return _pair_exchange_async([part], name="pair_w_in", collective_id=5)[0]

    def chip_scatter(part):
        return _chip_scatter_async([part], name="scatter_w_in", collective_id=6, waves=LATE_SCATTER_WAVES)[0]

    loss_part, grad_x, grads, big_done = _local_step(xs, target, W, scatter, finish, pair, chip_scatter)

    late = jnp.concatenate([grads["norm_pre_mix"].reshape(8, LANES), jnp.broadcast_to(loss_part, (8, LANES))], axis=0)
    gathered = _small_gather_async([_small_early_rows(grads), late], name="gather_small", collective_id=7)
    gathered, big_done["w_in"] = lax.optimization_barrier((gathered, big_done["w_in"]))
    loss_row, small_done = _small_update_call(
        *gathered, *[{n: given[prefix + n] for n in SMALL} for prefix in ("", "m_", "v_")], name="small_update")

    def pick(i):
        return [small_done[n][i] if n in SMALL else (big_done[n][i].T if n in COLUMN_SHARDED else big_done[n][i])[None]
                for n in WEIGHTS]

    return (loss_row[0, 0], grad_x[None], *pick(0), *pick(1), *pick(2), *pick(3))
```

```python
import jax
import jax.numpy as jnp
from jax import lax
from jax.experimental import pallas as pl
from jax.experimental.pallas import tpu as pltpu
from jax.experimental.pallas import tpu_sc as plsc

F32 = jnp.float32
BF16 = jnp.bfloat16

D_MODEL = 1024
N_DEV = 8
CHUNK = 64
HEADS = 4
DK = 128
DV = 256
QK = HEADS * DK
GV = HEADS * DV
RANK = 16
GROUPS = 4
SBLOCK = 128
DG = 256
D_FF = 2816
FF_BLK = 704
EPS = 1e-6
Q_SCALE = DK ** -0.5
LANES = 128
VMEM_BIG = 48 * 1024 * 1024

D_IN = 7184
IN_BLK = 898
A_COLS = (0, 3072)
LOW_COLS = (3072, 3088)
S_COLS = (3088, 5136)
G_COLS = (5136, 7184)

ADAM_LR = 0.001
ADAM_B1 = 0.9
ADAM_B2 = 0.999
ADAM_EPS = 1e-08
ADAM_WD = 0.01
ADAM_STEP = 10

MESH = pl.DeviceIdType.MESH
ANY = pl.BlockSpec(memory_space=pl.ANY)
VMEM_SPEC = pl.BlockSpec(memory_space=pltpu.VMEM)


def _cp(sem=None, vmem=None):
    return pltpu.CompilerParams(dimension_semantics=sem, vmem_limit_bytes=vmem)


def _hbm(*arrays):
    return [pltpu.with_memory_space_constraint(a, pltpu.HBM) for a in arrays]


def _sig(x):
    return 0.5 * jnp.tanh(0.5 * x) + 0.5


GELU_C = 0.7978845608028654
GELU_A = 0.044715


def _gelu(x):
    t = jnp.tanh((GELU_C * x) * (1.0 + GELU_A * (x * x)))
    return (0.5 * x) * (1.0 + t)


def _gelu_and_grad(x):
    x2 = x * x
    t = jnp.tanh((GELU_C * x) * (1.0 + GELU_A * x2))
    one_t = 1.0 + t
    hx = 0.5 * x
    grad = 0.5 * one_t + (hx * (1.0 - t * t)) * (GELU_C + (3.0 * GELU_A * GELU_C) * x2)
    return hx * one_t, grad


def _logsig(x):
    return jnp.minimum(x, 0.0) - jnp.log1p(jnp.exp(-jnp.abs(x)))


def _dot(a, b, dims):
    return lax.dot_general(a, b, (dims, ((), ())), preferred_element_type=F32)


NN = ((1,), (0,))
NT = ((1,), (1,))
TN = ((0,), (0,))


def _exact_mask_dot(mask_bf16, x):
    hi = x.astype(BF16)
    r1 = x - hi.astype(F32)
    mid = r1.astype(BF16)
    lo = (r1 - mid.astype(F32)).astype(BF16)
    return _dot(mask_bf16, hi, NN) + _dot(mask_bf16, mid, NN) + _dot(mask_bf16, lo, NN)


def _pick_tile(dim, target):
    if dim <= target:
        return dim
    best = None
    for t in range(LANES, int(1.4 * target) + 1, LANES):
        if dim % t == 0:
            best = t
    assert best is not None, (dim, target)
    return best


def _mm_call(a, b, *, name, grid, a_spec, b_spec, o_spec, out_shape, dims, acc_shape):
    nk = grid[2]

    def body(a_ref, b_ref, o_ref, *acc):
        part = _dot(a_ref[...], b_ref[...], dims)
        if nk == 1:
            o_ref[...] = part.astype(o_ref.dtype)
        else:
            acc_ref = acc[0]
            k = pl.program_id(2)

            @pl.when(k == 0)
            def _():
                acc_ref[...] = part

            @pl.when(k > 0)
            def _():
                acc_ref[...] += part

            @pl.when(k == nk - 1)
            def _():
                o_ref[...] = acc_ref[...].astype(o_ref.dtype)

    return pl.pallas_call(
        body, name=name, grid=grid, in_specs=[a_spec, b_spec], out_specs=o_spec, out_shape=out_shape,
        scratch_shapes=[] if nk == 1 else [pltpu.VMEM(acc_shape, F32)],
        compiler_params=_cp(("parallel", "parallel", "arbitrary"), VMEM_BIG),
    )(a, b)


def _mm(a, b, mode, out_dtype, *, name, tm=512, tn=1024, tk=1024):
    if mode == "nn":
        (M, K), (_, N) = a.shape, b.shape
    elif mode == "nt":
        (M, K), (N, _) = a.shape, b.shape
    else:
        (K, M), (_, N) = a.shape, b.shape
    tm, tn, tk = _pick_tile(M, tm), _pick_tile(N, tn), _pick_tile(K, tk)
    if mode == "nn":
        a_spec = pl.BlockSpec((tm, tk), lambda i, j, k: (i, k))
        b_spec = pl.BlockSpec((tk, tn), lambda i, j, k: (k, j))
        dims = NN
    elif mode == "nt":
        a_spec = pl.BlockSpec((tm, tk), lambda i, j, k: (i, k))
        b_spec = pl.BlockSpec((tn, tk), lambda i, j, k: (j, k))
        dims = NT
    else:
        a_spec = pl.BlockSpec((tk, tm), lambda i, j, k: (k, i))
        b_spec = pl.BlockSpec((tk, tn), lambda i, j, k: (k, j))
        dims = TN
    return _mm_call(a, b, name=name, grid=(M // tm, N // tn, K // tk), a_spec=a_spec, b_spec=b_spec,
                    o_spec=pl.BlockSpec((tm, tn), lambda i, j, k: (i, j)),
                    out_shape=jax.ShapeDtypeStruct((M, N), out_dtype), dims=dims, acc_shape=(tm, tn))


ROW_TILE = 512
SUB_ROWS = 256


def _sub_tiles(T):
    return [pl.ds(r0, min(SUB_ROWS, T)) for r0 in range(0, T, SUB_ROWS)]


def _rt(T, W, c=0):
    return pl.BlockSpec((T, W), lambda i: (i, c))


def _rt3(nb, T, W):
    return pl.BlockSpec((nb, T, W), lambda i: (0, i, 0))


def _res(shape):
    nd = len(shape)
    return pl.BlockSpec(tuple(shape), lambda i: (0,) * nd, pipeline_mode=pl.Buffered(1))


def _acc(shape):
    nd = len(shape)
    return pl.BlockSpec(tuple(shape), lambda i: (0,) * nd, pipeline_mode=pl.Buffered(1))


def _nbytes(shape, dtype):
    n = jnp.dtype(dtype).itemsize
    for s in shape:
        n *= s
    return n


def _vmem_limit(tiles, resident, temps=16 * 1024 * 1024):
    need = 2 * sum(_nbytes(s, d) for s, d in tiles) + sum(_nbytes(s, d) for s, d in resident) + temps
    return min(need, 60 * 1024 * 1024)


def _tok_call(body, *, name, S, T, ins, outs, semantics="parallel"):
    tiles = [(spec.block_shape, a.dtype) for a, spec, kind in ins + outs if kind == "tile"]
    resident = [(a.shape, a.dtype) for a, spec, kind in ins + outs if kind == "res"]
    return pl.pallas_call(
        body, name=name, grid=(S // T,),
        in_specs=[spec for _, spec, _ in ins], out_specs=[spec for _, spec, _ in outs],
        out_shape=[pltpu.HBM(a.shape, a.dtype) for a, _, _ in outs],
        compiler_params=_cp((semantics,), _vmem_limit(tiles, resident)),
    )(*_hbm(*[a for a, _, _ in ins]))


def _tile(a, spec):
    return (a, spec, "tile")


def _whole(a):
    return (a, _res(a.shape), "res")


def _out_tile(shape, dtype, spec):
    return (jax.ShapeDtypeStruct(shape, dtype), spec, "tile")


def _out_acc(shape, dtype=F32):
    return (jax.ShapeDtypeStruct(shape, dtype), _acc(shape), "res")


def _rms_stats(x):
    r = lax.rsqrt(jnp.mean(x * x, axis=-1, keepdims=True) + EPS)
    return r, x * r


def _rms_bwd(xh, r, g, dy):
    dxh = dy * g
    dx = r * (dxh - xh * jnp.mean(dxh * xh, axis=-1, keepdims=True))
    dg = jnp.sum(dy * xh, axis=0, keepdims=True)
    return dx, dg


def _accum(ref, val):
    @pl.when(pl.program_id(0) == 0)
    def _():
        ref[...] = val

    @pl.when(pl.program_id(0) > 0)
    def _():
        ref[...] += val


def _dot_rows_t(a, w_ref, row0, o_ref, chunk=1024):
    n = o_ref.shape[1]
    for n0 in range(0, n, chunk):
        n1 = min(n, n0 + chunk)
        o_ref[:, n0:n1] = _dot(a, w_ref[row0 + n0:row0 + n1, :], NT).astype(o_ref.dtype)


def _in_proj_call(x, g1, w_blocks, *, name, T=ROW_TILE):
    S = x.shape[0]

    def body(x_ref, g_ref, blk_ref, a_ref, pa_ref, al_ref, ps_ref, pg_ref, w_ref):
        @pl.when(pl.program_id(0) == 0)
        def _():
            for d in range(N_DEV):
                w_ref[d * IN_BLK:(d + 1) * IN_BLK, :] = blk_ref[d]

        _, xh = _rms_stats(x_ref[...])
        a = (xh * g_ref[...]).astype(BF16)
        a_ref[...] = a
        _dot_rows_t(a, w_ref, A_COLS[0], pa_ref)
        _dot_rows_t(a, w_ref, LOW_COLS[0], al_ref)
        _dot_rows_t(a, w_ref, S_COLS[0], ps_ref)
        _dot_rows_t(a, w_ref, G_COLS[0], pg_ref)

    widths = (D_MODEL, A_COLS[1] - A_COLS[0], LANES, S_COLS[1] - S_COLS[0], G_COLS[1] - G_COLS[0])
    return _tok_call(
        body, name=name, S=S, T=T, semantics="arbitrary",
        ins=[_tile(x, _rt(T, D_MODEL)), _whole(g1), _whole(w_blocks)],
        outs=[_out_tile((S, w), BF16, _rt(T, w)) for w in widths] + [_out_acc((D_IN, D_MODEL), BF16)])


def _mixer_tail_call(y_gla, y_sgu, pg, x, w_bg, w_bs, w_out, g2, g3, *, name, T=ROW_TILE):
    S = x.shape[0]

    def body(yg_ref, ys_ref, pg_ref, x_ref, wbg_ref, wbs_ref, wo_ref, g2_ref, g3_ref,
             t1_ref, t2_ref, mg_ref, mix_ref, x1_ref, h_ref):
        for rows in _sub_tiles(T):
            t1 = _dot(yg_ref[rows, :], wbg_ref[...], NN)
            t2 = _dot(ys_ref[rows, :], wbs_ref[...], NN)
            t1_ref[rows, :] = t1.astype(BF16)
            t2_ref[rows, :] = t2.astype(BF16)
            sg = _sig(pg_ref[rows, pl.ds(0, D_MODEL)].astype(F32))
            ss = _sig(pg_ref[rows, pl.ds(D_MODEL, D_MODEL)].astype(F32))
            merged = (sg * t1 + ss * t2).astype(BF16)
            mg_ref[rows, :] = merged
            mix = _dot(merged, wo_ref[...], NN)
            mix_ref[rows, :] = mix
            _, mh = _rms_stats(mix)
            x1 = x_ref[rows, :] + mh * g2_ref[...]
            x1_ref[rows, :] = x1
            _, xh = _rms_stats(x1)
            h_ref[rows, :] = (xh * g3_ref[...]).astype(BF16)

    row = _rt(T, D_MODEL)
    b16 = lambda: _out_tile((S, D_MODEL), BF16, row)
    f32 = lambda: _out_tile((S, D_MODEL), F32, row)
    return _tok_call(
        body, name=name, S=S, T=T,
        ins=[_tile(y_gla, row), _tile(y_sgu, row), _tile(pg, _rt(T, 2 * D_MODEL)), _tile(x, row),
             _whole(w_bg), _whole(w_bs), _whole(w_out), _whole(g2), _whole(g3)],
        outs=[b16(), b16(), b16(), f32(), f32(), b16()])


FF_CHUNKS = ((0, 1024), (1024, 2048), (2048, D_FF))


def _ffn_in_call(h, wfi, *, name, T=ROW_TILE):
    S = h.shape[0]

    def body(h_ref, w_ref, gu_ref, z_ref):
        hv = h_ref[...]
        for c0, c1 in FF_CHUNKS:
            gate = _dot(hv, w_ref[c0:c1, :], NT)
            up = _dot(hv, w_ref[D_FF + c0:D_FF + c1, :], NT)
            gu_ref[:, c0:c1] = gate.astype(BF16)
            gu_ref[:, D_FF + c0:D_FF + c1] = up.astype(BF16)
            z_ref[:, c0:c1] = (gate * _sig(gate) * up).astype(BF16)

    return _tok_call(
        body, name=name, S=S, T=T,
        ins=[_tile(h, _rt(T, D_MODEL)), _whole(wfi)],
        outs=[_out_tile((S, 2 * D_FF), BF16, _rt(T, 2 * D_FF)),
              _out_tile((S, D_FF), BF16, _rt(T, D_FF))])


def _ffn_out_loss_call(z, wfo, x1, target, g4, *, name, T=ROW_TILE):
    S = x1.shape[0]

    def body(z_ref, w_ref, x1_ref, t_ref, g4_ref, loss_ref, dx2_ref, dy_ref, dg4_ref):
        loss = jnp.zeros((1, 1), F32)
        dg4 = jnp.zeros((1, D_MODEL), F32)
        for rows in _sub_tiles(T):
            y = _dot(z_ref[rows, :], w_ref[...], NN)
            r, yh = _rms_stats(y)
            diff = x1_ref[rows, :] + yh * g4_ref[...] - t_ref[rows, :]
            loss = loss + 0.5 * jnp.sum(jnp.mean(diff * diff, axis=-1, keepdims=True), axis=0, keepdims=True)
            dx2 = diff * (1.0 / D_MODEL)
            dx2_ref[rows, :] = dx2
            dy, dg = _rms_bwd(yh, r, g4_ref[...], dx2)
            dy_ref[rows, :] = dy.astype(BF16)
            dg4 = dg4 + dg
        _accum(loss_ref, jnp.broadcast_to(loss, (1, LANES)))
        _accum(dg4_ref, dg4)

    row = _rt(T, D_MODEL)
    return _tok_call(
        body, name=name, S=S, T=T, semantics="arbitrary",
        ins=[_tile(z, _rt(T, D_FF)), _whole(wfo), _tile(x1, row), _tile(target, row), _whole(g4)],
        outs=[_out_acc((1, LANES)), _out_tile((S, D_MODEL), F32, row), _out_tile((S, D_MODEL), BF16, row),
              _out_acc((1, D_MODEL))])


def _ffn_out_bwd_call(dy, wfo, gu, *, name, T=ROW_TILE):
    S = dy.shape[0]

    def body(dy_ref, w_ref, gu_ref, dgu_ref):
        dyv = dy_ref[...]
        for c0, c1 in FF_CHUNKS:
            dz = _dot(dyv, w_ref[c0:c1, :], NT).astype(BF16)
            gt = gu_ref[:, c0:c1]
            up = gu_ref[:, D_FF + c0:D_FF + c1]
            s = _sig(gt)
            dgu_ref[:, c0:c1] = dz * up * s * (1.0 + gt * (1.0 - s))
            dgu_ref[:, D_FF + c0:D_FF + c1] = dz * gt * s

    wide = _rt(T, 2 * D_FF)
    return _tok_call(
        body, name=name, S=S, T=T,
        ins=[_tile(dy, _rt(T, D_MODEL)), _whole(wfo), _tile(gu, wide)],
        outs=[_out_tile((S, 2 * D_FF), BF16, wide)])[0]


def _ffn_in_bwd_call(dgu, wfi, dx2, x1, mix, g3, g2, *, name, T=ROW_TILE):
    S = x1.shape[0]

    def body(dgu_ref, w_ref, dx2_ref, x1_ref, mix_ref, g3_ref, g2_ref, dx1_ref, dmix_ref, dg3_ref, dg2_ref):
        dg3 = jnp.zeros((1, D_MODEL), F32)
        dg2 = jnp.zeros((1, D_MODEL), F32)
        for rows in _sub_tiles(T):
            dh = _dot(dgu_ref[rows, :], w_ref[...], NN)
            r3, xh = _rms_stats(x1_ref[rows, :])
            d3, g = _rms_bwd(xh, r3, g3_ref[...], dh)
            dg3 = dg3 + g
            dx1 = dx2_ref[rows, :] + d3
            dx1_ref[rows, :] = dx1
            r2, mh = _rms_stats(mix_ref[rows, :])
            dmix, g = _rms_bwd(mh, r2, g2_ref[...], dx1)
            dg2 = dg2 + g
            dmix_ref[rows, :] = dmix.astype(BF16)
        _accum(dg3_ref, dg3)
        _accum(dg2_ref, dg2)

    row = _rt(T, D_MODEL)
    return _tok_call(
        body, name=name, S=S, T=T, semantics="arbitrary",
        ins=[_tile(dgu, _rt(T, 2 * D_FF)), _whole(wfi), _tile(dx2, row), _tile(x1, row), _tile(mix, row),
             _whole(g3), _whole(g2)],
        outs=[_out_tile((S, D_MODEL), F32, row), _out_tile((S, D_MODEL), BF16, row),
              _out_acc((1, D_MODEL)), _out_acc((1, D_MODEL))])


def _mixer_bwd_call(dmix, t1, t2, pg, w_out, w_bg, w_bs, *, name, T=ROW_TILE):
    S = dmix.shape[0]

    def body(dmix_ref, t1_ref, t2_ref, pg_ref, wo_ref, wbg_ref, wbs_ref, dt1_ref, dt2_ref, dpg_ref, dyg_ref, dys_ref):
        gg, gs = pl.ds(0, D_MODEL), pl.ds(D_MODEL, D_MODEL)
        for rows in _sub_tiles(T):
            dm = _dot(dmix_ref[rows, :], wo_ref[...], NT)
            sg = _sig(pg_ref[rows, gg].astype(F32))
            ss = _sig(pg_ref[rows, gs].astype(F32))
            dt1 = (dm * sg).astype(BF16)
            dt2 = (dm * ss).astype(BF16)
            dt1_ref[rows, :] = dt1
            dt2_ref[rows, :] = dt2
            dpg_ref[rows, gg] = (dm * t1_ref[rows, :].astype(F32) * sg * (1.0 - sg)).astype(BF16)
            dpg_ref[rows, gs] = (dm * t2_ref[rows, :].astype(F32) * ss * (1.0 - ss)).astype(BF16)
            dyg_ref[rows, :] = _dot(dt1, wbg_ref[...], NT).astype(BF16)
            dys_ref[rows, :] = _dot(dt2, wbs_ref[...], NT).astype(BF16)

    row = _rt(T, D_MODEL)
    wide = _rt(T, 2 * D_MODEL)
    b16 = lambda: _out_tile((S, D_MODEL), BF16, row)
    return _tok_call(
        body, name=name, S=S, T=T,
        ins=[_tile(dmix, row), _tile(t1, row), _tile(t2, row), _tile(pg, wide), _whole(w_out), _whole(w_bg), _whole(w_bs)],
        outs=[b16(), b16(), _out_tile((S, 2 * D_MODEL), BF16, wide), b16(), b16()])


def _in_proj_bwd_call(dpa, dlow, dps, dpg, w_in_t, x, dx1, g1, *, name, T=ROW_TILE):
    S = x.shape[0]

    def body(dpa_ref, dl_ref, dps_ref, dpg_ref, w_ref, x_ref, dx1_ref, g1_ref, gx_ref, dg1_ref):
        da = (_dot(dpa_ref[...], w_ref[A_COLS[0]:A_COLS[1], :], NN)
              + _dot(dl_ref[...], w_ref[LOW_COLS[0]:LOW_COLS[0] + LANES, :], NN)
              + _dot(dps_ref[...], w_ref[S_COLS[0]:S_COLS[1], :], NN)
              + _dot(dpg_ref[...], w_ref[G_COLS[0]:G_COLS[1], :], NN))
        r, xh = _rms_stats(x_ref[...])
        dxa, dg = _rms_bwd(xh, r, g1_ref[...], da)
        gx_ref[...] = dx1_ref[...] + dxa
        _accum(dg1_ref, dg)

    row = _rt(T, D_MODEL)
    return _tok_call(
        body, name=name, S=S, T=T, semantics="arbitrary",
        ins=[_tile(dpa, _rt(T, dpa.shape[1])), _tile(dlow, _rt(T, dlow.shape[1])), _tile(dps, _rt(T, dps.shape[1])),
             _tile(dpg, _rt(T, dpg.shape[1])), _whole(w_in_t), _tile(x, row), _tile(dx1, row), _whole(g1)],
        outs=[_out_tile((S, D_MODEL), F32, row), _out_acc((1, D_MODEL))])


TOKEN_TILE = 512


def _weight_grads_part(arrays, pairs, *, tk=TOKEN_TILE, out_dtype=BF16):
    S = arrays[0].shape[-2]
    tk = min(tk, S)
    n_in = len(arrays)

    def out_shape(i, j):
        a, b = arrays[i], arrays[j]
        if a.ndim == 3:
            return (a.shape[0], a.shape[2], b.shape[1])
        if b.ndim == 3:
            return (b.shape[0], a.shape[1], b.shape[2])
        return (a.shape[1], b.shape[1])

    shapes = [out_shape(i, j) for i, j in pairs]

    in_place = out_dtype == F32

    def body(ins, outs, accs):
        k = pl.program_id(0)
        if in_place:
            accs = outs

        @pl.when(k == 0)
        def _():
            for acc in accs:
                acc[...] = jnp.zeros_like(acc)

        for (i, j), acc in zip(pairs, accs):
            a_ref, b_ref = ins[i], ins[j]
            if len(a_ref.shape) == 3:
                for n in range(a_ref.shape[0]):
                    acc[n] += _dot(a_ref[n], b_ref[...], TN)
            elif len(b_ref.shape) == 3:
                a = a_ref[...]
                for n in range(b_ref.shape[0]):
                    acc[n] += _dot(a, b_ref[n], TN)
            else:
                b = b_ref[...]
                for m0 in range(0, a_ref.shape[1], 1024):
                    m1 = min(a_ref.shape[1], m0 + 1024)
                    acc[m0:m1, :] += _dot(a_ref[:, m0:m1], b, TN)

        if not in_place:
            @pl.when(k == S // tk - 1)
            def _():
                for out, acc in zip(outs, accs):
                    out[...] = acc[...].astype(out.dtype)

    def in_spec(a):
        if a.ndim == 3:
            return pl.BlockSpec((a.shape[0], tk, a.shape[2]), lambda k: (0, k, 0))
        return pl.BlockSpec((tk, a.shape[1]), lambda k: (k, 0))

    return dict(
        body=body, steps=S // tk, arrays=list(arrays),
        in_specs=[in_spec(a) for a in arrays],
        out_specs=[_acc(s) for s in shapes],
        out_shapes=[pltpu.HBM(s, out_dtype) for s in shapes],
        scratch=[] if in_place else [pltpu.VMEM(s, F32) for s in shapes],
        tiles=[(in_spec(a).block_shape, a.dtype) for a in arrays],
        resident=[(s, F32) for s in shapes] + ([] if in_place else [(s, BF16) for s in shapes]))


def _weight_grads_call(arrays, pairs, *, name, tk=TOKEN_TILE):
    part = _weight_grads_part(arrays, pairs, tk=tk)
    n_in, n_out = len(part["arrays"]), len(part["out_shapes"])

    def body(*refs):
        part["body"](refs[:n_in], refs[n_in:n_in + n_out], refs[n_in + n_out:])

    return pl.pallas_call(
        body, name=name, grid=(part["steps"],),
        in_specs=part["in_specs"], out_specs=part["out_specs"], out_shape=part["out_shapes"],
        scratch_shapes=part["scratch"],
        compiler_params=_cp(("arbitrary",), _vmem_limit(part["tiles"], part["resident"])),
    )(*_hbm(*part["arrays"]))


def _ffn_in_grad_call(h, dgu, *, name, tk=TOKEN_TILE):
    S = h.shape[0]
    tk = min(tk, S)
    shape = (D_FF, D_MODEL)

    def body(h_ref, dgu_ref, out_ref, acc):
        k = pl.program_id(1)

        @pl.when(k == 0)
        def _():
            acc[...] = jnp.zeros_like(acc)

        hv = h_ref[...]
        for c0, c1 in FF_CHUNKS:
            acc[c0:c1, :] += _dot(dgu_ref[:, c0:c1], hv, TN)

        @pl.when(k == S // tk - 1)
        def _():
            out_ref[...] = acc[...].astype(out_ref.dtype)

    tiles = [((tk, D_MODEL), BF16), ((tk, D_FF), BF16), (shape, BF16)]
    return pl.pallas_call(
        body, name=name, grid=(2, S // tk),
        in_specs=[pl.BlockSpec((tk, D_MODEL), lambda g, k: (k, 0)),
                  pl.BlockSpec((tk, D_FF), lambda g, k: (k, g))],
        out_specs=pl.BlockSpec(shape, lambda g, k: (g, 0)),
        out_shape=pltpu.HBM((2 * D_FF, D_MODEL), BF16),
        scratch_shapes=[pltpu.VMEM(shape, F32)],
        compiler_params=_cp(("parallel", "arbitrary"), _vmem_limit(tiles, [(shape, F32)])),
    )(*_hbm(h, dgu))


GLA_TILE = 256
Q_OFF, K_OFF, V_OFF, R_OFF = 0, QK, 2 * QK, 2 * QK + GV


def _tri(lower):
    r = lax.broadcasted_iota(jnp.int32, (CHUNK, CHUNK), 0)
    c = lax.broadcasted_iota(jnp.int32, (CHUNK, CHUNK), 1)
    return jnp.where((r >= c) if lower else (c >= r), 1.0, 0.0).astype(BF16)


def _gla_fwd_call(pa, alow, wg, bgate, gnorm, *, name):
    S = pa.shape[0]
    Tg = min(GLA_TILE, S)
    cb = Tg // CHUNK

    def body(pa_ref, al_ref, wg_ref, bg_ref, gn_ref, y_ref, st_ref, state):
        @pl.when(pl.program_id(0) == 0)
        def _():
            state[...] = jnp.zeros_like(state)

        logit = _dot(al_ref[...], wg_ref[...], NN) + bg_ref[...]
        ls = _logsig(logit) * (1.0 / 16.0)
        tri = _tri(True)
        for c in range(cb):
            rows = pl.ds(c * CHUNK, CHUNK)
            cum = _exact_mask_dot(tri, ls[c * CHUNK:(c + 1) * CHUNK])
            tot = cum[CHUNK - 1:CHUNK]
            kd = (pa_ref[rows, pl.ds(K_OFF, QK)].astype(F32) * jnp.exp(tot - cum)).astype(BF16)
            decay = jnp.exp(tot)
            for h in range(HEADS):
                lanes = slice(h * DK, (h + 1) * DK)
                new = state[h] * decay[:, lanes] + _dot(pa_ref[rows, pl.ds(V_OFF + h * DV, DV)], kd[:, lanes], TN)
                state[h] = new
                st_ref[c, h] = new
        for c in range(cb):
            rows = pl.ds(c * CHUNK, CHUNK)
            for h in range(HEADS):
                qs = (pa_ref[rows, pl.ds(Q_OFF + h * DK, DK)].astype(F32) * Q_SCALE).astype(BF16)
                o = _dot(qs, st_ref[c, h].astype(BF16), NT)
                rs = lax.rsqrt(jnp.mean(o * o, axis=-1, keepdims=True) + EPS)
                rr = pa_ref[rows, pl.ds(R_OFF + h * DV, DV)].astype(F32)
                y_ref[rows, pl.ds(h * DV, DV)] = (o * rs * gn_ref[h] * (rr * _sig(rr))).astype(BF16)

    return pl.pallas_call(
        body, name=name, grid=(S // Tg,),
        in_specs=[
            pl.BlockSpec((Tg, 2 * QK + 2 * GV), lambda t: (t, 0)),
            pl.BlockSpec((Tg, LANES), lambda t: (t, 0)),
            pl.BlockSpec((LANES, QK), lambda t: (0, 0)),
            pl.BlockSpec((1, QK), lambda t: (0, 0)),
            pl.BlockSpec((HEADS, 1, DV), lambda t: (0, 0, 0)),
        ],
        out_specs=[
            pl.BlockSpec((Tg, GV), lambda t: (t, 0)),
            pl.BlockSpec((cb, HEADS, DV, DK), lambda t: (t, 0, 0, 0)),
        ],
        out_shape=[pltpu.HBM((S, GV), BF16), pltpu.HBM((S // CHUNK, HEADS, DV, DK), F32)],
        scratch_shapes=[pltpu.VMEM((HEADS, DV, DK), F32)],
        compiler_params=_cp(("arbitrary",), VMEM_BIG),
    )(*_hbm(pa, alow, wg, bgate, gnorm))


def _gla_bwd_call(pa, alow, wg, bgate, gnorm, states, dy, *, name):
    S = pa.shape[0]
    Tg = min(GLA_TILE, S)
    cb = Tg // CHUNK
    nt = S // Tg

    def rev(t):
        return nt - 1 - t

    def body(pa_ref, al_ref, wg_ref, bg_ref, gn_ref, st_ref, prev_ref, dy_ref,
             dpa_ref, dl_ref, dgn_ref, dbg_ref, carry, pbuf):
        t = pl.program_id(0)

        @pl.when(t == 0)
        def _():
            carry[...] = jnp.zeros_like(carry)
            dgn_ref[...] = jnp.zeros_like(dgn_ref)
            dbg_ref[...] = jnp.zeros_like(dbg_ref)

        logit = _dot(al_ref[...], wg_ref[...], NN) + bg_ref[...]
        ls = _logsig(logit) * (1.0 / 16.0)
        sneg = 1.0 / (1.0 + jnp.exp(logit))
        tri = _tri(True)
        upper = _tri(False)
        first_tile = rev(t) == 0
        heads = range(HEADS)

        w, decay, kd = [], [], []
        for c in range(cb):
            rows = pl.ds(c * CHUNK, CHUNK)
            cum = _exact_mask_dot(tri, ls[c * CHUNK:(c + 1) * CHUNK])
            tot = cum[CHUNK - 1:CHUNK]
            w.append(jnp.exp(tot - cum))
            decay.append(jnp.exp(tot))
            kd.append(pa_ref[rows, pl.ds(K_OFF, QK)].astype(F32) * w[c])

        dgn = [jnp.zeros((1, DV), F32) for _ in heads]
        for c in range(cb):
            rows = pl.ds(c * CHUNK, CHUNK)
            for h in heads:
                gn = gn_ref[h]
                qs = (pa_ref[rows, pl.ds(Q_OFF + h * DK, DK)].astype(F32) * Q_SCALE).astype(BF16)
                st16 = st_ref[c, h].astype(BF16)
                o = _dot(qs, st16, NT)
                rs = lax.rsqrt(jnp.mean(o * o, axis=-1, keepdims=True) + EPS)
                oh = o * rs
                rr = pa_ref[rows, pl.ds(R_OFF + h * DV, DV)].astype(F32)
                sr = _sig(rr)
                dyv = dy_ref[rows, pl.ds(h * DV, DV)].astype(F32)
                dpa_ref[rows, pl.ds(R_OFF + h * DV, DV)] = (
                    dyv * oh * gn * sr * (1.0 + rr * (1.0 - sr))).astype(BF16)
                don = dyv * (rr * sr)
                dgn[h] = dgn[h] + jnp.sum(don * oh, axis=0, keepdims=True)
                doh = don * gn
                do16 = (rs * (doh - oh * jnp.mean(doh * oh, axis=-1, keepdims=True))).astype(BF16)
                dpa_ref[rows, pl.ds(Q_OFF + h * DK, DK)] = (_dot(do16, st16, NN) * Q_SCALE).astype(BF16)
                pbuf[c, h] = _dot(do16, qs, TN)
        for h in heads:
            dgn_ref[h] += dgn[h]

        dkd = [[None] * HEADS for _ in range(cb)]
        ddecay = [[None] * HEADS for _ in range(cb)]
        for c in reversed(range(cb)):
            rows = pl.ds(c * CHUNK, CHUNK)
            for h in heads:
                lanes = slice(h * DK, (h + 1) * DK)
                gt = pbuf[c, h] + carry[h]
                gt16 = gt.astype(BF16)
                dkd[c][h] = _dot(pa_ref[rows, pl.ds(V_OFF + h * DV, DV)], gt16, NN)
                dpa_ref[rows, pl.ds(V_OFF + h * DV, DV)] = _dot(kd[c][:, lanes].astype(BF16), gt16, NT).astype(BF16)
                if c > 0:
                    st_prev = st_ref[c - 1, h]
                else:
                    st_prev = jnp.where(first_tile, 0.0, prev_ref[0, h])
                ddecay[c][h] = jnp.sum(gt * st_prev, axis=0, keepdims=True)
                carry[h] = gt * decay[c][:, lanes]

        dbg = jnp.zeros((1, QK), F32)
        for c in range(cb):
            rows = pl.ds(c * CHUNK, CHUNK)
            dkd_c = jnp.concatenate(dkd[c], axis=1)
            dpa_ref[rows, pl.ds(K_OFF, QK)] = (dkd_c * w[c]).astype(BF16)
            e = dkd_c * kd[c]
            dtot = jnp.sum(e, axis=0, keepdims=True) + jnp.concatenate(ddecay[c], axis=1) * decay[c]
            dls = dtot - _exact_mask_dot(upper, e)
            dlogit = dls * (1.0 / 16.0) * sneg[c * CHUNK:(c + 1) * CHUNK]
            dl_ref[rows, :] = dlogit.astype(BF16)
            dbg = dbg + jnp.sum(dlogit, axis=0, keepdims=True)
        dbg_ref[...] += dbg

    wide = 2 * QK + 2 * GV
    tiles = [((Tg, wide), BF16), ((cb + 1, HEADS, DV, DK), F32), ((Tg, GV), BF16), ((Tg, wide), BF16),
             ((Tg, QK), BF16)]
    resident = [((cb + 1, HEADS, DV, DK), F32)]
    return pl.pallas_call(
        body, name=name, grid=(nt,),
        in_specs=[
            pl.BlockSpec((Tg, wide), lambda t: (rev(t), 0)),
            pl.BlockSpec((Tg, LANES), lambda t: (rev(t), 0)),
            pl.BlockSpec((LANES, QK), lambda t: (0, 0)),
            pl.BlockSpec((1, QK), lambda t: (0, 0)),
            pl.BlockSpec((HEADS, 1, DV), lambda t: (0, 0, 0)),
            pl.BlockSpec((cb, HEADS, DV, DK), lambda t: (rev(t), 0, 0, 0)),
            pl.BlockSpec((1, HEADS, DV, DK), lambda t: (jnp.maximum(rev(t) * cb - 1, 0), 0, 0, 0)),
            pl.BlockSpec((Tg, GV), lambda t: (rev(t), 0)),
        ],
        out_specs=[
            pl.BlockSpec((Tg, wide), lambda t: (rev(t), 0)),
            pl.BlockSpec((Tg, QK), lambda t: (rev(t), 0)),
            pl.BlockSpec((HEADS, 1, DV), lambda t: (0, 0, 0)),
            pl.BlockSpec((1, QK), lambda t: (0, 0)),
        ],
        out_shape=[pltpu.HBM((S, wide), BF16), pltpu.HBM((S, QK), BF16),
                   jax.ShapeDtypeStruct((HEADS, 1, DV), F32), jax.ShapeDtypeStruct((1, QK), F32)],
        scratch_shapes=[pltpu.VMEM((HEADS, DV, DK), F32), pltpu.VMEM((cb, HEADS, DV, DK), F32)],
        compiler_params=_cp(("arbitrary",), _vmem_limit(tiles, resident)),
    )(*_hbm(pa, alow, wg, bgate, gnorm, states, states, dy))


SGU_TILE = 256


def _sgu_mask():
    r = lax.broadcasted_iota(jnp.int32, (SBLOCK, SBLOCK), 0)
    c = lax.broadcasted_iota(jnp.int32, (SBLOCK, SBLOCK), 1)
    return (c < CHUNK) | (r >= CHUNK)


def _ln_stats(vf):
    mu = jnp.mean(vf, axis=-1, keepdims=True)
    xc = vf - mu
    rs = lax.rsqrt(jnp.mean(xc * xc, axis=-1, keepdims=True) + EPS)
    return rs, xc * rs


def _sgu_fwd_call(ps, ln_g, ln_b, w_sp, b_sp, *, name):
    S = ps.shape[0]
    Ts = min(SGU_TILE, S)

    def body(ps_ref, lg_ref, lb_ref, w_ref, b_ref, y_ref):
        mask = _sgu_mask()
        for g in range(GROUPS):
            wm = jnp.where(mask, w_ref[g], 0.0).astype(BF16)
            for p in range(Ts // SBLOCK):
                rows = pl.ds(p * SBLOCK, SBLOCK)
                u = _gelu(ps_ref[rows, pl.ds(g * DG, DG)].astype(F32))
                _, xh = _ln_stats(_gelu(ps_ref[rows, pl.ds(D_MODEL + g * DG, DG)].astype(F32)))
                vn = xh * lg_ref[g] + lb_ref[g]
                mixed = _dot(wm, vn.astype(BF16), NN) + b_ref[g]
                y_ref[rows, pl.ds(g * DG, DG)] = (u * mixed).astype(BF16)

    full3 = lambda a, b, c: pl.BlockSpec((a, b, c), lambda t: (0, 0, 0))
    return pl.pallas_call(
        body, name=name, grid=(S // Ts,),
        in_specs=[pl.BlockSpec((Ts, 2 * D_MODEL), lambda t: (t, 0)),
                  full3(GROUPS, 1, DG), full3(GROUPS, 1, DG), full3(GROUPS, SBLOCK, SBLOCK), full3(GROUPS, SBLOCK, 1)],
        out_specs=pl.BlockSpec((Ts, D_MODEL), lambda t: (t, 0)),
        out_shape=pltpu.HBM((S, D_MODEL), BF16),
        compiler_params=_cp(("parallel",)),
    )(*_hbm(ps, ln_g, ln_b, w_sp, b_sp))


def _sgu_bwd_call(ps, ln_g, ln_b, w_sp, b_sp, dy, *, name):
    S = ps.shape[0]
    Ts = min(SGU_TILE, S)

    def body(ps_ref, lg_ref, lb_ref, w_ref, b_ref, dy_ref, ds_ref, dlg_ref, dlb_ref, dw_ref, db_ref):
        @pl.when(pl.program_id(0) == 0)
        def _():
            dlg_ref[...] = jnp.zeros_like(dlg_ref)
            dlb_ref[...] = jnp.zeros_like(dlb_ref)
            dw_ref[...] = jnp.zeros_like(dw_ref)
            db_ref[...] = jnp.zeros_like(db_ref)

        mask = _sgu_mask()
        for g in range(GROUPS):
            wm = jnp.where(mask, w_ref[g], 0.0).astype(BF16)
            lg = lg_ref[g]
            for p in range(Ts // SBLOCK):
                rows = pl.ds(p * SBLOCK, SBLOCK)
                su = ps_ref[rows, pl.ds(g * DG, DG)].astype(F32)
                sv = ps_ref[rows, pl.ds(D_MODEL + g * DG, DG)].astype(F32)
                u, du = _gelu_and_grad(su)
                gv, dgv = _gelu_and_grad(sv)
                rs, xh = _ln_stats(gv)
                vn16 = (xh * lg + lb_ref[g]).astype(BF16)
                mixed = _dot(wm, vn16, NN) + b_ref[g]
                dyv = dy_ref[rows, pl.ds(g * DG, DG)].astype(F32)
                ds_ref[rows, pl.ds(g * DG, DG)] = (dyv * mixed * du).astype(BF16)
                dmix = dyv * u
                dmix16 = dmix.astype(BF16)
                db_ref[g] += jnp.sum(dmix, axis=-1, keepdims=True)
                dw_ref[g] += jnp.where(mask, _dot(dmix16, vn16, NT), 0.0)
                dvn = _dot(wm, dmix16, TN)
                dlg_ref[g] += jnp.sum(dvn * xh, axis=0, keepdims=True)
                dlb_ref[g] += jnp.sum(dvn, axis=0, keepdims=True)
                dxh = dvn * lg
                dvf = rs * (dxh - jnp.mean(dxh, axis=-1, keepdims=True)
                            - xh * jnp.mean(dxh * xh, axis=-1, keepdims=True))
                ds_ref[rows, pl.ds(D_MODEL + g * DG, DG)] = (dvf * dgv).astype(BF16)

    full3 = lambda a, b, c: pl.BlockSpec((a, b, c), lambda t: (0, 0, 0))
    tiles = [((Ts, 2 * D_MODEL), BF16), ((Ts, D_MODEL), BF16), ((Ts, 2 * D_MODEL), BF16)]
    return pl.pallas_call(
        body, name=name, grid=(S // Ts,),
        in_specs=[pl.BlockSpec((Ts, 2 * D_MODEL), lambda t: (t, 0)),
                  full3(GROUPS, 1, DG), full3(GROUPS, 1, DG), full3(GROUPS, SBLOCK, SBLOCK), full3(GROUPS, SBLOCK, 1),
                  pl.BlockSpec((Ts, D_MODEL), lambda t: (t, 0))],
        out_specs=[pl.BlockSpec((Ts, 2 * D_MODEL), lambda t: (t, 0)),
                   full3(GROUPS, 1, DG), full3(GROUPS, 1, DG), full3(GROUPS, SBLOCK, SBLOCK), full3(GROUPS, SBLOCK, 1)],
        out_shape=[pltpu.HBM((S, 2 * D_MODEL), BF16),
                   jax.ShapeDtypeStruct((GROUPS, 1, DG), F32), jax.ShapeDtypeStruct((GROUPS, 1, DG), F32),
                   jax.ShapeDtypeStruct((GROUPS, SBLOCK, SBLOCK), F32),
                   jax.ShapeDtypeStruct((GROUPS, SBLOCK, 1), F32)],
        compiler_params=_cp(("arbitrary",), _vmem_limit(tiles, [])),
    )(*_hbm(ps, ln_g, ln_b, w_sp, b_sp, dy))


def _position():
    return lax.axis_index("x"), lax.axis_index("y"), lax.axis_index("c")


def _gather_copies(srcs, dsts, send_sems, recv_sems, local_sems):
    x, y, c = _position()
    me, sibling = (x, y, c), (x, y, 1 - c)
    chips = [(1 - x, y), (x, 1 - y), (1 - x, 1 - y)]
    n = len(srcs)

    def slab(a, block):
        px, py, pc = block
        return dsts[a].at[4 * px + 2 * py + pc]

    def copy(a, k, block, to, src=None):
        return pltpu.make_async_remote_copy(
            src_ref=slab(a, block) if src is None else src, dst_ref=slab(a, block),
            send_sem=send_sems.at[7 * a + k], recv_sem=recv_sems.at[7 * a + k], device_id=to, device_id_type=MESH)

    mine = [pltpu.make_async_copy(srcs[a], slab(a, me), local_sems.at[a]) for a in range(n)]
    for cp in mine:
        cp.start()
    first = []
    for a in range(n):
        first.append(copy(a, 0, me, sibling, src=srcs[a]))
        first += [copy(a, 1 + j, me, (*chip, c), src=srcs[a]) for j, chip in enumerate(chips)]
    for cp in first:
        cp.start()
    passed = []
    for j, chip in enumerate(chips):
        for a in range(n):
            copy(a, 1 + j, (*chip, c), me).wait_recv()
            cp = copy(a, 4 + j, (*chip, c), sibling)
            cp.start()
            passed.append(cp)
    for a in range(n):
        copy(a, 0, sibling, me).wait_recv()
        for j, chip in enumerate(chips):
            copy(a, 4 + j, (*chip, 1 - c), me).wait_recv()
    for cp in first + passed:
        cp.wait_send()
    for cp in mine:
        cp.wait()


def _gather_copies_relayed(srcs, dsts, send_sems, recv_sems, local_sems, waves=1):
    x, y, c = _position()
    me, sibling = (x, y, c), (x, y, 1 - c)
    x_chip, y_chip, d_chip = (1 - x, y), (x, 1 - y), (1 - x, 1 - y)
    south = c == 0
    relay_from = (jnp.where(south, x, 1 - x), jnp.where(south, 1 - y, y), c)
    relay_to = (jnp.where(south, 1 - x, x), jnp.where(south, y, 1 - y), c)
    n = len(srcs)

    def cut(a, w):
        rows = srcs[a].shape[0]
        step = (rows // waves) // 16 * 16
        if waves == 1 or step == 0:
            return pl.ds(0, rows) if w == 0 else None
        return pl.ds(w * step, step if w < waves - 1 else rows - w * step)

    def slab(a, block, w):
        px, py, pc = block
        return dsts[a].at[4 * px + 2 * py + pc, cut(a, w)]

    def copy(a, k, w, block, to, own=False):
        sem = (7 * a + k) * waves + w
        return pltpu.make_async_remote_copy(
            src_ref=srcs[a].at[cut(a, w)] if own else slab(a, block, w), dst_ref=slab(a, block, w),
            send_sem=send_sems.at[sem], recv_sem=recv_sems.at[sem], device_id=to, device_id_type=MESH)

    pieces = [(a, w) for w in range(waves) for a in range(n) if cut(a, w) is not None]
    mine = [pltpu.make_async_copy(srcs[a], dsts[a].at[4 * x + 2 * y + c], local_sems.at[a]) for a in range(n)]
    for cp in mine:
        cp.start()
    sent = []
    for a, w in pieces:
        sent += [copy(a, 0, w, me, sibling, own=True), copy(a, 1, w, me, (*x_chip, c), own=True),
                 copy(a, 2, w, me, (*y_chip, c), own=True)]
    for cp in sent:
        cp.start()
    for a, w in pieces:
        copy(a, 1, w, (*x_chip, c), me).wait_recv()
        copy(a, 2, w, (*y_chip, c), me).wait_recv()
        later = [copy(a, 3, w, relay_from, relay_to), copy(a, 4, w, (*x_chip, c), sibling),
                 copy(a, 5, w, (*y_chip, c), sibling)]
        for cp in later:
            cp.start()
        sent += later
    for a, w in pieces:
        copy(a, 3, w, (*d_chip, c), me).wait_recv()
        cp = copy(a, 6, w, (*d_chip, c), sibling)
        cp.start()
        sent.append(cp)
    for a, w in pieces:
        copy(a, 0, w, sibling, me).wait_recv()
        for k, chip in ((4, x_chip), (5, y_chip), (6, d_chip)):
            copy(a, k, w, (*chip, 1 - c), me).wait_recv()
    for cp in sent:
        cp.wait_send()
    for cp in mine:
        cp.wait()


def _all_gather_hbm(shards, *, name, waves=1):
    n = len(shards)

    def body(*refs):
        srcs, dsts = refs[:n], refs[n:2 * n]
        send_sems, recv_sems, local_sems = refs[2 * n:]
        _gather_copies_relayed(srcs, dsts, send_sems, recv_sems, local_sems, waves=waves)

    return pl.pallas_call(
        body, name=name,
        in_specs=[ANY] * n, out_specs=[ANY] * n,
        out_shape=[jax.ShapeDtypeStruct((N_DEV, *s.shape), s.dtype) for s in shards],
        scratch_shapes=_comm_sems(n, waves),
    )(*shards)


FLIPS = [(fx, fy, fc) for fx in (0, 1) for fy in (0, 1) for fc in (0, 1)][1:]


def _scatter_copies(srcs, dsts, send_sems, recv_sems, local_sems, waves=1):
    n = len(srcs)
    x, y, c = _position()
    me = 4 * x + 2 * y + c
    mine = [pltpu.make_async_copy(srcs[a].at[me], dsts[a].at[me], local_sems.at[a]) for a in range(n)]
    for cp in mine:
        cp.start()
    peers = []
    for fx, fy, fc in FLIPS:
        tx = 1 - x if fx else x
        ty = 1 - y if fy else y
        tc = 1 - c if fc else c
        peers.append(((tx, ty, tc), 4 * tx + 2 * ty + tc))
    copies = []
    for w in range(waves):
        wave = []
        for k, (peer_id, peer) in enumerate(peers):
            for a in range(n):
                rows = srcs[a].shape[1]
                step = rows if waves == 1 else (rows // waves) // 16 * 16
                r0 = w * step
                cut = pl.ds(r0, step if w < waves - 1 else rows - r0)
                sem = (7 * a + k) * waves + w
                cp = pltpu.make_async_remote_copy(
                    src_ref=srcs[a].at[peer, cut], dst_ref=dsts[a].at[me, cut],
                    send_sem=send_sems.at[sem], recv_sem=recv_sems.at[sem],
                    device_id=peer_id, device_id_type=MESH)
                cp.start()
                wave.append(cp)
        for cp in wave:
            cp.wait_send()
        copies += wave
    for cp in copies:
        cp.wait_recv()
    for cp in mine:
        cp.wait()


def _pair_copies(srcs, dsts, send_sems, recv_sems, local_sems):
    x, y, c = _position()
    copies = [pltpu.make_async_remote_copy(
        src_ref=srcs[a], dst_ref=dsts[a], send_sem=send_sems.at[a], recv_sem=recv_sems.at[a],
        device_id=(x, y, 1 - c), device_id_type=MESH) for a in range(len(srcs))]
    for cp in copies:
        cp.start()
    for cp in copies:
        cp.wait()


def _chip_scatter_copies(srcs, dsts, send_sems, recv_sems, local_sems, waves=1):
    n = len(srcs)
    x, y, c = _position()
    here = 2 * x + y
    mine = [pltpu.make_async_copy(srcs[a].at[here], dsts[a].at[here], local_sems.at[a]) for a in range(n)]
    for cp in mine:
        cp.start()
    peers = []
    for fx, fy in ((1, 0), (0, 1), (1, 1)):
        tx = 1 - x if fx else x
        ty = 1 - y if fy else y
        peers.append(((tx, ty, c), 2 * tx + ty))
    copies = []
    for w in range(waves):
        wave = []
        for k, (peer_id, peer) in enumerate(peers):
            for a in range(n):
                rows = srcs[a].shape[1]
                step = rows if waves == 1 else (rows // waves) // 16 * 16
                r0 = w * step
                cut = pl.ds(r0, step if w < waves - 1 else rows - r0)
                sem = (3 * a + k) * waves + w
                cp = pltpu.make_async_remote_copy(
                    src_ref=srcs[a].at[peer, cut], dst_ref=dsts[a].at[here, cut],
                    send_sem=send_sems.at[sem], recv_sem=recv_sems.at[sem],
                    device_id=peer_id, device_id_type=MESH)
                cp.start()
                wave.append(cp)
        for cp in wave:
            cp.wait_send()
        copies += wave
    for cp in copies:
        cp.wait_recv()
    for cp in mine:
        cp.wait()


def _comm_sems(n, waves=1):
    return [pltpu.SemaphoreType.DMA((7 * n * waves,)), pltpu.SemaphoreType.DMA((7 * n * waves,)),
            pltpu.SemaphoreType.DMA((n,))]


def _handshake(peers):
    barrier = pltpu.get_barrier_semaphore()
    for peer in peers:
        pl.semaphore_signal(barrier, inc=1, device_id=peer, device_id_type=MESH)
    pl.semaphore_wait(barrier, len(peers))


def _sequencer_call(arrays, out_types, copies_fn, peers_fn, *, name, collective_id, waves=1):
    n = len(arrays)
    srcs = [jax.new_ref(a, memory_space=pltpu.MemorySpace.HBM) for a in arrays]
    dsts = [jax.empty_ref(t, memory_space=pltpu.MemorySpace.HBM) for t in out_types]
    extra = {} if waves == 1 else {"waves": waves}

    @pl.kernel(mesh=plsc.ScalarSubcoreMesh(axis_name="sequencer", num_cores=1), name=name,
               scratch_types=_comm_sems(n, waves), compiler_params=pltpu.CompilerParams(collective_id=collective_id))
    def launch(send_sems, recv_sems, local_sems):
        _handshake(peers_fn())
        copies_fn(srcs, dsts, send_sems, recv_sems, local_sems, **extra)

    launch()
    return [d[...] for d in dsts]


def _all_other_devices():
    x, y, c = _position()
    return [(1 - x if fx else x, 1 - y if fy else y, 1 - c if fc else c) for fx, fy, fc in FLIPS]


def _gather_relay_peers():
    x, y, c = _position()
    return [(x, y, 1 - c), (1 - x, y, c), (x, 1 - y, c)]


def _gather_peers():
    x, y, c = _position()
    return [(x, y, 1 - c), (1 - x, y, c), (x, 1 - y, c), (1 - x, 1 - y, c)]


def _small_gather_async(shards, *, name, collective_id):
    return _sequencer_call(shards, [jax.ShapeDtypeStruct((N_DEV, *s.shape), s.dtype) for s in shards],
                           _gather_copies, _gather_peers, name=name, collective_id=collective_id)


def _sibling():
    x, y, c = _position()
    return [(x, y, 1 - c)]


def _same_core_of_other_chips():
    x, y, c = _position()
    return [(1 - x, y, c), (x, 1 - y, c), (1 - x, 1 - y, c)]


def _pair_exchange_async(parts, *, name, collective_id):
    return _sequencer_call(parts, [jax.ShapeDtypeStruct(p.shape, p.dtype) for p in parts],
                           _pair_copies, _sibling, name=name, collective_id=collective_id)


def _chip_scatter_async(parts, *, name, collective_id, waves=1):
    return _sequencer_call(parts, [jax.ShapeDtypeStruct(p.shape, p.dtype) for p in parts],
                           _chip_scatter_copies, _same_core_of_other_chips, name=name, collective_id=collective_id,
                           waves=waves)


def _pair_sum_call(mine, theirs, *, name, tw=D_MODEL):
    n, r, w = mine.shape

    def body(a_ref, b_ref, o_ref):
        o_ref[...] = (a_ref[...].astype(F32) + b_ref[...].astype(F32)).astype(o_ref.dtype)

    spec = pl.BlockSpec((None, r, tw), lambda i, j: (i, 0, j))
    return pl.pallas_call(
        body, name=name, grid=(n, w // tw), in_specs=[spec, spec], out_specs=spec,
        out_shape=pltpu.HBM(mine.shape, mine.dtype),
        compiler_params=_cp(("parallel", "parallel")),
    )(*_hbm(mine, theirs))


def _scatter_blocks_async(parts, *, name, collective_id, waves=1):
    return _sequencer_call(parts, [jax.ShapeDtypeStruct(p.shape, p.dtype) for p in parts],
                           _scatter_copies, _all_other_devices, name=name, collective_id=collective_id, waves=waves)


def _all_gather_async(shards, *, name, collective_id):
    return _sequencer_call(shards, [jax.ShapeDtypeStruct((N_DEV, *s.shape), s.dtype) for s in shards],
                           _gather_copies_relayed, _gather_relay_peers, name=name, collective_id=collective_id)


def _adamw_math(w, g, m, v):
    m = ADAM_B1 * m + (1.0 - ADAM_B1) * g
    v = ADAM_B2 * v + (1.0 - ADAM_B2) * (g * g)
    m_hat = m / (1.0 - ADAM_B1 ** ADAM_STEP)
    v_hat = v / (1.0 - ADAM_B2 ** ADAM_STEP)
    delta = -ADAM_LR * (m_hat / (jnp.sqrt(v_hat) + ADAM_EPS) + ADAM_WD * w)
    return delta, m, v


def _adamw_reduce_call(recv, w, m, v, *, name, T=128):
    R, W = w.shape
    n_parts = recv.shape[0]
    if R % T == 0:
        tr, tw = T, W
    elif (R // 2) % 16 == 0:
        tr, tw = R // 2, W
    else:
        tr, tw = R, 2 * LANES

    def body(p_ref, w_ref, m_ref, v_ref, g_out, d_out, m_out, v_out):
        g = p_ref[0].astype(F32)
        for d in range(1, n_parts):
            g = g + p_ref[d].astype(F32)
        g_out[...] = g
        d_out[...], m_out[...], v_out[...] = _adamw_math(w_ref[...], g, m_ref[...], v_ref[...])

    row = pl.BlockSpec((tr, tw), lambda i, j: (i, j))
    out = pltpu.HBM((R, W), F32)
    return pl.pallas_call(
        body, name=name, grid=(R // tr, W // tw),
        in_specs=[pl.BlockSpec((n_parts, tr, tw), lambda i, j: (0, i, j)), row, row, row],
        out_specs=[row] * 4, out_shape=[out] * 4,
        compiler_params=_cp(("parallel", "parallel"), VMEM_BIG),
    )(*_hbm(recv, w, m, v))


SMALL_EARLY = (("b_gate", 8), ("w_spatial", 512), ("b_spatial", 8), ("norm_post_mix", 8), ("norm_pre_ffn", 8),
               ("norm_post_ffn", 8), ("w_gate_up", 128), ("gla_norm", 64), ("sgu_ln_g", 64), ("sgu_ln_b", 64))
SMALL_EARLY_AT = {}
for _name, _rows in SMALL_EARLY:
    SMALL_EARLY_AT[_name] = sum(r for _, r in SMALL_EARLY[:len(SMALL_EARLY_AT)])
SMALL_EARLY_ROWS = sum(r for _, r in SMALL_EARLY)
SMALL_LATE_ROWS = 16


def _small_early_rows(grads):
    def rows(a, n_rows):
        a = a.reshape(-1, LANES)
        return jnp.pad(a, ((0, n_rows - a.shape[0]), (0, 0)))

    def device_major(a, n_rows):
        r, c = a.shape[0], a.shape[1] // N_DEV
        a = a.reshape(r, N_DEV, c).transpose(1, 0, 2)
        a = jnp.pad(a, ((0, 0), (0, n_rows // N_DEV - r), (0, LANES - c)))
        return a.reshape(n_rows, LANES)

    pieces = []
    for name, n_rows in SMALL_EARLY:
        g = grads[name]
        if name in SMALL_SHARDED:
            pieces.append(device_major(g.reshape(g.shape[0], -1) if g.ndim == 2 else g.reshape(g.shape[0], g.shape[-1]), n_rows))
        else:
            pieces.append(rows(g, n_rows))
    return jnp.concatenate(pieces, axis=0)


def _small_update_call(got_early, got_late, w, m, v, *, name):
    names = list(SMALL)
    n_p = len(names)
    E = SMALL_EARLY_ROWS

    def body(early_ref, late_ref, *rest):
        w_refs, m_refs, v_refs = [dict(zip(names, rest[i * n_p:(i + 1) * n_p])) for i in range(3)]
        outs, tot = rest[3 * n_p:-1], rest[-1]
        loss_out = outs[0]
        g_out, d_out, m_out, v_out = [dict(zip(names, outs[1 + i * n_p:1 + (i + 1) * n_p])) for i in range(4)]
        acc, acc_late = early_ref[0], late_ref[0]
        for d in range(1, N_DEV):
            acc, acc_late = acc + early_ref[d], acc_late + late_ref[d]
        tot[0:E, :] = acc
        tot[E:E + SMALL_LATE_ROWS, :] = acc_late
        loss_out[...] = tot[E + 8:E + 9, :]

        x, y, c = _position()
        me = 4 * x + 2 * y + c

        def update(name, g, ix):
            g_out[name][ix] = g
            d_out[name][ix], m_out[name][ix], v_out[name][ix] = _adamw_math(
                w_refs[name][ix], g, m_refs[name][ix], v_refs[name][ix])

        for name in names:
            shape = w[name].shape
            if name in SMALL_SHARDED:
                per_dev = dict(SMALL_EARLY)[name] // N_DEV
                at = pl.multiple_of(SMALL_EARLY_AT[name] + me * per_dev, 8)
                g = tot[pl.ds(at, per_dev), :]
                update(name, g[:shape[1], :shape[2]], (0,))
            elif name == "w_spatial":
                for grp in range(GROUPS):
                    at = SMALL_EARLY_AT[name] + grp * SBLOCK
                    update(name, tot[at:at + SBLOCK, :], (0, grp))
            elif name == "b_spatial":
                at = SMALL_EARLY_AT[name]
                update(name, tot[at:at + GROUPS, :], (0,))
            else:
                at = E if name == "norm_pre_mix" else SMALL_EARLY_AT[name]
                for k in range(shape[1] // LANES):
                    update(name, tot[at + k:at + k + 1, :], (slice(None), pl.ds(k * LANES, LANES)))

    state = [s[n] for s in (w, m, v) for n in names]
    out_shapes = [jax.ShapeDtypeStruct((1, LANES), F32)] + [jax.ShapeDtypeStruct(w[n].shape, F32) for n in names] * 4
    outs = pl.pallas_call(
        body, name=name,
        in_specs=[VMEM_SPEC] * (2 + len(state)), out_specs=[VMEM_SPEC] * len(out_shapes), out_shape=out_shapes,
        scratch_shapes=[pltpu.VMEM((E + SMALL_LATE_ROWS, LANES), F32)],
    )(got_early, got_late, *state)
    per_name = {n: tuple(outs[1 + i * n_p + j] for i in range(4)) for j, n in enumerate(names)}
    return outs[0], per_name


def _tile_rows(n_elems):
    return -(-n_elems // (8 * LANES)) * 8


def _pack_rows(parts, rows):
    pieces = []
    for p in parts:
        q = p.reshape(-1, LANES)
        pieces.append(jnp.pad(q, ((0, _tile_rows(p.size) - q.shape[0]), (0, 0))))
    buf = jnp.concatenate(pieces, axis=0)
    return jnp.pad(buf, ((0, rows - buf.shape[0]), (0, 0)))


def _in_w_blocks_call(a, live, finished, *, name, tk=TOKEN_TILE):
    S = a.shape[0]
    tk = min(tk, S)
    steps = S // tk
    n_live = len(live)
    n_chips = N_DEV // 2
    out_shape = (n_chips, IN_BLK, D_MODEL)

    def body(a_ref, *rest):
        live_refs, done_refs = rest[:n_live], rest[n_live:n_live + len(finished)]
        keep_ref, send_ref = rest[n_live + len(finished):n_live + len(finished) + 2]
        accs = rest[n_live + len(finished) + 2:]
        k = pl.program_id(0)

        @pl.when(k == 0)
        def _():
            for acc in accs:
                acc[...] = jnp.zeros_like(acc)

        av = a_ref[...]
        for src, acc in zip(live_refs, accs):
            for m0 in range(0, src.shape[1], 1024):
                m1 = min(src.shape[1], m0 + 1024)
                acc[m0:m1, :] += _dot(src[:, m0:m1], av, TN)

        @pl.when(k == steps - 1)
        def _():
            groups = [(acc, cols) for acc, (_, cols) in zip(accs, live)]
            groups += [(ref, cols) for ref, (_, cols) in zip(done_refs, finished)]
            core = lax.axis_index("c")

            def cut(d, out_ref):
                lo, hi = IN_BLK * d, IN_BLK * (d + 1)
                for ref, (c0, c1) in groups:
                    s0, e0 = max(lo, c0), min(hi, c1)
                    if s0 < e0:
                        out_ref[d // 2, s0 - lo:e0 - lo, :] = ref[s0 - c0:e0 - c0, :].astype(BF16)

            for d in range(N_DEV):
                pl.when(core == d % 2)(lambda d=d: cut(d, keep_ref))
                pl.when(core != d % 2)(lambda d=d: cut(d, send_ref))

    acc_shapes = [(arr.shape[1], D_MODEL) for arr, _ in live]
    tiles = [((tk, D_MODEL), BF16)] + [((tk, arr.shape[1]), BF16) for arr, _ in live]
    resident = ([(arr.shape, arr.dtype) for arr, _ in finished] + [(out_shape, BF16)] * 2
                + [(s, F32) for s in acc_shapes])
    return pl.pallas_call(
        body, name=name, grid=(steps,),
        in_specs=[pl.BlockSpec((tk, D_MODEL), lambda k: (k, 0))]
        + [pl.BlockSpec((tk, arr.shape[1]), lambda k: (k, 0)) for arr, _ in live]
        + [_res(arr.shape) for arr, _ in finished],
        out_specs=[_acc(out_shape)] * 2,
        out_shape=[pltpu.HBM(out_shape, BF16)] * 2,
        scratch_shapes=[pltpu.VMEM(s, F32) for s in acc_shapes],
        compiler_params=_cp(("arbitrary",), _vmem_limit(tiles, resident)),
    )(*_hbm(a, *[arr for arr, _ in live], *[arr for arr, _ in finished]))


def _local_step(x, target, W, scatter, finish, pair, chip_scatter):
    g1, g2, g3, g4 = [W[n].reshape(1, D_MODEL) for n in ("norm_pre_mix", "norm_post_mix", "norm_pre_ffn", "norm_post_ffn")]
    wfi, wfo = W["w_ffn_in"].reshape(2 * D_FF, D_MODEL), W["w_ffn_out"]
    wg = jnp.pad(W["w_gate_up"], ((0, LANES - RANK), (0, 0))).astype(BF16)
    bgate = W["b_gate"].reshape(1, QK)
    gnorm = W["gla_norm"].reshape(HEADS, 1, DV)
    ln_g = W["sgu_ln_g"].reshape(GROUPS, 1, DG)
    ln_b = W["sgu_ln_b"].reshape(GROUPS, 1, DG)
    w_sp = W["w_spatial"]
    b_sp = W["b_spatial"].reshape(GROUPS, SBLOCK, 1)

    a, pa, alow, ps, pg, w_in_t = _in_proj_call(x, g1, W["w_in_blocks"], name="in_proj")
    y_gla, states = _gla_fwd_call(pa, alow, wg, bgate, gnorm, name="gla_fwd")
    y_sgu = _sgu_fwd_call(ps, ln_g, ln_b, w_sp, b_sp, name="sgu_fwd")
    t1, t2, merged, mix, x1, h = _mixer_tail_call(y_gla, y_sgu, pg, x, W["w_branch_gla"], W["w_branch_sgu"],
                                                  W["w_out"], g2, g3, name="mixer_tail")
    gu, z = _ffn_in_call(h, wfi, name="ffn_in")
    loss, dx2, dy, dg4 = _ffn_out_loss_call(z, wfo, x1, target, g4, name="ffn_out_loss")

    grads = {"norm_post_ffn": dg4}
    done = {}
    dgu = _ffn_out_bwd_call(dy, wfo, gu, name="ffn_out_bwd")
    dw_ffn_out = _weight_grads_call([z, dy], [(0, 1)], name="d_ffn_out_w")[0].reshape(N_DEV, D_FF // N_DEV, D_MODEL)
    dw_ffn_in = _ffn_in_grad_call(h, dgu, name="d_ffn_in_w").reshape(N_DEV, FF_BLK, D_MODEL)
    dgu, dw_ffn_out, dw_ffn_in = lax.optimization_barrier((dgu, dw_ffn_out, dw_ffn_in))
    ffn_received = scatter(("w_ffn_out", "w_ffn_in"), [dw_ffn_out, dw_ffn_in])
    dx1, dmix, grads["norm_pre_ffn"], grads["norm_post_mix"] = _ffn_in_bwd_call(dgu, wfi, dx2, x1, mix, g3, g2, name="ffn_in_bwd")
    dt1, dt2, dpg, dy_gla, dy_sgu = _mixer_bwd_call(dmix, t1, t2, pg, W["w_out"], W["w_branch_gla"], W["w_branch_sgu"],
                                                    name="mixer_bwd")
    rows = D_MODEL // N_DEV
    mixer_grads = _weight_grads_call([merged, dmix, y_gla, dt1, y_sgu, dt2], [(0, 1), (2, 3), (4, 5)], name="d_mixer_w")
    dy_gla, dy_sgu, mixer_grads = lax.optimization_barrier((dy_gla, dy_sgu, mixer_grads))
    mixer_received = scatter(("w_out", "w_branch_gla", "w_branch_sgu"),
                             [g.reshape(N_DEV, rows, D_MODEL) for g in mixer_grads])
    dps, dlg, dlb, dwsp, dbsp = _sgu_bwd_call(ps, ln_g, ln_b, w_sp, b_sp, dy_sgu, name="sgu_bwd")
    dw_s, dw_g = _weight_grads_call([a, dps, dpg], [(1, 0), (2, 0)], name="d_in_w_sgu_gates")
    dy_gla, dw_s, dw_g = lax.optimization_barrier((dy_gla, dw_s, dw_g))
    dpa, dlogit, dgn, dbg = _gla_bwd_call(pa, alow, wg, bgate, gnorm, states, dy_gla, name="gla_bwd")
    ffn_received, mixer_received, dpa, dlogit = lax.optimization_barrier((ffn_received, mixer_received, dpa, dlogit))
    dlow = _mm(dlogit, wg, "nt", BF16, name="d_gate_up_x")
    keep, send = _in_w_blocks_call(a, [(dpa, A_COLS), (dlow, LOW_COLS)], [(dw_s, S_COLS), (dw_g, G_COLS)],
                                   name="d_in_w_blocks")
    ffn_received, mixer_received, send = lax.optimization_barrier((ffn_received, mixer_received, send))
    from_sibling = pair(send)
    done.update({**finish(ffn_received), **finish(mixer_received)})
    dwg = _mm(alow, dlogit, "tn", F32, name="d_gate_up_w")
    done, dwg, keep = lax.optimization_barrier((done, dwg, keep))
    chip_sum = _pair_sum_call(keep, from_sibling, name="pair_sum_w_in")
    chip_sum, dpa = lax.optimization_barrier((chip_sum, dpa))
    in_received = chip_scatter(chip_sum)
    grad_x, grads["norm_pre_mix"] = _in_proj_bwd_call(dpa, dlow, dps, dpg, w_in_t, x, dx1, g1, name="in_proj_bwd")

    grads["w_gate_up"] = dwg[:RANK]
    grads["b_gate"] = dbg
    grads["gla_norm"] = dgn
    grads["sgu_ln_g"] = dlg
    grads["sgu_ln_b"] = dlb
    grads["w_spatial"] = dwsp
    grads["b_spatial"] = dbsp
    done.update(finish({"w_in": in_received}))
    return loss, grad_x, grads, done


WEIGHTS = ("norm_pre_mix", "w_in", "w_gate_up", "b_gate", "gla_norm", "sgu_ln_g", "sgu_ln_b", "w_spatial",
           "b_spatial", "w_branch_gla", "w_branch_sgu", "w_out", "norm_post_mix", "norm_pre_ffn", "w_ffn_in",
           "w_ffn_out", "norm_post_ffn")
BIG = ("w_in", "w_branch_gla", "w_branch_sgu", "w_out", "w_ffn_in", "w_ffn_out")
MIXER = ("w_branch_gla", "w_branch_sgu", "w_out")
FFN = ("w_ffn_in", "w_ffn_out")
COLUMN_SHARDED = ("w_in", "w_ffn_in")
LATE_SCATTER_WAVES = 4
GATHER_WAVES = 4
SMALL = tuple(n for n in WEIGHTS if n not in BIG)
SMALL_SHARDED = ("w_gate_up", "gla_norm", "sgu_ln_g", "sgu_ln_b")
SMALL_GATHER_ROWS = 32


def kernel(x, norm_pre_mix, w_in, w_gate_up, b_gate, gla_norm, sgu_ln_g, sgu_ln_b, w_spatial, b_spatial, w_branch_gla, w_branch_sgu, w_out, norm_post_mix, norm_pre_ffn, w_ffn_in, w_ffn_out, norm_post_ffn, loss_target, m_norm_pre_mix, m_w_in, m_w_gate_up, m_b_gate, m_gla_norm, m_sgu_ln_g, m_sgu_ln_b, m_w_spatial, m_b_spatial, m_w_branch_gla, m_w_branch_sgu, m_w_out, m_norm_post_mix, m_norm_pre_ffn, m_w_ffn_in, m_w_ffn_out, m_norm_post_ffn, v_norm_pre_mix, v_w_in, v_w_gate_up, v_b_gate, v_gla_norm, v_sgu_ln_g, v_sgu_ln_b, v_w_spatial, v_b_spatial, v_w_branch_gla, v_w_branch_sgu, v_w_out, v_norm_post_mix, v_norm_pre_ffn, v_w_ffn_in, v_w_ffn_out, v_norm_post_ffn):
    given = dict(locals())
    def local(a, n):
        return a[0].T if n in COLUMN_SHARDED else a[0]

    w = {n: local(given[n], n) for n in WEIGHTS}
    m = {n: local(given["m_" + n], n) for n in WEIGHTS}
    v = {n: local(given["v_" + n], n) for n in WEIGHTS}
    xs, target = x[0], loss_target[0]

    small_shard = _pack_rows([w[n] for n in SMALL_SHARDED], SMALL_GATHER_ROWS)
    first = _all_gather_hbm([w["w_in"].astype(BF16), small_shard], name="gather_w_in", waves=GATHER_WAVES)
    rest, first = lax.optimization_barrier(([w[n].astype(BF16) for n in MIXER + FFN], first))
    rest_blocks = _all_gather_async(rest, name="gather_rest", collective_id=1)
    W = {n: w[n] for n in SMALL if n not in SMALL_SHARDED}
    blocks = dict(zip(MIXER + FFN, rest_blocks))
    for n in ("w_branch_gla", "w_branch_sgu", "w_out", "w_ffn_out"):
        W[n] = blocks[n].reshape(-1, D_MODEL)
    W["w_ffn_in"] = blocks["w_ffn_in"]
    W["w_in_blocks"] = first[0]
    small_blocks = first[1]
    off = 0
    for n in SMALL_SHARDED:
        r, c = w[n].shape
        blk = small_blocks[:, off:off + r * c // LANES].reshape(N_DEV, r, c)
        W[n] = blk.transpose(1, 0, 2).reshape(r, N_DEV * c)
        off += _tile_rows(r * c)

    scatter_ids = iter((3, 4))

    def scatter(names, parts):
        got = _scatter_blocks_async(parts, name="scatter_" + "_".join(names), collective_id=next(scatter_ids))
        return dict(zip(names, got))

    def finish(received):
        return {n: _adamw_reduce_call(r, w[n], m[n], v[n], name="adamw_" + n) for n, r in received.items()}

    def pair(part):
        return _pair_exchange_async([part], name="pair_w_in", collective_id=5)[0]

    def chip_scatter(part):
        return _chip_scatter_async([part], name="scatter_w_in", collective_id=6, waves=LATE_SCATTER_WAVES)[0]

    loss_part, grad_x, grads, big_done = _local_step(xs, target, W, scatter, finish, pair, chip_scatter)

    late = jnp.concatenate([grads["norm_pre_mix"].reshape(8, LANES), jnp.broadcast_to(loss_part, (8, LANES))], axis=0)
    gathered = _small_gather_async([_small_early_rows(grads), late], name="gather_small", collective_id=7)
    gathered, big_done["w_in"] = lax.optimization_barrier((gathered, big_done["w_in"]))
    loss_row, small_done = _small_update_call(
        *gathered, *[{n: given[prefix + n] for n in SMALL} for prefix in ("", "m_", "v_")], name="small_update")

    def pick(i):
        return [small_done[n][i] if n in SMALL else (big_done[n][i].T if n in COLUMN_SHARDED else big_done[n][i])[None]
                for n in WEIGHTS]

    return (loss_row[0, 0], grad_x[None], *pick(0), *pick(1), *pick(2), *pick(3))
```

```python
import jax
import jax.numpy as jnp
from jax import lax
from jax.experimental import pallas as pl
from jax.experimental.pallas import tpu as pltpu
from jax.experimental.pallas import tpu_sc as plsc

F32 = jnp.float32
BF16 = jnp.bfloat16

D_MODEL = 1024
N_DEV = 8
CHUNK = 64
HEADS = 4
DK = 128
DV = 256
QK = HEADS * DK
GV = HEADS * DV
RANK = 16
GROUPS = 4
SBLOCK = 128
DG = 256
D_FF = 2816
FF_BLK = 704
EPS = 1e-6
Q_SCALE = DK ** -0.5
LANES = 128
VMEM_BIG = 48 * 1024 * 1024

D_IN = 7184
IN_BLK = 898
A_COLS = (0, 3072)
LOW_COLS = (3072, 3088)
S_COLS = (3088, 5136)
G_COLS = (5136, 7184)

ADAM_LR = 0.001
ADAM_B1 = 0.9
ADAM_B2 = 0.999
ADAM_EPS = 1e-08
ADAM_WD = 0.01
ADAM_STEP = 10

MESH = pl.DeviceIdType.MESH
ANY = pl.BlockSpec(memory_space=pl.ANY)
VMEM_SPEC = pl.BlockSpec(memory_space=pltpu.VMEM)


def _cp(sem=None, vmem=None):
    return pltpu.CompilerParams(dimension_semantics=sem, vmem_limit_bytes=vmem)


def _hbm(*arrays):
    return [pltpu.with_memory_space_constraint(a, pltpu.HBM) for a in arrays]


def _sig(x):
    return 0.5 * jnp.tanh(0.5 * x) + 0.5


GELU_C = 0.7978845608028654
GELU_A = 0.044715


def _gelu(x):
    t = jnp.tanh((GELU_C * x) * (1.0 + GELU_A * (x * x)))
    return (0.5 * x) * (1.0 + t)


def _gelu_and_grad(x):
    x2 = x * x
    t = jnp.tanh((GELU_C * x) * (1.0 + GELU_A * x2))
    one_t = 1.0 + t
    hx = 0.5 * x
    grad = 0.5 * one_t + (hx * (1.0 - t * t)) * (GELU_C + (3.0 * GELU_A * GELU_C) * x2)
    return hx * one_t, grad


def _logsig(x):
    return jnp.minimum(x, 0.0) - jnp.log1p(jnp.exp(-jnp.abs(x)))


def _dot(a, b, dims):
    return lax.dot_general(a, b, (dims, ((), ())), preferred_element_type=F32)


NN = ((1,), (0,))
NT = ((1,), (1,))
TN = ((0,), (0,))


def _exact_mask_dot(mask_bf16, x):
    hi = x.astype(BF16)
    r1 = x - hi.astype(F32)
    mid = r1.astype(BF16)
    lo = (r1 - mid.astype(F32)).astype(BF16)
    return _dot(mask_bf16, hi, NN) + _dot(mask_bf16, mid, NN) + _dot(mask_bf16, lo, NN)


def _pick_tile(dim, target):
    if dim <= target:
        return dim
    best = None
    for t in range(LANES, int(1.4 * target) + 1, LANES):
        if dim % t == 0:
            best = t
    assert best is not None, (dim, target)
    return best


def _mm_call(a, b, *, name, grid, a_spec, b_spec, o_spec, out_shape, dims, acc_shape):
    nk = grid[2]

    def body(a_ref, b_ref, o_ref, *acc):
        part = _dot(a_ref[...], b_ref[...], dims)
        if nk == 1:
            o_ref[...] = part.astype(o_ref.dtype)
        else:
            acc_ref = acc[0]
            k = pl.program_id(2)

            @pl.when(k == 0)
            def _():
                acc_ref[...] = part

            @pl.when(k > 0)
            def _():
                acc_ref[...] += part

            @pl.when(k == nk - 1)
            def _():
                o_ref[...] = acc_ref[...].astype(o_ref.dtype)

    return pl.pallas_call(
        body, name=name, grid=grid, in_specs=[a_spec, b_spec], out_specs=o_spec, out_shape=out_shape,
        scratch_shapes=[] if nk == 1 else [pltpu.VMEM(acc_shape, F32)],
        compiler_params=_cp(("parallel", "parallel", "arbitrary"), VMEM_BIG),
    )(*_hbm(a, b))


def _mm(a, b, mode, out_dtype, *, name, tm=512, tn=1024, tk=1024):
    if mode == "nn":
        (M, K), (_, N) = a.shape, b.shape
    elif mode == "nt":
        (M, K), (N, _) = a.shape, b.shape
    else:
        (K, M), (_, N) = a.shape, b.shape
    tm, tn, tk = _pick_tile(M, tm), _pick_tile(N, tn), _pick_tile(K, tk)
    if mode == "nn":
        a_spec = pl.BlockSpec((tm, tk), lambda i, j, k: (i, k))
        b_spec = pl.BlockSpec((tk, tn), lambda i, j, k: (k, j))
        dims = NN
    elif mode == "nt":
        a_spec = pl.BlockSpec((tm, tk), lambda i, j, k: (i, k))
        b_spec = pl.BlockSpec((tn, tk), lambda i, j, k: (j, k))
        dims = NT
    else:
        a_spec = pl.BlockSpec((tk, tm), lambda i, j, k: (k, i))
        b_spec = pl.BlockSpec((tk, tn), lambda i, j, k: (k, j))
        dims = TN
    return _mm_call(a, b, name=name, grid=(M // tm, N // tn, K // tk), a_spec=a_spec, b_spec=b_spec,
                    o_spec=pl.BlockSpec((tm, tn), lambda i, j, k: (i, j)),
                    out_shape=pltpu.HBM((M, N), out_dtype), dims=dims, acc_shape=(tm, tn))


ROW_TILE = 512
SUB_ROWS = 256


def _sub_tiles(T):
    return [pl.ds(r0, min(SUB_ROWS, T)) for r0 in range(0, T, SUB_ROWS)]


def _rt(T, W, c=0):
    return pl.BlockSpec((T, W), lambda i: (i, c))


def _rt3(nb, T, W):
    return pl.BlockSpec((nb, T, W), lambda i: (0, i, 0))


def _res(shape):
    nd = len(shape)
    return pl.BlockSpec(tuple(shape), lambda i: (0,) * nd, pipeline_mode=pl.Buffered(1))


def _acc(shape):
    nd = len(shape)
    return pl.BlockSpec(tuple(shape), lambda i: (0,) * nd, pipeline_mode=pl.Buffered(1))


def _nbytes(shape, dtype):
    n = jnp.dtype(dtype).itemsize
    for s in shape:
        n *= s
    return n


def _vmem_limit(tiles, resident, temps=16 * 1024 * 1024):
    need = 2 * sum(_nbytes(s, d) for s, d in tiles) + sum(_nbytes(s, d) for s, d in resident) + temps
    return min(need, 60 * 1024 * 1024)


def _tok_call(body, *, name, S, T, ins, outs, semantics="parallel"):
    tiles = [(spec.block_shape, a.dtype) for a, spec, kind in ins + outs if kind == "tile"]
    resident = [(a.shape, a.dtype) for a, spec, kind in ins + outs if kind == "res"]
    return pl.pallas_call(
        body, name=name, grid=(S // T,),
        in_specs=[spec for _, spec, _ in ins], out_specs=[spec for _, spec, _ in outs],
        out_shape=[pltpu.HBM(a.shape, a.dtype) for a, _, _ in outs],
        compiler_params=_cp((semantics,), _vmem_limit(tiles, resident)),
    )(*_hbm(*[a for a, _, _ in ins]))


def _tile(a, spec):
    return (a, spec, "tile")


def _whole(a):
    return (a, _res(a.shape), "res")


def _out_tile(shape, dtype, spec):
    return (jax.ShapeDtypeStruct(shape, dtype), spec, "tile")


def _out_acc(shape, dtype=F32):
    return (jax.ShapeDtypeStruct(shape, dtype), _acc(shape), "res")


def _rms_stats(x):
    r = lax.rsqrt(jnp.mean(x * x, axis=-1, keepdims=True) + EPS)
    return r, x * r


def _rms_bwd(xh, r, g, dy):
    dxh = dy * g
    dx = r * (dxh - xh * jnp.mean(dxh * xh, axis=-1, keepdims=True))
    dg = jnp.sum(dy * xh, axis=0, keepdims=True)
    return dx, dg


def _accum(ref, val):
    @pl.when(pl.program_id(0) == 0)
    def _():
        ref[...] = val

    @pl.when(pl.program_id(0) > 0)
    def _():
        ref[...] += val


def _dot_rows_t(a, w_ref, row0, o_ref, chunk=1024):
    n = o_ref.shape[1]
    for n0 in range(0, n, chunk):
        n1 = min(n, n0 + chunk)
        o_ref[:, n0:n1] = _dot(a, w_ref[row0 + n0:row0 + n1, :], NT).astype(o_ref.dtype)


def _in_proj_call(x, g1, w_blocks, *, name, T=ROW_TILE):
    S = x.shape[0]

    def body(x_ref, g_ref, blk_ref, a_ref, pa_ref, al_ref, ps_ref, pg_ref, w_ref):
        @pl.when(pl.program_id(0) == 0)
        def _():
            for d in range(N_DEV):
                w_ref[d * IN_BLK:(d + 1) * IN_BLK, :] = blk_ref[d]

        _, xh = _rms_stats(x_ref[...])
        a = (xh * g_ref[...]).astype(BF16)
        a_ref[...] = a
        _dot_rows_t(a, w_ref, A_COLS[0], pa_ref)
        _dot_rows_t(a, w_ref, LOW_COLS[0], al_ref)
        _dot_rows_t(a, w_ref, S_COLS[0], ps_ref)
        _dot_rows_t(a, w_ref, G_COLS[0], pg_ref)

    widths = (D_MODEL, A_COLS[1] - A_COLS[0], LANES, S_COLS[1] - S_COLS[0], G_COLS[1] - G_COLS[0])
    return _tok_call(
        body, name=name, S=S, T=T, semantics="arbitrary",
        ins=[_tile(x, _rt(T, D_MODEL)), _whole(g1), _whole(w_blocks)],
        outs=[_out_tile((S, w), BF16, _rt(T, w)) for w in widths] + [_out_acc((D_IN, D_MODEL), BF16)])


def _mixer_tail_call(y_gla, y_sgu, pg, x, w_bg, w_bs, w_out, g2, g3, *, name, T=ROW_TILE):
    S = x.shape[0]

    def body(yg_ref, ys_ref, pg_ref, x_ref, wbg_ref, wbs_ref, wo_ref, g2_ref, g3_ref,
             t1_ref, t2_ref, mg_ref, mix_ref, x1_ref, h_ref):
        for rows in _sub_tiles(T):
            t1 = _dot(yg_ref[rows, :], wbg_ref[...], NN)
            t2 = _dot(ys_ref[rows, :], wbs_ref[...], NN)
            t1_ref[rows, :] = t1.astype(BF16)
            t2_ref[rows, :] = t2.astype(BF16)
            sg = _sig(pg_ref[rows, pl.ds(0, D_MODEL)].astype(F32))
            ss = _sig(pg_ref[rows, pl.ds(D_MODEL, D_MODEL)].astype(F32))
            merged = (sg * t1 + ss * t2).astype(BF16)
            mg_ref[rows, :] = merged
            mix = _dot(merged, wo_ref[...], NN)
            mix_ref[rows, :] = mix
            _, mh = _rms_stats(mix)
            x1 = x_ref[rows, :] + mh * g2_ref[...]
            x1_ref[rows, :] = x1
            _, xh = _rms_stats(x1)
            h_ref[rows, :] = (xh * g3_ref[...]).astype(BF16)

    row = _rt(T, D_MODEL)
    b16 = lambda: _out_tile((S, D_MODEL), BF16, row)
    f32 = lambda: _out_tile((S, D_MODEL), F32, row)
    return _tok_call(
        body, name=name, S=S, T=T,
        ins=[_tile(y_gla, row), _tile(y_sgu, row), _tile(pg, _rt(T, 2 * D_MODEL)), _tile(x, row),
             _whole(w_bg), _whole(w_bs), _whole(w_out), _whole(g2), _whole(g3)],
        outs=[b16(), b16(), b16(), f32(), f32(), b16()])


FF_CHUNKS = ((0, 1024), (1024, 2048), (2048, D_FF))


def _ffn_in_call(h, wfi, *, name, T=ROW_TILE):
    S = h.shape[0]

    def body(h_ref, w_ref, gu_ref, z_ref):
        hv = h_ref[...]
        for c0, c1 in FF_CHUNKS:
            gate = _dot(hv, w_ref[c0:c1, :], NT)
            up = _dot(hv, w_ref[D_FF + c0:D_FF + c1, :], NT)
            gu_ref[:, c0:c1] = gate.astype(BF16)
            gu_ref[:, D_FF + c0:D_FF + c1] = up.astype(BF16)
            z_ref[:, c0:c1] = (gate * _sig(gate) * up).astype(BF16)

    return _tok_call(
        body, name=name, S=S, T=T,
        ins=[_tile(h, _rt(T, D_MODEL)), _whole(wfi)],
        outs=[_out_tile((S, 2 * D_FF), BF16, _rt(T, 2 * D_FF)),
              _out_tile((S, D_FF), BF16, _rt(T, D_FF))])


def _ffn_out_loss_call(z, wfo, x1, target, g4, *, name, T=ROW_TILE):
    S = x1.shape[0]

    def body(z_ref, w_ref, x1_ref, t_ref, g4_ref, loss_ref, dx2_ref, dy_ref, dg4_ref):
        loss = jnp.zeros((1, 1), F32)
        dg4 = jnp.zeros((1, D_MODEL), F32)
        for rows in _sub_tiles(T):
            y = _dot(z_ref[rows, :], w_ref[...], NN)
            r, yh = _rms_stats(y)
            diff = x1_ref[rows, :] + yh * g4_ref[...] - t_ref[rows, :]
            loss = loss + 0.5 * jnp.sum(jnp.mean(diff * diff, axis=-1, keepdims=True), axis=0, keepdims=True)
            dx2 = diff * (1.0 / D_MODEL)
            dx2_ref[rows, :] = dx2
            dy, dg = _rms_bwd(yh, r, g4_ref[...], dx2)
            dy_ref[rows, :] = dy.astype(BF16)
            dg4 = dg4 + dg
        _accum(loss_ref, jnp.broadcast_to(loss, (1, LANES)))
        _accum(dg4_ref, dg4)

    row = _rt(T, D_MODEL)
    return _tok_call(
        body, name=name, S=S, T=T, semantics="arbitrary",
        ins=[_tile(z, _rt(T, D_FF)), _whole(wfo), _tile(x1, row), _tile(target, row), _whole(g4)],
        outs=[_out_acc((1, LANES)), _out_tile((S, D_MODEL), F32, row), _out_tile((S, D_MODEL), BF16, row),
              _out_acc((1, D_MODEL))])


def _ffn_out_bwd_call(dy, wfo, gu, *, name, T=ROW_TILE):
    S = dy.shape[0]

    def body(dy_ref, w_ref, gu_ref, dgu_ref):
        dyv = dy_ref[...]
        for c0, c1 in FF_CHUNKS:
            dz = _dot(dyv, w_ref[c0:c1, :], NT).astype(BF16)
            gt = gu_ref[:, c0:c1]
            up = gu_ref[:, D_FF + c0:D_FF + c1]
            s = _sig(gt)
            dgu_ref[:, c0:c1] = dz * up * s * (1.0 + gt * (1.0 - s))
            dgu_ref[:, D_FF + c0:D_FF + c1] = dz * gt * s

    wide = _rt(T, 2 * D_FF)
    return _tok_call(
        body, name=name, S=S, T=T,
        ins=[_tile(dy, _rt(T, D_MODEL)), _whole(wfo), _tile(gu, wide)],
        outs=[_out_tile((S, 2 * D_FF), BF16, wide)])[0]


def _ffn_in_bwd_call(dgu, wfi, dx2, x1, mix, g3, g2, *, name, T=ROW_TILE):
    S = x1.shape[0]

    def body(dgu_ref, w_ref, dx2_ref, x1_ref, mix_ref, g3_ref, g2_ref, dx1_ref, dmix_ref, dg3_ref, dg2_ref):
        dg3 = jnp.zeros((1, D_MODEL), F32)
        dg2 = jnp.zeros((1, D_MODEL), F32)
        for rows in _sub_tiles(T):
            dh = _dot(dgu_ref[rows, :], w_ref[...], NN)
            r3, xh = _rms_stats(x1_ref[rows, :])
            d3, g = _rms_bwd(xh, r3, g3_ref[...], dh)
            dg3 = dg3 + g
            dx1 = dx2_ref[rows, :] + d3
            dx1_ref[rows, :] = dx1
            r2, mh = _rms_stats(mix_ref[rows, :])
            dmix, g = _rms_bwd(mh, r2, g2_ref[...], dx1)
            dg2 = dg2 + g
            dmix_ref[rows, :] = dmix.astype(BF16)
        _accum(dg3_ref, dg3)
        _accum(dg2_ref, dg2)

    row = _rt(T, D_MODEL)
    return _tok_call(
        body, name=name, S=S, T=T, semantics="arbitrary",
        ins=[_tile(dgu, _rt(T, 2 * D_FF)), _whole(wfi), _tile(dx2, row), _tile(x1, row), _tile(mix, row),
             _whole(g3), _whole(g2)],
        outs=[_out_tile((S, D_MODEL), F32, row), _out_tile((S, D_MODEL), BF16, row),
              _out_acc((1, D_MODEL)), _out_acc((1, D_MODEL))])


def _mixer_bwd_call(dmix, t1, t2, pg, w_out, w_bg, w_bs, *, name, T=ROW_TILE):
    S = dmix.shape[0]

    def body(dmix_ref, t1_ref, t2_ref, pg_ref, wo_ref, wbg_ref, wbs_ref, dt1_ref, dt2_ref, dpg_ref, dyg_ref, dys_ref):
        gg, gs = pl.ds(0, D_MODEL), pl.ds(D_MODEL, D_MODEL)
        for rows in _sub_tiles(T):
            dm = _dot(dmix_ref[rows, :], wo_ref[...], NT)
            sg = _sig(pg_ref[rows, gg].astype(F32))
            ss = _sig(pg_ref[rows, gs].astype(F32))
            dt1 = (dm * sg).astype(BF16)
            dt2 = (dm * ss).astype(BF16)
            dt1_ref[rows, :] = dt1
            dt2_ref[rows, :] = dt2
            dpg_ref[rows, gg] = (dm * t1_ref[rows, :].astype(F32) * sg * (1.0 - sg)).astype(BF16)
            dpg_ref[rows, gs] = (dm * t2_ref[rows, :].astype(F32) * ss * (1.0 - ss)).astype(BF16)
            dyg_ref[rows, :] = _dot(dt1, wbg_ref[...], NT).astype(BF16)
            dys_ref[rows, :] = _dot(dt2, wbs_ref[...], NT).astype(BF16)

    row = _rt(T, D_MODEL)
    wide = _rt(T, 2 * D_MODEL)
    b16 = lambda: _out_tile((S, D_MODEL), BF16, row)
    return _tok_call(
        body, name=name, S=S, T=T,
        ins=[_tile(dmix, row), _tile(t1, row), _tile(t2, row), _tile(pg, wide), _whole(w_out), _whole(w_bg), _whole(w_bs)],
        outs=[b16(), b16(), _out_tile((S, 2 * D_MODEL), BF16, wide), b16(), b16()])


def _in_proj_bwd_call(dpa, dlow, dps, dpg, w_in_t, x, dx1, g1, *, name, T=ROW_TILE):
    S = x.shape[0]

    def body(dpa_ref, dl_ref, dps_ref, dpg_ref, w_ref, x_ref, dx1_ref, g1_ref, gx_ref, dg1_ref):
        da = (_dot(dpa_ref[...], w_ref[A_COLS[0]:A_COLS[1], :], NN)
              + _dot(dl_ref[...], w_ref[LOW_COLS[0]:LOW_COLS[0] + LANES, :], NN)
              + _dot(dps_ref[...], w_ref[S_COLS[0]:S_COLS[1], :], NN)
              + _dot(dpg_ref[...], w_ref[G_COLS[0]:G_COLS[1], :], NN))
        r, xh = _rms_stats(x_ref[...])
        dxa, dg = _rms_bwd(xh, r, g1_ref[...], da)
        gx_ref[...] = dx1_ref[...] + dxa
        _accum(dg1_ref, dg)

    row = _rt(T, D_MODEL)
    return _tok_call(
        body, name=name, S=S, T=T, semantics="arbitrary",
        ins=[_tile(dpa, _rt(T, dpa.shape[1])), _tile(dlow, _rt(T, dlow.shape[1])), _tile(dps, _rt(T, dps.shape[1])),
             _tile(dpg, _rt(T, dpg.shape[1])), _whole(w_in_t), _tile(x, row), _tile(dx1, row), _whole(g1)],
        outs=[_out_tile((S, D_MODEL), F32, row), _out_acc((1, D_MODEL))])


TOKEN_TILE = 512


def _weight_grads_part(arrays, pairs, *, tk=TOKEN_TILE, out_dtype=BF16):
    S = arrays[0].shape[-2]
    tk = min(tk, S)
    n_in = len(arrays)

    def out_shape(i, j):
        a, b = arrays[i], arrays[j]
        if a.ndim == 3:
            return (a.shape[0], a.shape[2], b.shape[1])
        if b.ndim == 3:
            return (b.shape[0], a.shape[1], b.shape[2])
        return (a.shape[1], b.shape[1])

    shapes = [out_shape(i, j) for i, j in pairs]

    in_place = out_dtype == F32

    def body(ins, outs, accs):
        k = pl.program_id(0)
        if in_place:
            accs = outs

        @pl.when(k == 0)
        def _():
            for acc in accs:
                acc[...] = jnp.zeros_like(acc)

        for (i, j), acc in zip(pairs, accs):
            a_ref, b_ref = ins[i], ins[j]
            if len(a_ref.shape) == 3:
                for n in range(a_ref.shape[0]):
                    acc[n] += _dot(a_ref[n], b_ref[...], TN)
            elif len(b_ref.shape) == 3:
                a = a_ref[...]
                for n in range(b_ref.shape[0]):
                    acc[n] += _dot(a, b_ref[n], TN)
            else:
                b = b_ref[...]
                for m0 in range(0, a_ref.shape[1], 1024):
                    m1 = min(a_ref.shape[1], m0 + 1024)
                    acc[m0:m1, :] += _dot(a_ref[:, m0:m1], b, TN)

        if not in_place:
            @pl.when(k == S // tk - 1)
            def _():
                for out, acc in zip(outs, accs):
                    out[...] = acc[...].astype(out.dtype)

    def in_spec(a):
        if a.ndim == 3:
            return pl.BlockSpec((a.shape[0], tk, a.shape[2]), lambda k: (0, k, 0))
        return pl.BlockSpec((tk, a.shape[1]), lambda k: (k, 0))

    return dict(
        body=body, steps=S // tk, arrays=list(arrays),
        in_specs=[in_spec(a) for a in arrays],
        out_specs=[_acc(s) for s in shapes],
        out_shapes=[pltpu.HBM(s, out_dtype) for s in shapes],
        scratch=[] if in_place else [pltpu.VMEM(s, F32) for s in shapes],
        tiles=[(in_spec(a).block_shape, a.dtype) for a in arrays],
        resident=[(s, F32) for s in shapes] + ([] if in_place else [(s, BF16) for s in shapes]))


def _weight_grads_call(arrays, pairs, *, name, tk=TOKEN_TILE):
    part = _weight_grads_part(arrays, pairs, tk=tk)
    n_in, n_out = len(part["arrays"]), len(part["out_shapes"])

    def body(*refs):
        part["body"](refs[:n_in], refs[n_in:n_in + n_out], refs[n_in + n_out:])

    return pl.pallas_call(
        body, name=name, grid=(part["steps"],),
        in_specs=part["in_specs"], out_specs=part["out_specs"], out_shape=part["out_shapes"],
        scratch_shapes=part["scratch"],
        compiler_params=_cp(("arbitrary",), _vmem_limit(part["tiles"], part["resident"])),
    )(*_hbm(*part["arrays"]))


def _ffn_in_grad_call(h, dgu, *, name, tk=TOKEN_TILE):
    S = h.shape[0]
    tk = min(tk, S)
    shape = (D_FF, D_MODEL)

    def body(h_ref, dgu_ref, out_ref, acc):
        k = pl.program_id(1)

        @pl.when(k == 0)
        def _():
            acc[...] = jnp.zeros_like(acc)

        hv = h_ref[...]
        for c0, c1 in FF_CHUNKS:
            acc[c0:c1, :] += _dot(dgu_ref[:, c0:c1], hv, TN)

        @pl.when(k == S // tk - 1)
        def _():
            out_ref[...] = acc[...].astype(out_ref.dtype)

    tiles = [((tk, D_MODEL), BF16), ((tk, D_FF), BF16), (shape, BF16)]
    return pl.pallas_call(
        body, name=name, grid=(2, S // tk),
        in_specs=[pl.BlockSpec((tk, D_MODEL), lambda g, k: (k, 0)),
                  pl.BlockSpec((tk, D_FF), lambda g, k: (k, g))],
        out_specs=pl.BlockSpec(shape, lambda g, k: (g, 0)),
        out_shape=pltpu.HBM((2 * D_FF, D_MODEL), BF16),
        scratch_shapes=[pltpu.VMEM(shape, F32)],
        compiler_params=_cp(("parallel", "arbitrary"), _vmem_limit(tiles, [(shape, F32)])),
    )(*_hbm(h, dgu))


GLA_TILE = 256
Q_OFF, K_OFF, V_OFF, R_OFF = 0, QK, 2 * QK, 2 * QK + GV


def _tri(lower):
    r = lax.broadcasted_iota(jnp.int32, (CHUNK, CHUNK), 0)
    c = lax.broadcasted_iota(jnp.int32, (CHUNK, CHUNK), 1)
    return jnp.where((r >= c) if lower else (c >= r), 1.0, 0.0).astype(BF16)


def _gla_fwd_call(pa, alow, wg, bgate, gnorm, *, name):
    S = pa.shape[0]
    Tg = min(GLA_TILE, S)
    cb = Tg // CHUNK

    def body(pa_ref, al_ref, wg_ref, bg_ref, gn_ref, y_ref, st_ref, state):
        @pl.when(pl.program_id(0) == 0)
        def _():
            state[...] = jnp.zeros_like(state)

        logit = _dot(al_ref[...], wg_ref[...], NN) + bg_ref[...]
        ls = _logsig(logit) * (1.0 / 16.0)
        tri = _tri(True)
        for c in range(cb):
            rows = pl.ds(c * CHUNK, CHUNK)
            cum = _exact_mask_dot(tri, ls[c * CHUNK:(c + 1) * CHUNK])
            tot = cum[CHUNK - 1:CHUNK]
            kd = (pa_ref[rows, pl.ds(K_OFF, QK)].astype(F32) * jnp.exp(tot - cum)).astype(BF16)
            decay = jnp.exp(tot)
            for h in range(HEADS):
                lanes = slice(h * DK, (h + 1) * DK)
                new = state[h] * decay[:, lanes] + _dot(pa_ref[rows, pl.ds(V_OFF + h * DV, DV)], kd[:, lanes], TN)
                state[h] = new
                st_ref[c, h] = new
        for c in range(cb):
            rows = pl.ds(c * CHUNK, CHUNK)
            for h in range(HEADS):
                qs = (pa_ref[rows, pl.ds(Q_OFF + h * DK, DK)].astype(F32) * Q_SCALE).astype(BF16)
                o = _dot(qs, st_ref[c, h].astype(BF16), NT)
                rs = lax.rsqrt(jnp.mean(o * o, axis=-1, keepdims=True) + EPS)
                rr = pa_ref[rows, pl.ds(R_OFF + h * DV, DV)].astype(F32)
                y_ref[rows, pl.ds(h * DV, DV)] = (o * rs * gn_ref[h] * (rr * _sig(rr))).astype(BF16)

    return pl.pallas_call(
        body, name=name, grid=(S // Tg,),
        in_specs=[
            pl.BlockSpec((Tg, 2 * QK + 2 * GV), lambda t: (t, 0)),
            pl.BlockSpec((Tg, LANES), lambda t: (t, 0)),
            pl.BlockSpec((LANES, QK), lambda t: (0, 0)),
            pl.BlockSpec((1, QK), lambda t: (0, 0)),
            pl.BlockSpec((HEADS, 1, DV), lambda t: (0, 0, 0)),
        ],
        out_specs=[
            pl.BlockSpec((Tg, GV), lambda t: (t, 0)),
            pl.BlockSpec((cb, HEADS, DV, DK), lambda t: (t, 0, 0, 0)),
        ],
        out_shape=[pltpu.HBM((S, GV), BF16), pltpu.HBM((S // CHUNK, HEADS, DV, DK), F32)],
        scratch_shapes=[pltpu.VMEM((HEADS, DV, DK), F32)],
        compiler_params=_cp(("arbitrary",), VMEM_BIG),
    )(*_hbm(pa, alow, wg, bgate, gnorm))


def _gla_bwd_call(pa, alow, wg, bgate, gnorm, states, dy, *, name):
    S = pa.shape[0]
    Tg = min(GLA_TILE, S)
    cb = Tg // CHUNK
    nt = S // Tg

    def rev(t):
        return nt - 1 - t

    def body(pa_ref, al_ref, wg_ref, bg_ref, gn_ref, st_ref, prev_ref, dy_ref,
             dpa_ref, dl_ref, dgn_ref, dbg_ref, carry, pbuf):
        t = pl.program_id(0)

        @pl.when(t == 0)
        def _():
            carry[...] = jnp.zeros_like(carry)
            dgn_ref[...] = jnp.zeros_like(dgn_ref)
            dbg_ref[...] = jnp.zeros_like(dbg_ref)

        logit = _dot(al_ref[...], wg_ref[...], NN) + bg_ref[...]
        ls = _logsig(logit) * (1.0 / 16.0)
        sneg = 1.0 / (1.0 + jnp.exp(logit))
        tri = _tri(True)
        upper = _tri(False)
        first_tile = rev(t) == 0
        heads = range(HEADS)

        w, decay, kd = [], [], []
        for c in range(cb):
            rows = pl.ds(c * CHUNK, CHUNK)
            cum = _exact_mask_dot(tri, ls[c * CHUNK:(c + 1) * CHUNK])
            tot = cum[CHUNK - 1:CHUNK]
            w.append(jnp.exp(tot - cum))
            decay.append(jnp.exp(tot))
            kd.append(pa_ref[rows, pl.ds(K_OFF, QK)].astype(F32) * w[c])

        dgn = [jnp.zeros((1, DV), F32) for _ in heads]
        for c in range(cb):
            rows = pl.ds(c * CHUNK, CHUNK)
            for h in heads:
                gn = gn_ref[h]
                qs = (pa_ref[rows, pl.ds(Q_OFF + h * DK, DK)].astype(F32) * Q_SCALE).astype(BF16)
                st16 = st_ref[c, h].astype(BF16)
                o = _dot(qs, st16, NT)
                rs = lax.rsqrt(jnp.mean(o * o, axis=-1, keepdims=True) + EPS)
                oh = o * rs
                rr = pa_ref[rows, pl.ds(R_OFF + h * DV, DV)].astype(F32)
                sr = _sig(rr)
                dyv = dy_ref[rows, pl.ds(h * DV, DV)].astype(F32)
                dpa_ref[rows, pl.ds(R_OFF + h * DV, DV)] = (
                    dyv * oh * gn * sr * (1.0 + rr * (1.0 - sr))).astype(BF16)
                don = dyv * (rr * sr)
                dgn[h] = dgn[h] + jnp.sum(don * oh, axis=0, keepdims=True)
                doh = don * gn
                do16 = (rs * (doh - oh * jnp.mean(doh * oh, axis=-1, keepdims=True))).astype(BF16)
                dpa_ref[rows, pl.ds(Q_OFF + h * DK, DK)] = (_dot(do16, st16, NN) * Q_SCALE).astype(BF16)
                pbuf[c, h] = _dot(do16, qs, TN)
        for h in heads:
            dgn_ref[h] += dgn[h]

        dkd = [[None] * HEADS for _ in range(cb)]
        ddecay = [[None] * HEADS for _ in range(cb)]
        for c in reversed(range(cb)):
            rows = pl.ds(c * CHUNK, CHUNK)
            for h in heads:
                lanes = slice(h * DK, (h + 1) * DK)
                gt = pbuf[c, h] + carry[h]
                gt16 = gt.astype(BF16)
                dkd[c][h] = _dot(pa_ref[rows, pl.ds(V_OFF + h * DV, DV)], gt16, NN)
                dpa_ref[rows, pl.ds(V_OFF + h * DV, DV)] = _dot(kd[c][:, lanes].astype(BF16), gt16, NT).astype(BF16)
                if c > 0:
                    st_prev = st_ref[c - 1, h]
                else:
                    st_prev = jnp.where(first_tile, 0.0, prev_ref[0, h])
                ddecay[c][h] = jnp.sum(gt * st_prev, axis=0, keepdims=True)
                carry[h] = gt * decay[c][:, lanes]

        dbg = jnp.zeros((1, QK), F32)
        for c in range(cb):
            rows = pl.ds(c * CHUNK, CHUNK)
            dkd_c = jnp.concatenate(dkd[c], axis=1)
            dpa_ref[rows, pl.ds(K_OFF, QK)] = (dkd_c * w[c]).astype(BF16)
            e = dkd_c * kd[c]
            dtot = jnp.sum(e, axis=0, keepdims=True) + jnp.concatenate(ddecay[c], axis=1) * decay[c]
            dls = dtot - _exact_mask_dot(upper, e)
            dlogit = dls * (1.0 / 16.0) * sneg[c * CHUNK:(c + 1) * CHUNK]
            dl_ref[rows, :] = dlogit.astype(BF16)
            dbg = dbg + jnp.sum(dlogit, axis=0, keepdims=True)
        dbg_ref[...] += dbg

    wide = 2 * QK + 2 * GV
    tiles = [((Tg, wide), BF16), ((cb + 1, HEADS, DV, DK), F32), ((Tg, GV), BF16), ((Tg, wide), BF16),
             ((Tg, QK), BF16)]
    resident = [((cb + 1, HEADS, DV, DK), F32)]
    return pl.pallas_call(
        body, name=name, grid=(nt,),
        in_specs=[
            pl.BlockSpec((Tg, wide), lambda t: (rev(t), 0)),
            pl.BlockSpec((Tg, LANES), lambda t: (rev(t), 0)),
            pl.BlockSpec((LANES, QK), lambda t: (0, 0)),
            pl.BlockSpec((1, QK), lambda t: (0, 0)),
            pl.BlockSpec((HEADS, 1, DV), lambda t: (0, 0, 0)),
            pl.BlockSpec((cb, HEADS, DV, DK), lambda t: (rev(t), 0, 0, 0)),
            pl.BlockSpec((1, HEADS, DV, DK), lambda t: (jnp.maximum(rev(t) * cb - 1, 0), 0, 0, 0)),
            pl.BlockSpec((Tg, GV), lambda t: (rev(t), 0)),
        ],
        out_specs=[
            pl.BlockSpec((Tg, wide), lambda t: (rev(t), 0)),
            pl.BlockSpec((Tg, QK), lambda t: (rev(t), 0)),
            pl.BlockSpec((HEADS, 1, DV), lambda t: (0, 0, 0)),
            pl.BlockSpec((1, QK), lambda t: (0, 0)),
        ],
        out_shape=[pltpu.HBM((S, wide), BF16), pltpu.HBM((S, QK), BF16),
                   jax.ShapeDtypeStruct((HEADS, 1, DV), F32), jax.ShapeDtypeStruct((1, QK), F32)],
        scratch_shapes=[pltpu.VMEM((HEADS, DV, DK), F32), pltpu.VMEM((cb, HEADS, DV, DK), F32)],
        compiler_params=_cp(("arbitrary",), _vmem_limit(tiles, resident)),
    )(*_hbm(pa, alow, wg, bgate, gnorm, states, states, dy))


SGU_TILE = 256


def _sgu_mask():
    r = lax.broadcasted_iota(jnp.int32, (SBLOCK, SBLOCK), 0)
    c = lax.broadcasted_iota(jnp.int32, (SBLOCK, SBLOCK), 1)
    return (c < CHUNK) | (r >= CHUNK)


def _ln_stats(vf):
    mu = jnp.mean(vf, axis=-1, keepdims=True)
    xc = vf - mu
    rs = lax.rsqrt(jnp.mean(xc * xc, axis=-1, keepdims=True) + EPS)
    return rs, xc * rs


def _sgu_fwd_call(ps, ln_g, ln_b, w_sp, b_sp, *, name):
    S = ps.shape[0]
    Ts = min(SGU_TILE, S)

    def body(ps_ref, lg_ref, lb_ref, w_ref, b_ref, y_ref):
        mask = _sgu_mask()
        for g in range(GROUPS):
            wm = jnp.where(mask, w_ref[g], 0.0).astype(BF16)
            for p in range(Ts // SBLOCK):
                rows = pl.ds(p * SBLOCK, SBLOCK)
                u = _gelu(ps_ref[rows, pl.ds(g * DG, DG)].astype(F32))
                _, xh = _ln_stats(_gelu(ps_ref[rows, pl.ds(D_MODEL + g * DG, DG)].astype(F32)))
                vn = xh * lg_ref[g] + lb_ref[g]
                mixed = _dot(wm, vn.astype(BF16), NN) + b_ref[g]
                y_ref[rows, pl.ds(g * DG, DG)] = (u * mixed).astype(BF16)

    full3 = lambda a, b, c: pl.BlockSpec((a, b, c), lambda t: (0, 0, 0))
    return pl.pallas_call(
        body, name=name, grid=(S // Ts,),
        in_specs=[pl.BlockSpec((Ts, 2 * D_MODEL), lambda t: (t, 0)),
                  full3(GROUPS, 1, DG), full3(GROUPS, 1, DG), full3(GROUPS, SBLOCK, SBLOCK), full3(GROUPS, SBLOCK, 1)],
        out_specs=pl.BlockSpec((Ts, D_MODEL), lambda t: (t, 0)),
        out_shape=pltpu.HBM((S, D_MODEL), BF16),
        compiler_params=_cp(("parallel",)),
    )(*_hbm(ps, ln_g, ln_b, w_sp, b_sp))


def _sgu_bwd_call(ps, ln_g, ln_b, w_sp, b_sp, dy, *, name):
    S = ps.shape[0]
    Ts = min(SGU_TILE, S)

    def body(ps_ref, lg_ref, lb_ref, w_ref, b_ref, dy_ref, ds_ref, dlg_ref, dlb_ref, dw_ref, db_ref):
        @pl.when(pl.program_id(0) == 0)
        def _():
            dlg_ref[...] = jnp.zeros_like(dlg_ref)
            dlb_ref[...] = jnp.zeros_like(dlb_ref)
            dw_ref[...] = jnp.zeros_like(dw_ref)
            db_ref[...] = jnp.zeros_like(db_ref)

        mask = _sgu_mask()
        for g in range(GROUPS):
            wm = jnp.where(mask, w_ref[g], 0.0).astype(BF16)
            lg = lg_ref[g]
            for p in range(Ts // SBLOCK):
                rows = pl.ds(p * SBLOCK, SBLOCK)
                su = ps_ref[rows, pl.ds(g * DG, DG)].astype(F32)
                sv = ps_ref[rows, pl.ds(D_MODEL + g * DG, DG)].astype(F32)
                u, du = _gelu_and_grad(su)
                gv, dgv = _gelu_and_grad(sv)
                rs, xh = _ln_stats(gv)
                vn16 = (xh * lg + lb_ref[g]).astype(BF16)
                mixed = _dot(wm, vn16, NN) + b_ref[g]
                dyv = dy_ref[rows, pl.ds(g * DG, DG)].astype(F32)
                ds_ref[rows, pl.ds(g * DG, DG)] = (dyv * mixed * du).astype(BF16)
                dmix = dyv * u
                dmix16 = dmix.astype(BF16)
                db_ref[g] += jnp.sum(dmix, axis=-1, keepdims=True)
                dw_ref[g] += jnp.where(mask, _dot(dmix16, vn16, NT), 0.0)
                dvn = _dot(wm, dmix16, TN)
                dlg_ref[g] += jnp.sum(dvn * xh, axis=0, keepdims=True)
                dlb_ref[g] += jnp.sum(dvn, axis=0, keepdims=True)
                dxh = dvn * lg
                dvf = rs * (dxh - jnp.mean(dxh, axis=-1, keepdims=True)
                            - xh * jnp.mean(dxh * xh, axis=-1, keepdims=True))
                ds_ref[rows, pl.ds(D_MODEL + g * DG, DG)] = (dvf * dgv).astype(BF16)

    full3 = lambda a, b, c: pl.BlockSpec((a, b, c), lambda t: (0, 0, 0))
    tiles = [((Ts, 2 * D_MODEL), BF16), ((Ts, D_MODEL), BF16), ((Ts, 2 * D_MODEL), BF16)]
    return pl.pallas_call(
        body, name=name, grid=(S // Ts,),
        in_specs=[pl.BlockSpec((Ts, 2 * D_MODEL), lambda t: (t, 0)),
                  full3(GROUPS, 1, DG), full3(GROUPS, 1, DG), full3(GROUPS, SBLOCK, SBLOCK), full3(GROUPS, SBLOCK, 1),
                  pl.BlockSpec((Ts, D_MODEL), lambda t: (t, 0))],
        out_specs=[pl.BlockSpec((Ts, 2 * D_MODEL), lambda t: (t, 0)),
                   full3(GROUPS, 1, DG), full3(GROUPS, 1, DG), full3(GROUPS, SBLOCK, SBLOCK), full3(GROUPS, SBLOCK, 1)],
        out_shape=[pltpu.HBM((S, 2 * D_MODEL), BF16),
                   jax.ShapeDtypeStruct((GROUPS, 1, DG), F32), jax.ShapeDtypeStruct((GROUPS, 1, DG), F32),
                   jax.ShapeDtypeStruct((GROUPS, SBLOCK, SBLOCK), F32),
                   jax.ShapeDtypeStruct((GROUPS, SBLOCK, 1), F32)],
        compiler_params=_cp(("arbitrary",), _vmem_limit(tiles, [])),
    )(*_hbm(ps, ln_g, ln_b, w_sp, b_sp, dy))


def _position():
    return lax.axis_index("x"), lax.axis_index("y"), lax.axis_index("c")


def _gather_copies(srcs, dsts, send_sems, recv_sems, local_sems):
    x, y, c = _position()
    me, sibling = (x, y, c), (x, y, 1 - c)
    chips = [(1 - x, y), (x, 1 - y), (1 - x, 1 - y)]
    n = len(srcs)

    def slab(a, block):
        px, py, pc = block
        return dsts[a].at[4 * px + 2 * py + pc]

    def copy(a, k, block, to, src=None):
        return pltpu.make_async_remote_copy(
            src_ref=slab(a, block) if src is None else src, dst_ref=slab(a, block),
            send_sem=send_sems.at[7 * a + k], recv_sem=recv_sems.at[7 * a + k], device_id=to, device_id_type=MESH)

    mine = [pltpu.make_async_copy(srcs[a], slab(a, me), local_sems.at[a]) for a in range(n)]
    for cp in mine:
        cp.start()
    first = []
    for a in range(n):
        first.append(copy(a, 0, me, sibling, src=srcs[a]))
        first += [copy(a, 1 + j, me, (*chip, c), src=srcs[a]) for j, chip in enumerate(chips)]
    for cp in first:
        cp.start()
    passed = []
    for j, chip in enumerate(chips):
        for a in range(n):
            copy(a, 1 + j, (*chip, c), me).wait_recv()
            cp = copy(a, 4 + j, (*chip, c), sibling)
            cp.start()
            passed.append(cp)
    for a in range(n):
        copy(a, 0, sibling, me).wait_recv()
        for j, chip in enumerate(chips):
            copy(a, 4 + j, (*chip, 1 - c), me).wait_recv()
    for cp in first + passed:
        cp.wait_send()
    for cp in mine:
        cp.wait()


def _gather_copies_relayed(srcs, dsts, send_sems, recv_sems, local_sems, waves=1):
    x, y, c = _position()
    me, sibling = (x, y, c), (x, y, 1 - c)
    x_chip, y_chip, d_chip = (1 - x, y), (x, 1 - y), (1 - x, 1 - y)
    south = c == 0
    relay_from = (jnp.where(south, x, 1 - x), jnp.where(south, 1 - y, y), c)
    relay_to = (jnp.where(south, 1 - x, x), jnp.where(south, y, 1 - y), c)
    n = len(srcs)

    def cut(a, w):
        rows = srcs[a].shape[0]
        step = (rows // waves) // 16 * 16
        if waves == 1 or step == 0:
            return pl.ds(0, rows) if w == 0 else None
        return pl.ds(w * step, step if w < waves - 1 else rows - w * step)

    def slab(a, block, w):
        px, py, pc = block
        return dsts[a].at[4 * px + 2 * py + pc, cut(a, w)]

    def copy(a, k, w, block, to, own=False):
        sem = (7 * a + k) * waves + w
        return pltpu.make_async_remote_copy(
            src_ref=srcs[a].at[cut(a, w)] if own else slab(a, block, w), dst_ref=slab(a, block, w),
            send_sem=send_sems.at[sem], recv_sem=recv_sems.at[sem], device_id=to, device_id_type=MESH)

    pieces = [(a, w) for w in range(waves) for a in range(n) if cut(a, w) is not None]
    mine = [pltpu.make_async_copy(srcs[a], dsts[a].at[4 * x + 2 * y + c], local_sems.at[a]) for a in range(n)]
    for cp in mine:
        cp.start()
    sent = []
    for a, w in pieces:
        sent += [copy(a, 0, w, me, sibling, own=True), copy(a, 1, w, me, (*x_chip, c), own=True),
                 copy(a, 2, w, me, (*y_chip, c), own=True)]
    for cp in sent:
        cp.start()
    for a, w in pieces:
        copy(a, 1, w, (*x_chip, c), me).wait_recv()
        copy(a, 2, w, (*y_chip, c), me).wait_recv()
        later = [copy(a, 3, w, relay_from, relay_to), copy(a, 4, w, (*x_chip, c), sibling),
                 copy(a, 5, w, (*y_chip, c), sibling)]
        for cp in later:
            cp.start()
        sent += later
    for a, w in pieces:
        copy(a, 3, w, (*d_chip, c), me).wait_recv()
        cp = copy(a, 6, w, (*d_chip, c), sibling)
        cp.start()
        sent.append(cp)
    for a, w in pieces:
        copy(a, 0, w, sibling, me).wait_recv()
        for k, chip in ((4, x_chip), (5, y_chip), (6, d_chip)):
            copy(a, k, w, (*chip, 1 - c), me).wait_recv()
    for cp in sent:
        cp.wait_send()
    for cp in mine:
        cp.wait()


def _all_gather_hbm(shards, *, name, waves=1):
    n = len(shards)

    def body(*refs):
        srcs, dsts = refs[:n], refs[n:2 * n]
        send_sems, recv_sems, local_sems = refs[2 * n:]
        _gather_copies_relayed(srcs, dsts, send_sems, recv_sems, local_sems, waves=waves)

    return pl.pallas_call(
        body, name=name,
        in_specs=[ANY] * n, out_specs=[ANY] * n,
        out_shape=[jax.ShapeDtypeStruct((N_DEV, *s.shape), s.dtype) for s in shards],
        scratch_shapes=_comm_sems(n, waves),
    )(*shards)


FLIPS = [(fx, fy, fc) for fx in (0, 1) for fy in (0, 1) for fc in (0, 1)][1:]


def _scatter_copies(srcs, dsts, send_sems, recv_sems, local_sems, waves=1):
    n = len(srcs)
    x, y, c = _position()
    me = 4 * x + 2 * y + c
    mine = [pltpu.make_async_copy(srcs[a].at[me], dsts[a].at[me], local_sems.at[a]) for a in range(n)]
    for cp in mine:
        cp.start()
    peers = []
    for fx, fy, fc in FLIPS:
        tx = 1 - x if fx else x
        ty = 1 - y if fy else y
        tc = 1 - c if fc else c
        peers.append(((tx, ty, tc), 4 * tx + 2 * ty + tc))
    copies = []
    for w in range(waves):
        wave = []
        for k, (peer_id, peer) in enumerate(peers):
            for a in range(n):
                rows = srcs[a].shape[1]
                step = rows if waves == 1 else (rows // waves) // 16 * 16
                r0 = w * step
                cut = pl.ds(r0, step if w < waves - 1 else rows - r0)
                sem = (7 * a + k) * waves + w
                cp = pltpu.make_async_remote_copy(
                    src_ref=srcs[a].at[peer, cut], dst_ref=dsts[a].at[me, cut],
                    send_sem=send_sems.at[sem], recv_sem=recv_sems.at[sem],
                    device_id=peer_id, device_id_type=MESH)
                cp.start()
                wave.append(cp)
        for cp in wave:
            cp.wait_send()
        copies += wave
    for cp in copies:
        cp.wait_recv()
    for cp in mine:
        cp.wait()


def _pair_copies(srcs, dsts, send_sems, recv_sems, local_sems):
    x, y, c = _position()
    copies = [pltpu.make_async_remote_copy(
        src_ref=srcs[a], dst_ref=dsts[a], send_sem=send_sems.at[a], recv_sem=recv_sems.at[a],
        device_id=(x, y, 1 - c), device_id_type=MESH) for a in range(len(srcs))]
    for cp in copies:
        cp.start()
    for cp in copies:
        cp.wait()


def _chip_scatter_copies(srcs, dsts, send_sems, recv_sems, local_sems, waves=1):
    n = len(srcs)
    x, y, c = _position()
    here = 2 * x + y
    mine = [pltpu.make_async_copy(srcs[a].at[here], dsts[a].at[here], local_sems.at[a]) for a in range(n)]
    for cp in mine:
        cp.start()
    peers = []
    for fx, fy in ((1, 0), (0, 1), (1, 1)):
        tx = 1 - x if fx else x
        ty = 1 - y if fy else y
        peers.append(((tx, ty, c), 2 * tx + ty))
    copies = []
    for w in range(waves):
        wave = []
        for k, (peer_id, peer) in enumerate(peers):
            for a in range(n):
                rows = srcs[a].shape[1]
                step = rows if waves == 1 else (rows // waves) // 16 * 16
                r0 = w * step
                cut = pl.ds(r0, step if w < waves - 1 else rows - r0)
                sem = (3 * a + k) * waves + w
                cp = pltpu.make_async_remote_copy(
                    src_ref=srcs[a].at[peer, cut], dst_ref=dsts[a].at[here, cut],
                    send_sem=send_sems.at[sem], recv_sem=recv_sems.at[sem],
                    device_id=peer_id, device_id_type=MESH)
                cp.start()
                wave.append(cp)
        for cp in wave:
            cp.wait_send()
        copies += wave
    for cp in copies:
        cp.wait_recv()
    for cp in mine:
        cp.wait()


def _comm_sems(n, waves=1):
    return [pltpu.SemaphoreType.DMA((7 * n * waves,)), pltpu.SemaphoreType.DMA((7 * n * waves,)),
            pltpu.SemaphoreType.DMA((n,))]


def _handshake(peers):
    barrier = pltpu.get_barrier_semaphore()
    for peer in peers:
        pl.semaphore_signal(barrier, inc=1, device_id=peer, device_id_type=MESH)
    pl.semaphore_wait(barrier, len(peers))


def _sequencer_call(arrays, out_types, copies_fn, peers_fn, *, name, collective_id, waves=1):
    n = len(arrays)
    srcs = [jax.new_ref(a, memory_space=pltpu.MemorySpace.HBM) for a in arrays]
    dsts = [jax.empty_ref(t, memory_space=pltpu.MemorySpace.HBM) for t in out_types]
    extra = {} if waves == 1 else {"waves": waves}

    @pl.kernel(mesh=plsc.ScalarSubcoreMesh(axis_name="sequencer", num_cores=1), name=name,
               scratch_types=_comm_sems(n, waves), compiler_params=pltpu.CompilerParams(collective_id=collective_id))
    def launch(send_sems, recv_sems, local_sems):
        _handshake(peers_fn())
        copies_fn(srcs, dsts, send_sems, recv_sems, local_sems, **extra)

    launch()
    return [d[...] for d in dsts]


def _all_other_devices():
    x, y, c = _position()
    return [(1 - x if fx else x, 1 - y if fy else y, 1 - c if fc else c) for fx, fy, fc in FLIPS]


def _gather_relay_peers():
    x, y, c = _position()
    return [(x, y, 1 - c), (1 - x, y, c), (x, 1 - y, c)]


def _gather_peers():
    x, y, c = _position()
    return [(x, y, 1 - c), (1 - x, y, c), (x, 1 - y, c), (1 - x, 1 - y, c)]


def _small_gather_async(shards, *, name, collective_id):
    return _sequencer_call(shards, [jax.ShapeDtypeStruct((N_DEV, *s.shape), s.dtype) for s in shards],
                           _gather_copies, _gather_peers, name=name, collective_id=collective_id)


def _sibling():
    x, y, c = _position()
    return [(x, y, 1 - c)]


def _same_core_of_other_chips():
    x, y, c = _position()
    return [(1 - x, y, c), (x, 1 - y, c), (1 - x, 1 - y, c)]


def _pair_exchange_async(parts, *, name, collective_id):
    return _sequencer_call(parts, [jax.ShapeDtypeStruct(p.shape, p.dtype) for p in parts],
                           _pair_copies, _sibling, name=name, collective_id=collective_id)


def _chip_scatter_async(parts, *, name, collective_id, waves=1):
    return _sequencer_call(parts, [jax.ShapeDtypeStruct(p.shape, p.dtype) for p in parts],
                           _chip_scatter_copies, _same_core_of_other_chips, name=name, collective_id=collective_id,
                           waves=waves)


def _pair_sum_call(mine, theirs, *, name, tw=D_MODEL):
    n, r, w = mine.shape

    def body(a_ref, b_ref, o_ref):
        o_ref[...] = (a_ref[...].astype(F32) + b_ref[...].astype(F32)).astype(o_ref.dtype)

    spec = pl.BlockSpec((None, r, tw), lambda i, j: (i, 0, j))
    return pl.pallas_call(
        body, name=name, grid=(n, w // tw), in_specs=[spec, spec], out_specs=spec,
        out_shape=pltpu.HBM(mine.shape, mine.dtype),
        compiler_params=_cp(("parallel", "parallel")),
    )(*_hbm(mine, theirs))


def _scatter_blocks_async(parts, *, name, collective_id, waves=1):
    return _sequencer_call(parts, [jax.ShapeDtypeStruct(p.shape, p.dtype) for p in parts],
                           _scatter_copies, _all_other_devices, name=name, collective_id=collective_id, waves=waves)


def _all_gather_async(shards, *, name, collective_id):
    return _sequencer_call(shards, [jax.ShapeDtypeStruct((N_DEV, *s.shape), s.dtype) for s in shards],
                           _gather_copies_relayed, _gather_relay_peers, name=name, collective_id=collective_id)


def _adamw_math(w, g, m, v):
    m = ADAM_B1 * m + (1.0 - ADAM_B1) * g
    v = ADAM_B2 * v + (1.0 - ADAM_B2) * (g * g)
    m_hat = m / (1.0 - ADAM_B1 ** ADAM_STEP)
    v_hat = v / (1.0 - ADAM_B2 ** ADAM_STEP)
    delta = -ADAM_LR * (m_hat / (jnp.sqrt(v_hat) + ADAM_EPS) + ADAM_WD * w)
    return delta, m, v


def _adamw_reduce_call(recv, w, m, v, *, name, T=128):
    R, W = w.shape
    n_parts = recv.shape[0]
    if R % T == 0:
        tr, tw = T, W
    elif (R // 2) % 16 == 0:
        tr, tw = R // 2, W
    else:
        tr, tw = R, 2 * LANES

    def body(p_ref, w_ref, m_ref, v_ref, g_out, d_out, m_out, v_out):
        g = p_ref[0].astype(F32)
        for d in range(1, n_parts):
            g = g + p_ref[d].astype(F32)
        g_out[...] = g
        d_out[...], m_out[...], v_out[...] = _adamw_math(w_ref[...], g, m_ref[...], v_ref[...])

    row = pl.BlockSpec((tr, tw), lambda i, j: (i, j))
    out = pltpu.HBM((R, W), F32)
    return pl.pallas_call(
        body, name=name, grid=(R // tr, W // tw),
        in_specs=[pl.BlockSpec((n_parts, tr, tw), lambda i, j: (0, i, j)), row, row, row],
        out_specs=[row] * 4, out_shape=[out] * 4,
        compiler_params=_cp(("parallel", "parallel"), VMEM_BIG),
    )(*_hbm(recv, w, m, v))


SMALL_EARLY = (("b_gate", 8), ("w_spatial", 512), ("b_spatial", 8), ("norm_post_mix", 8), ("norm_pre_ffn", 8),
               ("norm_post_ffn", 8), ("w_gate_up", 128), ("gla_norm", 64), ("sgu_ln_g", 64), ("sgu_ln_b", 64))
SMALL_EARLY_AT = {}
for _name, _rows in SMALL_EARLY:
    SMALL_EARLY_AT[_name] = sum(r for _, r in SMALL_EARLY[:len(SMALL_EARLY_AT)])
SMALL_EARLY_ROWS = sum(r for _, r in SMALL_EARLY)
SMALL_LATE_ROWS = 16


def _small_early_rows(grads):
    def rows(a, n_rows):
        a = a.reshape(-1, LANES)
        return jnp.pad(a, ((0, n_rows - a.shape[0]), (0, 0)))

    def device_major(a, n_rows):
        r, c = a.shape[0], a.shape[1] // N_DEV
        a = a.reshape(r, N_DEV, c).transpose(1, 0, 2)
        a = jnp.pad(a, ((0, 0), (0, n_rows // N_DEV - r), (0, LANES - c)))
        return a.reshape(n_rows, LANES)

    pieces = []
    for name, n_rows in SMALL_EARLY:
        g = grads[name]
        if name in SMALL_SHARDED:
            pieces.append(device_major(g.reshape(g.shape[0], -1) if g.ndim == 2 else g.reshape(g.shape[0], g.shape[-1]), n_rows))
        else:
            pieces.append(rows(g, n_rows))
    return jnp.concatenate(pieces, axis=0)


def _small_update_call(got_early, got_late, w, m, v, *, name):
    names = list(SMALL)
    n_p = len(names)
    E = SMALL_EARLY_ROWS

    def body(early_ref, late_ref, *rest):
        w_refs, m_refs, v_refs = [dict(zip(names, rest[i * n_p:(i + 1) * n_p])) for i in range(3)]
        outs, tot = rest[3 * n_p:-1], rest[-1]
        loss_out = outs[0]
        g_out, d_out, m_out, v_out = [dict(zip(names, outs[1 + i * n_p:1 + (i + 1) * n_p])) for i in range(4)]
        acc, acc_late = early_ref[0], late_ref[0]
        for d in range(1, N_DEV):
            acc, acc_late = acc + early_ref[d], acc_late + late_ref[d]
        tot[0:E, :] = acc
        tot[E:E + SMALL_LATE_ROWS, :] = acc_late
        loss_out[...] = tot[E + 8:E + 9, :]

        x, y, c = _position()
        me = 4 * x + 2 * y + c

        def update(name, g, ix):
            g_out[name][ix] = g
            d_out[name][ix], m_out[name][ix], v_out[name][ix] = _adamw_math(
                w_refs[name][ix], g, m_refs[name][ix], v_refs[name][ix])

        for name in names:
            shape = w[name].shape
            if name in SMALL_SHARDED:
                per_dev = dict(SMALL_EARLY)[name] // N_DEV
                at = pl.multiple_of(SMALL_EARLY_AT[name] + me * per_dev, 8)
                g = tot[pl.ds(at, per_dev), :]
                update(name, g[:shape[1], :shape[2]], (0,))
            elif name == "w_spatial":
                for grp in range(GROUPS):
                    at = SMALL_EARLY_AT[name] + grp * SBLOCK
                    update(name, tot[at:at + SBLOCK, :], (0, grp))
            elif name == "b_spatial":
                at = SMALL_EARLY_AT[name]
                update(name, tot[at:at + GROUPS, :], (0,))
            else:
                at = E if name == "norm_pre_mix" else SMALL_EARLY_AT[name]
                for k in range(shape[1] // LANES):
                    update(name, tot[at + k:at + k + 1, :], (slice(None), pl.ds(k * LANES, LANES)))

    state = [s[n] for s in (w, m, v) for n in names]
    out_shapes = [jax.ShapeDtypeStruct((1, LANES), F32)] + [jax.ShapeDtypeStruct(w[n].shape, F32) for n in names] * 4
    outs = pl.pallas_call(
        body, name=name,
        in_specs=[VMEM_SPEC] * (2 + len(state)), out_specs=[VMEM_SPEC] * len(out_shapes), out_shape=out_shapes,
        scratch_shapes=[pltpu.VMEM((E + SMALL_LATE_ROWS, LANES), F32)],
    )(got_early, got_late, *state)
    per_name = {n: tuple(outs[1 + i * n_p + j] for i in range(4)) for j, n in enumerate(names)}
    return outs[0], per_name


def _tile_rows(n_elems):
    return -(-n_elems // (8 * LANES)) * 8


def _pack_rows(parts, rows):
    pieces = []
    for p in parts:
        q = p.reshape(-1, LANES)
        pieces.append(jnp.pad(q, ((0, _tile_rows(p.size) - q.shape[0]), (0, 0))))
    buf = jnp.concatenate(pieces, axis=0)
    return jnp.pad(buf, ((0, rows - buf.shape[0]), (0, 0)))


def _in_w_blocks_call(a, live, finished, *, name, tk=TOKEN_TILE):
    S = a.shape[0]
    tk = min(tk, S)
    steps = S // tk
    n_live = len(live)
    n_chips = N_DEV // 2
    out_shape = (n_chips, IN_BLK, D_MODEL)

    def body(a_ref, *rest):
        live_refs, done_refs = rest[:n_live], rest[n_live:n_live + len(finished)]
        keep_ref, send_ref = rest[n_live + len(finished):n_live + len(finished) + 2]
        accs = rest[n_live + len(finished) + 2:]
        k = pl.program_id(0)

        @pl.when(k == 0)
        def _():
            for acc in accs:
                acc[...] = jnp.zeros_like(acc)

        av = a_ref[...]
        for src, acc in zip(live_refs, accs):
            for m0 in range(0, src.shape[1], 1024):
                m1 = min(src.shape[1], m0 + 1024)
                acc[m0:m1, :] += _dot(src[:, m0:m1], av, TN)

        @pl.when(k == steps - 1)
        def _():
            groups = [(acc, cols) for acc, (_, cols) in zip(accs, live)]
            groups += [(ref, cols) for ref, (_, cols) in zip(done_refs, finished)]
            core = lax.axis_index("c")

            def cut(d, out_ref):
                lo, hi = IN_BLK * d, IN_BLK * (d + 1)
                for ref, (c0, c1) in groups:
                    s0, e0 = max(lo, c0), min(hi, c1)
                    if s0 < e0:
                        out_ref[d // 2, s0 - lo:e0 - lo, :] = ref[s0 - c0:e0 - c0, :].astype(BF16)

            for d in range(N_DEV):
                pl.when(core == d % 2)(lambda d=d: cut(d, keep_ref))
                pl.when(core != d % 2)(lambda d=d: cut(d, send_ref))

    acc_shapes = [(arr.shape[1], D_MODEL) for arr, _ in live]
    tiles = [((tk, D_MODEL), BF16)] + [((tk, arr.shape[1]), BF16) for arr, _ in live]
    resident = ([(arr.shape, arr.dtype) for arr, _ in finished] + [(out_shape, BF16)] * 2
                + [(s, F32) for s in acc_shapes])
    return pl.pallas_call(
        body, name=name, grid=(steps,),
        in_specs=[pl.BlockSpec((tk, D_MODEL), lambda k: (k, 0))]
        + [pl.BlockSpec((tk, arr.shape[1]), lambda k: (k, 0)) for arr, _ in live]
        + [_res(arr.shape) for arr, _ in finished],
        out_specs=[_acc(out_shape)] * 2,
        out_shape=[pltpu.HBM(out_shape, BF16)] * 2,
        scratch_shapes=[pltpu.VMEM(s, F32) for s in acc_shapes],
        compiler_params=_cp(("arbitrary",), _vmem_limit(tiles, resident)),
    )(*_hbm(a, *[arr for arr, _ in live], *[arr for arr, _ in finished]))


def _local_step(x, target, W, scatter, finish, pair, chip_scatter):
    g1, g2, g3, g4 = [W[n].reshape(1, D_MODEL) for n in ("norm_pre_mix", "norm_post_mix", "norm_pre_ffn", "norm_post_ffn")]
    wfi, wfo = W["w_ffn_in"].reshape(2 * D_FF, D_MODEL), W["w_ffn_out"]
    wg = jnp.pad(W["w_gate_up"], ((0, LANES - RANK), (0, 0))).astype(BF16)
    bgate = W["b_gate"].reshape(1, QK)
    gnorm = W["gla_norm"].reshape(HEADS, 1, DV)
    ln_g = W["sgu_ln_g"].reshape(GROUPS, 1, DG)
    ln_b = W["sgu_ln_b"].reshape(GROUPS, 1, DG)
    w_sp = W["w_spatial"]
    b_sp = W["b_spatial"].reshape(GROUPS, SBLOCK, 1)

    a, pa, alow, ps, pg, w_in_t = _in_proj_call(x, g1, W["w_in_blocks"], name="in_proj")
    y_gla, states = _gla_fwd_call(pa, alow, wg, bgate, gnorm, name="gla_fwd")
    y_sgu = _sgu_fwd_call(ps, ln_g, ln_b, w_sp, b_sp, name="sgu_fwd")
    t1, t2, merged, mix, x1, h = _mixer_tail_call(y_gla, y_sgu, pg, x, W["w_branch_gla"], W["w_branch_sgu"],
                                                  W["w_out"], g2, g3, name="mixer_tail")
    gu, z = _ffn_in_call(h, wfi, name="ffn_in")
    loss, dx2, dy, dg4 = _ffn_out_loss_call(z, wfo, x1, target, g4, name="ffn_out_loss")

    grads = {"norm_post_ffn": dg4}
    done = {}
    dgu = _ffn_out_bwd_call(dy, wfo, gu, name="ffn_out_bwd")
    dw_ffn_out = _weight_grads_call([z, dy], [(0, 1)], name="d_ffn_out_w")[0].reshape(N_DEV, D_FF // N_DEV, D_MODEL)
    dw_ffn_in = _ffn_in_grad_call(h, dgu, name="d_ffn_in_w").reshape(N_DEV, FF_BLK, D_MODEL)
    dgu, dw_ffn_out, dw_ffn_in = lax.optimization_barrier((dgu, dw_ffn_out, dw_ffn_in))
    ffn_received = scatter(("w_ffn_out", "w_ffn_in"), [dw_ffn_out, dw_ffn_in])
    dx1, dmix, grads["norm_pre_ffn"], grads["norm_post_mix"] = _ffn_in_bwd_call(dgu, wfi, dx2, x1, mix, g3, g2, name="ffn_in_bwd")
    dt1, dt2, dpg, dy_gla, dy_sgu = _mixer_bwd_call(dmix, t1, t2, pg, W["w_out"], W["w_branch_gla"], W["w_branch_sgu"],
                                                    name="mixer_bwd")
    rows = D_MODEL // N_DEV
    mixer_grads = _weight_grads_call([merged, dmix, y_gla, dt1, y_sgu, dt2], [(0, 1), (2, 3), (4, 5)], name="d_mixer_w")
    dy_gla, dy_sgu, mixer_grads = lax.optimization_barrier((dy_gla, dy_sgu, mixer_grads))
    mixer_received = scatter(("w_out", "w_branch_gla", "w_branch_sgu"),
                             [g.reshape(N_DEV, rows, D_MODEL) for g in mixer_grads])
    dps, dlg, dlb, dwsp, dbsp = _sgu_bwd_call(ps, ln_g, ln_b, w_sp, b_sp, dy_sgu, name="sgu_bwd")
    dw_s, dw_g = _weight_grads_call([a, dps, dpg], [(1, 0), (2, 0)], name="d_in_w_sgu_gates")
    dy_gla, dw_s, dw_g = lax.optimization_barrier((dy_gla, dw_s, dw_g))
    dpa, dlogit, dgn, dbg = _gla_bwd_call(pa, alow, wg, bgate, gnorm, states, dy_gla, name="gla_bwd")
    ffn_received, mixer_received, dpa, dlogit = lax.optimization_barrier((ffn_received, mixer_received, dpa, dlogit))
    dlow = _mm(dlogit, wg, "nt", BF16, name="d_gate_up_x")
    keep, send = _in_w_blocks_call(a, [(dpa, A_COLS), (dlow, LOW_COLS)], [(dw_s, S_COLS), (dw_g, G_COLS)],
                                   name="d_in_w_blocks")
    ffn_received, mixer_received, send = lax.optimization_barrier((ffn_received, mixer_received, send))
    from_sibling = pair(send)
    done.update({**finish(ffn_received), **finish(mixer_received)})
    dwg = _mm(alow, dlogit, "tn", F32, name="d_gate_up_w")
    done, dwg, keep = lax.optimization_barrier((done, dwg, keep))
    chip_sum = _pair_sum_call(keep, from_sibling, name="pair_sum_w_in")
    chip_sum, dpa = lax.optimization_barrier((chip_sum, dpa))
    in_received = chip_scatter(chip_sum)
    grad_x, grads["norm_pre_mix"] = _in_proj_bwd_call(dpa, dlow, dps, dpg, w_in_t, x, dx1, g1, name="in_proj_bwd")

    grads["w_gate_up"] = dwg[:RANK]
    grads["b_gate"] = dbg
    grads["gla_norm"] = dgn
    grads["sgu_ln_g"] = dlg
    grads["sgu_ln_b"] = dlb
    grads["w_spatial"] = dwsp
    grads["b_spatial"] = dbsp
    done.update(finish({"w_in": in_received}))
    return loss, grad_x, grads, done


WEIGHTS = ("norm_pre_mix", "w_in", "w_gate_up", "b_gate", "gla_norm", "sgu_ln_g", "sgu_ln_b", "w_spatial",
           "b_spatial", "w_branch_gla", "w_branch_sgu", "w_out", "norm_post_mix", "norm_pre_ffn", "w_ffn_in",
           "w_ffn_out", "norm_post_ffn")
BIG = ("w_in", "w_branch_gla", "w_branch_sgu", "w_out", "w_ffn_in", "w_ffn_out")
MIXER = ("w_branch_gla", "w_branch_sgu", "w_out")
FFN = ("w_ffn_in", "w_ffn_out")
COLUMN_SHARDED = ("w_in", "w_ffn_in")
LATE_SCATTER_WAVES = 4
GATHER_WAVES = 4
SMALL = tuple(n for n in WEIGHTS if n not in BIG)
SMALL_SHARDED = ("w_gate_up", "gla_norm", "sgu_ln_g", "sgu_ln_b")
SMALL_GATHER_ROWS = 32


def kernel(x, norm_pre_mix, w_in, w_gate_up, b_gate, gla_norm, sgu_ln_g, sgu_ln_b, w_spatial, b_spatial, w_branch_gla, w_branch_sgu, w_out, norm_post_mix, norm_pre_ffn, w_ffn_in, w_ffn_out, norm_post_ffn, loss_target, m_norm_pre_mix, m_w_in, m_w_gate_up, m_b_gate, m_gla_norm, m_sgu_ln_g, m_sgu_ln_b, m_w_spatial, m_b_spatial, m_w_branch_gla, m_w_branch_sgu, m_w_out, m_norm_post_mix, m_norm_pre_ffn, m_w_ffn_in, m_w_ffn_out, m_norm_post_ffn, v_norm_pre_mix, v_w_in, v_w_gate_up, v_b_gate, v_gla_norm, v_sgu_ln_g, v_sgu_ln_b, v_w_spatial, v_b_spatial, v_w_branch_gla, v_w_branch_sgu, v_w_out, v_norm_post_mix, v_norm_pre_ffn, v_w_ffn_in, v_w_ffn_out, v_norm_post_ffn):
    given = dict(locals())
    def local(a, n):
        return a[0].T if n in COLUMN_SHARDED else a[0]

    w = {n: local(given[n], n) for n in WEIGHTS}
    m = {n: local(given["m_" + n], n) for n in WEIGHTS}
    v = {n: local(given["v_" + n], n) for n in WEIGHTS}
    xs, target = x[0], loss_target[0]

    small_shard = _pack_rows([w[n] for n in SMALL_SHARDED], SMALL_GATHER_ROWS)
    first = _all_gather_hbm([w["w_in"].astype(BF16), small_shard], name="gather_w_in", waves=GATHER_WAVES)
    rest, first = lax.optimization_barrier(([w[n].astype(BF16) for n in MIXER + FFN], first))
    rest_blocks = _all_gather_async(rest, name="gather_rest", collective_id=1)
    W = {n: w[n] for n in SMALL if n not in SMALL_SHARDED}
    blocks = dict(zip(MIXER + FFN, rest_blocks))
    for n in ("w_branch_gla", "w_branch_sgu", "w_out", "w_ffn_out"):
        W[n] = blocks[n].reshape(-1, D_MODEL)
    W["w_ffn_in"] = blocks["w_ffn_in"]
    W["w_in_blocks"] = first[0]
    small_blocks = first[1]
    off = 0
    for n in SMALL_SHARDED:
        r, c = w[n].shape
        blk = small_blocks[:, off:off + r * c // LANES].reshape(N_DEV, r, c)
        W[n] = blk.transpose(1, 0, 2).reshape(r, N_DEV * c)
        off += _tile_rows(r * c)

    scatter_ids = iter((3, 4))

    def scatter(names, parts):
        got = _scatter_blocks_async(parts, name="scatter_" + "_".join(names), collective_id=next(scatter_ids))
        return dict(zip(names, got))

    def finish(received):
        return {n: _adamw_reduce_call(r, w[n], m[n], v[n], name="adamw_" + n) for n, r in received.items()}

    def pair(part):
        return _pair_exchange_async([part], name="pair_w_in", collective_id=5)[0]

    def chip_scatter(part):
        return _chip_scatter_async([part], name="scatter_w_in", collective_id=6, waves=LATE_SCATTER_WAVES)[0]

    loss_part, grad_x, grads, big_done = _local_step(xs, target, W, scatter, finish, pair, chip_scatter)

    late = jnp.concatenate([grads["norm_pre_mix"].reshape(8, LANES), jnp.broadcast_to(loss_part, (8, LANES))], axis=0)
    gathered = _small_gather_async([_small_early_rows(grads), late], name="gather_small", collective_id=7)
    gathered, big_done["w_in"] = lax.optimization_barrier((gathered, big_done["w_in"]))
    loss_row, small_done = _small_update_call(
        *gathered, *[{n: given[prefix + n] for n in SMALL} for prefix in ("", "m_", "v_")], name="small_update")

    def pick(i):
        return [small_done[n][i] if n in SMALL else (big_done[n][i].T if n in COLUMN_SHARDED else big_done[n][i])[None]
                for n in WEIGHTS]

    return (loss_row[0, 0], grad_x[None], *pick(0), *pick(1), *pick(2), *pick(3))
```

```python
import jax
import jax.numpy as jnp
from jax import lax
from jax.experimental import pallas as pl
from jax.experimental.pallas import tpu as pltpu
from jax.experimental.pallas import tpu_sc as plsc

F32 = jnp.float32
BF16 = jnp.bfloat16

D_MODEL = 1024
N_DEV = 8
CHUNK = 64
HEADS = 4
DK = 128
DV = 256
QK = HEADS * DK
GV = HEADS * DV
RANK = 16
GROUPS = 4
SBLOCK = 128
DG = 256
D_FF = 2816
FF_BLK = 704
EPS = 1e-6
Q_SCALE = DK ** -0.5
LANES = 128
VMEM_BIG = 48 * 1024 * 1024

D_IN = 7184
IN_BLK = 898
A_COLS = (0, 3072)
LOW_COLS = (3072, 3088)
S_COLS = (3088, 5136)
G_COLS = (5136, 7184)

ADAM_LR = 0.001
ADAM_B1 = 0.9
ADAM_B2 = 0.999
ADAM_EPS = 1e-08
ADAM_WD = 0.01
ADAM_STEP = 10

MESH = pl.DeviceIdType.MESH
ANY = pl.BlockSpec(memory_space=pl.ANY)
VMEM_SPEC = pl.BlockSpec(memory_space=pltpu.VMEM)


def _cp(sem=None, vmem=None):
    return pltpu.CompilerParams(dimension_semantics=sem, vmem_limit_bytes=vmem)


def _hbm(*arrays):
    return [pltpu.with_memory_space_constraint(a, pltpu.HBM) for a in arrays]


def _sig(x):
    return 0.5 * jnp.tanh(0.5 * x) + 0.5


GELU_C = 0.7978845608028654
GELU_A = 0.044715


def _gelu(x):
    t = jnp.tanh((GELU_C * x) * (1.0 + GELU_A * (x * x)))
    return (0.5 * x) * (1.0 + t)


def _gelu_and_grad(x):
    x2 = x * x
    t = jnp.tanh((GELU_C * x) * (1.0 + GELU_A * x2))
    one_t = 1.0 + t
    hx = 0.5 * x
    grad = 0.5 * one_t + (hx * (1.0 - t * t)) * (GELU_C + (3.0 * GELU_A * GELU_C) * x2)
    return hx * one_t, grad


def _logsig(x):
    return jnp.minimum(x, 0.0) - jnp.log1p(jnp.exp(-jnp.abs(x)))


def _dot(a, b, dims):
    return lax.dot_general(a, b, (dims, ((), ())), preferred_element_type=F32)


NN = ((1,), (0,))
NT = ((1,), (1,))
TN = ((0,), (0,))


def _exact_mask_dot(mask_bf16, x):
    hi = x.astype(BF16)
    r1 = x - hi.astype(F32)
    mid = r1.astype(BF16)
    lo = (r1 - mid.astype(F32)).astype(BF16)
    return _dot(mask_bf16, hi, NN) + _dot(mask_bf16, mid, NN) + _dot(mask_bf16, lo, NN)


def _pick_tile(dim, target):
    if dim <= target:
        return dim
    best = None
    for t in range(LANES, int(1.4 * target) + 1, LANES):
        if dim % t == 0:
            best = t
    assert best is not None, (dim, target)
    return best


def _mm_call(a, b, *, name, grid, a_spec, b_spec, o_spec, out_shape, dims, acc_shape):
    nk = grid[2]

    def body(a_ref, b_ref, o_ref, *acc):
        part = _dot(a_ref[...], b_ref[...], dims)
        if nk == 1:
            o_ref[...] = part.astype(o_ref.dtype)
        else:
            acc_ref = acc[0]
            k = pl.program_id(2)

            @pl.when(k == 0)
            def _():
                acc_ref[...] = part

            @pl.when(k > 0)
            def _():
                acc_ref[...] += part

            @pl.when(k == nk - 1)
            def _():
                o_ref[...] = acc_ref[...].astype(o_ref.dtype)

    return pl.pallas_call(
        body, name=name, grid=grid, in_specs=[a_spec, b_spec], out_specs=o_spec, out_shape=out_shape,
        scratch_shapes=[] if nk == 1 else [pltpu.VMEM(acc_shape, F32)],
        compiler_params=_cp(("parallel", "parallel", "arbitrary"), VMEM_BIG),
    )(*_hbm(a, b))


def _mm(a, b, mode, out_dtype, *, name, tm=512, tn=1024, tk=1024):
    if mode == "nn":
        (M, K), (_, N) = a.shape, b.shape
    elif mode == "nt":
        (M, K), (N, _) = a.shape, b.shape
    else:
        (K, M), (_, N) = a.shape, b.shape
    tm, tn, tk = _pick_tile(M, tm), _pick_tile(N, tn), _pick_tile(K, tk)
    if mode == "nn":
        a_spec = pl.BlockSpec((tm, tk), lambda i, j, k: (i, k))
        b_spec = pl.BlockSpec((tk, tn), lambda i, j, k: (k, j))
        dims = NN
    elif mode == "nt":
        a_spec = pl.BlockSpec((tm, tk), lambda i, j, k: (i, k))
        b_spec = pl.BlockSpec((tn, tk), lambda i, j, k: (j, k))
        dims = NT
    else:
        a_spec = pl.BlockSpec((tk, tm), lambda i, j, k: (k, i))
        b_spec = pl.BlockSpec((tk, tn), lambda i, j, k: (k, j))
        dims = TN
    return _mm_call(a, b, name=name, grid=(M // tm, N // tn, K // tk), a_spec=a_spec, b_spec=b_spec,
                    o_spec=pl.BlockSpec((tm, tn), lambda i, j, k: (i, j)),
                    out_shape=pltpu.HBM((M, N), out_dtype), dims=dims, acc_shape=(tm, tn))


ROW_TILE = 512
SUB_ROWS = 256


def _sub_tiles(T):
    return [pl.ds(r0, min(SUB_ROWS, T)) for r0 in range(0, T, SUB_ROWS)]


def _rt(T, W, c=0):
    return pl.BlockSpec((T, W), lambda i: (i, c))


def _rt3(nb, T, W):
    return pl.BlockSpec((nb, T, W), lambda i: (0, i, 0))


def _res(shape):
    nd = len(shape)
    return pl.BlockSpec(tuple(shape), lambda i: (0,) * nd, pipeline_mode=pl.Buffered(1))


def _acc(shape):
    nd = len(shape)
    return pl.BlockSpec(tuple(shape), lambda i: (0,) * nd, pipeline_mode=pl.Buffered(1))


def _nbytes(shape, dtype):
    n = jnp.dtype(dtype).itemsize
    for s in shape:
        n *= s
    return n


def _vmem_limit(tiles, resident, temps=16 * 1024 * 1024):
    need = 2 * sum(_nbytes(s, d) for s, d in tiles) + sum(_nbytes(s, d) for s, d in resident) + temps
    return min(need, 60 * 1024 * 1024)


def _tok_call(body, *, name, S, T, ins, outs, semantics="parallel"):
    tiles = [(spec.block_shape, a.dtype) for a, spec, kind in ins + outs if kind == "tile"]
    resident = [(a.shape, a.dtype) for a, spec, kind in ins + outs if kind == "res"]
    return pl.pallas_call(
        body, name=name, grid=(S // T,),
        in_specs=[spec for _, spec, _ in ins], out_specs=[spec for _, spec, _ in outs],
        out_shape=[pltpu.HBM(a.shape, a.dtype) for a, _, _ in outs],
        compiler_params=_cp((semantics,), _vmem_limit(tiles, resident)),
    )(*_hbm(*[a for a, _, _ in ins]))


def _tile(a, spec):
    return (a, spec, "tile")


def _whole(a):
    return (a, _res(a.shape), "res")


def _out_tile(shape, dtype, spec):
    return (jax.ShapeDtypeStruct(shape, dtype), spec, "tile")


def _out_acc(shape, dtype=F32):
    return (jax.ShapeDtypeStruct(shape, dtype), _acc(shape), "res")


def _rms_stats(x):
    r = lax.rsqrt(jnp.mean(x * x, axis=-1, keepdims=True) + EPS)
    return r, x * r


def _rms_bwd(xh, r, g, dy):
    dxh = dy * g
    dx = r * (dxh - xh * jnp.mean(dxh * xh, axis=-1, keepdims=True))
    dg = jnp.sum(dy * xh, axis=0, keepdims=True)
    return dx, dg


def _accum(ref, val):
    @pl.when(pl.program_id(0) == 0)
    def _():
        ref[...] = val

    @pl.when(pl.program_id(0) > 0)
    def _():
        ref[...] += val


def _dot_rows_t(a, w_ref, row0, o_ref, chunk=1024):
    n = o_ref.shape[1]
    for n0 in range(0, n, chunk):
        n1 = min(n, n0 + chunk)
        o_ref[:, n0:n1] = _dot(a, w_ref[row0 + n0:row0 + n1, :], NT).astype(o_ref.dtype)


def _in_proj_call(x, g1, w_blocks, *, name, T=ROW_TILE):
    S = x.shape[0]

    def body(x_ref, g_ref, blk_ref, a_ref, pa_ref, al_ref, ps_ref, pg_ref, w_ref):
        @pl.when(pl.program_id(0) == 0)
        def _():
            for d in range(N_DEV):
                w_ref[d * IN_BLK:(d + 1) * IN_BLK, :] = blk_ref[d]

        _, xh = _rms_stats(x_ref[...])
        a = (xh * g_ref[...]).astype(BF16)
        a_ref[...] = a
        _dot_rows_t(a, w_ref, A_COLS[0], pa_ref)
        _dot_rows_t(a, w_ref, LOW_COLS[0], al_ref)
        _dot_rows_t(a, w_ref, S_COLS[0], ps_ref)
        _dot_rows_t(a, w_ref, G_COLS[0], pg_ref)

    widths = (D_MODEL, A_COLS[1] - A_COLS[0], LANES, S_COLS[1] - S_COLS[0], G_COLS[1] - G_COLS[0])
    return _tok_call(
        body, name=name, S=S, T=T, semantics="arbitrary",
        ins=[_tile(x, _rt(T, D_MODEL)), _whole(g1), _whole(w_blocks)],
        outs=[_out_tile((S, w), BF16, _rt(T, w)) for w in widths] + [_out_acc((D_IN, D_MODEL), BF16)])


def _mixer_tail_call(y_gla, y_sgu, pg, x, w_bg, w_bs, w_out, g2, g3, *, name, T=ROW_TILE):
    S = x.shape[0]

    def body(yg_ref, ys_ref, pg_ref, x_ref, wbg_ref, wbs_ref, wo_ref, g2_ref, g3_ref,
             t1_ref, t2_ref, mg_ref, mix_ref, x1_ref, h_ref):
        for rows in _sub_tiles(T):
            t1 = _dot(yg_ref[rows, :], wbg_ref[...], NN)
            t2 = _dot(ys_ref[rows, :], wbs_ref[...], NN)
            t1_ref[rows, :] = t1.astype(BF16)
            t2_ref[rows, :] = t2.astype(BF16)
            sg = _sig(pg_ref[rows, pl.ds(0, D_MODEL)].astype(F32))
            ss = _sig(pg_ref[rows, pl.ds(D_MODEL, D_MODEL)].astype(F32))
            merged = (sg * t1 + ss * t2).astype(BF16)
            mg_ref[rows, :] = merged
            mix = _dot(merged, wo_ref[...], NN)
            mix_ref[rows, :] = mix
            _, mh = _rms_stats(mix)
            x1 = x_ref[rows, :] + mh * g2_ref[...]
            x1_ref[rows, :] = x1
            _, xh = _rms_stats(x1)
            h_ref[rows, :] = (xh * g3_ref[...]).astype(BF16)

    row = _rt(T, D_MODEL)
    b16 = lambda: _out_tile((S, D_MODEL), BF16, row)
    f32 = lambda: _out_tile((S, D_MODEL), F32, row)
    return _tok_call(
        body, name=name, S=S, T=T,
        ins=[_tile(y_gla, row), _tile(y_sgu, row), _tile(pg, _rt(T, 2 * D_MODEL)), _tile(x, row),
             _whole(w_bg), _whole(w_bs), _whole(w_out), _whole(g2), _whole(g3)],
        outs=[b16(), b16(), b16(), f32(), f32(), b16()])


FF_CHUNKS = ((0, 1024), (1024, 2048), (2048, D_FF))


def _ffn_in_call(h, wfi, *, name, T=ROW_TILE):
    S = h.shape[0]

    def body(h_ref, w_ref, gu_ref, z_ref):
        hv = h_ref[...]
        for c0, c1 in FF_CHUNKS:
            gate = _dot(hv, w_ref[c0:c1, :], NT)
            up = _dot(hv, w_ref[D_FF + c0:D_FF + c1, :], NT)
            gu_ref[:, c0:c1] = gate.astype(BF16)
            gu_ref[:, D_FF + c0:D_FF + c1] = up.astype(BF16)
            z_ref[:, c0:c1] = (gate * _sig(gate) * up).astype(BF16)

    return _tok_call(
        body, name=name, S=S, T=T,
        ins=[_tile(h, _rt(T, D_MODEL)), _whole(wfi)],
        outs=[_out_tile((S, 2 * D_FF), BF16, _rt(T, 2 * D_FF)),
              _out_tile((S, D_FF), BF16, _rt(T, D_FF))])


def _ffn_out_loss_call(z, wfo, x1, target, g4, *, name, T=ROW_TILE):
    S = x1.shape[0]

    def body(z_ref, w_ref, x1_ref, t_ref, g4_ref, loss_ref, dx2_ref, dy_ref, dg4_ref):
        loss = jnp.zeros((1, 1), F32)
        dg4 = jnp.zeros((1, D_MODEL), F32)
        for rows in _sub_tiles(T):
            y = _dot(z_ref[rows, :], w_ref[...], NN)
            r, yh = _rms_stats(y)
            diff = x1_ref[rows, :] + yh * g4_ref[...] - t_ref[rows, :]
            loss = loss + 0.5 * jnp.sum(jnp.mean(diff * diff, axis=-1, keepdims=True), axis=0, keepdims=True)
            dx2 = diff * (1.0 / D_MODEL)
            dx2_ref[rows, :] = dx2
            dy, dg = _rms_bwd(yh, r, g4_ref[...], dx2)
            dy_ref[rows, :] = dy.astype(BF16)
            dg4 = dg4 + dg
        _accum(loss_ref, jnp.broadcast_to(loss, (1, LANES)))
        _accum(dg4_ref, dg4)

    row = _rt(T, D_MODEL)
    return _tok_call(
        body, name=name, S=S, T=T, semantics="arbitrary",
        ins=[_tile(z, _rt(T, D_FF)), _whole(wfo), _tile(x1, row), _tile(target, row), _whole(g4)],
        outs=[_out_acc((1, LANES)), _out_tile((S, D_MODEL), F32, row), _out_tile((S, D_MODEL), BF16, row),
              _out_acc((1, D_MODEL))])


def _ffn_out_bwd_call(dy, wfo, gu, *, name, T=ROW_TILE):
    S = dy.shape[0]

    def body(dy_ref, w_ref, gu_ref, dgu_ref):
        dyv = dy_ref[...]
        for c0, c1 in FF_CHUNKS:
            dz = _dot(dyv, w_ref[c0:c1, :], NT).astype(BF16)
            gt = gu_ref[:, c0:c1]
            up = gu_ref[:, D_FF + c0:D_FF + c1]
            s = _sig(gt)
            dgu_ref[:, c0:c1] = dz * up * s * (1.0 + gt * (1.0 - s))
            dgu_ref[:, D_FF + c0:D_FF + c1] = dz * gt * s

    wide = _rt(T, 2 * D_FF)
    return _tok_call(
        body, name=name, S=S, T=T,
        ins=[_tile(dy, _rt(T, D_MODEL)), _whole(wfo), _tile(gu, wide)],
        outs=[_out_tile((S, 2 * D_FF), BF16, wide)])[0]


def _ffn_in_bwd_call(dgu, wfi, dx2, x1, mix, g3, g2, *, name, T=ROW_TILE):
    S = x1.shape[0]

    def body(dgu_ref, w_ref, dx2_ref, x1_ref, mix_ref, g3_ref, g2_ref, dx1_ref, dmix_ref, dg3_ref, dg2_ref):
        dg3 = jnp.zeros((1, D_MODEL), F32)
        dg2 = jnp.zeros((1, D_MODEL), F32)
        for rows in _sub_tiles(T):
            dh = _dot(dgu_ref[rows, :], w_ref[...], NN)
            r3, xh = _rms_stats(x1_ref[rows, :])
            d3, g = _rms_bwd(xh, r3, g3_ref[...], dh)
            dg3 = dg3 + g
            dx1 = dx2_ref[rows, :] + d3
            dx1_ref[rows, :] = dx1
            r2, mh = _rms_stats(mix_ref[rows, :])
            dmix, g = _rms_bwd(mh, r2, g2_ref[...], dx1)
            dg2 = dg2 + g
            dmix_ref[rows, :] = dmix.astype(BF16)
        _accum(dg3_ref, dg3)
        _accum(dg2_ref, dg2)

    row = _rt(T, D_MODEL)
    return _tok_call(
        body, name=name, S=S, T=T, semantics="arbitrary",
        ins=[_tile(dgu, _rt(T, 2 * D_FF)), _whole(wfi), _tile(dx2, row), _tile(x1, row), _tile(mix, row),
             _whole(g3), _whole(g2)],
        outs=[_out_tile((S, D_MODEL), F32, row), _out_tile((S, D_MODEL), BF16, row),
              _out_acc((1, D_MODEL)), _out_acc((1, D_MODEL))])


def _mixer_bwd_call(dmix, t1, t2, pg, w_out, w_bg, w_bs, *, name, T=ROW_TILE):
    S = dmix.shape[0]

    def body(dmix_ref, t1_ref, t2_ref, pg_ref, wo_ref, wbg_ref, wbs_ref, dt1_ref, dt2_ref, dpg_ref, dyg_ref, dys_ref):
        gg, gs = pl.ds(0, D_MODEL), pl.ds(D_MODEL, D_MODEL)
        for rows in _sub_tiles(T):
            dm = _dot(dmix_ref[rows, :], wo_ref[...], NT)
            sg = _sig(pg_ref[rows, gg].astype(F32))
            ss = _sig(pg_ref[rows, gs].astype(F32))
            dt1 = (dm * sg).astype(BF16)
            dt2 = (dm * ss).astype(BF16)
            dt1_ref[rows, :] = dt1
            dt2_ref[rows, :] = dt2
            dpg_ref[rows, gg] = (dm * t1_ref[rows, :].astype(F32) * sg * (1.0 - sg)).astype(BF16)
            dpg_ref[rows, gs] = (dm * t2_ref[rows, :].astype(F32) * ss * (1.0 - ss)).astype(BF16)
            dyg_ref[rows, :] = _dot(dt1, wbg_ref[...], NT).astype(BF16)
            dys_ref[rows, :] = _dot(dt2, wbs_ref[...], NT).astype(BF16)

    row = _rt(T, D_MODEL)
    wide = _rt(T, 2 * D_MODEL)
    b16 = lambda: _out_tile((S, D_MODEL), BF16, row)
    return _tok_call(
        body, name=name, S=S, T=T,
        ins=[_tile(dmix, row), _tile(t1, row), _tile(t2, row), _tile(pg, wide), _whole(w_out), _whole(w_bg), _whole(w_bs)],
        outs=[b16(), b16(), _out_tile((S, 2 * D_MODEL), BF16, wide), b16(), b16()])


def _in_proj_bwd_call(dpa, dlow, dps, dpg, w_in_t, x, dx1, g1, *, name, T=ROW_TILE):
    S = x.shape[0]

    def body(dpa_ref, dl_ref, dps_ref, dpg_ref, w_ref, x_ref, dx1_ref, g1_ref, gx_ref, dg1_ref):
        da = (_dot(dpa_ref[...], w_ref[A_COLS[0]:A_COLS[1], :], NN)
              + _dot(dl_ref[...], w_ref[LOW_COLS[0]:LOW_COLS[0] + LANES, :], NN)
              + _dot(dps_ref[...], w_ref[S_COLS[0]:S_COLS[1], :], NN)
              + _dot(dpg_ref[...], w_ref[G_COLS[0]:G_COLS[1], :], NN))
        r, xh = _rms_stats(x_ref[...])
        dxa, dg = _rms_bwd(xh, r, g1_ref[...], da)
        gx_ref[...] = dx1_ref[...] + dxa
        _accum(dg1_ref, dg)

    row = _rt(T, D_MODEL)
    return _tok_call(
        body, name=name, S=S, T=T, semantics="arbitrary",
        ins=[_tile(dpa, _rt(T, dpa.shape[1])), _tile(dlow, _rt(T, dlow.shape[1])), _tile(dps, _rt(T, dps.shape[1])),
             _tile(dpg, _rt(T, dpg.shape[1])), _whole(w_in_t), _tile(x, row), _tile(dx1, row), _whole(g1)],
        outs=[_out_tile((S, D_MODEL), F32, row), _out_acc((1, D_MODEL))])


TOKEN_TILE = 512


def _weight_grads_part(arrays, pairs, *, tk=TOKEN_TILE, out_dtype=BF16):
    S = arrays[0].shape[-2]
    tk = min(tk, S)
    n_in = len(arrays)

    def out_shape(i, j):
        a, b = arrays[i], arrays[j]
        if a.ndim == 3:
            return (a.shape[0], a.shape[2], b.shape[1])
        if b.ndim == 3:
            return (b.shape[0], a.shape[1], b.shape[2])
        return (a.shape[1], b.shape[1])

    shapes = [out_shape(i, j) for i, j in pairs]

    in_place = out_dtype == F32

    def body(ins, outs, accs):
        k = pl.program_id(0)
        if in_place:
            accs = outs

        @pl.when(k == 0)
        def _():
            for acc in accs:
                acc[...] = jnp.zeros_like(acc)

        for (i, j), acc in zip(pairs, accs):
            a_ref, b_ref = ins[i], ins[j]
            if len(a_ref.shape) == 3:
                for n in range(a_ref.shape[0]):
                    acc[n] += _dot(a_ref[n], b_ref[...], TN)
            elif len(b_ref.shape) == 3:
                a = a_ref[...]
                for n in range(b_ref.shape[0]):
                    acc[n] += _dot(a, b_ref[n], TN)
            else:
                b = b_ref[...]
                for m0 in range(0, a_ref.shape[1], 1024):
                    m1 = min(a_ref.shape[1], m0 + 1024)
                    acc[m0:m1, :] += _dot(a_ref[:, m0:m1], b, TN)

        if not in_place:
            @pl.when(k == S // tk - 1)
            def _():
                for out, acc in zip(outs, accs):
                    out[...] = acc[...].astype(out.dtype)

    def in_spec(a):
        if a.ndim == 3:
            return pl.BlockSpec((a.shape[0], tk, a.shape[2]), lambda k: (0, k, 0))
        return pl.BlockSpec((tk, a.shape[1]), lambda k: (k, 0))

    return dict(
        body=body, steps=S // tk, arrays=list(arrays),
        in_specs=[in_spec(a) for a in arrays],
        out_specs=[_acc(s) for s in shapes],
        out_shapes=[pltpu.HBM(s, out_dtype) for s in shapes],
        scratch=[] if in_place else [pltpu.VMEM(s, F32) for s in shapes],
        tiles=[(in_spec(a).block_shape, a.dtype) for a in arrays],
        resident=[(s, F32) for s in shapes] + ([] if in_place else [(s, BF16) for s in shapes]))


def _weight_grads_call(arrays, pairs, *, name, tk=TOKEN_TILE):
    part = _weight_grads_part(arrays, pairs, tk=tk)
    n_in, n_out = len(part["arrays"]), len(part["out_shapes"])

    def body(*refs):
        part["body"](refs[:n_in], refs[n_in:n_in + n_out], refs[n_in + n_out:])

    return pl.pallas_call(
        body, name=name, grid=(part["steps"],),
        in_specs=part["in_specs"], out_specs=part["out_specs"], out_shape=part["out_shapes"],
        scratch_shapes=part["scratch"],
        compiler_params=_cp(("arbitrary",), _vmem_limit(part["tiles"], part["resident"])),
    )(*_hbm(*part["arrays"]))


def _ffn_in_grad_call(h, dgu, *, name, tk=TOKEN_TILE):
    S = h.shape[0]
    tk = min(tk, S)
    shape = (D_FF, D_MODEL)

    def body(h_ref, dgu_ref, out_ref, acc):
        k = pl.program_id(1)

        @pl.when(k == 0)
        def _():
            acc[...] = jnp.zeros_like(acc)

        hv = h_ref[...]
        for c0, c1 in FF_CHUNKS:
            acc[c0:c1, :] += _dot(dgu_ref[:, c0:c1], hv, TN)

        @pl.when(k == S // tk - 1)
        def _():
            out_ref[...] = acc[...].astype(out_ref.dtype)

    tiles = [((tk, D_MODEL), BF16), ((tk, D_FF), BF16), (shape, BF16)]
    return pl.pallas_call(
        body, name=name, grid=(2, S // tk),
        in_specs=[pl.BlockSpec((tk, D_MODEL), lambda g, k: (k, 0)),
                  pl.BlockSpec((tk, D_FF), lambda g, k: (k, g))],
        out_specs=pl.BlockSpec(shape, lambda g, k: (g, 0)),
        out_shape=pltpu.HBM((2 * D_FF, D_MODEL), BF16),
        scratch_shapes=[pltpu.VMEM(shape, F32)],
        compiler_params=_cp(("parallel", "arbitrary"), _vmem_limit(tiles, [(shape, F32)])),
    )(*_hbm(h, dgu))


GLA_TILE = 256
Q_OFF, K_OFF, V_OFF, R_OFF = 0, QK, 2 * QK, 2 * QK + GV


def _tri(lower):
    r = lax.broadcasted_iota(jnp.int32, (CHUNK, CHUNK), 0)
    c = lax.broadcasted_iota(jnp.int32, (CHUNK, CHUNK), 1)
    return jnp.where((r >= c) if lower else (c >= r), 1.0, 0.0).astype(BF16)


def _gla_fwd_call(pa, alow, wg, bgate, gnorm, *, name):
    S = pa.shape[0]
    Tg = min(GLA_TILE, S)
    cb = Tg // CHUNK

    def body(pa_ref, al_ref, wg_ref, bg_ref, gn_ref, y_ref, st_ref, state):
        @pl.when(pl.program_id(0) == 0)
        def _():
            state[...] = jnp.zeros_like(state)

        logit = _dot(al_ref[...], wg_ref[...], NN) + bg_ref[...]
        ls = _logsig(logit) * (1.0 / 16.0)
        tri = _tri(True)
        for c in range(cb):
            rows = pl.ds(c * CHUNK, CHUNK)
            cum = _exact_mask_dot(tri, ls[c * CHUNK:(c + 1) * CHUNK])
            tot = cum[CHUNK - 1:CHUNK]
            kd = (pa_ref[rows, pl.ds(K_OFF, QK)].astype(F32) * jnp.exp(tot - cum)).astype(BF16)
            decay = jnp.exp(tot)
            for h in range(HEADS):
                lanes = slice(h * DK, (h + 1) * DK)
                new = state[h] * decay[:, lanes] + _dot(pa_ref[rows, pl.ds(V_OFF + h * DV, DV)], kd[:, lanes], TN)
                state[h] = new
                st_ref[c, h] = new
        for c in range(cb):
            rows = pl.ds(c * CHUNK, CHUNK)
            for h in range(HEADS):
                qs = (pa_ref[rows, pl.ds(Q_OFF + h * DK, DK)].astype(F32) * Q_SCALE).astype(BF16)
                o = _dot(qs, st_ref[c, h].astype(BF16), NT)
                rs = lax.rsqrt(jnp.mean(o * o, axis=-1, keepdims=True) + EPS)
                rr = pa_ref[rows, pl.ds(R_OFF + h * DV, DV)].astype(F32)
                y_ref[rows, pl.ds(h * DV, DV)] = (o * rs * gn_ref[h] * (rr * _sig(rr))).astype(BF16)

    return pl.pallas_call(
        body, name=name, grid=(S // Tg,),
        in_specs=[
            pl.BlockSpec((Tg, 2 * QK + 2 * GV), lambda t: (t, 0)),
            pl.BlockSpec((Tg, LANES), lambda t: (t, 0)),
            pl.BlockSpec((LANES, QK), lambda t: (0, 0)),
            pl.BlockSpec((1, QK), lambda t: (0, 0)),
            pl.BlockSpec((HEADS, 1, DV), lambda t: (0, 0, 0)),
        ],
        out_specs=[
            pl.BlockSpec((Tg, GV), lambda t: (t, 0)),
            pl.BlockSpec((cb, HEADS, DV, DK), lambda t: (t, 0, 0, 0)),
        ],
        out_shape=[pltpu.HBM((S, GV), BF16), pltpu.HBM((S // CHUNK, HEADS, DV, DK), F32)],
        scratch_shapes=[pltpu.VMEM((HEADS, DV, DK), F32)],
        compiler_params=_cp(("arbitrary",), VMEM_BIG),
    )(*_hbm(pa, alow, wg, bgate, gnorm))


def _gla_bwd_call(pa, alow, wg, bgate, gnorm, states, dy, *, name):
    S = pa.shape[0]
    Tg = min(GLA_TILE, S)
    cb = Tg // CHUNK
    nt = S // Tg

    def rev(t):
        return nt - 1 - t

    def body(pa_ref, al_ref, wg_ref, bg_ref, gn_ref, st_ref, prev_ref, dy_ref,
             dpa_ref, dl_ref, dgn_ref, dbg_ref, carry, pbuf):
        t = pl.program_id(0)

        @pl.when(t == 0)
        def _():
            carry[...] = jnp.zeros_like(carry)
            dgn_ref[...] = jnp.zeros_like(dgn_ref)
            dbg_ref[...] = jnp.zeros_like(dbg_ref)

        logit = _dot(al_ref[...], wg_ref[...], NN) + bg_ref[...]
        ls = _logsig(logit) * (1.0 / 16.0)
        sneg = 1.0 / (1.0 + jnp.exp(logit))
        tri = _tri(True)
        upper = _tri(False)
        first_tile = rev(t) == 0
        heads = range(HEADS)

        w, decay, kd = [], [], []
        for c in range(cb):
            rows = pl.ds(c * CHUNK, CHUNK)
            cum = _exact_mask_dot(tri, ls[c * CHUNK:(c + 1) * CHUNK])
            tot = cum[CHUNK - 1:CHUNK]
            w.append(jnp.exp(tot - cum))
            decay.append(jnp.exp(tot))
            kd.append(pa_ref[rows, pl.ds(K_OFF, QK)].astype(F32) * w[c])

        dgn = [jnp.zeros((1, DV), F32) for _ in heads]
        for c in range(cb):
            rows = pl.ds(c * CHUNK, CHUNK)
            for h in heads:
                gn = gn_ref[h]
                qs = (pa_ref[rows, pl.ds(Q_OFF + h * DK, DK)].astype(F32) * Q_SCALE).astype(BF16)
                st16 = st_ref[c, h].astype(BF16)
                o = _dot(qs, st16, NT)
                rs = lax.rsqrt(jnp.mean(o * o, axis=-1, keepdims=True) + EPS)
                oh = o * rs
                rr = pa_ref[rows, pl.ds(R_OFF + h * DV, DV)].astype(F32)
                sr = _sig(rr)
                dyv = dy_ref[rows, pl.ds(h * DV, DV)].astype(F32)
                dpa_ref[rows, pl.ds(R_OFF + h * DV, DV)] = (
                    dyv * oh * gn * sr * (1.0 + rr * (1.0 - sr))).astype(BF16)
                don = dyv * (rr * sr)
                dgn[h] = dgn[h] + jnp.sum(don * oh, axis=0, keepdims=True)
                doh = don * gn
                do16 = (rs * (doh - oh * jnp.mean(doh * oh, axis=-1, keepdims=True))).astype(BF16)
                dpa_ref[rows, pl.ds(Q_OFF + h * DK, DK)] = (_dot(do16, st16, NN) * Q_SCALE).astype(BF16)
                pbuf[c, h] = _dot(do16, qs, TN)
        for h in heads:
            dgn_ref[h] += dgn[h]

        dkd = [[None] * HEADS for _ in range(cb)]
        ddecay = [[None] * HEADS for _ in range(cb)]
        for c in reversed(range(cb)):
            rows = pl.ds(c * CHUNK, CHUNK)
            for h in heads:
                lanes = slice(h * DK, (h + 1) * DK)
                gt = pbuf[c, h] + carry[h]
                gt16 = gt.astype(BF16)
                dkd[c][h] = _dot(pa_ref[rows, pl.ds(V_OFF + h * DV, DV)], gt16, NN)
                dpa_ref[rows, pl.ds(V_OFF + h * DV, DV)] = _dot(kd[c][:, lanes].astype(BF16), gt16, NT).astype(BF16)
                if c > 0:
                    st_prev = st_ref[c - 1, h]
                else:
                    st_prev = jnp.where(first_tile, 0.0, prev_ref[0, h])
                ddecay[c][h] = jnp.sum(gt * st_prev, axis=0, keepdims=True)
                carry[h] = gt * decay[c][:, lanes]

        dbg = jnp.zeros((1, QK), F32)
        for c in range(cb):
            rows = pl.ds(c * CHUNK, CHUNK)
            dkd_c = jnp.concatenate(dkd[c], axis=1)
            dpa_ref[rows, pl.ds(K_OFF, QK)] = (dkd_c * w[c]).astype(BF16)
            e = dkd_c * kd[c]
            dtot = jnp.sum(e, axis=0, keepdims=True) + jnp.concatenate(ddecay[c], axis=1) * decay[c]
            dls = dtot - _exact_mask_dot(upper, e)
            dlogit = dls * (1.0 / 16.0) * sneg[c * CHUNK:(c + 1) * CHUNK]
            dl_ref[rows, :] = dlogit.astype(BF16)
            dbg = dbg + jnp.sum(dlogit, axis=0, keepdims=True)
        dbg_ref[...] += dbg

    wide = 2 * QK + 2 * GV
    tiles = [((Tg, wide), BF16), ((cb + 1, HEADS, DV, DK), F32), ((Tg, GV), BF16), ((Tg, wide), BF16),
             ((Tg, QK), BF16)]
    resident = [((cb + 1, HEADS, DV, DK), F32)]
    return pl.pallas_call(
        body, name=name, grid=(nt,),
        in_specs=[
            pl.BlockSpec((Tg, wide), lambda t: (rev(t), 0)),
            pl.BlockSpec((Tg, LANES), lambda t: (rev(t), 0)),
            pl.BlockSpec((LANES, QK), lambda t: (0, 0)),
            pl.BlockSpec((1, QK), lambda t: (0, 0)),
            pl.BlockSpec((HEADS, 1, DV), lambda t: (0, 0, 0)),
            pl.BlockSpec((cb, HEADS, DV, DK), lambda t: (rev(t), 0, 0, 0)),
            pl.BlockSpec((1, HEADS, DV, DK), lambda t: (jnp.maximum(rev(t) * cb - 1, 0), 0, 0, 0)),
            pl.BlockSpec((Tg, GV), lambda t: (rev(t), 0)),
        ],
        out_specs=[
            pl.BlockSpec((Tg, wide), lambda t: (rev(t), 0)),
            pl.BlockSpec((Tg, QK), lambda t: (rev(t), 0)),
            pl.BlockSpec((HEADS, 1, DV), lambda t: (0, 0, 0)),
            pl.BlockSpec((1, QK), lambda t: (0, 0)),
        ],
        out_shape=[pltpu.HBM((S, wide), BF16), pltpu.HBM((S, QK), BF16),
                   jax.ShapeDtypeStruct((HEADS, 1, DV), F32), jax.ShapeDtypeStruct((1, QK), F32)],
        scratch_shapes=[pltpu.VMEM((HEADS, DV, DK), F32), pltpu.VMEM((cb, HEADS, DV, DK), F32)],
        compiler_params=_cp(("arbitrary",), _vmem_limit(tiles, resident)),
    )(*_hbm(pa, alow, wg, bgate, gnorm, states, states, dy))


SGU_TILE = 256


def _sgu_mask():
    r = lax.broadcasted_iota(jnp.int32, (SBLOCK, SBLOCK), 0)
    c = lax.broadcasted_iota(jnp.int32, (SBLOCK, SBLOCK), 1)
    return (c < CHUNK) | (r >= CHUNK)


def _ln_stats(vf):
    mu = jnp.mean(vf, axis=-1, keepdims=True)
    xc = vf - mu
    rs = lax.rsqrt(jnp.mean(xc * xc, axis=-1, keepdims=True) + EPS)
    return rs, xc * rs


def _sgu_fwd_call(ps, ln_g, ln_b, w_sp, b_sp, *, name):
    S = ps.shape[0]
    Ts = min(SGU_TILE, S)

    def body(ps_ref, lg_ref, lb_ref, w_ref, b_ref, y_ref):
        mask = _sgu_mask()
        for g in range(GROUPS):
            wm = jnp.where(mask, w_ref[g], 0.0).astype(BF16)
            for p in range(Ts // SBLOCK):
                rows = pl.ds(p * SBLOCK, SBLOCK)
                u = _gelu(ps_ref[rows, pl.ds(g * DG, DG)].astype(F32))
                _, xh = _ln_stats(_gelu(ps_ref[rows, pl.ds(D_MODEL + g * DG, DG)].astype(F32)))
                vn = xh * lg_ref[g] + lb_ref[g]
                mixed = _dot(wm, vn.astype(BF16), NN) + b_ref[g]
                y_ref[rows, pl.ds(g * DG, DG)] = (u * mixed).astype(BF16)

    full3 = lambda a, b, c: pl.BlockSpec((a, b, c), lambda t: (0, 0, 0))
    return pl.pallas_call(
        body, name=name, grid=(S // Ts,),
        in_specs=[pl.BlockSpec((Ts, 2 * D_MODEL), lambda t: (t, 0)),
                  full3(GROUPS, 1, DG), full3(GROUPS, 1, DG), full3(GROUPS, SBLOCK, SBLOCK), full3(GROUPS, SBLOCK, 1)],
        out_specs=pl.BlockSpec((Ts, D_MODEL), lambda t: (t, 0)),
        out_shape=pltpu.HBM((S, D_MODEL), BF16),
        compiler_params=_cp(("parallel",)),
    )(*_hbm(ps, ln_g, ln_b, w_sp, b_sp))


def _sgu_bwd_call(ps, ln_g, ln_b, w_sp, b_sp, dy, *, name):
    S = ps.shape[0]
    Ts = min(SGU_TILE, S)

    def body(ps_ref, lg_ref, lb_ref, w_ref, b_ref, dy_ref, ds_ref, dlg_ref, dlb_ref, dw_ref, db_ref):
        @pl.when(pl.program_id(0) == 0)
        def _():
            dlg_ref[...] = jnp.zeros_like(dlg_ref)
            dlb_ref[...] = jnp.zeros_like(dlb_ref)
            dw_ref[...] = jnp.zeros_like(dw_ref)
            db_ref[...] = jnp.zeros_like(db_ref)

        mask = _sgu_mask()
        for g in range(GROUPS):
            wm = jnp.where(mask, w_ref[g], 0.0).astype(BF16)
            lg = lg_ref[g]
            for p in range(Ts // SBLOCK):
                rows = pl.ds(p * SBLOCK, SBLOCK)
                su = ps_ref[rows, pl.ds(g * DG, DG)].astype(F32)
                sv = ps_ref[rows, pl.ds(D_MODEL + g * DG, DG)].astype(F32)
                u, du = _gelu_and_grad(su)
                gv, dgv = _gelu_and_grad(sv)
                rs, xh = _ln_stats(gv)
                vn16 = (xh * lg + lb_ref[g]).astype(BF16)
                mixed = _dot(wm, vn16, NN) + b_ref[g]
                dyv = dy_ref[rows, pl.ds(g * DG, DG)].astype(F32)
                ds_ref[rows, pl.ds(g * DG, DG)] = (dyv * mixed * du).astype(BF16)
                dmix = dyv * u
                dmix16 = dmix.astype(BF16)
                db_ref[g] += jnp.sum(dmix, axis=-1, keepdims=True)
                dw_ref[g] += jnp.where(mask, _dot(dmix16, vn16, NT), 0.0)
                dvn = _dot(wm, dmix16, TN)
                dlg_ref[g] += jnp.sum(dvn * xh, axis=0, keepdims=True)
                dlb_ref[g] += jnp.sum(dvn, axis=0, keepdims=True)
                dxh = dvn * lg
                dvf = rs * (dxh - jnp.mean(dxh, axis=-1, keepdims=True)
                            - xh * jnp.mean(dxh * xh, axis=-1, keepdims=True))
                ds_ref[rows, pl.ds(D_MODEL + g * DG, DG)] = (dvf * dgv).astype(BF16)

    full3 = lambda a, b, c: pl.BlockSpec((a, b, c), lambda t: (0, 0, 0))
    tiles = [((Ts, 2 * D_MODEL), BF16), ((Ts, D_MODEL), BF16), ((Ts, 2 * D_MODEL), BF16)]
    return pl.pallas_call(
        body, name=name, grid=(S // Ts,),
        in_specs=[pl.BlockSpec((Ts, 2 * D_MODEL), lambda t: (t, 0)),
                  full3(GROUPS, 1, DG), full3(GROUPS, 1, DG), full3(GROUPS, SBLOCK, SBLOCK), full3(GROUPS, SBLOCK, 1),
                  pl.BlockSpec((Ts, D_MODEL), lambda t: (t, 0))],
        out_specs=[pl.BlockSpec((Ts, 2 * D_MODEL), lambda t: (t, 0)),
                   full3(GROUPS, 1, DG), full3(GROUPS, 1, DG), full3(GROUPS, SBLOCK, SBLOCK), full3(GROUPS, SBLOCK, 1)],
        out_shape=[pltpu.HBM((S, 2 * D_MODEL), BF16),
                   jax.ShapeDtypeStruct((GROUPS, 1, DG), F32), jax.ShapeDtypeStruct((GROUPS, 1, DG), F32),
                   jax.ShapeDtypeStruct((GROUPS, SBLOCK, SBLOCK), F32),
                   jax.ShapeDtypeStruct((GROUPS, SBLOCK, 1), F32)],
        compiler_params=_cp(("arbitrary",), _vmem_limit(tiles, [])),
    )(*_hbm(ps, ln_g, ln_b, w_sp, b_sp, dy))


def _position():
    return lax.axis_index("x"), lax.axis_index("y"), lax.axis_index("c")


def _gather_copies(srcs, dsts, send_sems, recv_sems, local_sems):
    x, y, c = _position()
    me, sibling = (x, y, c), (x, y, 1 - c)
    chips = [(1 - x, y), (x, 1 - y), (1 - x, 1 - y)]
    n = len(srcs)

    def slab(a, block):
        px, py, pc = block
        return dsts[a].at[4 * px + 2 * py + pc]

    def copy(a, k, block, to, src=None):
        return pltpu.make_async_remote_copy(
            src_ref=slab(a, block) if src is None else src, dst_ref=slab(a, block),
            send_sem=send_sems.at[7 * a + k], recv_sem=recv_sems.at[7 * a + k], device_id=to, device_id_type=MESH)

    mine = [pltpu.make_async_copy(srcs[a], slab(a, me), local_sems.at[a]) for a in range(n)]
    for cp in mine:
        cp.start()
    first = []
    for a in range(n):
        first.append(copy(a, 0, me, sibling, src=srcs[a]))
        first += [copy(a, 1 + j, me, (*chip, c), src=srcs[a]) for j, chip in enumerate(chips)]
    for cp in first:
        cp.start()
    passed = []
    for j, chip in enumerate(chips):
        for a in range(n):
            copy(a, 1 + j, (*chip, c), me).wait_recv()
            cp = copy(a, 4 + j, (*chip, c), sibling)
            cp.start()
            passed.append(cp)
    for a in range(n):
        copy(a, 0, sibling, me).wait_recv()
        for j, chip in enumerate(chips):
            copy(a, 4 + j, (*chip, 1 - c), me).wait_recv()
    for cp in first + passed:
        cp.wait_send()
    for cp in mine:
        cp.wait()


def _gather_copies_relayed(srcs, dsts, send_sems, recv_sems, local_sems, waves=1):
    x, y, c = _position()
    me, sibling = (x, y, c), (x, y, 1 - c)
    x_chip, y_chip, d_chip = (1 - x, y), (x, 1 - y), (1 - x, 1 - y)
    south = c == 0
    relay_from = (jnp.where(south, x, 1 - x), jnp.where(south, 1 - y, y), c)
    relay_to = (jnp.where(south, 1 - x, x), jnp.where(south, y, 1 - y), c)
    n = len(srcs)

    def cut(a, w):
        rows = srcs[a].shape[0]
        step = (rows // waves) // 16 * 16
        if waves == 1 or step == 0:
            return pl.ds(0, rows) if w == 0 else None
        return pl.ds(w * step, step if w < waves - 1 else rows - w * step)

    def slab(a, block, w):
        px, py, pc = block
        return dsts[a].at[4 * px + 2 * py + pc, cut(a, w)]

    def copy(a, k, w, block, to, own=False):
        sem = (7 * a + k) * waves + w
        return pltpu.make_async_remote_copy(
            src_ref=srcs[a].at[cut(a, w)] if own else slab(a, block, w), dst_ref=slab(a, block, w),
            send_sem=send_sems.at[sem], recv_sem=recv_sems.at[sem], device_id=to, device_id_type=MESH)

    pieces = [(a, w) for w in range(waves) for a in range(n) if cut(a, w) is not None]
    mine = [pltpu.make_async_copy(srcs[a], dsts[a].at[4 * x + 2 * y + c], local_sems.at[a]) for a in range(n)]
    for cp in mine:
        cp.start()
    sent = []
    for a, w in pieces:
        sent += [copy(a, 0, w, me, sibling, own=True), copy(a, 1, w, me, (*x_chip, c), own=True),
                 copy(a, 2, w, me, (*y_chip, c), own=True)]
    for cp in sent:
        cp.start()
    for a, w in pieces:
        copy(a, 1, w, (*x_chip, c), me).wait_recv()
        copy(a, 2, w, (*y_chip, c), me).wait_recv()
        later = [copy(a, 3, w, relay_from, relay_to), copy(a, 4, w, (*x_chip, c), sibling),
                 copy(a, 5, w, (*y_chip, c), sibling)]
        for cp in later:
            cp.start()
        sent += later
    for a, w in pieces:
        copy(a, 3, w, (*d_chip, c), me).wait_recv()
        cp = copy(a, 6, w, (*d_chip, c), sibling)
        cp.start()
        sent.append(cp)
    for a, w in pieces:
        copy(a, 0, w, sibling, me).wait_recv()
        for k, chip in ((4, x_chip), (5, y_chip), (6, d_chip)):
            copy(a, k, w, (*chip, 1 - c), me).wait_recv()
    for cp in sent:
        cp.wait_send()
    for cp in mine:
        cp.wait()


def _all_gather_hbm(shards, *, name, waves=1):
    n = len(shards)

    def body(*refs):
        srcs, dsts = refs[:n], refs[n:2 * n]
        send_sems, recv_sems, local_sems = refs[2 * n:]
        _gather_copies_relayed(srcs, dsts, send_sems, recv_sems, local_sems, waves=waves)

    return pl.pallas_call(
        body, name=name,
        in_specs=[ANY] * n, out_specs=[ANY] * n,
        out_shape=[jax.ShapeDtypeStruct((N_DEV, *s.shape), s.dtype) for s in shards],
        scratch_shapes=_comm_sems(n, waves),
    )(*shards)


FLIPS = [(fx, fy, fc) for fx in (0, 1) for fy in (0, 1) for fc in (0, 1)][1:]


def _scatter_copies(srcs, dsts, send_sems, recv_sems, local_sems, waves=1):
    n = len(srcs)
    x, y, c = _position()
    me = 4 * x + 2 * y + c
    mine = [pltpu.make_async_copy(srcs[a].at[me], dsts[a].at[me], local_sems.at[a]) for a in range(n)]
    for cp in mine:
        cp.start()
    peers = []
    for fx, fy, fc in FLIPS:
        tx = 1 - x if fx else x
        ty = 1 - y if fy else y
        tc = 1 - c if fc else c
        peers.append(((tx, ty, tc), 4 * tx + 2 * ty + tc))
    copies = []
    for w in range(waves):
        wave = []
        for k, (peer_id, peer) in enumerate(peers):
            for a in range(n):
                rows = srcs[a].shape[1]
                step = rows if waves == 1 else (rows // waves) // 16 * 16
                r0 = w * step
                cut = pl.ds(r0, step if w < waves - 1 else rows - r0)
                sem = (7 * a + k) * waves + w
                cp = pltpu.make_async_remote_copy(
                    src_ref=srcs[a].at[peer, cut], dst_ref=dsts[a].at[me, cut],
                    send_sem=send_sems.at[sem], recv_sem=recv_sems.at[sem],
                    device_id=peer_id, device_id_type=MESH)
                cp.start()
                wave.append(cp)
        for cp in wave:
            cp.wait_send()
        copies += wave
    for cp in copies:
        cp.wait_recv()
    for cp in mine:
        cp.wait()


def _pair_copies(srcs, dsts, send_sems, recv_sems, local_sems):
    x, y, c = _position()
    copies = [pltpu.make_async_remote_copy(
        src_ref=srcs[a], dst_ref=dsts[a], send_sem=send_sems.at[a], recv_sem=recv_sems.at[a],
        device_id=(x, y, 1 - c), device_id_type=MESH) for a in range(len(srcs))]
    for cp in copies:
        cp.start()
    for cp in copies:
        cp.wait()


def _chip_scatter_copies(srcs, dsts, send_sems, recv_sems, local_sems, waves=1):
    n = len(srcs)
    x, y, c = _position()
    here = 2 * x + y
    mine = [pltpu.make_async_copy(srcs[a].at[here], dsts[a].at[here], local_sems.at[a]) for a in range(n)]
    for cp in mine:
        cp.start()
    peers = []
    for fx, fy in ((1, 0), (0, 1), (1, 1)):
        tx = 1 - x if fx else x
        ty = 1 - y if fy else y
        peers.append(((tx, ty, c), 2 * tx + ty))
    copies = []
    for w in range(waves):
        wave = []
        for k, (peer_id, peer) in enumerate(peers):
            for a in range(n):
                rows = srcs[a].shape[1]
                step = rows if waves == 1 else (rows // waves) // 16 * 16
                r0 = w * step
                cut = pl.ds(r0, step if w < waves - 1 else rows - r0)
                sem = (3 * a + k) * waves + w
                cp = pltpu.make_async_remote_copy(
                    src_ref=srcs[a].at[peer, cut], dst_ref=dsts[a].at[here, cut],
                    send_sem=send_sems.at[sem], recv_sem=recv_sems.at[sem],
                    device_id=peer_id, device_id_type=MESH)
                cp.start()
                wave.append(cp)
        for cp in wave:
            cp.wait_send()
        copies += wave
    for cp in copies:
        cp.wait_recv()
    for cp in mine:
        cp.wait()


def _comm_sems(n, waves=1):
    return [pltpu.SemaphoreType.DMA((7 * n * waves,)), pltpu.SemaphoreType.DMA((7 * n * waves,)),
            pltpu.SemaphoreType.DMA((n,))]


def _handshake(peers):
    barrier = pltpu.get_barrier_semaphore()
    for peer in peers:
        pl.semaphore_signal(barrier, inc=1, device_id=peer, device_id_type=MESH)
    pl.semaphore_wait(barrier, len(peers))


def _sequencer_call(arrays, out_types, copies_fn, peers_fn, *, name, collective_id, waves=1):
    n = len(arrays)
    srcs = [jax.new_ref(a, memory_space=pltpu.MemorySpace.HBM) for a in arrays]
    dsts = [jax.empty_ref(t, memory_space=pltpu.MemorySpace.HBM) for t in out_types]
    extra = {} if waves == 1 else {"waves": waves}

    @pl.kernel(mesh=plsc.ScalarSubcoreMesh(axis_name="sequencer", num_cores=1), name=name,
               scratch_types=_comm_sems(n, waves), compiler_params=pltpu.CompilerParams(collective_id=collective_id))
    def launch(send_sems, recv_sems, local_sems):
        _handshake(peers_fn())
        copies_fn(srcs, dsts, send_sems, recv_sems, local_sems, **extra)

    launch()
    return [d[...] for d in dsts]


def _all_other_devices():
    x, y, c = _position()
    return [(1 - x if fx else x, 1 - y if fy else y, 1 - c if fc else c) for fx, fy, fc in FLIPS]


def _gather_relay_peers():
    x, y, c = _position()
    return [(x, y, 1 - c), (1 - x, y, c), (x, 1 - y, c)]


def _gather_peers():
    x, y, c = _position()
    return [(x, y, 1 - c), (1 - x, y, c), (x, 1 - y, c), (1 - x, 1 - y, c)]


def _small_gather_async(shards, *, name, collective_id):
    return _sequencer_call(shards, [jax.ShapeDtypeStruct((N_DEV, *s.shape), s.dtype) for s in shards],
                           _gather_copies, _gather_peers, name=name, collective_id=collective_id)


def _sibling():
    x, y, c = _position()
    return [(x, y, 1 - c)]


def _same_core_of_other_chips():
    x, y, c = _position()
    return [(1 - x, y, c), (x, 1 - y, c), (1 - x, 1 - y, c)]


def _pair_exchange_async(parts, *, name, collective_id):
    return _sequencer_call(parts, [jax.ShapeDtypeStruct(p.shape, p.dtype) for p in parts],
                           _pair_copies, _sibling, name=name, collective_id=collective_id)


def _chip_scatter_async(parts, *, name, collective_id, waves=1):
    return _sequencer_call(parts, [jax.ShapeDtypeStruct(p.shape, p.dtype) for p in parts],
                           _chip_scatter_copies, _same_core_of_other_chips, name=name, collective_id=collective_id,
                           waves=waves)


def _pair_sum_call(mine, theirs, *, name, tw=D_MODEL):
    n, r, w = mine.shape

    def body(a_ref, b_ref, o_ref):
        o_ref[...] = (a_ref[...].astype(F32) + b_ref[...].astype(F32)).astype(o_ref.dtype)

    spec = pl.BlockSpec((None, r, tw), lambda i, j: (i, 0, j))
    return pl.pallas_call(
        body, name=name, grid=(n, w // tw), in_specs=[spec, spec], out_specs=spec,
        out_shape=pltpu.HBM(mine.shape, mine.dtype),
        compiler_params=_cp(("parallel", "parallel")),
    )(*_hbm(mine, theirs))


def _scatter_blocks_async(parts, *, name, collective_id, waves=1):
    return _sequencer_call(parts, [jax.ShapeDtypeStruct(p.shape, p.dtype) for p in parts],
                           _scatter_copies, _all_other_devices, name=name, collective_id=collective_id, waves=waves)


def _all_gather_async(shards, *, name, collective_id):
    return _sequencer_call(shards, [jax.ShapeDtypeStruct((N_DEV, *s.shape), s.dtype) for s in shards],
                           _gather_copies_relayed, _gather_relay_peers, name=name, collective_id=collective_id)


def _adamw_math(w, g, m, v):
    m = ADAM_B1 * m + (1.0 - ADAM_B1) * g
    v = ADAM_B2 * v + (1.0 - ADAM_B2) * (g * g)
    m_hat = m / (1.0 - ADAM_B1 ** ADAM_STEP)
    v_hat = v / (1.0 - ADAM_B2 ** ADAM_STEP)
    delta = -ADAM_LR * (m_hat / (jnp.sqrt(v_hat) + ADAM_EPS) + ADAM_WD * w)
    return delta, m, v


def _adamw_reduce_call(recv, w, m, v, *, name, T=128):
    R, W = w.shape
    n_parts = recv.shape[0]
    if R % T == 0:
        tr, tw = T, W
    elif (R // 2) % 16 == 0:
        tr, tw = R // 2, W
    else:
        tr, tw = R, 2 * LANES

    def body(p_ref, w_ref, m_ref, v_ref, g_out, d_out, m_out, v_out):
        g = p_ref[0].astype(F32)
        for d in range(1, n_parts):
            g = g + p_ref[d].astype(F32)
        g_out[...] = g
        d_out[...], m_out[...], v_out[...] = _adamw_math(w_ref[...], g, m_ref[...], v_ref[...])

    row = pl.BlockSpec((tr, tw), lambda i, j: (i, j))
    out = pltpu.HBM((R, W), F32)
    return pl.pallas_call(
        body, name=name, grid=(R // tr, W // tw),
        in_specs=[pl.BlockSpec((n_parts, tr, tw), lambda i, j: (0, i, j)), row, row, row],
        out_specs=[row] * 4, out_shape=[out] * 4,
        compiler_params=_cp(("parallel", "parallel"), VMEM_BIG),
    )(*_hbm(recv, w, m, v))


SMALL_EARLY = (("b_gate", 8), ("w_spatial", 512), ("b_spatial", 8), ("norm_post_mix", 8), ("norm_pre_ffn", 8),
               ("norm_post_ffn", 8), ("w_gate_up", 128), ("gla_norm", 64), ("sgu_ln_g", 64), ("sgu_ln_b", 64))
SMALL_EARLY_AT = {}
for _name, _rows in SMALL_EARLY:
    SMALL_EARLY_AT[_name] = sum(r for _, r in SMALL_EARLY[:len(SMALL_EARLY_AT)])
SMALL_EARLY_ROWS = sum(r for _, r in SMALL_EARLY)
SMALL_LATE_ROWS = 16


def _small_early_rows(grads):
    def rows(a, n_rows):
        a = a.reshape(-1, LANES)
        return jnp.pad(a, ((0, n_rows - a.shape[0]), (0, 0)))

    def device_major(a, n_rows):
        r, c = a.shape[0], a.shape[1] // N_DEV
        a = a.reshape(r, N_DEV, c).transpose(1, 0, 2)
        a = jnp.pad(a, ((0, 0), (0, n_rows // N_DEV - r), (0, LANES - c)))
        return a.reshape(n_rows, LANES)

    pieces = []
    for name, n_rows in SMALL_EARLY:
        g = grads[name]
        if name in SMALL_SHARDED:
            pieces.append(device_major(g.reshape(g.shape[0], -1) if g.ndim == 2 else g.reshape(g.shape[0], g.shape[-1]), n_rows))
        else:
            pieces.append(rows(g, n_rows))
    return jnp.concatenate(pieces, axis=0)


def _small_update_call(got_early, got_late, w, m, v, *, name):
    names = list(SMALL)
    n_p = len(names)
    E = SMALL_EARLY_ROWS

    def body(early_ref, late_ref, *rest):
        w_refs, m_refs, v_refs = [dict(zip(names, rest[i * n_p:(i + 1) * n_p])) for i in range(3)]
        outs, tot = rest[3 * n_p:-1], rest[-1]
        loss_out = outs[0]
        g_out, d_out, m_out, v_out = [dict(zip(names, outs[1 + i * n_p:1 + (i + 1) * n_p])) for i in range(4)]
        acc, acc_late = early_ref[0], late_ref[0]
        for d in range(1, N_DEV):
            acc, acc_late = acc + early_ref[d], acc_late + late_ref[d]
        tot[0:E, :] = acc
        tot[E:E + SMALL_LATE_ROWS, :] = acc_late
        loss_out[...] = tot[E + 8:E + 9, :]

        x, y, c = _position()
        me = 4 * x + 2 * y + c

        def update(name, g, ix):
            g_out[name][ix] = g
            d_out[name][ix], m_out[name][ix], v_out[name][ix] = _adamw_math(
                w_refs[name][ix], g, m_refs[name][ix], v_refs[name][ix])

        for name in names:
            shape = w[name].shape
            if name in SMALL_SHARDED:
                per_dev = dict(SMALL_EARLY)[name] // N_DEV
                at = pl.multiple_of(SMALL_EARLY_AT[name] + me * per_dev, 8)
                g = tot[pl.ds(at, per_dev), :]
                update(name, g[:shape[1], :shape[2]], (0,))
            elif name == "w_spatial":
                for grp in range(GROUPS):
                    at = SMALL_EARLY_AT[name] + grp * SBLOCK
                    update(name, tot[at:at + SBLOCK, :], (0, grp))
            elif name == "b_spatial":
                at = SMALL_EARLY_AT[name]
                update(name, tot[at:at + GROUPS, :], (0,))
            else:
                at = E if name == "norm_pre_mix" else SMALL_EARLY_AT[name]
                for k in range(shape[1] // LANES):
                    update(name, tot[at + k:at + k + 1, :], (slice(None), pl.ds(k * LANES, LANES)))

    state = [s[n] for s in (w, m, v) for n in names]
    out_shapes = [jax.ShapeDtypeStruct((1, LANES), F32)] + [jax.ShapeDtypeStruct(w[n].shape, F32) for n in names] * 4
    outs = pl.pallas_call(
        body, name=name,
        in_specs=[VMEM_SPEC] * (2 + len(state)), out_specs=[VMEM_SPEC] * len(out_shapes), out_shape=out_shapes,
        scratch_shapes=[pltpu.VMEM((E + SMALL_LATE_ROWS, LANES), F32)],
    )(got_early, got_late, *state)
    per_name = {n: tuple(outs[1 + i * n_p + j] for i in range(4)) for j, n in enumerate(names)}
    return outs[0], per_name


def _tile_rows(n_elems):
    return -(-n_elems // (8 * LANES)) * 8


def _pack_rows(parts, rows):
    pieces = []
    for p in parts:
        q = p.reshape(-1, LANES)
        pieces.append(jnp.pad(q, ((0, _tile_rows(p.size) - q.shape[0]), (0, 0))))
    buf = jnp.concatenate(pieces, axis=0)
    return jnp.pad(buf, ((0, rows - buf.shape[0]), (0, 0)))


def _in_w_blocks_call(a, live, finished, *, name, tk=TOKEN_TILE):
    S = a.shape[0]
    tk = min(tk, S)
    steps = S // tk
    n_live = len(live)
    n_chips = N_DEV // 2
    out_shape = (n_chips, IN_BLK, D_MODEL)

    def body(a_ref, *rest):
        live_refs, done_refs = rest[:n_live], rest[n_live:n_live + len(finished)]
        keep_ref, send_ref = rest[n_live + len(finished):n_live + len(finished) + 2]
        accs = rest[n_live + len(finished) + 2:]
        k = pl.program_id(0)

        @pl.when(k == 0)
        def _():
            for acc in accs:
                acc[...] = jnp.zeros_like(acc)

        av = a_ref[...]
        for src, acc in zip(live_refs, accs):
            for m0 in range(0, src.shape[1], 1024):
                m1 = min(src.shape[1], m0 + 1024)
                acc[m0:m1, :] += _dot(src[:, m0:m1], av, TN)

        @pl.when(k == steps - 1)
        def _():
            groups = [(acc, cols) for acc, (_, cols) in zip(accs, live)]
            groups += [(ref, cols) for ref, (_, cols) in zip(done_refs, finished)]
            core = lax.axis_index("c")

            def cut(d, out_ref):
                lo, hi = IN_BLK * d, IN_BLK * (d + 1)
                for ref, (c0, c1) in groups:
                    s0, e0 = max(lo, c0), min(hi, c1)
                    if s0 < e0:
                        out_ref[d // 2, s0 - lo:e0 - lo, :] = ref[s0 - c0:e0 - c0, :].astype(BF16)

            for d in range(N_DEV):
                pl.when(core == d % 2)(lambda d=d: cut(d, keep_ref))
                pl.when(core != d % 2)(lambda d=d: cut(d, send_ref))

    acc_shapes = [(arr.shape[1], D_MODEL) for arr, _ in live]
    tiles = [((tk, D_MODEL), BF16)] + [((tk, arr.shape[1]), BF16) for arr, _ in live]
    resident = ([(arr.shape, arr.dtype) for arr, _ in finished] + [(out_shape, BF16)] * 2
                + [(s, F32) for s in acc_shapes])
    return pl.pallas_call(
        body, name=name, grid=(steps,),
        in_specs=[pl.BlockSpec((tk, D_MODEL), lambda k: (k, 0))]
        + [pl.BlockSpec((tk, arr.shape[1]), lambda k: (k, 0)) for arr, _ in live]
        + [_res(arr.shape) for arr, _ in finished],
        out_specs=[_acc(out_shape)] * 2,
        out_shape=[pltpu.HBM(out_shape, BF16)] * 2,
        scratch_shapes=[pltpu.VMEM(s, F32) for s in acc_shapes],
        compiler_params=_cp(("arbitrary",), _vmem_limit(tiles, resident)),
    )(*_hbm(a, *[arr for arr, _ in live], *[arr for arr, _ in finished]))


def _local_step(x, target, W, scatter, finish, pair, chip_scatter):
    g1, g2, g3, g4 = [W[n].reshape(1, D_MODEL) for n in ("norm_pre_mix", "norm_post_mix", "norm_pre_ffn", "norm_post_ffn")]
    wfi, wfo = W["w_ffn_in"].reshape(2 * D_FF, D_MODEL), W["w_ffn_out"]
    wg = jnp.pad(W["w_gate_up"], ((0, LANES - RANK), (0, 0))).astype(BF16)
    bgate = W["b_gate"].reshape(1, QK)
    gnorm = W["gla_norm"].reshape(HEADS, 1, DV)
    ln_g = W["sgu_ln_g"].reshape(GROUPS, 1, DG)
    ln_b = W["sgu_ln_b"].reshape(GROUPS, 1, DG)
    w_sp = W["w_spatial"]
    b_sp = W["b_spatial"].reshape(GROUPS, SBLOCK, 1)

    a, pa, alow, ps, pg, w_in_t = _in_proj_call(x, g1, W["w_in_blocks"], name="in_proj")
    y_gla, states = _gla_fwd_call(pa, alow, wg, bgate, gnorm, name="gla_fwd")
    y_sgu = _sgu_fwd_call(ps, ln_g, ln_b, w_sp, b_sp, name="sgu_fwd")
    t1, t2, merged, mix, x1, h = _mixer_tail_call(y_gla, y_sgu, pg, x, W["w_branch_gla"], W["w_branch_sgu"],
                                                  W["w_out"], g2, g3, name="mixer_tail")
    gu, z = _ffn_in_call(h, wfi, name="ffn_in")
    loss, dx2, dy, dg4 = _ffn_out_loss_call(z, wfo, x1, target, g4, name="ffn_out_loss")

    grads = {"norm_post_ffn": dg4}
    done = {}
    dgu = _ffn_out_bwd_call(dy, wfo, gu, name="ffn_out_bwd")
    dw_ffn_out = _weight_grads_call([z, dy], [(0, 1)], name="d_ffn_out_w")[0].reshape(N_DEV, D_FF // N_DEV, D_MODEL)
    dw_ffn_in = _ffn_in_grad_call(h, dgu, name="d_ffn_in_w").reshape(N_DEV, FF_BLK, D_MODEL)
    dgu, dw_ffn_out, dw_ffn_in = lax.optimization_barrier((dgu, dw_ffn_out, dw_ffn_in))
    ffn_received = scatter(("w_ffn_out", "w_ffn_in"), [dw_ffn_out, dw_ffn_in])
    dx1, dmix, grads["norm_pre_ffn"], grads["norm_post_mix"] = _ffn_in_bwd_call(dgu, wfi, dx2, x1, mix, g3, g2, name="ffn_in_bwd")
    dt1, dt2, dpg, dy_gla, dy_sgu = _mixer_bwd_call(dmix, t1, t2, pg, W["w_out"], W["w_branch_gla"], W["w_branch_sgu"],
                                                    name="mixer_bwd")
    rows = D_MODEL // N_DEV
    mixer_grads = _weight_grads_call([merged, dmix, y_gla, dt1, y_sgu, dt2], [(0, 1), (2, 3), (4, 5)], name="d_mixer_w")
    dy_gla, dy_sgu, mixer_grads = lax.optimization_barrier((dy_gla, dy_sgu, mixer_grads))
    mixer_received = scatter(("w_out", "w_branch_gla", "w_branch_sgu"),
                             [g.reshape(N_DEV, rows, D_MODEL) for g in mixer_grads])
    dps, dlg, dlb, dwsp, dbsp = _sgu_bwd_call(ps, ln_g, ln_b, w_sp, b_sp, dy_sgu, name="sgu_bwd")
    dw_s, dw_g = _weight_grads_call([a, dps, dpg], [(1, 0), (2, 0)], name="d_in_w_sgu_gates")
    dy_gla, dw_s, dw_g = lax.optimization_barrier((dy_gla, dw_s, dw_g))
    dpa, dlogit, dgn, dbg = _gla_bwd_call(pa, alow, wg, bgate, gnorm, states, dy_gla, name="gla_bwd")
    ffn_received, mixer_received, dpa, dlogit = lax.optimization_barrier((ffn_received, mixer_received, dpa, dlogit))
    dlow = _mm(dlogit, wg, "nt", BF16, name="d_gate_up_x", tm=2048)
    keep, send = _in_w_blocks_call(a, [(dpa, A_COLS), (dlow, LOW_COLS)], [(dw_s, S_COLS), (dw_g, G_COLS)],
                                   name="d_in_w_blocks")
    ffn_received, mixer_received, send = lax.optimization_barrier((ffn_received, mixer_received, send))
    from_sibling = pair(send)
    done.update({**finish(ffn_received), **finish(mixer_received)})
    dwg = _mm(alow, dlogit, "tn", F32, name="d_gate_up_w", tk=2048)
    done, dwg, keep = lax.optimization_barrier((done, dwg, keep))
    chip_sum = _pair_sum_call(keep, from_sibling, name="pair_sum_w_in")
    chip_sum, dpa = lax.optimization_barrier((chip_sum, dpa))
    in_received = chip_scatter(chip_sum)
    grad_x, grads["norm_pre_mix"] = _in_proj_bwd_call(dpa, dlow, dps, dpg, w_in_t, x, dx1, g1, name="in_proj_bwd")

    grads["w_gate_up"] = dwg[:RANK]
    grads["b_gate"] = dbg
    grads["gla_norm"] = dgn
    grads["sgu_ln_g"] = dlg
    grads["sgu_ln_b"] = dlb
    grads["w_spatial"] = dwsp
    grads["b_spatial"] = dbsp
    done.update(finish({"w_in": in_received}))
    return loss, grad_x, grads, done


WEIGHTS = ("norm_pre_mix", "w_in", "w_gate_up", "b_gate", "gla_norm", "sgu_ln_g", "sgu_ln_b", "w_spatial",
           "b_spatial", "w_branch_gla", "w_branch_sgu", "w_out", "norm_post_mix", "norm_pre_ffn", "w_ffn_in",
           "w_ffn_out", "norm_post_ffn")
BIG = ("w_in", "w_branch_gla", "w_branch_sgu", "w_out", "w_ffn_in", "w_ffn_out")
MIXER = ("w_branch_gla", "w_branch_sgu", "w_out")
FFN = ("w_ffn_in", "w_ffn_out")
COLUMN_SHARDED = ("w_in", "w_ffn_in")
LATE_SCATTER_WAVES = 4
GATHER_WAVES = 4
SMALL = tuple(n for n in WEIGHTS if n not in BIG)
SMALL_SHARDED = ("w_gate_up", "gla_norm", "sgu_ln_g", "sgu_ln_b")
SMALL_GATHER_ROWS = 32


def kernel(x, norm_pre_mix, w_in, w_gate_up, b_gate, gla_norm, sgu_ln_g, sgu_ln_b, w_spatial, b_spatial, w_branch_gla, w_branch_sgu, w_out, norm_post_mix, norm_pre_ffn, w_ffn_in, w_ffn_out, norm_post_ffn, loss_target, m_norm_pre_mix, m_w_in, m_w_gate_up, m_b_gate, m_gla_norm, m_sgu_ln_g, m_sgu_ln_b, m_w_spatial, m_b_spatial, m_w_branch_gla, m_w_branch_sgu, m_w_out, m_norm_post_mix, m_norm_pre_ffn, m_w_ffn_in, m_w_ffn_out, m_norm_post_ffn, v_norm_pre_mix, v_w_in, v_w_gate_up, v_b_gate, v_gla_norm, v_sgu_ln_g, v_sgu_ln_b, v_w_spatial, v_b_spatial, v_w_branch_gla, v_w_branch_sgu, v_w_out, v_norm_post_mix, v_norm_pre_ffn, v_w_ffn_in, v_w_ffn_out, v_norm_post_ffn):
    given = dict(locals())
    def local(a, n):
        return a[0].T if n in COLUMN_SHARDED else a[0]

    w = {n: local(given[n], n) for n in WEIGHTS}
    m = {n: local(given["m_" + n], n) for n in WEIGHTS}
    v = {n: local(given["v_" + n], n) for n in WEIGHTS}
    xs, target = x[0], loss_target[0]

    small_shard = _pack_rows([w[n] for n in SMALL_SHARDED], SMALL_GATHER_ROWS)
    first = _all_gather_hbm([w["w_in"].astype(BF16), small_shard], name="gather_w_in", waves=GATHER_WAVES)
    rest, first = lax.optimization_barrier(([w[n].astype(BF16) for n in MIXER + FFN], first))
    rest_blocks = _all_gather_async(rest, name="gather_rest", collective_id=1)
    W = {n: w[n] for n in SMALL if n not in SMALL_SHARDED}
    blocks = dict(zip(MIXER + FFN, rest_blocks))
    for n in ("w_branch_gla", "w_branch_sgu", "w_out", "w_ffn_out"):
        W[n] = blocks[n].reshape(-1, D_MODEL)
    W["w_ffn_in"] = blocks["w_ffn_in"]
    W["w_in_blocks"] = first[0]
    small_blocks = first[1]
    off = 0
    for n in SMALL_SHARDED:
        r, c = w[n].shape
        blk = small_blocks[:, off:off + r * c // LANES].reshape(N_DEV, r, c)
        W[n] = blk.transpose(1, 0, 2).reshape(r, N_DEV * c)
        off += _tile_rows(r * c)

    scatter_ids = iter((3, 4))

    def scatter(names, parts):
        got = _scatter_blocks_async(parts, name="scatter_" + "_".join(names), collective_id=next(scatter_ids))
        return dict(zip(names, got))

    def finish(received):
        return {n: _adamw_reduce_call(r, w[n], m[n], v[n], name="adamw_" + n) for n, r in received.items()}

    def pair(part):
        return _pair_exchange_async([part], name="pair_w_in", collective_id=5)[0]

    def chip_scatter(part):
        return _chip_scatter_async([part], name="scatter_w_in", collective_id=6, waves=LATE_SCATTER_WAVES)[0]

    loss_part, grad_x, grads, big_done = _local_step(xs, target, W, scatter, finish, pair, chip_scatter)

    late = jnp.concatenate([grads["norm_pre_mix"].reshape(8, LANES), jnp.broadcast_to(loss_part, (8, LANES))], axis=0)
    gathered = _small_gather_async([_small_early_rows(grads), late], name="gather_small", collective_id=7)
    gathered, big_done["w_in"] = lax.optimization_barrier((gathered, big_done["w_in"]))
    loss_row, small_done = _small_update_call(
        *gathered, *[{n: given[prefix + n] for n in SMALL} for prefix in ("", "m_", "v_")], name="small_update")

    def pick(i):
        return [small_done[n][i] if n in SMALL else (big_done[n][i].T if n in COLUMN_SHARDED else big_done[n][i])[None]
                for n in WEIGHTS]

    return (loss_row[0, 0], grad_x[None], *pick(0), *pick(1), *pick(2), *pick(3))
```

```python
import jax
import jax.numpy as jnp
from jax import lax
from jax.experimental import pallas as pl
from jax.experimental.pallas import tpu as pltpu
from jax.experimental.pallas import tpu_sc as plsc

F32 = jnp.float32
BF16 = jnp.bfloat16

D_MODEL = 1024
N_DEV = 8
CHUNK = 64
HEADS = 4
DK = 128
DV = 256
QK = HEADS * DK
GV = HEADS * DV
RANK = 16
GROUPS = 4
SBLOCK = 128
DG = 256
D_FF = 2816
FF_BLK = 704
EPS = 1e-6
Q_SCALE = DK ** -0.5
LANES = 128
VMEM_BIG = 48 * 1024 * 1024

D_IN = 7184
IN_BLK = 898
A_COLS = (0, 3072)
LOW_COLS = (3072, 3088)
S_COLS = (3088, 5136)
G_COLS = (5136, 7184)

ADAM_LR = 0.001
ADAM_B1 = 0.9
ADAM_B2 = 0.999
ADAM_EPS = 1e-08
ADAM_WD = 0.01
ADAM_STEP = 10

MESH = pl.DeviceIdType.MESH
ANY = pl.BlockSpec(memory_space=pl.ANY)
VMEM_SPEC = pl.BlockSpec(memory_space=pltpu.VMEM)


def _cp(sem=None, vmem=None):
    return pltpu.CompilerParams(dimension_semantics=sem, vmem_limit_bytes=vmem)


def _hbm(*arrays):
    return [pltpu.with_memory_space_constraint(a, pltpu.HBM) for a in arrays]


def _sig(x):
    return 0.5 * jnp.tanh(0.5 * x) + 0.5


GELU_C = 0.7978845608028654
GELU_A = 0.044715


def _gelu(x):
    t = jnp.tanh((GELU_C * x) * (1.0 + GELU_A * (x * x)))
    return (0.5 * x) * (1.0 + t)


def _gelu_and_grad(x):
    x2 = x * x
    t = jnp.tanh((GELU_C * x) * (1.0 + GELU_A * x2))
    one_t = 1.0 + t
    hx = 0.5 * x
    grad = 0.5 * one_t + (hx * (1.0 - t * t)) * (GELU_C + (3.0 * GELU_A * GELU_C) * x2)
    return hx * one_t, grad


def _logsig(x):
    return jnp.minimum(x, 0.0) - jnp.log1p(jnp.exp(-jnp.abs(x)))


def _dot(a, b, dims):
    return lax.dot_general(a, b, (dims, ((), ())), preferred_element_type=F32)


NN = ((1,), (0,))
NT = ((1,), (1,))
TN = ((0,), (0,))


def _exact_mask_dot(mask_bf16, x):
    hi = x.astype(BF16)
    r1 = x - hi.astype(F32)
    mid = r1.astype(BF16)
    lo = (r1 - mid.astype(F32)).astype(BF16)
    return _dot(mask_bf16, hi, NN) + _dot(mask_bf16, mid, NN) + _dot(mask_bf16, lo, NN)


ROW_TILE = 512
SUB_ROWS = 256


def _sub_tiles(T):
    return [pl.ds(r0, min(SUB_ROWS, T)) for r0 in range(0, T, SUB_ROWS)]


def _rt(T, W, c=0):
    return pl.BlockSpec((T, W), lambda i: (i, c))


def _rt3(nb, T, W):
    return pl.BlockSpec((nb, T, W), lambda i: (0, i, 0))


def _res(shape):
    nd = len(shape)
    return pl.BlockSpec(tuple(shape), lambda i: (0,) * nd, pipeline_mode=pl.Buffered(1))


def _acc(shape):
    nd = len(shape)
    return pl.BlockSpec(tuple(shape), lambda i: (0,) * nd, pipeline_mode=pl.Buffered(1))


def _nbytes(shape, dtype):
    n = jnp.dtype(dtype).itemsize
    for s in shape:
        n *= s
    return n


def _vmem_limit(tiles, resident, temps=16 * 1024 * 1024):
    need = 2 * sum(_nbytes(s, d) for s, d in tiles) + sum(_nbytes(s, d) for s, d in resident) + temps
    return min(need, 60 * 1024 * 1024)


def _tok_call(body, *, name, S, T, ins, outs, semantics="parallel"):
    tiles = [(spec.block_shape, a.dtype) for a, spec, kind in ins + outs if kind == "tile"]
    resident = [(a.shape, a.dtype) for a, spec, kind in ins + outs if kind == "res"]
    return pl.pallas_call(
        body, name=name, grid=(S // T,),
        in_specs=[spec for _, spec, _ in ins], out_specs=[spec for _, spec, _ in outs],
        out_shape=[pltpu.HBM(a.shape, a.dtype) for a, _, _ in outs],
        compiler_params=_cp((semantics,), _vmem_limit(tiles, resident)),
    )(*_hbm(*[a for a, _, _ in ins]))


def _tile(a, spec):
    return (a, spec, "tile")


def _whole(a):
    return (a, _res(a.shape), "res")


def _out_tile(shape, dtype, spec):
    return (jax.ShapeDtypeStruct(shape, dtype), spec, "tile")


def _out_acc(shape, dtype=F32):
    return (jax.ShapeDtypeStruct(shape, dtype), _acc(shape), "res")


def _rms_stats(x):
    r = lax.rsqrt(jnp.mean(x * x, axis=-1, keepdims=True) + EPS)
    return r, x * r


def _rms_bwd(xh, r, g, dy):
    dxh = dy * g
    dx = r * (dxh - xh * jnp.mean(dxh * xh, axis=-1, keepdims=True))
    dg = jnp.sum(dy * xh, axis=0, keepdims=True)
    return dx, dg


def _accum(ref, val):
    @pl.when(pl.program_id(0) == 0)
    def _():
        ref[...] = val

    @pl.when(pl.program_id(0) > 0)
    def _():
        ref[...] += val


def _dot_rows_t(a, w_ref, row0, o_ref, chunk=1024):
    n = o_ref.shape[1]
    for n0 in range(0, n, chunk):
        n1 = min(n, n0 + chunk)
        o_ref[:, n0:n1] = _dot(a, w_ref[row0 + n0:row0 + n1, :], NT).astype(o_ref.dtype)


def _in_proj_call(x, g1, w_blocks, *, name, T=ROW_TILE):
    S = x.shape[0]

    def body(x_ref, g_ref, blk_ref, a_ref, pa_ref, al_ref, ps_ref, pg_ref, w_ref):
        @pl.when(pl.program_id(0) == 0)
        def _():
            for d in range(N_DEV):
                w_ref[d * IN_BLK:(d + 1) * IN_BLK, :] = blk_ref[d]

        _, xh = _rms_stats(x_ref[...])
        a = (xh * g_ref[...]).astype(BF16)
        a_ref[...] = a
        _dot_rows_t(a, w_ref, A_COLS[0], pa_ref)
        _dot_rows_t(a, w_ref, LOW_COLS[0], al_ref)
        _dot_rows_t(a, w_ref, S_COLS[0], ps_ref)
        _dot_rows_t(a, w_ref, G_COLS[0], pg_ref)

    widths = (D_MODEL, A_COLS[1] - A_COLS[0], LANES, S_COLS[1] - S_COLS[0], G_COLS[1] - G_COLS[0])
    return _tok_call(
        body, name=name, S=S, T=T, semantics="arbitrary",
        ins=[_tile(x, _rt(T, D_MODEL)), _whole(g1), _whole(w_blocks)],
        outs=[_out_tile((S, w), BF16, _rt(T, w)) for w in widths] + [_out_acc((D_IN, D_MODEL), BF16)])


def _mixer_tail_call(y_gla, y_sgu, pg, x, w_bg, w_bs, w_out, g2, g3, *, name, T=ROW_TILE):
    S = x.shape[0]

    def body(yg_ref, ys_ref, pg_ref, x_ref, wbg_ref, wbs_ref, wo_ref, g2_ref, g3_ref,
             t1_ref, t2_ref, mg_ref, mix_ref, x1_ref, h_ref):
        for rows in _sub_tiles(T):
            t1 = _dot(yg_ref[rows, :], wbg_ref[...], NN)
            t2 = _dot(ys_ref[rows, :], wbs_ref[...], NN)
            t1_ref[rows, :] = t1.astype(BF16)
            t2_ref[rows, :] = t2.astype(BF16)
            sg = _sig(pg_ref[rows, pl.ds(0, D_MODEL)].astype(F32))
            ss = _sig(pg_ref[rows, pl.ds(D_MODEL, D_MODEL)].astype(F32))
            merged = (sg * t1 + ss * t2).astype(BF16)
            mg_ref[rows, :] = merged
            mix = _dot(merged, wo_ref[...], NN)
            mix_ref[rows, :] = mix
            _, mh = _rms_stats(mix)
            x1 = x_ref[rows, :] + mh * g2_ref[...]
            x1_ref[rows, :] = x1
            _, xh = _rms_stats(x1)
            h_ref[rows, :] = (xh * g3_ref[...]).astype(BF16)

    row = _rt(T, D_MODEL)
    b16 = lambda: _out_tile((S, D_MODEL), BF16, row)
    f32 = lambda: _out_tile((S, D_MODEL), F32, row)
    return _tok_call(
        body, name=name, S=S, T=T,
        ins=[_tile(y_gla, row), _tile(y_sgu, row), _tile(pg, _rt(T, 2 * D_MODEL)), _tile(x, row),
             _whole(w_bg), _whole(w_bs), _whole(w_out), _whole(g2), _whole(g3)],
        outs=[b16(), b16(), b16(), f32(), f32(), b16()])


FF_CHUNKS = ((0, 1024), (1024, 2048), (2048, D_FF))


def _ffn_in_call(h, wfi, *, name, T=ROW_TILE):
    S = h.shape[0]

    def body(h_ref, w_ref, gu_ref, z_ref):
        hv = h_ref[...]
        for c0, c1 in FF_CHUNKS:
            gate = _dot(hv, w_ref[c0:c1, :], NT)
            up = _dot(hv, w_ref[D_FF + c0:D_FF + c1, :], NT)
            gu_ref[:, c0:c1] = gate.astype(BF16)
            gu_ref[:, D_FF + c0:D_FF + c1] = up.astype(BF16)
            z_ref[:, c0:c1] = (gate * _sig(gate) * up).astype(BF16)

    return _tok_call(
        body, name=name, S=S, T=T,
        ins=[_tile(h, _rt(T, D_MODEL)), _whole(wfi)],
        outs=[_out_tile((S, 2 * D_FF), BF16, _rt(T, 2 * D_FF)),
              _out_tile((S, D_FF), BF16, _rt(T, D_FF))])


def _ffn_out_loss_call(z, wfo, x1, target, g4, *, name, T=ROW_TILE):
    S = x1.shape[0]

    def body(z_ref, w_ref, x1_ref, t_ref, g4_ref, loss_ref, dx2_ref, dy_ref, dg4_ref):
        loss = jnp.zeros((1, 1), F32)
        dg4 = jnp.zeros((1, D_MODEL), F32)
        for rows in _sub_tiles(T):
            y = _dot(z_ref[rows, :], w_ref[...], NN)
            r, yh = _rms_stats(y)
            diff = x1_ref[rows, :] + yh * g4_ref[...] - t_ref[rows, :]
            loss = loss + 0.5 * jnp.sum(jnp.mean(diff * diff, axis=-1, keepdims=True), axis=0, keepdims=True)
            dx2 = diff * (1.0 / D_MODEL)
            dx2_ref[rows, :] = dx2
            dy, dg = _rms_bwd(yh, r, g4_ref[...], dx2)
            dy_ref[rows, :] = dy.astype(BF16)
            dg4 = dg4 + dg
        _accum(loss_ref, jnp.broadcast_to(loss, (1, LANES)))
        _accum(dg4_ref, dg4)

    row = _rt(T, D_MODEL)
    return _tok_call(
        body, name=name, S=S, T=T, semantics="arbitrary",
        ins=[_tile(z, _rt(T, D_FF)), _whole(wfo), _tile(x1, row), _tile(target, row), _whole(g4)],
        outs=[_out_acc((1, LANES)), _out_tile((S, D_MODEL), F32, row), _out_tile((S, D_MODEL), BF16, row),
              _out_acc((1, D_MODEL))])


def _ffn_out_bwd_call(dy, wfo, gu, *, name, T=ROW_TILE):
    S = dy.shape[0]

    def body(dy_ref, w_ref, gu_ref, dgu_ref):
        dyv = dy_ref[...]
        for c0, c1 in FF_CHUNKS:
            dz = _dot(dyv, w_ref[c0:c1, :], NT).astype(BF16)
            gt = gu_ref[:, c0:c1]
            up = gu_ref[:, D_FF + c0:D_FF + c1]
            s = _sig(gt)
            dgu_ref[:, c0:c1] = dz * up * s * (1.0 + gt * (1.0 - s))
            dgu_ref[:, D_FF + c0:D_FF + c1] = dz * gt * s

    wide = _rt(T, 2 * D_FF)
    return _tok_call(
        body, name=name, S=S, T=T,
        ins=[_tile(dy, _rt(T, D_MODEL)), _whole(wfo), _tile(gu, wide)],
        outs=[_out_tile((S, 2 * D_FF), BF16, wide)])[0]


def _ffn_in_bwd_call(dgu, wfi, dx2, x1, mix, g3, g2, *, name, T=ROW_TILE):
    S = x1.shape[0]

    def body(dgu_ref, w_ref, dx2_ref, x1_ref, mix_ref, g3_ref, g2_ref, dx1_ref, dmix_ref, dg3_ref, dg2_ref):
        dg3 = jnp.zeros((1, D_MODEL), F32)
        dg2 = jnp.zeros((1, D_MODEL), F32)
        for rows in _sub_tiles(T):
            dh = _dot(dgu_ref[rows, :], w_ref[...], NN)
            r3, xh = _rms_stats(x1_ref[rows, :])
            d3, g = _rms_bwd(xh, r3, g3_ref[...], dh)
            dg3 = dg3 + g
            dx1 = dx2_ref[rows, :] + d3
            dx1_ref[rows, :] = dx1
            r2, mh = _rms_stats(mix_ref[rows, :])
            dmix, g = _rms_bwd(mh, r2, g2_ref[...], dx1)
            dg2 = dg2 + g
            dmix_ref[rows, :] = dmix.astype(BF16)
        _accum(dg3_ref, dg3)
        _accum(dg2_ref, dg2)

    row = _rt(T, D_MODEL)
    return _tok_call(
        body, name=name, S=S, T=T, semantics="arbitrary",
        ins=[_tile(dgu, _rt(T, 2 * D_FF)), _whole(wfi), _tile(dx2, row), _tile(x1, row), _tile(mix, row),
             _whole(g3), _whole(g2)],
        outs=[_out_tile((S, D_MODEL), F32, row), _out_tile((S, D_MODEL), BF16, row),
              _out_acc((1, D_MODEL)), _out_acc((1, D_MODEL))])


def _mixer_bwd_call(dmix, t1, t2, pg, w_out, w_bg, w_bs, *, name, T=ROW_TILE):
    S = dmix.shape[0]

    def body(dmix_ref, t1_ref, t2_ref, pg_ref, wo_ref, wbg_ref, wbs_ref, dt1_ref, dt2_ref, dpg_ref, dyg_ref, dys_ref):
        gg, gs = pl.ds(0, D_MODEL), pl.ds(D_MODEL, D_MODEL)
        for rows in _sub_tiles(T):
            dm = _dot(dmix_ref[rows, :], wo_ref[...], NT)
            sg = _sig(pg_ref[rows, gg].astype(F32))
            ss = _sig(pg_ref[rows, gs].astype(F32))
            dt1 = (dm * sg).astype(BF16)
            dt2 = (dm * ss).astype(BF16)
            dt1_ref[rows, :] = dt1
            dt2_ref[rows, :] = dt2
            dpg_ref[rows, gg] = (dm * t1_ref[rows, :].astype(F32) * sg * (1.0 - sg)).astype(BF16)
            dpg_ref[rows, gs] = (dm * t2_ref[rows, :].astype(F32) * ss * (1.0 - ss)).astype(BF16)
            dyg_ref[rows, :] = _dot(dt1, wbg_ref[...], NT).astype(BF16)
            dys_ref[rows, :] = _dot(dt2, wbs_ref[...], NT).astype(BF16)

    row = _rt(T, D_MODEL)
    wide = _rt(T, 2 * D_MODEL)
    b16 = lambda: _out_tile((S, D_MODEL), BF16, row)
    return _tok_call(
        body, name=name, S=S, T=T,
        ins=[_tile(dmix, row), _tile(t1, row), _tile(t2, row), _tile(pg, wide), _whole(w_out), _whole(w_bg), _whole(w_bs)],
        outs=[b16(), b16(), _out_tile((S, 2 * D_MODEL), BF16, wide), b16(), b16()])


def _in_proj_bwd_call(dpa, dlow, dps, dpg, w_in_t, x, dx1, g1, *, name, T=ROW_TILE):
    S = x.shape[0]

    def body(dpa_ref, dl_ref, dps_ref, dpg_ref, w_ref, x_ref, dx1_ref, g1_ref, gx_ref, dg1_ref):
        da = (_dot(dpa_ref[...], w_ref[A_COLS[0]:A_COLS[1], :], NN)
              + _dot(dl_ref[...], w_ref[LOW_COLS[0]:LOW_COLS[0] + LANES, :], NN)
              + _dot(dps_ref[...], w_ref[S_COLS[0]:S_COLS[1], :], NN)
              + _dot(dpg_ref[...], w_ref[G_COLS[0]:G_COLS[1], :], NN))
        r, xh = _rms_stats(x_ref[...])
        dxa, dg = _rms_bwd(xh, r, g1_ref[...], da)
        gx_ref[...] = dx1_ref[...] + dxa
        _accum(dg1_ref, dg)

    row = _rt(T, D_MODEL)
    return _tok_call(
        body, name=name, S=S, T=T, semantics="arbitrary",
        ins=[_tile(dpa, _rt(T, dpa.shape[1])), _tile(dlow, _rt(T, dlow.shape[1])), _tile(dps, _rt(T, dps.shape[1])),
             _tile(dpg, _rt(T, dpg.shape[1])), _whole(w_in_t), _tile(x, row), _tile(dx1, row), _whole(g1)],
        outs=[_out_tile((S, D_MODEL), F32, row), _out_acc((1, D_MODEL))])


TOKEN_TILE = 512


def _weight_grads_part(arrays, pairs, *, tk=TOKEN_TILE, out_dtype=BF16):
    S = arrays[0].shape[-2]
    tk = min(tk, S)
    n_in = len(arrays)

    def out_shape(i, j):
        a, b = arrays[i], arrays[j]
        if a.ndim == 3:
            return (a.shape[0], a.shape[2], b.shape[1])
        if b.ndim == 3:
            return (b.shape[0], a.shape[1], b.shape[2])
        return (a.shape[1], b.shape[1])

    shapes = [out_shape(i, j) for i, j in pairs]

    in_place = out_dtype == F32

    def body(ins, outs, accs):
        k = pl.program_id(0)
        if in_place:
            accs = outs

        @pl.when(k == 0)
        def _():
            for acc in accs:
                acc[...] = jnp.zeros_like(acc)

        for (i, j), acc in zip(pairs, accs):
            a_ref, b_ref = ins[i], ins[j]
            if len(a_ref.shape) == 3:
                for n in range(a_ref.shape[0]):
                    acc[n] += _dot(a_ref[n], b_ref[...], TN)
            elif len(b_ref.shape) == 3:
                a = a_ref[...]
                for n in range(b_ref.shape[0]):
                    acc[n] += _dot(a, b_ref[n], TN)
            else:
                b = b_ref[...]
                for m0 in range(0, a_ref.shape[1], 1024):
                    m1 = min(a_ref.shape[1], m0 + 1024)
                    acc[m0:m1, :] += _dot(a_ref[:, m0:m1], b, TN)

        if not in_place:
            @pl.when(k == S // tk - 1)
            def _():
                for out, acc in zip(outs, accs):
                    out[...] = acc[...].astype(out.dtype)

    def in_spec(a):
        if a.ndim == 3:
            return pl.BlockSpec((a.shape[0], tk, a.shape[2]), lambda k: (0, k, 0))
        return pl.BlockSpec((tk, a.shape[1]), lambda k: (k, 0))

    return dict(
        body=body, steps=S // tk, arrays=list(arrays),
        in_specs=[in_spec(a) for a in arrays],
        out_specs=[_acc(s) for s in shapes],
        out_shapes=[pltpu.HBM(s, out_dtype) for s in shapes],
        scratch=[] if in_place else [pltpu.VMEM(s, F32) for s in shapes],
        tiles=[(in_spec(a).block_shape, a.dtype) for a in arrays],
        resident=[(s, F32) for s in shapes] + ([] if in_place else [(s, BF16) for s in shapes]))


def _weight_grads_call(arrays, pairs, *, name, tk=TOKEN_TILE):
    part = _weight_grads_part(arrays, pairs, tk=tk)
    n_in, n_out = len(part["arrays"]), len(part["out_shapes"])

    def body(*refs):
        part["body"](refs[:n_in], refs[n_in:n_in + n_out], refs[n_in + n_out:])

    return pl.pallas_call(
        body, name=name, grid=(part["steps"],),
        in_specs=part["in_specs"], out_specs=part["out_specs"], out_shape=part["out_shapes"],
        scratch_shapes=part["scratch"],
        compiler_params=_cp(("arbitrary",), _vmem_limit(part["tiles"], part["resident"])),
    )(*_hbm(*part["arrays"]))


def _ffn_in_grad_call(h, dgu, *, name, tk=TOKEN_TILE):
    S = h.shape[0]
    tk = min(tk, S)
    shape = (D_FF, D_MODEL)

    def body(h_ref, dgu_ref, out_ref, acc):
        k = pl.program_id(1)

        @pl.when(k == 0)
        def _():
            acc[...] = jnp.zeros_like(acc)

        hv = h_ref[...]
        for c0, c1 in FF_CHUNKS:
            acc[c0:c1, :] += _dot(dgu_ref[:, c0:c1], hv, TN)

        @pl.when(k == S // tk - 1)
        def _():
            out_ref[...] = acc[...].astype(out_ref.dtype)

    tiles = [((tk, D_MODEL), BF16), ((tk, D_FF), BF16), (shape, BF16)]
    return pl.pallas_call(
        body, name=name, grid=(2, S // tk),
        in_specs=[pl.BlockSpec((tk, D_MODEL), lambda g, k: (k, 0)),
                  pl.BlockSpec((tk, D_FF), lambda g, k: (k, g))],
        out_specs=pl.BlockSpec(shape, lambda g, k: (g, 0)),
        out_shape=pltpu.HBM((2 * D_FF, D_MODEL), BF16),
        scratch_shapes=[pltpu.VMEM(shape, F32)],
        compiler_params=_cp(("parallel", "arbitrary"), _vmem_limit(tiles, [(shape, F32)])),
    )(*_hbm(h, dgu))


GLA_TILE = 256
Q_OFF, K_OFF, V_OFF, R_OFF = 0, QK, 2 * QK, 2 * QK + GV


def _tri(lower):
    r = lax.broadcasted_iota(jnp.int32, (CHUNK, CHUNK), 0)
    c = lax.broadcasted_iota(jnp.int32, (CHUNK, CHUNK), 1)
    return jnp.where((r >= c) if lower else (c >= r), 1.0, 0.0).astype(BF16)


def _gla_fwd_call(pa, alow, wg, bgate, gnorm, *, name):
    S = pa.shape[0]
    Tg = min(GLA_TILE, S)
    cb = Tg // CHUNK

    def body(pa_ref, al_ref, wg_ref, bg_ref, gn_ref, y_ref, st_ref, state):
        @pl.when(pl.program_id(0) == 0)
        def _():
            state[...] = jnp.zeros_like(state)

        logit = _dot(al_ref[...], wg_ref[...], NN) + bg_ref[...]
        ls = _logsig(logit) * (1.0 / 16.0)
        tri = _tri(True)
        for c in range(cb):
            rows = pl.ds(c * CHUNK, CHUNK)
            cum = _exact_mask_dot(tri, ls[c * CHUNK:(c + 1) * CHUNK])
            tot = cum[CHUNK - 1:CHUNK]
            kd = (pa_ref[rows, pl.ds(K_OFF, QK)].astype(F32) * jnp.exp(tot - cum)).astype(BF16)
            decay = jnp.exp(tot)
            for h in range(HEADS):
                lanes = slice(h * DK, (h + 1) * DK)
                new = state[h] * decay[:, lanes] + _dot(pa_ref[rows, pl.ds(V_OFF + h * DV, DV)], kd[:, lanes], TN)
                state[h] = new
                st_ref[c, h] = new
        for c in range(cb):
            rows = pl.ds(c * CHUNK, CHUNK)
            for h in range(HEADS):
                qs = (pa_ref[rows, pl.ds(Q_OFF + h * DK, DK)].astype(F32) * Q_SCALE).astype(BF16)
                o = _dot(qs, st_ref[c, h].astype(BF16), NT)
                rs = lax.rsqrt(jnp.mean(o * o, axis=-1, keepdims=True) + EPS)
                rr = pa_ref[rows, pl.ds(R_OFF + h * DV, DV)].astype(F32)
                y_ref[rows, pl.ds(h * DV, DV)] = (o * rs * gn_ref[h] * (rr * _sig(rr))).astype(BF16)

    return pl.pallas_call(
        body, name=name, grid=(S // Tg,),
        in_specs=[
            pl.BlockSpec((Tg, 2 * QK + 2 * GV), lambda t: (t, 0)),
            pl.BlockSpec((Tg, LANES), lambda t: (t, 0)),
            pl.BlockSpec((LANES, QK), lambda t: (0, 0)),
            pl.BlockSpec((1, QK), lambda t: (0, 0)),
            pl.BlockSpec((HEADS, 1, DV), lambda t: (0, 0, 0)),
        ],
        out_specs=[
            pl.BlockSpec((Tg, GV), lambda t: (t, 0)),
            pl.BlockSpec((cb, HEADS, DV, DK), lambda t: (t, 0, 0, 0)),
        ],
        out_shape=[pltpu.HBM((S, GV), BF16), pltpu.HBM((S // CHUNK, HEADS, DV, DK), F32)],
        scratch_shapes=[pltpu.VMEM((HEADS, DV, DK), F32)],
        compiler_params=_cp(("arbitrary",), VMEM_BIG),
    )(*_hbm(pa, alow, wg, bgate, gnorm))


def _gla_bwd_call(pa, alow, wg, bgate, gnorm, states, dy, *, name):
    S = pa.shape[0]
    Tg = min(GLA_TILE, S)
    cb = Tg // CHUNK
    nt = S // Tg

    def rev(t):
        return nt - 1 - t

    def body(pa_ref, al_ref, wg_ref, bg_ref, gn_ref, st_ref, prev_ref, dy_ref,
             dpa_ref, dlow_ref, dwg_ref, dgn_ref, dbg_ref, carry, pbuf, dlbuf):
        t = pl.program_id(0)

        @pl.when(t == 0)
        def _():
            carry[...] = jnp.zeros_like(carry)
            dwg_ref[...] = jnp.zeros_like(dwg_ref)
            dgn_ref[...] = jnp.zeros_like(dgn_ref)
            dbg_ref[...] = jnp.zeros_like(dbg_ref)

        logit = _dot(al_ref[...], wg_ref[...], NN) + bg_ref[...]
        ls = _logsig(logit) * (1.0 / 16.0)
        sneg = 1.0 / (1.0 + jnp.exp(logit))
        tri = _tri(True)
        upper = _tri(False)
        first_tile = rev(t) == 0
        heads = range(HEADS)

        w, decay, kd = [], [], []
        for c in range(cb):
            rows = pl.ds(c * CHUNK, CHUNK)
            cum = _exact_mask_dot(tri, ls[c * CHUNK:(c + 1) * CHUNK])
            tot = cum[CHUNK - 1:CHUNK]
            w.append(jnp.exp(tot - cum))
            decay.append(jnp.exp(tot))
            kd.append(pa_ref[rows, pl.ds(K_OFF, QK)].astype(F32) * w[c])

        dgn = [jnp.zeros((1, DV), F32) for _ in heads]
        for c in range(cb):
            rows = pl.ds(c * CHUNK, CHUNK)
            for h in heads:
                gn = gn_ref[h]
                qs = (pa_ref[rows, pl.ds(Q_OFF + h * DK, DK)].astype(F32) * Q_SCALE).astype(BF16)
                st16 = st_ref[c, h].astype(BF16)
                o = _dot(qs, st16, NT)
                rs = lax.rsqrt(jnp.mean(o * o, axis=-1, keepdims=True) + EPS)
                oh = o * rs
                rr = pa_ref[rows, pl.ds(R_OFF + h * DV, DV)].astype(F32)
                sr = _sig(rr)
                dyv = dy_ref[rows, pl.ds(h * DV, DV)].astype(F32)
                dpa_ref[rows, pl.ds(R_OFF + h * DV, DV)] = (
                    dyv * oh * gn * sr * (1.0 + rr * (1.0 - sr))).astype(BF16)
                don = dyv * (rr * sr)
                dgn[h] = dgn[h] + jnp.sum(don * oh, axis=0, keepdims=True)
                doh = don * gn
                do16 = (rs * (doh - oh * jnp.mean(doh * oh, axis=-1, keepdims=True))).astype(BF16)
                dpa_ref[rows, pl.ds(Q_OFF + h * DK, DK)] = (_dot(do16, st16, NN) * Q_SCALE).astype(BF16)
                pbuf[c, h] = _dot(do16, qs, TN)
        for h in heads:
            dgn_ref[h] += dgn[h]

        dkd = [[None] * HEADS for _ in range(cb)]
        ddecay = [[None] * HEADS for _ in range(cb)]
        for c in reversed(range(cb)):
            rows = pl.ds(c * CHUNK, CHUNK)
            for h in heads:
                lanes = slice(h * DK, (h + 1) * DK)
                gt = pbuf[c, h] + carry[h]
                gt16 = gt.astype(BF16)
                dkd[c][h] = _dot(pa_ref[rows, pl.ds(V_OFF + h * DV, DV)], gt16, NN)
                dpa_ref[rows, pl.ds(V_OFF + h * DV, DV)] = _dot(kd[c][:, lanes].astype(BF16), gt16, NT).astype(BF16)
                if c > 0:
                    st_prev = st_ref[c - 1, h]
                else:
                    st_prev = jnp.where(first_tile, 0.0, prev_ref[0, h])
                ddecay[c][h] = jnp.sum(gt * st_prev, axis=0, keepdims=True)
                carry[h] = gt * decay[c][:, lanes]

        dbg = jnp.zeros((1, QK), F32)
        for c in range(cb):
            rows = pl.ds(c * CHUNK, CHUNK)
            dkd_c = jnp.concatenate(dkd[c], axis=1)
            dpa_ref[rows, pl.ds(K_OFF, QK)] = (dkd_c * w[c]).astype(BF16)
            e = dkd_c * kd[c]
            dtot = jnp.sum(e, axis=0, keepdims=True) + jnp.concatenate(ddecay[c], axis=1) * decay[c]
            dls = dtot - _exact_mask_dot(upper, e)
            dlogit = dls * (1.0 / 16.0) * sneg[c * CHUNK:(c + 1) * CHUNK]
            dlbuf[rows, :] = dlogit.astype(BF16)
            dbg = dbg + jnp.sum(dlogit, axis=0, keepdims=True)
        dbg_ref[...] += dbg

        dl16 = dlbuf[...]
        dlow_ref[...] = _dot(dl16, wg_ref[...], NT).astype(BF16)
        dwg_ref[...] += _dot(al_ref[...], dl16, TN)

    wide = 2 * QK + 2 * GV
    tiles = [((Tg, wide), BF16), ((cb + 1, HEADS, DV, DK), F32), ((Tg, GV), BF16), ((Tg, wide), BF16),
             ((Tg, QK), BF16)]
    resident = [((cb + 1, HEADS, DV, DK), F32)]
    return pl.pallas_call(
        body, name=name, grid=(nt,),
        in_specs=[
            pl.BlockSpec((Tg, wide), lambda t: (rev(t), 0)),
            pl.BlockSpec((Tg, LANES), lambda t: (rev(t), 0)),
            pl.BlockSpec((LANES, QK), lambda t: (0, 0)),
            pl.BlockSpec((1, QK), lambda t: (0, 0)),
            pl.BlockSpec((HEADS, 1, DV), lambda t: (0, 0, 0)),
            pl.BlockSpec((cb, HEADS, DV, DK), lambda t: (rev(t), 0, 0, 0)),
            pl.BlockSpec((1, HEADS, DV, DK), lambda t: (jnp.maximum(rev(t) * cb - 1, 0), 0, 0, 0)),
            pl.BlockSpec((Tg, GV), lambda t: (rev(t), 0)),
        ],
        out_specs=[
            pl.BlockSpec((Tg, wide), lambda t: (rev(t), 0)),
            pl.BlockSpec((Tg, LANES), lambda t: (rev(t), 0)),
            pl.BlockSpec((LANES, QK), lambda t: (0, 0)),
            pl.BlockSpec((HEADS, 1, DV), lambda t: (0, 0, 0)),
            pl.BlockSpec((1, QK), lambda t: (0, 0)),
        ],
        out_shape=[pltpu.HBM((S, wide), BF16), pltpu.HBM((S, LANES), BF16), jax.ShapeDtypeStruct((LANES, QK), F32),
                   jax.ShapeDtypeStruct((HEADS, 1, DV), F32), jax.ShapeDtypeStruct((1, QK), F32)],
        scratch_shapes=[pltpu.VMEM((HEADS, DV, DK), F32), pltpu.VMEM((cb, HEADS, DV, DK), F32),
                        pltpu.VMEM((Tg, QK), BF16)],
        compiler_params=_cp(("arbitrary",), _vmem_limit(tiles, resident)),
    )(*_hbm(pa, alow, wg, bgate, gnorm, states, states, dy))


SGU_TILE = 256


def _sgu_mask():
    r = lax.broadcasted_iota(jnp.int32, (SBLOCK, SBLOCK), 0)
    c = lax.broadcasted_iota(jnp.int32, (SBLOCK, SBLOCK), 1)
    return (c < CHUNK) | (r >= CHUNK)


def _ln_stats(vf):
    mu = jnp.mean(vf, axis=-1, keepdims=True)
    xc = vf - mu
    rs = lax.rsqrt(jnp.mean(xc * xc, axis=-1, keepdims=True) + EPS)
    return rs, xc * rs


def _sgu_fwd_call(ps, ln_g, ln_b, w_sp, b_sp, *, name):
    S = ps.shape[0]
    Ts = min(SGU_TILE, S)

    def body(ps_ref, lg_ref, lb_ref, w_ref, b_ref, y_ref):
        mask = _sgu_mask()
        for g in range(GROUPS):
            wm = jnp.where(mask, w_ref[g], 0.0).astype(BF16)
            for p in range(Ts // SBLOCK):
                rows = pl.ds(p * SBLOCK, SBLOCK)
                u = _gelu(ps_ref[rows, pl.ds(g * DG, DG)].astype(F32))
                _, xh = _ln_stats(_gelu(ps_ref[rows, pl.ds(D_MODEL + g * DG, DG)].astype(F32)))
                vn = xh * lg_ref[g] + lb_ref[g]
                mixed = _dot(wm, vn.astype(BF16), NN) + b_ref[g]
                y_ref[rows, pl.ds(g * DG, DG)] = (u * mixed).astype(BF16)

    full3 = lambda a, b, c: pl.BlockSpec((a, b, c), lambda t: (0, 0, 0))
    return pl.pallas_call(
        body, name=name, grid=(S // Ts,),
        in_specs=[pl.BlockSpec((Ts, 2 * D_MODEL), lambda t: (t, 0)),
                  full3(GROUPS, 1, DG), full3(GROUPS, 1, DG), full3(GROUPS, SBLOCK, SBLOCK), full3(GROUPS, SBLOCK, 1)],
        out_specs=pl.BlockSpec((Ts, D_MODEL), lambda t: (t, 0)),
        out_shape=pltpu.HBM((S, D_MODEL), BF16),
        compiler_params=_cp(("parallel",)),
    )(*_hbm(ps, ln_g, ln_b, w_sp, b_sp))


def _sgu_bwd_call(ps, ln_g, ln_b, w_sp, b_sp, dy, *, name):
    S = ps.shape[0]
    Ts = min(SGU_TILE, S)

    def body(ps_ref, lg_ref, lb_ref, w_ref, b_ref, dy_ref, ds_ref, dlg_ref, dlb_ref, dw_ref, db_ref):
        @pl.when(pl.program_id(0) == 0)
        def _():
            dlg_ref[...] = jnp.zeros_like(dlg_ref)
            dlb_ref[...] = jnp.zeros_like(dlb_ref)
            dw_ref[...] = jnp.zeros_like(dw_ref)
            db_ref[...] = jnp.zeros_like(db_ref)

        mask = _sgu_mask()
        for g in range(GROUPS):
            wm = jnp.where(mask, w_ref[g], 0.0).astype(BF16)
            lg = lg_ref[g]
            for p in range(Ts // SBLOCK):
                rows = pl.ds(p * SBLOCK, SBLOCK)
                su = ps_ref[rows, pl.ds(g * DG, DG)].astype(F32)
                sv = ps_ref[rows, pl.ds(D_MODEL + g * DG, DG)].astype(F32)
                u, du = _gelu_and_grad(su)
                gv, dgv = _gelu_and_grad(sv)
                rs, xh = _ln_stats(gv)
                vn16 = (xh * lg + lb_ref[g]).astype(BF16)
                mixed = _dot(wm, vn16, NN) + b_ref[g]
                dyv = dy_ref[rows, pl.ds(g * DG, DG)].astype(F32)
                ds_ref[rows, pl.ds(g * DG, DG)] = (dyv * mixed * du).astype(BF16)
                dmix = dyv * u
                dmix16 = dmix.astype(BF16)
                db_ref[g] += jnp.sum(dmix, axis=-1, keepdims=True)
                dw_ref[g] += jnp.where(mask, _dot(dmix16, vn16, NT), 0.0)
                dvn = _dot(wm, dmix16, TN)
                dlg_ref[g] += jnp.sum(dvn * xh, axis=0, keepdims=True)
                dlb_ref[g] += jnp.sum(dvn, axis=0, keepdims=True)
                dxh = dvn * lg
                dvf = rs * (dxh - jnp.mean(dxh, axis=-1, keepdims=True)
                            - xh * jnp.mean(dxh * xh, axis=-1, keepdims=True))
                ds_ref[rows, pl.ds(D_MODEL + g * DG, DG)] = (dvf * dgv).astype(BF16)

    full3 = lambda a, b, c: pl.BlockSpec((a, b, c), lambda t: (0, 0, 0))
    tiles = [((Ts, 2 * D_MODEL), BF16), ((Ts, D_MODEL), BF16), ((Ts, 2 * D_MODEL), BF16)]
    return pl.pallas_call(
        body, name=name, grid=(S // Ts,),
        in_specs=[pl.BlockSpec((Ts, 2 * D_MODEL), lambda t: (t, 0)),
                  full3(GROUPS, 1, DG), full3(GROUPS, 1, DG), full3(GROUPS, SBLOCK, SBLOCK), full3(GROUPS, SBLOCK, 1),
                  pl.BlockSpec((Ts, D_MODEL), lambda t: (t, 0))],
        out_specs=[pl.BlockSpec((Ts, 2 * D_MODEL), lambda t: (t, 0)),
                   full3(GROUPS, 1, DG), full3(GROUPS, 1, DG), full3(GROUPS, SBLOCK, SBLOCK), full3(GROUPS, SBLOCK, 1)],
        out_shape=[pltpu.HBM((S, 2 * D_MODEL), BF16),
                   jax.ShapeDtypeStruct((GROUPS, 1, DG), F32), jax.ShapeDtypeStruct((GROUPS, 1, DG), F32),
                   jax.ShapeDtypeStruct((GROUPS, SBLOCK, SBLOCK), F32),
                   jax.ShapeDtypeStruct((GROUPS, SBLOCK, 1), F32)],
        compiler_params=_cp(("arbitrary",), _vmem_limit(tiles, [])),
    )(*_hbm(ps, ln_g, ln_b, w_sp, b_sp, dy))


def _position():
    return lax.axis_index("x"), lax.axis_index("y"), lax.axis_index("c")


def _gather_copies(srcs, dsts, send_sems, recv_sems, local_sems):
    x, y, c = _position()
    me, sibling = (x, y, c), (x, y, 1 - c)
    chips = [(1 - x, y), (x, 1 - y), (1 - x, 1 - y)]
    n = len(srcs)

    def slab(a, block):
        px, py, pc = block
        return dsts[a].at[4 * px + 2 * py + pc]

    def copy(a, k, block, to, src=None):
        return pltpu.make_async_remote_copy(
            src_ref=slab(a, block) if src is None else src, dst_ref=slab(a, block),
            send_sem=send_sems.at[7 * a + k], recv_sem=recv_sems.at[7 * a + k], device_id=to, device_id_type=MESH)

    mine = [pltpu.make_async_copy(srcs[a], slab(a, me), local_sems.at[a]) for a in range(n)]
    for cp in mine:
        cp.start()
    first = []
    for a in range(n):
        first.append(copy(a, 0, me, sibling, src=srcs[a]))
        first += [copy(a, 1 + j, me, (*chip, c), src=srcs[a]) for j, chip in enumerate(chips)]
    for cp in first:
        cp.start()
    passed = []
    for j, chip in enumerate(chips):
        for a in range(n):
            copy(a, 1 + j, (*chip, c), me).wait_recv()
            cp = copy(a, 4 + j, (*chip, c), sibling)
            cp.start()
            passed.append(cp)
    for a in range(n):
        copy(a, 0, sibling, me).wait_recv()
        for j, chip in enumerate(chips):
            copy(a, 4 + j, (*chip, 1 - c), me).wait_recv()
    for cp in first + passed:
        cp.wait_send()
    for cp in mine:
        cp.wait()


def _gather_copies_relayed(srcs, dsts, send_sems, recv_sems, local_sems, waves=1):
    x, y, c = _position()
    me, sibling = (x, y, c), (x, y, 1 - c)
    x_chip, y_chip, d_chip = (1 - x, y), (x, 1 - y), (1 - x, 1 - y)
    south = c == 0
    relay_from = (jnp.where(south, x, 1 - x), jnp.where(south, 1 - y, y), c)
    relay_to = (jnp.where(south, 1 - x, x), jnp.where(south, y, 1 - y), c)
    n = len(srcs)

    def cut(a, w):
        rows = srcs[a].shape[0]
        step = (rows // waves) // 16 * 16
        if waves == 1 or step == 0:
            return pl.ds(0, rows) if w == 0 else None
        return pl.ds(w * step, step if w < waves - 1 else rows - w * step)

    def slab(a, block, w):
        px, py, pc = block
        return dsts[a].at[4 * px + 2 * py + pc, cut(a, w)]

    def copy(a, k, w, block, to, own=False):
        sem = (7 * a + k) * waves + w
        return pltpu.make_async_remote_copy(
            src_ref=srcs[a].at[cut(a, w)] if own else slab(a, block, w), dst_ref=slab(a, block, w),
            send_sem=send_sems.at[sem], recv_sem=recv_sems.at[sem], device_id=to, device_id_type=MESH)

    pieces = [(a, w) for w in range(waves) for a in range(n) if cut(a, w) is not None]
    mine = [pltpu.make_async_copy(srcs[a], dsts[a].at[4 * x + 2 * y + c], local_sems.at[a]) for a in range(n)]
    for cp in mine:
        cp.start()
    sent = []
    for a, w in pieces:
        sent += [copy(a, 0, w, me, sibling, own=True), copy(a, 1, w, me, (*x_chip, c), own=True),
                 copy(a, 2, w, me, (*y_chip, c), own=True)]
    for cp in sent:
        cp.start()
    for a, w in pieces:
        copy(a, 1, w, (*x_chip, c), me).wait_recv()
        copy(a, 2, w, (*y_chip, c), me).wait_recv()
        later = [copy(a, 3, w, relay_from, relay_to), copy(a, 4, w, (*x_chip, c), sibling),
                 copy(a, 5, w, (*y_chip, c), sibling)]
        for cp in later:
            cp.start()
        sent += later
    for a, w in pieces:
        copy(a, 3, w, (*d_chip, c), me).wait_recv()
        cp = copy(a, 6, w, (*d_chip, c), sibling)
        cp.start()
        sent.append(cp)
    for a, w in pieces:
        copy(a, 0, w, sibling, me).wait_recv()
        for k, chip in ((4, x_chip), (5, y_chip), (6, d_chip)):
            copy(a, k, w, (*chip, 1 - c), me).wait_recv()
    for cp in sent:
        cp.wait_send()
    for cp in mine:
        cp.wait()


def _all_gather_hbm(shards, *, name, waves=1):
    n = len(shards)

    def body(*refs):
        srcs, dsts = refs[:n], refs[n:2 * n]
        send_sems, recv_sems, local_sems = refs[2 * n:]
        _gather_copies_relayed(srcs, dsts, send_sems, recv_sems, local_sems, waves=waves)

    return pl.pallas_call(
        body, name=name,
        in_specs=[ANY] * n, out_specs=[ANY] * n,
        out_shape=[jax.ShapeDtypeStruct((N_DEV, *s.shape), s.dtype) for s in shards],
        scratch_shapes=_comm_sems(n, waves),
    )(*shards)


FLIPS = [(fx, fy, fc) for fx in (0, 1) for fy in (0, 1) for fc in (0, 1)][1:]


def _scatter_copies(srcs, dsts, send_sems, recv_sems, local_sems, waves=1):
    n = len(srcs)
    x, y, c = _position()
    me = 4 * x + 2 * y + c
    mine = [pltpu.make_async_copy(srcs[a].at[me], dsts[a].at[me], local_sems.at[a]) for a in range(n)]
    for cp in mine:
        cp.start()
    peers = []
    for fx, fy, fc in FLIPS:
        tx = 1 - x if fx else x
        ty = 1 - y if fy else y
        tc = 1 - c if fc else c
        peers.append(((tx, ty, tc), 4 * tx + 2 * ty + tc))
    copies = []
    for w in range(waves):
        wave = []
        for k, (peer_id, peer) in enumerate(peers):
            for a in range(n):
                rows = srcs[a].shape[1]
                step = rows if waves == 1 else (rows // waves) // 16 * 16
                r0 = w * step
                cut = pl.ds(r0, step if w < waves - 1 else rows - r0)
                sem = (7 * a + k) * waves + w
                cp = pltpu.make_async_remote_copy(
                    src_ref=srcs[a].at[peer, cut], dst_ref=dsts[a].at[me, cut],
                    send_sem=send_sems.at[sem], recv_sem=recv_sems.at[sem],
                    device_id=peer_id, device_id_type=MESH)
                cp.start()
                wave.append(cp)
        for cp in wave:
            cp.wait_send()
        copies += wave
    for cp in copies:
        cp.wait_recv()
    for cp in mine:
        cp.wait()


def _pair_copies(srcs, dsts, send_sems, recv_sems, local_sems):
    x, y, c = _position()
    copies = [pltpu.make_async_remote_copy(
        src_ref=srcs[a], dst_ref=dsts[a], send_sem=send_sems.at[a], recv_sem=recv_sems.at[a],
        device_id=(x, y, 1 - c), device_id_type=MESH) for a in range(len(srcs))]
    for cp in copies:
        cp.start()
    for cp in copies:
        cp.wait()


def _chip_scatter_copies(srcs, dsts, send_sems, recv_sems, local_sems, waves=1):
    n = len(srcs)
    x, y, c = _position()
    here = 2 * x + y
    mine = [pltpu.make_async_copy(srcs[a].at[here], dsts[a].at[here], local_sems.at[a]) for a in range(n)]
    for cp in mine:
        cp.start()
    peers = []
    for fx, fy in ((1, 0), (0, 1), (1, 1)):
        tx = 1 - x if fx else x
        ty = 1 - y if fy else y
        peers.append(((tx, ty, c), 2 * tx + ty))
    copies = []
    for w in range(waves):
        wave = []
        for k, (peer_id, peer) in enumerate(peers):
            for a in range(n):
                rows = srcs[a].shape[1]
                step = rows if waves == 1 else (rows // waves) // 16 * 16
                r0 = w * step
                cut = pl.ds(r0, step if w < waves - 1 else rows - r0)
                sem = (3 * a + k) * waves + w
                cp = pltpu.make_async_remote_copy(
                    src_ref=srcs[a].at[peer, cut], dst_ref=dsts[a].at[here, cut],
                    send_sem=send_sems.at[sem], recv_sem=recv_sems.at[sem],
                    device_id=peer_id, device_id_type=MESH)
                cp.start()
                wave.append(cp)
        for cp in wave:
            cp.wait_send()
        copies += wave
    for cp in copies:
        cp.wait_recv()
    for cp in mine:
        cp.wait()


def _comm_sems(n, waves=1):
    return [pltpu.SemaphoreType.DMA((7 * n * waves,)), pltpu.SemaphoreType.DMA((7 * n * waves,)),
            pltpu.SemaphoreType.DMA((n,))]


def _handshake(peers):
    barrier = pltpu.get_barrier_semaphore()
    for peer in peers:
        pl.semaphore_signal(barrier, inc=1, device_id=peer, device_id_type=MESH)
    pl.semaphore_wait(barrier, len(peers))


def _sequencer_call(arrays, out_types, copies_fn, peers_fn, *, name, collective_id, waves=1):
    n = len(arrays)
    srcs = [jax.new_ref(a, memory_space=pltpu.MemorySpace.HBM) for a in arrays]
    dsts = [jax.empty_ref(t, memory_space=pltpu.MemorySpace.HBM) for t in out_types]
    extra = {} if waves == 1 else {"waves": waves}

    @pl.kernel(mesh=plsc.ScalarSubcoreMesh(axis_name="sequencer", num_cores=1), name=name,
               scratch_types=_comm_sems(n, waves), compiler_params=pltpu.CompilerParams(collective_id=collective_id))
    def launch(send_sems, recv_sems, local_sems):
        _handshake(peers_fn())
        copies_fn(srcs, dsts, send_sems, recv_sems, local_sems, **extra)

    launch()
    return [d[...] for d in dsts]


def _all_other_devices():
    x, y, c = _position()
    return [(1 - x if fx else x, 1 - y if fy else y, 1 - c if fc else c) for fx, fy, fc in FLIPS]


def _gather_relay_peers():
    x, y, c = _position()
    return [(x, y, 1 - c), (1 - x, y, c), (x, 1 - y, c)]


def _gather_peers():
    x, y, c = _position()
    return [(x, y, 1 - c), (1 - x, y, c), (x, 1 - y, c), (1 - x, 1 - y, c)]


def _small_gather_async(shards, *, name, collective_id):
    return _sequencer_call(shards, [jax.ShapeDtypeStruct((N_DEV, *s.shape), s.dtype) for s in shards],
                           _gather_copies, _gather_peers, name=name, collective_id=collective_id)


def _sibling():
    x, y, c = _position()
    return [(x, y, 1 - c)]


def _same_core_of_other_chips():
    x, y, c = _position()
    return [(1 - x, y, c), (x, 1 - y, c), (1 - x, 1 - y, c)]


def _pair_exchange_async(parts, *, name, collective_id):
    return _sequencer_call(parts, [jax.ShapeDtypeStruct(p.shape, p.dtype) for p in parts],
                           _pair_copies, _sibling, name=name, collective_id=collective_id)


def _chip_scatter_async(parts, *, name, collective_id, waves=1):
    return _sequencer_call(parts, [jax.ShapeDtypeStruct(p.shape, p.dtype) for p in parts],
                           _chip_scatter_copies, _same_core_of_other_chips, name=name, collective_id=collective_id,
                           waves=waves)


def _pair_sum_call(mine, theirs, *, name, tw=D_MODEL):
    n, r, w = mine.shape

    def body(a_ref, b_ref, o_ref):
        o_ref[...] = (a_ref[...].astype(F32) + b_ref[...].astype(F32)).astype(o_ref.dtype)

    spec = pl.BlockSpec((None, r, tw), lambda i, j: (i, 0, j))
    return pl.pallas_call(
        body, name=name, grid=(n, w // tw), in_specs=[spec, spec], out_specs=spec,
        out_shape=pltpu.HBM(mine.shape, mine.dtype),
        compiler_params=_cp(("parallel", "parallel")),
    )(*_hbm(mine, theirs))


def _scatter_blocks_async(parts, *, name, collective_id, waves=1):
    return _sequencer_call(parts, [jax.ShapeDtypeStruct(p.shape, p.dtype) for p in parts],
                           _scatter_copies, _all_other_devices, name=name, collective_id=collective_id, waves=waves)


def _all_gather_async(shards, *, name, collective_id):
    return _sequencer_call(shards, [jax.ShapeDtypeStruct((N_DEV, *s.shape), s.dtype) for s in shards],
                           _gather_copies_relayed, _gather_relay_peers, name=name, collective_id=collective_id)


def _adamw_math(w, g, m, v):
    m = ADAM_B1 * m + (1.0 - ADAM_B1) * g
    v = ADAM_B2 * v + (1.0 - ADAM_B2) * (g * g)
    m_hat = m / (1.0 - ADAM_B1 ** ADAM_STEP)
    v_hat = v / (1.0 - ADAM_B2 ** ADAM_STEP)
    delta = -ADAM_LR * (m_hat / (jnp.sqrt(v_hat) + ADAM_EPS) + ADAM_WD * w)
    return delta, m, v


def _adamw_reduce_call(recv, w, m, v, *, name, T=128):
    R, W = w.shape
    n_parts = recv.shape[0]
    if R % T == 0:
        tr, tw = T, W
    elif (R // 2) % 16 == 0:
        tr, tw = R // 2, W
    else:
        tr, tw = R, 2 * LANES

    def body(p_ref, w_ref, m_ref, v_ref, g_out, d_out, m_out, v_out):
        g = p_ref[0].astype(F32)
        for d in range(1, n_parts):
            g = g + p_ref[d].astype(F32)
        g_out[...] = g
        d_out[...], m_out[...], v_out[...] = _adamw_math(w_ref[...], g, m_ref[...], v_ref[...])

    row = pl.BlockSpec((tr, tw), lambda i, j: (i, j))
    out = pltpu.HBM((R, W), F32)
    return pl.pallas_call(
        body, name=name, grid=(R // tr, W // tw),
        in_specs=[pl.BlockSpec((n_parts, tr, tw), lambda i, j: (0, i, j)), row, row, row],
        out_specs=[row] * 4, out_shape=[out] * 4,
        compiler_params=_cp(("parallel", "parallel"), VMEM_BIG),
    )(*_hbm(recv, w, m, v))


SMALL_EARLY = (("b_gate", 8), ("w_spatial", 512), ("b_spatial", 8), ("norm_post_mix", 8), ("norm_pre_ffn", 8),
               ("norm_post_ffn", 8), ("w_gate_up", 128), ("gla_norm", 64), ("sgu_ln_g", 64), ("sgu_ln_b", 64))
SMALL_EARLY_AT = {}
for _name, _rows in SMALL_EARLY:
    SMALL_EARLY_AT[_name] = sum(r for _, r in SMALL_EARLY[:len(SMALL_EARLY_AT)])
SMALL_EARLY_ROWS = sum(r for _, r in SMALL_EARLY)
SMALL_LATE_ROWS = 16


def _small_early_rows(grads):
    def rows(a, n_rows):
        a = a.reshape(-1, LANES)
        return jnp.pad(a, ((0, n_rows - a.shape[0]), (0, 0)))

    def device_major(a, n_rows):
        r, c = a.shape[0], a.shape[1] // N_DEV
        a = a.reshape(r, N_DEV, c).transpose(1, 0, 2)
        a = jnp.pad(a, ((0, 0), (0, n_rows // N_DEV - r), (0, LANES - c)))
        return a.reshape(n_rows, LANES)

    pieces = []
    for name, n_rows in SMALL_EARLY:
        g = grads[name]
        if name in SMALL_SHARDED:
            pieces.append(device_major(g.reshape(g.shape[0], -1) if g.ndim == 2 else g.reshape(g.shape[0], g.shape[-1]), n_rows))
        else:
            pieces.append(rows(g, n_rows))
    return jnp.concatenate(pieces, axis=0)


def _small_update_call(got_early, got_late, w, m, v, *, name):
    names = list(SMALL)
    n_p = len(names)
    E = SMALL_EARLY_ROWS

    def body(early_ref, late_ref, *rest):
        w_refs, m_refs, v_refs = [dict(zip(names, rest[i * n_p:(i + 1) * n_p])) for i in range(3)]
        outs, tot = rest[3 * n_p:-1], rest[-1]
        loss_out = outs[0]
        g_out, d_out, m_out, v_out = [dict(zip(names, outs[1 + i * n_p:1 + (i + 1) * n_p])) for i in range(4)]
        acc, acc_late = early_ref[0], late_ref[0]
        for d in range(1, N_DEV):
            acc, acc_late = acc + early_ref[d], acc_late + late_ref[d]
        tot[0:E, :] = acc
        tot[E:E + SMALL_LATE_ROWS, :] = acc_late
        loss_out[...] = tot[E + 8:E + 9, :]

        x, y, c = _position()
        me = 4 * x + 2 * y + c

        def update(name, g, ix):
            g_out[name][ix] = g
            d_out[name][ix], m_out[name][ix], v_out[name][ix] = _adamw_math(
                w_refs[name][ix], g, m_refs[name][ix], v_refs[name][ix])

        for name in names:
            shape = w[name].shape
            if name in SMALL_SHARDED:
                per_dev = dict(SMALL_EARLY)[name] // N_DEV
                at = pl.multiple_of(SMALL_EARLY_AT[name] + me * per_dev, 8)
                g = tot[pl.ds(at, per_dev), :]
                update(name, g[:shape[1], :shape[2]], (0,))
            elif name == "w_spatial":
                for grp in range(GROUPS):
                    at = SMALL_EARLY_AT[name] + grp * SBLOCK
                    update(name, tot[at:at + SBLOCK, :], (0, grp))
            elif name == "b_spatial":
                at = SMALL_EARLY_AT[name]
                update(name, tot[at:at + GROUPS, :], (0,))
            else:
                at = E if name == "norm_pre_mix" else SMALL_EARLY_AT[name]
                for k in range(shape[1] // LANES):
                    update(name, tot[at + k:at + k + 1, :], (slice(None), pl.ds(k * LANES, LANES)))

    state = [s[n] for s in (w, m, v) for n in names]
    out_shapes = [jax.ShapeDtypeStruct((1, LANES), F32)] + [jax.ShapeDtypeStruct(w[n].shape, F32) for n in names] * 4
    outs = pl.pallas_call(
        body, name=name,
        in_specs=[VMEM_SPEC] * (2 + len(state)), out_specs=[VMEM_SPEC] * len(out_shapes), out_shape=out_shapes,
        scratch_shapes=[pltpu.VMEM((E + SMALL_LATE_ROWS, LANES), F32)],
    )(got_early, got_late, *state)
    per_name = {n: tuple(outs[1 + i * n_p + j] for i in range(4)) for j, n in enumerate(names)}
    return outs[0], per_name


def _tile_rows(n_elems):
    return -(-n_elems // (8 * LANES)) * 8


def _pack_rows(parts, rows):
    pieces = []
    for p in parts:
        q = p.reshape(-1, LANES)
        pieces.append(jnp.pad(q, ((0, _tile_rows(p.size) - q.shape[0]), (0, 0))))
    buf = jnp.concatenate(pieces, axis=0)
    return jnp.pad(buf, ((0, rows - buf.shape[0]), (0, 0)))


def _in_w_blocks_call(a, live, finished, *, name, tk=TOKEN_TILE):
    S = a.shape[0]
    tk = min(tk, S)
    steps = S // tk
    n_live = len(live)
    n_chips = N_DEV // 2
    out_shape = (n_chips, IN_BLK, D_MODEL)

    def body(a_ref, *rest):
        live_refs, done_refs = rest[:n_live], rest[n_live:n_live + len(finished)]
        keep_ref, send_ref = rest[n_live + len(finished):n_live + len(finished) + 2]
        accs = rest[n_live + len(finished) + 2:]
        k = pl.program_id(0)

        @pl.when(k == 0)
        def _():
            for acc in accs:
                acc[...] = jnp.zeros_like(acc)

        av = a_ref[...]
        for src, acc in zip(live_refs, accs):
            for m0 in range(0, src.shape[1], 1024):
                m1 = min(src.shape[1], m0 + 1024)
                acc[m0:m1, :] += _dot(src[:, m0:m1], av, TN)

        @pl.when(k == steps - 1)
        def _():
            groups = [(acc, cols) for acc, (_, cols) in zip(accs, live)]
            groups += [(ref, cols) for ref, (_, cols) in zip(done_refs, finished)]
            core = lax.axis_index("c")

            def cut(d, out_ref):
                lo, hi = IN_BLK * d, IN_BLK * (d + 1)
                for ref, (c0, c1) in groups:
                    s0, e0 = max(lo, c0), min(hi, c1)
                    if s0 < e0:
                        out_ref[d // 2, s0 - lo:e0 - lo, :] = ref[s0 - c0:e0 - c0, :].astype(BF16)

            for d in range(N_DEV):
                pl.when(core == d % 2)(lambda d=d: cut(d, keep_ref))
                pl.when(core != d % 2)(lambda d=d: cut(d, send_ref))

    acc_shapes = [(arr.shape[1], D_MODEL) for arr, _ in live]
    tiles = [((tk, D_MODEL), BF16)] + [((tk, arr.shape[1]), BF16) for arr, _ in live]
    resident = ([(arr.shape, arr.dtype) for arr, _ in finished] + [(out_shape, BF16)] * 2
                + [(s, F32) for s in acc_shapes])
    return pl.pallas_call(
        body, name=name, grid=(steps,),
        in_specs=[pl.BlockSpec((tk, D_MODEL), lambda k: (k, 0))]
        + [pl.BlockSpec((tk, arr.shape[1]), lambda k: (k, 0)) for arr, _ in live]
        + [_res(arr.shape) for arr, _ in finished],
        out_specs=[_acc(out_shape)] * 2,
        out_shape=[pltpu.HBM(out_shape, BF16)] * 2,
        scratch_shapes=[pltpu.VMEM(s, F32) for s in acc_shapes],
        compiler_params=_cp(("arbitrary",), _vmem_limit(tiles, resident)),
    )(*_hbm(a, *[arr for arr, _ in live], *[arr for arr, _ in finished]))


def _local_step(x, target, W, scatter, finish, pair, chip_scatter):
    g1, g2, g3, g4 = [W[n].reshape(1, D_MODEL) for n in ("norm_pre_mix", "norm_post_mix", "norm_pre_ffn", "norm_post_ffn")]
    wfi, wfo = W["w_ffn_in"].reshape(2 * D_FF, D_MODEL), W["w_ffn_out"]
    wg = jnp.pad(W["w_gate_up"], ((0, LANES - RANK), (0, 0))).astype(BF16)
    bgate = W["b_gate"].reshape(1, QK)
    gnorm = W["gla_norm"].reshape(HEADS, 1, DV)
    ln_g = W["sgu_ln_g"].reshape(GROUPS, 1, DG)
    ln_b = W["sgu_ln_b"].reshape(GROUPS, 1, DG)
    w_sp = W["w_spatial"]
    b_sp = W["b_spatial"].reshape(GROUPS, SBLOCK, 1)

    a, pa, alow, ps, pg, w_in_t = _in_proj_call(x, g1, W["w_in_blocks"], name="in_proj")
    y_gla, states = _gla_fwd_call(pa, alow, wg, bgate, gnorm, name="gla_fwd")
    y_sgu = _sgu_fwd_call(ps, ln_g, ln_b, w_sp, b_sp, name="sgu_fwd")
    t1, t2, merged, mix, x1, h = _mixer_tail_call(y_gla, y_sgu, pg, x, W["w_branch_gla"], W["w_branch_sgu"],
                                                  W["w_out"], g2, g3, name="mixer_tail")
    gu, z = _ffn_in_call(h, wfi, name="ffn_in")
    loss, dx2, dy, dg4 = _ffn_out_loss_call(z, wfo, x1, target, g4, name="ffn_out_loss")

    grads = {"norm_post_ffn": dg4}
    done = {}
    dgu = _ffn_out_bwd_call(dy, wfo, gu, name="ffn_out_bwd")
    dw_ffn_out = _weight_grads_call([z, dy], [(0, 1)], name="d_ffn_out_w")[0].reshape(N_DEV, D_FF // N_DEV, D_MODEL)
    dw_ffn_in = _ffn_in_grad_call(h, dgu, name="d_ffn_in_w").reshape(N_DEV, FF_BLK, D_MODEL)
    dgu, dw_ffn_out, dw_ffn_in = lax.optimization_barrier((dgu, dw_ffn_out, dw_ffn_in))
    ffn_received = scatter(("w_ffn_out", "w_ffn_in"), [dw_ffn_out, dw_ffn_in])
    dx1, dmix, grads["norm_pre_ffn"], grads["norm_post_mix"] = _ffn_in_bwd_call(dgu, wfi, dx2, x1, mix, g3, g2, name="ffn_in_bwd")
    dt1, dt2, dpg, dy_gla, dy_sgu = _mixer_bwd_call(dmix, t1, t2, pg, W["w_out"], W["w_branch_gla"], W["w_branch_sgu"],
                                                    name="mixer_bwd")
    rows = D_MODEL // N_DEV
    mixer_grads = _weight_grads_call([merged, dmix, y_gla, dt1, y_sgu, dt2], [(0, 1), (2, 3), (4, 5)], name="d_mixer_w")
    dy_gla, dy_sgu, mixer_grads = lax.optimization_barrier((dy_gla, dy_sgu, mixer_grads))
    mixer_received = scatter(("w_out", "w_branch_gla", "w_branch_sgu"),
                             [g.reshape(N_DEV, rows, D_MODEL) for g in mixer_grads])
    dps, dlg, dlb, dwsp, dbsp = _sgu_bwd_call(ps, ln_g, ln_b, w_sp, b_sp, dy_sgu, name="sgu_bwd")
    dw_s, dw_g = _weight_grads_call([a, dps, dpg], [(1, 0), (2, 0)], name="d_in_w_sgu_gates")
    dy_gla, dw_s, dw_g = lax.optimization_barrier((dy_gla, dw_s, dw_g))
    dpa, dlow, dwg, dgn, dbg = _gla_bwd_call(pa, alow, wg, bgate, gnorm, states, dy_gla, name="gla_bwd")
    ffn_received, mixer_received, dpa, dlow = lax.optimization_barrier((ffn_received, mixer_received, dpa, dlow))
    keep, send = _in_w_blocks_call(a, [(dpa, A_COLS), (dlow, LOW_COLS)], [(dw_s, S_COLS), (dw_g, G_COLS)],
                                   name="d_in_w_blocks")
    ffn_received, mixer_received, send = lax.optimization_barrier((ffn_received, mixer_received, send))
    from_sibling = pair(send)
    done.update({**finish(ffn_received), **finish(mixer_received)})
    done, keep = lax.optimization_barrier((done, keep))
    chip_sum = _pair_sum_call(keep, from_sibling, name="pair_sum_w_in")
    chip_sum, dpa = lax.optimization_barrier((chip_sum, dpa))
    in_received = chip_scatter(chip_sum)
    grad_x, grads["norm_pre_mix"] = _in_proj_bwd_call(dpa, dlow, dps, dpg, w_in_t, x, dx1, g1, name="in_proj_bwd")

    grads["w_gate_up"] = dwg[:RANK]
    grads["b_gate"] = dbg
    grads["gla_norm"] = dgn
    grads["sgu_ln_g"] = dlg
    grads["sgu_ln_b"] = dlb
    grads["w_spatial"] = dwsp
    grads["b_spatial"] = dbsp
    done.update(finish({"w_in": in_received}))
    return loss, grad_x, grads, done


WEIGHTS = ("norm_pre_mix", "w_in", "w_gate_up", "b_gate", "gla_norm", "sgu_ln_g", "sgu_ln_b", "w_spatial",
           "b_spatial", "w_branch_gla", "w_branch_sgu", "w_out", "norm_post_mix", "norm_pre_ffn", "w_ffn_in",
           "w_ffn_out", "norm_post_ffn")
BIG = ("w_in", "w_branch_gla", "w_branch_sgu", "w_out", "w_ffn_in", "w_ffn_out")
MIXER = ("w_branch_gla", "w_branch_sgu", "w_out")
FFN = ("w_ffn_in", "w_ffn_out")
COLUMN_SHARDED = ("w_in", "w_ffn_in")
LATE_SCATTER_WAVES = 4
GATHER_WAVES = 4
SMALL = tuple(n for n in WEIGHTS if n not in BIG)
SMALL_SHARDED = ("w_gate_up", "gla_norm", "sgu_ln_g", "sgu_ln_b")
SMALL_GATHER_ROWS = 32


def kernel(x, norm_pre_mix, w_in, w_gate_up, b_gate, gla_norm, sgu_ln_g, sgu_ln_b, w_spatial, b_spatial, w_branch_gla, w_branch_sgu, w_out, norm_post_mix, norm_pre_ffn, w_ffn_in, w_ffn_out, norm_post_ffn, loss_target, m_norm_pre_mix, m_w_in, m_w_gate_up, m_b_gate, m_gla_norm, m_sgu_ln_g, m_sgu_ln_b, m_w_spatial, m_b_spatial, m_w_branch_gla, m_w_branch_sgu, m_w_out, m_norm_post_mix, m_norm_pre_ffn, m_w_ffn_in, m_w_ffn_out, m_norm_post_ffn, v_norm_pre_mix, v_w_in, v_w_gate_up, v_b_gate, v_gla_norm, v_sgu_ln_g, v_sgu_ln_b, v_w_spatial, v_b_spatial, v_w_branch_gla, v_w_branch_sgu, v_w_out, v_norm_post_mix, v_norm_pre_ffn, v_w_ffn_in, v_w_ffn_out, v_norm_post_ffn):
    given = dict(locals())
    def local(a, n):
        return a[0].T if n in COLUMN_SHARDED else a[0]

    w = {n: local(given[n], n) for n in WEIGHTS}
    m = {n: local(given["m_" + n], n) for n in WEIGHTS}
    v = {n: local(given["v_" + n], n) for n in WEIGHTS}
    xs, target = x[0], loss_target[0]

    small_shard = _pack_rows([w[n] for n in SMALL_SHARDED], SMALL_GATHER_ROWS)
    first = _all_gather_hbm([w["w_in"].astype(BF16), small_shard], name="gather_w_in", waves=GATHER_WAVES)
    rest, first = lax.optimization_barrier(([w[n].astype(BF16) for n in MIXER + FFN], first))
    rest_blocks = _all_gather_async(rest, name="gather_rest", collective_id=1)
    W = {n: w[n] for n in SMALL if n not in SMALL_SHARDED}
    blocks = dict(zip(MIXER + FFN, rest_blocks))
    for n in ("w_branch_gla", "w_branch_sgu", "w_out", "w_ffn_out"):
        W[n] = blocks[n].reshape(-1, D_MODEL)
    W["w_ffn_in"] = blocks["w_ffn_in"]
    W["w_in_blocks"] = first[0]
    small_blocks = first[1]
    off = 0
    for n in SMALL_SHARDED:
        r, c = w[n].shape
        blk = small_blocks[:, off:off + r * c // LANES].reshape(N_DEV, r, c)
        W[n] = blk.transpose(1, 0, 2).reshape(r, N_DEV * c)
        off += _tile_rows(r * c)

    scatter_ids = iter((3, 4))

    def scatter(names, parts):
        got = _scatter_blocks_async(parts, name="scatter_" + "_".join(names), collective_id=next(scatter_ids))
        return dict(zip(names, got))

    def finish(received):
        return {n: _adamw_reduce_call(r, w[n], m[n], v[n], name="adamw_" + n) for n, r in received.items()}

    def pair(part):
        return _pair_exchange_async([part], name="pair_w_in", collective_id=5)[0]

    def chip_scatter(part):
        return _chip_scatter_async([part], name="scatter_w_in", collective_id=6, waves=LATE_SCATTER_WAVES)[0]

    loss_part, grad_x, grads, big_done = _local_step(xs, target, W, scatter, finish, pair, chip_scatter)

    late = jnp.concatenate([grads["norm_pre_mix"].reshape(8, LANES), jnp.broadcast_to(loss_part, (8, LANES))], axis=0)
    gathered = _small_gather_async([_small_early_rows(grads), late], name="gather_small", collective_id=7)
    gathered, big_done["w_in"] = lax.optimization_barrier((gathered, big_done["w_in"]))
    loss_row, small_done = _small_update_call(
        *gathered, *[{n: given[prefix + n] for n in SMALL} for prefix in ("", "m_", "v_")], name="small_update")

    def pick(i):
        return [small_done[n][i] if n in SMALL else (big_done[n][i].T if n in COLUMN_SHARDED else big_done[n][i])[None]
                for n in WEIGHTS]

    return (loss_row[0, 0], grad_x[None], *pick(0), *pick(1), *pick(2), *pick(3))
```

```python
import jax
import jax.numpy as jnp
from jax import lax
from jax.experimental import pallas as pl
from jax.experimental.pallas import tpu as pltpu
from jax.experimental.pallas import tpu_sc as plsc

F32 = jnp.float32
BF16 = jnp.bfloat16

D_MODEL = 1024
N_DEV = 8
CHUNK = 64
HEADS = 4
DK = 128
DV = 256
QK = HEADS * DK
GV = HEADS * DV
RANK = 16
GROUPS = 4
SBLOCK = 128
DG = 256
D_FF = 2816
FF_BLK = 704
EPS = 1e-6
Q_SCALE = DK ** -0.5
LANES = 128
VMEM_BIG = 48 * 1024 * 1024

D_IN = 7184
IN_BLK = 898
A_COLS = (0, 3072)
LOW_COLS = (3072, 3088)
S_COLS = (3088, 5136)
G_COLS = (5136, 7184)

ADAM_LR = 0.001
ADAM_B1 = 0.9
ADAM_B2 = 0.999
ADAM_EPS = 1e-08
ADAM_WD = 0.01
ADAM_STEP = 10

MESH = pl.DeviceIdType.MESH
ANY = pl.BlockSpec(memory_space=pl.ANY)
VMEM_SPEC = pl.BlockSpec(memory_space=pltpu.VMEM)


def _cp(sem=None, vmem=None):
    return pltpu.CompilerParams(dimension_semantics=sem, vmem_limit_bytes=vmem)


def _hbm(*arrays):
    return [pltpu.with_memory_space_constraint(a, pltpu.HBM) for a in arrays]


def _sig(x):
    return 0.5 * jnp.tanh(0.5 * x) + 0.5


GELU_C = 0.7978845608028654
GELU_A = 0.044715


def _gelu(x):
    t = jnp.tanh((GELU_C * x) * (1.0 + GELU_A * (x * x)))
    return (0.5 * x) * (1.0 + t)


def _gelu_and_grad(x):
    x2 = x * x
    t = jnp.tanh((GELU_C * x) * (1.0 + GELU_A * x2))
    one_t = 1.0 + t
    hx = 0.5 * x
    grad = 0.5 * one_t + (hx * (1.0 - t * t)) * (GELU_C + (3.0 * GELU_A * GELU_C) * x2)
    return hx * one_t, grad


def _logsig(x):
    return jnp.minimum(x, 0.0) - jnp.log1p(jnp.exp(-jnp.abs(x)))


def _dot(a, b, dims):
    return lax.dot_general(a, b, (dims, ((), ())), preferred_element_type=F32)


NN = ((1,), (0,))
NT = ((1,), (1,))
TN = ((0,), (0,))


def _exact_mask_dot(mask_bf16, x):
    hi = x.astype(BF16)
    r1 = x - hi.astype(F32)
    mid = r1.astype(BF16)
    lo = (r1 - mid.astype(F32)).astype(BF16)
    return _dot(mask_bf16, hi, NN) + _dot(mask_bf16, mid, NN) + _dot(mask_bf16, lo, NN)


ROW_TILE = 512
SUB_ROWS = 256


def _sub_tiles(T):
    return [pl.ds(r0, min(SUB_ROWS, T)) for r0 in range(0, T, SUB_ROWS)]


def _rt(T, W, c=0):
    return pl.BlockSpec((T, W), lambda i: (i, c))


def _rt3(nb, T, W):
    return pl.BlockSpec((nb, T, W), lambda i: (0, i, 0))


def _res(shape):
    nd = len(shape)
    return pl.BlockSpec(tuple(shape), lambda i: (0,) * nd, pipeline_mode=pl.Buffered(1))


def _acc(shape):
    nd = len(shape)
    return pl.BlockSpec(tuple(shape), lambda i: (0,) * nd, pipeline_mode=pl.Buffered(1))


def _nbytes(shape, dtype):
    n = jnp.dtype(dtype).itemsize
    for s in shape:
        n *= s
    return n


def _vmem_limit(tiles, resident, temps=16 * 1024 * 1024):
    need = 2 * sum(_nbytes(s, d) for s, d in tiles) + sum(_nbytes(s, d) for s, d in resident) + temps
    return min(need, 60 * 1024 * 1024)


def _tok_call(body, *, name, S, T, ins, outs, semantics="parallel"):
    tiles = [(spec.block_shape, a.dtype) for a, spec, kind in ins + outs if kind == "tile"]
    resident = [(a.shape, a.dtype) for a, spec, kind in ins + outs if kind == "res"]
    return pl.pallas_call(
        body, name=name, grid=(S // T,),
        in_specs=[spec for _, spec, _ in ins], out_specs=[spec for _, spec, _ in outs],
        out_shape=[pltpu.HBM(a.shape, a.dtype) for a, _, _ in outs],
        compiler_params=_cp((semantics,), _vmem_limit(tiles, resident)),
    )(*_hbm(*[a for a, _, _ in ins]))


def _tile(a, spec):
    return (a, spec, "tile")


def _whole(a):
    return (a, _res(a.shape), "res")


def _out_tile(shape, dtype, spec):
    return (jax.ShapeDtypeStruct(shape, dtype), spec, "tile")


def _out_acc(shape, dtype=F32):
    return (jax.ShapeDtypeStruct(shape, dtype), _acc(shape), "res")


def _rms_stats(x):
    r = lax.rsqrt(jnp.mean(x * x, axis=-1, keepdims=True) + EPS)
    return r, x * r


def _rms_bwd(xh, r, g, dy):
    dxh = dy * g
    dx = r * (dxh - xh * jnp.mean(dxh * xh, axis=-1, keepdims=True))
    dg = jnp.sum(dy * xh, axis=0, keepdims=True)
    return dx, dg


def _accum(ref, val):
    @pl.when(pl.program_id(0) == 0)
    def _():
        ref[...] = val

    @pl.when(pl.program_id(0) > 0)
    def _():
        ref[...] += val


def _dot_rows_t(a, w_ref, row0, o_ref, chunk=1024):
    n = o_ref.shape[1]
    for n0 in range(0, n, chunk):
        n1 = min(n, n0 + chunk)
        o_ref[:, n0:n1] = _dot(a, w_ref[row0 + n0:row0 + n1, :], NT).astype(o_ref.dtype)


def _in_proj_call(x, g1, w_blocks, *, name, T=ROW_TILE):
    S = x.shape[0]

    def body(x_ref, g_ref, blk_ref, a_ref, pa_ref, al_ref, ps_ref, pg_ref, w_ref):
        @pl.when(pl.program_id(0) == 0)
        def _():
            for d in range(N_DEV):
                w_ref[d * IN_BLK:(d + 1) * IN_BLK, :] = blk_ref[d]

        _, xh = _rms_stats(x_ref[...])
        a = (xh * g_ref[...]).astype(BF16)
        a_ref[...] = a
        _dot_rows_t(a, w_ref, A_COLS[0], pa_ref)
        _dot_rows_t(a, w_ref, LOW_COLS[0], al_ref)
        _dot_rows_t(a, w_ref, S_COLS[0], ps_ref)
        _dot_rows_t(a, w_ref, G_COLS[0], pg_ref)

    widths = (D_MODEL, A_COLS[1] - A_COLS[0], LANES, S_COLS[1] - S_COLS[0], G_COLS[1] - G_COLS[0])
    return _tok_call(
        body, name=name, S=S, T=T, semantics="arbitrary",
        ins=[_tile(x, _rt(T, D_MODEL)), _whole(g1), _whole(w_blocks)],
        outs=[_out_tile((S, w), BF16, _rt(T, w)) for w in widths] + [_out_acc((D_IN, D_MODEL), BF16)])


def _mixer_tail_call(y_gla, y_sgu, pg, x, w_bg, w_bs, w_out, g2, g3, *, name, T=ROW_TILE):
    S = x.shape[0]

    def body(yg_ref, ys_ref, pg_ref, x_ref, wbg_ref, wbs_ref, wo_ref, g2_ref, g3_ref,
             t1_ref, t2_ref, mg_ref, mix_ref, x1_ref, h_ref):
        for rows in _sub_tiles(T):
            t1 = _dot(yg_ref[rows, :], wbg_ref[...], NN)
            t2 = _dot(ys_ref[rows, :], wbs_ref[...], NN)
            t1_ref[rows, :] = t1.astype(BF16)
            t2_ref[rows, :] = t2.astype(BF16)
            sg = _sig(pg_ref[rows, pl.ds(0, D_MODEL)].astype(F32))
            ss = _sig(pg_ref[rows, pl.ds(D_MODEL, D_MODEL)].astype(F32))
            merged = (sg * t1 + ss * t2).astype(BF16)
            mg_ref[rows, :] = merged
            mix = _dot(merged, wo_ref[...], NN)
            mix_ref[rows, :] = mix
            _, mh = _rms_stats(mix)
            x1 = x_ref[rows, :] + mh * g2_ref[...]
            x1_ref[rows, :] = x1
            _, xh = _rms_stats(x1)
            h_ref[rows, :] = (xh * g3_ref[...]).astype(BF16)

    row = _rt(T, D_MODEL)
    b16 = lambda: _out_tile((S, D_MODEL), BF16, row)
    f32 = lambda: _out_tile((S, D_MODEL), F32, row)
    return _tok_call(
        body, name=name, S=S, T=T,
        ins=[_tile(y_gla, row), _tile(y_sgu, row), _tile(pg, _rt(T, 2 * D_MODEL)), _tile(x, row),
             _whole(w_bg), _whole(w_bs), _whole(w_out), _whole(g2), _whole(g3)],
        outs=[b16(), b16(), b16(), f32(), f32(), b16()])


FF_CHUNKS = ((0, 1024), (1024, 2048), (2048, D_FF))


def _ffn_in_call(h, wfi, *, name, T=ROW_TILE):
    S = h.shape[0]

    def body(h_ref, w_ref, gu_ref, z_ref):
        hv = h_ref[...]
        for c0, c1 in FF_CHUNKS:
            gate = _dot(hv, w_ref[c0:c1, :], NT)
            up = _dot(hv, w_ref[D_FF + c0:D_FF + c1, :], NT)
            gu_ref[:, c0:c1] = gate.astype(BF16)
            gu_ref[:, D_FF + c0:D_FF + c1] = up.astype(BF16)
            z_ref[:, c0:c1] = (gate * _sig(gate) * up).astype(BF16)

    return _tok_call(
        body, name=name, S=S, T=T,
        ins=[_tile(h, _rt(T, D_MODEL)), _whole(wfi)],
        outs=[_out_tile((S, 2 * D_FF), BF16, _rt(T, 2 * D_FF)),
              _out_tile((S, D_FF), BF16, _rt(T, D_FF))])


def _ffn_out_loss_call(z, wfo, x1, target, g4, *, name, T=ROW_TILE):
    S = x1.shape[0]

    def body(z_ref, w_ref, x1_ref, t_ref, g4_ref, loss_ref, dx2_ref, dy_ref, dg4_ref):
        loss = jnp.zeros((1, 1), F32)
        dg4 = jnp.zeros((1, D_MODEL), F32)
        for rows in _sub_tiles(T):
            y = _dot(z_ref[rows, :], w_ref[...], NN)
            r, yh = _rms_stats(y)
            diff = x1_ref[rows, :] + yh * g4_ref[...] - t_ref[rows, :]
            loss = loss + 0.5 * jnp.sum(jnp.mean(diff * diff, axis=-1, keepdims=True), axis=0, keepdims=True)
            dx2 = diff * (1.0 / D_MODEL)
            dx2_ref[rows, :] = dx2
            dy, dg = _rms_bwd(yh, r, g4_ref[...], dx2)
            dy_ref[rows, :] = dy.astype(BF16)
            dg4 = dg4 + dg
        _accum(loss_ref, jnp.broadcast_to(loss, (1, LANES)))
        _accum(dg4_ref, dg4)

    row = _rt(T, D_MODEL)
    return _tok_call(
        body, name=name, S=S, T=T, semantics="arbitrary",
        ins=[_tile(z, _rt(T, D_FF)), _whole(wfo), _tile(x1, row), _tile(target, row), _whole(g4)],
        outs=[_out_acc((1, LANES)), _out_tile((S, D_MODEL), F32, row), _out_tile((S, D_MODEL), BF16, row),
              _out_acc((1, D_MODEL))])


def _ffn_out_bwd_call(dy, wfo, gu, *, name, T=ROW_TILE):
    S = dy.shape[0]

    def body(dy_ref, w_ref, gu_ref, dgu_ref):
        dyv = dy_ref[...]
        for c0, c1 in FF_CHUNKS:
            dz = _dot(dyv, w_ref[c0:c1, :], NT).astype(BF16)
            gt = gu_ref[:, c0:c1]
            up = gu_ref[:, D_FF + c0:D_FF + c1]
            s = _sig(gt)
            dgu_ref[:, c0:c1] = dz * up * s * (1.0 + gt * (1.0 - s))
            dgu_ref[:, D_FF + c0:D_FF + c1] = dz * gt * s

    wide = _rt(T, 2 * D_FF)
    return _tok_call(
        body, name=name, S=S, T=T,
        ins=[_tile(dy, _rt(T, D_MODEL)), _whole(wfo), _tile(gu, wide)],
        outs=[_out_tile((S, 2 * D_FF), BF16, wide)])[0]


def _ffn_in_bwd_call(dgu, wfi, dx2, x1, mix, g3, g2, *, name, T=ROW_TILE):
    S = x1.shape[0]

    def body(dgu_ref, w_ref, dx2_ref, x1_ref, mix_ref, g3_ref, g2_ref, dx1_ref, dmix_ref, dg3_ref, dg2_ref):
        dg3 = jnp.zeros((1, D_MODEL), F32)
        dg2 = jnp.zeros((1, D_MODEL), F32)
        for rows in _sub_tiles(T):
            dh = _dot(dgu_ref[rows, :], w_ref[...], NN)
            r3, xh = _rms_stats(x1_ref[rows, :])
            d3, g = _rms_bwd(xh, r3, g3_ref[...], dh)
            dg3 = dg3 + g
            dx1 = dx2_ref[rows, :] + d3
            dx1_ref[rows, :] = dx1
            r2, mh = _rms_stats(mix_ref[rows, :])
            dmix, g = _rms_bwd(mh, r2, g2_ref[...], dx1)
            dg2 = dg2 + g
            dmix_ref[rows, :] = dmix.astype(BF16)
        _accum(dg3_ref, dg3)
        _accum(dg2_ref, dg2)

    row = _rt(T, D_MODEL)
    return _tok_call(
        body, name=name, S=S, T=T, semantics="arbitrary",
        ins=[_tile(dgu, _rt(T, 2 * D_FF)), _whole(wfi), _tile(dx2, row), _tile(x1, row), _tile(mix, row),
             _whole(g3), _whole(g2)],
        outs=[_out_tile((S, D_MODEL), F32, row), _out_tile((S, D_MODEL), BF16, row),
              _out_acc((1, D_MODEL)), _out_acc((1, D_MODEL))])


def _mixer_bwd_call(dmix, t1, t2, pg, w_out, w_bg, w_bs, *, name, T=ROW_TILE):
    S = dmix.shape[0]

    def body(dmix_ref, t1_ref, t2_ref, pg_ref, wo_ref, wbg_ref, wbs_ref, dt1_ref, dt2_ref, dpg_ref, dyg_ref, dys_ref):
        gg, gs = pl.ds(0, D_MODEL), pl.ds(D_MODEL, D_MODEL)
        for rows in _sub_tiles(T):
            dm = _dot(dmix_ref[rows, :], wo_ref[...], NT)
            sg = _sig(pg_ref[rows, gg].astype(F32))
            ss = _sig(pg_ref[rows, gs].astype(F32))
            dt1 = (dm * sg).astype(BF16)
            dt2 = (dm * ss).astype(BF16)
            dt1_ref[rows, :] = dt1
            dt2_ref[rows, :] = dt2
            dpg_ref[rows, gg] = (dm * t1_ref[rows, :].astype(F32) * sg * (1.0 - sg)).astype(BF16)
            dpg_ref[rows, gs] = (dm * t2_ref[rows, :].astype(F32) * ss * (1.0 - ss)).astype(BF16)
            dyg_ref[rows, :] = _dot(dt1, wbg_ref[...], NT).astype(BF16)
            dys_ref[rows, :] = _dot(dt2, wbs_ref[...], NT).astype(BF16)

    row = _rt(T, D_MODEL)
    wide = _rt(T, 2 * D_MODEL)
    b16 = lambda: _out_tile((S, D_MODEL), BF16, row)
    return _tok_call(
        body, name=name, S=S, T=T,
        ins=[_tile(dmix, row), _tile(t1, row), _tile(t2, row), _tile(pg, wide), _whole(w_out), _whole(w_bg), _whole(w_bs)],
        outs=[b16(), b16(), _out_tile((S, 2 * D_MODEL), BF16, wide), b16(), b16()])


def _in_proj_bwd_call(dpa, dlow, dps, dpg, w_in_t, x, dx1, g1, *, name, T=ROW_TILE):
    S = x.shape[0]

    def body(dpa_ref, dl_ref, dps_ref, dpg_ref, w_ref, x_ref, dx1_ref, g1_ref, gx_ref, dg1_ref):
        da = (_dot(dpa_ref[...], w_ref[A_COLS[0]:A_COLS[1], :], NN)
              + _dot(dl_ref[...], w_ref[LOW_COLS[0]:LOW_COLS[0] + LANES, :], NN)
              + _dot(dps_ref[...], w_ref[S_COLS[0]:S_COLS[1], :], NN)
              + _dot(dpg_ref[...], w_ref[G_COLS[0]:G_COLS[1], :], NN))
        r, xh = _rms_stats(x_ref[...])
        dxa, dg = _rms_bwd(xh, r, g1_ref[...], da)
        gx_ref[...] = dx1_ref[...] + dxa
        _accum(dg1_ref, dg)

    row = _rt(T, D_MODEL)
    return _tok_call(
        body, name=name, S=S, T=T, semantics="arbitrary",
        ins=[_tile(dpa, _rt(T, dpa.shape[1])), _tile(dlow, _rt(T, dlow.shape[1])), _tile(dps, _rt(T, dps.shape[1])),
             _tile(dpg, _rt(T, dpg.shape[1])), _whole(w_in_t), _tile(x, row), _tile(dx1, row), _whole(g1)],
        outs=[_out_tile((S, D_MODEL), F32, row), _out_acc((1, D_MODEL))])


TOKEN_TILE = 512


def _weight_grads_part(arrays, pairs, *, tk=TOKEN_TILE, out_dtype=BF16):
    S = arrays[0].shape[-2]
    tk = min(tk, S)
    n_in = len(arrays)

    def out_shape(i, j):
        a, b = arrays[i], arrays[j]
        if a.ndim == 3:
            return (a.shape[0], a.shape[2], b.shape[1])
        if b.ndim == 3:
            return (b.shape[0], a.shape[1], b.shape[2])
        return (a.shape[1], b.shape[1])

    shapes = [out_shape(i, j) for i, j in pairs]

    in_place = out_dtype == F32

    def body(ins, outs, accs):
        k = pl.program_id(0)
        if in_place:
            accs = outs

        @pl.when(k == 0)
        def _():
            for acc in accs:
                acc[...] = jnp.zeros_like(acc)

        for (i, j), acc in zip(pairs, accs):
            a_ref, b_ref = ins[i], ins[j]
            if len(a_ref.shape) == 3:
                for n in range(a_ref.shape[0]):
                    acc[n] += _dot(a_ref[n], b_ref[...], TN)
            elif len(b_ref.shape) == 3:
                a = a_ref[...]
                for n in range(b_ref.shape[0]):
                    acc[n] += _dot(a, b_ref[n], TN)
            else:
                b = b_ref[...]
                for m0 in range(0, a_ref.shape[1], 1024):
                    m1 = min(a_ref.shape[1], m0 + 1024)
                    acc[m0:m1, :] += _dot(a_ref[:, m0:m1], b, TN)

        if not in_place:
            @pl.when(k == S // tk - 1)
            def _():
                for out, acc in zip(outs, accs):
                    out[...] = acc[...].astype(out.dtype)

    def in_spec(a):
        if a.ndim == 3:
            return pl.BlockSpec((a.shape[0], tk, a.shape[2]), lambda k: (0, k, 0))
        return pl.BlockSpec((tk, a.shape[1]), lambda k: (k, 0))

    return dict(
        body=body, steps=S // tk, arrays=list(arrays),
        in_specs=[in_spec(a) for a in arrays],
        out_specs=[_acc(s) for s in shapes],
        out_shapes=[pltpu.HBM(s, out_dtype) for s in shapes],
        scratch=[] if in_place else [pltpu.VMEM(s, F32) for s in shapes],
        tiles=[(in_spec(a).block_shape, a.dtype) for a in arrays],
        resident=[(s, F32) for s in shapes] + ([] if in_place else [(s, BF16) for s in shapes]))


def _weight_grads_call(arrays, pairs, *, name, tk=TOKEN_TILE):
    part = _weight_grads_part(arrays, pairs, tk=tk)
    n_in, n_out = len(part["arrays"]), len(part["out_shapes"])

    def body(*refs):
        part["body"](refs[:n_in], refs[n_in:n_in + n_out], refs[n_in + n_out:])

    return pl.pallas_call(
        body, name=name, grid=(part["steps"],),
        in_specs=part["in_specs"], out_specs=part["out_specs"], out_shape=part["out_shapes"],
        scratch_shapes=part["scratch"],
        compiler_params=_cp(("arbitrary",), _vmem_limit(part["tiles"], part["resident"])),
    )(*_hbm(*part["arrays"]))


def _ffn_in_grad_call(h, dgu, *, name, tk=TOKEN_TILE):
    S = h.shape[0]
    tk = min(tk, S)
    shape = (D_FF, D_MODEL)

    def body(h_ref, dgu_ref, out_ref, acc):
        k = pl.program_id(1)

        @pl.when(k == 0)
        def _():
            acc[...] = jnp.zeros_like(acc)

        hv = h_ref[...]
        for c0, c1 in FF_CHUNKS:
            acc[c0:c1, :] += _dot(dgu_ref[:, c0:c1], hv, TN)

        @pl.when(k == S // tk - 1)
        def _():
            out_ref[...] = acc[...].astype(out_ref.dtype)

    tiles = [((tk, D_MODEL), BF16), ((tk, D_FF), BF16), (shape, BF16)]
    return pl.pallas_call(
        body, name=name, grid=(2, S // tk),
        in_specs=[pl.BlockSpec((tk, D_MODEL), lambda g, k: (k, 0)),
                  pl.BlockSpec((tk, D_FF), lambda g, k: (k, g))],
        out_specs=pl.BlockSpec(shape, lambda g, k: (g, 0)),
        out_shape=pltpu.HBM((2 * D_FF, D_MODEL), BF16),
        scratch_shapes=[pltpu.VMEM(shape, F32)],
        compiler_params=_cp(("parallel", "arbitrary"), _vmem_limit(tiles, [(shape, F32)])),
    )(*_hbm(h, dgu))


GLA_TILE = 256
Q_OFF, K_OFF, V_OFF, R_OFF = 0, QK, 2 * QK, 2 * QK + GV


def _tri(lower):
    r = lax.broadcasted_iota(jnp.int32, (CHUNK, CHUNK), 0)
    c = lax.broadcasted_iota(jnp.int32, (CHUNK, CHUNK), 1)
    return jnp.where((r >= c) if lower else (c >= r), 1.0, 0.0).astype(BF16)


def _gla_fwd_call(pa, alow, wg, bgate, gnorm, *, name):
    S = pa.shape[0]
    Tg = min(GLA_TILE, S)
    cb = Tg // CHUNK

    def body(pa_ref, al_ref, wg_ref, bg_ref, gn_ref, y_ref, st_ref, state):
        @pl.when(pl.program_id(0) == 0)
        def _():
            state[...] = jnp.zeros_like(state)

        logit = _dot(al_ref[...], wg_ref[...], NN) + bg_ref[...]
        ls = _logsig(logit) * (1.0 / 16.0)
        tri = _tri(True)
        for c in range(cb):
            rows = pl.ds(c * CHUNK, CHUNK)
            cum = _exact_mask_dot(tri, ls[c * CHUNK:(c + 1) * CHUNK])
            tot = cum[CHUNK - 1:CHUNK]
            kd = (pa_ref[rows, pl.ds(K_OFF, QK)].astype(F32) * jnp.exp(tot - cum)).astype(BF16)
            decay = jnp.exp(tot)
            for h in range(HEADS):
                lanes = slice(h * DK, (h + 1) * DK)
                new = state[h] * decay[:, lanes] + _dot(pa_ref[rows, pl.ds(V_OFF + h * DV, DV)], kd[:, lanes], TN)
                state[h] = new
                st_ref[c, h] = new
        for c in range(cb):
            rows = pl.ds(c * CHUNK, CHUNK)
            for h in range(HEADS):
                qs = (pa_ref[rows, pl.ds(Q_OFF + h * DK, DK)].astype(F32) * Q_SCALE).astype(BF16)
                o = _dot(qs, st_ref[c, h].astype(BF16), NT)
                rs = lax.rsqrt(jnp.mean(o * o, axis=-1, keepdims=True) + EPS)
                rr = pa_ref[rows, pl.ds(R_OFF + h * DV, DV)].astype(F32)
                y_ref[rows, pl.ds(h * DV, DV)] = (o * rs * gn_ref[h] * (rr * _sig(rr))).astype(BF16)

    return pl.pallas_call(
        body, name=name, grid=(S // Tg,),
        in_specs=[
            pl.BlockSpec((Tg, 2 * QK + 2 * GV), lambda t: (t, 0)),
            pl.BlockSpec((Tg, LANES), lambda t: (t, 0)),
            pl.BlockSpec((LANES, QK), lambda t: (0, 0)),
            pl.BlockSpec((1, QK), lambda t: (0, 0)),
            pl.BlockSpec((HEADS, 1, DV), lambda t: (0, 0, 0)),
        ],
        out_specs=[
            pl.BlockSpec((Tg, GV), lambda t: (t, 0)),
            pl.BlockSpec((cb, HEADS, DV, DK), lambda t: (t, 0, 0, 0)),
        ],
        out_shape=[pltpu.HBM((S, GV), BF16), pltpu.HBM((S // CHUNK, HEADS, DV, DK), F32)],
        scratch_shapes=[pltpu.VMEM((HEADS, DV, DK), F32)],
        compiler_params=_cp(("arbitrary",), VMEM_BIG),
    )(*_hbm(pa, alow, wg, bgate, gnorm))


def _gla_bwd_call(pa, alow, wg, bgate, gnorm, states, dy, *, name):
    S = pa.shape[0]
    Tg = min(GLA_TILE, S)
    cb = Tg // CHUNK
    nt = S // Tg

    def rev(t):
        return nt - 1 - t

    def body(pa_ref, al_ref, wg_ref, bg_ref, gn_ref, st_ref, prev_ref, dy_ref,
             dpa_ref, dlow_ref, dwg_ref, dgn_ref, dbg_ref, carry, pbuf, dlbuf):
        t = pl.program_id(0)

        @pl.when(t == 0)
        def _():
            carry[...] = jnp.zeros_like(carry)
            dwg_ref[...] = jnp.zeros_like(dwg_ref)
            dgn_ref[...] = jnp.zeros_like(dgn_ref)
            dbg_ref[...] = jnp.zeros_like(dbg_ref)

        logit = _dot(al_ref[...], wg_ref[...], NN) + bg_ref[...]
        ls = _logsig(logit) * (1.0 / 16.0)
        sneg = 1.0 / (1.0 + jnp.exp(logit))
        tri = _tri(True)
        upper = _tri(False)
        first_tile = rev(t) == 0
        heads = range(HEADS)

        w, decay, kd = [], [], []
        for c in range(cb):
            rows = pl.ds(c * CHUNK, CHUNK)
            cum = _exact_mask_dot(tri, ls[c * CHUNK:(c + 1) * CHUNK])
            tot = cum[CHUNK - 1:CHUNK]
            w.append(jnp.exp(tot - cum))
            decay.append(jnp.exp(tot))
            kd.append(pa_ref[rows, pl.ds(K_OFF, QK)].astype(F32) * w[c])

        dgn = [jnp.zeros((1, DV), F32) for _ in heads]
        for c in range(cb):
            rows = pl.ds(c * CHUNK, CHUNK)
            for h in heads:
                gn = gn_ref[h]
                qs = (pa_ref[rows, pl.ds(Q_OFF + h * DK, DK)].astype(F32) * Q_SCALE).astype(BF16)
                st16 = st_ref[c, h].astype(BF16)
                o = _dot(qs, st16, NT)
                rs = lax.rsqrt(jnp.mean(o * o, axis=-1, keepdims=True) + EPS)
                oh = o * rs
                rr = pa_ref[rows, pl.ds(R_OFF + h * DV, DV)].astype(F32)
                sr = _sig(rr)
                dyv = dy_ref[rows, pl.ds(h * DV, DV)].astype(F32)
                dpa_ref[rows, pl.ds(R_OFF + h * DV, DV)] = (
                    dyv * oh * gn * sr * (1.0 + rr * (1.0 - sr))).astype(BF16)
                don = dyv * (rr * sr)
                dgn[h] = dgn[h] + jnp.sum(don * oh, axis=0, keepdims=True)
                doh = don * gn
                do16 = (rs * (doh - oh * jnp.mean(doh * oh, axis=-1, keepdims=True))).astype(BF16)
                dpa_ref[rows, pl.ds(Q_OFF + h * DK, DK)] = (_dot(do16, st16, NN) * Q_SCALE).astype(BF16)
                pbuf[c, h] = _dot(do16, qs, TN)
        for h in heads:
            dgn_ref[h] += dgn[h]

        dkd = [[None] * HEADS for _ in range(cb)]
        ddecay = [[None] * HEADS for _ in range(cb)]
        for c in reversed(range(cb)):
            rows = pl.ds(c * CHUNK, CHUNK)
            for h in heads:
                lanes = slice(h * DK, (h + 1) * DK)
                gt = pbuf[c, h] + carry[h]
                gt16 = gt.astype(BF16)
                dkd[c][h] = _dot(pa_ref[rows, pl.ds(V_OFF + h * DV, DV)], gt16, NN)
                dpa_ref[rows, pl.ds(V_OFF + h * DV, DV)] = _dot(kd[c][:, lanes].astype(BF16), gt16, NT).astype(BF16)
                if c > 0:
                    st_prev = st_ref[c - 1, h]
                else:
                    st_prev = jnp.where(first_tile, 0.0, prev_ref[0, h])
                ddecay[c][h] = jnp.sum(gt * st_prev, axis=0, keepdims=True)
                carry[h] = gt * decay[c][:, lanes]

        dbg = jnp.zeros((1, QK), F32)
        for c in range(cb):
            rows = pl.ds(c * CHUNK, CHUNK)
            dkd_c = jnp.concatenate(dkd[c], axis=1)
            dpa_ref[rows, pl.ds(K_OFF, QK)] = (dkd_c * w[c]).astype(BF16)
            e = dkd_c * kd[c]
            dtot = jnp.sum(e, axis=0, keepdims=True) + jnp.concatenate(ddecay[c], axis=1) * decay[c]
            dls = dtot - _exact_mask_dot(upper, e)
            dlogit = dls * (1.0 / 16.0) * sneg[c * CHUNK:(c + 1) * CHUNK]
            dlbuf[rows, :] = dlogit.astype(BF16)
            dbg = dbg + jnp.sum(dlogit, axis=0, keepdims=True)
        dbg_ref[...] += dbg

        dl16 = dlbuf[...]
        dlow_ref[...] = _dot(dl16, wg_ref[...], NT).astype(BF16)
        dwg_ref[...] += _dot(al_ref[...], dl16, TN)

    wide = 2 * QK + 2 * GV
    tiles = [((Tg, wide), BF16), ((cb + 1, HEADS, DV, DK), F32), ((Tg, GV), BF16), ((Tg, wide), BF16),
             ((Tg, QK), BF16)]
    resident = [((cb + 1, HEADS, DV, DK), F32)]
    return pl.pallas_call(
        body, name=name, grid=(nt,),
        in_specs=[
            pl.BlockSpec((Tg, wide), lambda t: (rev(t), 0)),
            pl.BlockSpec((Tg, LANES), lambda t: (rev(t), 0)),
            pl.BlockSpec((LANES, QK), lambda t: (0, 0)),
            pl.BlockSpec((1, QK), lambda t: (0, 0)),
            pl.BlockSpec((HEADS, 1, DV), lambda t: (0, 0, 0)),
            pl.BlockSpec((cb, HEADS, DV, DK), lambda t: (rev(t), 0, 0, 0)),
            pl.BlockSpec((1, HEADS, DV, DK), lambda t: (jnp.maximum(rev(t) * cb - 1, 0), 0, 0, 0)),
            pl.BlockSpec((Tg, GV), lambda t: (rev(t), 0)),
        ],
        out_specs=[
            pl.BlockSpec((Tg, wide), lambda t: (rev(t), 0)),
            pl.BlockSpec((Tg, LANES), lambda t: (rev(t), 0)),
            pl.BlockSpec((LANES, QK), lambda t: (0, 0)),
            pl.BlockSpec((HEADS, 1, DV), lambda t: (0, 0, 0)),
            pl.BlockSpec((1, QK), lambda t: (0, 0)),
        ],
        out_shape=[pltpu.HBM((S, wide), BF16), pltpu.HBM((S, LANES), BF16), jax.ShapeDtypeStruct((LANES, QK), F32),
                   jax.ShapeDtypeStruct((HEADS, 1, DV), F32), jax.ShapeDtypeStruct((1, QK), F32)],
        scratch_shapes=[pltpu.VMEM((HEADS, DV, DK), F32), pltpu.VMEM((cb, HEADS, DV, DK), F32),
                        pltpu.VMEM((Tg, QK), BF16)],
        compiler_params=_cp(("arbitrary",), _vmem_limit(tiles, resident)),
    )(*_hbm(pa, alow, wg, bgate, gnorm, states, states, dy))


SGU_TILE = 256


def _sgu_mask():
    r = lax.broadcasted_iota(jnp.int32, (SBLOCK, SBLOCK), 0)
    c = lax.broadcasted_iota(jnp.int32, (SBLOCK, SBLOCK), 1)
    return (c < CHUNK) | (r >= CHUNK)


def _ln_stats(vf):
    mu = jnp.mean(vf, axis=-1, keepdims=True)
    xc = vf - mu
    rs = lax.rsqrt(jnp.mean(xc * xc, axis=-1, keepdims=True) + EPS)
    return rs, xc * rs


def _sgu_fwd_call(ps, ln_g, ln_b, w_sp, b_sp, *, name):
    S = ps.shape[0]
    Ts = min(SGU_TILE, S)

    def body(ps_ref, lg_ref, lb_ref, w_ref, b_ref, y_ref):
        mask = _sgu_mask()
        for g in range(GROUPS):
            wm = jnp.where(mask, w_ref[g], 0.0).astype(BF16)
            for p in range(Ts // SBLOCK):
                rows = pl.ds(p * SBLOCK, SBLOCK)
                u = _gelu(ps_ref[rows, pl.ds(g * DG, DG)].astype(F32))
                _, xh = _ln_stats(_gelu(ps_ref[rows, pl.ds(D_MODEL + g * DG, DG)].astype(F32)))
                vn = xh * lg_ref[g] + lb_ref[g]
                mixed = _dot(wm, vn.astype(BF16), NN) + b_ref[g]
                y_ref[rows, pl.ds(g * DG, DG)] = (u * mixed).astype(BF16)

    full3 = lambda a, b, c: pl.BlockSpec((a, b, c), lambda t: (0, 0, 0))
    return pl.pallas_call(
        body, name=name, grid=(S // Ts,),
        in_specs=[pl.BlockSpec((Ts, 2 * D_MODEL), lambda t: (t, 0)),
                  full3(GROUPS, 1, DG), full3(GROUPS, 1, DG), full3(GROUPS, SBLOCK, SBLOCK), full3(GROUPS, SBLOCK, 1)],
        out_specs=pl.BlockSpec((Ts, D_MODEL), lambda t: (t, 0)),
        out_shape=pltpu.HBM((S, D_MODEL), BF16),
        compiler_params=_cp(("parallel",)),
    )(*_hbm(ps, ln_g, ln_b, w_sp, b_sp))


def _sgu_bwd_call(ps, ln_g, ln_b, w_sp, b_sp, dy, *, name):
    S = ps.shape[0]
    Ts = min(SGU_TILE, S)

    def body(ps_ref, lg_ref, lb_ref, w_ref, b_ref, dy_ref, ds_ref, dlg_ref, dlb_ref, dw_ref, db_ref):
        @pl.when(pl.program_id(0) == 0)
        def _():
            dlg_ref[...] = jnp.zeros_like(dlg_ref)
            dlb_ref[...] = jnp.zeros_like(dlb_ref)
            dw_ref[...] = jnp.zeros_like(dw_ref)
            db_ref[...] = jnp.zeros_like(db_ref)

        mask = _sgu_mask()
        for g in range(GROUPS):
            wm = jnp.where(mask, w_ref[g], 0.0).astype(BF16)
            lg = lg_ref[g]
            for p in range(Ts // SBLOCK):
                rows = pl.ds(p * SBLOCK, SBLOCK)
                su = ps_ref[rows, pl.ds(g * DG, DG)].astype(F32)
                sv = ps_ref[rows, pl.ds(D_MODEL + g * DG, DG)].astype(F32)
                u, du = _gelu_and_grad(su)
                gv, dgv = _gelu_and_grad(sv)
                rs, xh = _ln_stats(gv)
                vn16 = (xh * lg + lb_ref[g]).astype(BF16)
                mixed = _dot(wm, vn16, NN) + b_ref[g]
                dyv = dy_ref[rows, pl.ds(g * DG, DG)].astype(F32)
                ds_ref[rows, pl.ds(g * DG, DG)] = (dyv * mixed * du).astype(BF16)
                dmix = dyv * u
                dmix16 = dmix.astype(BF16)
                db_ref[g] += jnp.sum(dmix, axis=-1, keepdims=True)
                dw_ref[g] += jnp.where(mask, _dot(dmix16, vn16, NT), 0.0)
                dvn = _dot(wm, dmix16, TN)
                dlg_ref[g] += jnp.sum(dvn * xh, axis=0, keepdims=True)
                dlb_ref[g] += jnp.sum(dvn, axis=0, keepdims=True)
                dxh = dvn * lg
                dvf = rs * (dxh - jnp.mean(dxh, axis=-1, keepdims=True)
                            - xh * jnp.mean(dxh * xh, axis=-1, keepdims=True))
                ds_ref[rows, pl.ds(D_MODEL + g * DG, DG)] = (dvf * dgv).astype(BF16)

    full3 = lambda a, b, c: pl.BlockSpec((a, b, c), lambda t: (0, 0, 0))
    tiles = [((Ts, 2 * D_MODEL), BF16), ((Ts, D_MODEL), BF16), ((Ts, 2 * D_MODEL), BF16)]
    return pl.pallas_call(
        body, name=name, grid=(S // Ts,),
        in_specs=[pl.BlockSpec((Ts, 2 * D_MODEL), lambda t: (t, 0)),
                  full3(GROUPS, 1, DG), full3(GROUPS, 1, DG), full3(GROUPS, SBLOCK, SBLOCK), full3(GROUPS, SBLOCK, 1),
                  pl.BlockSpec((Ts, D_MODEL), lambda t: (t, 0))],
        out_specs=[pl.BlockSpec((Ts, 2 * D_MODEL), lambda t: (t, 0)),
                   full3(GROUPS, 1, DG), full3(GROUPS, 1, DG), full3(GROUPS, SBLOCK, SBLOCK), full3(GROUPS, SBLOCK, 1)],
        out_shape=[pltpu.HBM((S, 2 * D_MODEL), BF16),
                   jax.ShapeDtypeStruct((GROUPS, 1, DG), F32), jax.ShapeDtypeStruct((GROUPS, 1, DG), F32),
                   jax.ShapeDtypeStruct((GROUPS, SBLOCK, SBLOCK), F32),
                   jax.ShapeDtypeStruct((GROUPS, SBLOCK, 1), F32)],
        compiler_params=_cp(("arbitrary",), _vmem_limit(tiles, [])),
    )(*_hbm(ps, ln_g, ln_b, w_sp, b_sp, dy))


def _position():
    return lax.axis_index("x"), lax.axis_index("y"), lax.axis_index("c")


def _gather_copies(srcs, dsts, send_sems, recv_sems, local_sems):
    x, y, c = _position()
    me, sibling = (x, y, c), (x, y, 1 - c)
    chips = [(1 - x, y), (x, 1 - y), (1 - x, 1 - y)]
    n = len(srcs)

    def slab(a, block):
        px, py, pc = block
        return dsts[a].at[4 * px + 2 * py + pc]

    def copy(a, k, block, to, src=None):
        return pltpu.make_async_remote_copy(
            src_ref=slab(a, block) if src is None else src, dst_ref=slab(a, block),
            send_sem=send_sems.at[7 * a + k], recv_sem=recv_sems.at[7 * a + k], device_id=to, device_id_type=MESH)

    mine = [pltpu.make_async_copy(srcs[a], slab(a, me), local_sems.at[a]) for a in range(n)]
    for cp in mine:
        cp.start()
    first = []
    for a in range(n):
        first.append(copy(a, 0, me, sibling, src=srcs[a]))
        first += [copy(a, 1 + j, me, (*chip, c), src=srcs[a]) for j, chip in enumerate(chips)]
    for cp in first:
        cp.start()
    passed = []
    for j, chip in enumerate(chips):
        for a in range(n):
            copy(a, 1 + j, (*chip, c), me).wait_recv()
            cp = copy(a, 4 + j, (*chip, c), sibling)
            cp.start()
            passed.append(cp)
    for a in range(n):
        copy(a, 0, sibling, me).wait_recv()
        for j, chip in enumerate(chips):
            copy(a, 4 + j, (*chip, 1 - c), me).wait_recv()
    for cp in first + passed:
        cp.wait_send()
    for cp in mine:
        cp.wait()


def _gather_copies_relayed(srcs, dsts, send_sems, recv_sems, local_sems, waves=1):
    x, y, c = _position()
    me, sibling = (x, y, c), (x, y, 1 - c)
    x_chip, y_chip, d_chip = (1 - x, y), (x, 1 - y), (1 - x, 1 - y)
    south = c == 0
    relay_from = (jnp.where(south, x, 1 - x), jnp.where(south, 1 - y, y), c)
    relay_to = (jnp.where(south, 1 - x, x), jnp.where(south, y, 1 - y), c)
    n = len(srcs)

    def cut(a, w):
        rows = srcs[a].shape[0]
        step = (rows // waves) // 16 * 16
        if waves == 1 or step == 0:
            return pl.ds(0, rows) if w == 0 else None
        return pl.ds(w * step, step if w < waves - 1 else rows - w * step)

    def slab(a, block, w):
        px, py, pc = block
        return dsts[a].at[4 * px + 2 * py + pc, cut(a, w)]

    def copy(a, k, w, block, to, own=False):
        sem = (7 * a + k) * waves + w
        return pltpu.make_async_remote_copy(
            src_ref=srcs[a].at[cut(a, w)] if own else slab(a, block, w), dst_ref=slab(a, block, w),
            send_sem=send_sems.at[sem], recv_sem=recv_sems.at[sem], device_id=to, device_id_type=MESH)

    pieces = [(a, w) for w in range(waves) for a in range(n) if cut(a, w) is not None]
    mine = [pltpu.make_async_copy(srcs[a], dsts[a].at[4 * x + 2 * y + c], local_sems.at[a]) for a in range(n)]
    for cp in mine:
        cp.start()
    sent = []
    for a, w in pieces:
        sent += [copy(a, 0, w, me, sibling, own=True), copy(a, 1, w, me, (*x_chip, c), own=True),
                 copy(a, 2, w, me, (*y_chip, c), own=True)]
    for cp in sent:
        cp.start()
    for a, w in pieces:
        copy(a, 1, w, (*x_chip, c), me).wait_recv()
        copy(a, 2, w, (*y_chip, c), me).wait_recv()
        later = [copy(a, 3, w, relay_from, relay_to), copy(a, 4, w, (*x_chip, c), sibling),
                 copy(a, 5, w, (*y_chip, c), sibling)]
        for cp in later:
            cp.start()
        sent += later
    for a, w in pieces:
        copy(a, 3, w, (*d_chip, c), me).wait_recv()
        cp = copy(a, 6, w, (*d_chip, c), sibling)
        cp.start()
        sent.append(cp)
    for a, w in pieces:
        copy(a, 0, w, sibling, me).wait_recv()
        for k, chip in ((4, x_chip), (5, y_chip), (6, d_chip)):
            copy(a, k, w, (*chip, 1 - c), me).wait_recv()
    for cp in sent:
        cp.wait_send()
    for cp in mine:
        cp.wait()


def _all_gather_hbm(shards, *, name, waves=1):
    n = len(shards)

    def body(*refs):
        srcs, dsts = refs[:n], refs[n:2 * n]
        send_sems, recv_sems, local_sems = refs[2 * n:]
        _gather_copies_relayed(srcs, dsts, send_sems, recv_sems, local_sems, waves=waves)

    return pl.pallas_call(
        body, name=name,
        in_specs=[ANY] * n, out_specs=[ANY] * n,
        out_shape=[jax.ShapeDtypeStruct((N_DEV, *s.shape), s.dtype) for s in shards],
        scratch_shapes=_comm_sems(n, waves),
    )(*shards)


FLIPS = [(fx, fy, fc) for fx in (0, 1) for fy in (0, 1) for fc in (0, 1)][1:]


def _scatter_copies(srcs, dsts, send_sems, recv_sems, local_sems, waves=1):
    n = len(srcs)
    x, y, c = _position()
    me = 4 * x + 2 * y + c
    mine = [pltpu.make_async_copy(srcs[a].at[me], dsts[a].at[me], local_sems.at[a]) for a in range(n)]
    for cp in mine:
        cp.start()
    peers = []
    for fx, fy, fc in FLIPS:
        tx = 1 - x if fx else x
        ty = 1 - y if fy else y
        tc = 1 - c if fc else c
        peers.append(((tx, ty, tc), 4 * tx + 2 * ty + tc))
    copies = []
    for w in range(waves):
        wave = []
        for k, (peer_id, peer) in enumerate(peers):
            for a in range(n):
                rows = srcs[a].shape[1]
                step = rows if waves == 1 else (rows // waves) // 16 * 16
                r0 = w * step
                cut = pl.ds(r0, step if w < waves - 1 else rows - r0)
                sem = (7 * a + k) * waves + w
                cp = pltpu.make_async_remote_copy(
                    src_ref=srcs[a].at[peer, cut], dst_ref=dsts[a].at[me, cut],
                    send_sem=send_sems.at[sem], recv_sem=recv_sems.at[sem],
                    device_id=peer_id, device_id_type=MESH)
                cp.start()
                wave.append(cp)
        for cp in wave:
            cp.wait_send()
        copies += wave
    for cp in copies:
        cp.wait_recv()
    for cp in mine:
        cp.wait()


def _pair_copies(srcs, dsts, send_sems, recv_sems, local_sems):
    x, y, c = _position()
    copies = [pltpu.make_async_remote_copy(
        src_ref=srcs[a], dst_ref=dsts[a], send_sem=send_sems.at[a], recv_sem=recv_sems.at[a],
        device_id=(x, y, 1 - c), device_id_type=MESH) for a in range(len(srcs))]
    for cp in copies:
        cp.start()
    for cp in copies:
        cp.wait()


def _chip_scatter_copies(srcs, dsts, send_sems, recv_sems, local_sems, waves=1):
    n = len(srcs)
    x, y, c = _position()
    here = 2 * x + y
    mine = [pltpu.make_async_copy(srcs[a].at[here], dsts[a].at[here], local_sems.at[a]) for a in range(n)]
    for cp in mine:
        cp.start()
    peers = []
    for fx, fy in ((1, 0), (0, 1), (1, 1)):
        tx = 1 - x if fx else x
        ty = 1 - y if fy else y
        peers.append(((tx, ty, c), 2 * tx + ty))
    copies = []
    for w in range(waves):
        wave = []
        for k, (peer_id, peer) in enumerate(peers):
            for a in range(n):
                rows = srcs[a].shape[1]
                step = rows if waves == 1 else (rows // waves) // 16 * 16
                r0 = w * step
                cut = pl.ds(r0, step if w < waves - 1 else rows - r0)
                sem = (3 * a + k) * waves + w
                cp = pltpu.make_async_remote_copy(
                    src_ref=srcs[a].at[peer, cut], dst_ref=dsts[a].at[here, cut],
                    send_sem=send_sems.at[sem], recv_sem=recv_sems.at[sem],
                    device_id=peer_id, device_id_type=MESH)
                cp.start()
                wave.append(cp)
        for cp in wave:
            cp.wait_send()
        copies += wave
    for cp in copies:
        cp.wait_recv()
    for cp in mine:
        cp.wait()


def _comm_sems(n, waves=1):
    return [pltpu.SemaphoreType.DMA((7 * n * waves,)), pltpu.SemaphoreType.DMA((7 * n * waves,)),
            pltpu.SemaphoreType.DMA((n,))]


def _handshake(peers):
    barrier = pltpu.get_barrier_semaphore()
    for peer in peers:
        pl.semaphore_signal(barrier, inc=1, device_id=peer, device_id_type=MESH)
    pl.semaphore_wait(barrier, len(peers))


def _sequencer_call(arrays, out_types, copies_fn, peers_fn, *, name, collective_id, waves=1):
    n = len(arrays)
    srcs = [jax.new_ref(a, memory_space=pltpu.MemorySpace.HBM) for a in arrays]
    dsts = [jax.empty_ref(t, memory_space=pltpu.MemorySpace.HBM) for t in out_types]
    extra = {} if waves == 1 else {"waves": waves}

    @pl.kernel(mesh=plsc.ScalarSubcoreMesh(axis_name="sequencer", num_cores=1), name=name,
               scratch_types=_comm_sems(n, waves), compiler_params=pltpu.CompilerParams(collective_id=collective_id))
    def launch(send_sems, recv_sems, local_sems):
        _handshake(peers_fn())
        copies_fn(srcs, dsts, send_sems, recv_sems, local_sems, **extra)

    launch()
    return [d[...] for d in dsts]


def _all_other_devices():
    x, y, c = _position()
    return [(1 - x if fx else x, 1 - y if fy else y, 1 - c if fc else c) for fx, fy, fc in FLIPS]


def _gather_relay_peers():
    x, y, c = _position()
    return [(x, y, 1 - c), (1 - x, y, c), (x, 1 - y, c)]


def _gather_peers():
    x, y, c = _position()
    return [(x, y, 1 - c), (1 - x, y, c), (x, 1 - y, c), (1 - x, 1 - y, c)]


def _small_gather_async(shards, *, name, collective_id):
    return _sequencer_call(shards, [jax.ShapeDtypeStruct((N_DEV, *s.shape), s.dtype) for s in shards],
                           _gather_copies, _gather_peers, name=name, collective_id=collective_id)


def _sibling():
    x, y, c = _position()
    return [(x, y, 1 - c)]


def _same_core_of_other_chips():
    x, y, c = _position()
    return [(1 - x, y, c), (x, 1 - y, c), (1 - x, 1 - y, c)]


def _pair_exchange_async(parts, *, name, collective_id):
    return _sequencer_call(parts, [jax.ShapeDtypeStruct(p.shape, p.dtype) for p in parts],
                           _pair_copies, _sibling, name=name, collective_id=collective_id)


def _chip_scatter_async(parts, *, name, collective_id, waves=1):
    return _sequencer_call(parts, [jax.ShapeDtypeStruct(p.shape, p.dtype) for p in parts],
                           _chip_scatter_copies, _same_core_of_other_chips, name=name, collective_id=collective_id,
                           waves=waves)


def _pair_sum_call(mine, theirs, *, name, tw=D_MODEL):
    n, r, w = mine.shape

    def body(a_ref, b_ref, o_ref):
        o_ref[...] = (a_ref[...].astype(F32) + b_ref[...].astype(F32)).astype(o_ref.dtype)

    spec = pl.BlockSpec((None, r, tw), lambda i, j: (i, 0, j))
    return pl.pallas_call(
        body, name=name, grid=(n, w // tw), in_specs=[spec, spec], out_specs=spec,
        out_shape=pltpu.HBM(mine.shape, mine.dtype),
        compiler_params=_cp(("parallel", "parallel")),
    )(*_hbm(mine, theirs))


def _scatter_blocks_async(parts, *, name, collective_id, waves=1):
    return _sequencer_call(parts, [jax.ShapeDtypeStruct(p.shape, p.dtype) for p in parts],
                           _scatter_copies, _all_other_devices, name=name, collective_id=collective_id, waves=waves)


def _all_gather_async(shards, *, name, collective_id, waves=1):
    return _sequencer_call(shards, [jax.ShapeDtypeStruct((N_DEV, *s.shape), s.dtype) for s in shards],
                           _gather_copies_relayed, _gather_relay_peers, name=name, collective_id=collective_id,
                           waves=waves)


def _adamw_math(w, g, m, v):
    m = ADAM_B1 * m + (1.0 - ADAM_B1) * g
    v = ADAM_B2 * v + (1.0 - ADAM_B2) * (g * g)
    m_hat = m / (1.0 - ADAM_B1 ** ADAM_STEP)
    v_hat = v / (1.0 - ADAM_B2 ** ADAM_STEP)
    delta = -ADAM_LR * (m_hat / (jnp.sqrt(v_hat) + ADAM_EPS) + ADAM_WD * w)
    return delta, m, v


def _adamw_reduce_call(recv, w, m, v, *, name, T=128):
    R, W = w.shape
    n_parts = recv.shape[0]
    if R % T == 0:
        tr, tw = T, W
    elif (R // 2) % 16 == 0:
        tr, tw = R // 2, W
    else:
        tr, tw = R, 2 * LANES

    def body(p_ref, w_ref, m_ref, v_ref, g_out, d_out, m_out, v_out):
        g = p_ref[0].astype(F32)
        for d in range(1, n_parts):
            g = g + p_ref[d].astype(F32)
        g_out[...] = g
        d_out[...], m_out[...], v_out[...] = _adamw_math(w_ref[...], g, m_ref[...], v_ref[...])

    row = pl.BlockSpec((tr, tw), lambda i, j: (i, j))
    out = pltpu.HBM((R, W), F32)
    return pl.pallas_call(
        body, name=name, grid=(R // tr, W // tw),
        in_specs=[pl.BlockSpec((n_parts, tr, tw), lambda i, j: (0, i, j)), row, row, row],
        out_specs=[row] * 4, out_shape=[out] * 4,
        compiler_params=_cp(("parallel", "parallel"), VMEM_BIG),
    )(*_hbm(recv, w, m, v))


SMALL_EARLY = (("b_gate", 8), ("w_spatial", 512), ("b_spatial", 8), ("norm_post_mix", 8), ("norm_pre_ffn", 8),
               ("norm_post_ffn", 8), ("w_gate_up", 128), ("gla_norm", 64), ("sgu_ln_g", 64), ("sgu_ln_b", 64))
SMALL_EARLY_AT = {}
for _name, _rows in SMALL_EARLY:
    SMALL_EARLY_AT[_name] = sum(r for _, r in SMALL_EARLY[:len(SMALL_EARLY_AT)])
SMALL_EARLY_ROWS = sum(r for _, r in SMALL_EARLY)
SMALL_LATE_ROWS = 16


def _small_early_rows(grads):
    def rows(a, n_rows):
        a = a.reshape(-1, LANES)
        return jnp.pad(a, ((0, n_rows - a.shape[0]), (0, 0)))

    def device_major(a, n_rows):
        r, c = a.shape[0], a.shape[1] // N_DEV
        a = a.reshape(r, N_DEV, c).transpose(1, 0, 2)
        a = jnp.pad(a, ((0, 0), (0, n_rows // N_DEV - r), (0, LANES - c)))
        return a.reshape(n_rows, LANES)

    pieces = []
    for name, n_rows in SMALL_EARLY:
        g = grads[name]
        if name in SMALL_SHARDED:
            pieces.append(device_major(g.reshape(g.shape[0], -1) if g.ndim == 2 else g.reshape(g.shape[0], g.shape[-1]), n_rows))
        else:
            pieces.append(rows(g, n_rows))
    return jnp.concatenate(pieces, axis=0)


def _small_update_call(got_early, got_late, w, m, v, *, name):
    names = list(SMALL)
    n_p = len(names)
    E = SMALL_EARLY_ROWS

    def body(early_ref, late_ref, *rest):
        w_refs, m_refs, v_refs = [dict(zip(names, rest[i * n_p:(i + 1) * n_p])) for i in range(3)]
        outs, tot = rest[3 * n_p:-1], rest[-1]
        loss_out = outs[0]
        g_out, d_out, m_out, v_out = [dict(zip(names, outs[1 + i * n_p:1 + (i + 1) * n_p])) for i in range(4)]
        acc, acc_late = early_ref[0], late_ref[0]
        for d in range(1, N_DEV):
            acc, acc_late = acc + early_ref[d], acc_late + late_ref[d]
        tot[0:E, :] = acc
        tot[E:E + SMALL_LATE_ROWS, :] = acc_late
        loss_out[...] = tot[E + 8:E + 9, :]

        x, y, c = _position()
        me = 4 * x + 2 * y + c

        def update(name, g, ix):
            g_out[name][ix] = g
            d_out[name][ix], m_out[name][ix], v_out[name][ix] = _adamw_math(
                w_refs[name][ix], g, m_refs[name][ix], v_refs[name][ix])

        for name in names:
            shape = w[name].shape
            if name in SMALL_SHARDED:
                per_dev = dict(SMALL_EARLY)[name] // N_DEV
                at = pl.multiple_of(SMALL_EARLY_AT[name] + me * per_dev, 8)
                g = tot[pl.ds(at, per_dev), :]
                update(name, g[:shape[1], :shape[2]], (0,))
            elif name == "w_spatial":
                for grp in range(GROUPS):
                    at = SMALL_EARLY_AT[name] + grp * SBLOCK
                    update(name, tot[at:at + SBLOCK, :], (0, grp))
            elif name == "b_spatial":
                at = SMALL_EARLY_AT[name]
                update(name, tot[at:at + GROUPS, :], (0,))
            else:
                at = E if name == "norm_pre_mix" else SMALL_EARLY_AT[name]
                for k in range(shape[1] // LANES):
                    update(name, tot[at + k:at + k + 1, :], (slice(None), pl.ds(k * LANES, LANES)))

    state = [s[n] for s in (w, m, v) for n in names]
    out_shapes = [jax.ShapeDtypeStruct((1, LANES), F32)] + [jax.ShapeDtypeStruct(w[n].shape, F32) for n in names] * 4
    outs = pl.pallas_call(
        body, name=name,
        in_specs=[VMEM_SPEC] * (2 + len(state)), out_specs=[VMEM_SPEC] * len(out_shapes), out_shape=out_shapes,
        scratch_shapes=[pltpu.VMEM((E + SMALL_LATE_ROWS, LANES), F32)],
    )(got_early, got_late, *state)
    per_name = {n: tuple(outs[1 + i * n_p + j] for i in range(4)) for j, n in enumerate(names)}
    return outs[0], per_name


def _tile_rows(n_elems):
    return -(-n_elems // (8 * LANES)) * 8


def _pack_rows(parts, rows):
    pieces = []
    for p in parts:
        q = p.reshape(-1, LANES)
        pieces.append(jnp.pad(q, ((0, _tile_rows(p.size) - q.shape[0]), (0, 0))))
    buf = jnp.concatenate(pieces, axis=0)
    return jnp.pad(buf, ((0, rows - buf.shape[0]), (0, 0)))


def _in_w_blocks_call(a, live, finished, *, name, tk=TOKEN_TILE):
    S = a.shape[0]
    tk = min(tk, S)
    steps = S // tk
    n_live = len(live)
    n_chips = N_DEV // 2
    out_shape = (n_chips, IN_BLK, D_MODEL)

    def body(a_ref, *rest):
        live_refs, done_refs = rest[:n_live], rest[n_live:n_live + len(finished)]
        keep_ref, send_ref = rest[n_live + len(finished):n_live + len(finished) + 2]
        accs = rest[n_live + len(finished) + 2:]
        k = pl.program_id(0)

        @pl.when(k == 0)
        def _():
            for acc in accs:
                acc[...] = jnp.zeros_like(acc)

        av = a_ref[...]
        for src, acc in zip(live_refs, accs):
            for m0 in range(0, src.shape[1], 1024):
                m1 = min(src.shape[1], m0 + 1024)
                acc[m0:m1, :] += _dot(src[:, m0:m1], av, TN)

        @pl.when(k == steps - 1)
        def _():
            groups = [(acc, cols) for acc, (_, cols) in zip(accs, live)]
            groups += [(ref, cols) for ref, (_, cols) in zip(done_refs, finished)]
            core = lax.axis_index("c")

            def cut(d, out_ref):
                lo, hi = IN_BLK * d, IN_BLK * (d + 1)
                for ref, (c0, c1) in groups:
                    s0, e0 = max(lo, c0), min(hi, c1)
                    if s0 < e0:
                        out_ref[d // 2, s0 - lo:e0 - lo, :] = ref[s0 - c0:e0 - c0, :].astype(BF16)

            for d in range(N_DEV):
                pl.when(core == d % 2)(lambda d=d: cut(d, keep_ref))
                pl.when(core != d % 2)(lambda d=d: cut(d, send_ref))

    acc_shapes = [(arr.shape[1], D_MODEL) for arr, _ in live]
    tiles = [((tk, D_MODEL), BF16)] + [((tk, arr.shape[1]), BF16) for arr, _ in live]
    resident = ([(arr.shape, arr.dtype) for arr, _ in finished] + [(out_shape, BF16)] * 2
                + [(s, F32) for s in acc_shapes])
    return pl.pallas_call(
        body, name=name, grid=(steps,),
        in_specs=[pl.BlockSpec((tk, D_MODEL), lambda k: (k, 0))]
        + [pl.BlockSpec((tk, arr.shape[1]), lambda k: (k, 0)) for arr, _ in live]
        + [_res(arr.shape) for arr, _ in finished],
        out_specs=[_acc(out_shape)] * 2,
        out_shape=[pltpu.HBM(out_shape, BF16)] * 2,
        scratch_shapes=[pltpu.VMEM(s, F32) for s in acc_shapes],
        compiler_params=_cp(("arbitrary",), _vmem_limit(tiles, resident)),
    )(*_hbm(a, *[arr for arr, _ in live], *[arr for arr, _ in finished]))


def _local_step(x, target, W, scatter, finish, pair, chip_scatter):
    g1, g2, g3, g4 = [W[n].reshape(1, D_MODEL) for n in ("norm_pre_mix", "norm_post_mix", "norm_pre_ffn", "norm_post_ffn")]
    wfi, wfo = W["w_ffn_in"].reshape(2 * D_FF, D_MODEL), W["w_ffn_out"]
    wg = jnp.pad(W["w_gate_up"], ((0, LANES - RANK), (0, 0))).astype(BF16)
    bgate = W["b_gate"].reshape(1, QK)
    gnorm = W["gla_norm"].reshape(HEADS, 1, DV)
    ln_g = W["sgu_ln_g"].reshape(GROUPS, 1, DG)
    ln_b = W["sgu_ln_b"].reshape(GROUPS, 1, DG)
    w_sp = W["w_spatial"]
    b_sp = W["b_spatial"].reshape(GROUPS, SBLOCK, 1)

    a, pa, alow, ps, pg, w_in_t = _in_proj_call(x, g1, W["w_in_blocks"], name="in_proj")
    y_gla, states = _gla_fwd_call(pa, alow, wg, bgate, gnorm, name="gla_fwd")
    y_sgu = _sgu_fwd_call(ps, ln_g, ln_b, w_sp, b_sp, name="sgu_fwd")
    t1, t2, merged, mix, x1, h = _mixer_tail_call(y_gla, y_sgu, pg, x, W["w_branch_gla"], W["w_branch_sgu"],
                                                  W["w_out"], g2, g3, name="mixer_tail")
    gu, z = _ffn_in_call(h, wfi, name="ffn_in")
    loss, dx2, dy, dg4 = _ffn_out_loss_call(z, wfo, x1, target, g4, name="ffn_out_loss")

    grads = {"norm_post_ffn": dg4}
    done = {}
    dgu = _ffn_out_bwd_call(dy, wfo, gu, name="ffn_out_bwd")
    dw_ffn_out = _weight_grads_call([z, dy], [(0, 1)], name="d_ffn_out_w")[0].reshape(N_DEV, D_FF // N_DEV, D_MODEL)
    dw_ffn_in = _ffn_in_grad_call(h, dgu, name="d_ffn_in_w").reshape(N_DEV, FF_BLK, D_MODEL)
    dgu, dw_ffn_out, dw_ffn_in = lax.optimization_barrier((dgu, dw_ffn_out, dw_ffn_in))
    ffn_received = scatter(("w_ffn_out", "w_ffn_in"), [dw_ffn_out, dw_ffn_in])
    dx1, dmix, grads["norm_pre_ffn"], grads["norm_post_mix"] = _ffn_in_bwd_call(dgu, wfi, dx2, x1, mix, g3, g2, name="ffn_in_bwd")
    dt1, dt2, dpg, dy_gla, dy_sgu = _mixer_bwd_call(dmix, t1, t2, pg, W["w_out"], W["w_branch_gla"], W["w_branch_sgu"],
                                                    name="mixer_bwd")
    rows = D_MODEL // N_DEV
    mixer_grads = _weight_grads_call([merged, dmix, y_gla, dt1, y_sgu, dt2], [(0, 1), (2, 3), (4, 5)], name="d_mixer_w")
    dy_gla, dy_sgu, mixer_grads = lax.optimization_barrier((dy_gla, dy_sgu, mixer_grads))
    mixer_received = scatter(("w_out", "w_branch_gla", "w_branch_sgu"),
                             [g.reshape(N_DEV, rows, D_MODEL) for g in mixer_grads])
    dps, dlg, dlb, dwsp, dbsp = _sgu_bwd_call(ps, ln_g, ln_b, w_sp, b_sp, dy_sgu, name="sgu_bwd")
    dw_s, dw_g = _weight_grads_call([a, dps, dpg], [(1, 0), (2, 0)], name="d_in_w_sgu_gates")
    dy_gla, dw_s, dw_g = lax.optimization_barrier((dy_gla, dw_s, dw_g))
    dpa, dlow, dwg, dgn, dbg = _gla_bwd_call(pa, alow, wg, bgate, gnorm, states, dy_gla, name="gla_bwd")
    ffn_received, mixer_received, dpa, dlow = lax.optimization_barrier((ffn_received, mixer_received, dpa, dlow))
    keep, send = _in_w_blocks_call(a, [(dpa, A_COLS), (dlow, LOW_COLS)], [(dw_s, S_COLS), (dw_g, G_COLS)],
                                   name="d_in_w_blocks")
    ffn_received, mixer_received, send = lax.optimization_barrier((ffn_received, mixer_received, send))
    from_sibling = pair(send)
    done.update({**finish(ffn_received), **finish(mixer_received)})
    done, keep = lax.optimization_barrier((done, keep))
    chip_sum = _pair_sum_call(keep, from_sibling, name="pair_sum_w_in")
    chip_sum, dpa = lax.optimization_barrier((chip_sum, dpa))
    in_received = chip_scatter(chip_sum)
    grad_x, grads["norm_pre_mix"] = _in_proj_bwd_call(dpa, dlow, dps, dpg, w_in_t, x, dx1, g1, name="in_proj_bwd")

    grads["w_gate_up"] = dwg[:RANK]
    grads["b_gate"] = dbg
    grads["gla_norm"] = dgn
    grads["sgu_ln_g"] = dlg
    grads["sgu_ln_b"] = dlb
    grads["w_spatial"] = dwsp
    grads["b_spatial"] = dbsp
    done.update(finish({"w_in": in_received}))
    return loss, grad_x, grads, done


WEIGHTS = ("norm_pre_mix", "w_in", "w_gate_up", "b_gate", "gla_norm", "sgu_ln_g", "sgu_ln_b", "w_spatial",
           "b_spatial", "w_branch_gla", "w_branch_sgu", "w_out", "norm_post_mix", "norm_pre_ffn", "w_ffn_in",
           "w_ffn_out", "norm_post_ffn")
BIG = ("w_in", "w_branch_gla", "w_branch_sgu", "w_out", "w_ffn_in", "w_ffn_out")
MIXER = ("w_branch_gla", "w_branch_sgu", "w_out")
FFN = ("w_ffn_in", "w_ffn_out")
COLUMN_SHARDED = ("w_in", "w_ffn_in")
LATE_SCATTER_WAVES = 4
GATHER_WAVES = 4
SMALL = tuple(n for n in WEIGHTS if n not in BIG)
SMALL_SHARDED = ("w_gate_up", "gla_norm", "sgu_ln_g", "sgu_ln_b")
SMALL_GATHER_ROWS = 32


def kernel(x, norm_pre_mix, w_in, w_gate_up, b_gate, gla_norm, sgu_ln_g, sgu_ln_b, w_spatial, b_spatial, w_branch_gla, w_branch_sgu, w_out, norm_post_mix, norm_pre_ffn, w_ffn_in, w_ffn_out, norm_post_ffn, loss_target, m_norm_pre_mix, m_w_in, m_w_gate_up, m_b_gate, m_gla_norm, m_sgu_ln_g, m_sgu_ln_b, m_w_spatial, m_b_spatial, m_w_branch_gla, m_w_branch_sgu, m_w_out, m_norm_post_mix, m_norm_pre_ffn, m_w_ffn_in, m_w_ffn_out, m_norm_post_ffn, v_norm_pre_mix, v_w_in, v_w_gate_up, v_b_gate, v_gla_norm, v_sgu_ln_g, v_sgu_ln_b, v_w_spatial, v_b_spatial, v_w_branch_gla, v_w_branch_sgu, v_w_out, v_norm_post_mix, v_norm_pre_ffn, v_w_ffn_in, v_w_ffn_out, v_norm_post_ffn):
    given = dict(locals())
    def local(a, n):
        return a[0].T if n in COLUMN_SHARDED else a[0]

    w = {n: local(given[n], n) for n in WEIGHTS}
    m = {n: local(given["m_" + n], n) for n in WEIGHTS}
    v = {n: local(given["v_" + n], n) for n in WEIGHTS}
    xs, target = x[0], loss_target[0]

    small_shard = _pack_rows([w[n] for n in SMALL_SHARDED], SMALL_GATHER_ROWS)
    first = _all_gather_async([w["w_in"].astype(BF16), small_shard], name="gather_w_in", collective_id=2,
                              waves=GATHER_WAVES)
    rest, first, m["w_in"], v["w_in"] = lax.optimization_barrier(
        ([w[n].astype(BF16) for n in MIXER + FFN], first, m["w_in"], v["w_in"]))
    rest_blocks = _all_gather_async(rest, name="gather_rest", collective_id=1)
    W = {n: w[n] for n in SMALL if n not in SMALL_SHARDED}
    blocks = dict(zip(MIXER + FFN, rest_blocks))
    for n in ("w_branch_gla", "w_branch_sgu", "w_out", "w_ffn_out"):
        W[n] = blocks[n].reshape(-1, D_MODEL)
    W["w_ffn_in"] = blocks["w_ffn_in"]
    W["w_in_blocks"] = first[0]
    small_blocks = first[1]
    off = 0
    for n in SMALL_SHARDED:
        r, c = w[n].shape
        blk = small_blocks[:, off:off + r * c // LANES].reshape(N_DEV, r, c)
        W[n] = blk.transpose(1, 0, 2).reshape(r, N_DEV * c)
        off += _tile_rows(r * c)

    scatter_ids = iter((3, 4))

    def scatter(names, parts):
        got = _scatter_blocks_async(parts, name="scatter_" + "_".join(names), collective_id=next(scatter_ids))
        return dict(zip(names, got))

    def finish(received):
        return {n: _adamw_reduce_call(r, w[n], m[n], v[n], name="adamw_" + n) for n, r in received.items()}

    def pair(part):
        return _pair_exchange_async([part], name="pair_w_in", collective_id=5)[0]

    def chip_scatter(part):
        return _chip_scatter_async([part], name="scatter_w_in", collective_id=6, waves=LATE_SCATTER_WAVES)[0]

    loss_part, grad_x, grads, big_done = _local_step(xs, target, W, scatter, finish, pair, chip_scatter)

    late = jnp.concatenate([grads["norm_pre_mix"].reshape(8, LANES), jnp.broadcast_to(loss_part, (8, LANES))], axis=0)
    gathered = _small_gather_async([_small_early_rows(grads), late], name="gather_small", collective_id=7)
    gathered, big_done["w_in"] = lax.optimization_barrier((gathered, big_done["w_in"]))
    loss_row, small_done = _small_update_call(
        *gathered, *[{n: given[prefix + n] for n in SMALL} for prefix in ("", "m_", "v_")], name="small_update")

    def pick(i):
        return [small_done[n][i] if n in SMALL else (big_done[n][i].T if n in COLUMN_SHARDED else big_done[n][i])[None]
                for n in WEIGHTS]

    return (loss_row[0, 0], grad_x[None], *pick(0), *pick(1), *pick(2), *pick(3))
```

```python
import jax
import jax.numpy as jnp
from jax import lax
from jax.experimental import pallas as pl
from jax.experimental.pallas import tpu as pltpu
from jax.experimental.pallas import tpu_sc as plsc

F32 = jnp.float32
BF16 = jnp.bfloat16

D_MODEL = 1024
N_DEV = 8
CHUNK = 64
HEADS = 4
DK = 128
DV = 256
QK = HEADS * DK
GV = HEADS * DV
RANK = 16
GROUPS = 4
SBLOCK = 128
DG = 256
D_FF = 2816
FF_BLK = 704
EPS = 1e-6
Q_SCALE = DK ** -0.5
LANES = 128
VMEM_BIG = 48 * 1024 * 1024

D_IN = 7184
IN_BLK = 898
A_COLS = (0, 3072)
LOW_COLS = (3072, 3088)
S_COLS = (3088, 5136)
G_COLS = (5136, 7184)

ADAM_LR = 0.001
ADAM_B1 = 0.9
ADAM_B2 = 0.999
ADAM_EPS = 1e-08
ADAM_WD = 0.01
ADAM_STEP = 10

MESH = pl.DeviceIdType.MESH
ANY = pl.BlockSpec(memory_space=pl.ANY)
VMEM_SPEC = pl.BlockSpec(memory_space=pltpu.VMEM)


def _cp(sem=None, vmem=None):
    return pltpu.CompilerParams(dimension_semantics=sem, vmem_limit_bytes=vmem)


def _hbm(*arrays):
    return [pltpu.with_memory_space_constraint(a, pltpu.HBM) for a in arrays]


def _sig(x):
    return 0.5 * jnp.tanh(0.5 * x) + 0.5


GELU_C = 0.7978845608028654
GELU_A = 0.044715


def _gelu(x):
    t = jnp.tanh((GELU_C * x) * (1.0 + GELU_A * (x * x)))
    return (0.5 * x) * (1.0 + t)


def _gelu_and_grad(x):
    x2 = x * x
    t = jnp.tanh((GELU_C * x) * (1.0 + GELU_A * x2))
    one_t = 1.0 + t
    hx = 0.5 * x
    grad = 0.5 * one_t + (hx * (1.0 - t * t)) * (GELU_C + (3.0 * GELU_A * GELU_C) * x2)
    return hx * one_t, grad


def _logsig(x):
    return jnp.minimum(x, 0.0) - jnp.log1p(jnp.exp(-jnp.abs(x)))


def _dot(a, b, dims):
    return lax.dot_general(a, b, (dims, ((), ())), preferred_element_type=F32)


NN = ((1,), (0,))
NT = ((1,), (1,))
TN = ((0,), (0,))


def _exact_mask_dot(mask_bf16, x):
    hi = x.astype(BF16)
    r1 = x - hi.astype(F32)
    mid = r1.astype(BF16)
    lo = (r1 - mid.astype(F32)).astype(BF16)
    return _dot(mask_bf16, hi, NN) + _dot(mask_bf16, mid, NN) + _dot(mask_bf16, lo, NN)


ROW_TILE = 512
SUB_ROWS = 256


def _sub_tiles(T):
    return [pl.ds(r0, min(SUB_ROWS, T)) for r0 in range(0, T, SUB_ROWS)]


def _rt(T, W, c=0):
    return pl.BlockSpec((T, W), lambda i: (i, c))


def _rt3(nb, T, W):
    return pl.BlockSpec((nb, T, W), lambda i: (0, i, 0))


def _res(shape):
    nd = len(shape)
    return pl.BlockSpec(tuple(shape), lambda i: (0,) * nd, pipeline_mode=pl.Buffered(1))


def _acc(shape):
    nd = len(shape)
    return pl.BlockSpec(tuple(shape), lambda i: (0,) * nd, pipeline_mode=pl.Buffered(1))


def _nbytes(shape, dtype):
    n = jnp.dtype(dtype).itemsize
    for s in shape:
        n *= s
    return n


def _vmem_limit(tiles, resident, temps=16 * 1024 * 1024):
    need = 2 * sum(_nbytes(s, d) for s, d in tiles) + sum(_nbytes(s, d) for s, d in resident) + temps
    return min(need, 60 * 1024 * 1024)


def _tok_call(body, *, name, S, T, ins, outs, semantics="parallel"):
    tiles = [(spec.block_shape, a.dtype) for a, spec, kind in ins + outs if kind == "tile"]
    resident = [(a.shape, a.dtype) for a, spec, kind in ins + outs if kind == "res"]
    return pl.pallas_call(
        body, name=name, grid=(S // T,),
        in_specs=[spec for _, spec, _ in ins], out_specs=[spec for _, spec, _ in outs],
        out_shape=[pltpu.HBM(a.shape, a.dtype) for a, _, _ in outs],
        compiler_params=_cp((semantics,), _vmem_limit(tiles, resident)),
    )(*_hbm(*[a for a, _, _ in ins]))


def _tile(a, spec):
    return (a, spec, "tile")


def _whole(a):
    return (a, _res(a.shape), "res")


def _out_tile(shape, dtype, spec):
    return (jax.ShapeDtypeStruct(shape, dtype), spec, "tile")


def _out_acc(shape, dtype=F32):
    return (jax.ShapeDtypeStruct(shape, dtype), _acc(shape), "res")


def _rms_stats(x):
    r = lax.rsqrt(jnp.mean(x * x, axis=-1, keepdims=True) + EPS)
    return r, x * r


def _rms_bwd(xh, r, g, dy):
    dxh = dy * g
    dx = r * (dxh - xh * jnp.mean(dxh * xh, axis=-1, keepdims=True))
    dg = jnp.sum(dy * xh, axis=0, keepdims=True)
    return dx, dg


def _accum(ref, val):
    @pl.when(pl.program_id(0) == 0)
    def _():
        ref[...] = val

    @pl.when(pl.program_id(0) > 0)
    def _():
        ref[...] += val


def _dot_rows_t(a, w_ref, row0, o_ref, chunk=1024):
    n = o_ref.shape[1]
    for n0 in range(0, n, chunk):
        n1 = min(n, n0 + chunk)
        o_ref[:, n0:n1] = _dot(a, w_ref[row0 + n0:row0 + n1, :], NT).astype(o_ref.dtype)


def _in_proj_call(x, g1, w_blocks, *, name, T=ROW_TILE):
    S = x.shape[0]

    def body(x_ref, g_ref, blk_ref, a_ref, pa_ref, al_ref, ps_ref, pg_ref, w_ref):
        @pl.when(pl.program_id(0) == 0)
        def _():
            for d in range(N_DEV):
                w_ref[d * IN_BLK:(d + 1) * IN_BLK, :] = blk_ref[d]

        _, xh = _rms_stats(x_ref[...])
        a = (xh * g_ref[...]).astype(BF16)
        a_ref[...] = a
        _dot_rows_t(a, w_ref, A_COLS[0], pa_ref)
        _dot_rows_t(a, w_ref, LOW_COLS[0], al_ref)
        _dot_rows_t(a, w_ref, S_COLS[0], ps_ref)
        _dot_rows_t(a, w_ref, G_COLS[0], pg_ref)

    widths = (D_MODEL, A_COLS[1] - A_COLS[0], LANES, S_COLS[1] - S_COLS[0], G_COLS[1] - G_COLS[0])
    return _tok_call(
        body, name=name, S=S, T=T, semantics="arbitrary",
        ins=[_tile(x, _rt(T, D_MODEL)), _whole(g1), _whole(w_blocks)],
        outs=[_out_tile((S, w), BF16, _rt(T, w)) for w in widths] + [_out_acc((D_IN, D_MODEL), BF16)])


def _mixer_tail_call(y_gla, y_sgu, pg, x, w_bg, w_bs, w_out, g2, g3, *, name, T=ROW_TILE):
    S = x.shape[0]

    def body(yg_ref, ys_ref, pg_ref, x_ref, wbg_ref, wbs_ref, wo_ref, g2_ref, g3_ref,
             t1_ref, t2_ref, mg_ref, mix_ref, x1_ref, h_ref):
        for rows in _sub_tiles(T):
            t1 = _dot(yg_ref[rows, :], wbg_ref[...], NN)
            t2 = _dot(ys_ref[rows, :], wbs_ref[...], NN)
            t1_ref[rows, :] = t1.astype(BF16)
            t2_ref[rows, :] = t2.astype(BF16)
            sg = _sig(pg_ref[rows, pl.ds(0, D_MODEL)].astype(F32))
            ss = _sig(pg_ref[rows, pl.ds(D_MODEL, D_MODEL)].astype(F32))
            merged = (sg * t1 + ss * t2).astype(BF16)
            mg_ref[rows, :] = merged
            mix = _dot(merged, wo_ref[...], NN)
            mix_ref[rows, :] = mix
            _, mh = _rms_stats(mix)
            x1 = x_ref[rows, :] + mh * g2_ref[...]
            x1_ref[rows, :] = x1
            _, xh = _rms_stats(x1)
            h_ref[rows, :] = (xh * g3_ref[...]).astype(BF16)

    row = _rt(T, D_MODEL)
    b16 = lambda: _out_tile((S, D_MODEL), BF16, row)
    f32 = lambda: _out_tile((S, D_MODEL), F32, row)
    return _tok_call(
        body, name=name, S=S, T=T,
        ins=[_tile(y_gla, row), _tile(y_sgu, row), _tile(pg, _rt(T, 2 * D_MODEL)), _tile(x, row),
             _whole(w_bg), _whole(w_bs), _whole(w_out), _whole(g2), _whole(g3)],
        outs=[b16(), b16(), b16(), f32(), f32(), b16()])


FF_CHUNKS = ((0, 1024), (1024, 2048), (2048, D_FF))


def _ffn_in_call(h, wfi, *, name, T=ROW_TILE):
    S = h.shape[0]

    def body(h_ref, w_ref, gu_ref, z_ref):
        hv = h_ref[...]
        for c0, c1 in FF_CHUNKS:
            gate = _dot(hv, w_ref[c0:c1, :], NT)
            up = _dot(hv, w_ref[D_FF + c0:D_FF + c1, :], NT)
            gu_ref[:, c0:c1] = gate.astype(BF16)
            gu_ref[:, D_FF + c0:D_FF + c1] = up.astype(BF16)
            z_ref[:, c0:c1] = (gate * _sig(gate) * up).astype(BF16)

    return _tok_call(
        body, name=name, S=S, T=T,
        ins=[_tile(h, _rt(T, D_MODEL)), _whole(wfi)],
        outs=[_out_tile((S, 2 * D_FF), BF16, _rt(T, 2 * D_FF)),
              _out_tile((S, D_FF), BF16, _rt(T, D_FF))])


def _ffn_out_loss_call(z, wfo, x1, target, g4, *, name, T=ROW_TILE):
    S = x1.shape[0]

    def body(z_ref, w_ref, x1_ref, t_ref, g4_ref, loss_ref, dx2_ref, dy_ref, dg4_ref):
        loss = jnp.zeros((1, 1), F32)
        dg4 = jnp.zeros((1, D_MODEL), F32)
        for rows in _sub_tiles(T):
            y = _dot(z_ref[rows, :], w_ref[...], NN)
            r, yh = _rms_stats(y)
            diff = x1_ref[rows, :] + yh * g4_ref[...] - t_ref[rows, :]
            loss = loss + 0.5 * jnp.sum(jnp.mean(diff * diff, axis=-1, keepdims=True), axis=0, keepdims=True)
            dx2 = diff * (1.0 / D_MODEL)
            dx2_ref[rows, :] = dx2
            dy, dg = _rms_bwd(yh, r, g4_ref[...], dx2)
            dy_ref[rows, :] = dy.astype(BF16)
            dg4 = dg4 + dg
        _accum(loss_ref, jnp.broadcast_to(loss, (1, LANES)))
        _accum(dg4_ref, dg4)

    row = _rt(T, D_MODEL)
    return _tok_call(
        body, name=name, S=S, T=T, semantics="arbitrary",
        ins=[_tile(z, _rt(T, D_FF)), _whole(wfo), _tile(x1, row), _tile(target, row), _whole(g4)],
        outs=[_out_acc((1, LANES)), _out_tile((S, D_MODEL), F32, row), _out_tile((S, D_MODEL), BF16, row),
              _out_acc((1, D_MODEL))])


def _ffn_out_bwd_call(dy, wfo, gu, *, name, T=ROW_TILE):
    S = dy.shape[0]

    def body(dy_ref, w_ref, gu_ref, dgu_ref):
        dyv = dy_ref[...]
        for c0, c1 in FF_CHUNKS:
            dz = _dot(dyv, w_ref[c0:c1, :], NT).astype(BF16)
            gt = gu_ref[:, c0:c1]
            up = gu_ref[:, D_FF + c0:D_FF + c1]
            s = _sig(gt)
            dgu_ref[:, c0:c1] = dz * up * s * (1.0 + gt * (1.0 - s))
            dgu_ref[:, D_FF + c0:D_FF + c1] = dz * gt * s

    wide = _rt(T, 2 * D_FF)
    return _tok_call(
        body, name=name, S=S, T=T,
        ins=[_tile(dy, _rt(T, D_MODEL)), _whole(wfo), _tile(gu, wide)],
        outs=[_out_tile((S, 2 * D_FF), BF16, wide)])[0]


def _ffn_in_bwd_call(dgu, wfi, dx2, x1, mix, g3, g2, *, name, T=ROW_TILE):
    S = x1.shape[0]

    def body(dgu_ref, w_ref, dx2_ref, x1_ref, mix_ref, g3_ref, g2_ref, dx1_ref, dmix_ref, dg3_ref, dg2_ref):
        dg3 = jnp.zeros((1, D_MODEL), F32)
        dg2 = jnp.zeros((1, D_MODEL), F32)
        for rows in _sub_tiles(T):
            dh = _dot(dgu_ref[rows, :], w_ref[...], NN)
            r3, xh = _rms_stats(x1_ref[rows, :])
            d3, g = _rms_bwd(xh, r3, g3_ref[...], dh)
            dg3 = dg3 + g
            dx1 = dx2_ref[rows, :] + d3
            dx1_ref[rows, :] = dx1
            r2, mh = _rms_stats(mix_ref[rows, :])
            dmix, g = _rms_bwd(mh, r2, g2_ref[...], dx1)
            dg2 = dg2 + g
            dmix_ref[rows, :] = dmix.astype(BF16)
        _accum(dg3_ref, dg3)
        _accum(dg2_ref, dg2)

    row = _rt(T, D_MODEL)
    return _tok_call(
        body, name=name, S=S, T=T, semantics="arbitrary",
        ins=[_tile(dgu, _rt(T, 2 * D_FF)), _whole(wfi), _tile(dx2, row), _tile(x1, row), _tile(mix, row),
             _whole(g3), _whole(g2)],
        outs=[_out_tile((S, D_MODEL), F32, row), _out_tile((S, D_MODEL), BF16, row),
              _out_acc((1, D_MODEL)), _out_acc((1, D_MODEL))])


def _mixer_bwd_call(dmix, t1, t2, pg, w_out, w_bg, w_bs, *, name, T=ROW_TILE):
    S = dmix.shape[0]

    def body(dmix_ref, t1_ref, t2_ref, pg_ref, wo_ref, wbg_ref, wbs_ref, dt1_ref, dt2_ref, dpg_ref, dyg_ref, dys_ref):
        gg, gs = pl.ds(0, D_MODEL), pl.ds(D_MODEL, D_MODEL)
        for rows in _sub_tiles(T):
            dm = _dot(dmix_ref[rows, :], wo_ref[...], NT)
            sg = _sig(pg_ref[rows, gg].astype(F32))
            ss = _sig(pg_ref[rows, gs].astype(F32))
            dt1 = (dm * sg).astype(BF16)
            dt2 = (dm * ss).astype(BF16)
            dt1_ref[rows, :] = dt1
            dt2_ref[rows, :] = dt2
            dpg_ref[rows, gg] = (dm * t1_ref[rows, :].astype(F32) * sg * (1.0 - sg)).astype(BF16)
            dpg_ref[rows, gs] = (dm * t2_ref[rows, :].astype(F32) * ss * (1.0 - ss)).astype(BF16)
            dyg_ref[rows, :] = _dot(dt1, wbg_ref[...], NT).astype(BF16)
            dys_ref[rows, :] = _dot(dt2, wbs_ref[...], NT).astype(BF16)

    row = _rt(T, D_MODEL)
    wide = _rt(T, 2 * D_MODEL)
    b16 = lambda: _out_tile((S, D_MODEL), BF16, row)
    return _tok_call(
        body, name=name, S=S, T=T,
        ins=[_tile(dmix, row), _tile(t1, row), _tile(t2, row), _tile(pg, wide), _whole(w_out), _whole(w_bg), _whole(w_bs)],
        outs=[b16(), b16(), _out_tile((S, 2 * D_MODEL), BF16, wide), b16(), b16()])


def _in_proj_bwd_call(dpa, dlow, dps, dpg, w_in_t, x, dx1, g1, *, name, T=ROW_TILE):
    S = x.shape[0]

    def body(dpa_ref, dl_ref, dps_ref, dpg_ref, w_ref, x_ref, dx1_ref, g1_ref, gx_ref, dg1_ref):
        da = (_dot(dpa_ref[...], w_ref[A_COLS[0]:A_COLS[1], :], NN)
              + _dot(dl_ref[...], w_ref[LOW_COLS[0]:LOW_COLS[0] + LANES, :], NN)
              + _dot(dps_ref[...], w_ref[S_COLS[0]:S_COLS[1], :], NN)
              + _dot(dpg_ref[...], w_ref[G_COLS[0]:G_COLS[1], :], NN))
        r, xh = _rms_stats(x_ref[...])
        dxa, dg = _rms_bwd(xh, r, g1_ref[...], da)
        gx_ref[...] = dx1_ref[...] + dxa
        _accum(dg1_ref, dg)

    row = _rt(T, D_MODEL)
    return _tok_call(
        body, name=name, S=S, T=T, semantics="arbitrary",
        ins=[_tile(dpa, _rt(T, dpa.shape[1])), _tile(dlow, _rt(T, dlow.shape[1])), _tile(dps, _rt(T, dps.shape[1])),
             _tile(dpg, _rt(T, dpg.shape[1])), _whole(w_in_t), _tile(x, row), _tile(dx1, row), _whole(g1)],
        outs=[_out_tile((S, D_MODEL), F32, row), _out_acc((1, D_MODEL))])


TOKEN_TILE = 512


def _weight_grads_part(arrays, pairs, *, tk=TOKEN_TILE, out_dtype=BF16):
    S = arrays[0].shape[-2]
    tk = min(tk, S)
    n_in = len(arrays)

    def out_shape(i, j):
        a, b = arrays[i], arrays[j]
        if a.ndim == 3:
            return (a.shape[0], a.shape[2], b.shape[1])
        if b.ndim == 3:
            return (b.shape[0], a.shape[1], b.shape[2])
        return (a.shape[1], b.shape[1])

    shapes = [out_shape(i, j) for i, j in pairs]

    in_place = out_dtype == F32

    def body(ins, outs, accs):
        k = pl.program_id(0)
        if in_place:
            accs = outs

        @pl.when(k == 0)
        def _():
            for acc in accs:
                acc[...] = jnp.zeros_like(acc)

        for (i, j), acc in zip(pairs, accs):
            a_ref, b_ref = ins[i], ins[j]
            if len(a_ref.shape) == 3:
                for n in range(a_ref.shape[0]):
                    acc[n] += _dot(a_ref[n], b_ref[...], TN)
            elif len(b_ref.shape) == 3:
                a = a_ref[...]
                for n in range(b_ref.shape[0]):
                    acc[n] += _dot(a, b_ref[n], TN)
            else:
                b = b_ref[...]
                for m0 in range(0, a_ref.shape[1], 1024):
                    m1 = min(a_ref.shape[1], m0 + 1024)
                    acc[m0:m1, :] += _dot(a_ref[:, m0:m1], b, TN)

        if not in_place:
            @pl.when(k == S // tk - 1)
            def _():
                for out, acc in zip(outs, accs):
                    out[...] = acc[...].astype(out.dtype)

    def in_spec(a):
        if a.ndim == 3:
            return pl.BlockSpec((a.shape[0], tk, a.shape[2]), lambda k: (0, k, 0))
        return pl.BlockSpec((tk, a.shape[1]), lambda k: (k, 0))

    return dict(
        body=body, steps=S // tk, arrays=list(arrays),
        in_specs=[in_spec(a) for a in arrays],
        out_specs=[_acc(s) for s in shapes],
        out_shapes=[pltpu.HBM(s, out_dtype) for s in shapes],
        scratch=[] if in_place else [pltpu.VMEM(s, F32) for s in shapes],
        tiles=[(in_spec(a).block_shape, a.dtype) for a in arrays],
        resident=[(s, F32) for s in shapes] + ([] if in_place else [(s, BF16) for s in shapes]))


def _weight_grads_call(arrays, pairs, *, name, tk=TOKEN_TILE):
    part = _weight_grads_part(arrays, pairs, tk=tk)
    n_in, n_out = len(part["arrays"]), len(part["out_shapes"])

    def body(*refs):
        part["body"](refs[:n_in], refs[n_in:n_in + n_out], refs[n_in + n_out:])

    return pl.pallas_call(
        body, name=name, grid=(part["steps"],),
        in_specs=part["in_specs"], out_specs=part["out_specs"], out_shape=part["out_shapes"],
        scratch_shapes=part["scratch"],
        compiler_params=_cp(("arbitrary",), _vmem_limit(part["tiles"], part["resident"])),
    )(*_hbm(*part["arrays"]))


def _ffn_in_grad_call(h, dgu, *, name, tk=TOKEN_TILE):
    S = h.shape[0]
    tk = min(tk, S)
    shape = (D_FF, D_MODEL)

    def body(h_ref, dgu_ref, out_ref, acc):
        k = pl.program_id(1)

        @pl.when(k == 0)
        def _():
            acc[...] = jnp.zeros_like(acc)

        hv = h_ref[...]
        for c0, c1 in FF_CHUNKS:
            acc[c0:c1, :] += _dot(dgu_ref[:, c0:c1], hv, TN)

        @pl.when(k == S // tk - 1)
        def _():
            out_ref[...] = acc[...].astype(out_ref.dtype)

    tiles = [((tk, D_MODEL), BF16), ((tk, D_FF), BF16), (shape, BF16)]
    return pl.pallas_call(
        body, name=name, grid=(2, S // tk),
        in_specs=[pl.BlockSpec((tk, D_MODEL), lambda g, k: (k, 0)),
                  pl.BlockSpec((tk, D_FF), lambda g, k: (k, g))],
        out_specs=pl.BlockSpec(shape, lambda g, k: (g, 0)),
        out_shape=pltpu.HBM((2 * D_FF, D_MODEL), BF16),
        scratch_shapes=[pltpu.VMEM(shape, F32)],
        compiler_params=_cp(("parallel", "arbitrary"), _vmem_limit(tiles, [(shape, F32)])),
    )(*_hbm(h, dgu))


GLA_TILE = 256
Q_OFF, K_OFF, V_OFF, R_OFF = 0, QK, 2 * QK, 2 * QK + GV


def _tri(lower):
    r = lax.broadcasted_iota(jnp.int32, (CHUNK, CHUNK), 0)
    c = lax.broadcasted_iota(jnp.int32, (CHUNK, CHUNK), 1)
    return jnp.where((r >= c) if lower else (c >= r), 1.0, 0.0).astype(BF16)


def _gla_fwd_call(pa, alow, wg, bgate, gnorm, *, name):
    S = pa.shape[0]
    Tg = min(GLA_TILE, S)
    cb = Tg // CHUNK

    def body(pa_ref, al_ref, wg_ref, bg_ref, gn_ref, y_ref, st_ref, state):
        @pl.when(pl.program_id(0) == 0)
        def _():
            state[...] = jnp.zeros_like(state)

        logit = _dot(al_ref[...], wg_ref[...], NN) + bg_ref[...]
        ls = _logsig(logit) * (1.0 / 16.0)
        tri = _tri(True)
        for c in range(cb):
            rows = pl.ds(c * CHUNK, CHUNK)
            cum = _exact_mask_dot(tri, ls[c * CHUNK:(c + 1) * CHUNK])
            tot = cum[CHUNK - 1:CHUNK]
            kd = (pa_ref[rows, pl.ds(K_OFF, QK)].astype(F32) * jnp.exp(tot - cum)).astype(BF16)
            decay = jnp.exp(tot)
            for h in range(HEADS):
                lanes = slice(h * DK, (h + 1) * DK)
                new = state[h] * decay[:, lanes] + _dot(pa_ref[rows, pl.ds(V_OFF + h * DV, DV)], kd[:, lanes], TN)
                state[h] = new
                st_ref[c, h] = new
        for c in range(cb):
            rows = pl.ds(c * CHUNK, CHUNK)
            for h in range(HEADS):
                qs = (pa_ref[rows, pl.ds(Q_OFF + h * DK, DK)].astype(F32) * Q_SCALE).astype(BF16)
                o = _dot(qs, st_ref[c, h].astype(BF16), NT)
                rs = lax.rsqrt(jnp.mean(o * o, axis=-1, keepdims=True) + EPS)
                rr = pa_ref[rows, pl.ds(R_OFF + h * DV, DV)].astype(F32)
                y_ref[rows, pl.ds(h * DV, DV)] = (o * rs * gn_ref[h] * (rr * _sig(rr))).astype(BF16)

    return pl.pallas_call(
        body, name=name, grid=(S // Tg,),
        in_specs=[
            pl.BlockSpec((Tg, 2 * QK + 2 * GV), lambda t: (t, 0)),
            pl.BlockSpec((Tg, LANES), lambda t: (t, 0)),
            pl.BlockSpec((LANES, QK), lambda t: (0, 0)),
            pl.BlockSpec((1, QK), lambda t: (0, 0)),
            pl.BlockSpec((HEADS, 1, DV), lambda t: (0, 0, 0)),
        ],
        out_specs=[
            pl.BlockSpec((Tg, GV), lambda t: (t, 0)),
            pl.BlockSpec((cb, HEADS, DV, DK), lambda t: (t, 0, 0, 0)),
        ],
        out_shape=[pltpu.HBM((S, GV), BF16), pltpu.HBM((S // CHUNK, HEADS, DV, DK), F32)],
        scratch_shapes=[pltpu.VMEM((HEADS, DV, DK), F32)],
        compiler_params=_cp(("arbitrary",), VMEM_BIG),
    )(*_hbm(pa, alow, wg, bgate, gnorm))


def _gla_bwd_call(pa, alow, wg, bgate, gnorm, states, dy, *, name):
    S = pa.shape[0]
    Tg = min(GLA_TILE, S)
    cb = Tg // CHUNK
    nt = S // Tg

    def rev(t):
        return nt - 1 - t

    def body(pa_ref, al_ref, wg_ref, bg_ref, gn_ref, st_ref, prev_ref, dy_ref,
             dpa_ref, dlow_ref, dwg_ref, dgn_ref, dbg_ref, carry, pbuf, dlbuf):
        t = pl.program_id(0)

        @pl.when(t == 0)
        def _():
            carry[...] = jnp.zeros_like(carry)
            dwg_ref[...] = jnp.zeros_like(dwg_ref)
            dgn_ref[...] = jnp.zeros_like(dgn_ref)
            dbg_ref[...] = jnp.zeros_like(dbg_ref)

        logit = _dot(al_ref[...], wg_ref[...], NN) + bg_ref[...]
        ls = _logsig(logit) * (1.0 / 16.0)
        sneg = 1.0 / (1.0 + jnp.exp(logit))
        tri = _tri(True)
        upper = _tri(False)
        first_tile = rev(t) == 0
        heads = range(HEADS)

        w, decay, kd = [], [], []
        for c in range(cb):
            rows = pl.ds(c * CHUNK, CHUNK)
            cum = _exact_mask_dot(tri, ls[c * CHUNK:(c + 1) * CHUNK])
            tot = cum[CHUNK - 1:CHUNK]
            w.append(jnp.exp(tot - cum))
            decay.append(jnp.exp(tot))
            kd.append(pa_ref[rows, pl.ds(K_OFF, QK)].astype(F32) * w[c])

        dgn = [jnp.zeros((1, DV), F32) for _ in heads]
        for c in range(cb):
            rows = pl.ds(c * CHUNK, CHUNK)
            for h in heads:
                gn = gn_ref[h]
                qs = (pa_ref[rows, pl.ds(Q_OFF + h * DK, DK)].astype(F32) * Q_SCALE).astype(BF16)
                st16 = st_ref[c, h].astype(BF16)
                o = _dot(qs, st16, NT)
                rs = lax.rsqrt(jnp.mean(o * o, axis=-1, keepdims=True) + EPS)
                oh = o * rs
                rr = pa_ref[rows, pl.ds(R_OFF + h * DV, DV)].astype(F32)
                sr = _sig(rr)
                dyv = dy_ref[rows, pl.ds(h * DV, DV)].astype(F32)
                dpa_ref[rows, pl.ds(R_OFF + h * DV, DV)] = (
                    dyv * oh * gn * sr * (1.0 + rr * (1.0 - sr))).astype(BF16)
                don = dyv * (rr * sr)
                dgn[h] = dgn[h] + jnp.sum(don * oh, axis=0, keepdims=True)
                doh = don * gn
                do16 = (rs * (doh - oh * jnp.mean(doh * oh, axis=-1, keepdims=True))).astype(BF16)
                dpa_ref[rows, pl.ds(Q_OFF + h * DK, DK)] = (_dot(do16, st16, NN) * Q_SCALE).astype(BF16)
                pbuf[c, h] = _dot(do16, qs, TN)
        for h in heads:
            dgn_ref[h] += dgn[h]

        dkd = [[None] * HEADS for _ in range(cb)]
        ddecay = [[None] * HEADS for _ in range(cb)]
        for c in reversed(range(cb)):
            rows = pl.ds(c * CHUNK, CHUNK)
            for h in heads:
                lanes = slice(h * DK, (h + 1) * DK)
                gt = pbuf[c, h] + carry[h]
                gt16 = gt.astype(BF16)
                dkd[c][h] = _dot(pa_ref[rows, pl.ds(V_OFF + h * DV, DV)], gt16, NN)
                dpa_ref[rows, pl.ds(V_OFF + h * DV, DV)] = _dot(kd[c][:, lanes].astype(BF16), gt16, NT).astype(BF16)
                if c > 0:
                    st_prev = st_ref[c - 1, h]
                else:
                    st_prev = jnp.where(first_tile, 0.0, prev_ref[0, h])
                ddecay[c][h] = jnp.sum(gt * st_prev, axis=0, keepdims=True)
                carry[h] = gt * decay[c][:, lanes]

        dbg = jnp.zeros((1, QK), F32)
        for c in range(cb):
            rows = pl.ds(c * CHUNK, CHUNK)
            dkd_c = jnp.concatenate(dkd[c], axis=1)
            dpa_ref[rows, pl.ds(K_OFF, QK)] = (dkd_c * w[c]).astype(BF16)
            e = dkd_c * kd[c]
            dtot = jnp.sum(e, axis=0, keepdims=True) + jnp.concatenate(ddecay[c], axis=1) * decay[c]
            dls = dtot - _exact_mask_dot(upper, e)
            dlogit = dls * (1.0 / 16.0) * sneg[c * CHUNK:(c + 1) * CHUNK]
            dlbuf[rows, :] = dlogit.astype(BF16)
            dbg = dbg + jnp.sum(dlogit, axis=0, keepdims=True)
        dbg_ref[...] += dbg

        dl16 = dlbuf[...]
        dlow_ref[...] = _dot(dl16, wg_ref[...], NT).astype(BF16)
        dwg_ref[...] += _dot(al_ref[...], dl16, TN)

    wide = 2 * QK + 2 * GV
    tiles = [((Tg, wide), BF16), ((cb + 1, HEADS, DV, DK), F32), ((Tg, GV), BF16), ((Tg, wide), BF16),
             ((Tg, QK), BF16)]
    resident = [((cb + 1, HEADS, DV, DK), F32)]
    return pl.pallas_call(
        body, name=name, grid=(nt,),
        in_specs=[
            pl.BlockSpec((Tg, wide), lambda t: (rev(t), 0)),
            pl.BlockSpec((Tg, LANES), lambda t: (rev(t), 0)),
            pl.BlockSpec((LANES, QK), lambda t: (0, 0)),
            pl.BlockSpec((1, QK), lambda t: (0, 0)),
            pl.BlockSpec((HEADS, 1, DV), lambda t: (0, 0, 0)),
            pl.BlockSpec((cb, HEADS, DV, DK), lambda t: (rev(t), 0, 0, 0)),
            pl.BlockSpec((1, HEADS, DV, DK), lambda t: (jnp.maximum(rev(t) * cb - 1, 0), 0, 0, 0)),
            pl.BlockSpec((Tg, GV), lambda t: (rev(t), 0)),
        ],
        out_specs=[
            pl.BlockSpec((Tg, wide), lambda t: (rev(t), 0)),
            pl.BlockSpec((Tg, LANES), lambda t: (rev(t), 0)),
            pl.BlockSpec((LANES, QK), lambda t: (0, 0)),
            pl.BlockSpec((HEADS, 1, DV), lambda t: (0, 0, 0)),
            pl.BlockSpec((1, QK), lambda t: (0, 0)),
        ],
        out_shape=[pltpu.HBM((S, wide), BF16), pltpu.HBM((S, LANES), BF16), jax.ShapeDtypeStruct((LANES, QK), F32),
                   jax.ShapeDtypeStruct((HEADS, 1, DV), F32), jax.ShapeDtypeStruct((1, QK), F32)],
        scratch_shapes=[pltpu.VMEM((HEADS, DV, DK), F32), pltpu.VMEM((cb, HEADS, DV, DK), F32),
                        pltpu.VMEM((Tg, QK), BF16)],
        compiler_params=_cp(("arbitrary",), _vmem_limit(tiles, resident)),
    )(*_hbm(pa, alow, wg, bgate, gnorm, states, states, dy))


SGU_TILE = 256


def _sgu_mask():
    r = lax.broadcasted_iota(jnp.int32, (SBLOCK, SBLOCK), 0)
    c = lax.broadcasted_iota(jnp.int32, (SBLOCK, SBLOCK), 1)
    return (c < CHUNK) | (r >= CHUNK)


def _ln_stats(vf):
    mu = jnp.mean(vf, axis=-1, keepdims=True)
    xc = vf - mu
    rs = lax.rsqrt(jnp.mean(xc * xc, axis=-1, keepdims=True) + EPS)
    return rs, xc * rs


def _sgu_fwd_call(ps, ln_g, ln_b, w_sp, b_sp, *, name):
    S = ps.shape[0]
    Ts = min(SGU_TILE, S)

    def body(ps_ref, lg_ref, lb_ref, w_ref, b_ref, y_ref):
        mask = _sgu_mask()
        for g in range(GROUPS):
            wm = jnp.where(mask, w_ref[g], 0.0).astype(BF16)
            for p in range(Ts // SBLOCK):
                rows = pl.ds(p * SBLOCK, SBLOCK)
                u = _gelu(ps_ref[rows, pl.ds(g * DG, DG)].astype(F32))
                _, xh = _ln_stats(_gelu(ps_ref[rows, pl.ds(D_MODEL + g * DG, DG)].astype(F32)))
                vn = xh * lg_ref[g] + lb_ref[g]
                mixed = _dot(wm, vn.astype(BF16), NN) + b_ref[g]
                y_ref[rows, pl.ds(g * DG, DG)] = (u * mixed).astype(BF16)

    full3 = lambda a, b, c: pl.BlockSpec((a, b, c), lambda t: (0, 0, 0))
    return pl.pallas_call(
        body, name=name, grid=(S // Ts,),
        in_specs=[pl.BlockSpec((Ts, 2 * D_MODEL), lambda t: (t, 0)),
                  full3(GROUPS, 1, DG), full3(GROUPS, 1, DG), full3(GROUPS, SBLOCK, SBLOCK), full3(GROUPS, SBLOCK, 1)],
        out_specs=pl.BlockSpec((Ts, D_MODEL), lambda t: (t, 0)),
        out_shape=pltpu.HBM((S, D_MODEL), BF16),
        compiler_params=_cp(("parallel",)),
    )(*_hbm(ps, ln_g, ln_b, w_sp, b_sp))


def _sgu_bwd_call(ps, ln_g, ln_b, w_sp, b_sp, dy, *, name):
    S = ps.shape[0]
    Ts = min(SGU_TILE, S)

    def body(ps_ref, lg_ref, lb_ref, w_ref, b_ref, dy_ref, ds_ref, dlg_ref, dlb_ref, dw_ref, db_ref):
        @pl.when(pl.program_id(0) == 0)
        def _():
            dlg_ref[...] = jnp.zeros_like(dlg_ref)
            dlb_ref[...] = jnp.zeros_like(dlb_ref)
            dw_ref[...] = jnp.zeros_like(dw_ref)
            db_ref[...] = jnp.zeros_like(db_ref)

        mask = _sgu_mask()
        for g in range(GROUPS):
            wm = jnp.where(mask, w_ref[g], 0.0).astype(BF16)
            lg = lg_ref[g]
            for p in range(Ts // SBLOCK):
                rows = pl.ds(p * SBLOCK, SBLOCK)
                su = ps_ref[rows, pl.ds(g * DG, DG)].astype(F32)
                sv = ps_ref[rows, pl.ds(D_MODEL + g * DG, DG)].astype(F32)
                u, du = _gelu_and_grad(su)
                gv, dgv = _gelu_and_grad(sv)
                rs, xh = _ln_stats(gv)
                vn16 = (xh * lg + lb_ref[g]).astype(BF16)
                mixed = _dot(wm, vn16, NN) + b_ref[g]
                dyv = dy_ref[rows, pl.ds(g * DG, DG)].astype(F32)
                ds_ref[rows, pl.ds(g * DG, DG)] = (dyv * mixed * du).astype(BF16)
                dmix = dyv * u
                dmix16 = dmix.astype(BF16)
                db_ref[g] += jnp.sum(dmix, axis=-1, keepdims=True)
                dw_ref[g] += jnp.where(mask, _dot(dmix16, vn16, NT), 0.0)
                dvn = _dot(wm, dmix16, TN)
                dlg_ref[g] += jnp.sum(dvn * xh, axis=0, keepdims=True)
                dlb_ref[g] += jnp.sum(dvn, axis=0, keepdims=True)
                dxh = dvn * lg
                dvf = rs * (dxh - jnp.mean(dxh, axis=-1, keepdims=True)
                            - xh * jnp.mean(dxh * xh, axis=-1, keepdims=True))
                ds_ref[rows, pl.ds(D_MODEL + g * DG, DG)] = (dvf * dgv).astype(BF16)

    full3 = lambda a, b, c: pl.BlockSpec((a, b, c), lambda t: (0, 0, 0))
    tiles = [((Ts, 2 * D_MODEL), BF16), ((Ts, D_MODEL), BF16), ((Ts, 2 * D_MODEL), BF16)]
    return pl.pallas_call(
        body, name=name, grid=(S // Ts,),
        in_specs=[pl.BlockSpec((Ts, 2 * D_MODEL), lambda t: (t, 0)),
                  full3(GROUPS, 1, DG), full3(GROUPS, 1, DG), full3(GROUPS, SBLOCK, SBLOCK), full3(GROUPS, SBLOCK, 1),
                  pl.BlockSpec((Ts, D_MODEL), lambda t: (t, 0))],
        out_specs=[pl.BlockSpec((Ts, 2 * D_MODEL), lambda t: (t, 0)),
                   full3(GROUPS, 1, DG), full3(GROUPS, 1, DG), full3(GROUPS, SBLOCK, SBLOCK), full3(GROUPS, SBLOCK, 1)],
        out_shape=[pltpu.HBM((S, 2 * D_MODEL), BF16),
                   jax.ShapeDtypeStruct((GROUPS, 1, DG), F32), jax.ShapeDtypeStruct((GROUPS, 1, DG), F32),
                   jax.ShapeDtypeStruct((GROUPS, SBLOCK, SBLOCK), F32),
                   jax.ShapeDtypeStruct((GROUPS, SBLOCK, 1), F32)],
        compiler_params=_cp(("arbitrary",), _vmem_limit(tiles, [])),
    )(*_hbm(ps, ln_g, ln_b, w_sp, b_sp, dy))


def _position():
    return lax.axis_index("x"), lax.axis_index("y"), lax.axis_index("c")


def _gather_copies(srcs, dsts, send_sems, recv_sems, local_sems):
    x, y, c = _position()
    me, sibling = (x, y, c), (x, y, 1 - c)
    chips = [(1 - x, y), (x, 1 - y), (1 - x, 1 - y)]
    n = len(srcs)

    def slab(a, block):
        px, py, pc = block
        return dsts[a].at[4 * px + 2 * py + pc]

    def copy(a, k, block, to, src=None):
        return pltpu.make_async_remote_copy(
            src_ref=slab(a, block) if src is None else src, dst_ref=slab(a, block),
            send_sem=send_sems.at[7 * a + k], recv_sem=recv_sems.at[7 * a + k], device_id=to, device_id_type=MESH)

    mine = [pltpu.make_async_copy(srcs[a], slab(a, me), local_sems.at[a]) for a in range(n)]
    for cp in mine:
        cp.start()
    first = []
    for a in range(n):
        first.append(copy(a, 0, me, sibling, src=srcs[a]))
        first += [copy(a, 1 + j, me, (*chip, c), src=srcs[a]) for j, chip in enumerate(chips)]
    for cp in first:
        cp.start()
    passed = []
    for j, chip in enumerate(chips):
        for a in range(n):
            copy(a, 1 + j, (*chip, c), me).wait_recv()
            cp = copy(a, 4 + j, (*chip, c), sibling)
            cp.start()
            passed.append(cp)
    for a in range(n):
        copy(a, 0, sibling, me).wait_recv()
        for j, chip in enumerate(chips):
            copy(a, 4 + j, (*chip, 1 - c), me).wait_recv()
    for cp in first + passed:
        cp.wait_send()
    for cp in mine:
        cp.wait()


def _gather_copies_relayed(srcs, dsts, send_sems, recv_sems, local_sems, waves=1):
    x, y, c = _position()
    me, sibling = (x, y, c), (x, y, 1 - c)
    x_chip, y_chip, d_chip = (1 - x, y), (x, 1 - y), (1 - x, 1 - y)
    south = c == 0
    relay_from = (jnp.where(south, x, 1 - x), jnp.where(south, 1 - y, y), c)
    relay_to = (jnp.where(south, 1 - x, x), jnp.where(south, y, 1 - y), c)
    n = len(srcs)

    def cut(a, w):
        rows = srcs[a].shape[0]
        step = (rows // waves) // 16 * 16
        if waves == 1 or step == 0:
            return pl.ds(0, rows) if w == 0 else None
        return pl.ds(w * step, step if w < waves - 1 else rows - w * step)

    def slab(a, block, w):
        px, py, pc = block
        return dsts[a].at[4 * px + 2 * py + pc, cut(a, w)]

    def copy(a, k, w, block, to, own=False):
        sem = (7 * a + k) * waves + w
        return pltpu.make_async_remote_copy(
            src_ref=srcs[a].at[cut(a, w)] if own else slab(a, block, w), dst_ref=slab(a, block, w),
            send_sem=send_sems.at[sem], recv_sem=recv_sems.at[sem], device_id=to, device_id_type=MESH)

    pieces = [(a, w) for w in range(waves) for a in range(n) if cut(a, w) is not None]
    mine = [pltpu.make_async_copy(srcs[a], dsts[a].at[4 * x + 2 * y + c], local_sems.at[a]) for a in range(n)]
    for cp in mine:
        cp.start()
    sent = []
    for a, w in pieces:
        sent += [copy(a, 0, w, me, sibling, own=True), copy(a, 1, w, me, (*x_chip, c), own=True),
                 copy(a, 2, w, me, (*y_chip, c), own=True)]
    for cp in sent:
        cp.start()
    for a, w in pieces:
        copy(a, 1, w, (*x_chip, c), me).wait_recv()
        copy(a, 2, w, (*y_chip, c), me).wait_recv()
        later = [copy(a, 3, w, relay_from, relay_to), copy(a, 4, w, (*x_chip, c), sibling),
                 copy(a, 5, w, (*y_chip, c), sibling)]
        for cp in later:
            cp.start()
        sent += later
    for a, w in pieces:
        copy(a, 3, w, (*d_chip, c), me).wait_recv()
        cp = copy(a, 6, w, (*d_chip, c), sibling)
        cp.start()
        sent.append(cp)
    for a, w in pieces:
        copy(a, 0, w, sibling, me).wait_recv()
        for k, chip in ((4, x_chip), (5, y_chip), (6, d_chip)):
            copy(a, k, w, (*chip, 1 - c), me).wait_recv()
    for cp in sent:
        cp.wait_send()
    for cp in mine:
        cp.wait()


def _all_gather_hbm(shards, *, name, waves=1):
    n = len(shards)

    def body(*refs):
        srcs, dsts = refs[:n], refs[n:2 * n]
        send_sems, recv_sems, local_sems = refs[2 * n:]
        _gather_copies_relayed(srcs, dsts, send_sems, recv_sems, local_sems, waves=waves)

    return pl.pallas_call(
        body, name=name,
        in_specs=[ANY] * n, out_specs=[ANY] * n,
        out_shape=[jax.ShapeDtypeStruct((N_DEV, *s.shape), s.dtype) for s in shards],
        scratch_shapes=_comm_sems(n, waves),
    )(*shards)


FLIPS = [(fx, fy, fc) for fx in (0, 1) for fy in (0, 1) for fc in (0, 1)][1:]


def _scatter_copies(srcs, dsts, send_sems, recv_sems, local_sems, waves=1):
    n = len(srcs)
    x, y, c = _position()
    me = 4 * x + 2 * y + c
    mine = [pltpu.make_async_copy(srcs[a].at[me], dsts[a].at[me], local_sems.at[a]) for a in range(n)]
    for cp in mine:
        cp.start()
    peers = []
    for fx, fy, fc in FLIPS:
        tx = 1 - x if fx else x
        ty = 1 - y if fy else y
        tc = 1 - c if fc else c
        peers.append(((tx, ty, tc), 4 * tx + 2 * ty + tc))
    copies = []
    for w in range(waves):
        wave = []
        for k, (peer_id, peer) in enumerate(peers):
            for a in range(n):
                rows = srcs[a].shape[1]
                step = rows if waves == 1 else (rows // waves) // 16 * 16
                r0 = w * step
                cut = pl.ds(r0, step if w < waves - 1 else rows - r0)
                sem = (7 * a + k) * waves + w
                cp = pltpu.make_async_remote_copy(
                    src_ref=srcs[a].at[peer, cut], dst_ref=dsts[a].at[me, cut],
                    send_sem=send_sems.at[sem], recv_sem=recv_sems.at[sem],
                    device_id=peer_id, device_id_type=MESH)
                cp.start()
                wave.append(cp)
        for cp in wave:
            cp.wait_send()
        copies += wave
    for cp in copies:
        cp.wait_recv()
    for cp in mine:
        cp.wait()


def _pair_copies(srcs, dsts, send_sems, recv_sems, local_sems):
    x, y, c = _position()
    copies = [pltpu.make_async_remote_copy(
        src_ref=srcs[a], dst_ref=dsts[a], send_sem=send_sems.at[a], recv_sem=recv_sems.at[a],
        device_id=(x, y, 1 - c), device_id_type=MESH) for a in range(len(srcs))]
    for cp in copies:
        cp.start()
    for cp in copies:
        cp.wait()


def _chip_scatter_copies(srcs, dsts, send_sems, recv_sems, local_sems, waves=1):
    n = len(srcs)
    x, y, c = _position()
    here = 2 * x + y
    mine = [pltpu.make_async_copy(srcs[a].at[here], dsts[a].at[here], local_sems.at[a]) for a in range(n)]
    for cp in mine:
        cp.start()
    peers = []
    for fx, fy in ((1, 0), (0, 1), (1, 1)):
        tx = 1 - x if fx else x
        ty = 1 - y if fy else y
        peers.append(((tx, ty, c), 2 * tx + ty))
    copies = []
    for w in range(waves):
        wave = []
        for k, (peer_id, peer) in enumerate(peers):
            for a in range(n):
                rows = srcs[a].shape[1]
                step = rows if waves == 1 else (rows // waves) // 16 * 16
                r0 = w * step
                cut = pl.ds(r0, step if w < waves - 1 else rows - r0)
                sem = (3 * a + k) * waves + w
                cp = pltpu.make_async_remote_copy(
                    src_ref=srcs[a].at[peer, cut], dst_ref=dsts[a].at[here, cut],
                    send_sem=send_sems.at[sem], recv_sem=recv_sems.at[sem],
                    device_id=peer_id, device_id_type=MESH)
                cp.start()
                wave.append(cp)
        for cp in wave:
            cp.wait_send()
        copies += wave
    for cp in copies:
        cp.wait_recv()
    for cp in mine:
        cp.wait()


def _comm_sems(n, waves=1):
    return [pltpu.SemaphoreType.DMA((7 * n * waves,)), pltpu.SemaphoreType.DMA((7 * n * waves,)),
            pltpu.SemaphoreType.DMA((n,))]


def _handshake(peers):
    barrier = pltpu.get_barrier_semaphore()
    for peer in peers:
        pl.semaphore_signal(barrier, inc=1, device_id=peer, device_id_type=MESH)
    pl.semaphore_wait(barrier, len(peers))


def _sequencer_call(arrays, out_types, copies_fn, peers_fn, *, name, collective_id, waves=1):
    n = len(arrays)
    srcs = [jax.new_ref(a, memory_space=pltpu.MemorySpace.HBM) for a in arrays]
    dsts = [jax.empty_ref(t, memory_space=pltpu.MemorySpace.HBM) for t in out_types]
    extra = {} if waves == 1 else {"waves": waves}

    @pl.kernel(mesh=plsc.ScalarSubcoreMesh(axis_name="sequencer", num_cores=1), name=name,
               scratch_types=_comm_sems(n, waves), compiler_params=pltpu.CompilerParams(collective_id=collective_id))
    def launch(send_sems, recv_sems, local_sems):
        _handshake(peers_fn())
        copies_fn(srcs, dsts, send_sems, recv_sems, local_sems, **extra)

    launch()
    return [d[...] for d in dsts]


def _all_other_devices():
    x, y, c = _position()
    return [(1 - x if fx else x, 1 - y if fy else y, 1 - c if fc else c) for fx, fy, fc in FLIPS]


def _gather_relay_peers():
    x, y, c = _position()
    return [(x, y, 1 - c), (1 - x, y, c), (x, 1 - y, c)]


def _gather_peers():
    x, y, c = _position()
    return [(x, y, 1 - c), (1 - x, y, c), (x, 1 - y, c), (1 - x, 1 - y, c)]


def _small_gather_async(shards, *, name, collective_id):
    return _sequencer_call(shards, [jax.ShapeDtypeStruct((N_DEV, *s.shape), s.dtype) for s in shards],
                           _gather_copies, _gather_peers, name=name, collective_id=collective_id)


def _sibling():
    x, y, c = _position()
    return [(x, y, 1 - c)]


def _same_core_of_other_chips():
    x, y, c = _position()
    return [(1 - x, y, c), (x, 1 - y, c), (1 - x, 1 - y, c)]


def _pair_exchange_async(parts, *, name, collective_id):
    return _sequencer_call(parts, [jax.ShapeDtypeStruct(p.shape, p.dtype) for p in parts],
                           _pair_copies, _sibling, name=name, collective_id=collective_id)


def _chip_scatter_async(parts, *, name, collective_id, waves=1):
    return _sequencer_call(parts, [jax.ShapeDtypeStruct(p.shape, p.dtype) for p in parts],
                           _chip_scatter_copies, _same_core_of_other_chips, name=name, collective_id=collective_id,
                           waves=waves)


def _pair_sum_call(mine, theirs, *, name, tw=D_MODEL):
    n, r, w = mine.shape

    def body(a_ref, b_ref, o_ref):
        o_ref[...] = (a_ref[...].astype(F32) + b_ref[...].astype(F32)).astype(o_ref.dtype)

    spec = pl.BlockSpec((None, r, tw), lambda i, j: (i, 0, j))
    return pl.pallas_call(
        body, name=name, grid=(n, w // tw), in_specs=[spec, spec], out_specs=spec,
        out_shape=pltpu.HBM(mine.shape, mine.dtype),
        compiler_params=_cp(("parallel", "parallel")),
    )(*_hbm(mine, theirs))


def _scatter_blocks_async(parts, *, name, collective_id, waves=1):
    return _sequencer_call(parts, [jax.ShapeDtypeStruct(p.shape, p.dtype) for p in parts],
                           _scatter_copies, _all_other_devices, name=name, collective_id=collective_id, waves=waves)


def _all_gather_async(shards, *, name, collective_id, waves=1):
    return _sequencer_call(shards, [jax.ShapeDtypeStruct((N_DEV, *s.shape), s.dtype) for s in shards],
                           _gather_copies_relayed, _gather_relay_peers, name=name, collective_id=collective_id,
                           waves=waves)


def _adamw_math(w, g, m, v):
    m = ADAM_B1 * m + (1.0 - ADAM_B1) * g
    v = ADAM_B2 * v + (1.0 - ADAM_B2) * (g * g)
    m_hat = m / (1.0 - ADAM_B1 ** ADAM_STEP)
    v_hat = v / (1.0 - ADAM_B2 ** ADAM_STEP)
    delta = -ADAM_LR * (m_hat / (jnp.sqrt(v_hat) + ADAM_EPS) + ADAM_WD * w)
    return delta, m, v


def _adamw_reduce_call(recv, w, m, v, *, name, T=128):
    R, W = w.shape
    n_parts = recv.shape[0]
    if R % T == 0:
        tr, tw = T, W
    elif (R // 2) % 16 == 0:
        tr, tw = R // 2, W
    else:
        tr, tw = R, 2 * LANES

    def body(p_ref, w_ref, m_ref, v_ref, g_out, d_out, m_out, v_out):
        g = p_ref[0].astype(F32)
        for d in range(1, n_parts):
            g = g + p_ref[d].astype(F32)
        g_out[...] = g
        d_out[...], m_out[...], v_out[...] = _adamw_math(w_ref[...], g, m_ref[...], v_ref[...])

    row = pl.BlockSpec((tr, tw), lambda i, j: (i, j))
    out = pltpu.HBM((R, W), F32)
    return pl.pallas_call(
        body, name=name, grid=(R // tr, W // tw),
        in_specs=[pl.BlockSpec((n_parts, tr, tw), lambda i, j: (0, i, j)), row, row, row],
        out_specs=[row] * 4, out_shape=[out] * 4,
        compiler_params=_cp(("parallel", "parallel"), VMEM_BIG),
    )(*_hbm(recv, w, m, v))


SMALL_EARLY = (("b_gate", 8), ("w_spatial", 512), ("b_spatial", 8), ("norm_post_mix", 8), ("norm_pre_ffn", 8),
               ("norm_post_ffn", 8), ("w_gate_up", 128), ("gla_norm", 64), ("sgu_ln_g", 64), ("sgu_ln_b", 64))
SMALL_EARLY_AT = {}
for _name, _rows in SMALL_EARLY:
    SMALL_EARLY_AT[_name] = sum(r for _, r in SMALL_EARLY[:len(SMALL_EARLY_AT)])
SMALL_EARLY_ROWS = sum(r for _, r in SMALL_EARLY)
SMALL_LATE_ROWS = 16


def _small_early_rows(grads):
    def rows(a, n_rows):
        a = a.reshape(-1, LANES)
        return jnp.pad(a, ((0, n_rows - a.shape[0]), (0, 0)))

    def device_major(a, n_rows):
        r, c = a.shape[0], a.shape[1] // N_DEV
        a = a.reshape(r, N_DEV, c).transpose(1, 0, 2)
        a = jnp.pad(a, ((0, 0), (0, n_rows // N_DEV - r), (0, LANES - c)))
        return a.reshape(n_rows, LANES)

    pieces = []
    for name, n_rows in SMALL_EARLY:
        g = grads[name]
        if name in SMALL_SHARDED:
            pieces.append(device_major(g.reshape(g.shape[0], -1) if g.ndim == 2 else g.reshape(g.shape[0], g.shape[-1]), n_rows))
        else:
            pieces.append(rows(g, n_rows))
    return jnp.concatenate(pieces, axis=0)


def _small_update_call(got_early, got_late, w, m, v, *, name):
    names = list(SMALL)
    n_p = len(names)
    E = SMALL_EARLY_ROWS

    def body(early_ref, late_ref, *rest):
        w_refs, m_refs, v_refs = [dict(zip(names, rest[i * n_p:(i + 1) * n_p])) for i in range(3)]
        outs, tot = rest[3 * n_p:-1], rest[-1]
        loss_out = outs[0]
        g_out, d_out, m_out, v_out = [dict(zip(names, outs[1 + i * n_p:1 + (i + 1) * n_p])) for i in range(4)]
        acc, acc_late = early_ref[0], late_ref[0]
        for d in range(1, N_DEV):
            acc, acc_late = acc + early_ref[d], acc_late + late_ref[d]
        tot[0:E, :] = acc
        tot[E:E + SMALL_LATE_ROWS, :] = acc_late
        loss_out[...] = tot[E + 8:E + 9, :]

        x, y, c = _position()
        me = 4 * x + 2 * y + c

        def update(name, g, ix):
            g_out[name][ix] = g
            d_out[name][ix], m_out[name][ix], v_out[name][ix] = _adamw_math(
                w_refs[name][ix], g, m_refs[name][ix], v_refs[name][ix])

        for name in names:
            shape = w[name].shape
            if name in SMALL_SHARDED:
                per_dev = dict(SMALL_EARLY)[name] // N_DEV
                at = pl.multiple_of(SMALL_EARLY_AT[name] + me * per_dev, 8)
                g = tot[pl.ds(at, per_dev), :]
                update(name, g[:shape[1], :shape[2]], (0,))
            elif name == "w_spatial":
                for grp in range(GROUPS):
                    at = SMALL_EARLY_AT[name] + grp * SBLOCK
                    update(name, tot[at:at + SBLOCK, :], (0, grp))
            elif name == "b_spatial":
                at = SMALL_EARLY_AT[name]
                update(name, tot[at:at + GROUPS, :], (0,))
            else:
                at = E if name == "norm_pre_mix" else SMALL_EARLY_AT[name]
                for k in range(shape[1] // LANES):
                    update(name, tot[at + k:at + k + 1, :], (slice(None), pl.ds(k * LANES, LANES)))

    state = [s[n] for s in (w, m, v) for n in names]
    out_shapes = [jax.ShapeDtypeStruct((1, LANES), F32)] + [jax.ShapeDtypeStruct(w[n].shape, F32) for n in names] * 4
    outs = pl.pallas_call(
        body, name=name,
        in_specs=[VMEM_SPEC] * (2 + len(state)), out_specs=[VMEM_SPEC] * len(out_shapes), out_shape=out_shapes,
        scratch_shapes=[pltpu.VMEM((E + SMALL_LATE_ROWS, LANES), F32)],
    )(got_early, got_late, *state)
    per_name = {n: tuple(outs[1 + i * n_p + j] for i in range(4)) for j, n in enumerate(names)}
    return outs[0], per_name


def _tile_rows(n_elems):
    return -(-n_elems // (8 * LANES)) * 8


def _pack_rows(parts, rows):
    pieces = []
    for p in parts:
        q = p.reshape(-1, LANES)
        pieces.append(jnp.pad(q, ((0, _tile_rows(p.size) - q.shape[0]), (0, 0))))
    buf = jnp.concatenate(pieces, axis=0)
    return jnp.pad(buf, ((0, rows - buf.shape[0]), (0, 0)))


def _in_w_blocks_call(a, live, finished, *, name, tk=TOKEN_TILE):
    S = a.shape[0]
    tk = min(tk, S)
    steps = S // tk
    n_live = len(live)
    n_chips = N_DEV // 2
    out_shape = (n_chips, IN_BLK, D_MODEL)

    def body(a_ref, *rest):
        live_refs, done_refs = rest[:n_live], rest[n_live:n_live + len(finished)]
        keep_ref, send_ref = rest[n_live + len(finished):n_live + len(finished) + 2]
        accs = rest[n_live + len(finished) + 2:]
        k = pl.program_id(0)

        @pl.when(k == 0)
        def _():
            for acc in accs:
                acc[...] = jnp.zeros_like(acc)

        av = a_ref[...]
        for src, acc in zip(live_refs, accs):
            for m0 in range(0, src.shape[1], 1024):
                m1 = min(src.shape[1], m0 + 1024)
                acc[m0:m1, :] += _dot(src[:, m0:m1], av, TN)

        @pl.when(k == steps - 1)
        def _():
            groups = [(acc, cols) for acc, (_, cols) in zip(accs, live)]
            groups += [(ref, cols) for ref, (_, cols) in zip(done_refs, finished)]
            core = lax.axis_index("c")

            def cut(d, out_ref):
                lo, hi = IN_BLK * d, IN_BLK * (d + 1)
                for ref, (c0, c1) in groups:
                    s0, e0 = max(lo, c0), min(hi, c1)
                    if s0 < e0:
                        out_ref[d // 2, s0 - lo:e0 - lo, :] = ref[s0 - c0:e0 - c0, :].astype(BF16)

            for d in range(N_DEV):
                pl.when(core == d % 2)(lambda d=d: cut(d, keep_ref))
                pl.when(core != d % 2)(lambda d=d: cut(d, send_ref))

    acc_shapes = [(arr.shape[1], D_MODEL) for arr, _ in live]
    tiles = [((tk, D_MODEL), BF16)] + [((tk, arr.shape[1]), BF16) for arr, _ in live]
    resident = ([(arr.shape, arr.dtype) for arr, _ in finished] + [(out_shape, BF16)] * 2
                + [(s, F32) for s in acc_shapes])
    return pl.pallas_call(
        body, name=name, grid=(steps,),
        in_specs=[pl.BlockSpec((tk, D_MODEL), lambda k: (k, 0))]
        + [pl.BlockSpec((tk, arr.shape[1]), lambda k: (k, 0)) for arr, _ in live]
        + [_res(arr.shape) for arr, _ in finished],
        out_specs=[_acc(out_shape)] * 2,
        out_shape=[pltpu.HBM(out_shape, BF16)] * 2,
        scratch_shapes=[pltpu.VMEM(s, F32) for s in acc_shapes],
        compiler_params=_cp(("arbitrary",), _vmem_limit(tiles, resident)),
    )(*_hbm(a, *[arr for arr, _ in live], *[arr for arr, _ in finished]))


def _local_step(x, target, W, scatter, finish, pair, chip_scatter, gather_early):
    g1, g2, g3, g4 = [W[n].reshape(1, D_MODEL) for n in ("norm_pre_mix", "norm_post_mix", "norm_pre_ffn", "norm_post_ffn")]
    wfi, wfo = W["w_ffn_in"].reshape(2 * D_FF, D_MODEL), W["w_ffn_out"]
    wg = jnp.pad(W["w_gate_up"], ((0, LANES - RANK), (0, 0))).astype(BF16)
    bgate = W["b_gate"].reshape(1, QK)
    gnorm = W["gla_norm"].reshape(HEADS, 1, DV)
    ln_g = W["sgu_ln_g"].reshape(GROUPS, 1, DG)
    ln_b = W["sgu_ln_b"].reshape(GROUPS, 1, DG)
    w_sp = W["w_spatial"]
    b_sp = W["b_spatial"].reshape(GROUPS, SBLOCK, 1)

    a, pa, alow, ps, pg, w_in_t = _in_proj_call(x, g1, W["w_in_blocks"], name="in_proj")
    y_gla, states = _gla_fwd_call(pa, alow, wg, bgate, gnorm, name="gla_fwd")
    y_sgu = _sgu_fwd_call(ps, ln_g, ln_b, w_sp, b_sp, name="sgu_fwd")
    t1, t2, merged, mix, x1, h = _mixer_tail_call(y_gla, y_sgu, pg, x, W["w_branch_gla"], W["w_branch_sgu"],
                                                  W["w_out"], g2, g3, name="mixer_tail")
    gu, z = _ffn_in_call(h, wfi, name="ffn_in")
    loss, dx2, dy, dg4 = _ffn_out_loss_call(z, wfo, x1, target, g4, name="ffn_out_loss")

    grads = {"norm_post_ffn": dg4}
    done = {}
    dgu = _ffn_out_bwd_call(dy, wfo, gu, name="ffn_out_bwd")
    dw_ffn_out = _weight_grads_call([z, dy], [(0, 1)], name="d_ffn_out_w")[0].reshape(N_DEV, D_FF // N_DEV, D_MODEL)
    dw_ffn_in = _ffn_in_grad_call(h, dgu, name="d_ffn_in_w").reshape(N_DEV, FF_BLK, D_MODEL)
    dgu, dw_ffn_out, dw_ffn_in = lax.optimization_barrier((dgu, dw_ffn_out, dw_ffn_in))
    ffn_received = scatter(("w_ffn_out", "w_ffn_in"), [dw_ffn_out, dw_ffn_in])
    dx1, dmix, grads["norm_pre_ffn"], grads["norm_post_mix"] = _ffn_in_bwd_call(dgu, wfi, dx2, x1, mix, g3, g2, name="ffn_in_bwd")
    dt1, dt2, dpg, dy_gla, dy_sgu = _mixer_bwd_call(dmix, t1, t2, pg, W["w_out"], W["w_branch_gla"], W["w_branch_sgu"],
                                                    name="mixer_bwd")
    rows = D_MODEL // N_DEV
    mixer_grads = _weight_grads_call([merged, dmix, y_gla, dt1, y_sgu, dt2], [(0, 1), (2, 3), (4, 5)], name="d_mixer_w")
    dy_gla, dy_sgu, mixer_grads = lax.optimization_barrier((dy_gla, dy_sgu, mixer_grads))
    mixer_received = scatter(("w_out", "w_branch_gla", "w_branch_sgu"),
                             [g.reshape(N_DEV, rows, D_MODEL) for g in mixer_grads])
    dps, dlg, dlb, dwsp, dbsp = _sgu_bwd_call(ps, ln_g, ln_b, w_sp, b_sp, dy_sgu, name="sgu_bwd")
    dw_s, dw_g = _weight_grads_call([a, dps, dpg], [(1, 0), (2, 0)], name="d_in_w_sgu_gates")
    dy_gla, dw_s, dw_g = lax.optimization_barrier((dy_gla, dw_s, dw_g))
    dpa, dlow, dwg, dgn, dbg = _gla_bwd_call(pa, alow, wg, bgate, gnorm, states, dy_gla, name="gla_bwd")
    ffn_received, mixer_received, dpa, dlow = lax.optimization_barrier((ffn_received, mixer_received, dpa, dlow))
    keep, send = _in_w_blocks_call(a, [(dpa, A_COLS), (dlow, LOW_COLS)], [(dw_s, S_COLS), (dw_g, G_COLS)],
                                   name="d_in_w_blocks")
    ffn_received, mixer_received, send = lax.optimization_barrier((ffn_received, mixer_received, send))
    from_sibling = pair(send)
    done.update({**finish(ffn_received), **finish(mixer_received)})
    done, keep = lax.optimization_barrier((done, keep))
    chip_sum = _pair_sum_call(keep, from_sibling, name="pair_sum_w_in")
    grads["w_gate_up"] = dwg[:RANK]
    grads["b_gate"] = dbg
    grads["gla_norm"] = dgn
    grads["sgu_ln_g"] = dlg
    grads["sgu_ln_b"] = dlb
    grads["w_spatial"] = dwsp
    grads["b_spatial"] = dbsp
    early_rows = _small_early_rows(grads)
    chip_sum, dpa, early_rows = lax.optimization_barrier((chip_sum, dpa, early_rows))
    in_received = chip_scatter(chip_sum)
    early = gather_early(early_rows)
    grad_x, grads["norm_pre_mix"] = _in_proj_bwd_call(dpa, dlow, dps, dpg, w_in_t, x, dx1, g1, name="in_proj_bwd")
    done.update(finish({"w_in": in_received}))
    return loss, grad_x, grads, done, early


WEIGHTS = ("norm_pre_mix", "w_in", "w_gate_up", "b_gate", "gla_norm", "sgu_ln_g", "sgu_ln_b", "w_spatial",
           "b_spatial", "w_branch_gla", "w_branch_sgu", "w_out", "norm_post_mix", "norm_pre_ffn", "w_ffn_in",
           "w_ffn_out", "norm_post_ffn")
BIG = ("w_in", "w_branch_gla", "w_branch_sgu", "w_out", "w_ffn_in", "w_ffn_out")
MIXER = ("w_branch_gla", "w_branch_sgu", "w_out")
FFN = ("w_ffn_in", "w_ffn_out")
COLUMN_SHARDED = ("w_in", "w_ffn_in")
LATE_SCATTER_WAVES = 4
GATHER_WAVES = 4
SMALL = tuple(n for n in WEIGHTS if n not in BIG)
SMALL_SHARDED = ("w_gate_up", "gla_norm", "sgu_ln_g", "sgu_ln_b")
SMALL_GATHER_ROWS = 32


def kernel(x, norm_pre_mix, w_in, w_gate_up, b_gate, gla_norm, sgu_ln_g, sgu_ln_b, w_spatial, b_spatial, w_branch_gla, w_branch_sgu, w_out, norm_post_mix, norm_pre_ffn, w_ffn_in, w_ffn_out, norm_post_ffn, loss_target, m_norm_pre_mix, m_w_in, m_w_gate_up, m_b_gate, m_gla_norm, m_sgu_ln_g, m_sgu_ln_b, m_w_spatial, m_b_spatial, m_w_branch_gla, m_w_branch_sgu, m_w_out, m_norm_post_mix, m_norm_pre_ffn, m_w_ffn_in, m_w_ffn_out, m_norm_post_ffn, v_norm_pre_mix, v_w_in, v_w_gate_up, v_b_gate, v_gla_norm, v_sgu_ln_g, v_sgu_ln_b, v_w_spatial, v_b_spatial, v_w_branch_gla, v_w_branch_sgu, v_w_out, v_norm_post_mix, v_norm_pre_ffn, v_w_ffn_in, v_w_ffn_out, v_norm_post_ffn):
    given = dict(locals())
    def local(a, n):
        return a[0].T if n in COLUMN_SHARDED else a[0]

    w = {n: local(given[n], n) for n in WEIGHTS}
    m = {n: local(given["m_" + n], n) for n in WEIGHTS}
    v = {n: local(given["v_" + n], n) for n in WEIGHTS}
    xs, target = x[0], loss_target[0]

    small_shard = _pack_rows([w[n] for n in SMALL_SHARDED], SMALL_GATHER_ROWS)
    first = _all_gather_async([w["w_in"].astype(BF16), small_shard], name="gather_w_in", collective_id=2,
                              waves=GATHER_WAVES)
    rest, first, w["w_in"], m["w_in"], v["w_in"] = lax.optimization_barrier(
        ([w[n].astype(BF16) for n in MIXER + FFN], first, w["w_in"], m["w_in"], v["w_in"]))
    rest_blocks = _all_gather_async(rest, name="gather_rest", collective_id=1)
    W = {n: w[n] for n in SMALL if n not in SMALL_SHARDED}
    blocks = dict(zip(MIXER + FFN, rest_blocks))
    for n in ("w_branch_gla", "w_branch_sgu", "w_out", "w_ffn_out"):
        W[n] = blocks[n].reshape(-1, D_MODEL)
    W["w_ffn_in"] = blocks["w_ffn_in"]
    W["w_in_blocks"] = first[0]
    small_blocks = first[1]
    off = 0
    for n in SMALL_SHARDED:
        r, c = w[n].shape
        blk = small_blocks[:, off:off + r * c // LANES].reshape(N_DEV, r, c)
        W[n] = blk.transpose(1, 0, 2).reshape(r, N_DEV * c)
        off += _tile_rows(r * c)

    scatter_ids = iter((3, 4))

    def scatter(names, parts):
        got = _scatter_blocks_async(parts, name="scatter_" + "_".join(names), collective_id=next(scatter_ids))
        return dict(zip(names, got))

    def finish(received):
        return {n: _adamw_reduce_call(r, w[n], m[n], v[n], name="adamw_" + n) for n, r in received.items()}

    def pair(part):
        return _pair_exchange_async([part], name="pair_w_in", collective_id=5)[0]

    def chip_scatter(part):
        return _chip_scatter_async([part], name="scatter_w_in", collective_id=6, waves=LATE_SCATTER_WAVES)[0]

    def gather_early(rows):
        return _small_gather_async([rows], name="gather_small", collective_id=7)[0]

    loss_part, grad_x, grads, big_done, got_early = _local_step(xs, target, W, scatter, finish, pair, chip_scatter,
                                                                gather_early)

    late = jnp.concatenate([grads["norm_pre_mix"].reshape(8, LANES), jnp.broadcast_to(loss_part, (8, LANES))], axis=0)
    got_late = _small_gather_async([late], name="gather_small_late", collective_id=8)[0]
    got_late, big_done["w_in"] = lax.optimization_barrier((got_late, big_done["w_in"]))
    loss_row, small_done = _small_update_call(
        got_early, got_late, *[{n: given[prefix + n] for n in SMALL} for prefix in ("", "m_", "v_")],
        name="small_update")

    def pick(i):
        return [small_done[n][i] if n in SMALL else (big_done[n][i].T if n in COLUMN_SHARDED else big_done[n][i])[None]
                for n in WEIGHTS]

    return (loss_row[0, 0], grad_x[None], *pick(0), *pick(1), *pick(2), *pick(3))
```

```python
import jax
import jax.numpy as jnp
from jax import lax
from jax.experimental import pallas as pl
from jax.experimental.pallas import tpu as pltpu
from jax.experimental.pallas import tpu_sc as plsc

F32 = jnp.float32
BF16 = jnp.bfloat16

D_MODEL = 1024
N_DEV = 8
CHUNK = 64
HEADS = 4
DK = 128
DV = 256
QK = HEADS * DK
GV = HEADS * DV
RANK = 16
GROUPS = 4
SBLOCK = 128
DG = 256
D_FF = 2816
FF_BLK = 704
EPS = 1e-6
Q_SCALE = DK ** -0.5
LANES = 128
VMEM_BIG = 48 * 1024 * 1024

D_IN = 7184
IN_BLK = 898
A_COLS = (0, 3072)
LOW_COLS = (3072, 3088)
S_COLS = (3088, 5136)
G_COLS = (5136, 7184)

ADAM_LR = 0.001
ADAM_B1 = 0.9
ADAM_B2 = 0.999
ADAM_EPS = 1e-08
ADAM_WD = 0.01
ADAM_STEP = 10

MESH = pl.DeviceIdType.MESH
ANY = pl.BlockSpec(memory_space=pl.ANY)
VMEM_SPEC = pl.BlockSpec(memory_space=pltpu.VMEM)


def _cp(sem=None, vmem=None):
    return pltpu.CompilerParams(dimension_semantics=sem, vmem_limit_bytes=vmem)


def _hbm(*arrays):
    return [pltpu.with_memory_space_constraint(a, pltpu.HBM) for a in arrays]


def _sig(x):
    return 0.5 * jnp.tanh(0.5 * x) + 0.5


GELU_C = 0.7978845608028654
GELU_A = 0.044715


def _gelu(x):
    t = jnp.tanh((GELU_C * x) * (1.0 + GELU_A * (x * x)))
    return (0.5 * x) * (1.0 + t)


def _gelu_and_grad(x):
    x2 = x * x
    t = jnp.tanh((GELU_C * x) * (1.0 + GELU_A * x2))
    one_t = 1.0 + t
    hx = 0.5 * x
    grad = 0.5 * one_t + (hx * (1.0 - t * t)) * (GELU_C + (3.0 * GELU_A * GELU_C) * x2)
    return hx * one_t, grad


def _logsig(x):
    return jnp.minimum(x, 0.0) - jnp.log1p(jnp.exp(-jnp.abs(x)))


def _dot(a, b, dims):
    return lax.dot_general(a, b, (dims, ((), ())), preferred_element_type=F32)


NN = ((1,), (0,))
NT = ((1,), (1,))
TN = ((0,), (0,))


def _exact_mask_dot(mask_bf16, x):
    hi = x.astype(BF16)
    r1 = x - hi.astype(F32)
    mid = r1.astype(BF16)
    lo = (r1 - mid.astype(F32)).astype(BF16)
    return _dot(mask_bf16, hi, NN) + _dot(mask_bf16, mid, NN) + _dot(mask_bf16, lo, NN)


ROW_TILE = 512
SUB_ROWS = 256


def _sub_tiles(T):
    return [pl.ds(r0, min(SUB_ROWS, T)) for r0 in range(0, T, SUB_ROWS)]


def _rt(T, W, c=0):
    return pl.BlockSpec((T, W), lambda i: (i, c))


def _rt3(nb, T, W):
    return pl.BlockSpec((nb, T, W), lambda i: (0, i, 0))


def _res(shape):
    nd = len(shape)
    return pl.BlockSpec(tuple(shape), lambda i: (0,) * nd, pipeline_mode=pl.Buffered(1))


def _acc(shape):
    nd = len(shape)
    return pl.BlockSpec(tuple(shape), lambda i: (0,) * nd, pipeline_mode=pl.Buffered(1))


def _nbytes(shape, dtype):
    n = jnp.dtype(dtype).itemsize
    for s in shape:
        n *= s
    return n


def _vmem_limit(tiles, resident, temps=16 * 1024 * 1024):
    need = 2 * sum(_nbytes(s, d) for s, d in tiles) + sum(_nbytes(s, d) for s, d in resident) + temps
    return min(need, 60 * 1024 * 1024)


def _tok_call(body, *, name, S, T, ins, outs, semantics="parallel"):
    tiles = [(spec.block_shape, a.dtype) for a, spec, kind in ins + outs if kind == "tile"]
    resident = [(a.shape, a.dtype) for a, spec, kind in ins + outs if kind == "res"]
    return pl.pallas_call(
        body, name=name, grid=(S // T,),
        in_specs=[spec for _, spec, _ in ins], out_specs=[spec for _, spec, _ in outs],
        out_shape=[pltpu.HBM(a.shape, a.dtype) for a, _, _ in outs],
        compiler_params=_cp((semantics,), _vmem_limit(tiles, resident)),
    )(*_hbm(*[a for a, _, _ in ins]))


def _tile(a, spec):
    return (a, spec, "tile")


def _whole(a):
    return (a, _res(a.shape), "res")


def _out_tile(shape, dtype, spec):
    return (jax.ShapeDtypeStruct(shape, dtype), spec, "tile")


def _out_acc(shape, dtype=F32):
    return (jax.ShapeDtypeStruct(shape, dtype), _acc(shape), "res")


def _rms_stats(x):
    r = lax.rsqrt(jnp.mean(x * x, axis=-1, keepdims=True) + EPS)
    return r, x * r


def _rms_bwd(xh, r, g, dy):
    dxh = dy * g
    dx = r * (dxh - xh * jnp.mean(dxh * xh, axis=-1, keepdims=True))
    dg = jnp.sum(dy * xh, axis=0, keepdims=True)
    return dx, dg


def _accum(ref, val):
    @pl.when(pl.program_id(0) == 0)
    def _():
        ref[...] = val

    @pl.when(pl.program_id(0) > 0)
    def _():
        ref[...] += val


def _dot_rows_t(a, w_ref, row0, o_ref, chunk=1024):
    n = o_ref.shape[1]
    for n0 in range(0, n, chunk):
        n1 = min(n, n0 + chunk)
        o_ref[:, n0:n1] = _dot(a, w_ref[row0 + n0:row0 + n1, :], NT).astype(o_ref.dtype)


def _in_proj_call(x, g1, w_blocks, *, name, T=ROW_TILE):
    S = x.shape[0]

    def body(x_ref, g_ref, blk_ref, a_ref, pa_ref, al_ref, ps_ref, pg_ref, w_ref):
        @pl.when(pl.program_id(0) == 0)
        def _():
            for d in range(N_DEV):
                w_ref[d * IN_BLK:(d + 1) * IN_BLK, :] = blk_ref[d]

        _, xh = _rms_stats(x_ref[...])
        a = (xh * g_ref[...]).astype(BF16)
        a_ref[...] = a
        _dot_rows_t(a, w_ref, A_COLS[0], pa_ref)
        _dot_rows_t(a, w_ref, LOW_COLS[0], al_ref)
        _dot_rows_t(a, w_ref, S_COLS[0], ps_ref)
        _dot_rows_t(a, w_ref, G_COLS[0], pg_ref)

    widths = (D_MODEL, A_COLS[1] - A_COLS[0], LANES, S_COLS[1] - S_COLS[0], G_COLS[1] - G_COLS[0])
    return _tok_call(
        body, name=name, S=S, T=T, semantics="arbitrary",
        ins=[_tile(x, _rt(T, D_MODEL)), _whole(g1), _whole(w_blocks)],
        outs=[_out_tile((S, w), BF16, _rt(T, w)) for w in widths] + [_out_acc((D_IN, D_MODEL), BF16)])


def _mixer_tail_call(y_gla, y_sgu, pg, x, w_bg, w_bs, w_out, g2, g3, *, name, T=ROW_TILE):
    S = x.shape[0]

    def body(yg_ref, ys_ref, pg_ref, x_ref, wbg_ref, wbs_ref, wo_ref, g2_ref, g3_ref,
             t1_ref, t2_ref, mg_ref, mix_ref, x1_ref, h_ref):
        for rows in _sub_tiles(T):
            t1 = _dot(yg_ref[rows, :], wbg_ref[...], NN)
            t2 = _dot(ys_ref[rows, :], wbs_ref[...], NN)
            t1_ref[rows, :] = t1.astype(BF16)
            t2_ref[rows, :] = t2.astype(BF16)
            sg = _sig(pg_ref[rows, pl.ds(0, D_MODEL)].astype(F32))
            ss = _sig(pg_ref[rows, pl.ds(D_MODEL, D_MODEL)].astype(F32))
            merged = (sg * t1 + ss * t2).astype(BF16)
            mg_ref[rows, :] = merged
            mix = _dot(merged, wo_ref[...], NN)
            mix_ref[rows, :] = mix
            _, mh = _rms_stats(mix)
            x1 = x_ref[rows, :] + mh * g2_ref[...]
            x1_ref[rows, :] = x1
            _, xh = _rms_stats(x1)
            h_ref[rows, :] = (xh * g3_ref[...]).astype(BF16)

    row = _rt(T, D_MODEL)
    b16 = lambda: _out_tile((S, D_MODEL), BF16, row)
    f32 = lambda: _out_tile((S, D_MODEL), F32, row)
    return _tok_call(
        body, name=name, S=S, T=T,
        ins=[_tile(y_gla, row), _tile(y_sgu, row), _tile(pg, _rt(T, 2 * D_MODEL)), _tile(x, row),
             _whole(w_bg), _whole(w_bs), _whole(w_out), _whole(g2), _whole(g3)],
        outs=[b16(), b16(), b16(), f32(), f32(), b16()])


FF_CHUNKS = ((0, 1024), (1024, 2048), (2048, D_FF))


def _ffn_in_call(h, wfi, *, name, T=ROW_TILE):
    S = h.shape[0]

    def body(h_ref, w_ref, gu_ref, z_ref):
        hv = h_ref[...]
        for c0, c1 in FF_CHUNKS:
            gate = _dot(hv, w_ref[c0:c1, :], NT)
            up = _dot(hv, w_ref[D_FF + c0:D_FF + c1, :], NT)
            gu_ref[:, c0:c1] = gate.astype(BF16)
            gu_ref[:, D_FF + c0:D_FF + c1] = up.astype(BF16)
            z_ref[:, c0:c1] = (gate * _sig(gate) * up).astype(BF16)

    return _tok_call(
        body, name=name, S=S, T=T,
        ins=[_tile(h, _rt(T, D_MODEL)), _whole(wfi)],
        outs=[_out_tile((S, 2 * D_FF), BF16, _rt(T, 2 * D_FF)),
              _out_tile((S, D_FF), BF16, _rt(T, D_FF))])


def _ffn_out_loss_call(z, wfo, x1, target, g4, *, name, T=ROW_TILE):
    S = x1.shape[0]

    def body(z_ref, w_ref, x1_ref, t_ref, g4_ref, loss_ref, dx2_ref, dy_ref, dg4_ref):
        loss = jnp.zeros((1, 1), F32)
        dg4 = jnp.zeros((1, D_MODEL), F32)
        for rows in _sub_tiles(T):
            y = _dot(z_ref[rows, :], w_ref[...], NN)
            r, yh = _rms_stats(y)
            diff = x1_ref[rows, :] + yh * g4_ref[...] - t_ref[rows, :]
            loss = loss + 0.5 * jnp.sum(jnp.mean(diff * diff, axis=-1, keepdims=True), axis=0, keepdims=True)
            dx2 = diff * (1.0 / D_MODEL)
            dx2_ref[rows, :] = dx2
            dy, dg = _rms_bwd(yh, r, g4_ref[...], dx2)
            dy_ref[rows, :] = dy.astype(BF16)
            dg4 = dg4 + dg
        _accum(loss_ref, jnp.broadcast_to(loss, (1, LANES)))
        _accum(dg4_ref, dg4)

    row = _rt(T, D_MODEL)
    return _tok_call(
        body, name=name, S=S, T=T, semantics="arbitrary",
        ins=[_tile(z, _rt(T, D_FF)), _whole(wfo), _tile(x1, row), _tile(target, row), _whole(g4)],
        outs=[_out_acc((1, LANES)), _out_tile((S, D_MODEL), F32, row), _out_tile((S, D_MODEL), BF16, row),
              _out_acc((1, D_MODEL))])


def _ffn_out_bwd_call(dy, wfo, gu, *, name, T=ROW_TILE):
    S = dy.shape[0]

    def body(dy_ref, w_ref, gu_ref, dgu_ref):
        dyv = dy_ref[...]
        for c0, c1 in FF_CHUNKS:
            dz = _dot(dyv, w_ref[c0:c1, :], NT).astype(BF16)
            gt = gu_ref[:, c0:c1]
            up = gu_ref[:, D_FF + c0:D_FF + c1]
            s = _sig(gt)
            dgu_ref[:, c0:c1] = dz * up * s * (1.0 + gt * (1.0 - s))
            dgu_ref[:, D_FF + c0:D_FF + c1] = dz * gt * s

    wide = _rt(T, 2 * D_FF)
    return _tok_call(
        body, name=name, S=S, T=T,
        ins=[_tile(dy, _rt(T, D_MODEL)), _whole(wfo), _tile(gu, wide)],
        outs=[_out_tile((S, 2 * D_FF), BF16, wide)])[0]


def _ffn_in_bwd_call(dgu, wfi, dx2, x1, mix, g3, g2, *, name, T=ROW_TILE):
    S = x1.shape[0]

    def body(dgu_ref, w_ref, dx2_ref, x1_ref, mix_ref, g3_ref, g2_ref, dx1_ref, dmix_ref, dg3_ref, dg2_ref):
        dg3 = jnp.zeros((1, D_MODEL), F32)
        dg2 = jnp.zeros((1, D_MODEL), F32)
        for rows in _sub_tiles(T):
            dh = _dot(dgu_ref[rows, :], w_ref[...], NN)
            r3, xh = _rms_stats(x1_ref[rows, :])
            d3, g = _rms_bwd(xh, r3, g3_ref[...], dh)
            dg3 = dg3 + g
            dx1 = dx2_ref[rows, :] + d3
            dx1_ref[rows, :] = dx1
            r2, mh = _rms_stats(mix_ref[rows, :])
            dmix, g = _rms_bwd(mh, r2, g2_ref[...], dx1)
            dg2 = dg2 + g
            dmix_ref[rows, :] = dmix.astype(BF16)
        _accum(dg3_ref, dg3)
        _accum(dg2_ref, dg2)

    row = _rt(T, D_MODEL)
    return _tok_call(
        body, name=name, S=S, T=T, semantics="arbitrary",
        ins=[_tile(dgu, _rt(T, 2 * D_FF)), _whole(wfi), _tile(dx2, row), _tile(x1, row), _tile(mix, row),
             _whole(g3), _whole(g2)],
        outs=[_out_tile((S, D_MODEL), F32, row), _out_tile((S, D_MODEL), BF16, row),
              _out_acc((1, D_MODEL)), _out_acc((1, D_MODEL))])


def _mixer_bwd_call(dmix, t1, t2, pg, w_out, w_bg, w_bs, *, name, T=ROW_TILE):
    S = dmix.shape[0]

    def body(dmix_ref, t1_ref, t2_ref, pg_ref, wo_ref, wbg_ref, wbs_ref, dt1_ref, dt2_ref, dpg_ref, dyg_ref, dys_ref):
        gg, gs = pl.ds(0, D_MODEL), pl.ds(D_MODEL, D_MODEL)
        for rows in _sub_tiles(T):
            dm = _dot(dmix_ref[rows, :], wo_ref[...], NT)
            sg = _sig(pg_ref[rows, gg].astype(F32))
            ss = _sig(pg_ref[rows, gs].astype(F32))
            dt1 = (dm * sg).astype(BF16)
            dt2 = (dm * ss).astype(BF16)
            dt1_ref[rows, :] = dt1
            dt2_ref[rows, :] = dt2
            dpg_ref[rows, gg] = (dm * t1_ref[rows, :].astype(F32) * sg * (1.0 - sg)).astype(BF16)
            dpg_ref[rows, gs] = (dm * t2_ref[rows, :].astype(F32) * ss * (1.0 - ss)).astype(BF16)
            dyg_ref[rows, :] = _dot(dt1, wbg_ref[...], NT).astype(BF16)
            dys_ref[rows, :] = _dot(dt2, wbs_ref[...], NT).astype(BF16)

    row = _rt(T, D_MODEL)
    wide = _rt(T, 2 * D_MODEL)
    b16 = lambda: _out_tile((S, D_MODEL), BF16, row)
    return _tok_call(
        body, name=name, S=S, T=T,
        ins=[_tile(dmix, row), _tile(t1, row), _tile(t2, row), _tile(pg, wide), _whole(w_out), _whole(w_bg), _whole(w_bs)],
        outs=[b16(), b16(), _out_tile((S, 2 * D_MODEL), BF16, wide), b16(), b16()])


def _in_proj_bwd_call(dpa, dlow, dps, dpg, w_in_t, x, dx1, g1, *, name, T=ROW_TILE):
    S = x.shape[0]

    def body(dpa_ref, dl_ref, dps_ref, dpg_ref, w_ref, x_ref, dx1_ref, g1_ref, gx_ref, dg1_ref):
        da = (_dot(dpa_ref[...], w_ref[A_COLS[0]:A_COLS[1], :], NN)
              + _dot(dl_ref[...], w_ref[LOW_COLS[0]:LOW_COLS[0] + LANES, :], NN)
              + _dot(dps_ref[...], w_ref[S_COLS[0]:S_COLS[1], :], NN)
              + _dot(dpg_ref[...], w_ref[G_COLS[0]:G_COLS[1], :], NN))
        r, xh = _rms_stats(x_ref[...])
        dxa, dg = _rms_bwd(xh, r, g1_ref[...], da)
        gx_ref[...] = dx1_ref[...] + dxa
        _accum(dg1_ref, dg)

    row = _rt(T, D_MODEL)
    return _tok_call(
        body, name=name, S=S, T=T, semantics="arbitrary",
        ins=[_tile(dpa, _rt(T, dpa.shape[1])), _tile(dlow, _rt(T, dlow.shape[1])), _tile(dps, _rt(T, dps.shape[1])),
             _tile(dpg, _rt(T, dpg.shape[1])), _whole(w_in_t), _tile(x, row), _tile(dx1, row), _whole(g1)],
        outs=[_out_tile((S, D_MODEL), F32, row), _out_acc((1, D_MODEL))])


TOKEN_TILE = 512


def _weight_grads_part(arrays, pairs, *, tk=TOKEN_TILE, out_dtype=BF16):
    S = arrays[0].shape[-2]
    tk = min(tk, S)
    n_in = len(arrays)

    def out_shape(i, j):
        a, b = arrays[i], arrays[j]
        if a.ndim == 3:
            return (a.shape[0], a.shape[2], b.shape[1])
        if b.ndim == 3:
            return (b.shape[0], a.shape[1], b.shape[2])
        return (a.shape[1], b.shape[1])

    shapes = [out_shape(i, j) for i, j in pairs]

    in_place = out_dtype == F32

    def body(ins, outs, accs):
        k = pl.program_id(0)
        if in_place:
            accs = outs

        @pl.when(k == 0)
        def _():
            for acc in accs:
                acc[...] = jnp.zeros_like(acc)

        for (i, j), acc in zip(pairs, accs):
            a_ref, b_ref = ins[i], ins[j]
            if len(a_ref.shape) == 3:
                for n in range(a_ref.shape[0]):
                    acc[n] += _dot(a_ref[n], b_ref[...], TN)
            elif len(b_ref.shape) == 3:
                a = a_ref[...]
                for n in range(b_ref.shape[0]):
                    acc[n] += _dot(a, b_ref[n], TN)
            else:
                b = b_ref[...]
                for m0 in range(0, a_ref.shape[1], 1024):
                    m1 = min(a_ref.shape[1], m0 + 1024)
                    acc[m0:m1, :] += _dot(a_ref[:, m0:m1], b, TN)

        if not in_place:
            @pl.when(k == S // tk - 1)
            def _():
                for out, acc in zip(outs, accs):
                    out[...] = acc[...].astype(out.dtype)

    def in_spec(a):
        if a.ndim == 3:
            return pl.BlockSpec((a.shape[0], tk, a.shape[2]), lambda k: (0, k, 0))
        return pl.BlockSpec((tk, a.shape[1]), lambda k: (k, 0))

    return dict(
        body=body, steps=S // tk, arrays=list(arrays),
        in_specs=[in_spec(a) for a in arrays],
        out_specs=[_acc(s) for s in shapes],
        out_shapes=[pltpu.HBM(s, out_dtype) for s in shapes],
        scratch=[] if in_place else [pltpu.VMEM(s, F32) for s in shapes],
        tiles=[(in_spec(a).block_shape, a.dtype) for a in arrays],
        resident=[(s, F32) for s in shapes] + ([] if in_place else [(s, BF16) for s in shapes]))


def _weight_grads_call(arrays, pairs, *, name, tk=TOKEN_TILE):
    part = _weight_grads_part(arrays, pairs, tk=tk)
    n_in, n_out = len(part["arrays"]), len(part["out_shapes"])

    def body(*refs):
        part["body"](refs[:n_in], refs[n_in:n_in + n_out], refs[n_in + n_out:])

    return pl.pallas_call(
        body, name=name, grid=(part["steps"],),
        in_specs=part["in_specs"], out_specs=part["out_specs"], out_shape=part["out_shapes"],
        scratch_shapes=part["scratch"],
        compiler_params=_cp(("arbitrary",), _vmem_limit(part["tiles"], part["resident"])),
    )(*_hbm(*part["arrays"]))


def _ffn_in_grad_call(h, dgu, *, name, tk=TOKEN_TILE):
    S = h.shape[0]
    tk = min(tk, S)
    shape = (D_FF, D_MODEL)

    def body(h_ref, dgu_ref, out_ref, acc):
        k = pl.program_id(1)

        @pl.when(k == 0)
        def _():
            acc[...] = jnp.zeros_like(acc)

        hv = h_ref[...]
        for c0, c1 in FF_CHUNKS:
            acc[c0:c1, :] += _dot(dgu_ref[:, c0:c1], hv, TN)

        @pl.when(k == S // tk - 1)
        def _():
            out_ref[...] = acc[...].astype(out_ref.dtype)

    tiles = [((tk, D_MODEL), BF16), ((tk, D_FF), BF16), (shape, BF16)]
    return pl.pallas_call(
        body, name=name, grid=(2, S // tk),
        in_specs=[pl.BlockSpec((tk, D_MODEL), lambda g, k: (k, 0)),
                  pl.BlockSpec((tk, D_FF), lambda g, k: (k, g))],
        out_specs=pl.BlockSpec(shape, lambda g, k: (g, 0)),
        out_shape=pltpu.HBM((2 * D_FF, D_MODEL), BF16),
        scratch_shapes=[pltpu.VMEM(shape, F32)],
        compiler_params=_cp(("parallel", "arbitrary"), _vmem_limit(tiles, [(shape, F32)])),
    )(*_hbm(h, dgu))


GLA_TILE = 256
Q_OFF, K_OFF, V_OFF, R_OFF = 0, QK, 2 * QK, 2 * QK + GV


def _tri(lower):
    r = lax.broadcasted_iota(jnp.int32, (CHUNK, CHUNK), 0)
    c = lax.broadcasted_iota(jnp.int32, (CHUNK, CHUNK), 1)
    return jnp.where((r >= c) if lower else (c >= r), 1.0, 0.0).astype(BF16)


def _gla_fwd_call(pa, alow, wg, bgate, gnorm, *, name):
    S = pa.shape[0]
    Tg = min(GLA_TILE, S)
    cb = Tg // CHUNK

    def body(pa_ref, al_ref, wg_ref, bg_ref, gn_ref, y_ref, st_ref, state):
        @pl.when(pl.program_id(0) == 0)
        def _():
            state[...] = jnp.zeros_like(state)

        logit = _dot(al_ref[...], wg_ref[...], NN) + bg_ref[...]
        ls = _logsig(logit) * (1.0 / 16.0)
        tri = _tri(True)
        for c in range(cb):
            rows = pl.ds(c * CHUNK, CHUNK)
            cum = _exact_mask_dot(tri, ls[c * CHUNK:(c + 1) * CHUNK])
            tot = cum[CHUNK - 1:CHUNK]
            kd = (pa_ref[rows, pl.ds(K_OFF, QK)].astype(F32) * jnp.exp(tot - cum)).astype(BF16)
            decay = jnp.exp(tot)
            for h in range(HEADS):
                lanes = slice(h * DK, (h + 1) * DK)
                new = state[h] * decay[:, lanes] + _dot(pa_ref[rows, pl.ds(V_OFF + h * DV, DV)], kd[:, lanes], TN)
                state[h] = new
                st_ref[c, h] = new
        for c in range(cb):
            rows = pl.ds(c * CHUNK, CHUNK)
            for h in range(HEADS):
                qs = (pa_ref[rows, pl.ds(Q_OFF + h * DK, DK)].astype(F32) * Q_SCALE).astype(BF16)
                o = _dot(qs, st_ref[c, h].astype(BF16), NT)
                rs = lax.rsqrt(jnp.mean(o * o, axis=-1, keepdims=True) + EPS)
                rr = pa_ref[rows, pl.ds(R_OFF + h * DV, DV)].astype(F32)
                y_ref[rows, pl.ds(h * DV, DV)] = (o * rs * gn_ref[h] * (rr * _sig(rr))).astype(BF16)

    return pl.pallas_call(
        body, name=name, grid=(S // Tg,),
        in_specs=[
            pl.BlockSpec((Tg, 2 * QK + 2 * GV), lambda t: (t, 0)),
            pl.BlockSpec((Tg, LANES), lambda t: (t, 0)),
            pl.BlockSpec((LANES, QK), lambda t: (0, 0)),
            pl.BlockSpec((1, QK), lambda t: (0, 0)),
            pl.BlockSpec((HEADS, 1, DV), lambda t: (0, 0, 0)),
        ],
        out_specs=[
            pl.BlockSpec((Tg, GV), lambda t: (t, 0)),
            pl.BlockSpec((cb, HEADS, DV, DK), lambda t: (t, 0, 0, 0)),
        ],
        out_shape=[pltpu.HBM((S, GV), BF16), pltpu.HBM((S // CHUNK, HEADS, DV, DK), F32)],
        scratch_shapes=[pltpu.VMEM((HEADS, DV, DK), F32)],
        compiler_params=_cp(("arbitrary",), VMEM_BIG),
    )(*_hbm(pa, alow, wg, bgate, gnorm))


def _gla_bwd_call(pa, alow, wg, bgate, gnorm, states, dy, *, name):
    S = pa.shape[0]
    Tg = min(GLA_TILE, S)
    cb = Tg // CHUNK
    nt = S // Tg

    def rev(t):
        return nt - 1 - t

    def body(pa_ref, al_ref, wg_ref, bg_ref, gn_ref, st_ref, prev_ref, dy_ref,
             dpa_ref, dlow_ref, dwg_ref, dgn_ref, dbg_ref, carry, pbuf, dlbuf):
        t = pl.program_id(0)

        @pl.when(t == 0)
        def _():
            carry[...] = jnp.zeros_like(carry)
            dwg_ref[...] = jnp.zeros_like(dwg_ref)
            dgn_ref[...] = jnp.zeros_like(dgn_ref)
            dbg_ref[...] = jnp.zeros_like(dbg_ref)

        logit = _dot(al_ref[...], wg_ref[...], NN) + bg_ref[...]
        ls = _logsig(logit) * (1.0 / 16.0)
        sneg = 1.0 / (1.0 + jnp.exp(logit))
        tri = _tri(True)
        upper = _tri(False)
        first_tile = rev(t) == 0
        heads = range(HEADS)

        w, decay, kd = [], [], []
        for c in range(cb):
            rows = pl.ds(c * CHUNK, CHUNK)
            cum = _exact_mask_dot(tri, ls[c * CHUNK:(c + 1) * CHUNK])
            tot = cum[CHUNK - 1:CHUNK]
            w.append(jnp.exp(tot - cum))
            decay.append(jnp.exp(tot))
            kd.append(pa_ref[rows, pl.ds(K_OFF, QK)].astype(F32) * w[c])

        dgn = [jnp.zeros((1, DV), F32) for _ in heads]
        for c in range(cb):
            rows = pl.ds(c * CHUNK, CHUNK)
            for h in heads:
                gn = gn_ref[h]
                qs = (pa_ref[rows, pl.ds(Q_OFF + h * DK, DK)].astype(F32) * Q_SCALE).astype(BF16)
                st16 = st_ref[c, h].astype(BF16)
                o = _dot(qs, st16, NT)
                rs = lax.rsqrt(jnp.mean(o * o, axis=-1, keepdims=True) + EPS)
                oh = o * rs
                rr = pa_ref[rows, pl.ds(R_OFF + h * DV, DV)].astype(F32)
                sr = _sig(rr)
                dyv = dy_ref[rows, pl.ds(h * DV, DV)].astype(F32)
                dpa_ref[rows, pl.ds(R_OFF + h * DV, DV)] = (
                    dyv * oh * gn * sr * (1.0 + rr * (1.0 - sr))).astype(BF16)
                don = dyv * (rr * sr)
                dgn[h] = dgn[h] + jnp.sum(don * oh, axis=0, keepdims=True)
                doh = don * gn
                do16 = (rs * (doh - oh * jnp.mean(doh * oh, axis=-1, keepdims=True))).astype(BF16)
                dpa_ref[rows, pl.ds(Q_OFF + h * DK, DK)] = (_dot(do16, st16, NN) * Q_SCALE).astype(BF16)
                pbuf[c, h] = _dot(do16, qs, TN)
        for h in heads:
            dgn_ref[h] += dgn[h]

        dkd = [[None] * HEADS for _ in range(cb)]
        ddecay = [[None] * HEADS for _ in range(cb)]
        for c in reversed(range(cb)):
            rows = pl.ds(c * CHUNK, CHUNK)
            for h in heads:
                lanes = slice(h * DK, (h + 1) * DK)
                gt = pbuf[c, h] + carry[h]
                gt16 = gt.astype(BF16)
                dkd[c][h] = _dot(pa_ref[rows, pl.ds(V_OFF + h * DV, DV)], gt16, NN)
                dpa_ref[rows, pl.ds(V_OFF + h * DV, DV)] = _dot(kd[c][:, lanes].astype(BF16), gt16, NT).astype(BF16)
                if c > 0:
                    st_prev = st_ref[c - 1, h]
                else:
                    st_prev = jnp.where(first_tile, 0.0, prev_ref[0, h])
                ddecay[c][h] = jnp.sum(gt * st_prev, axis=0, keepdims=True)
                carry[h] = gt * decay[c][:, lanes]

        dbg = jnp.zeros((1, QK), F32)
        for c in range(cb):
            rows = pl.ds(c * CHUNK, CHUNK)
            dkd_c = jnp.concatenate(dkd[c], axis=1)
            dpa_ref[rows, pl.ds(K_OFF, QK)] = (dkd_c * w[c]).astype(BF16)
            e = dkd_c * kd[c]
            dtot = jnp.sum(e, axis=0, keepdims=True) + jnp.concatenate(ddecay[c], axis=1) * decay[c]
            dls = dtot - _exact_mask_dot(upper, e)
            dlogit = dls * (1.0 / 16.0) * sneg[c * CHUNK:(c + 1) * CHUNK]
            dlbuf[rows, :] = dlogit.astype(BF16)
            dbg = dbg + jnp.sum(dlogit, axis=0, keepdims=True)
        dbg_ref[...] += dbg

        dl16 = dlbuf[...]
        dlow_ref[...] = _dot(dl16, wg_ref[...], NT).astype(BF16)
        dwg_ref[...] += _dot(al_ref[...], dl16, TN)

    wide = 2 * QK + 2 * GV
    tiles = [((Tg, wide), BF16), ((cb + 1, HEADS, DV, DK), F32), ((Tg, GV), BF16), ((Tg, wide), BF16),
             ((Tg, QK), BF16)]
    resident = [((cb + 1, HEADS, DV, DK), F32)]
    return pl.pallas_call(
        body, name=name, grid=(nt,),
        in_specs=[
            pl.BlockSpec((Tg, wide), lambda t: (rev(t), 0)),
            pl.BlockSpec((Tg, LANES), lambda t: (rev(t), 0)),
            pl.BlockSpec((LANES, QK), lambda t: (0, 0)),
            pl.BlockSpec((1, QK), lambda t: (0, 0)),
            pl.BlockSpec((HEADS, 1, DV), lambda t: (0, 0, 0)),
            pl.BlockSpec((cb, HEADS, DV, DK), lambda t: (rev(t), 0, 0, 0)),
            pl.BlockSpec((1, HEADS, DV, DK), lambda t: (jnp.maximum(rev(t) * cb - 1, 0), 0, 0, 0)),
            pl.BlockSpec((Tg, GV), lambda t: (rev(t), 0)),
        ],
        out_specs=[
            pl.BlockSpec((Tg, wide), lambda t: (rev(t), 0)),
            pl.BlockSpec((Tg, LANES), lambda t: (rev(t), 0)),
            pl.BlockSpec((LANES, QK), lambda t: (0, 0)),
            pl.BlockSpec((HEADS, 1, DV), lambda t: (0, 0, 0)),
            pl.BlockSpec((1, QK), lambda t: (0, 0)),
        ],
        out_shape=[pltpu.HBM((S, wide), BF16), pltpu.HBM((S, LANES), BF16), jax.ShapeDtypeStruct((LANES, QK), F32),
                   jax.ShapeDtypeStruct((HEADS, 1, DV), F32), jax.ShapeDtypeStruct((1, QK), F32)],
        scratch_shapes=[pltpu.VMEM((HEADS, DV, DK), F32), pltpu.VMEM((cb, HEADS, DV, DK), F32),
                        pltpu.VMEM((Tg, QK), BF16)],
        compiler_params=_cp(("arbitrary",), _vmem_limit(tiles, resident)),
    )(*_hbm(pa, alow, wg, bgate, gnorm, states, states, dy))


SGU_TILE = 256


def _sgu_mask():
    r = lax.broadcasted_iota(jnp.int32, (SBLOCK, SBLOCK), 0)
    c = lax.broadcasted_iota(jnp.int32, (SBLOCK, SBLOCK), 1)
    return (c < CHUNK) | (r >= CHUNK)


def _ln_stats(vf):
    mu = jnp.mean(vf, axis=-1, keepdims=True)
    xc = vf - mu
    rs = lax.rsqrt(jnp.mean(xc * xc, axis=-1, keepdims=True) + EPS)
    return rs, xc * rs


def _sgu_fwd_call(ps, ln_g, ln_b, w_sp, b_sp, *, name):
    S = ps.shape[0]
    Ts = min(SGU_TILE, S)

    def body(ps_ref, lg_ref, lb_ref, w_ref, b_ref, y_ref):
        mask = _sgu_mask()
        for g in range(GROUPS):
            wm = jnp.where(mask, w_ref[g], 0.0).astype(BF16)
            for p in range(Ts // SBLOCK):
                rows = pl.ds(p * SBLOCK, SBLOCK)
                u = _gelu(ps_ref[rows, pl.ds(g * DG, DG)].astype(F32))
                _, xh = _ln_stats(_gelu(ps_ref[rows, pl.ds(D_MODEL + g * DG, DG)].astype(F32)))
                vn = xh * lg_ref[g] + lb_ref[g]
                mixed = _dot(wm, vn.astype(BF16), NN) + b_ref[g]
                y_ref[rows, pl.ds(g * DG, DG)] = (u * mixed).astype(BF16)

    full3 = lambda a, b, c: pl.BlockSpec((a, b, c), lambda t: (0, 0, 0))
    return pl.pallas_call(
        body, name=name, grid=(S // Ts,),
        in_specs=[pl.BlockSpec((Ts, 2 * D_MODEL), lambda t: (t, 0)),
                  full3(GROUPS, 1, DG), full3(GROUPS, 1, DG), full3(GROUPS, SBLOCK, SBLOCK), full3(GROUPS, SBLOCK, 1)],
        out_specs=pl.BlockSpec((Ts, D_MODEL), lambda t: (t, 0)),
        out_shape=pltpu.HBM((S, D_MODEL), BF16),
        compiler_params=_cp(("parallel",)),
    )(*_hbm(ps, ln_g, ln_b, w_sp, b_sp))


def _sgu_bwd_call(ps, ln_g, ln_b, w_sp, b_sp, dy, *, name):
    S = ps.shape[0]
    Ts = min(SGU_TILE, S)

    def body(ps_ref, lg_ref, lb_ref, w_ref, b_ref, dy_ref, ds_ref, dlg_ref, dlb_ref, dw_ref, db_ref):
        @pl.when(pl.program_id(0) == 0)
        def _():
            dlg_ref[...] = jnp.zeros_like(dlg_ref)
            dlb_ref[...] = jnp.zeros_like(dlb_ref)
            dw_ref[...] = jnp.zeros_like(dw_ref)
            db_ref[...] = jnp.zeros_like(db_ref)

        mask = _sgu_mask()
        for g in range(GROUPS):
            wm = jnp.where(mask, w_ref[g], 0.0).astype(BF16)
            lg = lg_ref[g]
            for p in range(Ts // SBLOCK):
                rows = pl.ds(p * SBLOCK, SBLOCK)
                su = ps_ref[rows, pl.ds(g * DG, DG)].astype(F32)
                sv = ps_ref[rows, pl.ds(D_MODEL + g * DG, DG)].astype(F32)
                u, du = _gelu_and_grad(su)
                gv, dgv = _gelu_and_grad(sv)
                rs, xh = _ln_stats(gv)
                vn16 = (xh * lg + lb_ref[g]).astype(BF16)
                mixed = _dot(wm, vn16, NN) + b_ref[g]
                dyv = dy_ref[rows, pl.ds(g * DG, DG)].astype(F32)
                ds_ref[rows, pl.ds(g * DG, DG)] = (dyv * mixed * du).astype(BF16)
                dmix = dyv * u
                dmix16 = dmix.astype(BF16)
                db_ref[g] += jnp.sum(dmix, axis=-1, keepdims=True)
                dw_ref[g] += jnp.where(mask, _dot(dmix16, vn16, NT), 0.0)
                dvn = _dot(wm, dmix16, TN)
                dlg_ref[g] += jnp.sum(dvn * xh, axis=0, keepdims=True)
                dlb_ref[g] += jnp.sum(dvn, axis=0, keepdims=True)
                dxh = dvn * lg
                dvf = rs * (dxh - jnp.mean(dxh, axis=-1, keepdims=True)
                            - xh * jnp.mean(dxh * xh, axis=-1, keepdims=True))
                ds_ref[rows, pl.ds(D_MODEL + g * DG, DG)] = (dvf * dgv).astype(BF16)

    full3 = lambda a, b, c: pl.BlockSpec((a, b, c), lambda t: (0, 0, 0))
    tiles = [((Ts, 2 * D_MODEL), BF16), ((Ts, D_MODEL), BF16), ((Ts, 2 * D_MODEL), BF16)]
    return pl.pallas_call(
        body, name=name, grid=(S // Ts,),
        in_specs=[pl.BlockSpec((Ts, 2 * D_MODEL), lambda t: (t, 0)),
                  full3(GROUPS, 1, DG), full3(GROUPS, 1, DG), full3(GROUPS, SBLOCK, SBLOCK), full3(GROUPS, SBLOCK, 1),
                  pl.BlockSpec((Ts, D_MODEL), lambda t: (t, 0))],
        out_specs=[pl.BlockSpec((Ts, 2 * D_MODEL), lambda t: (t, 0)),
                   full3(GROUPS, 1, DG), full3(GROUPS, 1, DG), full3(GROUPS, SBLOCK, SBLOCK), full3(GROUPS, SBLOCK, 1)],
        out_shape=[pltpu.HBM((S, 2 * D_MODEL), BF16),
                   jax.ShapeDtypeStruct((GROUPS, 1, DG), F32), jax.ShapeDtypeStruct((GROUPS, 1, DG), F32),
                   jax.ShapeDtypeStruct((GROUPS, SBLOCK, SBLOCK), F32),
                   jax.ShapeDtypeStruct((GROUPS, SBLOCK, 1), F32)],
        compiler_params=_cp(("arbitrary",), _vmem_limit(tiles, [])),
    )(*_hbm(ps, ln_g, ln_b, w_sp, b_sp, dy))


def _position():
    return lax.axis_index("x"), lax.axis_index("y"), lax.axis_index("c")


def _gather_copies(srcs, dsts, send_sems, recv_sems, local_sems):
    x, y, c = _position()
    me, sibling = (x, y, c), (x, y, 1 - c)
    chips = [(1 - x, y), (x, 1 - y), (1 - x, 1 - y)]
    n = len(srcs)

    def slab(a, block):
        px, py, pc = block
        return dsts[a].at[4 * px + 2 * py + pc]

    def copy(a, k, block, to, src=None):
        return pltpu.make_async_remote_copy(
            src_ref=slab(a, block) if src is None else src, dst_ref=slab(a, block),
            send_sem=send_sems.at[7 * a + k], recv_sem=recv_sems.at[7 * a + k], device_id=to, device_id_type=MESH)

    mine = [pltpu.make_async_copy(srcs[a], slab(a, me), local_sems.at[a]) for a in range(n)]
    for cp in mine:
        cp.start()
    first = []
    for a in range(n):
        first.append(copy(a, 0, me, sibling, src=srcs[a]))
        first += [copy(a, 1 + j, me, (*chip, c), src=srcs[a]) for j, chip in enumerate(chips)]
    for cp in first:
        cp.start()
    passed = []
    for j, chip in enumerate(chips):
        for a in range(n):
            copy(a, 1 + j, (*chip, c), me).wait_recv()
            cp = copy(a, 4 + j, (*chip, c), sibling)
            cp.start()
            passed.append(cp)
    for a in range(n):
        copy(a, 0, sibling, me).wait_recv()
        for j, chip in enumerate(chips):
            copy(a, 4 + j, (*chip, 1 - c), me).wait_recv()
    for cp in first + passed:
        cp.wait_send()
    for cp in mine:
        cp.wait()


def _gather_copies_relayed(srcs, dsts, send_sems, recv_sems, local_sems, waves=1):
    x, y, c = _position()
    me, sibling = (x, y, c), (x, y, 1 - c)
    x_chip, y_chip, d_chip = (1 - x, y), (x, 1 - y), (1 - x, 1 - y)
    south = c == 0
    relay_from = (jnp.where(south, x, 1 - x), jnp.where(south, 1 - y, y), c)
    relay_to = (jnp.where(south, 1 - x, x), jnp.where(south, y, 1 - y), c)
    n = len(srcs)

    def cut(a, w):
        rows = srcs[a].shape[0]
        step = (rows // waves) // 16 * 16
        if waves == 1 or step == 0:
            return pl.ds(0, rows) if w == 0 else None
        return pl.ds(w * step, step if w < waves - 1 else rows - w * step)

    def slab(a, block, w):
        px, py, pc = block
        return dsts[a].at[4 * px + 2 * py + pc, cut(a, w)]

    def copy(a, k, w, block, to, own=False):
        sem = (7 * a + k) * waves + w
        return pltpu.make_async_remote_copy(
            src_ref=srcs[a].at[cut(a, w)] if own else slab(a, block, w), dst_ref=slab(a, block, w),
            send_sem=send_sems.at[sem], recv_sem=recv_sems.at[sem], device_id=to, device_id_type=MESH)

    pieces = [(a, w) for w in range(waves) for a in range(n) if cut(a, w) is not None]
    mine = [pltpu.make_async_copy(srcs[a], dsts[a].at[4 * x + 2 * y + c], local_sems.at[a]) for a in range(n)]
    for cp in mine:
        cp.start()
    sent = []
    for a, w in pieces:
        sent += [copy(a, 0, w, me, sibling, own=True), copy(a, 1, w, me, (*x_chip, c), own=True),
                 copy(a, 2, w, me, (*y_chip, c), own=True)]
    for cp in sent:
        cp.start()
    for a, w in pieces:
        copy(a, 1, w, (*x_chip, c), me).wait_recv()
        copy(a, 2, w, (*y_chip, c), me).wait_recv()
        later = [copy(a, 3, w, relay_from, relay_to), copy(a, 4, w, (*x_chip, c), sibling),
                 copy(a, 5, w, (*y_chip, c), sibling)]
        for cp in later:
            cp.start()
        sent += later
    for a, w in pieces:
        copy(a, 3, w, (*d_chip, c), me).wait_recv()
        cp = copy(a, 6, w, (*d_chip, c), sibling)
        cp.start()
        sent.append(cp)
    for a, w in pieces:
        copy(a, 0, w, sibling, me).wait_recv()
        for k, chip in ((4, x_chip), (5, y_chip), (6, d_chip)):
            copy(a, k, w, (*chip, 1 - c), me).wait_recv()
    for cp in sent:
        cp.wait_send()
    for cp in mine:
        cp.wait()


def _all_gather_hbm(shards, *, name, waves=1):
    n = len(shards)

    def body(*refs):
        srcs, dsts = refs[:n], refs[n:2 * n]
        send_sems, recv_sems, local_sems = refs[2 * n:]
        _gather_copies_relayed(srcs, dsts, send_sems, recv_sems, local_sems, waves=waves)

    return pl.pallas_call(
        body, name=name,
        in_specs=[ANY] * n, out_specs=[ANY] * n,
        out_shape=[jax.ShapeDtypeStruct((N_DEV, *s.shape), s.dtype) for s in shards],
        scratch_shapes=_comm_sems(n, waves),
    )(*shards)


FLIPS = [(fx, fy, fc) for fx in (0, 1) for fy in (0, 1) for fc in (0, 1)][1:]


def _scatter_copies(srcs, dsts, send_sems, recv_sems, local_sems, waves=1):
    n = len(srcs)
    x, y, c = _position()
    me = 4 * x + 2 * y + c
    mine = [pltpu.make_async_copy(srcs[a].at[me], dsts[a].at[me], local_sems.at[a]) for a in range(n)]
    for cp in mine:
        cp.start()
    peers = []
    for fx, fy, fc in FLIPS:
        tx = 1 - x if fx else x
        ty = 1 - y if fy else y
        tc = 1 - c if fc else c
        peers.append(((tx, ty, tc), 4 * tx + 2 * ty + tc))
    copies = []
    for w in range(waves):
        wave = []
        for k, (peer_id, peer) in enumerate(peers):
            for a in range(n):
                rows = srcs[a].shape[1]
                step = rows if waves == 1 else (rows // waves) // 16 * 16
                r0 = w * step
                cut = pl.ds(r0, step if w < waves - 1 else rows - r0)
                sem = (7 * a + k) * waves + w
                cp = pltpu.make_async_remote_copy(
                    src_ref=srcs[a].at[peer, cut], dst_ref=dsts[a].at[me, cut],
                    send_sem=send_sems.at[sem], recv_sem=recv_sems.at[sem],
                    device_id=peer_id, device_id_type=MESH)
                cp.start()
                wave.append(cp)
        for cp in wave:
            cp.wait_send()
        copies += wave
    for cp in copies:
        cp.wait_recv()
    for cp in mine:
        cp.wait()


def _pair_copies(srcs, dsts, send_sems, recv_sems, local_sems):
    x, y, c = _position()
    copies = [pltpu.make_async_remote_copy(
        src_ref=srcs[a], dst_ref=dsts[a], send_sem=send_sems.at[a], recv_sem=recv_sems.at[a],
        device_id=(x, y, 1 - c), device_id_type=MESH) for a in range(len(srcs))]
    for cp in copies:
        cp.start()
    for cp in copies:
        cp.wait()


def _chip_scatter_copies(srcs, dsts, send_sems, recv_sems, local_sems, waves=1):
    n = len(srcs)
    x, y, c = _position()
    here = 2 * x + y
    mine = [pltpu.make_async_copy(srcs[a].at[here], dsts[a].at[here], local_sems.at[a]) for a in range(n)]
    for cp in mine:
        cp.start()
    peers = []
    for fx, fy in ((1, 0), (0, 1), (1, 1)):
        tx = 1 - x if fx else x
        ty = 1 - y if fy else y
        peers.append(((tx, ty, c), 2 * tx + ty))
    copies = []
    for w in range(waves):
        wave = []
        for k, (peer_id, peer) in enumerate(peers):
            for a in range(n):
                rows = srcs[a].shape[1]
                step = rows if waves == 1 else (rows // waves) // 16 * 16
                r0 = w * step
                cut = pl.ds(r0, step if w < waves - 1 else rows - r0)
                sem = (3 * a + k) * waves + w
                cp = pltpu.make_async_remote_copy(
                    src_ref=srcs[a].at[peer, cut], dst_ref=dsts[a].at[here, cut],
                    send_sem=send_sems.at[sem], recv_sem=recv_sems.at[sem],
                    device_id=peer_id, device_id_type=MESH)
                cp.start()
                wave.append(cp)
        for cp in wave:
            cp.wait_send()
        copies += wave
    for cp in copies:
        cp.wait_recv()
    for cp in mine:
        cp.wait()


def _comm_sems(n, waves=1):
    return [pltpu.SemaphoreType.DMA((7 * n * waves,)), pltpu.SemaphoreType.DMA((7 * n * waves,)),
            pltpu.SemaphoreType.DMA((n,))]


def _handshake(peers):
    barrier = pltpu.get_barrier_semaphore()
    for peer in peers:
        pl.semaphore_signal(barrier, inc=1, device_id=peer, device_id_type=MESH)
    pl.semaphore_wait(barrier, len(peers))


def _sequencer_call(arrays, out_types, copies_fn, peers_fn, *, name, collective_id, waves=1):
    n = len(arrays)
    srcs = [jax.new_ref(a, memory_space=pltpu.MemorySpace.HBM) for a in arrays]
    dsts = [jax.empty_ref(t, memory_space=pltpu.MemorySpace.HBM) for t in out_types]
    extra = {} if waves == 1 else {"waves": waves}

    @pl.kernel(mesh=plsc.ScalarSubcoreMesh(axis_name="sequencer", num_cores=1), name=name,
               scratch_types=_comm_sems(n, waves), compiler_params=pltpu.CompilerParams(collective_id=collective_id))
    def launch(send_sems, recv_sems, local_sems):
        _handshake(peers_fn())
        copies_fn(srcs, dsts, send_sems, recv_sems, local_sems, **extra)

    launch()
    return [d[...] for d in dsts]


def _all_other_devices():
    x, y, c = _position()
    return [(1 - x if fx else x, 1 - y if fy else y, 1 - c if fc else c) for fx, fy, fc in FLIPS]


def _gather_relay_peers():
    x, y, c = _position()
    return [(x, y, 1 - c), (1 - x, y, c), (x, 1 - y, c)]


def _gather_peers():
    x, y, c = _position()
    return [(x, y, 1 - c), (1 - x, y, c), (x, 1 - y, c), (1 - x, 1 - y, c)]


def _small_gather_async(shards, *, name, collective_id):
    return _sequencer_call(shards, [jax.ShapeDtypeStruct((N_DEV, *s.shape), s.dtype) for s in shards],
                           _gather_copies, _gather_peers, name=name, collective_id=collective_id)


def _sibling():
    x, y, c = _position()
    return [(x, y, 1 - c)]


def _same_core_of_other_chips():
    x, y, c = _position()
    return [(1 - x, y, c), (x, 1 - y, c), (1 - x, 1 - y, c)]


def _pair_exchange_async(parts, *, name, collective_id):
    return _sequencer_call(parts, [jax.ShapeDtypeStruct(p.shape, p.dtype) for p in parts],
                           _pair_copies, _sibling, name=name, collective_id=collective_id)


def _chip_scatter_async(parts, *, name, collective_id, waves=1):
    return _sequencer_call(parts, [jax.ShapeDtypeStruct(p.shape, p.dtype) for p in parts],
                           _chip_scatter_copies, _same_core_of_other_chips, name=name, collective_id=collective_id,
                           waves=waves)


def _pair_sum_call(mine, theirs, *, name, tw=D_MODEL):
    n, r, w = mine.shape

    def body(a_ref, b_ref, o_ref):
        o_ref[...] = (a_ref[...].astype(F32) + b_ref[...].astype(F32)).astype(o_ref.dtype)

    spec = pl.BlockSpec((None, r, tw), lambda i, j: (i, 0, j))
    return pl.pallas_call(
        body, name=name, grid=(n, w // tw), in_specs=[spec, spec], out_specs=spec,
        out_shape=pltpu.HBM(mine.shape, mine.dtype),
        compiler_params=_cp(("parallel", "parallel")),
    )(*_hbm(mine, theirs))


def _scatter_blocks_async(parts, *, name, collective_id, waves=1):
    return _sequencer_call(parts, [jax.ShapeDtypeStruct(p.shape, p.dtype) for p in parts],
                           _scatter_copies, _all_other_devices, name=name, collective_id=collective_id, waves=waves)


def _all_gather_async(shards, *, name, collective_id, waves=1):
    return _sequencer_call(shards, [jax.ShapeDtypeStruct((N_DEV, *s.shape), s.dtype) for s in shards],
                           _gather_copies_relayed, _gather_relay_peers, name=name, collective_id=collective_id,
                           waves=waves)


def _adamw_math(w, g, m, v):
    m = ADAM_B1 * m + (1.0 - ADAM_B1) * g
    v = ADAM_B2 * v + (1.0 - ADAM_B2) * (g * g)
    m_hat = m / (1.0 - ADAM_B1 ** ADAM_STEP)
    v_hat = v / (1.0 - ADAM_B2 ** ADAM_STEP)
    delta = -ADAM_LR * (m_hat / (jnp.sqrt(v_hat) + ADAM_EPS) + ADAM_WD * w)
    return delta, m, v


def _adamw_reduce_call(recv, w, m, v, *, name, T=128):
    R, W = w.shape
    n_parts = recv.shape[0]
    if R % T == 0:
        tr, tw = T, W
    elif (R // 2) % 16 == 0:
        tr, tw = R // 2, W
    else:
        tr, tw = R, 2 * LANES

    def body(p_ref, w_ref, m_ref, v_ref, g_out, d_out, m_out, v_out):
        g = p_ref[0].astype(F32)
        for d in range(1, n_parts):
            g = g + p_ref[d].astype(F32)
        g_out[...] = g
        d_out[...], m_out[...], v_out[...] = _adamw_math(w_ref[...], g, m_ref[...], v_ref[...])

    row = pl.BlockSpec((tr, tw), lambda i, j: (i, j))
    out = pltpu.HBM((R, W), F32)
    return pl.pallas_call(
        body, name=name, grid=(R // tr, W // tw),
        in_specs=[pl.BlockSpec((n_parts, tr, tw), lambda i, j: (0, i, j)), row, row, row],
        out_specs=[row] * 4, out_shape=[out] * 4,
        compiler_params=_cp(("parallel", "parallel"), VMEM_BIG),
    )(*_hbm(recv, w, m, v))


SMALL_EARLY = (("b_gate", 8), ("w_spatial", 512), ("b_spatial", 8), ("norm_post_mix", 8), ("norm_pre_ffn", 8),
               ("norm_post_ffn", 8), ("w_gate_up", 128), ("gla_norm", 64), ("sgu_ln_g", 64), ("sgu_ln_b", 64))
SMALL_EARLY_AT = {}
for _name, _rows in SMALL_EARLY:
    SMALL_EARLY_AT[_name] = sum(r for _, r in SMALL_EARLY[:len(SMALL_EARLY_AT)])
SMALL_EARLY_ROWS = sum(r for _, r in SMALL_EARLY)
SMALL_LATE_ROWS = 16


def _small_early_rows(grads):
    def rows(a, n_rows):
        a = a.reshape(-1, LANES)
        return jnp.pad(a, ((0, n_rows - a.shape[0]), (0, 0)))

    def device_major(a, n_rows):
        r, c = a.shape[0], a.shape[1] // N_DEV
        a = a.reshape(r, N_DEV, c).transpose(1, 0, 2)
        a = jnp.pad(a, ((0, 0), (0, n_rows // N_DEV - r), (0, LANES - c)))
        return a.reshape(n_rows, LANES)

    pieces = []
    for name, n_rows in SMALL_EARLY:
        g = grads[name]
        if name in SMALL_SHARDED:
            pieces.append(device_major(g.reshape(g.shape[0], -1) if g.ndim == 2 else g.reshape(g.shape[0], g.shape[-1]), n_rows))
        else:
            pieces.append(rows(g, n_rows))
    return jnp.concatenate(pieces, axis=0)


def _small_update_call(got_early, got_late, w, m, v, *, name):
    names = list(SMALL)
    n_p = len(names)
    E = SMALL_EARLY_ROWS

    def body(early_ref, late_ref, *rest):
        w_refs, m_refs, v_refs = [dict(zip(names, rest[i * n_p:(i + 1) * n_p])) for i in range(3)]
        outs, tot = rest[3 * n_p:-1], rest[-1]
        loss_out = outs[0]
        g_out, d_out, m_out, v_out = [dict(zip(names, outs[1 + i * n_p:1 + (i + 1) * n_p])) for i in range(4)]
        acc, acc_late = early_ref[0], late_ref[0]
        for d in range(1, N_DEV):
            acc, acc_late = acc + early_ref[d], acc_late + late_ref[d]
        tot[0:E, :] = acc
        tot[E:E + SMALL_LATE_ROWS, :] = acc_late
        loss_out[...] = tot[E + 8:E + 9, :]

        x, y, c = _position()
        me = 4 * x + 2 * y + c

        def update(name, g, ix):
            g_out[name][ix] = g
            d_out[name][ix], m_out[name][ix], v_out[name][ix] = _adamw_math(
                w_refs[name][ix], g, m_refs[name][ix], v_refs[name][ix])

        for name in names:
            shape = w[name].shape
            if name in SMALL_SHARDED:
                per_dev = dict(SMALL_EARLY)[name] // N_DEV
                at = pl.multiple_of(SMALL_EARLY_AT[name] + me * per_dev, 8)
                g = tot[pl.ds(at, per_dev), :]
                update(name, g[:shape[1], :shape[2]], (0,))
            elif name == "w_spatial":
                for grp in range(GROUPS):
                    at = SMALL_EARLY_AT[name] + grp * SBLOCK
                    update(name, tot[at:at + SBLOCK, :], (0, grp))
            elif name == "b_spatial":
                at = SMALL_EARLY_AT[name]
                update(name, tot[at:at + GROUPS, :], (0,))
            else:
                at = E if name == "norm_pre_mix" else SMALL_EARLY_AT[name]
                for k in range(shape[1] // LANES):
                    update(name, tot[at + k:at + k + 1, :], (slice(None), pl.ds(k * LANES, LANES)))

    state = [s[n] for s in (w, m, v) for n in names]
    out_shapes = [jax.ShapeDtypeStruct((1, LANES), F32)] + [jax.ShapeDtypeStruct(w[n].shape, F32) for n in names] * 4
    outs = pl.pallas_call(
        body, name=name,
        in_specs=[VMEM_SPEC] * (2 + len(state)), out_specs=[VMEM_SPEC] * len(out_shapes), out_shape=out_shapes,
        scratch_shapes=[pltpu.VMEM((E + SMALL_LATE_ROWS, LANES), F32)],
    )(got_early, got_late, *state)
    per_name = {n: tuple(outs[1 + i * n_p + j] for i in range(4)) for j, n in enumerate(names)}
    return outs[0], per_name


def _tile_rows(n_elems):
    return -(-n_elems // (8 * LANES)) * 8


def _pack_rows(parts, rows):
    pieces = []
    for p in parts:
        q = p.reshape(-1, LANES)
        pieces.append(jnp.pad(q, ((0, _tile_rows(p.size) - q.shape[0]), (0, 0))))
    buf = jnp.concatenate(pieces, axis=0)
    return jnp.pad(buf, ((0, rows - buf.shape[0]), (0, 0)))


def _in_w_blocks_call(a, live, finished, *, name, tk=TOKEN_TILE):
    S = a.shape[0]
    tk = min(tk, S)
    steps = S // tk
    n_live = len(live)
    n_chips = N_DEV // 2
    out_shape = (n_chips, IN_BLK, D_MODEL)

    def body(a_ref, *rest):
        live_refs, done_refs = rest[:n_live], rest[n_live:n_live + len(finished)]
        keep_ref, send_ref = rest[n_live + len(finished):n_live + len(finished) + 2]
        accs = rest[n_live + len(finished) + 2:]
        k = pl.program_id(0)

        @pl.when(k == 0)
        def _():
            for acc in accs:
                acc[...] = jnp.zeros_like(acc)

        av = a_ref[...]
        for src, acc in zip(live_refs, accs):
            for m0 in range(0, src.shape[1], 1024):
                m1 = min(src.shape[1], m0 + 1024)
                acc[m0:m1, :] += _dot(src[:, m0:m1], av, TN)

        @pl.when(k == steps - 1)
        def _():
            groups = [(acc, cols) for acc, (_, cols) in zip(accs, live)]
            groups += [(ref, cols) for ref, (_, cols) in zip(done_refs, finished)]
            core = lax.axis_index("c")

            def cut(d, out_ref):
                lo, hi = IN_BLK * d, IN_BLK * (d + 1)
                for ref, (c0, c1) in groups:
                    s0, e0 = max(lo, c0), min(hi, c1)
                    if s0 < e0:
                        out_ref[d // 2, s0 - lo:e0 - lo, :] = ref[s0 - c0:e0 - c0, :].astype(BF16)

            for d in range(N_DEV):
                pl.when(core == d % 2)(lambda d=d: cut(d, keep_ref))
                pl.when(core != d % 2)(lambda d=d: cut(d, send_ref))

    acc_shapes = [(arr.shape[1], D_MODEL) for arr, _ in live]
    tiles = [((tk, D_MODEL), BF16)] + [((tk, arr.shape[1]), BF16) for arr, _ in live]
    resident = ([(arr.shape, arr.dtype) for arr, _ in finished] + [(out_shape, BF16)] * 2
                + [(s, F32) for s in acc_shapes])
    return pl.pallas_call(
        body, name=name, grid=(steps,),
        in_specs=[pl.BlockSpec((tk, D_MODEL), lambda k: (k, 0))]
        + [pl.BlockSpec((tk, arr.shape[1]), lambda k: (k, 0)) for arr, _ in live]
        + [_res(arr.shape) for arr, _ in finished],
        out_specs=[_acc(out_shape)] * 2,
        out_shape=[pltpu.HBM(out_shape, BF16)] * 2,
        scratch_shapes=[pltpu.VMEM(s, F32) for s in acc_shapes],
        compiler_params=_cp(("arbitrary",), _vmem_limit(tiles, resident)),
    )(*_hbm(a, *[arr for arr, _ in live], *[arr for arr, _ in finished]))


def _local_step(x, target, W, scatter, finish, pair, chip_scatter):
    g1, g2, g3, g4 = [W[n].reshape(1, D_MODEL) for n in ("norm_pre_mix", "norm_post_mix", "norm_pre_ffn", "norm_post_ffn")]
    wfi, wfo = W["w_ffn_in"].reshape(2 * D_FF, D_MODEL), W["w_ffn_out"]
    wg = jnp.pad(W["w_gate_up"], ((0, LANES - RANK), (0, 0))).astype(BF16)
    bgate = W["b_gate"].reshape(1, QK)
    gnorm = W["gla_norm"].reshape(HEADS, 1, DV)
    ln_g = W["sgu_ln_g"].reshape(GROUPS, 1, DG)
    ln_b = W["sgu_ln_b"].reshape(GROUPS, 1, DG)
    w_sp = W["w_spatial"]
    b_sp = W["b_spatial"].reshape(GROUPS, SBLOCK, 1)

    a, pa, alow, ps, pg, w_in_t = _in_proj_call(x, g1, W["w_in_blocks"], name="in_proj")
    y_gla, states = _gla_fwd_call(pa, alow, wg, bgate, gnorm, name="gla_fwd")
    y_sgu = _sgu_fwd_call(ps, ln_g, ln_b, w_sp, b_sp, name="sgu_fwd")
    t1, t2, merged, mix, x1, h = _mixer_tail_call(y_gla, y_sgu, pg, x, W["w_branch_gla"], W["w_branch_sgu"],
                                                  W["w_out"], g2, g3, name="mixer_tail")
    gu, z = _ffn_in_call(h, wfi, name="ffn_in")
    loss, dx2, dy, dg4 = _ffn_out_loss_call(z, wfo, x1, target, g4, name="ffn_out_loss")

    grads = {"norm_post_ffn": dg4}
    done = {}
    dgu = _ffn_out_bwd_call(dy, wfo, gu, name="ffn_out_bwd")
    dw_ffn_out = _weight_grads_call([z, dy], [(0, 1)], name="d_ffn_out_w")[0].reshape(N_DEV, D_FF // N_DEV, D_MODEL)
    dw_ffn_in = _ffn_in_grad_call(h, dgu, name="d_ffn_in_w").reshape(N_DEV, FF_BLK, D_MODEL)
    dgu, dw_ffn_out, dw_ffn_in = lax.optimization_barrier((dgu, dw_ffn_out, dw_ffn_in))
    ffn_received = scatter(("w_ffn_out", "w_ffn_in"), [dw_ffn_out, dw_ffn_in])
    dx1, dmix, grads["norm_pre_ffn"], grads["norm_post_mix"] = _ffn_in_bwd_call(dgu, wfi, dx2, x1, mix, g3, g2, name="ffn_in_bwd")
    dt1, dt2, dpg, dy_gla, dy_sgu = _mixer_bwd_call(dmix, t1, t2, pg, W["w_out"], W["w_branch_gla"], W["w_branch_sgu"],
                                                    name="mixer_bwd")
    rows = D_MODEL // N_DEV
    mixer_grads = _weight_grads_call([merged, dmix, y_gla, dt1, y_sgu, dt2], [(0, 1), (2, 3), (4, 5)], name="d_mixer_w")
    dy_gla, dy_sgu, mixer_grads = lax.optimization_barrier((dy_gla, dy_sgu, mixer_grads))
    mixer_received = scatter(("w_out", "w_branch_gla", "w_branch_sgu"),
                             [g.reshape(N_DEV, rows, D_MODEL) for g in mixer_grads])
    dps, dlg, dlb, dwsp, dbsp = _sgu_bwd_call(ps, ln_g, ln_b, w_sp, b_sp, dy_sgu, name="sgu_bwd")
    dw_s, dw_g = _weight_grads_call([a, dps, dpg], [(1, 0), (2, 0)], name="d_in_w_sgu_gates")
    dy_gla, dw_s, dw_g = lax.optimization_barrier((dy_gla, dw_s, dw_g))
    dpa, dlow, dwg, dgn, dbg = _gla_bwd_call(pa, alow, wg, bgate, gnorm, states, dy_gla, name="gla_bwd")
    ffn_received, mixer_received, dpa, dlow = lax.optimization_barrier((ffn_received, mixer_received, dpa, dlow))
    keep, send = _in_w_blocks_call(a, [(dpa, A_COLS), (dlow, LOW_COLS)], [(dw_s, S_COLS), (dw_g, G_COLS)],
                                   name="d_in_w_blocks")
    ffn_received, mixer_received, send = lax.optimization_barrier((ffn_received, mixer_received, send))
    from_sibling = pair(send)
    done.update({**finish(ffn_received), **finish(mixer_received)})
    done, keep = lax.optimization_barrier((done, keep))
    chip_sum = _pair_sum_call(keep, from_sibling, name="pair_sum_w_in")
    chip_sum, dpa = lax.optimization_barrier((chip_sum, dpa))
    in_received = chip_scatter(chip_sum)
    grad_x, grads["norm_pre_mix"] = _in_proj_bwd_call(dpa, dlow, dps, dpg, w_in_t, x, dx1, g1, name="in_proj_bwd")

    grads["w_gate_up"] = dwg[:RANK]
    grads["b_gate"] = dbg
    grads["gla_norm"] = dgn
    grads["sgu_ln_g"] = dlg
    grads["sgu_ln_b"] = dlb
    grads["w_spatial"] = dwsp
    grads["b_spatial"] = dbsp
    done.update(finish({"w_in": in_received}))
    return loss, grad_x, grads, done


WEIGHTS = ("norm_pre_mix", "w_in", "w_gate_up", "b_gate", "gla_norm", "sgu_ln_g", "sgu_ln_b", "w_spatial",
           "b_spatial", "w_branch_gla", "w_branch_sgu", "w_out", "norm_post_mix", "norm_pre_ffn", "w_ffn_in",
           "w_ffn_out", "norm_post_ffn")
BIG = ("w_in", "w_branch_gla", "w_branch_sgu", "w_out", "w_ffn_in", "w_ffn_out")
MIXER = ("w_branch_gla", "w_branch_sgu", "w_out")
FFN = ("w_ffn_in", "w_ffn_out")
COLUMN_SHARDED = ("w_in", "w_ffn_in")
LATE_SCATTER_WAVES = 4
GATHER_WAVES = 4
SMALL = tuple(n for n in WEIGHTS if n not in BIG)
SMALL_SHARDED = ("w_gate_up", "gla_norm", "sgu_ln_g", "sgu_ln_b")
SMALL_GATHER_ROWS = 32


def kernel(x, norm_pre_mix, w_in, w_gate_up, b_gate, gla_norm, sgu_ln_g, sgu_ln_b, w_spatial, b_spatial, w_branch_gla, w_branch_sgu, w_out, norm_post_mix, norm_pre_ffn, w_ffn_in, w_ffn_out, norm_post_ffn, loss_target, m_norm_pre_mix, m_w_in, m_w_gate_up, m_b_gate, m_gla_norm, m_sgu_ln_g, m_sgu_ln_b, m_w_spatial, m_b_spatial, m_w_branch_gla, m_w_branch_sgu, m_w_out, m_norm_post_mix, m_norm_pre_ffn, m_w_ffn_in, m_w_ffn_out, m_norm_post_ffn, v_norm_pre_mix, v_w_in, v_w_gate_up, v_b_gate, v_gla_norm, v_sgu_ln_g, v_sgu_ln_b, v_w_spatial, v_b_spatial, v_w_branch_gla, v_w_branch_sgu, v_w_out, v_norm_post_mix, v_norm_pre_ffn, v_w_ffn_in, v_w_ffn_out, v_norm_post_ffn):
    given = dict(locals())
    def local(a, n):
        return a[0].T if n in COLUMN_SHARDED else a[0]

    w = {n: local(given[n], n) for n in WEIGHTS}
    m = {n: local(given["m_" + n], n) for n in WEIGHTS}
    v = {n: local(given["v_" + n], n) for n in WEIGHTS}
    xs, target = x[0], loss_target[0]

    small_shard = _pack_rows([w[n] for n in SMALL_SHARDED], SMALL_GATHER_ROWS)
    first = _all_gather_async([w["w_in"].astype(BF16), small_shard], name="gather_w_in", collective_id=2,
                              waves=GATHER_WAVES)
    rest, first, w["w_in"], m["w_in"], v["w_in"] = lax.optimization_barrier(
        ([w[n].astype(BF16) for n in MIXER + FFN], first, w["w_in"], m["w_in"], v["w_in"]))
    rest_blocks = _all_gather_async(rest, name="gather_rest", collective_id=1)
    W = {n: w[n] for n in SMALL if n not in SMALL_SHARDED}
    blocks = dict(zip(MIXER + FFN, rest_blocks))
    for n in ("w_branch_gla", "w_branch_sgu", "w_out", "w_ffn_out"):
        W[n] = blocks[n].reshape(-1, D_MODEL)
    W["w_ffn_in"] = blocks["w_ffn_in"]
    W["w_in_blocks"] = first[0]
    small_blocks = first[1]
    off = 0
    for n in SMALL_SHARDED:
        r, c = w[n].shape
        blk = small_blocks[:, off:off + r * c // LANES].reshape(N_DEV, r, c)
        W[n] = blk.transpose(1, 0, 2).reshape(r, N_DEV * c)
        off += _tile_rows(r * c)

    scatter_ids = iter((3, 4))

    def scatter(names, parts):
        got = _scatter_blocks_async(parts, name="scatter_" + "_".join(names), collective_id=next(scatter_ids))
        return dict(zip(names, got))

    def finish(received):
        return {n: _adamw_reduce_call(r, w[n], m[n], v[n], name="adamw_" + n) for n, r in received.items()}

    def pair(part):
        return _pair_exchange_async([part], name="pair_w_in", collective_id=5)[0]

    def chip_scatter(part):
        return _chip_scatter_async([part], name="scatter_w_in", collective_id=6, waves=LATE_SCATTER_WAVES)[0]

    loss_part, grad_x, grads, big_done = _local_step(xs, target, W, scatter, finish, pair, chip_scatter)

    late = jnp.concatenate([grads["norm_pre_mix"].reshape(8, LANES), jnp.broadcast_to(loss_part, (8, LANES))], axis=0)
    gathered = _small_gather_async([_small_early_rows(grads), late], name="gather_small", collective_id=7)
    gathered, big_done["w_in"] = lax.optimization_barrier((gathered, big_done["w_in"]))
    loss_row, small_done = _small_update_call(
        *gathered, *[{n: given[prefix + n] for n in SMALL} for prefix in ("", "m_", "v_")], name="small_update")

    def pick(i):
        return [small_done[n][i] if n in SMALL else (big_done[n][i].T if n in COLUMN_SHARDED else big_done[n][i])[None]
                for n in WEIGHTS]

    return (loss_row[0, 0], grad_x[None], *pick(0), *pick(1), *pick(2), *pick(3))
```

```python
import jax
import jax.numpy as jnp
from jax import lax
from jax.experimental import pallas as pl
from jax.experimental.pallas import tpu as pltpu
from jax.experimental.pallas import tpu_sc as plsc

F32 = jnp.float32
BF16 = jnp.bfloat16

D_MODEL = 1024
N_DEV = 8
CHUNK = 64
HEADS = 4
DK = 128
DV = 256
QK = HEADS * DK
GV = HEADS * DV
RANK = 16
GROUPS = 4
SBLOCK = 128
DG = 256
D_FF = 2816
FF_BLK = 704
EPS = 1e-6
Q_SCALE = DK ** -0.5
LANES = 128
VMEM_BIG = 48 * 1024 * 1024

D_IN = 7184
IN_BLK = 898
A_COLS = (0, 3072)
LOW_COLS = (3072, 3088)
S_COLS = (3088, 5136)
G_COLS = (5136, 7184)

ADAM_LR = 0.001
ADAM_B1 = 0.9
ADAM_B2 = 0.999
ADAM_EPS = 1e-08
ADAM_WD = 0.01
ADAM_STEP = 10

MESH = pl.DeviceIdType.MESH
ANY = pl.BlockSpec(memory_space=pl.ANY)
VMEM_SPEC = pl.BlockSpec(memory_space=pltpu.VMEM)


def _cp(sem=None, vmem=None):
    return pltpu.CompilerParams(dimension_semantics=sem, vmem_limit_bytes=vmem)


def _hbm(*arrays):
    return [pltpu.with_memory_space_constraint(a, pltpu.HBM) for a in arrays]


def _sig(x):
    return 0.5 * jnp.tanh(0.5 * x) + 0.5


GELU_C = 0.7978845608028654
GELU_A = 0.044715


def _gelu(x):
    t = jnp.tanh((GELU_C * x) * (1.0 + GELU_A * (x * x)))
    return (0.5 * x) * (1.0 + t)


def _gelu_and_grad(x):
    x2 = x * x
    t = jnp.tanh((GELU_C * x) * (1.0 + GELU_A * x2))
    one_t = 1.0 + t
    hx = 0.5 * x
    grad = 0.5 * one_t + (hx * (1.0 - t * t)) * (GELU_C + (3.0 * GELU_A * GELU_C) * x2)
    return hx * one_t, grad


def _logsig(x):
    return jnp.minimum(x, 0.0) - jnp.log1p(jnp.exp(-jnp.abs(x)))


def _dot(a, b, dims):
    return lax.dot_general(a, b, (dims, ((), ())), preferred_element_type=F32)


NN = ((1,), (0,))
NT = ((1,), (1,))
TN = ((0,), (0,))


def _exact_mask_dot(mask_bf16, x):
    hi = x.astype(BF16)
    r1 = x - hi.astype(F32)
    mid = r1.astype(BF16)
    lo = (r1 - mid.astype(F32)).astype(BF16)
    return _dot(mask_bf16, hi, NN) + _dot(mask_bf16, mid, NN) + _dot(mask_bf16, lo, NN)


ROW_TILE = 512
SUB_ROWS = 256


def _sub_tiles(T):
    return [pl.ds(r0, min(SUB_ROWS, T)) for r0 in range(0, T, SUB_ROWS)]


def _rt(T, W, c=0):
    return pl.BlockSpec((T, W), lambda i: (i, c))


def _rt3(nb, T, W):
    return pl.BlockSpec((nb, T, W), lambda i: (0, i, 0))


def _res(shape):
    nd = len(shape)
    return pl.BlockSpec(tuple(shape), lambda i: (0,) * nd, pipeline_mode=pl.Buffered(1))


def _acc(shape):
    nd = len(shape)
    return pl.BlockSpec(tuple(shape), lambda i: (0,) * nd, pipeline_mode=pl.Buffered(1))


def _nbytes(shape, dtype):
    n = jnp.dtype(dtype).itemsize
    for s in shape:
        n *= s
    return n


def _vmem_limit(tiles, resident, temps=16 * 1024 * 1024):
    need = 2 * sum(_nbytes(s, d) for s, d in tiles) + sum(_nbytes(s, d) for s, d in resident) + temps
    return min(need, 60 * 1024 * 1024)


def _tok_call(body, *, name, S, T, ins, outs, semantics="parallel"):
    tiles = [(spec.block_shape, a.dtype) for a, spec, kind in ins + outs if kind == "tile"]
    resident = [(a.shape, a.dtype) for a, spec, kind in ins + outs if kind == "res"]
    return pl.pallas_call(
        body, name=name, grid=(S // T,),
        in_specs=[spec for _, spec, _ in ins], out_specs=[spec for _, spec, _ in outs],
        out_shape=[pltpu.HBM(a.shape, a.dtype) for a, _, _ in outs],
        compiler_params=_cp((semantics,), _vmem_limit(tiles, resident)),
    )(*_hbm(*[a for a, _, _ in ins]))


def _tile(a, spec):
    return (a, spec, "tile")


def _whole(a):
    return (a, _res(a.shape), "res")


def _out_tile(shape, dtype, spec):
    return (jax.ShapeDtypeStruct(shape, dtype), spec, "tile")


def _out_acc(shape, dtype=F32):
    return (jax.ShapeDtypeStruct(shape, dtype), _acc(shape), "res")


def _rms_stats(x):
    r = lax.rsqrt(jnp.mean(x * x, axis=-1, keepdims=True) + EPS)
    return r, x * r


def _rms_bwd(xh, r, g, dy):
    dxh = dy * g
    dx = r * (dxh - xh * jnp.mean(dxh * xh, axis=-1, keepdims=True))
    dg = jnp.sum(dy * xh, axis=0, keepdims=True)
    return dx, dg


def _accum(ref, val):
    @pl.when(pl.program_id(0) == 0)
    def _():
        ref[...] = val

    @pl.when(pl.program_id(0) > 0)
    def _():
        ref[...] += val


def _dot_rows_t(a, w_ref, row0, o_ref, chunk=1024):
    n = o_ref.shape[1]
    for n0 in range(0, n, chunk):
        n1 = min(n, n0 + chunk)
        o_ref[:, n0:n1] = _dot(a, w_ref[row0 + n0:row0 + n1, :], NT).astype(o_ref.dtype)


def _in_proj_call(x, g1, w_blocks, *, name, T=ROW_TILE):
    S = x.shape[0]

    def body(x_ref, g_ref, blk_ref, a_ref, pa_ref, al_ref, ps_ref, pg_ref, w_ref):
        @pl.when(pl.program_id(0) == 0)
        def _():
            for d in range(N_DEV):
                w_ref[d * IN_BLK:(d + 1) * IN_BLK, :] = blk_ref[d]

        _, xh = _rms_stats(x_ref[...])
        a = (xh * g_ref[...]).astype(BF16)
        a_ref[...] = a
        _dot_rows_t(a, w_ref, A_COLS[0], pa_ref)
        _dot_rows_t(a, w_ref, LOW_COLS[0], al_ref)
        _dot_rows_t(a, w_ref, S_COLS[0], ps_ref)
        _dot_rows_t(a, w_ref, G_COLS[0], pg_ref)

    widths = (D_MODEL, A_COLS[1] - A_COLS[0], LANES, S_COLS[1] - S_COLS[0], G_COLS[1] - G_COLS[0])
    return _tok_call(
        body, name=name, S=S, T=T, semantics="arbitrary",
        ins=[_tile(x, _rt(T, D_MODEL)), _whole(g1), _whole(w_blocks)],
        outs=[_out_tile((S, w), BF16, _rt(T, w)) for w in widths] + [_out_acc((D_IN, D_MODEL), BF16)])


def _mixer_tail_call(y_gla, y_sgu, pg, x, w_bg, w_bs, w_out, g2, g3, *, name, T=ROW_TILE):
    S = x.shape[0]

    def body(yg_ref, ys_ref, pg_ref, x_ref, wbg_ref, wbs_ref, wo_ref, g2_ref, g3_ref,
             t1_ref, t2_ref, mg_ref, mix_ref, x1_ref, h_ref):
        for rows in _sub_tiles(T):
            t1 = _dot(yg_ref[rows, :], wbg_ref[...], NN)
            t2 = _dot(ys_ref[rows, :], wbs_ref[...], NN)
            t1_ref[rows, :] = t1.astype(BF16)
            t2_ref[rows, :] = t2.astype(BF16)
            sg = _sig(pg_ref[rows, pl.ds(0, D_MODEL)].astype(F32))
            ss = _sig(pg_ref[rows, pl.ds(D_MODEL, D_MODEL)].astype(F32))
            merged = (sg * t1 + ss * t2).astype(BF16)
            mg_ref[rows, :] = merged
            mix = _dot(merged, wo_ref[...], NN)
            mix_ref[rows, :] = mix
            _, mh = _rms_stats(mix)
            x1 = x_ref[rows, :] + mh * g2_ref[...]
            x1_ref[rows, :] = x1
            _, xh = _rms_stats(x1)
            h_ref[rows, :] = (xh * g3_ref[...]).astype(BF16)

    row = _rt(T, D_MODEL)
    b16 = lambda: _out_tile((S, D_MODEL), BF16, row)
    f32 = lambda: _out_tile((S, D_MODEL), F32, row)
    return _tok_call(
        body, name=name, S=S, T=T,
        ins=[_tile(y_gla, row), _tile(y_sgu, row), _tile(pg, _rt(T, 2 * D_MODEL)), _tile(x, row),
             _whole(w_bg), _whole(w_bs), _whole(w_out), _whole(g2), _whole(g3)],
        outs=[b16(), b16(), b16(), f32(), f32(), b16()])


FF_CHUNKS = ((0, 1024), (1024, 2048), (2048, D_FF))


def _ffn_in_call(h, wfi, *, name, T=ROW_TILE):
    S = h.shape[0]

    def body(h_ref, w_ref, gu_ref, z_ref):
        hv = h_ref[...]
        for c0, c1 in FF_CHUNKS:
            gate = _dot(hv, w_ref[c0:c1, :], NT)
            up = _dot(hv, w_ref[D_FF + c0:D_FF + c1, :], NT)
            gu_ref[:, c0:c1] = gate.astype(BF16)
            gu_ref[:, D_FF + c0:D_FF + c1] = up.astype(BF16)
            z_ref[:, c0:c1] = (gate * _sig(gate) * up).astype(BF16)

    return _tok_call(
        body, name=name, S=S, T=T,
        ins=[_tile(h, _rt(T, D_MODEL)), _whole(wfi)],
        outs=[_out_tile((S, 2 * D_FF), BF16, _rt(T, 2 * D_FF)),
              _out_tile((S, D_FF), BF16, _rt(T, D_FF))])


def _ffn_out_loss_call(z, wfo, x1, target, g4, *, name, T=ROW_TILE):
    S = x1.shape[0]

    def body(z_ref, w_ref, x1_ref, t_ref, g4_ref, loss_ref, dx2_ref, dy_ref, dg4_ref):
        loss = jnp.zeros((1, 1), F32)
        dg4 = jnp.zeros((1, D_MODEL), F32)
        for rows in _sub_tiles(T):
            y = _dot(z_ref[rows, :], w_ref[...], NN)
            r, yh = _rms_stats(y)
            diff = x1_ref[rows, :] + yh * g4_ref[...] - t_ref[rows, :]
            loss = loss + 0.5 * jnp.sum(jnp.mean(diff * diff, axis=-1, keepdims=True), axis=0, keepdims=True)
            dx2 = diff * (1.0 / D_MODEL)
            dx2_ref[rows, :] = dx2
            dy, dg = _rms_bwd(yh, r, g4_ref[...], dx2)
            dy_ref[rows, :] = dy.astype(BF16)
            dg4 = dg4 + dg
        _accum(loss_ref, jnp.broadcast_to(loss, (1, LANES)))
        _accum(dg4_ref, dg4)

    row = _rt(T, D_MODEL)
    return _tok_call(
        body, name=name, S=S, T=T, semantics="arbitrary",
        ins=[_tile(z, _rt(T, D_FF)), _whole(wfo), _tile(x1, row), _tile(target, row), _whole(g4)],
        outs=[_out_acc((1, LANES)), _out_tile((S, D_MODEL), F32, row), _out_tile((S, D_MODEL), BF16, row),
              _out_acc((1, D_MODEL))])


def _ffn_out_bwd_call(dy, wfo, gu, *, name, T=ROW_TILE):
    S = dy.shape[0]

    def body(dy_ref, w_ref, gu_ref, dgu_ref):
        dyv = dy_ref[...]
        for c0, c1 in FF_CHUNKS:
            dz = _dot(dyv, w_ref[c0:c1, :], NT).astype(BF16)
            gt = gu_ref[:, c0:c1]
            up = gu_ref[:, D_FF + c0:D_FF + c1]
            s = _sig(gt)
            dgu_ref[:, c0:c1] = dz * up * s * (1.0 + gt * (1.0 - s))
            dgu_ref[:, D_FF + c0:D_FF + c1] = dz * gt * s

    wide = _rt(T, 2 * D_FF)
    return _tok_call(
        body, name=name, S=S, T=T,
        ins=[_tile(dy, _rt(T, D_MODEL)), _whole(wfo), _tile(gu, wide)],
        outs=[_out_tile((S, 2 * D_FF), BF16, wide)])[0]


def _ffn_in_bwd_call(dgu, wfi, dx2, x1, mix, g3, g2, *, name, T=ROW_TILE):
    S = x1.shape[0]

    def body(dgu_ref, w_ref, dx2_ref, x1_ref, mix_ref, g3_ref, g2_ref, dx1_ref, dmix_ref, dg3_ref, dg2_ref):
        dg3 = jnp.zeros((1, D_MODEL), F32)
        dg2 = jnp.zeros((1, D_MODEL), F32)
        for rows in _sub_tiles(T):
            dh = _dot(dgu_ref[rows, :], w_ref[...], NN)
            r3, xh = _rms_stats(x1_ref[rows, :])
            d3, g = _rms_bwd(xh, r3, g3_ref[...], dh)
            dg3 = dg3 + g
            dx1 = dx2_ref[rows, :] + d3
            dx1_ref[rows, :] = dx1
            r2, mh = _rms_stats(mix_ref[rows, :])
            dmix, g = _rms_bwd(mh, r2, g2_ref[...], dx1)
            dg2 = dg2 + g
            dmix_ref[rows, :] = dmix.astype(BF16)
        _accum(dg3_ref, dg3)
        _accum(dg2_ref, dg2)

    row = _rt(T, D_MODEL)
    return _tok_call(
        body, name=name, S=S, T=T, semantics="arbitrary",
        ins=[_tile(dgu, _rt(T, 2 * D_FF)), _whole(wfi), _tile(dx2, row), _tile(x1, row), _tile(mix, row),
             _whole(g3), _whole(g2)],
        outs=[_out_tile((S, D_MODEL), F32, row), _out_tile((S, D_MODEL), BF16, row),
              _out_acc((1, D_MODEL)), _out_acc((1, D_MODEL))])


def _mixer_bwd_call(dmix, t1, t2, pg, w_out, w_bg, w_bs, *, name, T=ROW_TILE):
    S = dmix.shape[0]

    def body(dmix_ref, t1_ref, t2_ref, pg_ref, wo_ref, wbg_ref, wbs_ref, dt1_ref, dt2_ref, dpg_ref, dyg_ref, dys_ref):
        gg, gs = pl.ds(0, D_MODEL), pl.ds(D_MODEL, D_MODEL)
        for rows in _sub_tiles(T):
            dm = _dot(dmix_ref[rows, :], wo_ref[...], NT)
            sg = _sig(pg_ref[rows, gg].astype(F32))
            ss = _sig(pg_ref[rows, gs].astype(F32))
            dt1 = (dm * sg).astype(BF16)
            dt2 = (dm * ss).astype(BF16)
            dt1_ref[rows, :] = dt1
            dt2_ref[rows, :] = dt2
            dpg_ref[rows, gg] = (dm * t1_ref[rows, :].astype(F32) * sg * (1.0 - sg)).astype(BF16)
            dpg_ref[rows, gs] = (dm * t2_ref[rows, :].astype(F32) * ss * (1.0 - ss)).astype(BF16)
            dyg_ref[rows, :] = _dot(dt1, wbg_ref[...], NT).astype(BF16)
            dys_ref[rows, :] = _dot(dt2, wbs_ref[...], NT).astype(BF16)

    row = _rt(T, D_MODEL)
    wide = _rt(T, 2 * D_MODEL)
    b16 = lambda: _out_tile((S, D_MODEL), BF16, row)
    return _tok_call(
        body, name=name, S=S, T=T,
        ins=[_tile(dmix, row), _tile(t1, row), _tile(t2, row), _tile(pg, wide), _whole(w_out), _whole(w_bg), _whole(w_bs)],
        outs=[b16(), b16(), _out_tile((S, 2 * D_MODEL), BF16, wide), b16(), b16()])


def _in_proj_bwd_call(dpa, dlow, dps, dpg, w_in_t, x, dx1, g1, *, name, T=ROW_TILE):
    S = x.shape[0]

    def body(dpa_ref, dl_ref, dps_ref, dpg_ref, w_ref, x_ref, dx1_ref, g1_ref, gx_ref, dg1_ref):
        da = (_dot(dpa_ref[...], w_ref[A_COLS[0]:A_COLS[1], :], NN)
              + _dot(dl_ref[...], w_ref[LOW_COLS[0]:LOW_COLS[0] + LANES, :], NN)
              + _dot(dps_ref[...], w_ref[S_COLS[0]:S_COLS[1], :], NN)
              + _dot(dpg_ref[...], w_ref[G_COLS[0]:G_COLS[1], :], NN))
        r, xh = _rms_stats(x_ref[...])
        dxa, dg = _rms_bwd(xh, r, g1_ref[...], da)
        gx_ref[...] = dx1_ref[...] + dxa
        _accum(dg1_ref, dg)

    row = _rt(T, D_MODEL)
    return _tok_call(
        body, name=name, S=S, T=T, semantics="arbitrary",
        ins=[_tile(dpa, _rt(T, dpa.shape[1])), _tile(dlow, _rt(T, dlow.shape[1])), _tile(dps, _rt(T, dps.shape[1])),
             _tile(dpg, _rt(T, dpg.shape[1])), _whole(w_in_t), _tile(x, row), _tile(dx1, row), _whole(g1)],
        outs=[_out_tile((S, D_MODEL), F32, row), _out_acc((1, D_MODEL))])


TOKEN_TILE = 512


def _weight_grads_part(arrays, pairs, *, tk=TOKEN_TILE, out_dtype=BF16):
    S = arrays[0].shape[-2]
    tk = min(tk, S)
    n_in = len(arrays)

    def out_shape(i, j):
        a, b = arrays[i], arrays[j]
        if a.ndim == 3:
            return (a.shape[0], a.shape[2], b.shape[1])
        if b.ndim == 3:
            return (b.shape[0], a.shape[1], b.shape[2])
        return (a.shape[1], b.shape[1])

    shapes = [out_shape(i, j) for i, j in pairs]

    in_place = out_dtype == F32

    def body(ins, outs, accs):
        k = pl.program_id(0)
        if in_place:
            accs = outs

        @pl.when(k == 0)
        def _():
            for acc in accs:
                acc[...] = jnp.zeros_like(acc)

        for (i, j), acc in zip(pairs, accs):
            a_ref, b_ref = ins[i], ins[j]
            if len(a_ref.shape) == 3:
                for n in range(a_ref.shape[0]):
                    acc[n] += _dot(a_ref[n], b_ref[...], TN)
            elif len(b_ref.shape) == 3:
                a = a_ref[...]
                for n in range(b_ref.shape[0]):
                    acc[n] += _dot(a, b_ref[n], TN)
            else:
                b = b_ref[...]
                for m0 in range(0, a_ref.shape[1], 1024):
                    m1 = min(a_ref.shape[1], m0 + 1024)
                    acc[m0:m1, :] += _dot(a_ref[:, m0:m1], b, TN)

        if not in_place:
            @pl.when(k == S // tk - 1)
            def _():
                for out, acc in zip(outs, accs):
                    out[...] = acc[...].astype(out.dtype)

    def in_spec(a):
        if a.ndim == 3:
            return pl.BlockSpec((a.shape[0], tk, a.shape[2]), lambda k: (0, k, 0))
        return pl.BlockSpec((tk, a.shape[1]), lambda k: (k, 0))

    return dict(
        body=body, steps=S // tk, arrays=list(arrays),
        in_specs=[in_spec(a) for a in arrays],
        out_specs=[_acc(s) for s in shapes],
        out_shapes=[pltpu.HBM(s, out_dtype) for s in shapes],
        scratch=[] if in_place else [pltpu.VMEM(s, F32) for s in shapes],
        tiles=[(in_spec(a).block_shape, a.dtype) for a in arrays],
        resident=[(s, F32) for s in shapes] + ([] if in_place else [(s, BF16) for s in shapes]))


def _weight_grads_call(arrays, pairs, *, name, tk=TOKEN_TILE):
    part = _weight_grads_part(arrays, pairs, tk=tk)
    n_in, n_out = len(part["arrays"]), len(part["out_shapes"])

    def body(*refs):
        part["body"](refs[:n_in], refs[n_in:n_in + n_out], refs[n_in + n_out:])

    return pl.pallas_call(
        body, name=name, grid=(part["steps"],),
        in_specs=part["in_specs"], out_specs=part["out_specs"], out_shape=part["out_shapes"],
        scratch_shapes=part["scratch"],
        compiler_params=_cp(("arbitrary",), _vmem_limit(part["tiles"], part["resident"])),
    )(*_hbm(*part["arrays"]))


def _ffn_in_grad_call(h, dgu, *, name, tk=TOKEN_TILE):
    S = h.shape[0]
    tk = min(tk, S)
    shape = (D_FF, D_MODEL)

    def body(h_ref, dgu_ref, out_ref, acc):
        k = pl.program_id(1)

        @pl.when(k == 0)
        def _():
            acc[...] = jnp.zeros_like(acc)

        hv = h_ref[...]
        for c0, c1 in FF_CHUNKS:
            acc[c0:c1, :] += _dot(dgu_ref[:, c0:c1], hv, TN)

        @pl.when(k == S // tk - 1)
        def _():
            out_ref[...] = acc[...].astype(out_ref.dtype)

    tiles = [((tk, D_MODEL), BF16), ((tk, D_FF), BF16), (shape, BF16)]
    return pl.pallas_call(
        body, name=name, grid=(2, S // tk),
        in_specs=[pl.BlockSpec((tk, D_MODEL), lambda g, k: (k, 0)),
                  pl.BlockSpec((tk, D_FF), lambda g, k: (k, g))],
        out_specs=pl.BlockSpec(shape, lambda g, k: (g, 0)),
        out_shape=pltpu.HBM((2 * D_FF, D_MODEL), BF16),
        scratch_shapes=[pltpu.VMEM(shape, F32)],
        compiler_params=_cp(("parallel", "arbitrary"), _vmem_limit(tiles, [(shape, F32)])),
    )(*_hbm(h, dgu))


GLA_TILE = 256
Q_OFF, K_OFF, V_OFF, R_OFF = 0, QK, 2 * QK, 2 * QK + GV


def _tri(lower):
    r = lax.broadcasted_iota(jnp.int32, (CHUNK, CHUNK), 0)
    c = lax.broadcasted_iota(jnp.int32, (CHUNK, CHUNK), 1)
    return jnp.where((r >= c) if lower else (c >= r), 1.0, 0.0).astype(BF16)


def _gla_fwd_call(pa, alow, wg, bgate, gnorm, *, name):
    S = pa.shape[0]
    Tg = min(GLA_TILE, S)
    cb = Tg // CHUNK

    def body(pa_ref, al_ref, wg_ref, bg_ref, gn_ref, y_ref, st_ref, state):
        @pl.when(pl.program_id(0) == 0)
        def _():
            state[...] = jnp.zeros_like(state)

        logit = _dot(al_ref[...], wg_ref[...], NN) + bg_ref[...]
        ls = _logsig(logit) * (1.0 / 16.0)
        tri = _tri(True)
        for c in range(cb):
            rows = pl.ds(c * CHUNK, CHUNK)
            cum = _exact_mask_dot(tri, ls[c * CHUNK:(c + 1) * CHUNK])
            tot = cum[CHUNK - 1:CHUNK]
            kd = (pa_ref[rows, pl.ds(K_OFF, QK)].astype(F32) * jnp.exp(tot - cum)).astype(BF16)
            decay = jnp.exp(tot)
            for h in range(HEADS):
                lanes = slice(h * DK, (h + 1) * DK)
                new = state[h] * decay[:, lanes] + _dot(pa_ref[rows, pl.ds(V_OFF + h * DV, DV)], kd[:, lanes], TN)
                state[h] = new
                st_ref[c, h] = new
        for c in range(cb):
            rows = pl.ds(c * CHUNK, CHUNK)
            for h in range(HEADS):
                qs = (pa_ref[rows, pl.ds(Q_OFF + h * DK, DK)].astype(F32) * Q_SCALE).astype(BF16)
                o = _dot(qs, st_ref[c, h].astype(BF16), NT)
                rs = lax.rsqrt(jnp.mean(o * o, axis=-1, keepdims=True) + EPS)
                rr = pa_ref[rows, pl.ds(R_OFF + h * DV, DV)].astype(F32)
                y_ref[rows, pl.ds(h * DV, DV)] = (o * rs * gn_ref[h] * (rr * _sig(rr))).astype(BF16)

    return pl.pallas_call(
        body, name=name, grid=(S // Tg,),
        in_specs=[
            pl.BlockSpec((Tg, 2 * QK + 2 * GV), lambda t: (t, 0)),
            pl.BlockSpec((Tg, LANES), lambda t: (t, 0)),
            pl.BlockSpec((LANES, QK), lambda t: (0, 0)),
            pl.BlockSpec((1, QK), lambda t: (0, 0)),
            pl.BlockSpec((HEADS, 1, DV), lambda t: (0, 0, 0)),
        ],
        out_specs=[
            pl.BlockSpec((Tg, GV), lambda t: (t, 0)),
            pl.BlockSpec((cb, HEADS, DV, DK), lambda t: (t, 0, 0, 0)),
        ],
        out_shape=[pltpu.HBM((S, GV), BF16), pltpu.HBM((S // CHUNK, HEADS, DV, DK), F32)],
        scratch_shapes=[pltpu.VMEM((HEADS, DV, DK), F32)],
        compiler_params=_cp(("arbitrary",), VMEM_BIG),
    )(*_hbm(pa, alow, wg, bgate, gnorm))


def _gla_bwd_call(pa, alow, wg, bgate, gnorm, states, dy, *, name):
    S = pa.shape[0]
    Tg = min(GLA_TILE, S)
    cb = Tg // CHUNK
    nt = S // Tg

    def rev(t):
        return nt - 1 - t

    def body(pa_ref, al_ref, wg_ref, bg_ref, gn_ref, st_ref, prev_ref, dy_ref,
             dpa_ref, dlow_ref, dwg_ref, dgn_ref, dbg_ref, carry, pbuf, dlbuf):
        t = pl.program_id(0)

        @pl.when(t == 0)
        def _():
            carry[...] = jnp.zeros_like(carry)
            dwg_ref[...] = jnp.zeros_like(dwg_ref)
            dgn_ref[...] = jnp.zeros_like(dgn_ref)
            dbg_ref[...] = jnp.zeros_like(dbg_ref)

        logit = _dot(al_ref[...], wg_ref[...], NN) + bg_ref[...]
        ls = _logsig(logit) * (1.0 / 16.0)
        sneg = 1.0 / (1.0 + jnp.exp(logit))
        tri = _tri(True)
        upper = _tri(False)
        first_tile = rev(t) == 0
        heads = range(HEADS)

        w, decay, kd = [], [], []
        for c in range(cb):
            rows = pl.ds(c * CHUNK, CHUNK)
            cum = _exact_mask_dot(tri, ls[c * CHUNK:(c + 1) * CHUNK])
            tot = cum[CHUNK - 1:CHUNK]
            w.append(jnp.exp(tot - cum))
            decay.append(jnp.exp(tot))
            kd.append(pa_ref[rows, pl.ds(K_OFF, QK)].astype(F32) * w[c])

        dgn = [jnp.zeros((1, DV), F32) for _ in heads]
        for c in range(cb):
            rows = pl.ds(c * CHUNK, CHUNK)
            for h in heads:
                gn = gn_ref[h]
                qs = (pa_ref[rows, pl.ds(Q_OFF + h * DK, DK)].astype(F32) * Q_SCALE).astype(BF16)
                st16 = st_ref[c, h].astype(BF16)
                o = _dot(qs, st16, NT)
                rs = lax.rsqrt(jnp.mean(o * o, axis=-1, keepdims=True) + EPS)
                oh = o * rs
                rr = pa_ref[rows, pl.ds(R_OFF + h * DV, DV)].astype(F32)
                sr = _sig(rr)
                dyv = dy_ref[rows, pl.ds(h * DV, DV)].astype(F32)
                dpa_ref[rows, pl.ds(R_OFF + h * DV, DV)] = (
                    dyv * oh * gn * sr * (1.0 + rr * (1.0 - sr))).astype(BF16)
                don = dyv * (rr * sr)
                dgn[h] = dgn[h] + jnp.sum(don * oh, axis=0, keepdims=True)
                doh = don * gn
                do16 = (rs * (doh - oh * jnp.mean(doh * oh, axis=-1, keepdims=True))).astype(BF16)
                dpa_ref[rows, pl.ds(Q_OFF + h * DK, DK)] = (_dot(do16, st16, NN) * Q_SCALE).astype(BF16)
                pbuf[c, h] = _dot(do16, qs, TN)
        for h in heads:
            dgn_ref[h] += dgn[h]

        dkd = [[None] * HEADS for _ in range(cb)]
        ddecay = [[None] * HEADS for _ in range(cb)]
        for c in reversed(range(cb)):
            rows = pl.ds(c * CHUNK, CHUNK)
            for h in heads:
                lanes = slice(h * DK, (h + 1) * DK)
                gt = pbuf[c, h] + carry[h]
                gt16 = gt.astype(BF16)
                dkd[c][h] = _dot(pa_ref[rows, pl.ds(V_OFF + h * DV, DV)], gt16, NN)
                dpa_ref[rows, pl.ds(V_OFF + h * DV, DV)] = _dot(kd[c][:, lanes].astype(BF16), gt16, NT).astype(BF16)
                if c > 0:
                    st_prev = st_ref[c - 1, h]
                else:
                    st_prev = jnp.where(first_tile, 0.0, prev_ref[0, h])
                ddecay[c][h] = jnp.sum(gt * st_prev, axis=0, keepdims=True)
                carry[h] = gt * decay[c][:, lanes]

        dbg = jnp.zeros((1, QK), F32)
        for c in range(cb):
            rows = pl.ds(c * CHUNK, CHUNK)
            dkd_c = jnp.concatenate(dkd[c], axis=1)
            dpa_ref[rows, pl.ds(K_OFF, QK)] = (dkd_c * w[c]).astype(BF16)
            e = dkd_c * kd[c]
            dtot = jnp.sum(e, axis=0, keepdims=True) + jnp.concatenate(ddecay[c], axis=1) * decay[c]
            dls = dtot - _exact_mask_dot(upper, e)
            dlogit = dls * (1.0 / 16.0) * sneg[c * CHUNK:(c + 1) * CHUNK]
            dlbuf[rows, :] = dlogit.astype(BF16)
            dbg = dbg + jnp.sum(dlogit, axis=0, keepdims=True)
        dbg_ref[...] += dbg

        dl16 = dlbuf[...]
        dlow_ref[...] = _dot(dl16, wg_ref[...], NT).astype(BF16)
        dwg_ref[...] += _dot(al_ref[...], dl16, TN)

    wide = 2 * QK + 2 * GV
    tiles = [((Tg, wide), BF16), ((cb + 1, HEADS, DV, DK), F32), ((Tg, GV), BF16), ((Tg, wide), BF16),
             ((Tg, QK), BF16)]
    resident = [((cb + 1, HEADS, DV, DK), F32)]
    return pl.pallas_call(
        body, name=name, grid=(nt,),
        in_specs=[
            pl.BlockSpec((Tg, wide), lambda t: (rev(t), 0)),
            pl.BlockSpec((Tg, LANES), lambda t: (rev(t), 0)),
            pl.BlockSpec((LANES, QK), lambda t: (0, 0)),
            pl.BlockSpec((1, QK), lambda t: (0, 0)),
            pl.BlockSpec((HEADS, 1, DV), lambda t: (0, 0, 0)),
            pl.BlockSpec((cb, HEADS, DV, DK), lambda t: (rev(t), 0, 0, 0)),
            pl.BlockSpec((1, HEADS, DV, DK), lambda t: (jnp.maximum(rev(t) * cb - 1, 0), 0, 0, 0)),
            pl.BlockSpec((Tg, GV), lambda t: (rev(t), 0)),
        ],
        out_specs=[
            pl.BlockSpec((Tg, wide), lambda t: (rev(t), 0)),
            pl.BlockSpec((Tg, LANES), lambda t: (rev(t), 0)),
            pl.BlockSpec((LANES, QK), lambda t: (0, 0)),
            pl.BlockSpec((HEADS, 1, DV), lambda t: (0, 0, 0)),
            pl.BlockSpec((1, QK), lambda t: (0, 0)),
        ],
        out_shape=[pltpu.HBM((S, wide), BF16), pltpu.HBM((S, LANES), BF16), jax.ShapeDtypeStruct((LANES, QK), F32),
                   jax.ShapeDtypeStruct((HEADS, 1, DV), F32), jax.ShapeDtypeStruct((1, QK), F32)],
        scratch_shapes=[pltpu.VMEM((HEADS, DV, DK), F32), pltpu.VMEM((cb, HEADS, DV, DK), F32),
                        pltpu.VMEM((Tg, QK), BF16)],
        compiler_params=_cp(("arbitrary",), _vmem_limit(tiles, resident)),
    )(*_hbm(pa, alow, wg, bgate, gnorm, states, states, dy))


SGU_TILE = 256


def _sgu_mask():
    r = lax.broadcasted_iota(jnp.int32, (SBLOCK, SBLOCK), 0)
    c = lax.broadcasted_iota(jnp.int32, (SBLOCK, SBLOCK), 1)
    return (c < CHUNK) | (r >= CHUNK)


def _ln_stats(vf):
    mu = jnp.mean(vf, axis=-1, keepdims=True)
    xc = vf - mu
    rs = lax.rsqrt(jnp.mean(xc * xc, axis=-1, keepdims=True) + EPS)
    return rs, xc * rs


def _sgu_fwd_call(ps, ln_g, ln_b, w_sp, b_sp, *, name):
    S = ps.shape[0]
    Ts = min(SGU_TILE, S)

    def body(ps_ref, lg_ref, lb_ref, w_ref, b_ref, y_ref):
        mask = _sgu_mask()
        for g in range(GROUPS):
            wm = jnp.where(mask, w_ref[g], 0.0).astype(BF16)
            for p in range(Ts // SBLOCK):
                rows = pl.ds(p * SBLOCK, SBLOCK)
                u = _gelu(ps_ref[rows, pl.ds(g * DG, DG)].astype(F32))
                _, xh = _ln_stats(_gelu(ps_ref[rows, pl.ds(D_MODEL + g * DG, DG)].astype(F32)))
                vn = xh * lg_ref[g] + lb_ref[g]
                mixed = _dot(wm, vn.astype(BF16), NN) + b_ref[g]
                y_ref[rows, pl.ds(g * DG, DG)] = (u * mixed).astype(BF16)

    full3 = lambda a, b, c: pl.BlockSpec((a, b, c), lambda t: (0, 0, 0))
    return pl.pallas_call(
        body, name=name, grid=(S // Ts,),
        in_specs=[pl.BlockSpec((Ts, 2 * D_MODEL), lambda t: (t, 0)),
                  full3(GROUPS, 1, DG), full3(GROUPS, 1, DG), full3(GROUPS, SBLOCK, SBLOCK), full3(GROUPS, SBLOCK, 1)],
        out_specs=pl.BlockSpec((Ts, D_MODEL), lambda t: (t, 0)),
        out_shape=pltpu.HBM((S, D_MODEL), BF16),
        compiler_params=_cp(("parallel",)),
    )(*_hbm(ps, ln_g, ln_b, w_sp, b_sp))


def _sgu_bwd_call(ps, ln_g, ln_b, w_sp, b_sp, dy, *, name):
    S = ps.shape[0]
    Ts = min(SGU_TILE, S)

    def body(ps_ref, lg_ref, lb_ref, w_ref, b_ref, dy_ref, ds_ref, dlg_ref, dlb_ref, dw_ref, db_ref):
        @pl.when(pl.program_id(0) == 0)
        def _():
            dlg_ref[...] = jnp.zeros_like(dlg_ref)
            dlb_ref[...] = jnp.zeros_like(dlb_ref)
            dw_ref[...] = jnp.zeros_like(dw_ref)
            db_ref[...] = jnp.zeros_like(db_ref)

        mask = _sgu_mask()
        for g in range(GROUPS):
            wm = jnp.where(mask, w_ref[g], 0.0).astype(BF16)
            lg = lg_ref[g]
            for p in range(Ts // SBLOCK):
                rows = pl.ds(p * SBLOCK, SBLOCK)
                su = ps_ref[rows, pl.ds(g * DG, DG)].astype(F32)
                sv = ps_ref[rows, pl.ds(D_MODEL + g * DG, DG)].astype(F32)
                u, du = _gelu_and_grad(su)
                gv, dgv = _gelu_and_grad(sv)
                rs, xh = _ln_stats(gv)
                vn16 = (xh * lg + lb_ref[g]).astype(BF16)
                mixed = _dot(wm, vn16, NN) + b_ref[g]
                dyv = dy_ref[rows, pl.ds(g * DG, DG)].astype(F32)
                ds_ref[rows, pl.ds(g * DG, DG)] = (dyv * mixed * du).astype(BF16)
                dmix = dyv * u
                dmix16 = dmix.astype(BF16)
                db_ref[g] += jnp.sum(dmix, axis=-1, keepdims=True)
                dw_ref[g] += jnp.where(mask, _dot(dmix16, vn16, NT), 0.0)
                dvn = _dot(wm, dmix16, TN)
                dlg_ref[g] += jnp.sum(dvn * xh, axis=0, keepdims=True)
                dlb_ref[g] += jnp.sum(dvn, axis=0, keepdims=True)
                dxh = dvn * lg
                dvf = rs * (dxh - jnp.mean(dxh, axis=-1, keepdims=True)
                            - xh * jnp.mean(dxh * xh, axis=-1, keepdims=True))
                ds_ref[rows, pl.ds(D_MODEL + g * DG, DG)] = (dvf * dgv).astype(BF16)

    full3 = lambda a, b, c: pl.BlockSpec((a, b, c), lambda t: (0, 0, 0))
    tiles = [((Ts, 2 * D_MODEL), BF16), ((Ts, D_MODEL), BF16), ((Ts, 2 * D_MODEL), BF16)]
    return pl.pallas_call(
        body, name=name, grid=(S // Ts,),
        in_specs=[pl.BlockSpec((Ts, 2 * D_MODEL), lambda t: (t, 0)),
                  full3(GROUPS, 1, DG), full3(GROUPS, 1, DG), full3(GROUPS, SBLOCK, SBLOCK), full3(GROUPS, SBLOCK, 1),
                  pl.BlockSpec((Ts, D_MODEL), lambda t: (t, 0))],
        out_specs=[pl.BlockSpec((Ts, 2 * D_MODEL), lambda t: (t, 0)),
                   full3(GROUPS, 1, DG), full3(GROUPS, 1, DG), full3(GROUPS, SBLOCK, SBLOCK), full3(GROUPS, SBLOCK, 1)],
        out_shape=[pltpu.HBM((S, 2 * D_MODEL), BF16),
                   jax.ShapeDtypeStruct((GROUPS, 1, DG), F32), jax.ShapeDtypeStruct((GROUPS, 1, DG), F32),
                   jax.ShapeDtypeStruct((GROUPS, SBLOCK, SBLOCK), F32),
                   jax.ShapeDtypeStruct((GROUPS, SBLOCK, 1), F32)],
        compiler_params=_cp(("arbitrary",), _vmem_limit(tiles, [])),
    )(*_hbm(ps, ln_g, ln_b, w_sp, b_sp, dy))


def _position():
    return lax.axis_index("x"), lax.axis_index("y"), lax.axis_index("c")


def _gather_copies(srcs, dsts, send_sems, recv_sems, local_sems):
    x, y, c = _position()
    me, sibling = (x, y, c), (x, y, 1 - c)
    chips = [(1 - x, y), (x, 1 - y), (1 - x, 1 - y)]
    n = len(srcs)

    def slab(a, block):
        px, py, pc = block
        return dsts[a].at[4 * px + 2 * py + pc]

    def copy(a, k, block, to, src=None):
        return pltpu.make_async_remote_copy(
            src_ref=slab(a, block) if src is None else src, dst_ref=slab(a, block),
            send_sem=send_sems.at[7 * a + k], recv_sem=recv_sems.at[7 * a + k], device_id=to, device_id_type=MESH)

    mine = [pltpu.make_async_copy(srcs[a], slab(a, me), local_sems.at[a]) for a in range(n)]
    for cp in mine:
        cp.start()
    first = []
    for a in range(n):
        first.append(copy(a, 0, me, sibling, src=srcs[a]))
        first += [copy(a, 1 + j, me, (*chip, c), src=srcs[a]) for j, chip in enumerate(chips)]
    for cp in first:
        cp.start()
    passed = []
    for j, chip in enumerate(chips):
        for a in range(n):
            copy(a, 1 + j, (*chip, c), me).wait_recv()
            cp = copy(a, 4 + j, (*chip, c), sibling)
            cp.start()
            passed.append(cp)
    for a in range(n):
        copy(a, 0, sibling, me).wait_recv()
        for j, chip in enumerate(chips):
            copy(a, 4 + j, (*chip, 1 - c), me).wait_recv()
    for cp in first + passed:
        cp.wait_send()
    for cp in mine:
        cp.wait()


def _gather_copies_relayed(srcs, dsts, send_sems, recv_sems, local_sems, waves=1):
    x, y, c = _position()
    me, sibling = (x, y, c), (x, y, 1 - c)
    x_chip, y_chip, d_chip = (1 - x, y), (x, 1 - y), (1 - x, 1 - y)
    south = c == 0
    relay_from = (jnp.where(south, x, 1 - x), jnp.where(south, 1 - y, y), c)
    relay_to = (jnp.where(south, 1 - x, x), jnp.where(south, y, 1 - y), c)
    n = len(srcs)

    def cut(a, w):
        rows = srcs[a].shape[0]
        step = (rows // waves) // 16 * 16
        if waves == 1 or step == 0:
            return pl.ds(0, rows) if w == 0 else None
        return pl.ds(w * step, step if w < waves - 1 else rows - w * step)

    def slab(a, block, w):
        px, py, pc = block
        return dsts[a].at[4 * px + 2 * py + pc, cut(a, w)]

    def copy(a, k, w, block, to, own=False):
        sem = (7 * a + k) * waves + w
        return pltpu.make_async_remote_copy(
            src_ref=srcs[a].at[cut(a, w)] if own else slab(a, block, w), dst_ref=slab(a, block, w),
            send_sem=send_sems.at[sem], recv_sem=recv_sems.at[sem], device_id=to, device_id_type=MESH)

    pieces = [(a, w) for w in range(waves) for a in range(n) if cut(a, w) is not None]
    mine = [pltpu.make_async_copy(srcs[a], dsts[a].at[4 * x + 2 * y + c], local_sems.at[a]) for a in range(n)]
    for cp in mine:
        cp.start()
    sent = []
    for a, w in pieces:
        sent += [copy(a, 0, w, me, sibling, own=True), copy(a, 1, w, me, (*x_chip, c), own=True),
                 copy(a, 2, w, me, (*y_chip, c), own=True)]
    for cp in sent:
        cp.start()
    for a, w in pieces:
        copy(a, 1, w, (*x_chip, c), me).wait_recv()
        copy(a, 2, w, (*y_chip, c), me).wait_recv()
        later = [copy(a, 3, w, relay_from, relay_to), copy(a, 4, w, (*x_chip, c), sibling),
                 copy(a, 5, w, (*y_chip, c), sibling)]
        for cp in later:
            cp.start()
        sent += later
    for a, w in pieces:
        copy(a, 3, w, (*d_chip, c), me).wait_recv()
        cp = copy(a, 6, w, (*d_chip, c), sibling)
        cp.start()
        sent.append(cp)
    for a, w in pieces:
        copy(a, 0, w, sibling, me).wait_recv()
        for k, chip in ((4, x_chip), (5, y_chip), (6, d_chip)):
            copy(a, k, w, (*chip, 1 - c), me).wait_recv()
    for cp in sent:
        cp.wait_send()
    for cp in mine:
        cp.wait()


def _all_gather_hbm(shards, *, name, waves=1):
    n = len(shards)

    def body(*refs):
        srcs, dsts = refs[:n], refs[n:2 * n]
        send_sems, recv_sems, local_sems = refs[2 * n:]
        _gather_copies_relayed(srcs, dsts, send_sems, recv_sems, local_sems, waves=waves)

    return pl.pallas_call(
        body, name=name,
        in_specs=[ANY] * n, out_specs=[ANY] * n,
        out_shape=[jax.ShapeDtypeStruct((N_DEV, *s.shape), s.dtype) for s in shards],
        scratch_shapes=_comm_sems(n, waves),
    )(*shards)


FLIPS = [(fx, fy, fc) for fx in (0, 1) for fy in (0, 1) for fc in (0, 1)][1:]


def _scatter_copies(srcs, dsts, send_sems, recv_sems, local_sems, waves=1):
    n = len(srcs)
    x, y, c = _position()
    me = 4 * x + 2 * y + c
    mine = [pltpu.make_async_copy(srcs[a].at[me], dsts[a].at[me], local_sems.at[a]) for a in range(n)]
    for cp in mine:
        cp.start()
    peers = []
    for fx, fy, fc in FLIPS:
        tx = 1 - x if fx else x
        ty = 1 - y if fy else y
        tc = 1 - c if fc else c
        peers.append(((tx, ty, tc), 4 * tx + 2 * ty + tc))
    copies = []
    for w in range(waves):
        wave = []
        for k, (peer_id, peer) in enumerate(peers):
            for a in range(n):
                rows = srcs[a].shape[1]
                step = rows if waves == 1 else (rows // waves) // 16 * 16
                r0 = w * step
                cut = pl.ds(r0, step if w < waves - 1 else rows - r0)
                sem = (7 * a + k) * waves + w
                cp = pltpu.make_async_remote_copy(
                    src_ref=srcs[a].at[peer, cut], dst_ref=dsts[a].at[me, cut],
                    send_sem=send_sems.at[sem], recv_sem=recv_sems.at[sem],
                    device_id=peer_id, device_id_type=MESH)
                cp.start()
                wave.append(cp)
        for cp in wave:
            cp.wait_send()
        copies += wave
    for cp in copies:
        cp.wait_recv()
    for cp in mine:
        cp.wait()


def _pair_copies(srcs, dsts, send_sems, recv_sems, local_sems):
    x, y, c = _position()
    copies = [pltpu.make_async_remote_copy(
        src_ref=srcs[a], dst_ref=dsts[a], send_sem=send_sems.at[a], recv_sem=recv_sems.at[a],
        device_id=(x, y, 1 - c), device_id_type=MESH) for a in range(len(srcs))]
    for cp in copies:
        cp.start()
    for cp in copies:
        cp.wait()


def _chip_scatter_copies(srcs, dsts, send_sems, recv_sems, local_sems, waves=1):
    n = len(srcs)
    x, y, c = _position()
    here = 2 * x + y
    mine = [pltpu.make_async_copy(srcs[a].at[here], dsts[a].at[here], local_sems.at[a]) for a in range(n)]
    for cp in mine:
        cp.start()
    peers = []
    for fx, fy in ((1, 0), (0, 1), (1, 1)):
        tx = 1 - x if fx else x
        ty = 1 - y if fy else y
        peers.append(((tx, ty, c), 2 * tx + ty))
    copies = []
    for w in range(waves):
        wave = []
        for k, (peer_id, peer) in enumerate(peers):
            for a in range(n):
                rows = srcs[a].shape[1]
                step = rows if waves == 1 else (rows // waves) // 16 * 16
                r0 = w * step
                cut = pl.ds(r0, step if w < waves - 1 else rows - r0)
                sem = (3 * a + k) * waves + w
                cp = pltpu.make_async_remote_copy(
                    src_ref=srcs[a].at[peer, cut], dst_ref=dsts[a].at[here, cut],
                    send_sem=send_sems.at[sem], recv_sem=recv_sems.at[sem],
                    device_id=peer_id, device_id_type=MESH)
                cp.start()
                wave.append(cp)
        for cp in wave:
            cp.wait_send()
        copies += wave
    for cp in copies:
        cp.wait_recv()
    for cp in mine:
        cp.wait()


def _comm_sems(n, waves=1):
    return [pltpu.SemaphoreType.DMA((7 * n * waves,)), pltpu.SemaphoreType.DMA((7 * n * waves,)),
            pltpu.SemaphoreType.DMA((n,))]


def _handshake(peers):
    barrier = pltpu.get_barrier_semaphore()
    for peer in peers:
        pl.semaphore_signal(barrier, inc=1, device_id=peer, device_id_type=MESH)
    pl.semaphore_wait(barrier, len(peers))


def _sequencer_call(arrays, out_types, copies_fn, peers_fn, *, name, collective_id, waves=1):
    n = len(arrays)
    srcs = [jax.new_ref(a, memory_space=pltpu.MemorySpace.HBM) for a in arrays]
    dsts = [jax.empty_ref(t, memory_space=pltpu.MemorySpace.HBM) for t in out_types]
    extra = {} if waves == 1 else {"waves": waves}

    @pl.kernel(mesh=plsc.ScalarSubcoreMesh(axis_name="sequencer", num_cores=1), name=name,
               scratch_types=_comm_sems(n, waves), compiler_params=pltpu.CompilerParams(collective_id=collective_id))
    def launch(send_sems, recv_sems, local_sems):
        _handshake(peers_fn())
        copies_fn(srcs, dsts, send_sems, recv_sems, local_sems, **extra)

    launch()
    return [d[...] for d in dsts]


def _all_other_devices():
    x, y, c = _position()
    return [(1 - x if fx else x, 1 - y if fy else y, 1 - c if fc else c) for fx, fy, fc in FLIPS]


def _gather_relay_peers():
    x, y, c = _position()
    return [(x, y, 1 - c), (1 - x, y, c), (x, 1 - y, c)]


def _gather_peers():
    x, y, c = _position()
    return [(x, y, 1 - c), (1 - x, y, c), (x, 1 - y, c), (1 - x, 1 - y, c)]


def _small_gather_async(shards, *, name, collective_id):
    return _sequencer_call(shards, [jax.ShapeDtypeStruct((N_DEV, *s.shape), s.dtype) for s in shards],
                           _gather_copies, _gather_peers, name=name, collective_id=collective_id)


def _sibling():
    x, y, c = _position()
    return [(x, y, 1 - c)]


def _same_core_of_other_chips():
    x, y, c = _position()
    return [(1 - x, y, c), (x, 1 - y, c), (1 - x, 1 - y, c)]


def _pair_exchange_async(parts, *, name, collective_id):
    return _sequencer_call(parts, [jax.ShapeDtypeStruct(p.shape, p.dtype) for p in parts],
                           _pair_copies, _sibling, name=name, collective_id=collective_id)


def _chip_scatter_async(parts, *, name, collective_id, waves=1):
    return _sequencer_call(parts, [jax.ShapeDtypeStruct(p.shape, p.dtype) for p in parts],
                           _chip_scatter_copies, _same_core_of_other_chips, name=name, collective_id=collective_id,
                           waves=waves)


def _pair_sum_call(mine, theirs, *, name, tw=D_MODEL):
    n, r, w = mine.shape

    def body(a_ref, b_ref, o_ref):
        o_ref[...] = (a_ref[...].astype(F32) + b_ref[...].astype(F32)).astype(o_ref.dtype)

    spec = pl.BlockSpec((None, r, tw), lambda i, j: (i, 0, j))
    return pl.pallas_call(
        body, name=name, grid=(n, w // tw), in_specs=[spec, spec], out_specs=spec,
        out_shape=pltpu.HBM(mine.shape, mine.dtype),
        compiler_params=_cp(("parallel", "parallel")),
    )(*_hbm(mine, theirs))


def _scatter_blocks_async(parts, *, name, collective_id, waves=1):
    return _sequencer_call(parts, [jax.ShapeDtypeStruct(p.shape, p.dtype) for p in parts],
                           _scatter_copies, _all_other_devices, name=name, collective_id=collective_id, waves=waves)


def _all_gather_async(shards, *, name, collective_id, waves=1):
    return _sequencer_call(shards, [jax.ShapeDtypeStruct((N_DEV, *s.shape), s.dtype) for s in shards],
                           _gather_copies_relayed, _gather_relay_peers, name=name, collective_id=collective_id,
                           waves=waves)


def _adamw_math(w, g, m, v):
    m = ADAM_B1 * m + (1.0 - ADAM_B1) * g
    v = ADAM_B2 * v + (1.0 - ADAM_B2) * (g * g)
    m_hat = m / (1.0 - ADAM_B1 ** ADAM_STEP)
    v_hat = v / (1.0 - ADAM_B2 ** ADAM_STEP)
    delta = -ADAM_LR * (m_hat / (jnp.sqrt(v_hat) + ADAM_EPS) + ADAM_WD * w)
    return delta, m, v


def _adamw_reduce_call(recv, w, m, v, *, name, T=128):
    R, W = w.shape
    n_parts = recv.shape[0]
    if R % T == 0:
        tr, tw = T, W
    elif (R // 2) % 16 == 0:
        tr, tw = R // 2, W
    else:
        tr, tw = R, 2 * LANES

    def body(p_ref, w_ref, m_ref, v_ref, g_out, d_out, m_out, v_out):
        g = p_ref[0].astype(F32)
        for d in range(1, n_parts):
            g = g + p_ref[d].astype(F32)
        g_out[...] = g
        d_out[...], m_out[...], v_out[...] = _adamw_math(w_ref[...], g, m_ref[...], v_ref[...])

    row = pl.BlockSpec((tr, tw), lambda i, j: (i, j))
    out = pltpu.HBM((R, W), F32)
    return pl.pallas_call(
        body, name=name, grid=(R // tr, W // tw),
        in_specs=[pl.BlockSpec((n_parts, tr, tw), lambda i, j: (0, i, j)), row, row, row],
        out_specs=[row] * 4, out_shape=[out] * 4,
        compiler_params=_cp(("parallel", "parallel"), VMEM_BIG),
    )(*_hbm(recv, w, m, v))


SMALL_EARLY = (("b_gate", 8), ("w_spatial", 512), ("b_spatial", 8), ("norm_post_mix", 8), ("norm_pre_ffn", 8),
               ("norm_post_ffn", 8), ("w_gate_up", 128), ("gla_norm", 64), ("sgu_ln_g", 64), ("sgu_ln_b", 64))
SMALL_EARLY_AT = {}
for _name, _rows in SMALL_EARLY:
    SMALL_EARLY_AT[_name] = sum(r for _, r in SMALL_EARLY[:len(SMALL_EARLY_AT)])
SMALL_EARLY_ROWS = sum(r for _, r in SMALL_EARLY)
SMALL_LATE_ROWS = 16


def _small_early_rows(grads):
    def rows(a, n_rows):
        a = a.reshape(-1, LANES)
        return jnp.pad(a, ((0, n_rows - a.shape[0]), (0, 0)))

    def device_major(a, n_rows):
        r, c = a.shape[0], a.shape[1] // N_DEV
        a = a.reshape(r, N_DEV, c).transpose(1, 0, 2)
        a = jnp.pad(a, ((0, 0), (0, n_rows // N_DEV - r), (0, LANES - c)))
        return a.reshape(n_rows, LANES)

    pieces = []
    for name, n_rows in SMALL_EARLY:
        g = grads[name]
        if name in SMALL_SHARDED:
            pieces.append(device_major(g.reshape(g.shape[0], -1) if g.ndim == 2 else g.reshape(g.shape[0], g.shape[-1]), n_rows))
        else:
            pieces.append(rows(g, n_rows))
    return jnp.concatenate(pieces, axis=0)


def _small_update_call(got_early, got_late, w, m, v, *, name):
    names = list(SMALL)
    n_p = len(names)
    E = SMALL_EARLY_ROWS

    def body(early_ref, late_ref, *rest):
        w_refs, m_refs, v_refs = [dict(zip(names, rest[i * n_p:(i + 1) * n_p])) for i in range(3)]
        outs, tot = rest[3 * n_p:-1], rest[-1]
        loss_out = outs[0]
        g_out, d_out, m_out, v_out = [dict(zip(names, outs[1 + i * n_p:1 + (i + 1) * n_p])) for i in range(4)]
        acc, acc_late = early_ref[0], late_ref[0]
        for d in range(1, N_DEV):
            acc, acc_late = acc + early_ref[d], acc_late + late_ref[d]
        tot[0:E, :] = acc
        tot[E:E + SMALL_LATE_ROWS, :] = acc_late
        loss_out[...] = tot[E + 8:E + 9, :]

        x, y, c = _position()
        me = 4 * x + 2 * y + c

        def update(name, g, ix):
            g_out[name][ix] = g
            d_out[name][ix], m_out[name][ix], v_out[name][ix] = _adamw_math(
                w_refs[name][ix], g, m_refs[name][ix], v_refs[name][ix])

        for name in names:
            shape = w[name].shape
            if name in SMALL_SHARDED:
                per_dev = dict(SMALL_EARLY)[name] // N_DEV
                at = pl.multiple_of(SMALL_EARLY_AT[name] + me * per_dev, 8)
                g = tot[pl.ds(at, per_dev), :]
                update(name, g[:shape[1], :shape[2]], (0,))
            elif name == "w_spatial":
                for grp in range(GROUPS):
                    at = SMALL_EARLY_AT[name] + grp * SBLOCK
                    update(name, tot[at:at + SBLOCK, :], (0, grp))
            elif name == "b_spatial":
                at = SMALL_EARLY_AT[name]
                update(name, tot[at:at + GROUPS, :], (0,))
            else:
                at = E if name == "norm_pre_mix" else SMALL_EARLY_AT[name]
                for k in range(shape[1] // LANES):
                    update(name, tot[at + k:at + k + 1, :], (slice(None), pl.ds(k * LANES, LANES)))

    state = [s[n] for s in (w, m, v) for n in names]
    out_shapes = [jax.ShapeDtypeStruct((1, LANES), F32)] + [jax.ShapeDtypeStruct(w[n].shape, F32) for n in names] * 4
    outs = pl.pallas_call(
        body, name=name,
        in_specs=[VMEM_SPEC] * (2 + len(state)), out_specs=[VMEM_SPEC] * len(out_shapes), out_shape=out_shapes,
        scratch_shapes=[pltpu.VMEM((E + SMALL_LATE_ROWS, LANES), F32)],
    )(got_early, got_late, *state)
    per_name = {n: tuple(outs[1 + i * n_p + j] for i in range(4)) for j, n in enumerate(names)}
    return outs[0], per_name


def _tile_rows(n_elems):
    return -(-n_elems // (8 * LANES)) * 8


def _pack_rows(parts, rows):
    pieces = []
    for p in parts:
        q = p.reshape(-1, LANES)
        pieces.append(jnp.pad(q, ((0, _tile_rows(p.size) - q.shape[0]), (0, 0))))
    buf = jnp.concatenate(pieces, axis=0)
    return jnp.pad(buf, ((0, rows - buf.shape[0]), (0, 0)))


def _in_w_blocks_call(a, live, finished, *, name, tk=TOKEN_TILE):
    S = a.shape[0]
    tk = min(tk, S)
    steps = S // tk
    n_live = len(live)
    n_chips = N_DEV // 2
    out_shape = (n_chips, IN_BLK, D_MODEL)

    def body(a_ref, *rest):
        live_refs, done_refs = rest[:n_live], rest[n_live:n_live + len(finished)]
        keep_ref, send_ref = rest[n_live + len(finished):n_live + len(finished) + 2]
        accs = rest[n_live + len(finished) + 2:]
        k = pl.program_id(0)

        @pl.when(k == 0)
        def _():
            for acc in accs:
                acc[...] = jnp.zeros_like(acc)

        av = a_ref[...]
        for src, acc in zip(live_refs, accs):
            for m0 in range(0, src.shape[1], 1024):
                m1 = min(src.shape[1], m0 + 1024)
                acc[m0:m1, :] += _dot(src[:, m0:m1], av, TN)

        @pl.when(k == steps - 1)
        def _():
            groups = [(acc, cols) for acc, (_, cols) in zip(accs, live)]
            groups += [(ref, cols) for ref, (_, cols) in zip(done_refs, finished)]
            core = lax.axis_index("c")

            def cut(d, out_ref):
                lo, hi = IN_BLK * d, IN_BLK * (d + 1)
                for ref, (c0, c1) in groups:
                    s0, e0 = max(lo, c0), min(hi, c1)
                    if s0 < e0:
                        out_ref[d // 2, s0 - lo:e0 - lo, :] = ref[s0 - c0:e0 - c0, :].astype(BF16)

            for d in range(N_DEV):
                pl.when(core == d % 2)(lambda d=d: cut(d, keep_ref))
                pl.when(core != d % 2)(lambda d=d: cut(d, send_ref))

    acc_shapes = [(arr.shape[1], D_MODEL) for arr, _ in live]
    tiles = [((tk, D_MODEL), BF16)] + [((tk, arr.shape[1]), BF16) for arr, _ in live]
    resident = ([(arr.shape, arr.dtype) for arr, _ in finished] + [(out_shape, BF16)] * 2
                + [(s, F32) for s in acc_shapes])
    return pl.pallas_call(
        body, name=name, grid=(steps,),
        in_specs=[pl.BlockSpec((tk, D_MODEL), lambda k: (k, 0))]
        + [pl.BlockSpec((tk, arr.shape[1]), lambda k: (k, 0)) for arr, _ in live]
        + [_res(arr.shape) for arr, _ in finished],
        out_specs=[_acc(out_shape)] * 2,
        out_shape=[pltpu.HBM(out_shape, BF16)] * 2,
        scratch_shapes=[pltpu.VMEM(s, F32) for s in acc_shapes],
        compiler_params=_cp(("arbitrary",), _vmem_limit(tiles, resident)),
    )(*_hbm(a, *[arr for arr, _ in live], *[arr for arr, _ in finished]))


def _local_step(x, target, W, scatter, finish, pair, chip_scatter):
    g1, g2, g3, g4 = [W[n].reshape(1, D_MODEL) for n in ("norm_pre_mix", "norm_post_mix", "norm_pre_ffn", "norm_post_ffn")]
    wfi, wfo = W["w_ffn_in"].reshape(2 * D_FF, D_MODEL), W["w_ffn_out"]
    wg = jnp.pad(W["w_gate_up"], ((0, LANES - RANK), (0, 0))).astype(BF16)
    bgate = W["b_gate"].reshape(1, QK)
    gnorm = W["gla_norm"].reshape(HEADS, 1, DV)
    ln_g = W["sgu_ln_g"].reshape(GROUPS, 1, DG)
    ln_b = W["sgu_ln_b"].reshape(GROUPS, 1, DG)
    w_sp = W["w_spatial"]
    b_sp = W["b_spatial"].reshape(GROUPS, SBLOCK, 1)

    a, pa, alow, ps, pg, w_in_t = _in_proj_call(x, g1, W["w_in_blocks"], name="in_proj")
    y_gla, states = _gla_fwd_call(pa, alow, wg, bgate, gnorm, name="gla_fwd")
    y_sgu = _sgu_fwd_call(ps, ln_g, ln_b, w_sp, b_sp, name="sgu_fwd")
    t1, t2, merged, mix, x1, h = _mixer_tail_call(y_gla, y_sgu, pg, x, W["w_branch_gla"], W["w_branch_sgu"],
                                                  W["w_out"], g2, g3, name="mixer_tail")
    gu, z = _ffn_in_call(h, wfi, name="ffn_in")
    loss, dx2, dy, dg4 = _ffn_out_loss_call(z, wfo, x1, target, g4, name="ffn_out_loss")

    grads = {"norm_post_ffn": dg4}
    done = {}
    dgu = _ffn_out_bwd_call(dy, wfo, gu, name="ffn_out_bwd")
    dw_ffn_out = _weight_grads_call([z, dy], [(0, 1)], name="d_ffn_out_w")[0].reshape(N_DEV, D_FF // N_DEV, D_MODEL)
    dw_ffn_in = _ffn_in_grad_call(h, dgu, name="d_ffn_in_w").reshape(N_DEV, FF_BLK, D_MODEL)
    dgu, dw_ffn_out, dw_ffn_in = lax.optimization_barrier((dgu, dw_ffn_out, dw_ffn_in))
    ffn_received = scatter(("w_ffn_out", "w_ffn_in"), [dw_ffn_out, dw_ffn_in])
    dx1, dmix, grads["norm_pre_ffn"], grads["norm_post_mix"] = _ffn_in_bwd_call(dgu, wfi, dx2, x1, mix, g3, g2, name="ffn_in_bwd")
    dt1, dt2, dpg, dy_gla, dy_sgu = _mixer_bwd_call(dmix, t1, t2, pg, W["w_out"], W["w_branch_gla"], W["w_branch_sgu"],
                                                    name="mixer_bwd")
    rows = D_MODEL // N_DEV
    mixer_grads = _weight_grads_call([merged, dmix, y_gla, dt1, y_sgu, dt2], [(0, 1), (2, 3), (4, 5)], name="d_mixer_w")
    dy_gla, dy_sgu, mixer_grads = lax.optimization_barrier((dy_gla, dy_sgu, mixer_grads))
    mixer_received = scatter(("w_out", "w_branch_gla", "w_branch_sgu"),
                             [g.reshape(N_DEV, rows, D_MODEL) for g in mixer_grads])
    dps, dlg, dlb, dwsp, dbsp = _sgu_bwd_call(ps, ln_g, ln_b, w_sp, b_sp, dy_sgu, name="sgu_bwd")
    dw_s, dw_g = _weight_grads_call([a, dps, dpg], [(1, 0), (2, 0)], name="d_in_w_sgu_gates")
    dy_gla, dw_s, dw_g = lax.optimization_barrier((dy_gla, dw_s, dw_g))
    dpa, dlow, dwg, dgn, dbg = _gla_bwd_call(pa, alow, wg, bgate, gnorm, states, dy_gla, name="gla_bwd")
    ffn_received, mixer_received, dpa, dlow = lax.optimization_barrier((ffn_received, mixer_received, dpa, dlow))
    keep, send = _in_w_blocks_call(a, [(dpa, A_COLS), (dlow, LOW_COLS)], [(dw_s, S_COLS), (dw_g, G_COLS)],
                                   name="d_in_w_blocks")
    ffn_received, mixer_received, send = lax.optimization_barrier((ffn_received, mixer_received, send))
    from_sibling = pair(send)
    done.update({**finish(ffn_received), **finish(mixer_received)})
    done, keep = lax.optimization_barrier((done, keep))
    chip_sum = _pair_sum_call(keep, from_sibling, name="pair_sum_w_in")
    chip_sum, dpa = lax.optimization_barrier((chip_sum, dpa))
    in_received = chip_scatter(chip_sum)
    grad_x, grads["norm_pre_mix"] = _in_proj_bwd_call(dpa, dlow, dps, dpg, w_in_t, x, dx1, g1, name="in_proj_bwd")

    grads["w_gate_up"] = dwg[:RANK]
    grads["b_gate"] = dbg
    grads["gla_norm"] = dgn
    grads["sgu_ln_g"] = dlg
    grads["sgu_ln_b"] = dlb
    grads["w_spatial"] = dwsp
    grads["b_spatial"] = dbsp
    done.update(finish({"w_in": in_received}))
    return loss, grad_x, grads, done


WEIGHTS = ("norm_pre_mix", "w_in", "w_gate_up", "b_gate", "gla_norm", "sgu_ln_g", "sgu_ln_b", "w_spatial",
           "b_spatial", "w_branch_gla", "w_branch_sgu", "w_out", "norm_post_mix", "norm_pre_ffn", "w_ffn_in",
           "w_ffn_out", "norm_post_ffn")
BIG = ("w_in", "w_branch_gla", "w_branch_sgu", "w_out", "w_ffn_in", "w_ffn_out")
MIXER = ("w_branch_gla", "w_branch_sgu", "w_out")
FFN = ("w_ffn_in", "w_ffn_out")
COLUMN_SHARDED = ("w_in", "w_ffn_in")
LATE_SCATTER_WAVES = 4
GATHER_WAVES = 4
SMALL = tuple(n for n in WEIGHTS if n not in BIG)
SMALL_SHARDED = ("w_gate_up", "gla_norm", "sgu_ln_g", "sgu_ln_b")
SMALL_GATHER_ROWS = 32


def kernel(x, norm_pre_mix, w_in, w_gate_up, b_gate, gla_norm, sgu_ln_g, sgu_ln_b, w_spatial, b_spatial, w_branch_gla, w_branch_sgu, w_out, norm_post_mix, norm_pre_ffn, w_ffn_in, w_ffn_out, norm_post_ffn, loss_target, m_norm_pre_mix, m_w_in, m_w_gate_up, m_b_gate, m_gla_norm, m_sgu_ln_g, m_sgu_ln_b, m_w_spatial, m_b_spatial, m_w_branch_gla, m_w_branch_sgu, m_w_out, m_norm_post_mix, m_norm_pre_ffn, m_w_ffn_in, m_w_ffn_out, m_norm_post_ffn, v_norm_pre_mix, v_w_in, v_w_gate_up, v_b_gate, v_gla_norm, v_sgu_ln_g, v_sgu_ln_b, v_w_spatial, v_b_spatial, v_w_branch_gla, v_w_branch_sgu, v_w_out, v_norm_post_mix, v_norm_pre_ffn, v_w_ffn_in, v_w_ffn_out, v_norm_post_ffn):
    given = dict(locals())
    def local(a, n):
        return a[0].T if n in COLUMN_SHARDED else a[0]

    w = {n: local(given[n], n) for n in WEIGHTS}
    m = {n: local(given["m_" + n], n) for n in WEIGHTS}
    v = {n: local(given["v_" + n], n) for n in WEIGHTS}
    xs, target = x[0], loss_target[0]

    small_shard = _pack_rows([w[n] for n in SMALL_SHARDED], SMALL_GATHER_ROWS)
    first = _all_gather_async([w["w_in"].astype(BF16), small_shard], name="gather_w_in", collective_id=2,
                              waves=GATHER_WAVES)
    rest, first, m["w_in"], v["w_in"] = lax.optimization_barrier(
        ([w[n].astype(BF16) for n in MIXER + FFN], first, m["w_in"], v["w_in"]))
    rest_blocks = _all_gather_async(rest, name="gather_rest", collective_id=1)
    W = {n: w[n] for n in SMALL if n not in SMALL_SHARDED}
    blocks = dict(zip(MIXER + FFN, rest_blocks))
    for n in ("w_branch_gla", "w_branch_sgu", "w_out", "w_ffn_out"):
        W[n] = blocks[n].reshape(-1, D_MODEL)
    W["w_ffn_in"] = blocks["w_ffn_in"]
    W["w_in_blocks"] = first[0]
    small_blocks = first[1]
    off = 0
    for n in SMALL_SHARDED:
        r, c = w[n].shape
        blk = small_blocks[:, off:off + r * c // LANES].reshape(N_DEV, r, c)
        W[n] = blk.transpose(1, 0, 2).reshape(r, N_DEV * c)
        off += _tile_rows(r * c)

    scatter_ids = iter((3, 4))

    def scatter(names, parts):
        got = _scatter_blocks_async(parts, name="scatter_" + "_".join(names), collective_id=next(scatter_ids))
        return dict(zip(names, got))

    def finish(received):
        return {n: _adamw_reduce_call(r, w[n], m[n], v[n], name="adamw_" + n) for n, r in received.items()}

    def pair(part):
        return _pair_exchange_async([part], name="pair_w_in", collective_id=5)[0]

    def chip_scatter(part):
        return _chip_scatter_async([part], name="scatter_w_in", collective_id=6, waves=LATE_SCATTER_WAVES)[0]

    loss_part, grad_x, grads, big_done = _local_step(xs, target, W, scatter, finish, pair, chip_scatter)

    late = jnp.concatenate([grads["norm_pre_mix"].reshape(8, LANES), jnp.broadcast_to(loss_part, (8, LANES))], axis=0)
    gathered = _small_gather_async([_small_early_rows(grads), late], name="gather_small", collective_id=7)
    big_done = {n: [a.T for a in r] if n in COLUMN_SHARDED else r for n, r in big_done.items()}
    gathered, big_done["w_in"] = lax.optimization_barrier((gathered, big_done["w_in"]))
    loss_row, small_done = _small_update_call(
        *gathered, *[{n: given[prefix + n] for n in SMALL} for prefix in ("", "m_", "v_")], name="small_update")

    def pick(i):
        return [small_done[n][i] if n in SMALL else big_done[n][i][None] for n in WEIGHTS]

    return (loss_row[0, 0], grad_x[None], *pick(0), *pick(1), *pick(2), *pick(3))
```

```python
import jax
import jax.numpy as jnp
from jax import lax
from jax.experimental import pallas as pl
from jax.experimental.pallas import tpu as pltpu
from jax.experimental.pallas import tpu_sc as plsc

F32 = jnp.float32
BF16 = jnp.bfloat16

D_MODEL = 1024
N_DEV = 8
CHUNK = 64
HEADS = 4
DK = 128
DV = 256
QK = HEADS * DK
GV = HEADS * DV
RANK = 16
GROUPS = 4
SBLOCK = 128
DG = 256
D_FF = 2816
FF_BLK = 704
EPS = 1e-6
Q_SCALE = DK ** -0.5
LANES = 128
VMEM_BIG = 48 * 1024 * 1024

D_IN = 7184
IN_BLK = 898
A_COLS = (0, 3072)
LOW_COLS = (3072, 3088)
S_COLS = (3088, 5136)
G_COLS = (5136, 7184)

ADAM_LR = 0.001
ADAM_B1 = 0.9
ADAM_B2 = 0.999
ADAM_EPS = 1e-08
ADAM_WD = 0.01
ADAM_STEP = 10

MESH = pl.DeviceIdType.MESH
ANY = pl.BlockSpec(memory_space=pl.ANY)
VMEM_SPEC = pl.BlockSpec(memory_space=pltpu.VMEM)


def _cp(sem=None, vmem=None):
    return pltpu.CompilerParams(dimension_semantics=sem, vmem_limit_bytes=vmem)


def _hbm(*arrays):
    return [pltpu.with_memory_space_constraint(a, pltpu.HBM) for a in arrays]


def _sig(x):
    return 0.5 * jnp.tanh(0.5 * x) + 0.5


GELU_C = 0.7978845608028654
GELU_A = 0.044715


def _gelu(x):
    t = jnp.tanh((GELU_C * x) * (1.0 + GELU_A * (x * x)))
    return (0.5 * x) * (1.0 + t)


def _gelu_and_grad(x):
    x2 = x * x
    t = jnp.tanh((GELU_C * x) * (1.0 + GELU_A * x2))
    one_t = 1.0 + t
    hx = 0.5 * x
    grad = 0.5 * one_t + (hx * (1.0 - t * t)) * (GELU_C + (3.0 * GELU_A * GELU_C) * x2)
    return hx * one_t, grad


def _logsig(x):
    return jnp.minimum(x, 0.0) - jnp.log1p(jnp.exp(-jnp.abs(x)))


def _dot(a, b, dims):
    return lax.dot_general(a, b, (dims, ((), ())), preferred_element_type=F32)


NN = ((1,), (0,))
NT = ((1,), (1,))
TN = ((0,), (0,))


def _exact_mask_dot(mask_bf16, x):
    hi = x.astype(BF16)
    r1 = x - hi.astype(F32)
    mid = r1.astype(BF16)
    lo = (r1 - mid.astype(F32)).astype(BF16)
    return _dot(mask_bf16, hi, NN) + _dot(mask_bf16, mid, NN) + _dot(mask_bf16, lo, NN)


ROW_TILE = 512
SUB_ROWS = 256


def _sub_tiles(T):
    return [pl.ds(r0, min(SUB_ROWS, T)) for r0 in range(0, T, SUB_ROWS)]


def _rt(T, W, c=0):
    return pl.BlockSpec((T, W), lambda i: (i, c))


def _rt3(nb, T, W):
    return pl.BlockSpec((nb, T, W), lambda i: (0, i, 0))


def _res(shape):
    nd = len(shape)
    return pl.BlockSpec(tuple(shape), lambda i: (0,) * nd, pipeline_mode=pl.Buffered(1))


def _acc(shape):
    nd = len(shape)
    return pl.BlockSpec(tuple(shape), lambda i: (0,) * nd, pipeline_mode=pl.Buffered(1))


def _nbytes(shape, dtype):
    n = jnp.dtype(dtype).itemsize
    for s in shape:
        n *= s
    return n


def _vmem_limit(tiles, resident, temps=16 * 1024 * 1024):
    need = 2 * sum(_nbytes(s, d) for s, d in tiles) + sum(_nbytes(s, d) for s, d in resident) + temps
    return min(need, 60 * 1024 * 1024)


def _tok_call(body, *, name, S, T, ins, outs, semantics="parallel"):
    tiles = [(spec.block_shape, a.dtype) for a, spec, kind in ins + outs if kind == "tile"]
    resident = [(a.shape, a.dtype) for a, spec, kind in ins + outs if kind == "res"]
    return pl.pallas_call(
        body, name=name, grid=(S // T,),
        in_specs=[spec for _, spec, _ in ins], out_specs=[spec for _, spec, _ in outs],
        out_shape=[pltpu.HBM(a.shape, a.dtype) for a, _, _ in outs],
        compiler_params=_cp((semantics,), _vmem_limit(tiles, resident)),
    )(*_hbm(*[a for a, _, _ in ins]))


def _tile(a, spec):
    return (a, spec, "tile")


def _whole(a):
    return (a, _res(a.shape), "res")


def _out_tile(shape, dtype, spec):
    return (jax.ShapeDtypeStruct(shape, dtype), spec, "tile")


def _out_acc(shape, dtype=F32):
    return (jax.ShapeDtypeStruct(shape, dtype), _acc(shape), "res")


def _rms_stats(x):
    r = lax.rsqrt(jnp.mean(x * x, axis=-1, keepdims=True) + EPS)
    return r, x * r


def _rms_bwd(xh, r, g, dy):
    dxh = dy * g
    dx = r * (dxh - xh * jnp.mean(dxh * xh, axis=-1, keepdims=True))
    dg = jnp.sum(dy * xh, axis=0, keepdims=True)
    return dx, dg


def _accum(ref, val):
    @pl.when(pl.program_id(0) == 0)
    def _():
        ref[...] = val

    @pl.when(pl.program_id(0) > 0)
    def _():
        ref[...] += val


def _dot_rows_t(a, w_ref, row0, o_ref, chunk=1024):
    n = o_ref.shape[1]
    for n0 in range(0, n, chunk):
        n1 = min(n, n0 + chunk)
        o_ref[:, n0:n1] = _dot(a, w_ref[row0 + n0:row0 + n1, :], NT).astype(o_ref.dtype)


def _in_proj_call(x, g1, w_blocks, *, name, T=ROW_TILE):
    S = x.shape[0]

    def body(x_ref, g_ref, blk_ref, a_ref, pa_ref, al_ref, ps_ref, pg_ref, w_ref):
        @pl.when(pl.program_id(0) == 0)
        def _():
            for d in range(N_DEV):
                w_ref[d * IN_BLK:(d + 1) * IN_BLK, :] = blk_ref[d]

        _, xh = _rms_stats(x_ref[...])
        a = (xh * g_ref[...]).astype(BF16)
        a_ref[...] = a
        _dot_rows_t(a, w_ref, A_COLS[0], pa_ref)
        _dot_rows_t(a, w_ref, LOW_COLS[0], al_ref)
        _dot_rows_t(a, w_ref, S_COLS[0], ps_ref)
        _dot_rows_t(a, w_ref, G_COLS[0], pg_ref)

    widths = (D_MODEL, A_COLS[1] - A_COLS[0], LANES, S_COLS[1] - S_COLS[0], G_COLS[1] - G_COLS[0])
    return _tok_call(
        body, name=name, S=S, T=T, semantics="arbitrary",
        ins=[_tile(x, _rt(T, D_MODEL)), _whole(g1), _whole(w_blocks)],
        outs=[_out_tile((S, w), BF16, _rt(T, w)) for w in widths] + [_out_acc((D_IN, D_MODEL), BF16)])


def _mixer_tail_call(y_gla, y_sgu, pg, x, w_bg, w_bs, w_out, g2, g3, *, name, T=ROW_TILE):
    S = x.shape[0]

    def body(yg_ref, ys_ref, pg_ref, x_ref, wbg_ref, wbs_ref, wo_ref, g2_ref, g3_ref,
             t1_ref, t2_ref, mg_ref, mix_ref, x1_ref, h_ref):
        for rows in _sub_tiles(T):
            t1 = _dot(yg_ref[rows, :], wbg_ref[...], NN)
            t2 = _dot(ys_ref[rows, :], wbs_ref[...], NN)
            t1_ref[rows, :] = t1.astype(BF16)
            t2_ref[rows, :] = t2.astype(BF16)
            sg = _sig(pg_ref[rows, pl.ds(0, D_MODEL)].astype(F32))
            ss = _sig(pg_ref[rows, pl.ds(D_MODEL, D_MODEL)].astype(F32))
            merged = (sg * t1 + ss * t2).astype(BF16)
            mg_ref[rows, :] = merged
            mix = _dot(merged, wo_ref[...], NN)
            mix_ref[rows, :] = mix
            _, mh = _rms_stats(mix)
            x1 = x_ref[rows, :] + mh * g2_ref[...]
            x1_ref[rows, :] = x1
            _, xh = _rms_stats(x1)
            h_ref[rows, :] = (xh * g3_ref[...]).astype(BF16)

    row = _rt(T, D_MODEL)
    b16 = lambda: _out_tile((S, D_MODEL), BF16, row)
    f32 = lambda: _out_tile((S, D_MODEL), F32, row)
    return _tok_call(
        body, name=name, S=S, T=T,
        ins=[_tile(y_gla, row), _tile(y_sgu, row), _tile(pg, _rt(T, 2 * D_MODEL)), _tile(x, row),
             _whole(w_bg), _whole(w_bs), _whole(w_out), _whole(g2), _whole(g3)],
        outs=[b16(), b16(), b16(), f32(), f32(), b16()])


FF_CHUNKS = ((0, 1024), (1024, 2048), (2048, D_FF))


def _ffn_in_call(h, wfi, *, name, T=ROW_TILE):
    S = h.shape[0]

    def body(h_ref, w_ref, gu_ref, z_ref):
        hv = h_ref[...]
        for c0, c1 in FF_CHUNKS:
            gate = _dot(hv, w_ref[c0:c1, :], NT)
            up = _dot(hv, w_ref[D_FF + c0:D_FF + c1, :], NT)
            gu_ref[:, c0:c1] = gate.astype(BF16)
            gu_ref[:, D_FF + c0:D_FF + c1] = up.astype(BF16)
            z_ref[:, c0:c1] = (gate * _sig(gate) * up).astype(BF16)

    return _tok_call(
        body, name=name, S=S, T=T,
        ins=[_tile(h, _rt(T, D_MODEL)), _whole(wfi)],
        outs=[_out_tile((S, 2 * D_FF), BF16, _rt(T, 2 * D_FF)),
              _out_tile((S, D_FF), BF16, _rt(T, D_FF))])


def _ffn_out_loss_call(z, wfo, x1, target, g4, *, name, T=ROW_TILE):
    S = x1.shape[0]

    def body(z_ref, w_ref, x1_ref, t_ref, g4_ref, loss_ref, dx2_ref, dy_ref, dg4_ref):
        loss = jnp.zeros((1, 1), F32)
        dg4 = jnp.zeros((1, D_MODEL), F32)
        for rows in _sub_tiles(T):
            y = _dot(z_ref[rows, :], w_ref[...], NN)
            r, yh = _rms_stats(y)
            diff = x1_ref[rows, :] + yh * g4_ref[...] - t_ref[rows, :]
            loss = loss + 0.5 * jnp.sum(jnp.mean(diff * diff, axis=-1, keepdims=True), axis=0, keepdims=True)
            dx2 = diff * (1.0 / D_MODEL)
            dx2_ref[rows, :] = dx2
            dy, dg = _rms_bwd(yh, r, g4_ref[...], dx2)
            dy_ref[rows, :] = dy.astype(BF16)
            dg4 = dg4 + dg
        _accum(loss_ref, jnp.broadcast_to(loss, (1, LANES)))
        _accum(dg4_ref, dg4)

    row = _rt(T, D_MODEL)
    return _tok_call(
        body, name=name, S=S, T=T, semantics="arbitrary",
        ins=[_tile(z, _rt(T, D_FF)), _whole(wfo), _tile(x1, row), _tile(target, row), _whole(g4)],
        outs=[_out_acc((1, LANES)), _out_tile((S, D_MODEL), F32, row), _out_tile((S, D_MODEL), BF16, row),
              _out_acc((1, D_MODEL))])


def _ffn_out_bwd_call(dy, wfo, gu, *, name, T=ROW_TILE):
    S = dy.shape[0]

    def body(dy_ref, w_ref, gu_ref, dgu_ref):
        dyv = dy_ref[...]
        for c0, c1 in FF_CHUNKS:
            dz = _dot(dyv, w_ref[c0:c1, :], NT).astype(BF16)
            gt = gu_ref[:, c0:c1]
            up = gu_ref[:, D_FF + c0:D_FF + c1]
            s = _sig(gt)
            dgu_ref[:, c0:c1] = dz * up * s * (1.0 + gt * (1.0 - s))
            dgu_ref[:, D_FF + c0:D_FF + c1] = dz * gt * s

    wide = _rt(T, 2 * D_FF)
    return _tok_call(
        body, name=name, S=S, T=T,
        ins=[_tile(dy, _rt(T, D_MODEL)), _whole(wfo), _tile(gu, wide)],
        outs=[_out_tile((S, 2 * D_FF), BF16, wide)])[0]


def _ffn_in_bwd_call(dgu, wfi, dx2, x1, mix, g3, g2, *, name, T=ROW_TILE):
    S = x1.shape[0]

    def body(dgu_ref, w_ref, dx2_ref, x1_ref, mix_ref, g3_ref, g2_ref, dx1_ref, dmix_ref, dg3_ref, dg2_ref):
        dg3 = jnp.zeros((1, D_MODEL), F32)
        dg2 = jnp.zeros((1, D_MODEL), F32)
        for rows in _sub_tiles(T):
            dh = _dot(dgu_ref[rows, :], w_ref[...], NN)
            r3, xh = _rms_stats(x1_ref[rows, :])
            d3, g = _rms_bwd(xh, r3, g3_ref[...], dh)
            dg3 = dg3 + g
            dx1 = dx2_ref[rows, :] + d3
            dx1_ref[rows, :] = dx1
            r2, mh = _rms_stats(mix_ref[rows, :])
            dmix, g = _rms_bwd(mh, r2, g2_ref[...], dx1)
            dg2 = dg2 + g
            dmix_ref[rows, :] = dmix.astype(BF16)
        _accum(dg3_ref, dg3)
        _accum(dg2_ref, dg2)

    row = _rt(T, D_MODEL)
    return _tok_call(
        body, name=name, S=S, T=T, semantics="arbitrary",
        ins=[_tile(dgu, _rt(T, 2 * D_FF)), _whole(wfi), _tile(dx2, row), _tile(x1, row), _tile(mix, row),
             _whole(g3), _whole(g2)],
        outs=[_out_tile((S, D_MODEL), F32, row), _out_tile((S, D_MODEL), BF16, row),
              _out_acc((1, D_MODEL)), _out_acc((1, D_MODEL))])


def _mixer_bwd_call(dmix, t1, t2, pg, w_out, w_bg, w_bs, *, name, T=ROW_TILE):
    S = dmix.shape[0]

    def body(dmix_ref, t1_ref, t2_ref, pg_ref, wo_ref, wbg_ref, wbs_ref, dt1_ref, dt2_ref, dpg_ref, dyg_ref, dys_ref):
        gg, gs = pl.ds(0, D_MODEL), pl.ds(D_MODEL, D_MODEL)
        for rows in _sub_tiles(T):
            dm = _dot(dmix_ref[rows, :], wo_ref[...], NT)
            sg = _sig(pg_ref[rows, gg].astype(F32))
            ss = _sig(pg_ref[rows, gs].astype(F32))
            dt1 = (dm * sg).astype(BF16)
            dt2 = (dm * ss).astype(BF16)
            dt1_ref[rows, :] = dt1
            dt2_ref[rows, :] = dt2
            dpg_ref[rows, gg] = (dm * t1_ref[rows, :].astype(F32) * sg * (1.0 - sg)).astype(BF16)
            dpg_ref[rows, gs] = (dm * t2_ref[rows, :].astype(F32) * ss * (1.0 - ss)).astype(BF16)
            dyg_ref[rows, :] = _dot(dt1, wbg_ref[...], NT).astype(BF16)
            dys_ref[rows, :] = _dot(dt2, wbs_ref[...], NT).astype(BF16)

    row = _rt(T, D_MODEL)
    wide = _rt(T, 2 * D_MODEL)
    b16 = lambda: _out_tile((S, D_MODEL), BF16, row)
    return _tok_call(
        body, name=name, S=S, T=T,
        ins=[_tile(dmix, row), _tile(t1, row), _tile(t2, row), _tile(pg, wide), _whole(w_out), _whole(w_bg), _whole(w_bs)],
        outs=[b16(), b16(), _out_tile((S, 2 * D_MODEL), BF16, wide), b16(), b16()])


def _in_proj_bwd_call(dpa, dlow, dps, dpg, w_in_t, x, dx1, g1, *, name, T=ROW_TILE):
    S = x.shape[0]

    def body(dpa_ref, dl_ref, dps_ref, dpg_ref, w_ref, x_ref, dx1_ref, g1_ref, gx_ref, dg1_ref):
        da = (_dot(dpa_ref[...], w_ref[A_COLS[0]:A_COLS[1], :], NN)
              + _dot(dl_ref[...], w_ref[LOW_COLS[0]:LOW_COLS[0] + LANES, :], NN)
              + _dot(dps_ref[...], w_ref[S_COLS[0]:S_COLS[1], :], NN)
              + _dot(dpg_ref[...], w_ref[G_COLS[0]:G_COLS[1], :], NN))
        r, xh = _rms_stats(x_ref[...])
        dxa, dg = _rms_bwd(xh, r, g1_ref[...], da)
        gx_ref[...] = dx1_ref[...] + dxa
        _accum(dg1_ref, dg)

    row = _rt(T, D_MODEL)
    return _tok_call(
        body, name=name, S=S, T=T, semantics="arbitrary",
        ins=[_tile(dpa, _rt(T, dpa.shape[1])), _tile(dlow, _rt(T, dlow.shape[1])), _tile(dps, _rt(T, dps.shape[1])),
             _tile(dpg, _rt(T, dpg.shape[1])), _whole(w_in_t), _tile(x, row), _tile(dx1, row), _whole(g1)],
        outs=[_out_tile((S, D_MODEL), F32, row), _out_acc((1, D_MODEL))])


TOKEN_TILE = 512


def _weight_grads_part(arrays, pairs, *, tk=TOKEN_TILE, out_dtype=BF16):
    S = arrays[0].shape[-2]
    tk = min(tk, S)
    n_in = len(arrays)

    def out_shape(i, j):
        a, b = arrays[i], arrays[j]
        if a.ndim == 3:
            return (a.shape[0], a.shape[2], b.shape[1])
        if b.ndim == 3:
            return (b.shape[0], a.shape[1], b.shape[2])
        return (a.shape[1], b.shape[1])

    shapes = [out_shape(i, j) for i, j in pairs]

    in_place = out_dtype == F32

    def body(ins, outs, accs):
        k = pl.program_id(0)
        if in_place:
            accs = outs

        @pl.when(k == 0)
        def _():
            for acc in accs:
                acc[...] = jnp.zeros_like(acc)

        for (i, j), acc in zip(pairs, accs):
            a_ref, b_ref = ins[i], ins[j]
            if len(a_ref.shape) == 3:
                for n in range(a_ref.shape[0]):
                    acc[n] += _dot(a_ref[n], b_ref[...], TN)
            elif len(b_ref.shape) == 3:
                a = a_ref[...]
                for n in range(b_ref.shape[0]):
                    acc[n] += _dot(a, b_ref[n], TN)
            else:
                b = b_ref[...]
                for m0 in range(0, a_ref.shape[1], 1024):
                    m1 = min(a_ref.shape[1], m0 + 1024)
                    acc[m0:m1, :] += _dot(a_ref[:, m0:m1], b, TN)

        if not in_place:
            @pl.when(k == S // tk - 1)
            def _():
                for out, acc in zip(outs, accs):
                    out[...] = acc[...].astype(out.dtype)

    def in_spec(a):
        if a.ndim == 3:
            return pl.BlockSpec((a.shape[0], tk, a.shape[2]), lambda k: (0, k, 0))
        return pl.BlockSpec((tk, a.shape[1]), lambda k: (k, 0))

    return dict(
        body=body, steps=S // tk, arrays=list(arrays),
        in_specs=[in_spec(a) for a in arrays],
        out_specs=[_acc(s) for s in shapes],
        out_shapes=[pltpu.HBM(s, out_dtype) for s in shapes],
        scratch=[] if in_place else [pltpu.VMEM(s, F32) for s in shapes],
        tiles=[(in_spec(a).block_shape, a.dtype) for a in arrays],
        resident=[(s, F32) for s in shapes] + ([] if in_place else [(s, BF16) for s in shapes]))


def _weight_grads_call(arrays, pairs, *, name, tk=TOKEN_TILE):
    part = _weight_grads_part(arrays, pairs, tk=tk)
    n_in, n_out = len(part["arrays"]), len(part["out_shapes"])

    def body(*refs):
        part["body"](refs[:n_in], refs[n_in:n_in + n_out], refs[n_in + n_out:])

    return pl.pallas_call(
        body, name=name, grid=(part["steps"],),
        in_specs=part["in_specs"], out_specs=part["out_specs"], out_shape=part["out_shapes"],
        scratch_shapes=part["scratch"],
        compiler_params=_cp(("arbitrary",), _vmem_limit(part["tiles"], part["resident"])),
    )(*_hbm(*part["arrays"]))


def _ffn_in_grad_call(h, dgu, *, name, tk=TOKEN_TILE):
    S = h.shape[0]
    tk = min(tk, S)
    shape = (D_FF, D_MODEL)

    def body(h_ref, dgu_ref, out_ref, acc):
        k = pl.program_id(1)

        @pl.when(k == 0)
        def _():
            acc[...] = jnp.zeros_like(acc)

        hv = h_ref[...]
        for c0, c1 in FF_CHUNKS:
            acc[c0:c1, :] += _dot(dgu_ref[:, c0:c1], hv, TN)

        @pl.when(k == S // tk - 1)
        def _():
            out_ref[...] = acc[...].astype(out_ref.dtype)

    tiles = [((tk, D_MODEL), BF16), ((tk, D_FF), BF16), (shape, BF16)]
    return pl.pallas_call(
        body, name=name, grid=(2, S // tk),
        in_specs=[pl.BlockSpec((tk, D_MODEL), lambda g, k: (k, 0)),
                  pl.BlockSpec((tk, D_FF), lambda g, k: (k, g))],
        out_specs=pl.BlockSpec(shape, lambda g, k: (g, 0)),
        out_shape=pltpu.HBM((2 * D_FF, D_MODEL), BF16),
        scratch_shapes=[pltpu.VMEM(shape, F32)],
        compiler_params=_cp(("parallel", "arbitrary"), _vmem_limit(tiles, [(shape, F32)])),
    )(*_hbm(h, dgu))


GLA_TILE = 256
Q_OFF, K_OFF, V_OFF, R_OFF = 0, QK, 2 * QK, 2 * QK + GV


def _tri(lower):
    r = lax.broadcasted_iota(jnp.int32, (CHUNK, CHUNK), 0)
    c = lax.broadcasted_iota(jnp.int32, (CHUNK, CHUNK), 1)
    return jnp.where((r >= c) if lower else (c >= r), 1.0, 0.0).astype(BF16)


def _gla_fwd_call(pa, alow, wg, bgate, gnorm, *, name):
    S = pa.shape[0]
    Tg = min(GLA_TILE, S)
    cb = Tg // CHUNK

    def body(pa_ref, al_ref, wg_ref, bg_ref, gn_ref, y_ref, st_ref, state):
        @pl.when(pl.program_id(0) == 0)
        def _():
            state[...] = jnp.zeros_like(state)

        logit = _dot(al_ref[...], wg_ref[...], NN) + bg_ref[...]
        ls = _logsig(logit) * (1.0 / 16.0)
        tri = _tri(True)
        for c in range(cb):
            rows = pl.ds(c * CHUNK, CHUNK)
            cum = _exact_mask_dot(tri, ls[c * CHUNK:(c + 1) * CHUNK])
            tot = cum[CHUNK - 1:CHUNK]
            kd = (pa_ref[rows, pl.ds(K_OFF, QK)].astype(F32) * jnp.exp(tot - cum)).astype(BF16)
            decay = jnp.exp(tot)
            for h in range(HEADS):
                lanes = slice(h * DK, (h + 1) * DK)
                new = state[h] * decay[:, lanes] + _dot(pa_ref[rows, pl.ds(V_OFF + h * DV, DV)], kd[:, lanes], TN)
                state[h] = new
                st_ref[c, h] = new
        for c in range(cb):
            rows = pl.ds(c * CHUNK, CHUNK)
            for h in range(HEADS):
                qs = (pa_ref[rows, pl.ds(Q_OFF + h * DK, DK)].astype(F32) * Q_SCALE).astype(BF16)
                o = _dot(qs, st_ref[c, h].astype(BF16), NT)
                rs = lax.rsqrt(jnp.mean(o * o, axis=-1, keepdims=True) + EPS)
                rr = pa_ref[rows, pl.ds(R_OFF + h * DV, DV)].astype(F32)
                y_ref[rows, pl.ds(h * DV, DV)] = (o * rs * gn_ref[h] * (rr * _sig(rr))).astype(BF16)

    return pl.pallas_call(
        body, name=name, grid=(S // Tg,),
        in_specs=[
            pl.BlockSpec((Tg, 2 * QK + 2 * GV), lambda t: (t, 0)),
            pl.BlockSpec((Tg, LANES), lambda t: (t, 0)),
            pl.BlockSpec((LANES, QK), lambda t: (0, 0)),
            pl.BlockSpec((1, QK), lambda t: (0, 0)),
            pl.BlockSpec((HEADS, 1, DV), lambda t: (0, 0, 0)),
        ],
        out_specs=[
            pl.BlockSpec((Tg, GV), lambda t: (t, 0)),
            pl.BlockSpec((cb, HEADS, DV, DK), lambda t: (t, 0, 0, 0)),
        ],
        out_shape=[pltpu.HBM((S, GV), BF16), pltpu.HBM((S // CHUNK, HEADS, DV, DK), F32)],
        scratch_shapes=[pltpu.VMEM((HEADS, DV, DK), F32)],
        compiler_params=_cp(("arbitrary",), VMEM_BIG),
    )(*_hbm(pa, alow, wg, bgate, gnorm))


def _gla_bwd_call(pa, alow, wg, bgate, gnorm, states, dy, *, name):
    S = pa.shape[0]
    Tg = min(GLA_TILE, S)
    cb = Tg // CHUNK
    nt = S // Tg

    def rev(t):
        return nt - 1 - t

    def body(pa_ref, al_ref, wg_ref, bg_ref, gn_ref, st_ref, prev_ref, dy_ref,
             dpa_ref, dlow_ref, dwg_ref, dgn_ref, dbg_ref, carry, pbuf, dlbuf):
        t = pl.program_id(0)

        @pl.when(t == 0)
        def _():
            carry[...] = jnp.zeros_like(carry)
            dwg_ref[...] = jnp.zeros_like(dwg_ref)
            dgn_ref[...] = jnp.zeros_like(dgn_ref)
            dbg_ref[...] = jnp.zeros_like(dbg_ref)

        logit = _dot(al_ref[...], wg_ref[...], NN) + bg_ref[...]
        ls = _logsig(logit) * (1.0 / 16.0)
        sneg = 1.0 / (1.0 + jnp.exp(logit))
        tri = _tri(True)
        upper = _tri(False)
        first_tile = rev(t) == 0
        heads = range(HEADS)

        w, decay, kd = [], [], []
        for c in range(cb):
            rows = pl.ds(c * CHUNK, CHUNK)
            cum = _exact_mask_dot(tri, ls[c * CHUNK:(c + 1) * CHUNK])
            tot = cum[CHUNK - 1:CHUNK]
            w.append(jnp.exp(tot - cum))
            decay.append(jnp.exp(tot))
            kd.append(pa_ref[rows, pl.ds(K_OFF, QK)].astype(F32) * w[c])

        dgn = [jnp.zeros((1, DV), F32) for _ in heads]
        for c in range(cb):
            rows = pl.ds(c * CHUNK, CHUNK)
            for h in heads:
                gn = gn_ref[h]
                qs = (pa_ref[rows, pl.ds(Q_OFF + h * DK, DK)].astype(F32) * Q_SCALE).astype(BF16)
                st16 = st_ref[c, h].astype(BF16)
                o = _dot(qs, st16, NT)
                rs = lax.rsqrt(jnp.mean(o * o, axis=-1, keepdims=True) + EPS)
                oh = o * rs
                rr = pa_ref[rows, pl.ds(R_OFF + h * DV, DV)].astype(F32)
                sr = _sig(rr)
                dyv = dy_ref[rows, pl.ds(h * DV, DV)].astype(F32)
                dpa_ref[rows, pl.ds(R_OFF + h * DV, DV)] = (
                    dyv * oh * gn * sr * (1.0 + rr * (1.0 - sr))).astype(BF16)
                don = dyv * (rr * sr)
                dgn[h] = dgn[h] + jnp.sum(don * oh, axis=0, keepdims=True)
                doh = don * gn
                do16 = (rs * (doh - oh * jnp.mean(doh * oh, axis=-1, keepdims=True))).astype(BF16)
                dpa_ref[rows, pl.ds(Q_OFF + h * DK, DK)] = (_dot(do16, st16, NN) * Q_SCALE).astype(BF16)
                pbuf[c, h] = _dot(do16, qs, TN)
        for h in heads:
            dgn_ref[h] += dgn[h]

        dkd = [[None] * HEADS for _ in range(cb)]
        ddecay = [[None] * HEADS for _ in range(cb)]
        for c in reversed(range(cb)):
            rows = pl.ds(c * CHUNK, CHUNK)
            for h in heads:
                lanes = slice(h * DK, (h + 1) * DK)
                gt = pbuf[c, h] + carry[h]
                gt16 = gt.astype(BF16)
                dkd[c][h] = _dot(pa_ref[rows, pl.ds(V_OFF + h * DV, DV)], gt16, NN)
                dpa_ref[rows, pl.ds(V_OFF + h * DV, DV)] = _dot(kd[c][:, lanes].astype(BF16), gt16, NT).astype(BF16)
                if c > 0:
                    st_prev = st_ref[c - 1, h]
                else:
                    st_prev = jnp.where(first_tile, 0.0, prev_ref[0, h])
                ddecay[c][h] = jnp.sum(gt * st_prev, axis=0, keepdims=True)
                carry[h] = gt * decay[c][:, lanes]

        dbg = jnp.zeros((1, QK), F32)
        for c in range(cb):
            rows = pl.ds(c * CHUNK, CHUNK)
            dkd_c = jnp.concatenate(dkd[c], axis=1)
            dpa_ref[rows, pl.ds(K_OFF, QK)] = (dkd_c * w[c]).astype(BF16)
            e = dkd_c * kd[c]
            dtot = jnp.sum(e, axis=0, keepdims=True) + jnp.concatenate(ddecay[c], axis=1) * decay[c]
            dls = dtot - _exact_mask_dot(upper, e)
            dlogit = dls * (1.0 / 16.0) * sneg[c * CHUNK:(c + 1) * CHUNK]
            dlbuf[rows, :] = dlogit.astype(BF16)
            dbg = dbg + jnp.sum(dlogit, axis=0, keepdims=True)
        dbg_ref[...] += dbg

        dl16 = dlbuf[...]
        dlow_ref[...] = _dot(dl16, wg_ref[...], NT).astype(BF16)
        dwg_ref[...] += _dot(al_ref[...], dl16, TN)

    wide = 2 * QK + 2 * GV
    tiles = [((Tg, wide), BF16), ((cb + 1, HEADS, DV, DK), F32), ((Tg, GV), BF16), ((Tg, wide), BF16),
             ((Tg, QK), BF16)]
    resident = [((cb + 1, HEADS, DV, DK), F32)]
    return pl.pallas_call(
        body, name=name, grid=(nt,),
        in_specs=[
            pl.BlockSpec((Tg, wide), lambda t: (rev(t), 0)),
            pl.BlockSpec((Tg, LANES), lambda t: (rev(t), 0)),
            pl.BlockSpec((LANES, QK), lambda t: (0, 0)),
            pl.BlockSpec((1, QK), lambda t: (0, 0)),
            pl.BlockSpec((HEADS, 1, DV), lambda t: (0, 0, 0)),
            pl.BlockSpec((cb, HEADS, DV, DK), lambda t: (rev(t), 0, 0, 0)),
            pl.BlockSpec((1, HEADS, DV, DK), lambda t: (jnp.maximum(rev(t) * cb - 1, 0), 0, 0, 0)),
            pl.BlockSpec((Tg, GV), lambda t: (rev(t), 0)),
        ],
        out_specs=[
            pl.BlockSpec((Tg, wide), lambda t: (rev(t), 0)),
            pl.BlockSpec((Tg, LANES), lambda t: (rev(t), 0)),
            pl.BlockSpec((LANES, QK), lambda t: (0, 0)),
            pl.BlockSpec((HEADS, 1, DV), lambda t: (0, 0, 0)),
            pl.BlockSpec((1, QK), lambda t: (0, 0)),
        ],
        out_shape=[pltpu.HBM((S, wide), BF16), pltpu.HBM((S, LANES), BF16), jax.ShapeDtypeStruct((LANES, QK), F32),
                   jax.ShapeDtypeStruct((HEADS, 1, DV), F32), jax.ShapeDtypeStruct((1, QK), F32)],
        scratch_shapes=[pltpu.VMEM((HEADS, DV, DK), F32), pltpu.VMEM((cb, HEADS, DV, DK), F32),
                        pltpu.VMEM((Tg, QK), BF16)],
        compiler_params=_cp(("arbitrary",), _vmem_limit(tiles, resident)),
    )(*_hbm(pa, alow, wg, bgate, gnorm, states, states, dy))


SGU_TILE = 256


def _sgu_mask():
    r = lax.broadcasted_iota(jnp.int32, (SBLOCK, SBLOCK), 0)
    c = lax.broadcasted_iota(jnp.int32, (SBLOCK, SBLOCK), 1)
    return (c < CHUNK) | (r >= CHUNK)


def _ln_stats(vf):
    mu = jnp.mean(vf, axis=-1, keepdims=True)
    xc = vf - mu
    rs = lax.rsqrt(jnp.mean(xc * xc, axis=-1, keepdims=True) + EPS)
    return rs, xc * rs


def _sgu_fwd_call(ps, ln_g, ln_b, w_sp, b_sp, *, name):
    S = ps.shape[0]
    Ts = min(SGU_TILE, S)

    def body(ps_ref, lg_ref, lb_ref, w_ref, b_ref, y_ref):
        mask = _sgu_mask()
        for g in range(GROUPS):
            wm = jnp.where(mask, w_ref[g], 0.0).astype(BF16)
            for p in range(Ts // SBLOCK):
                rows = pl.ds(p * SBLOCK, SBLOCK)
                u = _gelu(ps_ref[rows, pl.ds(g * DG, DG)].astype(F32))
                _, xh = _ln_stats(_gelu(ps_ref[rows, pl.ds(D_MODEL + g * DG, DG)].astype(F32)))
                vn = xh * lg_ref[g] + lb_ref[g]
                mixed = _dot(wm, vn.astype(BF16), NN) + b_ref[g]
                y_ref[rows, pl.ds(g * DG, DG)] = (u * mixed).astype(BF16)

    full3 = lambda a, b, c: pl.BlockSpec((a, b, c), lambda t: (0, 0, 0))
    return pl.pallas_call(
        body, name=name, grid=(S // Ts,),
        in_specs=[pl.BlockSpec((Ts, 2 * D_MODEL), lambda t: (t, 0)),
                  full3(GROUPS, 1, DG), full3(GROUPS, 1, DG), full3(GROUPS, SBLOCK, SBLOCK), full3(GROUPS, SBLOCK, 1)],
        out_specs=pl.BlockSpec((Ts, D_MODEL), lambda t: (t, 0)),
        out_shape=pltpu.HBM((S, D_MODEL), BF16),
        compiler_params=_cp(("parallel",)),
    )(*_hbm(ps, ln_g, ln_b, w_sp, b_sp))


def _sgu_bwd_call(ps, ln_g, ln_b, w_sp, b_sp, dy, *, name):
    S = ps.shape[0]
    Ts = min(SGU_TILE, S)

    def body(ps_ref, lg_ref, lb_ref, w_ref, b_ref, dy_ref, ds_ref, dlg_ref, dlb_ref, dw_ref, db_ref):
        @pl.when(pl.program_id(0) == 0)
        def _():
            dlg_ref[...] = jnp.zeros_like(dlg_ref)
            dlb_ref[...] = jnp.zeros_like(dlb_ref)
            dw_ref[...] = jnp.zeros_like(dw_ref)
            db_ref[...] = jnp.zeros_like(db_ref)

        mask = _sgu_mask()
        for g in range(GROUPS):
            wm = jnp.where(mask, w_ref[g], 0.0).astype(BF16)
            lg = lg_ref[g]
            for p in range(Ts // SBLOCK):
                rows = pl.ds(p * SBLOCK, SBLOCK)
                su = ps_ref[rows, pl.ds(g * DG, DG)].astype(F32)
                sv = ps_ref[rows, pl.ds(D_MODEL + g * DG, DG)].astype(F32)
                u, du = _gelu_and_grad(su)
                gv, dgv = _gelu_and_grad(sv)
                rs, xh = _ln_stats(gv)
                vn16 = (xh * lg + lb_ref[g]).astype(BF16)
                mixed = _dot(wm, vn16, NN) + b_ref[g]
                dyv = dy_ref[rows, pl.ds(g * DG, DG)].astype(F32)
                ds_ref[rows, pl.ds(g * DG, DG)] = (dyv * mixed * du).astype(BF16)
                dmix = dyv * u
                dmix16 = dmix.astype(BF16)
                db_ref[g] += jnp.sum(dmix, axis=-1, keepdims=True)
                dw_ref[g] += jnp.where(mask, _dot(dmix16, vn16, NT), 0.0)
                dvn = _dot(wm, dmix16, TN)
                dlg_ref[g] += jnp.sum(dvn * xh, axis=0, keepdims=True)
                dlb_ref[g] += jnp.sum(dvn, axis=0, keepdims=True)
                dxh = dvn * lg
                dvf = rs * (dxh - jnp.mean(dxh, axis=-1, keepdims=True)
                            - xh * jnp.mean(dxh * xh, axis=-1, keepdims=True))
                ds_ref[rows, pl.ds(D_MODEL + g * DG, DG)] = (dvf * dgv).astype(BF16)

    full3 = lambda a, b, c: pl.BlockSpec((a, b, c), lambda t: (0, 0, 0))
    tiles = [((Ts, 2 * D_MODEL), BF16), ((Ts, D_MODEL), BF16), ((Ts, 2 * D_MODEL), BF16)]
    return pl.pallas_call(
        body, name=name, grid=(S // Ts,),
        in_specs=[pl.BlockSpec((Ts, 2 * D_MODEL), lambda t: (t, 0)),
                  full3(GROUPS, 1, DG), full3(GROUPS, 1, DG), full3(GROUPS, SBLOCK, SBLOCK), full3(GROUPS, SBLOCK, 1),
                  pl.BlockSpec((Ts, D_MODEL), lambda t: (t, 0))],
        out_specs=[pl.BlockSpec((Ts, 2 * D_MODEL), lambda t: (t, 0)),
                   full3(GROUPS, 1, DG), full3(GROUPS, 1, DG), full3(GROUPS, SBLOCK, SBLOCK), full3(GROUPS, SBLOCK, 1)],
        out_shape=[pltpu.HBM((S, 2 * D_MODEL), BF16),
                   jax.ShapeDtypeStruct((GROUPS, 1, DG), F32), jax.ShapeDtypeStruct((GROUPS, 1, DG), F32),
                   jax.ShapeDtypeStruct((GROUPS, SBLOCK, SBLOCK), F32),
                   jax.ShapeDtypeStruct((GROUPS, SBLOCK, 1), F32)],
        compiler_params=_cp(("arbitrary",), _vmem_limit(tiles, [])),
    )(*_hbm(ps, ln_g, ln_b, w_sp, b_sp, dy))


def _position():
    return lax.axis_index("x"), lax.axis_index("y"), lax.axis_index("c")


def _gather_copies(srcs, dsts, send_sems, recv_sems, local_sems):
    x, y, c = _position()
    me, sibling = (x, y, c), (x, y, 1 - c)
    chips = [(1 - x, y), (x, 1 - y), (1 - x, 1 - y)]
    n = len(srcs)

    def slab(a, block):
        px, py, pc = block
        return dsts[a].at[4 * px + 2 * py + pc]

    def copy(a, k, block, to, src=None):
        return pltpu.make_async_remote_copy(
            src_ref=slab(a, block) if src is None else src, dst_ref=slab(a, block),
            send_sem=send_sems.at[7 * a + k], recv_sem=recv_sems.at[7 * a + k], device_id=to, device_id_type=MESH)

    mine = [pltpu.make_async_copy(srcs[a], slab(a, me), local_sems.at[a]) for a in range(n)]
    for cp in mine:
        cp.start()
    first = []
    for a in range(n):
        first.append(copy(a, 0, me, sibling, src=srcs[a]))
        first += [copy(a, 1 + j, me, (*chip, c), src=srcs[a]) for j, chip in enumerate(chips)]
    for cp in first:
        cp.start()
    passed = []
    for j, chip in enumerate(chips):
        for a in range(n):
            copy(a, 1 + j, (*chip, c), me).wait_recv()
            cp = copy(a, 4 + j, (*chip, c), sibling)
            cp.start()
            passed.append(cp)
    for a in range(n):
        copy(a, 0, sibling, me).wait_recv()
        for j, chip in enumerate(chips):
            copy(a, 4 + j, (*chip, 1 - c), me).wait_recv()
    for cp in first + passed:
        cp.wait_send()
    for cp in mine:
        cp.wait()


def _gather_copies_relayed(srcs, dsts, send_sems, recv_sems, local_sems, waves=1):
    x, y, c = _position()
    me, sibling = (x, y, c), (x, y, 1 - c)
    x_chip, y_chip, d_chip = (1 - x, y), (x, 1 - y), (1 - x, 1 - y)
    south = c == 0
    relay_from = (jnp.where(south, x, 1 - x), jnp.where(south, 1 - y, y), c)
    relay_to = (jnp.where(south, 1 - x, x), jnp.where(south, y, 1 - y), c)
    n = len(srcs)

    def cut(a, w):
        rows = srcs[a].shape[0]
        step = (rows // waves) // 16 * 16
        if waves == 1 or step == 0:
            return pl.ds(0, rows) if w == 0 else None
        return pl.ds(w * step, step if w < waves - 1 else rows - w * step)

    def slab(a, block, w):
        px, py, pc = block
        return dsts[a].at[4 * px + 2 * py + pc, cut(a, w)]

    def copy(a, k, w, block, to, own=False):
        sem = (7 * a + k) * waves + w
        return pltpu.make_async_remote_copy(
            src_ref=srcs[a].at[cut(a, w)] if own else slab(a, block, w), dst_ref=slab(a, block, w),
            send_sem=send_sems.at[sem], recv_sem=recv_sems.at[sem], device_id=to, device_id_type=MESH)

    pieces = [(a, w) for w in range(waves) for a in range(n) if cut(a, w) is not None]
    mine = [pltpu.make_async_copy(srcs[a], dsts[a].at[4 * x + 2 * y + c], local_sems.at[a]) for a in range(n)]
    for cp in mine:
        cp.start()
    sent = []
    for a, w in pieces:
        sent += [copy(a, 0, w, me, sibling, own=True), copy(a, 1, w, me, (*x_chip, c), own=True),
                 copy(a, 2, w, me, (*y_chip, c), own=True)]
    for cp in sent:
        cp.start()
    for a, w in pieces:
        copy(a, 1, w, (*x_chip, c), me).wait_recv()
        copy(a, 2, w, (*y_chip, c), me).wait_recv()
        later = [copy(a, 3, w, relay_from, relay_to), copy(a, 4, w, (*x_chip, c), sibling),
                 copy(a, 5, w, (*y_chip, c), sibling)]
        for cp in later:
            cp.start()
        sent += later
    for a, w in pieces:
        copy(a, 3, w, (*d_chip, c), me).wait_recv()
        cp = copy(a, 6, w, (*d_chip, c), sibling)
        cp.start()
        sent.append(cp)
    for a, w in pieces:
        copy(a, 0, w, sibling, me).wait_recv()
        for k, chip in ((4, x_chip), (5, y_chip), (6, d_chip)):
            copy(a, k, w, (*chip, 1 - c), me).wait_recv()
    for cp in sent:
        cp.wait_send()
    for cp in mine:
        cp.wait()


FLIPS = [(fx, fy, fc) for fx in (0, 1) for fy in (0, 1) for fc in (0, 1)][1:]


def _scatter_copies(srcs, dsts, send_sems, recv_sems, local_sems, waves=1):
    n = len(srcs)
    x, y, c = _position()
    me = 4 * x + 2 * y + c
    mine = [pltpu.make_async_copy(srcs[a].at[me], dsts[a].at[me], local_sems.at[a]) for a in range(n)]
    for cp in mine:
        cp.start()
    peers = []
    for fx, fy, fc in FLIPS:
        tx = 1 - x if fx else x
        ty = 1 - y if fy else y
        tc = 1 - c if fc else c
        peers.append(((tx, ty, tc), 4 * tx + 2 * ty + tc))
    copies = []
    for w in range(waves):
        wave = []
        for k, (peer_id, peer) in enumerate(peers):
            for a in range(n):
                rows = srcs[a].shape[1]
                step = rows if waves == 1 else (rows // waves) // 16 * 16
                r0 = w * step
                cut = pl.ds(r0, step if w < waves - 1 else rows - r0)
                sem = (7 * a + k) * waves + w
                cp = pltpu.make_async_remote_copy(
                    src_ref=srcs[a].at[peer, cut], dst_ref=dsts[a].at[me, cut],
                    send_sem=send_sems.at[sem], recv_sem=recv_sems.at[sem],
                    device_id=peer_id, device_id_type=MESH)
                cp.start()
                wave.append(cp)
        for cp in wave:
            cp.wait_send()
        copies += wave
    for cp in copies:
        cp.wait_recv()
    for cp in mine:
        cp.wait()


def _pair_copies(srcs, dsts, send_sems, recv_sems, local_sems):
    x, y, c = _position()
    copies = [pltpu.make_async_remote_copy(
        src_ref=srcs[a], dst_ref=dsts[a], send_sem=send_sems.at[a], recv_sem=recv_sems.at[a],
        device_id=(x, y, 1 - c), device_id_type=MESH) for a in range(len(srcs))]
    for cp in copies:
        cp.start()
    for cp in copies:
        cp.wait()


def _chip_scatter_copies(srcs, dsts, send_sems, recv_sems, local_sems, waves=1):
    n = len(srcs)
    x, y, c = _position()
    here = 2 * x + y
    mine = [pltpu.make_async_copy(srcs[a].at[here], dsts[a].at[here], local_sems.at[a]) for a in range(n)]
    for cp in mine:
        cp.start()
    peers = []
    for fx, fy in ((1, 0), (0, 1), (1, 1)):
        tx = 1 - x if fx else x
        ty = 1 - y if fy else y
        peers.append(((tx, ty, c), 2 * tx + ty))
    copies = []
    for w in range(waves):
        wave = []
        for k, (peer_id, peer) in enumerate(peers):
            for a in range(n):
                rows = srcs[a].shape[1]
                step = rows if waves == 1 else (rows // waves) // 16 * 16
                r0 = w * step
                cut = pl.ds(r0, step if w < waves - 1 else rows - r0)
                sem = (3 * a + k) * waves + w
                cp = pltpu.make_async_remote_copy(
                    src_ref=srcs[a].at[peer, cut], dst_ref=dsts[a].at[here, cut],
                    send_sem=send_sems.at[sem], recv_sem=recv_sems.at[sem],
                    device_id=peer_id, device_id_type=MESH)
                cp.start()
                wave.append(cp)
        for cp in wave:
            cp.wait_send()
        copies += wave
    for cp in copies:
        cp.wait_recv()
    for cp in mine:
        cp.wait()


def _comm_sems(n, waves=1):
    return [pltpu.SemaphoreType.DMA((7 * n * waves,)), pltpu.SemaphoreType.DMA((7 * n * waves,)),
            pltpu.SemaphoreType.DMA((n,))]


def _handshake(peers):
    barrier = pltpu.get_barrier_semaphore()
    for peer in peers:
        pl.semaphore_signal(barrier, inc=1, device_id=peer, device_id_type=MESH)
    pl.semaphore_wait(barrier, len(peers))


def _sequencer_call(arrays, out_types, copies_fn, peers_fn, *, name, collective_id, waves=1):
    n = len(arrays)
    srcs = [jax.new_ref(a, memory_space=pltpu.MemorySpace.HBM) for a in arrays]
    dsts = [jax.empty_ref(t, memory_space=pltpu.MemorySpace.HBM) for t in out_types]
    extra = {} if waves == 1 else {"waves": waves}

    @pl.kernel(mesh=plsc.ScalarSubcoreMesh(axis_name="sequencer", num_cores=1), name=name,
               scratch_types=_comm_sems(n, waves), compiler_params=pltpu.CompilerParams(collective_id=collective_id))
    def launch(send_sems, recv_sems, local_sems):
        _handshake(peers_fn())
        copies_fn(srcs, dsts, send_sems, recv_sems, local_sems, **extra)

    launch()
    return [d[...] for d in dsts]


def _all_other_devices():
    x, y, c = _position()
    return [(1 - x if fx else x, 1 - y if fy else y, 1 - c if fc else c) for fx, fy, fc in FLIPS]


def _gather_relay_peers():
    x, y, c = _position()
    return [(x, y, 1 - c), (1 - x, y, c), (x, 1 - y, c)]


def _gather_peers():
    x, y, c = _position()
    return [(x, y, 1 - c), (1 - x, y, c), (x, 1 - y, c), (1 - x, 1 - y, c)]


def _small_gather_async(shards, *, name, collective_id):
    return _sequencer_call(shards, [jax.ShapeDtypeStruct((N_DEV, *s.shape), s.dtype) for s in shards],
                           _gather_copies, _gather_peers, name=name, collective_id=collective_id)


def _sibling():
    x, y, c = _position()
    return [(x, y, 1 - c)]


def _same_core_of_other_chips():
    x, y, c = _position()
    return [(1 - x, y, c), (x, 1 - y, c), (1 - x, 1 - y, c)]


def _pair_exchange_async(parts, *, name, collective_id):
    return _sequencer_call(parts, [jax.ShapeDtypeStruct(p.shape, p.dtype) for p in parts],
                           _pair_copies, _sibling, name=name, collective_id=collective_id)


def _chip_scatter_async(parts, *, name, collective_id, waves=1):
    return _sequencer_call(parts, [jax.ShapeDtypeStruct(p.shape, p.dtype) for p in parts],
                           _chip_scatter_copies, _same_core_of_other_chips, name=name, collective_id=collective_id,
                           waves=waves)


def _pair_sum_call(mine, theirs, *, name, tw=D_MODEL):
    n, r, w = mine.shape

    def body(a_ref, b_ref, o_ref):
        o_ref[...] = (a_ref[...].astype(F32) + b_ref[...].astype(F32)).astype(o_ref.dtype)

    spec = pl.BlockSpec((None, r, tw), lambda i, j: (i, 0, j))
    return pl.pallas_call(
        body, name=name, grid=(n, w // tw), in_specs=[spec, spec], out_specs=spec,
        out_shape=pltpu.HBM(mine.shape, mine.dtype),
        compiler_params=_cp(("parallel", "parallel")),
    )(*_hbm(mine, theirs))


def _scatter_blocks_async(parts, *, name, collective_id, waves=1):
    return _sequencer_call(parts, [jax.ShapeDtypeStruct(p.shape, p.dtype) for p in parts],
                           _scatter_copies, _all_other_devices, name=name, collective_id=collective_id, waves=waves)


def _all_gather_async(shards, *, name, collective_id, waves=1):
    return _sequencer_call(shards, [jax.ShapeDtypeStruct((N_DEV, *s.shape), s.dtype) for s in shards],
                           _gather_copies_relayed, _gather_relay_peers, name=name, collective_id=collective_id,
                           waves=waves)


def _adamw_math(w, g, m, v):
    m = ADAM_B1 * m + (1.0 - ADAM_B1) * g
    v = ADAM_B2 * v + (1.0 - ADAM_B2) * (g * g)
    m_hat = m / (1.0 - ADAM_B1 ** ADAM_STEP)
    v_hat = v / (1.0 - ADAM_B2 ** ADAM_STEP)
    delta = -ADAM_LR * (m_hat / (jnp.sqrt(v_hat) + ADAM_EPS) + ADAM_WD * w)
    return delta, m, v


def _adamw_reduce_call(recv, w, m, v, *, name, T=128):
    R, W = w.shape
    n_parts = recv.shape[0]
    if R % T == 0:
        tr, tw = T, W
    elif (R // 2) % 16 == 0:
        tr, tw = R // 2, W
    else:
        tr, tw = R, 2 * LANES

    def body(p_ref, w_ref, m_ref, v_ref, g_out, d_out, m_out, v_out):
        g = p_ref[0].astype(F32)
        for d in range(1, n_parts):
            g = g + p_ref[d].astype(F32)
        g_out[...] = g
        d_out[...], m_out[...], v_out[...] = _adamw_math(w_ref[...], g, m_ref[...], v_ref[...])

    row = pl.BlockSpec((tr, tw), lambda i, j: (i, j))
    out = pltpu.HBM((R, W), F32)
    return pl.pallas_call(
        body, name=name, grid=(R // tr, W // tw),
        in_specs=[pl.BlockSpec((n_parts, tr, tw), lambda i, j: (0, i, j)), row, row, row],
        out_specs=[row] * 4, out_shape=[out] * 4,
        compiler_params=_cp(("parallel", "parallel"), VMEM_BIG),
    )(*_hbm(recv, w, m, v))


SMALL_EARLY = (("b_gate", 8), ("w_spatial", 512), ("b_spatial", 8), ("norm_post_mix", 8), ("norm_pre_ffn", 8),
               ("norm_post_ffn", 8), ("w_gate_up", 128), ("gla_norm", 64), ("sgu_ln_g", 64), ("sgu_ln_b", 64))
SMALL_EARLY_AT = {}
for _name, _rows in SMALL_EARLY:
    SMALL_EARLY_AT[_name] = sum(r for _, r in SMALL_EARLY[:len(SMALL_EARLY_AT)])
SMALL_EARLY_ROWS = sum(r for _, r in SMALL_EARLY)
SMALL_LATE_ROWS = 16


def _small_early_rows(grads):
    def rows(a, n_rows):
        a = a.reshape(-1, LANES)
        return jnp.pad(a, ((0, n_rows - a.shape[0]), (0, 0)))

    def device_major(a, n_rows):
        r, c = a.shape[0], a.shape[1] // N_DEV
        a = a.reshape(r, N_DEV, c).transpose(1, 0, 2)
        a = jnp.pad(a, ((0, 0), (0, n_rows // N_DEV - r), (0, LANES - c)))
        return a.reshape(n_rows, LANES)

    pieces = []
    for name, n_rows in SMALL_EARLY:
        g = grads[name]
        if name in SMALL_SHARDED:
            pieces.append(device_major(g.reshape(g.shape[0], -1) if g.ndim == 2 else g.reshape(g.shape[0], g.shape[-1]), n_rows))
        else:
            pieces.append(rows(g, n_rows))
    return jnp.concatenate(pieces, axis=0)


def _small_update_call(got_early, got_late, w, m, v, *, name):
    names = list(SMALL)
    n_p = len(names)
    E = SMALL_EARLY_ROWS

    def body(early_ref, late_ref, *rest):
        w_refs, m_refs, v_refs = [dict(zip(names, rest[i * n_p:(i + 1) * n_p])) for i in range(3)]
        outs, tot = rest[3 * n_p:-1], rest[-1]
        loss_out = outs[0]
        g_out, d_out, m_out, v_out = [dict(zip(names, outs[1 + i * n_p:1 + (i + 1) * n_p])) for i in range(4)]
        acc, acc_late = early_ref[0], late_ref[0]
        for d in range(1, N_DEV):
            acc, acc_late = acc + early_ref[d], acc_late + late_ref[d]
        tot[0:E, :] = acc
        tot[E:E + SMALL_LATE_ROWS, :] = acc_late
        loss_out[...] = tot[E + 8:E + 9, :]

        x, y, c = _position()
        me = 4 * x + 2 * y + c

        def update(name, g, ix):
            g_out[name][ix] = g
            d_out[name][ix], m_out[name][ix], v_out[name][ix] = _adamw_math(
                w_refs[name][ix], g, m_refs[name][ix], v_refs[name][ix])

        for name in names:
            shape = w[name].shape
            if name in SMALL_SHARDED:
                per_dev = dict(SMALL_EARLY)[name] // N_DEV
                at = pl.multiple_of(SMALL_EARLY_AT[name] + me * per_dev, 8)
                g = tot[pl.ds(at, per_dev), :]
                update(name, g[:shape[1], :shape[2]], (0,))
            elif name == "w_spatial":
                for grp in range(GROUPS):
                    at = SMALL_EARLY_AT[name] + grp * SBLOCK
                    update(name, tot[at:at + SBLOCK, :], (0, grp))
            elif name == "b_spatial":
                at = SMALL_EARLY_AT[name]
                update(name, tot[at:at + GROUPS, :], (0,))
            else:
                at = E if name == "norm_pre_mix" else SMALL_EARLY_AT[name]
                for k in range(shape[1] // LANES):
                    update(name, tot[at + k:at + k + 1, :], (slice(None), pl.ds(k * LANES, LANES)))

    state = [s[n] for s in (w, m, v) for n in names]
    out_shapes = [jax.ShapeDtypeStruct((1, LANES), F32)] + [jax.ShapeDtypeStruct(w[n].shape, F32) for n in names] * 4
    outs = pl.pallas_call(
        body, name=name,
        in_specs=[VMEM_SPEC] * (2 + len(state)), out_specs=[VMEM_SPEC] * len(out_shapes), out_shape=out_shapes,
        scratch_shapes=[pltpu.VMEM((E + SMALL_LATE_ROWS, LANES), F32)],
    )(got_early, got_late, *state)
    per_name = {n: tuple(outs[1 + i * n_p + j] for i in range(4)) for j, n in enumerate(names)}
    return outs[0], per_name


def _tile_rows(n_elems):
    return -(-n_elems // (8 * LANES)) * 8


def _pack_rows(parts, rows):
    pieces = []
    for p in parts:
        q = p.reshape(-1, LANES)
        pieces.append(jnp.pad(q, ((0, _tile_rows(p.size) - q.shape[0]), (0, 0))))
    buf = jnp.concatenate(pieces, axis=0)
    return jnp.pad(buf, ((0, rows - buf.shape[0]), (0, 0)))


def _in_w_blocks_call(a, live, finished, *, name, tk=TOKEN_TILE):
    S = a.shape[0]
    tk = min(tk, S)
    steps = S // tk
    n_live = len(live)
    n_chips = N_DEV // 2
    out_shape = (n_chips, IN_BLK, D_MODEL)

    def body(a_ref, *rest):
        live_refs, done_refs = rest[:n_live], rest[n_live:n_live + len(finished)]
        keep_ref, send_ref = rest[n_live + len(finished):n_live + len(finished) + 2]
        accs = rest[n_live + len(finished) + 2:]
        k = pl.program_id(0)

        @pl.when(k == 0)
        def _():
            for acc in accs:
                acc[...] = jnp.zeros_like(acc)

        av = a_ref[...]
        for src, acc in zip(live_refs, accs):
            for m0 in range(0, src.shape[1], 1024):
                m1 = min(src.shape[1], m0 + 1024)
                acc[m0:m1, :] += _dot(src[:, m0:m1], av, TN)

        @pl.when(k == steps - 1)
        def _():
            groups = [(acc, cols) for acc, (_, cols) in zip(accs, live)]
            groups += [(ref, cols) for ref, (_, cols) in zip(done_refs, finished)]
            core = lax.axis_index("c")

            def cut(d, out_ref):
                lo, hi = IN_BLK * d, IN_BLK * (d + 1)
                for ref, (c0, c1) in groups:
                    s0, e0 = max(lo, c0), min(hi, c1)
                    if s0 < e0:
                        out_ref[d // 2, s0 - lo:e0 - lo, :] = ref[s0 - c0:e0 - c0, :].astype(BF16)

            for d in range(N_DEV):
                pl.when(core == d % 2)(lambda d=d: cut(d, keep_ref))
                pl.when(core != d % 2)(lambda d=d: cut(d, send_ref))

    acc_shapes = [(arr.shape[1], D_MODEL) for arr, _ in live]
    tiles = [((tk, D_MODEL), BF16)] + [((tk, arr.shape[1]), BF16) for arr, _ in live]
    resident = ([(arr.shape, arr.dtype) for arr, _ in finished] + [(out_shape, BF16)] * 2
                + [(s, F32) for s in acc_shapes])
    return pl.pallas_call(
        body, name=name, grid=(steps,),
        in_specs=[pl.BlockSpec((tk, D_MODEL), lambda k: (k, 0))]
        + [pl.BlockSpec((tk, arr.shape[1]), lambda k: (k, 0)) for arr, _ in live]
        + [_res(arr.shape) for arr, _ in finished],
        out_specs=[_acc(out_shape)] * 2,
        out_shape=[pltpu.HBM(out_shape, BF16)] * 2,
        scratch_shapes=[pltpu.VMEM(s, F32) for s in acc_shapes],
        compiler_params=_cp(("arbitrary",), _vmem_limit(tiles, resident)),
    )(*_hbm(a, *[arr for arr, _ in live], *[arr for arr, _ in finished]))


def _local_step(x, target, W, scatter, finish, pair, chip_scatter):
    g1, g2, g3, g4 = [W[n].reshape(1, D_MODEL) for n in ("norm_pre_mix", "norm_post_mix", "norm_pre_ffn", "norm_post_ffn")]
    wfi, wfo = W["w_ffn_in"].reshape(2 * D_FF, D_MODEL), W["w_ffn_out"]
    wg = jnp.pad(W["w_gate_up"], ((0, LANES - RANK), (0, 0))).astype(BF16)
    bgate = W["b_gate"].reshape(1, QK)
    gnorm = W["gla_norm"].reshape(HEADS, 1, DV)
    ln_g = W["sgu_ln_g"].reshape(GROUPS, 1, DG)
    ln_b = W["sgu_ln_b"].reshape(GROUPS, 1, DG)
    w_sp = W["w_spatial"]
    b_sp = W["b_spatial"].reshape(GROUPS, SBLOCK, 1)

    a, pa, alow, ps, pg, w_in_t = _in_proj_call(x, g1, W["w_in_blocks"], name="in_proj")
    y_gla, states = _gla_fwd_call(pa, alow, wg, bgate, gnorm, name="gla_fwd")
    y_sgu = _sgu_fwd_call(ps, ln_g, ln_b, w_sp, b_sp, name="sgu_fwd")
    t1, t2, merged, mix, x1, h = _mixer_tail_call(y_gla, y_sgu, pg, x, W["w_branch_gla"], W["w_branch_sgu"],
                                                  W["w_out"], g2, g3, name="mixer_tail")
    gu, z = _ffn_in_call(h, wfi, name="ffn_in")
    loss, dx2, dy, dg4 = _ffn_out_loss_call(z, wfo, x1, target, g4, name="ffn_out_loss")

    grads = {"norm_post_ffn": dg4}
    done = {}
    dgu = _ffn_out_bwd_call(dy, wfo, gu, name="ffn_out_bwd")
    dw_ffn_out = _weight_grads_call([z, dy], [(0, 1)], name="d_ffn_out_w")[0].reshape(N_DEV, D_FF // N_DEV, D_MODEL)
    dw_ffn_in = _ffn_in_grad_call(h, dgu, name="d_ffn_in_w").reshape(N_DEV, FF_BLK, D_MODEL)
    dgu, dw_ffn_out, dw_ffn_in = lax.optimization_barrier((dgu, dw_ffn_out, dw_ffn_in))
    ffn_received = scatter(("w_ffn_out", "w_ffn_in"), [dw_ffn_out, dw_ffn_in])
    dx1, dmix, grads["norm_pre_ffn"], grads["norm_post_mix"] = _ffn_in_bwd_call(dgu, wfi, dx2, x1, mix, g3, g2, name="ffn_in_bwd")
    dt1, dt2, dpg, dy_gla, dy_sgu = _mixer_bwd_call(dmix, t1, t2, pg, W["w_out"], W["w_branch_gla"], W["w_branch_sgu"],
                                                    name="mixer_bwd")
    rows = D_MODEL // N_DEV
    mixer_grads = _weight_grads_call([merged, dmix, y_gla, dt1, y_sgu, dt2], [(0, 1), (2, 3), (4, 5)], name="d_mixer_w")
    dy_gla, dy_sgu, mixer_grads = lax.optimization_barrier((dy_gla, dy_sgu, mixer_grads))
    mixer_received = scatter(("w_out", "w_branch_gla", "w_branch_sgu"),
                             [g.reshape(N_DEV, rows, D_MODEL) for g in mixer_grads])
    dps, dlg, dlb, dwsp, dbsp = _sgu_bwd_call(ps, ln_g, ln_b, w_sp, b_sp, dy_sgu, name="sgu_bwd")
    dw_s, dw_g = _weight_grads_call([a, dps, dpg], [(1, 0), (2, 0)], name="d_in_w_sgu_gates")
    dy_gla, dw_s, dw_g = lax.optimization_barrier((dy_gla, dw_s, dw_g))
    dpa, dlow, dwg, dgn, dbg = _gla_bwd_call(pa, alow, wg, bgate, gnorm, states, dy_gla, name="gla_bwd")
    ffn_received, mixer_received, dpa, dlow = lax.optimization_barrier((ffn_received, mixer_received, dpa, dlow))
    keep, send = _in_w_blocks_call(a, [(dpa, A_COLS), (dlow, LOW_COLS)], [(dw_s, S_COLS), (dw_g, G_COLS)],
                                   name="d_in_w_blocks")
    ffn_received, mixer_received, send = lax.optimization_barrier((ffn_received, mixer_received, send))
    from_sibling = pair(send)
    done.update({**finish(ffn_received), **finish(mixer_received)})
    done, keep = lax.optimization_barrier((done, keep))
    chip_sum = _pair_sum_call(keep, from_sibling, name="pair_sum_w_in")
    chip_sum, dpa = lax.optimization_barrier((chip_sum, dpa))
    in_received = chip_scatter(chip_sum)
    grad_x, grads["norm_pre_mix"] = _in_proj_bwd_call(dpa, dlow, dps, dpg, w_in_t, x, dx1, g1, name="in_proj_bwd")

    grads["w_gate_up"] = dwg[:RANK]
    grads["b_gate"] = dbg
    grads["gla_norm"] = dgn
    grads["sgu_ln_g"] = dlg
    grads["sgu_ln_b"] = dlb
    grads["w_spatial"] = dwsp
    grads["b_spatial"] = dbsp
    done.update(finish({"w_in": in_received}))
    return loss, grad_x, grads, done


WEIGHTS = ("norm_pre_mix", "w_in", "w_gate_up", "b_gate", "gla_norm", "sgu_ln_g", "sgu_ln_b", "w_spatial",
           "b_spatial", "w_branch_gla", "w_branch_sgu", "w_out", "norm_post_mix", "norm_pre_ffn", "w_ffn_in",
           "w_ffn_out", "norm_post_ffn")
BIG = ("w_in", "w_branch_gla", "w_branch_sgu", "w_out", "w_ffn_in", "w_ffn_out")
MIXER = ("w_branch_gla", "w_branch_sgu", "w_out")
FFN = ("w_ffn_in", "w_ffn_out")
COLUMN_SHARDED = ("w_in", "w_ffn_in")
LATE_SCATTER_WAVES = 4
GATHER_WAVES = 8
SMALL = tuple(n for n in WEIGHTS if n not in BIG)
SMALL_SHARDED = ("w_gate_up", "gla_norm", "sgu_ln_g", "sgu_ln_b")
SMALL_GATHER_ROWS = 32


def kernel(x, norm_pre_mix, w_in, w_gate_up, b_gate, gla_norm, sgu_ln_g, sgu_ln_b, w_spatial, b_spatial, w_branch_gla, w_branch_sgu, w_out, norm_post_mix, norm_pre_ffn, w_ffn_in, w_ffn_out, norm_post_ffn, loss_target, m_norm_pre_mix, m_w_in, m_w_gate_up, m_b_gate, m_gla_norm, m_sgu_ln_g, m_sgu_ln_b, m_w_spatial, m_b_spatial, m_w_branch_gla, m_w_branch_sgu, m_w_out, m_norm_post_mix, m_norm_pre_ffn, m_w_ffn_in, m_w_ffn_out, m_norm_post_ffn, v_norm_pre_mix, v_w_in, v_w_gate_up, v_b_gate, v_gla_norm, v_sgu_ln_g, v_sgu_ln_b, v_w_spatial, v_b_spatial, v_w_branch_gla, v_w_branch_sgu, v_w_out, v_norm_post_mix, v_norm_pre_ffn, v_w_ffn_in, v_w_ffn_out, v_norm_post_ffn):
    given = dict(locals())
    def local(a, n):
        return a[0].T if n in COLUMN_SHARDED else a[0]

    w = {n: local(given[n], n) for n in WEIGHTS}
    m = {n: local(given["m_" + n], n) for n in WEIGHTS}
    v = {n: local(given["v_" + n], n) for n in WEIGHTS}
    xs, target = x[0], loss_target[0]

    small_shard = _pack_rows([w[n] for n in SMALL_SHARDED], SMALL_GATHER_ROWS)
    first = _all_gather_async([w["w_in"].astype(BF16), small_shard], name="gather_w_in", collective_id=2,
                              waves=GATHER_WAVES)
    rest, first, m["w_in"], v["w_in"] = lax.optimization_barrier(
        ([w[n].astype(BF16) for n in MIXER + FFN], first, m["w_in"], v["w_in"]))
    rest_blocks = _all_gather_async(rest, name="gather_rest", collective_id=1)
    W = {n: w[n] for n in SMALL if n not in SMALL_SHARDED}
    blocks = dict(zip(MIXER + FFN, rest_blocks))
    for n in ("w_branch_gla", "w_branch_sgu", "w_out", "w_ffn_out"):
        W[n] = blocks[n].reshape(-1, D_MODEL)
    W["w_ffn_in"] = blocks["w_ffn_in"]
    W["w_in_blocks"] = first[0]
    small_blocks = first[1]
    off = 0
    for n in SMALL_SHARDED:
        r, c = w[n].shape
        blk = small_blocks[:, off:off + r * c // LANES].reshape(N_DEV, r, c)
        W[n] = blk.transpose(1, 0, 2).reshape(r, N_DEV * c)
        off += _tile_rows(r * c)

    scatter_ids = iter((3, 4))

    def scatter(names, parts):
        got = _scatter_blocks_async(parts, name="scatter_" + "_".join(names), collective_id=next(scatter_ids))
        return dict(zip(names, got))

    def finish(received):
        return {n: _adamw_reduce_call(r, w[n], m[n], v[n], name="adamw_" + n) for n, r in received.items()}

    def pair(part):
        return _pair_exchange_async([part], name="pair_w_in", collective_id=5)[0]

    def chip_scatter(part):
        return _chip_scatter_async([part], name="scatter_w_in", collective_id=6, waves=LATE_SCATTER_WAVES)[0]

    loss_part, grad_x, grads, big_done = _local_step(xs, target, W, scatter, finish, pair, chip_scatter)

    late = jnp.concatenate([grads["norm_pre_mix"].reshape(8, LANES), jnp.broadcast_to(loss_part, (8, LANES))], axis=0)
    gathered = _small_gather_async([_small_early_rows(grads), late], name="gather_small", collective_id=7)
    gathered, big_done["w_in"] = lax.optimization_barrier((gathered, big_done["w_in"]))
    loss_row, small_done = _small_update_call(
        *gathered, *[{n: given[prefix + n] for n in SMALL} for prefix in ("", "m_", "v_")], name="small_update")

    def pick(i):
        return [small_done[n][i] if n in SMALL else (big_done[n][i].T if n in COLUMN_SHARDED else big_done[n][i])[None]
                for n in WEIGHTS]

    return (loss_row[0, 0], grad_x[None], *pick(0), *pick(1), *pick(2), *pick(3))
```

```python
import jax
import jax.numpy as jnp
from jax import lax
from jax.experimental import pallas as pl
from jax.experimental.pallas import tpu as pltpu
from jax.experimental.pallas import tpu_sc as plsc

F32 = jnp.float32
BF16 = jnp.bfloat16

D_MODEL = 1024
N_DEV = 8
CHUNK = 64
HEADS = 4
DK = 128
DV = 256
QK = HEADS * DK
GV = HEADS * DV
RANK = 16
GROUPS = 4
SBLOCK = 128
DG = 256
D_FF = 2816
FF_BLK = 704
EPS = 1e-6
Q_SCALE = DK ** -0.5
LANES = 128
VMEM_BIG = 48 * 1024 * 1024

D_IN = 7184
IN_BLK = 898
A_COLS = (0, 3072)
LOW_COLS = (3072, 3088)
S_COLS = (3088, 5136)
G_COLS = (5136, 7184)

ADAM_LR = 0.001
ADAM_B1 = 0.9
ADAM_B2 = 0.999
ADAM_EPS = 1e-08
ADAM_WD = 0.01
ADAM_STEP = 10

MESH = pl.DeviceIdType.MESH
ANY = pl.BlockSpec(memory_space=pl.ANY)
VMEM_SPEC = pl.BlockSpec(memory_space=pltpu.VMEM)


def _cp(sem=None, vmem=None):
    return pltpu.CompilerParams(dimension_semantics=sem, vmem_limit_bytes=vmem)


def _hbm(*arrays):
    return [pltpu.with_memory_space_constraint(a, pltpu.HBM) for a in arrays]


def _sig(x):
    return 0.5 * jnp.tanh(0.5 * x) + 0.5


GELU_C = 0.7978845608028654
GELU_A = 0.044715


def _gelu(x):
    t = jnp.tanh((GELU_C * x) * (1.0 + GELU_A * (x * x)))
    return (0.5 * x) * (1.0 + t)


def _gelu_and_grad(x):
    x2 = x * x
    t = jnp.tanh((GELU_C * x) * (1.0 + GELU_A * x2))
    one_t = 1.0 + t
    hx = 0.5 * x
    grad = 0.5 * one_t + (hx * (1.0 - t * t)) * (GELU_C + (3.0 * GELU_A * GELU_C) * x2)
    return hx * one_t, grad


def _logsig(x):
    return jnp.minimum(x, 0.0) - jnp.log1p(jnp.exp(-jnp.abs(x)))


def _dot(a, b, dims):
    return lax.dot_general(a, b, (dims, ((), ())), preferred_element_type=F32)


NN = ((1,), (0,))
NT = ((1,), (1,))
TN = ((0,), (0,))


def _exact_mask_dot(mask_bf16, x):
    hi = x.astype(BF16)
    r1 = x - hi.astype(F32)
    mid = r1.astype(BF16)
    lo = (r1 - mid.astype(F32)).astype(BF16)
    return _dot(mask_bf16, hi, NN) + _dot(mask_bf16, mid, NN) + _dot(mask_bf16, lo, NN)


ROW_TILE = 512
SUB_ROWS = 256


def _sub_tiles(T):
    return [pl.ds(r0, min(SUB_ROWS, T)) for r0 in range(0, T, SUB_ROWS)]


def _rt(T, W, c=0):
    return pl.BlockSpec((T, W), lambda i: (i, c))


def _rt3(nb, T, W):
    return pl.BlockSpec((nb, T, W), lambda i: (0, i, 0))


def _res(shape):
    nd = len(shape)
    return pl.BlockSpec(tuple(shape), lambda i: (0,) * nd, pipeline_mode=pl.Buffered(1))


def _acc(shape):
    nd = len(shape)
    return pl.BlockSpec(tuple(shape), lambda i: (0,) * nd, pipeline_mode=pl.Buffered(1))


def _nbytes(shape, dtype):
    n = jnp.dtype(dtype).itemsize
    for s in shape:
        n *= s
    return n


def _vmem_limit(tiles, resident, temps=16 * 1024 * 1024):
    need = 2 * sum(_nbytes(s, d) for s, d in tiles) + sum(_nbytes(s, d) for s, d in resident) + temps
    return min(need, 60 * 1024 * 1024)


def _tok_call(body, *, name, S, T, ins, outs, semantics="parallel"):
    tiles = [(spec.block_shape, a.dtype) for a, spec, kind in ins + outs if kind == "tile"]
    resident = [(a.shape, a.dtype) for a, spec, kind in ins + outs if kind == "res"]
    return pl.pallas_call(
        body, name=name, grid=(S // T,),
        in_specs=[spec for _, spec, _ in ins], out_specs=[spec for _, spec, _ in outs],
        out_shape=[pltpu.HBM(a.shape, a.dtype) for a, _, _ in outs],
        compiler_params=_cp((semantics,), _vmem_limit(tiles, resident)),
    )(*_hbm(*[a for a, _, _ in ins]))


def _tile(a, spec):
    return (a, spec, "tile")


def _whole(a):
    return (a, _res(a.shape), "res")


def _out_tile(shape, dtype, spec):
    return (jax.ShapeDtypeStruct(shape, dtype), spec, "tile")


def _out_acc(shape, dtype=F32):
    return (jax.ShapeDtypeStruct(shape, dtype), _acc(shape), "res")


def _rms_stats(x):
    r = lax.rsqrt(jnp.mean(x * x, axis=-1, keepdims=True) + EPS)
    return r, x * r


def _rms_bwd(xh, r, g, dy):
    dxh = dy * g
    dx = r * (dxh - xh * jnp.mean(dxh * xh, axis=-1, keepdims=True))
    dg = jnp.sum(dy * xh, axis=0, keepdims=True)
    return dx, dg


def _accum(ref, val):
    @pl.when(pl.program_id(0) == 0)
    def _():
        ref[...] = val

    @pl.when(pl.program_id(0) > 0)
    def _():
        ref[...] += val


def _dot_rows_t(a, w_ref, row0, o_ref, chunk=1024):
    n = o_ref.shape[1]
    for n0 in range(0, n, chunk):
        n1 = min(n, n0 + chunk)
        o_ref[:, n0:n1] = _dot(a, w_ref[row0 + n0:row0 + n1, :], NT).astype(o_ref.dtype)


def _in_proj_call(x, g1, w_blocks, *, name, T=ROW_TILE):
    S = x.shape[0]

    def body(x_ref, g_ref, blk_ref, a_ref, pa_ref, al_ref, ps_ref, pg_ref, w_ref):
        @pl.when(pl.program_id(0) == 0)
        def _():
            for d in range(N_DEV):
                w_ref[d * IN_BLK:(d + 1) * IN_BLK, :] = blk_ref[d]

        _, xh = _rms_stats(x_ref[...])
        a = (xh * g_ref[...]).astype(BF16)
        a_ref[...] = a
        _dot_rows_t(a, w_ref, A_COLS[0], pa_ref)
        _dot_rows_t(a, w_ref, LOW_COLS[0], al_ref)
        _dot_rows_t(a, w_ref, S_COLS[0], ps_ref)
        _dot_rows_t(a, w_ref, G_COLS[0], pg_ref)

    widths = (D_MODEL, A_COLS[1] - A_COLS[0], LANES, S_COLS[1] - S_COLS[0], G_COLS[1] - G_COLS[0])
    return _tok_call(
        body, name=name, S=S, T=T, semantics="arbitrary",
        ins=[_tile(x, _rt(T, D_MODEL)), _whole(g1), _whole(w_blocks)],
        outs=[_out_tile((S, w), BF16, _rt(T, w)) for w in widths] + [_out_acc((D_IN, D_MODEL), BF16)])


def _mixer_tail_call(y_gla, y_sgu, pg, x, w_bg, w_bs, w_out, g2, g3, *, name, T=ROW_TILE):
    S = x.shape[0]

    def body(yg_ref, ys_ref, pg_ref, x_ref, wbg_ref, wbs_ref, wo_ref, g2_ref, g3_ref,
             t1_ref, t2_ref, mg_ref, mix_ref, x1_ref, h_ref):
        for rows in _sub_tiles(T):
            t1 = _dot(yg_ref[rows, :], wbg_ref[...], NN)
            t2 = _dot(ys_ref[rows, :], wbs_ref[...], NN)
            t1_ref[rows, :] = t1.astype(BF16)
            t2_ref[rows, :] = t2.astype(BF16)
            sg = _sig(pg_ref[rows, pl.ds(0, D_MODEL)].astype(F32))
            ss = _sig(pg_ref[rows, pl.ds(D_MODEL, D_MODEL)].astype(F32))
            merged = (sg * t1 + ss * t2).astype(BF16)
            mg_ref[rows, :] = merged
            mix = _dot(merged, wo_ref[...], NN)
            mix_ref[rows, :] = mix
            _, mh = _rms_stats(mix)
            x1 = x_ref[rows, :] + mh * g2_ref[...]
            x1_ref[rows, :] = x1
            _, xh = _rms_stats(x1)
            h_ref[rows, :] = (xh * g3_ref[...]).astype(BF16)

    row = _rt(T, D_MODEL)
    b16 = lambda: _out_tile((S, D_MODEL), BF16, row)
    f32 = lambda: _out_tile((S, D_MODEL), F32, row)
    return _tok_call(
        body, name=name, S=S, T=T,
        ins=[_tile(y_gla, row), _tile(y_sgu, row), _tile(pg, _rt(T, 2 * D_MODEL)), _tile(x, row),
             _whole(w_bg), _whole(w_bs), _whole(w_out), _whole(g2), _whole(g3)],
        outs=[b16(), b16(), b16(), f32(), f32(), b16()])


FF_CHUNKS = ((0, 1024), (1024, 2048), (2048, D_FF))


def _ffn_in_call(h, wfi, *, name, T=ROW_TILE):
    S = h.shape[0]

    def body(h_ref, w_ref, gu_ref, z_ref):
        hv = h_ref[...]
        for c0, c1 in FF_CHUNKS:
            gate = _dot(hv, w_ref[c0:c1, :], NT)
            up = _dot(hv, w_ref[D_FF + c0:D_FF + c1, :], NT)
            gu_ref[:, c0:c1] = gate.astype(BF16)
            gu_ref[:, D_FF + c0:D_FF + c1] = up.astype(BF16)
            z_ref[:, c0:c1] = (gate * _sig(gate) * up).astype(BF16)

    return _tok_call(
        body, name=name, S=S, T=T,
        ins=[_tile(h, _rt(T, D_MODEL)), _whole(wfi)],
        outs=[_out_tile((S, 2 * D_FF), BF16, _rt(T, 2 * D_FF)),
              _out_tile((S, D_FF), BF16, _rt(T, D_FF))])


def _ffn_out_loss_call(z, wfo, x1, target, g4, *, name, T=ROW_TILE):
    S = x1.shape[0]

    def body(z_ref, w_ref, x1_ref, t_ref, g4_ref, loss_ref, dx2_ref, dy_ref, dg4_ref):
        loss = jnp.zeros((1, 1), F32)
        dg4 = jnp.zeros((1, D_MODEL), F32)
        for rows in _sub_tiles(T):
            y = _dot(z_ref[rows, :], w_ref[...], NN)
            r, yh = _rms_stats(y)
            diff = x1_ref[rows, :] + yh * g4_ref[...] - t_ref[rows, :]
            loss = loss + 0.5 * jnp.sum(jnp.mean(diff * diff, axis=-1, keepdims=True), axis=0, keepdims=True)
            dx2 = diff * (1.0 / D_MODEL)
            dx2_ref[rows, :] = dx2
            dy, dg = _rms_bwd(yh, r, g4_ref[...], dx2)
            dy_ref[rows, :] = dy.astype(BF16)
            dg4 = dg4 + dg
        _accum(loss_ref, jnp.broadcast_to(loss, (1, LANES)))
        _accum(dg4_ref, dg4)

    row = _rt(T, D_MODEL)
    return _tok_call(
        body, name=name, S=S, T=T, semantics="arbitrary",
        ins=[_tile(z, _rt(T, D_FF)), _whole(wfo), _tile(x1, row), _tile(target, row), _whole(g4)],
        outs=[_out_acc((1, LANES)), _out_tile((S, D_MODEL), F32, row), _out_tile((S, D_MODEL), BF16, row),
              _out_acc((1, D_MODEL))])


def _ffn_out_bwd_call(dy, wfo, gu, *, name, T=ROW_TILE):
    S = dy.shape[0]

    def body(dy_ref, w_ref, gu_ref, dgu_ref):
        dyv = dy_ref[...]
        for c0, c1 in FF_CHUNKS:
            dz = _dot(dyv, w_ref[c0:c1, :], NT).astype(BF16)
            gt = gu_ref[:, c0:c1]
            up = gu_ref[:, D_FF + c0:D_FF + c1]
            s = _sig(gt)
            dgu_ref[:, c0:c1] = dz * up * s * (1.0 + gt * (1.0 - s))
            dgu_ref[:, D_FF + c0:D_FF + c1] = dz * gt * s

    wide = _rt(T, 2 * D_FF)
    return _tok_call(
        body, name=name, S=S, T=T,
        ins=[_tile(dy, _rt(T, D_MODEL)), _whole(wfo), _tile(gu, wide)],
        outs=[_out_tile((S, 2 * D_FF), BF16, wide)])[0]


def _ffn_in_bwd_call(dgu, wfi, dx2, x1, mix, g3, g2, *, name, T=ROW_TILE):
    S = x1.shape[0]

    def body(dgu_ref, w_ref, dx2_ref, x1_ref, mix_ref, g3_ref, g2_ref, dx1_ref, dmix_ref, dg3_ref, dg2_ref):
        dg3 = jnp.zeros((1, D_MODEL), F32)
        dg2 = jnp.zeros((1, D_MODEL), F32)
        for rows in _sub_tiles(T):
            dh = _dot(dgu_ref[rows, :], w_ref[...], NN)
            r3, xh = _rms_stats(x1_ref[rows, :])
            d3, g = _rms_bwd(xh, r3, g3_ref[...], dh)
            dg3 = dg3 + g
            dx1 = dx2_ref[rows, :] + d3
            dx1_ref[rows, :] = dx1
            r2, mh = _rms_stats(mix_ref[rows, :])
            dmix, g = _rms_bwd(mh, r2, g2_ref[...], dx1)
            dg2 = dg2 + g
            dmix_ref[rows, :] = dmix.astype(BF16)
        _accum(dg3_ref, dg3)
        _accum(dg2_ref, dg2)

    row = _rt(T, D_MODEL)
    return _tok_call(
        body, name=name, S=S, T=T, semantics="arbitrary",
        ins=[_tile(dgu, _rt(T, 2 * D_FF)), _whole(wfi), _tile(dx2, row), _tile(x1, row), _tile(mix, row),
             _whole(g3), _whole(g2)],
        outs=[_out_tile((S, D_MODEL), F32, row), _out_tile((S, D_MODEL), BF16, row),
              _out_acc((1, D_MODEL)), _out_acc((1, D_MODEL))])


def _mixer_bwd_call(dmix, t1, t2, pg, w_out, w_bg, w_bs, *, name, T=ROW_TILE):
    S = dmix.shape[0]

    def body(dmix_ref, t1_ref, t2_ref, pg_ref, wo_ref, wbg_ref, wbs_ref, dt1_ref, dt2_ref, dpg_ref, dyg_ref, dys_ref):
        gg, gs = pl.ds(0, D_MODEL), pl.ds(D_MODEL, D_MODEL)
        for rows in _sub_tiles(T):
            dm = _dot(dmix_ref[rows, :], wo_ref[...], NT)
            sg = _sig(pg_ref[rows, gg].astype(F32))
            ss = _sig(pg_ref[rows, gs].astype(F32))
            dt1 = (dm * sg).astype(BF16)
            dt2 = (dm * ss).astype(BF16)
            dt1_ref[rows, :] = dt1
            dt2_ref[rows, :] = dt2
            dpg_ref[rows, gg] = (dm * t1_ref[rows, :].astype(F32) * sg * (1.0 - sg)).astype(BF16)
            dpg_ref[rows, gs] = (dm * t2_ref[rows, :].astype(F32) * ss * (1.0 - ss)).astype(BF16)
            dyg_ref[rows, :] = _dot(dt1, wbg_ref[...], NT).astype(BF16)
            dys_ref[rows, :] = _dot(dt2, wbs_ref[...], NT).astype(BF16)

    row = _rt(T, D_MODEL)
    wide = _rt(T, 2 * D_MODEL)
    b16 = lambda: _out_tile((S, D_MODEL), BF16, row)
    return _tok_call(
        body, name=name, S=S, T=T,
        ins=[_tile(dmix, row), _tile(t1, row), _tile(t2, row), _tile(pg, wide), _whole(w_out), _whole(w_bg), _whole(w_bs)],
        outs=[b16(), b16(), _out_tile((S, 2 * D_MODEL), BF16, wide), b16(), b16()])


def _in_proj_bwd_call(dpa, dlow, dps, dpg, w_in_t, x, dx1, g1, *, name, T=ROW_TILE):
    S = x.shape[0]

    def body(dpa_ref, dl_ref, dps_ref, dpg_ref, w_ref, x_ref, dx1_ref, g1_ref, gx_ref, dg1_ref):
        da = (_dot(dpa_ref[...], w_ref[A_COLS[0]:A_COLS[1], :], NN)
              + _dot(dl_ref[...], w_ref[LOW_COLS[0]:LOW_COLS[0] + LANES, :], NN)
              + _dot(dps_ref[...], w_ref[S_COLS[0]:S_COLS[1], :], NN)
              + _dot(dpg_ref[...], w_ref[G_COLS[0]:G_COLS[1], :], NN))
        r, xh = _rms_stats(x_ref[...])
        dxa, dg = _rms_bwd(xh, r, g1_ref[...], da)
        gx_ref[...] = dx1_ref[...] + dxa
        _accum(dg1_ref, dg)

    row = _rt(T, D_MODEL)
    return _tok_call(
        body, name=name, S=S, T=T, semantics="arbitrary",
        ins=[_tile(dpa, _rt(T, dpa.shape[1])), _tile(dlow, _rt(T, dlow.shape[1])), _tile(dps, _rt(T, dps.shape[1])),
             _tile(dpg, _rt(T, dpg.shape[1])), _whole(w_in_t), _tile(x, row), _tile(dx1, row), _whole(g1)],
        outs=[_out_tile((S, D_MODEL), F32, row), _out_acc((1, D_MODEL))])


TOKEN_TILE = 512


def _weight_grads_part(arrays, pairs, *, tk=TOKEN_TILE, out_dtype=BF16):
    S = arrays[0].shape[-2]
    tk = min(tk, S)
    n_in = len(arrays)

    def out_shape(i, j):
        a, b = arrays[i], arrays[j]
        if a.ndim == 3:
            return (a.shape[0], a.shape[2], b.shape[1])
        if b.ndim == 3:
            return (b.shape[0], a.shape[1], b.shape[2])
        return (a.shape[1], b.shape[1])

    shapes = [out_shape(i, j) for i, j in pairs]

    in_place = out_dtype == F32

    def body(ins, outs, accs):
        k = pl.program_id(0)
        if in_place:
            accs = outs

        @pl.when(k == 0)
        def _():
            for acc in accs:
                acc[...] = jnp.zeros_like(acc)

        for (i, j), acc in zip(pairs, accs):
            a_ref, b_ref = ins[i], ins[j]
            if len(a_ref.shape) == 3:
                for n in range(a_ref.shape[0]):
                    acc[n] += _dot(a_ref[n], b_ref[...], TN)
            elif len(b_ref.shape) == 3:
                a = a_ref[...]
                for n in range(b_ref.shape[0]):
                    acc[n] += _dot(a, b_ref[n], TN)
            else:
                b = b_ref[...]
                for m0 in range(0, a_ref.shape[1], 1024):
                    m1 = min(a_ref.shape[1], m0 + 1024)
                    acc[m0:m1, :] += _dot(a_ref[:, m0:m1], b, TN)

        if not in_place:
            @pl.when(k == S // tk - 1)
            def _():
                for out, acc in zip(outs, accs):
                    out[...] = acc[...].astype(out.dtype)

    def in_spec(a):
        if a.ndim == 3:
            return pl.BlockSpec((a.shape[0], tk, a.shape[2]), lambda k: (0, k, 0))
        return pl.BlockSpec((tk, a.shape[1]), lambda k: (k, 0))

    return dict(
        body=body, steps=S // tk, arrays=list(arrays),
        in_specs=[in_spec(a) for a in arrays],
        out_specs=[_acc(s) for s in shapes],
        out_shapes=[pltpu.HBM(s, out_dtype) for s in shapes],
        scratch=[] if in_place else [pltpu.VMEM(s, F32) for s in shapes],
        tiles=[(in_spec(a).block_shape, a.dtype) for a in arrays],
        resident=[(s, F32) for s in shapes] + ([] if in_place else [(s, BF16) for s in shapes]))


def _weight_grads_call(arrays, pairs, *, name, tk=TOKEN_TILE):
    part = _weight_grads_part(arrays, pairs, tk=tk)
    n_in, n_out = len(part["arrays"]), len(part["out_shapes"])

    def body(*refs):
        part["body"](refs[:n_in], refs[n_in:n_in + n_out], refs[n_in + n_out:])

    return pl.pallas_call(
        body, name=name, grid=(part["steps"],),
        in_specs=part["in_specs"], out_specs=part["out_specs"], out_shape=part["out_shapes"],
        scratch_shapes=part["scratch"],
        compiler_params=_cp(("arbitrary",), _vmem_limit(part["tiles"], part["resident"])),
    )(*_hbm(*part["arrays"]))


def _ffn_in_grad_call(h, dgu, *, name, tk=TOKEN_TILE):
    S = h.shape[0]
    tk = min(tk, S)
    shape = (D_FF, D_MODEL)

    def body(h_ref, dgu_ref, out_ref, acc):
        k = pl.program_id(1)

        @pl.when(k == 0)
        def _():
            acc[...] = jnp.zeros_like(acc)

        hv = h_ref[...]
        for c0, c1 in FF_CHUNKS:
            acc[c0:c1, :] += _dot(dgu_ref[:, c0:c1], hv, TN)

        @pl.when(k == S // tk - 1)
        def _():
            out_ref[...] = acc[...].astype(out_ref.dtype)

    tiles = [((tk, D_MODEL), BF16), ((tk, D_FF), BF16), (shape, BF16)]
    return pl.pallas_call(
        body, name=name, grid=(2, S // tk),
        in_specs=[pl.BlockSpec((tk, D_MODEL), lambda g, k: (k, 0)),
                  pl.BlockSpec((tk, D_FF), lambda g, k: (k, g))],
        out_specs=pl.BlockSpec(shape, lambda g, k: (g, 0)),
        out_shape=pltpu.HBM((2 * D_FF, D_MODEL), BF16),
        scratch_shapes=[pltpu.VMEM(shape, F32)],
        compiler_params=_cp(("parallel", "arbitrary"), _vmem_limit(tiles, [(shape, F32)])),
    )(*_hbm(h, dgu))


GLA_TILE = 256
Q_OFF, K_OFF, V_OFF, R_OFF = 0, QK, 2 * QK, 2 * QK + GV


def _tri(lower):
    r = lax.broadcasted_iota(jnp.int32, (CHUNK, CHUNK), 0)
    c = lax.broadcasted_iota(jnp.int32, (CHUNK, CHUNK), 1)
    return jnp.where((r >= c) if lower else (c >= r), 1.0, 0.0).astype(BF16)


def _gla_fwd_call(pa, alow, wg, bgate, gnorm, *, name):
    S = pa.shape[0]
    Tg = min(GLA_TILE, S)
    cb = Tg // CHUNK

    def body(pa_ref, al_ref, wg_ref, bg_ref, gn_ref, y_ref, st_ref, state):
        @pl.when(pl.program_id(0) == 0)
        def _():
            state[...] = jnp.zeros_like(state)

        logit = _dot(al_ref[...], wg_ref[...], NN) + bg_ref[...]
        ls = _logsig(logit) * (1.0 / 16.0)
        tri = _tri(True)
        for c in range(cb):
            rows = pl.ds(c * CHUNK, CHUNK)
            cum = _exact_mask_dot(tri, ls[c * CHUNK:(c + 1) * CHUNK])
            tot = cum[CHUNK - 1:CHUNK]
            kd = (pa_ref[rows, pl.ds(K_OFF, QK)].astype(F32) * jnp.exp(tot - cum)).astype(BF16)
            decay = jnp.exp(tot)
            for h in range(HEADS):
                lanes = slice(h * DK, (h + 1) * DK)
                new = state[h] * decay[:, lanes] + _dot(pa_ref[rows, pl.ds(V_OFF + h * DV, DV)], kd[:, lanes], TN)
                state[h] = new
                st_ref[c, h] = new
        for c in range(cb):
            rows = pl.ds(c * CHUNK, CHUNK)
            for h in range(HEADS):
                qs = (pa_ref[rows, pl.ds(Q_OFF + h * DK, DK)].astype(F32) * Q_SCALE).astype(BF16)
                o = _dot(qs, st_ref[c, h].astype(BF16), NT)
                rs = lax.rsqrt(jnp.mean(o * o, axis=-1, keepdims=True) + EPS)
                rr = pa_ref[rows, pl.ds(R_OFF + h * DV, DV)].astype(F32)
                y_ref[rows, pl.ds(h * DV, DV)] = (o * rs * gn_ref[h] * (rr * _sig(rr))).astype(BF16)

    return pl.pallas_call(
        body, name=name, grid=(S // Tg,),
        in_specs=[
            pl.BlockSpec((Tg, 2 * QK + 2 * GV), lambda t: (t, 0)),
            pl.BlockSpec((Tg, LANES), lambda t: (t, 0)),
            pl.BlockSpec((LANES, QK), lambda t: (0, 0)),
            pl.BlockSpec((1, QK), lambda t: (0, 0)),
            pl.BlockSpec((HEADS, 1, DV), lambda t: (0, 0, 0)),
        ],
        out_specs=[
            pl.BlockSpec((Tg, GV), lambda t: (t, 0)),
            pl.BlockSpec((cb, HEADS, DV, DK), lambda t: (t, 0, 0, 0)),
        ],
        out_shape=[pltpu.HBM((S, GV), BF16), pltpu.HBM((S // CHUNK, HEADS, DV, DK), F32)],
        scratch_shapes=[pltpu.VMEM((HEADS, DV, DK), F32)],
        compiler_params=_cp(("arbitrary",), VMEM_BIG),
    )(*_hbm(pa, alow, wg, bgate, gnorm))


def _gla_bwd_call(pa, alow, wg, bgate, gnorm, states, dy, *, name):
    S = pa.shape[0]
    Tg = min(GLA_TILE, S)
    cb = Tg // CHUNK
    nt = S // Tg

    def rev(t):
        return nt - 1 - t

    def body(pa_ref, al_ref, wg_ref, bg_ref, gn_ref, st_ref, prev_ref, dy_ref,
             dpa_ref, dlow_ref, dwg_ref, dgn_ref, dbg_ref, carry, pbuf, dlbuf):
        t = pl.program_id(0)

        @pl.when(t == 0)
        def _():
            carry[...] = jnp.zeros_like(carry)
            dwg_ref[...] = jnp.zeros_like(dwg_ref)
            dgn_ref[...] = jnp.zeros_like(dgn_ref)
            dbg_ref[...] = jnp.zeros_like(dbg_ref)

        logit = _dot(al_ref[...], wg_ref[...], NN) + bg_ref[...]
        ls = _logsig(logit) * (1.0 / 16.0)
        sneg = 1.0 / (1.0 + jnp.exp(logit))
        tri = _tri(True)
        upper = _tri(False)
        first_tile = rev(t) == 0
        heads = range(HEADS)

        w, decay, kd = [], [], []
        for c in range(cb):
            rows = pl.ds(c * CHUNK, CHUNK)
            cum = _exact_mask_dot(tri, ls[c * CHUNK:(c + 1) * CHUNK])
            tot = cum[CHUNK - 1:CHUNK]
            w.append(jnp.exp(tot - cum))
            decay.append(jnp.exp(tot))
            kd.append(pa_ref[rows, pl.ds(K_OFF, QK)].astype(F32) * w[c])

        dgn = [jnp.zeros((1, DV), F32) for _ in heads]
        for c in range(cb):
            rows = pl.ds(c * CHUNK, CHUNK)
            for h in heads:
                gn = gn_ref[h]
                qs = (pa_ref[rows, pl.ds(Q_OFF + h * DK, DK)].astype(F32) * Q_SCALE).astype(BF16)
                st16 = st_ref[c, h].astype(BF16)
                o = _dot(qs, st16, NT)
                rs = lax.rsqrt(jnp.mean(o * o, axis=-1, keepdims=True) + EPS)
                oh = o * rs
                rr = pa_ref[rows, pl.ds(R_OFF + h * DV, DV)].astype(F32)
                sr = _sig(rr)
                dyv = dy_ref[rows, pl.ds(h * DV, DV)].astype(F32)
                dpa_ref[rows, pl.ds(R_OFF + h * DV, DV)] = (
                    dyv * oh * gn * sr * (1.0 + rr * (1.0 - sr))).astype(BF16)
                don = dyv * (rr * sr)
                dgn[h] = dgn[h] + jnp.sum(don * oh, axis=0, keepdims=True)
                doh = don * gn
                do16 = (rs * (doh - oh * jnp.mean(doh * oh, axis=-1, keepdims=True))).astype(BF16)
                dpa_ref[rows, pl.ds(Q_OFF + h * DK, DK)] = (_dot(do16, st16, NN) * Q_SCALE).astype(BF16)
                pbuf[c, h] = _dot(do16, qs, TN)
        for h in heads:
            dgn_ref[h] += dgn[h]

        dkd = [[None] * HEADS for _ in range(cb)]
        ddecay = [[None] * HEADS for _ in range(cb)]
        for c in reversed(range(cb)):
            rows = pl.ds(c * CHUNK, CHUNK)
            for h in heads:
                lanes = slice(h * DK, (h + 1) * DK)
                gt = pbuf[c, h] + carry[h]
                gt16 = gt.astype(BF16)
                dkd[c][h] = _dot(pa_ref[rows, pl.ds(V_OFF + h * DV, DV)], gt16, NN)
                dpa_ref[rows, pl.ds(V_OFF + h * DV, DV)] = _dot(kd[c][:, lanes].astype(BF16), gt16, NT).astype(BF16)
                if c > 0:
                    st_prev = st_ref[c - 1, h]
                else:
                    st_prev = jnp.where(first_tile, 0.0, prev_ref[0, h])
                ddecay[c][h] = jnp.sum(gt * st_prev, axis=0, keepdims=True)
                carry[h] = gt * decay[c][:, lanes]

        dbg = jnp.zeros((1, QK), F32)
        for c in range(cb):
            rows = pl.ds(c * CHUNK, CHUNK)
            dkd_c = jnp.concatenate(dkd[c], axis=1)
            dpa_ref[rows, pl.ds(K_OFF, QK)] = (dkd_c * w[c]).astype(BF16)
            e = dkd_c * kd[c]
            dtot = jnp.sum(e, axis=0, keepdims=True) + jnp.concatenate(ddecay[c], axis=1) * decay[c]
            dls = dtot - _exact_mask_dot(upper, e)
            dlogit = dls * (1.0 / 16.0) * sneg[c * CHUNK:(c + 1) * CHUNK]
            dlbuf[rows, :] = dlogit.astype(BF16)
            dbg = dbg + jnp.sum(dlogit, axis=0, keepdims=True)
        dbg_ref[...] += dbg

        dl16 = dlbuf[...]
        dlow_ref[...] = _dot(dl16, wg_ref[...], NT).astype(BF16)
        dwg_ref[...] += _dot(al_ref[...], dl16, TN)

    wide = 2 * QK + 2 * GV
    tiles = [((Tg, wide), BF16), ((cb + 1, HEADS, DV, DK), F32), ((Tg, GV), BF16), ((Tg, wide), BF16),
             ((Tg, QK), BF16)]
    resident = [((cb + 1, HEADS, DV, DK), F32)]
    return pl.pallas_call(
        body, name=name, grid=(nt,),
        in_specs=[
            pl.BlockSpec((Tg, wide), lambda t: (rev(t), 0)),
            pl.BlockSpec((Tg, LANES), lambda t: (rev(t), 0)),
            pl.BlockSpec((LANES, QK), lambda t: (0, 0)),
            pl.BlockSpec((1, QK), lambda t: (0, 0)),
            pl.BlockSpec((HEADS, 1, DV), lambda t: (0, 0, 0)),
            pl.BlockSpec((cb, HEADS, DV, DK), lambda t: (rev(t), 0, 0, 0)),
            pl.BlockSpec((1, HEADS, DV, DK), lambda t: (jnp.maximum(rev(t) * cb - 1, 0), 0, 0, 0)),
            pl.BlockSpec((Tg, GV), lambda t: (rev(t), 0)),
        ],
        out_specs=[
            pl.BlockSpec((Tg, wide), lambda t: (rev(t), 0)),
            pl.BlockSpec((Tg, LANES), lambda t: (rev(t), 0)),
            pl.BlockSpec((LANES, QK), lambda t: (0, 0)),
            pl.BlockSpec((HEADS, 1, DV), lambda t: (0, 0, 0)),
            pl.BlockSpec((1, QK), lambda t: (0, 0)),
        ],
        out_shape=[pltpu.HBM((S, wide), BF16), pltpu.HBM((S, LANES), BF16), jax.ShapeDtypeStruct((LANES, QK), F32),
                   jax.ShapeDtypeStruct((HEADS, 1, DV), F32), jax.ShapeDtypeStruct((1, QK), F32)],
        scratch_shapes=[pltpu.VMEM((HEADS, DV, DK), F32), pltpu.VMEM((cb, HEADS, DV, DK), F32),
                        pltpu.VMEM((Tg, QK), BF16)],
        compiler_params=_cp(("arbitrary",), _vmem_limit(tiles, resident)),
    )(*_hbm(pa, alow, wg, bgate, gnorm, states, states, dy))


SGU_TILE = 256


def _sgu_mask():
    r = lax.broadcasted_iota(jnp.int32, (SBLOCK, SBLOCK), 0)
    c = lax.broadcasted_iota(jnp.int32, (SBLOCK, SBLOCK), 1)
    return (c < CHUNK) | (r >= CHUNK)


def _ln_stats(vf):
    mu = jnp.mean(vf, axis=-1, keepdims=True)
    xc = vf - mu
    rs = lax.rsqrt(jnp.mean(xc * xc, axis=-1, keepdims=True) + EPS)
    return rs, xc * rs


def _sgu_fwd_call(ps, ln_g, ln_b, w_sp, b_sp, *, name):
    S = ps.shape[0]
    Ts = min(SGU_TILE, S)

    def body(ps_ref, lg_ref, lb_ref, w_ref, b_ref, y_ref):
        mask = _sgu_mask()
        for g in range(GROUPS):
            wm = jnp.where(mask, w_ref[g], 0.0).astype(BF16)
            for p in range(Ts // SBLOCK):
                rows = pl.ds(p * SBLOCK, SBLOCK)
                u = _gelu(ps_ref[rows, pl.ds(g * DG, DG)].astype(F32))
                _, xh = _ln_stats(_gelu(ps_ref[rows, pl.ds(D_MODEL + g * DG, DG)].astype(F32)))
                vn = xh * lg_ref[g] + lb_ref[g]
                mixed = _dot(wm, vn.astype(BF16), NN) + b_ref[g]
                y_ref[rows, pl.ds(g * DG, DG)] = (u * mixed).astype(BF16)

    full3 = lambda a, b, c: pl.BlockSpec((a, b, c), lambda t: (0, 0, 0))
    return pl.pallas_call(
        body, name=name, grid=(S // Ts,),
        in_specs=[pl.BlockSpec((Ts, 2 * D_MODEL), lambda t: (t, 0)),
                  full3(GROUPS, 1, DG), full3(GROUPS, 1, DG), full3(GROUPS, SBLOCK, SBLOCK), full3(GROUPS, SBLOCK, 1)],
        out_specs=pl.BlockSpec((Ts, D_MODEL), lambda t: (t, 0)),
        out_shape=pltpu.HBM((S, D_MODEL), BF16),
        compiler_params=_cp(("parallel",)),
    )(*_hbm(ps, ln_g, ln_b, w_sp, b_sp))


def _sgu_bwd_call(ps, ln_g, ln_b, w_sp, b_sp, dy, *, name):
    S = ps.shape[0]
    Ts = min(SGU_TILE, S)

    def body(ps_ref, lg_ref, lb_ref, w_ref, b_ref, dy_ref, ds_ref, dlg_ref, dlb_ref, dw_ref, db_ref):
        @pl.when(pl.program_id(0) == 0)
        def _():
            dlg_ref[...] = jnp.zeros_like(dlg_ref)
            dlb_ref[...] = jnp.zeros_like(dlb_ref)
            dw_ref[...] = jnp.zeros_like(dw_ref)
            db_ref[...] = jnp.zeros_like(db_ref)

        mask = _sgu_mask()
        for g in range(GROUPS):
            wm = jnp.where(mask, w_ref[g], 0.0).astype(BF16)
            lg = lg_ref[g]
            for p in range(Ts // SBLOCK):
                rows = pl.ds(p * SBLOCK, SBLOCK)
                su = ps_ref[rows, pl.ds(g * DG, DG)].astype(F32)
                sv = ps_ref[rows, pl.ds(D_MODEL + g * DG, DG)].astype(F32)
                u, du = _gelu_and_grad(su)
                gv, dgv = _gelu_and_grad(sv)
                rs, xh = _ln_stats(gv)
                vn16 = (xh * lg + lb_ref[g]).astype(BF16)
                mixed = _dot(wm, vn16, NN) + b_ref[g]
                dyv = dy_ref[rows, pl.ds(g * DG, DG)].astype(F32)
                ds_ref[rows, pl.ds(g * DG, DG)] = (dyv * mixed * du).astype(BF16)
                dmix = dyv * u
                dmix16 = dmix.astype(BF16)
                db_ref[g] += jnp.sum(dmix, axis=-1, keepdims=True)
                dw_ref[g] += jnp.where(mask, _dot(dmix16, vn16, NT), 0.0)
                dvn = _dot(wm, dmix16, TN)
                dlg_ref[g] += jnp.sum(dvn * xh, axis=0, keepdims=True)
                dlb_ref[g] += jnp.sum(dvn, axis=0, keepdims=True)
                dxh = dvn * lg
                dvf = rs * (dxh - jnp.mean(dxh, axis=-1, keepdims=True)
                            - xh * jnp.mean(dxh * xh, axis=-1, keepdims=True))
                ds_ref[rows, pl.ds(D_MODEL + g * DG, DG)] = (dvf * dgv).astype(BF16)

    full3 = lambda a, b, c: pl.BlockSpec((a, b, c), lambda t: (0, 0, 0))
    tiles = [((Ts, 2 * D_MODEL), BF16), ((Ts, D_MODEL), BF16), ((Ts, 2 * D_MODEL), BF16)]
    return pl.pallas_call(
        body, name=name, grid=(S // Ts,),
        in_specs=[pl.BlockSpec((Ts, 2 * D_MODEL), lambda t: (t, 0)),
                  full3(GROUPS, 1, DG), full3(GROUPS, 1, DG), full3(GROUPS, SBLOCK, SBLOCK), full3(GROUPS, SBLOCK, 1),
                  pl.BlockSpec((Ts, D_MODEL), lambda t: (t, 0))],
        out_specs=[pl.BlockSpec((Ts, 2 * D_MODEL), lambda t: (t, 0)),
                   full3(GROUPS, 1, DG), full3(GROUPS, 1, DG), full3(GROUPS, SBLOCK, SBLOCK), full3(GROUPS, SBLOCK, 1)],
        out_shape=[pltpu.HBM((S, 2 * D_MODEL), BF16),
                   jax.ShapeDtypeStruct((GROUPS, 1, DG), F32), jax.ShapeDtypeStruct((GROUPS, 1, DG), F32),
                   jax.ShapeDtypeStruct((GROUPS, SBLOCK, SBLOCK), F32),
                   jax.ShapeDtypeStruct((GROUPS, SBLOCK, 1), F32)],
        compiler_params=_cp(("arbitrary",), _vmem_limit(tiles, [])),
    )(*_hbm(ps, ln_g, ln_b, w_sp, b_sp, dy))


def _position():
    return lax.axis_index("x"), lax.axis_index("y"), lax.axis_index("c")


def _gather_copies(srcs, dsts, send_sems, recv_sems, local_sems):
    x, y, c = _position()
    me, sibling = (x, y, c), (x, y, 1 - c)
    chips = [(1 - x, y), (x, 1 - y), (1 - x, 1 - y)]
    n = len(srcs)

    def slab(a, block):
        px, py, pc = block
        return dsts[a].at[4 * px + 2 * py + pc]

    def copy(a, k, block, to, src=None):
        return pltpu.make_async_remote_copy(
            src_ref=slab(a, block) if src is None else src, dst_ref=slab(a, block),
            send_sem=send_sems.at[7 * a + k], recv_sem=recv_sems.at[7 * a + k], device_id=to, device_id_type=MESH)

    mine = [pltpu.make_async_copy(srcs[a], slab(a, me), local_sems.at[a]) for a in range(n)]
    for cp in mine:
        cp.start()
    first = []
    for a in range(n):
        first.append(copy(a, 0, me, sibling, src=srcs[a]))
        first += [copy(a, 1 + j, me, (*chip, c), src=srcs[a]) for j, chip in enumerate(chips)]
    for cp in first:
        cp.start()
    passed = []
    for j, chip in enumerate(chips):
        for a in range(n):
            copy(a, 1 + j, (*chip, c), me).wait_recv()
            cp = copy(a, 4 + j, (*chip, c), sibling)
            cp.start()
            passed.append(cp)
    for a in range(n):
        copy(a, 0, sibling, me).wait_recv()
        for j, chip in enumerate(chips):
            copy(a, 4 + j, (*chip, 1 - c), me).wait_recv()
    for cp in first + passed:
        cp.wait_send()
    for cp in mine:
        cp.wait()


def _gather_copies_relayed(srcs, dsts, send_sems, recv_sems, local_sems, waves=1):
    x, y, c = _position()
    me, sibling = (x, y, c), (x, y, 1 - c)
    x_chip, y_chip, d_chip = (1 - x, y), (x, 1 - y), (1 - x, 1 - y)
    south = c == 0
    relay_from = (jnp.where(south, x, 1 - x), jnp.where(south, 1 - y, y), c)
    relay_to = (jnp.where(south, 1 - x, x), jnp.where(south, y, 1 - y), c)
    n = len(srcs)

    def cut(a, w):
        rows = srcs[a].shape[0]
        step = (rows // waves) // 16 * 16
        if waves == 1 or step == 0:
            return pl.ds(0, rows) if w == 0 else None
        return pl.ds(w * step, step if w < waves - 1 else rows - w * step)

    def slab(a, block, w):
        px, py, pc = block
        return dsts[a].at[4 * px + 2 * py + pc, cut(a, w)]

    def copy(a, k, w, block, to, own=False):
        sem = (7 * a + k) * waves + w
        return pltpu.make_async_remote_copy(
            src_ref=srcs[a].at[cut(a, w)] if own else slab(a, block, w), dst_ref=slab(a, block, w),
            send_sem=send_sems.at[sem], recv_sem=recv_sems.at[sem], device_id=to, device_id_type=MESH)

    pieces = [(a, w) for w in range(waves) for a in range(n) if cut(a, w) is not None]
    mine = [pltpu.make_async_copy(srcs[a], dsts[a].at[4 * x + 2 * y + c], local_sems.at[a]) for a in range(n)]
    for cp in mine:
        cp.start()
    sent = []
    for a, w in pieces:
        sent += [copy(a, 0, w, me, sibling, own=True), copy(a, 1, w, me, (*x_chip, c), own=True),
                 copy(a, 2, w, me, (*y_chip, c), own=True)]
    for cp in sent:
        cp.start()
    for a, w in pieces:
        copy(a, 1, w, (*x_chip, c), me).wait_recv()
        copy(a, 2, w, (*y_chip, c), me).wait_recv()
        later = [copy(a, 3, w, relay_from, relay_to), copy(a, 4, w, (*x_chip, c), sibling),
                 copy(a, 5, w, (*y_chip, c), sibling)]
        for cp in later:
            cp.start()
        sent += later
    for a, w in pieces:
        copy(a, 3, w, (*d_chip, c), me).wait_recv()
        cp = copy(a, 6, w, (*d_chip, c), sibling)
        cp.start()
        sent.append(cp)
    for a, w in pieces:
        copy(a, 0, w, sibling, me).wait_recv()
        for k, chip in ((4, x_chip), (5, y_chip), (6, d_chip)):
            copy(a, k, w, (*chip, 1 - c), me).wait_recv()
    for cp in sent:
        cp.wait_send()
    for cp in mine:
        cp.wait()


FLIPS = [(fx, fy, fc) for fx in (0, 1) for fy in (0, 1) for fc in (0, 1)][1:]


def _scatter_copies(srcs, dsts, send_sems, recv_sems, local_sems, waves=1):
    n = len(srcs)
    x, y, c = _position()
    me = 4 * x + 2 * y + c
    mine = [pltpu.make_async_copy(srcs[a].at[me], dsts[a].at[me], local_sems.at[a]) for a in range(n)]
    for cp in mine:
        cp.start()
    peers = []
    for fx, fy, fc in FLIPS:
        tx = 1 - x if fx else x
        ty = 1 - y if fy else y
        tc = 1 - c if fc else c
        peers.append(((tx, ty, tc), 4 * tx + 2 * ty + tc))
    copies = []
    for w in range(waves):
        wave = []
        for k, (peer_id, peer) in enumerate(peers):
            for a in range(n):
                rows = srcs[a].shape[1]
                step = rows if waves == 1 else (rows // waves) // 16 * 16
                r0 = w * step
                cut = pl.ds(r0, step if w < waves - 1 else rows - r0)
                sem = (7 * a + k) * waves + w
                cp = pltpu.make_async_remote_copy(
                    src_ref=srcs[a].at[peer, cut], dst_ref=dsts[a].at[me, cut],
                    send_sem=send_sems.at[sem], recv_sem=recv_sems.at[sem],
                    device_id=peer_id, device_id_type=MESH)
                cp.start()
                wave.append(cp)
        for cp in wave:
            cp.wait_send()
        copies += wave
    for cp in copies:
        cp.wait_recv()
    for cp in mine:
        cp.wait()


def _pair_copies(srcs, dsts, send_sems, recv_sems, local_sems):
    x, y, c = _position()
    copies = [pltpu.make_async_remote_copy(
        src_ref=srcs[a], dst_ref=dsts[a], send_sem=send_sems.at[a], recv_sem=recv_sems.at[a],
        device_id=(x, y, 1 - c), device_id_type=MESH) for a in range(len(srcs))]
    for cp in copies:
        cp.start()
    for cp in copies:
        cp.wait()


def _chip_scatter_copies(srcs, dsts, send_sems, recv_sems, local_sems, waves=1):
    n = len(srcs)
    x, y, c = _position()
    here = 2 * x + y
    mine = [pltpu.make_async_copy(srcs[a].at[here], dsts[a].at[here], local_sems.at[a]) for a in range(n)]
    for cp in mine:
        cp.start()
    peers = []
    for fx, fy in ((1, 0), (0, 1), (1, 1)):
        tx = 1 - x if fx else x
        ty = 1 - y if fy else y
        peers.append(((tx, ty, c), 2 * tx + ty))
    copies = []
    for w in range(waves):
        wave = []
        for k, (peer_id, peer) in enumerate(peers):
            for a in range(n):
                rows = srcs[a].shape[1]
                step = rows if waves == 1 else (rows // waves) // 16 * 16
                r0 = w * step
                cut = pl.ds(r0, step if w < waves - 1 else rows - r0)
                sem = (3 * a + k) * waves + w
                cp = pltpu.make_async_remote_copy(
                    src_ref=srcs[a].at[peer, cut], dst_ref=dsts[a].at[here, cut],
                    send_sem=send_sems.at[sem], recv_sem=recv_sems.at[sem],
                    device_id=peer_id, device_id_type=MESH)
                cp.start()
                wave.append(cp)
        for cp in wave:
            cp.wait_send()
        copies += wave
    for cp in copies:
        cp.wait_recv()
    for cp in mine:
        cp.wait()


def _comm_sems(n, waves=1):
    return [pltpu.SemaphoreType.DMA((7 * n * waves,)), pltpu.SemaphoreType.DMA((7 * n * waves,)),
            pltpu.SemaphoreType.DMA((n,))]


def _handshake(peers):
    barrier = pltpu.get_barrier_semaphore()
    for peer in peers:
        pl.semaphore_signal(barrier, inc=1, device_id=peer, device_id_type=MESH)
    pl.semaphore_wait(barrier, len(peers))


def _sequencer_call(arrays, out_types, copies_fn, peers_fn, *, name, collective_id, waves=1):
    n = len(arrays)
    srcs = [jax.new_ref(a, memory_space=pltpu.MemorySpace.HBM) for a in arrays]
    dsts = [jax.empty_ref(t, memory_space=pltpu.MemorySpace.HBM) for t in out_types]
    extra = {} if waves == 1 else {"waves": waves}

    @pl.kernel(mesh=plsc.ScalarSubcoreMesh(axis_name="sequencer", num_cores=1), name=name,
               scratch_types=_comm_sems(n, waves), compiler_params=pltpu.CompilerParams(collective_id=collective_id))
    def launch(send_sems, recv_sems, local_sems):
        _handshake(peers_fn())
        copies_fn(srcs, dsts, send_sems, recv_sems, local_sems, **extra)

    launch()
    return [d[...] for d in dsts]


def _all_other_devices():
    x, y, c = _position()
    return [(1 - x if fx else x, 1 - y if fy else y, 1 - c if fc else c) for fx, fy, fc in FLIPS]


def _gather_relay_peers():
    x, y, c = _position()
    return [(x, y, 1 - c), (1 - x, y, c), (x, 1 - y, c)]


def _gather_peers():
    x, y, c = _position()
    return [(x, y, 1 - c), (1 - x, y, c), (x, 1 - y, c), (1 - x, 1 - y, c)]


def _small_gather_async(shards, *, name, collective_id):
    return _sequencer_call(shards, [jax.ShapeDtypeStruct((N_DEV, *s.shape), s.dtype) for s in shards],
                           _gather_copies, _gather_peers, name=name, collective_id=collective_id)


def _sibling():
    x, y, c = _position()
    return [(x, y, 1 - c)]


def _same_core_of_other_chips():
    x, y, c = _position()
    return [(1 - x, y, c), (x, 1 - y, c), (1 - x, 1 - y, c)]


def _pair_exchange_async(parts, *, name, collective_id):
    return _sequencer_call(parts, [jax.ShapeDtypeStruct(p.shape, p.dtype) for p in parts],
                           _pair_copies, _sibling, name=name, collective_id=collective_id)


def _chip_scatter_async(parts, *, name, collective_id, waves=1):
    return _sequencer_call(parts, [jax.ShapeDtypeStruct(p.shape, p.dtype) for p in parts],
                           _chip_scatter_copies, _same_core_of_other_chips, name=name, collective_id=collective_id,
                           waves=waves)


def _pair_sum_call(mine, theirs, *, name, tw=D_MODEL):
    n, r, w = mine.shape

    def body(a_ref, b_ref, o_ref):
        o_ref[...] = (a_ref[...].astype(F32) + b_ref[...].astype(F32)).astype(o_ref.dtype)

    spec = pl.BlockSpec((None, r, tw), lambda i, j: (i, 0, j))
    return pl.pallas_call(
        body, name=name, grid=(n, w // tw), in_specs=[spec, spec], out_specs=spec,
        out_shape=pltpu.HBM(mine.shape, mine.dtype),
        compiler_params=_cp(("parallel", "parallel")),
    )(*_hbm(mine, theirs))


def _scatter_blocks_async(parts, *, name, collective_id, waves=1):
    return _sequencer_call(parts, [jax.ShapeDtypeStruct(p.shape, p.dtype) for p in parts],
                           _scatter_copies, _all_other_devices, name=name, collective_id=collective_id, waves=waves)


def _all_gather_async(shards, *, name, collective_id, waves=1):
    return _sequencer_call(shards, [jax.ShapeDtypeStruct((N_DEV, *s.shape), s.dtype) for s in shards],
                           _gather_copies_relayed, _gather_relay_peers, name=name, collective_id=collective_id,
                           waves=waves)


def _adamw_math(w, g, m, v):
    m = ADAM_B1 * m + (1.0 - ADAM_B1) * g
    v = ADAM_B2 * v + (1.0 - ADAM_B2) * (g * g)
    m_hat = m / (1.0 - ADAM_B1 ** ADAM_STEP)
    v_hat = v / (1.0 - ADAM_B2 ** ADAM_STEP)
    delta = -ADAM_LR * (m_hat / (jnp.sqrt(v_hat) + ADAM_EPS) + ADAM_WD * w)
    return delta, m, v


def _adamw_reduce_call(recv, w, m, v, *, name, T=128):
    R, W = w.shape
    n_parts = recv.shape[0]
    if R % T == 0:
        tr, tw = T, W
    elif (R // 2) % 16 == 0:
        tr, tw = R // 2, W
    else:
        tr, tw = R, 2 * LANES

    def body(p_ref, w_ref, m_ref, v_ref, g_out, d_out, m_out, v_out):
        g = p_ref[0].astype(F32)
        for d in range(1, n_parts):
            g = g + p_ref[d].astype(F32)
        g_out[...] = g
        d_out[...], m_out[...], v_out[...] = _adamw_math(w_ref[...], g, m_ref[...], v_ref[...])

    row = pl.BlockSpec((tr, tw), lambda i, j: (i, j))
    out = pltpu.HBM((R, W), F32)
    return pl.pallas_call(
        body, name=name, grid=(R // tr, W // tw),
        in_specs=[pl.BlockSpec((n_parts, tr, tw), lambda i, j: (0, i, j)), row, row, row],
        out_specs=[row] * 4, out_shape=[out] * 4,
        compiler_params=_cp(("parallel", "parallel"), VMEM_BIG),
    )(*_hbm(recv, w, m, v))


SMALL_EARLY = (("b_gate", 8), ("w_spatial", 512), ("b_spatial", 8), ("norm_post_mix", 8), ("norm_pre_ffn", 8),
               ("norm_post_ffn", 8), ("w_gate_up", 128), ("gla_norm", 64), ("sgu_ln_g", 64), ("sgu_ln_b", 64))
SMALL_EARLY_AT = {}
for _name, _rows in SMALL_EARLY:
    SMALL_EARLY_AT[_name] = sum(r for _, r in SMALL_EARLY[:len(SMALL_EARLY_AT)])
SMALL_EARLY_ROWS = sum(r for _, r in SMALL_EARLY)
SMALL_LATE_ROWS = 16


def _small_early_rows(grads):
    def rows(a, n_rows):
        a = a.reshape(-1, LANES)
        return jnp.pad(a, ((0, n_rows - a.shape[0]), (0, 0)))

    def device_major(a, n_rows):
        r, c = a.shape[0], a.shape[1] // N_DEV
        a = a.reshape(r, N_DEV, c).transpose(1, 0, 2)
        a = jnp.pad(a, ((0, 0), (0, n_rows // N_DEV - r), (0, LANES - c)))
        return a.reshape(n_rows, LANES)

    pieces = []
    for name, n_rows in SMALL_EARLY:
        g = grads[name]
        if name in SMALL_SHARDED:
            pieces.append(device_major(g.reshape(g.shape[0], -1) if g.ndim == 2 else g.reshape(g.shape[0], g.shape[-1]), n_rows))
        else:
            pieces.append(rows(g, n_rows))
    return jnp.concatenate(pieces, axis=0)


def _small_update_call(got_early, got_late, w, m, v, *, name):
    names = list(SMALL)
    n_p = len(names)
    E = SMALL_EARLY_ROWS

    def body(early_ref, late_ref, *rest):
        w_refs, m_refs, v_refs = [dict(zip(names, rest[i * n_p:(i + 1) * n_p])) for i in range(3)]
        outs, tot = rest[3 * n_p:-1], rest[-1]
        loss_out = outs[0]
        g_out, d_out, m_out, v_out = [dict(zip(names, outs[1 + i * n_p:1 + (i + 1) * n_p])) for i in range(4)]
        acc, acc_late = early_ref[0], late_ref[0]
        for d in range(1, N_DEV):
            acc, acc_late = acc + early_ref[d], acc_late + late_ref[d]
        tot[0:E, :] = acc
        tot[E:E + SMALL_LATE_ROWS, :] = acc_late
        loss_out[...] = tot[E + 8:E + 9, :]

        x, y, c = _position()
        me = 4 * x + 2 * y + c

        def update(name, g, ix):
            g_out[name][ix] = g
            d_out[name][ix], m_out[name][ix], v_out[name][ix] = _adamw_math(
                w_refs[name][ix], g, m_refs[name][ix], v_refs[name][ix])

        for name in names:
            shape = w[name].shape
            if name in SMALL_SHARDED:
                per_dev = dict(SMALL_EARLY)[name] // N_DEV
                at = pl.multiple_of(SMALL_EARLY_AT[name] + me * per_dev, 8)
                g = tot[pl.ds(at, per_dev), :]
                update(name, g[:shape[1], :shape[2]], (0,))
            elif name == "w_spatial":
                for grp in range(GROUPS):
                    at = SMALL_EARLY_AT[name] + grp * SBLOCK
                    update(name, tot[at:at + SBLOCK, :], (0, grp))
            elif name == "b_spatial":
                at = SMALL_EARLY_AT[name]
                update(name, tot[at:at + GROUPS, :], (0,))
            else:
                at = E if name == "norm_pre_mix" else SMALL_EARLY_AT[name]
                for k in range(shape[1] // LANES):
                    update(name, tot[at + k:at + k + 1, :], (slice(None), pl.ds(k * LANES, LANES)))

    state = [s[n] for s in (w, m, v) for n in names]
    out_shapes = [jax.ShapeDtypeStruct((1, LANES), F32)] + [jax.ShapeDtypeStruct(w[n].shape, F32) for n in names] * 4
    outs = pl.pallas_call(
        body, name=name,
        in_specs=[VMEM_SPEC] * (2 + len(state)), out_specs=[VMEM_SPEC] * len(out_shapes), out_shape=out_shapes,
        scratch_shapes=[pltpu.VMEM((E + SMALL_LATE_ROWS, LANES), F32)],
    )(got_early, got_late, *state)
    per_name = {n: tuple(outs[1 + i * n_p + j] for i in range(4)) for j, n in enumerate(names)}
    return outs[0], per_name


def _tile_rows(n_elems):
    return -(-n_elems // (8 * LANES)) * 8


def _pack_rows(parts, rows):
    pieces = []
    for p in parts:
        q = p.reshape(-1, LANES)
        pieces.append(jnp.pad(q, ((0, _tile_rows(p.size) - q.shape[0]), (0, 0))))
    buf = jnp.concatenate(pieces, axis=0)
    return jnp.pad(buf, ((0, rows - buf.shape[0]), (0, 0)))


def _in_w_blocks_call(a, live, finished, *, name, tk=TOKEN_TILE):
    S = a.shape[0]
    tk = min(tk, S)
    steps = S // tk
    n_live = len(live)
    n_chips = N_DEV // 2
    out_shape = (n_chips, IN_BLK, D_MODEL)

    def body(a_ref, *rest):
        live_refs, done_refs = rest[:n_live], rest[n_live:n_live + len(finished)]
        keep_ref, send_ref = rest[n_live + len(finished):n_live + len(finished) + 2]
        accs = rest[n_live + len(finished) + 2:]
        k = pl.program_id(0)

        @pl.when(k == 0)
        def _():
            for acc in accs:
                acc[...] = jnp.zeros_like(acc)

        av = a_ref[...]
        for src, acc in zip(live_refs, accs):
            for m0 in range(0, src.shape[1], 1024):
                m1 = min(src.shape[1], m0 + 1024)
                acc[m0:m1, :] += _dot(src[:, m0:m1], av, TN)

        @pl.when(k == steps - 1)
        def _():
            groups = [(acc, cols) for acc, (_, cols) in zip(accs, live)]
            groups += [(ref, cols) for ref, (_, cols) in zip(done_refs, finished)]
            core = lax.axis_index("c")

            def cut(d, out_ref):
                lo, hi = IN_BLK * d, IN_BLK * (d + 1)
                for ref, (c0, c1) in groups:
                    s0, e0 = max(lo, c0), min(hi, c1)
                    if s0 < e0:
                        out_ref[d // 2, s0 - lo:e0 - lo, :] = ref[s0 - c0:e0 - c0, :].astype(BF16)

            for d in range(N_DEV):
                pl.when(core == d % 2)(lambda d=d: cut(d, keep_ref))
                pl.when(core != d % 2)(lambda d=d: cut(d, send_ref))

    acc_shapes = [(arr.shape[1], D_MODEL) for arr, _ in live]
    tiles = [((tk, D_MODEL), BF16)] + [((tk, arr.shape[1]), BF16) for arr, _ in live]
    resident = ([(arr.shape, arr.dtype) for arr, _ in finished] + [(out_shape, BF16)] * 2
                + [(s, F32) for s in acc_shapes])
    return pl.pallas_call(
        body, name=name, grid=(steps,),
        in_specs=[pl.BlockSpec((tk, D_MODEL), lambda k: (k, 0))]
        + [pl.BlockSpec((tk, arr.shape[1]), lambda k: (k, 0)) for arr, _ in live]
        + [_res(arr.shape) for arr, _ in finished],
        out_specs=[_acc(out_shape)] * 2,
        out_shape=[pltpu.HBM(out_shape, BF16)] * 2,
        scratch_shapes=[pltpu.VMEM(s, F32) for s in acc_shapes],
        compiler_params=_cp(("arbitrary",), _vmem_limit(tiles, resident)),
    )(*_hbm(a, *[arr for arr, _ in live], *[arr for arr, _ in finished]))


def _local_step(x, target, W, scatter, finish, pair, chip_scatter):
    g1, g2, g3, g4 = [W[n].reshape(1, D_MODEL) for n in ("norm_pre_mix", "norm_post_mix", "norm_pre_ffn", "norm_post_ffn")]
    wfi, wfo = W["w_ffn_in"].reshape(2 * D_FF, D_MODEL), W["w_ffn_out"]
    wg = jnp.pad(W["w_gate_up"], ((0, LANES - RANK), (0, 0))).astype(BF16)
    bgate = W["b_gate"].reshape(1, QK)
    gnorm = W["gla_norm"].reshape(HEADS, 1, DV)
    ln_g = W["sgu_ln_g"].reshape(GROUPS, 1, DG)
    ln_b = W["sgu_ln_b"].reshape(GROUPS, 1, DG)
    w_sp = W["w_spatial"]
    b_sp = W["b_spatial"].reshape(GROUPS, SBLOCK, 1)

    a, pa, alow, ps, pg, w_in_t = _in_proj_call(x, g1, W["w_in_blocks"], name="in_proj")
    y_gla, states = _gla_fwd_call(pa, alow, wg, bgate, gnorm, name="gla_fwd")
    y_sgu = _sgu_fwd_call(ps, ln_g, ln_b, w_sp, b_sp, name="sgu_fwd")
    t1, t2, merged, mix, x1, h = _mixer_tail_call(y_gla, y_sgu, pg, x, W["w_branch_gla"], W["w_branch_sgu"],
                                                  W["w_out"], g2, g3, name="mixer_tail")
    gu, z = _ffn_in_call(h, wfi, name="ffn_in")
    loss, dx2, dy, dg4 = _ffn_out_loss_call(z, wfo, x1, target, g4, name="ffn_out_loss")

    grads = {"norm_post_ffn": dg4}
    done = {}
    dgu = _ffn_out_bwd_call(dy, wfo, gu, name="ffn_out_bwd")
    dw_ffn_out = _weight_grads_call([z, dy], [(0, 1)], name="d_ffn_out_w")[0].reshape(N_DEV, D_FF // N_DEV, D_MODEL)
    dw_ffn_in = _ffn_in_grad_call(h, dgu, name="d_ffn_in_w").reshape(N_DEV, FF_BLK, D_MODEL)
    dgu, dw_ffn_out, dw_ffn_in = lax.optimization_barrier((dgu, dw_ffn_out, dw_ffn_in))
    ffn_received = scatter(("w_ffn_out", "w_ffn_in"), [dw_ffn_out, dw_ffn_in])
    dx1, dmix, grads["norm_pre_ffn"], grads["norm_post_mix"] = _ffn_in_bwd_call(dgu, wfi, dx2, x1, mix, g3, g2, name="ffn_in_bwd")
    dt1, dt2, dpg, dy_gla, dy_sgu = _mixer_bwd_call(dmix, t1, t2, pg, W["w_out"], W["w_branch_gla"], W["w_branch_sgu"],
                                                    name="mixer_bwd")
    rows = D_MODEL // N_DEV
    mixer_grads = _weight_grads_call([merged, dmix, y_gla, dt1, y_sgu, dt2], [(0, 1), (2, 3), (4, 5)], name="d_mixer_w")
    dy_gla, dy_sgu, mixer_grads = lax.optimization_barrier((dy_gla, dy_sgu, mixer_grads))
    mixer_received = scatter(("w_out", "w_branch_gla", "w_branch_sgu"),
                             [g.reshape(N_DEV, rows, D_MODEL) for g in mixer_grads])
    dps, dlg, dlb, dwsp, dbsp = _sgu_bwd_call(ps, ln_g, ln_b, w_sp, b_sp, dy_sgu, name="sgu_bwd")
    dw_s, dw_g = _weight_grads_call([a, dps, dpg], [(1, 0), (2, 0)], name="d_in_w_sgu_gates")
    dy_gla, dw_s, dw_g = lax.optimization_barrier((dy_gla, dw_s, dw_g))
    dpa, dlow, dwg, dgn, dbg = _gla_bwd_call(pa, alow, wg, bgate, gnorm, states, dy_gla, name="gla_bwd")
    ffn_received, mixer_received, dpa, dlow = lax.optimization_barrier((ffn_received, mixer_received, dpa, dlow))
    keep, send = _in_w_blocks_call(a, [(dpa, A_COLS), (dlow, LOW_COLS)], [(dw_s, S_COLS), (dw_g, G_COLS)],
                                   name="d_in_w_blocks")
    ffn_received, mixer_received, send = lax.optimization_barrier((ffn_received, mixer_received, send))
    from_sibling = pair(send)
    done.update({**finish(ffn_received), **finish(mixer_received)})
    done, keep = lax.optimization_barrier((done, keep))
    chip_sum = _pair_sum_call(keep, from_sibling, name="pair_sum_w_in")
    chip_sum, dpa = lax.optimization_barrier((chip_sum, dpa))
    in_received = chip_scatter(chip_sum)
    grad_x, grads["norm_pre_mix"] = _in_proj_bwd_call(dpa, dlow, dps, dpg, w_in_t, x, dx1, g1, name="in_proj_bwd")

    grads["w_gate_up"] = dwg[:RANK]
    grads["b_gate"] = dbg
    grads["gla_norm"] = dgn
    grads["sgu_ln_g"] = dlg
    grads["sgu_ln_b"] = dlb
    grads["w_spatial"] = dwsp
    grads["b_spatial"] = dbsp
    done.update(finish({"w_in": in_received}))
    return loss, grad_x, grads, done


WEIGHTS = ("norm_pre_mix", "w_in", "w_gate_up", "b_gate", "gla_norm", "sgu_ln_g", "sgu_ln_b", "w_spatial",
           "b_spatial", "w_branch_gla", "w_branch_sgu", "w_out", "norm_post_mix", "norm_pre_ffn", "w_ffn_in",
           "w_ffn_out", "norm_post_ffn")
BIG = ("w_in", "w_branch_gla", "w_branch_sgu", "w_out", "w_ffn_in", "w_ffn_out")
MIXER = ("w_branch_gla", "w_branch_sgu", "w_out")
FFN = ("w_ffn_in", "w_ffn_out")
COLUMN_SHARDED = ("w_in", "w_ffn_in")
LATE_SCATTER_WAVES = 4
GATHER_WAVES = 8
SMALL = tuple(n for n in WEIGHTS if n not in BIG)
SMALL_SHARDED = ("w_gate_up", "gla_norm", "sgu_ln_g", "sgu_ln_b")
SMALL_GATHER_ROWS = 32


def kernel(x, norm_pre_mix, w_in, w_gate_up, b_gate, gla_norm, sgu_ln_g, sgu_ln_b, w_spatial, b_spatial, w_branch_gla, w_branch_sgu, w_out, norm_post_mix, norm_pre_ffn, w_ffn_in, w_ffn_out, norm_post_ffn, loss_target, m_norm_pre_mix, m_w_in, m_w_gate_up, m_b_gate, m_gla_norm, m_sgu_ln_g, m_sgu_ln_b, m_w_spatial, m_b_spatial, m_w_branch_gla, m_w_branch_sgu, m_w_out, m_norm_post_mix, m_norm_pre_ffn, m_w_ffn_in, m_w_ffn_out, m_norm_post_ffn, v_norm_pre_mix, v_w_in, v_w_gate_up, v_b_gate, v_gla_norm, v_sgu_ln_g, v_sgu_ln_b, v_w_spatial, v_b_spatial, v_w_branch_gla, v_w_branch_sgu, v_w_out, v_norm_post_mix, v_norm_pre_ffn, v_w_ffn_in, v_w_ffn_out, v_norm_post_ffn):
    given = dict(locals())
    def local(a, n):
        return a[0].T if n in COLUMN_SHARDED else a[0]

    w = {n: local(given[n], n) for n in WEIGHTS}
    m = {n: local(given["m_" + n], n) for n in WEIGHTS}
    v = {n: local(given["v_" + n], n) for n in WEIGHTS}
    xs, target = x[0], loss_target[0]

    small_shard = _pack_rows([w[n] for n in SMALL_SHARDED], SMALL_GATHER_ROWS)
    first = _all_gather_async([w["w_in"].astype(BF16), small_shard], name="gather_w_in", collective_id=2,
                              waves=GATHER_WAVES)
    rest, first, m["w_in"], v["w_in"] = lax.optimization_barrier(
        ([w[n].astype(BF16) for n in MIXER + FFN], first, m["w_in"], v["w_in"]))
    rest_blocks = _all_gather_async(rest, name="gather_rest", collective_id=1)
    W = {n: w[n] for n in SMALL if n not in SMALL_SHARDED}
    blocks = dict(zip(MIXER + FFN, rest_blocks))
    for n in ("w_branch_gla", "w_branch_sgu", "w_out", "w_ffn_out"):
        W[n] = blocks[n].reshape(-1, D_MODEL)
    W["w_ffn_in"] = blocks["w_ffn_in"]
    W["w_in_blocks"] = first[0]
    small_blocks = first[1]
    off = 0
    for n in SMALL_SHARDED:
        r, c = w[n].shape
        blk = small_blocks[:, off:off + r * c // LANES].reshape(N_DEV, r, c)
        W[n] = blk.transpose(1, 0, 2).reshape(r, N_DEV * c)
        off += _tile_rows(r * c)

    scatter_ids = iter((3, 4))

    def scatter(names, parts):
        got = _scatter_blocks_async(parts, name="scatter_" + "_".join(names), collective_id=next(scatter_ids))
        return dict(zip(names, got))

    def finish(received):
        return {n: _adamw_reduce_call(r, w[n], m[n], v[n], name="adamw_" + n) for n, r in received.items()}

    def pair(part):
        return _pair_exchange_async([part], name="pair_w_in", collective_id=5)[0]

    def chip_scatter(part):
        return _chip_scatter_async([part], name="scatter_w_in", collective_id=6, waves=LATE_SCATTER_WAVES)[0]

    loss_part, grad_x, grads, big_done = _local_step(xs, target, W, scatter, finish, pair, chip_scatter)

    late = jnp.concatenate([grads["norm_pre_mix"].reshape(8, LANES), jnp.broadcast_to(loss_part, (8, LANES))], axis=0)
    gathered = _small_gather_async([_small_early_rows(grads), late], name="gather_small", collective_id=7)
    w_in_done = list(big_done["w_in"])
    gathered, w_in_done[0] = lax.optimization_barrier((gathered, w_in_done[0]))
    big_done["w_in"] = w_in_done
    loss_row, small_done = _small_update_call(
        *gathered, *[{n: given[prefix + n] for n in SMALL} for prefix in ("", "m_", "v_")], name="small_update")

    def pick(i):
        return [small_done[n][i] if n in SMALL else (big_done[n][i].T if n in COLUMN_SHARDED else big_done[n][i])[None]
                for n in WEIGHTS]

    return (loss_row[0, 0], grad_x[None], *pick(0), *pick(1), *pick(2), *pick(3))
```

```python
import jax
import jax.numpy as jnp
from jax import lax
from jax.experimental import pallas as pl
from jax.experimental.pallas import tpu as pltpu
from jax.experimental.pallas import tpu_sc as plsc

F32 = jnp.float32
BF16 = jnp.bfloat16

D_MODEL = 1024
N_DEV = 8
CHUNK = 64
HEADS = 4
DK = 128
DV = 256
QK = HEADS * DK
GV = HEADS * DV
RANK = 16
GROUPS = 4
SBLOCK = 128
DG = 256
D_FF = 2816
FF_BLK = 704
EPS = 1e-6
Q_SCALE = DK ** -0.5
LANES = 128
VMEM_BIG = 48 * 1024 * 1024

D_IN = 7184
IN_BLK = 898
A_COLS = (0, 3072)
LOW_COLS = (3072, 3088)
S_COLS = (3088, 5136)
G_COLS = (5136, 7184)

ADAM_LR = 0.001
ADAM_B1 = 0.9
ADAM_B2 = 0.999
ADAM_EPS = 1e-08
ADAM_WD = 0.01
ADAM_STEP = 10

MESH = pl.DeviceIdType.MESH
ANY = pl.BlockSpec(memory_space=pl.ANY)
VMEM_SPEC = pl.BlockSpec(memory_space=pltpu.VMEM)


def _cp(sem=None, vmem=None):
    return pltpu.CompilerParams(dimension_semantics=sem, vmem_limit_bytes=vmem)


def _hbm(*arrays):
    return [pltpu.with_memory_space_constraint(a, pltpu.HBM) for a in arrays]


def _sig(x):
    return 0.5 * jnp.tanh(0.5 * x) + 0.5


GELU_C = 0.7978845608028654
GELU_A = 0.044715


def _gelu(x):
    t = jnp.tanh((GELU_C * x) * (1.0 + GELU_A * (x * x)))
    return (0.5 * x) * (1.0 + t)


def _gelu_and_grad(x):
    x2 = x * x
    t = jnp.tanh((GELU_C * x) * (1.0 + GELU_A * x2))
    one_t = 1.0 + t
    hx = 0.5 * x
    grad = 0.5 * one_t + (hx * (1.0 - t * t)) * (GELU_C + (3.0 * GELU_A * GELU_C) * x2)
    return hx * one_t, grad


def _logsig(x):
    return jnp.minimum(x, 0.0) - jnp.log1p(jnp.exp(-jnp.abs(x)))


def _dot(a, b, dims):
    return lax.dot_general(a, b, (dims, ((), ())), preferred_element_type=F32)


NN = ((1,), (0,))
NT = ((1,), (1,))
TN = ((0,), (0,))


def _exact_mask_dot(mask_bf16, x):
    hi = x.astype(BF16)
    r1 = x - hi.astype(F32)
    mid = r1.astype(BF16)
    lo = (r1 - mid.astype(F32)).astype(BF16)
    return _dot(mask_bf16, hi, NN) + _dot(mask_bf16, mid, NN) + _dot(mask_bf16, lo, NN)


ROW_TILE = 512
SUB_ROWS = 256


def _sub_tiles(T):
    return [pl.ds(r0, min(SUB_ROWS, T)) for r0 in range(0, T, SUB_ROWS)]


def _rt(T, W, c=0):
    return pl.BlockSpec((T, W), lambda i: (i, c))


def _rt3(nb, T, W):
    return pl.BlockSpec((nb, T, W), lambda i: (0, i, 0))


def _res(shape):
    nd = len(shape)
    return pl.BlockSpec(tuple(shape), lambda i: (0,) * nd, pipeline_mode=pl.Buffered(1))


def _acc(shape):
    nd = len(shape)
    return pl.BlockSpec(tuple(shape), lambda i: (0,) * nd, pipeline_mode=pl.Buffered(1))


def _nbytes(shape, dtype):
    n = jnp.dtype(dtype).itemsize
    for s in shape:
        n *= s
    return n


def _vmem_limit(tiles, resident, temps=16 * 1024 * 1024):
    need = 2 * sum(_nbytes(s, d) for s, d in tiles) + sum(_nbytes(s, d) for s, d in resident) + temps
    return min(need, 60 * 1024 * 1024)


def _tok_call(body, *, name, S, T, ins, outs, semantics="parallel"):
    tiles = [(spec.block_shape, a.dtype) for a, spec, kind in ins + outs if kind == "tile"]
    resident = [(a.shape, a.dtype) for a, spec, kind in ins + outs if kind == "res"]
    return pl.pallas_call(
        body, name=name, grid=(S // T,),
        in_specs=[spec for _, spec, _ in ins], out_specs=[spec for _, spec, _ in outs],
        out_shape=[pltpu.HBM(a.shape, a.dtype) for a, _, _ in outs],
        compiler_params=_cp((semantics,), _vmem_limit(tiles, resident)),
    )(*_hbm(*[a for a, _, _ in ins]))


def _tile(a, spec):
    return (a, spec, "tile")


def _whole(a):
    return (a, _res(a.shape), "res")


def _out_tile(shape, dtype, spec):
    return (jax.ShapeDtypeStruct(shape, dtype), spec, "tile")


def _out_acc(shape, dtype=F32):
    return (jax.ShapeDtypeStruct(shape, dtype), _acc(shape), "res")


def _rms_stats(x):
    r = lax.rsqrt(jnp.mean(x * x, axis=-1, keepdims=True) + EPS)
    return r, x * r


def _rms_bwd(xh, r, g, dy):
    dxh = dy * g
    dx = r * (dxh - xh * jnp.mean(dxh * xh, axis=-1, keepdims=True))
    dg = jnp.sum(dy * xh, axis=0, keepdims=True)
    return dx, dg


def _accum(ref, val):
    @pl.when(pl.program_id(0) == 0)
    def _():
        ref[...] = val

    @pl.when(pl.program_id(0) > 0)
    def _():
        ref[...] += val


def _dot_rows_t(a, w_ref, row0, o_ref, chunk=1024):
    n = o_ref.shape[1]
    for n0 in range(0, n, chunk):
        n1 = min(n, n0 + chunk)
        o_ref[:, n0:n1] = _dot(a, w_ref[row0 + n0:row0 + n1, :], NT).astype(o_ref.dtype)


def _in_proj_call(x, g1, w_blocks, *, name, T=ROW_TILE):
    S = x.shape[0]

    def body(x_ref, g_ref, blk_ref, a_ref, pa_ref, al_ref, ps_ref, pg_ref, w_ref):
        @pl.when(pl.program_id(0) == 0)
        def _():
            for d in range(N_DEV):
                w_ref[d * IN_BLK:(d + 1) * IN_BLK, :] = blk_ref[d]

        _, xh = _rms_stats(x_ref[...])
        a = (xh * g_ref[...]).astype(BF16)
        a_ref[...] = a
        _dot_rows_t(a, w_ref, A_COLS[0], pa_ref)
        _dot_rows_t(a, w_ref, LOW_COLS[0], al_ref)
        _dot_rows_t(a, w_ref, S_COLS[0], ps_ref)
        _dot_rows_t(a, w_ref, G_COLS[0], pg_ref)

    widths = (D_MODEL, A_COLS[1] - A_COLS[0], LANES, S_COLS[1] - S_COLS[0], G_COLS[1] - G_COLS[0])
    return _tok_call(
        body, name=name, S=S, T=T, semantics="arbitrary",
        ins=[_tile(x, _rt(T, D_MODEL)), _whole(g1), _whole(w_blocks)],
        outs=[_out_tile((S, w), BF16, _rt(T, w)) for w in widths] + [_out_acc((D_IN, D_MODEL), BF16)])


def _mixer_tail_call(y_gla, y_sgu, pg, x, w_bg, w_bs, w_out, g2, g3, *, name, T=ROW_TILE):
    S = x.shape[0]

    def body(yg_ref, ys_ref, pg_ref, x_ref, wbg_ref, wbs_ref, wo_ref, g2_ref, g3_ref,
             t1_ref, t2_ref, mg_ref, mix_ref, x1_ref, h_ref):
        for rows in _sub_tiles(T):
            t1 = _dot(yg_ref[rows, :], wbg_ref[...], NN)
            t2 = _dot(ys_ref[rows, :], wbs_ref[...], NN)
            t1_ref[rows, :] = t1.astype(BF16)
            t2_ref[rows, :] = t2.astype(BF16)
            sg = _sig(pg_ref[rows, pl.ds(0, D_MODEL)].astype(F32))
            ss = _sig(pg_ref[rows, pl.ds(D_MODEL, D_MODEL)].astype(F32))
            merged = (sg * t1 + ss * t2).astype(BF16)
            mg_ref[rows, :] = merged
            mix = _dot(merged, wo_ref[...], NN)
            mix_ref[rows, :] = mix
            _, mh = _rms_stats(mix)
            x1 = x_ref[rows, :] + mh * g2_ref[...]
            x1_ref[rows, :] = x1
            _, xh = _rms_stats(x1)
            h_ref[rows, :] = (xh * g3_ref[...]).astype(BF16)

    row = _rt(T, D_MODEL)
    b16 = lambda: _out_tile((S, D_MODEL), BF16, row)
    f32 = lambda: _out_tile((S, D_MODEL), F32, row)
    return _tok_call(
        body, name=name, S=S, T=T,
        ins=[_tile(y_gla, row), _tile(y_sgu, row), _tile(pg, _rt(T, 2 * D_MODEL)), _tile(x, row),
             _whole(w_bg), _whole(w_bs), _whole(w_out), _whole(g2), _whole(g3)],
        outs=[b16(), b16(), b16(), f32(), f32(), b16()])


FF_CHUNKS = ((0, 1024), (1024, 2048), (2048, D_FF))


def _ffn_in_call(h, wfi, *, name, T=ROW_TILE):
    S = h.shape[0]

    def body(h_ref, w_ref, gu_ref, z_ref):
        hv = h_ref[...]
        for c0, c1 in FF_CHUNKS:
            gate = _dot(hv, w_ref[c0:c1, :], NT)
            up = _dot(hv, w_ref[D_FF + c0:D_FF + c1, :], NT)
            gu_ref[:, c0:c1] = gate.astype(BF16)
            gu_ref[:, D_FF + c0:D_FF + c1] = up.astype(BF16)
            z_ref[:, c0:c1] = (gate * _sig(gate) * up).astype(BF16)

    return _tok_call(
        body, name=name, S=S, T=T,
        ins=[_tile(h, _rt(T, D_MODEL)), _whole(wfi)],
        outs=[_out_tile((S, 2 * D_FF), BF16, _rt(T, 2 * D_FF)),
              _out_tile((S, D_FF), BF16, _rt(T, D_FF))])


def _ffn_out_loss_call(z, wfo, x1, target, g4, *, name, T=ROW_TILE):
    S = x1.shape[0]

    def body(z_ref, w_ref, x1_ref, t_ref, g4_ref, loss_ref, dx2_ref, dy_ref, dg4_ref):
        loss = jnp.zeros((1, 1), F32)
        dg4 = jnp.zeros((1, D_MODEL), F32)
        for rows in _sub_tiles(T):
            y = _dot(z_ref[rows, :], w_ref[...], NN)
            r, yh = _rms_stats(y)
            diff = x1_ref[rows, :] + yh * g4_ref[...] - t_ref[rows, :]
            loss = loss + 0.5 * jnp.sum(jnp.mean(diff * diff, axis=-1, keepdims=True), axis=0, keepdims=True)
            dx2 = diff * (1.0 / D_MODEL)
            dx2_ref[rows, :] = dx2
            dy, dg = _rms_bwd(yh, r, g4_ref[...], dx2)
            dy_ref[rows, :] = dy.astype(BF16)
            dg4 = dg4 + dg
        _accum(loss_ref, jnp.broadcast_to(loss, (1, LANES)))
        _accum(dg4_ref, dg4)

    row = _rt(T, D_MODEL)
    return _tok_call(
        body, name=name, S=S, T=T, semantics="arbitrary",
        ins=[_tile(z, _rt(T, D_FF)), _whole(wfo), _tile(x1, row), _tile(target, row), _whole(g4)],
        outs=[_out_acc((1, LANES)), _out_tile((S, D_MODEL), F32, row), _out_tile((S, D_MODEL), BF16, row),
              _out_acc((1, D_MODEL))])


def _ffn_out_bwd_call(dy, wfo, gu, *, name, T=ROW_TILE):
    S = dy.shape[0]

    def body(dy_ref, w_ref, gu_ref, dgu_ref):
        dyv = dy_ref[...]
        for c0, c1 in FF_CHUNKS:
            dz = _dot(dyv, w_ref[c0:c1, :], NT).astype(BF16)
            gt = gu_ref[:, c0:c1]
            up = gu_ref[:, D_FF + c0:D_FF + c1]
            s = _sig(gt)
            dgu_ref[:, c0:c1] = dz * up * s * (1.0 + gt * (1.0 - s))
            dgu_ref[:, D_FF + c0:D_FF + c1] = dz * gt * s

    wide = _rt(T, 2 * D_FF)
    return _tok_call(
        body, name=name, S=S, T=T,
        ins=[_tile(dy, _rt(T, D_MODEL)), _whole(wfo), _tile(gu, wide)],
        outs=[_out_tile((S, 2 * D_FF), BF16, wide)])[0]


def _ffn_in_bwd_call(dgu, wfi, dx2, x1, mix, g3, g2, *, name, T=ROW_TILE):
    S = x1.shape[0]

    def body(dgu_ref, w_ref, dx2_ref, x1_ref, mix_ref, g3_ref, g2_ref, dx1_ref, dmix_ref, dg3_ref, dg2_ref):
        dg3 = jnp.zeros((1, D_MODEL), F32)
        dg2 = jnp.zeros((1, D_MODEL), F32)
        for rows in _sub_tiles(T):
            dh = _dot(dgu_ref[rows, :], w_ref[...], NN)
            r3, xh = _rms_stats(x1_ref[rows, :])
            d3, g = _rms_bwd(xh, r3, g3_ref[...], dh)
            dg3 = dg3 + g
            dx1 = dx2_ref[rows, :] + d3
            dx1_ref[rows, :] = dx1
            r2, mh = _rms_stats(mix_ref[rows, :])
            dmix, g = _rms_bwd(mh, r2, g2_ref[...], dx1)
            dg2 = dg2 + g
            dmix_ref[rows, :] = dmix.astype(BF16)
        _accum(dg3_ref, dg3)
        _accum(dg2_ref, dg2)

    row = _rt(T, D_MODEL)
    return _tok_call(
        body, name=name, S=S, T=T, semantics="arbitrary",
        ins=[_tile(dgu, _rt(T, 2 * D_FF)), _whole(wfi), _tile(dx2, row), _tile(x1, row), _tile(mix, row),
             _whole(g3), _whole(g2)],
        outs=[_out_tile((S, D_MODEL), F32, row), _out_tile((S, D_MODEL), BF16, row),
              _out_acc((1, D_MODEL)), _out_acc((1, D_MODEL))])


def _mixer_bwd_call(dmix, t1, t2, pg, w_out, w_bg, w_bs, *, name, T=ROW_TILE):
    S = dmix.shape[0]

    def body(dmix_ref, t1_ref, t2_ref, pg_ref, wo_ref, wbg_ref, wbs_ref, dt1_ref, dt2_ref, dpg_ref, dyg_ref, dys_ref):
        gg, gs = pl.ds(0, D_MODEL), pl.ds(D_MODEL, D_MODEL)
        for rows in _sub_tiles(T):
            dm = _dot(dmix_ref[rows, :], wo_ref[...], NT)
            sg = _sig(pg_ref[rows, gg].astype(F32))
            ss = _sig(pg_ref[rows, gs].astype(F32))
            dt1 = (dm * sg).astype(BF16)
            dt2 = (dm * ss).astype(BF16)
            dt1_ref[rows, :] = dt1
            dt2_ref[rows, :] = dt2
            dpg_ref[rows, gg] = (dm * t1_ref[rows, :].astype(F32) * sg * (1.0 - sg)).astype(BF16)
            dpg_ref[rows, gs] = (dm * t2_ref[rows, :].astype(F32) * ss * (1.0 - ss)).astype(BF16)
            dyg_ref[rows, :] = _dot(dt1, wbg_ref[...], NT).astype(BF16)
            dys_ref[rows, :] = _dot(dt2, wbs_ref[...], NT).astype(BF16)

    row = _rt(T, D_MODEL)
    wide = _rt(T, 2 * D_MODEL)
    b16 = lambda: _out_tile((S, D_MODEL), BF16, row)
    return _tok_call(
        body, name=name, S=S, T=T,
        ins=[_tile(dmix, row), _tile(t1, row), _tile(t2, row), _tile(pg, wide), _whole(w_out), _whole(w_bg), _whole(w_bs)],
        outs=[b16(), b16(), _out_tile((S, 2 * D_MODEL), BF16, wide), b16(), b16()])


def _in_proj_bwd_call(dpa, dlow, dps, dpg, w_in_t, x, dx1, g1, *, name, T=ROW_TILE):
    S = x.shape[0]

    def body(dpa_ref, dl_ref, dps_ref, dpg_ref, w_ref, x_ref, dx1_ref, g1_ref, gx_ref, dg1_ref):
        da = (_dot(dpa_ref[...], w_ref[A_COLS[0]:A_COLS[1], :], NN)
              + _dot(dl_ref[...], w_ref[LOW_COLS[0]:LOW_COLS[0] + LANES, :], NN)
              + _dot(dps_ref[...], w_ref[S_COLS[0]:S_COLS[1], :], NN)
              + _dot(dpg_ref[...], w_ref[G_COLS[0]:G_COLS[1], :], NN))
        r, xh = _rms_stats(x_ref[...])
        dxa, dg = _rms_bwd(xh, r, g1_ref[...], da)
        gx_ref[...] = dx1_ref[...] + dxa
        _accum(dg1_ref, dg)

    row = _rt(T, D_MODEL)
    return _tok_call(
        body, name=name, S=S, T=T, semantics="arbitrary",
        ins=[_tile(dpa, _rt(T, dpa.shape[1])), _tile(dlow, _rt(T, dlow.shape[1])), _tile(dps, _rt(T, dps.shape[1])),
             _tile(dpg, _rt(T, dpg.shape[1])), _whole(w_in_t), _tile(x, row), _tile(dx1, row), _whole(g1)],
        outs=[_out_tile((S, D_MODEL), F32, row), _out_acc((1, D_MODEL))])


TOKEN_TILE = 512


def _weight_grads_part(arrays, pairs, *, tk=TOKEN_TILE, out_dtype=BF16):
    S = arrays[0].shape[-2]
    tk = min(tk, S)
    n_in = len(arrays)

    def out_shape(i, j):
        a, b = arrays[i], arrays[j]
        if a.ndim == 3:
            return (a.shape[0], a.shape[2], b.shape[1])
        if b.ndim == 3:
            return (b.shape[0], a.shape[1], b.shape[2])
        return (a.shape[1], b.shape[1])

    shapes = [out_shape(i, j) for i, j in pairs]

    in_place = out_dtype == F32

    def body(ins, outs, accs):
        k = pl.program_id(0)
        if in_place:
            accs = outs

        @pl.when(k == 0)
        def _():
            for acc in accs:
                acc[...] = jnp.zeros_like(acc)

        for (i, j), acc in zip(pairs, accs):
            a_ref, b_ref = ins[i], ins[j]
            if len(a_ref.shape) == 3:
                for n in range(a_ref.shape[0]):
                    acc[n] += _dot(a_ref[n], b_ref[...], TN)
            elif len(b_ref.shape) == 3:
                a = a_ref[...]
                for n in range(b_ref.shape[0]):
                    acc[n] += _dot(a, b_ref[n], TN)
            else:
                b = b_ref[...]
                for m0 in range(0, a_ref.shape[1], 1024):
                    m1 = min(a_ref.shape[1], m0 + 1024)
                    acc[m0:m1, :] += _dot(a_ref[:, m0:m1], b, TN)

        if not in_place:
            @pl.when(k == S // tk - 1)
            def _():
                for out, acc in zip(outs, accs):
                    out[...] = acc[...].astype(out.dtype)

    def in_spec(a):
        if a.ndim == 3:
            return pl.BlockSpec((a.shape[0], tk, a.shape[2]), lambda k: (0, k, 0))
        return pl.BlockSpec((tk, a.shape[1]), lambda k: (k, 0))

    return dict(
        body=body, steps=S // tk, arrays=list(arrays),
        in_specs=[in_spec(a) for a in arrays],
        out_specs=[_acc(s) for s in shapes],
        out_shapes=[pltpu.HBM(s, out_dtype) for s in shapes],
        scratch=[] if in_place else [pltpu.VMEM(s, F32) for s in shapes],
        tiles=[(in_spec(a).block_shape, a.dtype) for a in arrays],
        resident=[(s, F32) for s in shapes] + ([] if in_place else [(s, BF16) for s in shapes]))


def _weight_grads_call(arrays, pairs, *, name, tk=TOKEN_TILE):
    part = _weight_grads_part(arrays, pairs, tk=tk)
    n_in, n_out = len(part["arrays"]), len(part["out_shapes"])

    def body(*refs):
        part["body"](refs[:n_in], refs[n_in:n_in + n_out], refs[n_in + n_out:])

    return pl.pallas_call(
        body, name=name, grid=(part["steps"],),
        in_specs=part["in_specs"], out_specs=part["out_specs"], out_shape=part["out_shapes"],
        scratch_shapes=part["scratch"],
        compiler_params=_cp(("arbitrary",), _vmem_limit(part["tiles"], part["resident"])),
    )(*_hbm(*part["arrays"]))


def _ffn_in_grad_call(h, dgu, *, name, tk=TOKEN_TILE):
    S = h.shape[0]
    tk = min(tk, S)
    shape = (D_FF, D_MODEL)

    def body(h_ref, dgu_ref, out_ref, acc):
        k = pl.program_id(1)

        @pl.when(k == 0)
        def _():
            acc[...] = jnp.zeros_like(acc)

        hv = h_ref[...]
        for c0, c1 in FF_CHUNKS:
            acc[c0:c1, :] += _dot(dgu_ref[:, c0:c1], hv, TN)

        @pl.when(k == S // tk - 1)
        def _():
            out_ref[...] = acc[...].astype(out_ref.dtype)

    tiles = [((tk, D_MODEL), BF16), ((tk, D_FF), BF16), (shape, BF16)]
    return pl.pallas_call(
        body, name=name, grid=(2, S // tk),
        in_specs=[pl.BlockSpec((tk, D_MODEL), lambda g, k: (k, 0)),
                  pl.BlockSpec((tk, D_FF), lambda g, k: (k, g))],
        out_specs=pl.BlockSpec(shape, lambda g, k: (g, 0)),
        out_shape=pltpu.HBM((2 * D_FF, D_MODEL), BF16),
        scratch_shapes=[pltpu.VMEM(shape, F32)],
        compiler_params=_cp(("parallel", "arbitrary"), _vmem_limit(tiles, [(shape, F32)])),
    )(*_hbm(h, dgu))


GLA_TILE = 256
Q_OFF, K_OFF, V_OFF, R_OFF = 0, QK, 2 * QK, 2 * QK + GV


def _tri(lower):
    r = lax.broadcasted_iota(jnp.int32, (CHUNK, CHUNK), 0)
    c = lax.broadcasted_iota(jnp.int32, (CHUNK, CHUNK), 1)
    return jnp.where((r >= c) if lower else (c >= r), 1.0, 0.0).astype(BF16)


def _gla_fwd_call(pa, alow, wg, bgate, gnorm, *, name):
    S = pa.shape[0]
    Tg = min(GLA_TILE, S)
    cb = Tg // CHUNK

    def body(pa_ref, al_ref, wg_ref, bg_ref, gn_ref, y_ref, st_ref, state):
        @pl.when(pl.program_id(0) == 0)
        def _():
            state[...] = jnp.zeros_like(state)

        logit = _dot(al_ref[...], wg_ref[...], NN) + bg_ref[...]
        ls = _logsig(logit) * (1.0 / 16.0)
        tri = _tri(True)
        for c in range(cb):
            rows = pl.ds(c * CHUNK, CHUNK)
            cum = _exact_mask_dot(tri, ls[c * CHUNK:(c + 1) * CHUNK])
            tot = cum[CHUNK - 1:CHUNK]
            kd = (pa_ref[rows, pl.ds(K_OFF, QK)].astype(F32) * jnp.exp(tot - cum)).astype(BF16)
            decay = jnp.exp(tot)
            for h in range(HEADS):
                lanes = slice(h * DK, (h + 1) * DK)
                new = state[h] * decay[:, lanes] + _dot(pa_ref[rows, pl.ds(V_OFF + h * DV, DV)], kd[:, lanes], TN)
                state[h] = new
                st_ref[c, h] = new
        for c in range(cb):
            rows = pl.ds(c * CHUNK, CHUNK)
            for h in range(HEADS):
                qs = (pa_ref[rows, pl.ds(Q_OFF + h * DK, DK)].astype(F32) * Q_SCALE).astype(BF16)
                o = _dot(qs, st_ref[c, h].astype(BF16), NT)
                rs = lax.rsqrt(jnp.mean(o * o, axis=-1, keepdims=True) + EPS)
                rr = pa_ref[rows, pl.ds(R_OFF + h * DV, DV)].astype(F32)
                y_ref[rows, pl.ds(h * DV, DV)] = (o * rs * gn_ref[h] * (rr * _sig(rr))).astype(BF16)

    return pl.pallas_call(
        body, name=name, grid=(S // Tg,),
        in_specs=[
            pl.BlockSpec((Tg, 2 * QK + 2 * GV), lambda t: (t, 0)),
            pl.BlockSpec((Tg, LANES), lambda t: (t, 0)),
            pl.BlockSpec((LANES, QK), lambda t: (0, 0)),
            pl.BlockSpec((1, QK), lambda t: (0, 0)),
            pl.BlockSpec((HEADS, 1, DV), lambda t: (0, 0, 0)),
        ],
        out_specs=[
            pl.BlockSpec((Tg, GV), lambda t: (t, 0)),
            pl.BlockSpec((cb, HEADS, DV, DK), lambda t: (t, 0, 0, 0)),
        ],
        out_shape=[pltpu.HBM((S, GV), BF16), pltpu.HBM((S // CHUNK, HEADS, DV, DK), F32)],
        scratch_shapes=[pltpu.VMEM((HEADS, DV, DK), F32)],
        compiler_params=_cp(("arbitrary",), VMEM_BIG),
    )(*_hbm(pa, alow, wg, bgate, gnorm))


def _gla_bwd_call(pa, alow, wg, bgate, gnorm, states, dy, *, name):
    S = pa.shape[0]
    Tg = min(GLA_TILE, S)
    cb = Tg // CHUNK
    nt = S // Tg

    def rev(t):
        return nt - 1 - t

    def body(pa_ref, al_ref, wg_ref, bg_ref, gn_ref, st_ref, prev_ref, dy_ref,
             dpa_ref, dlow_ref, dwg_ref, dgn_ref, dbg_ref, carry, pbuf, dlbuf):
        t = pl.program_id(0)

        @pl.when(t == 0)
        def _():
            carry[...] = jnp.zeros_like(carry)
            dwg_ref[...] = jnp.zeros_like(dwg_ref)
            dgn_ref[...] = jnp.zeros_like(dgn_ref)
            dbg_ref[...] = jnp.zeros_like(dbg_ref)

        logit = _dot(al_ref[...], wg_ref[...], NN) + bg_ref[...]
        ls = _logsig(logit) * (1.0 / 16.0)
        sneg = 1.0 / (1.0 + jnp.exp(logit))
        tri = _tri(True)
        upper = _tri(False)
        first_tile = rev(t) == 0
        heads = range(HEADS)

        w, decay, kd = [], [], []
        for c in range(cb):
            rows = pl.ds(c * CHUNK, CHUNK)
            cum = _exact_mask_dot(tri, ls[c * CHUNK:(c + 1) * CHUNK])
            tot = cum[CHUNK - 1:CHUNK]
            w.append(jnp.exp(tot - cum))
            decay.append(jnp.exp(tot))
            kd.append(pa_ref[rows, pl.ds(K_OFF, QK)].astype(F32) * w[c])

        dgn = [jnp.zeros((1, DV), F32) for _ in heads]
        for c in range(cb):
            rows = pl.ds(c * CHUNK, CHUNK)
            for h in heads:
                gn = gn_ref[h]
                qs = (pa_ref[rows, pl.ds(Q_OFF + h * DK, DK)].astype(F32) * Q_SCALE).astype(BF16)
                st16 = st_ref[c, h].astype(BF16)
                o = _dot(qs, st16, NT)
                rs = lax.rsqrt(jnp.mean(o * o, axis=-1, keepdims=True) + EPS)
                oh = o * rs
                rr = pa_ref[rows, pl.ds(R_OFF + h * DV, DV)].astype(F32)
                sr = _sig(rr)
                dyv = dy_ref[rows, pl.ds(h * DV, DV)].astype(F32)
                dpa_ref[rows, pl.ds(R_OFF + h * DV, DV)] = (
                    dyv * oh * gn * sr * (1.0 + rr * (1.0 - sr))).astype(BF16)
                don = dyv * (rr * sr)
                dgn[h] = dgn[h] + jnp.sum(don * oh, axis=0, keepdims=True)
                doh = don * gn
                do16 = (rs * (doh - oh * jnp.mean(doh * oh, axis=-1, keepdims=True))).astype(BF16)
                dpa_ref[rows, pl.ds(Q_OFF + h * DK, DK)] = (_dot(do16, st16, NN) * Q_SCALE).astype(BF16)
                pbuf[c, h] = _dot(do16, qs, TN)
        for h in heads:
            dgn_ref[h] += dgn[h]

        dkd = [[None] * HEADS for _ in range(cb)]
        ddecay = [[None] * HEADS for _ in range(cb)]
        for c in reversed(range(cb)):
            rows = pl.ds(c * CHUNK, CHUNK)
            for h in heads:
                lanes = slice(h * DK, (h + 1) * DK)
                gt = pbuf[c, h] + carry[h]
                gt16 = gt.astype(BF16)
                dkd[c][h] = _dot(pa_ref[rows, pl.ds(V_OFF + h * DV, DV)], gt16, NN)
                dpa_ref[rows, pl.ds(V_OFF + h * DV, DV)] = _dot(kd[c][:, lanes].astype(BF16), gt16, NT).astype(BF16)
                if c > 0:
                    st_prev = st_ref[c - 1, h]
                else:
                    st_prev = jnp.where(first_tile, 0.0, prev_ref[0, h])
                ddecay[c][h] = jnp.sum(gt * st_prev, axis=0, keepdims=True)
                carry[h] = gt * decay[c][:, lanes]

        dbg = jnp.zeros((1, QK), F32)
        for c in range(cb):
            rows = pl.ds(c * CHUNK, CHUNK)
            dkd_c = jnp.concatenate(dkd[c], axis=1)
            dpa_ref[rows, pl.ds(K_OFF, QK)] = (dkd_c * w[c]).astype(BF16)
            e = dkd_c * kd[c]
            dtot = jnp.sum(e, axis=0, keepdims=True) + jnp.concatenate(ddecay[c], axis=1) * decay[c]
            dls = dtot - _exact_mask_dot(upper, e)
            dlogit = dls * (1.0 / 16.0) * sneg[c * CHUNK:(c + 1) * CHUNK]
            dlbuf[rows, :] = dlogit.astype(BF16)
            dbg = dbg + jnp.sum(dlogit, axis=0, keepdims=True)
        dbg_ref[...] += dbg

        dl16 = dlbuf[...]
        dlow_ref[...] = _dot(dl16, wg_ref[...], NT).astype(BF16)
        dwg_ref[...] += _dot(al_ref[...], dl16, TN)

    wide = 2 * QK + 2 * GV
    tiles = [((Tg, wide), BF16), ((cb + 1, HEADS, DV, DK), F32), ((Tg, GV), BF16), ((Tg, wide), BF16),
             ((Tg, QK), BF16)]
    resident = [((cb + 1, HEADS, DV, DK), F32)]
    return pl.pallas_call(
        body, name=name, grid=(nt,),
        in_specs=[
            pl.BlockSpec((Tg, wide), lambda t: (rev(t), 0)),
            pl.BlockSpec((Tg, LANES), lambda t: (rev(t), 0)),
            pl.BlockSpec((LANES, QK), lambda t: (0, 0)),
            pl.BlockSpec((1, QK), lambda t: (0, 0)),
            pl.BlockSpec((HEADS, 1, DV), lambda t: (0, 0, 0)),
            pl.BlockSpec((cb, HEADS, DV, DK), lambda t: (rev(t), 0, 0, 0)),
            pl.BlockSpec((1, HEADS, DV, DK), lambda t: (jnp.maximum(rev(t) * cb - 1, 0), 0, 0, 0)),
            pl.BlockSpec((Tg, GV), lambda t: (rev(t), 0)),
        ],
        out_specs=[
            pl.BlockSpec((Tg, wide), lambda t: (rev(t), 0)),
            pl.BlockSpec((Tg, LANES), lambda t: (rev(t), 0)),
            pl.BlockSpec((LANES, QK), lambda t: (0, 0)),
            pl.BlockSpec((HEADS, 1, DV), lambda t: (0, 0, 0)),
            pl.BlockSpec((1, QK), lambda t: (0, 0)),
        ],
        out_shape=[pltpu.HBM((S, wide), BF16), pltpu.HBM((S, LANES), BF16), jax.ShapeDtypeStruct((LANES, QK), F32),
                   jax.ShapeDtypeStruct((HEADS, 1, DV), F32), jax.ShapeDtypeStruct((1, QK), F32)],
        scratch_shapes=[pltpu.VMEM((HEADS, DV, DK), F32), pltpu.VMEM((cb, HEADS, DV, DK), F32),
                        pltpu.VMEM((Tg, QK), BF16)],
        compiler_params=_cp(("arbitrary",), _vmem_limit(tiles, resident)),
    )(*_hbm(pa, alow, wg, bgate, gnorm, states, states, dy))


SGU_TILE = 256


def _sgu_mask():
    r = lax.broadcasted_iota(jnp.int32, (SBLOCK, SBLOCK), 0)
    c = lax.broadcasted_iota(jnp.int32, (SBLOCK, SBLOCK), 1)
    return (c < CHUNK) | (r >= CHUNK)


def _ln_stats(vf):
    mu = jnp.mean(vf, axis=-1, keepdims=True)
    xc = vf - mu
    rs = lax.rsqrt(jnp.mean(xc * xc, axis=-1, keepdims=True) + EPS)
    return rs, xc * rs


def _sgu_fwd_call(ps, ln_g, ln_b, w_sp, b_sp, *, name):
    S = ps.shape[0]
    Ts = min(SGU_TILE, S)

    def body(ps_ref, lg_ref, lb_ref, w_ref, b_ref, y_ref):
        mask = _sgu_mask()
        for g in range(GROUPS):
            wm = jnp.where(mask, w_ref[g], 0.0).astype(BF16)
            for p in range(Ts // SBLOCK):
                rows = pl.ds(p * SBLOCK, SBLOCK)
                u = _gelu(ps_ref[rows, pl.ds(g * DG, DG)].astype(F32))
                _, xh = _ln_stats(_gelu(ps_ref[rows, pl.ds(D_MODEL + g * DG, DG)].astype(F32)))
                vn = xh * lg_ref[g] + lb_ref[g]
                mixed = _dot(wm, vn.astype(BF16), NN) + b_ref[g]
                y_ref[rows, pl.ds(g * DG, DG)] = (u * mixed).astype(BF16)

    full3 = lambda a, b, c: pl.BlockSpec((a, b, c), lambda t: (0, 0, 0))
    return pl.pallas_call(
        body, name=name, grid=(S // Ts,),
        in_specs=[pl.BlockSpec((Ts, 2 * D_MODEL), lambda t: (t, 0)),
                  full3(GROUPS, 1, DG), full3(GROUPS, 1, DG), full3(GROUPS, SBLOCK, SBLOCK), full3(GROUPS, SBLOCK, 1)],
        out_specs=pl.BlockSpec((Ts, D_MODEL), lambda t: (t, 0)),
        out_shape=pltpu.HBM((S, D_MODEL), BF16),
        compiler_params=_cp(("parallel",)),
    )(*_hbm(ps, ln_g, ln_b, w_sp, b_sp))


def _sgu_bwd_call(ps, ln_g, ln_b, w_sp, b_sp, dy, *, name):
    S = ps.shape[0]
    Ts = min(SGU_TILE, S)

    def body(ps_ref, lg_ref, lb_ref, w_ref, b_ref, dy_ref, ds_ref, dlg_ref, dlb_ref, dw_ref, db_ref):
        @pl.when(pl.program_id(0) == 0)
        def _():
            dlg_ref[...] = jnp.zeros_like(dlg_ref)
            dlb_ref[...] = jnp.zeros_like(dlb_ref)
            dw_ref[...] = jnp.zeros_like(dw_ref)
            db_ref[...] = jnp.zeros_like(db_ref)

        mask = _sgu_mask()
        for g in range(GROUPS):
            wm = jnp.where(mask, w_ref[g], 0.0).astype(BF16)
            lg = lg_ref[g]
            for p in range(Ts // SBLOCK):
                rows = pl.ds(p * SBLOCK, SBLOCK)
                su = ps_ref[rows, pl.ds(g * DG, DG)].astype(F32)
                sv = ps_ref[rows, pl.ds(D_MODEL + g * DG, DG)].astype(F32)
                u, du = _gelu_and_grad(su)
                gv, dgv = _gelu_and_grad(sv)
                rs, xh = _ln_stats(gv)
                vn16 = (xh * lg + lb_ref[g]).astype(BF16)
                mixed = _dot(wm, vn16, NN) + b_ref[g]
                dyv = dy_ref[rows, pl.ds(g * DG, DG)].astype(F32)
                ds_ref[rows, pl.ds(g * DG, DG)] = (dyv * mixed * du).astype(BF16)
                dmix = dyv * u
                dmix16 = dmix.astype(BF16)
                db_ref[g] += jnp.sum(dmix, axis=-1, keepdims=True)
                dw_ref[g] += jnp.where(mask, _dot(dmix16, vn16, NT), 0.0)
                dvn = _dot(wm, dmix16, TN)
                dlg_ref[g] += jnp.sum(dvn * xh, axis=0, keepdims=True)
                dlb_ref[g] += jnp.sum(dvn, axis=0, keepdims=True)
                dxh = dvn * lg
                dvf = rs * (dxh - jnp.mean(dxh, axis=-1, keepdims=True)
                            - xh * jnp.mean(dxh * xh, axis=-1, keepdims=True))
                ds_ref[rows, pl.ds(D_MODEL + g * DG, DG)] = (dvf * dgv).astype(BF16)

    full3 = lambda a, b, c: pl.BlockSpec((a, b, c), lambda t: (0, 0, 0))
    tiles = [((Ts, 2 * D_MODEL), BF16), ((Ts, D_MODEL), BF16), ((Ts, 2 * D_MODEL), BF16)]
    return pl.pallas_call(
        body, name=name, grid=(S // Ts,),
        in_specs=[pl.BlockSpec((Ts, 2 * D_MODEL), lambda t: (t, 0)),
                  full3(GROUPS, 1, DG), full3(GROUPS, 1, DG), full3(GROUPS, SBLOCK, SBLOCK), full3(GROUPS, SBLOCK, 1),
                  pl.BlockSpec((Ts, D_MODEL), lambda t: (t, 0))],
        out_specs=[pl.BlockSpec((Ts, 2 * D_MODEL), lambda t: (t, 0)),
                   full3(GROUPS, 1, DG), full3(GROUPS, 1, DG), full3(GROUPS, SBLOCK, SBLOCK), full3(GROUPS, SBLOCK, 1)],
        out_shape=[pltpu.HBM((S, 2 * D_MODEL), BF16),
                   jax.ShapeDtypeStruct((GROUPS, 1, DG), F32), jax.ShapeDtypeStruct((GROUPS, 1, DG), F32),
                   jax.ShapeDtypeStruct((GROUPS, SBLOCK, SBLOCK), F32),
                   jax.ShapeDtypeStruct((GROUPS, SBLOCK, 1), F32)],
        compiler_params=_cp(("arbitrary",), _vmem_limit(tiles, [])),
    )(*_hbm(ps, ln_g, ln_b, w_sp, b_sp, dy))


def _position():
    return lax.axis_index("x"), lax.axis_index("y"), lax.axis_index("c")


def _gather_copies(srcs, dsts, send_sems, recv_sems, local_sems):
    x, y, c = _position()
    me, sibling = (x, y, c), (x, y, 1 - c)
    chips = [(1 - x, y), (x, 1 - y), (1 - x, 1 - y)]
    n = len(srcs)

    def slab(a, block):
        px, py, pc = block
        return dsts[a].at[4 * px + 2 * py + pc]

    def copy(a, k, block, to, src=None):
        return pltpu.make_async_remote_copy(
            src_ref=slab(a, block) if src is None else src, dst_ref=slab(a, block),
            send_sem=send_sems.at[7 * a + k], recv_sem=recv_sems.at[7 * a + k], device_id=to, device_id_type=MESH)

    mine = [pltpu.make_async_copy(srcs[a], slab(a, me), local_sems.at[a]) for a in range(n)]
    for cp in mine:
        cp.start()
    first = []
    for a in range(n):
        first.append(copy(a, 0, me, sibling, src=srcs[a]))
        first += [copy(a, 1 + j, me, (*chip, c), src=srcs[a]) for j, chip in enumerate(chips)]
    for cp in first:
        cp.start()
    passed = []
    for j, chip in enumerate(chips):
        for a in range(n):
            copy(a, 1 + j, (*chip, c), me).wait_recv()
            cp = copy(a, 4 + j, (*chip, c), sibling)
            cp.start()
            passed.append(cp)
    for a in range(n):
        copy(a, 0, sibling, me).wait_recv()
        for j, chip in enumerate(chips):
            copy(a, 4 + j, (*chip, 1 - c), me).wait_recv()
    for cp in first + passed:
        cp.wait_send()
    for cp in mine:
        cp.wait()


def _gather_copies_relayed(srcs, dsts, send_sems, recv_sems, local_sems, waves=1):
    x, y, c = _position()
    me, sibling = (x, y, c), (x, y, 1 - c)
    x_chip, y_chip, d_chip = (1 - x, y), (x, 1 - y), (1 - x, 1 - y)
    south = c == 0
    relay_from = (jnp.where(south, x, 1 - x), jnp.where(south, 1 - y, y), c)
    relay_to = (jnp.where(south, 1 - x, x), jnp.where(south, y, 1 - y), c)
    n = len(srcs)

    def cut(a, w):
        rows = srcs[a].shape[0]
        step = (rows // waves) // 16 * 16
        if waves == 1 or step == 0:
            return pl.ds(0, rows) if w == 0 else None
        return pl.ds(w * step, step if w < waves - 1 else rows - w * step)

    def slab(a, block, w):
        px, py, pc = block
        return dsts[a].at[4 * px + 2 * py + pc, cut(a, w)]

    def copy(a, k, w, block, to, own=False):
        sem = (7 * a + k) * waves + w
        return pltpu.make_async_remote_copy(
            src_ref=srcs[a].at[cut(a, w)] if own else slab(a, block, w), dst_ref=slab(a, block, w),
            send_sem=send_sems.at[sem], recv_sem=recv_sems.at[sem], device_id=to, device_id_type=MESH)

    pieces = [(a, w) for w in range(waves) for a in range(n) if cut(a, w) is not None]
    mine = [pltpu.make_async_copy(srcs[a], dsts[a].at[4 * x + 2 * y + c], local_sems.at[a]) for a in range(n)]
    for cp in mine:
        cp.start()
    sent = []
    for a, w in pieces:
        sent += [copy(a, 0, w, me, sibling, own=True), copy(a, 1, w, me, (*x_chip, c), own=True),
                 copy(a, 2, w, me, (*y_chip, c), own=True)]
    for cp in sent:
        cp.start()
    for a, w in pieces:
        copy(a, 1, w, (*x_chip, c), me).wait_recv()
        copy(a, 2, w, (*y_chip, c), me).wait_recv()
        later = [copy(a, 3, w, relay_from, relay_to), copy(a, 4, w, (*x_chip, c), sibling),
                 copy(a, 5, w, (*y_chip, c), sibling)]
        for cp in later:
            cp.start()
        sent += later
    for a, w in pieces:
        copy(a, 3, w, (*d_chip, c), me).wait_recv()
        cp = copy(a, 6, w, (*d_chip, c), sibling)
        cp.start()
        sent.append(cp)
    for a, w in pieces:
        copy(a, 0, w, sibling, me).wait_recv()
        for k, chip in ((4, x_chip), (5, y_chip), (6, d_chip)):
            copy(a, k, w, (*chip, 1 - c), me).wait_recv()
    for cp in sent:
        cp.wait_send()
    for cp in mine:
        cp.wait()


FLIPS = [(fx, fy, fc) for fx in (0, 1) for fy in (0, 1) for fc in (0, 1)][1:]


def _scatter_copies(srcs, dsts, send_sems, recv_sems, local_sems, waves=1):
    n = len(srcs)
    x, y, c = _position()
    me = 4 * x + 2 * y + c
    mine = [pltpu.make_async_copy(srcs[a].at[me], dsts[a].at[me], local_sems.at[a]) for a in range(n)]
    for cp in mine:
        cp.start()
    peers = []
    for fx, fy, fc in FLIPS:
        tx = 1 - x if fx else x
        ty = 1 - y if fy else y
        tc = 1 - c if fc else c
        peers.append(((tx, ty, tc), 4 * tx + 2 * ty + tc))
    copies = []
    for w in range(waves):
        wave = []
        for k, (peer_id, peer) in enumerate(peers):
            for a in range(n):
                rows = srcs[a].shape[1]
                step = rows if waves == 1 else (rows // waves) // 16 * 16
                r0 = w * step
                cut = pl.ds(r0, step if w < waves - 1 else rows - r0)
                sem = (7 * a + k) * waves + w
                cp = pltpu.make_async_remote_copy(
                    src_ref=srcs[a].at[peer, cut], dst_ref=dsts[a].at[me, cut],
                    send_sem=send_sems.at[sem], recv_sem=recv_sems.at[sem],
                    device_id=peer_id, device_id_type=MESH)
                cp.start()
                wave.append(cp)
        for cp in wave:
            cp.wait_send()
        copies += wave
    for cp in copies:
        cp.wait_recv()
    for cp in mine:
        cp.wait()


def _pair_copies(srcs, dsts, send_sems, recv_sems, local_sems):
    x, y, c = _position()
    copies = [pltpu.make_async_remote_copy(
        src_ref=srcs[a], dst_ref=dsts[a], send_sem=send_sems.at[a], recv_sem=recv_sems.at[a],
        device_id=(x, y, 1 - c), device_id_type=MESH) for a in range(len(srcs))]
    for cp in copies:
        cp.start()
    for cp in copies:
        cp.wait()


def _chip_scatter_copies(srcs, dsts, send_sems, recv_sems, local_sems, waves=1):
    n = len(srcs)
    x, y, c = _position()
    here = 2 * x + y
    mine = [pltpu.make_async_copy(srcs[a].at[here], dsts[a].at[here], local_sems.at[a]) for a in range(n)]
    for cp in mine:
        cp.start()
    peers = []
    for fx, fy in ((1, 0), (0, 1), (1, 1)):
        tx = 1 - x if fx else x
        ty = 1 - y if fy else y
        peers.append(((tx, ty, c), 2 * tx + ty))
    copies = []
    for w in range(waves):
        wave = []
        for k, (peer_id, peer) in enumerate(peers):
            for a in range(n):
                rows = srcs[a].shape[1]
                step = rows if waves == 1 else (rows // waves) // 16 * 16
                r0 = w * step
                cut = pl.ds(r0, step if w < waves - 1 else rows - r0)
                sem = (3 * a + k) * waves + w
                cp = pltpu.make_async_remote_copy(
                    src_ref=srcs[a].at[peer, cut], dst_ref=dsts[a].at[here, cut],
                    send_sem=send_sems.at[sem], recv_sem=recv_sems.at[sem],
                    device_id=peer_id, device_id_type=MESH)
                cp.start()
                wave.append(cp)
        for cp in wave:
            cp.wait_send()
        copies += wave
    for cp in copies:
        cp.wait_recv()
    for cp in mine:
        cp.wait()


def _comm_sems(n, waves=1):
    return [pltpu.SemaphoreType.DMA((7 * n * waves,)), pltpu.SemaphoreType.DMA((7 * n * waves,)),
            pltpu.SemaphoreType.DMA((n,))]


def _handshake(peers):
    barrier = pltpu.get_barrier_semaphore()
    for peer in peers:
        pl.semaphore_signal(barrier, inc=1, device_id=peer, device_id_type=MESH)
    pl.semaphore_wait(barrier, len(peers))


def _sequencer_call(arrays, out_types, copies_fn, peers_fn, *, name, collective_id, waves=1):
    n = len(arrays)
    srcs = [jax.new_ref(a, memory_space=pltpu.MemorySpace.HBM) for a in arrays]
    dsts = [jax.empty_ref(t, memory_space=pltpu.MemorySpace.HBM) for t in out_types]
    extra = {} if waves == 1 else {"waves": waves}

    @pl.kernel(mesh=plsc.ScalarSubcoreMesh(axis_name="sequencer", num_cores=1), name=name,
               scratch_types=_comm_sems(n, waves), compiler_params=pltpu.CompilerParams(collective_id=collective_id))
    def launch(send_sems, recv_sems, local_sems):
        _handshake(peers_fn())
        copies_fn(srcs, dsts, send_sems, recv_sems, local_sems, **extra)

    launch()
    return [d[...] for d in dsts]


def _all_other_devices():
    x, y, c = _position()
    return [(1 - x if fx else x, 1 - y if fy else y, 1 - c if fc else c) for fx, fy, fc in FLIPS]


def _gather_relay_peers():
    x, y, c = _position()
    return [(x, y, 1 - c), (1 - x, y, c), (x, 1 - y, c)]


def _gather_peers():
    x, y, c = _position()
    return [(x, y, 1 - c), (1 - x, y, c), (x, 1 - y, c), (1 - x, 1 - y, c)]


def _small_gather_async(shards, *, name, collective_id):
    return _sequencer_call(shards, [jax.ShapeDtypeStruct((N_DEV, *s.shape), s.dtype) for s in shards],
                           _gather_copies, _gather_peers, name=name, collective_id=collective_id)


def _sibling():
    x, y, c = _position()
    return [(x, y, 1 - c)]


def _same_core_of_other_chips():
    x, y, c = _position()
    return [(1 - x, y, c), (x, 1 - y, c), (1 - x, 1 - y, c)]


def _pair_exchange_async(parts, *, name, collective_id):
    return _sequencer_call(parts, [jax.ShapeDtypeStruct(p.shape, p.dtype) for p in parts],
                           _pair_copies, _sibling, name=name, collective_id=collective_id)


def _chip_scatter_async(parts, *, name, collective_id, waves=1):
    return _sequencer_call(parts, [jax.ShapeDtypeStruct(p.shape, p.dtype) for p in parts],
                           _chip_scatter_copies, _same_core_of_other_chips, name=name, collective_id=collective_id,
                           waves=waves)


def _pair_sum_call(mine, theirs, *, name, tw=D_MODEL):
    n, r, w = mine.shape

    def body(a_ref, b_ref, o_ref):
        o_ref[...] = (a_ref[...].astype(F32) + b_ref[...].astype(F32)).astype(o_ref.dtype)

    spec = pl.BlockSpec((None, r, tw), lambda i, j: (i, 0, j))
    return pl.pallas_call(
        body, name=name, grid=(n, w // tw), in_specs=[spec, spec], out_specs=spec,
        out_shape=pltpu.HBM(mine.shape, mine.dtype),
        compiler_params=_cp(("parallel", "parallel")),
    )(*_hbm(mine, theirs))


def _scatter_blocks_async(parts, *, name, collective_id, waves=1):
    return _sequencer_call(parts, [jax.ShapeDtypeStruct(p.shape, p.dtype) for p in parts],
                           _scatter_copies, _all_other_devices, name=name, collective_id=collective_id, waves=waves)


def _all_gather_async(shards, *, name, collective_id, waves=1):
    return _sequencer_call(shards, [jax.ShapeDtypeStruct((N_DEV, *s.shape), s.dtype) for s in shards],
                           _gather_copies_relayed, _gather_relay_peers, name=name, collective_id=collective_id,
                           waves=waves)


def _adamw_math(w, g, m, v):
    m = ADAM_B1 * m + (1.0 - ADAM_B1) * g
    v = ADAM_B2 * v + (1.0 - ADAM_B2) * (g * g)
    m_hat = m / (1.0 - ADAM_B1 ** ADAM_STEP)
    v_hat = v / (1.0 - ADAM_B2 ** ADAM_STEP)
    delta = -ADAM_LR * (m_hat / (jnp.sqrt(v_hat) + ADAM_EPS) + ADAM_WD * w)
    return delta, m, v


def _adamw_reduce_call(recv, w, m, v, *, name, T=128):
    R, W = w.shape
    n_parts = recv.shape[0]
    if R % T == 0:
        tr, tw = T, W
    elif (R // 2) % 16 == 0:
        tr, tw = R // 2, W
    else:
        tr, tw = R, 2 * LANES

    def body(p_ref, w_ref, m_ref, v_ref, g_out, d_out, m_out, v_out):
        g = p_ref[0].astype(F32)
        for d in range(1, n_parts):
            g = g + p_ref[d].astype(F32)
        g_out[...] = g
        d_out[...], m_out[...], v_out[...] = _adamw_math(w_ref[...], g, m_ref[...], v_ref[...])

    row = pl.BlockSpec((tr, tw), lambda i, j: (i, j))
    out = pltpu.HBM((R, W), F32)
    return pl.pallas_call(
        body, name=name, grid=(R // tr, W // tw),
        in_specs=[pl.BlockSpec((n_parts, tr, tw), lambda i, j: (0, i, j)), row, row, row],
        out_specs=[row] * 4, out_shape=[out] * 4,
        compiler_params=_cp(("parallel", "parallel"), VMEM_BIG),
    )(*_hbm(recv, w, m, v))


SMALL_EARLY = (("b_gate", 8), ("w_spatial", 512), ("b_spatial", 8), ("norm_post_mix", 8), ("norm_pre_ffn", 8),
               ("norm_post_ffn", 8), ("w_gate_up", 128), ("gla_norm", 64), ("sgu_ln_g", 64), ("sgu_ln_b", 64))
SMALL_EARLY_AT = {}
for _name, _rows in SMALL_EARLY:
    SMALL_EARLY_AT[_name] = sum(r for _, r in SMALL_EARLY[:len(SMALL_EARLY_AT)])
SMALL_EARLY_ROWS = sum(r for _, r in SMALL_EARLY)
SMALL_LATE_ROWS = 16


def _small_early_rows(grads):
    def rows(a, n_rows):
        a = a.reshape(-1, LANES)
        return jnp.pad(a, ((0, n_rows - a.shape[0]), (0, 0)))

    def device_major(a, n_rows):
        r, c = a.shape[0], a.shape[1] // N_DEV
        a = a.reshape(r, N_DEV, c).transpose(1, 0, 2)
        a = jnp.pad(a, ((0, 0), (0, n_rows // N_DEV - r), (0, LANES - c)))
        return a.reshape(n_rows, LANES)

    pieces = []
    for name, n_rows in SMALL_EARLY:
        g = grads[name]
        if name in SMALL_SHARDED:
            pieces.append(device_major(g.reshape(g.shape[0], -1) if g.ndim == 2 else g.reshape(g.shape[0], g.shape[-1]), n_rows))
        else:
            pieces.append(rows(g, n_rows))
    return jnp.concatenate(pieces, axis=0)


def _small_update_call(got_early, got_late, w, m, v, *, name):
    names = list(SMALL)
    n_p = len(names)
    E = SMALL_EARLY_ROWS

    def body(early_ref, late_ref, *rest):
        w_refs, m_refs, v_refs = [dict(zip(names, rest[i * n_p:(i + 1) * n_p])) for i in range(3)]
        outs, tot = rest[3 * n_p:-1], rest[-1]
        loss_out = outs[0]
        g_out, d_out, m_out, v_out = [dict(zip(names, outs[1 + i * n_p:1 + (i + 1) * n_p])) for i in range(4)]
        acc, acc_late = early_ref[0], late_ref[0]
        for d in range(1, N_DEV):
            acc, acc_late = acc + early_ref[d], acc_late + late_ref[d]
        tot[0:E, :] = acc
        tot[E:E + SMALL_LATE_ROWS, :] = acc_late
        loss_out[...] = tot[E + 8:E + 9, :]

        x, y, c = _position()
        me = 4 * x + 2 * y + c

        def update(name, g, ix):
            g_out[name][ix] = g
            d_out[name][ix], m_out[name][ix], v_out[name][ix] = _adamw_math(
                w_refs[name][ix], g, m_refs[name][ix], v_refs[name][ix])

        for name in names:
            shape = w[name].shape
            if name in SMALL_SHARDED:
                per_dev = dict(SMALL_EARLY)[name] // N_DEV
                at = pl.multiple_of(SMALL_EARLY_AT[name] + me * per_dev, 8)
                g = tot[pl.ds(at, per_dev), :]
                update(name, g[:shape[1], :shape[2]], (0,))
            elif name == "w_spatial":
                for grp in range(GROUPS):
                    at = SMALL_EARLY_AT[name] + grp * SBLOCK
                    update(name, tot[at:at + SBLOCK, :], (0, grp))
            elif name == "b_spatial":
                at = SMALL_EARLY_AT[name]
                update(name, tot[at:at + GROUPS, :], (0,))
            else:
                at = E if name == "norm_pre_mix" else SMALL_EARLY_AT[name]
                for k in range(shape[1] // LANES):
                    update(name, tot[at + k:at + k + 1, :], (slice(None), pl.ds(k * LANES, LANES)))

    state = [s[n] for s in (w, m, v) for n in names]
    out_shapes = [jax.ShapeDtypeStruct((1, LANES), F32)] + [jax.ShapeDtypeStruct(w[n].shape, F32) for n in names] * 4
    outs = pl.pallas_call(
        body, name=name,
        in_specs=[VMEM_SPEC] * (2 + len(state)), out_specs=[VMEM_SPEC] * len(out_shapes), out_shape=out_shapes,
        scratch_shapes=[pltpu.VMEM((E + SMALL_LATE_ROWS, LANES), F32)],
    )(got_early, got_late, *state)
    per_name = {n: tuple(outs[1 + i * n_p + j] for i in range(4)) for j, n in enumerate(names)}
    return outs[0], per_name


def _tile_rows(n_elems):
    return -(-n_elems // (8 * LANES)) * 8


def _pack_rows(parts, rows):
    pieces = []
    for p in parts:
        q = p.reshape(-1, LANES)
        pieces.append(jnp.pad(q, ((0, _tile_rows(p.size) - q.shape[0]), (0, 0))))
    buf = jnp.concatenate(pieces, axis=0)
    return jnp.pad(buf, ((0, rows - buf.shape[0]), (0, 0)))


def _in_w_blocks_call(a, live, finished, *, name, tk=TOKEN_TILE):
    S = a.shape[0]
    tk = min(tk, S)
    steps = S // tk
    n_live = len(live)
    n_chips = N_DEV // 2
    out_shape = (n_chips, IN_BLK, D_MODEL)

    def body(a_ref, *rest):
        live_refs, done_refs = rest[:n_live], rest[n_live:n_live + len(finished)]
        keep_ref, send_ref = rest[n_live + len(finished):n_live + len(finished) + 2]
        accs = rest[n_live + len(finished) + 2:]
        k = pl.program_id(0)

        @pl.when(k == 0)
        def _():
            for acc in accs:
                acc[...] = jnp.zeros_like(acc)

        av = a_ref[...]
        for src, acc in zip(live_refs, accs):
            for m0 in range(0, src.shape[1], 1024):
                m1 = min(src.shape[1], m0 + 1024)
                acc[m0:m1, :] += _dot(src[:, m0:m1], av, TN)

        @pl.when(k == steps - 1)
        def _():
            groups = [(acc, cols) for acc, (_, cols) in zip(accs, live)]
            groups += [(ref, cols) for ref, (_, cols) in zip(done_refs, finished)]
            core = lax.axis_index("c")

            def cut(d, out_ref):
                lo, hi = IN_BLK * d, IN_BLK * (d + 1)
                for ref, (c0, c1) in groups:
                    s0, e0 = max(lo, c0), min(hi, c1)
                    if s0 < e0:
                        out_ref[d // 2, s0 - lo:e0 - lo, :] = ref[s0 - c0:e0 - c0, :].astype(BF16)

            for d in range(N_DEV):
                pl.when(core == d % 2)(lambda d=d: cut(d, keep_ref))
                pl.when(core != d % 2)(lambda d=d: cut(d, send_ref))

    acc_shapes = [(arr.shape[1], D_MODEL) for arr, _ in live]
    tiles = [((tk, D_MODEL), BF16)] + [((tk, arr.shape[1]), BF16) for arr, _ in live]
    resident = ([(arr.shape, arr.dtype) for arr, _ in finished] + [(out_shape, BF16)] * 2
                + [(s, F32) for s in acc_shapes])
    return pl.pallas_call(
        body, name=name, grid=(steps,),
        in_specs=[pl.BlockSpec((tk, D_MODEL), lambda k: (k, 0))]
        + [pl.BlockSpec((tk, arr.shape[1]), lambda k: (k, 0)) for arr, _ in live]
        + [_res(arr.shape) for arr, _ in finished],
        out_specs=[_acc(out_shape)] * 2,
        out_shape=[pltpu.HBM(out_shape, BF16)] * 2,
        scratch_shapes=[pltpu.VMEM(s, F32) for s in acc_shapes],
        compiler_params=_cp(("arbitrary",), _vmem_limit(tiles, resident)),
    )(*_hbm(a, *[arr for arr, _ in live], *[arr for arr, _ in finished]))


def _local_step(x, target, W, scatter, finish, pair, chip_scatter):
    g1, g2, g3, g4 = [W[n].reshape(1, D_MODEL) for n in ("norm_pre_mix", "norm_post_mix", "norm_pre_ffn", "norm_post_ffn")]
    wfi, wfo = W["w_ffn_in"].reshape(2 * D_FF, D_MODEL), W["w_ffn_out"]
    wg = jnp.pad(W["w_gate_up"], ((0, LANES - RANK), (0, 0))).astype(BF16)
    bgate = W["b_gate"].reshape(1, QK)
    gnorm = W["gla_norm"].reshape(HEADS, 1, DV)
    ln_g = W["sgu_ln_g"].reshape(GROUPS, 1, DG)
    ln_b = W["sgu_ln_b"].reshape(GROUPS, 1, DG)
    w_sp = W["w_spatial"]
    b_sp = W["b_spatial"].reshape(GROUPS, SBLOCK, 1)

    a, pa, alow, ps, pg, w_in_t = _in_proj_call(x, g1, W["w_in_blocks"], name="in_proj")
    y_gla, states = _gla_fwd_call(pa, alow, wg, bgate, gnorm, name="gla_fwd")
    y_sgu = _sgu_fwd_call(ps, ln_g, ln_b, w_sp, b_sp, name="sgu_fwd")
    t1, t2, merged, mix, x1, h = _mixer_tail_call(y_gla, y_sgu, pg, x, W["w_branch_gla"], W["w_branch_sgu"],
                                                  W["w_out"], g2, g3, name="mixer_tail")
    gu, z = _ffn_in_call(h, wfi, name="ffn_in")
    loss, dx2, dy, dg4 = _ffn_out_loss_call(z, wfo, x1, target, g4, name="ffn_out_loss")

    grads = {"norm_post_ffn": dg4}
    done = {}
    dgu = _ffn_out_bwd_call(dy, wfo, gu, name="ffn_out_bwd")
    dw_ffn_out = _weight_grads_call([z, dy], [(0, 1)], name="d_ffn_out_w")[0].reshape(N_DEV, D_FF // N_DEV, D_MODEL)
    dw_ffn_in = _ffn_in_grad_call(h, dgu, name="d_ffn_in_w").reshape(N_DEV, FF_BLK, D_MODEL)
    dgu, dw_ffn_out, dw_ffn_in = lax.optimization_barrier((dgu, dw_ffn_out, dw_ffn_in))
    ffn_received = scatter(("w_ffn_out", "w_ffn_in"), [dw_ffn_out, dw_ffn_in])
    dx1, dmix, grads["norm_pre_ffn"], grads["norm_post_mix"] = _ffn_in_bwd_call(dgu, wfi, dx2, x1, mix, g3, g2, name="ffn_in_bwd")
    dt1, dt2, dpg, dy_gla, dy_sgu = _mixer_bwd_call(dmix, t1, t2, pg, W["w_out"], W["w_branch_gla"], W["w_branch_sgu"],
                                                    name="mixer_bwd")
    rows = D_MODEL // N_DEV
    mixer_grads = _weight_grads_call([merged, dmix, y_gla, dt1, y_sgu, dt2], [(0, 1), (2, 3), (4, 5)], name="d_mixer_w")
    dy_gla, dy_sgu, mixer_grads = lax.optimization_barrier((dy_gla, dy_sgu, mixer_grads))
    mixer_received = scatter(("w_out", "w_branch_gla", "w_branch_sgu"),
                             [g.reshape(N_DEV, rows, D_MODEL) for g in mixer_grads])
    dps, dlg, dlb, dwsp, dbsp = _sgu_bwd_call(ps, ln_g, ln_b, w_sp, b_sp, dy_sgu, name="sgu_bwd")
    dw_s, dw_g = _weight_grads_call([a, dps, dpg], [(1, 0), (2, 0)], name="d_in_w_sgu_gates")
    dy_gla, dw_s, dw_g = lax.optimization_barrier((dy_gla, dw_s, dw_g))
    dpa, dlow, dwg, dgn, dbg = _gla_bwd_call(pa, alow, wg, bgate, gnorm, states, dy_gla, name="gla_bwd")
    ffn_received, mixer_received, dpa, dlow = lax.optimization_barrier((ffn_received, mixer_received, dpa, dlow))
    keep, send = _in_w_blocks_call(a, [(dpa, A_COLS), (dlow, LOW_COLS)], [(dw_s, S_COLS), (dw_g, G_COLS)],
                                   name="d_in_w_blocks")
    ffn_received, mixer_received, send = lax.optimization_barrier((ffn_received, mixer_received, send))
    from_sibling = pair(send)
    done.update({**finish(ffn_received), **finish(mixer_received)})
    done, keep = lax.optimization_barrier((done, keep))
    chip_sum = _pair_sum_call(keep, from_sibling, name="pair_sum_w_in")
    chip_sum, dpa = lax.optimization_barrier((chip_sum, dpa))
    in_received = chip_scatter(chip_sum)
    grad_x, grads["norm_pre_mix"] = _in_proj_bwd_call(dpa, dlow, dps, dpg, w_in_t, x, dx1, g1, name="in_proj_bwd")

    grads["w_gate_up"] = dwg[:RANK]
    grads["b_gate"] = dbg
    grads["gla_norm"] = dgn
    grads["sgu_ln_g"] = dlg
    grads["sgu_ln_b"] = dlb
    grads["w_spatial"] = dwsp
    grads["b_spatial"] = dbsp
    done.update(finish({"w_in": in_received}))
    return loss, grad_x, grads, done


WEIGHTS = ("norm_pre_mix", "w_in", "w_gate_up", "b_gate", "gla_norm", "sgu_ln_g", "sgu_ln_b", "w_spatial",
           "b_spatial", "w_branch_gla", "w_branch_sgu", "w_out", "norm_post_mix", "norm_pre_ffn", "w_ffn_in",
           "w_ffn_out", "norm_post_ffn")
BIG = ("w_in", "w_branch_gla", "w_branch_sgu", "w_out", "w_ffn_in", "w_ffn_out")
MIXER = ("w_branch_gla", "w_branch_sgu", "w_out")
FFN = ("w_ffn_in", "w_ffn_out")
COLUMN_SHARDED = ("w_in", "w_ffn_in")
LATE_SCATTER_WAVES = 2
GATHER_WAVES = 8
SMALL = tuple(n for n in WEIGHTS if n not in BIG)
SMALL_SHARDED = ("w_gate_up", "gla_norm", "sgu_ln_g", "sgu_ln_b")
SMALL_GATHER_ROWS = 32


def kernel(x, norm_pre_mix, w_in, w_gate_up, b_gate, gla_norm, sgu_ln_g, sgu_ln_b, w_spatial, b_spatial, w_branch_gla, w_branch_sgu, w_out, norm_post_mix, norm_pre_ffn, w_ffn_in, w_ffn_out, norm_post_ffn, loss_target, m_norm_pre_mix, m_w_in, m_w_gate_up, m_b_gate, m_gla_norm, m_sgu_ln_g, m_sgu_ln_b, m_w_spatial, m_b_spatial, m_w_branch_gla, m_w_branch_sgu, m_w_out, m_norm_post_mix, m_norm_pre_ffn, m_w_ffn_in, m_w_ffn_out, m_norm_post_ffn, v_norm_pre_mix, v_w_in, v_w_gate_up, v_b_gate, v_gla_norm, v_sgu_ln_g, v_sgu_ln_b, v_w_spatial, v_b_spatial, v_w_branch_gla, v_w_branch_sgu, v_w_out, v_norm_post_mix, v_norm_pre_ffn, v_w_ffn_in, v_w_ffn_out, v_norm_post_ffn):
    given = dict(locals())
    def local(a, n):
        return a[0].T if n in COLUMN_SHARDED else a[0]

    w = {n: local(given[n], n) for n in WEIGHTS}
    m = {n: local(given["m_" + n], n) for n in WEIGHTS}
    v = {n: local(given["v_" + n], n) for n in WEIGHTS}
    xs, target = x[0], loss_target[0]

    small_shard = _pack_rows([w[n] for n in SMALL_SHARDED], SMALL_GATHER_ROWS)
    first = _all_gather_async([w["w_in"].astype(BF16), small_shard], name="gather_w_in", collective_id=2,
                              waves=GATHER_WAVES)
    rest, first, m["w_in"], v["w_in"] = lax.optimization_barrier(
        ([w[n].astype(BF16) for n in MIXER + FFN], first, m["w_in"], v["w_in"]))
    rest_blocks = _all_gather_async(rest, name="gather_rest", collective_id=1)
    W = {n: w[n] for n in SMALL if n not in SMALL_SHARDED}
    blocks = dict(zip(MIXER + FFN, rest_blocks))
    for n in ("w_branch_gla", "w_branch_sgu", "w_out", "w_ffn_out"):
        W[n] = blocks[n].reshape(-1, D_MODEL)
    W["w_ffn_in"] = blocks["w_ffn_in"]
    W["w_in_blocks"] = first[0]
    small_blocks = first[1]
    off = 0
    for n in SMALL_SHARDED:
        r, c = w[n].shape
        blk = small_blocks[:, off:off + r * c // LANES].reshape(N_DEV, r, c)
        W[n] = blk.transpose(1, 0, 2).reshape(r, N_DEV * c)
        off += _tile_rows(r * c)

    scatter_ids = iter((3, 4))

    def scatter(names, parts):
        got = _scatter_blocks_async(parts, name="scatter_" + "_".join(names), collective_id=next(scatter_ids))
        return dict(zip(names, got))

    def finish(received):
        return {n: _adamw_reduce_call(r, w[n], m[n], v[n], name="adamw_" + n) for n, r in received.items()}

    def pair(part):
        return _pair_exchange_async([part], name="pair_w_in", collective_id=5)[0]

    def chip_scatter(part):
        return _chip_scatter_async([part], name="scatter_w_in", collective_id=6, waves=LATE_SCATTER_WAVES)[0]

    loss_part, grad_x, grads, big_done = _local_step(xs, target, W, scatter, finish, pair, chip_scatter)

    late = jnp.concatenate([grads["norm_pre_mix"].reshape(8, LANES), jnp.broadcast_to(loss_part, (8, LANES))], axis=0)
    gathered = _small_gather_async([_small_early_rows(grads), late], name="gather_small", collective_id=7)
    gathered, big_done["w_in"] = lax.optimization_barrier((gathered, big_done["w_in"]))
    loss_row, small_done = _small_update_call(
        *gathered, *[{n: given[prefix + n] for n in SMALL} for prefix in ("", "m_", "v_")], name="small_update")

    def pick(i):
        return [small_done[n][i] if n in SMALL else (big_done[n][i].T if n in COLUMN_SHARDED else big_done[n][i])[None]
                for n in WEIGHTS]

    return (loss_row[0, 0], grad_x[None], *pick(0), *pick(1), *pick(2), *pick(3))
```
